```python
import jax, jax.numpy as jnp
from jax import lax
import numpy as np

D_MODEL = 1024
BATCH = 8
SEQ = 2048
DEPTH = 2

MEM_LEN = 256
HEAD_DIM = 64
SB_HEADS = 8
FOX_HEADS = 8
MEM_HEADS = 4
SB_WIDTH = SB_HEADS * HEAD_DIM
FOX_WIDTH = FOX_HEADS * HEAD_DIM
MEM_WIDTH = MEM_HEADS * HEAD_DIM
MIX_WIDTH = SB_WIDTH + FOX_WIDTH + MEM_WIDTH
TOTAL_HEADS = SB_HEADS + FOX_HEADS + MEM_HEADS
IN_WIDTH = 3 * SB_WIDTH + 3 * FOX_WIDTH + FOX_HEADS + MEM_WIDTH + MIX_WIDTH
Q_BLOCK = 128
EPS = 1e-6

kernel_name = "hybrid_stickbreak_fox_memxattn"


def rmsnorm(x, g):
    xf = x.astype(jnp.float32)
    y = xf * lax.rsqrt(jnp.mean(xf * xf, axis=-1, keepdims=True) + EPS)
    return (y * g.astype(jnp.float32)).astype(x.dtype)


def split_heads(t, n_heads):
    b, s, _ = t.shape
    return t.reshape(b, s, n_heads, HEAD_DIM).transpose(0, 2, 1, 3)


def merge_heads(t):
    b, h, s, d = t.shape
    return t.transpose(0, 2, 1, 3).reshape(b, s, h * d)


def stick_breaking_attention(q, k, v):
    seq = q.shape[2]
    scale = HEAD_DIM ** -0.5
    outs = []
    for i in range(seq // Q_BLOCK):
        start, end = i * Q_BLOCK, (i + 1) * Q_BLOCK
        qb, kb, vb = q[:, :, start:end], k[:, :, :end], v[:, :, :end]
        z = jnp.einsum('bhqd,bhkd->bhqk', qb, kb,
                       preferred_element_type=jnp.float32) * scale
        q_pos = start + jnp.arange(Q_BLOCK)[:, None]
        k_pos = jnp.arange(end)[None, :]
        strict = k_pos < q_pos
        log_fail = jnp.where(strict, -jax.nn.softplus(z), 0.0)
        suffix = lax.cumsum(log_fail, axis=3, reverse=True) - log_fail
        log_a = jax.nn.log_sigmoid(z) + suffix
        a = jnp.where(strict, jnp.exp(log_a), 0.0)
        outs.append(jnp.einsum('bhqk,bhkd->bhqd', a.astype(vb.dtype), vb))
    return jnp.concatenate(outs, axis=2)


def forgetting_attention(q, k, v, log_f_cum):
    seq = q.shape[2]
    scale = HEAD_DIM ** -0.5
    outs = []
    for i in range(seq // Q_BLOCK):
        start, end = i * Q_BLOCK, (i + 1) * Q_BLOCK
        qb, kb, vb = q[:, :, start:end], k[:, :, :end], v[:, :, :end]
        logits = jnp.einsum('bhqd,bhkd->bhqk', qb, kb,
                            preferred_element_type=jnp.float32) * scale
        logits = logits + log_f_cum[:, :, start:end, None] - log_f_cum[:, :, None, :end]
        q_pos = start + jnp.arange(Q_BLOCK)[:, None]
        k_pos = jnp.arange(end)[None, :]
        logits = jnp.where(k_pos <= q_pos, logits, -jnp.inf)
        p = jax.nn.softmax(logits, axis=-1)
        outs.append(jnp.einsum('bhqk,bhkd->bhqd', p.astype(vb.dtype), vb))
    return jnp.concatenate(outs, axis=2)


def memory_attention(q, k, v):
    logits = jnp.einsum('bhqd,bhkd->bhqk', q, k,
                        preferred_element_type=jnp.float32) * (HEAD_DIM ** -0.5)
    p = jax.nn.softmax(logits, axis=-1)
    return jnp.einsum('bhqk,bhkd->bhqd', p.astype(v.dtype), v)


def _fwd_setup_inputs(seed: int = 0) -> dict:
    key = jax.random.key(seed)
    ks = jax.random.split(key, 11)
    f32 = jnp.float32
    x = jax.random.normal(ks[0], (BATCH, SEQ, D_MODEL), f32)
    mem = jax.random.normal(ks[1], (BATCH, MEM_LEN, D_MODEL), f32)
    norm_w = 1.0 + 0.02 * jax.random.normal(ks[2], (DEPTH, D_MODEL), f32)
    w_in = jax.random.normal(ks[3], (DEPTH, D_MODEL, IN_WIDTH), f32) * D_MODEL ** -0.5
    b_forget = 0.1 * jax.random.normal(ks[4], (DEPTH, FOX_HEADS), f32)
    mem_norm_w = 1.0 + 0.02 * jax.random.normal(ks[5], (DEPTH, D_MODEL), f32)
    w_mem_kv = jax.random.normal(ks[6], (DEPTH, D_MODEL, 2 * MEM_WIDTH), f32) * D_MODEL ** -0.5
    out_norm_w = 1.0 + 0.02 * jax.random.normal(ks[7], (DEPTH, MIX_WIDTH), f32)
    w_out = jax.random.normal(ks[8], (DEPTH, MIX_WIDTH, D_MODEL), f32) * MIX_WIDTH ** -0.5
    final_norm_w = 1.0 + 0.02 * jax.random.normal(ks[9], (D_MODEL,), f32)
    return {"x": x, "mem": mem, "norm_w": norm_w, "w_in": w_in, "b_forget": b_forget,
            "mem_norm_w": mem_norm_w, "w_mem_kv": w_mem_kv, "out_norm_w": out_norm_w,
            "w_out": w_out, "final_norm_w": final_norm_w}


def _fwd_reference(x, mem, norm_w, w_in, b_forget, mem_norm_w, w_mem_kv, out_norm_w, w_out,
              final_norm_w):
    sizes = [SB_WIDTH] * 3 + [FOX_WIDTH] * 3 + [FOX_HEADS, MEM_WIDTH, MIX_WIDTH]
    offsets = [int(o) for o in np.cumsum(sizes)[:-1]]
    b, s, _ = x.shape
    for layer in range(DEPTH):
        h = rmsnorm(x, norm_w[layer])
        proj = h @ w_in[layer]
        (sb_q, sb_k, sb_v, fx_q, fx_k, fx_v, f_logit, m_q, gate) = jnp.split(proj, offsets, axis=-1)

        sb_out = stick_breaking_attention(split_heads(sb_q, SB_HEADS), split_heads(sb_k, SB_HEADS),
                                          split_heads(sb_v, SB_HEADS))

        log_f = jax.nn.log_sigmoid((f_logit + b_forget[layer]).astype(jnp.float32))
        log_f_cum = jnp.cumsum(log_f.transpose(0, 2, 1), axis=-1)
        fx_out = forgetting_attention(split_heads(fx_q, FOX_HEADS), split_heads(fx_k, FOX_HEADS),
                                      split_heads(fx_v, FOX_HEADS), log_f_cum)

        mem_kv = rmsnorm(mem, mem_norm_w[layer]) @ w_mem_kv[layer]
        m_k, m_v = jnp.split(mem_kv, 2, axis=-1)
        m_out = memory_attention(split_heads(m_q, MEM_HEADS), split_heads(m_k, MEM_HEADS),
                                 split_heads(m_v, MEM_HEADS))

        y = jnp.concatenate([merge_heads(sb_out), merge_heads(fx_out), merge_heads(m_out)], axis=-1)
        yf = y.astype(jnp.float32).reshape(b, s, TOTAL_HEADS, HEAD_DIM)
        yf = yf * lax.rsqrt(jnp.mean(yf * yf, axis=-1, keepdims=True) + EPS)
        yf = yf.reshape(b, s, MIX_WIDTH) * out_norm_w[layer].astype(jnp.float32)
        y = (yf * jax.nn.silu(gate.astype(jnp.float32))).astype(x.dtype)
        x = x + y @ w_out[layer]
    return rmsnorm(x, final_norm_w)


import jax as _jax
import jax.numpy as _jnp

TWIN_FORMAT = 'train_step'
FWD_PARAMS = ['x', 'mem', 'norm_w', 'w_in', 'b_forget', 'mem_norm_w', 'w_mem_kv', 'out_norm_w', 'w_out', 'final_norm_w']
TWIN_WEIGHTS = ['norm_w', 'w_in', 'b_forget', 'mem_norm_w', 'w_mem_kv', 'out_norm_w', 'w_out', 'final_norm_w']
TWIN_DIFF_INPUT = 'x'
TWIN_INPUTS = ['x', 'mem', 'norm_w', 'w_in', 'b_forget', 'mem_norm_w', 'w_mem_kv', 'out_norm_w', 'w_out', 'final_norm_w', 'loss_target', 'm_norm_w', 'm_w_in', 'm_b_forget', 'm_mem_norm_w', 'm_w_mem_kv', 'm_out_norm_w', 'm_w_out', 'm_final_norm_w', 'v_norm_w', 'v_w_in', 'v_b_forget', 'v_mem_norm_w', 'v_w_mem_kv', 'v_out_norm_w', 'v_w_out', 'v_final_norm_w']
TWIN_OUTPUTS = ['loss', 'grad_x', 'grad_norm_w', 'grad_w_in', 'grad_b_forget', 'grad_mem_norm_w', 'grad_w_mem_kv', 'grad_out_norm_w', 'grad_w_out', 'grad_final_norm_w', 'delta_norm_w', 'delta_w_in', 'delta_b_forget', 'delta_mem_norm_w', 'delta_w_mem_kv', 'delta_out_norm_w', 'delta_w_out', 'delta_final_norm_w', 'new_m_norm_w', 'new_m_w_in', 'new_m_b_forget', 'new_m_mem_norm_w', 'new_m_w_mem_kv', 'new_m_out_norm_w', 'new_m_w_out', 'new_m_final_norm_w', 'new_v_norm_w', 'new_v_w_in', 'new_v_b_forget', 'new_v_mem_norm_w', 'new_v_w_mem_kv', 'new_v_out_norm_w', 'new_v_w_out', 'new_v_final_norm_w']
TWIN_LEAF_KINDS = {'loss': 'loss', 'grad_x': 'grad_x', 'grad_norm_w': 'grad_w', 'grad_w_in': 'grad_w', 'grad_b_forget': 'grad_w', 'grad_mem_norm_w': 'grad_w', 'grad_w_mem_kv': 'grad_w', 'grad_out_norm_w': 'grad_w', 'grad_w_out': 'grad_w', 'grad_final_norm_w': 'grad_w', 'delta_norm_w': 'delta_w', 'delta_w_in': 'delta_w', 'delta_b_forget': 'delta_w', 'delta_mem_norm_w': 'delta_w', 'delta_w_mem_kv': 'delta_w', 'delta_out_norm_w': 'delta_w', 'delta_w_out': 'delta_w', 'delta_final_norm_w': 'delta_w', 'new_m_norm_w': 'new_m', 'new_m_w_in': 'new_m', 'new_m_b_forget': 'new_m', 'new_m_mem_norm_w': 'new_m', 'new_m_w_mem_kv': 'new_m', 'new_m_out_norm_w': 'new_m', 'new_m_w_out': 'new_m', 'new_m_final_norm_w': 'new_m', 'new_v_norm_w': 'new_v', 'new_v_w_in': 'new_v', 'new_v_b_forget': 'new_v', 'new_v_mem_norm_w': 'new_v', 'new_v_w_mem_kv': 'new_v', 'new_v_out_norm_w': 'new_v', 'new_v_w_out': 'new_v', 'new_v_final_norm_w': 'new_v'}


def _forward(args):
    return _fwd_reference(*[args[k] for k in FWD_PARAMS])


def _output_shape():
    out = _jax.eval_shape(lambda: _forward(_fwd_setup_inputs(0)))
    return out.shape, out.dtype

N_MICROBATCH = 1
ADAM_LR = 0.001
ADAM_B1 = 0.9
ADAM_B2 = 0.999
ADAM_EPS = 1e-08
ADAM_WD = 0.01
ADAM_STEP = 10
PER_EXAMPLE_BATCH_AXIS = {'x': 0, 'mem': 0, 'loss_target': 0}
SHARED_INPUTS = []
_WEIGHT_DTYPES = {'norm_w': _jnp.float32, 'w_in': _jnp.float32, 'b_forget': _jnp.float32, 'mem_norm_w': _jnp.float32, 'w_mem_kv': _jnp.float32, 'out_norm_w': _jnp.float32, 'w_out': _jnp.float32, 'final_norm_w': _jnp.float32}
MOMENT_SCALE = {'norm_w': 1.075384e-01, 'w_in': 4.941533e-02, 'b_forget': 1.567996e-01, 'mem_norm_w': 4.349851e-02, 'w_mem_kv': 5.880396e-02, 'out_norm_w': 6.028891e-02, 'w_out': 6.588349e-02, 'final_norm_w': 1.599007e+01}


def _to_microbatches(a, axis):
    t = _jnp.moveaxis(a, axis, 0)
    t = t.reshape((N_MICROBATCH, t.shape[0] // N_MICROBATCH) + t.shape[1:])
    return _jnp.moveaxis(t, 1, axis + 1)


def setup_inputs(seed: int = 0) -> dict:
    inp = _fwd_setup_inputs(seed)
    key = _jax.random.fold_in(_jax.random.key(seed), 7919)
    shape, _ = _output_shape()
    out = dict(inp)
    out["loss_target"] = _jax.random.normal(_jax.random.fold_in(key, 0), shape, _jnp.float32)
    for i, name in enumerate(TWIN_WEIGHTS):
        w = inp[name].astype(_jnp.float32)
        if MOMENT_SCALE is None:
            s = _jnp.sqrt(_jnp.mean(_jnp.square(w)) + 1e-30)
        else:
            s = MOMENT_SCALE[name]
        km, kv = _jax.random.split(_jax.random.fold_in(key, i + 1))
        out[name] = w
        out["m_" + name] = s * _jax.random.normal(km, w.shape, _jnp.float32)
        out["v_" + name] = (s * s) * _jax.random.uniform(kv, w.shape, _jnp.float32, 0.5, 1.5)
    if N_MICROBATCH > 1:
        for name, axis in PER_EXAMPLE_BATCH_AXIS.items():
            out[name] = _to_microbatches(out[name], axis)
    return {'x': out['x'], 'mem': out['mem'], 'norm_w': out['norm_w'], 'w_in': out['w_in'], 'b_forget': out['b_forget'], 'mem_norm_w': out['mem_norm_w'], 'w_mem_kv': out['w_mem_kv'], 'out_norm_w': out['out_norm_w'], 'w_out': out['w_out'], 'final_norm_w': out['final_norm_w'], 'loss_target': out['loss_target'], 'm_norm_w': out['m_norm_w'], 'm_w_in': out['m_w_in'], 'm_b_forget': out['m_b_forget'], 'm_mem_norm_w': out['m_mem_norm_w'], 'm_w_mem_kv': out['m_w_mem_kv'], 'm_out_norm_w': out['m_out_norm_w'], 'm_w_out': out['m_w_out'], 'm_final_norm_w': out['m_final_norm_w'], 'v_norm_w': out['v_norm_w'], 'v_w_in': out['v_w_in'], 'v_b_forget': out['v_b_forget'], 'v_mem_norm_w': out['v_mem_norm_w'], 'v_w_mem_kv': out['v_w_mem_kv'], 'v_out_norm_w': out['v_out_norm_w'], 'v_w_out': out['v_w_out'], 'v_final_norm_w': out['v_final_norm_w']}


def _loss(weights, diff, rest, loss_target):
    with _jax.named_scope("forward"):
        args = {**rest, TWIN_DIFF_INPUT: diff, **{k: w.astype(_WEIGHT_DTYPES[k]) for k, w in weights.items()}}
        y = _forward(args)
    with _jax.named_scope("loss_head"):
        err = _jnp.square(y.astype(_jnp.float32) - loss_target)
        return 0.5 * _jnp.sum(_jnp.mean(err, axis=-1)) if err.ndim else 0.5 * err


def _adamw(w, g, m, v):
    m = ADAM_B1 * m + (1.0 - ADAM_B1) * g
    v = ADAM_B2 * v + (1.0 - ADAM_B2) * _jnp.square(g)
    m_hat = m / (1.0 - ADAM_B1 ** ADAM_STEP)
    v_hat = v / (1.0 - ADAM_B2 ** ADAM_STEP)
    delta = -ADAM_LR * (m_hat / (_jnp.sqrt(v_hat) + ADAM_EPS) + ADAM_WD * w)
    return delta, m, v


def reference(x, mem, norm_w, w_in, b_forget, mem_norm_w, w_mem_kv, out_norm_w, w_out, final_norm_w, loss_target, m_norm_w, m_w_in, m_b_forget, m_mem_norm_w, m_w_mem_kv, m_out_norm_w, m_w_out, m_final_norm_w, v_norm_w, v_w_in, v_b_forget, v_mem_norm_w, v_w_mem_kv, v_out_norm_w, v_w_out, v_final_norm_w):
    given = dict(x=x, mem=mem, norm_w=norm_w, w_in=w_in, b_forget=b_forget, mem_norm_w=mem_norm_w, w_mem_kv=w_mem_kv, out_norm_w=out_norm_w, w_out=w_out, final_norm_w=final_norm_w, loss_target=loss_target, m_norm_w=m_norm_w, m_w_in=m_w_in, m_b_forget=m_b_forget, m_mem_norm_w=m_mem_norm_w, m_w_mem_kv=m_w_mem_kv, m_out_norm_w=m_out_norm_w, m_w_out=m_w_out, m_final_norm_w=m_final_norm_w, v_norm_w=v_norm_w, v_w_in=v_w_in, v_b_forget=v_b_forget, v_mem_norm_w=v_mem_norm_w, v_w_mem_kv=v_w_mem_kv, v_out_norm_w=v_out_norm_w, v_w_out=v_w_out, v_final_norm_w=v_final_norm_w)
    weights = {n: given[n] for n in TWIN_WEIGHTS}
    shared = {n: given[n] for n in SHARED_INPUTS}
    per_example = {n: given[n] for n in ['x', 'mem']}
    grad_fn = _jax.value_and_grad(_loss, argnums=(0, 1))

    def one_microbatch(ex, loss_target):
        ex = dict(ex)
        diff = ex.pop(TWIN_DIFF_INPUT)
        return grad_fn(weights, diff, {**shared, **ex}, loss_target)

    if N_MICROBATCH == 1:
        loss, (grad_w, grad_x) = one_microbatch(per_example, given["loss_target"])
    else:
        def body(carry, xs):
            loss_sum, grad_sum = carry
            l_k, (gw_k, gx_k) = one_microbatch(xs[0], xs[1])
            with _jax.named_scope("update"):
                return (loss_sum + l_k, _jax.tree.map(_jnp.add, grad_sum, gw_k)), gx_k

        init = (_jnp.zeros((), _jnp.float32), _jax.tree.map(_jnp.zeros_like, weights))
        (loss, grad_w), grad_x = _jax.lax.scan(body, init, (per_example, given["loss_target"]))
    with _jax.named_scope("update"):
        delta_w, new_m, new_v = {}, {}, {}
        for n in TWIN_WEIGHTS:
            delta_w[n], new_m[n], new_v[n] = _adamw(weights[n], grad_w[n], given["m_" + n], given["v_" + n])
    return (loss, grad_x, *[grad_w[n] for n in TWIN_WEIGHTS], *[delta_w[n] for n in TWIN_WEIGHTS],
            *[new_m[n] for n in TWIN_WEIGHTS], *[new_v[n] for n in TWIN_WEIGHTS])
```

```python
import functools

import jax
import jax.numpy as jnp
from jax import lax
from jax.experimental import pallas as pl
from jax.experimental.pallas import tpu as pltpu

F32 = jnp.float32
BF16 = jnp.bfloat16

HEAD_DIM = 64
SB_WIDTH = 512
FOX_WIDTH = 512
FOX_HEADS = 8
MEM_WIDTH = 256
MIX_WIDTH = SB_WIDTH + FOX_WIDTH + MEM_WIDTH
TOTAL_HEADS = MIX_WIDTH // HEAD_DIM
IN_WIDTH = 3 * SB_WIDTH + 3 * FOX_WIDTH + FOX_HEADS + MEM_WIDTH + MIX_WIDTH
LANES = 128
QKV_WIDTH = 3 * SB_WIDTH + 3 * FOX_WIDTH
PA = QKV_WIDTH + MEM_WIDTH
PB = LANES + MIX_WIDTH
EPS = 1e-6
SCALE = HEAD_DIM ** -0.5
TILE = 128
NEG_INF = float("-inf")

ADAM_LR = 0.001
ADAM_B1 = 0.9
ADAM_B2 = 0.999
ADAM_EPS = 1e-08
ADAM_WD = 0.01
ADAM_STEP = 10

N_CHIPS = 4
N_DEV = 8
VMEM_LIMIT = 48 * 1024 * 1024
MESH = pl.DeviceIdType.MESH


def _params(*sem):
    return pltpu.CompilerParams(dimension_semantics=tuple(sem), vmem_limit_bytes=VMEM_LIMIT)


def _dot(a, b):
    return jnp.dot(a, b, preferred_element_type=F32)


def _dot_nt(a, b):
    return lax.dot_general(a, b, (((1,), (1,)), ((), ())), preferred_element_type=F32)


def _dot_tn(a, b):
    return lax.dot_general(a, b, (((0,), (0,)), ((), ())), preferred_element_type=F32)


def _split2(x):
    hi = x.astype(BF16)
    lo = (x - hi.astype(F32)).astype(BF16)
    return hi, lo


def _split3(x):
    hi = x.astype(BF16)
    r = x - hi.astype(F32)
    mid = r.astype(BF16)
    lo = (r - mid.astype(F32)).astype(BF16)
    return hi, mid, lo


def _sum_l2(x, u):
    hi, lo = _split2(x)
    return _dot(hi, u) + _dot(lo, u)


def _sum_l3(x, u):
    hi, mid, lo = _split3(x)
    return _dot(hi, u) + _dot(mid, u) + _dot(lo, u)


def _sum_r3(u, x):
    hi, mid, lo = _split3(x)
    return _dot(u, hi) + _dot(u, mid) + _dot(u, lo)


def _softplus(z):
    return jnp.maximum(z, 0.0) + jnp.log1p(jnp.exp(-jnp.abs(z)))


def _tri(n, pred):
    r = lax.broadcasted_iota(jnp.int32, (n, n), 0)
    c = lax.broadcasted_iota(jnp.int32, (n, n), 1)
    return jnp.where(pred(r, c), 1.0, 0.0).astype(BF16)


def _rows(ref, j, n=TILE):
    return pl.ds(pl.multiple_of(j * n, n), n)


def _mm(name, a, b, mode, tm, tn, out_dtype, res=None, a_lead=(), b_lead=()):
    a2, b2 = a.shape[len(a_lead):], b.shape[len(b_lead):]
    if mode == "tn":
        k, m = a2
    else:
        m, k = a2
    n = b2[0] if mode == "nt" else b2[1]
    assert m % tm == 0 and n % tn == 0, (name, m, tm, n, tn)
    na, nb = (None,) * len(a_lead), (None,) * len(b_lead)
    if mode == "tn":
        a_spec = pl.BlockSpec(na + (k, tm), lambda j, i: a_lead + (0, i))
    else:
        a_spec = pl.BlockSpec(na + (tm, k), lambda j, i: a_lead + (i, 0))
    if mode == "nt":
        b_spec = pl.BlockSpec(nb + (tn, k), lambda j, i: b_lead + (j, 0))
    else:
        b_spec = pl.BlockSpec(nb + (k, tn), lambda j, i: b_lead + (0, j))
    o_spec = pl.BlockSpec((tm, tn), lambda j, i: (i, j))
    dot = {"nn": _dot, "nt": _dot_nt, "tn": _dot_tn}[mode]

    def body(a_ref, b_ref, *rest):
        o_ref = rest[-1]
        acc = dot(a_ref[...].astype(BF16), b_ref[...].astype(BF16))
        if res is not None:
            acc = acc + rest[0][...]
        o_ref[...] = acc.astype(o_ref.dtype)

    args, specs = [a, b], [a_spec, b_spec]
    if res is not None:
        args.append(res)
        specs.append(o_spec)
    return pl.pallas_call(
        body, name=name, grid=(n // tn, m // tm), in_specs=specs, out_specs=o_spec,
        out_shape=jax.ShapeDtypeStruct((m, n), out_dtype),
        compiler_params=_params("parallel", "parallel"),
    )(*args)


def _rms_fwd(name, x, g, ts):
    s, d = x.shape

    def body(x_ref, g_ref, o_ref):
        xf = x_ref[...]
        r = lax.rsqrt(jnp.mean(xf * xf, axis=1, keepdims=True) + EPS)
        o_ref[...] = (xf * r * g_ref[...]).astype(BF16)

    return pl.pallas_call(
        body, name=name, grid=(s // ts,),
        in_specs=[pl.BlockSpec((ts, d), lambda i: (i, 0)), pl.BlockSpec((1, d), lambda i: (0, 0))],
        out_specs=pl.BlockSpec((ts, d), lambda i: (i, 0)),
        out_shape=jax.ShapeDtypeStruct((s, d), BF16),
        compiler_params=_params("parallel"),
    )(x, g)


def _rms_bwd(name, x, g, dh, dres, ts):
    s, d = x.shape

    def body(x_ref, g_ref, dh_ref, dres_ref, dx_ref, dxb_ref, dg_ref):
        @pl.when(pl.program_id(0) == 0)
        def _():
            dg_ref[...] = jnp.zeros_like(dg_ref)

        xf = x_ref[...]
        r = lax.rsqrt(jnp.mean(xf * xf, axis=1, keepdims=True) + EPS)
        xh = xf * r
        dhf = dh_ref[...]
        dg_ref[...] += jnp.sum(dhf * xh, axis=0, keepdims=True)
        dxh = dhf * g_ref[...]
        m = jnp.mean(dxh * xh, axis=1, keepdims=True)
        dx = r * (dxh - xh * m) + dres_ref[...]
        dx_ref[...] = dx
        dxb_ref[...] = dx.astype(BF16)

    row = pl.BlockSpec((ts, d), lambda i: (i, 0))
    vec = pl.BlockSpec((1, d), lambda i: (0, 0))
    return pl.pallas_call(
        body, name=name, grid=(s // ts,), in_specs=[row, vec, row, row], out_specs=[row, row, vec],
        out_shape=[jax.ShapeDtypeStruct((s, d), F32), jax.ShapeDtypeStruct((s, d), BF16),
                   jax.ShapeDtypeStruct((1, d), F32)],
        compiler_params=_params("arbitrary"),
    )(x, g, dh, dres)


def _rms_wgrad(name, x, dh):
    m_, d = x.shape

    def body(x_ref, dh_ref, dg_ref):
        xf = x_ref[...]
        r = lax.rsqrt(jnp.mean(xf * xf, axis=1, keepdims=True) + EPS)
        dg_ref[...] = jnp.sum(dh_ref[...] * xf * r, axis=0, keepdims=True)

    return pl.pallas_call(
        body, name=name, out_shape=jax.ShapeDtypeStruct((1, d), F32),
    )(x, dh)


def _final_loss(name, x, g, target, ts):
    s, d = x.shape

    def body(x_ref, g_ref, t_ref, loss_ref, dx_ref, dxb_ref, dg_ref):
        @pl.when(pl.program_id(0) == 0)
        def _():
            dg_ref[...] = jnp.zeros_like(dg_ref)
            loss_ref[...] = jnp.zeros_like(loss_ref)

        xf = x_ref[...]
        gw = g_ref[...]
        r = lax.rsqrt(jnp.mean(xf * xf, axis=1, keepdims=True) + EPS)
        xh = xf * r
        e = xh * gw - t_ref[...]
        part = 0.5 * jnp.sum(jnp.mean(e * e, axis=1, keepdims=True), axis=0, keepdims=True)
        loss_ref[...] += jnp.broadcast_to(part, loss_ref.shape)
        dy = e * (1.0 / d)
        dg_ref[...] += jnp.sum(dy * xh, axis=0, keepdims=True)
        dxh = dy * gw
        m = jnp.mean(dxh * xh, axis=1, keepdims=True)
        dx = r * (dxh - xh * m)
        dx_ref[...] = dx
        dxb_ref[...] = dx.astype(BF16)

    row = pl.BlockSpec((ts, d), lambda i: (i, 0))
    vec = pl.BlockSpec((1, d), lambda i: (0, 0))
    lvec = pl.BlockSpec((1, LANES), lambda i: (0, 0))
    return pl.pallas_call(
        body, name=name, grid=(s // ts,), in_specs=[row, vec, row], out_specs=[lvec, row, row, vec],
        out_shape=[jax.ShapeDtypeStruct((1, LANES), F32), jax.ShapeDtypeStruct((s, d), F32),
                   jax.ShapeDtypeStruct((s, d), BF16), jax.ShapeDtypeStruct((1, d), F32)],
        compiler_params=_params("arbitrary"),
    )(x, g, target)


def _gate_fwd(name, pb, bpad, fl_block):
    s = pb.shape[0]
    nb = s // TILE

    def body(fl_ref, b_ref, ccol_ref, crow_ref, carry):
        @pl.when(pl.program_id(0) == 0)
        def _():
            carry[...] = jnp.zeros_like(carry)

        u = fl_ref[...] + b_ref[...]
        lf = jnp.minimum(u, 0.0) - jnp.log1p(jnp.exp(-jnp.abs(u)))
        lower = _tri(TILE, lambda r, c: c <= r)
        c = _sum_r3(lower, lf) + carry[0:1, :]
        ccol_ref[...] = c
        crow_ref[0] = c.T[0:8, :]
        carry[...] = jnp.broadcast_to(c[TILE - 1:TILE, :], carry.shape)

    return pl.pallas_call(
        body, name=name, grid=(nb,),
        in_specs=[pl.BlockSpec((TILE, LANES), lambda i: (i, fl_block)), pl.BlockSpec((1, LANES), lambda i: (0, 0))],
        out_specs=[pl.BlockSpec((TILE, LANES), lambda i: (i, 0)), pl.BlockSpec((1, 8, TILE), lambda i: (i, 0, 0))],
        out_shape=[jax.ShapeDtypeStruct((s, LANES), F32), jax.ShapeDtypeStruct((nb, 8, TILE), F32)],
        scratch_shapes=[pltpu.VMEM((8, LANES), F32)],
        compiler_params=_params("arbitrary"),
    )(pb, bpad)


def _gate_bwd(name, pb, bpad, colsum, fl_block):
    s = pb.shape[0]
    nb = s // TILE

    def body(fl_ref, b_ref, cs_ref, dl_ref, db_ref, carry):
        @pl.when(pl.program_id(0) == 0)
        def _():
            carry[...] = jnp.zeros_like(carry)
            db_ref[...] = jnp.zeros_like(db_ref)

        upper = _tri(TILE, lambda r, c: r >= c)
        rsum = _sum_l3(cs_ref[0], upper) + carry[:, 0:1]
        carry[...] = jnp.broadcast_to(rsum[:, 0:1], carry.shape)
        full = jnp.concatenate([rsum, jnp.zeros((LANES - 8, TILE), F32)], axis=0)
        dlf = -full.T
        u = fl_ref[...] + b_ref[...]
        dlogit = dlf * (1.0 - jax.nn.sigmoid(u))
        dl_ref[...] = dlogit.astype(BF16)
        db_ref[...] += jnp.sum(dlogit, axis=0, keepdims=True)

    rev = lambda i: (nb - 1 - i, 0)
    return pl.pallas_call(
        body, name=name, grid=(nb,),
        in_specs=[pl.BlockSpec((TILE, LANES), lambda i: (nb - 1 - i, fl_block)),
                  pl.BlockSpec((1, LANES), lambda i: (0, 0)),
                  pl.BlockSpec((1, 8, TILE), lambda i: (nb - 1 - i, 0, 0))],
        out_specs=[pl.BlockSpec((TILE, LANES), rev), pl.BlockSpec((1, LANES), lambda i: (0, 0))],
        out_shape=[jax.ShapeDtypeStruct((s, LANES), BF16), jax.ShapeDtypeStruct((1, LANES), F32)],
        scratch_shapes=[pltpu.VMEM((8, LANES), F32)],
        compiler_params=_params("arbitrary"),
    )(pb, bpad, colsum)


def _head_slices(hh):
    return slice(HEAD_DIM * hh, HEAD_DIM * (hh + 1))


def _scaled_q(q_ref, sl):
    return (q_ref[:, sl].astype(F32) * SCALE).astype(BF16)


def _sb_tile(q, kj, carry, strict, u_after, diag):
    z = _dot_nt(q, kj)
    sp = _softplus(z)
    lf = -sp
    if diag:
        lf = jnp.where(strict, lf, 0.0)
    sx = _sum_l2(lf, u_after)
    a = jnp.exp((z - sp) + sx + carry)
    if diag:
        a = jnp.where(strict, a, 0.0)
    return z, sp, a, carry + sx[:, 0:1] + lf[:, 0:1]


def _sb_fwd(name, pa, col0):
    s = pa.shape[0]
    nb = s // TILE
    cb = col0 // LANES

    def body(q_ref, k_ref, v_ref, o_ref):
        i = pl.program_id(1)
        r = lax.broadcasted_iota(jnp.int32, (TILE, TILE), 0)
        c = lax.broadcasted_iota(jnp.int32, (TILE, TILE), 1)
        strict = c < r
        u_after = _tri(TILE, lambda rr, cc: rr > cc)
        outs = []
        for hh in range(2):
            sl = _head_slices(hh)
            q = _scaled_q(q_ref, sl)

            def tile(j, carry, acc, diag):
                kj = k_ref[_rows(k_ref, j), sl]
                vj = v_ref[_rows(v_ref, j), sl]
                _, _, a, carry = _sb_tile(q, kj, carry, strict, u_after, diag)
                return carry, acc + _dot(a.astype(BF16), vj)

            carry, acc = tile(i, jnp.zeros((TILE, 1), F32), jnp.zeros((TILE, HEAD_DIM), F32), True)
            carry, acc = lax.fori_loop(0, i, lambda t, ca: tile(i - 1 - t, ca[0], ca[1], False), (carry, acc))
            outs.append(acc)
        o_ref[...] = jnp.concatenate(outs, axis=1)

    return pl.pallas_call(
        body, name=name, grid=(4, nb),
        in_specs=[pl.BlockSpec((TILE, LANES), lambda p, i: (i, cb + p)),
                  pl.BlockSpec((s, LANES), lambda p, i: (0, cb + 4 + p)),
                  pl.BlockSpec((s, LANES), lambda p, i: (0, cb + 8 + p))],
        out_specs=pl.BlockSpec((TILE, LANES), lambda p, i: (i, p)),
        out_shape=jax.ShapeDtypeStruct((s, SB_WIDTH), F32),
        compiler_params=_params("parallel", "arbitrary"),
    )(pa, pa, pa)


def _sb_bwd(name, pa, col0, dout, dcol0):
    s = pa.shape[0]
    nb = s // TILE
    cb = col0 // LANES
    db = dcol0 // LANES

    def body(q_ref, k_ref, v_ref, do_ref, dq_ref, dk_ref, dv_ref, dk_acc, dv_acc, gpan, span):
        i = pl.program_id(1)

        @pl.when(i == 0)
        def _():
            dk_acc[...] = jnp.zeros_like(dk_acc)
            dv_acc[...] = jnp.zeros_like(dv_acc)

        r = lax.broadcasted_iota(jnp.int32, (TILE, TILE), 0)
        c = lax.broadcasted_iota(jnp.int32, (TILE, TILE), 1)
        strict = c < r
        u_after = _tri(TILE, lambda rr, cc: rr > cc)
        u_before = _tri(TILE, lambda rr, cc: rr < cc)
        dqs = []
        for hh in range(2):
            sl = _head_slices(hh)
            q = _scaled_q(q_ref, sl)
            do = do_ref[:, sl].astype(BF16)

            def pass1(j, carry, diag):
                kj = k_ref[_rows(k_ref, j), sl]
                vj = v_ref[_rows(v_ref, j), sl]
                z, sp, a, carry = _sb_tile(q, kj, carry, strict, u_after, diag)
                gpan[j] = a * _dot_nt(do, vj)
                span[j] = jnp.exp(z - sp)
                dv_acc[hh, _rows(None, j), :] += _dot_tn(a.astype(BF16), do)
                return carry

            carry = pass1(i, jnp.zeros((TILE, 1), F32), True)
            lax.fori_loop(0, i, lambda t, ca: pass1(i - 1 - t, ca, False), carry)

            def pass2(j, before, dq, diag):
                kj = k_ref[_rows(k_ref, j), sl]
                g = gpan[j]
                sig = span[j]
                pfx = _sum_l2(g, u_before) + before
                dz = g * (1.0 - sig) - sig * pfx
                if diag:
                    dz = jnp.where(strict, dz, 0.0)
                dzb = dz.astype(BF16)
                dk_acc[hh, _rows(None, j), :] += _dot_tn(dzb, q)
                return pfx[:, TILE - 1:TILE] + g[:, TILE - 1:TILE], dq + _dot(dzb, kj)

            before, dq = lax.fori_loop(
                0, i, lambda j, bd: pass2(j, bd[0], bd[1], False),
                (jnp.zeros((TILE, 1), F32), jnp.zeros((TILE, HEAD_DIM), F32)))
            _, dq = pass2(i, before, dq, True)
            dqs.append(dq * SCALE)
        dq_ref[...] = jnp.concatenate(dqs, axis=1).astype(BF16)

        @pl.when(i == nb - 1)
        def _():
            dk_ref[...] = jnp.concatenate([dk_acc[0], dk_acc[1]], axis=1).astype(BF16)
            dv_ref[...] = jnp.concatenate([dv_acc[0], dv_acc[1]], axis=1).astype(BF16)

    qspec = pl.BlockSpec((TILE, LANES), lambda p, i: (i, p))
    kvspec = pl.BlockSpec((s, LANES), lambda p, i: (0, p))
    out = jax.ShapeDtypeStruct((s, SB_WIDTH), BF16)
    return pl.pallas_call(
        body, name=name, grid=(4, nb),
        in_specs=[pl.BlockSpec((TILE, LANES), lambda p, i: (i, cb + p)),
                  pl.BlockSpec((s, LANES), lambda p, i: (0, cb + 4 + p)),
                  pl.BlockSpec((s, LANES), lambda p, i: (0, cb + 8 + p)),
                  pl.BlockSpec((TILE, LANES), lambda p, i: (i, db + p))],
        out_specs=[qspec, kvspec, kvspec], out_shape=[out, out, out],
        scratch_shapes=[pltpu.VMEM((2, s, HEAD_DIM), F32), pltpu.VMEM((2, s, HEAD_DIM), F32),
                        pltpu.VMEM((nb, TILE, TILE), F32), pltpu.VMEM((nb, TILE, TILE), F32)],
        compiler_params=_params("arbitrary", "arbitrary"),
    )(pa, pa, pa, dout)


def _fox_scores(q, kj, cq, crj, causal, diag):
    sc = _dot_nt(q, kj) + (cq - crj)
    if diag:
        sc = jnp.where(causal, sc, NEG_INF)
    return sc


def _fox_fwd(name, pa, col0, ccol4, crow4):
    s = pa.shape[0]
    nb = s // TILE
    cb = col0 // LANES

    def body(q_ref, k_ref, v_ref, cc_ref, cr_ref, o_ref, lse_ref):
        i = pl.program_id(1)
        r = lax.broadcasted_iota(jnp.int32, (TILE, TILE), 0)
        c = lax.broadcasted_iota(jnp.int32, (TILE, TILE), 1)
        causal = c <= r
        outs, lses = [], []
        for hh in range(2):
            sl = _head_slices(hh)
            q = _scaled_q(q_ref, sl)
            cq = cc_ref[:, HEAD_DIM * hh:HEAD_DIM * hh + 1]

            def tile(j, m, l, acc, diag):
                kj = k_ref[_rows(k_ref, j), sl]
                vj = v_ref[_rows(v_ref, j), sl]
                sc = _fox_scores(q, kj, cq, cr_ref[j, hh:hh + 1, :], causal, diag)
                m2 = jnp.maximum(m, jnp.max(sc, axis=1, keepdims=True))
                alpha = jnp.exp(m - m2)
                p = jnp.exp(sc - m2)
                return m2, l * alpha + jnp.sum(p, axis=1, keepdims=True), acc * alpha + _dot(p.astype(BF16), vj)

            m, l, acc = tile(i, jnp.full((TILE, 1), NEG_INF, F32), jnp.zeros((TILE, 1), F32),
                             jnp.zeros((TILE, HEAD_DIM), F32), True)
            m, l, acc = lax.fori_loop(0, i, lambda t, st: tile(i - 1 - t, st[0], st[1], st[2], False), (m, l, acc))
            outs.append(acc / l)
            lses.append(jnp.broadcast_to(m + jnp.log(l), (TILE, HEAD_DIM)))
        o_ref[...] = jnp.concatenate(outs, axis=1)
        lse_ref[...] = jnp.concatenate(lses, axis=1)

    return pl.pallas_call(
        body, name=name, grid=(4, nb),
        in_specs=[pl.BlockSpec((TILE, LANES), lambda p, i: (i, cb + p)),
                  pl.BlockSpec((s, LANES), lambda p, i: (0, cb + 4 + p)),
                  pl.BlockSpec((s, LANES), lambda p, i: (0, cb + 8 + p)),
                  pl.BlockSpec((None, TILE, LANES), lambda p, i: (p, i, 0)),
                  pl.BlockSpec((None, nb, 8, TILE), lambda p, i: (p, 0, 0, 0))],
        out_specs=[pl.BlockSpec((TILE, LANES), lambda p, i: (i, p)),
                   pl.BlockSpec((None, TILE, LANES), lambda p, i: (p, i, 0))],
        out_shape=[jax.ShapeDtypeStruct((s, FOX_WIDTH), F32), jax.ShapeDtypeStruct((4, s, LANES), F32)],
        compiler_params=_params("parallel", "arbitrary"),
    )(pa, pa, pa, ccol4, crow4)


def _fox_bwd(name, pa, col0, ccol4, crow4, out, lse, dout, dcol0):
    s = pa.shape[0]
    nb = s // TILE
    cb = col0 // LANES
    db = dcol0 // LANES

    def body(q_ref, k_ref, v_ref, cc_ref, cr_ref, o_ref, lse_ref, do_ref,
             dq_ref, dk_ref, dv_ref, cs_ref, dk_acc, dv_acc):
        i = pl.program_id(1)

        @pl.when(i == 0)
        def _():
            dk_acc[...] = jnp.zeros_like(dk_acc)
            dv_acc[...] = jnp.zeros_like(dv_acc)
            cs_ref[...] = jnp.zeros_like(cs_ref)

        r = lax.broadcasted_iota(jnp.int32, (TILE, TILE), 0)
        c = lax.broadcasted_iota(jnp.int32, (TILE, TILE), 1)
        causal = c <= r
        dqs = []
        for hh in range(2):
            sl = _head_slices(hh)
            q = _scaled_q(q_ref, sl)
            cq = cc_ref[:, HEAD_DIM * hh:HEAD_DIM * hh + 1]
            lse_h = lse_ref[:, HEAD_DIM * hh:HEAD_DIM * hh + 1]
            dof = do_ref[:, sl]
            do = dof.astype(BF16)
            delta = jnp.sum(dof * o_ref[:, sl], axis=1, keepdims=True)

            def tile(j, dq, rs, diag):
                kj = k_ref[_rows(k_ref, j), sl]
                vj = v_ref[_rows(v_ref, j), sl]
                sc = _fox_scores(q, kj, cq, cr_ref[j, hh:hh + 1, :], causal, diag)
                p = jnp.exp(sc - lse_h)
                ds = p * (_dot_nt(do, vj) - delta)
                dsb = ds.astype(BF16)
                dv_acc[hh, _rows(None, j), :] += _dot_tn(p.astype(BF16), do)
                dk_acc[hh, _rows(None, j), :] += _dot_tn(dsb, q)
                cs_ref[j, hh:hh + 1, :] += jnp.sum(ds, axis=0, keepdims=True)
                return dq + _dot(dsb, kj), rs + jnp.sum(ds, axis=1, keepdims=True)

            dq, rs = tile(i, jnp.zeros((TILE, HEAD_DIM), F32), jnp.zeros((TILE, 1), F32), True)
            dq, rs = lax.fori_loop(0, i, lambda t, st: tile(i - 1 - t, st[0], st[1], False), (dq, rs))
            cs_ref[i, hh:hh + 1, :] -= jnp.broadcast_to(rs, (TILE, LANES)).T[0:1, :]
            dqs.append(dq * SCALE)
        dq_ref[...] = jnp.concatenate(dqs, axis=1).astype(BF16)

        @pl.when(i == nb - 1)
        def _():
            dk_ref[...] = jnp.concatenate([dk_acc[0], dk_acc[1]], axis=1).astype(BF16)
            dv_ref[...] = jnp.concatenate([dv_acc[0], dv_acc[1]], axis=1).astype(BF16)

    qspec = pl.BlockSpec((TILE, LANES), lambda p, i: (i, p))
    kvspec = pl.BlockSpec((s, LANES), lambda p, i: (0, p))
    o3 = jax.ShapeDtypeStruct((s, FOX_WIDTH), BF16)
    return pl.pallas_call(
        body, name=name, grid=(4, nb),
        in_specs=[pl.BlockSpec((TILE, LANES), lambda p, i: (i, cb + p)),
                  pl.BlockSpec((s, LANES), lambda p, i: (0, cb + 4 + p)),
                  pl.BlockSpec((s, LANES), lambda p, i: (0, cb + 8 + p)),
                  pl.BlockSpec((None, TILE, LANES), lambda p, i: (p, i, 0)),
                  pl.BlockSpec((None, nb, 8, TILE), lambda p, i: (p, 0, 0, 0)),
                  qspec,
                  pl.BlockSpec((None, TILE, LANES), lambda p, i: (p, i, 0)),
                  pl.BlockSpec((TILE, LANES), lambda p, i: (i, db + p))],
        out_specs=[qspec, kvspec, kvspec, pl.BlockSpec((None, nb, 8, TILE), lambda p, i: (p, 0, 0, 0))],
        out_shape=[o3, o3, o3, jax.ShapeDtypeStruct((4, nb, 8, TILE), F32)],
        scratch_shapes=[pltpu.VMEM((2, s, HEAD_DIM), F32), pltpu.VMEM((2, s, HEAD_DIM), F32)],
        compiler_params=_params("arbitrary", "arbitrary"),
    )(pa, pa, pa, ccol4, crow4, out, lse, dout)


def _mem_fwd(name, pa, mkv):
    s = pa.shape[0]
    ml = mkv.shape[0]
    nb = s // TILE
    cb = QKV_WIDTH // LANES

    def body(q_ref, k_ref, v_ref, o_ref, lse_ref):
        outs, lses = [], []
        for hh in range(2):
            sl = _head_slices(hh)
            sc = _dot_nt(_scaled_q(q_ref, sl), k_ref[:, sl])
            m = jnp.max(sc, axis=1, keepdims=True)
            p = jnp.exp(sc - m)
            l = jnp.sum(p, axis=1, keepdims=True)
            outs.append(_dot(p.astype(BF16), v_ref[:, sl]) / l)
            lses.append(jnp.broadcast_to(m + jnp.log(l), (TILE, HEAD_DIM)))
        o_ref[...] = jnp.concatenate(outs, axis=1)
        lse_ref[...] = jnp.concatenate(lses, axis=1)

    return pl.pallas_call(
        body, name=name, grid=(2, nb),
        in_specs=[pl.BlockSpec((TILE, LANES), lambda p, i: (i, cb + p)),
                  pl.BlockSpec((ml, LANES), lambda p, i: (0, p)),
                  pl.BlockSpec((ml, LANES), lambda p, i: (0, 2 + p))],
        out_specs=[pl.BlockSpec((TILE, LANES), lambda p, i: (i, p)),
                   pl.BlockSpec((None, TILE, LANES), lambda p, i: (p, i, 0))],
        out_shape=[jax.ShapeDtypeStruct((s, MEM_WIDTH), F32), jax.ShapeDtypeStruct((2, s, LANES), F32)],
        compiler_params=_params("parallel", "parallel"),
    )(pa, mkv, mkv)


def _mem_bwd(name, pa, mkv, out, lse, dout, dcol0):
    s = pa.shape[0]
    ml = mkv.shape[0]
    nb = s // TILE
    cb = QKV_WIDTH // LANES
    db = dcol0 // LANES

    def body(q_ref, k_ref, v_ref, o_ref, lse_ref, do_ref, dq_ref, dk_ref, dv_ref, dk_acc, dv_acc):
        i = pl.program_id(1)

        @pl.when(i == 0)
        def _():
            dk_acc[...] = jnp.zeros_like(dk_acc)
            dv_acc[...] = jnp.zeros_like(dv_acc)

        dqs = []
        for hh in range(2):
            sl = _head_slices(hh)
            q = _scaled_q(q_ref, sl)
            kh = k_ref[:, sl]
            dof = do_ref[:, sl]
            do = dof.astype(BF16)
            delta = jnp.sum(dof * o_ref[:, sl], axis=1, keepdims=True)
            p = jnp.exp(_dot_nt(q, kh) - lse_ref[:, HEAD_DIM * hh:HEAD_DIM * hh + 1])
            ds = (p * (_dot_nt(do, v_ref[:, sl]) - delta)).astype(BF16)
            dv_acc[hh] += _dot_tn(p.astype(BF16), do)
            dk_acc[hh] += _dot_tn(ds, q)
            dqs.append(_dot(ds, kh) * SCALE)
        dq_ref[...] = jnp.concatenate(dqs, axis=1).astype(BF16)

        @pl.when(i == nb - 1)
        def _():
            dk_ref[...] = jnp.concatenate([dk_acc[0], dk_acc[1]], axis=1).astype(BF16)
            dv_ref[...] = jnp.concatenate([dv_acc[0], dv_acc[1]], axis=1).astype(BF16)

    qspec = pl.BlockSpec((TILE, LANES), lambda p, i: (i, p))
    kvspec = pl.BlockSpec((ml, LANES), lambda p, i: (0, p))
    okv = jax.ShapeDtypeStruct((ml, MEM_WIDTH), BF16)
    return pl.pallas_call(
        body, name=name, grid=(2, nb),
        in_specs=[pl.BlockSpec((TILE, LANES), lambda p, i: (i, cb + p)),
                  pl.BlockSpec((ml, LANES), lambda p, i: (0, p)),
                  pl.BlockSpec((ml, LANES), lambda p, i: (0, 2 + p)),
                  qspec,
                  pl.BlockSpec((None, TILE, LANES), lambda p, i: (p, i, 0)),
                  pl.BlockSpec((TILE, LANES), lambda p, i: (i, db + p))],
        out_specs=[qspec, kvspec, kvspec],
        out_shape=[jax.ShapeDtypeStruct((s, MEM_WIDTH), BF16), okv, okv],
        scratch_shapes=[pltpu.VMEM((2, ml, HEAD_DIM), F32), pltpu.VMEM((2, ml, HEAD_DIM), F32)],
        compiler_params=_params("arbitrary", "arbitrary"),
    )(pa, mkv, mkv, out, lse, dout)


def _head_maps():
    col = jnp.arange(MIX_WIDTH)[:, None] // HEAD_DIM
    g = (col == jnp.arange(LANES)[None, :]).astype(BF16)
    return g, g.T


def _normed_heads(osb_ref, ofx_ref, om_ref, g_ref, gt_ref):
    y = jnp.concatenate([osb_ref[...], ofx_ref[...], om_ref[...]], axis=1)
    msq = _sum_l3(y * y, g_ref[...]) * (1.0 / HEAD_DIM)
    rf = _sum_l3(lax.rsqrt(msq + EPS), gt_ref[...])
    return y * rf, rf


def _out_fwd(name, o_sb, o_fx, o_m, pb, ow, x, w_out, layer, ts):
    s, d = x.shape
    g, gt = _head_maps()

    def body(osb_ref, ofx_ref, om_ref, gate_ref, ow_ref, x_ref, w_ref, g_ref, gt_ref, xo_ref, y2_ref):
        yh, _ = _normed_heads(osb_ref, ofx_ref, om_ref, g_ref, gt_ref)
        gate = gate_ref[...]
        y2 = (yh * ow_ref[...] * (gate * jax.nn.sigmoid(gate))).astype(BF16)
        y2_ref[...] = y2
        xo_ref[...] = x_ref[...] + _dot(y2, w_ref[...])

    return pl.pallas_call(
        body, name=name, grid=(s // ts,),
        in_specs=[_row_spec(ts, SB_WIDTH), _row_spec(ts, FOX_WIDTH), _row_spec(ts, MEM_WIDTH),
                  _row_spec(ts, MIX_WIDTH), _const_spec((1, MIX_WIDTH)), _row_spec(ts, d),
                  pl.BlockSpec((None, MIX_WIDTH, d), lambda i: (layer, 0, 0)),
                  _const_spec((MIX_WIDTH, LANES)), _const_spec((LANES, MIX_WIDTH))],
        out_specs=[_row_spec(ts, d), _row_spec(ts, MIX_WIDTH)],
        out_shape=[jax.ShapeDtypeStruct((s, d), F32), jax.ShapeDtypeStruct((s, MIX_WIDTH), BF16)],
        compiler_params=_params("parallel"),
    )(o_sb, o_fx, o_m, pb, ow, x, w_out, g, gt)


def _row_spec(ts, w):
    return pl.BlockSpec((ts, w), lambda i: (i, 0))


def _const_spec(shape):
    return pl.BlockSpec(shape, lambda i: (0,) * len(shape))


def _out_bwd(name, dxb, o_sb, o_fx, o_m, pb, ow, w_out, layer, ts):
    s, d = dxb.shape
    g, gt = _head_maps()

    def body(dx_ref, osb_ref, ofx_ref, om_ref, gate_ref, ow_ref, w_ref, g_ref, gt_ref, dy_ref, dgate_ref, dow_ref):
        @pl.when(pl.program_id(0) == 0)
        def _():
            dow_ref[...] = jnp.zeros_like(dow_ref)

        dy2 = _dot_nt(dx_ref[...], w_ref[...])
        yh, rf = _normed_heads(osb_ref, ofx_ref, om_ref, g_ref, gt_ref)
        gate = gate_ref[...]
        sig = jax.nn.sigmoid(gate)
        ow_v = ow_ref[...]
        dgate_ref[...] = (dy2 * (yh * ow_v) * (sig * (1.0 + gate * (1.0 - sig)))).astype(BF16)
        dn = dy2 * (gate * sig)
        dow_ref[...] += jnp.sum(dn * yh, axis=0, keepdims=True)
        dyh = dn * ow_v
        t = _sum_l3(dyh * yh, g_ref[...]) * (1.0 / HEAD_DIM)
        dy_ref[...] = rf * (dyh - yh * _sum_l3(t, gt_ref[...]))

    return pl.pallas_call(
        body, name=name, grid=(s // ts,),
        in_specs=[_row_spec(ts, d), _row_spec(ts, SB_WIDTH), _row_spec(ts, FOX_WIDTH), _row_spec(ts, MEM_WIDTH),
                  _row_spec(ts, MIX_WIDTH), _const_spec((1, MIX_WIDTH)),
                  pl.BlockSpec((None, MIX_WIDTH, d), lambda i: (layer, 0, 0)),
                  _const_spec((MIX_WIDTH, LANES)), _const_spec((LANES, MIX_WIDTH))],
        out_specs=[_row_spec(ts, MIX_WIDTH), _row_spec(ts, MIX_WIDTH), _const_spec((1, MIX_WIDTH))],
        out_shape=[jax.ShapeDtypeStruct((s, MIX_WIDTH), F32), jax.ShapeDtypeStruct((s, MIX_WIDTH), BF16),
                   jax.ShapeDtypeStruct((1, MIX_WIDTH), F32)],
        compiler_params=_params("arbitrary"),
    )(dxb, o_sb, o_fx, o_m, pb, ow, w_out, g, gt)


def _adamw(name, w, g, m, v, tr):
    rows, cols = w.shape

    def body(w_ref, g_ref, m_ref, v_ref, d_ref, m2_ref, v2_ref):
        gv = g_ref[...]
        m2 = ADAM_B1 * m_ref[...] + (1.0 - ADAM_B1) * gv
        v2 = ADAM_B2 * v_ref[...] + (1.0 - ADAM_B2) * (gv * gv)
        m_hat = m2 / (1.0 - ADAM_B1 ** ADAM_STEP)
        v_hat = v2 / (1.0 - ADAM_B2 ** ADAM_STEP)
        d_ref[...] = -ADAM_LR * (m_hat / (jnp.sqrt(v_hat) + ADAM_EPS) + ADAM_WD * w_ref[...])
        m2_ref[...] = m2
        v2_ref[...] = v2

    spec = _row_spec(tr, cols)
    shp = jax.ShapeDtypeStruct((rows, cols), F32)
    return pl.pallas_call(
        body, name=name, grid=(rows // tr,), in_specs=[spec] * 4, out_specs=[spec] * 3, out_shape=[shp] * 3,
        compiler_params=_params("parallel"),
    )(w, g, m, v)


HBM_SPEC = pl.BlockSpec(memory_space=pltpu.HBM)


def _place():
    x, y, c = lax.axis_index("x"), lax.axis_index("y"), lax.axis_index("c")
    chips = [(1 - x, y), (x, 1 - y), (1 - x, 1 - y)]
    return x, y, c, chips


def _remote(src, dst, send_sems, recv_sems, k, to):
    return pltpu.make_async_remote_copy(src_ref=src, dst_ref=dst, send_sem=send_sems.at[k], recv_sem=recv_sems.at[k],
                                        device_id=to, device_id_type=MESH)


def _gather_weights(name, bucket):
    r, w = bucket.shape
    rh = r // 2

    def body(in_ref, out_ref, send_sems, recv_sems, local_sem):
        x, y, c, chips = _place()
        mine = 2 * x + y
        sibling = (x, y, 1 - c)

        def half(cc):
            return pl.ds(pl.multiple_of(cc * rh, 16), rh)

        local = pltpu.make_async_copy(in_ref, out_ref.at[mine], local_sem)
        local.start()
        first = [_remote(in_ref.at[half(c)], out_ref.at[mine, half(c)], send_sems, recv_sems, j, (cx, cy, c))
                 for j, (cx, cy) in enumerate(chips)]
        for cp in first:
            cp.start()
        passed = []
        for j, (cx, cy) in enumerate(chips):
            landed = out_ref.at[2 * cx + cy, half(c)]
            _remote(landed, landed, send_sems, recv_sems, j, sibling).wait_recv()
            cp = _remote(landed, landed, send_sems, recv_sems, 3 + j, sibling)
            cp.start()
            passed.append(cp)
        for j, (cx, cy) in enumerate(chips):
            other = out_ref.at[2 * cx + cy, half(1 - c)]
            _remote(other, other, send_sems, recv_sems, 3 + j, sibling).wait_recv()
        for cp in first + passed:
            cp.wait_send()
        local.wait()

    return pl.pallas_call(
        body, name=name, in_specs=[HBM_SPEC], out_specs=HBM_SPEC,
        out_shape=jax.ShapeDtypeStruct((N_CHIPS, r, w), bucket.dtype),
        scratch_shapes=[pltpu.SemaphoreType.DMA((6,)), pltpu.SemaphoreType.DMA((6,)), pltpu.SemaphoreType.DMA],
    )(bucket)


def _swap_halves(name, g4):
    n, r, w = g4.shape
    rh = r // 2

    def body(in_ref, out_ref, send_sems, recv_sems):
        x, y, c, _ = _place()
        src = in_ref.at[:, pl.ds(pl.multiple_of((1 - c) * rh, 16), rh), :]
        cp = _remote(src, out_ref, send_sems, recv_sems, 0, (x, y, 1 - c))
        cp.start()
        cp.wait()

    return pl.pallas_call(
        body, name=name, in_specs=[HBM_SPEC], out_specs=HBM_SPEC,
        out_shape=jax.ShapeDtypeStruct((n, rh, w), g4.dtype),
        scratch_shapes=[pltpu.SemaphoreType.DMA((1,)), pltpu.SemaphoreType.DMA((1,))],
    )(g4)


def _add_half(name, g4, r1, cvec, tr):
    n, r, w = g4.shape
    rh = r // 2
    nblk = rh // tr

    def body(c_ref, a_ref, b_ref, o_ref):
        o_ref[...] = a_ref[...] + b_ref[...]

    return pl.pallas_call(
        body, name=name,
        grid_spec=pltpu.PrefetchScalarGridSpec(
            num_scalar_prefetch=1, grid=(n, nblk),
            in_specs=[pl.BlockSpec((None, tr, w), lambda k, i, c_ref: (k, c_ref[0] * nblk + i, 0)),
                      pl.BlockSpec((None, tr, w), lambda k, i, c_ref: (k, i, 0))],
            out_specs=pl.BlockSpec((None, tr, w), lambda k, i, c_ref: (k, i, 0))),
        out_shape=jax.ShapeDtypeStruct((n, rh, w), F32),
        compiler_params=_params("parallel", "parallel"),
    )(cvec, g4, r1)


def _scatter_chips(name, h4):
    n, rh, w = h4.shape

    def body(in_ref, out_ref, send_sems, recv_sems, local_sem):
        x, y, c, chips = _place()
        mine = 2 * x + y
        local = pltpu.make_async_copy(in_ref.at[mine], out_ref.at[mine], local_sem)
        local.start()
        sends = [_remote(in_ref.at[2 * cx + cy], out_ref.at[mine], send_sems, recv_sems, j, (cx, cy, c))
                 for j, (cx, cy) in enumerate(chips)]
        for cp in sends:
            cp.start()
        for j, (cx, cy) in enumerate(chips):
            got = out_ref.at[2 * cx + cy]
            _remote(got, got, send_sems, recv_sems, j, (cx, cy, c)).wait_recv()
        for cp in sends:
            cp.wait_send()
        local.wait()

    return pl.pallas_call(
        body, name=name, in_specs=[HBM_SPEC], out_specs=HBM_SPEC,
        out_shape=jax.ShapeDtypeStruct((n, rh, w), h4.dtype),
        scratch_shapes=[pltpu.SemaphoreType.DMA((3,)), pltpu.SemaphoreType.DMA((3,)), pltpu.SemaphoreType.DMA],
    )(h4)


def _sum4(name, r2, tr):
    n, rh, w = r2.shape

    def body(a_ref, b_ref, c_ref, d_ref, o_ref):
        o_ref[...] = ((a_ref[...] + b_ref[...]) + c_ref[...]) + d_ref[...]

    specs = [pl.BlockSpec((None, tr, w), functools.partial(lambda k, i: (k, i, 0), k)) for k in range(n)]
    return pl.pallas_call(
        body, name=name, grid=(rh // tr,), in_specs=specs, out_specs=pl.BlockSpec((tr, w), lambda i: (i, 0)),
        out_shape=jax.ShapeDtypeStruct((rh, w), F32),
        compiler_params=_params("parallel"),
    )(r2, r2, r2, r2)


def _join_halves(name, gh):
    rh, w = gh.shape

    def body(in_ref, out_ref, send_sems, recv_sems, local_sem):
        x, y, c, _ = _place()
        mine = out_ref.at[pl.ds(pl.multiple_of(c * rh, 16), rh), :]
        theirs = out_ref.at[pl.ds(pl.multiple_of((1 - c) * rh, 16), rh), :]
        local = pltpu.make_async_copy(in_ref, mine, local_sem)
        local.start()
        cp = _remote(in_ref, mine, send_sems, recv_sems, 0, (x, y, 1 - c))
        cp.start()
        _remote(theirs, theirs, send_sems, recv_sems, 0, (x, y, 1 - c)).wait_recv()
        cp.wait_send()
        local.wait()

    return pl.pallas_call(
        body, name=name, in_specs=[HBM_SPEC], out_specs=HBM_SPEC,
        out_shape=jax.ShapeDtypeStruct((2 * rh, w), gh.dtype),
        scratch_shapes=[pltpu.SemaphoreType.DMA((1,)), pltpu.SemaphoreType.DMA((1,)), pltpu.SemaphoreType.DMA],
    )(gh)


def _allreduce_small(name, vec):
    rows, w = vec.shape

    def body(v_ref, o_ref, buf, send_sems, recv_sems):
        x, y, c, _ = _place()
        me = 4 * x + 2 * y + c
        buf[me] = v_ref[...]
        flips = [(fx, fy, fc) for fx in (0, 1) for fy in (0, 1) for fc in (0, 1)][1:]
        peers = [(x + fx - 2 * x * fx, y + fy - 2 * y * fy, c + fc - 2 * c * fc) for fx, fy, fc in flips]
        sends = [_remote(v_ref, buf.at[me], send_sems, recv_sems, k, peer) for k, peer in enumerate(peers)]
        for cp in sends:
            cp.start()
        for k, (px, py, pc) in enumerate(peers):
            got = buf.at[4 * px + 2 * py + pc]
            _remote(got, got, send_sems, recv_sems, k, (px, py, pc)).wait_recv()
        for cp in sends:
            cp.wait_send()
        acc = buf[0]
        for dev in range(1, N_DEV):
            acc = acc + buf[dev]
        o_ref[...] = acc

    vm = pl.BlockSpec(memory_space=pltpu.VMEM)
    return pl.pallas_call(
        body, name=name, in_specs=[vm], out_specs=vm, out_shape=jax.ShapeDtypeStruct((rows, w), F32),
        scratch_shapes=[pltpu.VMEM((N_DEV, rows, w), F32), pltpu.SemaphoreType.DMA((7,)), pltpu.SemaphoreType.DMA((7,))],
    )(vec)


BUCKET_WIDTH = 1024
GATE_COL = 3 * SB_WIDTH + 3 * FOX_WIDTH + FOX_HEADS + MEM_WIDTH
FL_COL = QKV_WIDTH


def _to_groups(w):
    pad = jnp.zeros(w.shape[:-1] + (LANES - FOX_HEADS,), w.dtype)
    wa = jnp.concatenate([w[..., :QKV_WIDTH], w[..., FL_COL + FOX_HEADS:GATE_COL]], axis=-1)
    wb = jnp.concatenate([w[..., GATE_COL:], w[..., FL_COL:FL_COL + FOX_HEADS], pad], axis=-1)
    return wa, wb


def _from_groups(ga, gb):
    return jnp.concatenate([ga[..., :QKV_WIDTH], gb[..., MIX_WIDTH:MIX_WIDTH + FOX_HEADS], ga[..., QKV_WIDTH:],
                            gb[..., :MIX_WIDTH]], axis=-1)


def _tile_of(n, cap, unit):
    if n <= cap:
        return n
    best = None
    for t in range(unit, cap + 1, unit):
        if n % t == 0:
            best = t
    assert best is not None, (n, cap, unit)
    return best


def _bucket_rows(n_rows):
    half = -(-n_rows // 32) * 16
    return 2 * half


def _part_rows(size):
    return -(-size // (16 * BUCKET_WIDTH)) * 16


def _pack_rows(parts, rows):
    flat = []
    for p in parts:
        f = p.reshape(-1, BUCKET_WIDTH)
        flat.append(jnp.pad(f, ((0, _part_rows(p.size) - f.shape[0]), (0, 0))))
    flat = jnp.concatenate(flat, axis=0)
    return jnp.pad(flat, ((0, rows - flat.shape[0]), (0, 0)))


def _unpack_rows(bucket, like):
    outs, r = [], 0
    for a in like:
        outs.append(bucket[r:r + a.size // BUCKET_WIDTH].reshape(a.shape))
        r += _part_rows(a.size)
    return outs


def _pack_small(parts):
    rows = []
    for p in parts:
        f = p.reshape(-1).astype(F32)
        f = jnp.pad(f, (0, (-f.shape[0]) % LANES))
        rows.append(f.reshape(-1, LANES))
    out = jnp.concatenate(rows, axis=0)
    return jnp.pad(out, ((0, (-out.shape[0]) % 8), (0, 0)))


def _unpack_small(packed, shapes):
    outs, r = [], 0
    for shp in shapes:
        n = 1
        for s_ in shp:
            n *= s_
        nr = -(-n // LANES)
        outs.append(packed[r:r + nr].reshape(-1)[:n].reshape(shp))
        r += nr
    return outs


def kernel(x, mem, norm_w, w_in, b_forget, mem_norm_w, w_mem_kv, out_norm_w, w_out, final_norm_w, loss_target, m_norm_w, m_w_in, m_b_forget, m_mem_norm_w, m_w_mem_kv, m_out_norm_w, m_w_out, m_final_norm_w, v_norm_w, v_w_in, v_b_forget, v_mem_norm_w, v_w_mem_kv, v_out_norm_w, v_w_out, v_final_norm_w):
    xs = x[0]
    mems = mem[0]
    target = loss_target[0]
    s, d = xs.shape
    depth = norm_w.shape[0]
    nb = s // TILE
    ts = _tile_of(s, 256, 8)
    big = (w_in, w_mem_kv, w_out)
    rows = _bucket_rows(sum(_part_rows(a.size) for a in big))

    gathered = _gather_weights("gather_weights", _pack_rows([a.astype(BF16) for a in big], rows))
    shards = [_unpack_rows(gathered[j], big) for j in range(N_CHIPS)]
    w_in_full = jnp.concatenate([sh[0] for sh in shards], axis=2)
    wkv = jnp.concatenate([sh[1] for sh in shards], axis=1)
    wout = jnp.concatenate([sh[2] for sh in shards], axis=1)
    wa, wb = _to_groups(w_in_full)

    tm = _tile_of(s, 256, 8)
    fl_block = MIX_WIDTH // LANES

    saved = []
    cur = xs
    for l in range(depth):
        h = _rms_fwd(f"rms_fwd{l}", cur, norm_w[l][None], ts)
        pa = _mm(f"inproj_a{l}", h, wa, "nn", tm, _tile_of(PA, 1664, LANES), BF16, b_lead=(l,))
        pb = _mm(f"inproj_b{l}", h, wb, "nn", tm, PB, F32, b_lead=(l,))
        bpad = jnp.pad(b_forget[l], (0, LANES - FOX_HEADS))[None]
        ccol, crow = _gate_fwd(f"gate_fwd{l}", pb, bpad, fl_block)
        ccol4 = jnp.repeat(ccol[:, :FOX_HEADS].reshape(s, 4, 2).transpose(1, 0, 2), HEAD_DIM, axis=2)
        crow4 = jnp.pad(crow.reshape(nb, 4, 2, TILE).transpose(1, 0, 2, 3), ((0, 0), (0, 0), (0, 6), (0, 0)))
        o_sb = _sb_fwd(f"sb_fwd{l}", pa, 0)
        o_fx, lse_fx = _fox_fwd(f"fox_fwd{l}", pa, 3 * SB_WIDTH, ccol4, crow4)
        mn = _rms_fwd(f"mem_rms{l}", mems, mem_norm_w[l][None], mems.shape[0])
        mkv = _mm(f"mem_kv{l}", mn, wkv, "nn", mems.shape[0], 2 * MEM_WIDTH, BF16, b_lead=(l,))
        o_m, lse_m = _mem_fwd(f"mem_fwd{l}", pa, mkv)
        nxt, y2 = _out_fwd(f"out_fwd{l}", o_sb, o_fx, o_m, pb, out_norm_w[l][None], cur, wout, l, ts)
        saved.append((cur, h, pa, pb, bpad, ccol4, crow4, o_sb, o_fx, lse_fx, mn, mkv, o_m, lse_m, y2))
        cur = nxt

    loss_v, dx, dxb, g_final = _final_loss("final_loss", cur, final_norm_w[None], target, ts)

    g_norm, g_b, g_memnorm, g_outnorm = [None] * depth, [None] * depth, [None] * depth, [None] * depth
    g_wa, g_wb, g_wkv, g_wout = [None] * depth, [None] * depth, [None] * depth, [None] * depth
    for l in reversed(range(depth)):
        xin, h, pa, pb, bpad, ccol4, crow4, o_sb, o_fx, lse_fx, mn, mkv, o_m, lse_m, y2 = saved[l]
        dy, dgate, g_outnorm[l] = _out_bwd(f"out_bwd{l}", dxb, o_sb, o_fx, o_m, pb, out_norm_w[l][None], wout, l, ts)
        g_wout[l] = _mm(f"dw_out{l}", y2, dxb, "tn", _tile_of(MIX_WIDTH, 640, LANES), d, F32)
        dq_sb, dk_sb, dv_sb = _sb_bwd(f"sb_bwd{l}", pa, 0, dy, 0)
        dq_fx, dk_fx, dv_fx, cs4 = _fox_bwd(f"fox_bwd{l}", pa, 3 * SB_WIDTH, ccol4, crow4, o_fx, lse_fx, dy, SB_WIDTH)
        colsum = cs4[:, :, :2, :].transpose(1, 0, 2, 3).reshape(nb, 8, TILE)
        dlogit, g_b[l] = _gate_bwd(f"gate_bwd{l}", pb, bpad, colsum, fl_block)
        dq_m, dk_m, dv_m = _mem_bwd(f"mem_bwd{l}", pa, mkv, o_m, lse_m, dy, SB_WIDTH + FOX_WIDTH)
        dmkv = jnp.concatenate([dk_m, dv_m], axis=1)
        g_wkv[l] = _mm(f"dw_kv{l}", mn, dmkv, "tn", d, 2 * MEM_WIDTH, F32)
        dmn = _mm(f"dmem{l}", dmkv, wkv, "nt", mems.shape[0], d, F32, b_lead=(l,))
        g_memnorm[l] = _rms_wgrad(f"mem_norm_grad{l}", mems, dmn)
        dpa = jnp.concatenate([dq_sb, dk_sb, dv_sb, dq_fx, dk_fx, dv_fx, dq_m], axis=1)
        dpb = jnp.concatenate([dgate, dlogit], axis=1)
        tw = _tile_of(d, 512, LANES)
        g_wa[l] = _mm(f"dw_in_a{l}", h, dpa, "tn", tw, _tile_of(PA, 1664, LANES), F32)
        g_wb[l] = _mm(f"dw_in_b{l}", h, dpb, "tn", tw, PB, F32)
        dh = _mm(f"dh_a{l}", dpa, wa, "nt", tm, d, F32, b_lead=(l,))
        dh = _mm(f"dh_b{l}", dpb, wb, "nt", tm, d, F32, res=dh, b_lead=(l,))
        dx, dxb, g_norm[l] = _rms_bwd(f"rms_bwd{l}", xin, norm_w[l][None], dh, dx, ts)

    gw_in = _from_groups(jnp.stack(g_wa), jnp.stack(g_wb))
    gw_kv = jnp.stack(g_wkv)
    gw_out = jnp.stack(g_wout)
    cw, kr, orows = w_in.shape[2], w_mem_kv.shape[1], w_out.shape[1]
    g4 = jnp.stack([
        _pack_rows([gw_in[:, :, j * cw:(j + 1) * cw], gw_kv[:, j * kr:(j + 1) * kr], gw_out[:, j * orows:(j + 1) * orows]], rows)
        for j in range(N_CHIPS)])
    tr = _tile_of(rows // 2, 808, 8)
    cvec = lax.axis_index("c").astype(jnp.int32).reshape(1)
    from_sibling = _swap_halves("grad_swap_halves", g4)
    chip_sum = _add_half("grad_add_half", g4, from_sibling, cvec, tr)
    from_chips = _scatter_chips("grad_scatter_chips", chip_sum)
    g_half = _sum4("grad_sum_chips", from_chips, tr)
    g_bucket = _join_halves("grad_join_halves", g_half)
    big_grads = _unpack_rows(g_bucket, big)

    small_w = [norm_w, b_forget, mem_norm_w, out_norm_w, final_norm_w]
    small_m = [m_norm_w, m_b_forget, m_mem_norm_w, m_out_norm_w, m_final_norm_w]
    small_v = [v_norm_w, v_b_forget, v_mem_norm_w, v_out_norm_w, v_final_norm_w]
    small_shapes = [a.shape for a in small_w]
    local_small = [jnp.concatenate(g_norm, axis=0), jnp.stack([g[0, :FOX_HEADS] for g in g_b]),
                   jnp.concatenate(g_memnorm, axis=0), jnp.concatenate(g_outnorm, axis=0), g_final[0]]
    reduced = _allreduce_small("small_allreduce", _pack_small(local_small + [loss_v[0, :1]]))
    small_grads = _unpack_small(reduced, small_shapes)
    n_small_rows = sum(-(-a.size // LANES) for a in small_w)
    loss = reduced[n_small_rows, 0]

    d_s, m_s, v_s = _adamw("adamw_small", _pack_small(small_w), _pack_small(small_grads), _pack_small(small_m),
                           _pack_small(small_v), _pack_small(small_w).shape[0])
    small_delta, small_m2, small_v2 = (_unpack_small(a, small_shapes) for a in (d_s, m_s, v_s))
    big_delta, big_m2, big_v2 = [], [], []
    for nm, w_, g_, m_, v_ in zip(("w_in", "w_mem_kv", "w_out"), big, big_grads, (m_w_in, m_w_mem_kv, m_w_out),
                                  (v_w_in, v_w_mem_kv, v_w_out)):
        flat = (-1, w_.shape[-1])
        r_ = w_.size // w_.shape[-1]
        outs = _adamw(f"adamw_{nm}", w_.reshape(flat), g_.reshape(flat), m_.reshape(flat), v_.reshape(flat),
                      _tile_of(r_, 256, 8))
        for lst, o in zip((big_delta, big_m2, big_v2), outs):
            lst.append(o.reshape(w_.shape))

    def order(sm, bg):
        return [sm[0], bg[0], sm[1], sm[2], bg[1], sm[3], bg[2], sm[4]]

    return (loss, dx[None], *order(small_grads, big_grads), *order(small_delta, big_delta),
            *order(small_m2, big_m2), *order(small_v2, big_v2))
```

```python
import functools

import jax
import jax.numpy as jnp
from jax import lax
from jax.experimental import pallas as pl
from jax.experimental.pallas import tpu as pltpu

F32 = jnp.float32
BF16 = jnp.bfloat16

HEAD_DIM = 64
SB_WIDTH = 512
FOX_WIDTH = 512
FOX_HEADS = 8
MEM_WIDTH = 256
MIX_WIDTH = SB_WIDTH + FOX_WIDTH + MEM_WIDTH
TOTAL_HEADS = MIX_WIDTH // HEAD_DIM
IN_WIDTH = 3 * SB_WIDTH + 3 * FOX_WIDTH + FOX_HEADS + MEM_WIDTH + MIX_WIDTH
LANES = 128
QKV_WIDTH = 3 * SB_WIDTH + 3 * FOX_WIDTH
PA = QKV_WIDTH + MEM_WIDTH
PB = LANES + MIX_WIDTH
EPS = 1e-6
SCALE = HEAD_DIM ** -0.5
TILE = 256
NEG_INF = float("-inf")

ADAM_LR = 0.001
ADAM_B1 = 0.9
ADAM_B2 = 0.999
ADAM_EPS = 1e-08
ADAM_WD = 0.01
ADAM_STEP = 10

N_CHIPS = 4
N_DEV = 8
VMEM_LIMIT = 48 * 1024 * 1024
MESH = pl.DeviceIdType.MESH


def _params(*sem):
    return pltpu.CompilerParams(dimension_semantics=tuple(sem), vmem_limit_bytes=VMEM_LIMIT)


def _dot(a, b):
    return jnp.dot(a, b, preferred_element_type=F32)


def _dot_nt(a, b):
    return lax.dot_general(a, b, (((1,), (1,)), ((), ())), preferred_element_type=F32)


def _dot_tn(a, b):
    return lax.dot_general(a, b, (((0,), (0,)), ((), ())), preferred_element_type=F32)


def _split2(x):
    hi = x.astype(BF16)
    lo = (x - hi.astype(F32)).astype(BF16)
    return hi, lo


def _split3(x):
    hi = x.astype(BF16)
    r = x - hi.astype(F32)
    mid = r.astype(BF16)
    lo = (r - mid.astype(F32)).astype(BF16)
    return hi, mid, lo


def _sum_l2(x, u):
    hi, lo = _split2(x)
    return _dot(hi, u) + _dot(lo, u)


def _sum_l3(x, u):
    hi, mid, lo = _split3(x)
    return _dot(hi, u) + _dot(mid, u) + _dot(lo, u)


def _sum_r3(u, x):
    hi, mid, lo = _split3(x)
    return _dot(u, hi) + _dot(u, mid) + _dot(u, lo)


def _softplus(z):
    return jnp.maximum(z, 0.0) + jnp.log1p(jnp.exp(-jnp.abs(z)))


def _tri(n, pred):
    r = lax.broadcasted_iota(jnp.int32, (n, n), 0)
    c = lax.broadcasted_iota(jnp.int32, (n, n), 1)
    return jnp.where(pred(r, c), 1.0, 0.0).astype(BF16)


def _rows(ref, j, n=TILE):
    return pl.ds(pl.multiple_of(j * n, n), n)


def _mm(name, a, b, mode, tm, tn, out_dtype, res=None, a_lead=(), b_lead=()):
    a2, b2 = a.shape[len(a_lead):], b.shape[len(b_lead):]
    if mode == "tn":
        k, m = a2
    else:
        m, k = a2
    n = b2[0] if mode == "nt" else b2[1]
    assert m % tm == 0 and n % tn == 0, (name, m, tm, n, tn)
    na, nb = (None,) * len(a_lead), (None,) * len(b_lead)
    if mode == "tn":
        a_spec = pl.BlockSpec(na + (k, tm), lambda j, i: a_lead + (0, i))
    else:
        a_spec = pl.BlockSpec(na + (tm, k), lambda j, i: a_lead + (i, 0))
    if mode == "nt":
        b_spec = pl.BlockSpec(nb + (tn, k), lambda j, i: b_lead + (j, 0))
    else:
        b_spec = pl.BlockSpec(nb + (k, tn), lambda j, i: b_lead + (0, j))
    o_spec = pl.BlockSpec((tm, tn), lambda j, i: (i, j))
    dot = {"nn": _dot, "nt": _dot_nt, "tn": _dot_tn}[mode]

    def body(a_ref, b_ref, *rest):
        o_ref = rest[-1]
        acc = dot(a_ref[...].astype(BF16), b_ref[...].astype(BF16))
        if res is not None:
            acc = acc + rest[0][...]
        o_ref[...] = acc.astype(o_ref.dtype)

    args, specs = [a, b], [a_spec, b_spec]
    if res is not None:
        args.append(res)
        specs.append(o_spec)
    return pl.pallas_call(
        body, name=name, grid=(n // tn, m // tm), in_specs=specs, out_specs=o_spec,
        out_shape=jax.ShapeDtypeStruct((m, n), out_dtype),
        compiler_params=_params("parallel", "parallel"),
    )(*args)


def _rms_fwd(name, x, g, ts):
    s, d = x.shape

    def body(x_ref, g_ref, o_ref):
        xf = x_ref[...]
        r = lax.rsqrt(jnp.mean(xf * xf, axis=1, keepdims=True) + EPS)
        o_ref[...] = (xf * r * g_ref[...]).astype(BF16)

    return pl.pallas_call(
        body, name=name, grid=(s // ts,),
        in_specs=[pl.BlockSpec((ts, d), lambda i: (i, 0)), pl.BlockSpec((1, d), lambda i: (0, 0))],
        out_specs=pl.BlockSpec((ts, d), lambda i: (i, 0)),
        out_shape=jax.ShapeDtypeStruct((s, d), BF16),
        compiler_params=_params("parallel"),
    )(x, g)


def _rms_bwd(name, x, g, dh, dres, ts):
    s, d = x.shape

    def body(x_ref, g_ref, dh_ref, dres_ref, dx_ref, dxb_ref, dg_ref):
        @pl.when(pl.program_id(0) == 0)
        def _():
            dg_ref[...] = jnp.zeros_like(dg_ref)

        xf = x_ref[...]
        r = lax.rsqrt(jnp.mean(xf * xf, axis=1, keepdims=True) + EPS)
        xh = xf * r
        dhf = dh_ref[...]
        dg_ref[...] += jnp.sum(dhf * xh, axis=0, keepdims=True)
        dxh = dhf * g_ref[...]
        m = jnp.mean(dxh * xh, axis=1, keepdims=True)
        dx = r * (dxh - xh * m) + dres_ref[...]
        dx_ref[...] = dx
        dxb_ref[...] = dx.astype(BF16)

    row = pl.BlockSpec((ts, d), lambda i: (i, 0))
    vec = pl.BlockSpec((1, d), lambda i: (0, 0))
    return pl.pallas_call(
        body, name=name, grid=(s // ts,), in_specs=[row, vec, row, row], out_specs=[row, row, vec],
        out_shape=[jax.ShapeDtypeStruct((s, d), F32), jax.ShapeDtypeStruct((s, d), BF16),
                   jax.ShapeDtypeStruct((1, d), F32)],
        compiler_params=_params("arbitrary"),
    )(x, g, dh, dres)


def _rms_wgrad(name, x, dh):
    m_, d = x.shape

    def body(x_ref, dh_ref, dg_ref):
        xf = x_ref[...]
        r = lax.rsqrt(jnp.mean(xf * xf, axis=1, keepdims=True) + EPS)
        dg_ref[...] = jnp.sum(dh_ref[...] * xf * r, axis=0, keepdims=True)

    return pl.pallas_call(
        body, name=name, out_shape=jax.ShapeDtypeStruct((1, d), F32),
    )(x, dh)


def _final_loss(name, x, g, target, ts):
    s, d = x.shape

    def body(x_ref, g_ref, t_ref, loss_ref, dx_ref, dxb_ref, dg_ref):
        @pl.when(pl.program_id(0) == 0)
        def _():
            dg_ref[...] = jnp.zeros_like(dg_ref)
            loss_ref[...] = jnp.zeros_like(loss_ref)

        xf = x_ref[...]
        gw = g_ref[...]
        r = lax.rsqrt(jnp.mean(xf * xf, axis=1, keepdims=True) + EPS)
        xh = xf * r
        e = xh * gw - t_ref[...]
        part = 0.5 * jnp.sum(jnp.mean(e * e, axis=1, keepdims=True), axis=0, keepdims=True)
        loss_ref[...] += jnp.broadcast_to(part, loss_ref.shape)
        dy = e * (1.0 / d)
        dg_ref[...] += jnp.sum(dy * xh, axis=0, keepdims=True)
        dxh = dy * gw
        m = jnp.mean(dxh * xh, axis=1, keepdims=True)
        dx = r * (dxh - xh * m)
        dx_ref[...] = dx
        dxb_ref[...] = dx.astype(BF16)

    row = pl.BlockSpec((ts, d), lambda i: (i, 0))
    vec = pl.BlockSpec((1, d), lambda i: (0, 0))
    lvec = pl.BlockSpec((1, LANES), lambda i: (0, 0))
    return pl.pallas_call(
        body, name=name, grid=(s // ts,), in_specs=[row, vec, row], out_specs=[lvec, row, row, vec],
        out_shape=[jax.ShapeDtypeStruct((1, LANES), F32), jax.ShapeDtypeStruct((s, d), F32),
                   jax.ShapeDtypeStruct((s, d), BF16), jax.ShapeDtypeStruct((1, d), F32)],
        compiler_params=_params("arbitrary"),
    )(x, g, target)


def _gate_fwd(name, pb, bpad, fl_block):
    s = pb.shape[0]
    nb = s // TILE

    def body(fl_ref, b_ref, ccol_ref, crow_ref, carry):
        @pl.when(pl.program_id(0) == 0)
        def _():
            carry[...] = jnp.zeros_like(carry)

        u = fl_ref[...] + b_ref[...]
        lf = jnp.minimum(u, 0.0) - jnp.log1p(jnp.exp(-jnp.abs(u)))
        lower = _tri(TILE, lambda r, c: c <= r)
        c = _sum_r3(lower, lf) + carry[0:1, :]
        ccol_ref[...] = c
        crow_ref[0] = c.T[0:8, :]
        carry[...] = jnp.broadcast_to(c[TILE - 1:TILE, :], carry.shape)

    return pl.pallas_call(
        body, name=name, grid=(nb,),
        in_specs=[pl.BlockSpec((TILE, LANES), lambda i: (i, fl_block)), pl.BlockSpec((1, LANES), lambda i: (0, 0))],
        out_specs=[pl.BlockSpec((TILE, LANES), lambda i: (i, 0)), pl.BlockSpec((1, 8, TILE), lambda i: (i, 0, 0))],
        out_shape=[jax.ShapeDtypeStruct((s, LANES), F32), jax.ShapeDtypeStruct((nb, 8, TILE), F32)],
        scratch_shapes=[pltpu.VMEM((8, LANES), F32)],
        compiler_params=_params("arbitrary"),
    )(pb, bpad)


def _gate_bwd(name, pb, bpad, colsum, fl_block):
    s = pb.shape[0]
    nb = s // TILE

    def body(fl_ref, b_ref, cs_ref, dl_ref, db_ref, carry):
        @pl.when(pl.program_id(0) == 0)
        def _():
            carry[...] = jnp.zeros_like(carry)
            db_ref[...] = jnp.zeros_like(db_ref)

        upper = _tri(TILE, lambda r, c: r >= c)
        rsum = _sum_l3(cs_ref[0], upper) + carry[:, 0:1]
        carry[...] = jnp.broadcast_to(rsum[:, 0:1], carry.shape)
        full = jnp.concatenate([rsum, jnp.zeros((LANES - 8, TILE), F32)], axis=0)
        dlf = -full.T
        u = fl_ref[...] + b_ref[...]
        dlogit = dlf * (1.0 - jax.nn.sigmoid(u))
        dl_ref[...] = dlogit.astype(BF16)
        db_ref[...] += jnp.sum(dlogit, axis=0, keepdims=True)

    rev = lambda i: (nb - 1 - i, 0)
    return pl.pallas_call(
        body, name=name, grid=(nb,),
        in_specs=[pl.BlockSpec((TILE, LANES), lambda i: (nb - 1 - i, fl_block)),
                  pl.BlockSpec((1, LANES), lambda i: (0, 0)),
                  pl.BlockSpec((1, 8, TILE), lambda i: (nb - 1 - i, 0, 0))],
        out_specs=[pl.BlockSpec((TILE, LANES), rev), pl.BlockSpec((1, LANES), lambda i: (0, 0))],
        out_shape=[jax.ShapeDtypeStruct((s, LANES), BF16), jax.ShapeDtypeStruct((1, LANES), F32)],
        scratch_shapes=[pltpu.VMEM((8, LANES), F32)],
        compiler_params=_params("arbitrary"),
    )(pb, bpad, colsum)


def _head_slices(hh):
    return slice(HEAD_DIM * hh, HEAD_DIM * (hh + 1))


def _scaled_q(q_ref, sl):
    return (q_ref[:, sl].astype(F32) * SCALE).astype(BF16)


def _sb_tile(q, kj, carry, strict, u_after, diag):
    z = _dot_nt(q, kj)
    sp = _softplus(z)
    lf = -sp
    if diag:
        lf = jnp.where(strict, lf, 0.0)
    sx = _sum_l2(lf, u_after)
    a = jnp.exp((z - sp) + sx + carry)
    if diag:
        a = jnp.where(strict, a, 0.0)
    return z, sp, a, carry + sx[:, 0:1] + lf[:, 0:1]


def _sb_fwd(name, pa, col0):
    s = pa.shape[0]
    nb = s // TILE
    cb = col0 // LANES

    def body(q_ref, k_ref, v_ref, o_ref):
        i = pl.program_id(1)
        r = lax.broadcasted_iota(jnp.int32, (TILE, TILE), 0)
        c = lax.broadcasted_iota(jnp.int32, (TILE, TILE), 1)
        strict = c < r
        u_after = _tri(TILE, lambda rr, cc: rr > cc)
        qs = [_scaled_q(q_ref, _head_slices(hh)) for hh in range(2)]

        def tile(j, state, diag):
            kblk = k_ref[_rows(k_ref, j), :]
            vblk = v_ref[_rows(v_ref, j), :]
            new = []
            for hh in range(2):
                sl = _head_slices(hh)
                carry, acc = state[hh]
                _, _, a, carry = _sb_tile(qs[hh], kblk[:, sl], carry, strict, u_after, diag)
                new.append((carry, acc + _dot(a.astype(BF16), vblk[:, sl])))
            return tuple(new)

        zero = (jnp.zeros((TILE, 1), F32), jnp.zeros((TILE, HEAD_DIM), F32))
        state = tile(i, (zero, zero), True)
        state = lax.fori_loop(0, i, lambda t, st: tile(i - 1 - t, st, False), state)
        o_ref[...] = jnp.concatenate([state[0][1], state[1][1]], axis=1)

    return pl.pallas_call(
        body, name=name, grid=(4, nb),
        in_specs=[pl.BlockSpec((TILE, LANES), lambda p, i: (i, cb + p)),
                  pl.BlockSpec((s, LANES), lambda p, i: (0, cb + 4 + p)),
                  pl.BlockSpec((s, LANES), lambda p, i: (0, cb + 8 + p))],
        out_specs=pl.BlockSpec((TILE, LANES), lambda p, i: (i, p)),
        out_shape=jax.ShapeDtypeStruct((s, SB_WIDTH), F32),
        compiler_params=_params("parallel", "arbitrary"),
    )(pa, pa, pa)


def _sb_bwd(name, pa, col0, dout, dcol0):
    s = pa.shape[0]
    nb = s // TILE
    cb = col0 // LANES
    db = dcol0 // LANES

    def body(q_ref, k_ref, v_ref, do_ref, dq_ref, dk_ref, dv_ref, dk_acc, dv_acc, gpan, span):
        i = pl.program_id(1)

        @pl.when(i == 0)
        def _():
            dk_acc[...] = jnp.zeros_like(dk_acc)
            dv_acc[...] = jnp.zeros_like(dv_acc)

        r = lax.broadcasted_iota(jnp.int32, (TILE, TILE), 0)
        c = lax.broadcasted_iota(jnp.int32, (TILE, TILE), 1)
        strict = c < r
        u_after = _tri(TILE, lambda rr, cc: rr > cc)
        u_before = _tri(TILE, lambda rr, cc: rr < cc)
        qs = [_scaled_q(q_ref, _head_slices(hh)) for hh in range(2)]
        dos = [do_ref[:, _head_slices(hh)].astype(BF16) for hh in range(2)]

        def pass1(j, carries, diag):
            kblk = k_ref[_rows(k_ref, j), :]
            vblk = v_ref[_rows(v_ref, j), :]
            new = []
            for hh in range(2):
                sl = _head_slices(hh)
                z, sp, a, carry = _sb_tile(qs[hh], kblk[:, sl], carries[hh], strict, u_after, diag)
                gpan[hh, j] = a * _dot_nt(dos[hh], vblk[:, sl])
                span[hh, j] = jnp.exp(z - sp)
                dv_acc[hh, _rows(None, j), :] += _dot_tn(a.astype(BF16), dos[hh])
                new.append(carry)
            return tuple(new)

        zero1 = jnp.zeros((TILE, 1), F32)
        carries = pass1(i, (zero1, zero1), True)
        lax.fori_loop(0, i, lambda t, ca: pass1(i - 1 - t, ca, False), carries)

        def pass2(j, state, diag):
            kblk = k_ref[_rows(k_ref, j), :]
            new = []
            for hh in range(2):
                before, dq = state[hh]
                g = gpan[hh, j]
                sig = span[hh, j]
                pfx = _sum_l2(g, u_before) + before
                dz = g * (1.0 - sig) - sig * pfx
                if diag:
                    dz = jnp.where(strict, dz, 0.0)
                dzb = dz.astype(BF16)
                dk_acc[hh, _rows(None, j), :] += _dot_tn(dzb, qs[hh])
                new.append((pfx[:, TILE - 1:TILE] + g[:, TILE - 1:TILE], dq + _dot(dzb, kblk[:, _head_slices(hh)])))
            return tuple(new)

        zero2 = (zero1, jnp.zeros((TILE, HEAD_DIM), F32))
        state = lax.fori_loop(0, i, lambda j, st: pass2(j, st, False), (zero2, zero2))
        state = pass2(i, state, True)
        dq_ref[...] = jnp.concatenate([state[0][1] * SCALE, state[1][1] * SCALE], axis=1).astype(BF16)

        @pl.when(i == nb - 1)
        def _():
            dk_ref[...] = jnp.concatenate([dk_acc[0], dk_acc[1]], axis=1).astype(BF16)
            dv_ref[...] = jnp.concatenate([dv_acc[0], dv_acc[1]], axis=1).astype(BF16)

    qspec = pl.BlockSpec((TILE, LANES), lambda p, i: (i, p))
    kvspec = pl.BlockSpec((s, LANES), lambda p, i: (0, p))
    out = jax.ShapeDtypeStruct((s, SB_WIDTH), BF16)
    return pl.pallas_call(
        body, name=name, grid=(4, nb),
        in_specs=[pl.BlockSpec((TILE, LANES), lambda p, i: (i, cb + p)),
                  pl.BlockSpec((s, LANES), lambda p, i: (0, cb + 4 + p)),
                  pl.BlockSpec((s, LANES), lambda p, i: (0, cb + 8 + p)),
                  pl.BlockSpec((TILE, LANES), lambda p, i: (i, db + p))],
        out_specs=[qspec, kvspec, kvspec], out_shape=[out, out, out],
        scratch_shapes=[pltpu.VMEM((2, s, HEAD_DIM), F32), pltpu.VMEM((2, s, HEAD_DIM), F32),
                        pltpu.VMEM((2, nb, TILE, TILE), F32), pltpu.VMEM((2, nb, TILE, TILE), F32)],
        compiler_params=_params("arbitrary", "arbitrary"),
    )(pa, pa, pa, dout)


def _fox_scores(q, kj, cq, crj, causal, diag):
    sc = _dot_nt(q, kj) + (cq - crj)
    if diag:
        sc = jnp.where(causal, sc, NEG_INF)
    return sc


def _fox_fwd(name, pa, col0, ccol4, crow4):
    s = pa.shape[0]
    nb = s // TILE
    cb = col0 // LANES

    def body(q_ref, k_ref, v_ref, cc_ref, cr_ref, o_ref, lse_ref):
        i = pl.program_id(1)
        r = lax.broadcasted_iota(jnp.int32, (TILE, TILE), 0)
        c = lax.broadcasted_iota(jnp.int32, (TILE, TILE), 1)
        causal = c <= r
        qs = [_scaled_q(q_ref, _head_slices(hh)) for hh in range(2)]
        cqs = [cc_ref[:, HEAD_DIM * hh:HEAD_DIM * hh + 1] for hh in range(2)]

        def tile(j, state, diag):
            kblk = k_ref[_rows(k_ref, j), :]
            vblk = v_ref[_rows(v_ref, j), :]
            new = []
            for hh in range(2):
                sl = _head_slices(hh)
                m, l, acc = state[hh]
                sc = _fox_scores(qs[hh], kblk[:, sl], cqs[hh], cr_ref[j, hh:hh + 1, :], causal, diag)
                m2 = jnp.maximum(m, jnp.max(sc, axis=1, keepdims=True))
                alpha = jnp.exp(m - m2)
                p = jnp.exp(sc - m2)
                new.append((m2, l * alpha + jnp.sum(p, axis=1, keepdims=True),
                            acc * alpha + _dot(p.astype(BF16), vblk[:, sl])))
            return tuple(new)

        zero = (jnp.full((TILE, 1), NEG_INF, F32), jnp.zeros((TILE, 1), F32), jnp.zeros((TILE, HEAD_DIM), F32))
        state = tile(i, (zero, zero), True)
        state = lax.fori_loop(0, i, lambda t, st: tile(i - 1 - t, st, False), state)
        o_ref[...] = jnp.concatenate([state[hh][2] / state[hh][1] for hh in range(2)], axis=1)
        lse_ref[...] = jnp.concatenate(
            [jnp.broadcast_to(state[hh][0] + jnp.log(state[hh][1]), (TILE, HEAD_DIM)) for hh in range(2)], axis=1)

    return pl.pallas_call(
        body, name=name, grid=(4, nb),
        in_specs=[pl.BlockSpec((TILE, LANES), lambda p, i: (i, cb + p)),
                  pl.BlockSpec((s, LANES), lambda p, i: (0, cb + 4 + p)),
                  pl.BlockSpec((s, LANES), lambda p, i: (0, cb + 8 + p)),
                  pl.BlockSpec((None, TILE, LANES), lambda p, i: (p, i, 0)),
                  pl.BlockSpec((None, nb, 8, TILE), lambda p, i: (p, 0, 0, 0))],
        out_specs=[pl.BlockSpec((TILE, LANES), lambda p, i: (i, p)),
                   pl.BlockSpec((None, TILE, LANES), lambda p, i: (p, i, 0))],
        out_shape=[jax.ShapeDtypeStruct((s, FOX_WIDTH), F32), jax.ShapeDtypeStruct((4, s, LANES), F32)],
        compiler_params=_params("parallel", "arbitrary"),
    )(pa, pa, pa, ccol4, crow4)


def _fox_bwd(name, pa, col0, ccol4, crow4, out, lse, dout, dcol0):
    s = pa.shape[0]
    nb = s // TILE
    cb = col0 // LANES
    db = dcol0 // LANES

    def body(q_ref, k_ref, v_ref, cc_ref, cr_ref, o_ref, lse_ref, do_ref,
             dq_ref, dk_ref, dv_ref, cs_ref, dk_acc, dv_acc):
        i = pl.program_id(1)

        @pl.when(i == 0)
        def _():
            dk_acc[...] = jnp.zeros_like(dk_acc)
            dv_acc[...] = jnp.zeros_like(dv_acc)
            cs_ref[...] = jnp.zeros_like(cs_ref)

        r = lax.broadcasted_iota(jnp.int32, (TILE, TILE), 0)
        c = lax.broadcasted_iota(jnp.int32, (TILE, TILE), 1)
        causal = c <= r
        qs = [_scaled_q(q_ref, _head_slices(hh)) for hh in range(2)]
        cqs = [cc_ref[:, HEAD_DIM * hh:HEAD_DIM * hh + 1] for hh in range(2)]
        lses = [lse_ref[:, HEAD_DIM * hh:HEAD_DIM * hh + 1] for hh in range(2)]
        dofs = [do_ref[:, _head_slices(hh)] for hh in range(2)]
        dos = [d_.astype(BF16) for d_ in dofs]
        deltas = [jnp.sum(dofs[hh] * o_ref[:, _head_slices(hh)], axis=1, keepdims=True) for hh in range(2)]

        def tile(j, state, diag):
            kblk = k_ref[_rows(k_ref, j), :]
            vblk = v_ref[_rows(v_ref, j), :]
            new = []
            for hh in range(2):
                sl = _head_slices(hh)
                dq, rs = state[hh]
                sc = _fox_scores(qs[hh], kblk[:, sl], cqs[hh], cr_ref[j, hh:hh + 1, :], causal, diag)
                p = jnp.exp(sc - lses[hh])
                ds = p * (_dot_nt(dos[hh], vblk[:, sl]) - deltas[hh])
                dsb = ds.astype(BF16)
                dv_acc[hh, _rows(None, j), :] += _dot_tn(p.astype(BF16), dos[hh])
                dk_acc[hh, _rows(None, j), :] += _dot_tn(dsb, qs[hh])
                cs_ref[j, hh:hh + 1, :] += jnp.sum(ds, axis=0, keepdims=True)
                new.append((dq + _dot(dsb, kblk[:, sl]), rs + jnp.sum(ds, axis=1, keepdims=True)))
            return tuple(new)

        zero = (jnp.zeros((TILE, HEAD_DIM), F32), jnp.zeros((TILE, 1), F32))
        state = tile(i, (zero, zero), True)
        state = lax.fori_loop(0, i, lambda t, st: tile(i - 1 - t, st, False), state)
        for hh in range(2):
            cs_ref[i, hh:hh + 1, :] -= jnp.broadcast_to(state[hh][1], (TILE, LANES)).T[0:1, :]
        dq_ref[...] = jnp.concatenate([state[0][0] * SCALE, state[1][0] * SCALE], axis=1).astype(BF16)

        @pl.when(i == nb - 1)
        def _():
            dk_ref[...] = jnp.concatenate([dk_acc[0], dk_acc[1]], axis=1).astype(BF16)
            dv_ref[...] = jnp.concatenate([dv_acc[0], dv_acc[1]], axis=1).astype(BF16)

    qspec = pl.BlockSpec((TILE, LANES), lambda p, i: (i, p))
    kvspec = pl.BlockSpec((s, LANES), lambda p, i: (0, p))
    o3 = jax.ShapeDtypeStruct((s, FOX_WIDTH), BF16)
    return pl.pallas_call(
        body, name=name, grid=(4, nb),
        in_specs=[pl.BlockSpec((TILE, LANES), lambda p, i: (i, cb + p)),
                  pl.BlockSpec((s, LANES), lambda p, i: (0, cb + 4 + p)),
                  pl.BlockSpec((s, LANES), lambda p, i: (0, cb + 8 + p)),
                  pl.BlockSpec((None, TILE, LANES), lambda p, i: (p, i, 0)),
                  pl.BlockSpec((None, nb, 8, TILE), lambda p, i: (p, 0, 0, 0)),
                  qspec,
                  pl.BlockSpec((None, TILE, LANES), lambda p, i: (p, i, 0)),
                  pl.BlockSpec((TILE, LANES), lambda p, i: (i, db + p))],
        out_specs=[qspec, kvspec, kvspec, pl.BlockSpec((None, nb, 8, TILE), lambda p, i: (p, 0, 0, 0))],
        out_shape=[o3, o3, o3, jax.ShapeDtypeStruct((4, nb, 8, TILE), F32)],
        scratch_shapes=[pltpu.VMEM((2, s, HEAD_DIM), F32), pltpu.VMEM((2, s, HEAD_DIM), F32)],
        compiler_params=_params("arbitrary", "arbitrary"),
    )(pa, pa, pa, ccol4, crow4, out, lse, dout)


def _mem_fwd(name, pa, mkv):
    s = pa.shape[0]
    ml = mkv.shape[0]
    nb = s // TILE
    cb = QKV_WIDTH // LANES

    def body(q_ref, k_ref, v_ref, o_ref, lse_ref):
        outs, lses = [], []
        for hh in range(2):
            sl = _head_slices(hh)
            sc = _dot_nt(_scaled_q(q_ref, sl), k_ref[:, sl])
            m = jnp.max(sc, axis=1, keepdims=True)
            p = jnp.exp(sc - m)
            l = jnp.sum(p, axis=1, keepdims=True)
            outs.append(_dot(p.astype(BF16), v_ref[:, sl]) / l)
            lses.append(jnp.broadcast_to(m + jnp.log(l), (TILE, HEAD_DIM)))
        o_ref[...] = jnp.concatenate(outs, axis=1)
        lse_ref[...] = jnp.concatenate(lses, axis=1)

    return pl.pallas_call(
        body, name=name, grid=(2, nb),
        in_specs=[pl.BlockSpec((TILE, LANES), lambda p, i: (i, cb + p)),
                  pl.BlockSpec((ml, LANES), lambda p, i: (0, p)),
                  pl.BlockSpec((ml, LANES), lambda p, i: (0, 2 + p))],
        out_specs=[pl.BlockSpec((TILE, LANES), lambda p, i: (i, p)),
                   pl.BlockSpec((None, TILE, LANES), lambda p, i: (p, i, 0))],
        out_shape=[jax.ShapeDtypeStruct((s, MEM_WIDTH), F32), jax.ShapeDtypeStruct((2, s, LANES), F32)],
        compiler_params=_params("parallel", "parallel"),
    )(pa, mkv, mkv)


def _mem_bwd(name, pa, mkv, out, lse, dout, dcol0):
    s = pa.shape[0]
    ml = mkv.shape[0]
    nb = s // TILE
    cb = QKV_WIDTH // LANES
    db = dcol0 // LANES

    def body(q_ref, k_ref, v_ref, o_ref, lse_ref, do_ref, dq_ref, dk_ref, dv_ref, dk_acc, dv_acc):
        i = pl.program_id(1)

        @pl.when(i == 0)
        def _():
            dk_acc[...] = jnp.zeros_like(dk_acc)
            dv_acc[...] = jnp.zeros_like(dv_acc)

        dqs = []
        for hh in range(2):
            sl = _head_slices(hh)
            q = _scaled_q(q_ref, sl)
            kh = k_ref[:, sl]
            dof = do_ref[:, sl]
            do = dof.astype(BF16)
            delta = jnp.sum(dof * o_ref[:, sl], axis=1, keepdims=True)
            p = jnp.exp(_dot_nt(q, kh) - lse_ref[:, HEAD_DIM * hh:HEAD_DIM * hh + 1])
            ds = (p * (_dot_nt(do, v_ref[:, sl]) - delta)).astype(BF16)
            dv_acc[hh] += _dot_tn(p.astype(BF16), do)
            dk_acc[hh] += _dot_tn(ds, q)
            dqs.append(_dot(ds, kh) * SCALE)
        dq_ref[...] = jnp.concatenate(dqs, axis=1).astype(BF16)

        @pl.when(i == nb - 1)
        def _():
            dk_ref[...] = jnp.concatenate([dk_acc[0], dk_acc[1]], axis=1).astype(BF16)
            dv_ref[...] = jnp.concatenate([dv_acc[0], dv_acc[1]], axis=1).astype(BF16)

    qspec = pl.BlockSpec((TILE, LANES), lambda p, i: (i, p))
    kvspec = pl.BlockSpec((ml, LANES), lambda p, i: (0, p))
    okv = jax.ShapeDtypeStruct((ml, MEM_WIDTH), BF16)
    return pl.pallas_call(
        body, name=name, grid=(2, nb),
        in_specs=[pl.BlockSpec((TILE, LANES), lambda p, i: (i, cb + p)),
                  pl.BlockSpec((ml, LANES), lambda p, i: (0, p)),
                  pl.BlockSpec((ml, LANES), lambda p, i: (0, 2 + p)),
                  qspec,
                  pl.BlockSpec((None, TILE, LANES), lambda p, i: (p, i, 0)),
                  pl.BlockSpec((TILE, LANES), lambda p, i: (i, db + p))],
        out_specs=[qspec, kvspec, kvspec],
        out_shape=[jax.ShapeDtypeStruct((s, MEM_WIDTH), BF16), okv, okv],
        scratch_shapes=[pltpu.VMEM((2, ml, HEAD_DIM), F32), pltpu.VMEM((2, ml, HEAD_DIM), F32)],
        compiler_params=_params("arbitrary", "arbitrary"),
    )(pa, mkv, mkv, out, lse, dout)


def _head_maps():
    col = jnp.arange(MIX_WIDTH)[:, None] // HEAD_DIM
    g = (col == jnp.arange(LANES)[None, :]).astype(BF16)
    return g, g.T


def _normed_heads(osb_ref, ofx_ref, om_ref, g_ref, gt_ref):
    y = jnp.concatenate([osb_ref[...], ofx_ref[...], om_ref[...]], axis=1)
    msq = _sum_l3(y * y, g_ref[...]) * (1.0 / HEAD_DIM)
    rf = _sum_l3(lax.rsqrt(msq + EPS), gt_ref[...])
    return y * rf, rf


def _out_fwd(name, o_sb, o_fx, o_m, pb, ow, x, w_out, layer, ts):
    s, d = x.shape
    g, gt = _head_maps()

    def body(osb_ref, ofx_ref, om_ref, gate_ref, ow_ref, x_ref, w_ref, g_ref, gt_ref, xo_ref, y2_ref):
        yh, _ = _normed_heads(osb_ref, ofx_ref, om_ref, g_ref, gt_ref)
        gate = gate_ref[...]
        y2 = (yh * ow_ref[...] * (gate * jax.nn.sigmoid(gate))).astype(BF16)
        y2_ref[...] = y2
        xo_ref[...] = x_ref[...] + _dot(y2, w_ref[...])

    return pl.pallas_call(
        body, name=name, grid=(s // ts,),
        in_specs=[_row_spec(ts, SB_WIDTH), _row_spec(ts, FOX_WIDTH), _row_spec(ts, MEM_WIDTH),
                  _row_spec(ts, MIX_WIDTH), _const_spec((1, MIX_WIDTH)), _row_spec(ts, d),
                  pl.BlockSpec((None, MIX_WIDTH, d), lambda i: (layer, 0, 0)),
                  _const_spec((MIX_WIDTH, LANES)), _const_spec((LANES, MIX_WIDTH))],
        out_specs=[_row_spec(ts, d), _row_spec(ts, MIX_WIDTH)],
        out_shape=[jax.ShapeDtypeStruct((s, d), F32), jax.ShapeDtypeStruct((s, MIX_WIDTH), BF16)],
        compiler_params=_params("parallel"),
    )(o_sb, o_fx, o_m, pb, ow, x, w_out, g, gt)


def _row_spec(ts, w):
    return pl.BlockSpec((ts, w), lambda i: (i, 0))


def _const_spec(shape):
    return pl.BlockSpec(shape, lambda i: (0,) * len(shape))


def _out_bwd(name, dxb, o_sb, o_fx, o_m, pb, ow, w_out, layer, ts):
    s, d = dxb.shape
    g, gt = _head_maps()

    def body(dx_ref, osb_ref, ofx_ref, om_ref, gate_ref, ow_ref, w_ref, g_ref, gt_ref, dy_ref, dgate_ref, dow_ref):
        @pl.when(pl.program_id(0) == 0)
        def _():
            dow_ref[...] = jnp.zeros_like(dow_ref)

        dy2 = _dot_nt(dx_ref[...], w_ref[...])
        yh, rf = _normed_heads(osb_ref, ofx_ref, om_ref, g_ref, gt_ref)
        gate = gate_ref[...]
        sig = jax.nn.sigmoid(gate)
        ow_v = ow_ref[...]
        dgate_ref[...] = (dy2 * (yh * ow_v) * (sig * (1.0 + gate * (1.0 - sig)))).astype(BF16)
        dn = dy2 * (gate * sig)
        dow_ref[...] += jnp.sum(dn * yh, axis=0, keepdims=True)
        dyh = dn * ow_v
        t = _sum_l3(dyh * yh, g_ref[...]) * (1.0 / HEAD_DIM)
        dy_ref[...] = rf * (dyh - yh * _sum_l3(t, gt_ref[...]))

    return pl.pallas_call(
        body, name=name, grid=(s // ts,),
        in_specs=[_row_spec(ts, d), _row_spec(ts, SB_WIDTH), _row_spec(ts, FOX_WIDTH), _row_spec(ts, MEM_WIDTH),
                  _row_spec(ts, MIX_WIDTH), _const_spec((1, MIX_WIDTH)),
                  pl.BlockSpec((None, MIX_WIDTH, d), lambda i: (layer, 0, 0)),
                  _const_spec((MIX_WIDTH, LANES)), _const_spec((LANES, MIX_WIDTH))],
        out_specs=[_row_spec(ts, MIX_WIDTH), _row_spec(ts, MIX_WIDTH), _const_spec((1, MIX_WIDTH))],
        out_shape=[jax.ShapeDtypeStruct((s, MIX_WIDTH), F32), jax.ShapeDtypeStruct((s, MIX_WIDTH), BF16),
                   jax.ShapeDtypeStruct((1, MIX_WIDTH), F32)],
        compiler_params=_params("arbitrary"),
    )(dxb, o_sb, o_fx, o_m, pb, ow, w_out, g, gt)


def _adamw(name, w, g, m, v, tr):
    rows, cols = w.shape

    def body(w_ref, g_ref, m_ref, v_ref, d_ref, m2_ref, v2_ref):
        gv = g_ref[...]
        m2 = ADAM_B1 * m_ref[...] + (1.0 - ADAM_B1) * gv
        v2 = ADAM_B2 * v_ref[...] + (1.0 - ADAM_B2) * (gv * gv)
        m_hat = m2 / (1.0 - ADAM_B1 ** ADAM_STEP)
        v_hat = v2 / (1.0 - ADAM_B2 ** ADAM_STEP)
        d_ref[...] = -ADAM_LR * (m_hat / (jnp.sqrt(v_hat) + ADAM_EPS) + ADAM_WD * w_ref[...])
        m2_ref[...] = m2
        v2_ref[...] = v2

    spec = _row_spec(tr, cols)
    shp = jax.ShapeDtypeStruct((rows, cols), F32)
    return pl.pallas_call(
        body, name=name, grid=(rows // tr,), in_specs=[spec] * 4, out_specs=[spec] * 3, out_shape=[shp] * 3,
        compiler_params=_params("parallel"),
    )(w, g, m, v)


HBM_SPEC = pl.BlockSpec(memory_space=pltpu.HBM)


def _place():
    x, y, c = lax.axis_index("x"), lax.axis_index("y"), lax.axis_index("c")
    chips = [(1 - x, y), (x, 1 - y), (1 - x, 1 - y)]
    return x, y, c, chips


def _remote(src, dst, send_sems, recv_sems, k, to):
    return pltpu.make_async_remote_copy(src_ref=src, dst_ref=dst, send_sem=send_sems.at[k], recv_sem=recv_sems.at[k],
                                        device_id=to, device_id_type=MESH)


def _gather_weights(name, bucket):
    r, w = bucket.shape
    rh = r // 2

    def body(in_ref, out_ref, send_sems, recv_sems, local_sem):
        x, y, c, chips = _place()
        mine = 2 * x + y
        sibling = (x, y, 1 - c)

        def half(cc):
            return pl.ds(pl.multiple_of(cc * rh, 16), rh)

        local = pltpu.make_async_copy(in_ref, out_ref.at[mine], local_sem)
        local.start()
        first = [_remote(in_ref.at[half(c)], out_ref.at[mine, half(c)], send_sems, recv_sems, j, (cx, cy, c))
                 for j, (cx, cy) in enumerate(chips)]
        for cp in first:
            cp.start()
        passed = []
        for j, (cx, cy) in enumerate(chips):
            landed = out_ref.at[2 * cx + cy, half(c)]
            _remote(landed, landed, send_sems, recv_sems, j, sibling).wait_recv()
            cp = _remote(landed, landed, send_sems, recv_sems, 3 + j, sibling)
            cp.start()
            passed.append(cp)
        for j, (cx, cy) in enumerate(chips):
            other = out_ref.at[2 * cx + cy, half(1 - c)]
            _remote(other, other, send_sems, recv_sems, 3 + j, sibling).wait_recv()
        for cp in first + passed:
            cp.wait_send()
        local.wait()

    return pl.pallas_call(
        body, name=name, in_specs=[HBM_SPEC], out_specs=HBM_SPEC,
        out_shape=jax.ShapeDtypeStruct((N_CHIPS, r, w), bucket.dtype),
        scratch_shapes=[pltpu.SemaphoreType.DMA((6,)), pltpu.SemaphoreType.DMA((6,)), pltpu.SemaphoreType.DMA],
    )(bucket)


def _swap_halves(name, g4):
    n, r, w = g4.shape
    rh = r // 2

    def body(in_ref, out_ref, send_sems, recv_sems):
        x, y, c, _ = _place()
        src = in_ref.at[:, pl.ds(pl.multiple_of((1 - c) * rh, 16), rh), :]
        cp = _remote(src, out_ref, send_sems, recv_sems, 0, (x, y, 1 - c))
        cp.start()
        cp.wait()

    return pl.pallas_call(
        body, name=name, in_specs=[HBM_SPEC], out_specs=HBM_SPEC,
        out_shape=jax.ShapeDtypeStruct((n, rh, w), g4.dtype),
        scratch_shapes=[pltpu.SemaphoreType.DMA((1,)), pltpu.SemaphoreType.DMA((1,))],
    )(g4)


def _add_half(name, g4, r1, cvec, tr):
    n, r, w = g4.shape
    rh = r // 2
    nblk = rh // tr

    def body(c_ref, a_ref, b_ref, o_ref):
        o_ref[...] = a_ref[...] + b_ref[...]

    return pl.pallas_call(
        body, name=name,
        grid_spec=pltpu.PrefetchScalarGridSpec(
            num_scalar_prefetch=1, grid=(n, nblk),
            in_specs=[pl.BlockSpec((None, tr, w), lambda k, i, c_ref: (k, c_ref[0] * nblk + i, 0)),
                      pl.BlockSpec((None, tr, w), lambda k, i, c_ref: (k, i, 0))],
            out_specs=pl.BlockSpec((None, tr, w), lambda k, i, c_ref: (k, i, 0))),
        out_shape=jax.ShapeDtypeStruct((n, rh, w), F32),
        compiler_params=_params("parallel", "parallel"),
    )(cvec, g4, r1)


def _scatter_chips(name, h4):
    n, rh, w = h4.shape

    def body(in_ref, out_ref, send_sems, recv_sems, local_sem):
        x, y, c, chips = _place()
        mine = 2 * x + y
        local = pltpu.make_async_copy(in_ref.at[mine], out_ref.at[mine], local_sem)
        local.start()
        sends = [_remote(in_ref.at[2 * cx + cy], out_ref.at[mine], send_sems, recv_sems, j, (cx, cy, c))
                 for j, (cx, cy) in enumerate(chips)]
        for cp in sends:
            cp.start()
        for j, (cx, cy) in enumerate(chips):
            got = out_ref.at[2 * cx + cy]
            _remote(got, got, send_sems, recv_sems, j, (cx, cy, c)).wait_recv()
        for cp in sends:
            cp.wait_send()
        local.wait()

    return pl.pallas_call(
        body, name=name, in_specs=[HBM_SPEC], out_specs=HBM_SPEC,
        out_shape=jax.ShapeDtypeStruct((n, rh, w), h4.dtype),
        scratch_shapes=[pltpu.SemaphoreType.DMA((3,)), pltpu.SemaphoreType.DMA((3,)), pltpu.SemaphoreType.DMA],
    )(h4)


def _sum4(name, r2, tr):
    n, rh, w = r2.shape

    def body(a_ref, b_ref, c_ref, d_ref, o_ref):
        o_ref[...] = ((a_ref[...] + b_ref[...]) + c_ref[...]) + d_ref[...]

    specs = [pl.BlockSpec((None, tr, w), functools.partial(lambda k, i: (k, i, 0), k)) for k in range(n)]
    return pl.pallas_call(
        body, name=name, grid=(rh // tr,), in_specs=specs, out_specs=pl.BlockSpec((tr, w), lambda i: (i, 0)),
        out_shape=jax.ShapeDtypeStruct((rh, w), F32),
        compiler_params=_params("parallel"),
    )(r2, r2, r2, r2)


def _join_halves(name, gh):
    rh, w = gh.shape

    def body(in_ref, out_ref, send_sems, recv_sems, local_sem):
        x, y, c, _ = _place()
        mine = out_ref.at[pl.ds(pl.multiple_of(c * rh, 16), rh), :]
        theirs = out_ref.at[pl.ds(pl.multiple_of((1 - c) * rh, 16), rh), :]
        local = pltpu.make_async_copy(in_ref, mine, local_sem)
        local.start()
        cp = _remote(in_ref, mine, send_sems, recv_sems, 0, (x, y, 1 - c))
        cp.start()
        _remote(theirs, theirs, send_sems, recv_sems, 0, (x, y, 1 - c)).wait_recv()
        cp.wait_send()
        local.wait()

    return pl.pallas_call(
        body, name=name, in_specs=[HBM_SPEC], out_specs=HBM_SPEC,
        out_shape=jax.ShapeDtypeStruct((2 * rh, w), gh.dtype),
        scratch_shapes=[pltpu.SemaphoreType.DMA((1,)), pltpu.SemaphoreType.DMA((1,)), pltpu.SemaphoreType.DMA],
    )(gh)


def _allreduce_small(name, vec):
    rows, w = vec.shape

    def body(v_ref, o_ref, buf, send_sems, recv_sems):
        x, y, c, _ = _place()
        me = 4 * x + 2 * y + c
        buf[me] = v_ref[...]
        flips = [(fx, fy, fc) for fx in (0, 1) for fy in (0, 1) for fc in (0, 1)][1:]
        peers = [(x + fx - 2 * x * fx, y + fy - 2 * y * fy, c + fc - 2 * c * fc) for fx, fy, fc in flips]
        sends = [_remote(v_ref, buf.at[me], send_sems, recv_sems, k, peer) for k, peer in enumerate(peers)]
        for cp in sends:
            cp.start()
        for k, (px, py, pc) in enumerate(peers):
            got = buf.at[4 * px + 2 * py + pc]
            _remote(got, got, send_sems, recv_sems, k, (px, py, pc)).wait_recv()
        for cp in sends:
            cp.wait_send()
        acc = buf[0]
        for dev in range(1, N_DEV):
            acc = acc + buf[dev]
        o_ref[...] = acc

    vm = pl.BlockSpec(memory_space=pltpu.VMEM)
    return pl.pallas_call(
        body, name=name, in_specs=[vm], out_specs=vm, out_shape=jax.ShapeDtypeStruct((rows, w), F32),
        scratch_shapes=[pltpu.VMEM((N_DEV, rows, w), F32), pltpu.SemaphoreType.DMA((7,)), pltpu.SemaphoreType.DMA((7,))],
    )(vec)


BUCKET_WIDTH = 1024
GATE_COL = 3 * SB_WIDTH + 3 * FOX_WIDTH + FOX_HEADS + MEM_WIDTH
FL_COL = QKV_WIDTH


def _to_groups(w):
    pad = jnp.zeros(w.shape[:-1] + (LANES - FOX_HEADS,), w.dtype)
    wa = jnp.concatenate([w[..., :QKV_WIDTH], w[..., FL_COL + FOX_HEADS:GATE_COL]], axis=-1)
    wb = jnp.concatenate([w[..., GATE_COL:], w[..., FL_COL:FL_COL + FOX_HEADS], pad], axis=-1)
    return wa, wb


def _from_groups(ga, gb):
    return jnp.concatenate([ga[..., :QKV_WIDTH], gb[..., MIX_WIDTH:MIX_WIDTH + FOX_HEADS], ga[..., QKV_WIDTH:],
                            gb[..., :MIX_WIDTH]], axis=-1)


def _tile_of(n, cap, unit):
    if n <= cap:
        return n
    best = None
    for t in range(unit, cap + 1, unit):
        if n % t == 0:
            best = t
    assert best is not None, (n, cap, unit)
    return best


def _bucket_rows(n_rows):
    half = -(-n_rows // 32) * 16
    return 2 * half


def _part_rows(size):
    return -(-size // (16 * BUCKET_WIDTH)) * 16


def _pack_rows(parts, rows):
    flat = []
    for p in parts:
        f = p.reshape(-1, BUCKET_WIDTH)
        flat.append(jnp.pad(f, ((0, _part_rows(p.size) - f.shape[0]), (0, 0))))
    flat = jnp.concatenate(flat, axis=0)
    return jnp.pad(flat, ((0, rows - flat.shape[0]), (0, 0)))


def _unpack_rows(bucket, like):
    outs, r = [], 0
    for a in like:
        outs.append(bucket[r:r + a.size // BUCKET_WIDTH].reshape(a.shape))
        r += _part_rows(a.size)
    return outs


def _pack_small(parts):
    rows = []
    for p in parts:
        f = p.reshape(-1).astype(F32)
        f = jnp.pad(f, (0, (-f.shape[0]) % LANES))
        rows.append(f.reshape(-1, LANES))
    out = jnp.concatenate(rows, axis=0)
    return jnp.pad(out, ((0, (-out.shape[0]) % 8), (0, 0)))


def _unpack_small(packed, shapes):
    outs, r = [], 0
    for shp in shapes:
        n = 1
        for s_ in shp:
            n *= s_
        nr = -(-n // LANES)
        outs.append(packed[r:r + nr].reshape(-1)[:n].reshape(shp))
        r += nr
    return outs


def kernel(x, mem, norm_w, w_in, b_forget, mem_norm_w, w_mem_kv, out_norm_w, w_out, final_norm_w, loss_target, m_norm_w, m_w_in, m_b_forget, m_mem_norm_w, m_w_mem_kv, m_out_norm_w, m_w_out, m_final_norm_w, v_norm_w, v_w_in, v_b_forget, v_mem_norm_w, v_w_mem_kv, v_out_norm_w, v_w_out, v_final_norm_w):
    xs = x[0]
    mems = mem[0]
    target = loss_target[0]
    s, d = xs.shape
    depth = norm_w.shape[0]
    nb = s // TILE
    ts = _tile_of(s, 256, 8)
    big = (w_in, w_mem_kv, w_out)
    rows = _bucket_rows(sum(_part_rows(a.size) for a in big))

    gathered = _gather_weights("gather_weights", _pack_rows([a.astype(BF16) for a in big], rows))
    shards = [_unpack_rows(gathered[j], big) for j in range(N_CHIPS)]
    w_in_full = jnp.concatenate([sh[0] for sh in shards], axis=2)
    wkv = jnp.concatenate([sh[1] for sh in shards], axis=1)
    wout = jnp.concatenate([sh[2] for sh in shards], axis=1)
    wa, wb = _to_groups(w_in_full)

    tm = _tile_of(s, 256, 8)
    fl_block = MIX_WIDTH // LANES

    saved = []
    cur = xs
    for l in range(depth):
        h = _rms_fwd(f"rms_fwd{l}", cur, norm_w[l][None], ts)
        pa = _mm(f"inproj_a{l}", h, wa, "nn", tm, _tile_of(PA, 1664, LANES), BF16, b_lead=(l,))
        pb = _mm(f"inproj_b{l}", h, wb, "nn", tm, PB, F32, b_lead=(l,))
        bpad = jnp.pad(b_forget[l], (0, LANES - FOX_HEADS))[None]
        ccol, crow = _gate_fwd(f"gate_fwd{l}", pb, bpad, fl_block)
        ccol4 = jnp.repeat(ccol[:, :FOX_HEADS].reshape(s, 4, 2).transpose(1, 0, 2), HEAD_DIM, axis=2)
        crow4 = jnp.pad(crow.reshape(nb, 4, 2, TILE).transpose(1, 0, 2, 3), ((0, 0), (0, 0), (0, 6), (0, 0)))
        o_sb = _sb_fwd(f"sb_fwd{l}", pa, 0)
        o_fx, lse_fx = _fox_fwd(f"fox_fwd{l}", pa, 3 * SB_WIDTH, ccol4, crow4)
        mn = _rms_fwd(f"mem_rms{l}", mems, mem_norm_w[l][None], mems.shape[0])
        mkv = _mm(f"mem_kv{l}", mn, wkv, "nn", mems.shape[0], 2 * MEM_WIDTH, BF16, b_lead=(l,))
        o_m, lse_m = _mem_fwd(f"mem_fwd{l}", pa, mkv)
        nxt, y2 = _out_fwd(f"out_fwd{l}", o_sb, o_fx, o_m, pb, out_norm_w[l][None], cur, wout, l, ts)
        saved.append((cur, h, pa, pb, bpad, ccol4, crow4, o_sb, o_fx, lse_fx, mn, mkv, o_m, lse_m, y2))
        cur = nxt

    loss_v, dx, dxb, g_final = _final_loss("final_loss", cur, final_norm_w[None], target, ts)

    g_norm, g_b, g_memnorm, g_outnorm = [None] * depth, [None] * depth, [None] * depth, [None] * depth
    g_wa, g_wb, g_wkv, g_wout = [None] * depth, [None] * depth, [None] * depth, [None] * depth
    for l in reversed(range(depth)):
        xin, h, pa, pb, bpad, ccol4, crow4, o_sb, o_fx, lse_fx, mn, mkv, o_m, lse_m, y2 = saved[l]
        dy, dgate, g_outnorm[l] = _out_bwd(f"out_bwd{l}", dxb, o_sb, o_fx, o_m, pb, out_norm_w[l][None], wout, l, ts)
        g_wout[l] = _mm(f"dw_out{l}", y2, dxb, "tn", _tile_of(MIX_WIDTH, 640, LANES), d, F32)
        dq_sb, dk_sb, dv_sb = _sb_bwd(f"sb_bwd{l}", pa, 0, dy, 0)
        dq_fx, dk_fx, dv_fx, cs4 = _fox_bwd(f"fox_bwd{l}", pa, 3 * SB_WIDTH, ccol4, crow4, o_fx, lse_fx, dy, SB_WIDTH)
        colsum = cs4[:, :, :2, :].transpose(1, 0, 2, 3).reshape(nb, 8, TILE)
        dlogit, g_b[l] = _gate_bwd(f"gate_bwd{l}", pb, bpad, colsum, fl_block)
        dq_m, dk_m, dv_m = _mem_bwd(f"mem_bwd{l}", pa, mkv, o_m, lse_m, dy, SB_WIDTH + FOX_WIDTH)
        dmkv = jnp.concatenate([dk_m, dv_m], axis=1)
        g_wkv[l] = _mm(f"dw_kv{l}", mn, dmkv, "tn", d, 2 * MEM_WIDTH, F32)
        dmn = _mm(f"dmem{l}", dmkv, wkv, "nt", mems.shape[0], d, F32, b_lead=(l,))
        g_memnorm[l] = _rms_wgrad(f"mem_norm_grad{l}", mems, dmn)
        dpa = jnp.concatenate([dq_sb, dk_sb, dv_sb, dq_fx, dk_fx, dv_fx, dq_m], axis=1)
        dpb = jnp.concatenate([dgate, dlogit], axis=1)
        tw = _tile_of(d, 512, LANES)
        g_wa[l] = _mm(f"dw_in_a{l}", h, dpa, "tn", tw, _tile_of(PA, 1664, LANES), F32)
        g_wb[l] = _mm(f"dw_in_b{l}", h, dpb, "tn", tw, PB, F32)
        dh = _mm(f"dh_a{l}", dpa, wa, "nt", tm, d, F32, b_lead=(l,))
        dh = _mm(f"dh_b{l}", dpb, wb, "nt", tm, d, F32, res=dh, b_lead=(l,))
        dx, dxb, g_norm[l] = _rms_bwd(f"rms_bwd{l}", xin, norm_w[l][None], dh, dx, ts)

    gw_in = _from_groups(jnp.stack(g_wa), jnp.stack(g_wb))
    gw_kv = jnp.stack(g_wkv)
    gw_out = jnp.stack(g_wout)
    cw, kr, orows = w_in.shape[2], w_mem_kv.shape[1], w_out.shape[1]
    g4 = jnp.stack([
        _pack_rows([gw_in[:, :, j * cw:(j + 1) * cw], gw_kv[:, j * kr:(j + 1) * kr], gw_out[:, j * orows:(j + 1) * orows]], rows)
        for j in range(N_CHIPS)])
    tr = _tile_of(rows // 2, 808, 8)
    cvec = lax.axis_index("c").astype(jnp.int32).reshape(1)
    from_sibling = _swap_halves("grad_swap_halves", g4)
    chip_sum = _add_half("grad_add_half", g4, from_sibling, cvec, tr)
    from_chips = _scatter_chips("grad_scatter_chips", chip_sum)
    g_half = _sum4("grad_sum_chips", from_chips, tr)
    g_bucket = _join_halves("grad_join_halves", g_half)
    big_grads = _unpack_rows(g_bucket, big)

    small_w = [norm_w, b_forget, mem_norm_w, out_norm_w, final_norm_w]
    small_m = [m_norm_w, m_b_forget, m_mem_norm_w, m_out_norm_w, m_final_norm_w]
    small_v = [v_norm_w, v_b_forget, v_mem_norm_w, v_out_norm_w, v_final_norm_w]
    small_shapes = [a.shape for a in small_w]
    local_small = [jnp.concatenate(g_norm, axis=0), jnp.stack([g[0, :FOX_HEADS] for g in g_b]),
                   jnp.concatenate(g_memnorm, axis=0), jnp.concatenate(g_outnorm, axis=0), g_final[0]]
    reduced = _allreduce_small("small_allreduce", _pack_small(local_small + [loss_v[0, :1]]))
    small_grads = _unpack_small(reduced, small_shapes)
    n_small_rows = sum(-(-a.size // LANES) for a in small_w)
    loss = reduced[n_small_rows, 0]

    d_s, m_s, v_s = _adamw("adamw_small", _pack_small(small_w), _pack_small(small_grads), _pack_small(small_m),
                           _pack_small(small_v), _pack_small(small_w).shape[0])
    small_delta, small_m2, small_v2 = (_unpack_small(a, small_shapes) for a in (d_s, m_s, v_s))
    big_delta, big_m2, big_v2 = [], [], []
    for nm, w_, g_, m_, v_ in zip(("w_in", "w_mem_kv", "w_out"), big, big_grads, (m_w_in, m_w_mem_kv, m_w_out),
                                  (v_w_in, v_w_mem_kv, v_w_out)):
        flat = (-1, w_.shape[-1])
        r_ = w_.size // w_.shape[-1]
        outs = _adamw(f"adamw_{nm}", w_.reshape(flat), g_.reshape(flat), m_.reshape(flat), v_.reshape(flat),
                      _tile_of(r_, 256, 8))
        for lst, o in zip((big_delta, big_m2, big_v2), outs):
            lst.append(o.reshape(w_.shape))

    def order(sm, bg):
        return [sm[0], bg[0], sm[1], sm[2], bg[1], sm[3], bg[2], sm[4]]

    return (loss, dx[None], *order(small_grads, big_grads), *order(small_delta, big_delta),
            *order(small_m2, big_m2), *order(small_v2, big_v2))
```

```python
import functools

import jax
import jax.numpy as jnp
from jax import lax
from jax.experimental import pallas as pl
from jax.experimental.pallas import tpu as pltpu

F32 = jnp.float32
BF16 = jnp.bfloat16

HEAD_DIM = 64
SB_WIDTH = 512
FOX_WIDTH = 512
FOX_HEADS = 8
MEM_WIDTH = 256
MIX_WIDTH = SB_WIDTH + FOX_WIDTH + MEM_WIDTH
TOTAL_HEADS = MIX_WIDTH // HEAD_DIM
IN_WIDTH = 3 * SB_WIDTH + 3 * FOX_WIDTH + FOX_HEADS + MEM_WIDTH + MIX_WIDTH
LANES = 128
QKV_WIDTH = 3 * SB_WIDTH + 3 * FOX_WIDTH
PA = QKV_WIDTH + MEM_WIDTH
PB = LANES + MIX_WIDTH
EPS = 1e-6
SCALE = HEAD_DIM ** -0.5
TILE = 256
NEG_INF = float("-inf")

ADAM_LR = 0.001
ADAM_B1 = 0.9
ADAM_B2 = 0.999
ADAM_EPS = 1e-08
ADAM_WD = 0.01
ADAM_STEP = 10

N_CHIPS = 4
N_DEV = 8
VMEM_LIMIT = 48 * 1024 * 1024
MESH = pl.DeviceIdType.MESH


def _params(*sem):
    return pltpu.CompilerParams(dimension_semantics=tuple(sem), vmem_limit_bytes=VMEM_LIMIT)


def _dot(a, b):
    return jnp.dot(a, b, preferred_element_type=F32)


def _dot_nt(a, b):
    return lax.dot_general(a, b, (((1,), (1,)), ((), ())), preferred_element_type=F32)


def _dot_tn(a, b):
    return lax.dot_general(a, b, (((0,), (0,)), ((), ())), preferred_element_type=F32)


def _split2(x):
    hi = x.astype(BF16)
    lo = (x - hi.astype(F32)).astype(BF16)
    return hi, lo


def _split3(x):
    hi = x.astype(BF16)
    r = x - hi.astype(F32)
    mid = r.astype(BF16)
    lo = (r - mid.astype(F32)).astype(BF16)
    return hi, mid, lo


def _sum_l2(x, u):
    hi, lo = _split2(x)
    return _dot(hi, u) + _dot(lo, u)


def _sum_l3(x, u):
    hi, mid, lo = _split3(x)
    return _dot(hi, u) + _dot(mid, u) + _dot(lo, u)


def _sum_r3(u, x):
    hi, mid, lo = _split3(x)
    return _dot(u, hi) + _dot(u, mid) + _dot(u, lo)


def _softplus(z):
    return jnp.maximum(z, 0.0) + jnp.log1p(jnp.exp(-jnp.abs(z)))


def _tri(n, pred):
    r = lax.broadcasted_iota(jnp.int32, (n, n), 0)
    c = lax.broadcasted_iota(jnp.int32, (n, n), 1)
    return jnp.where(pred(r, c), 1.0, 0.0).astype(BF16)


def _rows(ref, j, n=TILE):
    return pl.ds(pl.multiple_of(j * n, n), n)


def _mm(name, a, b, mode, tm, tn, out_dtype, res=None, a_lead=(), b_lead=()):
    a2, b2 = a.shape[len(a_lead):], b.shape[len(b_lead):]
    if mode == "tn":
        k, m = a2
    else:
        m, k = a2
    n = b2[0] if mode == "nt" else b2[1]
    assert m % tm == 0 and n % tn == 0, (name, m, tm, n, tn)
    na, nb = (None,) * len(a_lead), (None,) * len(b_lead)
    if mode == "tn":
        a_spec = pl.BlockSpec(na + (k, tm), lambda j, i: a_lead + (0, i))
    else:
        a_spec = pl.BlockSpec(na + (tm, k), lambda j, i: a_lead + (i, 0))
    if mode == "nt":
        b_spec = pl.BlockSpec(nb + (tn, k), lambda j, i: b_lead + (j, 0))
    else:
        b_spec = pl.BlockSpec(nb + (k, tn), lambda j, i: b_lead + (0, j))
    o_spec = pl.BlockSpec((tm, tn), lambda j, i: (i, j))
    dot = {"nn": _dot, "nt": _dot_nt, "tn": _dot_tn}[mode]

    def body(a_ref, b_ref, *rest):
        o_ref = rest[-1]
        acc = dot(a_ref[...].astype(BF16), b_ref[...].astype(BF16))
        if res is not None:
            acc = acc + rest[0][...]
        o_ref[...] = acc.astype(o_ref.dtype)

    args, specs = [a, b], [a_spec, b_spec]
    if res is not None:
        args.append(res)
        specs.append(o_spec)
    return pl.pallas_call(
        body, name=name, grid=(n // tn, m // tm), in_specs=specs, out_specs=o_spec,
        out_shape=jax.ShapeDtypeStruct((m, n), out_dtype),
        compiler_params=_params("parallel", "parallel"),
    )(*args)


def _rms_fwd(name, x, g, ts):
    s, d = x.shape

    def body(x_ref, g_ref, o_ref):
        xf = x_ref[...]
        r = lax.rsqrt(jnp.mean(xf * xf, axis=1, keepdims=True) + EPS)
        o_ref[...] = (xf * r * g_ref[...]).astype(BF16)

    return pl.pallas_call(
        body, name=name, grid=(s // ts,),
        in_specs=[pl.BlockSpec((ts, d), lambda i: (i, 0)), pl.BlockSpec((1, d), lambda i: (0, 0))],
        out_specs=pl.BlockSpec((ts, d), lambda i: (i, 0)),
        out_shape=jax.ShapeDtypeStruct((s, d), BF16),
        compiler_params=_params("parallel"),
    )(x, g)


def _rms_bwd(name, x, g, dh, dres, ts):
    s, d = x.shape

    def body(x_ref, g_ref, dh_ref, dres_ref, dx_ref, dxb_ref, dg_ref):
        @pl.when(pl.program_id(0) == 0)
        def _():
            dg_ref[...] = jnp.zeros_like(dg_ref)

        xf = x_ref[...]
        r = lax.rsqrt(jnp.mean(xf * xf, axis=1, keepdims=True) + EPS)
        xh = xf * r
        dhf = dh_ref[...]
        dg_ref[...] += jnp.sum(dhf * xh, axis=0, keepdims=True)
        dxh = dhf * g_ref[...]
        m = jnp.mean(dxh * xh, axis=1, keepdims=True)
        dx = r * (dxh - xh * m) + dres_ref[...]
        dx_ref[...] = dx
        dxb_ref[...] = dx.astype(BF16)

    row = pl.BlockSpec((ts, d), lambda i: (i, 0))
    vec = pl.BlockSpec((1, d), lambda i: (0, 0))
    return pl.pallas_call(
        body, name=name, grid=(s // ts,), in_specs=[row, vec, row, row], out_specs=[row, row, vec],
        out_shape=[jax.ShapeDtypeStruct((s, d), F32), jax.ShapeDtypeStruct((s, d), BF16),
                   jax.ShapeDtypeStruct((1, d), F32)],
        compiler_params=_params("arbitrary"),
    )(x, g, dh, dres)


def _rms_wgrad(name, x, dh):
    m_, d = x.shape

    def body(x_ref, dh_ref, dg_ref):
        xf = x_ref[...]
        r = lax.rsqrt(jnp.mean(xf * xf, axis=1, keepdims=True) + EPS)
        dg_ref[...] = jnp.sum(dh_ref[...] * xf * r, axis=0, keepdims=True)

    return pl.pallas_call(
        body, name=name, out_shape=jax.ShapeDtypeStruct((1, d), F32),
    )(x, dh)


def _final_loss(name, x, g, target, ts):
    s, d = x.shape

    def body(x_ref, g_ref, t_ref, loss_ref, dx_ref, dxb_ref, dg_ref):
        @pl.when(pl.program_id(0) == 0)
        def _():
            dg_ref[...] = jnp.zeros_like(dg_ref)
            loss_ref[...] = jnp.zeros_like(loss_ref)

        xf = x_ref[...]
        gw = g_ref[...]
        r = lax.rsqrt(jnp.mean(xf * xf, axis=1, keepdims=True) + EPS)
        xh = xf * r
        e = xh * gw - t_ref[...]
        part = 0.5 * jnp.sum(jnp.mean(e * e, axis=1, keepdims=True), axis=0, keepdims=True)
        loss_ref[...] += jnp.broadcast_to(part, loss_ref.shape)
        dy = e * (1.0 / d)
        dg_ref[...] += jnp.sum(dy * xh, axis=0, keepdims=True)
        dxh = dy * gw
        m = jnp.mean(dxh * xh, axis=1, keepdims=True)
        dx = r * (dxh - xh * m)
        dx_ref[...] = dx
        dxb_ref[...] = dx.astype(BF16)

    row = pl.BlockSpec((ts, d), lambda i: (i, 0))
    vec = pl.BlockSpec((1, d), lambda i: (0, 0))
    lvec = pl.BlockSpec((1, LANES), lambda i: (0, 0))
    return pl.pallas_call(
        body, name=name, grid=(s // ts,), in_specs=[row, vec, row], out_specs=[lvec, row, row, vec],
        out_shape=[jax.ShapeDtypeStruct((1, LANES), F32), jax.ShapeDtypeStruct((s, d), F32),
                   jax.ShapeDtypeStruct((s, d), BF16), jax.ShapeDtypeStruct((1, d), F32)],
        compiler_params=_params("arbitrary"),
    )(x, g, target)


def _gate_fwd(name, pb, bpad, fl_block):
    s = pb.shape[0]
    nb = s // TILE

    def body(fl_ref, b_ref, ccol_ref, crow_ref, carry):
        @pl.when(pl.program_id(0) == 0)
        def _():
            carry[...] = jnp.zeros_like(carry)

        u = fl_ref[...] + b_ref[...]
        lf = jnp.minimum(u, 0.0) - jnp.log1p(jnp.exp(-jnp.abs(u)))
        lower = _tri(TILE, lambda r, c: c <= r)
        c = _sum_r3(lower, lf) + carry[0:1, :]
        ccol_ref[...] = c
        crow_ref[0] = c.T[0:8, :]
        carry[...] = jnp.broadcast_to(c[TILE - 1:TILE, :], carry.shape)

    return pl.pallas_call(
        body, name=name, grid=(nb,),
        in_specs=[pl.BlockSpec((TILE, LANES), lambda i: (i, fl_block)), pl.BlockSpec((1, LANES), lambda i: (0, 0))],
        out_specs=[pl.BlockSpec((TILE, LANES), lambda i: (i, 0)), pl.BlockSpec((1, 8, TILE), lambda i: (i, 0, 0))],
        out_shape=[jax.ShapeDtypeStruct((s, LANES), F32), jax.ShapeDtypeStruct((nb, 8, TILE), F32)],
        scratch_shapes=[pltpu.VMEM((8, LANES), F32)],
        compiler_params=_params("arbitrary"),
    )(pb, bpad)


def _gate_bwd(name, pb, bpad, colsum, fl_block):
    s = pb.shape[0]
    nb = s // TILE

    def body(fl_ref, b_ref, cs_ref, dl_ref, db_ref, carry):
        @pl.when(pl.program_id(0) == 0)
        def _():
            carry[...] = jnp.zeros_like(carry)
            db_ref[...] = jnp.zeros_like(db_ref)

        upper = _tri(TILE, lambda r, c: r >= c)
        rsum = _sum_l3(cs_ref[0], upper) + carry[:, 0:1]
        carry[...] = jnp.broadcast_to(rsum[:, 0:1], carry.shape)
        full = jnp.concatenate([rsum, jnp.zeros((LANES - 8, TILE), F32)], axis=0)
        dlf = -full.T
        u = fl_ref[...] + b_ref[...]
        dlogit = dlf * (1.0 - jax.nn.sigmoid(u))
        dl_ref[...] = dlogit.astype(BF16)
        db_ref[...] += jnp.sum(dlogit, axis=0, keepdims=True)

    rev = lambda i: (nb - 1 - i, 0)
    return pl.pallas_call(
        body, name=name, grid=(nb,),
        in_specs=[pl.BlockSpec((TILE, LANES), lambda i: (nb - 1 - i, fl_block)),
                  pl.BlockSpec((1, LANES), lambda i: (0, 0)),
                  pl.BlockSpec((1, 8, TILE), lambda i: (nb - 1 - i, 0, 0))],
        out_specs=[pl.BlockSpec((TILE, LANES), rev), pl.BlockSpec((1, LANES), lambda i: (0, 0))],
        out_shape=[jax.ShapeDtypeStruct((s, LANES), BF16), jax.ShapeDtypeStruct((1, LANES), F32)],
        scratch_shapes=[pltpu.VMEM((8, LANES), F32)],
        compiler_params=_params("arbitrary"),
    )(pb, bpad, colsum)


def _head_slices(hh):
    return slice(HEAD_DIM * hh, HEAD_DIM * (hh + 1))


def _scaled_q(q_ref, sl):
    return (q_ref[:, sl].astype(F32) * SCALE).astype(BF16)


def _sb_tile(q, kj, carry, strict, u_after, diag):
    z = _dot_nt(q, kj)
    sp = _softplus(z)
    lf = -sp
    if diag:
        lf = jnp.where(strict, lf, 0.0)
    sx = _sum_l2(lf, u_after)
    a = jnp.exp((z - sp) + sx + carry)
    if diag:
        a = jnp.where(strict, a, 0.0)
    return z, sp, a, carry + sx[:, 0:1] + lf[:, 0:1]


def _sb_fwd(name, pa, col0):
    s = pa.shape[0]
    nb = s // TILE
    cb = col0 // LANES

    def body(q_ref, k_ref, v_ref, o_ref):
        i = pl.program_id(1)
        r = lax.broadcasted_iota(jnp.int32, (TILE, TILE), 0)
        c = lax.broadcasted_iota(jnp.int32, (TILE, TILE), 1)
        strict = c < r
        u_after = _tri(TILE, lambda rr, cc: rr > cc)
        qs = [_scaled_q(q_ref, _head_slices(hh)) for hh in range(2)]

        def tile(j, state, diag):
            kblk = k_ref[_rows(k_ref, j), :]
            vblk = v_ref[_rows(v_ref, j), :]
            new = []
            for hh in range(2):
                sl = _head_slices(hh)
                carry, acc = state[hh]
                _, _, a, carry = _sb_tile(qs[hh], kblk[:, sl], carry, strict, u_after, diag)
                new.append((carry, acc + _dot(a.astype(BF16), vblk[:, sl])))
            return tuple(new)

        zero = (jnp.zeros((TILE, 1), F32), jnp.zeros((TILE, HEAD_DIM), F32))
        state = tile(i, (zero, zero), True)
        state = lax.fori_loop(0, i, lambda t, st: tile(i - 1 - t, st, False), state)
        o_ref[...] = jnp.concatenate([state[0][1], state[1][1]], axis=1)

    return pl.pallas_call(
        body, name=name, grid=(4, nb),
        in_specs=[pl.BlockSpec((TILE, LANES), lambda p, i: (i, cb + p)),
                  pl.BlockSpec((s, LANES), lambda p, i: (0, cb + 4 + p)),
                  pl.BlockSpec((s, LANES), lambda p, i: (0, cb + 8 + p))],
        out_specs=pl.BlockSpec((TILE, LANES), lambda p, i: (i, p)),
        out_shape=jax.ShapeDtypeStruct((s, SB_WIDTH), F32),
        compiler_params=_params("parallel", "arbitrary"),
    )(pa, pa, pa)


def _sb_bwd(name, pa, col0, dout, dcol0):
    s = pa.shape[0]
    nb = s // TILE
    cb = col0 // LANES
    db = dcol0 // LANES

    def body(q_ref, k_ref, v_ref, do_ref, dq_ref, dk_ref, dv_ref, dk_acc, dv_acc, gpan, span):
        i = pl.program_id(1)

        @pl.when(i == 0)
        def _():
            dk_acc[...] = jnp.zeros_like(dk_acc)
            dv_acc[...] = jnp.zeros_like(dv_acc)

        r = lax.broadcasted_iota(jnp.int32, (TILE, TILE), 0)
        c = lax.broadcasted_iota(jnp.int32, (TILE, TILE), 1)
        strict = c < r
        u_after = _tri(TILE, lambda rr, cc: rr > cc)
        u_before = _tri(TILE, lambda rr, cc: rr < cc)
        qs = [_scaled_q(q_ref, _head_slices(hh)) for hh in range(2)]
        dos = [do_ref[:, _head_slices(hh)].astype(BF16) for hh in range(2)]

        def pass1(j, carries, diag):
            kblk = k_ref[_rows(k_ref, j), :]
            vblk = v_ref[_rows(v_ref, j), :]
            new = []
            for hh in range(2):
                sl = _head_slices(hh)
                z, sp, a, carry = _sb_tile(qs[hh], kblk[:, sl], carries[hh], strict, u_after, diag)
                gpan[hh, j] = a * _dot_nt(dos[hh], vblk[:, sl])
                span[hh, j] = jnp.exp(z - sp)
                dv_acc[hh, _rows(None, j), :] += _dot_tn(a.astype(BF16), dos[hh])
                new.append(carry)
            return tuple(new)

        zero1 = jnp.zeros((TILE, 1), F32)
        carries = pass1(i, (zero1, zero1), True)
        lax.fori_loop(0, i, lambda t, ca: pass1(i - 1 - t, ca, False), carries)

        def pass2(j, state, diag):
            kblk = k_ref[_rows(k_ref, j), :]
            new = []
            for hh in range(2):
                before, dq = state[hh]
                g = gpan[hh, j]
                sig = span[hh, j]
                pfx = _sum_l2(g, u_before) + before
                dz = g * (1.0 - sig) - sig * pfx
                if diag:
                    dz = jnp.where(strict, dz, 0.0)
                dzb = dz.astype(BF16)
                dk_acc[hh, _rows(None, j), :] += _dot_tn(dzb, qs[hh])
                new.append((pfx[:, TILE - 1:TILE] + g[:, TILE - 1:TILE], dq + _dot(dzb, kblk[:, _head_slices(hh)])))
            return tuple(new)

        zero2 = (zero1, jnp.zeros((TILE, HEAD_DIM), F32))
        state = lax.fori_loop(0, i, lambda j, st: pass2(j, st, False), (zero2, zero2))
        state = pass2(i, state, True)
        dq_ref[...] = jnp.concatenate([state[0][1] * SCALE, state[1][1] * SCALE], axis=1).astype(BF16)

        @pl.when(i == nb - 1)
        def _():
            dk_ref[...] = jnp.concatenate([dk_acc[0], dk_acc[1]], axis=1).astype(BF16)
            dv_ref[...] = jnp.concatenate([dv_acc[0], dv_acc[1]], axis=1).astype(BF16)

    qspec = pl.BlockSpec((TILE, LANES), lambda p, i: (i, p))
    kvspec = pl.BlockSpec((s, LANES), lambda p, i: (0, p))
    out = jax.ShapeDtypeStruct((s, SB_WIDTH), BF16)
    return pl.pallas_call(
        body, name=name, grid=(4, nb),
        in_specs=[pl.BlockSpec((TILE, LANES), lambda p, i: (i, cb + p)),
                  pl.BlockSpec((s, LANES), lambda p, i: (0, cb + 4 + p)),
                  pl.BlockSpec((s, LANES), lambda p, i: (0, cb + 8 + p)),
                  pl.BlockSpec((TILE, LANES), lambda p, i: (i, db + p))],
        out_specs=[qspec, kvspec, kvspec], out_shape=[out, out, out],
        scratch_shapes=[pltpu.VMEM((2, s, HEAD_DIM), F32), pltpu.VMEM((2, s, HEAD_DIM), F32),
                        pltpu.VMEM((2, nb, TILE, TILE), F32), pltpu.VMEM((2, nb, TILE, TILE), F32)],
        compiler_params=_params("arbitrary", "arbitrary"),
    )(pa, pa, pa, dout)


def _fox_scores(q, kj, cq, crj, causal, diag):
    sc = _dot_nt(q, kj) + (cq - crj)
    if diag:
        sc = jnp.where(causal, sc, NEG_INF)
    return sc


def _fox_fwd(name, pa, col0, ccol4, crow4):
    s = pa.shape[0]
    nb = s // TILE
    cb = col0 // LANES

    def body(q_ref, k_ref, v_ref, cc_ref, cr_ref, o_ref, lse_ref):
        i = pl.program_id(1)
        r = lax.broadcasted_iota(jnp.int32, (TILE, TILE), 0)
        c = lax.broadcasted_iota(jnp.int32, (TILE, TILE), 1)
        causal = c <= r
        qs = [_scaled_q(q_ref, _head_slices(hh)) for hh in range(2)]
        cqs = [cc_ref[:, HEAD_DIM * hh:HEAD_DIM * hh + 1] for hh in range(2)]

        def tile(j, state, diag):
            kblk = k_ref[_rows(k_ref, j), :]
            vblk = v_ref[_rows(v_ref, j), :]
            new = []
            for hh in range(2):
                sl = _head_slices(hh)
                m, l, acc = state[hh]
                sc = _fox_scores(qs[hh], kblk[:, sl], cqs[hh], cr_ref[j, hh:hh + 1, :], causal, diag)
                m2 = jnp.maximum(m, jnp.max(sc, axis=1, keepdims=True))
                alpha = jnp.exp(m - m2)
                p = jnp.exp(sc - m2)
                new.append((m2, l * alpha + jnp.sum(p, axis=1, keepdims=True),
                            acc * alpha + _dot(p.astype(BF16), vblk[:, sl])))
            return tuple(new)

        zero = (jnp.full((TILE, 1), NEG_INF, F32), jnp.zeros((TILE, 1), F32), jnp.zeros((TILE, HEAD_DIM), F32))
        state = tile(i, (zero, zero), True)
        state = lax.fori_loop(0, i, lambda t, st: tile(i - 1 - t, st, False), state)
        o_ref[...] = jnp.concatenate([state[hh][2] / state[hh][1] for hh in range(2)], axis=1)
        lse_ref[...] = jnp.concatenate(
            [jnp.broadcast_to(state[hh][0] + jnp.log(state[hh][1]), (TILE, HEAD_DIM)) for hh in range(2)], axis=1)

    return pl.pallas_call(
        body, name=name, grid=(4, nb),
        in_specs=[pl.BlockSpec((TILE, LANES), lambda p, i: (i, cb + p)),
                  pl.BlockSpec((s, LANES), lambda p, i: (0, cb + 4 + p)),
                  pl.BlockSpec((s, LANES), lambda p, i: (0, cb + 8 + p)),
                  pl.BlockSpec((None, TILE, LANES), lambda p, i: (p, i, 0)),
                  pl.BlockSpec((None, nb, 8, TILE), lambda p, i: (p, 0, 0, 0))],
        out_specs=[pl.BlockSpec((TILE, LANES), lambda p, i: (i, p)),
                   pl.BlockSpec((None, TILE, LANES), lambda p, i: (p, i, 0))],
        out_shape=[jax.ShapeDtypeStruct((s, FOX_WIDTH), F32), jax.ShapeDtypeStruct((4, s, LANES), F32)],
        compiler_params=_params("parallel", "arbitrary"),
    )(pa, pa, pa, ccol4, crow4)


def _fox_bwd(name, pa, col0, ccol4, crow4, out, lse, dout, dcol0):
    s = pa.shape[0]
    nb = s // TILE
    cb = col0 // LANES
    db = dcol0 // LANES

    def body(q_ref, k_ref, v_ref, cc_ref, cr_ref, o_ref, lse_ref, do_ref,
             dq_ref, dk_ref, dv_ref, cs_ref, dk_acc, dv_acc):
        i = pl.program_id(1)

        @pl.when(i == 0)
        def _():
            dk_acc[...] = jnp.zeros_like(dk_acc)
            dv_acc[...] = jnp.zeros_like(dv_acc)
            cs_ref[...] = jnp.zeros_like(cs_ref)

        r = lax.broadcasted_iota(jnp.int32, (TILE, TILE), 0)
        c = lax.broadcasted_iota(jnp.int32, (TILE, TILE), 1)
        causal = c <= r
        qs = [_scaled_q(q_ref, _head_slices(hh)) for hh in range(2)]
        cqs = [cc_ref[:, HEAD_DIM * hh:HEAD_DIM * hh + 1] for hh in range(2)]
        lses = [lse_ref[:, HEAD_DIM * hh:HEAD_DIM * hh + 1] for hh in range(2)]
        dofs = [do_ref[:, _head_slices(hh)] for hh in range(2)]
        dos = [d_.astype(BF16) for d_ in dofs]
        deltas = [jnp.sum(dofs[hh] * o_ref[:, _head_slices(hh)], axis=1, keepdims=True) for hh in range(2)]

        def tile(j, state, diag):
            kblk = k_ref[_rows(k_ref, j), :]
            vblk = v_ref[_rows(v_ref, j), :]
            new = []
            for hh in range(2):
                sl = _head_slices(hh)
                dq, rs = state[hh]
                sc = _fox_scores(qs[hh], kblk[:, sl], cqs[hh], cr_ref[j, hh:hh + 1, :], causal, diag)
                p = jnp.exp(sc - lses[hh])
                ds = p * (_dot_nt(dos[hh], vblk[:, sl]) - deltas[hh])
                dsb = ds.astype(BF16)
                dv_acc[hh, _rows(None, j), :] += _dot_tn(p.astype(BF16), dos[hh])
                dk_acc[hh, _rows(None, j), :] += _dot_tn(dsb, qs[hh])
                cs_ref[j, hh:hh + 1, :] += jnp.sum(ds, axis=0, keepdims=True)
                new.append((dq + _dot(dsb, kblk[:, sl]), rs + jnp.sum(ds, axis=1, keepdims=True)))
            return tuple(new)

        zero = (jnp.zeros((TILE, HEAD_DIM), F32), jnp.zeros((TILE, 1), F32))
        state = tile(i, (zero, zero), True)
        state = lax.fori_loop(0, i, lambda t, st: tile(i - 1 - t, st, False), state)
        for hh in range(2):
            cs_ref[i, hh:hh + 1, :] -= jnp.broadcast_to(state[hh][1], (TILE, LANES)).T[0:1, :]
        dq_ref[...] = jnp.concatenate([state[0][0] * SCALE, state[1][0] * SCALE], axis=1).astype(BF16)

        @pl.when(i == nb - 1)
        def _():
            dk_ref[...] = jnp.concatenate([dk_acc[0], dk_acc[1]], axis=1).astype(BF16)
            dv_ref[...] = jnp.concatenate([dv_acc[0], dv_acc[1]], axis=1).astype(BF16)

    qspec = pl.BlockSpec((TILE, LANES), lambda p, i: (i, p))
    kvspec = pl.BlockSpec((s, LANES), lambda p, i: (0, p))
    o3 = jax.ShapeDtypeStruct((s, FOX_WIDTH), BF16)
    return pl.pallas_call(
        body, name=name, grid=(4, nb),
        in_specs=[pl.BlockSpec((TILE, LANES), lambda p, i: (i, cb + p)),
                  pl.BlockSpec((s, LANES), lambda p, i: (0, cb + 4 + p)),
                  pl.BlockSpec((s, LANES), lambda p, i: (0, cb + 8 + p)),
                  pl.BlockSpec((None, TILE, LANES), lambda p, i: (p, i, 0)),
                  pl.BlockSpec((None, nb, 8, TILE), lambda p, i: (p, 0, 0, 0)),
                  qspec,
                  pl.BlockSpec((None, TILE, LANES), lambda p, i: (p, i, 0)),
                  pl.BlockSpec((TILE, LANES), lambda p, i: (i, db + p))],
        out_specs=[qspec, kvspec, kvspec, pl.BlockSpec((None, nb, 8, TILE), lambda p, i: (p, 0, 0, 0))],
        out_shape=[o3, o3, o3, jax.ShapeDtypeStruct((4, nb, 8, TILE), F32)],
        scratch_shapes=[pltpu.VMEM((2, s, HEAD_DIM), F32), pltpu.VMEM((2, s, HEAD_DIM), F32)],
        compiler_params=_params("arbitrary", "arbitrary"),
    )(pa, pa, pa, ccol4, crow4, out, lse, dout)


def _mem_fwd(name, pa, mkv):
    s = pa.shape[0]
    ml = mkv.shape[0]
    nb = s // TILE
    cb = QKV_WIDTH // LANES

    def body(q_ref, k_ref, v_ref, o_ref, lse_ref):
        outs, lses = [], []
        for hh in range(2):
            sl = _head_slices(hh)
            sc = _dot_nt(_scaled_q(q_ref, sl), k_ref[:, sl])
            m = jnp.max(sc, axis=1, keepdims=True)
            p = jnp.exp(sc - m)
            l = jnp.sum(p, axis=1, keepdims=True)
            outs.append(_dot(p.astype(BF16), v_ref[:, sl]) / l)
            lses.append(jnp.broadcast_to(m + jnp.log(l), (TILE, HEAD_DIM)))
        o_ref[...] = jnp.concatenate(outs, axis=1)
        lse_ref[...] = jnp.concatenate(lses, axis=1)

    return pl.pallas_call(
        body, name=name, grid=(2, nb),
        in_specs=[pl.BlockSpec((TILE, LANES), lambda p, i: (i, cb + p)),
                  pl.BlockSpec((ml, LANES), lambda p, i: (0, p)),
                  pl.BlockSpec((ml, LANES), lambda p, i: (0, 2 + p))],
        out_specs=[pl.BlockSpec((TILE, LANES), lambda p, i: (i, p)),
                   pl.BlockSpec((None, TILE, LANES), lambda p, i: (p, i, 0))],
        out_shape=[jax.ShapeDtypeStruct((s, MEM_WIDTH), F32), jax.ShapeDtypeStruct((2, s, LANES), F32)],
        compiler_params=_params("parallel", "parallel"),
    )(pa, mkv, mkv)


def _mem_bwd(name, pa, mkv, out, lse, dout, dcol0):
    s = pa.shape[0]
    ml = mkv.shape[0]
    nb = s // TILE
    cb = QKV_WIDTH // LANES
    db = dcol0 // LANES

    def body(q_ref, k_ref, v_ref, o_ref, lse_ref, do_ref, dq_ref, dk_ref, dv_ref, dk_acc, dv_acc):
        i = pl.program_id(1)

        @pl.when(i == 0)
        def _():
            dk_acc[...] = jnp.zeros_like(dk_acc)
            dv_acc[...] = jnp.zeros_like(dv_acc)

        dqs = []
        for hh in range(2):
            sl = _head_slices(hh)
            q = _scaled_q(q_ref, sl)
            kh = k_ref[:, sl]
            dof = do_ref[:, sl]
            do = dof.astype(BF16)
            delta = jnp.sum(dof * o_ref[:, sl], axis=1, keepdims=True)
            p = jnp.exp(_dot_nt(q, kh) - lse_ref[:, HEAD_DIM * hh:HEAD_DIM * hh + 1])
            ds = (p * (_dot_nt(do, v_ref[:, sl]) - delta)).astype(BF16)
            dv_acc[hh] += _dot_tn(p.astype(BF16), do)
            dk_acc[hh] += _dot_tn(ds, q)
            dqs.append(_dot(ds, kh) * SCALE)
        dq_ref[...] = jnp.concatenate(dqs, axis=1).astype(BF16)

        @pl.when(i == nb - 1)
        def _():
            dk_ref[...] = jnp.concatenate([dk_acc[0], dk_acc[1]], axis=1).astype(BF16)
            dv_ref[...] = jnp.concatenate([dv_acc[0], dv_acc[1]], axis=1).astype(BF16)

    qspec = pl.BlockSpec((TILE, LANES), lambda p, i: (i, p))
    kvspec = pl.BlockSpec((ml, LANES), lambda p, i: (0, p))
    okv = jax.ShapeDtypeStruct((ml, MEM_WIDTH), BF16)
    return pl.pallas_call(
        body, name=name, grid=(2, nb),
        in_specs=[pl.BlockSpec((TILE, LANES), lambda p, i: (i, cb + p)),
                  pl.BlockSpec((ml, LANES), lambda p, i: (0, p)),
                  pl.BlockSpec((ml, LANES), lambda p, i: (0, 2 + p)),
                  qspec,
                  pl.BlockSpec((None, TILE, LANES), lambda p, i: (p, i, 0)),
                  pl.BlockSpec((TILE, LANES), lambda p, i: (i, db + p))],
        out_specs=[qspec, kvspec, kvspec],
        out_shape=[jax.ShapeDtypeStruct((s, MEM_WIDTH), BF16), okv, okv],
        scratch_shapes=[pltpu.VMEM((2, ml, HEAD_DIM), F32), pltpu.VMEM((2, ml, HEAD_DIM), F32)],
        compiler_params=_params("arbitrary", "arbitrary"),
    )(pa, mkv, mkv, out, lse, dout)


def _head_maps():
    col = jnp.arange(MIX_WIDTH)[:, None] // HEAD_DIM
    g = (col == jnp.arange(LANES)[None, :]).astype(BF16)
    return g, g.T


def _normed_heads(osb_ref, ofx_ref, om_ref, g_ref, gt_ref):
    y = jnp.concatenate([osb_ref[...], ofx_ref[...], om_ref[...]], axis=1)
    msq = _sum_l3(y * y, g_ref[...]) * (1.0 / HEAD_DIM)
    rf = _sum_l3(lax.rsqrt(msq + EPS), gt_ref[...])
    return y * rf, rf


def _out_fwd(name, o_sb, o_fx, o_m, pb, ow, x, w_out, layer, ts):
    s, d = x.shape
    g, gt = _head_maps()

    def body(osb_ref, ofx_ref, om_ref, gate_ref, ow_ref, x_ref, w_ref, g_ref, gt_ref, xo_ref, y2_ref):
        yh, _ = _normed_heads(osb_ref, ofx_ref, om_ref, g_ref, gt_ref)
        gate = gate_ref[...]
        y2 = (yh * ow_ref[...] * (gate * jax.nn.sigmoid(gate))).astype(BF16)
        y2_ref[...] = y2
        xo_ref[...] = x_ref[...] + _dot(y2, w_ref[...])

    return pl.pallas_call(
        body, name=name, grid=(s // ts,),
        in_specs=[_row_spec(ts, SB_WIDTH), _row_spec(ts, FOX_WIDTH), _row_spec(ts, MEM_WIDTH),
                  _row_spec(ts, MIX_WIDTH), _const_spec((1, MIX_WIDTH)), _row_spec(ts, d),
                  pl.BlockSpec((None, MIX_WIDTH, d), lambda i: (layer, 0, 0)),
                  _const_spec((MIX_WIDTH, LANES)), _const_spec((LANES, MIX_WIDTH))],
        out_specs=[_row_spec(ts, d), _row_spec(ts, MIX_WIDTH)],
        out_shape=[jax.ShapeDtypeStruct((s, d), F32), jax.ShapeDtypeStruct((s, MIX_WIDTH), BF16)],
        compiler_params=_params("parallel"),
    )(o_sb, o_fx, o_m, pb, ow, x, w_out, g, gt)


def _row_spec(ts, w):
    return pl.BlockSpec((ts, w), lambda i: (i, 0))


def _const_spec(shape):
    return pl.BlockSpec(shape, lambda i: (0,) * len(shape))


def _out_bwd(name, dxb, o_sb, o_fx, o_m, pb, ow, w_out, layer, ts):
    s, d = dxb.shape
    g, gt = _head_maps()

    def body(dx_ref, osb_ref, ofx_ref, om_ref, gate_ref, ow_ref, w_ref, g_ref, gt_ref, dy_ref, dgate_ref, dow_ref):
        @pl.when(pl.program_id(0) == 0)
        def _():
            dow_ref[...] = jnp.zeros_like(dow_ref)

        dy2 = _dot_nt(dx_ref[...], w_ref[...])
        yh, rf = _normed_heads(osb_ref, ofx_ref, om_ref, g_ref, gt_ref)
        gate = gate_ref[...]
        sig = jax.nn.sigmoid(gate)
        ow_v = ow_ref[...]
        dgate_ref[...] = (dy2 * (yh * ow_v) * (sig * (1.0 + gate * (1.0 - sig)))).astype(BF16)
        dn = dy2 * (gate * sig)
        dow_ref[...] += jnp.sum(dn * yh, axis=0, keepdims=True)
        dyh = dn * ow_v
        t = _sum_l3(dyh * yh, g_ref[...]) * (1.0 / HEAD_DIM)
        dy_ref[...] = rf * (dyh - yh * _sum_l3(t, gt_ref[...]))

    return pl.pallas_call(
        body, name=name, grid=(s // ts,),
        in_specs=[_row_spec(ts, d), _row_spec(ts, SB_WIDTH), _row_spec(ts, FOX_WIDTH), _row_spec(ts, MEM_WIDTH),
                  _row_spec(ts, MIX_WIDTH), _const_spec((1, MIX_WIDTH)),
                  pl.BlockSpec((None, MIX_WIDTH, d), lambda i: (layer, 0, 0)),
                  _const_spec((MIX_WIDTH, LANES)), _const_spec((LANES, MIX_WIDTH))],
        out_specs=[_row_spec(ts, MIX_WIDTH), _row_spec(ts, MIX_WIDTH), _const_spec((1, MIX_WIDTH))],
        out_shape=[jax.ShapeDtypeStruct((s, MIX_WIDTH), F32), jax.ShapeDtypeStruct((s, MIX_WIDTH), BF16),
                   jax.ShapeDtypeStruct((1, MIX_WIDTH), F32)],
        compiler_params=_params("arbitrary"),
    )(dxb, o_sb, o_fx, o_m, pb, ow, w_out, g, gt)


def _adamw(name, w, g, m, v, tr):
    rows, cols = w.shape

    def body(w_ref, g_ref, m_ref, v_ref, d_ref, m2_ref, v2_ref):
        gv = g_ref[...]
        m2 = ADAM_B1 * m_ref[...] + (1.0 - ADAM_B1) * gv
        v2 = ADAM_B2 * v_ref[...] + (1.0 - ADAM_B2) * (gv * gv)
        m_hat = m2 / (1.0 - ADAM_B1 ** ADAM_STEP)
        v_hat = v2 / (1.0 - ADAM_B2 ** ADAM_STEP)
        d_ref[...] = -ADAM_LR * (m_hat / (jnp.sqrt(v_hat) + ADAM_EPS) + ADAM_WD * w_ref[...])
        m2_ref[...] = m2
        v2_ref[...] = v2

    spec = _row_spec(tr, cols)
    shp = jax.ShapeDtypeStruct((rows, cols), F32)
    return pl.pallas_call(
        body, name=name, grid=(rows // tr,), in_specs=[spec] * 4, out_specs=[spec] * 3, out_shape=[shp] * 3,
        compiler_params=_params("parallel"),
    )(w, g, m, v)


HBM_SPEC = pl.BlockSpec(memory_space=pltpu.HBM)


def _place():
    x, y, c = lax.axis_index("x"), lax.axis_index("y"), lax.axis_index("c")
    chips = [(1 - x, y), (x, 1 - y), (1 - x, 1 - y)]
    return x, y, c, chips


def _remote(src, dst, send_sems, recv_sems, k, to):
    return pltpu.make_async_remote_copy(src_ref=src, dst_ref=dst, send_sem=send_sems.at[k], recv_sem=recv_sems.at[k],
                                        device_id=to, device_id_type=MESH)


def _half_rows(n_rows, cc):
    rh = n_rows // 2
    return pl.ds(pl.multiple_of(cc * rh, 16), rh)


def _dma_sems(n):
    return [pltpu.SemaphoreType.DMA((n,)), pltpu.SemaphoreType.DMA((n,))]


def _gather_weights(name, shards):
    n = len(shards)

    def body(*refs):
        in_refs, out_refs, (send_sems, recv_sems) = refs[:n], refs[n:2 * n], refs[2 * n:]
        x, y, c, chips = _place()
        mine = 2 * x + y
        sibling = (x, y, 1 - c)
        first, passed = [], []
        for a, (in_ref, out_ref) in enumerate(zip(in_refs, out_refs)):
            mh = _half_rows(in_ref.shape[0], c)
            for j, (cx, cy) in enumerate(chips):
                first.append(_remote(in_ref.at[mh], out_ref.at[mine, mh], send_sems, recv_sems, 6 * a + j, (cx, cy, c)))
        for cp in first:
            cp.start()
        for j, (cx, cy) in enumerate(chips):
            for a, out_ref in enumerate(out_refs):
                landed = out_ref.at[2 * cx + cy, _half_rows(out_ref.shape[1], c)]
                _remote(landed, landed, send_sems, recv_sems, 6 * a + j, sibling).wait_recv()
                cp = _remote(landed, landed, send_sems, recv_sems, 6 * a + 3 + j, sibling)
                cp.start()
                passed.append(cp)
        for j, (cx, cy) in enumerate(chips):
            for a, out_ref in enumerate(out_refs):
                other = out_ref.at[2 * cx + cy, _half_rows(out_ref.shape[1], 1 - c)]
                _remote(other, other, send_sems, recv_sems, 6 * a + 3 + j, sibling).wait_recv()
        for cp in first + passed:
            cp.wait_send()

    return pl.pallas_call(
        body, name=name, in_specs=[HBM_SPEC] * n, out_specs=[HBM_SPEC] * n,
        out_shape=[jax.ShapeDtypeStruct((N_CHIPS,) + s_.shape, s_.dtype) for s_ in shards],
        scratch_shapes=_dma_sems(6 * n),
    )(*shards)


def _swap_halves(name, g4s):
    n = len(g4s)

    def body(*refs):
        in_refs, out_refs, (send_sems, recv_sems) = refs[:n], refs[n:2 * n], refs[2 * n:]
        x, y, c, _ = _place()
        cps = [_remote(in_ref.at[:, _half_rows(in_ref.shape[1], 1 - c), :], out_ref, send_sems, recv_sems, a, (x, y, 1 - c))
               for a, (in_ref, out_ref) in enumerate(zip(in_refs, out_refs))]
        for cp in cps:
            cp.start()
        for cp in cps:
            cp.wait()

    return pl.pallas_call(
        body, name=name, in_specs=[HBM_SPEC] * n, out_specs=[HBM_SPEC] * n,
        out_shape=[jax.ShapeDtypeStruct((g.shape[0], g.shape[1] // 2, g.shape[2]), g.dtype) for g in g4s],
        scratch_shapes=_dma_sems(n),
    )(*g4s)


def _add_half(name, g4, r1, cvec, tr):
    n, r, w = g4.shape
    rh = r // 2
    nblk = rh // tr

    def body(c_ref, a_ref, b_ref, o_ref):
        o_ref[...] = (a_ref[...] + b_ref[...]).astype(BF16)

    return pl.pallas_call(
        body, name=name,
        grid_spec=pltpu.PrefetchScalarGridSpec(
            num_scalar_prefetch=1, grid=(n, nblk),
            in_specs=[pl.BlockSpec((None, tr, w), lambda k, i, c_ref: (k, c_ref[0] * nblk + i, 0)),
                      pl.BlockSpec((None, tr, w), lambda k, i, c_ref: (k, i, 0))],
            out_specs=pl.BlockSpec((None, tr, w), lambda k, i, c_ref: (k, i, 0))),
        out_shape=jax.ShapeDtypeStruct((n, rh, w), BF16),
        compiler_params=_params("parallel", "parallel"),
    )(cvec, g4, r1)


def _scatter_chips(name, h4s):
    n = len(h4s)

    def body(*refs):
        in_refs, out_refs, (send_sems, recv_sems) = refs[:n], refs[n:2 * n], refs[2 * n:]
        x, y, c, chips = _place()
        sends = [_remote(in_ref.at[2 * cx + cy], out_ref.at[j], send_sems, recv_sems, 3 * a + j, (cx, cy, c))
                 for a, (in_ref, out_ref) in enumerate(zip(in_refs, out_refs)) for j, (cx, cy) in enumerate(chips)]
        for cp in sends:
            cp.start()
        for a, out_ref in enumerate(out_refs):
            for j, (cx, cy) in enumerate(chips):
                got = out_ref.at[j]
                _remote(got, got, send_sems, recv_sems, 3 * a + j, (cx, cy, c)).wait_recv()
        for cp in sends:
            cp.wait_send()

    return pl.pallas_call(
        body, name=name, in_specs=[HBM_SPEC] * n, out_specs=[HBM_SPEC] * n,
        out_shape=[jax.ShapeDtypeStruct((3,) + h.shape[1:], h.dtype) for h in h4s],
        scratch_shapes=_dma_sems(3 * n),
    )(*h4s)


def _sum_chips(name, h4, r3, mvec, tr):
    _, rh, w = h4.shape

    def body(m_ref, a_ref, b_ref, c_ref, d_ref, o_ref):
        o_ref[...] = ((a_ref[...].astype(F32) + b_ref[...].astype(F32)) + c_ref[...].astype(F32)) + d_ref[...].astype(F32)

    specs = [pl.BlockSpec((None, tr, w), lambda i, m_ref: (m_ref[0], i, 0))]
    specs += [pl.BlockSpec((None, tr, w), functools.partial(lambda k, i, m_ref: (k, i, 0), k)) for k in range(3)]
    return pl.pallas_call(
        body, name=name,
        grid_spec=pltpu.PrefetchScalarGridSpec(
            num_scalar_prefetch=1, grid=(rh // tr,), in_specs=specs,
            out_specs=pl.BlockSpec((tr, w), lambda i, m_ref: (i, 0))),
        out_shape=jax.ShapeDtypeStruct((rh, w), F32),
        compiler_params=_params("parallel"),
    )(mvec, h4, r3, r3, r3)


def _swap_reduced(name, ghs):
    n = len(ghs)

    def body(*refs):
        in_refs, out_refs, (send_sems, recv_sems) = refs[:n], refs[n:2 * n], refs[2 * n:]
        x, y, c, _ = _place()
        cps = [_remote(in_ref, out_ref, send_sems, recv_sems, a, (x, y, 1 - c))
               for a, (in_ref, out_ref) in enumerate(zip(in_refs, out_refs))]
        for cp in cps:
            cp.start()
        for cp in cps:
            cp.wait()

    return pl.pallas_call(
        body, name=name, in_specs=[HBM_SPEC] * n, out_specs=[HBM_SPEC] * n,
        out_shape=[jax.ShapeDtypeStruct(g.shape, g.dtype) for g in ghs],
        scratch_shapes=_dma_sems(n),
    )(*ghs)


def _allreduce_small(name, vec):
    rows, w = vec.shape

    def body(v_ref, o_ref, buf, send_sems, recv_sems):
        x, y, c, _ = _place()
        me = 4 * x + 2 * y + c
        buf[me] = v_ref[...]
        flips = [(fx, fy, fc) for fx in (0, 1) for fy in (0, 1) for fc in (0, 1)][1:]
        peers = [(x + fx - 2 * x * fx, y + fy - 2 * y * fy, c + fc - 2 * c * fc) for fx, fy, fc in flips]
        sends = [_remote(v_ref, buf.at[me], send_sems, recv_sems, k, peer) for k, peer in enumerate(peers)]
        for cp in sends:
            cp.start()
        for k, (px, py, pc) in enumerate(peers):
            got = buf.at[4 * px + 2 * py + pc]
            _remote(got, got, send_sems, recv_sems, k, (px, py, pc)).wait_recv()
        for cp in sends:
            cp.wait_send()
        acc = buf[0]
        for dev in range(1, N_DEV):
            acc = acc + buf[dev]
        o_ref[...] = acc

    vm = pl.BlockSpec(memory_space=pltpu.VMEM)
    return pl.pallas_call(
        body, name=name, in_specs=[vm], out_specs=vm, out_shape=jax.ShapeDtypeStruct((rows, w), F32),
        scratch_shapes=[pltpu.VMEM((N_DEV, rows, w), F32), pltpu.SemaphoreType.DMA((7,)), pltpu.SemaphoreType.DMA((7,))],
    )(vec)


GATE_COL = 3 * SB_WIDTH + 3 * FOX_WIDTH + FOX_HEADS + MEM_WIDTH
FL_COL = QKV_WIDTH


def _to_groups(w):
    pad = jnp.zeros(w.shape[:-1] + (LANES - FOX_HEADS,), w.dtype)
    wa = jnp.concatenate([w[..., :QKV_WIDTH], w[..., FL_COL + FOX_HEADS:GATE_COL]], axis=-1)
    wb = jnp.concatenate([w[..., GATE_COL:], w[..., FL_COL:FL_COL + FOX_HEADS], pad], axis=-1)
    return wa, wb


def _from_groups(ga, gb):
    return jnp.concatenate([ga[..., :QKV_WIDTH], gb[..., MIX_WIDTH:MIX_WIDTH + FOX_HEADS], ga[..., QKV_WIDTH:],
                            gb[..., :MIX_WIDTH]], axis=-1)


def _tile_of(n, cap, unit):
    if n <= cap:
        return n
    best = None
    for t in range(unit, cap + 1, unit):
        if n % t == 0:
            best = t
    assert best is not None, (n, cap, unit)
    return best


def _flat2(a):
    return a.reshape(-1, a.shape[-1])


def _pack_small(parts):
    rows = []
    for p in parts:
        f = p.reshape(-1).astype(F32)
        f = jnp.pad(f, (0, (-f.shape[0]) % LANES))
        rows.append(f.reshape(-1, LANES))
    out = jnp.concatenate(rows, axis=0)
    return jnp.pad(out, ((0, (-out.shape[0]) % 8), (0, 0)))


def _unpack_small(packed, shapes):
    outs, r = [], 0
    for shp in shapes:
        n = 1
        for s_ in shp:
            n *= s_
        nr = -(-n // LANES)
        outs.append(packed[r:r + nr].reshape(-1)[:n].reshape(shp))
        r += nr
    return outs


def kernel(x, mem, norm_w, w_in, b_forget, mem_norm_w, w_mem_kv, out_norm_w, w_out, final_norm_w, loss_target, m_norm_w, m_w_in, m_b_forget, m_mem_norm_w, m_w_mem_kv, m_out_norm_w, m_w_out, m_final_norm_w, v_norm_w, v_w_in, v_b_forget, v_mem_norm_w, v_w_mem_kv, v_out_norm_w, v_w_out, v_final_norm_w):
    xs = x[0]
    mems = mem[0]
    target = loss_target[0]
    s, d = xs.shape
    depth = norm_w.shape[0]
    nb = s // TILE
    ts = _tile_of(s, 256, 8)
    big = (w_in, w_mem_kv, w_out)
    core = lax.axis_index("c")
    chip = 2 * lax.axis_index("x") + lax.axis_index("y")

    own = [_flat2(a.astype(BF16)) for a in big]
    gathered = _gather_weights("gather_weights", own)
    shards = [lax.dynamic_update_slice(g, o[None], (chip, 0, 0)).reshape((N_CHIPS,) + a.shape)
              for g, o, a in zip(gathered, own, big)]
    w_in_full = jnp.concatenate([shards[0][j] for j in range(N_CHIPS)], axis=2)
    wkv = jnp.concatenate([shards[1][j] for j in range(N_CHIPS)], axis=1)
    wout = jnp.concatenate([shards[2][j] for j in range(N_CHIPS)], axis=1)
    wa, wb = _to_groups(w_in_full)

    tm = _tile_of(s, 256, 8)
    fl_block = MIX_WIDTH // LANES

    saved = []
    cur = xs
    for l in range(depth):
        h = _rms_fwd(f"rms_fwd{l}", cur, norm_w[l][None], ts)
        pa = _mm(f"inproj_a{l}", h, wa, "nn", tm, _tile_of(PA, 1664, LANES), BF16, b_lead=(l,))
        pb = _mm(f"inproj_b{l}", h, wb, "nn", tm, PB, F32, b_lead=(l,))
        bpad = jnp.pad(b_forget[l], (0, LANES - FOX_HEADS))[None]
        ccol, crow = _gate_fwd(f"gate_fwd{l}", pb, bpad, fl_block)
        ccol4 = jnp.repeat(ccol[:, :FOX_HEADS].reshape(s, 4, 2).transpose(1, 0, 2), HEAD_DIM, axis=2)
        crow4 = jnp.pad(crow.reshape(nb, 4, 2, TILE).transpose(1, 0, 2, 3), ((0, 0), (0, 0), (0, 6), (0, 0)))
        o_sb = _sb_fwd(f"sb_fwd{l}", pa, 0)
        o_fx, lse_fx = _fox_fwd(f"fox_fwd{l}", pa, 3 * SB_WIDTH, ccol4, crow4)
        mn = _rms_fwd(f"mem_rms{l}", mems, mem_norm_w[l][None], mems.shape[0])
        mkv = _mm(f"mem_kv{l}", mn, wkv, "nn", mems.shape[0], 2 * MEM_WIDTH, BF16, b_lead=(l,))
        o_m, lse_m = _mem_fwd(f"mem_fwd{l}", pa, mkv)
        nxt, y2 = _out_fwd(f"out_fwd{l}", o_sb, o_fx, o_m, pb, out_norm_w[l][None], cur, wout, l, ts)
        saved.append((cur, h, pa, pb, bpad, ccol4, crow4, o_sb, o_fx, lse_fx, mn, mkv, o_m, lse_m, y2))
        cur = nxt

    loss_v, dx, dxb, g_final = _final_loss("final_loss", cur, final_norm_w[None], target, ts)

    g_norm, g_b, g_memnorm, g_outnorm = [None] * depth, [None] * depth, [None] * depth, [None] * depth
    g_wa, g_wb, g_wkv, g_wout = [None] * depth, [None] * depth, [None] * depth, [None] * depth
    for l in reversed(range(depth)):
        xin, h, pa, pb, bpad, ccol4, crow4, o_sb, o_fx, lse_fx, mn, mkv, o_m, lse_m, y2 = saved[l]
        dy, dgate, g_outnorm[l] = _out_bwd(f"out_bwd{l}", dxb, o_sb, o_fx, o_m, pb, out_norm_w[l][None], wout, l, ts)
        g_wout[l] = _mm(f"dw_out{l}", y2, dxb, "tn", _tile_of(MIX_WIDTH, 640, LANES), d, F32)
        dq_sb, dk_sb, dv_sb = _sb_bwd(f"sb_bwd{l}", pa, 0, dy, 0)
        dq_fx, dk_fx, dv_fx, cs4 = _fox_bwd(f"fox_bwd{l}", pa, 3 * SB_WIDTH, ccol4, crow4, o_fx, lse_fx, dy, SB_WIDTH)
        colsum = cs4[:, :, :2, :].transpose(1, 0, 2, 3).reshape(nb, 8, TILE)
        dlogit, g_b[l] = _gate_bwd(f"gate_bwd{l}", pb, bpad, colsum, fl_block)
        dq_m, dk_m, dv_m = _mem_bwd(f"mem_bwd{l}", pa, mkv, o_m, lse_m, dy, SB_WIDTH + FOX_WIDTH)
        dmkv = jnp.concatenate([dk_m, dv_m], axis=1)
        g_wkv[l] = _mm(f"dw_kv{l}", mn, dmkv, "tn", d, 2 * MEM_WIDTH, F32)
        dmn = _mm(f"dmem{l}", dmkv, wkv, "nt", mems.shape[0], d, F32, b_lead=(l,))
        g_memnorm[l] = _rms_wgrad(f"mem_norm_grad{l}", mems, dmn)
        dpa = jnp.concatenate([dq_sb, dk_sb, dv_sb, dq_fx, dk_fx, dv_fx, dq_m], axis=1)
        dpb = jnp.concatenate([dgate, dlogit], axis=1)
        tw = _tile_of(d, 512, LANES)
        g_wa[l] = _mm(f"dw_in_a{l}", h, dpa, "tn", tw, _tile_of(PA, 1664, LANES), F32)
        g_wb[l] = _mm(f"dw_in_b{l}", h, dpb, "tn", tw, PB, F32)
        dh = _mm(f"dh_a{l}", dpa, wa, "nt", tm, d, F32, b_lead=(l,))
        dh = _mm(f"dh_b{l}", dpb, wb, "nt", tm, d, F32, res=dh, b_lead=(l,))
        dx, dxb, g_norm[l] = _rms_bwd(f"rms_bwd{l}", xin, norm_w[l][None], dh, dx, ts)

    gw_in = _from_groups(jnp.stack(g_wa), jnp.stack(g_wb))
    gw_kv = jnp.stack(g_wkv)
    gw_out = jnp.stack(g_wout)
    cw, kr, orows = w_in.shape[2], w_mem_kv.shape[1], w_out.shape[1]
    g4s = [jnp.stack([_flat2(gw_in[:, :, j * cw:(j + 1) * cw]) for j in range(N_CHIPS)]),
           jnp.stack([_flat2(gw_kv[:, j * kr:(j + 1) * kr]) for j in range(N_CHIPS)]),
           jnp.stack([_flat2(gw_out[:, j * orows:(j + 1) * orows]) for j in range(N_CHIPS)])]
    tiles = [_tile_of(g.shape[1] // 2, 256, 16) for g in g4s]
    cvec = core.astype(jnp.int32).reshape(1)
    mvec = chip.astype(jnp.int32).reshape(1)
    from_sibling = _swap_halves("grad_swap_halves", g4s)
    chip_sums = [_add_half(f"grad_add_half{k}", g, r, cvec, t) for k, (g, r, t) in enumerate(zip(g4s, from_sibling, tiles))]
    from_chips = _scatter_chips("grad_scatter_chips", chip_sums)
    g_halves = [_sum_chips(f"grad_sum_chips{k}", h, r, mvec, t) for k, (h, r, t) in enumerate(zip(chip_sums, from_chips, tiles))]
    other_halves = _swap_reduced("grad_swap_reduced", g_halves)
    big_grads = [jnp.concatenate([jnp.where(core == 0, gh, th), jnp.where(core == 0, th, gh)], axis=0).reshape(a.shape)
                 for gh, th, a in zip(g_halves, other_halves, big)]

    small_w = [norm_w, b_forget, mem_norm_w, out_norm_w, final_norm_w]
    small_m = [m_norm_w, m_b_forget, m_mem_norm_w, m_out_norm_w, m_final_norm_w]
    small_v = [v_norm_w, v_b_forget, v_mem_norm_w, v_out_norm_w, v_final_norm_w]
    small_shapes = [a.shape for a in small_w]
    local_small = [jnp.concatenate(g_norm, axis=0), jnp.stack([g[0, :FOX_HEADS] for g in g_b]),
                   jnp.concatenate(g_memnorm, axis=0), jnp.concatenate(g_outnorm, axis=0), g_final[0]]
    reduced = _allreduce_small("small_allreduce", _pack_small(local_small + [loss_v[0, :1]]))
    small_grads = _unpack_small(reduced, small_shapes)
    n_small_rows = sum(-(-a.size // LANES) for a in small_w)
    loss = reduced[n_small_rows, 0]

    d_s, m_s, v_s = _adamw("adamw_small", _pack_small(small_w), _pack_small(small_grads), _pack_small(small_m),
                           _pack_small(small_v), _pack_small(small_w).shape[0])
    small_delta, small_m2, small_v2 = (_unpack_small(a, small_shapes) for a in (d_s, m_s, v_s))
    big_delta, big_m2, big_v2 = [], [], []
    for nm, w_, g_, m_, v_ in zip(("w_in", "w_mem_kv", "w_out"), big, big_grads, (m_w_in, m_w_mem_kv, m_w_out),
                                  (v_w_in, v_w_mem_kv, v_w_out)):
        flat = (-1, w_.shape[-1])
        r_ = w_.size // w_.shape[-1]
        outs = _adamw(f"adamw_{nm}", w_.reshape(flat), g_.reshape(flat), m_.reshape(flat), v_.reshape(flat),
                      _tile_of(r_, 256, 8))
        for lst, o in zip((big_delta, big_m2, big_v2), outs):
            lst.append(o.reshape(w_.shape))

    def order(sm, bg):
        return [sm[0], bg[0], sm[1], sm[2], bg[1], sm[3], bg[2], sm[4]]

    return (loss, dx[None], *order(small_grads, big_grads), *order(small_delta, big_delta),
            *order(small_m2, big_m2), *order(small_v2, big_v2))
```

```python
import functools

import jax
import jax.numpy as jnp
from jax import lax
from jax.experimental import pallas as pl
from jax.experimental.pallas import tpu as pltpu

F32 = jnp.float32
BF16 = jnp.bfloat16

HEAD_DIM = 64
SB_WIDTH = 512
FOX_WIDTH = 512
FOX_HEADS = 8
MEM_WIDTH = 256
MIX_WIDTH = SB_WIDTH + FOX_WIDTH + MEM_WIDTH
TOTAL_HEADS = MIX_WIDTH // HEAD_DIM
IN_WIDTH = 3 * SB_WIDTH + 3 * FOX_WIDTH + FOX_HEADS + MEM_WIDTH + MIX_WIDTH
LANES = 128
QKV_WIDTH = 3 * SB_WIDTH + 3 * FOX_WIDTH
PA = QKV_WIDTH + MEM_WIDTH
PB = LANES + MIX_WIDTH
EPS = 1e-6
SCALE = HEAD_DIM ** -0.5
TILE = 256
NEG_INF = float("-inf")

ADAM_LR = 0.001
ADAM_B1 = 0.9
ADAM_B2 = 0.999
ADAM_EPS = 1e-08
ADAM_WD = 0.01
ADAM_STEP = 10

N_CHIPS = 4
N_DEV = 8
VMEM_LIMIT = 48 * 1024 * 1024
MESH = pl.DeviceIdType.MESH


def _params(*sem):
    return pltpu.CompilerParams(dimension_semantics=tuple(sem), vmem_limit_bytes=VMEM_LIMIT)


def _dot(a, b):
    return jnp.dot(a, b, preferred_element_type=F32)


def _dot_nt(a, b):
    return lax.dot_general(a, b, (((1,), (1,)), ((), ())), preferred_element_type=F32)


def _dot_tn(a, b):
    return lax.dot_general(a, b, (((0,), (0,)), ((), ())), preferred_element_type=F32)


def _split2(x):
    hi = x.astype(BF16)
    lo = (x - hi.astype(F32)).astype(BF16)
    return hi, lo


def _split3(x):
    hi = x.astype(BF16)
    r = x - hi.astype(F32)
    mid = r.astype(BF16)
    lo = (r - mid.astype(F32)).astype(BF16)
    return hi, mid, lo


def _sum_l2(x, u):
    hi, lo = _split2(x)
    return _dot(hi, u) + _dot(lo, u)


def _sum_l3(x, u):
    hi, mid, lo = _split3(x)
    return _dot(hi, u) + _dot(mid, u) + _dot(lo, u)


def _sum_r3(u, x):
    hi, mid, lo = _split3(x)
    return _dot(u, hi) + _dot(u, mid) + _dot(u, lo)


def _softplus(z):
    return jnp.maximum(z, 0.0) + jnp.log1p(jnp.exp(-jnp.abs(z)))


def _tri(n, pred):
    r = lax.broadcasted_iota(jnp.int32, (n, n), 0)
    c = lax.broadcasted_iota(jnp.int32, (n, n), 1)
    return jnp.where(pred(r, c), 1.0, 0.0).astype(BF16)


def _rows(ref, j, n=TILE):
    return pl.ds(pl.multiple_of(j * n, n), n)


def _mm(name, a, b, mode, tm, tn, out_dtype, res=None, a_lead=(), b_lead=()):
    a2, b2 = a.shape[len(a_lead):], b.shape[len(b_lead):]
    if mode == "tn":
        k, m = a2
    else:
        m, k = a2
    n = b2[0] if mode == "nt" else b2[1]
    assert m % tm == 0 and n % tn == 0, (name, m, tm, n, tn)
    na, nb = (None,) * len(a_lead), (None,) * len(b_lead)
    if mode == "tn":
        a_spec = pl.BlockSpec(na + (k, tm), lambda j, i: a_lead + (0, i))
    else:
        a_spec = pl.BlockSpec(na + (tm, k), lambda j, i: a_lead + (i, 0))
    if mode == "nt":
        b_spec = pl.BlockSpec(nb + (tn, k), lambda j, i: b_lead + (j, 0))
    else:
        b_spec = pl.BlockSpec(nb + (k, tn), lambda j, i: b_lead + (0, j))
    o_spec = pl.BlockSpec((tm, tn), lambda j, i: (i, j))
    dot = {"nn": _dot, "nt": _dot_nt, "tn": _dot_tn}[mode]

    def body(a_ref, b_ref, *rest):
        o_ref = rest[-1]
        acc = dot(a_ref[...].astype(BF16), b_ref[...].astype(BF16))
        if res is not None:
            acc = acc + rest[0][...]
        o_ref[...] = acc.astype(o_ref.dtype)

    args, specs = [a, b], [a_spec, b_spec]
    if res is not None:
        args.append(res)
        specs.append(o_spec)
    return pl.pallas_call(
        body, name=name, grid=(n // tn, m // tm), in_specs=specs, out_specs=o_spec,
        out_shape=jax.ShapeDtypeStruct((m, n), out_dtype),
        compiler_params=_params("parallel", "parallel"),
    )(*args)


def _rms_fwd(name, x, g, ts):
    s, d = x.shape

    def body(x_ref, g_ref, o_ref):
        xf = x_ref[...]
        r = lax.rsqrt(jnp.mean(xf * xf, axis=1, keepdims=True) + EPS)
        o_ref[...] = (xf * r * g_ref[...]).astype(BF16)

    return pl.pallas_call(
        body, name=name, grid=(s // ts,),
        in_specs=[pl.BlockSpec((ts, d), lambda i: (i, 0)), pl.BlockSpec((1, d), lambda i: (0, 0))],
        out_specs=pl.BlockSpec((ts, d), lambda i: (i, 0)),
        out_shape=jax.ShapeDtypeStruct((s, d), BF16),
        compiler_params=_params("parallel"),
    )(x, g)


def _rms_bwd(name, x, g, dh, dres, ts):
    s, d = x.shape

    def body(x_ref, g_ref, dh_ref, dres_ref, dx_ref, dxb_ref, dg_ref):
        @pl.when(pl.program_id(0) == 0)
        def _():
            dg_ref[...] = jnp.zeros_like(dg_ref)

        xf = x_ref[...]
        r = lax.rsqrt(jnp.mean(xf * xf, axis=1, keepdims=True) + EPS)
        xh = xf * r
        dhf = dh_ref[...]
        dg_ref[...] += jnp.sum(dhf * xh, axis=0, keepdims=True)
        dxh = dhf * g_ref[...]
        m = jnp.mean(dxh * xh, axis=1, keepdims=True)
        dx = r * (dxh - xh * m) + dres_ref[...]
        dx_ref[...] = dx
        dxb_ref[...] = dx.astype(BF16)

    row = pl.BlockSpec((ts, d), lambda i: (i, 0))
    vec = pl.BlockSpec((1, d), lambda i: (0, 0))
    return pl.pallas_call(
        body, name=name, grid=(s // ts,), in_specs=[row, vec, row, row], out_specs=[row, row, vec],
        out_shape=[jax.ShapeDtypeStruct((s, d), F32), jax.ShapeDtypeStruct((s, d), BF16),
                   jax.ShapeDtypeStruct((1, d), F32)],
        compiler_params=_params("arbitrary"),
    )(x, g, dh, dres)


def _rms_wgrad(name, x, dh):
    m_, d = x.shape

    def body(x_ref, dh_ref, dg_ref):
        xf = x_ref[...]
        r = lax.rsqrt(jnp.mean(xf * xf, axis=1, keepdims=True) + EPS)
        dg_ref[...] = jnp.sum(dh_ref[...] * xf * r, axis=0, keepdims=True)

    return pl.pallas_call(
        body, name=name, out_shape=jax.ShapeDtypeStruct((1, d), F32),
    )(x, dh)


def _final_loss(name, x, g, target, ts):
    s, d = x.shape

    def body(x_ref, g_ref, t_ref, loss_ref, dx_ref, dxb_ref, dg_ref):
        @pl.when(pl.program_id(0) == 0)
        def _():
            dg_ref[...] = jnp.zeros_like(dg_ref)
            loss_ref[...] = jnp.zeros_like(loss_ref)

        xf = x_ref[...]
        gw = g_ref[...]
        r = lax.rsqrt(jnp.mean(xf * xf, axis=1, keepdims=True) + EPS)
        xh = xf * r
        e = xh * gw - t_ref[...]
        part = 0.5 * jnp.sum(jnp.mean(e * e, axis=1, keepdims=True), axis=0, keepdims=True)
        loss_ref[...] += jnp.broadcast_to(part, loss_ref.shape)
        dy = e * (1.0 / d)
        dg_ref[...] += jnp.sum(dy * xh, axis=0, keepdims=True)
        dxh = dy * gw
        m = jnp.mean(dxh * xh, axis=1, keepdims=True)
        dx = r * (dxh - xh * m)
        dx_ref[...] = dx
        dxb_ref[...] = dx.astype(BF16)

    row = pl.BlockSpec((ts, d), lambda i: (i, 0))
    vec = pl.BlockSpec((1, d), lambda i: (0, 0))
    lvec = pl.BlockSpec((1, LANES), lambda i: (0, 0))
    return pl.pallas_call(
        body, name=name, grid=(s // ts,), in_specs=[row, vec, row], out_specs=[lvec, row, row, vec],
        out_shape=[jax.ShapeDtypeStruct((1, LANES), F32), jax.ShapeDtypeStruct((s, d), F32),
                   jax.ShapeDtypeStruct((s, d), BF16), jax.ShapeDtypeStruct((1, d), F32)],
        compiler_params=_params("arbitrary"),
    )(x, g, target)


def _gate_fwd(name, pb, bpad, fl_block):
    s = pb.shape[0]
    nb = s // TILE

    def body(fl_ref, b_ref, ccol_ref, crow_ref, carry):
        @pl.when(pl.program_id(0) == 0)
        def _():
            carry[...] = jnp.zeros_like(carry)

        u = fl_ref[...] + b_ref[...]
        lf = jnp.minimum(u, 0.0) - jnp.log1p(jnp.exp(-jnp.abs(u)))
        lower = _tri(TILE, lambda r, c: c <= r)
        c = _sum_r3(lower, lf) + carry[0:1, :]
        ccol_ref[...] = c
        crow_ref[0] = c.T[0:8, :]
        carry[...] = jnp.broadcast_to(c[TILE - 1:TILE, :], carry.shape)

    return pl.pallas_call(
        body, name=name, grid=(nb,),
        in_specs=[pl.BlockSpec((TILE, LANES), lambda i: (i, fl_block)), pl.BlockSpec((1, LANES), lambda i: (0, 0))],
        out_specs=[pl.BlockSpec((TILE, LANES), lambda i: (i, 0)), pl.BlockSpec((1, 8, TILE), lambda i: (i, 0, 0))],
        out_shape=[jax.ShapeDtypeStruct((s, LANES), F32), jax.ShapeDtypeStruct((nb, 8, TILE), F32)],
        scratch_shapes=[pltpu.VMEM((8, LANES), F32)],
        compiler_params=_params("arbitrary"),
    )(pb, bpad)


def _gate_bwd(name, pb, bpad, colsum, fl_block):
    s = pb.shape[0]
    nb = s // TILE

    def body(fl_ref, b_ref, cs_ref, dl_ref, db_ref, carry):
        @pl.when(pl.program_id(0) == 0)
        def _():
            carry[...] = jnp.zeros_like(carry)
            db_ref[...] = jnp.zeros_like(db_ref)

        upper = _tri(TILE, lambda r, c: r >= c)
        rsum = _sum_l3(cs_ref[0], upper) + carry[:, 0:1]
        carry[...] = jnp.broadcast_to(rsum[:, 0:1], carry.shape)
        full = jnp.concatenate([rsum, jnp.zeros((LANES - 8, TILE), F32)], axis=0)
        dlf = -full.T
        u = fl_ref[...] + b_ref[...]
        dlogit = dlf * (1.0 - jax.nn.sigmoid(u))
        dl_ref[...] = dlogit.astype(BF16)
        db_ref[...] += jnp.sum(dlogit, axis=0, keepdims=True)

    rev = lambda i: (nb - 1 - i, 0)
    return pl.pallas_call(
        body, name=name, grid=(nb,),
        in_specs=[pl.BlockSpec((TILE, LANES), lambda i: (nb - 1 - i, fl_block)),
                  pl.BlockSpec((1, LANES), lambda i: (0, 0)),
                  pl.BlockSpec((1, 8, TILE), lambda i: (nb - 1 - i, 0, 0))],
        out_specs=[pl.BlockSpec((TILE, LANES), rev), pl.BlockSpec((1, LANES), lambda i: (0, 0))],
        out_shape=[jax.ShapeDtypeStruct((s, LANES), BF16), jax.ShapeDtypeStruct((1, LANES), F32)],
        scratch_shapes=[pltpu.VMEM((8, LANES), F32)],
        compiler_params=_params("arbitrary"),
    )(pb, bpad, colsum)


def _head_slices(hh):
    return slice(HEAD_DIM * hh, HEAD_DIM * (hh + 1))


def _scaled_q(q_ref, sl):
    return (q_ref[:, sl].astype(F32) * SCALE).astype(BF16)


def _sb_tile(q, kj, carry, strict, u_after, diag):
    z = _dot_nt(q, kj)
    sp = _softplus(z)
    lf = -sp
    if diag:
        lf = jnp.where(strict, lf, 0.0)
    sx = _sum_l2(lf, u_after)
    a = jnp.exp((z - sp) + sx + carry)
    if diag:
        a = jnp.where(strict, a, 0.0)
    return z, sp, a, carry + sx[:, 0:1] + lf[:, 0:1]


def _sb_fwd(name, pa, col0):
    s = pa.shape[0]
    nb = s // TILE
    cb = col0 // LANES

    def body(q_ref, k_ref, v_ref, o_ref):
        i = pl.program_id(1)
        r = lax.broadcasted_iota(jnp.int32, (TILE, TILE), 0)
        c = lax.broadcasted_iota(jnp.int32, (TILE, TILE), 1)
        strict = c < r
        u_after = _tri(TILE, lambda rr, cc: rr > cc)
        qs = [_scaled_q(q_ref, _head_slices(hh)) for hh in range(2)]

        def tile(j, state, diag):
            kblk = k_ref[_rows(k_ref, j), :]
            vblk = v_ref[_rows(v_ref, j), :]
            new = []
            for hh in range(2):
                sl = _head_slices(hh)
                carry, acc = state[hh]
                _, _, a, carry = _sb_tile(qs[hh], kblk[:, sl], carry, strict, u_after, diag)
                new.append((carry, acc + _dot(a.astype(BF16), vblk[:, sl])))
            return tuple(new)

        zero = (jnp.zeros((TILE, 1), F32), jnp.zeros((TILE, HEAD_DIM), F32))
        state = tile(i, (zero, zero), True)
        state = lax.fori_loop(0, i, lambda t, st: tile(i - 1 - t, st, False), state)
        o_ref[...] = jnp.concatenate([state[0][1], state[1][1]], axis=1)

    return pl.pallas_call(
        body, name=name, grid=(4, nb),
        in_specs=[pl.BlockSpec((TILE, LANES), lambda p, i: (i, cb + p)),
                  pl.BlockSpec((s, LANES), lambda p, i: (0, cb + 4 + p)),
                  pl.BlockSpec((s, LANES), lambda p, i: (0, cb + 8 + p))],
        out_specs=pl.BlockSpec((TILE, LANES), lambda p, i: (i, p)),
        out_shape=jax.ShapeDtypeStruct((s, SB_WIDTH), F32),
        compiler_params=_params("parallel", "arbitrary"),
    )(pa, pa, pa)


def _sb_bwd(name, pa, col0, dout, dcol0):
    s = pa.shape[0]
    nb = s // TILE
    cb = col0 // LANES
    db = dcol0 // LANES

    def body(q_ref, k_ref, v_ref, do_ref, dq_ref, dk_ref, dv_ref, dk_acc, dv_acc, gpan, span):
        i = pl.program_id(1)

        @pl.when(i == 0)
        def _():
            dk_acc[...] = jnp.zeros_like(dk_acc)
            dv_acc[...] = jnp.zeros_like(dv_acc)

        r = lax.broadcasted_iota(jnp.int32, (TILE, TILE), 0)
        c = lax.broadcasted_iota(jnp.int32, (TILE, TILE), 1)
        strict = c < r
        u_after = _tri(TILE, lambda rr, cc: rr > cc)
        u_before = _tri(TILE, lambda rr, cc: rr < cc)
        qs = [_scaled_q(q_ref, _head_slices(hh)) for hh in range(2)]
        dos = [do_ref[:, _head_slices(hh)].astype(BF16) for hh in range(2)]

        def pass1(j, carries, diag):
            kblk = k_ref[_rows(k_ref, j), :]
            vblk = v_ref[_rows(v_ref, j), :]
            new = []
            for hh in range(2):
                sl = _head_slices(hh)
                z, sp, a, carry = _sb_tile(qs[hh], kblk[:, sl], carries[hh], strict, u_after, diag)
                gpan[hh, j] = a * _dot_nt(dos[hh], vblk[:, sl])
                span[hh, j] = jnp.exp(z - sp)
                dv_acc[hh, _rows(None, j), :] += _dot_tn(a.astype(BF16), dos[hh])
                new.append(carry)
            return tuple(new)

        zero1 = jnp.zeros((TILE, 1), F32)
        carries = pass1(i, (zero1, zero1), True)
        lax.fori_loop(0, i, lambda t, ca: pass1(i - 1 - t, ca, False), carries)

        def pass2(j, state, diag):
            kblk = k_ref[_rows(k_ref, j), :]
            new = []
            for hh in range(2):
                before, dq = state[hh]
                g = gpan[hh, j]
                sig = span[hh, j]
                pfx = _sum_l2(g, u_before) + before
                dz = g * (1.0 - sig) - sig * pfx
                if diag:
                    dz = jnp.where(strict, dz, 0.0)
                dzb = dz.astype(BF16)
                dk_acc[hh, _rows(None, j), :] += _dot_tn(dzb, qs[hh])
                new.append((pfx[:, TILE - 1:TILE] + g[:, TILE - 1:TILE], dq + _dot(dzb, kblk[:, _head_slices(hh)])))
            return tuple(new)

        zero2 = (zero1, jnp.zeros((TILE, HEAD_DIM), F32))
        state = lax.fori_loop(0, i, lambda j, st: pass2(j, st, False), (zero2, zero2))
        state = pass2(i, state, True)
        dq_ref[...] = jnp.concatenate([state[0][1] * SCALE, state[1][1] * SCALE], axis=1).astype(BF16)

        @pl.when(i == nb - 1)
        def _():
            dk_ref[...] = jnp.concatenate([dk_acc[0], dk_acc[1]], axis=1).astype(BF16)
            dv_ref[...] = jnp.concatenate([dv_acc[0], dv_acc[1]], axis=1).astype(BF16)

    qspec = pl.BlockSpec((TILE, LANES), lambda p, i: (i, p))
    kvspec = pl.BlockSpec((s, LANES), lambda p, i: (0, p))
    out = jax.ShapeDtypeStruct((s, SB_WIDTH), BF16)
    return pl.pallas_call(
        body, name=name, grid=(4, nb),
        in_specs=[pl.BlockSpec((TILE, LANES), lambda p, i: (i, cb + p)),
                  pl.BlockSpec((s, LANES), lambda p, i: (0, cb + 4 + p)),
                  pl.BlockSpec((s, LANES), lambda p, i: (0, cb + 8 + p)),
                  pl.BlockSpec((TILE, LANES), lambda p, i: (i, db + p))],
        out_specs=[qspec, kvspec, kvspec], out_shape=[out, out, out],
        scratch_shapes=[pltpu.VMEM((2, s, HEAD_DIM), F32), pltpu.VMEM((2, s, HEAD_DIM), F32),
                        pltpu.VMEM((2, nb, TILE, TILE), F32), pltpu.VMEM((2, nb, TILE, TILE), F32)],
        compiler_params=_params("arbitrary", "arbitrary"),
    )(pa, pa, pa, dout)


def _fox_scores(q, kj, cq, crj, causal, diag):
    sc = _dot_nt(q, kj) + (cq - crj)
    if diag:
        sc = jnp.where(causal, sc, NEG_INF)
    return sc


def _fox_fwd(name, pa, col0, ccol4, crow4):
    s = pa.shape[0]
    nb = s // TILE
    cb = col0 // LANES

    def body(q_ref, k_ref, v_ref, cc_ref, cr_ref, o_ref, lse_ref):
        i = pl.program_id(1)
        r = lax.broadcasted_iota(jnp.int32, (TILE, TILE), 0)
        c = lax.broadcasted_iota(jnp.int32, (TILE, TILE), 1)
        causal = c <= r
        qs = [_scaled_q(q_ref, _head_slices(hh)) for hh in range(2)]
        cqs = [cc_ref[:, HEAD_DIM * hh:HEAD_DIM * hh + 1] for hh in range(2)]

        def tile(j, state, diag):
            kblk = k_ref[_rows(k_ref, j), :]
            vblk = v_ref[_rows(v_ref, j), :]
            new = []
            for hh in range(2):
                sl = _head_slices(hh)
                m, l, acc = state[hh]
                sc = _fox_scores(qs[hh], kblk[:, sl], cqs[hh], cr_ref[j, hh:hh + 1, :], causal, diag)
                m2 = jnp.maximum(m, jnp.max(sc, axis=1, keepdims=True))
                alpha = jnp.exp(m - m2)
                p = jnp.exp(sc - m2)
                new.append((m2, l * alpha + jnp.sum(p, axis=1, keepdims=True),
                            acc * alpha + _dot(p.astype(BF16), vblk[:, sl])))
            return tuple(new)

        zero = (jnp.full((TILE, 1), NEG_INF, F32), jnp.zeros((TILE, 1), F32), jnp.zeros((TILE, HEAD_DIM), F32))
        state = tile(i, (zero, zero), True)
        state = lax.fori_loop(0, i, lambda t, st: tile(i - 1 - t, st, False), state)
        o_ref[...] = jnp.concatenate([state[hh][2] / state[hh][1] for hh in range(2)], axis=1)
        lse_ref[...] = jnp.concatenate(
            [jnp.broadcast_to(state[hh][0] + jnp.log(state[hh][1]), (TILE, HEAD_DIM)) for hh in range(2)], axis=1)

    return pl.pallas_call(
        body, name=name, grid=(4, nb),
        in_specs=[pl.BlockSpec((TILE, LANES), lambda p, i: (i, cb + p)),
                  pl.BlockSpec((s, LANES), lambda p, i: (0, cb + 4 + p)),
                  pl.BlockSpec((s, LANES), lambda p, i: (0, cb + 8 + p)),
                  pl.BlockSpec((None, TILE, LANES), lambda p, i: (p, i, 0)),
                  pl.BlockSpec((None, nb, 8, TILE), lambda p, i: (p, 0, 0, 0))],
        out_specs=[pl.BlockSpec((TILE, LANES), lambda p, i: (i, p)),
                   pl.BlockSpec((None, TILE, LANES), lambda p, i: (p, i, 0))],
        out_shape=[jax.ShapeDtypeStruct((s, FOX_WIDTH), F32), jax.ShapeDtypeStruct((4, s, LANES), F32)],
        compiler_params=_params("parallel", "arbitrary"),
    )(pa, pa, pa, ccol4, crow4)


def _fox_bwd(name, pa, col0, ccol4, crow4, out, lse, dout, dcol0):
    s = pa.shape[0]
    nb = s // TILE
    cb = col0 // LANES
    db = dcol0 // LANES

    def body(q_ref, k_ref, v_ref, cc_ref, cr_ref, o_ref, lse_ref, do_ref,
             dq_ref, dk_ref, dv_ref, cs_ref, dk_acc, dv_acc):
        i = pl.program_id(1)

        @pl.when(i == 0)
        def _():
            dk_acc[...] = jnp.zeros_like(dk_acc)
            dv_acc[...] = jnp.zeros_like(dv_acc)
            cs_ref[...] = jnp.zeros_like(cs_ref)

        r = lax.broadcasted_iota(jnp.int32, (TILE, TILE), 0)
        c = lax.broadcasted_iota(jnp.int32, (TILE, TILE), 1)
        causal = c <= r
        qs = [_scaled_q(q_ref, _head_slices(hh)) for hh in range(2)]
        cqs = [cc_ref[:, HEAD_DIM * hh:HEAD_DIM * hh + 1] for hh in range(2)]
        lses = [lse_ref[:, HEAD_DIM * hh:HEAD_DIM * hh + 1] for hh in range(2)]
        dofs = [do_ref[:, _head_slices(hh)] for hh in range(2)]
        dos = [d_.astype(BF16) for d_ in dofs]
        deltas = [jnp.sum(dofs[hh] * o_ref[:, _head_slices(hh)], axis=1, keepdims=True) for hh in range(2)]

        def tile(j, state, diag):
            kblk = k_ref[_rows(k_ref, j), :]
            vblk = v_ref[_rows(v_ref, j), :]
            new = []
            for hh in range(2):
                sl = _head_slices(hh)
                dq, rs = state[hh]
                sc = _fox_scores(qs[hh], kblk[:, sl], cqs[hh], cr_ref[j, hh:hh + 1, :], causal, diag)
                p = jnp.exp(sc - lses[hh])
                ds = p * (_dot_nt(dos[hh], vblk[:, sl]) - deltas[hh])
                dsb = ds.astype(BF16)
                dv_acc[hh, _rows(None, j), :] += _dot_tn(p.astype(BF16), dos[hh])
                dk_acc[hh, _rows(None, j), :] += _dot_tn(dsb, qs[hh])
                cs_ref[j, hh:hh + 1, :] += jnp.sum(ds, axis=0, keepdims=True)
                new.append((dq + _dot(dsb, kblk[:, sl]), rs + jnp.sum(ds, axis=1, keepdims=True)))
            return tuple(new)

        zero = (jnp.zeros((TILE, HEAD_DIM), F32), jnp.zeros((TILE, 1), F32))
        state = tile(i, (zero, zero), True)
        state = lax.fori_loop(0, i, lambda t, st: tile(i - 1 - t, st, False), state)
        for hh in range(2):
            cs_ref[i, hh:hh + 1, :] -= jnp.broadcast_to(state[hh][1], (TILE, LANES)).T[0:1, :]
        dq_ref[...] = jnp.concatenate([state[0][0] * SCALE, state[1][0] * SCALE], axis=1).astype(BF16)

        @pl.when(i == nb - 1)
        def _():
            dk_ref[...] = jnp.concatenate([dk_acc[0], dk_acc[1]], axis=1).astype(BF16)
            dv_ref[...] = jnp.concatenate([dv_acc[0], dv_acc[1]], axis=1).astype(BF16)

    qspec = pl.BlockSpec((TILE, LANES), lambda p, i: (i, p))
    kvspec = pl.BlockSpec((s, LANES), lambda p, i: (0, p))
    o3 = jax.ShapeDtypeStruct((s, FOX_WIDTH), BF16)
    return pl.pallas_call(
        body, name=name, grid=(4, nb),
        in_specs=[pl.BlockSpec((TILE, LANES), lambda p, i: (i, cb + p)),
                  pl.BlockSpec((s, LANES), lambda p, i: (0, cb + 4 + p)),
                  pl.BlockSpec((s, LANES), lambda p, i: (0, cb + 8 + p)),
                  pl.BlockSpec((None, TILE, LANES), lambda p, i: (p, i, 0)),
                  pl.BlockSpec((None, nb, 8, TILE), lambda p, i: (p, 0, 0, 0)),
                  qspec,
                  pl.BlockSpec((None, TILE, LANES), lambda p, i: (p, i, 0)),
                  pl.BlockSpec((TILE, LANES), lambda p, i: (i, db + p))],
        out_specs=[qspec, kvspec, kvspec, pl.BlockSpec((None, nb, 8, TILE), lambda p, i: (p, 0, 0, 0))],
        out_shape=[o3, o3, o3, jax.ShapeDtypeStruct((4, nb, 8, TILE), F32)],
        scratch_shapes=[pltpu.VMEM((2, s, HEAD_DIM), F32), pltpu.VMEM((2, s, HEAD_DIM), F32)],
        compiler_params=_params("arbitrary", "arbitrary"),
    )(pa, pa, pa, ccol4, crow4, out, lse, dout)


def _mem_fwd(name, pa, mkv):
    s = pa.shape[0]
    ml = mkv.shape[0]
    nb = s // TILE
    cb = QKV_WIDTH // LANES

    def body(q_ref, k_ref, v_ref, o_ref, lse_ref):
        outs, lses = [], []
        for hh in range(2):
            sl = _head_slices(hh)
            sc = _dot_nt(_scaled_q(q_ref, sl), k_ref[:, sl])
            m = jnp.max(sc, axis=1, keepdims=True)
            p = jnp.exp(sc - m)
            l = jnp.sum(p, axis=1, keepdims=True)
            outs.append(_dot(p.astype(BF16), v_ref[:, sl]) / l)
            lses.append(jnp.broadcast_to(m + jnp.log(l), (TILE, HEAD_DIM)))
        o_ref[...] = jnp.concatenate(outs, axis=1)
        lse_ref[...] = jnp.concatenate(lses, axis=1)

    return pl.pallas_call(
        body, name=name, grid=(2, nb),
        in_specs=[pl.BlockSpec((TILE, LANES), lambda p, i: (i, cb + p)),
                  pl.BlockSpec((ml, LANES), lambda p, i: (0, p)),
                  pl.BlockSpec((ml, LANES), lambda p, i: (0, 2 + p))],
        out_specs=[pl.BlockSpec((TILE, LANES), lambda p, i: (i, p)),
                   pl.BlockSpec((None, TILE, LANES), lambda p, i: (p, i, 0))],
        out_shape=[jax.ShapeDtypeStruct((s, MEM_WIDTH), F32), jax.ShapeDtypeStruct((2, s, LANES), F32)],
        compiler_params=_params("parallel", "parallel"),
    )(pa, mkv, mkv)


def _mem_bwd(name, pa, mkv, out, lse, dout, dcol0):
    s = pa.shape[0]
    ml = mkv.shape[0]
    nb = s // TILE
    cb = QKV_WIDTH // LANES
    db = dcol0 // LANES

    def body(q_ref, k_ref, v_ref, o_ref, lse_ref, do_ref, dq_ref, dk_ref, dv_ref, dk_acc, dv_acc):
        i = pl.program_id(1)

        @pl.when(i == 0)
        def _():
            dk_acc[...] = jnp.zeros_like(dk_acc)
            dv_acc[...] = jnp.zeros_like(dv_acc)

        dqs = []
        for hh in range(2):
            sl = _head_slices(hh)
            q = _scaled_q(q_ref, sl)
            kh = k_ref[:, sl]
            dof = do_ref[:, sl]
            do = dof.astype(BF16)
            delta = jnp.sum(dof * o_ref[:, sl], axis=1, keepdims=True)
            p = jnp.exp(_dot_nt(q, kh) - lse_ref[:, HEAD_DIM * hh:HEAD_DIM * hh + 1])
            ds = (p * (_dot_nt(do, v_ref[:, sl]) - delta)).astype(BF16)
            dv_acc[hh] += _dot_tn(p.astype(BF16), do)
            dk_acc[hh] += _dot_tn(ds, q)
            dqs.append(_dot(ds, kh) * SCALE)
        dq_ref[...] = jnp.concatenate(dqs, axis=1).astype(BF16)

        @pl.when(i == nb - 1)
        def _():
            dk_ref[...] = jnp.concatenate([dk_acc[0], dk_acc[1]], axis=1).astype(BF16)
            dv_ref[...] = jnp.concatenate([dv_acc[0], dv_acc[1]], axis=1).astype(BF16)

    qspec = pl.BlockSpec((TILE, LANES), lambda p, i: (i, p))
    kvspec = pl.BlockSpec((ml, LANES), lambda p, i: (0, p))
    okv = jax.ShapeDtypeStruct((ml, MEM_WIDTH), BF16)
    return pl.pallas_call(
        body, name=name, grid=(2, nb),
        in_specs=[pl.BlockSpec((TILE, LANES), lambda p, i: (i, cb + p)),
                  pl.BlockSpec((ml, LANES), lambda p, i: (0, p)),
                  pl.BlockSpec((ml, LANES), lambda p, i: (0, 2 + p)),
                  qspec,
                  pl.BlockSpec((None, TILE, LANES), lambda p, i: (p, i, 0)),
                  pl.BlockSpec((TILE, LANES), lambda p, i: (i, db + p))],
        out_specs=[qspec, kvspec, kvspec],
        out_shape=[jax.ShapeDtypeStruct((s, MEM_WIDTH), BF16), okv, okv],
        scratch_shapes=[pltpu.VMEM((2, ml, HEAD_DIM), F32), pltpu.VMEM((2, ml, HEAD_DIM), F32)],
        compiler_params=_params("arbitrary", "arbitrary"),
    )(pa, mkv, mkv, out, lse, dout)


def _head_maps():
    col = jnp.arange(MIX_WIDTH)[:, None] // HEAD_DIM
    g = (col == jnp.arange(LANES)[None, :]).astype(BF16)
    return g, g.T


def _normed_heads(osb_ref, ofx_ref, om_ref, g_ref, gt_ref):
    y = jnp.concatenate([osb_ref[...], ofx_ref[...], om_ref[...]], axis=1)
    msq = _sum_l3(y * y, g_ref[...]) * (1.0 / HEAD_DIM)
    rf = _sum_l3(lax.rsqrt(msq + EPS), gt_ref[...])
    return y * rf, rf


def _out_fwd(name, o_sb, o_fx, o_m, pb, ow, x, w_out, ts):
    s, d = x.shape
    g, gt = _head_maps()

    def body(osb_ref, ofx_ref, om_ref, gate_ref, ow_ref, x_ref, w_ref, g_ref, gt_ref, xo_ref, y2_ref):
        yh, _ = _normed_heads(osb_ref, ofx_ref, om_ref, g_ref, gt_ref)
        gate = gate_ref[...]
        y2 = (yh * ow_ref[...] * (gate * jax.nn.sigmoid(gate))).astype(BF16)
        y2_ref[...] = y2
        xo_ref[...] = x_ref[...] + _dot(y2, w_ref[...])

    return pl.pallas_call(
        body, name=name, grid=(s // ts,),
        in_specs=[_row_spec(ts, SB_WIDTH), _row_spec(ts, FOX_WIDTH), _row_spec(ts, MEM_WIDTH),
                  _row_spec(ts, MIX_WIDTH), _const_spec((1, MIX_WIDTH)), _row_spec(ts, d),
                  _const_spec((MIX_WIDTH, d)),
                  _const_spec((MIX_WIDTH, LANES)), _const_spec((LANES, MIX_WIDTH))],
        out_specs=[_row_spec(ts, d), _row_spec(ts, MIX_WIDTH)],
        out_shape=[jax.ShapeDtypeStruct((s, d), F32), jax.ShapeDtypeStruct((s, MIX_WIDTH), BF16)],
        compiler_params=_params("parallel"),
    )(o_sb, o_fx, o_m, pb, ow, x, w_out, g, gt)


def _row_spec(ts, w):
    return pl.BlockSpec((ts, w), lambda i: (i, 0))


def _const_spec(shape):
    return pl.BlockSpec(shape, lambda i: (0,) * len(shape))


def _out_bwd(name, dxb, o_sb, o_fx, o_m, pb, ow, w_out, ts):
    s, d = dxb.shape
    g, gt = _head_maps()

    def body(dx_ref, osb_ref, ofx_ref, om_ref, gate_ref, ow_ref, w_ref, g_ref, gt_ref, dy_ref, dgate_ref, dow_ref):
        @pl.when(pl.program_id(0) == 0)
        def _():
            dow_ref[...] = jnp.zeros_like(dow_ref)

        dy2 = _dot_nt(dx_ref[...], w_ref[...])
        yh, rf = _normed_heads(osb_ref, ofx_ref, om_ref, g_ref, gt_ref)
        gate = gate_ref[...]
        sig = jax.nn.sigmoid(gate)
        ow_v = ow_ref[...]
        dgate_ref[...] = (dy2 * (yh * ow_v) * (sig * (1.0 + gate * (1.0 - sig)))).astype(BF16)
        dn = dy2 * (gate * sig)
        dow_ref[...] += jnp.sum(dn * yh, axis=0, keepdims=True)
        dyh = dn * ow_v
        t = _sum_l3(dyh * yh, g_ref[...]) * (1.0 / HEAD_DIM)
        dy_ref[...] = rf * (dyh - yh * _sum_l3(t, gt_ref[...]))

    return pl.pallas_call(
        body, name=name, grid=(s // ts,),
        in_specs=[_row_spec(ts, d), _row_spec(ts, SB_WIDTH), _row_spec(ts, FOX_WIDTH), _row_spec(ts, MEM_WIDTH),
                  _row_spec(ts, MIX_WIDTH), _const_spec((1, MIX_WIDTH)),
                  _const_spec((MIX_WIDTH, d)),
                  _const_spec((MIX_WIDTH, LANES)), _const_spec((LANES, MIX_WIDTH))],
        out_specs=[_row_spec(ts, MIX_WIDTH), _row_spec(ts, MIX_WIDTH), _const_spec((1, MIX_WIDTH))],
        out_shape=[jax.ShapeDtypeStruct((s, MIX_WIDTH), F32), jax.ShapeDtypeStruct((s, MIX_WIDTH), BF16),
                   jax.ShapeDtypeStruct((1, MIX_WIDTH), F32)],
        compiler_params=_params("arbitrary"),
    )(dxb, o_sb, o_fx, o_m, pb, ow, w_out, g, gt)


def _adamw(name, w, g, m, v, tr):
    rows, cols = w.shape

    def body(w_ref, g_ref, m_ref, v_ref, d_ref, m2_ref, v2_ref):
        gv = g_ref[...]
        m2 = ADAM_B1 * m_ref[...] + (1.0 - ADAM_B1) * gv
        v2 = ADAM_B2 * v_ref[...] + (1.0 - ADAM_B2) * (gv * gv)
        m_hat = m2 / (1.0 - ADAM_B1 ** ADAM_STEP)
        v_hat = v2 / (1.0 - ADAM_B2 ** ADAM_STEP)
        d_ref[...] = -ADAM_LR * (m_hat / (jnp.sqrt(v_hat) + ADAM_EPS) + ADAM_WD * w_ref[...])
        m2_ref[...] = m2
        v2_ref[...] = v2

    spec = _row_spec(tr, cols)
    shp = jax.ShapeDtypeStruct((rows, cols), F32)
    return pl.pallas_call(
        body, name=name, grid=(rows // tr,), in_specs=[spec] * 4, out_specs=[spec] * 3, out_shape=[shp] * 3,
        compiler_params=_params("parallel"),
    )(w, g, m, v)


def _adamw_sharded(name, w, m, v, g_own, g_other, cvec, tr):
    depth, rows, cols = w.shape
    nt = rows // 2 // tr

    def body(c_ref, w_ref, m_ref, v_ref, *rest):
        g_refs, (g_ref, d_ref, m2_ref, v2_ref) = rest[:2 * depth], rest[2 * depth:]
        layer, mine = pl.program_id(0), pl.program_id(1) == c_ref[0]
        gv = None
        for lt in range(depth):
            cand = jnp.where(mine, g_refs[lt][...], g_refs[depth + lt][...])
            gv = cand if gv is None else jnp.where(layer == lt, cand, gv)
        m2 = ADAM_B1 * m_ref[...] + (1.0 - ADAM_B1) * gv
        v2 = ADAM_B2 * v_ref[...] + (1.0 - ADAM_B2) * (gv * gv)
        m_hat = m2 / (1.0 - ADAM_B1 ** ADAM_STEP)
        v_hat = v2 / (1.0 - ADAM_B2 ** ADAM_STEP)
        g_ref[...] = gv
        d_ref[...] = -ADAM_LR * (m_hat / (jnp.sqrt(v_hat) + ADAM_EPS) + ADAM_WD * w_ref[...])
        m2_ref[...] = m2
        v2_ref[...] = v2

    def g_map(lt, own):
        def index(l, hf, i, c_ref):
            use = jnp.logical_and(l == lt, (hf == c_ref[0]) == own)
            return jnp.where(use, i, 0), 0
        return index

    full = pl.BlockSpec((None, tr, cols), lambda l, hf, i, c_ref: (l, hf * nt + i, 0))
    g_specs = [pl.BlockSpec((tr, cols), g_map(lt, own)) for own in (True, False) for lt in range(depth)]
    shp = jax.ShapeDtypeStruct((depth, rows, cols), F32)
    return pl.pallas_call(
        body, name=name,
        grid_spec=pltpu.PrefetchScalarGridSpec(
            num_scalar_prefetch=1, grid=(depth, 2, nt), in_specs=[full] * 3 + g_specs, out_specs=[full] * 4),
        out_shape=[shp] * 4,
        compiler_params=_params("arbitrary", "arbitrary", "arbitrary"),
    )(cvec, w, m, v, *g_own, *g_other)


HBM_SPEC = pl.BlockSpec(memory_space=pltpu.HBM)


def _place():
    x, y, c = lax.axis_index("x"), lax.axis_index("y"), lax.axis_index("c")
    chips = [(1 - x, y), (x, 1 - y), (1 - x, 1 - y)]
    return x, y, c, chips


def _remote(src, dst, send_sems, recv_sems, k, to):
    return pltpu.make_async_remote_copy(src_ref=src, dst_ref=dst, send_sem=send_sems.at[k], recv_sem=recv_sems.at[k],
                                        device_id=to, device_id_type=MESH)


def _half_rows(n_rows, cc):
    rh = n_rows // 2
    return pl.ds(pl.multiple_of(cc * rh, 16), rh)


def _dma_sems(n):
    return [pltpu.SemaphoreType.DMA((n,)), pltpu.SemaphoreType.DMA((n,))]


def _gather_weights(name, shards):
    n = len(shards)

    def body(*refs):
        in_refs, out_refs, (send_sems, recv_sems) = refs[:n], refs[n:2 * n], refs[2 * n:]
        x, y, c, chips = _place()
        mine = 2 * x + y
        sibling = (x, y, 1 - c)
        first, passed = [], []
        for a, (in_ref, out_ref) in enumerate(zip(in_refs, out_refs)):
            mh = _half_rows(in_ref.shape[0], c)
            for j, (cx, cy) in enumerate(chips):
                first.append(_remote(in_ref.at[mh], out_ref.at[mine, mh], send_sems, recv_sems, 6 * a + j, (cx, cy, c)))
        for cp in first:
            cp.start()
        for j, (cx, cy) in enumerate(chips):
            for a, out_ref in enumerate(out_refs):
                landed = out_ref.at[2 * cx + cy, _half_rows(out_ref.shape[1], c)]
                _remote(landed, landed, send_sems, recv_sems, 6 * a + j, sibling).wait_recv()
                cp = _remote(landed, landed, send_sems, recv_sems, 6 * a + 3 + j, sibling)
                cp.start()
                passed.append(cp)
        for j, (cx, cy) in enumerate(chips):
            for a, out_ref in enumerate(out_refs):
                other = out_ref.at[2 * cx + cy, _half_rows(out_ref.shape[1], 1 - c)]
                _remote(other, other, send_sems, recv_sems, 6 * a + 3 + j, sibling).wait_recv()
        for cp in first + passed:
            cp.wait_send()

    return pl.pallas_call(
        body, name=name, in_specs=[HBM_SPEC] * n, out_specs=[HBM_SPEC] * n,
        out_shape=[jax.ShapeDtypeStruct((N_CHIPS,) + s_.shape, s_.dtype) for s_ in shards],
        scratch_shapes=_dma_sems(6 * n),
    )(*shards)


def _swap_halves(name, g4s):
    n = len(g4s)

    def body(*refs):
        in_refs, out_refs, (send_sems, recv_sems) = refs[:n], refs[n:2 * n], refs[2 * n:]
        x, y, c, _ = _place()
        cps = [_remote(in_ref.at[:, _half_rows(in_ref.shape[1], 1 - c), :], out_ref, send_sems, recv_sems, a, (x, y, 1 - c))
               for a, (in_ref, out_ref) in enumerate(zip(in_refs, out_refs))]
        for cp in cps:
            cp.start()
        for cp in cps:
            cp.wait()

    return pl.pallas_call(
        body, name=name, in_specs=[HBM_SPEC] * n, out_specs=[HBM_SPEC] * n,
        out_shape=[jax.ShapeDtypeStruct((g.shape[0], g.shape[1] // 2, g.shape[2]), g.dtype) for g in g4s],
        scratch_shapes=_dma_sems(n),
    )(*g4s)


def _add_half(name, g4, r1, cvec, tr):
    n, r, w = g4.shape
    rh = r // 2
    nblk = rh // tr

    def body(c_ref, a_ref, b_ref, o_ref):
        o_ref[...] = (a_ref[...] + b_ref[...]).astype(BF16)

    return pl.pallas_call(
        body, name=name,
        grid_spec=pltpu.PrefetchScalarGridSpec(
            num_scalar_prefetch=1, grid=(n, nblk),
            in_specs=[pl.BlockSpec((None, tr, w), lambda k, i, c_ref: (k, c_ref[0] * nblk + i, 0)),
                      pl.BlockSpec((None, tr, w), lambda k, i, c_ref: (k, i, 0))],
            out_specs=pl.BlockSpec((None, tr, w), lambda k, i, c_ref: (k, i, 0))),
        out_shape=jax.ShapeDtypeStruct((n, rh, w), BF16),
        compiler_params=_params("parallel", "parallel"),
    )(cvec, g4, r1)


def _scatter_chips(name, h4s):
    n = len(h4s)

    def body(*refs):
        in_refs, out_refs, (send_sems, recv_sems) = refs[:n], refs[n:2 * n], refs[2 * n:]
        x, y, c, chips = _place()
        sends = [_remote(in_ref.at[2 * cx + cy], out_ref.at[j], send_sems, recv_sems, 3 * a + j, (cx, cy, c))
                 for a, (in_ref, out_ref) in enumerate(zip(in_refs, out_refs)) for j, (cx, cy) in enumerate(chips)]
        for cp in sends:
            cp.start()
        for a, out_ref in enumerate(out_refs):
            for j, (cx, cy) in enumerate(chips):
                got = out_ref.at[j]
                _remote(got, got, send_sems, recv_sems, 3 * a + j, (cx, cy, c)).wait_recv()
        for cp in sends:
            cp.wait_send()

    return pl.pallas_call(
        body, name=name, in_specs=[HBM_SPEC] * n, out_specs=[HBM_SPEC] * n,
        out_shape=[jax.ShapeDtypeStruct((3,) + h.shape[1:], h.dtype) for h in h4s],
        scratch_shapes=_dma_sems(3 * n),
    )(*h4s)


def _sum_chips(name, h4, r3, mvec, tr):
    _, rh, w = h4.shape

    def body(m_ref, a_ref, b_ref, c_ref, d_ref, o_ref):
        o_ref[...] = ((a_ref[...].astype(F32) + b_ref[...].astype(F32)) + c_ref[...].astype(F32)) + d_ref[...].astype(F32)

    specs = [pl.BlockSpec((None, tr, w), lambda i, m_ref: (m_ref[0], i, 0))]
    specs += [pl.BlockSpec((None, tr, w), functools.partial(lambda k, i, m_ref: (k, i, 0), k)) for k in range(3)]
    return pl.pallas_call(
        body, name=name,
        grid_spec=pltpu.PrefetchScalarGridSpec(
            num_scalar_prefetch=1, grid=(rh // tr,), in_specs=specs,
            out_specs=pl.BlockSpec((tr, w), lambda i, m_ref: (i, 0))),
        out_shape=jax.ShapeDtypeStruct((rh, w), F32),
        compiler_params=_params("parallel"),
    )(mvec, h4, r3, r3, r3)


def _swap_reduced(name, ghs):
    n = len(ghs)

    def body(*refs):
        in_refs, out_refs, (send_sems, recv_sems) = refs[:n], refs[n:2 * n], refs[2 * n:]
        x, y, c, _ = _place()
        cps = [_remote(in_ref, out_ref, send_sems, recv_sems, a, (x, y, 1 - c))
               for a, (in_ref, out_ref) in enumerate(zip(in_refs, out_refs))]
        for cp in cps:
            cp.start()
        for cp in cps:
            cp.wait()

    return pl.pallas_call(
        body, name=name, in_specs=[HBM_SPEC] * n, out_specs=[HBM_SPEC] * n,
        out_shape=[jax.ShapeDtypeStruct(g.shape, g.dtype) for g in ghs],
        scratch_shapes=_dma_sems(n),
    )(*ghs)


def _allreduce_small(name, vec):
    rows, w = vec.shape

    def body(v_ref, o_ref, buf, send_sems, recv_sems):
        x, y, c, _ = _place()
        me = 4 * x + 2 * y + c
        buf[me] = v_ref[...]
        flips = [(fx, fy, fc) for fx in (0, 1) for fy in (0, 1) for fc in (0, 1)][1:]
        peers = [(x + fx - 2 * x * fx, y + fy - 2 * y * fy, c + fc - 2 * c * fc) for fx, fy, fc in flips]
        sends = [_remote(v_ref, buf.at[me], send_sems, recv_sems, k, peer) for k, peer in enumerate(peers)]
        for cp in sends:
            cp.start()
        for k, (px, py, pc) in enumerate(peers):
            got = buf.at[4 * px + 2 * py + pc]
            _remote(got, got, send_sems, recv_sems, k, (px, py, pc)).wait_recv()
        for cp in sends:
            cp.wait_send()
        acc = buf[0]
        for dev in range(1, N_DEV):
            acc = acc + buf[dev]
        o_ref[...] = acc

    vm = pl.BlockSpec(memory_space=pltpu.VMEM)
    return pl.pallas_call(
        body, name=name, in_specs=[vm], out_specs=vm, out_shape=jax.ShapeDtypeStruct((rows, w), F32),
        scratch_shapes=[pltpu.VMEM((N_DEV, rows, w), F32), pltpu.SemaphoreType.DMA((7,)), pltpu.SemaphoreType.DMA((7,))],
    )(vec)


GATE_COL = 3 * SB_WIDTH + 3 * FOX_WIDTH + FOX_HEADS + MEM_WIDTH
FL_COL = QKV_WIDTH


GROUP_A_COLS = [(0, QKV_WIDTH), (FL_COL + FOX_HEADS, MEM_WIDTH)]
GROUP_B_COLS = [(GATE_COL, MIX_WIDTH), (FL_COL, FOX_HEADS)]


def _group_from_shards(shard_of, cw, spans, pad):
    parts = []
    for lo, width in spans:
        hi = lo + width
        for j in range(N_CHIPS):
            a, b = max(lo, j * cw), min(hi, (j + 1) * cw)
            if a < b:
                parts.append(shard_of(j)[:, a - j * cw:b - j * cw])
    if pad:
        parts.append(jnp.zeros((parts[0].shape[0], pad), parts[0].dtype))
    return jnp.concatenate(parts, axis=1)


def _shard_from_groups(ga, gb, j, cw):
    lo, hi = j * cw, (j + 1) * cw
    placed = []
    for grp, spans in ((ga, GROUP_A_COLS), (gb, GROUP_B_COLS)):
        at = 0
        for first, width in spans:
            a, b = max(lo, first), min(hi, first + width)
            if a < b:
                placed.append((a, grp[:, at + a - first:at + b - first]))
            at += width
    return jnp.concatenate([p for _, p in sorted(placed, key=lambda t: t[0])], axis=1)


def _tile_of(n, cap, unit):
    if n <= cap:
        return n
    best = None
    for t in range(unit, cap + 1, unit):
        if n % t == 0:
            best = t
    assert best is not None, (n, cap, unit)
    return best


def _flat2(a):
    return a.reshape(-1, a.shape[-1])


def _pack_small(parts):
    rows = []
    for p in parts:
        f = p.reshape(-1).astype(F32)
        f = jnp.pad(f, (0, (-f.shape[0]) % LANES))
        rows.append(f.reshape(-1, LANES))
    out = jnp.concatenate(rows, axis=0)
    return jnp.pad(out, ((0, (-out.shape[0]) % 8), (0, 0)))


def _unpack_small(packed, shapes):
    outs, r = [], 0
    for shp in shapes:
        n = 1
        for s_ in shp:
            n *= s_
        nr = -(-n // LANES)
        outs.append(packed[r:r + nr].reshape(-1)[:n].reshape(shp))
        r += nr
    return outs


def kernel(x, mem, norm_w, w_in, b_forget, mem_norm_w, w_mem_kv, out_norm_w, w_out, final_norm_w, loss_target, m_norm_w, m_w_in, m_b_forget, m_mem_norm_w, m_w_mem_kv, m_out_norm_w, m_w_out, m_final_norm_w, v_norm_w, v_w_in, v_b_forget, v_mem_norm_w, v_w_mem_kv, v_out_norm_w, v_w_out, v_final_norm_w):
    xs = x[0]
    mems = mem[0]
    target = loss_target[0]
    s, d = xs.shape
    depth = norm_w.shape[0]
    nb = s // TILE
    ts = _tile_of(s, 256, 8)
    big = (w_in, w_mem_kv, w_out)
    core = lax.axis_index("c")
    chip = 2 * lax.axis_index("x") + lax.axis_index("y")
    cvec = core.astype(jnp.int32).reshape(1)
    mvec = chip.astype(jnp.int32).reshape(1)
    cw = w_in.shape[2]

    def gather_layer(l):
        own = [a[l].astype(BF16) for a in big]
        got = _gather_weights(f"gather_weights{l}", own)
        full = [jnp.where(lax.broadcasted_iota(jnp.int32, g.shape, 0) == chip, o[None], g) for g, o in zip(got, own)]
        shard_of = lambda j: full[0][j]
        wa_l = _group_from_shards(shard_of, cw, GROUP_A_COLS, 0)
        wb_l = _group_from_shards(shard_of, cw, GROUP_B_COLS, LANES - FOX_HEADS)
        return wa_l, wb_l, full[1].reshape(-1, full[1].shape[2]), full[2].reshape(-1, full[2].shape[2])

    layer_w = [gather_layer(l) for l in range(depth)]

    tm = _tile_of(s, 256, 8)
    fl_block = MIX_WIDTH // LANES

    saved = []
    cur = xs
    for l in range(depth):
        wa, wb, wkv, wout = layer_w[l]
        h = _rms_fwd(f"rms_fwd{l}", cur, norm_w[l][None], ts)
        pa = _mm(f"inproj_a{l}", h, wa, "nn", tm, _tile_of(PA, 1664, LANES), BF16)
        pb = _mm(f"inproj_b{l}", h, wb, "nn", tm, PB, F32)
        bpad = jnp.pad(b_forget[l], (0, LANES - FOX_HEADS))[None]
        ccol, crow = _gate_fwd(f"gate_fwd{l}", pb, bpad, fl_block)
        ccol4 = jnp.repeat(ccol[:, :FOX_HEADS].reshape(s, 4, 2).transpose(1, 0, 2), HEAD_DIM, axis=2)
        crow4 = jnp.pad(crow.reshape(nb, 4, 2, TILE).transpose(1, 0, 2, 3), ((0, 0), (0, 0), (0, 6), (0, 0)))
        o_sb = _sb_fwd(f"sb_fwd{l}", pa, 0)
        o_fx, lse_fx = _fox_fwd(f"fox_fwd{l}", pa, 3 * SB_WIDTH, ccol4, crow4)
        mn = _rms_fwd(f"mem_rms{l}", mems, mem_norm_w[l][None], mems.shape[0])
        mkv = _mm(f"mem_kv{l}", mn, wkv, "nn", mems.shape[0], 2 * MEM_WIDTH, BF16)
        o_m, lse_m = _mem_fwd(f"mem_fwd{l}", pa, mkv)
        nxt, y2 = _out_fwd(f"out_fwd{l}", o_sb, o_fx, o_m, pb, out_norm_w[l][None], cur, wout, ts)
        saved.append((cur, h, pa, pb, bpad, ccol4, crow4, o_sb, o_fx, lse_fx, mn, mkv, o_m, lse_m, y2))
        cur = nxt

    loss_v, dx, dxb, g_final = _final_loss("final_loss", cur, final_norm_w[None], target, ts)

    g_norm, g_b, g_memnorm, g_outnorm = [None] * depth, [None] * depth, [None] * depth, [None] * depth
    g_wa, g_wb, g_wkv, g_wout = [None] * depth, [None] * depth, [None] * depth, [None] * depth
    for l in reversed(range(depth)):
        xin, h, pa, pb, bpad, ccol4, crow4, o_sb, o_fx, lse_fx, mn, mkv, o_m, lse_m, y2 = saved[l]
        wa, wb, wkv, wout = layer_w[l]
        dy, dgate, g_outnorm[l] = _out_bwd(f"out_bwd{l}", dxb, o_sb, o_fx, o_m, pb, out_norm_w[l][None], wout, ts)
        g_wout[l] = _mm(f"dw_out{l}", y2, dxb, "tn", _tile_of(MIX_WIDTH, 640, LANES), d, F32)
        dq_sb, dk_sb, dv_sb = _sb_bwd(f"sb_bwd{l}", pa, 0, dy, 0)
        dq_fx, dk_fx, dv_fx, cs4 = _fox_bwd(f"fox_bwd{l}", pa, 3 * SB_WIDTH, ccol4, crow4, o_fx, lse_fx, dy, SB_WIDTH)
        colsum = cs4[:, :, :2, :].transpose(1, 0, 2, 3).reshape(nb, 8, TILE)
        dlogit, g_b[l] = _gate_bwd(f"gate_bwd{l}", pb, bpad, colsum, fl_block)
        dq_m, dk_m, dv_m = _mem_bwd(f"mem_bwd{l}", pa, mkv, o_m, lse_m, dy, SB_WIDTH + FOX_WIDTH)
        dmkv = jnp.concatenate([dk_m, dv_m], axis=1)
        g_wkv[l] = _mm(f"dw_kv{l}", mn, dmkv, "tn", d, 2 * MEM_WIDTH, F32)
        dmn = _mm(f"dmem{l}", dmkv, wkv, "nt", mems.shape[0], d, F32)
        g_memnorm[l] = _rms_wgrad(f"mem_norm_grad{l}", mems, dmn)
        dpa = jnp.concatenate([dq_sb, dk_sb, dv_sb, dq_fx, dk_fx, dv_fx, dq_m], axis=1)
        dpb = jnp.concatenate([dgate, dlogit], axis=1)
        tw = _tile_of(d, 512, LANES)
        g_wa[l] = _mm(f"dw_in_a{l}", h, dpa, "tn", tw, _tile_of(PA, 1664, LANES), F32)
        g_wb[l] = _mm(f"dw_in_b{l}", h, dpb, "tn", tw, PB, F32)
        dh = _mm(f"dh_a{l}", dpa, wa, "nt", tm, d, F32)
        dh = _mm(f"dh_b{l}", dpb, wb, "nt", tm, d, F32, res=dh)
        dx, dxb, g_norm[l] = _rms_bwd(f"rms_bwd{l}", xin, norm_w[l][None], dh, dx, ts)

    g_own = [[None] * depth for _ in big]
    g_other = [[None] * depth for _ in big]
    for l in reversed(range(depth)):
        g4s = [jnp.stack([_shard_from_groups(g_wa[l], g_wb[l], j, cw) for j in range(N_CHIPS)]),
               g_wkv[l].reshape(N_CHIPS, -1, g_wkv[l].shape[1]), g_wout[l].reshape(N_CHIPS, -1, d)]
        tiles = [_tile_of(g.shape[1] // 2, 256, 16) for g in g4s]
        from_sibling = _swap_halves(f"grad_swap_halves{l}", g4s)
        chip_sums = [_add_half(f"grad_add_half{l}_{k}", g, r, cvec, t)
                     for k, (g, r, t) in enumerate(zip(g4s, from_sibling, tiles))]
        from_chips = _scatter_chips(f"grad_scatter_chips{l}", chip_sums)
        halves = [_sum_chips(f"grad_sum_chips{l}_{k}", h_, r, mvec, t)
                  for k, (h_, r, t) in enumerate(zip(chip_sums, from_chips, tiles))]
        others = _swap_reduced(f"grad_swap_reduced{l}", halves)
        for k in range(len(big)):
            g_own[k][l], g_other[k][l] = halves[k], others[k]

    small_w = [norm_w, b_forget, mem_norm_w, out_norm_w, final_norm_w]
    small_m = [m_norm_w, m_b_forget, m_mem_norm_w, m_out_norm_w, m_final_norm_w]
    small_v = [v_norm_w, v_b_forget, v_mem_norm_w, v_out_norm_w, v_final_norm_w]
    small_shapes = [a.shape for a in small_w]
    local_small = [jnp.concatenate(g_norm, axis=0), jnp.stack([g[0, :FOX_HEADS] for g in g_b]),
                   jnp.concatenate(g_memnorm, axis=0), jnp.concatenate(g_outnorm, axis=0), g_final[0]]
    reduced = _allreduce_small("small_allreduce", _pack_small(local_small + [loss_v[0, :1]]))
    small_grads = _unpack_small(reduced, small_shapes)
    n_small_rows = sum(-(-a.size // LANES) for a in small_w)
    loss = reduced[n_small_rows, 0]

    d_s, m_s, v_s = _adamw("adamw_small", _pack_small(small_w), _pack_small(small_grads), _pack_small(small_m),
                           _pack_small(small_v), _pack_small(small_w).shape[0])
    small_delta, small_m2, small_v2 = (_unpack_small(a, small_shapes) for a in (d_s, m_s, v_s))
    big_grads, big_delta, big_m2, big_v2 = [], [], [], []
    for k, (nm, w_, m_, v_) in enumerate(zip(("w_in", "w_mem_kv", "w_out"), big, (m_w_in, m_w_mem_kv, m_w_out),
                                             (v_w_in, v_w_mem_kv, v_w_out))):
        outs = _adamw_sharded(f"adamw_{nm}", w_, m_, v_, g_own[k], g_other[k], cvec,
                              _tile_of(w_.shape[1] // 2, 256, 8))
        for lst, o in zip((big_grads, big_delta, big_m2, big_v2), outs):
            lst.append(o)

    def order(sm, bg):
        return [sm[0], bg[0], sm[1], sm[2], bg[1], sm[3], bg[2], sm[4]]

    return (loss, dx[None], *order(small_grads, big_grads), *order(small_delta, big_delta),
            *order(small_m2, big_m2), *order(small_v2, big_v2))
```

```python
import functools

import jax
import jax.numpy as jnp
from jax import lax
from jax.experimental import pallas as pl
from jax.experimental.pallas import tpu as pltpu

F32 = jnp.float32
BF16 = jnp.bfloat16

HEAD_DIM = 64
SB_WIDTH = 512
FOX_WIDTH = 512
FOX_HEADS = 8
MEM_WIDTH = 256
MIX_WIDTH = SB_WIDTH + FOX_WIDTH + MEM_WIDTH
TOTAL_HEADS = MIX_WIDTH // HEAD_DIM
IN_WIDTH = 3 * SB_WIDTH + 3 * FOX_WIDTH + FOX_HEADS + MEM_WIDTH + MIX_WIDTH
LANES = 128
QKV_WIDTH = 3 * SB_WIDTH + 3 * FOX_WIDTH
PA = QKV_WIDTH + MEM_WIDTH
PB = LANES + MIX_WIDTH
EPS = 1e-6
SCALE = HEAD_DIM ** -0.5
TILE = 256
NEG_INF = float("-inf")
MASKED = -1e30

ADAM_LR = 0.001
ADAM_B1 = 0.9
ADAM_B2 = 0.999
ADAM_EPS = 1e-08
ADAM_WD = 0.01
ADAM_STEP = 10

N_CHIPS = 4
N_DEV = 8
VMEM_LIMIT = 48 * 1024 * 1024
MESH = pl.DeviceIdType.MESH


def _params(*sem):
    return pltpu.CompilerParams(dimension_semantics=tuple(sem), vmem_limit_bytes=VMEM_LIMIT)


def _dot(a, b):
    return jnp.dot(a, b, preferred_element_type=F32)


def _dot_nt(a, b):
    return lax.dot_general(a, b, (((1,), (1,)), ((), ())), preferred_element_type=F32)


def _dot_tn(a, b):
    return lax.dot_general(a, b, (((0,), (0,)), ((), ())), preferred_element_type=F32)


def _split2(x):
    hi = x.astype(BF16)
    lo = (x - hi.astype(F32)).astype(BF16)
    return hi, lo


def _split3(x):
    hi = x.astype(BF16)
    r = x - hi.astype(F32)
    mid = r.astype(BF16)
    lo = (r - mid.astype(F32)).astype(BF16)
    return hi, mid, lo


def _sum_l2(x, u):
    hi, lo = _split2(x)
    return _dot(hi, u) + _dot(lo, u)


def _sum_l3(x, u):
    hi, mid, lo = _split3(x)
    return _dot(hi, u) + _dot(mid, u) + _dot(lo, u)


def _sum_r3(u, x):
    hi, mid, lo = _split3(x)
    return _dot(u, hi) + _dot(u, mid) + _dot(u, lo)


def _softplus(z):
    return jnp.maximum(z, 0.0) + jnp.log1p(jnp.exp(-jnp.abs(z)))


def _tri(n, pred):
    r = lax.broadcasted_iota(jnp.int32, (n, n), 0)
    c = lax.broadcasted_iota(jnp.int32, (n, n), 1)
    return jnp.where(pred(r, c), 1.0, 0.0).astype(BF16)


def _rows(ref, j, n=TILE):
    return pl.ds(pl.multiple_of(j * n, n), n)


def _mm(name, a, b, mode, tm, tn, out_dtype, res=None, a_lead=(), b_lead=()):
    a2, b2 = a.shape[len(a_lead):], b.shape[len(b_lead):]
    if mode == "tn":
        k, m = a2
    else:
        m, k = a2
    n = b2[0] if mode == "nt" else b2[1]
    assert m % tm == 0 and n % tn == 0, (name, m, tm, n, tn)
    na, nb = (None,) * len(a_lead), (None,) * len(b_lead)
    if mode == "tn":
        a_spec = pl.BlockSpec(na + (k, tm), lambda j, i: a_lead + (0, i))
    else:
        a_spec = pl.BlockSpec(na + (tm, k), lambda j, i: a_lead + (i, 0))
    if mode == "nt":
        b_spec = pl.BlockSpec(nb + (tn, k), lambda j, i: b_lead + (j, 0))
    else:
        b_spec = pl.BlockSpec(nb + (k, tn), lambda j, i: b_lead + (0, j))
    o_spec = pl.BlockSpec((tm, tn), lambda j, i: (i, j))
    dot = {"nn": _dot, "nt": _dot_nt, "tn": _dot_tn}[mode]

    def body(a_ref, b_ref, *rest):
        o_ref = rest[-1]
        acc = dot(a_ref[...].astype(BF16), b_ref[...].astype(BF16))
        if res is not None:
            acc = acc + rest[0][...]
        o_ref[...] = acc.astype(o_ref.dtype)

    args, specs = [a, b], [a_spec, b_spec]
    if res is not None:
        args.append(res)
        specs.append(o_spec)
    return pl.pallas_call(
        body, name=name, grid=(n // tn, m // tm), in_specs=specs, out_specs=o_spec,
        out_shape=jax.ShapeDtypeStruct((m, n), out_dtype),
        compiler_params=_params("parallel", "parallel"),
    )(*args)


def _rms_fwd(name, x, g, ts):
    s, d = x.shape

    def body(x_ref, g_ref, o_ref):
        xf = x_ref[...]
        r = lax.rsqrt(jnp.mean(xf * xf, axis=1, keepdims=True) + EPS)
        o_ref[...] = (xf * r * g_ref[...]).astype(BF16)

    return pl.pallas_call(
        body, name=name, grid=(s // ts,),
        in_specs=[pl.BlockSpec((ts, d), lambda i: (i, 0)), pl.BlockSpec((1, d), lambda i: (0, 0))],
        out_specs=pl.BlockSpec((ts, d), lambda i: (i, 0)),
        out_shape=jax.ShapeDtypeStruct((s, d), BF16),
        compiler_params=_params("parallel"),
    )(x, g)


def _rms_bwd(name, x, g, dh, dres, ts):
    s, d = x.shape

    def body(x_ref, g_ref, dh_ref, dres_ref, dx_ref, dxb_ref, dg_ref):
        @pl.when(pl.program_id(0) == 0)
        def _():
            dg_ref[...] = jnp.zeros_like(dg_ref)

        xf = x_ref[...]
        r = lax.rsqrt(jnp.mean(xf * xf, axis=1, keepdims=True) + EPS)
        xh = xf * r
        dhf = dh_ref[...]
        dg_ref[...] += jnp.sum(dhf * xh, axis=0, keepdims=True)
        dxh = dhf * g_ref[...]
        m = jnp.mean(dxh * xh, axis=1, keepdims=True)
        dx = r * (dxh - xh * m) + dres_ref[...]
        dx_ref[...] = dx
        dxb_ref[...] = dx.astype(BF16)

    row = pl.BlockSpec((ts, d), lambda i: (i, 0))
    vec = pl.BlockSpec((1, d), lambda i: (0, 0))
    return pl.pallas_call(
        body, name=name, grid=(s // ts,), in_specs=[row, vec, row, row], out_specs=[row, row, vec],
        out_shape=[jax.ShapeDtypeStruct((s, d), F32), jax.ShapeDtypeStruct((s, d), BF16),
                   jax.ShapeDtypeStruct((1, d), F32)],
        compiler_params=_params("arbitrary"),
    )(x, g, dh, dres)


def _rms_wgrad(name, x, dh):
    m_, d = x.shape

    def body(x_ref, dh_ref, dg_ref):
        xf = x_ref[...]
        r = lax.rsqrt(jnp.mean(xf * xf, axis=1, keepdims=True) + EPS)
        dg_ref[...] = jnp.sum(dh_ref[...] * xf * r, axis=0, keepdims=True)

    return pl.pallas_call(
        body, name=name, out_shape=jax.ShapeDtypeStruct((1, d), F32),
    )(x, dh)


def _final_loss(name, x, g, target, ts):
    s, d = x.shape

    def body(x_ref, g_ref, t_ref, loss_ref, dx_ref, dxb_ref, dg_ref):
        @pl.when(pl.program_id(0) == 0)
        def _():
            dg_ref[...] = jnp.zeros_like(dg_ref)
            loss_ref[...] = jnp.zeros_like(loss_ref)

        xf = x_ref[...]
        gw = g_ref[...]
        r = lax.rsqrt(jnp.mean(xf * xf, axis=1, keepdims=True) + EPS)
        xh = xf * r
        e = xh * gw - t_ref[...]
        part = 0.5 * jnp.sum(jnp.mean(e * e, axis=1, keepdims=True), axis=0, keepdims=True)
        loss_ref[...] += jnp.broadcast_to(part, loss_ref.shape)
        dy = e * (1.0 / d)
        dg_ref[...] += jnp.sum(dy * xh, axis=0, keepdims=True)
        dxh = dy * gw
        m = jnp.mean(dxh * xh, axis=1, keepdims=True)
        dx = r * (dxh - xh * m)
        dx_ref[...] = dx
        dxb_ref[...] = dx.astype(BF16)

    row = pl.BlockSpec((ts, d), lambda i: (i, 0))
    vec = pl.BlockSpec((1, d), lambda i: (0, 0))
    lvec = pl.BlockSpec((1, LANES), lambda i: (0, 0))
    return pl.pallas_call(
        body, name=name, grid=(s // ts,), in_specs=[row, vec, row], out_specs=[lvec, row, row, vec],
        out_shape=[jax.ShapeDtypeStruct((1, LANES), F32), jax.ShapeDtypeStruct((s, d), F32),
                   jax.ShapeDtypeStruct((s, d), BF16), jax.ShapeDtypeStruct((1, d), F32)],
        compiler_params=_params("arbitrary"),
    )(x, g, target)


def _gate_fwd(name, pb, bpad, fl_block):
    s = pb.shape[0]
    nb = s // TILE

    def body(fl_ref, b_ref, ccol_ref, crow_ref, carry):
        @pl.when(pl.program_id(0) == 0)
        def _():
            carry[...] = jnp.zeros_like(carry)

        u = fl_ref[...] + b_ref[...]
        lf = jnp.minimum(u, 0.0) - jnp.log1p(jnp.exp(-jnp.abs(u)))
        lower = _tri(TILE, lambda r, c: c <= r)
        c = _sum_r3(lower, lf) + carry[0:1, :]
        ccol_ref[...] = c
        crow_ref[0] = c.T[0:8, :]
        carry[...] = jnp.broadcast_to(c[TILE - 1:TILE, :], carry.shape)

    return pl.pallas_call(
        body, name=name, grid=(nb,),
        in_specs=[pl.BlockSpec((TILE, LANES), lambda i: (i, fl_block)), pl.BlockSpec((1, LANES), lambda i: (0, 0))],
        out_specs=[pl.BlockSpec((TILE, LANES), lambda i: (i, 0)), pl.BlockSpec((1, 8, TILE), lambda i: (i, 0, 0))],
        out_shape=[jax.ShapeDtypeStruct((s, LANES), F32), jax.ShapeDtypeStruct((nb, 8, TILE), F32)],
        scratch_shapes=[pltpu.VMEM((8, LANES), F32)],
        compiler_params=_params("arbitrary"),
    )(pb, bpad)


def _gate_bwd(name, pb, bpad, colsum, fl_block):
    s = pb.shape[0]
    nb = s // TILE

    def body(fl_ref, b_ref, cs_ref, dl_ref, db_ref, carry):
        @pl.when(pl.program_id(0) == 0)
        def _():
            carry[...] = jnp.zeros_like(carry)
            db_ref[...] = jnp.zeros_like(db_ref)

        upper = _tri(TILE, lambda r, c: r >= c)
        rsum = _sum_l3(cs_ref[0], upper) + carry[:, 0:1]
        carry[...] = jnp.broadcast_to(rsum[:, 0:1], carry.shape)
        full = jnp.concatenate([rsum, jnp.zeros((LANES - 8, TILE), F32)], axis=0)
        dlf = -full.T
        u = fl_ref[...] + b_ref[...]
        dlogit = dlf * (1.0 - jax.nn.sigmoid(u))
        dl_ref[...] = dlogit.astype(BF16)
        db_ref[...] += jnp.sum(dlogit, axis=0, keepdims=True)

    rev = lambda i: (nb - 1 - i, 0)
    return pl.pallas_call(
        body, name=name, grid=(nb,),
        in_specs=[pl.BlockSpec((TILE, LANES), lambda i: (nb - 1 - i, fl_block)),
                  pl.BlockSpec((1, LANES), lambda i: (0, 0)),
                  pl.BlockSpec((1, 8, TILE), lambda i: (nb - 1 - i, 0, 0))],
        out_specs=[pl.BlockSpec((TILE, LANES), rev), pl.BlockSpec((1, LANES), lambda i: (0, 0))],
        out_shape=[jax.ShapeDtypeStruct((s, LANES), BF16), jax.ShapeDtypeStruct((1, LANES), F32)],
        scratch_shapes=[pltpu.VMEM((8, LANES), F32)],
        compiler_params=_params("arbitrary"),
    )(pb, bpad, colsum)


def _head_slices(hh):
    return slice(HEAD_DIM * hh, HEAD_DIM * (hh + 1))


def _scaled_q(q_ref, sl, scale=SCALE):
    return (q_ref[:, sl].astype(F32) * scale).astype(BF16)


def _neg_abs(x):
    sign = jnp.uint32(0x80000000)
    return lax.bitcast_convert_type(lax.bitcast_convert_type(x, jnp.uint32) | sign, F32)


def _sb_tile(qn, kj, carry, strict, u_after, diag):
    nz = _dot_nt(qn, kj)
    lf = jnp.minimum(nz, 0.0) - jnp.log(1.0 + jnp.exp(_neg_abs(nz)))
    lsig = lf - nz
    if diag:
        lf = jnp.where(strict, lf, 0.0)
    sx = _dot(lf.astype(BF16), u_after)
    a = jnp.exp(lsig + sx + carry)
    if diag:
        a = jnp.where(strict, a, 0.0)
    return lsig, a, carry + sx[:, 0:1] + lf[:, 0:1]


def _sb_fwd(name, pa, col0):
    s = pa.shape[0]
    nb = s // TILE
    cb = col0 // LANES

    def body(q_ref, k_ref, v_ref, o_ref, lsig_s, lf_s):
        i = pl.program_id(1)
        r = lax.broadcasted_iota(jnp.int32, (TILE, TILE), 0)
        c = lax.broadcasted_iota(jnp.int32, (TILE, TILE), 1)
        strict = c < r
        u_after = _tri(TILE, lambda rr, cc: rr > cc)
        qs = [_scaled_q(q_ref, _head_slices(hh), -SCALE) for hh in range(2)]

        def neg_z(j):
            kblk = k_ref[_rows(k_ref, j), :]
            return [_dot_nt(qs[hh], kblk[:, _head_slices(hh)]) for hh in range(2)]

        def scores(nzs, slot, diag):
            for hh, nz in enumerate(nzs):
                lf = jnp.minimum(nz, 0.0) - jnp.log(1.0 + jnp.exp(_neg_abs(nz)))
                lsig = lf - nz
                if diag:
                    lf = jnp.where(strict, lf, 0.0)
                    lsig = jnp.where(strict, lsig, MASKED)
                lsig_s[slot, hh] = lsig
                lf_s[slot, hh] = lf.astype(BF16)

        def weigh(j, slot, state):
            vblk = v_ref[_rows(v_ref, j), :]
            new = []
            for hh in range(2):
                carry, acc = state[hh]
                lfb = lf_s[slot, hh]
                sx = _dot(lfb, u_after)
                a = jnp.exp(lsig_s[slot, hh] + sx + carry)
                new.append((carry + sx[:, 0:1] + lfb[:, 0:1].astype(F32),
                            acc + _dot(a.astype(BF16), vblk[:, _head_slices(hh)])))
            return tuple(new)

        def step(t, state):
            state = weigh(i - t + 1, (t - 1) % 2, state)
            scores(neg_z(i - t), t % 2, False)
            return state

        zero = (jnp.zeros((TILE, 1), F32), jnp.zeros((TILE, HEAD_DIM), F32))
        scores(neg_z(i), 0, True)
        state = lax.fori_loop(1, i + 1, step, (zero, zero))
        state = weigh(0, i % 2, state)
        o_ref[...] = jnp.concatenate([state[0][1], state[1][1]], axis=1)

    return pl.pallas_call(
        body, name=name, grid=(4, nb),
        in_specs=[pl.BlockSpec((TILE, LANES), lambda p, i: (i, cb + p)),
                  pl.BlockSpec((s, LANES), lambda p, i: (0, cb + 4 + p)),
                  pl.BlockSpec((s, LANES), lambda p, i: (0, cb + 8 + p))],
        out_specs=pl.BlockSpec((TILE, LANES), lambda p, i: (i, p)),
        out_shape=jax.ShapeDtypeStruct((s, SB_WIDTH), F32),
        scratch_shapes=[pltpu.VMEM((2, 2, TILE, TILE), F32), pltpu.VMEM((2, 2, TILE, TILE), BF16)],
        compiler_params=_params("parallel", "arbitrary"),
    )(pa, pa, pa)


def _sb_bwd(name, pa, col0, dout, dcol0):
    s = pa.shape[0]
    nb = s // TILE
    cb = col0 // LANES
    db = dcol0 // LANES

    def body(q_ref, k_ref, v_ref, do_ref, dq_ref, dk_ref, dv_ref, dk_acc, dv_acc, gpan, span):
        i = pl.program_id(1)

        @pl.when(i == 0)
        def _():
            dk_acc[...] = jnp.zeros_like(dk_acc)
            dv_acc[...] = jnp.zeros_like(dv_acc)

        r = lax.broadcasted_iota(jnp.int32, (TILE, TILE), 0)
        c = lax.broadcasted_iota(jnp.int32, (TILE, TILE), 1)
        strict = c < r
        u_after = _tri(TILE, lambda rr, cc: rr > cc)
        u_before = _tri(TILE, lambda rr, cc: rr < cc)
        qs = [_scaled_q(q_ref, _head_slices(hh), -SCALE) for hh in range(2)]
        dos = [do_ref[:, _head_slices(hh)].astype(BF16) for hh in range(2)]

        def pass1(j, carries, diag):
            kblk = k_ref[_rows(k_ref, j), :]
            vblk = v_ref[_rows(v_ref, j), :]
            new = []
            for hh in range(2):
                sl = _head_slices(hh)
                lsig, a, carry = _sb_tile(qs[hh], kblk[:, sl], carries[hh], strict, u_after, diag)
                gpan[hh, j] = a * _dot_nt(dos[hh], vblk[:, sl])
                span[hh, j] = jnp.exp(lsig)
                dv_acc[hh, _rows(None, j), :] += _dot_tn(a.astype(BF16), dos[hh])
                new.append(carry)
            return tuple(new)

        zero1 = jnp.zeros((TILE, 1), F32)
        carries = pass1(i, (zero1, zero1), True)
        lax.fori_loop(0, i, lambda t, ca: pass1(i - 1 - t, ca, False), carries)

        def pass2(j, state, diag):
            kblk = k_ref[_rows(k_ref, j), :]
            new = []
            for hh in range(2):
                before, ndq = state[hh]
                g = gpan[hh, j]
                sig = span[hh, j]
                pfx = _dot(g.astype(BF16), u_before) + before
                ndz = sig * (pfx + g) - g
                if diag:
                    ndz = jnp.where(strict, ndz, 0.0)
                ndzb = ndz.astype(BF16)
                dk_acc[hh, _rows(None, j), :] += _dot_tn(ndzb, qs[hh])
                new.append((pfx[:, TILE - 1:TILE] + g[:, TILE - 1:TILE], ndq + _dot(ndzb, kblk[:, _head_slices(hh)])))
            return tuple(new)

        zero2 = (zero1, jnp.zeros((TILE, HEAD_DIM), F32))
        state = lax.fori_loop(0, i, lambda j, st: pass2(j, st, False), (zero2, zero2))
        state = pass2(i, state, True)
        dq_ref[...] = jnp.concatenate([state[0][1] * -SCALE, state[1][1] * -SCALE], axis=1).astype(BF16)

        @pl.when(i == nb - 1)
        def _():
            dk_ref[...] = jnp.concatenate([dk_acc[0], dk_acc[1]], axis=1).astype(BF16)
            dv_ref[...] = jnp.concatenate([dv_acc[0], dv_acc[1]], axis=1).astype(BF16)

    qspec = pl.BlockSpec((TILE, LANES), lambda p, i: (i, p))
    kvspec = pl.BlockSpec((s, LANES), lambda p, i: (0, p))
    out = jax.ShapeDtypeStruct((s, SB_WIDTH), BF16)
    return pl.pallas_call(
        body, name=name, grid=(4, nb),
        in_specs=[pl.BlockSpec((TILE, LANES), lambda p, i: (i, cb + p)),
                  pl.BlockSpec((s, LANES), lambda p, i: (0, cb + 4 + p)),
                  pl.BlockSpec((s, LANES), lambda p, i: (0, cb + 8 + p)),
                  pl.BlockSpec((TILE, LANES), lambda p, i: (i, db + p))],
        out_specs=[qspec, kvspec, kvspec], out_shape=[out, out, out],
        scratch_shapes=[pltpu.VMEM((2, s, HEAD_DIM), F32), pltpu.VMEM((2, s, HEAD_DIM), F32),
                        pltpu.VMEM((2, nb, TILE, TILE), F32), pltpu.VMEM((2, nb, TILE, TILE), F32)],
        compiler_params=_params("arbitrary", "arbitrary"),
    )(pa, pa, pa, dout)


def _fox_scores(q, kj, cq, crj, causal, diag):
    sc = _dot_nt(q, kj) + (cq - crj)
    if diag:
        sc = jnp.where(causal, sc, NEG_INF)
    return sc


def _fox_fwd(name, pa, col0, ccol4, crow4):
    s = pa.shape[0]
    nb = s // TILE
    cb = col0 // LANES

    def body(q_ref, k_ref, v_ref, cc_ref, cr_ref, o_ref, lse_ref):
        i = pl.program_id(1)
        r = lax.broadcasted_iota(jnp.int32, (TILE, TILE), 0)
        c = lax.broadcasted_iota(jnp.int32, (TILE, TILE), 1)
        causal = c <= r
        qs = [_scaled_q(q_ref, _head_slices(hh)) for hh in range(2)]
        cqs = [cc_ref[:, HEAD_DIM * hh:HEAD_DIM * hh + 1] for hh in range(2)]

        def tile(j, state, diag):
            kblk = k_ref[_rows(k_ref, j), :]
            vblk = v_ref[_rows(v_ref, j), :]
            new = []
            for hh in range(2):
                sl = _head_slices(hh)
                m, l, acc = state[hh]
                sc = _fox_scores(qs[hh], kblk[:, sl], cqs[hh], cr_ref[j, hh:hh + 1, :], causal, diag)
                m2 = jnp.maximum(m, jnp.max(sc, axis=1, keepdims=True))
                alpha = jnp.exp(m - m2)
                p = jnp.exp(sc - m2)
                new.append((m2, l * alpha + jnp.sum(p, axis=1, keepdims=True),
                            acc * alpha + _dot(p.astype(BF16), vblk[:, sl])))
            return tuple(new)

        zero = (jnp.full((TILE, 1), NEG_INF, F32), jnp.zeros((TILE, 1), F32), jnp.zeros((TILE, HEAD_DIM), F32))
        state = tile(i, (zero, zero), True)
        state = lax.fori_loop(0, i, lambda t, st: tile(i - 1 - t, st, False), state)
        o_ref[...] = jnp.concatenate([state[hh][2] / state[hh][1] for hh in range(2)], axis=1)
        lse_ref[...] = jnp.concatenate(
            [jnp.broadcast_to(state[hh][0] + jnp.log(state[hh][1]), (TILE, HEAD_DIM)) for hh in range(2)], axis=1)

    return pl.pallas_call(
        body, name=name, grid=(4, nb),
        in_specs=[pl.BlockSpec((TILE, LANES), lambda p, i: (i, cb + p)),
                  pl.BlockSpec((s, LANES), lambda p, i: (0, cb + 4 + p)),
                  pl.BlockSpec((s, LANES), lambda p, i: (0, cb + 8 + p)),
                  pl.BlockSpec((None, TILE, LANES), lambda p, i: (p, i, 0)),
                  pl.BlockSpec((None, nb, 8, TILE), lambda p, i: (p, 0, 0, 0))],
        out_specs=[pl.BlockSpec((TILE, LANES), lambda p, i: (i, p)),
                   pl.BlockSpec((None, TILE, LANES), lambda p, i: (p, i, 0))],
        out_shape=[jax.ShapeDtypeStruct((s, FOX_WIDTH), F32), jax.ShapeDtypeStruct((4, s, LANES), F32)],
        compiler_params=_params("parallel", "arbitrary"),
    )(pa, pa, pa, ccol4, crow4)


def _fox_bwd(name, pa, col0, ccol4, crow4, out, lse, dout, dcol0):
    s = pa.shape[0]
    nb = s // TILE
    cb = col0 // LANES
    db = dcol0 // LANES

    def body(q_ref, k_ref, v_ref, cc_ref, cr_ref, o_ref, lse_ref, do_ref,
             dq_ref, dk_ref, dv_ref, cs_ref, dk_acc, dv_acc):
        i = pl.program_id(1)

        @pl.when(i == 0)
        def _():
            dk_acc[...] = jnp.zeros_like(dk_acc)
            dv_acc[...] = jnp.zeros_like(dv_acc)
            cs_ref[...] = jnp.zeros_like(cs_ref)

        r = lax.broadcasted_iota(jnp.int32, (TILE, TILE), 0)
        c = lax.broadcasted_iota(jnp.int32, (TILE, TILE), 1)
        causal = c <= r
        qs = [_scaled_q(q_ref, _head_slices(hh)) for hh in range(2)]
        cqs = [cc_ref[:, HEAD_DIM * hh:HEAD_DIM * hh + 1] for hh in range(2)]
        lses = [lse_ref[:, HEAD_DIM * hh:HEAD_DIM * hh + 1] for hh in range(2)]
        dofs = [do_ref[:, _head_slices(hh)] for hh in range(2)]
        dos = [d_.astype(BF16) for d_ in dofs]
        deltas = [jnp.sum(dofs[hh] * o_ref[:, _head_slices(hh)], axis=1, keepdims=True) for hh in range(2)]

        def tile(j, state, diag):
            kblk = k_ref[_rows(k_ref, j), :]
            vblk = v_ref[_rows(v_ref, j), :]
            new = []
            for hh in range(2):
                sl = _head_slices(hh)
                dq, rs = state[hh]
                sc = _fox_scores(qs[hh], kblk[:, sl], cqs[hh], cr_ref[j, hh:hh + 1, :], causal, diag)
                p = jnp.exp(sc - lses[hh])
                ds = p * (_dot_nt(dos[hh], vblk[:, sl]) - deltas[hh])
                dsb = ds.astype(BF16)
                dv_acc[hh, _rows(None, j), :] += _dot_tn(p.astype(BF16), dos[hh])
                dk_acc[hh, _rows(None, j), :] += _dot_tn(dsb, qs[hh])
                cs_ref[j, hh:hh + 1, :] += jnp.sum(ds, axis=0, keepdims=True)
                new.append((dq + _dot(dsb, kblk[:, sl]), rs + jnp.sum(ds, axis=1, keepdims=True)))
            return tuple(new)

        zero = (jnp.zeros((TILE, HEAD_DIM), F32), jnp.zeros((TILE, 1), F32))
        state = tile(i, (zero, zero), True)
        state = lax.fori_loop(0, i, lambda t, st: tile(i - 1 - t, st, False), state)
        for hh in range(2):
            cs_ref[i, hh:hh + 1, :] -= jnp.broadcast_to(state[hh][1], (TILE, LANES)).T[0:1, :]
        dq_ref[...] = jnp.concatenate([state[0][0] * SCALE, state[1][0] * SCALE], axis=1).astype(BF16)

        @pl.when(i == nb - 1)
        def _():
            dk_ref[...] = jnp.concatenate([dk_acc[0], dk_acc[1]], axis=1).astype(BF16)
            dv_ref[...] = jnp.concatenate([dv_acc[0], dv_acc[1]], axis=1).astype(BF16)

    qspec = pl.BlockSpec((TILE, LANES), lambda p, i: (i, p))
    kvspec = pl.BlockSpec((s, LANES), lambda p, i: (0, p))
    o3 = jax.ShapeDtypeStruct((s, FOX_WIDTH), BF16)
    return pl.pallas_call(
        body, name=name, grid=(4, nb),
        in_specs=[pl.BlockSpec((TILE, LANES), lambda p, i: (i, cb + p)),
                  pl.BlockSpec((s, LANES), lambda p, i: (0, cb + 4 + p)),
                  pl.BlockSpec((s, LANES), lambda p, i: (0, cb + 8 + p)),
                  pl.BlockSpec((None, TILE, LANES), lambda p, i: (p, i, 0)),
                  pl.BlockSpec((None, nb, 8, TILE), lambda p, i: (p, 0, 0, 0)),
                  qspec,
                  pl.BlockSpec((None, TILE, LANES), lambda p, i: (p, i, 0)),
                  pl.BlockSpec((TILE, LANES), lambda p, i: (i, db + p))],
        out_specs=[qspec, kvspec, kvspec, pl.BlockSpec((None, nb, 8, TILE), lambda p, i: (p, 0, 0, 0))],
        out_shape=[o3, o3, o3, jax.ShapeDtypeStruct((4, nb, 8, TILE), F32)],
        scratch_shapes=[pltpu.VMEM((2, s, HEAD_DIM), F32), pltpu.VMEM((2, s, HEAD_DIM), F32)],
        compiler_params=_params("arbitrary", "arbitrary"),
    )(pa, pa, pa, ccol4, crow4, out, lse, dout)


def _mem_fwd(name, pa, mkv):
    s = pa.shape[0]
    ml = mkv.shape[0]
    nb = s // TILE
    cb = QKV_WIDTH // LANES

    def body(q_ref, k_ref, v_ref, o_ref, lse_ref):
        outs, lses = [], []
        for hh in range(2):
            sl = _head_slices(hh)
            sc = _dot_nt(_scaled_q(q_ref, sl), k_ref[:, sl])
            m = jnp.max(sc, axis=1, keepdims=True)
            p = jnp.exp(sc - m)
            l = jnp.sum(p, axis=1, keepdims=True)
            outs.append(_dot(p.astype(BF16), v_ref[:, sl]) / l)
            lses.append(jnp.broadcast_to(m + jnp.log(l), (TILE, HEAD_DIM)))
        o_ref[...] = jnp.concatenate(outs, axis=1)
        lse_ref[...] = jnp.concatenate(lses, axis=1)

    return pl.pallas_call(
        body, name=name, grid=(2, nb),
        in_specs=[pl.BlockSpec((TILE, LANES), lambda p, i: (i, cb + p)),
                  pl.BlockSpec((ml, LANES), lambda p, i: (0, p)),
                  pl.BlockSpec((ml, LANES), lambda p, i: (0, 2 + p))],
        out_specs=[pl.BlockSpec((TILE, LANES), lambda p, i: (i, p)),
                   pl.BlockSpec((None, TILE, LANES), lambda p, i: (p, i, 0))],
        out_shape=[jax.ShapeDtypeStruct((s, MEM_WIDTH), F32), jax.ShapeDtypeStruct((2, s, LANES), F32)],
        compiler_params=_params("parallel", "parallel"),
    )(pa, mkv, mkv)


def _mem_bwd(name, pa, mkv, out, lse, dout, dcol0):
    s = pa.shape[0]
    ml = mkv.shape[0]
    nb = s // TILE
    cb = QKV_WIDTH // LANES
    db = dcol0 // LANES

    def body(q_ref, k_ref, v_ref, o_ref, lse_ref, do_ref, dq_ref, dk_ref, dv_ref, dk_acc, dv_acc):
        i = pl.program_id(1)

        @pl.when(i == 0)
        def _():
            dk_acc[...] = jnp.zeros_like(dk_acc)
            dv_acc[...] = jnp.zeros_like(dv_acc)

        dqs = []
        for hh in range(2):
            sl = _head_slices(hh)
            q = _scaled_q(q_ref, sl)
            kh = k_ref[:, sl]
            dof = do_ref[:, sl]
            do = dof.astype(BF16)
            delta = jnp.sum(dof * o_ref[:, sl], axis=1, keepdims=True)
            p = jnp.exp(_dot_nt(q, kh) - lse_ref[:, HEAD_DIM * hh:HEAD_DIM * hh + 1])
            ds = (p * (_dot_nt(do, v_ref[:, sl]) - delta)).astype(BF16)
            dv_acc[hh] += _dot_tn(p.astype(BF16), do)
            dk_acc[hh] += _dot_tn(ds, q)
            dqs.append(_dot(ds, kh) * SCALE)
        dq_ref[...] = jnp.concatenate(dqs, axis=1).astype(BF16)

        @pl.when(i == nb - 1)
        def _():
            dk_ref[...] = jnp.concatenate([dk_acc[0], dk_acc[1]], axis=1).astype(BF16)
            dv_ref[...] = jnp.concatenate([dv_acc[0], dv_acc[1]], axis=1).astype(BF16)

    qspec = pl.BlockSpec((TILE, LANES), lambda p, i: (i, p))
    kvspec = pl.BlockSpec((ml, LANES), lambda p, i: (0, p))
    okv = jax.ShapeDtypeStruct((ml, MEM_WIDTH), BF16)
    return pl.pallas_call(
        body, name=name, grid=(2, nb),
        in_specs=[pl.BlockSpec((TILE, LANES), lambda p, i: (i, cb + p)),
                  pl.BlockSpec((ml, LANES), lambda p, i: (0, p)),
                  pl.BlockSpec((ml, LANES), lambda p, i: (0, 2 + p)),
                  qspec,
                  pl.BlockSpec((None, TILE, LANES), lambda p, i: (p, i, 0)),
                  pl.BlockSpec((TILE, LANES), lambda p, i: (i, db + p))],
        out_specs=[qspec, kvspec, kvspec],
        out_shape=[jax.ShapeDtypeStruct((s, MEM_WIDTH), BF16), okv, okv],
        scratch_shapes=[pltpu.VMEM((2, ml, HEAD_DIM), F32), pltpu.VMEM((2, ml, HEAD_DIM), F32)],
        compiler_params=_params("arbitrary", "arbitrary"),
    )(pa, mkv, mkv, out, lse, dout)


def _head_maps():
    col = jnp.arange(MIX_WIDTH)[:, None] // HEAD_DIM
    g = (col == jnp.arange(LANES)[None, :]).astype(BF16)
    return g, g.T


def _normed_heads(osb_ref, ofx_ref, om_ref, g_ref, gt_ref):
    y = jnp.concatenate([osb_ref[...], ofx_ref[...], om_ref[...]], axis=1)
    msq = _sum_l3(y * y, g_ref[...]) * (1.0 / HEAD_DIM)
    rf = _sum_l3(lax.rsqrt(msq + EPS), gt_ref[...])
    return y * rf, rf


def _out_fwd(name, o_sb, o_fx, o_m, pb, ow, x, w_out, ts):
    s, d = x.shape
    g, gt = _head_maps()

    def body(osb_ref, ofx_ref, om_ref, gate_ref, ow_ref, x_ref, w_ref, g_ref, gt_ref, xo_ref, y2_ref):
        yh, _ = _normed_heads(osb_ref, ofx_ref, om_ref, g_ref, gt_ref)
        gate = gate_ref[...]
        y2 = (yh * ow_ref[...] * (gate * jax.nn.sigmoid(gate))).astype(BF16)
        y2_ref[...] = y2
        xo_ref[...] = x_ref[...] + _dot(y2, w_ref[...])

    return pl.pallas_call(
        body, name=name, grid=(s // ts,),
        in_specs=[_row_spec(ts, SB_WIDTH), _row_spec(ts, FOX_WIDTH), _row_spec(ts, MEM_WIDTH),
                  _row_spec(ts, MIX_WIDTH), _const_spec((1, MIX_WIDTH)), _row_spec(ts, d),
                  _const_spec((MIX_WIDTH, d)),
                  _const_spec((MIX_WIDTH, LANES)), _const_spec((LANES, MIX_WIDTH))],
        out_specs=[_row_spec(ts, d), _row_spec(ts, MIX_WIDTH)],
        out_shape=[jax.ShapeDtypeStruct((s, d), F32), jax.ShapeDtypeStruct((s, MIX_WIDTH), BF16)],
        compiler_params=_params("parallel"),
    )(o_sb, o_fx, o_m, pb, ow, x, w_out, g, gt)


def _row_spec(ts, w):
    return pl.BlockSpec((ts, w), lambda i: (i, 0))


def _const_spec(shape):
    return pl.BlockSpec(shape, lambda i: (0,) * len(shape))


def _out_bwd(name, dxb, o_sb, o_fx, o_m, pb, ow, w_out, ts):
    s, d = dxb.shape
    g, gt = _head_maps()

    def body(dx_ref, osb_ref, ofx_ref, om_ref, gate_ref, ow_ref, w_ref, g_ref, gt_ref, dy_ref, dgate_ref, dow_ref):
        @pl.when(pl.program_id(0) == 0)
        def _():
            dow_ref[...] = jnp.zeros_like(dow_ref)

        dy2 = _dot_nt(dx_ref[...], w_ref[...])
        yh, rf = _normed_heads(osb_ref, ofx_ref, om_ref, g_ref, gt_ref)
        gate = gate_ref[...]
        sig = jax.nn.sigmoid(gate)
        ow_v = ow_ref[...]
        dgate_ref[...] = (dy2 * (yh * ow_v) * (sig * (1.0 + gate * (1.0 - sig)))).astype(BF16)
        dn = dy2 * (gate * sig)
        dow_ref[...] += jnp.sum(dn * yh, axis=0, keepdims=True)
        dyh = dn * ow_v
        t = _sum_l3(dyh * yh, g_ref[...]) * (1.0 / HEAD_DIM)
        dy_ref[...] = rf * (dyh - yh * _sum_l3(t, gt_ref[...]))

    return pl.pallas_call(
        body, name=name, grid=(s // ts,),
        in_specs=[_row_spec(ts, d), _row_spec(ts, SB_WIDTH), _row_spec(ts, FOX_WIDTH), _row_spec(ts, MEM_WIDTH),
                  _row_spec(ts, MIX_WIDTH), _const_spec((1, MIX_WIDTH)),
                  _const_spec((MIX_WIDTH, d)),
                  _const_spec((MIX_WIDTH, LANES)), _const_spec((LANES, MIX_WIDTH))],
        out_specs=[_row_spec(ts, MIX_WIDTH), _row_spec(ts, MIX_WIDTH), _const_spec((1, MIX_WIDTH))],
        out_shape=[jax.ShapeDtypeStruct((s, MIX_WIDTH), F32), jax.ShapeDtypeStruct((s, MIX_WIDTH), BF16),
                   jax.ShapeDtypeStruct((1, MIX_WIDTH), F32)],
        compiler_params=_params("arbitrary"),
    )(dxb, o_sb, o_fx, o_m, pb, ow, w_out, g, gt)


def _adamw(name, w, g, m, v, tr):
    rows, cols = w.shape

    def body(w_ref, g_ref, m_ref, v_ref, d_ref, m2_ref, v2_ref):
        gv = g_ref[...]
        m2 = ADAM_B1 * m_ref[...] + (1.0 - ADAM_B1) * gv
        v2 = ADAM_B2 * v_ref[...] + (1.0 - ADAM_B2) * (gv * gv)
        m_hat = m2 / (1.0 - ADAM_B1 ** ADAM_STEP)
        v_hat = v2 / (1.0 - ADAM_B2 ** ADAM_STEP)
        d_ref[...] = -ADAM_LR * (m_hat / (jnp.sqrt(v_hat) + ADAM_EPS) + ADAM_WD * w_ref[...])
        m2_ref[...] = m2
        v2_ref[...] = v2

    spec = _row_spec(tr, cols)
    shp = jax.ShapeDtypeStruct((rows, cols), F32)
    return pl.pallas_call(
        body, name=name, grid=(rows // tr,), in_specs=[spec] * 4, out_specs=[spec] * 3, out_shape=[shp] * 3,
        compiler_params=_params("parallel"),
    )(w, g, m, v)


def _adamw_sharded(name, w, m, v, g_own, g_other, cvec, tr):
    depth, rows, cols = w.shape
    nt = rows // 2 // tr

    def body(c_ref, w_ref, m_ref, v_ref, *rest):
        g_refs, (g_ref, d_ref, m2_ref, v2_ref) = rest[:2 * depth], rest[2 * depth:]
        layer, mine = pl.program_id(0), pl.program_id(1) == c_ref[0]
        gv = None
        for lt in range(depth):
            cand = jnp.where(mine, g_refs[lt][...], g_refs[depth + lt][...])
            gv = cand if gv is None else jnp.where(layer == lt, cand, gv)
        m2 = ADAM_B1 * m_ref[...] + (1.0 - ADAM_B1) * gv
        v2 = ADAM_B2 * v_ref[...] + (1.0 - ADAM_B2) * (gv * gv)
        m_hat = m2 / (1.0 - ADAM_B1 ** ADAM_STEP)
        v_hat = v2 / (1.0 - ADAM_B2 ** ADAM_STEP)
        g_ref[...] = gv
        d_ref[...] = -ADAM_LR * (m_hat / (jnp.sqrt(v_hat) + ADAM_EPS) + ADAM_WD * w_ref[...])
        m2_ref[...] = m2
        v2_ref[...] = v2

    def g_map(lt, own):
        def index(l, hf, i, c_ref):
            use = jnp.logical_and(l == lt, (hf == c_ref[0]) == own)
            return jnp.where(use, i, 0), 0
        return index

    full = pl.BlockSpec((None, tr, cols), lambda l, hf, i, c_ref: (l, hf * nt + i, 0))
    g_specs = [pl.BlockSpec((tr, cols), g_map(lt, own)) for own in (True, False) for lt in range(depth)]
    shp = jax.ShapeDtypeStruct((depth, rows, cols), F32)
    return pl.pallas_call(
        body, name=name,
        grid_spec=pltpu.PrefetchScalarGridSpec(
            num_scalar_prefetch=1, grid=(depth, 2, nt), in_specs=[full] * 3 + g_specs, out_specs=[full] * 4),
        out_shape=[shp] * 4,
        compiler_params=_params("arbitrary", "arbitrary", "arbitrary"),
    )(cvec, w, m, v, *g_own, *g_other)


HBM_SPEC = pl.BlockSpec(memory_space=pltpu.HBM)


def _place():
    x, y, c = lax.axis_index("x"), lax.axis_index("y"), lax.axis_index("c")
    chips = [(1 - x, y), (x, 1 - y), (1 - x, 1 - y)]
    return x, y, c, chips


def _remote(src, dst, send_sems, recv_sems, k, to):
    return pltpu.make_async_remote_copy(src_ref=src, dst_ref=dst, send_sem=send_sems.at[k], recv_sem=recv_sems.at[k],
                                        device_id=to, device_id_type=MESH)


def _half_rows(n_rows, cc):
    rh = n_rows // 2
    return pl.ds(pl.multiple_of(cc * rh, 16), rh)


def _dma_sems(n):
    return [pltpu.SemaphoreType.DMA((n,)), pltpu.SemaphoreType.DMA((n,))]


def _gather_weights(name, shards):
    n = len(shards)

    def body(*refs):
        in_refs, out_refs, (send_sems, recv_sems) = refs[:n], refs[n:2 * n], refs[2 * n:]
        x, y, c, chips = _place()
        mine = 2 * x + y
        sibling = (x, y, 1 - c)
        first, passed = [], []
        for a, (in_ref, out_ref) in enumerate(zip(in_refs, out_refs)):
            mh = _half_rows(in_ref.shape[0], c)
            for j, (cx, cy) in enumerate(chips):
                first.append(_remote(in_ref.at[mh], out_ref.at[mine, mh], send_sems, recv_sems, 6 * a + j, (cx, cy, c)))
        for cp in first:
            cp.start()
        for j, (cx, cy) in enumerate(chips):
            for a, out_ref in enumerate(out_refs):
                landed = out_ref.at[2 * cx + cy, _half_rows(out_ref.shape[1], c)]
                _remote(landed, landed, send_sems, recv_sems, 6 * a + j, sibling).wait_recv()
                cp = _remote(landed, landed, send_sems, recv_sems, 6 * a + 3 + j, sibling)
                cp.start()
                passed.append(cp)
        for j, (cx, cy) in enumerate(chips):
            for a, out_ref in enumerate(out_refs):
                other = out_ref.at[2 * cx + cy, _half_rows(out_ref.shape[1], 1 - c)]
                _remote(other, other, send_sems, recv_sems, 6 * a + 3 + j, sibling).wait_recv()
        for cp in first + passed:
            cp.wait_send()

    return pl.pallas_call(
        body, name=name, in_specs=[HBM_SPEC] * n, out_specs=[HBM_SPEC] * n,
        out_shape=[jax.ShapeDtypeStruct((N_CHIPS,) + s_.shape, s_.dtype) for s_ in shards],
        scratch_shapes=_dma_sems(6 * n),
    )(*shards)


def _swap_halves(name, g4s):
    n = len(g4s)

    def body(*refs):
        in_refs, out_refs, (send_sems, recv_sems) = refs[:n], refs[n:2 * n], refs[2 * n:]
        x, y, c, _ = _place()
        cps = [_remote(in_ref.at[:, _half_rows(in_ref.shape[1], 1 - c), :], out_ref, send_sems, recv_sems, a, (x, y, 1 - c))
               for a, (in_ref, out_ref) in enumerate(zip(in_refs, out_refs))]
        for cp in cps:
            cp.start()
        for cp in cps:
            cp.wait()

    return pl.pallas_call(
        body, name=name, in_specs=[HBM_SPEC] * n, out_specs=[HBM_SPEC] * n,
        out_shape=[jax.ShapeDtypeStruct((g.shape[0], g.shape[1] // 2, g.shape[2]), g.dtype) for g in g4s],
        scratch_shapes=_dma_sems(n),
    )(*g4s)


def _add_half(name, g4, r1, cvec, tr):
    n, r, w = g4.shape
    rh = r // 2
    nblk = rh // tr

    def body(c_ref, a_ref, b_ref, o_ref):
        o_ref[...] = (a_ref[...] + b_ref[...]).astype(BF16)

    return pl.pallas_call(
        body, name=name,
        grid_spec=pltpu.PrefetchScalarGridSpec(
            num_scalar_prefetch=1, grid=(n, nblk),
            in_specs=[pl.BlockSpec((None, tr, w), lambda k, i, c_ref: (k, c_ref[0] * nblk + i, 0)),
                      pl.BlockSpec((None, tr, w), lambda k, i, c_ref: (k, i, 0))],
            out_specs=pl.BlockSpec((None, tr, w), lambda k, i, c_ref: (k, i, 0))),
        out_shape=jax.ShapeDtypeStruct((n, rh, w), BF16),
        compiler_params=_params("parallel", "parallel"),
    )(cvec, g4, r1)


def _scatter_chips(name, h4s):
    n = len(h4s)

    def body(*refs):
        in_refs, out_refs, (send_sems, recv_sems) = refs[:n], refs[n:2 * n], refs[2 * n:]
        x, y, c, chips = _place()
        sends = [_remote(in_ref.at[2 * cx + cy], out_ref.at[j], send_sems, recv_sems, 3 * a + j, (cx, cy, c))
                 for a, (in_ref, out_ref) in enumerate(zip(in_refs, out_refs)) for j, (cx, cy) in enumerate(chips)]
        for cp in sends:
            cp.start()
        for a, out_ref in enumerate(out_refs):
            for j, (cx, cy) in enumerate(chips):
                got = out_ref.at[j]
                _remote(got, got, send_sems, recv_sems, 3 * a + j, (cx, cy, c)).wait_recv()
        for cp in sends:
            cp.wait_send()

    return pl.pallas_call(
        body, name=name, in_specs=[HBM_SPEC] * n, out_specs=[HBM_SPEC] * n,
        out_shape=[jax.ShapeDtypeStruct((3,) + h.shape[1:], h.dtype) for h in h4s],
        scratch_shapes=_dma_sems(3 * n),
    )(*h4s)


def _sum_chips(name, h4, r3, mvec, tr):
    _, rh, w = h4.shape

    def body(m_ref, a_ref, b_ref, c_ref, d_ref, o_ref):
        o_ref[...] = ((a_ref[...].astype(F32) + b_ref[...].astype(F32)) + c_ref[...].astype(F32)) + d_ref[...].astype(F32)

    specs = [pl.BlockSpec((None, tr, w), lambda i, m_ref: (m_ref[0], i, 0))]
    specs += [pl.BlockSpec((None, tr, w), functools.partial(lambda k, i, m_ref: (k, i, 0), k)) for k in range(3)]
    return pl.pallas_call(
        body, name=name,
        grid_spec=pltpu.PrefetchScalarGridSpec(
            num_scalar_prefetch=1, grid=(rh // tr,), in_specs=specs,
            out_specs=pl.BlockSpec((tr, w), lambda i, m_ref: (i, 0))),
        out_shape=jax.ShapeDtypeStruct((rh, w), F32),
        compiler_params=_params("parallel"),
    )(mvec, h4, r3, r3, r3)


def _swap_reduced(name, ghs):
    n = len(ghs)

    def body(*refs):
        in_refs, out_refs, (send_sems, recv_sems) = refs[:n], refs[n:2 * n], refs[2 * n:]
        x, y, c, _ = _place()
        cps = [_remote(in_ref, out_ref, send_sems, recv_sems, a, (x, y, 1 - c))
               for a, (in_ref, out_ref) in enumerate(zip(in_refs, out_refs))]
        for cp in cps:
            cp.start()
        for cp in cps:
            cp.wait()

    return pl.pallas_call(
        body, name=name, in_specs=[HBM_SPEC] * n, out_specs=[HBM_SPEC] * n,
        out_shape=[jax.ShapeDtypeStruct(g.shape, g.dtype) for g in ghs],
        scratch_shapes=_dma_sems(n),
    )(*ghs)


def _allreduce_small(name, vec):
    rows, w = vec.shape

    def body(v_ref, o_ref, buf, send_sems, recv_sems):
        x, y, c, _ = _place()
        me = 4 * x + 2 * y + c
        buf[me] = v_ref[...]
        flips = [(fx, fy, fc) for fx in (0, 1) for fy in (0, 1) for fc in (0, 1)][1:]
        peers = [(x + fx - 2 * x * fx, y + fy - 2 * y * fy, c + fc - 2 * c * fc) for fx, fy, fc in flips]
        sends = [_remote(v_ref, buf.at[me], send_sems, recv_sems, k, peer) for k, peer in enumerate(peers)]
        for cp in sends:
            cp.start()
        for k, (px, py, pc) in enumerate(peers):
            got = buf.at[4 * px + 2 * py + pc]
            _remote(got, got, send_sems, recv_sems, k, (px, py, pc)).wait_recv()
        for cp in sends:
            cp.wait_send()
        acc = buf[0]
        for dev in range(1, N_DEV):
            acc = acc + buf[dev]
        o_ref[...] = acc

    vm = pl.BlockSpec(memory_space=pltpu.VMEM)
    return pl.pallas_call(
        body, name=name, in_specs=[vm], out_specs=vm, out_shape=jax.ShapeDtypeStruct((rows, w), F32),
        scratch_shapes=[pltpu.VMEM((N_DEV, rows, w), F32), pltpu.SemaphoreType.DMA((7,)), pltpu.SemaphoreType.DMA((7,))],
    )(vec)


GATE_COL = 3 * SB_WIDTH + 3 * FOX_WIDTH + FOX_HEADS + MEM_WIDTH
FL_COL = QKV_WIDTH


GROUP_A_COLS = [(0, QKV_WIDTH), (FL_COL + FOX_HEADS, MEM_WIDTH)]
GROUP_B_COLS = [(GATE_COL, MIX_WIDTH), (FL_COL, FOX_HEADS)]


def _group_from_shards(shard_of, cw, spans, pad):
    parts = []
    for lo, width in spans:
        hi = lo + width
        for j in range(N_CHIPS):
            a, b = max(lo, j * cw), min(hi, (j + 1) * cw)
            if a < b:
                parts.append(shard_of(j)[:, a - j * cw:b - j * cw])
    if pad:
        parts.append(jnp.zeros((parts[0].shape[0], pad), parts[0].dtype))
    return jnp.concatenate(parts, axis=1)


def _shard_from_groups(ga, gb, j, cw):
    lo, hi = j * cw, (j + 1) * cw
    placed = []
    for grp, spans in ((ga, GROUP_A_COLS), (gb, GROUP_B_COLS)):
        at = 0
        for first, width in spans:
            a, b = max(lo, first), min(hi, first + width)
            if a < b:
                placed.append((a, grp[:, at + a - first:at + b - first]))
            at += width
    return jnp.concatenate([p for _, p in sorted(placed, key=lambda t: t[0])], axis=1)


def _tile_of(n, cap, unit):
    if n <= cap:
        return n
    best = None
    for t in range(unit, cap + 1, unit):
        if n % t == 0:
            best = t
    assert best is not None, (n, cap, unit)
    return best


def _flat2(a):
    return a.reshape(-1, a.shape[-1])


def _pack_small(parts):
    rows = []
    for p in parts:
        f = p.reshape(-1).astype(F32)
        f = jnp.pad(f, (0, (-f.shape[0]) % LANES))
        rows.append(f.reshape(-1, LANES))
    out = jnp.concatenate(rows, axis=0)
    return jnp.pad(out, ((0, (-out.shape[0]) % 8), (0, 0)))


def _unpack_small(packed, shapes):
    outs, r = [], 0
    for shp in shapes:
        n = 1
        for s_ in shp:
            n *= s_
        nr = -(-n // LANES)
        outs.append(packed[r:r + nr].reshape(-1)[:n].reshape(shp))
        r += nr
    return outs


def kernel(x, mem, norm_w, w_in, b_forget, mem_norm_w, w_mem_kv, out_norm_w, w_out, final_norm_w, loss_target, m_norm_w, m_w_in, m_b_forget, m_mem_norm_w, m_w_mem_kv, m_out_norm_w, m_w_out, m_final_norm_w, v_norm_w, v_w_in, v_b_forget, v_mem_norm_w, v_w_mem_kv, v_out_norm_w, v_w_out, v_final_norm_w):
    xs = x[0]
    mems = mem[0]
    target = loss_target[0]
    s, d = xs.shape
    depth = norm_w.shape[0]
    nb = s // TILE
    ts = _tile_of(s, 256, 8)
    big = (w_in, w_mem_kv, w_out)
    core = lax.axis_index("c")
    chip = 2 * lax.axis_index("x") + lax.axis_index("y")
    cvec = core.astype(jnp.int32).reshape(1)
    mvec = chip.astype(jnp.int32).reshape(1)
    cw = w_in.shape[2]

    def gather_layer(l):
        own = [a[l].astype(BF16) for a in big]
        got = _gather_weights(f"gather_weights{l}", own)
        full = [jnp.where(lax.broadcasted_iota(jnp.int32, g.shape, 0) == chip, o[None], g) for g, o in zip(got, own)]
        shard_of = lambda j: full[0][j]
        wa_l = _group_from_shards(shard_of, cw, GROUP_A_COLS, 0)
        wb_l = _group_from_shards(shard_of, cw, GROUP_B_COLS, LANES - FOX_HEADS)
        return wa_l, wb_l, full[1].reshape(-1, full[1].shape[2]), full[2].reshape(-1, full[2].shape[2])

    layer_w = [gather_layer(l) for l in range(depth)]

    tm = _tile_of(s, 256, 8)
    fl_block = MIX_WIDTH // LANES

    saved = []
    cur = xs
    for l in range(depth):
        wa, wb, wkv, wout = layer_w[l]
        h = _rms_fwd(f"rms_fwd{l}", cur, norm_w[l][None], ts)
        pa = _mm(f"inproj_a{l}", h, wa, "nn", tm, _tile_of(PA, 1664, LANES), BF16)
        pb = _mm(f"inproj_b{l}", h, wb, "nn", tm, PB, F32)
        bpad = jnp.pad(b_forget[l], (0, LANES - FOX_HEADS))[None]
        ccol, crow = _gate_fwd(f"gate_fwd{l}", pb, bpad, fl_block)
        ccol4 = jnp.repeat(ccol[:, :FOX_HEADS].reshape(s, 4, 2).transpose(1, 0, 2), HEAD_DIM, axis=2)
        crow4 = jnp.pad(crow.reshape(nb, 4, 2, TILE).transpose(1, 0, 2, 3), ((0, 0), (0, 0), (0, 6), (0, 0)))
        o_sb = _sb_fwd(f"sb_fwd{l}", pa, 0)
        o_fx, lse_fx = _fox_fwd(f"fox_fwd{l}", pa, 3 * SB_WIDTH, ccol4, crow4)
        mn = _rms_fwd(f"mem_rms{l}", mems, mem_norm_w[l][None], mems.shape[0])
        mkv = _mm(f"mem_kv{l}", mn, wkv, "nn", mems.shape[0], 2 * MEM_WIDTH, BF16)
        o_m, lse_m = _mem_fwd(f"mem_fwd{l}", pa, mkv)
        nxt, y2 = _out_fwd(f"out_fwd{l}", o_sb, o_fx, o_m, pb, out_norm_w[l][None], cur, wout, ts)
        saved.append((cur, h, pa, pb, bpad, ccol4, crow4, o_sb, o_fx, lse_fx, mn, mkv, o_m, lse_m, y2))
        cur = nxt

    loss_v, dx, dxb, g_final = _final_loss("final_loss", cur, final_norm_w[None], target, ts)

    g_norm, g_b, g_memnorm, g_outnorm = [None] * depth, [None] * depth, [None] * depth, [None] * depth
    g_wa, g_wb, g_wkv, g_wout = [None] * depth, [None] * depth, [None] * depth, [None] * depth
    for l in reversed(range(depth)):
        xin, h, pa, pb, bpad, ccol4, crow4, o_sb, o_fx, lse_fx, mn, mkv, o_m, lse_m, y2 = saved[l]
        wa, wb, wkv, wout = layer_w[l]
        dy, dgate, g_outnorm[l] = _out_bwd(f"out_bwd{l}", dxb, o_sb, o_fx, o_m, pb, out_norm_w[l][None], wout, ts)
        g_wout[l] = _mm(f"dw_out{l}", y2, dxb, "tn", _tile_of(MIX_WIDTH, 640, LANES), d, F32)
        dq_sb, dk_sb, dv_sb = _sb_bwd(f"sb_bwd{l}", pa, 0, dy, 0)
        dq_fx, dk_fx, dv_fx, cs4 = _fox_bwd(f"fox_bwd{l}", pa, 3 * SB_WIDTH, ccol4, crow4, o_fx, lse_fx, dy, SB_WIDTH)
        colsum = cs4[:, :, :2, :].transpose(1, 0, 2, 3).reshape(nb, 8, TILE)
        dlogit, g_b[l] = _gate_bwd(f"gate_bwd{l}", pb, bpad, colsum, fl_block)
        dq_m, dk_m, dv_m = _mem_bwd(f"mem_bwd{l}", pa, mkv, o_m, lse_m, dy, SB_WIDTH + FOX_WIDTH)
        dmkv = jnp.concatenate([dk_m, dv_m], axis=1)
        g_wkv[l] = _mm(f"dw_kv{l}", mn, dmkv, "tn", d, 2 * MEM_WIDTH, F32)
        dmn = _mm(f"dmem{l}", dmkv, wkv, "nt", mems.shape[0], d, F32)
        g_memnorm[l] = _rms_wgrad(f"mem_norm_grad{l}", mems, dmn)
        dpa = jnp.concatenate([dq_sb, dk_sb, dv_sb, dq_fx, dk_fx, dv_fx, dq_m], axis=1)
        dpb = jnp.concatenate([dgate, dlogit], axis=1)
        tw = _tile_of(d, 512, LANES)
        g_wa[l] = _mm(f"dw_in_a{l}", h, dpa, "tn", tw, _tile_of(PA, 1664, LANES), F32)
        g_wb[l] = _mm(f"dw_in_b{l}", h, dpb, "tn", tw, PB, F32)
        dh = _mm(f"dh_a{l}", dpa, wa, "nt", tm, d, F32)
        dh = _mm(f"dh_b{l}", dpb, wb, "nt", tm, d, F32, res=dh)
        dx, dxb, g_norm[l] = _rms_bwd(f"rms_bwd{l}", xin, norm_w[l][None], dh, dx, ts)

    g_own = [[None] * depth for _ in big]
    g_other = [[None] * depth for _ in big]
    for l in reversed(range(depth)):
        g4s = [jnp.stack([_shard_from_groups(g_wa[l], g_wb[l], j, cw) for j in range(N_CHIPS)]),
               g_wkv[l].reshape(N_CHIPS, -1, g_wkv[l].shape[1]), g_wout[l].reshape(N_CHIPS, -1, d)]
        tiles = [_tile_of(g.shape[1] // 2, 256, 16) for g in g4s]
        from_sibling = _swap_halves(f"grad_swap_halves{l}", g4s)
        chip_sums = [_add_half(f"grad_add_half{l}_{k}", g, r, cvec, t)
                     for k, (g, r, t) in enumerate(zip(g4s, from_sibling, tiles))]
        from_chips = _scatter_chips(f"grad_scatter_chips{l}", chip_sums)
        halves = [_sum_chips(f"grad_sum_chips{l}_{k}", h_, r, mvec, t)
                  for k, (h_, r, t) in enumerate(zip(chip_sums, from_chips, tiles))]
        others = _swap_reduced(f"grad_swap_reduced{l}", halves)
        for k in range(len(big)):
            g_own[k][l], g_other[k][l] = halves[k], others[k]

    small_w = [norm_w, b_forget, mem_norm_w, out_norm_w, final_norm_w]
    small_m = [m_norm_w, m_b_forget, m_mem_norm_w, m_out_norm_w, m_final_norm_w]
    small_v = [v_norm_w, v_b_forget, v_mem_norm_w, v_out_norm_w, v_final_norm_w]
    small_shapes = [a.shape for a in small_w]
    local_small = [jnp.concatenate(g_norm, axis=0), jnp.stack([g[0, :FOX_HEADS] for g in g_b]),
                   jnp.concatenate(g_memnorm, axis=0), jnp.concatenate(g_outnorm, axis=0), g_final[0]]
    reduced = _allreduce_small("small_allreduce", _pack_small(local_small + [loss_v[0, :1]]))
    small_grads = _unpack_small(reduced, small_shapes)
    n_small_rows = sum(-(-a.size // LANES) for a in small_w)
    loss = reduced[n_small_rows, 0]

    d_s, m_s, v_s = _adamw("adamw_small", _pack_small(small_w), _pack_small(small_grads), _pack_small(small_m),
                           _pack_small(small_v), _pack_small(small_w).shape[0])
    small_delta, small_m2, small_v2 = (_unpack_small(a, small_shapes) for a in (d_s, m_s, v_s))
    big_grads, big_delta, big_m2, big_v2 = [], [], [], []
    for k, (nm, w_, m_, v_) in enumerate(zip(("w_in", "w_mem_kv", "w_out"), big, (m_w_in, m_w_mem_kv, m_w_out),
                                             (v_w_in, v_w_mem_kv, v_w_out))):
        outs = _adamw_sharded(f"adamw_{nm}", w_, m_, v_, g_own[k], g_other[k], cvec,
                              _tile_of(w_.shape[1] // 2, 256, 8))
        for lst, o in zip((big_grads, big_delta, big_m2, big_v2), outs):
            lst.append(o)

    def order(sm, bg):
        return [sm[0], bg[0], sm[1], sm[2], bg[1], sm[3], bg[2], sm[4]]

    return (loss, dx[None], *order(small_grads, big_grads), *order(small_delta, big_delta),
            *order(small_m2, big_m2), *order(small_v2, big_v2))
```

```python
import functools

import jax
import jax.numpy as jnp
from jax import lax
from jax.experimental import pallas as pl
from jax.experimental.pallas import tpu as pltpu

F32 = jnp.float32
BF16 = jnp.bfloat16

HEAD_DIM = 64
SB_WIDTH = 512
FOX_WIDTH = 512
FOX_HEADS = 8
MEM_WIDTH = 256
MIX_WIDTH = SB_WIDTH + FOX_WIDTH + MEM_WIDTH
TOTAL_HEADS = MIX_WIDTH // HEAD_DIM
IN_WIDTH = 3 * SB_WIDTH + 3 * FOX_WIDTH + FOX_HEADS + MEM_WIDTH + MIX_WIDTH
LANES = 128
QKV_WIDTH = 3 * SB_WIDTH + 3 * FOX_WIDTH
PA = QKV_WIDTH + MEM_WIDTH
PB = LANES + MIX_WIDTH
EPS = 1e-6
SCALE = HEAD_DIM ** -0.5
TILE = 256
NEG_INF = float("-inf")
MASKED = -1e30

ADAM_LR = 0.001
ADAM_B1 = 0.9
ADAM_B2 = 0.999
ADAM_EPS = 1e-08
ADAM_WD = 0.01
ADAM_STEP = 10

N_CHIPS = 4
N_DEV = 8
VMEM_LIMIT = 48 * 1024 * 1024
MESH = pl.DeviceIdType.MESH


def _params(*sem):
    return pltpu.CompilerParams(dimension_semantics=tuple(sem), vmem_limit_bytes=VMEM_LIMIT)


def _dot(a, b):
    return jnp.dot(a, b, preferred_element_type=F32)


def _dot_nt(a, b):
    return lax.dot_general(a, b, (((1,), (1,)), ((), ())), preferred_element_type=F32)


def _dot_tn(a, b):
    return lax.dot_general(a, b, (((0,), (0,)), ((), ())), preferred_element_type=F32)


def _split2(x):
    hi = x.astype(BF16)
    lo = (x - hi.astype(F32)).astype(BF16)
    return hi, lo


def _split3(x):
    hi = x.astype(BF16)
    r = x - hi.astype(F32)
    mid = r.astype(BF16)
    lo = (r - mid.astype(F32)).astype(BF16)
    return hi, mid, lo


def _sum_l2(x, u):
    hi, lo = _split2(x)
    return _dot(hi, u) + _dot(lo, u)


def _sum_l3(x, u):
    hi, mid, lo = _split3(x)
    return _dot(hi, u) + _dot(mid, u) + _dot(lo, u)


def _sum_r3(u, x):
    hi, mid, lo = _split3(x)
    return _dot(u, hi) + _dot(u, mid) + _dot(u, lo)


def _softplus(z):
    return jnp.maximum(z, 0.0) + jnp.log1p(jnp.exp(-jnp.abs(z)))


def _tri(n, pred):
    r = lax.broadcasted_iota(jnp.int32, (n, n), 0)
    c = lax.broadcasted_iota(jnp.int32, (n, n), 1)
    return jnp.where(pred(r, c), 1.0, 0.0).astype(BF16)


def _rows(ref, j, n=TILE):
    return pl.ds(pl.multiple_of(j * n, n), n)


def _mm(name, a, b, mode, tm, tn, out_dtype, res=None, a_lead=(), b_lead=()):
    a2, b2 = a.shape[len(a_lead):], b.shape[len(b_lead):]
    if mode == "tn":
        k, m = a2
    else:
        m, k = a2
    n = b2[0] if mode == "nt" else b2[1]
    assert m % tm == 0 and n % tn == 0, (name, m, tm, n, tn)
    na, nb = (None,) * len(a_lead), (None,) * len(b_lead)
    if mode == "tn":
        a_spec = pl.BlockSpec(na + (k, tm), lambda j, i: a_lead + (0, i))
    else:
        a_spec = pl.BlockSpec(na + (tm, k), lambda j, i: a_lead + (i, 0))
    if mode == "nt":
        b_spec = pl.BlockSpec(nb + (tn, k), lambda j, i: b_lead + (j, 0))
    else:
        b_spec = pl.BlockSpec(nb + (k, tn), lambda j, i: b_lead + (0, j))
    o_spec = pl.BlockSpec((tm, tn), lambda j, i: (i, j))
    dot = {"nn": _dot, "nt": _dot_nt, "tn": _dot_tn}[mode]

    def body(a_ref, b_ref, *rest):
        o_ref = rest[-1]
        acc = dot(a_ref[...].astype(BF16), b_ref[...].astype(BF16))
        if res is not None:
            acc = acc + rest[0][...]
        o_ref[...] = acc.astype(o_ref.dtype)

    args, specs = [a, b], [a_spec, b_spec]
    if res is not None:
        args.append(res)
        specs.append(o_spec)
    return pl.pallas_call(
        body, name=name, grid=(n // tn, m // tm), in_specs=specs, out_specs=o_spec,
        out_shape=jax.ShapeDtypeStruct((m, n), out_dtype),
        compiler_params=_params("parallel", "parallel"),
    )(*args)


def _rms_fwd(name, x, g, ts):
    s, d = x.shape

    def body(x_ref, g_ref, o_ref):
        xf = x_ref[...]
        r = lax.rsqrt(jnp.mean(xf * xf, axis=1, keepdims=True) + EPS)
        o_ref[...] = (xf * r * g_ref[...]).astype(BF16)

    return pl.pallas_call(
        body, name=name, grid=(s // ts,),
        in_specs=[pl.BlockSpec((ts, d), lambda i: (i, 0)), pl.BlockSpec((1, d), lambda i: (0, 0))],
        out_specs=pl.BlockSpec((ts, d), lambda i: (i, 0)),
        out_shape=jax.ShapeDtypeStruct((s, d), BF16),
        compiler_params=_params("parallel"),
    )(x, g)


def _rms_bwd(name, x, g, dh, dres, ts):
    s, d = x.shape

    def body(x_ref, g_ref, dh_ref, dres_ref, dx_ref, dxb_ref, dg_ref):
        @pl.when(pl.program_id(0) == 0)
        def _():
            dg_ref[...] = jnp.zeros_like(dg_ref)

        xf = x_ref[...]
        r = lax.rsqrt(jnp.mean(xf * xf, axis=1, keepdims=True) + EPS)
        xh = xf * r
        dhf = dh_ref[...]
        dg_ref[...] += jnp.sum(dhf * xh, axis=0, keepdims=True)
        dxh = dhf * g_ref[...]
        m = jnp.mean(dxh * xh, axis=1, keepdims=True)
        dx = r * (dxh - xh * m) + dres_ref[...]
        dx_ref[...] = dx
        dxb_ref[...] = dx.astype(BF16)

    row = pl.BlockSpec((ts, d), lambda i: (i, 0))
    vec = pl.BlockSpec((1, d), lambda i: (0, 0))
    return pl.pallas_call(
        body, name=name, grid=(s // ts,), in_specs=[row, vec, row, row], out_specs=[row, row, vec],
        out_shape=[jax.ShapeDtypeStruct((s, d), F32), jax.ShapeDtypeStruct((s, d), BF16),
                   jax.ShapeDtypeStruct((1, d), F32)],
        compiler_params=_params("arbitrary"),
    )(x, g, dh, dres)


def _rms_wgrad(name, x, dh):
    m_, d = x.shape

    def body(x_ref, dh_ref, dg_ref):
        xf = x_ref[...]
        r = lax.rsqrt(jnp.mean(xf * xf, axis=1, keepdims=True) + EPS)
        dg_ref[...] = jnp.sum(dh_ref[...] * xf * r, axis=0, keepdims=True)

    return pl.pallas_call(
        body, name=name, out_shape=jax.ShapeDtypeStruct((1, d), F32),
    )(x, dh)


def _final_loss(name, x, g, target, ts):
    s, d = x.shape

    def body(x_ref, g_ref, t_ref, loss_ref, dx_ref, dxb_ref, dg_ref):
        @pl.when(pl.program_id(0) == 0)
        def _():
            dg_ref[...] = jnp.zeros_like(dg_ref)
            loss_ref[...] = jnp.zeros_like(loss_ref)

        xf = x_ref[...]
        gw = g_ref[...]
        r = lax.rsqrt(jnp.mean(xf * xf, axis=1, keepdims=True) + EPS)
        xh = xf * r
        e = xh * gw - t_ref[...]
        part = 0.5 * jnp.sum(jnp.mean(e * e, axis=1, keepdims=True), axis=0, keepdims=True)
        loss_ref[...] += jnp.broadcast_to(part, loss_ref.shape)
        dy = e * (1.0 / d)
        dg_ref[...] += jnp.sum(dy * xh, axis=0, keepdims=True)
        dxh = dy * gw
        m = jnp.mean(dxh * xh, axis=1, keepdims=True)
        dx = r * (dxh - xh * m)
        dx_ref[...] = dx
        dxb_ref[...] = dx.astype(BF16)

    row = pl.BlockSpec((ts, d), lambda i: (i, 0))
    vec = pl.BlockSpec((1, d), lambda i: (0, 0))
    lvec = pl.BlockSpec((1, LANES), lambda i: (0, 0))
    return pl.pallas_call(
        body, name=name, grid=(s // ts,), in_specs=[row, vec, row], out_specs=[lvec, row, row, vec],
        out_shape=[jax.ShapeDtypeStruct((1, LANES), F32), jax.ShapeDtypeStruct((s, d), F32),
                   jax.ShapeDtypeStruct((s, d), BF16), jax.ShapeDtypeStruct((1, d), F32)],
        compiler_params=_params("arbitrary"),
    )(x, g, target)


def _gate_fwd(name, pb, bpad, fl_block):
    s = pb.shape[0]
    nb = s // TILE

    def body(fl_ref, b_ref, ccol_ref, crow_ref, carry):
        @pl.when(pl.program_id(0) == 0)
        def _():
            carry[...] = jnp.zeros_like(carry)

        u = fl_ref[...] + b_ref[...]
        lf = jnp.minimum(u, 0.0) - jnp.log1p(jnp.exp(-jnp.abs(u)))
        lower = _tri(TILE, lambda r, c: c <= r)
        c = _sum_r3(lower, lf) + carry[0:1, :]
        ccol_ref[...] = c
        crow_ref[0] = c.T[0:8, :]
        carry[...] = jnp.broadcast_to(c[TILE - 1:TILE, :], carry.shape)

    return pl.pallas_call(
        body, name=name, grid=(nb,),
        in_specs=[pl.BlockSpec((TILE, LANES), lambda i: (i, fl_block)), pl.BlockSpec((1, LANES), lambda i: (0, 0))],
        out_specs=[pl.BlockSpec((TILE, LANES), lambda i: (i, 0)), pl.BlockSpec((1, 8, TILE), lambda i: (i, 0, 0))],
        out_shape=[jax.ShapeDtypeStruct((s, LANES), F32), jax.ShapeDtypeStruct((nb, 8, TILE), F32)],
        scratch_shapes=[pltpu.VMEM((8, LANES), F32)],
        compiler_params=_params("arbitrary"),
    )(pb, bpad)


def _gate_bwd(name, pb, bpad, colsum, fl_block):
    s = pb.shape[0]
    nb = s // TILE

    def body(fl_ref, b_ref, cs_ref, dl_ref, db_ref, carry):
        @pl.when(pl.program_id(0) == 0)
        def _():
            carry[...] = jnp.zeros_like(carry)
            db_ref[...] = jnp.zeros_like(db_ref)

        upper = _tri(TILE, lambda r, c: r >= c)
        rsum = _sum_l3(cs_ref[0], upper) + carry[:, 0:1]
        carry[...] = jnp.broadcast_to(rsum[:, 0:1], carry.shape)
        full = jnp.concatenate([rsum, jnp.zeros((LANES - 8, TILE), F32)], axis=0)
        dlf = -full.T
        u = fl_ref[...] + b_ref[...]
        dlogit = dlf * (1.0 - jax.nn.sigmoid(u))
        dl_ref[...] = dlogit.astype(BF16)
        db_ref[...] += jnp.sum(dlogit, axis=0, keepdims=True)

    rev = lambda i: (nb - 1 - i, 0)
    return pl.pallas_call(
        body, name=name, grid=(nb,),
        in_specs=[pl.BlockSpec((TILE, LANES), lambda i: (nb - 1 - i, fl_block)),
                  pl.BlockSpec((1, LANES), lambda i: (0, 0)),
                  pl.BlockSpec((1, 8, TILE), lambda i: (nb - 1 - i, 0, 0))],
        out_specs=[pl.BlockSpec((TILE, LANES), rev), pl.BlockSpec((1, LANES), lambda i: (0, 0))],
        out_shape=[jax.ShapeDtypeStruct((s, LANES), BF16), jax.ShapeDtypeStruct((1, LANES), F32)],
        scratch_shapes=[pltpu.VMEM((8, LANES), F32)],
        compiler_params=_params("arbitrary"),
    )(pb, bpad, colsum)


def _head_slices(hh):
    return slice(HEAD_DIM * hh, HEAD_DIM * (hh + 1))


def _scaled_q(q_ref, sl, scale=SCALE):
    return (q_ref[:, sl].astype(F32) * scale).astype(BF16)


def _neg_abs(x):
    sign = jnp.uint32(0x80000000)
    return lax.bitcast_convert_type(lax.bitcast_convert_type(x, jnp.uint32) | sign, F32)


def _sb_tile(qn, kj, carry, strict, u_after, diag):
    nz = _dot_nt(qn, kj)
    lf = jnp.minimum(nz, 0.0) - jnp.log(1.0 + jnp.exp(_neg_abs(nz)))
    lsig = lf - nz
    if diag:
        lf = jnp.where(strict, lf, 0.0)
    sx = _dot(lf.astype(BF16), u_after)
    a = jnp.exp(lsig + sx + carry)
    if diag:
        a = jnp.where(strict, a, 0.0)
    return lsig, a, carry + sx[:, 0:1] + lf[:, 0:1]


def _sb_fwd(name, pa, col0):
    s = pa.shape[0]
    nb = s // TILE
    cb = col0 // LANES

    def body(q_ref, k_ref, v_ref, o_ref, lsig_s, lf_s):
        i = pl.program_id(1)
        r = lax.broadcasted_iota(jnp.int32, (TILE, TILE), 0)
        c = lax.broadcasted_iota(jnp.int32, (TILE, TILE), 1)
        strict = c < r
        u_after = _tri(TILE, lambda rr, cc: rr > cc)
        qs = [_scaled_q(q_ref, _head_slices(hh), -SCALE) for hh in range(2)]

        def neg_z(j):
            kblk = k_ref[_rows(k_ref, j), :]
            return [_dot_nt(qs[hh], kblk[:, _head_slices(hh)]) for hh in range(2)]

        def scores(nzs, slot, diag):
            for hh, nz in enumerate(nzs):
                lf = jnp.minimum(nz, 0.0) - jnp.log(1.0 + jnp.exp(_neg_abs(nz)))
                lsig = lf - nz
                if diag:
                    lf = jnp.where(strict, lf, 0.0)
                    lsig = jnp.where(strict, lsig, MASKED)
                lsig_s[slot, hh] = lsig
                lf_s[slot, hh] = lf.astype(BF16)

        def weigh(j, slot, state):
            vblk = v_ref[_rows(v_ref, j), :]
            new = []
            for hh in range(2):
                carry, acc = state[hh]
                lfb = lf_s[slot, hh]
                sx = _dot(lfb, u_after)
                a = jnp.exp(lsig_s[slot, hh] + sx + carry)
                new.append((carry + sx[:, 0:1] + lfb[:, 0:1].astype(F32),
                            acc + _dot(a.astype(BF16), vblk[:, _head_slices(hh)])))
            return tuple(new)

        def step(t, state):
            state = weigh(i - t + 1, (t - 1) % 2, state)
            scores(neg_z(i - t), t % 2, False)
            return state

        zero = (jnp.zeros((TILE, 1), F32), jnp.zeros((TILE, HEAD_DIM), F32))
        scores(neg_z(i), 0, True)
        state = lax.fori_loop(1, i + 1, step, (zero, zero))
        state = weigh(0, i % 2, state)
        o_ref[...] = jnp.concatenate([state[0][1], state[1][1]], axis=1)

    return pl.pallas_call(
        body, name=name, grid=(4, nb),
        in_specs=[pl.BlockSpec((TILE, LANES), lambda p, i: (i, cb + p)),
                  pl.BlockSpec((s, LANES), lambda p, i: (0, cb + 4 + p)),
                  pl.BlockSpec((s, LANES), lambda p, i: (0, cb + 8 + p))],
        out_specs=pl.BlockSpec((TILE, LANES), lambda p, i: (i, p)),
        out_shape=jax.ShapeDtypeStruct((s, SB_WIDTH), F32),
        scratch_shapes=[pltpu.VMEM((2, 2, TILE, TILE), F32), pltpu.VMEM((2, 2, TILE, TILE), BF16)],
        compiler_params=_params("parallel", "arbitrary"),
    )(pa, pa, pa)


def _sb_bwd(name, pa, col0, dout, dcol0):
    s = pa.shape[0]
    nb = s // TILE
    cb = col0 // LANES
    db = dcol0 // LANES

    def body(q_ref, k_ref, v_ref, do_ref, dq_ref, dk_ref, dv_ref, dk_acc, dv_acc, dpan, span, gsum, lsig_s, lf_s):
        i = pl.program_id(1)

        @pl.when(i == 0)
        def _():
            dk_acc[...] = jnp.zeros_like(dk_acc)
            dv_acc[...] = jnp.zeros_like(dv_acc)

        r = lax.broadcasted_iota(jnp.int32, (TILE, TILE), 0)
        c = lax.broadcasted_iota(jnp.int32, (TILE, TILE), 1)
        strict = c < r
        u_after = _tri(TILE, lambda rr, cc: rr > cc)
        u_before = _tri(TILE, lambda rr, cc: rr < cc)
        qs = [_scaled_q(q_ref, _head_slices(hh), -SCALE) for hh in range(2)]
        dos = [do_ref[:, _head_slices(hh)].astype(BF16) for hh in range(2)]

        def scores(j, slot, diag):
            kblk = k_ref[_rows(k_ref, j), :]
            for hh in range(2):
                nz = _dot_nt(qs[hh], kblk[:, _head_slices(hh)])
                lf = jnp.minimum(nz, 0.0) - jnp.log(1.0 + jnp.exp(_neg_abs(nz)))
                lsig = lf - nz
                if diag:
                    lf = jnp.where(strict, lf, 0.0)
                    lsig = jnp.where(strict, lsig, MASKED)
                lsig_s[slot, hh] = lsig
                lf_s[slot, hh] = lf.astype(BF16)

        def grads(j, slot, carries):
            vblk = v_ref[_rows(v_ref, j), :]
            new = []
            for hh in range(2):
                lfb = lf_s[slot, hh]
                lsig = lsig_s[slot, hh]
                sx = _dot(lfb, u_after)
                a = jnp.exp(lsig + sx + carries[hh])
                g = a * _dot_nt(dos[hh], vblk[:, _head_slices(hh)])
                sig = jnp.exp(lsig)
                inside = _dot(g.astype(BF16), u_before)
                dpan[hh, j] = sig * (inside + g) - g
                span[hh, j] = sig
                gsum[hh, j] = inside[:, TILE - 1:TILE] + g[:, TILE - 1:TILE]
                dv_acc[hh, _rows(None, j), :] += _dot_tn(a.astype(BF16), dos[hh])
                new.append(carries[hh] + sx[:, 0:1] + lfb[:, 0:1].astype(F32))
            return tuple(new)

        def step1(t, carries):
            carries = grads(i - t + 1, (t - 1) % 2, carries)
            scores(i - t, t % 2, False)
            return carries

        zero1 = jnp.zeros((TILE, 1), F32)
        scores(i, 0, True)
        carries = lax.fori_loop(1, i + 1, step1, (zero1, zero1))
        grads(0, i % 2, carries)

        def pass2(j, state):
            kblk = k_ref[_rows(k_ref, j), :]
            new = []
            for hh in range(2):
                before, ndq = state[hh]
                ndzb = (dpan[hh, j] + span[hh, j] * before).astype(BF16)
                dk_acc[hh, _rows(None, j), :] += _dot_tn(ndzb, qs[hh])
                new.append((before + gsum[hh, j], ndq + _dot(ndzb, kblk[:, _head_slices(hh)])))
            return tuple(new)

        zero2 = (zero1, jnp.zeros((TILE, HEAD_DIM), F32))
        state = lax.fori_loop(0, i + 1, pass2, (zero2, zero2))
        dq_ref[...] = jnp.concatenate([state[0][1] * -SCALE, state[1][1] * -SCALE], axis=1).astype(BF16)

        @pl.when(i == nb - 1)
        def _():
            dk_ref[...] = jnp.concatenate([dk_acc[0], dk_acc[1]], axis=1).astype(BF16)
            dv_ref[...] = jnp.concatenate([dv_acc[0], dv_acc[1]], axis=1).astype(BF16)

    qspec = pl.BlockSpec((TILE, LANES), lambda p, i: (i, p))
    kvspec = pl.BlockSpec((s, LANES), lambda p, i: (0, p))
    out = jax.ShapeDtypeStruct((s, SB_WIDTH), BF16)
    return pl.pallas_call(
        body, name=name, grid=(4, nb),
        in_specs=[pl.BlockSpec((TILE, LANES), lambda p, i: (i, cb + p)),
                  pl.BlockSpec((s, LANES), lambda p, i: (0, cb + 4 + p)),
                  pl.BlockSpec((s, LANES), lambda p, i: (0, cb + 8 + p)),
                  pl.BlockSpec((TILE, LANES), lambda p, i: (i, db + p))],
        out_specs=[qspec, kvspec, kvspec], out_shape=[out, out, out],
        scratch_shapes=[pltpu.VMEM((2, s, HEAD_DIM), F32), pltpu.VMEM((2, s, HEAD_DIM), F32),
                        pltpu.VMEM((2, nb, TILE, TILE), F32), pltpu.VMEM((2, nb, TILE, TILE), F32),
                        pltpu.VMEM((2, nb, TILE, 1), F32),
                        pltpu.VMEM((2, 2, TILE, TILE), F32), pltpu.VMEM((2, 2, TILE, TILE), BF16)],
        compiler_params=_params("arbitrary", "arbitrary"),
    )(pa, pa, pa, dout)


def _fox_scores(q, kj, cq, crj, causal, diag):
    sc = _dot_nt(q, kj) + (cq - crj)
    if diag:
        sc = jnp.where(causal, sc, NEG_INF)
    return sc


def _fox_fwd(name, pa, col0, ccol4, crow4):
    s = pa.shape[0]
    nb = s // TILE
    cb = col0 // LANES

    def body(q_ref, k_ref, v_ref, cc_ref, cr_ref, o_ref, lse_ref, sc_s):
        i = pl.program_id(1)
        r = lax.broadcasted_iota(jnp.int32, (TILE, TILE), 0)
        c = lax.broadcasted_iota(jnp.int32, (TILE, TILE), 1)
        causal = c <= r
        qs = [_scaled_q(q_ref, _head_slices(hh)) for hh in range(2)]
        cqs = [cc_ref[:, HEAD_DIM * hh:HEAD_DIM * hh + 1] for hh in range(2)]

        def logits(j, slot, diag):
            kblk = k_ref[_rows(k_ref, j), :]
            tops = []
            for hh in range(2):
                sc = _fox_scores(qs[hh], kblk[:, _head_slices(hh)], cqs[hh], cr_ref[j, hh:hh + 1, :], causal, diag)
                sc_s[slot, hh] = sc
                tops.append(jnp.max(sc, axis=1, keepdims=True))
            return tuple(tops)

        def update(j, slot, tops, state):
            vblk = v_ref[_rows(v_ref, j), :]
            new = []
            for hh in range(2):
                m, l, acc = state[hh]
                m2 = jnp.maximum(m, tops[hh])
                alpha = jnp.exp(m - m2)
                p = jnp.exp(sc_s[slot, hh] - m2)
                new.append((m2, l * alpha + jnp.sum(p, axis=1, keepdims=True),
                            acc * alpha + _dot(p.astype(BF16), vblk[:, _head_slices(hh)])))
            return tuple(new)

        def step(t, both):
            tops, state = both
            state = update(i - t + 1, (t - 1) % 2, tops, state)
            return logits(i - t, t % 2, False), state

        zero = (jnp.full((TILE, 1), NEG_INF, F32), jnp.zeros((TILE, 1), F32), jnp.zeros((TILE, HEAD_DIM), F32))
        tops, state = lax.fori_loop(1, i + 1, step, (logits(i, 0, True), (zero, zero)))
        state = update(0, i % 2, tops, state)
        o_ref[...] = jnp.concatenate([state[hh][2] / state[hh][1] for hh in range(2)], axis=1)
        lse_ref[...] = jnp.concatenate(
            [jnp.broadcast_to(state[hh][0] + jnp.log(state[hh][1]), (TILE, HEAD_DIM)) for hh in range(2)], axis=1)

    return pl.pallas_call(
        body, name=name, grid=(4, nb),
        in_specs=[pl.BlockSpec((TILE, LANES), lambda p, i: (i, cb + p)),
                  pl.BlockSpec((s, LANES), lambda p, i: (0, cb + 4 + p)),
                  pl.BlockSpec((s, LANES), lambda p, i: (0, cb + 8 + p)),
                  pl.BlockSpec((None, TILE, LANES), lambda p, i: (p, i, 0)),
                  pl.BlockSpec((None, nb, 8, TILE), lambda p, i: (p, 0, 0, 0))],
        out_specs=[pl.BlockSpec((TILE, LANES), lambda p, i: (i, p)),
                   pl.BlockSpec((None, TILE, LANES), lambda p, i: (p, i, 0))],
        out_shape=[jax.ShapeDtypeStruct((s, FOX_WIDTH), F32), jax.ShapeDtypeStruct((4, s, LANES), F32)],
        scratch_shapes=[pltpu.VMEM((2, 2, TILE, TILE), F32)],
        compiler_params=_params("parallel", "arbitrary"),
    )(pa, pa, pa, ccol4, crow4)


def _fox_bwd(name, pa, col0, ccol4, crow4, out, lse, dout, dcol0):
    s = pa.shape[0]
    nb = s // TILE
    cb = col0 // LANES
    db = dcol0 // LANES

    def body(q_ref, k_ref, v_ref, cc_ref, cr_ref, o_ref, lse_ref, do_ref,
             dq_ref, dk_ref, dv_ref, cs_ref, dk_acc, dv_acc, p_s, ds_s):
        i = pl.program_id(1)

        @pl.when(i == 0)
        def _():
            dk_acc[...] = jnp.zeros_like(dk_acc)
            dv_acc[...] = jnp.zeros_like(dv_acc)
            cs_ref[...] = jnp.zeros_like(cs_ref)

        r = lax.broadcasted_iota(jnp.int32, (TILE, TILE), 0)
        c = lax.broadcasted_iota(jnp.int32, (TILE, TILE), 1)
        causal = c <= r
        qs = [_scaled_q(q_ref, _head_slices(hh)) for hh in range(2)]
        cqs = [cc_ref[:, HEAD_DIM * hh:HEAD_DIM * hh + 1] for hh in range(2)]
        lses = [lse_ref[:, HEAD_DIM * hh:HEAD_DIM * hh + 1] for hh in range(2)]
        dofs = [do_ref[:, _head_slices(hh)] for hh in range(2)]
        dos = [d_.astype(BF16) for d_ in dofs]
        deltas = [jnp.sum(dofs[hh] * o_ref[:, _head_slices(hh)], axis=1, keepdims=True) for hh in range(2)]

        def probs(j, slot, rowsums, diag):
            kblk = k_ref[_rows(k_ref, j), :]
            vblk = v_ref[_rows(v_ref, j), :]
            new = []
            for hh in range(2):
                sl = _head_slices(hh)
                sc = _fox_scores(qs[hh], kblk[:, sl], cqs[hh], cr_ref[j, hh:hh + 1, :], causal, diag)
                p = jnp.exp(sc - lses[hh])
                ds = p * (_dot_nt(dos[hh], vblk[:, sl]) - deltas[hh])
                p_s[slot, hh] = p.astype(BF16)
                ds_s[slot, hh] = ds.astype(BF16)
                cs_ref[j, hh:hh + 1, :] += jnp.sum(ds, axis=0, keepdims=True)
                new.append(rowsums[hh] + jnp.sum(ds, axis=1, keepdims=True))
            return tuple(new)

        def accumulate(j, slot, dqs):
            kblk = k_ref[_rows(k_ref, j), :]
            new = []
            for hh in range(2):
                dsb = ds_s[slot, hh]
                dv_acc[hh, _rows(None, j), :] += _dot_tn(p_s[slot, hh], dos[hh])
                dk_acc[hh, _rows(None, j), :] += _dot_tn(dsb, qs[hh])
                new.append(dqs[hh] + _dot(dsb, kblk[:, _head_slices(hh)]))
            return tuple(new)

        def step(t, both):
            rowsums, dqs = both
            dqs = accumulate(i - t + 1, (t - 1) % 2, dqs)
            return probs(i - t, t % 2, rowsums, False), dqs

        zero1 = jnp.zeros((TILE, 1), F32)
        zero64 = jnp.zeros((TILE, HEAD_DIM), F32)
        rowsums, dqs = lax.fori_loop(1, i + 1, step, (probs(i, 0, (zero1, zero1), True), (zero64, zero64)))
        dqs = accumulate(0, i % 2, dqs)
        for hh in range(2):
            cs_ref[i, hh:hh + 1, :] -= jnp.broadcast_to(rowsums[hh], (TILE, LANES)).T[0:1, :]
        dq_ref[...] = jnp.concatenate([dqs[0] * SCALE, dqs[1] * SCALE], axis=1).astype(BF16)

        @pl.when(i == nb - 1)
        def _():
            dk_ref[...] = jnp.concatenate([dk_acc[0], dk_acc[1]], axis=1).astype(BF16)
            dv_ref[...] = jnp.concatenate([dv_acc[0], dv_acc[1]], axis=1).astype(BF16)

    qspec = pl.BlockSpec((TILE, LANES), lambda p, i: (i, p))
    kvspec = pl.BlockSpec((s, LANES), lambda p, i: (0, p))
    o3 = jax.ShapeDtypeStruct((s, FOX_WIDTH), BF16)
    return pl.pallas_call(
        body, name=name, grid=(4, nb),
        in_specs=[pl.BlockSpec((TILE, LANES), lambda p, i: (i, cb + p)),
                  pl.BlockSpec((s, LANES), lambda p, i: (0, cb + 4 + p)),
                  pl.BlockSpec((s, LANES), lambda p, i: (0, cb + 8 + p)),
                  pl.BlockSpec((None, TILE, LANES), lambda p, i: (p, i, 0)),
                  pl.BlockSpec((None, nb, 8, TILE), lambda p, i: (p, 0, 0, 0)),
                  qspec,
                  pl.BlockSpec((None, TILE, LANES), lambda p, i: (p, i, 0)),
                  pl.BlockSpec((TILE, LANES), lambda p, i: (i, db + p))],
        out_specs=[qspec, kvspec, kvspec, pl.BlockSpec((None, nb, 8, TILE), lambda p, i: (p, 0, 0, 0))],
        out_shape=[o3, o3, o3, jax.ShapeDtypeStruct((4, nb, 8, TILE), F32)],
        scratch_shapes=[pltpu.VMEM((2, s, HEAD_DIM), F32), pltpu.VMEM((2, s, HEAD_DIM), F32),
                        pltpu.VMEM((2, 2, TILE, TILE), BF16), pltpu.VMEM((2, 2, TILE, TILE), BF16)],
        compiler_params=_params("arbitrary", "arbitrary"),
    )(pa, pa, pa, ccol4, crow4, out, lse, dout)


def _mem_fwd(name, pa, mkv):
    s = pa.shape[0]
    ml = mkv.shape[0]
    nb = s // TILE
    cb = QKV_WIDTH // LANES

    def body(q_ref, k_ref, v_ref, o_ref, lse_ref):
        outs, lses = [], []
        for hh in range(2):
            sl = _head_slices(hh)
            sc = _dot_nt(_scaled_q(q_ref, sl), k_ref[:, sl])
            m = jnp.max(sc, axis=1, keepdims=True)
            p = jnp.exp(sc - m)
            l = jnp.sum(p, axis=1, keepdims=True)
            outs.append(_dot(p.astype(BF16), v_ref[:, sl]) / l)
            lses.append(jnp.broadcast_to(m + jnp.log(l), (TILE, HEAD_DIM)))
        o_ref[...] = jnp.concatenate(outs, axis=1)
        lse_ref[...] = jnp.concatenate(lses, axis=1)

    return pl.pallas_call(
        body, name=name, grid=(2, nb),
        in_specs=[pl.BlockSpec((TILE, LANES), lambda p, i: (i, cb + p)),
                  pl.BlockSpec((ml, LANES), lambda p, i: (0, p)),
                  pl.BlockSpec((ml, LANES), lambda p, i: (0, 2 + p))],
        out_specs=[pl.BlockSpec((TILE, LANES), lambda p, i: (i, p)),
                   pl.BlockSpec((None, TILE, LANES), lambda p, i: (p, i, 0))],
        out_shape=[jax.ShapeDtypeStruct((s, MEM_WIDTH), F32), jax.ShapeDtypeStruct((2, s, LANES), F32)],
        compiler_params=_params("parallel", "parallel"),
    )(pa, mkv, mkv)


def _mem_bwd(name, pa, mkv, out, lse, dout, dcol0):
    s = pa.shape[0]
    ml = mkv.shape[0]
    nb = s // TILE
    cb = QKV_WIDTH // LANES
    db = dcol0 // LANES

    def body(q_ref, k_ref, v_ref, o_ref, lse_ref, do_ref, dq_ref, dk_ref, dv_ref, dk_acc, dv_acc):
        i = pl.program_id(1)

        @pl.when(i == 0)
        def _():
            dk_acc[...] = jnp.zeros_like(dk_acc)
            dv_acc[...] = jnp.zeros_like(dv_acc)

        dqs = []
        for hh in range(2):
            sl = _head_slices(hh)
            q = _scaled_q(q_ref, sl)
            kh = k_ref[:, sl]
            dof = do_ref[:, sl]
            do = dof.astype(BF16)
            delta = jnp.sum(dof * o_ref[:, sl], axis=1, keepdims=True)
            p = jnp.exp(_dot_nt(q, kh) - lse_ref[:, HEAD_DIM * hh:HEAD_DIM * hh + 1])
            ds = (p * (_dot_nt(do, v_ref[:, sl]) - delta)).astype(BF16)
            dv_acc[hh] += _dot_tn(p.astype(BF16), do)
            dk_acc[hh] += _dot_tn(ds, q)
            dqs.append(_dot(ds, kh) * SCALE)
        dq_ref[...] = jnp.concatenate(dqs, axis=1).astype(BF16)

        @pl.when(i == nb - 1)
        def _():
            dk_ref[...] = jnp.concatenate([dk_acc[0], dk_acc[1]], axis=1).astype(BF16)
            dv_ref[...] = jnp.concatenate([dv_acc[0], dv_acc[1]], axis=1).astype(BF16)

    qspec = pl.BlockSpec((TILE, LANES), lambda p, i: (i, p))
    kvspec = pl.BlockSpec((ml, LANES), lambda p, i: (0, p))
    okv = jax.ShapeDtypeStruct((ml, MEM_WIDTH), BF16)
    return pl.pallas_call(
        body, name=name, grid=(2, nb),
        in_specs=[pl.BlockSpec((TILE, LANES), lambda p, i: (i, cb + p)),
                  pl.BlockSpec((ml, LANES), lambda p, i: (0, p)),
                  pl.BlockSpec((ml, LANES), lambda p, i: (0, 2 + p)),
                  qspec,
                  pl.BlockSpec((None, TILE, LANES), lambda p, i: (p, i, 0)),
                  pl.BlockSpec((TILE, LANES), lambda p, i: (i, db + p))],
        out_specs=[qspec, kvspec, kvspec],
        out_shape=[jax.ShapeDtypeStruct((s, MEM_WIDTH), BF16), okv, okv],
        scratch_shapes=[pltpu.VMEM((2, ml, HEAD_DIM), F32), pltpu.VMEM((2, ml, HEAD_DIM), F32)],
        compiler_params=_params("arbitrary", "arbitrary"),
    )(pa, mkv, mkv, out, lse, dout)


def _head_maps():
    col = jnp.arange(MIX_WIDTH)[:, None] // HEAD_DIM
    g = (col == jnp.arange(LANES)[None, :]).astype(BF16)
    return g, g.T


def _normed_heads(osb_ref, ofx_ref, om_ref, g_ref, gt_ref):
    y = jnp.concatenate([osb_ref[...], ofx_ref[...], om_ref[...]], axis=1)
    msq = _sum_l3(y * y, g_ref[...]) * (1.0 / HEAD_DIM)
    rf = _sum_l3(lax.rsqrt(msq + EPS), gt_ref[...])
    return y * rf, rf


def _out_fwd(name, o_sb, o_fx, o_m, pb, ow, x, w_out, ts):
    s, d = x.shape
    g, gt = _head_maps()

    def body(osb_ref, ofx_ref, om_ref, gate_ref, ow_ref, x_ref, w_ref, g_ref, gt_ref, xo_ref, y2_ref):
        yh, _ = _normed_heads(osb_ref, ofx_ref, om_ref, g_ref, gt_ref)
        gate = gate_ref[...]
        y2 = (yh * ow_ref[...] * (gate * jax.nn.sigmoid(gate))).astype(BF16)
        y2_ref[...] = y2
        xo_ref[...] = x_ref[...] + _dot(y2, w_ref[...])

    return pl.pallas_call(
        body, name=name, grid=(s // ts,),
        in_specs=[_row_spec(ts, SB_WIDTH), _row_spec(ts, FOX_WIDTH), _row_spec(ts, MEM_WIDTH),
                  _row_spec(ts, MIX_WIDTH), _const_spec((1, MIX_WIDTH)), _row_spec(ts, d),
                  _const_spec((MIX_WIDTH, d)),
                  _const_spec((MIX_WIDTH, LANES)), _const_spec((LANES, MIX_WIDTH))],
        out_specs=[_row_spec(ts, d), _row_spec(ts, MIX_WIDTH)],
        out_shape=[jax.ShapeDtypeStruct((s, d), F32), jax.ShapeDtypeStruct((s, MIX_WIDTH), BF16)],
        compiler_params=_params("parallel"),
    )(o_sb, o_fx, o_m, pb, ow, x, w_out, g, gt)


def _row_spec(ts, w):
    return pl.BlockSpec((ts, w), lambda i: (i, 0))


def _const_spec(shape):
    return pl.BlockSpec(shape, lambda i: (0,) * len(shape))


def _out_bwd(name, dxb, o_sb, o_fx, o_m, pb, ow, w_out, ts):
    s, d = dxb.shape
    g, gt = _head_maps()

    def body(dx_ref, osb_ref, ofx_ref, om_ref, gate_ref, ow_ref, w_ref, g_ref, gt_ref, dy_ref, dgate_ref, dow_ref):
        @pl.when(pl.program_id(0) == 0)
        def _():
            dow_ref[...] = jnp.zeros_like(dow_ref)

        dy2 = _dot_nt(dx_ref[...], w_ref[...])
        yh, rf = _normed_heads(osb_ref, ofx_ref, om_ref, g_ref, gt_ref)
        gate = gate_ref[...]
        sig = jax.nn.sigmoid(gate)
        ow_v = ow_ref[...]
        dgate_ref[...] = (dy2 * (yh * ow_v) * (sig * (1.0 + gate * (1.0 - sig)))).astype(BF16)
        dn = dy2 * (gate * sig)
        dow_ref[...] += jnp.sum(dn * yh, axis=0, keepdims=True)
        dyh = dn * ow_v
        t = _sum_l3(dyh * yh, g_ref[...]) * (1.0 / HEAD_DIM)
        dy_ref[...] = rf * (dyh - yh * _sum_l3(t, gt_ref[...]))

    return pl.pallas_call(
        body, name=name, grid=(s // ts,),
        in_specs=[_row_spec(ts, d), _row_spec(ts, SB_WIDTH), _row_spec(ts, FOX_WIDTH), _row_spec(ts, MEM_WIDTH),
                  _row_spec(ts, MIX_WIDTH), _const_spec((1, MIX_WIDTH)),
                  _const_spec((MIX_WIDTH, d)),
                  _const_spec((MIX_WIDTH, LANES)), _const_spec((LANES, MIX_WIDTH))],
        out_specs=[_row_spec(ts, MIX_WIDTH), _row_spec(ts, MIX_WIDTH), _const_spec((1, MIX_WIDTH))],
        out_shape=[jax.ShapeDtypeStruct((s, MIX_WIDTH), F32), jax.ShapeDtypeStruct((s, MIX_WIDTH), BF16),
                   jax.ShapeDtypeStruct((1, MIX_WIDTH), F32)],
        compiler_params=_params("arbitrary"),
    )(dxb, o_sb, o_fx, o_m, pb, ow, w_out, g, gt)


def _adamw(name, w, g, m, v, tr):
    rows, cols = w.shape

    def body(w_ref, g_ref, m_ref, v_ref, d_ref, m2_ref, v2_ref):
        gv = g_ref[...]
        m2 = ADAM_B1 * m_ref[...] + (1.0 - ADAM_B1) * gv
        v2 = ADAM_B2 * v_ref[...] + (1.0 - ADAM_B2) * (gv * gv)
        m_hat = m2 / (1.0 - ADAM_B1 ** ADAM_STEP)
        v_hat = v2 / (1.0 - ADAM_B2 ** ADAM_STEP)
        d_ref[...] = -ADAM_LR * (m_hat / (jnp.sqrt(v_hat) + ADAM_EPS) + ADAM_WD * w_ref[...])
        m2_ref[...] = m2
        v2_ref[...] = v2

    spec = _row_spec(tr, cols)
    shp = jax.ShapeDtypeStruct((rows, cols), F32)
    return pl.pallas_call(
        body, name=name, grid=(rows // tr,), in_specs=[spec] * 4, out_specs=[spec] * 3, out_shape=[shp] * 3,
        compiler_params=_params("parallel"),
    )(w, g, m, v)


def _adamw_sharded(name, w, m, v, g_own, g_other, cvec, tr):
    depth, rows, cols = w.shape
    nt = rows // 2 // tr

    def body(c_ref, w_ref, m_ref, v_ref, *rest):
        g_refs, (g_ref, d_ref, m2_ref, v2_ref) = rest[:2 * depth], rest[2 * depth:]
        layer, mine = pl.program_id(0), pl.program_id(1) == c_ref[0]
        gv = None
        for lt in range(depth):
            cand = jnp.where(mine, g_refs[lt][...], g_refs[depth + lt][...])
            gv = cand if gv is None else jnp.where(layer == lt, cand, gv)
        m2 = ADAM_B1 * m_ref[...] + (1.0 - ADAM_B1) * gv
        v2 = ADAM_B2 * v_ref[...] + (1.0 - ADAM_B2) * (gv * gv)
        m_hat = m2 / (1.0 - ADAM_B1 ** ADAM_STEP)
        v_hat = v2 / (1.0 - ADAM_B2 ** ADAM_STEP)
        g_ref[...] = gv
        d_ref[...] = -ADAM_LR * (m_hat / (jnp.sqrt(v_hat) + ADAM_EPS) + ADAM_WD * w_ref[...])
        m2_ref[...] = m2
        v2_ref[...] = v2

    def g_map(lt, own):
        def index(l, hf, i, c_ref):
            use = jnp.logical_and(l == lt, (hf == c_ref[0]) == own)
            return jnp.where(use, i, 0), 0
        return index

    full = pl.BlockSpec((None, tr, cols), lambda l, hf, i, c_ref: (l, hf * nt + i, 0))
    g_specs = [pl.BlockSpec((tr, cols), g_map(lt, own)) for own in (True, False) for lt in range(depth)]
    shp = jax.ShapeDtypeStruct((depth, rows, cols), F32)
    return pl.pallas_call(
        body, name=name,
        grid_spec=pltpu.PrefetchScalarGridSpec(
            num_scalar_prefetch=1, grid=(depth, 2, nt), in_specs=[full] * 3 + g_specs, out_specs=[full] * 4),
        out_shape=[shp] * 4,
        compiler_params=_params("arbitrary", "arbitrary", "arbitrary"),
    )(cvec, w, m, v, *g_own, *g_other)


HBM_SPEC = pl.BlockSpec(memory_space=pltpu.HBM)


def _place():
    x, y, c = lax.axis_index("x"), lax.axis_index("y"), lax.axis_index("c")
    chips = [(1 - x, y), (x, 1 - y), (1 - x, 1 - y)]
    return x, y, c, chips


def _remote(src, dst, send_sems, recv_sems, k, to):
    return pltpu.make_async_remote_copy(src_ref=src, dst_ref=dst, send_sem=send_sems.at[k], recv_sem=recv_sems.at[k],
                                        device_id=to, device_id_type=MESH)


def _half_rows(n_rows, cc):
    rh = n_rows // 2
    return pl.ds(pl.multiple_of(cc * rh, 16), rh)


def _dma_sems(n):
    return [pltpu.SemaphoreType.DMA((n,)), pltpu.SemaphoreType.DMA((n,))]


def _gather_weights(name, shards):
    n = len(shards)

    def body(*refs):
        in_refs, out_refs, (send_sems, recv_sems) = refs[:n], refs[n:2 * n], refs[2 * n:]
        x, y, c, chips = _place()
        mine = 2 * x + y
        sibling = (x, y, 1 - c)
        first, passed = [], []
        for a, (in_ref, out_ref) in enumerate(zip(in_refs, out_refs)):
            mh = _half_rows(in_ref.shape[0], c)
            for j, (cx, cy) in enumerate(chips):
                first.append(_remote(in_ref.at[mh], out_ref.at[mine, mh], send_sems, recv_sems, 6 * a + j, (cx, cy, c)))
        for cp in first:
            cp.start()
        for j, (cx, cy) in enumerate(chips):
            for a, out_ref in enumerate(out_refs):
                landed = out_ref.at[2 * cx + cy, _half_rows(out_ref.shape[1], c)]
                _remote(landed, landed, send_sems, recv_sems, 6 * a + j, sibling).wait_recv()
                cp = _remote(landed, landed, send_sems, recv_sems, 6 * a + 3 + j, sibling)
                cp.start()
                passed.append(cp)
        for j, (cx, cy) in enumerate(chips):
            for a, out_ref in enumerate(out_refs):
                other = out_ref.at[2 * cx + cy, _half_rows(out_ref.shape[1], 1 - c)]
                _remote(other, other, send_sems, recv_sems, 6 * a + 3 + j, sibling).wait_recv()
        for cp in first + passed:
            cp.wait_send()

    return pl.pallas_call(
        body, name=name, in_specs=[HBM_SPEC] * n, out_specs=[HBM_SPEC] * n,
        out_shape=[jax.ShapeDtypeStruct((N_CHIPS,) + s_.shape, s_.dtype) for s_ in shards],
        scratch_shapes=_dma_sems(6 * n),
    )(*shards)


def _swap_halves(name, g4s):
    n = len(g4s)

    def body(*refs):
        in_refs, out_refs, (send_sems, recv_sems) = refs[:n], refs[n:2 * n], refs[2 * n:]
        x, y, c, _ = _place()
        cps = [_remote(in_ref.at[:, _half_rows(in_ref.shape[1], 1 - c), :], out_ref, send_sems, recv_sems, a, (x, y, 1 - c))
               for a, (in_ref, out_ref) in enumerate(zip(in_refs, out_refs))]
        for cp in cps:
            cp.start()
        for cp in cps:
            cp.wait()

    return pl.pallas_call(
        body, name=name, in_specs=[HBM_SPEC] * n, out_specs=[HBM_SPEC] * n,
        out_shape=[jax.ShapeDtypeStruct((g.shape[0], g.shape[1] // 2, g.shape[2]), g.dtype) for g in g4s],
        scratch_shapes=_dma_sems(n),
    )(*g4s)


def _add_half(name, g4, r1, cvec, tr):
    n, r, w = g4.shape
    rh = r // 2
    nblk = rh // tr

    def body(c_ref, a_ref, b_ref, o_ref):
        o_ref[...] = (a_ref[...] + b_ref[...]).astype(BF16)

    return pl.pallas_call(
        body, name=name,
        grid_spec=pltpu.PrefetchScalarGridSpec(
            num_scalar_prefetch=1, grid=(n, nblk),
            in_specs=[pl.BlockSpec((None, tr, w), lambda k, i, c_ref: (k, c_ref[0] * nblk + i, 0)),
                      pl.BlockSpec((None, tr, w), lambda k, i, c_ref: (k, i, 0))],
            out_specs=pl.BlockSpec((None, tr, w), lambda k, i, c_ref: (k, i, 0))),
        out_shape=jax.ShapeDtypeStruct((n, rh, w), BF16),
        compiler_params=_params("parallel", "parallel"),
    )(cvec, g4, r1)


def _scatter_chips(name, h4s):
    n = len(h4s)

    def body(*refs):
        in_refs, out_refs, (send_sems, recv_sems) = refs[:n], refs[n:2 * n], refs[2 * n:]
        x, y, c, chips = _place()
        sends = [_remote(in_ref.at[2 * cx + cy], out_ref.at[j], send_sems, recv_sems, 3 * a + j, (cx, cy, c))
                 for a, (in_ref, out_ref) in enumerate(zip(in_refs, out_refs)) for j, (cx, cy) in enumerate(chips)]
        for cp in sends:
            cp.start()
        for a, out_ref in enumerate(out_refs):
            for j, (cx, cy) in enumerate(chips):
                got = out_ref.at[j]
                _remote(got, got, send_sems, recv_sems, 3 * a + j, (cx, cy, c)).wait_recv()
        for cp in sends:
            cp.wait_send()

    return pl.pallas_call(
        body, name=name, in_specs=[HBM_SPEC] * n, out_specs=[HBM_SPEC] * n,
        out_shape=[jax.ShapeDtypeStruct((3,) + h.shape[1:], h.dtype) for h in h4s],
        scratch_shapes=_dma_sems(3 * n),
    )(*h4s)


def _sum_chips(name, h4, r3, mvec, tr):
    _, rh, w = h4.shape

    def body(m_ref, a_ref, b_ref, c_ref, d_ref, o_ref):
        o_ref[...] = ((a_ref[...].astype(F32) + b_ref[...].astype(F32)) + c_ref[...].astype(F32)) + d_ref[...].astype(F32)

    specs = [pl.BlockSpec((None, tr, w), lambda i, m_ref: (m_ref[0], i, 0))]
    specs += [pl.BlockSpec((None, tr, w), functools.partial(lambda k, i, m_ref: (k, i, 0), k)) for k in range(3)]
    return pl.pallas_call(
        body, name=name,
        grid_spec=pltpu.PrefetchScalarGridSpec(
            num_scalar_prefetch=1, grid=(rh // tr,), in_specs=specs,
            out_specs=pl.BlockSpec((tr, w), lambda i, m_ref: (i, 0))),
        out_shape=jax.ShapeDtypeStruct((rh, w), F32),
        compiler_params=_params("parallel"),
    )(mvec, h4, r3, r3, r3)


def _swap_reduced(name, ghs):
    n = len(ghs)

    def body(*refs):
        in_refs, out_refs, (send_sems, recv_sems) = refs[:n], refs[n:2 * n], refs[2 * n:]
        x, y, c, _ = _place()
        cps = [_remote(in_ref, out_ref, send_sems, recv_sems, a, (x, y, 1 - c))
               for a, (in_ref, out_ref) in enumerate(zip(in_refs, out_refs))]
        for cp in cps:
            cp.start()
        for cp in cps:
            cp.wait()

    return pl.pallas_call(
        body, name=name, in_specs=[HBM_SPEC] * n, out_specs=[HBM_SPEC] * n,
        out_shape=[jax.ShapeDtypeStruct(g.shape, g.dtype) for g in ghs],
        scratch_shapes=_dma_sems(n),
    )(*ghs)


def _allreduce_small(name, vec):
    rows, w = vec.shape

    def body(v_ref, o_ref, buf, send_sems, recv_sems):
        x, y, c, _ = _place()
        me = 4 * x + 2 * y + c
        buf[me] = v_ref[...]
        flips = [(fx, fy, fc) for fx in (0, 1) for fy in (0, 1) for fc in (0, 1)][1:]
        peers = [(x + fx - 2 * x * fx, y + fy - 2 * y * fy, c + fc - 2 * c * fc) for fx, fy, fc in flips]
        sends = [_remote(v_ref, buf.at[me], send_sems, recv_sems, k, peer) for k, peer in enumerate(peers)]
        for cp in sends:
            cp.start()
        for k, (px, py, pc) in enumerate(peers):
            got = buf.at[4 * px + 2 * py + pc]
            _remote(got, got, send_sems, recv_sems, k, (px, py, pc)).wait_recv()
        for cp in sends:
            cp.wait_send()
        acc = buf[0]
        for dev in range(1, N_DEV):
            acc = acc + buf[dev]
        o_ref[...] = acc

    vm = pl.BlockSpec(memory_space=pltpu.VMEM)
    return pl.pallas_call(
        body, name=name, in_specs=[vm], out_specs=vm, out_shape=jax.ShapeDtypeStruct((rows, w), F32),
        scratch_shapes=[pltpu.VMEM((N_DEV, rows, w), F32), pltpu.SemaphoreType.DMA((7,)), pltpu.SemaphoreType.DMA((7,))],
    )(vec)


GATE_COL = 3 * SB_WIDTH + 3 * FOX_WIDTH + FOX_HEADS + MEM_WIDTH
FL_COL = QKV_WIDTH


GROUP_A_COLS = [(0, QKV_WIDTH), (FL_COL + FOX_HEADS, MEM_WIDTH)]
GROUP_B_COLS = [(GATE_COL, MIX_WIDTH), (FL_COL, FOX_HEADS)]


def _group_from_shards(shard_of, cw, spans, pad):
    parts = []
    for lo, width in spans:
        hi = lo + width
        for j in range(N_CHIPS):
            a, b = max(lo, j * cw), min(hi, (j + 1) * cw)
            if a < b:
                parts.append(shard_of(j)[:, a - j * cw:b - j * cw])
    if pad:
        parts.append(jnp.zeros((parts[0].shape[0], pad), parts[0].dtype))
    return jnp.concatenate(parts, axis=1)


def _shard_from_groups(ga, gb, j, cw):
    lo, hi = j * cw, (j + 1) * cw
    placed = []
    for grp, spans in ((ga, GROUP_A_COLS), (gb, GROUP_B_COLS)):
        at = 0
        for first, width in spans:
            a, b = max(lo, first), min(hi, first + width)
            if a < b:
                placed.append((a, grp[:, at + a - first:at + b - first]))
            at += width
    return jnp.concatenate([p for _, p in sorted(placed, key=lambda t: t[0])], axis=1)


def _tile_of(n, cap, unit):
    if n <= cap:
        return n
    best = None
    for t in range(unit, cap + 1, unit):
        if n % t == 0:
            best = t
    assert best is not None, (n, cap, unit)
    return best


def _flat2(a):
    return a.reshape(-1, a.shape[-1])


def _pack_small(parts):
    rows = []
    for p in parts:
        f = p.reshape(-1).astype(F32)
        f = jnp.pad(f, (0, (-f.shape[0]) % LANES))
        rows.append(f.reshape(-1, LANES))
    out = jnp.concatenate(rows, axis=0)
    return jnp.pad(out, ((0, (-out.shape[0]) % 8), (0, 0)))


def _unpack_small(packed, shapes):
    outs, r = [], 0
    for shp in shapes:
        n = 1
        for s_ in shp:
            n *= s_
        nr = -(-n // LANES)
        outs.append(packed[r:r + nr].reshape(-1)[:n].reshape(shp))
        r += nr
    return outs


def kernel(x, mem, norm_w, w_in, b_forget, mem_norm_w, w_mem_kv, out_norm_w, w_out, final_norm_w, loss_target, m_norm_w, m_w_in, m_b_forget, m_mem_norm_w, m_w_mem_kv, m_out_norm_w, m_w_out, m_final_norm_w, v_norm_w, v_w_in, v_b_forget, v_mem_norm_w, v_w_mem_kv, v_out_norm_w, v_w_out, v_final_norm_w):
    xs = x[0]
    mems = mem[0]
    target = loss_target[0]
    s, d = xs.shape
    depth = norm_w.shape[0]
    nb = s // TILE
    ts = _tile_of(s, 256, 8)
    big = (w_in, w_mem_kv, w_out)
    core = lax.axis_index("c")
    chip = 2 * lax.axis_index("x") + lax.axis_index("y")
    cvec = core.astype(jnp.int32).reshape(1)
    mvec = chip.astype(jnp.int32).reshape(1)
    cw = w_in.shape[2]

    def gather_layer(l):
        own = [a[l].astype(BF16) for a in big]
        got = _gather_weights(f"gather_weights{l}", own)
        full = [jnp.where(lax.broadcasted_iota(jnp.int32, g.shape, 0) == chip, o[None], g) for g, o in zip(got, own)]
        shard_of = lambda j: full[0][j]
        wa_l = _group_from_shards(shard_of, cw, GROUP_A_COLS, 0)
        wb_l = _group_from_shards(shard_of, cw, GROUP_B_COLS, LANES - FOX_HEADS)
        return wa_l, wb_l, full[1].reshape(-1, full[1].shape[2]), full[2].reshape(-1, full[2].shape[2])

    layer_w = [gather_layer(l) for l in range(depth)]

    tm = _tile_of(s, 256, 8)
    fl_block = MIX_WIDTH // LANES

    saved = []
    cur = xs
    for l in range(depth):
        wa, wb, wkv, wout = layer_w[l]
        h = _rms_fwd(f"rms_fwd{l}", cur, norm_w[l][None], ts)
        pa = _mm(f"inproj_a{l}", h, wa, "nn", tm, _tile_of(PA, 1664, LANES), BF16)
        pb = _mm(f"inproj_b{l}", h, wb, "nn", tm, PB, F32)
        bpad = jnp.pad(b_forget[l], (0, LANES - FOX_HEADS))[None]
        ccol, crow = _gate_fwd(f"gate_fwd{l}", pb, bpad, fl_block)
        ccol4 = jnp.repeat(ccol[:, :FOX_HEADS].reshape(s, 4, 2).transpose(1, 0, 2), HEAD_DIM, axis=2)
        crow4 = jnp.pad(crow.reshape(nb, 4, 2, TILE).transpose(1, 0, 2, 3), ((0, 0), (0, 0), (0, 6), (0, 0)))
        o_sb = _sb_fwd(f"sb_fwd{l}", pa, 0)
        o_fx, lse_fx = _fox_fwd(f"fox_fwd{l}", pa, 3 * SB_WIDTH, ccol4, crow4)
        mn = _rms_fwd(f"mem_rms{l}", mems, mem_norm_w[l][None], mems.shape[0])
        mkv = _mm(f"mem_kv{l}", mn, wkv, "nn", mems.shape[0], 2 * MEM_WIDTH, BF16)
        o_m, lse_m = _mem_fwd(f"mem_fwd{l}", pa, mkv)
        nxt, y2 = _out_fwd(f"out_fwd{l}", o_sb, o_fx, o_m, pb, out_norm_w[l][None], cur, wout, ts)
        saved.append((cur, h, pa, pb, bpad, ccol4, crow4, o_sb, o_fx, lse_fx, mn, mkv, o_m, lse_m, y2))
        cur = nxt

    loss_v, dx, dxb, g_final = _final_loss("final_loss", cur, final_norm_w[None], target, ts)

    g_norm, g_b, g_memnorm, g_outnorm = [None] * depth, [None] * depth, [None] * depth, [None] * depth
    g_wa, g_wb, g_wkv, g_wout = [None] * depth, [None] * depth, [None] * depth, [None] * depth
    for l in reversed(range(depth)):
        xin, h, pa, pb, bpad, ccol4, crow4, o_sb, o_fx, lse_fx, mn, mkv, o_m, lse_m, y2 = saved[l]
        wa, wb, wkv, wout = layer_w[l]
        dy, dgate, g_outnorm[l] = _out_bwd(f"out_bwd{l}", dxb, o_sb, o_fx, o_m, pb, out_norm_w[l][None], wout, ts)
        g_wout[l] = _mm(f"dw_out{l}", y2, dxb, "tn", _tile_of(MIX_WIDTH, 640, LANES), d, F32)
        dq_sb, dk_sb, dv_sb = _sb_bwd(f"sb_bwd{l}", pa, 0, dy, 0)
        dq_fx, dk_fx, dv_fx, cs4 = _fox_bwd(f"fox_bwd{l}", pa, 3 * SB_WIDTH, ccol4, crow4, o_fx, lse_fx, dy, SB_WIDTH)
        colsum = cs4[:, :, :2, :].transpose(1, 0, 2, 3).reshape(nb, 8, TILE)
        dlogit, g_b[l] = _gate_bwd(f"gate_bwd{l}", pb, bpad, colsum, fl_block)
        dq_m, dk_m, dv_m = _mem_bwd(f"mem_bwd{l}", pa, mkv, o_m, lse_m, dy, SB_WIDTH + FOX_WIDTH)
        dmkv = jnp.concatenate([dk_m, dv_m], axis=1)
        g_wkv[l] = _mm(f"dw_kv{l}", mn, dmkv, "tn", d, 2 * MEM_WIDTH, F32)
        dmn = _mm(f"dmem{l}", dmkv, wkv, "nt", mems.shape[0], d, F32)
        g_memnorm[l] = _rms_wgrad(f"mem_norm_grad{l}", mems, dmn)
        dpa = jnp.concatenate([dq_sb, dk_sb, dv_sb, dq_fx, dk_fx, dv_fx, dq_m], axis=1)
        dpb = jnp.concatenate([dgate, dlogit], axis=1)
        tw = _tile_of(d, 512, LANES)
        g_wa[l] = _mm(f"dw_in_a{l}", h, dpa, "tn", tw, _tile_of(PA, 1664, LANES), F32)
        g_wb[l] = _mm(f"dw_in_b{l}", h, dpb, "tn", tw, PB, F32)
        dh = _mm(f"dh_a{l}", dpa, wa, "nt", tm, d, F32)
        dh = _mm(f"dh_b{l}", dpb, wb, "nt", tm, d, F32, res=dh)
        dx, dxb, g_norm[l] = _rms_bwd(f"rms_bwd{l}", xin, norm_w[l][None], dh, dx, ts)

    g_own = [[None] * depth for _ in big]
    g_other = [[None] * depth for _ in big]
    for l in reversed(range(depth)):
        g4s = [jnp.stack([_shard_from_groups(g_wa[l], g_wb[l], j, cw) for j in range(N_CHIPS)]),
               g_wkv[l].reshape(N_CHIPS, -1, g_wkv[l].shape[1]), g_wout[l].reshape(N_CHIPS, -1, d)]
        tiles = [_tile_of(g.shape[1] // 2, 256, 16) for g in g4s]
        from_sibling = _swap_halves(f"grad_swap_halves{l}", g4s)
        chip_sums = [_add_half(f"grad_add_half{l}_{k}", g, r, cvec, t)
                     for k, (g, r, t) in enumerate(zip(g4s, from_sibling, tiles))]
        from_chips = _scatter_chips(f"grad_scatter_chips{l}", chip_sums)
        halves = [_sum_chips(f"grad_sum_chips{l}_{k}", h_, r, mvec, t)
                  for k, (h_, r, t) in enumerate(zip(chip_sums, from_chips, tiles))]
        others = _swap_reduced(f"grad_swap_reduced{l}", halves)
        for k in range(len(big)):
            g_own[k][l], g_other[k][l] = halves[k], others[k]

    small_w = [norm_w, b_forget, mem_norm_w, out_norm_w, final_norm_w]
    small_m = [m_norm_w, m_b_forget, m_mem_norm_w, m_out_norm_w, m_final_norm_w]
    small_v = [v_norm_w, v_b_forget, v_mem_norm_w, v_out_norm_w, v_final_norm_w]
    small_shapes = [a.shape for a in small_w]
    local_small = [jnp.concatenate(g_norm, axis=0), jnp.stack([g[0, :FOX_HEADS] for g in g_b]),
                   jnp.concatenate(g_memnorm, axis=0), jnp.concatenate(g_outnorm, axis=0), g_final[0]]
    reduced = _allreduce_small("small_allreduce", _pack_small(local_small + [loss_v[0, :1]]))
    small_grads = _unpack_small(reduced, small_shapes)
    n_small_rows = sum(-(-a.size // LANES) for a in small_w)
    loss = reduced[n_small_rows, 0]

    d_s, m_s, v_s = _adamw("adamw_small", _pack_small(small_w), _pack_small(small_grads), _pack_small(small_m),
                           _pack_small(small_v), _pack_small(small_w).shape[0])
    small_delta, small_m2, small_v2 = (_unpack_small(a, small_shapes) for a in (d_s, m_s, v_s))
    big_grads, big_delta, big_m2, big_v2 = [], [], [], []
    for k, (nm, w_, m_, v_) in enumerate(zip(("w_in", "w_mem_kv", "w_out"), big, (m_w_in, m_w_mem_kv, m_w_out),
                                             (v_w_in, v_w_mem_kv, v_w_out))):
        outs = _adamw_sharded(f"adamw_{nm}", w_, m_, v_, g_own[k], g_other[k], cvec,
                              _tile_of(w_.shape[1] // 2, 256, 8))
        for lst, o in zip((big_grads, big_delta, big_m2, big_v2), outs):
            lst.append(o)

    def order(sm, bg):
        return [sm[0], bg[0], sm[1], sm[2], bg[1], sm[3], bg[2], sm[4]]

    return (loss, dx[None], *order(small_grads, big_grads), *order(small_delta, big_delta),
            *order(small_m2, big_m2), *order(small_v2, big_v2))
```

```python
import functools

import jax
import jax.numpy as jnp
from jax import lax
from jax.experimental import pallas as pl
from jax.experimental.pallas import tpu as pltpu

F32 = jnp.float32
BF16 = jnp.bfloat16

HEAD_DIM = 64
SB_WIDTH = 512
FOX_WIDTH = 512
FOX_HEADS = 8
MEM_WIDTH = 256
MIX_WIDTH = SB_WIDTH + FOX_WIDTH + MEM_WIDTH
TOTAL_HEADS = MIX_WIDTH // HEAD_DIM
IN_WIDTH = 3 * SB_WIDTH + 3 * FOX_WIDTH + FOX_HEADS + MEM_WIDTH + MIX_WIDTH
LANES = 128
QKV_WIDTH = 3 * SB_WIDTH + 3 * FOX_WIDTH
PA = QKV_WIDTH + MEM_WIDTH
PB = LANES + MIX_WIDTH
EPS = 1e-6
SCALE = HEAD_DIM ** -0.5
TILE = 256
NEG_INF = float("-inf")
MASKED = -1e30

ADAM_LR = 0.001
ADAM_B1 = 0.9
ADAM_B2 = 0.999
ADAM_EPS = 1e-08
ADAM_WD = 0.01
ADAM_STEP = 10

N_CHIPS = 4
N_DEV = 8
VMEM_LIMIT = 48 * 1024 * 1024
MESH = pl.DeviceIdType.MESH


def _params(*sem):
    return pltpu.CompilerParams(dimension_semantics=tuple(sem), vmem_limit_bytes=VMEM_LIMIT)


def _dot(a, b):
    return jnp.dot(a, b, preferred_element_type=F32)


def _dot_nt(a, b):
    return lax.dot_general(a, b, (((1,), (1,)), ((), ())), preferred_element_type=F32)


def _dot_tn(a, b):
    return lax.dot_general(a, b, (((0,), (0,)), ((), ())), preferred_element_type=F32)


def _split2(x):
    hi = x.astype(BF16)
    lo = (x - hi.astype(F32)).astype(BF16)
    return hi, lo


def _split3(x):
    hi = x.astype(BF16)
    r = x - hi.astype(F32)
    mid = r.astype(BF16)
    lo = (r - mid.astype(F32)).astype(BF16)
    return hi, mid, lo


def _sum_l2(x, u):
    hi, lo = _split2(x)
    return _dot(hi, u) + _dot(lo, u)


def _sum_l3(x, u):
    hi, mid, lo = _split3(x)
    return _dot(hi, u) + _dot(mid, u) + _dot(lo, u)


def _sum_r3(u, x):
    hi, mid, lo = _split3(x)
    return _dot(u, hi) + _dot(u, mid) + _dot(u, lo)


def _softplus(z):
    return jnp.maximum(z, 0.0) + jnp.log1p(jnp.exp(-jnp.abs(z)))


def _tri(n, pred):
    r = lax.broadcasted_iota(jnp.int32, (n, n), 0)
    c = lax.broadcasted_iota(jnp.int32, (n, n), 1)
    return jnp.where(pred(r, c), 1.0, 0.0).astype(BF16)


def _rows(ref, j, n=TILE):
    return pl.ds(pl.multiple_of(j * n, n), n)


def _mm(name, a, b, mode, tm, tn, out_dtype, res=None, a_lead=(), b_lead=()):
    a2, b2 = a.shape[len(a_lead):], b.shape[len(b_lead):]
    if mode == "tn":
        k, m = a2
    else:
        m, k = a2
    n = b2[0] if mode == "nt" else b2[1]
    assert m % tm == 0 and n % tn == 0, (name, m, tm, n, tn)
    na, nb = (None,) * len(a_lead), (None,) * len(b_lead)
    if mode == "tn":
        a_spec = pl.BlockSpec(na + (k, tm), lambda j, i: a_lead + (0, i))
    else:
        a_spec = pl.BlockSpec(na + (tm, k), lambda j, i: a_lead + (i, 0))
    if mode == "nt":
        b_spec = pl.BlockSpec(nb + (tn, k), lambda j, i: b_lead + (j, 0))
    else:
        b_spec = pl.BlockSpec(nb + (k, tn), lambda j, i: b_lead + (0, j))
    o_spec = pl.BlockSpec((tm, tn), lambda j, i: (i, j))
    dot = {"nn": _dot, "nt": _dot_nt, "tn": _dot_tn}[mode]

    def body(a_ref, b_ref, *rest):
        o_ref = rest[-1]
        acc = dot(a_ref[...].astype(BF16), b_ref[...].astype(BF16))
        if res is not None:
            acc = acc + rest[0][...]
        o_ref[...] = acc.astype(o_ref.dtype)

    args, specs = [a, b], [a_spec, b_spec]
    if res is not None:
        args.append(res)
        specs.append(o_spec)
    return pl.pallas_call(
        body, name=name, grid=(n // tn, m // tm), in_specs=specs, out_specs=o_spec,
        out_shape=jax.ShapeDtypeStruct((m, n), out_dtype),
        compiler_params=_params("parallel", "parallel"),
    )(*args)


def _rms_fwd(name, x, g, ts):
    s, d = x.shape

    def body(x_ref, g_ref, o_ref):
        xf = x_ref[...]
        r = lax.rsqrt(jnp.mean(xf * xf, axis=1, keepdims=True) + EPS)
        o_ref[...] = (xf * r * g_ref[...]).astype(BF16)

    return pl.pallas_call(
        body, name=name, grid=(s // ts,),
        in_specs=[pl.BlockSpec((ts, d), lambda i: (i, 0)), pl.BlockSpec((1, d), lambda i: (0, 0))],
        out_specs=pl.BlockSpec((ts, d), lambda i: (i, 0)),
        out_shape=jax.ShapeDtypeStruct((s, d), BF16),
        compiler_params=_params("parallel"),
    )(x, g)


def _rms_bwd(name, x, g, dh, dres, ts):
    s, d = x.shape

    def body(x_ref, g_ref, dh_ref, dres_ref, dx_ref, dxb_ref, dg_ref):
        @pl.when(pl.program_id(0) == 0)
        def _():
            dg_ref[...] = jnp.zeros_like(dg_ref)

        xf = x_ref[...]
        r = lax.rsqrt(jnp.mean(xf * xf, axis=1, keepdims=True) + EPS)
        xh = xf * r
        dhf = dh_ref[...]
        dg_ref[...] += jnp.sum(dhf * xh, axis=0, keepdims=True)
        dxh = dhf * g_ref[...]
        m = jnp.mean(dxh * xh, axis=1, keepdims=True)
        dx = r * (dxh - xh * m) + dres_ref[...]
        dx_ref[...] = dx
        dxb_ref[...] = dx.astype(BF16)

    row = pl.BlockSpec((ts, d), lambda i: (i, 0))
    vec = pl.BlockSpec((1, d), lambda i: (0, 0))
    return pl.pallas_call(
        body, name=name, grid=(s // ts,), in_specs=[row, vec, row, row], out_specs=[row, row, vec],
        out_shape=[jax.ShapeDtypeStruct((s, d), F32), jax.ShapeDtypeStruct((s, d), BF16),
                   jax.ShapeDtypeStruct((1, d), F32)],
        compiler_params=_params("arbitrary"),
    )(x, g, dh, dres)


def _rms_wgrad(name, x, dh):
    m_, d = x.shape

    def body(x_ref, dh_ref, dg_ref):
        xf = x_ref[...]
        r = lax.rsqrt(jnp.mean(xf * xf, axis=1, keepdims=True) + EPS)
        dg_ref[...] = jnp.sum(dh_ref[...] * xf * r, axis=0, keepdims=True)

    return pl.pallas_call(
        body, name=name, out_shape=jax.ShapeDtypeStruct((1, d), F32),
    )(x, dh)


def _final_loss(name, x, g, target, ts):
    s, d = x.shape

    def body(x_ref, g_ref, t_ref, loss_ref, dx_ref, dxb_ref, dg_ref):
        @pl.when(pl.program_id(0) == 0)
        def _():
            dg_ref[...] = jnp.zeros_like(dg_ref)
            loss_ref[...] = jnp.zeros_like(loss_ref)

        xf = x_ref[...]
        gw = g_ref[...]
        r = lax.rsqrt(jnp.mean(xf * xf, axis=1, keepdims=True) + EPS)
        xh = xf * r
        e = xh * gw - t_ref[...]
        part = 0.5 * jnp.sum(jnp.mean(e * e, axis=1, keepdims=True), axis=0, keepdims=True)
        loss_ref[...] += jnp.broadcast_to(part, loss_ref.shape)
        dy = e * (1.0 / d)
        dg_ref[...] += jnp.sum(dy * xh, axis=0, keepdims=True)
        dxh = dy * gw
        m = jnp.mean(dxh * xh, axis=1, keepdims=True)
        dx = r * (dxh - xh * m)
        dx_ref[...] = dx
        dxb_ref[...] = dx.astype(BF16)

    row = pl.BlockSpec((ts, d), lambda i: (i, 0))
    vec = pl.BlockSpec((1, d), lambda i: (0, 0))
    lvec = pl.BlockSpec((1, LANES), lambda i: (0, 0))
    return pl.pallas_call(
        body, name=name, grid=(s // ts,), in_specs=[row, vec, row], out_specs=[lvec, row, row, vec],
        out_shape=[jax.ShapeDtypeStruct((1, LANES), F32), jax.ShapeDtypeStruct((s, d), F32),
                   jax.ShapeDtypeStruct((s, d), BF16), jax.ShapeDtypeStruct((1, d), F32)],
        compiler_params=_params("arbitrary"),
    )(x, g, target)


def _gate_fwd(name, pb, bpad, fl_block):
    s = pb.shape[0]
    nb = s // TILE

    def body(fl_ref, b_ref, ccol_ref, crow_ref, carry):
        @pl.when(pl.program_id(0) == 0)
        def _():
            carry[...] = jnp.zeros_like(carry)

        u = fl_ref[...] + b_ref[...]
        lf = jnp.minimum(u, 0.0) - jnp.log1p(jnp.exp(-jnp.abs(u)))
        lower = _tri(TILE, lambda r, c: c <= r)
        c = _sum_r3(lower, lf) + carry[0:1, :]
        ccol_ref[...] = c
        crow_ref[0] = c.T[0:8, :]
        carry[...] = jnp.broadcast_to(c[TILE - 1:TILE, :], carry.shape)

    return pl.pallas_call(
        body, name=name, grid=(nb,),
        in_specs=[pl.BlockSpec((TILE, LANES), lambda i: (i, fl_block)), pl.BlockSpec((1, LANES), lambda i: (0, 0))],
        out_specs=[pl.BlockSpec((TILE, LANES), lambda i: (i, 0)), pl.BlockSpec((1, 8, TILE), lambda i: (i, 0, 0))],
        out_shape=[jax.ShapeDtypeStruct((s, LANES), F32), jax.ShapeDtypeStruct((nb, 8, TILE), F32)],
        scratch_shapes=[pltpu.VMEM((8, LANES), F32)],
        compiler_params=_params("arbitrary"),
    )(pb, bpad)


def _gate_bwd(name, pb, bpad, colsum, fl_block):
    s = pb.shape[0]
    nb = s // TILE

    def body(fl_ref, b_ref, cs_ref, dl_ref, db_ref, carry):
        @pl.when(pl.program_id(0) == 0)
        def _():
            carry[...] = jnp.zeros_like(carry)
            db_ref[...] = jnp.zeros_like(db_ref)

        upper = _tri(TILE, lambda r, c: r >= c)
        rsum = _sum_l3(cs_ref[0], upper) + carry[:, 0:1]
        carry[...] = jnp.broadcast_to(rsum[:, 0:1], carry.shape)
        full = jnp.concatenate([rsum, jnp.zeros((LANES - 8, TILE), F32)], axis=0)
        dlf = -full.T
        u = fl_ref[...] + b_ref[...]
        dlogit = dlf * (1.0 - jax.nn.sigmoid(u))
        dl_ref[...] = dlogit.astype(BF16)
        db_ref[...] += jnp.sum(dlogit, axis=0, keepdims=True)

    rev = lambda i: (nb - 1 - i, 0)
    return pl.pallas_call(
        body, name=name, grid=(nb,),
        in_specs=[pl.BlockSpec((TILE, LANES), lambda i: (nb - 1 - i, fl_block)),
                  pl.BlockSpec((1, LANES), lambda i: (0, 0)),
                  pl.BlockSpec((1, 8, TILE), lambda i: (nb - 1 - i, 0, 0))],
        out_specs=[pl.BlockSpec((TILE, LANES), rev), pl.BlockSpec((1, LANES), lambda i: (0, 0))],
        out_shape=[jax.ShapeDtypeStruct((s, LANES), BF16), jax.ShapeDtypeStruct((1, LANES), F32)],
        scratch_shapes=[pltpu.VMEM((8, LANES), F32)],
        compiler_params=_params("arbitrary"),
    )(pb, bpad, colsum)


def _head_slices(hh):
    return slice(HEAD_DIM * hh, HEAD_DIM * (hh + 1))


def _scaled_q(q_ref, sl, scale=SCALE):
    return (q_ref[:, sl].astype(F32) * scale).astype(BF16)


def _neg_abs(x):
    sign = jnp.uint32(0x80000000)
    return lax.bitcast_convert_type(lax.bitcast_convert_type(x, jnp.uint32) | sign, F32)


def _sb_tile(qn, kj, carry, strict, u_after, diag):
    nz = _dot_nt(qn, kj)
    lf = jnp.minimum(nz, 0.0) - jnp.log(1.0 + jnp.exp(_neg_abs(nz)))
    lsig = lf - nz
    if diag:
        lf = jnp.where(strict, lf, 0.0)
    sx = _dot(lf.astype(BF16), u_after)
    a = jnp.exp(lsig + sx + carry)
    if diag:
        a = jnp.where(strict, a, 0.0)
    return lsig, a, carry + sx[:, 0:1] + lf[:, 0:1]


def _pair_grid_call(name, body, nb, in_specs, out_specs, out_shape, scratch, args, carried=None):
    if carried is None:
        return pl.pallas_call(
            body, name=name, grid=(4, nb), in_specs=in_specs, out_specs=out_specs, out_shape=out_shape,
            scratch_shapes=scratch, compiler_params=_params("arbitrary", "arbitrary"),
        )(*args)
    n_in, n_out, n_ex = len(in_specs), len(out_specs), carried.n

    def body_with_copies(*refs):
        own_in, ex_in = refs[:n_in], refs[n_in:n_in + n_ex]
        own_out = refs[n_in + n_ex:n_in + n_ex + n_out]
        ex_out = refs[n_in + n_ex + n_out:n_in + 2 * n_ex + n_out]
        own_scratch, sems = refs[n_in + 2 * n_ex + n_out:-2], refs[-2:]
        parts = (ex_in, ex_out, sems[0], sems[1])
        p, i = pl.program_id(0), pl.program_id(1)
        pl.when(jnp.logical_and(p == 0, i == 0))(lambda: carried.begin(*parts))
        if carried.relay is not None:
            pl.when(jnp.logical_and(p == 3, i == 0))(lambda: carried.relay(*parts))
        body(*own_in, *own_out, *own_scratch)
        pl.when(jnp.logical_and(p == 3, i == nb - 1))(lambda: carried.finish(*parts))

    return pl.pallas_call(
        body_with_copies, name=name, grid=(4, nb), in_specs=list(in_specs) + [HBM_SPEC] * n_ex,
        out_specs=list(out_specs) + [HBM_SPEC] * n_ex, out_shape=list(out_shape) + carried.out_shapes,
        scratch_shapes=list(scratch) + _dma_sems(carried.n_sems),
        compiler_params=_params("arbitrary", "arbitrary"),
    )(*args, *carried.inputs)


def _sb_fwd(name, pa, col0, carried=None):
    s = pa.shape[0]
    nb = s // TILE
    cb = col0 // LANES

    def body(q_ref, k_ref, v_ref, o_ref, lsig_s, lf_s):
        i = pl.program_id(1)
        r = lax.broadcasted_iota(jnp.int32, (TILE, TILE), 0)
        c = lax.broadcasted_iota(jnp.int32, (TILE, TILE), 1)
        strict = c < r
        u_after = _tri(TILE, lambda rr, cc: rr > cc)
        qs = [_scaled_q(q_ref, _head_slices(hh), -SCALE) for hh in range(2)]

        def neg_z(j):
            kblk = k_ref[_rows(k_ref, j), :]
            return [_dot_nt(qs[hh], kblk[:, _head_slices(hh)]) for hh in range(2)]

        def scores(nzs, slot, diag):
            for hh, nz in enumerate(nzs):
                lf = jnp.minimum(nz, 0.0) - jnp.log(1.0 + jnp.exp(_neg_abs(nz)))
                lsig = lf - nz
                if diag:
                    lf = jnp.where(strict, lf, 0.0)
                    lsig = jnp.where(strict, lsig, MASKED)
                lsig_s[slot, hh] = lsig
                lf_s[slot, hh] = lf.astype(BF16)

        def weigh(j, slot, state):
            vblk = v_ref[_rows(v_ref, j), :]
            new = []
            for hh in range(2):
                carry, acc = state[hh]
                lfb = lf_s[slot, hh]
                sx = _dot(lfb, u_after)
                a = jnp.exp(lsig_s[slot, hh] + sx + carry)
                new.append((carry + sx[:, 0:1] + lfb[:, 0:1].astype(F32),
                            acc + _dot(a.astype(BF16), vblk[:, _head_slices(hh)])))
            return tuple(new)

        def step(t, state):
            state = weigh(i - t + 1, (t - 1) % 2, state)
            scores(neg_z(i - t), t % 2, False)
            return state

        zero = (jnp.zeros((TILE, 1), F32), jnp.zeros((TILE, HEAD_DIM), F32))
        scores(neg_z(i), 0, True)
        state = lax.fori_loop(1, i + 1, step, (zero, zero))
        state = weigh(0, i % 2, state)
        o_ref[...] = jnp.concatenate([state[0][1], state[1][1]], axis=1)

    outs = _pair_grid_call(
        name, body, nb,
        in_specs=[pl.BlockSpec((TILE, LANES), lambda p, i: (i, cb + p)),
                  pl.BlockSpec((s, LANES), lambda p, i: (0, cb + 4 + p)),
                  pl.BlockSpec((s, LANES), lambda p, i: (0, cb + 8 + p))],
        out_specs=[pl.BlockSpec((TILE, LANES), lambda p, i: (i, p))],
        out_shape=[jax.ShapeDtypeStruct((s, SB_WIDTH), F32)],
        scratch=[pltpu.VMEM((2, 2, TILE, TILE), F32), pltpu.VMEM((2, 2, TILE, TILE), BF16)],
        args=(pa, pa, pa), carried=carried)
    return outs[0], outs[1:]


def _sb_bwd(name, pa, col0, dout, dcol0, carried=None):
    s = pa.shape[0]
    nb = s // TILE
    cb = col0 // LANES
    db = dcol0 // LANES

    def body(q_ref, k_ref, v_ref, do_ref, dq_ref, dk_ref, dv_ref, dk_acc, dv_acc, dpan, span, gsum, lsig_s, lf_s):
        i = pl.program_id(1)

        @pl.when(i == 0)
        def _():
            dk_acc[...] = jnp.zeros_like(dk_acc)
            dv_acc[...] = jnp.zeros_like(dv_acc)

        r = lax.broadcasted_iota(jnp.int32, (TILE, TILE), 0)
        c = lax.broadcasted_iota(jnp.int32, (TILE, TILE), 1)
        strict = c < r
        u_after = _tri(TILE, lambda rr, cc: rr > cc)
        u_before = _tri(TILE, lambda rr, cc: rr < cc)
        qs = [_scaled_q(q_ref, _head_slices(hh), -SCALE) for hh in range(2)]
        dos = [do_ref[:, _head_slices(hh)].astype(BF16) for hh in range(2)]

        def scores(j, slot, diag):
            kblk = k_ref[_rows(k_ref, j), :]
            for hh in range(2):
                nz = _dot_nt(qs[hh], kblk[:, _head_slices(hh)])
                lf = jnp.minimum(nz, 0.0) - jnp.log(1.0 + jnp.exp(_neg_abs(nz)))
                lsig = lf - nz
                if diag:
                    lf = jnp.where(strict, lf, 0.0)
                    lsig = jnp.where(strict, lsig, MASKED)
                lsig_s[slot, hh] = lsig
                lf_s[slot, hh] = lf.astype(BF16)

        def grads(j, slot, carries):
            vblk = v_ref[_rows(v_ref, j), :]
            new = []
            for hh in range(2):
                lfb = lf_s[slot, hh]
                lsig = lsig_s[slot, hh]
                sx = _dot(lfb, u_after)
                a = jnp.exp(lsig + sx + carries[hh])
                g = a * _dot_nt(dos[hh], vblk[:, _head_slices(hh)])
                sig = jnp.exp(lsig)
                inside = _dot(g.astype(BF16), u_before)
                dpan[hh, j] = sig * (inside + g) - g
                span[hh, j] = sig
                gsum[hh, j] = inside[:, TILE - 1:TILE] + g[:, TILE - 1:TILE]
                dv_acc[hh, _rows(None, j), :] += _dot_tn(a.astype(BF16), dos[hh])
                new.append(carries[hh] + sx[:, 0:1] + lfb[:, 0:1].astype(F32))
            return tuple(new)

        def step1(t, carries):
            carries = grads(i - t + 1, (t - 1) % 2, carries)
            scores(i - t, t % 2, False)
            return carries

        zero1 = jnp.zeros((TILE, 1), F32)
        scores(i, 0, True)
        carries = lax.fori_loop(1, i + 1, step1, (zero1, zero1))
        grads(0, i % 2, carries)

        def pass2(j, state):
            kblk = k_ref[_rows(k_ref, j), :]
            new = []
            for hh in range(2):
                before, ndq = state[hh]
                ndzb = (dpan[hh, j] + span[hh, j] * before).astype(BF16)
                dk_acc[hh, _rows(None, j), :] += _dot_tn(ndzb, qs[hh])
                new.append((before + gsum[hh, j], ndq + _dot(ndzb, kblk[:, _head_slices(hh)])))
            return tuple(new)

        zero2 = (zero1, jnp.zeros((TILE, HEAD_DIM), F32))
        state = lax.fori_loop(0, i + 1, pass2, (zero2, zero2))
        dq_ref[...] = jnp.concatenate([state[0][1] * -SCALE, state[1][1] * -SCALE], axis=1).astype(BF16)

        @pl.when(i == nb - 1)
        def _():
            dk_ref[...] = jnp.concatenate([dk_acc[0], dk_acc[1]], axis=1).astype(BF16)
            dv_ref[...] = jnp.concatenate([dv_acc[0], dv_acc[1]], axis=1).astype(BF16)

    qspec = pl.BlockSpec((TILE, LANES), lambda p, i: (i, p))
    kvspec = pl.BlockSpec((s, LANES), lambda p, i: (0, p))
    out = jax.ShapeDtypeStruct((s, SB_WIDTH), BF16)
    outs = _pair_grid_call(
        name, body, nb,
        in_specs=[pl.BlockSpec((TILE, LANES), lambda p, i: (i, cb + p)),
                  pl.BlockSpec((s, LANES), lambda p, i: (0, cb + 4 + p)),
                  pl.BlockSpec((s, LANES), lambda p, i: (0, cb + 8 + p)),
                  pl.BlockSpec((TILE, LANES), lambda p, i: (i, db + p))],
        out_specs=[qspec, kvspec, kvspec], out_shape=[out, out, out],
        scratch=[pltpu.VMEM((2, s, HEAD_DIM), F32), pltpu.VMEM((2, s, HEAD_DIM), F32),
                 pltpu.VMEM((2, nb, TILE, TILE), F32), pltpu.VMEM((2, nb, TILE, TILE), F32),
                 pltpu.VMEM((2, nb, TILE, 1), F32),
                 pltpu.VMEM((2, 2, TILE, TILE), F32), pltpu.VMEM((2, 2, TILE, TILE), BF16)],
        args=(pa, pa, pa, dout), carried=carried)
    return outs[:3], outs[3:]


def _fox_scores(q, kj, cq, crj, causal, diag):
    sc = _dot_nt(q, kj) + (cq - crj)
    if diag:
        sc = jnp.where(causal, sc, NEG_INF)
    return sc


def _fox_fwd(name, pa, col0, ccol4, crow4):
    s = pa.shape[0]
    nb = s // TILE
    cb = col0 // LANES

    def body(q_ref, k_ref, v_ref, cc_ref, cr_ref, o_ref, lse_ref, sc_s):
        i = pl.program_id(1)
        r = lax.broadcasted_iota(jnp.int32, (TILE, TILE), 0)
        c = lax.broadcasted_iota(jnp.int32, (TILE, TILE), 1)
        causal = c <= r
        qs = [_scaled_q(q_ref, _head_slices(hh)) for hh in range(2)]
        cqs = [cc_ref[:, HEAD_DIM * hh:HEAD_DIM * hh + 1] for hh in range(2)]

        def logits(j, slot, diag):
            kblk = k_ref[_rows(k_ref, j), :]
            tops = []
            for hh in range(2):
                sc = _fox_scores(qs[hh], kblk[:, _head_slices(hh)], cqs[hh], cr_ref[j, hh:hh + 1, :], causal, diag)
                sc_s[slot, hh] = sc
                tops.append(jnp.max(sc, axis=1, keepdims=True))
            return tuple(tops)

        def update(j, slot, tops, state):
            vblk = v_ref[_rows(v_ref, j), :]
            new = []
            for hh in range(2):
                m, l, acc = state[hh]
                m2 = jnp.maximum(m, tops[hh])
                alpha = jnp.exp(m - m2)
                p = jnp.exp(sc_s[slot, hh] - m2)
                new.append((m2, l * alpha + jnp.sum(p, axis=1, keepdims=True),
                            acc * alpha + _dot(p.astype(BF16), vblk[:, _head_slices(hh)])))
            return tuple(new)

        def step(t, both):
            tops, state = both
            state = update(i - t + 1, (t - 1) % 2, tops, state)
            return logits(i - t, t % 2, False), state

        zero = (jnp.full((TILE, 1), NEG_INF, F32), jnp.zeros((TILE, 1), F32), jnp.zeros((TILE, HEAD_DIM), F32))
        tops, state = lax.fori_loop(1, i + 1, step, (logits(i, 0, True), (zero, zero)))
        state = update(0, i % 2, tops, state)
        o_ref[...] = jnp.concatenate([state[hh][2] / state[hh][1] for hh in range(2)], axis=1)
        lse_ref[...] = jnp.concatenate(
            [jnp.broadcast_to(state[hh][0] + jnp.log(state[hh][1]), (TILE, HEAD_DIM)) for hh in range(2)], axis=1)

    return pl.pallas_call(
        body, name=name, grid=(4, nb),
        in_specs=[pl.BlockSpec((TILE, LANES), lambda p, i: (i, cb + p)),
                  pl.BlockSpec((s, LANES), lambda p, i: (0, cb + 4 + p)),
                  pl.BlockSpec((s, LANES), lambda p, i: (0, cb + 8 + p)),
                  pl.BlockSpec((None, TILE, LANES), lambda p, i: (p, i, 0)),
                  pl.BlockSpec((None, nb, 8, TILE), lambda p, i: (p, 0, 0, 0))],
        out_specs=[pl.BlockSpec((TILE, LANES), lambda p, i: (i, p)),
                   pl.BlockSpec((None, TILE, LANES), lambda p, i: (p, i, 0))],
        out_shape=[jax.ShapeDtypeStruct((s, FOX_WIDTH), F32), jax.ShapeDtypeStruct((4, s, LANES), F32)],
        scratch_shapes=[pltpu.VMEM((2, 2, TILE, TILE), F32)],
        compiler_params=_params("parallel", "arbitrary"),
    )(pa, pa, pa, ccol4, crow4)


def _fox_bwd(name, pa, col0, ccol4, crow4, out, lse, dout, dcol0):
    s = pa.shape[0]
    nb = s // TILE
    cb = col0 // LANES
    db = dcol0 // LANES

    def body(q_ref, k_ref, v_ref, cc_ref, cr_ref, o_ref, lse_ref, do_ref,
             dq_ref, dk_ref, dv_ref, cs_ref, dk_acc, dv_acc, p_s, ds_s):
        i = pl.program_id(1)

        @pl.when(i == 0)
        def _():
            dk_acc[...] = jnp.zeros_like(dk_acc)
            dv_acc[...] = jnp.zeros_like(dv_acc)
            cs_ref[...] = jnp.zeros_like(cs_ref)

        r = lax.broadcasted_iota(jnp.int32, (TILE, TILE), 0)
        c = lax.broadcasted_iota(jnp.int32, (TILE, TILE), 1)
        causal = c <= r
        qs = [_scaled_q(q_ref, _head_slices(hh)) for hh in range(2)]
        cqs = [cc_ref[:, HEAD_DIM * hh:HEAD_DIM * hh + 1] for hh in range(2)]
        lses = [lse_ref[:, HEAD_DIM * hh:HEAD_DIM * hh + 1] for hh in range(2)]
        dofs = [do_ref[:, _head_slices(hh)] for hh in range(2)]
        dos = [d_.astype(BF16) for d_ in dofs]
        deltas = [jnp.sum(dofs[hh] * o_ref[:, _head_slices(hh)], axis=1, keepdims=True) for hh in range(2)]

        def probs(j, slot, rowsums, diag):
            kblk = k_ref[_rows(k_ref, j), :]
            vblk = v_ref[_rows(v_ref, j), :]
            new = []
            for hh in range(2):
                sl = _head_slices(hh)
                sc = _fox_scores(qs[hh], kblk[:, sl], cqs[hh], cr_ref[j, hh:hh + 1, :], causal, diag)
                p = jnp.exp(sc - lses[hh])
                ds = p * (_dot_nt(dos[hh], vblk[:, sl]) - deltas[hh])
                p_s[slot, hh] = p.astype(BF16)
                ds_s[slot, hh] = ds.astype(BF16)
                cs_ref[j, hh:hh + 1, :] += jnp.sum(ds, axis=0, keepdims=True)
                new.append(rowsums[hh] + jnp.sum(ds, axis=1, keepdims=True))
            return tuple(new)

        def accumulate(j, slot, dqs):
            kblk = k_ref[_rows(k_ref, j), :]
            new = []
            for hh in range(2):
                dsb = ds_s[slot, hh]
                dv_acc[hh, _rows(None, j), :] += _dot_tn(p_s[slot, hh], dos[hh])
                dk_acc[hh, _rows(None, j), :] += _dot_tn(dsb, qs[hh])
                new.append(dqs[hh] + _dot(dsb, kblk[:, _head_slices(hh)]))
            return tuple(new)

        def step(t, both):
            rowsums, dqs = both
            dqs = accumulate(i - t + 1, (t - 1) % 2, dqs)
            return probs(i - t, t % 2, rowsums, False), dqs

        zero1 = jnp.zeros((TILE, 1), F32)
        zero64 = jnp.zeros((TILE, HEAD_DIM), F32)
        rowsums, dqs = lax.fori_loop(1, i + 1, step, (probs(i, 0, (zero1, zero1), True), (zero64, zero64)))
        dqs = accumulate(0, i % 2, dqs)
        for hh in range(2):
            cs_ref[i, hh:hh + 1, :] -= jnp.broadcast_to(rowsums[hh], (TILE, LANES)).T[0:1, :]
        dq_ref[...] = jnp.concatenate([dqs[0] * SCALE, dqs[1] * SCALE], axis=1).astype(BF16)

        @pl.when(i == nb - 1)
        def _():
            dk_ref[...] = jnp.concatenate([dk_acc[0], dk_acc[1]], axis=1).astype(BF16)
            dv_ref[...] = jnp.concatenate([dv_acc[0], dv_acc[1]], axis=1).astype(BF16)

    qspec = pl.BlockSpec((TILE, LANES), lambda p, i: (i, p))
    kvspec = pl.BlockSpec((s, LANES), lambda p, i: (0, p))
    o3 = jax.ShapeDtypeStruct((s, FOX_WIDTH), BF16)
    return pl.pallas_call(
        body, name=name, grid=(4, nb),
        in_specs=[pl.BlockSpec((TILE, LANES), lambda p, i: (i, cb + p)),
                  pl.BlockSpec((s, LANES), lambda p, i: (0, cb + 4 + p)),
                  pl.BlockSpec((s, LANES), lambda p, i: (0, cb + 8 + p)),
                  pl.BlockSpec((None, TILE, LANES), lambda p, i: (p, i, 0)),
                  pl.BlockSpec((None, nb, 8, TILE), lambda p, i: (p, 0, 0, 0)),
                  qspec,
                  pl.BlockSpec((None, TILE, LANES), lambda p, i: (p, i, 0)),
                  pl.BlockSpec((TILE, LANES), lambda p, i: (i, db + p))],
        out_specs=[qspec, kvspec, kvspec, pl.BlockSpec((None, nb, 8, TILE), lambda p, i: (p, 0, 0, 0))],
        out_shape=[o3, o3, o3, jax.ShapeDtypeStruct((4, nb, 8, TILE), F32)],
        scratch_shapes=[pltpu.VMEM((2, s, HEAD_DIM), F32), pltpu.VMEM((2, s, HEAD_DIM), F32),
                        pltpu.VMEM((2, 2, TILE, TILE), BF16), pltpu.VMEM((2, 2, TILE, TILE), BF16)],
        compiler_params=_params("arbitrary", "arbitrary"),
    )(pa, pa, pa, ccol4, crow4, out, lse, dout)


def _mem_fwd(name, pa, mkv):
    s = pa.shape[0]
    ml = mkv.shape[0]
    nb = s // TILE
    cb = QKV_WIDTH // LANES

    def body(q_ref, k_ref, v_ref, o_ref, lse_ref):
        outs, lses = [], []
        for hh in range(2):
            sl = _head_slices(hh)
            sc = _dot_nt(_scaled_q(q_ref, sl), k_ref[:, sl])
            m = jnp.max(sc, axis=1, keepdims=True)
            p = jnp.exp(sc - m)
            l = jnp.sum(p, axis=1, keepdims=True)
            outs.append(_dot(p.astype(BF16), v_ref[:, sl]) / l)
            lses.append(jnp.broadcast_to(m + jnp.log(l), (TILE, HEAD_DIM)))
        o_ref[...] = jnp.concatenate(outs, axis=1)
        lse_ref[...] = jnp.concatenate(lses, axis=1)

    return pl.pallas_call(
        body, name=name, grid=(2, nb),
        in_specs=[pl.BlockSpec((TILE, LANES), lambda p, i: (i, cb + p)),
                  pl.BlockSpec((ml, LANES), lambda p, i: (0, p)),
                  pl.BlockSpec((ml, LANES), lambda p, i: (0, 2 + p))],
        out_specs=[pl.BlockSpec((TILE, LANES), lambda p, i: (i, p)),
                   pl.BlockSpec((None, TILE, LANES), lambda p, i: (p, i, 0))],
        out_shape=[jax.ShapeDtypeStruct((s, MEM_WIDTH), F32), jax.ShapeDtypeStruct((2, s, LANES), F32)],
        compiler_params=_params("parallel", "parallel"),
    )(pa, mkv, mkv)


def _mem_bwd(name, pa, mkv, out, lse, dout, dcol0):
    s = pa.shape[0]
    ml = mkv.shape[0]
    nb = s // TILE
    cb = QKV_WIDTH // LANES
    db = dcol0 // LANES

    def body(q_ref, k_ref, v_ref, o_ref, lse_ref, do_ref, dq_ref, dk_ref, dv_ref, dk_acc, dv_acc):
        i = pl.program_id(1)

        @pl.when(i == 0)
        def _():
            dk_acc[...] = jnp.zeros_like(dk_acc)
            dv_acc[...] = jnp.zeros_like(dv_acc)

        dqs = []
        for hh in range(2):
            sl = _head_slices(hh)
            q = _scaled_q(q_ref, sl)
            kh = k_ref[:, sl]
            dof = do_ref[:, sl]
            do = dof.astype(BF16)
            delta = jnp.sum(dof * o_ref[:, sl], axis=1, keepdims=True)
            p = jnp.exp(_dot_nt(q, kh) - lse_ref[:, HEAD_DIM * hh:HEAD_DIM * hh + 1])
            ds = (p * (_dot_nt(do, v_ref[:, sl]) - delta)).astype(BF16)
            dv_acc[hh] += _dot_tn(p.astype(BF16), do)
            dk_acc[hh] += _dot_tn(ds, q)
            dqs.append(_dot(ds, kh) * SCALE)
        dq_ref[...] = jnp.concatenate(dqs, axis=1).astype(BF16)

        @pl.when(i == nb - 1)
        def _():
            dk_ref[...] = jnp.concatenate([dk_acc[0], dk_acc[1]], axis=1).astype(BF16)
            dv_ref[...] = jnp.concatenate([dv_acc[0], dv_acc[1]], axis=1).astype(BF16)

    qspec = pl.BlockSpec((TILE, LANES), lambda p, i: (i, p))
    kvspec = pl.BlockSpec((ml, LANES), lambda p, i: (0, p))
    okv = jax.ShapeDtypeStruct((ml, MEM_WIDTH), BF16)
    return pl.pallas_call(
        body, name=name, grid=(2, nb),
        in_specs=[pl.BlockSpec((TILE, LANES), lambda p, i: (i, cb + p)),
                  pl.BlockSpec((ml, LANES), lambda p, i: (0, p)),
                  pl.BlockSpec((ml, LANES), lambda p, i: (0, 2 + p)),
                  qspec,
                  pl.BlockSpec((None, TILE, LANES), lambda p, i: (p, i, 0)),
                  pl.BlockSpec((TILE, LANES), lambda p, i: (i, db + p))],
        out_specs=[qspec, kvspec, kvspec],
        out_shape=[jax.ShapeDtypeStruct((s, MEM_WIDTH), BF16), okv, okv],
        scratch_shapes=[pltpu.VMEM((2, ml, HEAD_DIM), F32), pltpu.VMEM((2, ml, HEAD_DIM), F32)],
        compiler_params=_params("arbitrary", "arbitrary"),
    )(pa, mkv, mkv, out, lse, dout)


def _head_maps():
    col = jnp.arange(MIX_WIDTH)[:, None] // HEAD_DIM
    g = (col == jnp.arange(LANES)[None, :]).astype(BF16)
    return g, g.T


def _normed_heads(osb_ref, ofx_ref, om_ref, g_ref, gt_ref):
    y = jnp.concatenate([osb_ref[...], ofx_ref[...], om_ref[...]], axis=1)
    msq = _sum_l3(y * y, g_ref[...]) * (1.0 / HEAD_DIM)
    rf = _sum_l3(lax.rsqrt(msq + EPS), gt_ref[...])
    return y * rf, rf


def _out_fwd(name, o_sb, o_fx, o_m, pb, ow, x, w_out, ts):
    s, d = x.shape
    g, gt = _head_maps()

    def body(osb_ref, ofx_ref, om_ref, gate_ref, ow_ref, x_ref, w_ref, g_ref, gt_ref, xo_ref, y2_ref):
        yh, _ = _normed_heads(osb_ref, ofx_ref, om_ref, g_ref, gt_ref)
        gate = gate_ref[...]
        y2 = (yh * ow_ref[...] * (gate * jax.nn.sigmoid(gate))).astype(BF16)
        y2_ref[...] = y2
        xo_ref[...] = x_ref[...] + _dot(y2, w_ref[...])

    return pl.pallas_call(
        body, name=name, grid=(s // ts,),
        in_specs=[_row_spec(ts, SB_WIDTH), _row_spec(ts, FOX_WIDTH), _row_spec(ts, MEM_WIDTH),
                  _row_spec(ts, MIX_WIDTH), _const_spec((1, MIX_WIDTH)), _row_spec(ts, d),
                  _const_spec((MIX_WIDTH, d)),
                  _const_spec((MIX_WIDTH, LANES)), _const_spec((LANES, MIX_WIDTH))],
        out_specs=[_row_spec(ts, d), _row_spec(ts, MIX_WIDTH)],
        out_shape=[jax.ShapeDtypeStruct((s, d), F32), jax.ShapeDtypeStruct((s, MIX_WIDTH), BF16)],
        compiler_params=_params("parallel"),
    )(o_sb, o_fx, o_m, pb, ow, x, w_out, g, gt)


def _row_spec(ts, w):
    return pl.BlockSpec((ts, w), lambda i: (i, 0))


def _const_spec(shape):
    return pl.BlockSpec(shape, lambda i: (0,) * len(shape))


def _out_bwd(name, dxb, o_sb, o_fx, o_m, pb, ow, w_out, ts):
    s, d = dxb.shape
    g, gt = _head_maps()

    def body(dx_ref, osb_ref, ofx_ref, om_ref, gate_ref, ow_ref, w_ref, g_ref, gt_ref, dy_ref, dgate_ref, dow_ref):
        @pl.when(pl.program_id(0) == 0)
        def _():
            dow_ref[...] = jnp.zeros_like(dow_ref)

        dy2 = _dot_nt(dx_ref[...], w_ref[...])
        yh, rf = _normed_heads(osb_ref, ofx_ref, om_ref, g_ref, gt_ref)
        gate = gate_ref[...]
        sig = jax.nn.sigmoid(gate)
        ow_v = ow_ref[...]
        dgate_ref[...] = (dy2 * (yh * ow_v) * (sig * (1.0 + gate * (1.0 - sig)))).astype(BF16)
        dn = dy2 * (gate * sig)
        dow_ref[...] += jnp.sum(dn * yh, axis=0, keepdims=True)
        dyh = dn * ow_v
        t = _sum_l3(dyh * yh, g_ref[...]) * (1.0 / HEAD_DIM)
        dy_ref[...] = rf * (dyh - yh * _sum_l3(t, gt_ref[...]))

    return pl.pallas_call(
        body, name=name, grid=(s // ts,),
        in_specs=[_row_spec(ts, d), _row_spec(ts, SB_WIDTH), _row_spec(ts, FOX_WIDTH), _row_spec(ts, MEM_WIDTH),
                  _row_spec(ts, MIX_WIDTH), _const_spec((1, MIX_WIDTH)),
                  _const_spec((MIX_WIDTH, d)),
                  _const_spec((MIX_WIDTH, LANES)), _const_spec((LANES, MIX_WIDTH))],
        out_specs=[_row_spec(ts, MIX_WIDTH), _row_spec(ts, MIX_WIDTH), _const_spec((1, MIX_WIDTH))],
        out_shape=[jax.ShapeDtypeStruct((s, MIX_WIDTH), F32), jax.ShapeDtypeStruct((s, MIX_WIDTH), BF16),
                   jax.ShapeDtypeStruct((1, MIX_WIDTH), F32)],
        compiler_params=_params("arbitrary"),
    )(dxb, o_sb, o_fx, o_m, pb, ow, w_out, g, gt)


def _adamw(name, w, g, m, v, tr):
    rows, cols = w.shape

    def body(w_ref, g_ref, m_ref, v_ref, d_ref, m2_ref, v2_ref):
        gv = g_ref[...]
        m2 = ADAM_B1 * m_ref[...] + (1.0 - ADAM_B1) * gv
        v2 = ADAM_B2 * v_ref[...] + (1.0 - ADAM_B2) * (gv * gv)
        m_hat = m2 / (1.0 - ADAM_B1 ** ADAM_STEP)
        v_hat = v2 / (1.0 - ADAM_B2 ** ADAM_STEP)
        d_ref[...] = -ADAM_LR * (m_hat / (jnp.sqrt(v_hat) + ADAM_EPS) + ADAM_WD * w_ref[...])
        m2_ref[...] = m2
        v2_ref[...] = v2

    spec = _row_spec(tr, cols)
    shp = jax.ShapeDtypeStruct((rows, cols), F32)
    return pl.pallas_call(
        body, name=name, grid=(rows // tr,), in_specs=[spec] * 4, out_specs=[spec] * 3, out_shape=[shp] * 3,
        compiler_params=_params("parallel"),
    )(w, g, m, v)


def _adamw_sharded(name, w, m, v, g_own, g_other, cvec, tr):
    depth, rows, cols = w.shape
    nt = rows // 2 // tr

    def body(c_ref, w_ref, m_ref, v_ref, *rest):
        g_refs, (g_ref, d_ref, m2_ref, v2_ref) = rest[:2 * depth], rest[2 * depth:]
        layer, mine = pl.program_id(0), pl.program_id(1) == c_ref[0]
        gv = None
        for lt in range(depth):
            cand = jnp.where(mine, g_refs[lt][...], g_refs[depth + lt][...])
            gv = cand if gv is None else jnp.where(layer == lt, cand, gv)
        m2 = ADAM_B1 * m_ref[...] + (1.0 - ADAM_B1) * gv
        v2 = ADAM_B2 * v_ref[...] + (1.0 - ADAM_B2) * (gv * gv)
        m_hat = m2 / (1.0 - ADAM_B1 ** ADAM_STEP)
        v_hat = v2 / (1.0 - ADAM_B2 ** ADAM_STEP)
        g_ref[...] = gv
        d_ref[...] = -ADAM_LR * (m_hat / (jnp.sqrt(v_hat) + ADAM_EPS) + ADAM_WD * w_ref[...])
        m2_ref[...] = m2
        v2_ref[...] = v2

    def g_map(lt, own):
        def index(l, hf, i, c_ref):
            use = jnp.logical_and(l == lt, (hf == c_ref[0]) == own)
            return jnp.where(use, i, 0), 0
        return index

    full = pl.BlockSpec((None, tr, cols), lambda l, hf, i, c_ref: (l, hf * nt + i, 0))
    g_specs = [pl.BlockSpec((tr, cols), g_map(lt, own)) for own in (True, False) for lt in range(depth)]
    shp = jax.ShapeDtypeStruct((depth, rows, cols), F32)
    return pl.pallas_call(
        body, name=name,
        grid_spec=pltpu.PrefetchScalarGridSpec(
            num_scalar_prefetch=1, grid=(depth, 2, nt), in_specs=[full] * 3 + g_specs, out_specs=[full] * 4),
        out_shape=[shp] * 4,
        compiler_params=_params("arbitrary", "arbitrary", "arbitrary"),
    )(cvec, w, m, v, *g_own, *g_other)


HBM_SPEC = pl.BlockSpec(memory_space=pltpu.HBM)


def _place():
    x, y, c = lax.axis_index("x"), lax.axis_index("y"), lax.axis_index("c")
    chips = [(1 - x, y), (x, 1 - y), (1 - x, 1 - y)]
    return x, y, c, chips


def _remote(src, dst, send_sems, recv_sems, k, to):
    return pltpu.make_async_remote_copy(src_ref=src, dst_ref=dst, send_sem=send_sems.at[k], recv_sem=recv_sems.at[k],
                                        device_id=to, device_id_type=MESH)


def _half_rows(n_rows, cc):
    rh = n_rows // 2
    return pl.ds(pl.multiple_of(cc * rh, 16), rh)


def _dma_sems(n):
    return [pltpu.SemaphoreType.DMA((n,)), pltpu.SemaphoreType.DMA((n,))]


class _Exchange:
    def __init__(self, inputs, out_shapes, n_sems, begin, relay, finish):
        self.inputs, self.out_shapes, self.n_sems = list(inputs), list(out_shapes), n_sems
        self.begin, self.relay, self.finish = begin, relay, finish

    @property
    def n(self):
        return len(self.inputs)

    def split(self, refs):
        return refs[:self.n], refs[self.n:2 * self.n], refs[2 * self.n], refs[2 * self.n + 1]


def _run_exchange(name, ex):
    def body(*refs):
        parts = ex.split(refs)
        for phase in (ex.begin, ex.relay, ex.finish):
            if phase is not None:
                phase(*parts)

    return pl.pallas_call(
        body, name=name, in_specs=[HBM_SPEC] * ex.n, out_specs=[HBM_SPEC] * ex.n, out_shape=ex.out_shapes,
        scratch_shapes=_dma_sems(ex.n_sems),
    )(*ex.inputs)


def _gather_exchange(shards):
    def ici(in_refs, out_refs, send_sems, recv_sems):
        x, y, c, chips = _place()
        return [_remote(in_ref.at[_half_rows(in_ref.shape[0], c)], out_ref.at[2 * x + y, _half_rows(in_ref.shape[0], c)],
                        send_sems, recv_sems, 6 * a + j, (cx, cy, c))
                for a, (in_ref, out_ref) in enumerate(zip(in_refs, out_refs)) for j, (cx, cy) in enumerate(chips)]

    def d2d(out_refs, send_sems, recv_sems, half_of):
        x, y, c, chips = _place()
        cps = []
        for a, out_ref in enumerate(out_refs):
            for j, (cx, cy) in enumerate(chips):
                piece = out_ref.at[2 * cx + cy, _half_rows(out_ref.shape[1], half_of(c))]
                cps.append(_remote(piece, piece, send_sems, recv_sems, 6 * a + 3 + j, (x, y, 1 - c)))
        return cps

    def begin(in_refs, out_refs, send_sems, recv_sems):
        for cp in ici(in_refs, out_refs, send_sems, recv_sems):
            cp.start()

    def relay(in_refs, out_refs, send_sems, recv_sems):
        x, y, c, chips = _place()
        for a, out_ref in enumerate(out_refs):
            for j, (cx, cy) in enumerate(chips):
                landed = out_ref.at[2 * cx + cy, _half_rows(out_ref.shape[1], c)]
                _remote(landed, landed, send_sems, recv_sems, 6 * a + j, (cx, cy, c)).wait_recv()
        for cp in d2d(out_refs, send_sems, recv_sems, lambda c_: c_):
            cp.start()

    def finish(in_refs, out_refs, send_sems, recv_sems):
        for cp in d2d(out_refs, send_sems, recv_sems, lambda c_: 1 - c_):
            cp.wait_recv()
        for cp in ici(in_refs, out_refs, send_sems, recv_sems) + d2d(out_refs, send_sems, recv_sems, lambda c_: c_):
            cp.wait_send()

    shapes = [jax.ShapeDtypeStruct((N_CHIPS,) + s_.shape, s_.dtype) for s_ in shards]
    return _Exchange(shards, shapes, 6 * len(shards), begin, relay, finish)


def _swap_halves(name, g4s):
    n = len(g4s)

    def body(*refs):
        in_refs, out_refs, (send_sems, recv_sems) = refs[:n], refs[n:2 * n], refs[2 * n:]
        x, y, c, _ = _place()
        cps = [_remote(in_ref.at[:, _half_rows(in_ref.shape[1], 1 - c), :], out_ref, send_sems, recv_sems, a, (x, y, 1 - c))
               for a, (in_ref, out_ref) in enumerate(zip(in_refs, out_refs))]
        for cp in cps:
            cp.start()
        for cp in cps:
            cp.wait()

    return pl.pallas_call(
        body, name=name, in_specs=[HBM_SPEC] * n, out_specs=[HBM_SPEC] * n,
        out_shape=[jax.ShapeDtypeStruct((g.shape[0], g.shape[1] // 2, g.shape[2]), g.dtype) for g in g4s],
        scratch_shapes=_dma_sems(n),
    )(*g4s)


def _add_half(name, g4, r1, cvec, tr):
    n, r, w = g4.shape
    rh = r // 2
    nblk = rh // tr

    def body(c_ref, a_ref, b_ref, o_ref):
        o_ref[...] = (a_ref[...] + b_ref[...]).astype(BF16)

    return pl.pallas_call(
        body, name=name,
        grid_spec=pltpu.PrefetchScalarGridSpec(
            num_scalar_prefetch=1, grid=(n, nblk),
            in_specs=[pl.BlockSpec((None, tr, w), lambda k, i, c_ref: (k, c_ref[0] * nblk + i, 0)),
                      pl.BlockSpec((None, tr, w), lambda k, i, c_ref: (k, i, 0))],
            out_specs=pl.BlockSpec((None, tr, w), lambda k, i, c_ref: (k, i, 0))),
        out_shape=jax.ShapeDtypeStruct((n, rh, w), BF16),
        compiler_params=_params("parallel", "parallel"),
    )(cvec, g4, r1)


def _scatter_exchange(h4s):
    def sends(in_refs, out_refs, send_sems, recv_sems):
        x, y, c, chips = _place()
        return [_remote(in_ref.at[2 * cx + cy], out_ref.at[j], send_sems, recv_sems, 3 * a + j, (cx, cy, c))
                for a, (in_ref, out_ref) in enumerate(zip(in_refs, out_refs)) for j, (cx, cy) in enumerate(chips)]

    def begin(*parts):
        for cp in sends(*parts):
            cp.start()

    def finish(in_refs, out_refs, send_sems, recv_sems):
        x, y, c, chips = _place()
        for a, out_ref in enumerate(out_refs):
            for j, (cx, cy) in enumerate(chips):
                got = out_ref.at[j]
                _remote(got, got, send_sems, recv_sems, 3 * a + j, (cx, cy, c)).wait_recv()
        for cp in sends(in_refs, out_refs, send_sems, recv_sems):
            cp.wait_send()

    shapes = [jax.ShapeDtypeStruct((3,) + h.shape[1:], h.dtype) for h in h4s]
    return _Exchange(h4s, shapes, 3 * len(h4s), begin, None, finish)


def _sum_chips(name, h4, r3, mvec, tr):
    _, rh, w = h4.shape

    def body(m_ref, a_ref, b_ref, c_ref, d_ref, o_ref):
        o_ref[...] = ((a_ref[...].astype(F32) + b_ref[...].astype(F32)) + c_ref[...].astype(F32)) + d_ref[...].astype(F32)

    specs = [pl.BlockSpec((None, tr, w), lambda i, m_ref: (m_ref[0], i, 0))]
    specs += [pl.BlockSpec((None, tr, w), functools.partial(lambda k, i, m_ref: (k, i, 0), k)) for k in range(3)]
    return pl.pallas_call(
        body, name=name,
        grid_spec=pltpu.PrefetchScalarGridSpec(
            num_scalar_prefetch=1, grid=(rh // tr,), in_specs=specs,
            out_specs=pl.BlockSpec((tr, w), lambda i, m_ref: (i, 0))),
        out_shape=jax.ShapeDtypeStruct((rh, w), F32),
        compiler_params=_params("parallel"),
    )(mvec, h4, r3, r3, r3)


def _swap_reduced(name, ghs):
    n = len(ghs)

    def body(*refs):
        in_refs, out_refs, (send_sems, recv_sems) = refs[:n], refs[n:2 * n], refs[2 * n:]
        x, y, c, _ = _place()
        cps = [_remote(in_ref, out_ref, send_sems, recv_sems, a, (x, y, 1 - c))
               for a, (in_ref, out_ref) in enumerate(zip(in_refs, out_refs))]
        for cp in cps:
            cp.start()
        for cp in cps:
            cp.wait()

    return pl.pallas_call(
        body, name=name, in_specs=[HBM_SPEC] * n, out_specs=[HBM_SPEC] * n,
        out_shape=[jax.ShapeDtypeStruct(g.shape, g.dtype) for g in ghs],
        scratch_shapes=_dma_sems(n),
    )(*ghs)


def _allreduce_small(name, vec):
    rows, w = vec.shape

    def body(v_ref, o_ref, buf, send_sems, recv_sems):
        x, y, c, _ = _place()
        me = 4 * x + 2 * y + c
        buf[me] = v_ref[...]
        flips = [(fx, fy, fc) for fx in (0, 1) for fy in (0, 1) for fc in (0, 1)][1:]
        peers = [(x + fx - 2 * x * fx, y + fy - 2 * y * fy, c + fc - 2 * c * fc) for fx, fy, fc in flips]
        sends = [_remote(v_ref, buf.at[me], send_sems, recv_sems, k, peer) for k, peer in enumerate(peers)]
        for cp in sends:
            cp.start()
        for k, (px, py, pc) in enumerate(peers):
            got = buf.at[4 * px + 2 * py + pc]
            _remote(got, got, send_sems, recv_sems, k, (px, py, pc)).wait_recv()
        for cp in sends:
            cp.wait_send()
        acc = buf[0]
        for dev in range(1, N_DEV):
            acc = acc + buf[dev]
        o_ref[...] = acc

    vm = pl.BlockSpec(memory_space=pltpu.VMEM)
    return pl.pallas_call(
        body, name=name, in_specs=[vm], out_specs=vm, out_shape=jax.ShapeDtypeStruct((rows, w), F32),
        scratch_shapes=[pltpu.VMEM((N_DEV, rows, w), F32), pltpu.SemaphoreType.DMA((7,)), pltpu.SemaphoreType.DMA((7,))],
    )(vec)


GATE_COL = 3 * SB_WIDTH + 3 * FOX_WIDTH + FOX_HEADS + MEM_WIDTH
FL_COL = QKV_WIDTH


GROUP_A_COLS = [(0, QKV_WIDTH), (FL_COL + FOX_HEADS, MEM_WIDTH)]
GROUP_B_COLS = [(GATE_COL, MIX_WIDTH), (FL_COL, FOX_HEADS)]


def _group_from_shards(shard_of, cw, spans, pad):
    parts = []
    for lo, width in spans:
        hi = lo + width
        for j in range(N_CHIPS):
            a, b = max(lo, j * cw), min(hi, (j + 1) * cw)
            if a < b:
                parts.append(shard_of(j)[:, a - j * cw:b - j * cw])
    if pad:
        parts.append(jnp.zeros((parts[0].shape[0], pad), parts[0].dtype))
    return jnp.concatenate(parts, axis=1)


def _shard_from_groups(ga, gb, j, cw):
    lo, hi = j * cw, (j + 1) * cw
    placed = []
    for grp, spans in ((ga, GROUP_A_COLS), (gb, GROUP_B_COLS)):
        at = 0
        for first, width in spans:
            a, b = max(lo, first), min(hi, first + width)
            if a < b:
                placed.append((a, grp[:, at + a - first:at + b - first]))
            at += width
    return jnp.concatenate([p for _, p in sorted(placed, key=lambda t: t[0])], axis=1)


def _tile_of(n, cap, unit):
    if n <= cap:
        return n
    best = None
    for t in range(unit, cap + 1, unit):
        if n % t == 0:
            best = t
    assert best is not None, (n, cap, unit)
    return best


def _flat2(a):
    return a.reshape(-1, a.shape[-1])


def _pack_small(parts):
    rows = []
    for p in parts:
        f = p.reshape(-1).astype(F32)
        f = jnp.pad(f, (0, (-f.shape[0]) % LANES))
        rows.append(f.reshape(-1, LANES))
    out = jnp.concatenate(rows, axis=0)
    return jnp.pad(out, ((0, (-out.shape[0]) % 8), (0, 0)))


def _unpack_small(packed, shapes):
    outs, r = [], 0
    for shp in shapes:
        n = 1
        for s_ in shp:
            n *= s_
        nr = -(-n // LANES)
        outs.append(packed[r:r + nr].reshape(-1)[:n].reshape(shp))
        r += nr
    return outs


def kernel(x, mem, norm_w, w_in, b_forget, mem_norm_w, w_mem_kv, out_norm_w, w_out, final_norm_w, loss_target, m_norm_w, m_w_in, m_b_forget, m_mem_norm_w, m_w_mem_kv, m_out_norm_w, m_w_out, m_final_norm_w, v_norm_w, v_w_in, v_b_forget, v_mem_norm_w, v_w_mem_kv, v_out_norm_w, v_w_out, v_final_norm_w):
    xs = x[0]
    mems = mem[0]
    target = loss_target[0]
    s, d = xs.shape
    depth = norm_w.shape[0]
    nb = s // TILE
    ts = _tile_of(s, 256, 8)
    big = (w_in, w_mem_kv, w_out)
    core = lax.axis_index("c")
    chip = 2 * lax.axis_index("x") + lax.axis_index("y")
    cvec = core.astype(jnp.int32).reshape(1)
    mvec = chip.astype(jnp.int32).reshape(1)
    cw = w_in.shape[2]

    own_w = [[a[l].astype(BF16) for a in big] for l in range(depth)]

    def lay_out(own, got):
        full = [jnp.where(lax.broadcasted_iota(jnp.int32, g.shape, 0) == chip, o[None], g) for g, o in zip(got, own)]
        shard_of = lambda j: full[0][j]
        wa_l = _group_from_shards(shard_of, cw, GROUP_A_COLS, 0)
        wb_l = _group_from_shards(shard_of, cw, GROUP_B_COLS, LANES - FOX_HEADS)
        return wa_l, wb_l, full[1].reshape(-1, full[1].shape[2]), full[2].reshape(-1, full[2].shape[2])

    layer_w = [lay_out(own_w[0], _run_exchange("gather_weights0", _gather_exchange(own_w[0])))]

    tm = _tile_of(s, 256, 8)
    fl_block = MIX_WIDTH // LANES

    saved = []
    cur = xs
    for l in range(depth):
        wa, wb, wkv, wout = layer_w[l]
        h = _rms_fwd(f"rms_fwd{l}", cur, norm_w[l][None], ts)
        pa = _mm(f"inproj_a{l}", h, wa, "nn", tm, _tile_of(PA, 1664, LANES), BF16)
        pb = _mm(f"inproj_b{l}", h, wb, "nn", tm, PB, F32)
        bpad = jnp.pad(b_forget[l], (0, LANES - FOX_HEADS))[None]
        ccol, crow = _gate_fwd(f"gate_fwd{l}", pb, bpad, fl_block)
        ccol4 = jnp.repeat(ccol[:, :FOX_HEADS].reshape(s, 4, 2).transpose(1, 0, 2), HEAD_DIM, axis=2)
        crow4 = jnp.pad(crow.reshape(nb, 4, 2, TILE).transpose(1, 0, 2, 3), ((0, 0), (0, 0), (0, 6), (0, 0)))
        next_gather = _gather_exchange(own_w[l + 1]) if l + 1 < depth else None
        o_sb, got = _sb_fwd(f"sb_fwd{l}", pa, 0, carried=next_gather)
        if next_gather is not None:
            layer_w.append(lay_out(own_w[l + 1], got))
        o_fx, lse_fx = _fox_fwd(f"fox_fwd{l}", pa, 3 * SB_WIDTH, ccol4, crow4)
        mn = _rms_fwd(f"mem_rms{l}", mems, mem_norm_w[l][None], mems.shape[0])
        mkv = _mm(f"mem_kv{l}", mn, wkv, "nn", mems.shape[0], 2 * MEM_WIDTH, BF16)
        o_m, lse_m = _mem_fwd(f"mem_fwd{l}", pa, mkv)
        nxt, y2 = _out_fwd(f"out_fwd{l}", o_sb, o_fx, o_m, pb, out_norm_w[l][None], cur, wout, ts)
        saved.append((cur, h, pa, pb, bpad, ccol4, crow4, o_sb, o_fx, lse_fx, mn, mkv, o_m, lse_m, y2))
        cur = nxt

    loss_v, dx, dxb, g_final = _final_loss("final_loss", cur, final_norm_w[None], target, ts)

    g_norm, g_b, g_memnorm, g_outnorm = [None] * depth, [None] * depth, [None] * depth, [None] * depth
    g_wa, g_wb, g_wkv, g_wout = [None] * depth, [None] * depth, [None] * depth, [None] * depth
    g_own = [[None] * depth for _ in big]
    g_other = [[None] * depth for _ in big]

    def reduce_at_owner(lr, chip_sums, from_chips, tiles):
        halves = [_sum_chips(f"grad_sum_chips{lr}_{k}", h_, r_, mvec, t_)
                  for k, (h_, r_, t_) in enumerate(zip(chip_sums, from_chips, tiles))]
        others = _swap_reduced(f"grad_swap_reduced{lr}", halves)
        for k in range(len(big)):
            g_own[k][lr], g_other[k][lr] = halves[k], others[k]

    pending = None
    for l in reversed(range(depth)):
        xin, h, pa, pb, bpad, ccol4, crow4, o_sb, o_fx, lse_fx, mn, mkv, o_m, lse_m, y2 = saved[l]
        wa, wb, wkv, wout = layer_w[l]
        dy, dgate, g_outnorm[l] = _out_bwd(f"out_bwd{l}", dxb, o_sb, o_fx, o_m, pb, out_norm_w[l][None], wout, ts)
        g_wout[l] = _mm(f"dw_out{l}", y2, dxb, "tn", _tile_of(MIX_WIDTH, 640, LANES), d, F32)
        scatter = _scatter_exchange(pending[1]) if pending is not None else None
        (dq_sb, dk_sb, dv_sb), from_chips = _sb_bwd(f"sb_bwd{l}", pa, 0, dy, 0, carried=scatter)
        if pending is not None:
            reduce_at_owner(pending[0], pending[1], from_chips, pending[2])
        dq_fx, dk_fx, dv_fx, cs4 = _fox_bwd(f"fox_bwd{l}", pa, 3 * SB_WIDTH, ccol4, crow4, o_fx, lse_fx, dy, SB_WIDTH)
        colsum = cs4[:, :, :2, :].transpose(1, 0, 2, 3).reshape(nb, 8, TILE)
        dlogit, g_b[l] = _gate_bwd(f"gate_bwd{l}", pb, bpad, colsum, fl_block)
        dq_m, dk_m, dv_m = _mem_bwd(f"mem_bwd{l}", pa, mkv, o_m, lse_m, dy, SB_WIDTH + FOX_WIDTH)
        dmkv = jnp.concatenate([dk_m, dv_m], axis=1)
        g_wkv[l] = _mm(f"dw_kv{l}", mn, dmkv, "tn", d, 2 * MEM_WIDTH, F32)
        dmn = _mm(f"dmem{l}", dmkv, wkv, "nt", mems.shape[0], d, F32)
        g_memnorm[l] = _rms_wgrad(f"mem_norm_grad{l}", mems, dmn)
        dpa = jnp.concatenate([dq_sb, dk_sb, dv_sb, dq_fx, dk_fx, dv_fx, dq_m], axis=1)
        dpb = jnp.concatenate([dgate, dlogit], axis=1)
        tw = _tile_of(d, 512, LANES)
        g_wa[l] = _mm(f"dw_in_a{l}", h, dpa, "tn", tw, _tile_of(PA, 1664, LANES), F32)
        g_wb[l] = _mm(f"dw_in_b{l}", h, dpb, "tn", tw, PB, F32)
        dh = _mm(f"dh_a{l}", dpa, wa, "nt", tm, d, F32)
        dh = _mm(f"dh_b{l}", dpb, wb, "nt", tm, d, F32, res=dh)
        dx, dxb, g_norm[l] = _rms_bwd(f"rms_bwd{l}", xin, norm_w[l][None], dh, dx, ts)
        g4s = [jnp.stack([_shard_from_groups(g_wa[l], g_wb[l], j, cw) for j in range(N_CHIPS)]),
               g_wkv[l].reshape(N_CHIPS, -1, g_wkv[l].shape[1]), g_wout[l].reshape(N_CHIPS, -1, d)]
        tiles = [_tile_of(g.shape[1] // 2, 256, 16) for g in g4s]
        from_sibling = _swap_halves(f"grad_swap_halves{l}", g4s)
        chip_sums = [_add_half(f"grad_add_half{l}_{k}", g, r, cvec, t)
                     for k, (g, r, t) in enumerate(zip(g4s, from_sibling, tiles))]
        pending = (l, chip_sums, tiles)
    reduce_at_owner(pending[0], pending[1], _run_exchange(f"grad_scatter_chips{pending[0]}", _scatter_exchange(pending[1])),
                    pending[2])

    small_w = [norm_w, b_forget, mem_norm_w, out_norm_w, final_norm_w]
    small_m = [m_norm_w, m_b_forget, m_mem_norm_w, m_out_norm_w, m_final_norm_w]
    small_v = [v_norm_w, v_b_forget, v_mem_norm_w, v_out_norm_w, v_final_norm_w]
    small_shapes = [a.shape for a in small_w]
    local_small = [jnp.concatenate(g_norm, axis=0), jnp.stack([g[0, :FOX_HEADS] for g in g_b]),
                   jnp.concatenate(g_memnorm, axis=0), jnp.concatenate(g_outnorm, axis=0), g_final[0]]
    reduced = _allreduce_small("small_allreduce", _pack_small(local_small + [loss_v[0, :1]]))
    small_grads = _unpack_small(reduced, small_shapes)
    n_small_rows = sum(-(-a.size // LANES) for a in small_w)
    loss = reduced[n_small_rows, 0]

    d_s, m_s, v_s = _adamw("adamw_small", _pack_small(small_w), _pack_small(small_grads), _pack_small(small_m),
                           _pack_small(small_v), _pack_small(small_w).shape[0])
    small_delta, small_m2, small_v2 = (_unpack_small(a, small_shapes) for a in (d_s, m_s, v_s))
    big_grads, big_delta, big_m2, big_v2 = [], [], [], []
    for k, (nm, w_, m_, v_) in enumerate(zip(("w_in", "w_mem_kv", "w_out"), big, (m_w_in, m_w_mem_kv, m_w_out),
                                             (v_w_in, v_w_mem_kv, v_w_out))):
        outs = _adamw_sharded(f"adamw_{nm}", w_, m_, v_, g_own[k], g_other[k], cvec,
                              _tile_of(w_.shape[1] // 2, 256, 8))
        for lst, o in zip((big_grads, big_delta, big_m2, big_v2), outs):
            lst.append(o)

    def order(sm, bg):
        return [sm[0], bg[0], sm[1], sm[2], bg[1], sm[3], bg[2], sm[4]]

    return (loss, dx[None], *order(small_grads, big_grads), *order(small_delta, big_delta),
            *order(small_m2, big_m2), *order(small_v2, big_v2))
```

```python
import functools

import jax
import jax.numpy as jnp
from jax import lax
from jax.experimental import pallas as pl
from jax.experimental.pallas import tpu as pltpu

F32 = jnp.float32
BF16 = jnp.bfloat16

HEAD_DIM = 64
SB_WIDTH = 512
FOX_WIDTH = 512
FOX_HEADS = 8
MEM_WIDTH = 256
MIX_WIDTH = SB_WIDTH + FOX_WIDTH + MEM_WIDTH
TOTAL_HEADS = MIX_WIDTH // HEAD_DIM
IN_WIDTH = 3 * SB_WIDTH + 3 * FOX_WIDTH + FOX_HEADS + MEM_WIDTH + MIX_WIDTH
LANES = 128
QKV_WIDTH = 3 * SB_WIDTH + 3 * FOX_WIDTH
PA = QKV_WIDTH + MEM_WIDTH
PB = LANES + MIX_WIDTH
EPS = 1e-6
SCALE = HEAD_DIM ** -0.5
TILE = 256
NEG_INF = float("-inf")
MASKED = -1e30

ADAM_LR = 0.001
ADAM_B1 = 0.9
ADAM_B2 = 0.999
ADAM_EPS = 1e-08
ADAM_WD = 0.01
ADAM_STEP = 10

N_CHIPS = 4
N_DEV = 8
VMEM_LIMIT = 48 * 1024 * 1024
MESH = pl.DeviceIdType.MESH


def _params(*sem):
    return pltpu.CompilerParams(dimension_semantics=tuple(sem), vmem_limit_bytes=VMEM_LIMIT)


def _dot(a, b):
    return jnp.dot(a, b, preferred_element_type=F32)


def _dot_nt(a, b):
    return lax.dot_general(a, b, (((1,), (1,)), ((), ())), preferred_element_type=F32)


def _dot_tn(a, b):
    return lax.dot_general(a, b, (((0,), (0,)), ((), ())), preferred_element_type=F32)


def _split2(x):
    hi = x.astype(BF16)
    lo = (x - hi.astype(F32)).astype(BF16)
    return hi, lo


def _split3(x):
    hi = x.astype(BF16)
    r = x - hi.astype(F32)
    mid = r.astype(BF16)
    lo = (r - mid.astype(F32)).astype(BF16)
    return hi, mid, lo


def _sum_l2(x, u):
    hi, lo = _split2(x)
    return _dot(hi, u) + _dot(lo, u)


def _sum_l3(x, u):
    hi, mid, lo = _split3(x)
    return _dot(hi, u) + _dot(mid, u) + _dot(lo, u)


def _sum_r3(u, x):
    hi, mid, lo = _split3(x)
    return _dot(u, hi) + _dot(u, mid) + _dot(u, lo)


def _softplus(z):
    return jnp.maximum(z, 0.0) + jnp.log1p(jnp.exp(-jnp.abs(z)))


def _tri(n, pred):
    r = lax.broadcasted_iota(jnp.int32, (n, n), 0)
    c = lax.broadcasted_iota(jnp.int32, (n, n), 1)
    return jnp.where(pred(r, c), 1.0, 0.0).astype(BF16)


def _rows(ref, j, n=TILE):
    return pl.ds(pl.multiple_of(j * n, n), n)


def _mm(name, a, b, mode, tm, tn, out_dtype, res=None, a_lead=(), b_lead=()):
    a2, b2 = a.shape[len(a_lead):], b.shape[len(b_lead):]
    if mode == "tn":
        k, m = a2
    else:
        m, k = a2
    n = b2[0] if mode == "nt" else b2[1]
    assert m % tm == 0 and n % tn == 0, (name, m, tm, n, tn)
    na, nb = (None,) * len(a_lead), (None,) * len(b_lead)
    if mode == "tn":
        a_spec = pl.BlockSpec(na + (k, tm), lambda j, i: a_lead + (0, i))
    else:
        a_spec = pl.BlockSpec(na + (tm, k), lambda j, i: a_lead + (i, 0))
    if mode == "nt":
        b_spec = pl.BlockSpec(nb + (tn, k), lambda j, i: b_lead + (j, 0))
    else:
        b_spec = pl.BlockSpec(nb + (k, tn), lambda j, i: b_lead + (0, j))
    o_spec = pl.BlockSpec((tm, tn), lambda j, i: (i, j))
    dot = {"nn": _dot, "nt": _dot_nt, "tn": _dot_tn}[mode]

    def body(a_ref, b_ref, *rest):
        o_ref = rest[-1]
        acc = dot(a_ref[...].astype(BF16), b_ref[...].astype(BF16))
        if res is not None:
            acc = acc + rest[0][...]
        o_ref[...] = acc.astype(o_ref.dtype)

    args, specs = [a, b], [a_spec, b_spec]
    if res is not None:
        args.append(res)
        specs.append(o_spec)
    return pl.pallas_call(
        body, name=name, grid=(n // tn, m // tm), in_specs=specs, out_specs=o_spec,
        out_shape=jax.ShapeDtypeStruct((m, n), out_dtype),
        compiler_params=_params("parallel", "parallel"),
    )(*args)


def _rms_fwd(name, x, g, ts):
    s, d = x.shape

    def body(x_ref, g_ref, o_ref):
        xf = x_ref[...]
        r = lax.rsqrt(jnp.mean(xf * xf, axis=1, keepdims=True) + EPS)
        o_ref[...] = (xf * r * g_ref[...]).astype(BF16)

    return pl.pallas_call(
        body, name=name, grid=(s // ts,),
        in_specs=[pl.BlockSpec((ts, d), lambda i: (i, 0)), pl.BlockSpec((1, d), lambda i: (0, 0))],
        out_specs=pl.BlockSpec((ts, d), lambda i: (i, 0)),
        out_shape=jax.ShapeDtypeStruct((s, d), BF16),
        compiler_params=_params("parallel"),
    )(x, g)


def _rms_bwd(name, x, g, dh, dres, ts):
    s, d = x.shape

    def body(x_ref, g_ref, dh_ref, dres_ref, dx_ref, dxb_ref, dg_ref):
        @pl.when(pl.program_id(0) == 0)
        def _():
            dg_ref[...] = jnp.zeros_like(dg_ref)

        xf = x_ref[...]
        r = lax.rsqrt(jnp.mean(xf * xf, axis=1, keepdims=True) + EPS)
        xh = xf * r
        dhf = dh_ref[...]
        dg_ref[...] += jnp.sum(dhf * xh, axis=0, keepdims=True)
        dxh = dhf * g_ref[...]
        m = jnp.mean(dxh * xh, axis=1, keepdims=True)
        dx = r * (dxh - xh * m) + dres_ref[...]
        dx_ref[...] = dx
        dxb_ref[...] = dx.astype(BF16)

    row = pl.BlockSpec((ts, d), lambda i: (i, 0))
    vec = pl.BlockSpec((1, d), lambda i: (0, 0))
    return pl.pallas_call(
        body, name=name, grid=(s // ts,), in_specs=[row, vec, row, row], out_specs=[row, row, vec],
        out_shape=[jax.ShapeDtypeStruct((s, d), F32), jax.ShapeDtypeStruct((s, d), BF16),
                   jax.ShapeDtypeStruct((1, d), F32)],
        compiler_params=_params("arbitrary"),
    )(x, g, dh, dres)


def _rms_wgrad(name, x, dh):
    m_, d = x.shape

    def body(x_ref, dh_ref, dg_ref):
        xf = x_ref[...]
        r = lax.rsqrt(jnp.mean(xf * xf, axis=1, keepdims=True) + EPS)
        dg_ref[...] = jnp.sum(dh_ref[...] * xf * r, axis=0, keepdims=True)

    return pl.pallas_call(
        body, name=name, out_shape=jax.ShapeDtypeStruct((1, d), F32),
    )(x, dh)


def _final_loss(name, x, g, target, ts):
    s, d = x.shape

    def body(x_ref, g_ref, t_ref, loss_ref, dx_ref, dxb_ref, dg_ref):
        @pl.when(pl.program_id(0) == 0)
        def _():
            dg_ref[...] = jnp.zeros_like(dg_ref)
            loss_ref[...] = jnp.zeros_like(loss_ref)

        xf = x_ref[...]
        gw = g_ref[...]
        r = lax.rsqrt(jnp.mean(xf * xf, axis=1, keepdims=True) + EPS)
        xh = xf * r
        e = xh * gw - t_ref[...]
        part = 0.5 * jnp.sum(jnp.mean(e * e, axis=1, keepdims=True), axis=0, keepdims=True)
        loss_ref[...] += jnp.broadcast_to(part, loss_ref.shape)
        dy = e * (1.0 / d)
        dg_ref[...] += jnp.sum(dy * xh, axis=0, keepdims=True)
        dxh = dy * gw
        m = jnp.mean(dxh * xh, axis=1, keepdims=True)
        dx = r * (dxh - xh * m)
        dx_ref[...] = dx
        dxb_ref[...] = dx.astype(BF16)

    row = pl.BlockSpec((ts, d), lambda i: (i, 0))
    vec = pl.BlockSpec((1, d), lambda i: (0, 0))
    lvec = pl.BlockSpec((1, LANES), lambda i: (0, 0))
    return pl.pallas_call(
        body, name=name, grid=(s // ts,), in_specs=[row, vec, row], out_specs=[lvec, row, row, vec],
        out_shape=[jax.ShapeDtypeStruct((1, LANES), F32), jax.ShapeDtypeStruct((s, d), F32),
                   jax.ShapeDtypeStruct((s, d), BF16), jax.ShapeDtypeStruct((1, d), F32)],
        compiler_params=_params("arbitrary"),
    )(x, g, target)


def _gate_fwd(name, pb, bpad, fl_block):
    s = pb.shape[0]
    nb = s // TILE

    def body(fl_ref, b_ref, ccol_ref, crow_ref, carry):
        @pl.when(pl.program_id(0) == 0)
        def _():
            carry[...] = jnp.zeros_like(carry)

        u = fl_ref[...] + b_ref[...]
        lf = jnp.minimum(u, 0.0) - jnp.log1p(jnp.exp(-jnp.abs(u)))
        lower = _tri(TILE, lambda r, c: c <= r)
        c = _sum_r3(lower, lf) + carry[0:1, :]
        ccol_ref[...] = c
        crow_ref[0] = c.T[0:8, :]
        carry[...] = jnp.broadcast_to(c[TILE - 1:TILE, :], carry.shape)

    return pl.pallas_call(
        body, name=name, grid=(nb,),
        in_specs=[pl.BlockSpec((TILE, LANES), lambda i: (i, fl_block)), pl.BlockSpec((1, LANES), lambda i: (0, 0))],
        out_specs=[pl.BlockSpec((TILE, LANES), lambda i: (i, 0)), pl.BlockSpec((1, 8, TILE), lambda i: (i, 0, 0))],
        out_shape=[jax.ShapeDtypeStruct((s, LANES), F32), jax.ShapeDtypeStruct((nb, 8, TILE), F32)],
        scratch_shapes=[pltpu.VMEM((8, LANES), F32)],
        compiler_params=_params("arbitrary"),
    )(pb, bpad)


def _gate_bwd(name, pb, bpad, colsum, fl_block):
    s = pb.shape[0]
    nb = s // TILE

    def body(fl_ref, b_ref, cs_ref, dl_ref, db_ref, carry):
        @pl.when(pl.program_id(0) == 0)
        def _():
            carry[...] = jnp.zeros_like(carry)
            db_ref[...] = jnp.zeros_like(db_ref)

        upper = _tri(TILE, lambda r, c: r >= c)
        rsum = _sum_l3(cs_ref[0], upper) + carry[:, 0:1]
        carry[...] = jnp.broadcast_to(rsum[:, 0:1], carry.shape)
        full = jnp.concatenate([rsum, jnp.zeros((LANES - 8, TILE), F32)], axis=0)
        dlf = -full.T
        u = fl_ref[...] + b_ref[...]
        dlogit = dlf * (1.0 - jax.nn.sigmoid(u))
        dl_ref[...] = dlogit.astype(BF16)
        db_ref[...] += jnp.sum(dlogit, axis=0, keepdims=True)

    rev = lambda i: (nb - 1 - i, 0)
    return pl.pallas_call(
        body, name=name, grid=(nb,),
        in_specs=[pl.BlockSpec((TILE, LANES), lambda i: (nb - 1 - i, fl_block)),
                  pl.BlockSpec((1, LANES), lambda i: (0, 0)),
                  pl.BlockSpec((1, 8, TILE), lambda i: (nb - 1 - i, 0, 0))],
        out_specs=[pl.BlockSpec((TILE, LANES), rev), pl.BlockSpec((1, LANES), lambda i: (0, 0))],
        out_shape=[jax.ShapeDtypeStruct((s, LANES), BF16), jax.ShapeDtypeStruct((1, LANES), F32)],
        scratch_shapes=[pltpu.VMEM((8, LANES), F32)],
        compiler_params=_params("arbitrary"),
    )(pb, bpad, colsum)


def _head_slices(hh):
    return slice(HEAD_DIM * hh, HEAD_DIM * (hh + 1))


def _scaled_q(q_ref, sl, scale=SCALE):
    return (q_ref[:, sl].astype(F32) * scale).astype(BF16)


def _neg_abs(x):
    sign = jnp.uint32(0x80000000)
    return lax.bitcast_convert_type(lax.bitcast_convert_type(x, jnp.uint32) | sign, F32)


def _sb_tile(qn, kj, carry, strict, u_after, diag):
    nz = _dot_nt(qn, kj)
    lf = jnp.minimum(nz, 0.0) - jnp.log(1.0 + jnp.exp(_neg_abs(nz)))
    lsig = lf - nz
    if diag:
        lf = jnp.where(strict, lf, 0.0)
    sx = _dot(lf.astype(BF16), u_after)
    a = jnp.exp(lsig + sx + carry)
    if diag:
        a = jnp.where(strict, a, 0.0)
    return lsig, a, carry + sx[:, 0:1] + lf[:, 0:1]


def _pair_grid_call(name, body, nb, in_specs, out_specs, out_shape, scratch, args, carried=None):
    if carried is None:
        return pl.pallas_call(
            body, name=name, grid=(4, nb), in_specs=in_specs, out_specs=out_specs, out_shape=out_shape,
            scratch_shapes=scratch, compiler_params=_params("arbitrary", "arbitrary"),
        )(*args)
    n_in, n_out, n_ex = len(in_specs), len(out_specs), carried.n

    def body_with_copies(*refs):
        own_in, ex_in = refs[:n_in], refs[n_in:n_in + n_ex]
        own_out = refs[n_in + n_ex:n_in + n_ex + n_out]
        ex_out = refs[n_in + n_ex + n_out:n_in + 2 * n_ex + n_out]
        own_scratch, sems = refs[n_in + 2 * n_ex + n_out:-2], refs[-2:]
        parts = (ex_in, ex_out, sems[0], sems[1])
        p, i = pl.program_id(0), pl.program_id(1)
        pl.when(jnp.logical_and(p == 0, i == 0))(lambda: carried.begin(*parts))
        if carried.relay is not None:
            pl.when(jnp.logical_and(p == 3, i == 0))(lambda: carried.relay(*parts))
        body(*own_in, *own_out, *own_scratch)
        pl.when(jnp.logical_and(p == 3, i == nb - 1))(lambda: carried.finish(*parts))

    return pl.pallas_call(
        body_with_copies, name=name, grid=(4, nb), in_specs=list(in_specs) + [HBM_SPEC] * n_ex,
        out_specs=list(out_specs) + [HBM_SPEC] * n_ex, out_shape=list(out_shape) + carried.out_shapes,
        scratch_shapes=list(scratch) + _dma_sems(carried.n_sems),
        compiler_params=_params("arbitrary", "arbitrary"),
    )(*args, *carried.inputs)


def _sb_fwd(name, pa, col0, carried=None):
    s = pa.shape[0]
    nb = s // TILE
    cb = col0 // LANES

    def body(q_ref, k_ref, v_ref, o_ref, lsig_s, lf_s):
        i = pl.program_id(1)
        r = lax.broadcasted_iota(jnp.int32, (TILE, TILE), 0)
        c = lax.broadcasted_iota(jnp.int32, (TILE, TILE), 1)
        strict = c < r
        u_after = _tri(TILE, lambda rr, cc: rr > cc)
        qs = [_scaled_q(q_ref, _head_slices(hh), -SCALE) for hh in range(2)]

        def neg_z(j):
            kblk = k_ref[_rows(k_ref, j), :]
            return [_dot_nt(qs[hh], kblk[:, _head_slices(hh)]) for hh in range(2)]

        def scores(nzs, slot, diag):
            for hh, nz in enumerate(nzs):
                lf = jnp.minimum(nz, 0.0) - jnp.log(1.0 + jnp.exp(_neg_abs(nz)))
                lsig = lf - nz
                if diag:
                    lf = jnp.where(strict, lf, 0.0)
                    lsig = jnp.where(strict, lsig, MASKED)
                lsig_s[slot, hh] = lsig
                lf_s[slot, hh] = lf.astype(BF16)

        def weigh(j, slot, state):
            vblk = v_ref[_rows(v_ref, j), :]
            new = []
            for hh in range(2):
                carry, acc = state[hh]
                lfb = lf_s[slot, hh]
                sx = _dot(lfb, u_after)
                a = jnp.exp(lsig_s[slot, hh] + sx + carry)
                new.append((carry + sx[:, 0:1] + lfb[:, 0:1].astype(F32),
                            acc + _dot(a.astype(BF16), vblk[:, _head_slices(hh)])))
            return tuple(new)

        def step(t, state):
            state = weigh(i - t + 1, (t - 1) % 2, state)
            scores(neg_z(i - t), t % 2, False)
            return state

        zero = (jnp.zeros((TILE, 1), F32), jnp.zeros((TILE, HEAD_DIM), F32))
        scores(neg_z(i), 0, True)
        state = lax.fori_loop(1, i + 1, step, (zero, zero))
        state = weigh(0, i % 2, state)
        o_ref[...] = jnp.concatenate([state[0][1], state[1][1]], axis=1)

    outs = _pair_grid_call(
        name, body, nb,
        in_specs=[pl.BlockSpec((TILE, LANES), lambda p, i: (i, cb + p)),
                  pl.BlockSpec((s, LANES), lambda p, i: (0, cb + 4 + p)),
                  pl.BlockSpec((s, LANES), lambda p, i: (0, cb + 8 + p))],
        out_specs=[pl.BlockSpec((TILE, LANES), lambda p, i: (i, p))],
        out_shape=[jax.ShapeDtypeStruct((s, SB_WIDTH), F32)],
        scratch=[pltpu.VMEM((2, 2, TILE, TILE), F32), pltpu.VMEM((2, 2, TILE, TILE), BF16)],
        args=(pa, pa, pa), carried=carried)
    return outs[0], outs[1:]


def _sb_bwd(name, pa, col0, dout, dcol0, carried=None):
    s = pa.shape[0]
    nb = s // TILE
    cb = col0 // LANES
    db = dcol0 // LANES

    def body(q_ref, k_ref, v_ref, do_ref, dq_ref, dk_ref, dv_ref, dk_acc, dv_acc, dpan, span, gsum, lsig_s, lf_s):
        i = pl.program_id(1)

        @pl.when(i == 0)
        def _():
            dk_acc[...] = jnp.zeros_like(dk_acc)
            dv_acc[...] = jnp.zeros_like(dv_acc)

        r = lax.broadcasted_iota(jnp.int32, (TILE, TILE), 0)
        c = lax.broadcasted_iota(jnp.int32, (TILE, TILE), 1)
        strict = c < r
        u_after = _tri(TILE, lambda rr, cc: rr > cc)
        u_before = _tri(TILE, lambda rr, cc: rr < cc)
        qs = [_scaled_q(q_ref, _head_slices(hh), -SCALE) for hh in range(2)]
        dos = [do_ref[:, _head_slices(hh)].astype(BF16) for hh in range(2)]

        def scores(j, slot, diag):
            kblk = k_ref[_rows(k_ref, j), :]
            for hh in range(2):
                nz = _dot_nt(qs[hh], kblk[:, _head_slices(hh)])
                lf = jnp.minimum(nz, 0.0) - jnp.log(1.0 + jnp.exp(_neg_abs(nz)))
                lsig = lf - nz
                if diag:
                    lf = jnp.where(strict, lf, 0.0)
                    lsig = jnp.where(strict, lsig, MASKED)
                lsig_s[slot, hh] = lsig
                lf_s[slot, hh] = lf.astype(BF16)

        def grads(j, slot, carries):
            vblk = v_ref[_rows(v_ref, j), :]
            new = []
            for hh in range(2):
                lfb = lf_s[slot, hh]
                lsig = lsig_s[slot, hh]
                sx = _dot(lfb, u_after)
                a = jnp.exp(lsig + sx + carries[hh])
                g = a * _dot_nt(dos[hh], vblk[:, _head_slices(hh)])
                sig = jnp.exp(lsig)
                inside = _dot(g.astype(BF16), u_before)
                dpan[hh, j] = sig * (inside + g) - g
                span[hh, j] = sig
                gsum[hh, j] = inside[:, TILE - 1:TILE] + g[:, TILE - 1:TILE]
                dv_acc[hh, _rows(None, j), :] += _dot_tn(a.astype(BF16), dos[hh])
                new.append(carries[hh] + sx[:, 0:1] + lfb[:, 0:1].astype(F32))
            return tuple(new)

        def step1(t, carries):
            carries = grads(i - t + 1, (t - 1) % 2, carries)
            scores(i - t, t % 2, False)
            return carries

        zero1 = jnp.zeros((TILE, 1), F32)
        scores(i, 0, True)
        carries = lax.fori_loop(1, i + 1, step1, (zero1, zero1))
        grads(0, i % 2, carries)

        def pass2(j, state):
            kblk = k_ref[_rows(k_ref, j), :]
            new = []
            for hh in range(2):
                before, ndq = state[hh]
                ndzb = (dpan[hh, j] + span[hh, j] * before).astype(BF16)
                dk_acc[hh, _rows(None, j), :] += _dot_tn(ndzb, qs[hh])
                new.append((before + gsum[hh, j], ndq + _dot(ndzb, kblk[:, _head_slices(hh)])))
            return tuple(new)

        zero2 = (zero1, jnp.zeros((TILE, HEAD_DIM), F32))
        state = lax.fori_loop(0, i + 1, pass2, (zero2, zero2))
        dq_ref[...] = jnp.concatenate([state[0][1] * -SCALE, state[1][1] * -SCALE], axis=1).astype(BF16)

        @pl.when(i == nb - 1)
        def _():
            dk_ref[...] = jnp.concatenate([dk_acc[0], dk_acc[1]], axis=1).astype(BF16)
            dv_ref[...] = jnp.concatenate([dv_acc[0], dv_acc[1]], axis=1).astype(BF16)

    qspec = pl.BlockSpec((TILE, LANES), lambda p, i: (i, p))
    kvspec = pl.BlockSpec((s, LANES), lambda p, i: (0, p))
    out = jax.ShapeDtypeStruct((s, SB_WIDTH), BF16)
    outs = _pair_grid_call(
        name, body, nb,
        in_specs=[pl.BlockSpec((TILE, LANES), lambda p, i: (i, cb + p)),
                  pl.BlockSpec((s, LANES), lambda p, i: (0, cb + 4 + p)),
                  pl.BlockSpec((s, LANES), lambda p, i: (0, cb + 8 + p)),
                  pl.BlockSpec((TILE, LANES), lambda p, i: (i, db + p))],
        out_specs=[qspec, kvspec, kvspec], out_shape=[out, out, out],
        scratch=[pltpu.VMEM((2, s, HEAD_DIM), F32), pltpu.VMEM((2, s, HEAD_DIM), F32),
                 pltpu.VMEM((2, nb, TILE, TILE), F32), pltpu.VMEM((2, nb, TILE, TILE), F32),
                 pltpu.VMEM((2, nb, TILE, 1), F32),
                 pltpu.VMEM((2, 2, TILE, TILE), F32), pltpu.VMEM((2, 2, TILE, TILE), BF16)],
        args=(pa, pa, pa, dout), carried=carried)
    return outs[:3], outs[3:]


def _fox_scores(q, kj, cq, crj, causal, diag):
    sc = _dot_nt(q, kj) + (cq - crj)
    if diag:
        sc = jnp.where(causal, sc, NEG_INF)
    return sc


def _fox_fwd(name, pa, col0, ccol4, crow4):
    s = pa.shape[0]
    nb = s // TILE
    cb = col0 // LANES

    def body(q_ref, k_ref, v_ref, cc_ref, cr_ref, o_ref, lse_ref, sc_s):
        i = pl.program_id(1)
        r = lax.broadcasted_iota(jnp.int32, (TILE, TILE), 0)
        c = lax.broadcasted_iota(jnp.int32, (TILE, TILE), 1)
        causal = c <= r
        qs = [_scaled_q(q_ref, _head_slices(hh)) for hh in range(2)]
        cqs = [cc_ref[:, HEAD_DIM * hh:HEAD_DIM * hh + 1] for hh in range(2)]

        def logits(j, slot, diag):
            kblk = k_ref[_rows(k_ref, j), :]
            tops = []
            for hh in range(2):
                sc = _fox_scores(qs[hh], kblk[:, _head_slices(hh)], cqs[hh], cr_ref[j, hh:hh + 1, :], causal, diag)
                sc_s[slot, hh] = sc
                tops.append(jnp.max(sc, axis=1, keepdims=True))
            return tuple(tops)

        def update(j, slot, tops, state):
            vblk = v_ref[_rows(v_ref, j), :]
            new = []
            for hh in range(2):
                m, l, acc = state[hh]
                m2 = jnp.maximum(m, tops[hh])
                alpha = jnp.exp(m - m2)
                p = jnp.exp(sc_s[slot, hh] - m2)
                new.append((m2, l * alpha + jnp.sum(p, axis=1, keepdims=True),
                            acc * alpha + _dot(p.astype(BF16), vblk[:, _head_slices(hh)])))
            return tuple(new)

        def step(t, both):
            tops, state = both
            state = update(i - t + 1, (t - 1) % 2, tops, state)
            return logits(i - t, t % 2, False), state

        zero = (jnp.full((TILE, 1), NEG_INF, F32), jnp.zeros((TILE, 1), F32), jnp.zeros((TILE, HEAD_DIM), F32))
        tops, state = lax.fori_loop(1, i + 1, step, (logits(i, 0, True), (zero, zero)))
        state = update(0, i % 2, tops, state)
        o_ref[...] = jnp.concatenate([state[hh][2] / state[hh][1] for hh in range(2)], axis=1)
        lse_ref[...] = jnp.concatenate(
            [jnp.broadcast_to(state[hh][0] + jnp.log(state[hh][1]), (TILE, HEAD_DIM)) for hh in range(2)], axis=1)

    return pl.pallas_call(
        body, name=name, grid=(4, nb),
        in_specs=[pl.BlockSpec((TILE, LANES), lambda p, i: (i, cb + p)),
                  pl.BlockSpec((s, LANES), lambda p, i: (0, cb + 4 + p)),
                  pl.BlockSpec((s, LANES), lambda p, i: (0, cb + 8 + p)),
                  pl.BlockSpec((None, TILE, LANES), lambda p, i: (p, i, 0)),
                  pl.BlockSpec((None, nb, 8, TILE), lambda p, i: (p, 0, 0, 0))],
        out_specs=[pl.BlockSpec((TILE, LANES), lambda p, i: (i, p)),
                   pl.BlockSpec((None, TILE, LANES), lambda p, i: (p, i, 0))],
        out_shape=[jax.ShapeDtypeStruct((s, FOX_WIDTH), F32), jax.ShapeDtypeStruct((4, s, LANES), F32)],
        scratch_shapes=[pltpu.VMEM((2, 2, TILE, TILE), F32)],
        compiler_params=_params("parallel", "arbitrary"),
    )(pa, pa, pa, ccol4, crow4)


def _fox_bwd(name, pa, col0, ccol4, crow4, out, lse, dout, dcol0):
    s = pa.shape[0]
    nb = s // TILE
    cb = col0 // LANES
    db = dcol0 // LANES

    def body(q_ref, k_ref, v_ref, cc_ref, cr_ref, o_ref, lse_ref, do_ref,
             dq_ref, dk_ref, dv_ref, cs_ref, dk_acc, dv_acc, p_s, ds_s):
        i = pl.program_id(1)

        @pl.when(i == 0)
        def _():
            dk_acc[...] = jnp.zeros_like(dk_acc)
            dv_acc[...] = jnp.zeros_like(dv_acc)
            cs_ref[...] = jnp.zeros_like(cs_ref)

        r = lax.broadcasted_iota(jnp.int32, (TILE, TILE), 0)
        c = lax.broadcasted_iota(jnp.int32, (TILE, TILE), 1)
        causal = c <= r
        qs = [_scaled_q(q_ref, _head_slices(hh)) for hh in range(2)]
        cqs = [cc_ref[:, HEAD_DIM * hh:HEAD_DIM * hh + 1] for hh in range(2)]
        lses = [lse_ref[:, HEAD_DIM * hh:HEAD_DIM * hh + 1] for hh in range(2)]
        dofs = [do_ref[:, _head_slices(hh)] for hh in range(2)]
        dos = [d_.astype(BF16) for d_ in dofs]
        deltas = [jnp.sum(dofs[hh] * o_ref[:, _head_slices(hh)], axis=1, keepdims=True) for hh in range(2)]

        def probs(j, slot, rowsums, diag):
            kblk = k_ref[_rows(k_ref, j), :]
            vblk = v_ref[_rows(v_ref, j), :]
            new = []
            for hh in range(2):
                sl = _head_slices(hh)
                sc = _fox_scores(qs[hh], kblk[:, sl], cqs[hh], cr_ref[j, hh:hh + 1, :], causal, diag)
                p = jnp.exp(sc - lses[hh])
                ds = p * (_dot_nt(dos[hh], vblk[:, sl]) - deltas[hh])
                p_s[slot, hh] = p.astype(BF16)
                ds_s[slot, hh] = ds.astype(BF16)
                cs_ref[j, hh:hh + 1, :] += jnp.sum(ds, axis=0, keepdims=True)
                new.append(rowsums[hh] + jnp.sum(ds, axis=1, keepdims=True))
            return tuple(new)

        def accumulate(j, slot, dqs):
            kblk = k_ref[_rows(k_ref, j), :]
            new = []
            for hh in range(2):
                dsb = ds_s[slot, hh]
                dv_acc[hh, _rows(None, j), :] += _dot_tn(p_s[slot, hh], dos[hh])
                dk_acc[hh, _rows(None, j), :] += _dot_tn(dsb, qs[hh])
                new.append(dqs[hh] + _dot(dsb, kblk[:, _head_slices(hh)]))
            return tuple(new)

        def step(t, both):
            rowsums, dqs = both
            dqs = accumulate(i - t + 1, (t - 1) % 2, dqs)
            return probs(i - t, t % 2, rowsums, False), dqs

        zero1 = jnp.zeros((TILE, 1), F32)
        zero64 = jnp.zeros((TILE, HEAD_DIM), F32)
        rowsums, dqs = lax.fori_loop(1, i + 1, step, (probs(i, 0, (zero1, zero1), True), (zero64, zero64)))
        dqs = accumulate(0, i % 2, dqs)
        for hh in range(2):
            cs_ref[i, hh:hh + 1, :] -= jnp.broadcast_to(rowsums[hh], (TILE, LANES)).T[0:1, :]
        dq_ref[...] = jnp.concatenate([dqs[0] * SCALE, dqs[1] * SCALE], axis=1).astype(BF16)

        @pl.when(i == nb - 1)
        def _():
            dk_ref[...] = jnp.concatenate([dk_acc[0], dk_acc[1]], axis=1).astype(BF16)
            dv_ref[...] = jnp.concatenate([dv_acc[0], dv_acc[1]], axis=1).astype(BF16)

    qspec = pl.BlockSpec((TILE, LANES), lambda p, i: (i, p))
    kvspec = pl.BlockSpec((s, LANES), lambda p, i: (0, p))
    o3 = jax.ShapeDtypeStruct((s, FOX_WIDTH), BF16)
    return pl.pallas_call(
        body, name=name, grid=(4, nb),
        in_specs=[pl.BlockSpec((TILE, LANES), lambda p, i: (i, cb + p)),
                  pl.BlockSpec((s, LANES), lambda p, i: (0, cb + 4 + p)),
                  pl.BlockSpec((s, LANES), lambda p, i: (0, cb + 8 + p)),
                  pl.BlockSpec((None, TILE, LANES), lambda p, i: (p, i, 0)),
                  pl.BlockSpec((None, nb, 8, TILE), lambda p, i: (p, 0, 0, 0)),
                  qspec,
                  pl.BlockSpec((None, TILE, LANES), lambda p, i: (p, i, 0)),
                  pl.BlockSpec((TILE, LANES), lambda p, i: (i, db + p))],
        out_specs=[qspec, kvspec, kvspec, pl.BlockSpec((None, nb, 8, TILE), lambda p, i: (p, 0, 0, 0))],
        out_shape=[o3, o3, o3, jax.ShapeDtypeStruct((4, nb, 8, TILE), F32)],
        scratch_shapes=[pltpu.VMEM((2, s, HEAD_DIM), F32), pltpu.VMEM((2, s, HEAD_DIM), F32),
                        pltpu.VMEM((2, 2, TILE, TILE), BF16), pltpu.VMEM((2, 2, TILE, TILE), BF16)],
        compiler_params=_params("arbitrary", "arbitrary"),
    )(pa, pa, pa, ccol4, crow4, out, lse, dout)


def _mem_fwd(name, pa, mkv):
    s = pa.shape[0]
    ml = mkv.shape[0]
    nb = s // TILE
    cb = QKV_WIDTH // LANES

    def body(q_ref, k_ref, v_ref, o_ref, lse_ref):
        outs, lses = [], []
        for hh in range(2):
            sl = _head_slices(hh)
            sc = _dot_nt(_scaled_q(q_ref, sl), k_ref[:, sl])
            m = jnp.max(sc, axis=1, keepdims=True)
            p = jnp.exp(sc - m)
            l = jnp.sum(p, axis=1, keepdims=True)
            outs.append(_dot(p.astype(BF16), v_ref[:, sl]) / l)
            lses.append(jnp.broadcast_to(m + jnp.log(l), (TILE, HEAD_DIM)))
        o_ref[...] = jnp.concatenate(outs, axis=1)
        lse_ref[...] = jnp.concatenate(lses, axis=1)

    return pl.pallas_call(
        body, name=name, grid=(2, nb),
        in_specs=[pl.BlockSpec((TILE, LANES), lambda p, i: (i, cb + p)),
                  pl.BlockSpec((ml, LANES), lambda p, i: (0, p)),
                  pl.BlockSpec((ml, LANES), lambda p, i: (0, 2 + p))],
        out_specs=[pl.BlockSpec((TILE, LANES), lambda p, i: (i, p)),
                   pl.BlockSpec((None, TILE, LANES), lambda p, i: (p, i, 0))],
        out_shape=[jax.ShapeDtypeStruct((s, MEM_WIDTH), F32), jax.ShapeDtypeStruct((2, s, LANES), F32)],
        compiler_params=_params("parallel", "parallel"),
    )(pa, mkv, mkv)


def _mem_bwd(name, pa, mkv, out, lse, dout, dcol0):
    s = pa.shape[0]
    ml = mkv.shape[0]
    nb = s // TILE
    cb = QKV_WIDTH // LANES
    db = dcol0 // LANES

    def body(q_ref, k_ref, v_ref, o_ref, lse_ref, do_ref, dq_ref, dk_ref, dv_ref, dk_acc, dv_acc):
        i = pl.program_id(1)

        @pl.when(i == 0)
        def _():
            dk_acc[...] = jnp.zeros_like(dk_acc)
            dv_acc[...] = jnp.zeros_like(dv_acc)

        dqs = []
        for hh in range(2):
            sl = _head_slices(hh)
            q = _scaled_q(q_ref, sl)
            kh = k_ref[:, sl]
            dof = do_ref[:, sl]
            do = dof.astype(BF16)
            delta = jnp.sum(dof * o_ref[:, sl], axis=1, keepdims=True)
            p = jnp.exp(_dot_nt(q, kh) - lse_ref[:, HEAD_DIM * hh:HEAD_DIM * hh + 1])
            ds = (p * (_dot_nt(do, v_ref[:, sl]) - delta)).astype(BF16)
            dv_acc[hh] += _dot_tn(p.astype(BF16), do)
            dk_acc[hh] += _dot_tn(ds, q)
            dqs.append(_dot(ds, kh) * SCALE)
        dq_ref[...] = jnp.concatenate(dqs, axis=1).astype(BF16)

        @pl.when(i == nb - 1)
        def _():
            dk_ref[...] = jnp.concatenate([dk_acc[0], dk_acc[1]], axis=1).astype(BF16)
            dv_ref[...] = jnp.concatenate([dv_acc[0], dv_acc[1]], axis=1).astype(BF16)

    qspec = pl.BlockSpec((TILE, LANES), lambda p, i: (i, p))
    kvspec = pl.BlockSpec((ml, LANES), lambda p, i: (0, p))
    okv = jax.ShapeDtypeStruct((ml, MEM_WIDTH), BF16)
    return pl.pallas_call(
        body, name=name, grid=(2, nb),
        in_specs=[pl.BlockSpec((TILE, LANES), lambda p, i: (i, cb + p)),
                  pl.BlockSpec((ml, LANES), lambda p, i: (0, p)),
                  pl.BlockSpec((ml, LANES), lambda p, i: (0, 2 + p)),
                  qspec,
                  pl.BlockSpec((None, TILE, LANES), lambda p, i: (p, i, 0)),
                  pl.BlockSpec((TILE, LANES), lambda p, i: (i, db + p))],
        out_specs=[qspec, kvspec, kvspec],
        out_shape=[jax.ShapeDtypeStruct((s, MEM_WIDTH), BF16), okv, okv],
        scratch_shapes=[pltpu.VMEM((2, ml, HEAD_DIM), F32), pltpu.VMEM((2, ml, HEAD_DIM), F32)],
        compiler_params=_params("arbitrary", "arbitrary"),
    )(pa, mkv, mkv, out, lse, dout)


def _head_maps():
    col = jnp.arange(MIX_WIDTH)[:, None] // HEAD_DIM
    g = (col == jnp.arange(LANES)[None, :]).astype(BF16)
    return g, g.T


def _normed_heads(osb_ref, ofx_ref, om_ref, g_ref, gt_ref):
    y = jnp.concatenate([osb_ref[...], ofx_ref[...], om_ref[...]], axis=1)
    msq = _sum_l3(y * y, g_ref[...]) * (1.0 / HEAD_DIM)
    rf = _sum_l3(lax.rsqrt(msq + EPS), gt_ref[...])
    return y * rf, rf


def _out_fwd(name, o_sb, o_fx, o_m, pb, ow, x, w_out, ts):
    s, d = x.shape
    g, gt = _head_maps()

    def body(osb_ref, ofx_ref, om_ref, gate_ref, ow_ref, x_ref, w_ref, g_ref, gt_ref, xo_ref, y2_ref):
        yh, _ = _normed_heads(osb_ref, ofx_ref, om_ref, g_ref, gt_ref)
        gate = gate_ref[...]
        y2 = (yh * ow_ref[...] * (gate * jax.nn.sigmoid(gate))).astype(BF16)
        y2_ref[...] = y2
        xo_ref[...] = x_ref[...] + _dot(y2, w_ref[...])

    return pl.pallas_call(
        body, name=name, grid=(s // ts,),
        in_specs=[_row_spec(ts, SB_WIDTH), _row_spec(ts, FOX_WIDTH), _row_spec(ts, MEM_WIDTH),
                  _row_spec(ts, MIX_WIDTH), _const_spec((1, MIX_WIDTH)), _row_spec(ts, d),
                  _const_spec((MIX_WIDTH, d)),
                  _const_spec((MIX_WIDTH, LANES)), _const_spec((LANES, MIX_WIDTH))],
        out_specs=[_row_spec(ts, d), _row_spec(ts, MIX_WIDTH)],
        out_shape=[jax.ShapeDtypeStruct((s, d), F32), jax.ShapeDtypeStruct((s, MIX_WIDTH), BF16)],
        compiler_params=_params("parallel"),
    )(o_sb, o_fx, o_m, pb, ow, x, w_out, g, gt)


def _row_spec(ts, w):
    return pl.BlockSpec((ts, w), lambda i: (i, 0))


def _const_spec(shape):
    return pl.BlockSpec(shape, lambda i: (0,) * len(shape))


def _out_bwd(name, dxb, o_sb, o_fx, o_m, pb, ow, w_out, ts):
    s, d = dxb.shape
    g, gt = _head_maps()

    def body(dx_ref, osb_ref, ofx_ref, om_ref, gate_ref, ow_ref, w_ref, g_ref, gt_ref, dy_ref, dgate_ref, dow_ref):
        @pl.when(pl.program_id(0) == 0)
        def _():
            dow_ref[...] = jnp.zeros_like(dow_ref)

        dy2 = _dot_nt(dx_ref[...], w_ref[...])
        yh, rf = _normed_heads(osb_ref, ofx_ref, om_ref, g_ref, gt_ref)
        gate = gate_ref[...]
        sig = jax.nn.sigmoid(gate)
        ow_v = ow_ref[...]
        dgate_ref[...] = (dy2 * (yh * ow_v) * (sig * (1.0 + gate * (1.0 - sig)))).astype(BF16)
        dn = dy2 * (gate * sig)
        dow_ref[...] += jnp.sum(dn * yh, axis=0, keepdims=True)
        dyh = dn * ow_v
        t = _sum_l3(dyh * yh, g_ref[...]) * (1.0 / HEAD_DIM)
        dy_ref[...] = rf * (dyh - yh * _sum_l3(t, gt_ref[...]))

    return pl.pallas_call(
        body, name=name, grid=(s // ts,),
        in_specs=[_row_spec(ts, d), _row_spec(ts, SB_WIDTH), _row_spec(ts, FOX_WIDTH), _row_spec(ts, MEM_WIDTH),
                  _row_spec(ts, MIX_WIDTH), _const_spec((1, MIX_WIDTH)),
                  _const_spec((MIX_WIDTH, d)),
                  _const_spec((MIX_WIDTH, LANES)), _const_spec((LANES, MIX_WIDTH))],
        out_specs=[_row_spec(ts, MIX_WIDTH), _row_spec(ts, MIX_WIDTH), _const_spec((1, MIX_WIDTH))],
        out_shape=[jax.ShapeDtypeStruct((s, MIX_WIDTH), F32), jax.ShapeDtypeStruct((s, MIX_WIDTH), BF16),
                   jax.ShapeDtypeStruct((1, MIX_WIDTH), F32)],
        compiler_params=_params("arbitrary"),
    )(dxb, o_sb, o_fx, o_m, pb, ow, w_out, g, gt)


def _adamw(name, w, g, m, v, tr):
    rows, cols = w.shape

    def body(w_ref, g_ref, m_ref, v_ref, d_ref, m2_ref, v2_ref):
        gv = g_ref[...]
        m2 = ADAM_B1 * m_ref[...] + (1.0 - ADAM_B1) * gv
        v2 = ADAM_B2 * v_ref[...] + (1.0 - ADAM_B2) * (gv * gv)
        m_hat = m2 / (1.0 - ADAM_B1 ** ADAM_STEP)
        v_hat = v2 / (1.0 - ADAM_B2 ** ADAM_STEP)
        d_ref[...] = -ADAM_LR * (m_hat / (jnp.sqrt(v_hat) + ADAM_EPS) + ADAM_WD * w_ref[...])
        m2_ref[...] = m2
        v2_ref[...] = v2

    spec = _row_spec(tr, cols)
    shp = jax.ShapeDtypeStruct((rows, cols), F32)
    return pl.pallas_call(
        body, name=name, grid=(rows // tr,), in_specs=[spec] * 4, out_specs=[spec] * 3, out_shape=[shp] * 3,
        compiler_params=_params("parallel"),
    )(w, g, m, v)


def _adamw_sharded(name, w, m, v, g_own, g_other, cvec, tr):
    depth, rows, cols = w.shape
    nt = rows // 2 // tr

    def body(c_ref, w_ref, m_ref, v_ref, *rest):
        g_refs, (g_ref, d_ref, m2_ref, v2_ref) = rest[:2 * depth], rest[2 * depth:]
        layer, mine = pl.program_id(0), pl.program_id(1) == c_ref[0]
        gv = None
        for lt in range(depth):
            cand = jnp.where(mine, g_refs[lt][...], g_refs[depth + lt][...])
            gv = cand if gv is None else jnp.where(layer == lt, cand, gv)
        m2 = ADAM_B1 * m_ref[...] + (1.0 - ADAM_B1) * gv
        v2 = ADAM_B2 * v_ref[...] + (1.0 - ADAM_B2) * (gv * gv)
        m_hat = m2 / (1.0 - ADAM_B1 ** ADAM_STEP)
        v_hat = v2 / (1.0 - ADAM_B2 ** ADAM_STEP)
        g_ref[...] = gv
        d_ref[...] = -ADAM_LR * (m_hat / (jnp.sqrt(v_hat) + ADAM_EPS) + ADAM_WD * w_ref[...])
        m2_ref[...] = m2
        v2_ref[...] = v2

    def g_map(lt, own):
        def index(l, hf, i, c_ref):
            use = jnp.logical_and(l == lt, (hf == c_ref[0]) == own)
            return jnp.where(use, i, 0), 0
        return index

    full = pl.BlockSpec((None, tr, cols), lambda l, hf, i, c_ref: (l, hf * nt + i, 0))
    g_specs = [pl.BlockSpec((tr, cols), g_map(lt, own)) for own in (True, False) for lt in range(depth)]
    shp = jax.ShapeDtypeStruct((depth, rows, cols), F32)
    return pl.pallas_call(
        body, name=name,
        grid_spec=pltpu.PrefetchScalarGridSpec(
            num_scalar_prefetch=1, grid=(depth, 2, nt), in_specs=[full] * 3 + g_specs, out_specs=[full] * 4),
        out_shape=[shp] * 4,
        compiler_params=_params("arbitrary", "arbitrary", "arbitrary"),
    )(cvec, w, m, v, *g_own, *g_other)


HBM_SPEC = pl.BlockSpec(memory_space=pltpu.HBM)


def _place():
    x, y, c = lax.axis_index("x"), lax.axis_index("y"), lax.axis_index("c")
    chips = [(1 - x, y), (x, 1 - y), (1 - x, 1 - y)]
    return x, y, c, chips


def _remote(src, dst, send_sems, recv_sems, k, to):
    return pltpu.make_async_remote_copy(src_ref=src, dst_ref=dst, send_sem=send_sems.at[k], recv_sem=recv_sems.at[k],
                                        device_id=to, device_id_type=MESH)


def _half_rows(n_rows, cc):
    rh = n_rows // 2
    return pl.ds(pl.multiple_of(cc * rh, 16), rh)


def _dma_sems(n):
    return [pltpu.SemaphoreType.DMA((n,)), pltpu.SemaphoreType.DMA((n,))]


class _Exchange:
    def __init__(self, inputs, out_shapes, n_sems, begin, relay, finish):
        self.inputs, self.out_shapes, self.n_sems = list(inputs), list(out_shapes), n_sems
        self.begin, self.relay, self.finish = begin, relay, finish

    @property
    def n(self):
        return len(self.inputs)

    def split(self, refs):
        return refs[:self.n], refs[self.n:2 * self.n], refs[2 * self.n], refs[2 * self.n + 1]


def _run_exchange(name, ex):
    def body(*refs):
        parts = ex.split(refs)
        for phase in (ex.begin, ex.relay, ex.finish):
            if phase is not None:
                phase(*parts)

    return pl.pallas_call(
        body, name=name, in_specs=[HBM_SPEC] * ex.n, out_specs=[HBM_SPEC] * ex.n, out_shape=ex.out_shapes,
        scratch_shapes=_dma_sems(ex.n_sems),
    )(*ex.inputs)


def _gather_exchange(shards):
    def ici(in_refs, out_refs, send_sems, recv_sems):
        x, y, c, chips = _place()
        return [_remote(in_ref.at[_half_rows(in_ref.shape[0], c)], out_ref.at[2 * x + y, _half_rows(in_ref.shape[0], c)],
                        send_sems, recv_sems, 6 * a + j, (cx, cy, c))
                for a, (in_ref, out_ref) in enumerate(zip(in_refs, out_refs)) for j, (cx, cy) in enumerate(chips)]

    def d2d(out_refs, send_sems, recv_sems, half_of):
        x, y, c, chips = _place()
        cps = []
        for a, out_ref in enumerate(out_refs):
            for j, (cx, cy) in enumerate(chips):
                piece = out_ref.at[2 * cx + cy, _half_rows(out_ref.shape[1], half_of(c))]
                cps.append(_remote(piece, piece, send_sems, recv_sems, 6 * a + 3 + j, (x, y, 1 - c)))
        return cps

    def begin(in_refs, out_refs, send_sems, recv_sems):
        for cp in ici(in_refs, out_refs, send_sems, recv_sems):
            cp.start()

    def relay(in_refs, out_refs, send_sems, recv_sems):
        x, y, c, chips = _place()
        for a, out_ref in enumerate(out_refs):
            for j, (cx, cy) in enumerate(chips):
                landed = out_ref.at[2 * cx + cy, _half_rows(out_ref.shape[1], c)]
                _remote(landed, landed, send_sems, recv_sems, 6 * a + j, (cx, cy, c)).wait_recv()
        for cp in d2d(out_refs, send_sems, recv_sems, lambda c_: c_):
            cp.start()

    def finish(in_refs, out_refs, send_sems, recv_sems):
        for cp in d2d(out_refs, send_sems, recv_sems, lambda c_: 1 - c_):
            cp.wait_recv()
        for cp in ici(in_refs, out_refs, send_sems, recv_sems) + d2d(out_refs, send_sems, recv_sems, lambda c_: c_):
            cp.wait_send()

    shapes = [jax.ShapeDtypeStruct((N_CHIPS,) + s_.shape, s_.dtype) for s_ in shards]
    return _Exchange(shards, shapes, 6 * len(shards), begin, relay, finish)


def _swap_halves(name, g4s):
    n = len(g4s)

    def body(*refs):
        in_refs, out_refs, (send_sems, recv_sems) = refs[:n], refs[n:2 * n], refs[2 * n:]
        x, y, c, _ = _place()
        cps = [_remote(in_ref.at[:, _half_rows(in_ref.shape[1], 1 - c), :], out_ref, send_sems, recv_sems, a, (x, y, 1 - c))
               for a, (in_ref, out_ref) in enumerate(zip(in_refs, out_refs))]
        for cp in cps:
            cp.start()
        for cp in cps:
            cp.wait()

    return pl.pallas_call(
        body, name=name, in_specs=[HBM_SPEC] * n, out_specs=[HBM_SPEC] * n,
        out_shape=[jax.ShapeDtypeStruct((g.shape[0], g.shape[1] // 2, g.shape[2]), g.dtype) for g in g4s],
        scratch_shapes=_dma_sems(n),
    )(*g4s)


def _add_half(name, g4, r1, cvec, tr):
    n, r, w = g4.shape
    rh = r // 2
    nblk = rh // tr

    def body(c_ref, a_ref, b_ref, o_ref):
        o_ref[...] = (a_ref[...] + b_ref[...]).astype(BF16)

    return pl.pallas_call(
        body, name=name,
        grid_spec=pltpu.PrefetchScalarGridSpec(
            num_scalar_prefetch=1, grid=(n, nblk),
            in_specs=[pl.BlockSpec((None, tr, w), lambda k, i, c_ref: (k, c_ref[0] * nblk + i, 0)),
                      pl.BlockSpec((None, tr, w), lambda k, i, c_ref: (k, i, 0))],
            out_specs=pl.BlockSpec((None, tr, w), lambda k, i, c_ref: (k, i, 0))),
        out_shape=jax.ShapeDtypeStruct((n, rh, w), BF16),
        compiler_params=_params("parallel", "parallel"),
    )(cvec, g4, r1)


def _scatter_exchange(h4s):
    def sends(in_refs, out_refs, send_sems, recv_sems):
        x, y, c, chips = _place()
        return [_remote(in_ref.at[2 * cx + cy], out_ref.at[j], send_sems, recv_sems, 3 * a + j, (cx, cy, c))
                for a, (in_ref, out_ref) in enumerate(zip(in_refs, out_refs)) for j, (cx, cy) in enumerate(chips)]

    def begin(*parts):
        for cp in sends(*parts):
            cp.start()

    def finish(in_refs, out_refs, send_sems, recv_sems):
        x, y, c, chips = _place()
        for a, out_ref in enumerate(out_refs):
            for j, (cx, cy) in enumerate(chips):
                got = out_ref.at[j]
                _remote(got, got, send_sems, recv_sems, 3 * a + j, (cx, cy, c)).wait_recv()
        for cp in sends(in_refs, out_refs, send_sems, recv_sems):
            cp.wait_send()

    shapes = [jax.ShapeDtypeStruct((3,) + h.shape[1:], h.dtype) for h in h4s]
    return _Exchange(h4s, shapes, 3 * len(h4s), begin, None, finish)


def _sum_chips(name, h4, r3, mvec, tr):
    _, rh, w = h4.shape

    def body(m_ref, a_ref, b_ref, c_ref, d_ref, o_ref):
        o_ref[...] = ((a_ref[...].astype(F32) + b_ref[...].astype(F32)) + c_ref[...].astype(F32)) + d_ref[...].astype(F32)

    specs = [pl.BlockSpec((None, tr, w), lambda i, m_ref: (m_ref[0], i, 0))]
    specs += [pl.BlockSpec((None, tr, w), functools.partial(lambda k, i, m_ref: (k, i, 0), k)) for k in range(3)]
    return pl.pallas_call(
        body, name=name,
        grid_spec=pltpu.PrefetchScalarGridSpec(
            num_scalar_prefetch=1, grid=(rh // tr,), in_specs=specs,
            out_specs=pl.BlockSpec((tr, w), lambda i, m_ref: (i, 0))),
        out_shape=jax.ShapeDtypeStruct((rh, w), F32),
        compiler_params=_params("parallel"),
    )(mvec, h4, r3, r3, r3)


def _swap_reduced(name, ghs):
    n = len(ghs)

    def body(*refs):
        in_refs, out_refs, (send_sems, recv_sems) = refs[:n], refs[n:2 * n], refs[2 * n:]
        x, y, c, _ = _place()
        cps = [_remote(in_ref, out_ref, send_sems, recv_sems, a, (x, y, 1 - c))
               for a, (in_ref, out_ref) in enumerate(zip(in_refs, out_refs))]
        for cp in cps:
            cp.start()
        for cp in cps:
            cp.wait()

    return pl.pallas_call(
        body, name=name, in_specs=[HBM_SPEC] * n, out_specs=[HBM_SPEC] * n,
        out_shape=[jax.ShapeDtypeStruct(g.shape, g.dtype) for g in ghs],
        scratch_shapes=_dma_sems(n),
    )(*ghs)


def _small_update(name, partials, weights, moments1, moments2):
    n = len(partials)
    width = max(p.shape[1] for p in partials)
    starts, at = [], 0
    for p in partials:
        starts.append(at)
        at += -(-p.shape[0] // 8) * 8
    rows = at
    has_w = [w is not None for w in weights]
    n_w = sum(has_w)

    def body(*refs):
        p_refs = refs[:n]
        w_refs, m_refs, v_refs = refs[n:n + n_w], refs[n + n_w:n + 2 * n_w], refs[n + 2 * n_w:n + 3 * n_w]
        outs = refs[n + 3 * n_w:-4]
        g_refs, upd_refs = outs[:n], outs[n:]
        vec, buf, send_sems, recv_sems = refs[-4:]
        x, y, c, _ = _place()
        me = 4 * x + 2 * y + c
        vec[...] = jnp.zeros_like(vec)
        for p_ref, r0 in zip(p_refs, starts):
            vec[r0:r0 + p_ref.shape[0], 0:p_ref.shape[1]] = p_ref[...]
        buf[me] = vec[...]
        flips = [(fx, fy, fc) for fx in (0, 1) for fy in (0, 1) for fc in (0, 1)][1:]
        peers = [(x + fx - 2 * x * fx, y + fy - 2 * y * fy, c + fc - 2 * c * fc) for fx, fy, fc in flips]
        sends = [_remote(vec, buf.at[me], send_sems, recv_sems, k, peer) for k, peer in enumerate(peers)]
        for cp in sends:
            cp.start()
        for k, (px, py, pc) in enumerate(peers):
            got = buf.at[4 * px + 2 * py + pc]
            _remote(got, got, send_sems, recv_sems, k, (px, py, pc)).wait_recv()
        for cp in sends:
            cp.wait_send()
        total = buf[0]
        for dev in range(1, N_DEV):
            total = total + buf[dev]
        k = 0
        for a in range(n):
            r, w = g_refs[a].shape
            g = total[starts[a]:starts[a] + r, 0:w]
            g_refs[a][...] = g
            if has_w[a]:
                m2 = ADAM_B1 * m_refs[k][...] + (1.0 - ADAM_B1) * g
                v2 = ADAM_B2 * v_refs[k][...] + (1.0 - ADAM_B2) * (g * g)
                m_hat = m2 / (1.0 - ADAM_B1 ** ADAM_STEP)
                v_hat = v2 / (1.0 - ADAM_B2 ** ADAM_STEP)
                upd_refs[3 * k][...] = -ADAM_LR * (m_hat / (jnp.sqrt(v_hat) + ADAM_EPS) + ADAM_WD * w_refs[k][...])
                upd_refs[3 * k + 1][...] = m2
                upd_refs[3 * k + 2][...] = v2
                k += 1

    ws = [w for w in weights if w is not None]
    g_shapes = [jax.ShapeDtypeStruct(p.shape if w is None else w.shape, F32) for p, w in zip(partials, weights)]
    u_shapes = [jax.ShapeDtypeStruct(w.shape, F32) for w in ws for _ in range(3)]
    vm = pl.BlockSpec(memory_space=pltpu.VMEM)
    n_args = n + 3 * n_w
    outs = pl.pallas_call(
        body, name=name, in_specs=[vm] * n_args, out_specs=[vm] * (n + 3 * n_w), out_shape=g_shapes + u_shapes,
        scratch_shapes=[pltpu.VMEM((rows, width), F32), pltpu.VMEM((N_DEV, rows, width), F32),
                        pltpu.SemaphoreType.DMA((7,)), pltpu.SemaphoreType.DMA((7,))],
    )(*partials, *ws, *[m for m in moments1 if m is not None], *[v for v in moments2 if v is not None])
    return outs[:n], outs[n:]


GATE_COL = 3 * SB_WIDTH + 3 * FOX_WIDTH + FOX_HEADS + MEM_WIDTH
FL_COL = QKV_WIDTH


GROUP_A_COLS = [(0, QKV_WIDTH), (FL_COL + FOX_HEADS, MEM_WIDTH)]
GROUP_B_COLS = [(GATE_COL, MIX_WIDTH), (FL_COL, FOX_HEADS)]


def _group_from_shards(shard_of, cw, spans, pad):
    parts = []
    for lo, width in spans:
        hi = lo + width
        for j in range(N_CHIPS):
            a, b = max(lo, j * cw), min(hi, (j + 1) * cw)
            if a < b:
                parts.append(shard_of(j)[:, a - j * cw:b - j * cw])
    if pad:
        parts.append(jnp.zeros((parts[0].shape[0], pad), parts[0].dtype))
    return jnp.concatenate(parts, axis=1)


def _shard_from_groups(ga, gb, j, cw):
    lo, hi = j * cw, (j + 1) * cw
    placed = []
    for grp, spans in ((ga, GROUP_A_COLS), (gb, GROUP_B_COLS)):
        at = 0
        for first, width in spans:
            a, b = max(lo, first), min(hi, first + width)
            if a < b:
                placed.append((a, grp[:, at + a - first:at + b - first]))
            at += width
    return jnp.concatenate([p for _, p in sorted(placed, key=lambda t: t[0])], axis=1)


def _tile_of(n, cap, unit):
    if n <= cap:
        return n
    best = None
    for t in range(unit, cap + 1, unit):
        if n % t == 0:
            best = t
    assert best is not None, (n, cap, unit)
    return best


def _column_major_rows(a):
    dp, r, c = a.shape
    return a.transpose(2, 0, 1).reshape(c, dp, r // LANES, LANES).transpose(0, 2, 1, 3).reshape(-1, LANES)


def _from_column_major_rows(b, shape):
    dp, r, c = shape
    return b.reshape(c, r // LANES, dp, LANES).transpose(0, 2, 1, 3).reshape(c, dp, r).transpose(1, 2, 0)


def _pack_small(parts):
    rows = []
    for p in parts:
        f = p.reshape(-1).astype(F32)
        f = jnp.pad(f, (0, (-f.shape[0]) % LANES))
        rows.append(f.reshape(-1, LANES))
    out = jnp.concatenate(rows, axis=0)
    return jnp.pad(out, ((0, (-out.shape[0]) % 8), (0, 0)))


def _unpack_small(packed, shapes):
    outs, r = [], 0
    for shp in shapes:
        n = 1
        for s_ in shp:
            n *= s_
        nr = -(-n // LANES)
        outs.append(packed[r:r + nr].reshape(-1)[:n].reshape(shp))
        r += nr
    return outs


def kernel(x, mem, norm_w, w_in, b_forget, mem_norm_w, w_mem_kv, out_norm_w, w_out, final_norm_w, loss_target, m_norm_w, m_w_in, m_b_forget, m_mem_norm_w, m_w_mem_kv, m_out_norm_w, m_w_out, m_final_norm_w, v_norm_w, v_w_in, v_b_forget, v_mem_norm_w, v_w_mem_kv, v_out_norm_w, v_w_out, v_final_norm_w):
    xs = x[0]
    mems = mem[0]
    target = loss_target[0]
    s, d = xs.shape
    depth = norm_w.shape[0]
    nb = s // TILE
    ts = _tile_of(s, 256, 8)
    big = (w_in, w_mem_kv, w_out)
    core = lax.axis_index("c")
    chip = 2 * lax.axis_index("x") + lax.axis_index("y")
    cvec = core.astype(jnp.int32).reshape(1)
    mvec = chip.astype(jnp.int32).reshape(1)
    cw = w_in.shape[2]

    own_w = [[a[l].astype(BF16) for a in big] for l in range(depth)]

    def lay_out(own, got):
        full = [jnp.where(lax.broadcasted_iota(jnp.int32, g.shape, 0) == chip, o[None], g) for g, o in zip(got, own)]
        shard_of = lambda j: full[0][j]
        wa_l = _group_from_shards(shard_of, cw, GROUP_A_COLS, 0)
        wb_l = _group_from_shards(shard_of, cw, GROUP_B_COLS, LANES - FOX_HEADS)
        return wa_l, wb_l, full[1].reshape(-1, full[1].shape[2]), full[2].reshape(-1, full[2].shape[2])

    layer_w = [lay_out(own_w[0], _run_exchange("gather_weights0", _gather_exchange(own_w[0])))]

    tm = _tile_of(s, 256, 8)
    fl_block = MIX_WIDTH // LANES

    saved = []
    cur = xs
    for l in range(depth):
        wa, wb, wkv, wout = layer_w[l]
        h = _rms_fwd(f"rms_fwd{l}", cur, norm_w[l][None], ts)
        pa = _mm(f"inproj_a{l}", h, wa, "nn", tm, _tile_of(PA, 1664, LANES), BF16)
        pb = _mm(f"inproj_b{l}", h, wb, "nn", tm, PB, F32)
        bpad = jnp.pad(b_forget[l], (0, LANES - FOX_HEADS))[None]
        ccol, crow = _gate_fwd(f"gate_fwd{l}", pb, bpad, fl_block)
        ccol4 = jnp.repeat(ccol[:, :FOX_HEADS].reshape(s, 4, 2).transpose(1, 0, 2), HEAD_DIM, axis=2)
        crow4 = jnp.pad(crow.reshape(nb, 4, 2, TILE).transpose(1, 0, 2, 3), ((0, 0), (0, 0), (0, 6), (0, 0)))
        next_gather = _gather_exchange(own_w[l + 1]) if l + 1 < depth else None
        o_sb, got = _sb_fwd(f"sb_fwd{l}", pa, 0, carried=next_gather)
        if next_gather is not None:
            layer_w.append(lay_out(own_w[l + 1], got))
        o_fx, lse_fx = _fox_fwd(f"fox_fwd{l}", pa, 3 * SB_WIDTH, ccol4, crow4)
        mn = _rms_fwd(f"mem_rms{l}", mems, mem_norm_w[l][None], mems.shape[0])
        mkv = _mm(f"mem_kv{l}", mn, wkv, "nn", mems.shape[0], 2 * MEM_WIDTH, BF16)
        o_m, lse_m = _mem_fwd(f"mem_fwd{l}", pa, mkv)
        nxt, y2 = _out_fwd(f"out_fwd{l}", o_sb, o_fx, o_m, pb, out_norm_w[l][None], cur, wout, ts)
        saved.append((cur, h, pa, pb, bpad, ccol4, crow4, o_sb, o_fx, lse_fx, mn, mkv, o_m, lse_m, y2))
        cur = nxt

    loss_v, dx, dxb, g_final = _final_loss("final_loss", cur, final_norm_w[None], target, ts)

    g_norm, g_b, g_memnorm, g_outnorm = [None] * depth, [None] * depth, [None] * depth, [None] * depth
    g_wa, g_wb, g_wkv, g_wout = [None] * depth, [None] * depth, [None] * depth, [None] * depth
    g_own = [[None] * depth for _ in big]
    g_other = [[None] * depth for _ in big]

    def reduce_at_owner(lr, chip_sums, from_chips, tiles):
        halves = [_sum_chips(f"grad_sum_chips{lr}_{k}", h_, r_, mvec, t_)
                  for k, (h_, r_, t_) in enumerate(zip(chip_sums, from_chips, tiles))]
        others = _swap_reduced(f"grad_swap_reduced{lr}", halves)
        for k in range(len(big)):
            g_own[k][lr], g_other[k][lr] = halves[k], others[k]

    pending = None
    for l in reversed(range(depth)):
        xin, h, pa, pb, bpad, ccol4, crow4, o_sb, o_fx, lse_fx, mn, mkv, o_m, lse_m, y2 = saved[l]
        wa, wb, wkv, wout = layer_w[l]
        dy, dgate, g_outnorm[l] = _out_bwd(f"out_bwd{l}", dxb, o_sb, o_fx, o_m, pb, out_norm_w[l][None], wout, ts)
        g_wout[l] = _mm(f"dw_out{l}", y2, dxb, "tn", _tile_of(MIX_WIDTH, 640, LANES), d, F32)
        scatter = _scatter_exchange(pending[1]) if pending is not None else None
        (dq_sb, dk_sb, dv_sb), from_chips = _sb_bwd(f"sb_bwd{l}", pa, 0, dy, 0, carried=scatter)
        if pending is not None:
            reduce_at_owner(pending[0], pending[1], from_chips, pending[2])
        dq_fx, dk_fx, dv_fx, cs4 = _fox_bwd(f"fox_bwd{l}", pa, 3 * SB_WIDTH, ccol4, crow4, o_fx, lse_fx, dy, SB_WIDTH)
        colsum = cs4[:, :, :2, :].transpose(1, 0, 2, 3).reshape(nb, 8, TILE)
        dlogit, g_b[l] = _gate_bwd(f"gate_bwd{l}", pb, bpad, colsum, fl_block)
        dq_m, dk_m, dv_m = _mem_bwd(f"mem_bwd{l}", pa, mkv, o_m, lse_m, dy, SB_WIDTH + FOX_WIDTH)
        dmkv = jnp.concatenate([dk_m, dv_m], axis=1)
        g_wkv[l] = _mm(f"dw_kv{l}", mn, dmkv, "tn", d, 2 * MEM_WIDTH, F32)
        dmn = _mm(f"dmem{l}", dmkv, wkv, "nt", mems.shape[0], d, F32)
        g_memnorm[l] = _rms_wgrad(f"mem_norm_grad{l}", mems, dmn)
        dpa = jnp.concatenate([dq_sb, dk_sb, dv_sb, dq_fx, dk_fx, dv_fx, dq_m], axis=1)
        dpb = jnp.concatenate([dgate, dlogit], axis=1)
        tw = _tile_of(d, 512, LANES)
        g_wa[l] = _mm(f"dw_in_a{l}", h, dpa, "tn", tw, _tile_of(PA, 1664, LANES), F32)
        g_wb[l] = _mm(f"dw_in_b{l}", h, dpb, "tn", tw, PB, F32)
        dh = _mm(f"dh_a{l}", dpa, wa, "nt", tm, d, F32)
        dh = _mm(f"dh_b{l}", dpb, wb, "nt", tm, d, F32, res=dh)
        dx, dxb, g_norm[l] = _rms_bwd(f"rms_bwd{l}", xin, norm_w[l][None], dh, dx, ts)
        g4s = [jnp.stack([_shard_from_groups(g_wa[l], g_wb[l], j, cw) for j in range(N_CHIPS)]),
               g_wkv[l].reshape(N_CHIPS, -1, g_wkv[l].shape[1]), g_wout[l].reshape(N_CHIPS, -1, d)]
        tiles = [_tile_of(g.shape[1] // 2, 256, 16) for g in g4s]
        from_sibling = _swap_halves(f"grad_swap_halves{l}", g4s)
        chip_sums = [_add_half(f"grad_add_half{l}_{k}", g, r, cvec, t)
                     for k, (g, r, t) in enumerate(zip(g4s, from_sibling, tiles))]
        pending = (l, chip_sums, tiles)
    reduce_at_owner(pending[0], pending[1], _run_exchange(f"grad_scatter_chips{pending[0]}", _scatter_exchange(pending[1])),
                    pending[2])

    small_w = [norm_w, b_forget, mem_norm_w, out_norm_w, final_norm_w]
    small_m = [m_norm_w, m_b_forget, m_mem_norm_w, m_out_norm_w, m_final_norm_w]
    small_v = [v_norm_w, v_b_forget, v_mem_norm_w, v_out_norm_w, v_final_norm_w]
    rows2 = lambda a: a.reshape(-1, a.shape[-1])
    partials = [jnp.concatenate(g_norm, axis=0), jnp.concatenate(g_b, axis=0), jnp.concatenate(g_memnorm, axis=0),
                jnp.concatenate(g_outnorm, axis=0), g_final, loss_v]
    sums, updates = _small_update("small_update", partials, [rows2(a) for a in small_w] + [None],
                                  [rows2(a) for a in small_m] + [None], [rows2(a) for a in small_v] + [None])
    small_grads = [g.reshape(a.shape) for g, a in zip(sums, small_w)]
    loss = sums[-1][0, 0]
    small_delta, small_m2, small_v2 = ([updates[3 * k + t].reshape(a.shape) for k, a in enumerate(small_w)]
                                       for t in range(3))
    big_grads, big_delta, big_m2, big_v2 = [], [], [], []
    for k, (nm, w_, m_, v_) in enumerate(zip(("w_in", "w_mem_kv", "w_out"), big, (m_w_in, m_w_mem_kv, m_w_out),
                                             (v_w_in, v_w_mem_kv, v_w_out))):
        if w_.shape[2] % LANES:
            g_full = jnp.stack([jnp.concatenate([jnp.where(core == 0, go, gt), jnp.where(core == 0, gt, go)], axis=0)
                                for go, gt in zip(g_own[k], g_other[k])])
            w_p, g_p, m_p, v_p = (_column_major_rows(a) for a in (w_, g_full, m_, v_))
            outs = _adamw(f"adamw_{nm}", w_p, g_p, m_p, v_p, _tile_of(w_p.shape[0], 2048, 8))
            outs = [_from_column_major_rows(o, w_.shape) for o in (g_p, *outs)]
        else:
            outs = _adamw_sharded(f"adamw_{nm}", w_, m_, v_, g_own[k], g_other[k], cvec,
                                  _tile_of(w_.shape[1] // 2, 256, 8))
        for lst, o in zip((big_grads, big_delta, big_m2, big_v2), outs):
            lst.append(o)

    def order(sm, bg):
        return [sm[0], bg[0], sm[1], sm[2], bg[1], sm[3], bg[2], sm[4]]

    return (loss, dx[None], *order(small_grads, big_grads), *order(small_delta, big_delta),
            *order(small_m2, big_m2), *order(small_v2, big_v2))
```

```python
import functools

import jax
import jax.numpy as jnp
from jax import lax
from jax.experimental import pallas as pl
from jax.experimental.pallas import tpu as pltpu

F32 = jnp.float32
BF16 = jnp.bfloat16

HEAD_DIM = 64
SB_WIDTH = 512
FOX_WIDTH = 512
FOX_HEADS = 8
MEM_WIDTH = 256
MIX_WIDTH = SB_WIDTH + FOX_WIDTH + MEM_WIDTH
TOTAL_HEADS = MIX_WIDTH // HEAD_DIM
IN_WIDTH = 3 * SB_WIDTH + 3 * FOX_WIDTH + FOX_HEADS + MEM_WIDTH + MIX_WIDTH
LANES = 128
QKV_WIDTH = 3 * SB_WIDTH + 3 * FOX_WIDTH
PA = QKV_WIDTH + MEM_WIDTH
PB = LANES + MIX_WIDTH
EPS = 1e-6
SCALE = HEAD_DIM ** -0.5
TILE = 256
NEG_INF = float("-inf")
MASKED = -1e30

ADAM_LR = 0.001
ADAM_B1 = 0.9
ADAM_B2 = 0.999
ADAM_EPS = 1e-08
ADAM_WD = 0.01
ADAM_STEP = 10

N_CHIPS = 4
N_DEV = 8
VMEM_LIMIT = 48 * 1024 * 1024
MESH = pl.DeviceIdType.MESH


def _params(*sem):
    return pltpu.CompilerParams(dimension_semantics=tuple(sem), vmem_limit_bytes=VMEM_LIMIT)


def _dot(a, b):
    return jnp.dot(a, b, preferred_element_type=F32)


def _dot_nt(a, b):
    return lax.dot_general(a, b, (((1,), (1,)), ((), ())), preferred_element_type=F32)


def _dot_tn(a, b):
    return lax.dot_general(a, b, (((0,), (0,)), ((), ())), preferred_element_type=F32)


def _split2(x):
    hi = x.astype(BF16)
    lo = (x - hi.astype(F32)).astype(BF16)
    return hi, lo


def _split3(x):
    hi = x.astype(BF16)
    r = x - hi.astype(F32)
    mid = r.astype(BF16)
    lo = (r - mid.astype(F32)).astype(BF16)
    return hi, mid, lo


def _sum_l2(x, u):
    hi, lo = _split2(x)
    return _dot(hi, u) + _dot(lo, u)


def _sum_l3(x, u):
    hi, mid, lo = _split3(x)
    return _dot(hi, u) + _dot(mid, u) + _dot(lo, u)


def _sum_r3(u, x):
    hi, mid, lo = _split3(x)
    return _dot(u, hi) + _dot(u, mid) + _dot(u, lo)


def _softplus(z):
    return jnp.maximum(z, 0.0) + jnp.log1p(jnp.exp(-jnp.abs(z)))


def _tri(n, pred):
    r = lax.broadcasted_iota(jnp.int32, (n, n), 0)
    c = lax.broadcasted_iota(jnp.int32, (n, n), 1)
    return jnp.where(pred(r, c), 1.0, 0.0).astype(BF16)


def _rows(ref, j, n=TILE):
    return pl.ds(pl.multiple_of(j * n, n), n)


def _mm(name, a, b, mode, tm, tn, out_dtype, res=None, a_lead=(), b_lead=()):
    a2, b2 = a.shape[len(a_lead):], b.shape[len(b_lead):]
    if mode == "tn":
        k, m = a2
    else:
        m, k = a2
    n = b2[0] if mode == "nt" else b2[1]
    assert m % tm == 0 and n % tn == 0, (name, m, tm, n, tn)
    na, nb = (None,) * len(a_lead), (None,) * len(b_lead)
    if mode == "tn":
        a_spec = pl.BlockSpec(na + (k, tm), lambda j, i: a_lead + (0, i))
    else:
        a_spec = pl.BlockSpec(na + (tm, k), lambda j, i: a_lead + (i, 0))
    if mode == "nt":
        b_spec = pl.BlockSpec(nb + (tn, k), lambda j, i: b_lead + (j, 0))
    else:
        b_spec = pl.BlockSpec(nb + (k, tn), lambda j, i: b_lead + (0, j))
    o_spec = pl.BlockSpec((tm, tn), lambda j, i: (i, j))
    dot = {"nn": _dot, "nt": _dot_nt, "tn": _dot_tn}[mode]

    def body(a_ref, b_ref, *rest):
        o_ref = rest[-1]
        acc = dot(a_ref[...].astype(BF16), b_ref[...].astype(BF16))
        if res is not None:
            acc = acc + rest[0][...]
        o_ref[...] = acc.astype(o_ref.dtype)

    args, specs = [a, b], [a_spec, b_spec]
    if res is not None:
        args.append(res)
        specs.append(o_spec)
    return pl.pallas_call(
        body, name=name, grid=(n // tn, m // tm), in_specs=specs, out_specs=o_spec,
        out_shape=jax.ShapeDtypeStruct((m, n), out_dtype),
        compiler_params=_params("parallel", "parallel"),
    )(*args)


def _rms_fwd(name, x, g, ts):
    s, d = x.shape

    def body(x_ref, g_ref, o_ref):
        xf = x_ref[...]
        r = lax.rsqrt(jnp.mean(xf * xf, axis=1, keepdims=True) + EPS)
        o_ref[...] = (xf * r * g_ref[...]).astype(BF16)

    return pl.pallas_call(
        body, name=name, grid=(s // ts,),
        in_specs=[pl.BlockSpec((ts, d), lambda i: (i, 0)), pl.BlockSpec((1, d), lambda i: (0, 0))],
        out_specs=pl.BlockSpec((ts, d), lambda i: (i, 0)),
        out_shape=jax.ShapeDtypeStruct((s, d), BF16),
        compiler_params=_params("parallel"),
    )(x, g)


def _rms_bwd(name, x, g, dh, dres, ts):
    s, d = x.shape

    def body(x_ref, g_ref, dh_ref, dres_ref, dx_ref, dxb_ref, dg_ref):
        @pl.when(pl.program_id(0) == 0)
        def _():
            dg_ref[...] = jnp.zeros_like(dg_ref)

        xf = x_ref[...]
        r = lax.rsqrt(jnp.mean(xf * xf, axis=1, keepdims=True) + EPS)
        xh = xf * r
        dhf = dh_ref[...]
        dg_ref[...] += jnp.sum(dhf * xh, axis=0, keepdims=True)
        dxh = dhf * g_ref[...]
        m = jnp.mean(dxh * xh, axis=1, keepdims=True)
        dx = r * (dxh - xh * m) + dres_ref[...]
        dx_ref[...] = dx
        dxb_ref[...] = dx.astype(BF16)

    row = pl.BlockSpec((ts, d), lambda i: (i, 0))
    vec = pl.BlockSpec((1, d), lambda i: (0, 0))
    return pl.pallas_call(
        body, name=name, grid=(s // ts,), in_specs=[row, vec, row, row], out_specs=[row, row, vec],
        out_shape=[jax.ShapeDtypeStruct((s, d), F32), jax.ShapeDtypeStruct((s, d), BF16),
                   jax.ShapeDtypeStruct((1, d), F32)],
        compiler_params=_params("arbitrary"),
    )(x, g, dh, dres)


def _rms_wgrad(name, x, dh):
    m_, d = x.shape

    def body(x_ref, dh_ref, dg_ref):
        xf = x_ref[...]
        r = lax.rsqrt(jnp.mean(xf * xf, axis=1, keepdims=True) + EPS)
        dg_ref[...] = jnp.sum(dh_ref[...] * xf * r, axis=0, keepdims=True)

    return pl.pallas_call(
        body, name=name, out_shape=jax.ShapeDtypeStruct((1, d), F32),
    )(x, dh)


def _final_loss(name, x, g, target, ts):
    s, d = x.shape

    def body(x_ref, g_ref, t_ref, loss_ref, dx_ref, dxb_ref, dg_ref):
        @pl.when(pl.program_id(0) == 0)
        def _():
            dg_ref[...] = jnp.zeros_like(dg_ref)
            loss_ref[...] = jnp.zeros_like(loss_ref)

        xf = x_ref[...]
        gw = g_ref[...]
        r = lax.rsqrt(jnp.mean(xf * xf, axis=1, keepdims=True) + EPS)
        xh = xf * r
        e = xh * gw - t_ref[...]
        part = 0.5 * jnp.sum(jnp.mean(e * e, axis=1, keepdims=True), axis=0, keepdims=True)
        loss_ref[...] += jnp.broadcast_to(part, loss_ref.shape)
        dy = e * (1.0 / d)
        dg_ref[...] += jnp.sum(dy * xh, axis=0, keepdims=True)
        dxh = dy * gw
        m = jnp.mean(dxh * xh, axis=1, keepdims=True)
        dx = r * (dxh - xh * m)
        dx_ref[...] = dx
        dxb_ref[...] = dx.astype(BF16)

    row = pl.BlockSpec((ts, d), lambda i: (i, 0))
    vec = pl.BlockSpec((1, d), lambda i: (0, 0))
    lvec = pl.BlockSpec((1, LANES), lambda i: (0, 0))
    return pl.pallas_call(
        body, name=name, grid=(s // ts,), in_specs=[row, vec, row], out_specs=[lvec, row, row, vec],
        out_shape=[jax.ShapeDtypeStruct((1, LANES), F32), jax.ShapeDtypeStruct((s, d), F32),
                   jax.ShapeDtypeStruct((s, d), BF16), jax.ShapeDtypeStruct((1, d), F32)],
        compiler_params=_params("arbitrary"),
    )(x, g, target)


def _gate_fwd(name, pb, bpad, fl_block):
    s = pb.shape[0]
    nb = s // TILE

    def body(fl_ref, b_ref, ccol_ref, crow_ref, carry):
        @pl.when(pl.program_id(0) == 0)
        def _():
            carry[...] = jnp.zeros_like(carry)

        u = fl_ref[...] + b_ref[...]
        lf = jnp.minimum(u, 0.0) - jnp.log1p(jnp.exp(-jnp.abs(u)))
        lower = _tri(TILE, lambda r, c: c <= r)
        c = _sum_r3(lower, lf) + carry[0:1, :]
        ccol_ref[...] = c
        crow_ref[0] = c.T[0:8, :]
        carry[...] = jnp.broadcast_to(c[TILE - 1:TILE, :], carry.shape)

    return pl.pallas_call(
        body, name=name, grid=(nb,),
        in_specs=[pl.BlockSpec((TILE, LANES), lambda i: (i, fl_block)), pl.BlockSpec((1, LANES), lambda i: (0, 0))],
        out_specs=[pl.BlockSpec((TILE, LANES), lambda i: (i, 0)), pl.BlockSpec((1, 8, TILE), lambda i: (i, 0, 0))],
        out_shape=[jax.ShapeDtypeStruct((s, LANES), F32), jax.ShapeDtypeStruct((nb, 8, TILE), F32)],
        scratch_shapes=[pltpu.VMEM((8, LANES), F32)],
        compiler_params=_params("arbitrary"),
    )(pb, bpad)


def _gate_bwd(name, pb, bpad, colsum, fl_block):
    s = pb.shape[0]
    nb = s // TILE

    def body(fl_ref, b_ref, cs_ref, dl_ref, db_ref, carry):
        @pl.when(pl.program_id(0) == 0)
        def _():
            carry[...] = jnp.zeros_like(carry)
            db_ref[...] = jnp.zeros_like(db_ref)

        upper = _tri(TILE, lambda r, c: r >= c)
        rsum = _sum_l3(cs_ref[0], upper) + carry[:, 0:1]
        carry[...] = jnp.broadcast_to(rsum[:, 0:1], carry.shape)
        full = jnp.concatenate([rsum, jnp.zeros((LANES - 8, TILE), F32)], axis=0)
        dlf = -full.T
        u = fl_ref[...] + b_ref[...]
        dlogit = dlf * (1.0 - jax.nn.sigmoid(u))
        dl_ref[...] = dlogit.astype(BF16)
        db_ref[...] += jnp.sum(dlogit, axis=0, keepdims=True)

    rev = lambda i: (nb - 1 - i, 0)
    return pl.pallas_call(
        body, name=name, grid=(nb,),
        in_specs=[pl.BlockSpec((TILE, LANES), lambda i: (nb - 1 - i, fl_block)),
                  pl.BlockSpec((1, LANES), lambda i: (0, 0)),
                  pl.BlockSpec((1, 8, TILE), lambda i: (nb - 1 - i, 0, 0))],
        out_specs=[pl.BlockSpec((TILE, LANES), rev), pl.BlockSpec((1, LANES), lambda i: (0, 0))],
        out_shape=[jax.ShapeDtypeStruct((s, LANES), BF16), jax.ShapeDtypeStruct((1, LANES), F32)],
        scratch_shapes=[pltpu.VMEM((8, LANES), F32)],
        compiler_params=_params("arbitrary"),
    )(pb, bpad, colsum)


def _head_slices(hh):
    return slice(HEAD_DIM * hh, HEAD_DIM * (hh + 1))


def _scaled_q(q_ref, sl, scale=SCALE):
    return (q_ref[:, sl].astype(F32) * scale).astype(BF16)


def _neg_abs(x):
    sign = jnp.uint32(0x80000000)
    return lax.bitcast_convert_type(lax.bitcast_convert_type(x, jnp.uint32) | sign, F32)


def _sb_tile(qn, kj, carry, strict, u_after, diag):
    nz = _dot_nt(qn, kj)
    lf = jnp.minimum(nz, 0.0) - jnp.log(1.0 + jnp.exp(_neg_abs(nz)))
    lsig = lf - nz
    if diag:
        lf = jnp.where(strict, lf, 0.0)
    sx = _dot(lf.astype(BF16), u_after)
    a = jnp.exp(lsig + sx + carry)
    if diag:
        a = jnp.where(strict, a, 0.0)
    return lsig, a, carry + sx[:, 0:1] + lf[:, 0:1]


def _pair_grid_call(name, body, nb, in_specs, out_specs, out_shape, scratch, args, carried=None):
    if carried is None:
        return pl.pallas_call(
            body, name=name, grid=(4, nb), in_specs=in_specs, out_specs=out_specs, out_shape=out_shape,
            scratch_shapes=scratch, compiler_params=_params("arbitrary", "arbitrary"),
        )(*args)
    n_in, n_out, n_ex = len(in_specs), len(out_specs), carried.n

    def body_with_copies(*refs):
        own_in, ex_in = refs[:n_in], refs[n_in:n_in + n_ex]
        own_out = refs[n_in + n_ex:n_in + n_ex + n_out]
        ex_out = refs[n_in + n_ex + n_out:n_in + 2 * n_ex + n_out]
        own_scratch, sems = refs[n_in + 2 * n_ex + n_out:-2], refs[-2:]
        parts = (ex_in, ex_out, sems[0], sems[1])
        p, i = pl.program_id(0), pl.program_id(1)
        pl.when(jnp.logical_and(p == 0, i == 0))(lambda: carried.begin(*parts))
        if carried.relay is not None:
            pl.when(jnp.logical_and(p == 3, i == 0))(lambda: carried.relay(*parts))
        body(*own_in, *own_out, *own_scratch)
        pl.when(jnp.logical_and(p == 3, i == nb - 1))(lambda: carried.finish(*parts))

    return pl.pallas_call(
        body_with_copies, name=name, grid=(4, nb), in_specs=list(in_specs) + [HBM_SPEC] * n_ex,
        out_specs=list(out_specs) + [HBM_SPEC] * n_ex, out_shape=list(out_shape) + carried.out_shapes,
        scratch_shapes=list(scratch) + _dma_sems(carried.n_sems),
        compiler_params=_params("arbitrary", "arbitrary"),
    )(*args, *carried.inputs)


def _sb_fwd(name, pa, col0, carried=None):
    s = pa.shape[0]
    nb = s // TILE
    cb = col0 // LANES

    def body(q_ref, k_ref, v_ref, o_ref, lsig_s, lf_s):
        i = pl.program_id(1)
        r = lax.broadcasted_iota(jnp.int32, (TILE, TILE), 0)
        c = lax.broadcasted_iota(jnp.int32, (TILE, TILE), 1)
        strict = c < r
        u_after = _tri(TILE, lambda rr, cc: rr > cc)
        qs = [_scaled_q(q_ref, _head_slices(hh), -SCALE) for hh in range(2)]

        def neg_z(j):
            kblk = k_ref[_rows(k_ref, j), :]
            return [_dot_nt(qs[hh], kblk[:, _head_slices(hh)]) for hh in range(2)]

        def scores(nzs, slot, diag):
            for hh, nz in enumerate(nzs):
                lf = jnp.minimum(nz, 0.0) - jnp.log(1.0 + jnp.exp(_neg_abs(nz)))
                lsig = lf - nz
                if diag:
                    lf = jnp.where(strict, lf, 0.0)
                    lsig = jnp.where(strict, lsig, MASKED)
                lsig_s[slot, hh] = lsig
                lf_s[slot, hh] = lf.astype(BF16)

        def weigh(j, slot, state):
            vblk = v_ref[_rows(v_ref, j), :]
            new = []
            for hh in range(2):
                carry, acc = state[hh]
                lfb = lf_s[slot, hh]
                sx = _dot(lfb, u_after)
                a = jnp.exp(lsig_s[slot, hh] + sx + carry)
                new.append((carry + sx[:, 0:1] + lfb[:, 0:1].astype(F32),
                            acc + _dot(a.astype(BF16), vblk[:, _head_slices(hh)])))
            return tuple(new)

        def step(t, state):
            state = weigh(i - t + 1, (t - 1) % 2, state)
            scores(neg_z(i - t), t % 2, False)
            return state

        zero = (jnp.zeros((TILE, 1), F32), jnp.zeros((TILE, HEAD_DIM), F32))
        scores(neg_z(i), 0, True)
        state = lax.fori_loop(1, i + 1, step, (zero, zero))
        state = weigh(0, i % 2, state)
        o_ref[...] = jnp.concatenate([state[0][1], state[1][1]], axis=1)

    outs = _pair_grid_call(
        name, body, nb,
        in_specs=[pl.BlockSpec((TILE, LANES), lambda p, i: (i, cb + p)),
                  pl.BlockSpec((s, LANES), lambda p, i: (0, cb + 4 + p)),
                  pl.BlockSpec((s, LANES), lambda p, i: (0, cb + 8 + p))],
        out_specs=[pl.BlockSpec((TILE, LANES), lambda p, i: (i, p))],
        out_shape=[jax.ShapeDtypeStruct((s, SB_WIDTH), F32)],
        scratch=[pltpu.VMEM((2, 2, TILE, TILE), F32), pltpu.VMEM((2, 2, TILE, TILE), BF16)],
        args=(pa, pa, pa), carried=carried)
    return outs[0], outs[1:]


def _sb_bwd(name, pa, col0, dout, dcol0, carried=None):
    s = pa.shape[0]
    nb = s // TILE
    cb = col0 // LANES
    db = dcol0 // LANES

    def body(q_ref, k_ref, v_ref, do_ref, dq_ref, dk_ref, dv_ref, dk_acc, dv_acc, dpan, span, gsum, lsig_s, lf_s):
        i = pl.program_id(1)

        @pl.when(i == 0)
        def _():
            dk_acc[...] = jnp.zeros_like(dk_acc)
            dv_acc[...] = jnp.zeros_like(dv_acc)

        r = lax.broadcasted_iota(jnp.int32, (TILE, TILE), 0)
        c = lax.broadcasted_iota(jnp.int32, (TILE, TILE), 1)
        strict = c < r
        u_after = _tri(TILE, lambda rr, cc: rr > cc)
        u_before = _tri(TILE, lambda rr, cc: rr < cc)
        qs = [_scaled_q(q_ref, _head_slices(hh), -SCALE) for hh in range(2)]
        dos = [do_ref[:, _head_slices(hh)].astype(BF16) for hh in range(2)]

        def scores(j, slot, diag):
            kblk = k_ref[_rows(k_ref, j), :]
            for hh in range(2):
                nz = _dot_nt(qs[hh], kblk[:, _head_slices(hh)])
                lf = jnp.minimum(nz, 0.0) - jnp.log(1.0 + jnp.exp(_neg_abs(nz)))
                lsig = lf - nz
                if diag:
                    lf = jnp.where(strict, lf, 0.0)
                    lsig = jnp.where(strict, lsig, MASKED)
                lsig_s[slot, hh] = lsig
                lf_s[slot, hh] = lf.astype(BF16)

        def grads(j, slot, carries):
            vblk = v_ref[_rows(v_ref, j), :]
            new = []
            for hh in range(2):
                lfb = lf_s[slot, hh]
                lsig = lsig_s[slot, hh]
                sx = _dot(lfb, u_after)
                a = jnp.exp(lsig + sx + carries[hh])
                g = a * _dot_nt(dos[hh], vblk[:, _head_slices(hh)])
                sig = jnp.exp(lsig)
                inside = _dot(g.astype(BF16), u_before)
                dpan[hh, j] = sig * (inside + g) - g
                span[hh, j] = sig
                gsum[hh, j] = inside[:, TILE - 1:TILE] + g[:, TILE - 1:TILE]
                dv_acc[hh, _rows(None, j), :] += _dot_tn(a.astype(BF16), dos[hh])
                new.append(carries[hh] + sx[:, 0:1] + lfb[:, 0:1].astype(F32))
            return tuple(new)

        def step1(t, carries):
            carries = grads(i - t + 1, (t - 1) % 2, carries)
            scores(i - t, t % 2, False)
            return carries

        zero1 = jnp.zeros((TILE, 1), F32)
        scores(i, 0, True)
        carries = lax.fori_loop(1, i + 1, step1, (zero1, zero1))
        grads(0, i % 2, carries)

        def pass2(j, state):
            kblk = k_ref[_rows(k_ref, j), :]
            new = []
            for hh in range(2):
                before, ndq = state[hh]
                ndzb = (dpan[hh, j] + span[hh, j] * before).astype(BF16)
                dk_acc[hh, _rows(None, j), :] += _dot_tn(ndzb, qs[hh])
                new.append((before + gsum[hh, j], ndq + _dot(ndzb, kblk[:, _head_slices(hh)])))
            return tuple(new)

        zero2 = (zero1, jnp.zeros((TILE, HEAD_DIM), F32))
        state = lax.fori_loop(0, i + 1, pass2, (zero2, zero2))
        dq_ref[...] = jnp.concatenate([state[0][1] * -SCALE, state[1][1] * -SCALE], axis=1).astype(BF16)

        @pl.when(i == nb - 1)
        def _():
            dk_ref[...] = jnp.concatenate([dk_acc[0], dk_acc[1]], axis=1).astype(BF16)
            dv_ref[...] = jnp.concatenate([dv_acc[0], dv_acc[1]], axis=1).astype(BF16)

    qspec = pl.BlockSpec((TILE, LANES), lambda p, i: (i, p))
    kvspec = pl.BlockSpec((s, LANES), lambda p, i: (0, p))
    out = jax.ShapeDtypeStruct((s, SB_WIDTH), BF16)
    outs = _pair_grid_call(
        name, body, nb,
        in_specs=[pl.BlockSpec((TILE, LANES), lambda p, i: (i, cb + p)),
                  pl.BlockSpec((s, LANES), lambda p, i: (0, cb + 4 + p)),
                  pl.BlockSpec((s, LANES), lambda p, i: (0, cb + 8 + p)),
                  pl.BlockSpec((TILE, LANES), lambda p, i: (i, db + p))],
        out_specs=[qspec, kvspec, kvspec], out_shape=[out, out, out],
        scratch=[pltpu.VMEM((2, s, HEAD_DIM), F32), pltpu.VMEM((2, s, HEAD_DIM), F32),
                 pltpu.VMEM((2, nb, TILE, TILE), F32), pltpu.VMEM((2, nb, TILE, TILE), F32),
                 pltpu.VMEM((2, nb, TILE, 1), F32),
                 pltpu.VMEM((2, 2, TILE, TILE), F32), pltpu.VMEM((2, 2, TILE, TILE), BF16)],
        args=(pa, pa, pa, dout), carried=carried)
    return outs[:3], outs[3:]


def _fox_scores(q, kj, cq, crj, causal, diag):
    sc = _dot_nt(q, kj) + (cq - crj)
    if diag:
        sc = jnp.where(causal, sc, NEG_INF)
    return sc


def _fox_fwd(name, pa, col0, ccol4, crow4):
    s = pa.shape[0]
    nb = s // TILE
    cb = col0 // LANES

    def body(q_ref, k_ref, v_ref, cc_ref, cr_ref, o_ref, lse_ref, sc_s):
        i = pl.program_id(1)
        r = lax.broadcasted_iota(jnp.int32, (TILE, TILE), 0)
        c = lax.broadcasted_iota(jnp.int32, (TILE, TILE), 1)
        causal = c <= r
        qs = [_scaled_q(q_ref, _head_slices(hh)) for hh in range(2)]
        cqs = [cc_ref[:, HEAD_DIM * hh:HEAD_DIM * hh + 1] for hh in range(2)]

        def logits(j, slot, diag):
            kblk = k_ref[_rows(k_ref, j), :]
            tops = []
            for hh in range(2):
                sc = _fox_scores(qs[hh], kblk[:, _head_slices(hh)], cqs[hh], cr_ref[j, hh:hh + 1, :], causal, diag)
                sc_s[slot, hh] = sc
                tops.append(jnp.max(sc, axis=1, keepdims=True))
            return tuple(tops)

        def update(j, slot, tops, state):
            vblk = v_ref[_rows(v_ref, j), :]
            new = []
            for hh in range(2):
                m, l, acc = state[hh]
                m2 = jnp.maximum(m, tops[hh])
                alpha = jnp.exp(m - m2)
                p = jnp.exp(sc_s[slot, hh] - m2)
                new.append((m2, l * alpha + jnp.sum(p, axis=1, keepdims=True),
                            acc * alpha + _dot(p.astype(BF16), vblk[:, _head_slices(hh)])))
            return tuple(new)

        def step(t, both):
            tops, state = both
            state = update(i - t + 1, (t - 1) % 2, tops, state)
            return logits(i - t, t % 2, False), state

        zero = (jnp.full((TILE, 1), NEG_INF, F32), jnp.zeros((TILE, 1), F32), jnp.zeros((TILE, HEAD_DIM), F32))
        tops, state = lax.fori_loop(1, i + 1, step, (logits(i, 0, True), (zero, zero)))
        state = update(0, i % 2, tops, state)
        o_ref[...] = jnp.concatenate([state[hh][2] / state[hh][1] for hh in range(2)], axis=1)
        lse_ref[...] = jnp.concatenate(
            [jnp.broadcast_to(state[hh][0] + jnp.log(state[hh][1]), (TILE, HEAD_DIM)) for hh in range(2)], axis=1)

    return pl.pallas_call(
        body, name=name, grid=(4, nb),
        in_specs=[pl.BlockSpec((TILE, LANES), lambda p, i: (i, cb + p)),
                  pl.BlockSpec((s, LANES), lambda p, i: (0, cb + 4 + p)),
                  pl.BlockSpec((s, LANES), lambda p, i: (0, cb + 8 + p)),
                  pl.BlockSpec((None, TILE, LANES), lambda p, i: (p, i, 0)),
                  pl.BlockSpec((None, nb, 8, TILE), lambda p, i: (p, 0, 0, 0))],
        out_specs=[pl.BlockSpec((TILE, LANES), lambda p, i: (i, p)),
                   pl.BlockSpec((None, TILE, LANES), lambda p, i: (p, i, 0))],
        out_shape=[jax.ShapeDtypeStruct((s, FOX_WIDTH), F32), jax.ShapeDtypeStruct((4, s, LANES), F32)],
        scratch_shapes=[pltpu.VMEM((2, 2, TILE, TILE), F32)],
        compiler_params=_params("parallel", "arbitrary"),
    )(pa, pa, pa, ccol4, crow4)


def _fox_bwd(name, pa, col0, ccol4, crow4, out, lse, dout, dcol0):
    s = pa.shape[0]
    nb = s // TILE
    cb = col0 // LANES
    db = dcol0 // LANES

    def body(q_ref, k_ref, v_ref, cc_ref, cr_ref, o_ref, lse_ref, do_ref,
             dq_ref, dk_ref, dv_ref, cs_ref, dk_acc, dv_acc, p_s, ds_s):
        i = pl.program_id(1)

        @pl.when(i == 0)
        def _():
            dk_acc[...] = jnp.zeros_like(dk_acc)
            dv_acc[...] = jnp.zeros_like(dv_acc)
            cs_ref[...] = jnp.zeros_like(cs_ref)

        r = lax.broadcasted_iota(jnp.int32, (TILE, TILE), 0)
        c = lax.broadcasted_iota(jnp.int32, (TILE, TILE), 1)
        causal = c <= r
        qs = [_scaled_q(q_ref, _head_slices(hh)) for hh in range(2)]
        cqs = [cc_ref[:, HEAD_DIM * hh:HEAD_DIM * hh + 1] for hh in range(2)]
        lses = [lse_ref[:, HEAD_DIM * hh:HEAD_DIM * hh + 1] for hh in range(2)]
        dofs = [do_ref[:, _head_slices(hh)] for hh in range(2)]
        dos = [d_.astype(BF16) for d_ in dofs]
        deltas = [jnp.sum(dofs[hh] * o_ref[:, _head_slices(hh)], axis=1, keepdims=True) for hh in range(2)]

        def probs(j, slot, rowsums, diag):
            kblk = k_ref[_rows(k_ref, j), :]
            vblk = v_ref[_rows(v_ref, j), :]
            new = []
            for hh in range(2):
                sl = _head_slices(hh)
                sc = _fox_scores(qs[hh], kblk[:, sl], cqs[hh], cr_ref[j, hh:hh + 1, :], causal, diag)
                p = jnp.exp(sc - lses[hh])
                ds = p * (_dot_nt(dos[hh], vblk[:, sl]) - deltas[hh])
                p_s[slot, hh] = p.astype(BF16)
                ds_s[slot, hh] = ds.astype(BF16)
                cs_ref[j, hh:hh + 1, :] += jnp.sum(ds, axis=0, keepdims=True)
                new.append(rowsums[hh] + jnp.sum(ds, axis=1, keepdims=True))
            return tuple(new)

        def accumulate(j, slot, dqs):
            kblk = k_ref[_rows(k_ref, j), :]
            new = []
            for hh in range(2):
                dsb = ds_s[slot, hh]
                dv_acc[hh, _rows(None, j), :] += _dot_tn(p_s[slot, hh], dos[hh])
                dk_acc[hh, _rows(None, j), :] += _dot_tn(dsb, qs[hh])
                new.append(dqs[hh] + _dot(dsb, kblk[:, _head_slices(hh)]))
            return tuple(new)

        def step(t, both):
            rowsums, dqs = both
            dqs = accumulate(i - t + 1, (t - 1) % 2, dqs)
            return probs(i - t, t % 2, rowsums, False), dqs

        zero1 = jnp.zeros((TILE, 1), F32)
        zero64 = jnp.zeros((TILE, HEAD_DIM), F32)
        rowsums, dqs = lax.fori_loop(1, i + 1, step, (probs(i, 0, (zero1, zero1), True), (zero64, zero64)))
        dqs = accumulate(0, i % 2, dqs)
        for hh in range(2):
            cs_ref[i, hh:hh + 1, :] -= jnp.broadcast_to(rowsums[hh], (TILE, LANES)).T[0:1, :]
        dq_ref[...] = jnp.concatenate([dqs[0] * SCALE, dqs[1] * SCALE], axis=1).astype(BF16)

        @pl.when(i == nb - 1)
        def _():
            dk_ref[...] = jnp.concatenate([dk_acc[0], dk_acc[1]], axis=1).astype(BF16)
            dv_ref[...] = jnp.concatenate([dv_acc[0], dv_acc[1]], axis=1).astype(BF16)

    qspec = pl.BlockSpec((TILE, LANES), lambda p, i: (i, p))
    kvspec = pl.BlockSpec((s, LANES), lambda p, i: (0, p))
    o3 = jax.ShapeDtypeStruct((s, FOX_WIDTH), BF16)
    return pl.pallas_call(
        body, name=name, grid=(4, nb),
        in_specs=[pl.BlockSpec((TILE, LANES), lambda p, i: (i, cb + p)),
                  pl.BlockSpec((s, LANES), lambda p, i: (0, cb + 4 + p)),
                  pl.BlockSpec((s, LANES), lambda p, i: (0, cb + 8 + p)),
                  pl.BlockSpec((None, TILE, LANES), lambda p, i: (p, i, 0)),
                  pl.BlockSpec((None, nb, 8, TILE), lambda p, i: (p, 0, 0, 0)),
                  qspec,
                  pl.BlockSpec((None, TILE, LANES), lambda p, i: (p, i, 0)),
                  pl.BlockSpec((TILE, LANES), lambda p, i: (i, db + p))],
        out_specs=[qspec, kvspec, kvspec, pl.BlockSpec((None, nb, 8, TILE), lambda p, i: (p, 0, 0, 0))],
        out_shape=[o3, o3, o3, jax.ShapeDtypeStruct((4, nb, 8, TILE), F32)],
        scratch_shapes=[pltpu.VMEM((2, s, HEAD_DIM), F32), pltpu.VMEM((2, s, HEAD_DIM), F32),
                        pltpu.VMEM((2, 2, TILE, TILE), BF16), pltpu.VMEM((2, 2, TILE, TILE), BF16)],
        compiler_params=_params("arbitrary", "arbitrary"),
    )(pa, pa, pa, ccol4, crow4, out, lse, dout)


def _mem_fwd(name, pa, mkv):
    s = pa.shape[0]
    ml = mkv.shape[0]
    nb = s // TILE
    cb = QKV_WIDTH // LANES

    def body(q_ref, k_ref, v_ref, o_ref, lse_ref):
        outs, lses = [], []
        for hh in range(2):
            sl = _head_slices(hh)
            sc = _dot_nt(_scaled_q(q_ref, sl), k_ref[:, sl])
            m = jnp.max(sc, axis=1, keepdims=True)
            p = jnp.exp(sc - m)
            l = jnp.sum(p, axis=1, keepdims=True)
            outs.append(_dot(p.astype(BF16), v_ref[:, sl]) / l)
            lses.append(jnp.broadcast_to(m + jnp.log(l), (TILE, HEAD_DIM)))
        o_ref[...] = jnp.concatenate(outs, axis=1)
        lse_ref[...] = jnp.concatenate(lses, axis=1)

    return pl.pallas_call(
        body, name=name, grid=(2, nb),
        in_specs=[pl.BlockSpec((TILE, LANES), lambda p, i: (i, cb + p)),
                  pl.BlockSpec((ml, LANES), lambda p, i: (0, p)),
                  pl.BlockSpec((ml, LANES), lambda p, i: (0, 2 + p))],
        out_specs=[pl.BlockSpec((TILE, LANES), lambda p, i: (i, p)),
                   pl.BlockSpec((None, TILE, LANES), lambda p, i: (p, i, 0))],
        out_shape=[jax.ShapeDtypeStruct((s, MEM_WIDTH), F32), jax.ShapeDtypeStruct((2, s, LANES), F32)],
        compiler_params=_params("parallel", "parallel"),
    )(pa, mkv, mkv)


def _mem_bwd(name, pa, mkv, out, lse, dout, dcol0):
    s = pa.shape[0]
    ml = mkv.shape[0]
    nb = s // TILE
    cb = QKV_WIDTH // LANES
    db = dcol0 // LANES

    def body(q_ref, k_ref, v_ref, o_ref, lse_ref, do_ref, dq_ref, dk_ref, dv_ref, dk_acc, dv_acc):
        i = pl.program_id(1)

        @pl.when(i == 0)
        def _():
            dk_acc[...] = jnp.zeros_like(dk_acc)
            dv_acc[...] = jnp.zeros_like(dv_acc)

        dqs = []
        for hh in range(2):
            sl = _head_slices(hh)
            q = _scaled_q(q_ref, sl)
            kh = k_ref[:, sl]
            dof = do_ref[:, sl]
            do = dof.astype(BF16)
            delta = jnp.sum(dof * o_ref[:, sl], axis=1, keepdims=True)
            p = jnp.exp(_dot_nt(q, kh) - lse_ref[:, HEAD_DIM * hh:HEAD_DIM * hh + 1])
            ds = (p * (_dot_nt(do, v_ref[:, sl]) - delta)).astype(BF16)
            dv_acc[hh] += _dot_tn(p.astype(BF16), do)
            dk_acc[hh] += _dot_tn(ds, q)
            dqs.append(_dot(ds, kh) * SCALE)
        dq_ref[...] = jnp.concatenate(dqs, axis=1).astype(BF16)

        @pl.when(i == nb - 1)
        def _():
            dk_ref[...] = jnp.concatenate([dk_acc[0], dk_acc[1]], axis=1).astype(BF16)
            dv_ref[...] = jnp.concatenate([dv_acc[0], dv_acc[1]], axis=1).astype(BF16)

    qspec = pl.BlockSpec((TILE, LANES), lambda p, i: (i, p))
    kvspec = pl.BlockSpec((ml, LANES), lambda p, i: (0, p))
    okv = jax.ShapeDtypeStruct((ml, MEM_WIDTH), BF16)
    return pl.pallas_call(
        body, name=name, grid=(2, nb),
        in_specs=[pl.BlockSpec((TILE, LANES), lambda p, i: (i, cb + p)),
                  pl.BlockSpec((ml, LANES), lambda p, i: (0, p)),
                  pl.BlockSpec((ml, LANES), lambda p, i: (0, 2 + p)),
                  qspec,
                  pl.BlockSpec((None, TILE, LANES), lambda p, i: (p, i, 0)),
                  pl.BlockSpec((TILE, LANES), lambda p, i: (i, db + p))],
        out_specs=[qspec, kvspec, kvspec],
        out_shape=[jax.ShapeDtypeStruct((s, MEM_WIDTH), BF16), okv, okv],
        scratch_shapes=[pltpu.VMEM((2, ml, HEAD_DIM), F32), pltpu.VMEM((2, ml, HEAD_DIM), F32)],
        compiler_params=_params("arbitrary", "arbitrary"),
    )(pa, mkv, mkv, out, lse, dout)


def _head_maps():
    col = jnp.arange(MIX_WIDTH)[:, None] // HEAD_DIM
    g = (col == jnp.arange(LANES)[None, :]).astype(BF16)
    return g, g.T


def _normed_heads(osb_ref, ofx_ref, om_ref, g_ref, gt_ref):
    y = jnp.concatenate([osb_ref[...], ofx_ref[...], om_ref[...]], axis=1)
    msq = _sum_l3(y * y, g_ref[...]) * (1.0 / HEAD_DIM)
    rf = _sum_l3(lax.rsqrt(msq + EPS), gt_ref[...])
    return y * rf, rf


def _out_fwd(name, o_sb, o_fx, o_m, pb, ow, x, w_out, ts):
    s, d = x.shape
    g, gt = _head_maps()

    def body(osb_ref, ofx_ref, om_ref, gate_ref, ow_ref, x_ref, w_ref, g_ref, gt_ref, xo_ref, y2_ref):
        yh, _ = _normed_heads(osb_ref, ofx_ref, om_ref, g_ref, gt_ref)
        gate = gate_ref[...]
        y2 = (yh * ow_ref[...] * (gate * jax.nn.sigmoid(gate))).astype(BF16)
        y2_ref[...] = y2
        xo_ref[...] = x_ref[...] + _dot(y2, w_ref[...])

    return pl.pallas_call(
        body, name=name, grid=(s // ts,),
        in_specs=[_row_spec(ts, SB_WIDTH), _row_spec(ts, FOX_WIDTH), _row_spec(ts, MEM_WIDTH),
                  _row_spec(ts, MIX_WIDTH), _const_spec((1, MIX_WIDTH)), _row_spec(ts, d),
                  _const_spec((MIX_WIDTH, d)),
                  _const_spec((MIX_WIDTH, LANES)), _const_spec((LANES, MIX_WIDTH))],
        out_specs=[_row_spec(ts, d), _row_spec(ts, MIX_WIDTH)],
        out_shape=[jax.ShapeDtypeStruct((s, d), F32), jax.ShapeDtypeStruct((s, MIX_WIDTH), BF16)],
        compiler_params=_params("parallel"),
    )(o_sb, o_fx, o_m, pb, ow, x, w_out, g, gt)


def _row_spec(ts, w):
    return pl.BlockSpec((ts, w), lambda i: (i, 0))


def _const_spec(shape):
    return pl.BlockSpec(shape, lambda i: (0,) * len(shape))


def _out_bwd(name, dxb, o_sb, o_fx, o_m, pb, ow, w_out, ts):
    s, d = dxb.shape
    g, gt = _head_maps()

    def body(dx_ref, osb_ref, ofx_ref, om_ref, gate_ref, ow_ref, w_ref, g_ref, gt_ref, dy_ref, dgate_ref, dow_ref):
        @pl.when(pl.program_id(0) == 0)
        def _():
            dow_ref[...] = jnp.zeros_like(dow_ref)

        dy2 = _dot_nt(dx_ref[...], w_ref[...])
        yh, rf = _normed_heads(osb_ref, ofx_ref, om_ref, g_ref, gt_ref)
        gate = gate_ref[...]
        sig = jax.nn.sigmoid(gate)
        ow_v = ow_ref[...]
        dgate_ref[...] = (dy2 * (yh * ow_v) * (sig * (1.0 + gate * (1.0 - sig)))).astype(BF16)
        dn = dy2 * (gate * sig)
        dow_ref[...] += jnp.sum(dn * yh, axis=0, keepdims=True)
        dyh = dn * ow_v
        t = _sum_l3(dyh * yh, g_ref[...]) * (1.0 / HEAD_DIM)
        dy_ref[...] = rf * (dyh - yh * _sum_l3(t, gt_ref[...]))

    return pl.pallas_call(
        body, name=name, grid=(s // ts,),
        in_specs=[_row_spec(ts, d), _row_spec(ts, SB_WIDTH), _row_spec(ts, FOX_WIDTH), _row_spec(ts, MEM_WIDTH),
                  _row_spec(ts, MIX_WIDTH), _const_spec((1, MIX_WIDTH)),
                  _const_spec((MIX_WIDTH, d)),
                  _const_spec((MIX_WIDTH, LANES)), _const_spec((LANES, MIX_WIDTH))],
        out_specs=[_row_spec(ts, MIX_WIDTH), _row_spec(ts, MIX_WIDTH), _const_spec((1, MIX_WIDTH))],
        out_shape=[jax.ShapeDtypeStruct((s, MIX_WIDTH), F32), jax.ShapeDtypeStruct((s, MIX_WIDTH), BF16),
                   jax.ShapeDtypeStruct((1, MIX_WIDTH), F32)],
        compiler_params=_params("arbitrary"),
    )(dxb, o_sb, o_fx, o_m, pb, ow, w_out, g, gt)


def _adamw(name, w, g, m, v, tr):
    def body(w_ref, g_ref, m_ref, v_ref, d_ref, m2_ref, v2_ref):
        gv = g_ref[...]
        m2 = ADAM_B1 * m_ref[...] + (1.0 - ADAM_B1) * gv
        v2 = ADAM_B2 * v_ref[...] + (1.0 - ADAM_B2) * (gv * gv)
        m_hat = m2 / (1.0 - ADAM_B1 ** ADAM_STEP)
        v_hat = v2 / (1.0 - ADAM_B2 ** ADAM_STEP)
        d_ref[...] = -ADAM_LR * (m_hat / (jnp.sqrt(v_hat) + ADAM_EPS) + ADAM_WD * w_ref[...])
        m2_ref[...] = m2
        v2_ref[...] = v2

    rest = w.shape[1:]
    spec = pl.BlockSpec((tr,) + rest, lambda i: (i,) + (0,) * len(rest))
    shp = jax.ShapeDtypeStruct(w.shape, F32)
    return pl.pallas_call(
        body, name=name, grid=(w.shape[0] // tr,), in_specs=[spec] * 4, out_specs=[spec] * 3, out_shape=[shp] * 3,
        compiler_params=_params("parallel"),
    )(w, g, m, v)


def _adamw_sharded(name, w, m, v, g_own, g_other, cvec, tr):
    depth, rows, cols = w.shape
    nt = rows // 2 // tr

    def body(c_ref, w_ref, m_ref, v_ref, *rest):
        g_refs, (g_ref, d_ref, m2_ref, v2_ref) = rest[:2 * depth], rest[2 * depth:]
        layer, mine = pl.program_id(0), pl.program_id(1) == c_ref[0]
        gv = None
        for lt in range(depth):
            cand = jnp.where(mine, g_refs[lt][...], g_refs[depth + lt][...])
            gv = cand if gv is None else jnp.where(layer == lt, cand, gv)
        m2 = ADAM_B1 * m_ref[...] + (1.0 - ADAM_B1) * gv
        v2 = ADAM_B2 * v_ref[...] + (1.0 - ADAM_B2) * (gv * gv)
        m_hat = m2 / (1.0 - ADAM_B1 ** ADAM_STEP)
        v_hat = v2 / (1.0 - ADAM_B2 ** ADAM_STEP)
        g_ref[...] = gv
        d_ref[...] = -ADAM_LR * (m_hat / (jnp.sqrt(v_hat) + ADAM_EPS) + ADAM_WD * w_ref[...])
        m2_ref[...] = m2
        v2_ref[...] = v2

    def g_map(lt, own):
        def index(l, hf, i, c_ref):
            use = jnp.logical_and(l == lt, (hf == c_ref[0]) == own)
            return jnp.where(use, i, 0), 0
        return index

    full = pl.BlockSpec((None, tr, cols), lambda l, hf, i, c_ref: (l, hf * nt + i, 0))
    g_specs = [pl.BlockSpec((tr, cols), g_map(lt, own)) for own in (True, False) for lt in range(depth)]
    shp = jax.ShapeDtypeStruct((depth, rows, cols), F32)
    return pl.pallas_call(
        body, name=name,
        grid_spec=pltpu.PrefetchScalarGridSpec(
            num_scalar_prefetch=1, grid=(depth, 2, nt), in_specs=[full] * 3 + g_specs, out_specs=[full] * 4),
        out_shape=[shp] * 4,
        compiler_params=_params("arbitrary", "arbitrary", "arbitrary"),
    )(cvec, w, m, v, *g_own, *g_other)


HBM_SPEC = pl.BlockSpec(memory_space=pltpu.HBM)


def _place():
    x, y, c = lax.axis_index("x"), lax.axis_index("y"), lax.axis_index("c")
    chips = [(1 - x, y), (x, 1 - y), (1 - x, 1 - y)]
    return x, y, c, chips


def _remote(src, dst, send_sems, recv_sems, k, to):
    return pltpu.make_async_remote_copy(src_ref=src, dst_ref=dst, send_sem=send_sems.at[k], recv_sem=recv_sems.at[k],
                                        device_id=to, device_id_type=MESH)


def _half_rows(n_rows, cc):
    rh = n_rows // 2
    return pl.ds(pl.multiple_of(cc * rh, 16), rh)


def _dma_sems(n):
    return [pltpu.SemaphoreType.DMA((n,)), pltpu.SemaphoreType.DMA((n,))]


class _Exchange:
    def __init__(self, inputs, out_shapes, n_sems, begin, relay, finish):
        self.inputs, self.out_shapes, self.n_sems = list(inputs), list(out_shapes), n_sems
        self.begin, self.relay, self.finish = begin, relay, finish

    @property
    def n(self):
        return len(self.inputs)

    def split(self, refs):
        return refs[:self.n], refs[self.n:2 * self.n], refs[2 * self.n], refs[2 * self.n + 1]


def _run_exchange(name, ex):
    def body(*refs):
        parts = ex.split(refs)
        for phase in (ex.begin, ex.relay, ex.finish):
            if phase is not None:
                phase(*parts)

    return pl.pallas_call(
        body, name=name, in_specs=[HBM_SPEC] * ex.n, out_specs=[HBM_SPEC] * ex.n, out_shape=ex.out_shapes,
        scratch_shapes=_dma_sems(ex.n_sems),
    )(*ex.inputs)


def _gather_exchange(shards):
    def ici(in_refs, out_refs, send_sems, recv_sems):
        x, y, c, chips = _place()
        return [_remote(in_ref.at[_half_rows(in_ref.shape[0], c)], out_ref.at[2 * x + y, _half_rows(in_ref.shape[0], c)],
                        send_sems, recv_sems, 6 * a + j, (cx, cy, c))
                for a, (in_ref, out_ref) in enumerate(zip(in_refs, out_refs)) for j, (cx, cy) in enumerate(chips)]

    def d2d(out_refs, send_sems, recv_sems, half_of):
        x, y, c, chips = _place()
        cps = []
        for a, out_ref in enumerate(out_refs):
            for j, (cx, cy) in enumerate(chips):
                piece = out_ref.at[2 * cx + cy, _half_rows(out_ref.shape[1], half_of(c))]
                cps.append(_remote(piece, piece, send_sems, recv_sems, 6 * a + 3 + j, (x, y, 1 - c)))
        return cps

    def begin(in_refs, out_refs, send_sems, recv_sems):
        for cp in ici(in_refs, out_refs, send_sems, recv_sems):
            cp.start()

    def relay(in_refs, out_refs, send_sems, recv_sems):
        x, y, c, chips = _place()
        for a, out_ref in enumerate(out_refs):
            for j, (cx, cy) in enumerate(chips):
                landed = out_ref.at[2 * cx + cy, _half_rows(out_ref.shape[1], c)]
                _remote(landed, landed, send_sems, recv_sems, 6 * a + j, (cx, cy, c)).wait_recv()
        for cp in d2d(out_refs, send_sems, recv_sems, lambda c_: c_):
            cp.start()

    def finish(in_refs, out_refs, send_sems, recv_sems):
        for cp in d2d(out_refs, send_sems, recv_sems, lambda c_: 1 - c_):
            cp.wait_recv()
        for cp in ici(in_refs, out_refs, send_sems, recv_sems) + d2d(out_refs, send_sems, recv_sems, lambda c_: c_):
            cp.wait_send()

    shapes = [jax.ShapeDtypeStruct((N_CHIPS,) + s_.shape, s_.dtype) for s_ in shards]
    return _Exchange(shards, shapes, 6 * len(shards), begin, relay, finish)


def _swap_halves(name, g4s):
    n = len(g4s)

    def body(*refs):
        in_refs, out_refs, (send_sems, recv_sems) = refs[:n], refs[n:2 * n], refs[2 * n:]
        x, y, c, _ = _place()
        cps = [_remote(in_ref.at[:, _half_rows(in_ref.shape[1], 1 - c), :], out_ref, send_sems, recv_sems, a, (x, y, 1 - c))
               for a, (in_ref, out_ref) in enumerate(zip(in_refs, out_refs))]
        for cp in cps:
            cp.start()
        for cp in cps:
            cp.wait()

    return pl.pallas_call(
        body, name=name, in_specs=[HBM_SPEC] * n, out_specs=[HBM_SPEC] * n,
        out_shape=[jax.ShapeDtypeStruct((g.shape[0], g.shape[1] // 2, g.shape[2]), g.dtype) for g in g4s],
        scratch_shapes=_dma_sems(n),
    )(*g4s)


def _add_half(name, g4, r1, cvec, tr):
    n, r, w = g4.shape
    rh = r // 2
    nblk = rh // tr

    def body(c_ref, a_ref, b_ref, o_ref):
        o_ref[...] = (a_ref[...] + b_ref[...]).astype(BF16)

    return pl.pallas_call(
        body, name=name,
        grid_spec=pltpu.PrefetchScalarGridSpec(
            num_scalar_prefetch=1, grid=(n, nblk),
            in_specs=[pl.BlockSpec((None, tr, w), lambda k, i, c_ref: (k, c_ref[0] * nblk + i, 0)),
                      pl.BlockSpec((None, tr, w), lambda k, i, c_ref: (k, i, 0))],
            out_specs=pl.BlockSpec((None, tr, w), lambda k, i, c_ref: (k, i, 0))),
        out_shape=jax.ShapeDtypeStruct((n, rh, w), BF16),
        compiler_params=_params("parallel", "parallel"),
    )(cvec, g4, r1)


def _scatter_exchange(h4s):
    def sends(in_refs, out_refs, send_sems, recv_sems):
        x, y, c, chips = _place()
        return [_remote(in_ref.at[2 * cx + cy], out_ref.at[j], send_sems, recv_sems, 3 * a + j, (cx, cy, c))
                for a, (in_ref, out_ref) in enumerate(zip(in_refs, out_refs)) for j, (cx, cy) in enumerate(chips)]

    def begin(*parts):
        for cp in sends(*parts):
            cp.start()

    def finish(in_refs, out_refs, send_sems, recv_sems):
        x, y, c, chips = _place()
        for a, out_ref in enumerate(out_refs):
            for j, (cx, cy) in enumerate(chips):
                got = out_ref.at[j]
                _remote(got, got, send_sems, recv_sems, 3 * a + j, (cx, cy, c)).wait_recv()
        for cp in sends(in_refs, out_refs, send_sems, recv_sems):
            cp.wait_send()

    shapes = [jax.ShapeDtypeStruct((3,) + h.shape[1:], h.dtype) for h in h4s]
    return _Exchange(h4s, shapes, 3 * len(h4s), begin, None, finish)


def _sum_chips(name, h4, r3, mvec, tr):
    _, rh, w = h4.shape

    def body(m_ref, a_ref, b_ref, c_ref, d_ref, o_ref):
        o_ref[...] = ((a_ref[...].astype(F32) + b_ref[...].astype(F32)) + c_ref[...].astype(F32)) + d_ref[...].astype(F32)

    specs = [pl.BlockSpec((None, tr, w), lambda i, m_ref: (m_ref[0], i, 0))]
    specs += [pl.BlockSpec((None, tr, w), functools.partial(lambda k, i, m_ref: (k, i, 0), k)) for k in range(3)]
    return pl.pallas_call(
        body, name=name,
        grid_spec=pltpu.PrefetchScalarGridSpec(
            num_scalar_prefetch=1, grid=(rh // tr,), in_specs=specs,
            out_specs=pl.BlockSpec((tr, w), lambda i, m_ref: (i, 0))),
        out_shape=jax.ShapeDtypeStruct((rh, w), F32),
        compiler_params=_params("parallel"),
    )(mvec, h4, r3, r3, r3)


def _swap_reduced(name, ghs):
    n = len(ghs)

    def body(*refs):
        in_refs, out_refs, (send_sems, recv_sems) = refs[:n], refs[n:2 * n], refs[2 * n:]
        x, y, c, _ = _place()
        cps = [_remote(in_ref, out_ref, send_sems, recv_sems, a, (x, y, 1 - c))
               for a, (in_ref, out_ref) in enumerate(zip(in_refs, out_refs))]
        for cp in cps:
            cp.start()
        for cp in cps:
            cp.wait()

    return pl.pallas_call(
        body, name=name, in_specs=[HBM_SPEC] * n, out_specs=[HBM_SPEC] * n,
        out_shape=[jax.ShapeDtypeStruct(g.shape, g.dtype) for g in ghs],
        scratch_shapes=_dma_sems(n),
    )(*ghs)


def _small_update(name, partials, weights, moments1, moments2):
    n = len(partials)
    width = max(p.shape[1] for p in partials)
    starts, at = [], 0
    for p in partials:
        starts.append(at)
        at += p.shape[0]
    rows = -(-at // 8) * 8
    has_w = [w is not None for w in weights]
    n_w = sum(has_w)

    def body(*refs):
        p_refs = refs[:n]
        w_refs, m_refs, v_refs = refs[n:n + n_w], refs[n + n_w:n + 2 * n_w], refs[n + 2 * n_w:n + 3 * n_w]
        outs = refs[n + 3 * n_w:-4]
        g_refs, upd_refs = outs[:n], outs[n:]
        vec, buf, send_sems, recv_sems = refs[-4:]
        x, y, c, _ = _place()
        me = 4 * x + 2 * y + c
        vec[...] = jnp.zeros_like(vec)
        for p_ref, r0 in zip(p_refs, starts):
            vec[r0:r0 + p_ref.shape[0], 0:p_ref.shape[1]] = p_ref[...]
        buf[me] = vec[...]
        flips = [(fx, fy, fc) for fx in (0, 1) for fy in (0, 1) for fc in (0, 1)][1:]
        peers = [(x + fx - 2 * x * fx, y + fy - 2 * y * fy, c + fc - 2 * c * fc) for fx, fy, fc in flips]
        sends = [_remote(vec, buf.at[me], send_sems, recv_sems, k, peer) for k, peer in enumerate(peers)]
        for cp in sends:
            cp.start()
        for k, (px, py, pc) in enumerate(peers):
            got = buf.at[4 * px + 2 * py + pc]
            _remote(got, got, send_sems, recv_sems, k, (px, py, pc)).wait_recv()
        for cp in sends:
            cp.wait_send()
        total = buf[0]
        for dev in range(1, N_DEV):
            total = total + buf[dev]
        k = 0
        for a in range(n):
            r, w = g_refs[a].shape
            g = total[starts[a]:starts[a] + r, 0:w]
            g_refs[a][...] = g
            if has_w[a]:
                m2 = ADAM_B1 * m_refs[k][...] + (1.0 - ADAM_B1) * g
                v2 = ADAM_B2 * v_refs[k][...] + (1.0 - ADAM_B2) * (g * g)
                m_hat = m2 / (1.0 - ADAM_B1 ** ADAM_STEP)
                v_hat = v2 / (1.0 - ADAM_B2 ** ADAM_STEP)
                upd_refs[3 * k][...] = -ADAM_LR * (m_hat / (jnp.sqrt(v_hat) + ADAM_EPS) + ADAM_WD * w_refs[k][...])
                upd_refs[3 * k + 1][...] = m2
                upd_refs[3 * k + 2][...] = v2
                k += 1

    ws = [w for w in weights if w is not None]
    g_shapes = [jax.ShapeDtypeStruct(p.shape if w is None else w.shape, F32) for p, w in zip(partials, weights)]
    u_shapes = [jax.ShapeDtypeStruct(w.shape, F32) for w in ws for _ in range(3)]
    vm = pl.BlockSpec(memory_space=pltpu.VMEM)
    n_args = n + 3 * n_w
    outs = pl.pallas_call(
        body, name=name, in_specs=[vm] * n_args, out_specs=[vm] * (n + 3 * n_w), out_shape=g_shapes + u_shapes,
        scratch_shapes=[pltpu.VMEM((rows, width), F32), pltpu.VMEM((N_DEV, rows, width), F32),
                        pltpu.SemaphoreType.DMA((7,)), pltpu.SemaphoreType.DMA((7,))],
    )(*partials, *ws, *[m for m in moments1 if m is not None], *[v for v in moments2 if v is not None])
    return outs[:n], outs[n:]


GATE_COL = 3 * SB_WIDTH + 3 * FOX_WIDTH + FOX_HEADS + MEM_WIDTH
FL_COL = QKV_WIDTH


GROUP_A_COLS = [(0, QKV_WIDTH), (FL_COL + FOX_HEADS, MEM_WIDTH)]
GROUP_B_COLS = [(GATE_COL, MIX_WIDTH), (FL_COL, FOX_HEADS)]


def _group_from_shards(shard_of, cw, spans, pad):
    parts = []
    for lo, width in spans:
        hi = lo + width
        for j in range(N_CHIPS):
            a, b = max(lo, j * cw), min(hi, (j + 1) * cw)
            if a < b:
                parts.append(shard_of(j)[:, a - j * cw:b - j * cw])
    if pad:
        parts.append(jnp.zeros((parts[0].shape[0], pad), parts[0].dtype))
    return jnp.concatenate(parts, axis=1)


def _shard_from_groups(ga, gb, j, cw):
    lo, hi = j * cw, (j + 1) * cw
    placed = []
    for grp, spans in ((ga, GROUP_A_COLS), (gb, GROUP_B_COLS)):
        at = 0
        for first, width in spans:
            a, b = max(lo, first), min(hi, first + width)
            if a < b:
                placed.append((a, grp[:, at + a - first:at + b - first]))
            at += width
    return jnp.concatenate([p for _, p in sorted(placed, key=lambda t: t[0])], axis=1)


def _tile_of(n, cap, unit):
    if n <= cap:
        return n
    best = None
    for t in range(unit, cap + 1, unit):
        if n % t == 0:
            best = t
    assert best is not None, (n, cap, unit)
    return best


def _column_major_rows(a):
    dp, r, c = a.shape
    return a.transpose(2, 0, 1).reshape(c, dp, r // LANES, LANES).transpose(0, 2, 1, 3).reshape(-1, 8, LANES)


def _from_column_major_rows(b, shape):
    dp, r, c = shape
    return b.reshape(c, r // LANES, dp, LANES).transpose(0, 2, 1, 3).reshape(c, dp, r).transpose(1, 2, 0)


def _pack_small(parts):
    rows = []
    for p in parts:
        f = p.reshape(-1).astype(F32)
        f = jnp.pad(f, (0, (-f.shape[0]) % LANES))
        rows.append(f.reshape(-1, LANES))
    out = jnp.concatenate(rows, axis=0)
    return jnp.pad(out, ((0, (-out.shape[0]) % 8), (0, 0)))


def _unpack_small(packed, shapes):
    outs, r = [], 0
    for shp in shapes:
        n = 1
        for s_ in shp:
            n *= s_
        nr = -(-n // LANES)
        outs.append(packed[r:r + nr].reshape(-1)[:n].reshape(shp))
        r += nr
    return outs


def kernel(x, mem, norm_w, w_in, b_forget, mem_norm_w, w_mem_kv, out_norm_w, w_out, final_norm_w, loss_target, m_norm_w, m_w_in, m_b_forget, m_mem_norm_w, m_w_mem_kv, m_out_norm_w, m_w_out, m_final_norm_w, v_norm_w, v_w_in, v_b_forget, v_mem_norm_w, v_w_mem_kv, v_out_norm_w, v_w_out, v_final_norm_w):
    xs = x[0]
    mems = mem[0]
    target = loss_target[0]
    s, d = xs.shape
    depth = norm_w.shape[0]
    nb = s // TILE
    ts = _tile_of(s, 256, 8)
    big = (w_in, w_mem_kv, w_out)
    core = lax.axis_index("c")
    chip = 2 * lax.axis_index("x") + lax.axis_index("y")
    cvec = core.astype(jnp.int32).reshape(1)
    mvec = chip.astype(jnp.int32).reshape(1)
    cw = w_in.shape[2]

    own_w = [[a[l].astype(BF16) for a in big] for l in range(depth)]

    def lay_out(own, got):
        full = [jnp.where(lax.broadcasted_iota(jnp.int32, g.shape, 0) == chip, o[None], g) for g, o in zip(got, own)]
        shard_of = lambda j: full[0][j]
        wa_l = _group_from_shards(shard_of, cw, GROUP_A_COLS, 0)
        wb_l = _group_from_shards(shard_of, cw, GROUP_B_COLS, LANES - FOX_HEADS)
        return wa_l, wb_l, full[1].reshape(-1, full[1].shape[2]), full[2].reshape(-1, full[2].shape[2])

    layer_w = [lay_out(own_w[0], _run_exchange("gather_weights0", _gather_exchange(own_w[0])))]

    tm = _tile_of(s, 256, 8)
    fl_block = MIX_WIDTH // LANES

    saved = []
    cur = xs
    for l in range(depth):
        wa, wb, wkv, wout = layer_w[l]
        h = _rms_fwd(f"rms_fwd{l}", cur, norm_w[l][None], ts)
        pa = _mm(f"inproj_a{l}", h, wa, "nn", tm, _tile_of(PA, 1664, LANES), BF16)
        pb = _mm(f"inproj_b{l}", h, wb, "nn", tm, PB, F32)
        bpad = jnp.pad(b_forget[l], (0, LANES - FOX_HEADS))[None]
        ccol, crow = _gate_fwd(f"gate_fwd{l}", pb, bpad, fl_block)
        ccol4 = jnp.repeat(ccol[:, :FOX_HEADS].reshape(s, 4, 2).transpose(1, 0, 2), HEAD_DIM, axis=2)
        crow4 = jnp.pad(crow.reshape(nb, 4, 2, TILE).transpose(1, 0, 2, 3), ((0, 0), (0, 0), (0, 6), (0, 0)))
        next_gather = _gather_exchange(own_w[l + 1]) if l + 1 < depth else None
        o_sb, got = _sb_fwd(f"sb_fwd{l}", pa, 0, carried=next_gather)
        if next_gather is not None:
            layer_w.append(lay_out(own_w[l + 1], got))
        o_fx, lse_fx = _fox_fwd(f"fox_fwd{l}", pa, 3 * SB_WIDTH, ccol4, crow4)
        mn = _rms_fwd(f"mem_rms{l}", mems, mem_norm_w[l][None], mems.shape[0])
        mkv = _mm(f"mem_kv{l}", mn, wkv, "nn", mems.shape[0], 2 * MEM_WIDTH, BF16)
        o_m, lse_m = _mem_fwd(f"mem_fwd{l}", pa, mkv)
        nxt, y2 = _out_fwd(f"out_fwd{l}", o_sb, o_fx, o_m, pb, out_norm_w[l][None], cur, wout, ts)
        saved.append((cur, h, pa, pb, bpad, ccol4, crow4, o_sb, o_fx, lse_fx, mn, mkv, o_m, lse_m, y2))
        cur = nxt

    loss_v, dx, dxb, g_final = _final_loss("final_loss", cur, final_norm_w[None], target, ts)

    g_norm, g_b, g_memnorm, g_outnorm = [None] * depth, [None] * depth, [None] * depth, [None] * depth
    g_wa, g_wb, g_wkv, g_wout = [None] * depth, [None] * depth, [None] * depth, [None] * depth
    g_own = [[None] * depth for _ in big]
    g_other = [[None] * depth for _ in big]

    def reduce_at_owner(lr, chip_sums, from_chips, tiles):
        halves = [_sum_chips(f"grad_sum_chips{lr}_{k}", h_, r_, mvec, t_)
                  for k, (h_, r_, t_) in enumerate(zip(chip_sums, from_chips, tiles))]
        others = _swap_reduced(f"grad_swap_reduced{lr}", halves)
        for k in range(len(big)):
            g_own[k][lr], g_other[k][lr] = halves[k], others[k]

    pending = None
    for l in reversed(range(depth)):
        xin, h, pa, pb, bpad, ccol4, crow4, o_sb, o_fx, lse_fx, mn, mkv, o_m, lse_m, y2 = saved[l]
        wa, wb, wkv, wout = layer_w[l]
        dy, dgate, g_outnorm[l] = _out_bwd(f"out_bwd{l}", dxb, o_sb, o_fx, o_m, pb, out_norm_w[l][None], wout, ts)
        g_wout[l] = _mm(f"dw_out{l}", y2, dxb, "tn", _tile_of(MIX_WIDTH, 640, LANES), d, F32)
        scatter = _scatter_exchange(pending[1]) if pending is not None else None
        (dq_sb, dk_sb, dv_sb), from_chips = _sb_bwd(f"sb_bwd{l}", pa, 0, dy, 0, carried=scatter)
        if pending is not None:
            reduce_at_owner(pending[0], pending[1], from_chips, pending[2])
        dq_fx, dk_fx, dv_fx, cs4 = _fox_bwd(f"fox_bwd{l}", pa, 3 * SB_WIDTH, ccol4, crow4, o_fx, lse_fx, dy, SB_WIDTH)
        colsum = cs4[:, :, :2, :].transpose(1, 0, 2, 3).reshape(nb, 8, TILE)
        dlogit, g_b[l] = _gate_bwd(f"gate_bwd{l}", pb, bpad, colsum, fl_block)
        dq_m, dk_m, dv_m = _mem_bwd(f"mem_bwd{l}", pa, mkv, o_m, lse_m, dy, SB_WIDTH + FOX_WIDTH)
        dmkv = jnp.concatenate([dk_m, dv_m], axis=1)
        g_wkv[l] = _mm(f"dw_kv{l}", mn, dmkv, "tn", d, 2 * MEM_WIDTH, F32)
        dmn = _mm(f"dmem{l}", dmkv, wkv, "nt", mems.shape[0], d, F32)
        g_memnorm[l] = _rms_wgrad(f"mem_norm_grad{l}", mems, dmn)
        dpa = jnp.concatenate([dq_sb, dk_sb, dv_sb, dq_fx, dk_fx, dv_fx, dq_m], axis=1)
        dpb = jnp.concatenate([dgate, dlogit], axis=1)
        tw = _tile_of(d, 512, LANES)
        g_wa[l] = _mm(f"dw_in_a{l}", h, dpa, "tn", tw, _tile_of(PA, 1664, LANES), F32)
        g_wb[l] = _mm(f"dw_in_b{l}", h, dpb, "tn", tw, PB, F32)
        dh = _mm(f"dh_a{l}", dpa, wa, "nt", tm, d, F32)
        dh = _mm(f"dh_b{l}", dpb, wb, "nt", tm, d, F32, res=dh)
        dx, dxb, g_norm[l] = _rms_bwd(f"rms_bwd{l}", xin, norm_w[l][None], dh, dx, ts)
        g4s = [jnp.stack([_shard_from_groups(g_wa[l], g_wb[l], j, cw) for j in range(N_CHIPS)]),
               g_wkv[l].reshape(N_CHIPS, -1, g_wkv[l].shape[1]), g_wout[l].reshape(N_CHIPS, -1, d)]
        tiles = [_tile_of(g.shape[1] // 2, 256, 16) for g in g4s]
        from_sibling = _swap_halves(f"grad_swap_halves{l}", g4s)
        chip_sums = [_add_half(f"grad_add_half{l}_{k}", g, r, cvec, t)
                     for k, (g, r, t) in enumerate(zip(g4s, from_sibling, tiles))]
        pending = (l, chip_sums, tiles)
    reduce_at_owner(pending[0], pending[1], _run_exchange(f"grad_scatter_chips{pending[0]}", _scatter_exchange(pending[1])),
                    pending[2])

    small_w = [norm_w, b_forget, mem_norm_w, out_norm_w, final_norm_w]
    small_m = [m_norm_w, m_b_forget, m_mem_norm_w, m_out_norm_w, m_final_norm_w]
    small_v = [v_norm_w, v_b_forget, v_mem_norm_w, v_out_norm_w, v_final_norm_w]
    rows2 = lambda a: a.reshape(-1, a.shape[-1])
    partials = [jnp.concatenate(g_norm, axis=0), jnp.concatenate(g_b, axis=0), jnp.concatenate(g_memnorm, axis=0),
                jnp.concatenate(g_outnorm, axis=0), g_final, loss_v]
    sums, updates = _small_update("small_update", partials, [rows2(a) for a in small_w] + [None],
                                  [rows2(a) for a in small_m] + [None], [rows2(a) for a in small_v] + [None])
    small_grads = [g.reshape(a.shape) for g, a in zip(sums, small_w)]
    loss = sums[-1][0, 0]
    small_delta, small_m2, small_v2 = ([updates[3 * k + t].reshape(a.shape) for k, a in enumerate(small_w)]
                                       for t in range(3))
    big_grads, big_delta, big_m2, big_v2 = [], [], [], []
    for k, (nm, w_, m_, v_) in enumerate(zip(("w_in", "w_mem_kv", "w_out"), big, (m_w_in, m_w_mem_kv, m_w_out),
                                             (v_w_in, v_w_mem_kv, v_w_out))):
        if w_.shape[2] % LANES:
            g_full = jnp.stack([jnp.concatenate([jnp.where(core == 0, go, gt), jnp.where(core == 0, gt, go)], axis=0)
                                for go, gt in zip(g_own[k], g_other[k])])
            w_p, g_p, m_p, v_p = (_column_major_rows(a) for a in (w_, g_full, m_, v_))
            outs = _adamw(f"adamw_{nm}", w_p, g_p, m_p, v_p, _tile_of(w_p.shape[0], 600, 1))
            outs = [_from_column_major_rows(o, w_.shape) for o in (g_p, *outs)]
        else:
            outs = _adamw_sharded(f"adamw_{nm}", w_, m_, v_, g_own[k], g_other[k], cvec,
                                  _tile_of(w_.shape[1] // 2, 256, 8))
        for lst, o in zip((big_grads, big_delta, big_m2, big_v2), outs):
            lst.append(o)

    def order(sm, bg):
        return [sm[0], bg[0], sm[1], sm[2], bg[1], sm[3], bg[2], sm[4]]

    return (loss, dx[None], *order(small_grads, big_grads), *order(small_delta, big_delta),
            *order(small_m2, big_m2), *order(small_v2, big_v2))
```

```python
import functools

import jax
import jax.numpy as jnp
from jax import lax
from jax.experimental import pallas as pl
from jax.experimental.pallas import tpu as pltpu

F32 = jnp.float32
BF16 = jnp.bfloat16

HEAD_DIM = 64
SB_WIDTH = 512
FOX_WIDTH = 512
FOX_HEADS = 8
MEM_WIDTH = 256
MIX_WIDTH = SB_WIDTH + FOX_WIDTH + MEM_WIDTH
TOTAL_HEADS = MIX_WIDTH // HEAD_DIM
IN_WIDTH = 3 * SB_WIDTH + 3 * FOX_WIDTH + FOX_HEADS + MEM_WIDTH + MIX_WIDTH
LANES = 128
QKV_WIDTH = 3 * SB_WIDTH + 3 * FOX_WIDTH
PA = QKV_WIDTH + MEM_WIDTH
PB = LANES + MIX_WIDTH
EPS = 1e-6
SCALE = HEAD_DIM ** -0.5
TILE = 256
SB_GROUP = 4
SB_LANES = SB_GROUP * HEAD_DIM
NEG_INF = float("-inf")
MASKED = -1e30

ADAM_LR = 0.001
ADAM_B1 = 0.9
ADAM_B2 = 0.999
ADAM_EPS = 1e-08
ADAM_WD = 0.01
ADAM_STEP = 10

N_CHIPS = 4
N_DEV = 8
VMEM_LIMIT = 48 * 1024 * 1024
MESH = pl.DeviceIdType.MESH


def _params(*sem):
    return pltpu.CompilerParams(dimension_semantics=tuple(sem), vmem_limit_bytes=VMEM_LIMIT)


def _dot(a, b):
    return jnp.dot(a, b, preferred_element_type=F32)


def _dot_nt(a, b):
    return lax.dot_general(a, b, (((1,), (1,)), ((), ())), preferred_element_type=F32)


def _dot_tn(a, b):
    return lax.dot_general(a, b, (((0,), (0,)), ((), ())), preferred_element_type=F32)


def _split2(x):
    hi = x.astype(BF16)
    lo = (x - hi.astype(F32)).astype(BF16)
    return hi, lo


def _split3(x):
    hi = x.astype(BF16)
    r = x - hi.astype(F32)
    mid = r.astype(BF16)
    lo = (r - mid.astype(F32)).astype(BF16)
    return hi, mid, lo


def _sum_l2(x, u):
    hi, lo = _split2(x)
    return _dot(hi, u) + _dot(lo, u)


def _sum_l3(x, u):
    hi, mid, lo = _split3(x)
    return _dot(hi, u) + _dot(mid, u) + _dot(lo, u)


def _sum_r3(u, x):
    hi, mid, lo = _split3(x)
    return _dot(u, hi) + _dot(u, mid) + _dot(u, lo)


def _softplus(z):
    return jnp.maximum(z, 0.0) + jnp.log1p(jnp.exp(-jnp.abs(z)))


def _tri(n, pred):
    r = lax.broadcasted_iota(jnp.int32, (n, n), 0)
    c = lax.broadcasted_iota(jnp.int32, (n, n), 1)
    return jnp.where(pred(r, c), 1.0, 0.0).astype(BF16)


def _rows(ref, j, n=TILE):
    return pl.ds(pl.multiple_of(j * n, n), n)


def _mm(name, a, b, mode, tm, tn, out_dtype, res=None, a_lead=(), b_lead=()):
    a2, b2 = a.shape[len(a_lead):], b.shape[len(b_lead):]
    if mode == "tn":
        k, m = a2
    else:
        m, k = a2
    n = b2[0] if mode == "nt" else b2[1]
    assert m % tm == 0 and n % tn == 0, (name, m, tm, n, tn)
    na, nb = (None,) * len(a_lead), (None,) * len(b_lead)
    if mode == "tn":
        a_spec = pl.BlockSpec(na + (k, tm), lambda j, i: a_lead + (0, i))
    else:
        a_spec = pl.BlockSpec(na + (tm, k), lambda j, i: a_lead + (i, 0))
    if mode == "nt":
        b_spec = pl.BlockSpec(nb + (tn, k), lambda j, i: b_lead + (j, 0))
    else:
        b_spec = pl.BlockSpec(nb + (k, tn), lambda j, i: b_lead + (0, j))
    o_spec = pl.BlockSpec((tm, tn), lambda j, i: (i, j))
    dot = {"nn": _dot, "nt": _dot_nt, "tn": _dot_tn}[mode]

    def body(a_ref, b_ref, *rest):
        o_ref = rest[-1]
        acc = dot(a_ref[...].astype(BF16), b_ref[...].astype(BF16))
        if res is not None:
            acc = acc + rest[0][...]
        o_ref[...] = acc.astype(o_ref.dtype)

    args, specs = [a, b], [a_spec, b_spec]
    if res is not None:
        args.append(res)
        specs.append(o_spec)
    return pl.pallas_call(
        body, name=name, grid=(n // tn, m // tm), in_specs=specs, out_specs=o_spec,
        out_shape=jax.ShapeDtypeStruct((m, n), out_dtype),
        compiler_params=_params("parallel", "parallel"),
    )(*args)


def _rms_fwd(name, x, g, ts):
    s, d = x.shape

    def body(x_ref, g_ref, o_ref):
        xf = x_ref[...]
        r = lax.rsqrt(jnp.mean(xf * xf, axis=1, keepdims=True) + EPS)
        o_ref[...] = (xf * r * g_ref[...]).astype(BF16)

    return pl.pallas_call(
        body, name=name, grid=(s // ts,),
        in_specs=[pl.BlockSpec((ts, d), lambda i: (i, 0)), pl.BlockSpec((1, d), lambda i: (0, 0))],
        out_specs=pl.BlockSpec((ts, d), lambda i: (i, 0)),
        out_shape=jax.ShapeDtypeStruct((s, d), BF16),
        compiler_params=_params("parallel"),
    )(x, g)


def _rms_bwd(name, x, g, dh, dres, ts):
    s, d = x.shape

    def body(x_ref, g_ref, dh_ref, dres_ref, dx_ref, dxb_ref, dg_ref):
        @pl.when(pl.program_id(0) == 0)
        def _():
            dg_ref[...] = jnp.zeros_like(dg_ref)

        xf = x_ref[...]
        r = lax.rsqrt(jnp.mean(xf * xf, axis=1, keepdims=True) + EPS)
        xh = xf * r
        dhf = dh_ref[...]
        dg_ref[...] += jnp.sum(dhf * xh, axis=0, keepdims=True)
        dxh = dhf * g_ref[...]
        m = jnp.mean(dxh * xh, axis=1, keepdims=True)
        dx = r * (dxh - xh * m) + dres_ref[...]
        dx_ref[...] = dx
        dxb_ref[...] = dx.astype(BF16)

    row = pl.BlockSpec((ts, d), lambda i: (i, 0))
    vec = pl.BlockSpec((1, d), lambda i: (0, 0))
    return pl.pallas_call(
        body, name=name, grid=(s // ts,), in_specs=[row, vec, row, row], out_specs=[row, row, vec],
        out_shape=[jax.ShapeDtypeStruct((s, d), F32), jax.ShapeDtypeStruct((s, d), BF16),
                   jax.ShapeDtypeStruct((1, d), F32)],
        compiler_params=_params("arbitrary"),
    )(x, g, dh, dres)


def _rms_wgrad(name, x, dh):
    m_, d = x.shape

    def body(x_ref, dh_ref, dg_ref):
        xf = x_ref[...]
        r = lax.rsqrt(jnp.mean(xf * xf, axis=1, keepdims=True) + EPS)
        dg_ref[...] = jnp.sum(dh_ref[...] * xf * r, axis=0, keepdims=True)

    return pl.pallas_call(
        body, name=name, out_shape=jax.ShapeDtypeStruct((1, d), F32),
    )(x, dh)


def _final_loss(name, x, g, target, ts):
    s, d = x.shape

    def body(x_ref, g_ref, t_ref, loss_ref, dx_ref, dxb_ref, dg_ref):
        @pl.when(pl.program_id(0) == 0)
        def _():
            dg_ref[...] = jnp.zeros_like(dg_ref)
            loss_ref[...] = jnp.zeros_like(loss_ref)

        xf = x_ref[...]
        gw = g_ref[...]
        r = lax.rsqrt(jnp.mean(xf * xf, axis=1, keepdims=True) + EPS)
        xh = xf * r
        e = xh * gw - t_ref[...]
        part = 0.5 * jnp.sum(jnp.mean(e * e, axis=1, keepdims=True), axis=0, keepdims=True)
        loss_ref[...] += jnp.broadcast_to(part, loss_ref.shape)
        dy = e * (1.0 / d)
        dg_ref[...] += jnp.sum(dy * xh, axis=0, keepdims=True)
        dxh = dy * gw
        m = jnp.mean(dxh * xh, axis=1, keepdims=True)
        dx = r * (dxh - xh * m)
        dx_ref[...] = dx
        dxb_ref[...] = dx.astype(BF16)

    row = pl.BlockSpec((ts, d), lambda i: (i, 0))
    vec = pl.BlockSpec((1, d), lambda i: (0, 0))
    lvec = pl.BlockSpec((1, LANES), lambda i: (0, 0))
    return pl.pallas_call(
        body, name=name, grid=(s // ts,), in_specs=[row, vec, row], out_specs=[lvec, row, row, vec],
        out_shape=[jax.ShapeDtypeStruct((1, LANES), F32), jax.ShapeDtypeStruct((s, d), F32),
                   jax.ShapeDtypeStruct((s, d), BF16), jax.ShapeDtypeStruct((1, d), F32)],
        compiler_params=_params("arbitrary"),
    )(x, g, target)


def _gate_fwd(name, pb, bpad, fl_block):
    s = pb.shape[0]
    nb = s // TILE

    def body(fl_ref, b_ref, ccol_ref, crow_ref, carry):
        @pl.when(pl.program_id(0) == 0)
        def _():
            carry[...] = jnp.zeros_like(carry)

        u = fl_ref[...] + b_ref[...]
        lf = jnp.minimum(u, 0.0) - jnp.log1p(jnp.exp(-jnp.abs(u)))
        lower = _tri(TILE, lambda r, c: c <= r)
        c = _sum_r3(lower, lf) + carry[0:1, :]
        ccol_ref[...] = c
        crow_ref[0] = c.T[0:8, :]
        carry[...] = jnp.broadcast_to(c[TILE - 1:TILE, :], carry.shape)

    return pl.pallas_call(
        body, name=name, grid=(nb,),
        in_specs=[pl.BlockSpec((TILE, LANES), lambda i: (i, fl_block)), pl.BlockSpec((1, LANES), lambda i: (0, 0))],
        out_specs=[pl.BlockSpec((TILE, LANES), lambda i: (i, 0)), pl.BlockSpec((1, 8, TILE), lambda i: (i, 0, 0))],
        out_shape=[jax.ShapeDtypeStruct((s, LANES), F32), jax.ShapeDtypeStruct((nb, 8, TILE), F32)],
        scratch_shapes=[pltpu.VMEM((8, LANES), F32)],
        compiler_params=_params("arbitrary"),
    )(pb, bpad)


def _gate_bwd(name, pb, bpad, colsum, fl_block):
    s = pb.shape[0]
    nb = s // TILE

    def body(fl_ref, b_ref, cs_ref, dl_ref, db_ref, carry):
        @pl.when(pl.program_id(0) == 0)
        def _():
            carry[...] = jnp.zeros_like(carry)
            db_ref[...] = jnp.zeros_like(db_ref)

        upper = _tri(TILE, lambda r, c: r >= c)
        rsum = _sum_l3(cs_ref[0], upper) + carry[:, 0:1]
        carry[...] = jnp.broadcast_to(rsum[:, 0:1], carry.shape)
        full = jnp.concatenate([rsum, jnp.zeros((LANES - 8, TILE), F32)], axis=0)
        dlf = -full.T
        u = fl_ref[...] + b_ref[...]
        dlogit = dlf * (1.0 - jax.nn.sigmoid(u))
        dl_ref[...] = dlogit.astype(BF16)
        db_ref[...] += jnp.sum(dlogit, axis=0, keepdims=True)

    rev = lambda i: (nb - 1 - i, 0)
    return pl.pallas_call(
        body, name=name, grid=(nb,),
        in_specs=[pl.BlockSpec((TILE, LANES), lambda i: (nb - 1 - i, fl_block)),
                  pl.BlockSpec((1, LANES), lambda i: (0, 0)),
                  pl.BlockSpec((1, 8, TILE), lambda i: (nb - 1 - i, 0, 0))],
        out_specs=[pl.BlockSpec((TILE, LANES), rev), pl.BlockSpec((1, LANES), lambda i: (0, 0))],
        out_shape=[jax.ShapeDtypeStruct((s, LANES), BF16), jax.ShapeDtypeStruct((1, LANES), F32)],
        scratch_shapes=[pltpu.VMEM((8, LANES), F32)],
        compiler_params=_params("arbitrary"),
    )(pb, bpad, colsum)


def _head_slices(hh):
    return slice(HEAD_DIM * hh, HEAD_DIM * (hh + 1))


def _scaled_q(q_ref, sl, scale=SCALE):
    return (q_ref[:, sl].astype(F32) * scale).astype(BF16)


def _neg_abs(x):
    sign = jnp.uint32(0x80000000)
    return lax.bitcast_convert_type(lax.bitcast_convert_type(x, jnp.uint32) | sign, F32)


def _sb_tile(qn, kj, carry, strict, u_after, diag):
    nz = _dot_nt(qn, kj)
    lf = jnp.minimum(nz, 0.0) - jnp.log(1.0 + jnp.exp(_neg_abs(nz)))
    lsig = lf - nz
    if diag:
        lf = jnp.where(strict, lf, 0.0)
    sx = _dot(lf.astype(BF16), u_after)
    a = jnp.exp(lsig + sx + carry)
    if diag:
        a = jnp.where(strict, a, 0.0)
    return lsig, a, carry + sx[:, 0:1] + lf[:, 0:1]


def _pair_grid_call(name, body, nb, in_specs, out_specs, out_shape, scratch, args, carried=None, groups=4):
    if carried is None:
        return pl.pallas_call(
            body, name=name, grid=(groups, nb), in_specs=in_specs, out_specs=out_specs, out_shape=out_shape,
            scratch_shapes=scratch, compiler_params=_params("arbitrary", "arbitrary"),
        )(*args)
    n_in, n_out, n_ex = len(in_specs), len(out_specs), carried.n

    def body_with_copies(*refs):
        own_in, ex_in = refs[:n_in], refs[n_in:n_in + n_ex]
        own_out = refs[n_in + n_ex:n_in + n_ex + n_out]
        ex_out = refs[n_in + n_ex + n_out:n_in + 2 * n_ex + n_out]
        own_scratch, sems = refs[n_in + 2 * n_ex + n_out:-2], refs[-2:]
        parts = (ex_in, ex_out, sems[0], sems[1])
        p, i = pl.program_id(0), pl.program_id(1)
        pl.when(jnp.logical_and(p == 0, i == 0))(lambda: carried.begin(*parts))
        if carried.relay is not None:
            pl.when(jnp.logical_and(p == groups - 1, i == 0))(lambda: carried.relay(*parts))
        body(*own_in, *own_out, *own_scratch)
        pl.when(jnp.logical_and(p == groups - 1, i == nb - 1))(lambda: carried.finish(*parts))

    return pl.pallas_call(
        body_with_copies, name=name, grid=(groups, nb), in_specs=list(in_specs) + [HBM_SPEC] * n_ex,
        out_specs=list(out_specs) + [HBM_SPEC] * n_ex, out_shape=list(out_shape) + carried.out_shapes,
        scratch_shapes=list(scratch) + _dma_sems(carried.n_sems),
        compiler_params=_params("arbitrary", "arbitrary"),
    )(*args, *carried.inputs)


def _sb_fwd(name, pa, col0, carried=None):
    s = pa.shape[0]
    nb = s // TILE
    cb = col0 // SB_LANES
    kb = SB_WIDTH // SB_LANES

    def body(q_ref, k_ref, v_ref, o_ref, lsig_s, lf_s):
        i = pl.program_id(1)
        r = lax.broadcasted_iota(jnp.int32, (TILE, TILE), 0)
        c = lax.broadcasted_iota(jnp.int32, (TILE, TILE), 1)
        strict = c < r
        u_after = _tri(TILE, lambda rr, cc: rr > cc)
        qs = [_scaled_q(q_ref, _head_slices(hh), -SCALE) for hh in range(SB_GROUP)]

        def neg_z(j):
            kblk = k_ref[_rows(k_ref, j), :]
            return [_dot_nt(qs[hh], kblk[:, _head_slices(hh)]) for hh in range(SB_GROUP)]

        def scores(nzs, slot, diag):
            for hh, nz in enumerate(nzs):
                lf = jnp.minimum(nz, 0.0) - jnp.log(1.0 + jnp.exp(_neg_abs(nz)))
                lsig = lf - nz
                if diag:
                    lf = jnp.where(strict, lf, 0.0)
                    lsig = jnp.where(strict, lsig, MASKED)
                lsig_s[slot, hh] = lsig
                lf_s[slot, hh] = lf.astype(BF16)

        def weigh(j, slot, state):
            vblk = v_ref[_rows(v_ref, j), :]
            new = []
            for hh in range(SB_GROUP):
                carry, acc = state[hh]
                lfb = lf_s[slot, hh]
                sx = _dot(lfb, u_after)
                a = jnp.exp(lsig_s[slot, hh] + sx + carry)
                new.append((carry + sx[:, 0:1] + lfb[:, 0:1].astype(F32),
                            acc + _dot(a.astype(BF16), vblk[:, _head_slices(hh)])))
            return tuple(new)

        def step(t, state):
            state = weigh(i - t + 1, (t - 1) % 2, state)
            scores(neg_z(i - t), t % 2, False)
            return state

        zero = (jnp.zeros((TILE, 1), F32), jnp.zeros((TILE, HEAD_DIM), F32))
        scores(neg_z(i), 0, True)
        state = lax.fori_loop(1, i + 1, step, (zero,) * SB_GROUP)
        state = weigh(0, i % 2, state)
        o_ref[...] = jnp.concatenate([st[1] for st in state], axis=1)

    outs = _pair_grid_call(
        name, body, nb,
        in_specs=[pl.BlockSpec((TILE, SB_LANES), lambda p, i: (i, cb + p)),
                  pl.BlockSpec((s, SB_LANES), lambda p, i: (0, cb + kb + p)),
                  pl.BlockSpec((s, SB_LANES), lambda p, i: (0, cb + 2 * kb + p))],
        out_specs=[pl.BlockSpec((TILE, SB_LANES), lambda p, i: (i, p))],
        out_shape=[jax.ShapeDtypeStruct((s, SB_WIDTH), F32)],
        scratch=[pltpu.VMEM((2, SB_GROUP, TILE, TILE), F32), pltpu.VMEM((2, SB_GROUP, TILE, TILE), BF16)],
        args=(pa, pa, pa), carried=carried, groups=kb)
    return outs[0], outs[1:]


def _sb_bwd(name, pa, col0, dout, dcol0, carried=None):
    s = pa.shape[0]
    nb = s // TILE
    cb = col0 // SB_LANES
    kb = SB_WIDTH // SB_LANES
    db = dcol0 // SB_LANES

    def body(q_ref, k_ref, v_ref, do_ref, dq_ref, dk_ref, dv_ref, dk_acc, dv_acc, dpan, span, gsum, lsig_s, lf_s):
        i = pl.program_id(1)

        @pl.when(i == 0)
        def _():
            dk_acc[...] = jnp.zeros_like(dk_acc)
            dv_acc[...] = jnp.zeros_like(dv_acc)

        r = lax.broadcasted_iota(jnp.int32, (TILE, TILE), 0)
        c = lax.broadcasted_iota(jnp.int32, (TILE, TILE), 1)
        strict = c < r
        u_after = _tri(TILE, lambda rr, cc: rr > cc)
        u_before = _tri(TILE, lambda rr, cc: rr < cc)
        qs = [_scaled_q(q_ref, _head_slices(hh), -SCALE) for hh in range(SB_GROUP)]
        dos = [do_ref[:, _head_slices(hh)].astype(BF16) for hh in range(SB_GROUP)]

        def scores(j, slot, diag):
            kblk = k_ref[_rows(k_ref, j), :]
            for hh in range(SB_GROUP):
                nz = _dot_nt(qs[hh], kblk[:, _head_slices(hh)])
                lf = jnp.minimum(nz, 0.0) - jnp.log(1.0 + jnp.exp(_neg_abs(nz)))
                lsig = lf - nz
                if diag:
                    lf = jnp.where(strict, lf, 0.0)
                    lsig = jnp.where(strict, lsig, MASKED)
                lsig_s[slot, hh] = lsig
                lf_s[slot, hh] = lf.astype(BF16)

        def grads(j, slot, carries):
            vblk = v_ref[_rows(v_ref, j), :]
            new = []
            for hh in range(SB_GROUP):
                lfb = lf_s[slot, hh]
                lsig = lsig_s[slot, hh]
                sx = _dot(lfb, u_after)
                a = jnp.exp(lsig + sx + carries[hh])
                g = a * _dot_nt(dos[hh], vblk[:, _head_slices(hh)])
                sig = jnp.exp(lsig)
                inside = _dot(g.astype(BF16), u_before)
                dpan[hh, j] = sig * (inside + g) - g
                span[hh, j] = sig
                gsum[hh, j] = inside[:, TILE - 1:TILE] + g[:, TILE - 1:TILE]
                dv_acc[hh, _rows(None, j), :] += _dot_tn(a.astype(BF16), dos[hh])
                new.append(carries[hh] + sx[:, 0:1] + lfb[:, 0:1].astype(F32))
            return tuple(new)

        def step1(t, carries):
            carries = grads(i - t + 1, (t - 1) % 2, carries)
            scores(i - t, t % 2, False)
            return carries

        zero1 = jnp.zeros((TILE, 1), F32)
        scores(i, 0, True)
        carries = lax.fori_loop(1, i + 1, step1, (zero1,) * SB_GROUP)
        grads(0, i % 2, carries)

        def pass2(j, state):
            kblk = k_ref[_rows(k_ref, j), :]
            new = []
            for hh in range(SB_GROUP):
                before, ndq = state[hh]
                ndzb = (dpan[hh, j] + span[hh, j] * before).astype(BF16)
                dk_acc[hh, _rows(None, j), :] += _dot_tn(ndzb, qs[hh])
                new.append((before + gsum[hh, j], ndq + _dot(ndzb, kblk[:, _head_slices(hh)])))
            return tuple(new)

        zero2 = (zero1, jnp.zeros((TILE, HEAD_DIM), F32))
        state = lax.fori_loop(0, i + 1, pass2, (zero2,) * SB_GROUP)
        dq_ref[...] = jnp.concatenate([st[1] * -SCALE for st in state], axis=1).astype(BF16)

        @pl.when(i == nb - 1)
        def _():
            dk_ref[...] = jnp.concatenate([dk_acc[hh] for hh in range(SB_GROUP)], axis=1).astype(BF16)
            dv_ref[...] = jnp.concatenate([dv_acc[hh] for hh in range(SB_GROUP)], axis=1).astype(BF16)

    qspec = pl.BlockSpec((TILE, SB_LANES), lambda p, i: (i, p))
    kvspec = pl.BlockSpec((s, SB_LANES), lambda p, i: (0, p))
    out = jax.ShapeDtypeStruct((s, SB_WIDTH), BF16)
    outs = _pair_grid_call(
        name, body, nb,
        in_specs=[pl.BlockSpec((TILE, SB_LANES), lambda p, i: (i, cb + p)),
                  pl.BlockSpec((s, SB_LANES), lambda p, i: (0, cb + kb + p)),
                  pl.BlockSpec((s, SB_LANES), lambda p, i: (0, cb + 2 * kb + p)),
                  pl.BlockSpec((TILE, SB_LANES), lambda p, i: (i, db + p))],
        out_specs=[qspec, kvspec, kvspec], out_shape=[out, out, out],
        scratch=[pltpu.VMEM((SB_GROUP, s, HEAD_DIM), F32), pltpu.VMEM((SB_GROUP, s, HEAD_DIM), F32),
                 pltpu.VMEM((SB_GROUP, nb, TILE, TILE), F32), pltpu.VMEM((SB_GROUP, nb, TILE, TILE), F32),
                 pltpu.VMEM((SB_GROUP, nb, TILE, 1), F32),
                 pltpu.VMEM((2, SB_GROUP, TILE, TILE), F32), pltpu.VMEM((2, SB_GROUP, TILE, TILE), BF16)],
        args=(pa, pa, pa, dout), carried=carried, groups=kb)
    return outs[:3], outs[3:]


def _fox_scores(q, kj, cq, crj, causal, diag):
    sc = _dot_nt(q, kj) + (cq - crj)
    if diag:
        sc = jnp.where(causal, sc, NEG_INF)
    return sc


def _fox_fwd(name, pa, col0, ccol4, crow4):
    s = pa.shape[0]
    nb = s // TILE
    cb = col0 // LANES

    def body(q_ref, k_ref, v_ref, cc_ref, cr_ref, o_ref, lse_ref, sc_s):
        i = pl.program_id(1)
        r = lax.broadcasted_iota(jnp.int32, (TILE, TILE), 0)
        c = lax.broadcasted_iota(jnp.int32, (TILE, TILE), 1)
        causal = c <= r
        qs = [_scaled_q(q_ref, _head_slices(hh)) for hh in range(2)]
        cqs = [cc_ref[:, HEAD_DIM * hh:HEAD_DIM * hh + 1] for hh in range(2)]

        def logits(j, slot, diag):
            kblk = k_ref[_rows(k_ref, j), :]
            tops = []
            for hh in range(2):
                sc = _fox_scores(qs[hh], kblk[:, _head_slices(hh)], cqs[hh], cr_ref[j, hh:hh + 1, :], causal, diag)
                sc_s[slot, hh] = sc
                tops.append(jnp.max(sc, axis=1, keepdims=True))
            return tuple(tops)

        def update(j, slot, tops, state):
            vblk = v_ref[_rows(v_ref, j), :]
            new = []
            for hh in range(2):
                m, l, acc = state[hh]
                m2 = jnp.maximum(m, tops[hh])
                alpha = jnp.exp(m - m2)
                p = jnp.exp(sc_s[slot, hh] - m2)
                new.append((m2, l * alpha + jnp.sum(p, axis=1, keepdims=True),
                            acc * alpha + _dot(p.astype(BF16), vblk[:, _head_slices(hh)])))
            return tuple(new)

        def step(t, both):
            tops, state = both
            state = update(i - t + 1, (t - 1) % 2, tops, state)
            return logits(i - t, t % 2, False), state

        zero = (jnp.full((TILE, 1), NEG_INF, F32), jnp.zeros((TILE, 1), F32), jnp.zeros((TILE, HEAD_DIM), F32))
        tops, state = lax.fori_loop(1, i + 1, step, (logits(i, 0, True), (zero, zero)))
        state = update(0, i % 2, tops, state)
        o_ref[...] = jnp.concatenate([state[hh][2] / state[hh][1] for hh in range(2)], axis=1)
        lse_ref[...] = jnp.concatenate(
            [jnp.broadcast_to(state[hh][0] + jnp.log(state[hh][1]), (TILE, HEAD_DIM)) for hh in range(2)], axis=1)

    return pl.pallas_call(
        body, name=name, grid=(4, nb),
        in_specs=[pl.BlockSpec((TILE, LANES), lambda p, i: (i, cb + p)),
                  pl.BlockSpec((s, LANES), lambda p, i: (0, cb + 4 + p)),
                  pl.BlockSpec((s, LANES), lambda p, i: (0, cb + 8 + p)),
                  pl.BlockSpec((None, TILE, LANES), lambda p, i: (p, i, 0)),
                  pl.BlockSpec((None, nb, 8, TILE), lambda p, i: (p, 0, 0, 0))],
        out_specs=[pl.BlockSpec((TILE, LANES), lambda p, i: (i, p)),
                   pl.BlockSpec((None, TILE, LANES), lambda p, i: (p, i, 0))],
        out_shape=[jax.ShapeDtypeStruct((s, FOX_WIDTH), F32), jax.ShapeDtypeStruct((4, s, LANES), F32)],
        scratch_shapes=[pltpu.VMEM((2, 2, TILE, TILE), F32)],
        compiler_params=_params("parallel", "arbitrary"),
    )(pa, pa, pa, ccol4, crow4)


def _fox_bwd(name, pa, col0, ccol4, crow4, out, lse, dout, dcol0):
    s = pa.shape[0]
    nb = s // TILE
    cb = col0 // LANES
    db = dcol0 // LANES

    def body(q_ref, k_ref, v_ref, cc_ref, cr_ref, o_ref, lse_ref, do_ref,
             dq_ref, dk_ref, dv_ref, cs_ref, dk_acc, dv_acc, p_s, ds_s):
        i = pl.program_id(1)

        @pl.when(i == 0)
        def _():
            dk_acc[...] = jnp.zeros_like(dk_acc)
            dv_acc[...] = jnp.zeros_like(dv_acc)
            cs_ref[...] = jnp.zeros_like(cs_ref)

        r = lax.broadcasted_iota(jnp.int32, (TILE, TILE), 0)
        c = lax.broadcasted_iota(jnp.int32, (TILE, TILE), 1)
        causal = c <= r
        qs = [_scaled_q(q_ref, _head_slices(hh)) for hh in range(2)]
        cqs = [cc_ref[:, HEAD_DIM * hh:HEAD_DIM * hh + 1] for hh in range(2)]
        lses = [lse_ref[:, HEAD_DIM * hh:HEAD_DIM * hh + 1] for hh in range(2)]
        dofs = [do_ref[:, _head_slices(hh)] for hh in range(2)]
        dos = [d_.astype(BF16) for d_ in dofs]
        deltas = [jnp.sum(dofs[hh] * o_ref[:, _head_slices(hh)], axis=1, keepdims=True) for hh in range(2)]

        def probs(j, slot, rowsums, diag):
            kblk = k_ref[_rows(k_ref, j), :]
            vblk = v_ref[_rows(v_ref, j), :]
            new = []
            for hh in range(2):
                sl = _head_slices(hh)
                sc = _fox_scores(qs[hh], kblk[:, sl], cqs[hh], cr_ref[j, hh:hh + 1, :], causal, diag)
                p = jnp.exp(sc - lses[hh])
                ds = p * (_dot_nt(dos[hh], vblk[:, sl]) - deltas[hh])
                p_s[slot, hh] = p.astype(BF16)
                ds_s[slot, hh] = ds.astype(BF16)
                cs_ref[j, hh:hh + 1, :] += jnp.sum(ds, axis=0, keepdims=True)
                new.append(rowsums[hh] + jnp.sum(ds, axis=1, keepdims=True))
            return tuple(new)

        def accumulate(j, slot, dqs):
            kblk = k_ref[_rows(k_ref, j), :]
            new = []
            for hh in range(2):
                dsb = ds_s[slot, hh]
                dv_acc[hh, _rows(None, j), :] += _dot_tn(p_s[slot, hh], dos[hh])
                dk_acc[hh, _rows(None, j), :] += _dot_tn(dsb, qs[hh])
                new.append(dqs[hh] + _dot(dsb, kblk[:, _head_slices(hh)]))
            return tuple(new)

        def step(t, both):
            rowsums, dqs = both
            dqs = accumulate(i - t + 1, (t - 1) % 2, dqs)
            return probs(i - t, t % 2, rowsums, False), dqs

        zero1 = jnp.zeros((TILE, 1), F32)
        zero64 = jnp.zeros((TILE, HEAD_DIM), F32)
        rowsums, dqs = lax.fori_loop(1, i + 1, step, (probs(i, 0, (zero1, zero1), True), (zero64, zero64)))
        dqs = accumulate(0, i % 2, dqs)
        for hh in range(2):
            cs_ref[i, hh:hh + 1, :] -= jnp.broadcast_to(rowsums[hh], (TILE, LANES)).T[0:1, :]
        dq_ref[...] = jnp.concatenate([dqs[0] * SCALE, dqs[1] * SCALE], axis=1).astype(BF16)

        @pl.when(i == nb - 1)
        def _():
            dk_ref[...] = jnp.concatenate([dk_acc[0], dk_acc[1]], axis=1).astype(BF16)
            dv_ref[...] = jnp.concatenate([dv_acc[0], dv_acc[1]], axis=1).astype(BF16)

    qspec = pl.BlockSpec((TILE, LANES), lambda p, i: (i, p))
    kvspec = pl.BlockSpec((s, LANES), lambda p, i: (0, p))
    o3 = jax.ShapeDtypeStruct((s, FOX_WIDTH), BF16)
    return pl.pallas_call(
        body, name=name, grid=(4, nb),
        in_specs=[pl.BlockSpec((TILE, LANES), lambda p, i: (i, cb + p)),
                  pl.BlockSpec((s, LANES), lambda p, i: (0, cb + 4 + p)),
                  pl.BlockSpec((s, LANES), lambda p, i: (0, cb + 8 + p)),
                  pl.BlockSpec((None, TILE, LANES), lambda p, i: (p, i, 0)),
                  pl.BlockSpec((None, nb, 8, TILE), lambda p, i: (p, 0, 0, 0)),
                  qspec,
                  pl.BlockSpec((None, TILE, LANES), lambda p, i: (p, i, 0)),
                  pl.BlockSpec((TILE, LANES), lambda p, i: (i, db + p))],
        out_specs=[qspec, kvspec, kvspec, pl.BlockSpec((None, nb, 8, TILE), lambda p, i: (p, 0, 0, 0))],
        out_shape=[o3, o3, o3, jax.ShapeDtypeStruct((4, nb, 8, TILE), F32)],
        scratch_shapes=[pltpu.VMEM((2, s, HEAD_DIM), F32), pltpu.VMEM((2, s, HEAD_DIM), F32),
                        pltpu.VMEM((2, 2, TILE, TILE), BF16), pltpu.VMEM((2, 2, TILE, TILE), BF16)],
        compiler_params=_params("arbitrary", "arbitrary"),
    )(pa, pa, pa, ccol4, crow4, out, lse, dout)


def _mem_fwd(name, pa, mkv):
    s = pa.shape[0]
    ml = mkv.shape[0]
    nb = s // TILE
    cb = QKV_WIDTH // LANES

    def body(q_ref, k_ref, v_ref, o_ref, lse_ref):
        outs, lses = [], []
        for hh in range(2):
            sl = _head_slices(hh)
            sc = _dot_nt(_scaled_q(q_ref, sl), k_ref[:, sl])
            m = jnp.max(sc, axis=1, keepdims=True)
            p = jnp.exp(sc - m)
            l = jnp.sum(p, axis=1, keepdims=True)
            outs.append(_dot(p.astype(BF16), v_ref[:, sl]) / l)
            lses.append(jnp.broadcast_to(m + jnp.log(l), (TILE, HEAD_DIM)))
        o_ref[...] = jnp.concatenate(outs, axis=1)
        lse_ref[...] = jnp.concatenate(lses, axis=1)

    return pl.pallas_call(
        body, name=name, grid=(2, nb),
        in_specs=[pl.BlockSpec((TILE, LANES), lambda p, i: (i, cb + p)),
                  pl.BlockSpec((ml, LANES), lambda p, i: (0, p)),
                  pl.BlockSpec((ml, LANES), lambda p, i: (0, 2 + p))],
        out_specs=[pl.BlockSpec((TILE, LANES), lambda p, i: (i, p)),
                   pl.BlockSpec((None, TILE, LANES), lambda p, i: (p, i, 0))],
        out_shape=[jax.ShapeDtypeStruct((s, MEM_WIDTH), F32), jax.ShapeDtypeStruct((2, s, LANES), F32)],
        compiler_params=_params("parallel", "parallel"),
    )(pa, mkv, mkv)


def _mem_bwd(name, pa, mkv, out, lse, dout, dcol0):
    s = pa.shape[0]
    ml = mkv.shape[0]
    nb = s // TILE
    cb = QKV_WIDTH // LANES
    db = dcol0 // LANES

    def body(q_ref, k_ref, v_ref, o_ref, lse_ref, do_ref, dq_ref, dk_ref, dv_ref, dk_acc, dv_acc):
        i = pl.program_id(1)

        @pl.when(i == 0)
        def _():
            dk_acc[...] = jnp.zeros_like(dk_acc)
            dv_acc[...] = jnp.zeros_like(dv_acc)

        dqs = []
        for hh in range(2):
            sl = _head_slices(hh)
            q = _scaled_q(q_ref, sl)
            kh = k_ref[:, sl]
            dof = do_ref[:, sl]
            do = dof.astype(BF16)
            delta = jnp.sum(dof * o_ref[:, sl], axis=1, keepdims=True)
            p = jnp.exp(_dot_nt(q, kh) - lse_ref[:, HEAD_DIM * hh:HEAD_DIM * hh + 1])
            ds = (p * (_dot_nt(do, v_ref[:, sl]) - delta)).astype(BF16)
            dv_acc[hh] += _dot_tn(p.astype(BF16), do)
            dk_acc[hh] += _dot_tn(ds, q)
            dqs.append(_dot(ds, kh) * SCALE)
        dq_ref[...] = jnp.concatenate(dqs, axis=1).astype(BF16)

        @pl.when(i == nb - 1)
        def _():
            dk_ref[...] = jnp.concatenate([dk_acc[0], dk_acc[1]], axis=1).astype(BF16)
            dv_ref[...] = jnp.concatenate([dv_acc[0], dv_acc[1]], axis=1).astype(BF16)

    qspec = pl.BlockSpec((TILE, LANES), lambda p, i: (i, p))
    kvspec = pl.BlockSpec((ml, LANES), lambda p, i: (0, p))
    okv = jax.ShapeDtypeStruct((ml, MEM_WIDTH), BF16)
    return pl.pallas_call(
        body, name=name, grid=(2, nb),
        in_specs=[pl.BlockSpec((TILE, LANES), lambda p, i: (i, cb + p)),
                  pl.BlockSpec((ml, LANES), lambda p, i: (0, p)),
                  pl.BlockSpec((ml, LANES), lambda p, i: (0, 2 + p)),
                  qspec,
                  pl.BlockSpec((None, TILE, LANES), lambda p, i: (p, i, 0)),
                  pl.BlockSpec((TILE, LANES), lambda p, i: (i, db + p))],
        out_specs=[qspec, kvspec, kvspec],
        out_shape=[jax.ShapeDtypeStruct((s, MEM_WIDTH), BF16), okv, okv],
        scratch_shapes=[pltpu.VMEM((2, ml, HEAD_DIM), F32), pltpu.VMEM((2, ml, HEAD_DIM), F32)],
        compiler_params=_params("arbitrary", "arbitrary"),
    )(pa, mkv, mkv, out, lse, dout)


def _head_maps():
    col = jnp.arange(MIX_WIDTH)[:, None] // HEAD_DIM
    g = (col == jnp.arange(LANES)[None, :]).astype(BF16)
    return g, g.T


def _normed_heads(osb_ref, ofx_ref, om_ref, g_ref, gt_ref):
    y = jnp.concatenate([osb_ref[...], ofx_ref[...], om_ref[...]], axis=1)
    msq = _sum_l3(y * y, g_ref[...]) * (1.0 / HEAD_DIM)
    rf = _sum_l3(lax.rsqrt(msq + EPS), gt_ref[...])
    return y * rf, rf


def _out_fwd(name, o_sb, o_fx, o_m, pb, ow, x, w_out, ts):
    s, d = x.shape
    g, gt = _head_maps()

    def body(osb_ref, ofx_ref, om_ref, gate_ref, ow_ref, x_ref, w_ref, g_ref, gt_ref, xo_ref, y2_ref):
        yh, _ = _normed_heads(osb_ref, ofx_ref, om_ref, g_ref, gt_ref)
        gate = gate_ref[...]
        y2 = (yh * ow_ref[...] * (gate * jax.nn.sigmoid(gate))).astype(BF16)
        y2_ref[...] = y2
        xo_ref[...] = x_ref[...] + _dot(y2, w_ref[...])

    return pl.pallas_call(
        body, name=name, grid=(s // ts,),
        in_specs=[_row_spec(ts, SB_WIDTH), _row_spec(ts, FOX_WIDTH), _row_spec(ts, MEM_WIDTH),
                  _row_spec(ts, MIX_WIDTH), _const_spec((1, MIX_WIDTH)), _row_spec(ts, d),
                  _const_spec((MIX_WIDTH, d)),
                  _const_spec((MIX_WIDTH, LANES)), _const_spec((LANES, MIX_WIDTH))],
        out_specs=[_row_spec(ts, d), _row_spec(ts, MIX_WIDTH)],
        out_shape=[jax.ShapeDtypeStruct((s, d), F32), jax.ShapeDtypeStruct((s, MIX_WIDTH), BF16)],
        compiler_params=_params("parallel"),
    )(o_sb, o_fx, o_m, pb, ow, x, w_out, g, gt)


def _row_spec(ts, w):
    return pl.BlockSpec((ts, w), lambda i: (i, 0))


def _const_spec(shape):
    return pl.BlockSpec(shape, lambda i: (0,) * len(shape))


def _out_bwd(name, dxb, o_sb, o_fx, o_m, pb, ow, w_out, ts):
    s, d = dxb.shape
    g, gt = _head_maps()

    def body(dx_ref, osb_ref, ofx_ref, om_ref, gate_ref, ow_ref, w_ref, g_ref, gt_ref, dy_ref, dgate_ref, dow_ref):
        @pl.when(pl.program_id(0) == 0)
        def _():
            dow_ref[...] = jnp.zeros_like(dow_ref)

        dy2 = _dot_nt(dx_ref[...], w_ref[...])
        yh, rf = _normed_heads(osb_ref, ofx_ref, om_ref, g_ref, gt_ref)
        gate = gate_ref[...]
        sig = jax.nn.sigmoid(gate)
        ow_v = ow_ref[...]
        dgate_ref[...] = (dy2 * (yh * ow_v) * (sig * (1.0 + gate * (1.0 - sig)))).astype(BF16)
        dn = dy2 * (gate * sig)
        dow_ref[...] += jnp.sum(dn * yh, axis=0, keepdims=True)
        dyh = dn * ow_v
        t = _sum_l3(dyh * yh, g_ref[...]) * (1.0 / HEAD_DIM)
        dy_ref[...] = rf * (dyh - yh * _sum_l3(t, gt_ref[...]))

    return pl.pallas_call(
        body, name=name, grid=(s // ts,),
        in_specs=[_row_spec(ts, d), _row_spec(ts, SB_WIDTH), _row_spec(ts, FOX_WIDTH), _row_spec(ts, MEM_WIDTH),
                  _row_spec(ts, MIX_WIDTH), _const_spec((1, MIX_WIDTH)),
                  _const_spec((MIX_WIDTH, d)),
                  _const_spec((MIX_WIDTH, LANES)), _const_spec((LANES, MIX_WIDTH))],
        out_specs=[_row_spec(ts, MIX_WIDTH), _row_spec(ts, MIX_WIDTH), _const_spec((1, MIX_WIDTH))],
        out_shape=[jax.ShapeDtypeStruct((s, MIX_WIDTH), F32), jax.ShapeDtypeStruct((s, MIX_WIDTH), BF16),
                   jax.ShapeDtypeStruct((1, MIX_WIDTH), F32)],
        compiler_params=_params("arbitrary"),
    )(dxb, o_sb, o_fx, o_m, pb, ow, w_out, g, gt)


def _adamw(name, w, g, m, v, tr):
    def body(w_ref, g_ref, m_ref, v_ref, d_ref, m2_ref, v2_ref):
        gv = g_ref[...]
        m2 = ADAM_B1 * m_ref[...] + (1.0 - ADAM_B1) * gv
        v2 = ADAM_B2 * v_ref[...] + (1.0 - ADAM_B2) * (gv * gv)
        m_hat = m2 / (1.0 - ADAM_B1 ** ADAM_STEP)
        v_hat = v2 / (1.0 - ADAM_B2 ** ADAM_STEP)
        d_ref[...] = -ADAM_LR * (m_hat / (jnp.sqrt(v_hat) + ADAM_EPS) + ADAM_WD * w_ref[...])
        m2_ref[...] = m2
        v2_ref[...] = v2

    rest = w.shape[1:]
    spec = pl.BlockSpec((tr,) + rest, lambda i: (i,) + (0,) * len(rest))
    shp = jax.ShapeDtypeStruct(w.shape, F32)
    return pl.pallas_call(
        body, name=name, grid=(w.shape[0] // tr,), in_specs=[spec] * 4, out_specs=[spec] * 3, out_shape=[shp] * 3,
        compiler_params=_params("parallel"),
    )(w, g, m, v)


def _adamw_sharded(name, w, m, v, g_own, g_other, cvec, tr):
    depth, rows, cols = w.shape
    nt = rows // 2 // tr

    def body(c_ref, w_ref, m_ref, v_ref, *rest):
        g_refs, (g_ref, d_ref, m2_ref, v2_ref) = rest[:2 * depth], rest[2 * depth:]
        layer, mine = pl.program_id(0), pl.program_id(1) == c_ref[0]
        gv = None
        for lt in range(depth):
            cand = jnp.where(mine, g_refs[lt][...], g_refs[depth + lt][...])
            gv = cand if gv is None else jnp.where(layer == lt, cand, gv)
        m2 = ADAM_B1 * m_ref[...] + (1.0 - ADAM_B1) * gv
        v2 = ADAM_B2 * v_ref[...] + (1.0 - ADAM_B2) * (gv * gv)
        m_hat = m2 / (1.0 - ADAM_B1 ** ADAM_STEP)
        v_hat = v2 / (1.0 - ADAM_B2 ** ADAM_STEP)
        g_ref[...] = gv
        d_ref[...] = -ADAM_LR * (m_hat / (jnp.sqrt(v_hat) + ADAM_EPS) + ADAM_WD * w_ref[...])
        m2_ref[...] = m2
        v2_ref[...] = v2

    def g_map(lt, own):
        def index(l, hf, i, c_ref):
            use = jnp.logical_and(l == lt, (hf == c_ref[0]) == own)
            return jnp.where(use, i, 0), 0
        return index

    full = pl.BlockSpec((None, tr, cols), lambda l, hf, i, c_ref: (l, hf * nt + i, 0))
    g_specs = [pl.BlockSpec((tr, cols), g_map(lt, own)) for own in (True, False) for lt in range(depth)]
    shp = jax.ShapeDtypeStruct((depth, rows, cols), F32)
    return pl.pallas_call(
        body, name=name,
        grid_spec=pltpu.PrefetchScalarGridSpec(
            num_scalar_prefetch=1, grid=(depth, 2, nt), in_specs=[full] * 3 + g_specs, out_specs=[full] * 4),
        out_shape=[shp] * 4,
        compiler_params=_params("arbitrary", "arbitrary", "arbitrary"),
    )(cvec, w, m, v, *g_own, *g_other)


HBM_SPEC = pl.BlockSpec(memory_space=pltpu.HBM)


def _place():
    x, y, c = lax.axis_index("x"), lax.axis_index("y"), lax.axis_index("c")
    chips = [(1 - x, y), (x, 1 - y), (1 - x, 1 - y)]
    return x, y, c, chips


def _remote(src, dst, send_sems, recv_sems, k, to):
    return pltpu.make_async_remote_copy(src_ref=src, dst_ref=dst, send_sem=send_sems.at[k], recv_sem=recv_sems.at[k],
                                        device_id=to, device_id_type=MESH)


def _half_rows(n_rows, cc):
    rh = n_rows // 2
    return pl.ds(pl.multiple_of(cc * rh, 16), rh)


def _dma_sems(n):
    return [pltpu.SemaphoreType.DMA((n,)), pltpu.SemaphoreType.DMA((n,))]


class _Exchange:
    def __init__(self, inputs, out_shapes, n_sems, begin, relay, finish):
        self.inputs, self.out_shapes, self.n_sems = list(inputs), list(out_shapes), n_sems
        self.begin, self.relay, self.finish = begin, relay, finish

    @property
    def n(self):
        return len(self.inputs)

    def split(self, refs):
        return refs[:self.n], refs[self.n:2 * self.n], refs[2 * self.n], refs[2 * self.n + 1]


def _run_exchange(name, ex):
    def body(*refs):
        parts = ex.split(refs)
        for phase in (ex.begin, ex.relay, ex.finish):
            if phase is not None:
                phase(*parts)

    return pl.pallas_call(
        body, name=name, in_specs=[HBM_SPEC] * ex.n, out_specs=[HBM_SPEC] * ex.n, out_shape=ex.out_shapes,
        scratch_shapes=_dma_sems(ex.n_sems),
    )(*ex.inputs)


def _gather_exchange(shards):
    def ici(in_refs, out_refs, send_sems, recv_sems):
        x, y, c, chips = _place()
        return [_remote(in_ref.at[_half_rows(in_ref.shape[0], c)], out_ref.at[2 * x + y, _half_rows(in_ref.shape[0], c)],
                        send_sems, recv_sems, 6 * a + j, (cx, cy, c))
                for a, (in_ref, out_ref) in enumerate(zip(in_refs, out_refs)) for j, (cx, cy) in enumerate(chips)]

    def d2d(out_refs, send_sems, recv_sems, half_of):
        x, y, c, chips = _place()
        cps = []
        for a, out_ref in enumerate(out_refs):
            for j, (cx, cy) in enumerate(chips):
                piece = out_ref.at[2 * cx + cy, _half_rows(out_ref.shape[1], half_of(c))]
                cps.append(_remote(piece, piece, send_sems, recv_sems, 6 * a + 3 + j, (x, y, 1 - c)))
        return cps

    def begin(in_refs, out_refs, send_sems, recv_sems):
        for cp in ici(in_refs, out_refs, send_sems, recv_sems):
            cp.start()

    def relay(in_refs, out_refs, send_sems, recv_sems):
        x, y, c, chips = _place()
        for a, out_ref in enumerate(out_refs):
            for j, (cx, cy) in enumerate(chips):
                landed = out_ref.at[2 * cx + cy, _half_rows(out_ref.shape[1], c)]
                _remote(landed, landed, send_sems, recv_sems, 6 * a + j, (cx, cy, c)).wait_recv()
        for cp in d2d(out_refs, send_sems, recv_sems, lambda c_: c_):
            cp.start()

    def finish(in_refs, out_refs, send_sems, recv_sems):
        for cp in d2d(out_refs, send_sems, recv_sems, lambda c_: 1 - c_):
            cp.wait_recv()
        for cp in ici(in_refs, out_refs, send_sems, recv_sems) + d2d(out_refs, send_sems, recv_sems, lambda c_: c_):
            cp.wait_send()

    shapes = [jax.ShapeDtypeStruct((N_CHIPS,) + s_.shape, s_.dtype) for s_ in shards]
    return _Exchange(shards, shapes, 6 * len(shards), begin, relay, finish)


def _swap_halves(name, g4s):
    n = len(g4s)

    def body(*refs):
        in_refs, out_refs, (send_sems, recv_sems) = refs[:n], refs[n:2 * n], refs[2 * n:]
        x, y, c, _ = _place()
        cps = [_remote(in_ref.at[:, _half_rows(in_ref.shape[1], 1 - c), :], out_ref, send_sems, recv_sems, a, (x, y, 1 - c))
               for a, (in_ref, out_ref) in enumerate(zip(in_refs, out_refs))]
        for cp in cps:
            cp.start()
        for cp in cps:
            cp.wait()

    return pl.pallas_call(
        body, name=name, in_specs=[HBM_SPEC] * n, out_specs=[HBM_SPEC] * n,
        out_shape=[jax.ShapeDtypeStruct((g.shape[0], g.shape[1] // 2, g.shape[2]), g.dtype) for g in g4s],
        scratch_shapes=_dma_sems(n),
    )(*g4s)


def _add_half(name, g4, r1, cvec, tr):
    n, r, w = g4.shape
    rh = r // 2
    nblk = rh // tr

    def body(c_ref, a_ref, b_ref, o_ref):
        o_ref[...] = (a_ref[...] + b_ref[...]).astype(BF16)

    return pl.pallas_call(
        body, name=name,
        grid_spec=pltpu.PrefetchScalarGridSpec(
            num_scalar_prefetch=1, grid=(n, nblk),
            in_specs=[pl.BlockSpec((None, tr, w), lambda k, i, c_ref: (k, c_ref[0] * nblk + i, 0)),
                      pl.BlockSpec((None, tr, w), lambda k, i, c_ref: (k, i, 0))],
            out_specs=pl.BlockSpec((None, tr, w), lambda k, i, c_ref: (k, i, 0))),
        out_shape=jax.ShapeDtypeStruct((n, rh, w), BF16),
        compiler_params=_params("parallel", "parallel"),
    )(cvec, g4, r1)


def _scatter_exchange(h4s):
    def sends(in_refs, out_refs, send_sems, recv_sems):
        x, y, c, chips = _place()
        return [_remote(in_ref.at[2 * cx + cy], out_ref.at[j], send_sems, recv_sems, 3 * a + j, (cx, cy, c))
                for a, (in_ref, out_ref) in enumerate(zip(in_refs, out_refs)) for j, (cx, cy) in enumerate(chips)]

    def begin(*parts):
        for cp in sends(*parts):
            cp.start()

    def finish(in_refs, out_refs, send_sems, recv_sems):
        x, y, c, chips = _place()
        for a, out_ref in enumerate(out_refs):
            for j, (cx, cy) in enumerate(chips):
                got = out_ref.at[j]
                _remote(got, got, send_sems, recv_sems, 3 * a + j, (cx, cy, c)).wait_recv()
        for cp in sends(in_refs, out_refs, send_sems, recv_sems):
            cp.wait_send()

    shapes = [jax.ShapeDtypeStruct((3,) + h.shape[1:], h.dtype) for h in h4s]
    return _Exchange(h4s, shapes, 3 * len(h4s), begin, None, finish)


def _sum_chips(name, h4, r3, mvec, tr):
    _, rh, w = h4.shape

    def body(m_ref, a_ref, b_ref, c_ref, d_ref, o_ref):
        o_ref[...] = ((a_ref[...].astype(F32) + b_ref[...].astype(F32)) + c_ref[...].astype(F32)) + d_ref[...].astype(F32)

    specs = [pl.BlockSpec((None, tr, w), lambda i, m_ref: (m_ref[0], i, 0))]
    specs += [pl.BlockSpec((None, tr, w), functools.partial(lambda k, i, m_ref: (k, i, 0), k)) for k in range(3)]
    return pl.pallas_call(
        body, name=name,
        grid_spec=pltpu.PrefetchScalarGridSpec(
            num_scalar_prefetch=1, grid=(rh // tr,), in_specs=specs,
            out_specs=pl.BlockSpec((tr, w), lambda i, m_ref: (i, 0))),
        out_shape=jax.ShapeDtypeStruct((rh, w), F32),
        compiler_params=_params("parallel"),
    )(mvec, h4, r3, r3, r3)


def _swap_reduced(name, ghs):
    n = len(ghs)

    def body(*refs):
        in_refs, out_refs, (send_sems, recv_sems) = refs[:n], refs[n:2 * n], refs[2 * n:]
        x, y, c, _ = _place()
        cps = [_remote(in_ref, out_ref, send_sems, recv_sems, a, (x, y, 1 - c))
               for a, (in_ref, out_ref) in enumerate(zip(in_refs, out_refs))]
        for cp in cps:
            cp.start()
        for cp in cps:
            cp.wait()

    return pl.pallas_call(
        body, name=name, in_specs=[HBM_SPEC] * n, out_specs=[HBM_SPEC] * n,
        out_shape=[jax.ShapeDtypeStruct(g.shape, g.dtype) for g in ghs],
        scratch_shapes=_dma_sems(n),
    )(*ghs)


def _small_update(name, partials, weights, moments1, moments2):
    n = len(partials)
    width = max(p.shape[1] for p in partials)
    starts, at = [], 0
    for p in partials:
        starts.append(at)
        at += p.shape[0]
    rows = -(-at // 8) * 8
    has_w = [w is not None for w in weights]
    n_w = sum(has_w)

    def body(*refs):
        p_refs = refs[:n]
        w_refs, m_refs, v_refs = refs[n:n + n_w], refs[n + n_w:n + 2 * n_w], refs[n + 2 * n_w:n + 3 * n_w]
        outs = refs[n + 3 * n_w:-4]
        g_refs, upd_refs = outs[:n], outs[n:]
        vec, buf, send_sems, recv_sems = refs[-4:]
        x, y, c, _ = _place()
        me = 4 * x + 2 * y + c
        vec[...] = jnp.zeros_like(vec)
        for p_ref, r0 in zip(p_refs, starts):
            vec[r0:r0 + p_ref.shape[0], 0:p_ref.shape[1]] = p_ref[...]
        buf[me] = vec[...]
        flips = [(fx, fy, fc) for fx in (0, 1) for fy in (0, 1) for fc in (0, 1)][1:]
        peers = [(x + fx - 2 * x * fx, y + fy - 2 * y * fy, c + fc - 2 * c * fc) for fx, fy, fc in flips]
        sends = [_remote(vec, buf.at[me], send_sems, recv_sems, k, peer) for k, peer in enumerate(peers)]
        for cp in sends:
            cp.start()
        for k, (px, py, pc) in enumerate(peers):
            got = buf.at[4 * px + 2 * py + pc]
            _remote(got, got, send_sems, recv_sems, k, (px, py, pc)).wait_recv()
        for cp in sends:
            cp.wait_send()
        total = buf[0]
        for dev in range(1, N_DEV):
            total = total + buf[dev]
        k = 0
        for a in range(n):
            r, w = g_refs[a].shape
            g = total[starts[a]:starts[a] + r, 0:w]
            g_refs[a][...] = g
            if has_w[a]:
                m2 = ADAM_B1 * m_refs[k][...] + (1.0 - ADAM_B1) * g
                v2 = ADAM_B2 * v_refs[k][...] + (1.0 - ADAM_B2) * (g * g)
                m_hat = m2 / (1.0 - ADAM_B1 ** ADAM_STEP)
                v_hat = v2 / (1.0 - ADAM_B2 ** ADAM_STEP)
                upd_refs[3 * k][...] = -ADAM_LR * (m_hat / (jnp.sqrt(v_hat) + ADAM_EPS) + ADAM_WD * w_refs[k][...])
                upd_refs[3 * k + 1][...] = m2
                upd_refs[3 * k + 2][...] = v2
                k += 1

    ws = [w for w in weights if w is not None]
    g_shapes = [jax.ShapeDtypeStruct(p.shape if w is None else w.shape, F32) for p, w in zip(partials, weights)]
    u_shapes = [jax.ShapeDtypeStruct(w.shape, F32) for w in ws for _ in range(3)]
    vm = pl.BlockSpec(memory_space=pltpu.VMEM)
    n_args = n + 3 * n_w
    outs = pl.pallas_call(
        body, name=name, in_specs=[vm] * n_args, out_specs=[vm] * (n + 3 * n_w), out_shape=g_shapes + u_shapes,
        scratch_shapes=[pltpu.VMEM((rows, width), F32), pltpu.VMEM((N_DEV, rows, width), F32),
                        pltpu.SemaphoreType.DMA((7,)), pltpu.SemaphoreType.DMA((7,))],
    )(*partials, *ws, *[m for m in moments1 if m is not None], *[v for v in moments2 if v is not None])
    return outs[:n], outs[n:]


GATE_COL = 3 * SB_WIDTH + 3 * FOX_WIDTH + FOX_HEADS + MEM_WIDTH
FL_COL = QKV_WIDTH


GROUP_A_COLS = [(0, QKV_WIDTH), (FL_COL + FOX_HEADS, MEM_WIDTH)]
GROUP_B_COLS = [(GATE_COL, MIX_WIDTH), (FL_COL, FOX_HEADS)]


def _group_from_shards(shard_of, cw, spans, pad):
    parts = []
    for lo, width in spans:
        hi = lo + width
        for j in range(N_CHIPS):
            a, b = max(lo, j * cw), min(hi, (j + 1) * cw)
            if a < b:
                parts.append(shard_of(j)[:, a - j * cw:b - j * cw])
    if pad:
        parts.append(jnp.zeros((parts[0].shape[0], pad), parts[0].dtype))
    return jnp.concatenate(parts, axis=1)


def _shard_from_groups(ga, gb, j, cw):
    lo, hi = j * cw, (j + 1) * cw
    placed = []
    for grp, spans in ((ga, GROUP_A_COLS), (gb, GROUP_B_COLS)):
        at = 0
        for first, width in spans:
            a, b = max(lo, first), min(hi, first + width)
            if a < b:
                placed.append((a, grp[:, at + a - first:at + b - first]))
            at += width
    return jnp.concatenate([p for _, p in sorted(placed, key=lambda t: t[0])], axis=1)


def _tile_of(n, cap, unit):
    if n <= cap:
        return n
    best = None
    for t in range(unit, cap + 1, unit):
        if n % t == 0:
            best = t
    assert best is not None, (n, cap, unit)
    return best


def _column_major_rows(a):
    dp, r, c = a.shape
    return a.transpose(2, 0, 1).reshape(c, dp, r // LANES, LANES).transpose(0, 2, 1, 3).reshape(-1, 8, LANES)


def _from_column_major_rows(b, shape):
    dp, r, c = shape
    return b.reshape(c, r // LANES, dp, LANES).transpose(0, 2, 1, 3).reshape(c, dp, r).transpose(1, 2, 0)


def _pack_small(parts):
    rows = []
    for p in parts:
        f = p.reshape(-1).astype(F32)
        f = jnp.pad(f, (0, (-f.shape[0]) % LANES))
        rows.append(f.reshape(-1, LANES))
    out = jnp.concatenate(rows, axis=0)
    return jnp.pad(out, ((0, (-out.shape[0]) % 8), (0, 0)))


def _unpack_small(packed, shapes):
    outs, r = [], 0
    for shp in shapes:
        n = 1
        for s_ in shp:
            n *= s_
        nr = -(-n // LANES)
        outs.append(packed[r:r + nr].reshape(-1)[:n].reshape(shp))
        r += nr
    return outs


def kernel(x, mem, norm_w, w_in, b_forget, mem_norm_w, w_mem_kv, out_norm_w, w_out, final_norm_w, loss_target, m_norm_w, m_w_in, m_b_forget, m_mem_norm_w, m_w_mem_kv, m_out_norm_w, m_w_out, m_final_norm_w, v_norm_w, v_w_in, v_b_forget, v_mem_norm_w, v_w_mem_kv, v_out_norm_w, v_w_out, v_final_norm_w):
    xs = x[0]
    mems = mem[0]
    target = loss_target[0]
    s, d = xs.shape
    depth = norm_w.shape[0]
    nb = s // TILE
    ts = _tile_of(s, 256, 8)
    big = (w_in, w_mem_kv, w_out)
    core = lax.axis_index("c")
    chip = 2 * lax.axis_index("x") + lax.axis_index("y")
    cvec = core.astype(jnp.int32).reshape(1)
    mvec = chip.astype(jnp.int32).reshape(1)
    cw = w_in.shape[2]

    own_w = [[a[l].astype(BF16) for a in big] for l in range(depth)]

    def lay_out(own, got):
        full = [jnp.where(lax.broadcasted_iota(jnp.int32, g.shape, 0) == chip, o[None], g) for g, o in zip(got, own)]
        shard_of = lambda j: full[0][j]
        wa_l = _group_from_shards(shard_of, cw, GROUP_A_COLS, 0)
        wb_l = _group_from_shards(shard_of, cw, GROUP_B_COLS, LANES - FOX_HEADS)
        return wa_l, wb_l, full[1].reshape(-1, full[1].shape[2]), full[2].reshape(-1, full[2].shape[2])

    layer_w = [lay_out(own_w[0], _run_exchange("gather_weights0", _gather_exchange(own_w[0])))]

    tm = _tile_of(s, 256, 8)
    fl_block = MIX_WIDTH // LANES

    saved = []
    cur = xs
    for l in range(depth):
        wa, wb, wkv, wout = layer_w[l]
        h = _rms_fwd(f"rms_fwd{l}", cur, norm_w[l][None], ts)
        pa = _mm(f"inproj_a{l}", h, wa, "nn", tm, _tile_of(PA, 1664, LANES), BF16)
        pb = _mm(f"inproj_b{l}", h, wb, "nn", tm, PB, F32)
        bpad = jnp.pad(b_forget[l], (0, LANES - FOX_HEADS))[None]
        ccol, crow = _gate_fwd(f"gate_fwd{l}", pb, bpad, fl_block)
        ccol4 = jnp.repeat(ccol[:, :FOX_HEADS].reshape(s, 4, 2).transpose(1, 0, 2), HEAD_DIM, axis=2)
        crow4 = jnp.pad(crow.reshape(nb, 4, 2, TILE).transpose(1, 0, 2, 3), ((0, 0), (0, 0), (0, 6), (0, 0)))
        next_gather = _gather_exchange(own_w[l + 1]) if l + 1 < depth else None
        o_sb, got = _sb_fwd(f"sb_fwd{l}", pa, 0, carried=next_gather)
        if next_gather is not None:
            layer_w.append(lay_out(own_w[l + 1], got))
        o_fx, lse_fx = _fox_fwd(f"fox_fwd{l}", pa, 3 * SB_WIDTH, ccol4, crow4)
        mn = _rms_fwd(f"mem_rms{l}", mems, mem_norm_w[l][None], mems.shape[0])
        mkv = _mm(f"mem_kv{l}", mn, wkv, "nn", mems.shape[0], 2 * MEM_WIDTH, BF16)
        o_m, lse_m = _mem_fwd(f"mem_fwd{l}", pa, mkv)
        nxt, y2 = _out_fwd(f"out_fwd{l}", o_sb, o_fx, o_m, pb, out_norm_w[l][None], cur, wout, ts)
        saved.append((cur, h, pa, pb, bpad, ccol4, crow4, o_sb, o_fx, lse_fx, mn, mkv, o_m, lse_m, y2))
        cur = nxt

    loss_v, dx, dxb, g_final = _final_loss("final_loss", cur, final_norm_w[None], target, ts)

    g_norm, g_b, g_memnorm, g_outnorm = [None] * depth, [None] * depth, [None] * depth, [None] * depth
    g_wa, g_wb, g_wkv, g_wout = [None] * depth, [None] * depth, [None] * depth, [None] * depth
    g_own = [[None] * depth for _ in big]
    g_other = [[None] * depth for _ in big]

    def reduce_at_owner(lr, chip_sums, from_chips, tiles):
        halves = [_sum_chips(f"grad_sum_chips{lr}_{k}", h_, r_, mvec, t_)
                  for k, (h_, r_, t_) in enumerate(zip(chip_sums, from_chips, tiles))]
        others = _swap_reduced(f"grad_swap_reduced{lr}", halves)
        for k in range(len(big)):
            g_own[k][lr], g_other[k][lr] = halves[k], others[k]

    pending = None
    for l in reversed(range(depth)):
        xin, h, pa, pb, bpad, ccol4, crow4, o_sb, o_fx, lse_fx, mn, mkv, o_m, lse_m, y2 = saved[l]
        wa, wb, wkv, wout = layer_w[l]
        dy, dgate, g_outnorm[l] = _out_bwd(f"out_bwd{l}", dxb, o_sb, o_fx, o_m, pb, out_norm_w[l][None], wout, ts)
        g_wout[l] = _mm(f"dw_out{l}", y2, dxb, "tn", _tile_of(MIX_WIDTH, 640, LANES), d, F32)
        scatter = _scatter_exchange(pending[1]) if pending is not None else None
        (dq_sb, dk_sb, dv_sb), from_chips = _sb_bwd(f"sb_bwd{l}", pa, 0, dy, 0, carried=scatter)
        if pending is not None:
            reduce_at_owner(pending[0], pending[1], from_chips, pending[2])
        dq_fx, dk_fx, dv_fx, cs4 = _fox_bwd(f"fox_bwd{l}", pa, 3 * SB_WIDTH, ccol4, crow4, o_fx, lse_fx, dy, SB_WIDTH)
        colsum = cs4[:, :, :2, :].transpose(1, 0, 2, 3).reshape(nb, 8, TILE)
        dlogit, g_b[l] = _gate_bwd(f"gate_bwd{l}", pb, bpad, colsum, fl_block)
        dq_m, dk_m, dv_m = _mem_bwd(f"mem_bwd{l}", pa, mkv, o_m, lse_m, dy, SB_WIDTH + FOX_WIDTH)
        dmkv = jnp.concatenate([dk_m, dv_m], axis=1)
        g_wkv[l] = _mm(f"dw_kv{l}", mn, dmkv, "tn", d, 2 * MEM_WIDTH, F32)
        dmn = _mm(f"dmem{l}", dmkv, wkv, "nt", mems.shape[0], d, F32)
        g_memnorm[l] = _rms_wgrad(f"mem_norm_grad{l}", mems, dmn)
        dpa = jnp.concatenate([dq_sb, dk_sb, dv_sb, dq_fx, dk_fx, dv_fx, dq_m], axis=1)
        dpb = jnp.concatenate([dgate, dlogit], axis=1)
        tw = _tile_of(d, 512, LANES)
        g_wa[l] = _mm(f"dw_in_a{l}", h, dpa, "tn", tw, _tile_of(PA, 1664, LANES), F32)
        g_wb[l] = _mm(f"dw_in_b{l}", h, dpb, "tn", tw, PB, F32)
        dh = _mm(f"dh_a{l}", dpa, wa, "nt", tm, d, F32)
        dh = _mm(f"dh_b{l}", dpb, wb, "nt", tm, d, F32, res=dh)
        dx, dxb, g_norm[l] = _rms_bwd(f"rms_bwd{l}", xin, norm_w[l][None], dh, dx, ts)
        g4s = [jnp.stack([_shard_from_groups(g_wa[l], g_wb[l], j, cw) for j in range(N_CHIPS)]),
               g_wkv[l].reshape(N_CHIPS, -1, g_wkv[l].shape[1]), g_wout[l].reshape(N_CHIPS, -1, d)]
        tiles = [_tile_of(g.shape[1] // 2, 256, 16) for g in g4s]
        from_sibling = _swap_halves(f"grad_swap_halves{l}", g4s)
        chip_sums = [_add_half(f"grad_add_half{l}_{k}", g, r, cvec, t)
                     for k, (g, r, t) in enumerate(zip(g4s, from_sibling, tiles))]
        pending = (l, chip_sums, tiles)
    reduce_at_owner(pending[0], pending[1], _run_exchange(f"grad_scatter_chips{pending[0]}", _scatter_exchange(pending[1])),
                    pending[2])

    small_w = [norm_w, b_forget, mem_norm_w, out_norm_w, final_norm_w]
    small_m = [m_norm_w, m_b_forget, m_mem_norm_w, m_out_norm_w, m_final_norm_w]
    small_v = [v_norm_w, v_b_forget, v_mem_norm_w, v_out_norm_w, v_final_norm_w]
    rows2 = lambda a: a.reshape(-1, a.shape[-1])
    partials = [jnp.concatenate(g_norm, axis=0), jnp.concatenate(g_b, axis=0), jnp.concatenate(g_memnorm, axis=0),
                jnp.concatenate(g_outnorm, axis=0), g_final, loss_v]
    sums, updates = _small_update("small_update", partials, [rows2(a) for a in small_w] + [None],
                                  [rows2(a) for a in small_m] + [None], [rows2(a) for a in small_v] + [None])
    small_grads = [g.reshape(a.shape) for g, a in zip(sums, small_w)]
    loss = sums[-1][0, 0]
    small_delta, small_m2, small_v2 = ([updates[3 * k + t].reshape(a.shape) for k, a in enumerate(small_w)]
                                       for t in range(3))
    big_grads, big_delta, big_m2, big_v2 = [], [], [], []
    for k, (nm, w_, m_, v_) in enumerate(zip(("w_in", "w_mem_kv", "w_out"), big, (m_w_in, m_w_mem_kv, m_w_out),
                                             (v_w_in, v_w_mem_kv, v_w_out))):
        if w_.shape[2] % LANES:
            g_full = jnp.stack([jnp.concatenate([jnp.where(core == 0, go, gt), jnp.where(core == 0, gt, go)], axis=0)
                                for go, gt in zip(g_own[k], g_other[k])])
            w_p, g_p, m_p, v_p = (_column_major_rows(a) for a in (w_, g_full, m_, v_))
            outs = _adamw(f"adamw_{nm}", w_p, g_p, m_p, v_p, _tile_of(w_p.shape[0], 600, 1))
            outs = [_from_column_major_rows(o, w_.shape) for o in (g_p, *outs)]
        else:
            outs = _adamw_sharded(f"adamw_{nm}", w_, m_, v_, g_own[k], g_other[k], cvec,
                                  _tile_of(w_.shape[1] // 2, 256, 8))
        for lst, o in zip((big_grads, big_delta, big_m2, big_v2), outs):
            lst.append(o)

    def order(sm, bg):
        return [sm[0], bg[0], sm[1], sm[2], bg[1], sm[3], bg[2], sm[4]]

    return (loss, dx[None], *order(small_grads, big_grads), *order(small_delta, big_delta),
            *order(small_m2, big_m2), *order(small_v2, big_v2))
```

```python
import functools

import jax
import jax.numpy as jnp
from jax import lax
from jax.experimental import pallas as pl
from jax.experimental.pallas import tpu as pltpu

F32 = jnp.float32
BF16 = jnp.bfloat16

HEAD_DIM = 64
SB_WIDTH = 512
FOX_WIDTH = 512
FOX_HEADS = 8
MEM_WIDTH = 256
MIX_WIDTH = SB_WIDTH + FOX_WIDTH + MEM_WIDTH
TOTAL_HEADS = MIX_WIDTH // HEAD_DIM
IN_WIDTH = 3 * SB_WIDTH + 3 * FOX_WIDTH + FOX_HEADS + MEM_WIDTH + MIX_WIDTH
LANES = 128
QKV_WIDTH = 3 * SB_WIDTH + 3 * FOX_WIDTH
PA = QKV_WIDTH + MEM_WIDTH
PB = LANES + MIX_WIDTH
EPS = 1e-6
SCALE = HEAD_DIM ** -0.5
TILE = 256
SB_GROUP = 4
SB_LANES = SB_GROUP * HEAD_DIM
NEG_INF = float("-inf")
MASKED = -1e30

ADAM_LR = 0.001
ADAM_B1 = 0.9
ADAM_B2 = 0.999
ADAM_EPS = 1e-08
ADAM_WD = 0.01
ADAM_STEP = 10

N_CHIPS = 4
N_DEV = 8
VMEM_LIMIT = 48 * 1024 * 1024
MESH = pl.DeviceIdType.MESH


def _params(*sem):
    return pltpu.CompilerParams(dimension_semantics=tuple(sem), vmem_limit_bytes=VMEM_LIMIT)


def _dot(a, b):
    return jnp.dot(a, b, preferred_element_type=F32)


def _dot_nt(a, b):
    return lax.dot_general(a, b, (((1,), (1,)), ((), ())), preferred_element_type=F32)


def _dot_tn(a, b):
    return lax.dot_general(a, b, (((0,), (0,)), ((), ())), preferred_element_type=F32)


def _split2(x):
    hi = x.astype(BF16)
    lo = (x - hi.astype(F32)).astype(BF16)
    return hi, lo


def _split3(x):
    hi = x.astype(BF16)
    r = x - hi.astype(F32)
    mid = r.astype(BF16)
    lo = (r - mid.astype(F32)).astype(BF16)
    return hi, mid, lo


def _sum_l2(x, u):
    hi, lo = _split2(x)
    return _dot(hi, u) + _dot(lo, u)


def _sum_l3(x, u):
    hi, mid, lo = _split3(x)
    return _dot(hi, u) + _dot(mid, u) + _dot(lo, u)


def _sum_r3(u, x):
    hi, mid, lo = _split3(x)
    return _dot(u, hi) + _dot(u, mid) + _dot(u, lo)


def _softplus(z):
    return jnp.maximum(z, 0.0) + jnp.log1p(jnp.exp(-jnp.abs(z)))


def _tri(n, pred):
    r = lax.broadcasted_iota(jnp.int32, (n, n), 0)
    c = lax.broadcasted_iota(jnp.int32, (n, n), 1)
    return jnp.where(pred(r, c), 1.0, 0.0).astype(BF16)


def _rows(ref, j, n=TILE):
    return pl.ds(pl.multiple_of(j * n, n), n)


def _mm(name, a, b, mode, tm, tn, out_dtype, res=None, a_lead=(), b_lead=()):
    a2, b2 = a.shape[len(a_lead):], b.shape[len(b_lead):]
    if mode == "tn":
        k, m = a2
    else:
        m, k = a2
    n = b2[0] if mode == "nt" else b2[1]
    assert m % tm == 0 and n % tn == 0, (name, m, tm, n, tn)
    na, nb = (None,) * len(a_lead), (None,) * len(b_lead)
    if mode == "tn":
        a_spec = pl.BlockSpec(na + (k, tm), lambda j, i: a_lead + (0, i))
    else:
        a_spec = pl.BlockSpec(na + (tm, k), lambda j, i: a_lead + (i, 0))
    if mode == "nt":
        b_spec = pl.BlockSpec(nb + (tn, k), lambda j, i: b_lead + (j, 0))
    else:
        b_spec = pl.BlockSpec(nb + (k, tn), lambda j, i: b_lead + (0, j))
    o_spec = pl.BlockSpec((tm, tn), lambda j, i: (i, j))
    dot = {"nn": _dot, "nt": _dot_nt, "tn": _dot_tn}[mode]

    def body(a_ref, b_ref, *rest):
        o_ref = rest[-1]
        acc = dot(a_ref[...].astype(BF16), b_ref[...].astype(BF16))
        if res is not None:
            acc = acc + rest[0][...]
        o_ref[...] = acc.astype(o_ref.dtype)

    args, specs = [a, b], [a_spec, b_spec]
    if res is not None:
        args.append(res)
        specs.append(o_spec)
    return pl.pallas_call(
        body, name=name, grid=(n // tn, m // tm), in_specs=specs, out_specs=o_spec,
        out_shape=jax.ShapeDtypeStruct((m, n), out_dtype),
        compiler_params=_params("parallel", "parallel"),
    )(*args)


def _rms_fwd(name, x, g, ts):
    s, d = x.shape

    def body(x_ref, g_ref, o_ref):
        xf = x_ref[...]
        r = lax.rsqrt(jnp.mean(xf * xf, axis=1, keepdims=True) + EPS)
        o_ref[...] = (xf * r * g_ref[...]).astype(BF16)

    return pl.pallas_call(
        body, name=name, grid=(s // ts,),
        in_specs=[pl.BlockSpec((ts, d), lambda i: (i, 0)), pl.BlockSpec((1, d), lambda i: (0, 0))],
        out_specs=pl.BlockSpec((ts, d), lambda i: (i, 0)),
        out_shape=jax.ShapeDtypeStruct((s, d), BF16),
        compiler_params=_params("parallel"),
    )(x, g)


def _rms_bwd(name, x, g, dh, dres, ts):
    s, d = x.shape

    def body(x_ref, g_ref, dh_ref, dres_ref, dx_ref, dxb_ref, dg_ref):
        @pl.when(pl.program_id(0) == 0)
        def _():
            dg_ref[...] = jnp.zeros_like(dg_ref)

        xf = x_ref[...]
        r = lax.rsqrt(jnp.mean(xf * xf, axis=1, keepdims=True) + EPS)
        xh = xf * r
        dhf = dh_ref[...]
        dg_ref[...] += jnp.sum(dhf * xh, axis=0, keepdims=True)
        dxh = dhf * g_ref[...]
        m = jnp.mean(dxh * xh, axis=1, keepdims=True)
        dx = r * (dxh - xh * m) + dres_ref[...]
        dx_ref[...] = dx
        dxb_ref[...] = dx.astype(BF16)

    row = pl.BlockSpec((ts, d), lambda i: (i, 0))
    vec = pl.BlockSpec((1, d), lambda i: (0, 0))
    return pl.pallas_call(
        body, name=name, grid=(s // ts,), in_specs=[row, vec, row, row], out_specs=[row, row, vec],
        out_shape=[jax.ShapeDtypeStruct((s, d), F32), jax.ShapeDtypeStruct((s, d), BF16),
                   jax.ShapeDtypeStruct((1, d), F32)],
        compiler_params=_params("arbitrary"),
    )(x, g, dh, dres)


def _rms_wgrad(name, x, dh):
    m_, d = x.shape

    def body(x_ref, dh_ref, dg_ref):
        xf = x_ref[...]
        r = lax.rsqrt(jnp.mean(xf * xf, axis=1, keepdims=True) + EPS)
        dg_ref[...] = jnp.sum(dh_ref[...] * xf * r, axis=0, keepdims=True)

    return pl.pallas_call(
        body, name=name, out_shape=jax.ShapeDtypeStruct((1, d), F32),
    )(x, dh)


def _final_loss(name, x, g, target, ts):
    s, d = x.shape

    def body(x_ref, g_ref, t_ref, loss_ref, dx_ref, dxb_ref, dg_ref):
        @pl.when(pl.program_id(0) == 0)
        def _():
            dg_ref[...] = jnp.zeros_like(dg_ref)
            loss_ref[...] = jnp.zeros_like(loss_ref)

        xf = x_ref[...]
        gw = g_ref[...]
        r = lax.rsqrt(jnp.mean(xf * xf, axis=1, keepdims=True) + EPS)
        xh = xf * r
        e = xh * gw - t_ref[...]
        part = 0.5 * jnp.sum(jnp.mean(e * e, axis=1, keepdims=True), axis=0, keepdims=True)
        loss_ref[...] += jnp.broadcast_to(part, loss_ref.shape)
        dy = e * (1.0 / d)
        dg_ref[...] += jnp.sum(dy * xh, axis=0, keepdims=True)
        dxh = dy * gw
        m = jnp.mean(dxh * xh, axis=1, keepdims=True)
        dx = r * (dxh - xh * m)
        dx_ref[...] = dx
        dxb_ref[...] = dx.astype(BF16)

    row = pl.BlockSpec((ts, d), lambda i: (i, 0))
    vec = pl.BlockSpec((1, d), lambda i: (0, 0))
    lvec = pl.BlockSpec((1, LANES), lambda i: (0, 0))
    return pl.pallas_call(
        body, name=name, grid=(s // ts,), in_specs=[row, vec, row], out_specs=[lvec, row, row, vec],
        out_shape=[jax.ShapeDtypeStruct((1, LANES), F32), jax.ShapeDtypeStruct((s, d), F32),
                   jax.ShapeDtypeStruct((s, d), BF16), jax.ShapeDtypeStruct((1, d), F32)],
        compiler_params=_params("arbitrary"),
    )(x, g, target)


def _gate_fwd(name, pb, bpad, fl_block):
    s = pb.shape[0]
    nb = s // TILE

    def body(fl_ref, b_ref, ccol_ref, crow_ref, carry):
        @pl.when(pl.program_id(0) == 0)
        def _():
            carry[...] = jnp.zeros_like(carry)

        u = fl_ref[...] + b_ref[...]
        lf = jnp.minimum(u, 0.0) - jnp.log1p(jnp.exp(-jnp.abs(u)))
        lower = _tri(TILE, lambda r, c: c <= r)
        c = _sum_r3(lower, lf) + carry[0:1, :]
        ccol_ref[...] = c
        crow_ref[0] = c.T[0:8, :]
        carry[...] = jnp.broadcast_to(c[TILE - 1:TILE, :], carry.shape)

    return pl.pallas_call(
        body, name=name, grid=(nb,),
        in_specs=[pl.BlockSpec((TILE, LANES), lambda i: (i, fl_block)), pl.BlockSpec((1, LANES), lambda i: (0, 0))],
        out_specs=[pl.BlockSpec((TILE, LANES), lambda i: (i, 0)), pl.BlockSpec((1, 8, TILE), lambda i: (i, 0, 0))],
        out_shape=[jax.ShapeDtypeStruct((s, LANES), F32), jax.ShapeDtypeStruct((nb, 8, TILE), F32)],
        scratch_shapes=[pltpu.VMEM((8, LANES), F32)],
        compiler_params=_params("arbitrary"),
    )(pb, bpad)


def _gate_bwd(name, pb, bpad, colsum, fl_block):
    s = pb.shape[0]
    nb = s // TILE

    def body(fl_ref, b_ref, cs_ref, dl_ref, db_ref, carry):
        @pl.when(pl.program_id(0) == 0)
        def _():
            carry[...] = jnp.zeros_like(carry)
            db_ref[...] = jnp.zeros_like(db_ref)

        upper = _tri(TILE, lambda r, c: r >= c)
        rsum = _sum_l3(cs_ref[0], upper) + carry[:, 0:1]
        carry[...] = jnp.broadcast_to(rsum[:, 0:1], carry.shape)
        full = jnp.concatenate([rsum, jnp.zeros((LANES - 8, TILE), F32)], axis=0)
        dlf = -full.T
        u = fl_ref[...] + b_ref[...]
        dlogit = dlf * (1.0 - jax.nn.sigmoid(u))
        dl_ref[...] = dlogit.astype(BF16)
        db_ref[...] += jnp.sum(dlogit, axis=0, keepdims=True)

    rev = lambda i: (nb - 1 - i, 0)
    return pl.pallas_call(
        body, name=name, grid=(nb,),
        in_specs=[pl.BlockSpec((TILE, LANES), lambda i: (nb - 1 - i, fl_block)),
                  pl.BlockSpec((1, LANES), lambda i: (0, 0)),
                  pl.BlockSpec((1, 8, TILE), lambda i: (nb - 1 - i, 0, 0))],
        out_specs=[pl.BlockSpec((TILE, LANES), rev), pl.BlockSpec((1, LANES), lambda i: (0, 0))],
        out_shape=[jax.ShapeDtypeStruct((s, LANES), BF16), jax.ShapeDtypeStruct((1, LANES), F32)],
        scratch_shapes=[pltpu.VMEM((8, LANES), F32)],
        compiler_params=_params("arbitrary"),
    )(pb, bpad, colsum)


def _head_slices(hh):
    return slice(HEAD_DIM * hh, HEAD_DIM * (hh + 1))


def _scaled_q(q_ref, sl, scale=SCALE):
    return (q_ref[:, sl].astype(F32) * scale).astype(BF16)


def _neg_abs(x):
    sign = jnp.uint32(0x80000000)
    return lax.bitcast_convert_type(lax.bitcast_convert_type(x, jnp.uint32) | sign, F32)


def _sb_tile(qn, kj, carry, strict, u_after, diag):
    nz = _dot_nt(qn, kj)
    lf = jnp.minimum(nz, 0.0) - jnp.log(1.0 + jnp.exp(_neg_abs(nz)))
    lsig = lf - nz
    if diag:
        lf = jnp.where(strict, lf, 0.0)
    sx = _dot(lf.astype(BF16), u_after)
    a = jnp.exp(lsig + sx + carry)
    if diag:
        a = jnp.where(strict, a, 0.0)
    return lsig, a, carry + sx[:, 0:1] + lf[:, 0:1]


def _pair_grid_call(name, body, nb, in_specs, out_specs, out_shape, scratch, args, carried=None, groups=4):
    if carried is None:
        return pl.pallas_call(
            body, name=name, grid=(groups, nb), in_specs=in_specs, out_specs=out_specs, out_shape=out_shape,
            scratch_shapes=scratch, compiler_params=_params("arbitrary", "arbitrary"),
        )(*args)
    n_in, n_out, n_ex = len(in_specs), len(out_specs), carried.n

    def body_with_copies(*refs):
        own_in, ex_in = refs[:n_in], refs[n_in:n_in + n_ex]
        own_out = refs[n_in + n_ex:n_in + n_ex + n_out]
        ex_out = refs[n_in + n_ex + n_out:n_in + 2 * n_ex + n_out]
        own_scratch, sems = refs[n_in + 2 * n_ex + n_out:-2], refs[-2:]
        parts = (ex_in, ex_out, sems[0], sems[1])
        p, i = pl.program_id(0), pl.program_id(1)
        pl.when(jnp.logical_and(p == 0, i == 0))(lambda: carried.begin(*parts))
        if carried.relay is not None:
            pl.when(jnp.logical_and(p == groups - 1, i == max(nb - 2, 0)))(lambda: carried.relay(*parts))
        body(*own_in, *own_out, *own_scratch)
        pl.when(jnp.logical_and(p == groups - 1, i == nb - 1))(lambda: carried.finish(*parts))

    return pl.pallas_call(
        body_with_copies, name=name, grid=(groups, nb), in_specs=list(in_specs) + [HBM_SPEC] * n_ex,
        out_specs=list(out_specs) + [HBM_SPEC] * n_ex, out_shape=list(out_shape) + carried.out_shapes,
        scratch_shapes=list(scratch) + _dma_sems(carried.n_sems),
        compiler_params=_params("arbitrary", "arbitrary"),
    )(*args, *carried.inputs)


def _sb_fwd(name, pa, col0, carried=None):
    s = pa.shape[0]
    nb = s // TILE
    cb = col0 // SB_LANES
    kb = SB_WIDTH // SB_LANES

    def body(q_ref, k_ref, v_ref, o_ref, lsig_s, lf_s):
        i = pl.program_id(1)
        r = lax.broadcasted_iota(jnp.int32, (TILE, TILE), 0)
        c = lax.broadcasted_iota(jnp.int32, (TILE, TILE), 1)
        strict = c < r
        u_after = _tri(TILE, lambda rr, cc: rr > cc)
        qs = [_scaled_q(q_ref, _head_slices(hh), -SCALE) for hh in range(SB_GROUP)]

        def neg_z(j):
            kblk = k_ref[_rows(k_ref, j), :]
            return [_dot_nt(qs[hh], kblk[:, _head_slices(hh)]) for hh in range(SB_GROUP)]

        def scores(nzs, slot, diag):
            for hh, nz in enumerate(nzs):
                lf = jnp.minimum(nz, 0.0) - jnp.log(1.0 + jnp.exp(_neg_abs(nz)))
                lsig = lf - nz
                if diag:
                    lf = jnp.where(strict, lf, 0.0)
                    lsig = jnp.where(strict, lsig, MASKED)
                lsig_s[slot, hh] = lsig
                lf_s[slot, hh] = lf.astype(BF16)

        def weigh(j, slot, state):
            vblk = v_ref[_rows(v_ref, j), :]
            new = []
            for hh in range(SB_GROUP):
                carry, acc = state[hh]
                lfb = lf_s[slot, hh]
                sx = _dot(lfb, u_after)
                a = jnp.exp(lsig_s[slot, hh] + sx + carry)
                new.append((carry + sx[:, 0:1] + lfb[:, 0:1].astype(F32),
                            acc + _dot(a.astype(BF16), vblk[:, _head_slices(hh)])))
            return tuple(new)

        def step(t, state):
            state = weigh(i - t + 1, (t - 1) % 2, state)
            scores(neg_z(i - t), t % 2, False)
            return state

        zero = (jnp.zeros((TILE, 1), F32), jnp.zeros((TILE, HEAD_DIM), F32))
        scores(neg_z(i), 0, True)
        state = lax.fori_loop(1, i + 1, step, (zero,) * SB_GROUP)
        state = weigh(0, i % 2, state)
        o_ref[...] = jnp.concatenate([st[1] for st in state], axis=1)

    outs = _pair_grid_call(
        name, body, nb,
        in_specs=[pl.BlockSpec((TILE, SB_LANES), lambda p, i: (i, cb + p)),
                  pl.BlockSpec((s, SB_LANES), lambda p, i: (0, cb + kb + p)),
                  pl.BlockSpec((s, SB_LANES), lambda p, i: (0, cb + 2 * kb + p))],
        out_specs=[pl.BlockSpec((TILE, SB_LANES), lambda p, i: (i, p))],
        out_shape=[jax.ShapeDtypeStruct((s, SB_WIDTH), F32)],
        scratch=[pltpu.VMEM((2, SB_GROUP, TILE, TILE), F32), pltpu.VMEM((2, SB_GROUP, TILE, TILE), BF16)],
        args=(pa, pa, pa), carried=carried, groups=kb)
    return outs[0], outs[1:]


def _sb_bwd(name, pa, col0, dout, dcol0, carried=None):
    s = pa.shape[0]
    nb = s // TILE
    cb = col0 // SB_LANES
    kb = SB_WIDTH // SB_LANES
    db = dcol0 // SB_LANES

    def body(q_ref, k_ref, v_ref, do_ref, dq_ref, dk_ref, dv_ref, dk_acc, dv_acc, dpan, span, gsum, lsig_s, lf_s):
        i = pl.program_id(1)

        @pl.when(i == 0)
        def _():
            dk_acc[...] = jnp.zeros_like(dk_acc)
            dv_acc[...] = jnp.zeros_like(dv_acc)

        r = lax.broadcasted_iota(jnp.int32, (TILE, TILE), 0)
        c = lax.broadcasted_iota(jnp.int32, (TILE, TILE), 1)
        strict = c < r
        u_after = _tri(TILE, lambda rr, cc: rr > cc)
        u_before = _tri(TILE, lambda rr, cc: rr < cc)
        qs = [_scaled_q(q_ref, _head_slices(hh), -SCALE) for hh in range(SB_GROUP)]
        dos = [do_ref[:, _head_slices(hh)].astype(BF16) for hh in range(SB_GROUP)]

        def scores(j, slot, diag):
            kblk = k_ref[_rows(k_ref, j), :]
            for hh in range(SB_GROUP):
                nz = _dot_nt(qs[hh], kblk[:, _head_slices(hh)])
                lf = jnp.minimum(nz, 0.0) - jnp.log(1.0 + jnp.exp(_neg_abs(nz)))
                lsig = lf - nz
                if diag:
                    lf = jnp.where(strict, lf, 0.0)
                    lsig = jnp.where(strict, lsig, MASKED)
                lsig_s[slot, hh] = lsig
                lf_s[slot, hh] = lf.astype(BF16)

        def grads(j, slot, carries):
            vblk = v_ref[_rows(v_ref, j), :]
            new = []
            for hh in range(SB_GROUP):
                lfb = lf_s[slot, hh]
                lsig = lsig_s[slot, hh]
                sx = _dot(lfb, u_after)
                a = jnp.exp(lsig + sx + carries[hh])
                g = a * _dot_nt(dos[hh], vblk[:, _head_slices(hh)])
                sig = jnp.exp(lsig)
                inside = _dot(g.astype(BF16), u_before)
                dpan[hh, j] = sig * (inside + g) - g
                span[hh, j] = sig
                gsum[hh, j] = inside[:, TILE - 1:TILE] + g[:, TILE - 1:TILE]
                dv_acc[hh, _rows(None, j), :] += _dot_tn(a.astype(BF16), dos[hh])
                new.append(carries[hh] + sx[:, 0:1] + lfb[:, 0:1].astype(F32))
            return tuple(new)

        def step1(t, carries):
            carries = grads(i - t + 1, (t - 1) % 2, carries)
            scores(i - t, t % 2, False)
            return carries

        zero1 = jnp.zeros((TILE, 1), F32)
        scores(i, 0, True)
        carries = lax.fori_loop(1, i + 1, step1, (zero1,) * SB_GROUP)
        grads(0, i % 2, carries)

        def pass2(j, state):
            kblk = k_ref[_rows(k_ref, j), :]
            new = []
            for hh in range(SB_GROUP):
                before, ndq = state[hh]
                ndzb = (dpan[hh, j] + span[hh, j] * before).astype(BF16)
                dk_acc[hh, _rows(None, j), :] += _dot_tn(ndzb, qs[hh])
                new.append((before + gsum[hh, j], ndq + _dot(ndzb, kblk[:, _head_slices(hh)])))
            return tuple(new)

        zero2 = (zero1, jnp.zeros((TILE, HEAD_DIM), F32))
        state = lax.fori_loop(0, i + 1, pass2, (zero2,) * SB_GROUP)
        dq_ref[...] = jnp.concatenate([st[1] * -SCALE for st in state], axis=1).astype(BF16)

        @pl.when(i == nb - 1)
        def _():
            dk_ref[...] = jnp.concatenate([dk_acc[hh] for hh in range(SB_GROUP)], axis=1).astype(BF16)
            dv_ref[...] = jnp.concatenate([dv_acc[hh] for hh in range(SB_GROUP)], axis=1).astype(BF16)

    qspec = pl.BlockSpec((TILE, SB_LANES), lambda p, i: (i, p))
    kvspec = pl.BlockSpec((s, SB_LANES), lambda p, i: (0, p))
    out = jax.ShapeDtypeStruct((s, SB_WIDTH), BF16)
    outs = _pair_grid_call(
        name, body, nb,
        in_specs=[pl.BlockSpec((TILE, SB_LANES), lambda p, i: (i, cb + p)),
                  pl.BlockSpec((s, SB_LANES), lambda p, i: (0, cb + kb + p)),
                  pl.BlockSpec((s, SB_LANES), lambda p, i: (0, cb + 2 * kb + p)),
                  pl.BlockSpec((TILE, SB_LANES), lambda p, i: (i, db + p))],
        out_specs=[qspec, kvspec, kvspec], out_shape=[out, out, out],
        scratch=[pltpu.VMEM((SB_GROUP, s, HEAD_DIM), F32), pltpu.VMEM((SB_GROUP, s, HEAD_DIM), F32),
                 pltpu.VMEM((SB_GROUP, nb, TILE, TILE), F32), pltpu.VMEM((SB_GROUP, nb, TILE, TILE), F32),
                 pltpu.VMEM((SB_GROUP, nb, TILE, 1), F32),
                 pltpu.VMEM((2, SB_GROUP, TILE, TILE), F32), pltpu.VMEM((2, SB_GROUP, TILE, TILE), BF16)],
        args=(pa, pa, pa, dout), carried=carried, groups=kb)
    return outs[:3], outs[3:]


def _fox_scores(q, kj, cq, crj, causal, diag):
    sc = _dot_nt(q, kj) + (cq - crj)
    if diag:
        sc = jnp.where(causal, sc, NEG_INF)
    return sc


def _fox_fwd(name, pa, col0, ccol4, crow4, carried=None):
    s = pa.shape[0]
    nb = s // TILE
    cb = col0 // LANES

    def body(q_ref, k_ref, v_ref, cc_ref, cr_ref, o_ref, lse_ref, sc_s):
        i = pl.program_id(1)
        r = lax.broadcasted_iota(jnp.int32, (TILE, TILE), 0)
        c = lax.broadcasted_iota(jnp.int32, (TILE, TILE), 1)
        causal = c <= r
        qs = [_scaled_q(q_ref, _head_slices(hh)) for hh in range(2)]
        cqs = [cc_ref[:, HEAD_DIM * hh:HEAD_DIM * hh + 1] for hh in range(2)]

        def logits(j, slot, diag):
            kblk = k_ref[_rows(k_ref, j), :]
            tops = []
            for hh in range(2):
                sc = _fox_scores(qs[hh], kblk[:, _head_slices(hh)], cqs[hh], cr_ref[j, hh:hh + 1, :], causal, diag)
                sc_s[slot, hh] = sc
                tops.append(jnp.max(sc, axis=1, keepdims=True))
            return tuple(tops)

        def update(j, slot, tops, state):
            vblk = v_ref[_rows(v_ref, j), :]
            new = []
            for hh in range(2):
                m, l, acc = state[hh]
                m2 = jnp.maximum(m, tops[hh])
                alpha = jnp.exp(m - m2)
                p = jnp.exp(sc_s[slot, hh] - m2)
                new.append((m2, l * alpha + jnp.sum(p, axis=1, keepdims=True),
                            acc * alpha + _dot(p.astype(BF16), vblk[:, _head_slices(hh)])))
            return tuple(new)

        def step(t, both):
            tops, state = both
            state = update(i - t + 1, (t - 1) % 2, tops, state)
            return logits(i - t, t % 2, False), state

        zero = (jnp.full((TILE, 1), NEG_INF, F32), jnp.zeros((TILE, 1), F32), jnp.zeros((TILE, HEAD_DIM), F32))
        tops, state = lax.fori_loop(1, i + 1, step, (logits(i, 0, True), (zero, zero)))
        state = update(0, i % 2, tops, state)
        o_ref[...] = jnp.concatenate([state[hh][2] / state[hh][1] for hh in range(2)], axis=1)
        lse_ref[...] = jnp.concatenate(
            [jnp.broadcast_to(state[hh][0] + jnp.log(state[hh][1]), (TILE, HEAD_DIM)) for hh in range(2)], axis=1)

    outs = _pair_grid_call(
        name, body, nb,
        in_specs=[pl.BlockSpec((TILE, LANES), lambda p, i: (i, cb + p)),
                  pl.BlockSpec((s, LANES), lambda p, i: (0, cb + 4 + p)),
                  pl.BlockSpec((s, LANES), lambda p, i: (0, cb + 8 + p)),
                  pl.BlockSpec((None, TILE, LANES), lambda p, i: (p, i, 0)),
                  pl.BlockSpec((None, nb, 8, TILE), lambda p, i: (p, 0, 0, 0))],
        out_specs=[pl.BlockSpec((TILE, LANES), lambda p, i: (i, p)),
                   pl.BlockSpec((None, TILE, LANES), lambda p, i: (p, i, 0))],
        out_shape=[jax.ShapeDtypeStruct((s, FOX_WIDTH), F32), jax.ShapeDtypeStruct((4, s, LANES), F32)],
        scratch=[pltpu.VMEM((2, 2, TILE, TILE), F32)],
        args=(pa, pa, pa, ccol4, crow4), carried=carried)
    return outs[0], outs[1], outs[2:]


def _fox_bwd(name, pa, col0, ccol4, crow4, out, lse, dout, dcol0):
    s = pa.shape[0]
    nb = s // TILE
    cb = col0 // LANES
    db = dcol0 // LANES

    def body(q_ref, k_ref, v_ref, cc_ref, cr_ref, o_ref, lse_ref, do_ref,
             dq_ref, dk_ref, dv_ref, cs_ref, dk_acc, dv_acc, p_s, ds_s):
        i = pl.program_id(1)

        @pl.when(i == 0)
        def _():
            dk_acc[...] = jnp.zeros_like(dk_acc)
            dv_acc[...] = jnp.zeros_like(dv_acc)
            cs_ref[...] = jnp.zeros_like(cs_ref)

        r = lax.broadcasted_iota(jnp.int32, (TILE, TILE), 0)
        c = lax.broadcasted_iota(jnp.int32, (TILE, TILE), 1)
        causal = c <= r
        qs = [_scaled_q(q_ref, _head_slices(hh)) for hh in range(2)]
        cqs = [cc_ref[:, HEAD_DIM * hh:HEAD_DIM * hh + 1] for hh in range(2)]
        lses = [lse_ref[:, HEAD_DIM * hh:HEAD_DIM * hh + 1] for hh in range(2)]
        dofs = [do_ref[:, _head_slices(hh)] for hh in range(2)]
        dos = [d_.astype(BF16) for d_ in dofs]
        deltas = [jnp.sum(dofs[hh] * o_ref[:, _head_slices(hh)], axis=1, keepdims=True) for hh in range(2)]

        def probs(j, slot, rowsums, diag):
            kblk = k_ref[_rows(k_ref, j), :]
            vblk = v_ref[_rows(v_ref, j), :]
            new = []
            for hh in range(2):
                sl = _head_slices(hh)
                sc = _fox_scores(qs[hh], kblk[:, sl], cqs[hh], cr_ref[j, hh:hh + 1, :], causal, diag)
                p = jnp.exp(sc - lses[hh])
                ds = p * (_dot_nt(dos[hh], vblk[:, sl]) - deltas[hh])
                p_s[slot, hh] = p.astype(BF16)
                ds_s[slot, hh] = ds.astype(BF16)
                cs_ref[j, hh:hh + 1, :] += jnp.sum(ds, axis=0, keepdims=True)
                new.append(rowsums[hh] + jnp.sum(ds, axis=1, keepdims=True))
            return tuple(new)

        def accumulate(j, slot, dqs):
            kblk = k_ref[_rows(k_ref, j), :]
            new = []
            for hh in range(2):
                dsb = ds_s[slot, hh]
                dv_acc[hh, _rows(None, j), :] += _dot_tn(p_s[slot, hh], dos[hh])
                dk_acc[hh, _rows(None, j), :] += _dot_tn(dsb, qs[hh])
                new.append(dqs[hh] + _dot(dsb, kblk[:, _head_slices(hh)]))
            return tuple(new)

        def step(t, both):
            rowsums, dqs = both
            dqs = accumulate(i - t + 1, (t - 1) % 2, dqs)
            return probs(i - t, t % 2, rowsums, False), dqs

        zero1 = jnp.zeros((TILE, 1), F32)
        zero64 = jnp.zeros((TILE, HEAD_DIM), F32)
        rowsums, dqs = lax.fori_loop(1, i + 1, step, (probs(i, 0, (zero1, zero1), True), (zero64, zero64)))
        dqs = accumulate(0, i % 2, dqs)
        for hh in range(2):
            cs_ref[i, hh:hh + 1, :] -= jnp.broadcast_to(rowsums[hh], (TILE, LANES)).T[0:1, :]
        dq_ref[...] = jnp.concatenate([dqs[0] * SCALE, dqs[1] * SCALE], axis=1).astype(BF16)

        @pl.when(i == nb - 1)
        def _():
            dk_ref[...] = jnp.concatenate([dk_acc[0], dk_acc[1]], axis=1).astype(BF16)
            dv_ref[...] = jnp.concatenate([dv_acc[0], dv_acc[1]], axis=1).astype(BF16)

    qspec = pl.BlockSpec((TILE, LANES), lambda p, i: (i, p))
    kvspec = pl.BlockSpec((s, LANES), lambda p, i: (0, p))
    o3 = jax.ShapeDtypeStruct((s, FOX_WIDTH), BF16)
    return pl.pallas_call(
        body, name=name, grid=(4, nb),
        in_specs=[pl.BlockSpec((TILE, LANES), lambda p, i: (i, cb + p)),
                  pl.BlockSpec((s, LANES), lambda p, i: (0, cb + 4 + p)),
                  pl.BlockSpec((s, LANES), lambda p, i: (0, cb + 8 + p)),
                  pl.BlockSpec((None, TILE, LANES), lambda p, i: (p, i, 0)),
                  pl.BlockSpec((None, nb, 8, TILE), lambda p, i: (p, 0, 0, 0)),
                  qspec,
                  pl.BlockSpec((None, TILE, LANES), lambda p, i: (p, i, 0)),
                  pl.BlockSpec((TILE, LANES), lambda p, i: (i, db + p))],
        out_specs=[qspec, kvspec, kvspec, pl.BlockSpec((None, nb, 8, TILE), lambda p, i: (p, 0, 0, 0))],
        out_shape=[o3, o3, o3, jax.ShapeDtypeStruct((4, nb, 8, TILE), F32)],
        scratch_shapes=[pltpu.VMEM((2, s, HEAD_DIM), F32), pltpu.VMEM((2, s, HEAD_DIM), F32),
                        pltpu.VMEM((2, 2, TILE, TILE), BF16), pltpu.VMEM((2, 2, TILE, TILE), BF16)],
        compiler_params=_params("arbitrary", "arbitrary"),
    )(pa, pa, pa, ccol4, crow4, out, lse, dout)


def _mem_fwd(name, pa, mkv):
    s = pa.shape[0]
    ml = mkv.shape[0]
    nb = s // TILE
    cb = QKV_WIDTH // LANES

    def body(q_ref, k_ref, v_ref, o_ref, lse_ref):
        outs, lses = [], []
        for hh in range(2):
            sl = _head_slices(hh)
            sc = _dot_nt(_scaled_q(q_ref, sl), k_ref[:, sl])
            m = jnp.max(sc, axis=1, keepdims=True)
            p = jnp.exp(sc - m)
            l = jnp.sum(p, axis=1, keepdims=True)
            outs.append(_dot(p.astype(BF16), v_ref[:, sl]) / l)
            lses.append(jnp.broadcast_to(m + jnp.log(l), (TILE, HEAD_DIM)))
        o_ref[...] = jnp.concatenate(outs, axis=1)
        lse_ref[...] = jnp.concatenate(lses, axis=1)

    return pl.pallas_call(
        body, name=name, grid=(2, nb),
        in_specs=[pl.BlockSpec((TILE, LANES), lambda p, i: (i, cb + p)),
                  pl.BlockSpec((ml, LANES), lambda p, i: (0, p)),
                  pl.BlockSpec((ml, LANES), lambda p, i: (0, 2 + p))],
        out_specs=[pl.BlockSpec((TILE, LANES), lambda p, i: (i, p)),
                   pl.BlockSpec((None, TILE, LANES), lambda p, i: (p, i, 0))],
        out_shape=[jax.ShapeDtypeStruct((s, MEM_WIDTH), F32), jax.ShapeDtypeStruct((2, s, LANES), F32)],
        compiler_params=_params("parallel", "parallel"),
    )(pa, mkv, mkv)


def _mem_bwd(name, pa, mkv, out, lse, dout, dcol0):
    s = pa.shape[0]
    ml = mkv.shape[0]
    nb = s // TILE
    cb = QKV_WIDTH // LANES
    db = dcol0 // LANES

    def body(q_ref, k_ref, v_ref, o_ref, lse_ref, do_ref, dq_ref, dk_ref, dv_ref, dk_acc, dv_acc):
        i = pl.program_id(1)

        @pl.when(i == 0)
        def _():
            dk_acc[...] = jnp.zeros_like(dk_acc)
            dv_acc[...] = jnp.zeros_like(dv_acc)

        dqs = []
        for hh in range(2):
            sl = _head_slices(hh)
            q = _scaled_q(q_ref, sl)
            kh = k_ref[:, sl]
            dof = do_ref[:, sl]
            do = dof.astype(BF16)
            delta = jnp.sum(dof * o_ref[:, sl], axis=1, keepdims=True)
            p = jnp.exp(_dot_nt(q, kh) - lse_ref[:, HEAD_DIM * hh:HEAD_DIM * hh + 1])
            ds = (p * (_dot_nt(do, v_ref[:, sl]) - delta)).astype(BF16)
            dv_acc[hh] += _dot_tn(p.astype(BF16), do)
            dk_acc[hh] += _dot_tn(ds, q)
            dqs.append(_dot(ds, kh) * SCALE)
        dq_ref[...] = jnp.concatenate(dqs, axis=1).astype(BF16)

        @pl.when(i == nb - 1)
        def _():
            dk_ref[...] = jnp.concatenate([dk_acc[0], dk_acc[1]], axis=1).astype(BF16)
            dv_ref[...] = jnp.concatenate([dv_acc[0], dv_acc[1]], axis=1).astype(BF16)

    qspec = pl.BlockSpec((TILE, LANES), lambda p, i: (i, p))
    kvspec = pl.BlockSpec((ml, LANES), lambda p, i: (0, p))
    okv = jax.ShapeDtypeStruct((ml, MEM_WIDTH), BF16)
    return pl.pallas_call(
        body, name=name, grid=(2, nb),
        in_specs=[pl.BlockSpec((TILE, LANES), lambda p, i: (i, cb + p)),
                  pl.BlockSpec((ml, LANES), lambda p, i: (0, p)),
                  pl.BlockSpec((ml, LANES), lambda p, i: (0, 2 + p)),
                  qspec,
                  pl.BlockSpec((None, TILE, LANES), lambda p, i: (p, i, 0)),
                  pl.BlockSpec((TILE, LANES), lambda p, i: (i, db + p))],
        out_specs=[qspec, kvspec, kvspec],
        out_shape=[jax.ShapeDtypeStruct((s, MEM_WIDTH), BF16), okv, okv],
        scratch_shapes=[pltpu.VMEM((2, ml, HEAD_DIM), F32), pltpu.VMEM((2, ml, HEAD_DIM), F32)],
        compiler_params=_params("arbitrary", "arbitrary"),
    )(pa, mkv, mkv, out, lse, dout)


def _head_maps():
    col = jnp.arange(MIX_WIDTH)[:, None] // HEAD_DIM
    g = (col == jnp.arange(LANES)[None, :]).astype(BF16)
    return g, g.T


def _normed_heads(osb_ref, ofx_ref, om_ref, g_ref, gt_ref):
    y = jnp.concatenate([osb_ref[...], ofx_ref[...], om_ref[...]], axis=1)
    msq = _sum_l2(y * y, g_ref[...]) * (1.0 / HEAD_DIM)
    rf = _sum_l3(lax.rsqrt(msq + EPS), gt_ref[...])
    return y * rf, rf


def _out_fwd(name, o_sb, o_fx, o_m, pb, ow, x, w_out, ts):
    s, d = x.shape
    g, gt = _head_maps()

    def body(osb_ref, ofx_ref, om_ref, gate_ref, ow_ref, x_ref, w_ref, g_ref, gt_ref, xo_ref, y2_ref):
        yh, _ = _normed_heads(osb_ref, ofx_ref, om_ref, g_ref, gt_ref)
        gate = gate_ref[...]
        y2 = (yh * ow_ref[...] * (gate * jax.nn.sigmoid(gate))).astype(BF16)
        y2_ref[...] = y2
        xo_ref[...] = x_ref[...] + _dot(y2, w_ref[...])

    return pl.pallas_call(
        body, name=name, grid=(s // ts,),
        in_specs=[_row_spec(ts, SB_WIDTH), _row_spec(ts, FOX_WIDTH), _row_spec(ts, MEM_WIDTH),
                  _row_spec(ts, MIX_WIDTH), _const_spec((1, MIX_WIDTH)), _row_spec(ts, d),
                  _const_spec((MIX_WIDTH, d)),
                  _const_spec((MIX_WIDTH, LANES)), _const_spec((LANES, MIX_WIDTH))],
        out_specs=[_row_spec(ts, d), _row_spec(ts, MIX_WIDTH)],
        out_shape=[jax.ShapeDtypeStruct((s, d), F32), jax.ShapeDtypeStruct((s, MIX_WIDTH), BF16)],
        compiler_params=_params("parallel"),
    )(o_sb, o_fx, o_m, pb, ow, x, w_out, g, gt)


def _row_spec(ts, w):
    return pl.BlockSpec((ts, w), lambda i: (i, 0))


def _const_spec(shape):
    return pl.BlockSpec(shape, lambda i: (0,) * len(shape))


def _out_bwd(name, dxb, o_sb, o_fx, o_m, pb, ow, w_out, ts):
    s, d = dxb.shape
    g, gt = _head_maps()

    def body(dx_ref, osb_ref, ofx_ref, om_ref, gate_ref, ow_ref, w_ref, g_ref, gt_ref, dy_ref, dgate_ref, dow_ref):
        @pl.when(pl.program_id(0) == 0)
        def _():
            dow_ref[...] = jnp.zeros_like(dow_ref)

        dy2 = _dot_nt(dx_ref[...], w_ref[...])
        yh, rf = _normed_heads(osb_ref, ofx_ref, om_ref, g_ref, gt_ref)
        gate = gate_ref[...]
        sig = jax.nn.sigmoid(gate)
        ow_v = ow_ref[...]
        dgate_ref[...] = (dy2 * (yh * ow_v) * (sig * (1.0 + gate * (1.0 - sig)))).astype(BF16)
        dn = dy2 * (gate * sig)
        dow_ref[...] += jnp.sum(dn * yh, axis=0, keepdims=True)
        dyh = dn * ow_v
        t = _sum_l2(dyh * yh, g_ref[...]) * (1.0 / HEAD_DIM)
        dy_ref[...] = rf * (dyh - yh * _sum_l3(t, gt_ref[...]))

    return pl.pallas_call(
        body, name=name, grid=(s // ts,),
        in_specs=[_row_spec(ts, d), _row_spec(ts, SB_WIDTH), _row_spec(ts, FOX_WIDTH), _row_spec(ts, MEM_WIDTH),
                  _row_spec(ts, MIX_WIDTH), _const_spec((1, MIX_WIDTH)),
                  _const_spec((MIX_WIDTH, d)),
                  _const_spec((MIX_WIDTH, LANES)), _const_spec((LANES, MIX_WIDTH))],
        out_specs=[_row_spec(ts, MIX_WIDTH), _row_spec(ts, MIX_WIDTH), _const_spec((1, MIX_WIDTH))],
        out_shape=[jax.ShapeDtypeStruct((s, MIX_WIDTH), F32), jax.ShapeDtypeStruct((s, MIX_WIDTH), BF16),
                   jax.ShapeDtypeStruct((1, MIX_WIDTH), F32)],
        compiler_params=_params("arbitrary"),
    )(dxb, o_sb, o_fx, o_m, pb, ow, w_out, g, gt)


def _adamw(name, w, g, m, v, tr):
    def body(w_ref, g_ref, m_ref, v_ref, d_ref, m2_ref, v2_ref):
        gv = g_ref[...]
        m2 = ADAM_B1 * m_ref[...] + (1.0 - ADAM_B1) * gv
        v2 = ADAM_B2 * v_ref[...] + (1.0 - ADAM_B2) * (gv * gv)
        m_hat = m2 / (1.0 - ADAM_B1 ** ADAM_STEP)
        v_hat = v2 / (1.0 - ADAM_B2 ** ADAM_STEP)
        d_ref[...] = -ADAM_LR * (m_hat / (jnp.sqrt(v_hat) + ADAM_EPS) + ADAM_WD * w_ref[...])
        m2_ref[...] = m2
        v2_ref[...] = v2

    rest = w.shape[1:]
    spec = pl.BlockSpec((tr,) + rest, lambda i: (i,) + (0,) * len(rest))
    shp = jax.ShapeDtypeStruct(w.shape, F32)
    return pl.pallas_call(
        body, name=name, grid=(w.shape[0] // tr,), in_specs=[spec] * 4, out_specs=[spec] * 3, out_shape=[shp] * 3,
        compiler_params=_params("parallel"),
    )(w, g, m, v)


def _adamw_sharded(name, w, m, v, g_own, g_other, cvec, tr):
    depth, rows, cols = w.shape
    nt = rows // 2 // tr

    def body(c_ref, w_ref, m_ref, v_ref, *rest):
        g_refs, (g_ref, d_ref, m2_ref, v2_ref) = rest[:2 * depth], rest[2 * depth:]
        layer, mine = pl.program_id(0), pl.program_id(1) == c_ref[0]
        gv = None
        for lt in range(depth):
            cand = jnp.where(mine, g_refs[lt][...], g_refs[depth + lt][...])
            gv = cand if gv is None else jnp.where(layer == lt, cand, gv)
        m2 = ADAM_B1 * m_ref[...] + (1.0 - ADAM_B1) * gv
        v2 = ADAM_B2 * v_ref[...] + (1.0 - ADAM_B2) * (gv * gv)
        m_hat = m2 / (1.0 - ADAM_B1 ** ADAM_STEP)
        v_hat = v2 / (1.0 - ADAM_B2 ** ADAM_STEP)
        g_ref[...] = gv
        d_ref[...] = -ADAM_LR * (m_hat / (jnp.sqrt(v_hat) + ADAM_EPS) + ADAM_WD * w_ref[...])
        m2_ref[...] = m2
        v2_ref[...] = v2

    def g_map(lt, own):
        def index(l, hf, i, c_ref):
            use = jnp.logical_and(l == lt, (hf == c_ref[0]) == own)
            return jnp.where(use, i, 0), 0
        return index

    full = pl.BlockSpec((None, tr, cols), lambda l, hf, i, c_ref: (l, hf * nt + i, 0))
    g_specs = [pl.BlockSpec((tr, cols), g_map(lt, own)) for own in (True, False) for lt in range(depth)]
    shp = jax.ShapeDtypeStruct((depth, rows, cols), F32)
    return pl.pallas_call(
        body, name=name,
        grid_spec=pltpu.PrefetchScalarGridSpec(
            num_scalar_prefetch=1, grid=(depth, 2, nt), in_specs=[full] * 3 + g_specs, out_specs=[full] * 4),
        out_shape=[shp] * 4,
        compiler_params=_params("arbitrary", "arbitrary", "arbitrary"),
    )(cvec, w, m, v, *g_own, *g_other)


HBM_SPEC = pl.BlockSpec(memory_space=pltpu.HBM)


def _place():
    x, y, c = lax.axis_index("x"), lax.axis_index("y"), lax.axis_index("c")
    chips = [(1 - x, y), (x, 1 - y), (1 - x, 1 - y)]
    return x, y, c, chips


def _remote(src, dst, send_sems, recv_sems, k, to):
    return pltpu.make_async_remote_copy(src_ref=src, dst_ref=dst, send_sem=send_sems.at[k], recv_sem=recv_sems.at[k],
                                        device_id=to, device_id_type=MESH)


def _half_rows(n_rows, cc):
    rh = n_rows // 2
    return pl.ds(pl.multiple_of(cc * rh, 16), rh)


def _dma_sems(n):
    return [pltpu.SemaphoreType.DMA((n,)), pltpu.SemaphoreType.DMA((n,))]


class _Exchange:
    def __init__(self, inputs, out_shapes, n_sems, begin, relay, finish):
        self.inputs, self.out_shapes, self.n_sems = list(inputs), list(out_shapes), n_sems
        self.begin, self.relay, self.finish = begin, relay, finish

    @property
    def n(self):
        return len(self.inputs)

    def split(self, refs):
        return refs[:self.n], refs[self.n:2 * self.n], refs[2 * self.n], refs[2 * self.n + 1]


def _run_exchange(name, ex):
    def body(*refs):
        parts = ex.split(refs)
        for phase in (ex.begin, ex.relay, ex.finish):
            if phase is not None:
                phase(*parts)

    return pl.pallas_call(
        body, name=name, in_specs=[HBM_SPEC] * ex.n, out_specs=[HBM_SPEC] * ex.n, out_shape=ex.out_shapes,
        scratch_shapes=_dma_sems(ex.n_sems),
    )(*ex.inputs)


def _gather_exchange(shards):
    def ici(in_refs, out_refs, send_sems, recv_sems):
        x, y, c, chips = _place()
        return [_remote(in_ref.at[_half_rows(in_ref.shape[0], c)], out_ref.at[2 * x + y, _half_rows(in_ref.shape[0], c)],
                        send_sems, recv_sems, 6 * a + j, (cx, cy, c))
                for a, (in_ref, out_ref) in enumerate(zip(in_refs, out_refs)) for j, (cx, cy) in enumerate(chips)]

    def d2d(out_refs, send_sems, recv_sems, half_of):
        x, y, c, chips = _place()
        cps = []
        for a, out_ref in enumerate(out_refs):
            for j, (cx, cy) in enumerate(chips):
                piece = out_ref.at[2 * cx + cy, _half_rows(out_ref.shape[1], half_of(c))]
                cps.append(_remote(piece, piece, send_sems, recv_sems, 6 * a + 3 + j, (x, y, 1 - c)))
        return cps

    def begin(in_refs, out_refs, send_sems, recv_sems):
        for cp in ici(in_refs, out_refs, send_sems, recv_sems):
            cp.start()

    def relay(in_refs, out_refs, send_sems, recv_sems):
        x, y, c, chips = _place()
        for a, out_ref in enumerate(out_refs):
            for j, (cx, cy) in enumerate(chips):
                landed = out_ref.at[2 * cx + cy, _half_rows(out_ref.shape[1], c)]
                _remote(landed, landed, send_sems, recv_sems, 6 * a + j, (cx, cy, c)).wait_recv()
        for cp in d2d(out_refs, send_sems, recv_sems, lambda c_: c_):
            cp.start()

    def finish(in_refs, out_refs, send_sems, recv_sems):
        for cp in d2d(out_refs, send_sems, recv_sems, lambda c_: 1 - c_):
            cp.wait_recv()
        for cp in ici(in_refs, out_refs, send_sems, recv_sems) + d2d(out_refs, send_sems, recv_sems, lambda c_: c_):
            cp.wait_send()

    shapes = [jax.ShapeDtypeStruct((N_CHIPS,) + s_.shape, s_.dtype) for s_ in shards]
    return _Exchange(shards, shapes, 6 * len(shards), begin, relay, finish)


def _swap_halves(name, g4s):
    n = len(g4s)

    def body(*refs):
        in_refs, out_refs, (send_sems, recv_sems) = refs[:n], refs[n:2 * n], refs[2 * n:]
        x, y, c, _ = _place()
        cps = [_remote(in_ref.at[:, _half_rows(in_ref.shape[1], 1 - c), :], out_ref, send_sems, recv_sems, a, (x, y, 1 - c))
               for a, (in_ref, out_ref) in enumerate(zip(in_refs, out_refs))]
        for cp in cps:
            cp.start()
        for cp in cps:
            cp.wait()

    return pl.pallas_call(
        body, name=name, in_specs=[HBM_SPEC] * n, out_specs=[HBM_SPEC] * n,
        out_shape=[jax.ShapeDtypeStruct((g.shape[0], g.shape[1] // 2, g.shape[2]), g.dtype) for g in g4s],
        scratch_shapes=_dma_sems(n),
    )(*g4s)


def _add_half(name, g4, r1, cvec, tr):
    n, r, w = g4.shape
    rh = r // 2
    nblk = rh // tr

    def body(c_ref, a_ref, b_ref, o_ref):
        o_ref[...] = (a_ref[...] + b_ref[...]).astype(BF16)

    return pl.pallas_call(
        body, name=name,
        grid_spec=pltpu.PrefetchScalarGridSpec(
            num_scalar_prefetch=1, grid=(n, nblk),
            in_specs=[pl.BlockSpec((None, tr, w), lambda k, i, c_ref: (k, c_ref[0] * nblk + i, 0)),
                      pl.BlockSpec((None, tr, w), lambda k, i, c_ref: (k, i, 0))],
            out_specs=pl.BlockSpec((None, tr, w), lambda k, i, c_ref: (k, i, 0))),
        out_shape=jax.ShapeDtypeStruct((n, rh, w), BF16),
        compiler_params=_params("parallel", "parallel"),
    )(cvec, g4, r1)


def _scatter_exchange(h4s):
    def sends(in_refs, out_refs, send_sems, recv_sems):
        x, y, c, chips = _place()
        return [_remote(in_ref.at[2 * cx + cy], out_ref.at[j], send_sems, recv_sems, 3 * a + j, (cx, cy, c))
                for a, (in_ref, out_ref) in enumerate(zip(in_refs, out_refs)) for j, (cx, cy) in enumerate(chips)]

    def begin(*parts):
        for cp in sends(*parts):
            cp.start()

    def finish(in_refs, out_refs, send_sems, recv_sems):
        x, y, c, chips = _place()
        for a, out_ref in enumerate(out_refs):
            for j, (cx, cy) in enumerate(chips):
                got = out_ref.at[j]
                _remote(got, got, send_sems, recv_sems, 3 * a + j, (cx, cy, c)).wait_recv()
        for cp in sends(in_refs, out_refs, send_sems, recv_sems):
            cp.wait_send()

    shapes = [jax.ShapeDtypeStruct((3,) + h.shape[1:], h.dtype) for h in h4s]
    return _Exchange(h4s, shapes, 3 * len(h4s), begin, None, finish)


def _sum_chips(name, h4, r3, mvec, tr):
    _, rh, w = h4.shape

    def body(m_ref, a_ref, b_ref, c_ref, d_ref, o_ref):
        o_ref[...] = ((a_ref[...].astype(F32) + b_ref[...].astype(F32)) + c_ref[...].astype(F32)) + d_ref[...].astype(F32)

    specs = [pl.BlockSpec((None, tr, w), lambda i, m_ref: (m_ref[0], i, 0))]
    specs += [pl.BlockSpec((None, tr, w), functools.partial(lambda k, i, m_ref: (k, i, 0), k)) for k in range(3)]
    return pl.pallas_call(
        body, name=name,
        grid_spec=pltpu.PrefetchScalarGridSpec(
            num_scalar_prefetch=1, grid=(rh // tr,), in_specs=specs,
            out_specs=pl.BlockSpec((tr, w), lambda i, m_ref: (i, 0))),
        out_shape=jax.ShapeDtypeStruct((rh, w), F32),
        compiler_params=_params("parallel"),
    )(mvec, h4, r3, r3, r3)


def _swap_reduced(name, ghs):
    n = len(ghs)

    def body(*refs):
        in_refs, out_refs, (send_sems, recv_sems) = refs[:n], refs[n:2 * n], refs[2 * n:]
        x, y, c, _ = _place()
        cps = [_remote(in_ref, out_ref, send_sems, recv_sems, a, (x, y, 1 - c))
               for a, (in_ref, out_ref) in enumerate(zip(in_refs, out_refs))]
        for cp in cps:
            cp.start()
        for cp in cps:
            cp.wait()

    return pl.pallas_call(
        body, name=name, in_specs=[HBM_SPEC] * n, out_specs=[HBM_SPEC] * n,
        out_shape=[jax.ShapeDtypeStruct(g.shape, g.dtype) for g in ghs],
        scratch_shapes=_dma_sems(n),
    )(*ghs)


def _small_update(name, partials, weights, moments1, moments2):
    n = len(partials)
    width = max(p.shape[1] for p in partials)
    starts, at = [], 0
    for p in partials:
        starts.append(at)
        at += p.shape[0]
    rows = -(-at // 8) * 8
    has_w = [w is not None for w in weights]
    n_w = sum(has_w)

    def body(*refs):
        p_refs = refs[:n]
        w_refs, m_refs, v_refs = refs[n:n + n_w], refs[n + n_w:n + 2 * n_w], refs[n + 2 * n_w:n + 3 * n_w]
        outs = refs[n + 3 * n_w:-4]
        g_refs, upd_refs = outs[:n], outs[n:]
        vec, buf, send_sems, recv_sems = refs[-4:]
        x, y, c, _ = _place()
        me = 4 * x + 2 * y + c
        vec[...] = jnp.zeros_like(vec)
        for p_ref, r0 in zip(p_refs, starts):
            vec[r0:r0 + p_ref.shape[0], 0:p_ref.shape[1]] = p_ref[...]
        buf[me] = vec[...]
        flips = [(fx, fy, fc) for fx in (0, 1) for fy in (0, 1) for fc in (0, 1)][1:]
        peers = [(x + fx - 2 * x * fx, y + fy - 2 * y * fy, c + fc - 2 * c * fc) for fx, fy, fc in flips]
        sends = [_remote(vec, buf.at[me], send_sems, recv_sems, k, peer) for k, peer in enumerate(peers)]
        for cp in sends:
            cp.start()
        for k, (px, py, pc) in enumerate(peers):
            got = buf.at[4 * px + 2 * py + pc]
            _remote(got, got, send_sems, recv_sems, k, (px, py, pc)).wait_recv()
        for cp in sends:
            cp.wait_send()
        total = buf[0]
        for dev in range(1, N_DEV):
            total = total + buf[dev]
        k = 0
        for a in range(n):
            r, w = g_refs[a].shape
            g = total[starts[a]:starts[a] + r, 0:w]
            g_refs[a][...] = g
            if has_w[a]:
                m2 = ADAM_B1 * m_refs[k][...] + (1.0 - ADAM_B1) * g
                v2 = ADAM_B2 * v_refs[k][...] + (1.0 - ADAM_B2) * (g * g)
                m_hat = m2 / (1.0 - ADAM_B1 ** ADAM_STEP)
                v_hat = v2 / (1.0 - ADAM_B2 ** ADAM_STEP)
                upd_refs[3 * k][...] = -ADAM_LR * (m_hat / (jnp.sqrt(v_hat) + ADAM_EPS) + ADAM_WD * w_refs[k][...])
                upd_refs[3 * k + 1][...] = m2
                upd_refs[3 * k + 2][...] = v2
                k += 1

    ws = [w for w in weights if w is not None]
    g_shapes = [jax.ShapeDtypeStruct(p.shape if w is None else w.shape, F32) for p, w in zip(partials, weights)]
    u_shapes = [jax.ShapeDtypeStruct(w.shape, F32) for w in ws for _ in range(3)]
    vm = pl.BlockSpec(memory_space=pltpu.VMEM)
    n_args = n + 3 * n_w
    outs = pl.pallas_call(
        body, name=name, in_specs=[vm] * n_args, out_specs=[vm] * (n + 3 * n_w), out_shape=g_shapes + u_shapes,
        scratch_shapes=[pltpu.VMEM((rows, width), F32), pltpu.VMEM((N_DEV, rows, width), F32),
                        pltpu.SemaphoreType.DMA((7,)), pltpu.SemaphoreType.DMA((7,))],
    )(*partials, *ws, *[m for m in moments1 if m is not None], *[v for v in moments2 if v is not None])
    return outs[:n], outs[n:]


GATE_COL = 3 * SB_WIDTH + 3 * FOX_WIDTH + FOX_HEADS + MEM_WIDTH
FL_COL = QKV_WIDTH


GROUP_A_COLS = [(0, QKV_WIDTH), (FL_COL + FOX_HEADS, MEM_WIDTH)]
GROUP_B_COLS = [(GATE_COL, MIX_WIDTH), (FL_COL, FOX_HEADS)]


def _group_from_shards(shard_of, cw, spans, pad):
    parts = []
    for lo, width in spans:
        hi = lo + width
        for j in range(N_CHIPS):
            a, b = max(lo, j * cw), min(hi, (j + 1) * cw)
            if a < b:
                parts.append(shard_of(j)[:, a - j * cw:b - j * cw])
    if pad:
        parts.append(jnp.zeros((parts[0].shape[0], pad), parts[0].dtype))
    return jnp.concatenate(parts, axis=1)


def _shard_from_groups(ga, gb, j, cw):
    lo, hi = j * cw, (j + 1) * cw
    placed = []
    for grp, spans in ((ga, GROUP_A_COLS), (gb, GROUP_B_COLS)):
        at = 0
        for first, width in spans:
            a, b = max(lo, first), min(hi, first + width)
            if a < b:
                placed.append((a, grp[:, at + a - first:at + b - first]))
            at += width
    return jnp.concatenate([p for _, p in sorted(placed, key=lambda t: t[0])], axis=1)


def _tile_of(n, cap, unit):
    if n <= cap:
        return n
    best = None
    for t in range(unit, cap + 1, unit):
        if n % t == 0:
            best = t
    assert best is not None, (n, cap, unit)
    return best


def _column_major_rows(a):
    dp, r, c = a.shape
    return a.transpose(2, 0, 1).reshape(c, dp, r // LANES, LANES).transpose(0, 2, 1, 3).reshape(-1, 8, LANES)


def _from_column_major_rows(b, shape):
    dp, r, c = shape
    return b.reshape(c, r // LANES, dp, LANES).transpose(0, 2, 1, 3).reshape(c, dp, r).transpose(1, 2, 0)


def _pack_small(parts):
    rows = []
    for p in parts:
        f = p.reshape(-1).astype(F32)
        f = jnp.pad(f, (0, (-f.shape[0]) % LANES))
        rows.append(f.reshape(-1, LANES))
    out = jnp.concatenate(rows, axis=0)
    return jnp.pad(out, ((0, (-out.shape[0]) % 8), (0, 0)))


def _unpack_small(packed, shapes):
    outs, r = [], 0
    for shp in shapes:
        n = 1
        for s_ in shp:
            n *= s_
        nr = -(-n // LANES)
        outs.append(packed[r:r + nr].reshape(-1)[:n].reshape(shp))
        r += nr
    return outs


def kernel(x, mem, norm_w, w_in, b_forget, mem_norm_w, w_mem_kv, out_norm_w, w_out, final_norm_w, loss_target, m_norm_w, m_w_in, m_b_forget, m_mem_norm_w, m_w_mem_kv, m_out_norm_w, m_w_out, m_final_norm_w, v_norm_w, v_w_in, v_b_forget, v_mem_norm_w, v_w_mem_kv, v_out_norm_w, v_w_out, v_final_norm_w):
    xs = x[0]
    mems = mem[0]
    target = loss_target[0]
    s, d = xs.shape
    depth = norm_w.shape[0]
    nb = s // TILE
    ts = _tile_of(s, 256, 8)
    big = (w_in, w_mem_kv, w_out)
    core = lax.axis_index("c")
    chip = 2 * lax.axis_index("x") + lax.axis_index("y")
    cvec = core.astype(jnp.int32).reshape(1)
    mvec = chip.astype(jnp.int32).reshape(1)
    cw = w_in.shape[2]

    own_w = [[a[l].astype(BF16) for a in big] for l in range(depth)]

    def lay_out(own, got):
        full = [jnp.where(lax.broadcasted_iota(jnp.int32, g.shape, 0) == chip, o[None], g) for g, o in zip(got, own)]
        shard_of = lambda j: full[0][j]
        wa_l = _group_from_shards(shard_of, cw, GROUP_A_COLS, 0)
        wb_l = _group_from_shards(shard_of, cw, GROUP_B_COLS, LANES - FOX_HEADS)
        return wa_l, wb_l, full[1].reshape(-1, full[1].shape[2]), full[2].reshape(-1, full[2].shape[2])

    layer_w = [lay_out(own_w[0], _run_exchange("gather_weights0", _gather_exchange(own_w[0])))]

    tm = _tile_of(s, 256, 8)
    fl_block = MIX_WIDTH // LANES

    saved = []
    cur = xs
    for l in range(depth):
        wa, wb, wkv, wout = layer_w[l]
        h = _rms_fwd(f"rms_fwd{l}", cur, norm_w[l][None], ts)
        pa = _mm(f"inproj_a{l}", h, wa, "nn", tm, _tile_of(PA, 1664, LANES), BF16)
        pb = _mm(f"inproj_b{l}", h, wb, "nn", tm, PB, F32)
        bpad = jnp.pad(b_forget[l], (0, LANES - FOX_HEADS))[None]
        ccol, crow = _gate_fwd(f"gate_fwd{l}", pb, bpad, fl_block)
        ccol4 = jnp.repeat(ccol[:, :FOX_HEADS].reshape(s, 4, 2).transpose(1, 0, 2), HEAD_DIM, axis=2)
        crow4 = jnp.pad(crow.reshape(nb, 4, 2, TILE).transpose(1, 0, 2, 3), ((0, 0), (0, 0), (0, 6), (0, 0)))
        more = l + 1 < depth
        o_sb, got_in = _sb_fwd(f"sb_fwd{l}", pa, 0, carried=_gather_exchange(own_w[l + 1][:1]) if more else None)
        o_fx, lse_fx, got_rest = _fox_fwd(f"fox_fwd{l}", pa, 3 * SB_WIDTH, ccol4, crow4,
                                          carried=_gather_exchange(own_w[l + 1][1:]) if more else None)
        if more:
            layer_w.append(lay_out(own_w[l + 1], list(got_in) + list(got_rest)))
        mn = _rms_fwd(f"mem_rms{l}", mems, mem_norm_w[l][None], mems.shape[0])
        mkv = _mm(f"mem_kv{l}", mn, wkv, "nn", mems.shape[0], 2 * MEM_WIDTH, BF16)
        o_m, lse_m = _mem_fwd(f"mem_fwd{l}", pa, mkv)
        nxt, y2 = _out_fwd(f"out_fwd{l}", o_sb, o_fx, o_m, pb, out_norm_w[l][None], cur, wout, ts)
        saved.append((cur, h, pa, pb, bpad, ccol4, crow4, o_sb, o_fx, lse_fx, mn, mkv, o_m, lse_m, y2))
        cur = nxt

    loss_v, dx, dxb, g_final = _final_loss("final_loss", cur, final_norm_w[None], target, ts)

    g_norm, g_b, g_memnorm, g_outnorm = [None] * depth, [None] * depth, [None] * depth, [None] * depth
    g_wa, g_wb, g_wkv, g_wout = [None] * depth, [None] * depth, [None] * depth, [None] * depth
    g_own = [[None] * depth for _ in big]
    g_other = [[None] * depth for _ in big]

    def reduce_at_owner(lr, chip_sums, from_chips, tiles):
        halves = [_sum_chips(f"grad_sum_chips{lr}_{k}", h_, r_, mvec, t_)
                  for k, (h_, r_, t_) in enumerate(zip(chip_sums, from_chips, tiles))]
        others = _swap_reduced(f"grad_swap_reduced{lr}", halves)
        for k in range(len(big)):
            g_own[k][lr], g_other[k][lr] = halves[k], others[k]

    pending = None
    for l in reversed(range(depth)):
        xin, h, pa, pb, bpad, ccol4, crow4, o_sb, o_fx, lse_fx, mn, mkv, o_m, lse_m, y2 = saved[l]
        wa, wb, wkv, wout = layer_w[l]
        dy, dgate, g_outnorm[l] = _out_bwd(f"out_bwd{l}", dxb, o_sb, o_fx, o_m, pb, out_norm_w[l][None], wout, ts)
        g_wout[l] = _mm(f"dw_out{l}", y2, dxb, "tn", _tile_of(MIX_WIDTH, 640, LANES), d, F32)
        scatter = _scatter_exchange(pending[1]) if pending is not None else None
        (dq_sb, dk_sb, dv_sb), from_chips = _sb_bwd(f"sb_bwd{l}", pa, 0, dy, 0, carried=scatter)
        if pending is not None:
            reduce_at_owner(pending[0], pending[1], from_chips, pending[2])
        dq_fx, dk_fx, dv_fx, cs4 = _fox_bwd(f"fox_bwd{l}", pa, 3 * SB_WIDTH, ccol4, crow4, o_fx, lse_fx, dy, SB_WIDTH)
        colsum = cs4[:, :, :2, :].transpose(1, 0, 2, 3).reshape(nb, 8, TILE)
        dlogit, g_b[l] = _gate_bwd(f"gate_bwd{l}", pb, bpad, colsum, fl_block)
        dq_m, dk_m, dv_m = _mem_bwd(f"mem_bwd{l}", pa, mkv, o_m, lse_m, dy, SB_WIDTH + FOX_WIDTH)
        dmkv = jnp.concatenate([dk_m, dv_m], axis=1)
        g_wkv[l] = _mm(f"dw_kv{l}", mn, dmkv, "tn", d, 2 * MEM_WIDTH, F32)
        dmn = _mm(f"dmem{l}", dmkv, wkv, "nt", mems.shape[0], d, F32)
        g_memnorm[l] = _rms_wgrad(f"mem_norm_grad{l}", mems, dmn)
        dpa = jnp.concatenate([dq_sb, dk_sb, dv_sb, dq_fx, dk_fx, dv_fx, dq_m], axis=1)
        dpb = jnp.concatenate([dgate, dlogit], axis=1)
        tw = _tile_of(d, 512, LANES)
        g_wa[l] = _mm(f"dw_in_a{l}", h, dpa, "tn", tw, _tile_of(PA, 1664, LANES), F32)
        g_wb[l] = _mm(f"dw_in_b{l}", h, dpb, "tn", tw, PB, F32)
        dh = _mm(f"dh_a{l}", dpa, wa, "nt", tm, d, F32)
        dh = _mm(f"dh_b{l}", dpb, wb, "nt", tm, d, F32, res=dh)
        dx, dxb, g_norm[l] = _rms_bwd(f"rms_bwd{l}", xin, norm_w[l][None], dh, dx, ts)
        g4s = [jnp.stack([_shard_from_groups(g_wa[l], g_wb[l], j, cw) for j in range(N_CHIPS)]),
               g_wkv[l].reshape(N_CHIPS, -1, g_wkv[l].shape[1]), g_wout[l].reshape(N_CHIPS, -1, d)]
        tiles = [_tile_of(g.shape[1] // 2, 256, 16) for g in g4s]
        from_sibling = _swap_halves(f"grad_swap_halves{l}", g4s)
        chip_sums = [_add_half(f"grad_add_half{l}_{k}", g, r, cvec, t)
                     for k, (g, r, t) in enumerate(zip(g4s, from_sibling, tiles))]
        pending = (l, chip_sums, tiles)
    reduce_at_owner(pending[0], pending[1], _run_exchange(f"grad_scatter_chips{pending[0]}", _scatter_exchange(pending[1])),
                    pending[2])

    small_w = [norm_w, b_forget, mem_norm_w, out_norm_w, final_norm_w]
    small_m = [m_norm_w, m_b_forget, m_mem_norm_w, m_out_norm_w, m_final_norm_w]
    small_v = [v_norm_w, v_b_forget, v_mem_norm_w, v_out_norm_w, v_final_norm_w]
    rows2 = lambda a: a.reshape(-1, a.shape[-1])
    partials = [jnp.concatenate(g_norm, axis=0), jnp.concatenate(g_b, axis=0), jnp.concatenate(g_memnorm, axis=0),
                jnp.concatenate(g_outnorm, axis=0), g_final, loss_v]
    sums, updates = _small_update("small_update", partials, [rows2(a) for a in small_w] + [None],
                                  [rows2(a) for a in small_m] + [None], [rows2(a) for a in small_v] + [None])
    small_grads = [g.reshape(a.shape) for g, a in zip(sums, small_w)]
    loss = sums[-1][0, 0]
    small_delta, small_m2, small_v2 = ([updates[3 * k + t].reshape(a.shape) for k, a in enumerate(small_w)]
                                       for t in range(3))
    big_grads, big_delta, big_m2, big_v2 = [], [], [], []
    for k, (nm, w_, m_, v_) in enumerate(zip(("w_in", "w_mem_kv", "w_out"), big, (m_w_in, m_w_mem_kv, m_w_out),
                                             (v_w_in, v_w_mem_kv, v_w_out))):
        if w_.shape[2] % LANES:
            g_full = jnp.stack([jnp.concatenate([jnp.where(core == 0, go, gt), jnp.where(core == 0, gt, go)], axis=0)
                                for go, gt in zip(g_own[k], g_other[k])])
            w_p, g_p, m_p, v_p = (_column_major_rows(a) for a in (w_, g_full, m_, v_))
            outs = _adamw(f"adamw_{nm}", w_p, g_p, m_p, v_p, _tile_of(w_p.shape[0], 600, 1))
            outs = [_from_column_major_rows(o, w_.shape) for o in (g_p, *outs)]
        else:
            outs = _adamw_sharded(f"adamw_{nm}", w_, m_, v_, g_own[k], g_other[k], cvec,
                                  _tile_of(w_.shape[1] // 2, 256, 8))
        for lst, o in zip((big_grads, big_delta, big_m2, big_v2), outs):
            lst.append(o)

    def order(sm, bg):
        return [sm[0], bg[0], sm[1], sm[2], bg[1], sm[3], bg[2], sm[4]]

    return (loss, dx[None], *order(small_grads, big_grads), *order(small_delta, big_delta),
            *order(small_m2, big_m2), *order(small_v2, big_v2))
```

```python
import functools

import jax
import jax.numpy as jnp
from jax import lax
from jax.experimental import pallas as pl
from jax.experimental.pallas import tpu as pltpu

F32 = jnp.float32
BF16 = jnp.bfloat16

HEAD_DIM = 64
SB_WIDTH = 512
FOX_WIDTH = 512
FOX_HEADS = 8
MEM_WIDTH = 256
MIX_WIDTH = SB_WIDTH + FOX_WIDTH + MEM_WIDTH
TOTAL_HEADS = MIX_WIDTH // HEAD_DIM
IN_WIDTH = 3 * SB_WIDTH + 3 * FOX_WIDTH + FOX_HEADS + MEM_WIDTH + MIX_WIDTH
LANES = 128
QKV_WIDTH = 3 * SB_WIDTH + 3 * FOX_WIDTH
PA = QKV_WIDTH + MEM_WIDTH
PB = LANES + MIX_WIDTH
EPS = 1e-6
SCALE = HEAD_DIM ** -0.5
TILE = 256
SB_GROUP = 4
SB_LANES = SB_GROUP * HEAD_DIM
FOX_GROUP = 4
FOX_LANES = FOX_GROUP * HEAD_DIM
NEG_INF = float("-inf")
MASKED = -1e30

ADAM_LR = 0.001
ADAM_B1 = 0.9
ADAM_B2 = 0.999
ADAM_EPS = 1e-08
ADAM_WD = 0.01
ADAM_STEP = 10

N_CHIPS = 4
N_DEV = 8
VMEM_LIMIT = 48 * 1024 * 1024
MESH = pl.DeviceIdType.MESH


def _params(*sem):
    return pltpu.CompilerParams(dimension_semantics=tuple(sem), vmem_limit_bytes=VMEM_LIMIT)


def _dot(a, b):
    return jnp.dot(a, b, preferred_element_type=F32)


def _dot_nt(a, b):
    return lax.dot_general(a, b, (((1,), (1,)), ((), ())), preferred_element_type=F32)


def _dot_tn(a, b):
    return lax.dot_general(a, b, (((0,), (0,)), ((), ())), preferred_element_type=F32)


def _split2(x):
    hi = x.astype(BF16)
    lo = (x - hi.astype(F32)).astype(BF16)
    return hi, lo


def _split3(x):
    hi = x.astype(BF16)
    r = x - hi.astype(F32)
    mid = r.astype(BF16)
    lo = (r - mid.astype(F32)).astype(BF16)
    return hi, mid, lo


def _sum_l2(x, u):
    hi, lo = _split2(x)
    return _dot(hi, u) + _dot(lo, u)


def _sum_l3(x, u):
    hi, mid, lo = _split3(x)
    return _dot(hi, u) + _dot(mid, u) + _dot(lo, u)


def _sum_r3(u, x):
    hi, mid, lo = _split3(x)
    return _dot(u, hi) + _dot(u, mid) + _dot(u, lo)


def _softplus(z):
    return jnp.maximum(z, 0.0) + jnp.log1p(jnp.exp(-jnp.abs(z)))


def _tri(n, pred):
    r = lax.broadcasted_iota(jnp.int32, (n, n), 0)
    c = lax.broadcasted_iota(jnp.int32, (n, n), 1)
    return jnp.where(pred(r, c), 1.0, 0.0).astype(BF16)


def _rows(ref, j, n=TILE):
    return pl.ds(pl.multiple_of(j * n, n), n)


def _mm(name, a, b, mode, tm, tn, out_dtype, res=None, a_lead=(), b_lead=()):
    a2, b2 = a.shape[len(a_lead):], b.shape[len(b_lead):]
    if mode == "tn":
        k, m = a2
    else:
        m, k = a2
    n = b2[0] if mode == "nt" else b2[1]
    assert m % tm == 0 and n % tn == 0, (name, m, tm, n, tn)
    na, nb = (None,) * len(a_lead), (None,) * len(b_lead)
    if mode == "tn":
        a_spec = pl.BlockSpec(na + (k, tm), lambda j, i: a_lead + (0, i))
    else:
        a_spec = pl.BlockSpec(na + (tm, k), lambda j, i: a_lead + (i, 0))
    if mode == "nt":
        b_spec = pl.BlockSpec(nb + (tn, k), lambda j, i: b_lead + (j, 0))
    else:
        b_spec = pl.BlockSpec(nb + (k, tn), lambda j, i: b_lead + (0, j))
    o_spec = pl.BlockSpec((tm, tn), lambda j, i: (i, j))
    dot = {"nn": _dot, "nt": _dot_nt, "tn": _dot_tn}[mode]

    def body(a_ref, b_ref, *rest):
        o_ref = rest[-1]
        acc = dot(a_ref[...].astype(BF16), b_ref[...].astype(BF16))
        if res is not None:
            acc = acc + rest[0][...]
        o_ref[...] = acc.astype(o_ref.dtype)

    args, specs = [a, b], [a_spec, b_spec]
    if res is not None:
        args.append(res)
        specs.append(o_spec)
    return pl.pallas_call(
        body, name=name, grid=(n // tn, m // tm), in_specs=specs, out_specs=o_spec,
        out_shape=jax.ShapeDtypeStruct((m, n), out_dtype),
        compiler_params=_params("parallel", "parallel"),
    )(*args)


def _rms_fwd(name, x, g, ts):
    s, d = x.shape

    def body(x_ref, g_ref, o_ref):
        xf = x_ref[...]
        r = lax.rsqrt(jnp.mean(xf * xf, axis=1, keepdims=True) + EPS)
        o_ref[...] = (xf * r * g_ref[...]).astype(BF16)

    return pl.pallas_call(
        body, name=name, grid=(s // ts,),
        in_specs=[pl.BlockSpec((ts, d), lambda i: (i, 0)), pl.BlockSpec((1, d), lambda i: (0, 0))],
        out_specs=pl.BlockSpec((ts, d), lambda i: (i, 0)),
        out_shape=jax.ShapeDtypeStruct((s, d), BF16),
        compiler_params=_params("parallel"),
    )(x, g)


def _rms_bwd(name, x, g, dh, dres, ts):
    s, d = x.shape

    def body(x_ref, g_ref, dh_ref, dres_ref, dx_ref, dxb_ref, dg_ref):
        @pl.when(pl.program_id(0) == 0)
        def _():
            dg_ref[...] = jnp.zeros_like(dg_ref)

        xf = x_ref[...]
        r = lax.rsqrt(jnp.mean(xf * xf, axis=1, keepdims=True) + EPS)
        xh = xf * r
        dhf = dh_ref[...]
        dg_ref[...] += jnp.sum(dhf * xh, axis=0, keepdims=True)
        dxh = dhf * g_ref[...]
        m = jnp.mean(dxh * xh, axis=1, keepdims=True)
        dx = r * (dxh - xh * m) + dres_ref[...]
        dx_ref[...] = dx
        dxb_ref[...] = dx.astype(BF16)

    row = pl.BlockSpec((ts, d), lambda i: (i, 0))
    vec = pl.BlockSpec((1, d), lambda i: (0, 0))
    return pl.pallas_call(
        body, name=name, grid=(s // ts,), in_specs=[row, vec, row, row], out_specs=[row, row, vec],
        out_shape=[jax.ShapeDtypeStruct((s, d), F32), jax.ShapeDtypeStruct((s, d), BF16),
                   jax.ShapeDtypeStruct((1, d), F32)],
        compiler_params=_params("arbitrary"),
    )(x, g, dh, dres)


def _rms_wgrad(name, x, dh):
    m_, d = x.shape

    def body(x_ref, dh_ref, dg_ref):
        xf = x_ref[...]
        r = lax.rsqrt(jnp.mean(xf * xf, axis=1, keepdims=True) + EPS)
        dg_ref[...] = jnp.sum(dh_ref[...] * xf * r, axis=0, keepdims=True)

    return pl.pallas_call(
        body, name=name, out_shape=jax.ShapeDtypeStruct((1, d), F32),
    )(x, dh)


def _final_loss(name, x, g, target, ts):
    s, d = x.shape

    def body(x_ref, g_ref, t_ref, loss_ref, dx_ref, dxb_ref, dg_ref):
        @pl.when(pl.program_id(0) == 0)
        def _():
            dg_ref[...] = jnp.zeros_like(dg_ref)
            loss_ref[...] = jnp.zeros_like(loss_ref)

        xf = x_ref[...]
        gw = g_ref[...]
        r = lax.rsqrt(jnp.mean(xf * xf, axis=1, keepdims=True) + EPS)
        xh = xf * r
        e = xh * gw - t_ref[...]
        part = 0.5 * jnp.sum(jnp.mean(e * e, axis=1, keepdims=True), axis=0, keepdims=True)
        loss_ref[...] += jnp.broadcast_to(part, loss_ref.shape)
        dy = e * (1.0 / d)
        dg_ref[...] += jnp.sum(dy * xh, axis=0, keepdims=True)
        dxh = dy * gw
        m = jnp.mean(dxh * xh, axis=1, keepdims=True)
        dx = r * (dxh - xh * m)
        dx_ref[...] = dx
        dxb_ref[...] = dx.astype(BF16)

    row = pl.BlockSpec((ts, d), lambda i: (i, 0))
    vec = pl.BlockSpec((1, d), lambda i: (0, 0))
    lvec = pl.BlockSpec((1, LANES), lambda i: (0, 0))
    return pl.pallas_call(
        body, name=name, grid=(s // ts,), in_specs=[row, vec, row], out_specs=[lvec, row, row, vec],
        out_shape=[jax.ShapeDtypeStruct((1, LANES), F32), jax.ShapeDtypeStruct((s, d), F32),
                   jax.ShapeDtypeStruct((s, d), BF16), jax.ShapeDtypeStruct((1, d), F32)],
        compiler_params=_params("arbitrary"),
    )(x, g, target)


def _gate_fwd(name, pb, bpad, fl_block):
    s = pb.shape[0]
    nb = s // TILE

    def body(fl_ref, b_ref, ccol_ref, crow_ref, carry):
        @pl.when(pl.program_id(0) == 0)
        def _():
            carry[...] = jnp.zeros_like(carry)

        u = fl_ref[...] + b_ref[...]
        lf = jnp.minimum(u, 0.0) - jnp.log1p(jnp.exp(-jnp.abs(u)))
        lower = _tri(TILE, lambda r, c: c <= r)
        c = _sum_r3(lower, lf) + carry[0:1, :]
        ccol_ref[...] = c
        crow_ref[0] = c.T[0:8, :]
        carry[...] = jnp.broadcast_to(c[TILE - 1:TILE, :], carry.shape)

    return pl.pallas_call(
        body, name=name, grid=(nb,),
        in_specs=[pl.BlockSpec((TILE, LANES), lambda i: (i, fl_block)), pl.BlockSpec((1, LANES), lambda i: (0, 0))],
        out_specs=[pl.BlockSpec((TILE, LANES), lambda i: (i, 0)), pl.BlockSpec((1, 8, TILE), lambda i: (i, 0, 0))],
        out_shape=[jax.ShapeDtypeStruct((s, LANES), F32), jax.ShapeDtypeStruct((nb, 8, TILE), F32)],
        scratch_shapes=[pltpu.VMEM((8, LANES), F32)],
        compiler_params=_params("arbitrary"),
    )(pb, bpad)


def _gate_bwd(name, pb, bpad, colsum, fl_block):
    s = pb.shape[0]
    nb = s // TILE

    def body(fl_ref, b_ref, cs_ref, dl_ref, db_ref, carry):
        @pl.when(pl.program_id(0) == 0)
        def _():
            carry[...] = jnp.zeros_like(carry)
            db_ref[...] = jnp.zeros_like(db_ref)

        upper = _tri(TILE, lambda r, c: r >= c)
        rsum = _sum_l3(cs_ref[0], upper) + carry[:, 0:1]
        carry[...] = jnp.broadcast_to(rsum[:, 0:1], carry.shape)
        full = jnp.concatenate([rsum, jnp.zeros((LANES - 8, TILE), F32)], axis=0)
        dlf = -full.T
        u = fl_ref[...] + b_ref[...]
        dlogit = dlf * (1.0 - jax.nn.sigmoid(u))
        dl_ref[...] = dlogit.astype(BF16)
        db_ref[...] += jnp.sum(dlogit, axis=0, keepdims=True)

    rev = lambda i: (nb - 1 - i, 0)
    return pl.pallas_call(
        body, name=name, grid=(nb,),
        in_specs=[pl.BlockSpec((TILE, LANES), lambda i: (nb - 1 - i, fl_block)),
                  pl.BlockSpec((1, LANES), lambda i: (0, 0)),
                  pl.BlockSpec((1, 8, TILE), lambda i: (nb - 1 - i, 0, 0))],
        out_specs=[pl.BlockSpec((TILE, LANES), rev), pl.BlockSpec((1, LANES), lambda i: (0, 0))],
        out_shape=[jax.ShapeDtypeStruct((s, LANES), BF16), jax.ShapeDtypeStruct((1, LANES), F32)],
        scratch_shapes=[pltpu.VMEM((8, LANES), F32)],
        compiler_params=_params("arbitrary"),
    )(pb, bpad, colsum)


def _head_slices(hh):
    return slice(HEAD_DIM * hh, HEAD_DIM * (hh + 1))


def _scaled_q(q_ref, sl, scale=SCALE):
    return (q_ref[:, sl].astype(F32) * scale).astype(BF16)


def _neg_abs(x):
    sign = jnp.uint32(0x80000000)
    return lax.bitcast_convert_type(lax.bitcast_convert_type(x, jnp.uint32) | sign, F32)


def _sb_tile(qn, kj, carry, strict, u_after, diag):
    nz = _dot_nt(qn, kj)
    lf = jnp.minimum(nz, 0.0) - jnp.log(1.0 + jnp.exp(_neg_abs(nz)))
    lsig = lf - nz
    if diag:
        lf = jnp.where(strict, lf, 0.0)
    sx = _dot(lf.astype(BF16), u_after)
    a = jnp.exp(lsig + sx + carry)
    if diag:
        a = jnp.where(strict, a, 0.0)
    return lsig, a, carry + sx[:, 0:1] + lf[:, 0:1]


def _pair_grid_call(name, body, nb, in_specs, out_specs, out_shape, scratch, args, carried=None, groups=4):
    if carried is None:
        return pl.pallas_call(
            body, name=name, grid=(groups, nb), in_specs=in_specs, out_specs=out_specs, out_shape=out_shape,
            scratch_shapes=scratch, compiler_params=_params("arbitrary", "arbitrary"),
        )(*args)
    n_in, n_out, n_ex = len(in_specs), len(out_specs), carried.n

    def body_with_copies(*refs):
        own_in, ex_in = refs[:n_in], refs[n_in:n_in + n_ex]
        own_out = refs[n_in + n_ex:n_in + n_ex + n_out]
        ex_out = refs[n_in + n_ex + n_out:n_in + 2 * n_ex + n_out]
        own_scratch, sems = refs[n_in + 2 * n_ex + n_out:-2], refs[-2:]
        parts = (ex_in, ex_out, sems[0], sems[1])
        p, i = pl.program_id(0), pl.program_id(1)
        pl.when(jnp.logical_and(p == 0, i == 0))(lambda: carried.begin(*parts))
        if carried.relay is not None:
            pl.when(jnp.logical_and(p == groups - 1, i == max(nb - 2, 0)))(lambda: carried.relay(*parts))
        body(*own_in, *own_out, *own_scratch)
        pl.when(jnp.logical_and(p == groups - 1, i == nb - 1))(lambda: carried.finish(*parts))

    return pl.pallas_call(
        body_with_copies, name=name, grid=(groups, nb), in_specs=list(in_specs) + [HBM_SPEC] * n_ex,
        out_specs=list(out_specs) + [HBM_SPEC] * n_ex, out_shape=list(out_shape) + carried.out_shapes,
        scratch_shapes=list(scratch) + _dma_sems(carried.n_sems),
        compiler_params=_params("arbitrary", "arbitrary"),
    )(*args, *carried.inputs)


def _sb_fwd(name, pa, col0, carried=None):
    s = pa.shape[0]
    nb = s // TILE
    cb = col0 // SB_LANES
    kb = SB_WIDTH // SB_LANES

    def body(q_ref, k_ref, v_ref, o_ref, lsig_s, lf_s):
        i = pl.program_id(1)
        r = lax.broadcasted_iota(jnp.int32, (TILE, TILE), 0)
        c = lax.broadcasted_iota(jnp.int32, (TILE, TILE), 1)
        strict = c < r
        u_after = _tri(TILE, lambda rr, cc: rr > cc)
        qs = [_scaled_q(q_ref, _head_slices(hh), -SCALE) for hh in range(SB_GROUP)]

        def neg_z(j):
            kblk = k_ref[_rows(k_ref, j), :]
            return [_dot_nt(qs[hh], kblk[:, _head_slices(hh)]) for hh in range(SB_GROUP)]

        def scores(nzs, slot, diag):
            for hh, nz in enumerate(nzs):
                lf = jnp.minimum(nz, 0.0) - jnp.log(1.0 + jnp.exp(_neg_abs(nz)))
                lsig = lf - nz
                if diag:
                    lf = jnp.where(strict, lf, 0.0)
                    lsig = jnp.where(strict, lsig, MASKED)
                lsig_s[slot, hh] = lsig
                lf_s[slot, hh] = lf.astype(BF16)

        def weigh(j, slot, state):
            vblk = v_ref[_rows(v_ref, j), :]
            new = []
            for hh in range(SB_GROUP):
                carry, acc = state[hh]
                lfb = lf_s[slot, hh]
                sx = _dot(lfb, u_after)
                a = jnp.exp(lsig_s[slot, hh] + sx + carry)
                new.append((carry + sx[:, 0:1] + lfb[:, 0:1].astype(F32),
                            acc + _dot(a.astype(BF16), vblk[:, _head_slices(hh)])))
            return tuple(new)

        def step(t, state):
            state = weigh(i - t + 1, (t - 1) % 2, state)
            scores(neg_z(i - t), t % 2, False)
            return state

        zero = (jnp.zeros((TILE, 1), F32), jnp.zeros((TILE, HEAD_DIM), F32))
        scores(neg_z(i), 0, True)
        state = lax.fori_loop(1, i + 1, step, (zero,) * SB_GROUP)
        state = weigh(0, i % 2, state)
        o_ref[...] = jnp.concatenate([st[1] for st in state], axis=1)

    outs = _pair_grid_call(
        name, body, nb,
        in_specs=[pl.BlockSpec((TILE, SB_LANES), lambda p, i: (i, cb + p)),
                  pl.BlockSpec((s, SB_LANES), lambda p, i: (0, cb + kb + p)),
                  pl.BlockSpec((s, SB_LANES), lambda p, i: (0, cb + 2 * kb + p))],
        out_specs=[pl.BlockSpec((TILE, SB_LANES), lambda p, i: (i, p))],
        out_shape=[jax.ShapeDtypeStruct((s, SB_WIDTH), F32)],
        scratch=[pltpu.VMEM((2, SB_GROUP, TILE, TILE), F32), pltpu.VMEM((2, SB_GROUP, TILE, TILE), BF16)],
        args=(pa, pa, pa), carried=carried, groups=kb)
    return outs[0], outs[1:]


def _sb_bwd(name, pa, col0, dout, dcol0, carried=None):
    s = pa.shape[0]
    nb = s // TILE
    cb = col0 // SB_LANES
    kb = SB_WIDTH // SB_LANES
    db = dcol0 // SB_LANES

    def body(q_ref, k_ref, v_ref, do_ref, dq_ref, dk_ref, dv_ref, dk_acc, dv_acc, dpan, span, gsum, lsig_s, lf_s):
        i = pl.program_id(1)

        @pl.when(i == 0)
        def _():
            dk_acc[...] = jnp.zeros_like(dk_acc)
            dv_acc[...] = jnp.zeros_like(dv_acc)

        r = lax.broadcasted_iota(jnp.int32, (TILE, TILE), 0)
        c = lax.broadcasted_iota(jnp.int32, (TILE, TILE), 1)
        strict = c < r
        u_after = _tri(TILE, lambda rr, cc: rr > cc)
        u_before = _tri(TILE, lambda rr, cc: rr < cc)
        qs = [_scaled_q(q_ref, _head_slices(hh), -SCALE) for hh in range(SB_GROUP)]
        dos = [do_ref[:, _head_slices(hh)].astype(BF16) for hh in range(SB_GROUP)]

        def scores(j, slot, diag):
            kblk = k_ref[_rows(k_ref, j), :]
            for hh in range(SB_GROUP):
                nz = _dot_nt(qs[hh], kblk[:, _head_slices(hh)])
                lf = jnp.minimum(nz, 0.0) - jnp.log(1.0 + jnp.exp(_neg_abs(nz)))
                lsig = lf - nz
                if diag:
                    lf = jnp.where(strict, lf, 0.0)
                    lsig = jnp.where(strict, lsig, MASKED)
                lsig_s[slot, hh] = lsig
                lf_s[slot, hh] = lf.astype(BF16)

        def grads(j, slot, carries):
            vblk = v_ref[_rows(v_ref, j), :]
            new = []
            for hh in range(SB_GROUP):
                lfb = lf_s[slot, hh]
                lsig = lsig_s[slot, hh]
                sx = _dot(lfb, u_after)
                a = jnp.exp(lsig + sx + carries[hh])
                g = a * _dot_nt(dos[hh], vblk[:, _head_slices(hh)])
                sig = jnp.exp(lsig)
                inside = _dot(g.astype(BF16), u_before)
                dpan[hh, j] = sig * (inside + g) - g
                span[hh, j] = sig
                gsum[hh, j] = inside[:, TILE - 1:TILE] + g[:, TILE - 1:TILE]
                dv_acc[hh, _rows(None, j), :] += _dot_tn(a.astype(BF16), dos[hh])
                new.append(carries[hh] + sx[:, 0:1] + lfb[:, 0:1].astype(F32))
            return tuple(new)

        def step1(t, carries):
            carries = grads(i - t + 1, (t - 1) % 2, carries)
            scores(i - t, t % 2, False)
            return carries

        zero1 = jnp.zeros((TILE, 1), F32)
        scores(i, 0, True)
        carries = lax.fori_loop(1, i + 1, step1, (zero1,) * SB_GROUP)
        grads(0, i % 2, carries)

        def pass2(j, state):
            kblk = k_ref[_rows(k_ref, j), :]
            new = []
            for hh in range(SB_GROUP):
                before, ndq = state[hh]
                ndzb = (dpan[hh, j] + span[hh, j] * before).astype(BF16)
                dk_acc[hh, _rows(None, j), :] += _dot_tn(ndzb, qs[hh])
                new.append((before + gsum[hh, j], ndq + _dot(ndzb, kblk[:, _head_slices(hh)])))
            return tuple(new)

        zero2 = (zero1, jnp.zeros((TILE, HEAD_DIM), F32))
        state = lax.fori_loop(0, i + 1, pass2, (zero2,) * SB_GROUP)
        dq_ref[...] = jnp.concatenate([st[1] * -SCALE for st in state], axis=1).astype(BF16)

        @pl.when(i == nb - 1)
        def _():
            dk_ref[...] = jnp.concatenate([dk_acc[hh] for hh in range(SB_GROUP)], axis=1).astype(BF16)
            dv_ref[...] = jnp.concatenate([dv_acc[hh] for hh in range(SB_GROUP)], axis=1).astype(BF16)

    qspec = pl.BlockSpec((TILE, SB_LANES), lambda p, i: (i, p))
    kvspec = pl.BlockSpec((s, SB_LANES), lambda p, i: (0, p))
    out = jax.ShapeDtypeStruct((s, SB_WIDTH), BF16)
    outs = _pair_grid_call(
        name, body, nb,
        in_specs=[pl.BlockSpec((TILE, SB_LANES), lambda p, i: (i, cb + p)),
                  pl.BlockSpec((s, SB_LANES), lambda p, i: (0, cb + kb + p)),
                  pl.BlockSpec((s, SB_LANES), lambda p, i: (0, cb + 2 * kb + p)),
                  pl.BlockSpec((TILE, SB_LANES), lambda p, i: (i, db + p))],
        out_specs=[qspec, kvspec, kvspec], out_shape=[out, out, out],
        scratch=[pltpu.VMEM((SB_GROUP, s, HEAD_DIM), F32), pltpu.VMEM((SB_GROUP, s, HEAD_DIM), F32),
                 pltpu.VMEM((SB_GROUP, nb, TILE, TILE), F32), pltpu.VMEM((SB_GROUP, nb, TILE, TILE), F32),
                 pltpu.VMEM((SB_GROUP, nb, TILE, 1), F32),
                 pltpu.VMEM((2, SB_GROUP, TILE, TILE), F32), pltpu.VMEM((2, SB_GROUP, TILE, TILE), BF16)],
        args=(pa, pa, pa, dout), carried=carried, groups=kb)
    return outs[:3], outs[3:]


def _fox_scores(q, kj, cq, crj, causal, diag):
    sc = _dot_nt(q, kj) + (cq - crj)
    if diag:
        sc = jnp.where(causal, sc, NEG_INF)
    return sc


def _fox_fwd(name, pa, col0, ccol4, crow4, carried=None):
    s = pa.shape[0]
    nb = s // TILE
    cb = col0 // FOX_LANES
    kb = FOX_WIDTH // FOX_LANES

    def body(q_ref, k_ref, v_ref, cc_ref, cr_ref, o_ref, lse_ref, sc_s):
        i = pl.program_id(1)
        r = lax.broadcasted_iota(jnp.int32, (TILE, TILE), 0)
        c = lax.broadcasted_iota(jnp.int32, (TILE, TILE), 1)
        causal = c <= r
        qs = [_scaled_q(q_ref, _head_slices(hh)) for hh in range(FOX_GROUP)]
        cqs = [cc_ref[:, HEAD_DIM * hh:HEAD_DIM * hh + 1] for hh in range(FOX_GROUP)]

        def logits(j, slot, diag):
            kblk = k_ref[_rows(k_ref, j), :]
            tops = []
            for hh in range(FOX_GROUP):
                sc = _fox_scores(qs[hh], kblk[:, _head_slices(hh)], cqs[hh], cr_ref[j, hh:hh + 1, :], causal, diag)
                sc_s[slot, hh] = sc
                tops.append(jnp.max(sc, axis=1, keepdims=True))
            return tuple(tops)

        def update(j, slot, tops, state):
            vblk = v_ref[_rows(v_ref, j), :]
            new = []
            for hh in range(FOX_GROUP):
                m, l, acc = state[hh]
                m2 = jnp.maximum(m, tops[hh])
                alpha = jnp.exp(m - m2)
                p = jnp.exp(sc_s[slot, hh] - m2)
                new.append((m2, l * alpha + jnp.sum(p, axis=1, keepdims=True),
                            acc * alpha + _dot(p.astype(BF16), vblk[:, _head_slices(hh)])))
            return tuple(new)

        def step(t, both):
            tops, state = both
            state = update(i - t + 1, (t - 1) % 2, tops, state)
            return logits(i - t, t % 2, False), state

        zero = (jnp.full((TILE, 1), NEG_INF, F32), jnp.zeros((TILE, 1), F32), jnp.zeros((TILE, HEAD_DIM), F32))
        tops, state = lax.fori_loop(1, i + 1, step, (logits(i, 0, True), (zero,) * FOX_GROUP))
        state = update(0, i % 2, tops, state)
        o_ref[...] = jnp.concatenate([st[2] / st[1] for st in state], axis=1)
        lse_ref[...] = jnp.concatenate(
            [jnp.broadcast_to(st[0] + jnp.log(st[1]), (TILE, HEAD_DIM)) for st in state], axis=1)

    outs = _pair_grid_call(
        name, body, nb,
        in_specs=[pl.BlockSpec((TILE, FOX_LANES), lambda p, i: (i, cb + p)),
                  pl.BlockSpec((s, FOX_LANES), lambda p, i: (0, cb + kb + p)),
                  pl.BlockSpec((s, FOX_LANES), lambda p, i: (0, cb + 2 * kb + p)),
                  pl.BlockSpec((None, TILE, FOX_LANES), lambda p, i: (p, i, 0)),
                  pl.BlockSpec((None, nb, 8, TILE), lambda p, i: (p, 0, 0, 0))],
        out_specs=[pl.BlockSpec((TILE, FOX_LANES), lambda p, i: (i, p)),
                   pl.BlockSpec((None, TILE, FOX_LANES), lambda p, i: (p, i, 0))],
        out_shape=[jax.ShapeDtypeStruct((s, FOX_WIDTH), F32), jax.ShapeDtypeStruct((kb, s, FOX_LANES), F32)],
        scratch=[pltpu.VMEM((2, FOX_GROUP, TILE, TILE), F32)],
        args=(pa, pa, pa, ccol4, crow4), carried=carried, groups=kb)
    return outs[0], outs[1], outs[2:]


def _fox_bwd(name, pa, col0, ccol4, crow4, out, lse, dout, dcol0):
    s = pa.shape[0]
    nb = s // TILE
    cb = col0 // FOX_LANES
    kb = FOX_WIDTH // FOX_LANES
    db = dcol0 // FOX_LANES

    def body(q_ref, k_ref, v_ref, cc_ref, cr_ref, o_ref, lse_ref, do_ref,
             dq_ref, dk_ref, dv_ref, cs_ref, dk_acc, dv_acc, p_s, ds_s):
        i = pl.program_id(1)

        @pl.when(i == 0)
        def _():
            dk_acc[...] = jnp.zeros_like(dk_acc)
            dv_acc[...] = jnp.zeros_like(dv_acc)
            cs_ref[...] = jnp.zeros_like(cs_ref)

        r = lax.broadcasted_iota(jnp.int32, (TILE, TILE), 0)
        c = lax.broadcasted_iota(jnp.int32, (TILE, TILE), 1)
        causal = c <= r
        qs = [_scaled_q(q_ref, _head_slices(hh)) for hh in range(FOX_GROUP)]
        cqs = [cc_ref[:, HEAD_DIM * hh:HEAD_DIM * hh + 1] for hh in range(FOX_GROUP)]
        lses = [lse_ref[:, HEAD_DIM * hh:HEAD_DIM * hh + 1] for hh in range(FOX_GROUP)]
        dofs = [do_ref[:, _head_slices(hh)] for hh in range(FOX_GROUP)]
        dos = [d_.astype(BF16) for d_ in dofs]
        deltas = [jnp.sum(dofs[hh] * o_ref[:, _head_slices(hh)], axis=1, keepdims=True) for hh in range(FOX_GROUP)]

        def probs(j, slot, rowsums, diag):
            kblk = k_ref[_rows(k_ref, j), :]
            vblk = v_ref[_rows(v_ref, j), :]
            new = []
            for hh in range(FOX_GROUP):
                sl = _head_slices(hh)
                sc = _fox_scores(qs[hh], kblk[:, sl], cqs[hh], cr_ref[j, hh:hh + 1, :], causal, diag)
                p = jnp.exp(sc - lses[hh])
                ds = p * (_dot_nt(dos[hh], vblk[:, sl]) - deltas[hh])
                p_s[slot, hh] = p.astype(BF16)
                ds_s[slot, hh] = ds.astype(BF16)
                cs_ref[j, hh:hh + 1, :] += jnp.sum(ds, axis=0, keepdims=True)
                new.append(rowsums[hh] + jnp.sum(ds, axis=1, keepdims=True))
            return tuple(new)

        def accumulate(j, slot, dqs):
            kblk = k_ref[_rows(k_ref, j), :]
            new = []
            for hh in range(FOX_GROUP):
                dsb = ds_s[slot, hh]
                dv_acc[hh, _rows(None, j), :] += _dot_tn(p_s[slot, hh], dos[hh])
                dk_acc[hh, _rows(None, j), :] += _dot_tn(dsb, qs[hh])
                new.append(dqs[hh] + _dot(dsb, kblk[:, _head_slices(hh)]))
            return tuple(new)

        def step(t, both):
            rowsums, dqs = both
            dqs = accumulate(i - t + 1, (t - 1) % 2, dqs)
            return probs(i - t, t % 2, rowsums, False), dqs

        zero1 = jnp.zeros((TILE, 1), F32)
        zero64 = jnp.zeros((TILE, HEAD_DIM), F32)
        rowsums, dqs = lax.fori_loop(1, i + 1, step,
                                     (probs(i, 0, (zero1,) * FOX_GROUP, True), (zero64,) * FOX_GROUP))
        dqs = accumulate(0, i % 2, dqs)
        for hh in range(FOX_GROUP):
            cs_ref[i, hh:hh + 1, :] -= jnp.broadcast_to(rowsums[hh], (TILE, LANES)).T[0:1, :]
        dq_ref[...] = jnp.concatenate([dq * SCALE for dq in dqs], axis=1).astype(BF16)

        @pl.when(i == nb - 1)
        def _():
            dk_ref[...] = jnp.concatenate([dk_acc[hh] for hh in range(FOX_GROUP)], axis=1).astype(BF16)
            dv_ref[...] = jnp.concatenate([dv_acc[hh] for hh in range(FOX_GROUP)], axis=1).astype(BF16)

    qspec = pl.BlockSpec((TILE, FOX_LANES), lambda p, i: (i, p))
    kvspec = pl.BlockSpec((s, FOX_LANES), lambda p, i: (0, p))
    o3 = jax.ShapeDtypeStruct((s, FOX_WIDTH), BF16)
    return pl.pallas_call(
        body, name=name, grid=(kb, nb),
        in_specs=[pl.BlockSpec((TILE, FOX_LANES), lambda p, i: (i, cb + p)),
                  pl.BlockSpec((s, FOX_LANES), lambda p, i: (0, cb + kb + p)),
                  pl.BlockSpec((s, FOX_LANES), lambda p, i: (0, cb + 2 * kb + p)),
                  pl.BlockSpec((None, TILE, FOX_LANES), lambda p, i: (p, i, 0)),
                  pl.BlockSpec((None, nb, 8, TILE), lambda p, i: (p, 0, 0, 0)),
                  qspec,
                  pl.BlockSpec((None, TILE, FOX_LANES), lambda p, i: (p, i, 0)),
                  pl.BlockSpec((TILE, FOX_LANES), lambda p, i: (i, db + p))],
        out_specs=[qspec, kvspec, kvspec, pl.BlockSpec((None, nb, 8, TILE), lambda p, i: (p, 0, 0, 0))],
        out_shape=[o3, o3, o3, jax.ShapeDtypeStruct((kb, nb, 8, TILE), F32)],
        scratch_shapes=[pltpu.VMEM((FOX_GROUP, s, HEAD_DIM), F32), pltpu.VMEM((FOX_GROUP, s, HEAD_DIM), F32),
                        pltpu.VMEM((2, FOX_GROUP, TILE, TILE), BF16), pltpu.VMEM((2, FOX_GROUP, TILE, TILE), BF16)],
        compiler_params=_params("arbitrary", "arbitrary"),
    )(pa, pa, pa, ccol4, crow4, out, lse, dout)


def _mem_fwd(name, pa, mkv):
    s = pa.shape[0]
    ml = mkv.shape[0]
    nb = s // TILE
    cb = QKV_WIDTH // LANES

    def body(q_ref, k_ref, v_ref, o_ref, lse_ref):
        outs, lses = [], []
        for hh in range(2):
            sl = _head_slices(hh)
            sc = _dot_nt(_scaled_q(q_ref, sl), k_ref[:, sl])
            m = jnp.max(sc, axis=1, keepdims=True)
            p = jnp.exp(sc - m)
            l = jnp.sum(p, axis=1, keepdims=True)
            outs.append(_dot(p.astype(BF16), v_ref[:, sl]) / l)
            lses.append(jnp.broadcast_to(m + jnp.log(l), (TILE, HEAD_DIM)))
        o_ref[...] = jnp.concatenate(outs, axis=1)
        lse_ref[...] = jnp.concatenate(lses, axis=1)

    return pl.pallas_call(
        body, name=name, grid=(2, nb),
        in_specs=[pl.BlockSpec((TILE, LANES), lambda p, i: (i, cb + p)),
                  pl.BlockSpec((ml, LANES), lambda p, i: (0, p)),
                  pl.BlockSpec((ml, LANES), lambda p, i: (0, 2 + p))],
        out_specs=[pl.BlockSpec((TILE, LANES), lambda p, i: (i, p)),
                   pl.BlockSpec((None, TILE, LANES), lambda p, i: (p, i, 0))],
        out_shape=[jax.ShapeDtypeStruct((s, MEM_WIDTH), F32), jax.ShapeDtypeStruct((2, s, LANES), F32)],
        compiler_params=_params("parallel", "parallel"),
    )(pa, mkv, mkv)


def _mem_bwd(name, pa, mkv, out, lse, dout, dcol0):
    s = pa.shape[0]
    ml = mkv.shape[0]
    nb = s // TILE
    cb = QKV_WIDTH // LANES
    db = dcol0 // LANES

    def body(q_ref, k_ref, v_ref, o_ref, lse_ref, do_ref, dq_ref, dk_ref, dv_ref, dk_acc, dv_acc):
        i = pl.program_id(1)

        @pl.when(i == 0)
        def _():
            dk_acc[...] = jnp.zeros_like(dk_acc)
            dv_acc[...] = jnp.zeros_like(dv_acc)

        dqs = []
        for hh in range(2):
            sl = _head_slices(hh)
            q = _scaled_q(q_ref, sl)
            kh = k_ref[:, sl]
            dof = do_ref[:, sl]
            do = dof.astype(BF16)
            delta = jnp.sum(dof * o_ref[:, sl], axis=1, keepdims=True)
            p = jnp.exp(_dot_nt(q, kh) - lse_ref[:, HEAD_DIM * hh:HEAD_DIM * hh + 1])
            ds = (p * (_dot_nt(do, v_ref[:, sl]) - delta)).astype(BF16)
            dv_acc[hh] += _dot_tn(p.astype(BF16), do)
            dk_acc[hh] += _dot_tn(ds, q)
            dqs.append(_dot(ds, kh) * SCALE)
        dq_ref[...] = jnp.concatenate(dqs, axis=1).astype(BF16)

        @pl.when(i == nb - 1)
        def _():
            dk_ref[...] = jnp.concatenate([dk_acc[0], dk_acc[1]], axis=1).astype(BF16)
            dv_ref[...] = jnp.concatenate([dv_acc[0], dv_acc[1]], axis=1).astype(BF16)

    qspec = pl.BlockSpec((TILE, LANES), lambda p, i: (i, p))
    kvspec = pl.BlockSpec((ml, LANES), lambda p, i: (0, p))
    okv = jax.ShapeDtypeStruct((ml, MEM_WIDTH), BF16)
    return pl.pallas_call(
        body, name=name, grid=(2, nb),
        in_specs=[pl.BlockSpec((TILE, LANES), lambda p, i: (i, cb + p)),
                  pl.BlockSpec((ml, LANES), lambda p, i: (0, p)),
                  pl.BlockSpec((ml, LANES), lambda p, i: (0, 2 + p)),
                  qspec,
                  pl.BlockSpec((None, TILE, LANES), lambda p, i: (p, i, 0)),
                  pl.BlockSpec((TILE, LANES), lambda p, i: (i, db + p))],
        out_specs=[qspec, kvspec, kvspec],
        out_shape=[jax.ShapeDtypeStruct((s, MEM_WIDTH), BF16), okv, okv],
        scratch_shapes=[pltpu.VMEM((2, ml, HEAD_DIM), F32), pltpu.VMEM((2, ml, HEAD_DIM), F32)],
        compiler_params=_params("arbitrary", "arbitrary"),
    )(pa, mkv, mkv, out, lse, dout)


def _head_maps():
    col = jnp.arange(MIX_WIDTH)[:, None] // HEAD_DIM
    g = (col == jnp.arange(LANES)[None, :]).astype(BF16)
    return g, g.T


def _normed_heads(osb_ref, ofx_ref, om_ref, g_ref, gt_ref):
    y = jnp.concatenate([osb_ref[...], ofx_ref[...], om_ref[...]], axis=1)
    msq = _sum_l2(y * y, g_ref[...]) * (1.0 / HEAD_DIM)
    rf = _sum_l3(lax.rsqrt(msq + EPS), gt_ref[...])
    return y * rf, rf


def _out_fwd(name, o_sb, o_fx, o_m, pb, ow, x, w_out, ts):
    s, d = x.shape
    g, gt = _head_maps()

    def body(osb_ref, ofx_ref, om_ref, gate_ref, ow_ref, x_ref, w_ref, g_ref, gt_ref, xo_ref, y2_ref):
        yh, _ = _normed_heads(osb_ref, ofx_ref, om_ref, g_ref, gt_ref)
        gate = gate_ref[...]
        y2 = (yh * ow_ref[...] * (gate * jax.nn.sigmoid(gate))).astype(BF16)
        y2_ref[...] = y2
        xo_ref[...] = x_ref[...] + _dot(y2, w_ref[...])

    return pl.pallas_call(
        body, name=name, grid=(s // ts,),
        in_specs=[_row_spec(ts, SB_WIDTH), _row_spec(ts, FOX_WIDTH), _row_spec(ts, MEM_WIDTH),
                  _row_spec(ts, MIX_WIDTH), _const_spec((1, MIX_WIDTH)), _row_spec(ts, d),
                  _const_spec((MIX_WIDTH, d)),
                  _const_spec((MIX_WIDTH, LANES)), _const_spec((LANES, MIX_WIDTH))],
        out_specs=[_row_spec(ts, d), _row_spec(ts, MIX_WIDTH)],
        out_shape=[jax.ShapeDtypeStruct((s, d), F32), jax.ShapeDtypeStruct((s, MIX_WIDTH), BF16)],
        compiler_params=_params("parallel"),
    )(o_sb, o_fx, o_m, pb, ow, x, w_out, g, gt)


def _row_spec(ts, w):
    return pl.BlockSpec((ts, w), lambda i: (i, 0))


def _const_spec(shape):
    return pl.BlockSpec(shape, lambda i: (0,) * len(shape))


def _out_bwd(name, dxb, o_sb, o_fx, o_m, pb, ow, w_out, ts):
    s, d = dxb.shape
    g, gt = _head_maps()

    def body(dx_ref, osb_ref, ofx_ref, om_ref, gate_ref, ow_ref, w_ref, g_ref, gt_ref, dy_ref, dgate_ref, dow_ref):
        @pl.when(pl.program_id(0) == 0)
        def _():
            dow_ref[...] = jnp.zeros_like(dow_ref)

        dy2 = _dot_nt(dx_ref[...], w_ref[...])
        yh, rf = _normed_heads(osb_ref, ofx_ref, om_ref, g_ref, gt_ref)
        gate = gate_ref[...]
        sig = jax.nn.sigmoid(gate)
        ow_v = ow_ref[...]
        dgate_ref[...] = (dy2 * (yh * ow_v) * (sig * (1.0 + gate * (1.0 - sig)))).astype(BF16)
        dn = dy2 * (gate * sig)
        dow_ref[...] += jnp.sum(dn * yh, axis=0, keepdims=True)
        dyh = dn * ow_v
        t = _sum_l2(dyh * yh, g_ref[...]) * (1.0 / HEAD_DIM)
        dy_ref[...] = rf * (dyh - yh * _sum_l3(t, gt_ref[...]))

    return pl.pallas_call(
        body, name=name, grid=(s // ts,),
        in_specs=[_row_spec(ts, d), _row_spec(ts, SB_WIDTH), _row_spec(ts, FOX_WIDTH), _row_spec(ts, MEM_WIDTH),
                  _row_spec(ts, MIX_WIDTH), _const_spec((1, MIX_WIDTH)),
                  _const_spec((MIX_WIDTH, d)),
                  _const_spec((MIX_WIDTH, LANES)), _const_spec((LANES, MIX_WIDTH))],
        out_specs=[_row_spec(ts, MIX_WIDTH), _row_spec(ts, MIX_WIDTH), _const_spec((1, MIX_WIDTH))],
        out_shape=[jax.ShapeDtypeStruct((s, MIX_WIDTH), F32), jax.ShapeDtypeStruct((s, MIX_WIDTH), BF16),
                   jax.ShapeDtypeStruct((1, MIX_WIDTH), F32)],
        compiler_params=_params("arbitrary"),
    )(dxb, o_sb, o_fx, o_m, pb, ow, w_out, g, gt)


def _adamw(name, w, g, m, v, tr):
    def body(w_ref, g_ref, m_ref, v_ref, d_ref, m2_ref, v2_ref):
        gv = g_ref[...]
        m2 = ADAM_B1 * m_ref[...] + (1.0 - ADAM_B1) * gv
        v2 = ADAM_B2 * v_ref[...] + (1.0 - ADAM_B2) * (gv * gv)
        m_hat = m2 / (1.0 - ADAM_B1 ** ADAM_STEP)
        v_hat = v2 / (1.0 - ADAM_B2 ** ADAM_STEP)
        d_ref[...] = -ADAM_LR * (m_hat / (jnp.sqrt(v_hat) + ADAM_EPS) + ADAM_WD * w_ref[...])
        m2_ref[...] = m2
        v2_ref[...] = v2

    rest = w.shape[1:]
    spec = pl.BlockSpec((tr,) + rest, lambda i: (i,) + (0,) * len(rest))
    shp = jax.ShapeDtypeStruct(w.shape, F32)
    return pl.pallas_call(
        body, name=name, grid=(w.shape[0] // tr,), in_specs=[spec] * 4, out_specs=[spec] * 3, out_shape=[shp] * 3,
        compiler_params=_params("parallel"),
    )(w, g, m, v)


def _adamw_sharded(name, w, m, v, g_own, g_other, cvec, tr):
    depth, rows, cols = w.shape
    nt = rows // 2 // tr

    def body(c_ref, w_ref, m_ref, v_ref, *rest):
        g_refs, (g_ref, d_ref, m2_ref, v2_ref) = rest[:2 * depth], rest[2 * depth:]
        layer, mine = pl.program_id(0), pl.program_id(1) == c_ref[0]
        gv = None
        for lt in range(depth):
            cand = jnp.where(mine, g_refs[lt][...], g_refs[depth + lt][...])
            gv = cand if gv is None else jnp.where(layer == lt, cand, gv)
        m2 = ADAM_B1 * m_ref[...] + (1.0 - ADAM_B1) * gv
        v2 = ADAM_B2 * v_ref[...] + (1.0 - ADAM_B2) * (gv * gv)
        m_hat = m2 / (1.0 - ADAM_B1 ** ADAM_STEP)
        v_hat = v2 / (1.0 - ADAM_B2 ** ADAM_STEP)
        g_ref[...] = gv
        d_ref[...] = -ADAM_LR * (m_hat / (jnp.sqrt(v_hat) + ADAM_EPS) + ADAM_WD * w_ref[...])
        m2_ref[...] = m2
        v2_ref[...] = v2

    def g_map(lt, own):
        def index(l, hf, i, c_ref):
            use = jnp.logical_and(l == lt, (hf == c_ref[0]) == own)
            return jnp.where(use, i, 0), 0
        return index

    full = pl.BlockSpec((None, tr, cols), lambda l, hf, i, c_ref: (l, hf * nt + i, 0))
    g_specs = [pl.BlockSpec((tr, cols), g_map(lt, own)) for own in (True, False) for lt in range(depth)]
    shp = jax.ShapeDtypeStruct((depth, rows, cols), F32)
    return pl.pallas_call(
        body, name=name,
        grid_spec=pltpu.PrefetchScalarGridSpec(
            num_scalar_prefetch=1, grid=(depth, 2, nt), in_specs=[full] * 3 + g_specs, out_specs=[full] * 4),
        out_shape=[shp] * 4,
        compiler_params=_params("arbitrary", "arbitrary", "arbitrary"),
    )(cvec, w, m, v, *g_own, *g_other)


HBM_SPEC = pl.BlockSpec(memory_space=pltpu.HBM)


def _place():
    x, y, c = lax.axis_index("x"), lax.axis_index("y"), lax.axis_index("c")
    chips = [(1 - x, y), (x, 1 - y), (1 - x, 1 - y)]
    return x, y, c, chips


def _remote(src, dst, send_sems, recv_sems, k, to):
    return pltpu.make_async_remote_copy(src_ref=src, dst_ref=dst, send_sem=send_sems.at[k], recv_sem=recv_sems.at[k],
                                        device_id=to, device_id_type=MESH)


def _half_rows(n_rows, cc):
    rh = n_rows // 2
    return pl.ds(pl.multiple_of(cc * rh, 16), rh)


def _dma_sems(n):
    return [pltpu.SemaphoreType.DMA((n,)), pltpu.SemaphoreType.DMA((n,))]


class _Exchange:
    def __init__(self, inputs, out_shapes, n_sems, begin, relay, finish):
        self.inputs, self.out_shapes, self.n_sems = list(inputs), list(out_shapes), n_sems
        self.begin, self.relay, self.finish = begin, relay, finish

    @property
    def n(self):
        return len(self.inputs)

    def split(self, refs):
        return refs[:self.n], refs[self.n:2 * self.n], refs[2 * self.n], refs[2 * self.n + 1]


def _run_exchange(name, ex):
    def body(*refs):
        parts = ex.split(refs)
        for phase in (ex.begin, ex.relay, ex.finish):
            if phase is not None:
                phase(*parts)

    return pl.pallas_call(
        body, name=name, in_specs=[HBM_SPEC] * ex.n, out_specs=[HBM_SPEC] * ex.n, out_shape=ex.out_shapes,
        scratch_shapes=_dma_sems(ex.n_sems),
    )(*ex.inputs)


def _gather_exchange(shards):
    def ici(in_refs, out_refs, send_sems, recv_sems):
        x, y, c, chips = _place()
        return [_remote(in_ref.at[_half_rows(in_ref.shape[0], c)], out_ref.at[2 * x + y, _half_rows(in_ref.shape[0], c)],
                        send_sems, recv_sems, 6 * a + j, (cx, cy, c))
                for a, (in_ref, out_ref) in enumerate(zip(in_refs, out_refs)) for j, (cx, cy) in enumerate(chips)]

    def d2d(out_refs, send_sems, recv_sems, half_of):
        x, y, c, chips = _place()
        cps = []
        for a, out_ref in enumerate(out_refs):
            for j, (cx, cy) in enumerate(chips):
                piece = out_ref.at[2 * cx + cy, _half_rows(out_ref.shape[1], half_of(c))]
                cps.append(_remote(piece, piece, send_sems, recv_sems, 6 * a + 3 + j, (x, y, 1 - c)))
        return cps

    def begin(in_refs, out_refs, send_sems, recv_sems):
        for cp in ici(in_refs, out_refs, send_sems, recv_sems):
            cp.start()

    def relay(in_refs, out_refs, send_sems, recv_sems):
        x, y, c, chips = _place()
        for a, out_ref in enumerate(out_refs):
            for j, (cx, cy) in enumerate(chips):
                landed = out_ref.at[2 * cx + cy, _half_rows(out_ref.shape[1], c)]
                _remote(landed, landed, send_sems, recv_sems, 6 * a + j, (cx, cy, c)).wait_recv()
        for cp in d2d(out_refs, send_sems, recv_sems, lambda c_: c_):
            cp.start()

    def finish(in_refs, out_refs, send_sems, recv_sems):
        for cp in d2d(out_refs, send_sems, recv_sems, lambda c_: 1 - c_):
            cp.wait_recv()
        for cp in ici(in_refs, out_refs, send_sems, recv_sems) + d2d(out_refs, send_sems, recv_sems, lambda c_: c_):
            cp.wait_send()

    shapes = [jax.ShapeDtypeStruct((N_CHIPS,) + s_.shape, s_.dtype) for s_ in shards]
    return _Exchange(shards, shapes, 6 * len(shards), begin, relay, finish)


def _swap_halves(name, g4s):
    n = len(g4s)

    def body(*refs):
        in_refs, out_refs, (send_sems, recv_sems) = refs[:n], refs[n:2 * n], refs[2 * n:]
        x, y, c, _ = _place()
        cps = [_remote(in_ref.at[:, _half_rows(in_ref.shape[1], 1 - c), :], out_ref, send_sems, recv_sems, a, (x, y, 1 - c))
               for a, (in_ref, out_ref) in enumerate(zip(in_refs, out_refs))]
        for cp in cps:
            cp.start()
        for cp in cps:
            cp.wait()

    return pl.pallas_call(
        body, name=name, in_specs=[HBM_SPEC] * n, out_specs=[HBM_SPEC] * n,
        out_shape=[jax.ShapeDtypeStruct((g.shape[0], g.shape[1] // 2, g.shape[2]), g.dtype) for g in g4s],
        scratch_shapes=_dma_sems(n),
    )(*g4s)


def _add_half(name, g4, r1, cvec, tr):
    n, r, w = g4.shape
    rh = r // 2
    nblk = rh // tr

    def body(c_ref, a_ref, b_ref, o_ref):
        o_ref[...] = (a_ref[...] + b_ref[...]).astype(BF16)

    return pl.pallas_call(
        body, name=name,
        grid_spec=pltpu.PrefetchScalarGridSpec(
            num_scalar_prefetch=1, grid=(n, nblk),
            in_specs=[pl.BlockSpec((None, tr, w), lambda k, i, c_ref: (k, c_ref[0] * nblk + i, 0)),
                      pl.BlockSpec((None, tr, w), lambda k, i, c_ref: (k, i, 0))],
            out_specs=pl.BlockSpec((None, tr, w), lambda k, i, c_ref: (k, i, 0))),
        out_shape=jax.ShapeDtypeStruct((n, rh, w), BF16),
        compiler_params=_params("parallel", "parallel"),
    )(cvec, g4, r1)


def _scatter_exchange(h4s):
    def sends(in_refs, out_refs, send_sems, recv_sems):
        x, y, c, chips = _place()
        return [_remote(in_ref.at[2 * cx + cy], out_ref.at[j], send_sems, recv_sems, 3 * a + j, (cx, cy, c))
                for a, (in_ref, out_ref) in enumerate(zip(in_refs, out_refs)) for j, (cx, cy) in enumerate(chips)]

    def begin(*parts):
        for cp in sends(*parts):
            cp.start()

    def finish(in_refs, out_refs, send_sems, recv_sems):
        x, y, c, chips = _place()
        for a, out_ref in enumerate(out_refs):
            for j, (cx, cy) in enumerate(chips):
                got = out_ref.at[j]
                _remote(got, got, send_sems, recv_sems, 3 * a + j, (cx, cy, c)).wait_recv()
        for cp in sends(in_refs, out_refs, send_sems, recv_sems):
            cp.wait_send()

    shapes = [jax.ShapeDtypeStruct((3,) + h.shape[1:], h.dtype) for h in h4s]
    return _Exchange(h4s, shapes, 3 * len(h4s), begin, None, finish)


def _sum_chips(name, h4, r3, mvec, tr):
    _, rh, w = h4.shape

    def body(m_ref, a_ref, b_ref, c_ref, d_ref, o_ref):
        o_ref[...] = ((a_ref[...].astype(F32) + b_ref[...].astype(F32)) + c_ref[...].astype(F32)) + d_ref[...].astype(F32)

    specs = [pl.BlockSpec((None, tr, w), lambda i, m_ref: (m_ref[0], i, 0))]
    specs += [pl.BlockSpec((None, tr, w), functools.partial(lambda k, i, m_ref: (k, i, 0), k)) for k in range(3)]
    return pl.pallas_call(
        body, name=name,
        grid_spec=pltpu.PrefetchScalarGridSpec(
            num_scalar_prefetch=1, grid=(rh // tr,), in_specs=specs,
            out_specs=pl.BlockSpec((tr, w), lambda i, m_ref: (i, 0))),
        out_shape=jax.ShapeDtypeStruct((rh, w), F32),
        compiler_params=_params("parallel"),
    )(mvec, h4, r3, r3, r3)


def _swap_reduced(name, ghs):
    n = len(ghs)

    def body(*refs):
        in_refs, out_refs, (send_sems, recv_sems) = refs[:n], refs[n:2 * n], refs[2 * n:]
        x, y, c, _ = _place()
        cps = [_remote(in_ref, out_ref, send_sems, recv_sems, a, (x, y, 1 - c))
               for a, (in_ref, out_ref) in enumerate(zip(in_refs, out_refs))]
        for cp in cps:
            cp.start()
        for cp in cps:
            cp.wait()

    return pl.pallas_call(
        body, name=name, in_specs=[HBM_SPEC] * n, out_specs=[HBM_SPEC] * n,
        out_shape=[jax.ShapeDtypeStruct(g.shape, g.dtype) for g in ghs],
        scratch_shapes=_dma_sems(n),
    )(*ghs)


def _small_update(name, partials, weights, moments1, moments2):
    n = len(partials)
    width = max(p.shape[1] for p in partials)
    starts, at = [], 0
    for p in partials:
        starts.append(at)
        at += p.shape[0]
    rows = -(-at // 8) * 8
    has_w = [w is not None for w in weights]
    n_w = sum(has_w)

    def body(*refs):
        p_refs = refs[:n]
        w_refs, m_refs, v_refs = refs[n:n + n_w], refs[n + n_w:n + 2 * n_w], refs[n + 2 * n_w:n + 3 * n_w]
        outs = refs[n + 3 * n_w:-4]
        g_refs, upd_refs = outs[:n], outs[n:]
        vec, buf, send_sems, recv_sems = refs[-4:]
        x, y, c, _ = _place()
        me = 4 * x + 2 * y + c
        vec[...] = jnp.zeros_like(vec)
        for p_ref, r0 in zip(p_refs, starts):
            vec[r0:r0 + p_ref.shape[0], 0:p_ref.shape[1]] = p_ref[...]
        buf[me] = vec[...]
        flips = [(fx, fy, fc) for fx in (0, 1) for fy in (0, 1) for fc in (0, 1)][1:]
        peers = [(x + fx - 2 * x * fx, y + fy - 2 * y * fy, c + fc - 2 * c * fc) for fx, fy, fc in flips]
        sends = [_remote(vec, buf.at[me], send_sems, recv_sems, k, peer) for k, peer in enumerate(peers)]
        for cp in sends:
            cp.start()
        for k, (px, py, pc) in enumerate(peers):
            got = buf.at[4 * px + 2 * py + pc]
            _remote(got, got, send_sems, recv_sems, k, (px, py, pc)).wait_recv()
        for cp in sends:
            cp.wait_send()
        total = buf[0]
        for dev in range(1, N_DEV):
            total = total + buf[dev]
        k = 0
        for a in range(n):
            r, w = g_refs[a].shape
            g = total[starts[a]:starts[a] + r, 0:w]
            g_refs[a][...] = g
            if has_w[a]:
                m2 = ADAM_B1 * m_refs[k][...] + (1.0 - ADAM_B1) * g
                v2 = ADAM_B2 * v_refs[k][...] + (1.0 - ADAM_B2) * (g * g)
                m_hat = m2 / (1.0 - ADAM_B1 ** ADAM_STEP)
                v_hat = v2 / (1.0 - ADAM_B2 ** ADAM_STEP)
                upd_refs[3 * k][...] = -ADAM_LR * (m_hat / (jnp.sqrt(v_hat) + ADAM_EPS) + ADAM_WD * w_refs[k][...])
                upd_refs[3 * k + 1][...] = m2
                upd_refs[3 * k + 2][...] = v2
                k += 1

    ws = [w for w in weights if w is not None]
    g_shapes = [jax.ShapeDtypeStruct(p.shape if w is None else w.shape, F32) for p, w in zip(partials, weights)]
    u_shapes = [jax.ShapeDtypeStruct(w.shape, F32) for w in ws for _ in range(3)]
    vm = pl.BlockSpec(memory_space=pltpu.VMEM)
    n_args = n + 3 * n_w
    outs = pl.pallas_call(
        body, name=name, in_specs=[vm] * n_args, out_specs=[vm] * (n + 3 * n_w), out_shape=g_shapes + u_shapes,
        scratch_shapes=[pltpu.VMEM((rows, width), F32), pltpu.VMEM((N_DEV, rows, width), F32),
                        pltpu.SemaphoreType.DMA((7,)), pltpu.SemaphoreType.DMA((7,))],
    )(*partials, *ws, *[m for m in moments1 if m is not None], *[v for v in moments2 if v is not None])
    return outs[:n], outs[n:]


GATE_COL = 3 * SB_WIDTH + 3 * FOX_WIDTH + FOX_HEADS + MEM_WIDTH
FL_COL = QKV_WIDTH


GROUP_A_COLS = [(0, QKV_WIDTH), (FL_COL + FOX_HEADS, MEM_WIDTH)]
GROUP_B_COLS = [(GATE_COL, MIX_WIDTH), (FL_COL, FOX_HEADS)]


def _group_from_shards(shard_of, cw, spans, pad):
    parts = []
    for lo, width in spans:
        hi = lo + width
        for j in range(N_CHIPS):
            a, b = max(lo, j * cw), min(hi, (j + 1) * cw)
            if a < b:
                parts.append(shard_of(j)[:, a - j * cw:b - j * cw])
    if pad:
        parts.append(jnp.zeros((parts[0].shape[0], pad), parts[0].dtype))
    return jnp.concatenate(parts, axis=1)


def _shard_from_groups(ga, gb, j, cw):
    lo, hi = j * cw, (j + 1) * cw
    placed = []
    for grp, spans in ((ga, GROUP_A_COLS), (gb, GROUP_B_COLS)):
        at = 0
        for first, width in spans:
            a, b = max(lo, first), min(hi, first + width)
            if a < b:
                placed.append((a, grp[:, at + a - first:at + b - first]))
            at += width
    return jnp.concatenate([p for _, p in sorted(placed, key=lambda t: t[0])], axis=1)


def _tile_of(n, cap, unit):
    if n <= cap:
        return n
    best = None
    for t in range(unit, cap + 1, unit):
        if n % t == 0:
            best = t
    assert best is not None, (n, cap, unit)
    return best


def _column_major_rows(a):
    dp, r, c = a.shape
    return a.transpose(2, 0, 1).reshape(c, dp, r // LANES, LANES).transpose(0, 2, 1, 3).reshape(-1, 8, LANES)


def _from_column_major_rows(b, shape):
    dp, r, c = shape
    return b.reshape(c, r // LANES, dp, LANES).transpose(0, 2, 1, 3).reshape(c, dp, r).transpose(1, 2, 0)


def _pack_small(parts):
    rows = []
    for p in parts:
        f = p.reshape(-1).astype(F32)
        f = jnp.pad(f, (0, (-f.shape[0]) % LANES))
        rows.append(f.reshape(-1, LANES))
    out = jnp.concatenate(rows, axis=0)
    return jnp.pad(out, ((0, (-out.shape[0]) % 8), (0, 0)))


def _unpack_small(packed, shapes):
    outs, r = [], 0
    for shp in shapes:
        n = 1
        for s_ in shp:
            n *= s_
        nr = -(-n // LANES)
        outs.append(packed[r:r + nr].reshape(-1)[:n].reshape(shp))
        r += nr
    return outs


def kernel(x, mem, norm_w, w_in, b_forget, mem_norm_w, w_mem_kv, out_norm_w, w_out, final_norm_w, loss_target, m_norm_w, m_w_in, m_b_forget, m_mem_norm_w, m_w_mem_kv, m_out_norm_w, m_w_out, m_final_norm_w, v_norm_w, v_w_in, v_b_forget, v_mem_norm_w, v_w_mem_kv, v_out_norm_w, v_w_out, v_final_norm_w):
    xs = x[0]
    mems = mem[0]
    target = loss_target[0]
    s, d = xs.shape
    depth = norm_w.shape[0]
    nb = s // TILE
    ts = _tile_of(s, 256, 8)
    big = (w_in, w_mem_kv, w_out)
    core = lax.axis_index("c")
    chip = 2 * lax.axis_index("x") + lax.axis_index("y")
    cvec = core.astype(jnp.int32).reshape(1)
    mvec = chip.astype(jnp.int32).reshape(1)
    cw = w_in.shape[2]

    own_w = [[a[l].astype(BF16) for a in big] for l in range(depth)]

    def lay_out(own, got):
        full = [jnp.where(lax.broadcasted_iota(jnp.int32, g.shape, 0) == chip, o[None], g)
                for g, o in zip(got[1:], own[1:])]
        shard_of = lambda j: jnp.where(chip == j, own[0], got[0][j])
        wa_l = _group_from_shards(shard_of, cw, GROUP_A_COLS, 0)
        wb_l = _group_from_shards(shard_of, cw, GROUP_B_COLS, LANES - FOX_HEADS)
        return wa_l, wb_l, full[0].reshape(-1, full[0].shape[2]), full[1].reshape(-1, full[1].shape[2])

    layer_w = [lay_out(own_w[0], _run_exchange("gather_weights0", _gather_exchange(own_w[0])))]

    tm = _tile_of(s, 256, 8)
    fl_block = MIX_WIDTH // LANES

    saved = []
    cur = xs
    for l in range(depth):
        wa, wb, wkv, wout = layer_w[l]
        h = _rms_fwd(f"rms_fwd{l}", cur, norm_w[l][None], ts)
        pa = _mm(f"inproj_a{l}", h, wa, "nn", tm, _tile_of(PA, 1664, LANES), BF16)
        pb = _mm(f"inproj_b{l}", h, wb, "nn", tm, PB, F32)
        bpad = jnp.pad(b_forget[l], (0, LANES - FOX_HEADS))[None]
        ccol, crow = _gate_fwd(f"gate_fwd{l}", pb, bpad, fl_block)
        fg = FOX_HEADS // FOX_GROUP
        ccol4 = jnp.repeat(ccol[:, :FOX_HEADS].reshape(s, fg, FOX_GROUP).transpose(1, 0, 2), HEAD_DIM, axis=2)
        crow4 = jnp.pad(crow.reshape(nb, fg, FOX_GROUP, TILE).transpose(1, 0, 2, 3),
                        ((0, 0), (0, 0), (0, 8 - FOX_GROUP), (0, 0)))
        more = l + 1 < depth
        o_sb, got_in = _sb_fwd(f"sb_fwd{l}", pa, 0, carried=_gather_exchange(own_w[l + 1][:1]) if more else None)
        o_fx, lse_fx, got_rest = _fox_fwd(f"fox_fwd{l}", pa, 3 * SB_WIDTH, ccol4, crow4,
                                          carried=_gather_exchange(own_w[l + 1][1:]) if more else None)
        if more:
            layer_w.append(lay_out(own_w[l + 1], list(got_in) + list(got_rest)))
        mn = _rms_fwd(f"mem_rms{l}", mems, mem_norm_w[l][None], mems.shape[0])
        mkv = _mm(f"mem_kv{l}", mn, wkv, "nn", mems.shape[0], 2 * MEM_WIDTH, BF16)
        o_m, lse_m = _mem_fwd(f"mem_fwd{l}", pa, mkv)
        nxt, y2 = _out_fwd(f"out_fwd{l}", o_sb, o_fx, o_m, pb, out_norm_w[l][None], cur, wout, ts)
        saved.append((cur, h, pa, pb, bpad, ccol4, crow4, o_sb, o_fx, lse_fx, mn, mkv, o_m, lse_m, y2))
        cur = nxt

    loss_v, dx, dxb, g_final = _final_loss("final_loss", cur, final_norm_w[None], target, ts)

    g_norm, g_b, g_memnorm, g_outnorm = [None] * depth, [None] * depth, [None] * depth, [None] * depth
    g_wa, g_wb, g_wkv, g_wout = [None] * depth, [None] * depth, [None] * depth, [None] * depth
    g_own = [[None] * depth for _ in big]
    g_other = [[None] * depth for _ in big]

    def reduce_at_owner(lr, chip_sums, from_chips, tiles):
        halves = [_sum_chips(f"grad_sum_chips{lr}_{k}", h_, r_, mvec, t_)
                  for k, (h_, r_, t_) in enumerate(zip(chip_sums, from_chips, tiles))]
        others = _swap_reduced(f"grad_swap_reduced{lr}", halves)
        for k in range(len(big)):
            g_own[k][lr], g_other[k][lr] = halves[k], others[k]

    pending = None
    for l in reversed(range(depth)):
        xin, h, pa, pb, bpad, ccol4, crow4, o_sb, o_fx, lse_fx, mn, mkv, o_m, lse_m, y2 = saved[l]
        wa, wb, wkv, wout = layer_w[l]
        dy, dgate, g_outnorm[l] = _out_bwd(f"out_bwd{l}", dxb, o_sb, o_fx, o_m, pb, out_norm_w[l][None], wout, ts)
        g_wout[l] = _mm(f"dw_out{l}", y2, dxb, "tn", _tile_of(MIX_WIDTH, 640, LANES), d, F32)
        scatter = _scatter_exchange(pending[1]) if pending is not None else None
        (dq_sb, dk_sb, dv_sb), from_chips = _sb_bwd(f"sb_bwd{l}", pa, 0, dy, 0, carried=scatter)
        if pending is not None:
            reduce_at_owner(pending[0], pending[1], from_chips, pending[2])
        dq_fx, dk_fx, dv_fx, cs4 = _fox_bwd(f"fox_bwd{l}", pa, 3 * SB_WIDTH, ccol4, crow4, o_fx, lse_fx, dy, SB_WIDTH)
        colsum = cs4[:, :, :FOX_GROUP, :].transpose(1, 0, 2, 3).reshape(nb, 8, TILE)
        dlogit, g_b[l] = _gate_bwd(f"gate_bwd{l}", pb, bpad, colsum, fl_block)
        dq_m, dk_m, dv_m = _mem_bwd(f"mem_bwd{l}", pa, mkv, o_m, lse_m, dy, SB_WIDTH + FOX_WIDTH)
        dmkv = jnp.concatenate([dk_m, dv_m], axis=1)
        g_wkv[l] = _mm(f"dw_kv{l}", mn, dmkv, "tn", d, 2 * MEM_WIDTH, F32)
        dmn = _mm(f"dmem{l}", dmkv, wkv, "nt", mems.shape[0], d, F32)
        g_memnorm[l] = _rms_wgrad(f"mem_norm_grad{l}", mems, dmn)
        dpa = jnp.concatenate([dq_sb, dk_sb, dv_sb, dq_fx, dk_fx, dv_fx, dq_m], axis=1)
        dpb = jnp.concatenate([dgate, dlogit], axis=1)
        tw = _tile_of(d, 512, LANES)
        g_wa[l] = _mm(f"dw_in_a{l}", h, dpa, "tn", tw, _tile_of(PA, 1664, LANES), F32)
        g_wb[l] = _mm(f"dw_in_b{l}", h, dpb, "tn", tw, PB, F32)
        dh = _mm(f"dh_a{l}", dpa, wa, "nt", tm, d, F32)
        dh = _mm(f"dh_b{l}", dpb, wb, "nt", tm, d, F32, res=dh)
        dx, dxb, g_norm[l] = _rms_bwd(f"rms_bwd{l}", xin, norm_w[l][None], dh, dx, ts)
        g4s = [jnp.stack([_shard_from_groups(g_wa[l], g_wb[l], j, cw) for j in range(N_CHIPS)]),
               g_wkv[l].reshape(N_CHIPS, -1, g_wkv[l].shape[1]), g_wout[l].reshape(N_CHIPS, -1, d)]
        tiles = [_tile_of(g.shape[1] // 2, 256, 16) for g in g4s]
        from_sibling = _swap_halves(f"grad_swap_halves{l}", g4s)
        chip_sums = [_add_half(f"grad_add_half{l}_{k}", g, r, cvec, t)
                     for k, (g, r, t) in enumerate(zip(g4s, from_sibling, tiles))]
        pending = (l, chip_sums, tiles)
    reduce_at_owner(pending[0], pending[1], _run_exchange(f"grad_scatter_chips{pending[0]}", _scatter_exchange(pending[1])),
                    pending[2])

    small_w = [norm_w, b_forget, mem_norm_w, out_norm_w, final_norm_w]
    small_m = [m_norm_w, m_b_forget, m_mem_norm_w, m_out_norm_w, m_final_norm_w]
    small_v = [v_norm_w, v_b_forget, v_mem_norm_w, v_out_norm_w, v_final_norm_w]
    rows2 = lambda a: a.reshape(-1, a.shape[-1])
    partials = [jnp.concatenate(g_norm, axis=0), jnp.concatenate(g_b, axis=0), jnp.concatenate(g_memnorm, axis=0),
                jnp.concatenate(g_outnorm, axis=0), g_final, loss_v]
    sums, updates = _small_update("small_update", partials, [rows2(a) for a in small_w] + [None],
                                  [rows2(a) for a in small_m] + [None], [rows2(a) for a in small_v] + [None])
    small_grads = [g.reshape(a.shape) for g, a in zip(sums, small_w)]
    loss = sums[-1][0, 0]
    small_delta, small_m2, small_v2 = ([updates[3 * k + t].reshape(a.shape) for k, a in enumerate(small_w)]
                                       for t in range(3))
    big_grads, big_delta, big_m2, big_v2 = [], [], [], []
    for k, (nm, w_, m_, v_) in enumerate(zip(("w_in", "w_mem_kv", "w_out"), big, (m_w_in, m_w_mem_kv, m_w_out),
                                             (v_w_in, v_w_mem_kv, v_w_out))):
        if w_.shape[2] % LANES:
            g_full = jnp.stack([jnp.concatenate([jnp.where(core == 0, go, gt), jnp.where(core == 0, gt, go)], axis=0)
                                for go, gt in zip(g_own[k], g_other[k])])
            w_p, g_p, m_p, v_p = (_column_major_rows(a) for a in (w_, g_full, m_, v_))
            outs = _adamw(f"adamw_{nm}", w_p, g_p, m_p, v_p, _tile_of(w_p.shape[0], 600, 1))
            outs = [_from_column_major_rows(o, w_.shape) for o in (g_p, *outs)]
        else:
            outs = _adamw_sharded(f"adamw_{nm}", w_, m_, v_, g_own[k], g_other[k], cvec,
                                  _tile_of(w_.shape[1] // 2, 256, 8))
        for lst, o in zip((big_grads, big_delta, big_m2, big_v2), outs):
            lst.append(o)

    def order(sm, bg):
        return [sm[0], bg[0], sm[1], sm[2], bg[1], sm[3], bg[2], sm[4]]

    return (loss, dx[None], *order(small_grads, big_grads), *order(small_delta, big_delta),
            *order(small_m2, big_m2), *order(small_v2, big_v2))
```

```python
import functools

import jax
import jax.numpy as jnp
from jax import lax
from jax.experimental import pallas as pl
from jax.experimental.pallas import tpu as pltpu

F32 = jnp.float32
BF16 = jnp.bfloat16

HEAD_DIM = 64
SB_WIDTH = 512
FOX_WIDTH = 512
FOX_HEADS = 8
MEM_WIDTH = 256
MIX_WIDTH = SB_WIDTH + FOX_WIDTH + MEM_WIDTH
TOTAL_HEADS = MIX_WIDTH // HEAD_DIM
IN_WIDTH = 3 * SB_WIDTH + 3 * FOX_WIDTH + FOX_HEADS + MEM_WIDTH + MIX_WIDTH
LANES = 128
QKV_WIDTH = 3 * SB_WIDTH + 3 * FOX_WIDTH
PA = QKV_WIDTH + MEM_WIDTH
PB = LANES + MIX_WIDTH
EPS = 1e-6
SCALE = HEAD_DIM ** -0.5
TILE = 256
SB_GROUP = 4
SB_LANES = SB_GROUP * HEAD_DIM
FOX_GROUP = 4
FOX_LANES = FOX_GROUP * HEAD_DIM
NEG_INF = float("-inf")
MASKED = -1e30

ADAM_LR = 0.001
ADAM_B1 = 0.9
ADAM_B2 = 0.999
ADAM_EPS = 1e-08
ADAM_WD = 0.01
ADAM_STEP = 10

N_CHIPS = 4
N_DEV = 8
VMEM_LIMIT = 48 * 1024 * 1024
MESH = pl.DeviceIdType.MESH


def _params(*sem):
    return pltpu.CompilerParams(dimension_semantics=tuple(sem), vmem_limit_bytes=VMEM_LIMIT)


def _dot(a, b):
    return jnp.dot(a, b, preferred_element_type=F32)


def _dot_nt(a, b):
    return lax.dot_general(a, b, (((1,), (1,)), ((), ())), preferred_element_type=F32)


def _dot_tn(a, b):
    return lax.dot_general(a, b, (((0,), (0,)), ((), ())), preferred_element_type=F32)


def _split2(x):
    hi = x.astype(BF16)
    lo = (x - hi.astype(F32)).astype(BF16)
    return hi, lo


def _split3(x):
    hi = x.astype(BF16)
    r = x - hi.astype(F32)
    mid = r.astype(BF16)
    lo = (r - mid.astype(F32)).astype(BF16)
    return hi, mid, lo


def _sum_l2(x, u):
    hi, lo = _split2(x)
    return _dot(hi, u) + _dot(lo, u)


def _sum_l3(x, u):
    hi, mid, lo = _split3(x)
    return _dot(hi, u) + _dot(mid, u) + _dot(lo, u)


def _sum_r3(u, x):
    hi, mid, lo = _split3(x)
    return _dot(u, hi) + _dot(u, mid) + _dot(u, lo)


def _softplus(z):
    return jnp.maximum(z, 0.0) + jnp.log1p(jnp.exp(-jnp.abs(z)))


def _tri(n, pred):
    r = lax.broadcasted_iota(jnp.int32, (n, n), 0)
    c = lax.broadcasted_iota(jnp.int32, (n, n), 1)
    return jnp.where(pred(r, c), 1.0, 0.0).astype(BF16)


def _rows(ref, j, n=TILE):
    return pl.ds(pl.multiple_of(j * n, n), n)


def _mm(name, a, b, mode, tm, tn, out_dtype, res=None, a_lead=(), b_lead=()):
    a2, b2 = a.shape[len(a_lead):], b.shape[len(b_lead):]
    if mode == "tn":
        k, m = a2
    else:
        m, k = a2
    n = b2[0] if mode == "nt" else b2[1]
    assert m % tm == 0 and n % tn == 0, (name, m, tm, n, tn)
    na, nb = (None,) * len(a_lead), (None,) * len(b_lead)
    if mode == "tn":
        a_spec = pl.BlockSpec(na + (k, tm), lambda j, i: a_lead + (0, i))
    else:
        a_spec = pl.BlockSpec(na + (tm, k), lambda j, i: a_lead + (i, 0))
    if mode == "nt":
        b_spec = pl.BlockSpec(nb + (tn, k), lambda j, i: b_lead + (j, 0))
    else:
        b_spec = pl.BlockSpec(nb + (k, tn), lambda j, i: b_lead + (0, j))
    o_spec = pl.BlockSpec((tm, tn), lambda j, i: (i, j))
    dot = {"nn": _dot, "nt": _dot_nt, "tn": _dot_tn}[mode]

    def body(a_ref, b_ref, *rest):
        o_ref = rest[-1]
        acc = dot(a_ref[...].astype(BF16), b_ref[...].astype(BF16))
        if res is not None:
            acc = acc + rest[0][...]
        o_ref[...] = acc.astype(o_ref.dtype)

    args, specs = [a, b], [a_spec, b_spec]
    if res is not None:
        args.append(res)
        specs.append(o_spec)
    return pl.pallas_call(
        body, name=name, grid=(n // tn, m // tm), in_specs=specs, out_specs=o_spec,
        out_shape=jax.ShapeDtypeStruct((m, n), out_dtype),
        compiler_params=_params("parallel", "parallel"),
    )(*args)


def _rms_fwd(name, x, g, ts):
    s, d = x.shape

    def body(x_ref, g_ref, o_ref):
        xf = x_ref[...]
        r = lax.rsqrt(jnp.mean(xf * xf, axis=1, keepdims=True) + EPS)
        o_ref[...] = (xf * r * g_ref[...]).astype(BF16)

    return pl.pallas_call(
        body, name=name, grid=(s // ts,),
        in_specs=[pl.BlockSpec((ts, d), lambda i: (i, 0)), pl.BlockSpec((1, d), lambda i: (0, 0))],
        out_specs=pl.BlockSpec((ts, d), lambda i: (i, 0)),
        out_shape=jax.ShapeDtypeStruct((s, d), BF16),
        compiler_params=_params("parallel"),
    )(x, g)


def _rms_bwd(name, x, g, dh, dres, ts):
    s, d = x.shape

    def body(x_ref, g_ref, dh_ref, dres_ref, dx_ref, dxb_ref, dg_ref):
        @pl.when(pl.program_id(0) == 0)
        def _():
            dg_ref[...] = jnp.zeros_like(dg_ref)

        xf = x_ref[...]
        r = lax.rsqrt(jnp.mean(xf * xf, axis=1, keepdims=True) + EPS)
        xh = xf * r
        dhf = dh_ref[...]
        dg_ref[...] += jnp.sum(dhf * xh, axis=0, keepdims=True)
        dxh = dhf * g_ref[...]
        m = jnp.mean(dxh * xh, axis=1, keepdims=True)
        dx = r * (dxh - xh * m) + dres_ref[...]
        dx_ref[...] = dx
        dxb_ref[...] = dx.astype(BF16)

    row = pl.BlockSpec((ts, d), lambda i: (i, 0))
    vec = pl.BlockSpec((1, d), lambda i: (0, 0))
    return pl.pallas_call(
        body, name=name, grid=(s // ts,), in_specs=[row, vec, row, row], out_specs=[row, row, vec],
        out_shape=[jax.ShapeDtypeStruct((s, d), F32), jax.ShapeDtypeStruct((s, d), BF16),
                   jax.ShapeDtypeStruct((1, d), F32)],
        compiler_params=_params("arbitrary"),
    )(x, g, dh, dres)


def _rms_wgrad(name, x, dh):
    m_, d = x.shape

    def body(x_ref, dh_ref, dg_ref):
        xf = x_ref[...]
        r = lax.rsqrt(jnp.mean(xf * xf, axis=1, keepdims=True) + EPS)
        dg_ref[...] = jnp.sum(dh_ref[...] * xf * r, axis=0, keepdims=True)

    return pl.pallas_call(
        body, name=name, out_shape=jax.ShapeDtypeStruct((1, d), F32),
    )(x, dh)


def _final_loss(name, x, g, target, ts):
    s, d = x.shape

    def body(x_ref, g_ref, t_ref, loss_ref, dx_ref, dxb_ref, dg_ref):
        @pl.when(pl.program_id(0) == 0)
        def _():
            dg_ref[...] = jnp.zeros_like(dg_ref)
            loss_ref[...] = jnp.zeros_like(loss_ref)

        xf = x_ref[...]
        gw = g_ref[...]
        r = lax.rsqrt(jnp.mean(xf * xf, axis=1, keepdims=True) + EPS)
        xh = xf * r
        e = xh * gw - t_ref[...]
        part = 0.5 * jnp.sum(jnp.mean(e * e, axis=1, keepdims=True), axis=0, keepdims=True)
        loss_ref[...] += jnp.broadcast_to(part, loss_ref.shape)
        dy = e * (1.0 / d)
        dg_ref[...] += jnp.sum(dy * xh, axis=0, keepdims=True)
        dxh = dy * gw
        m = jnp.mean(dxh * xh, axis=1, keepdims=True)
        dx = r * (dxh - xh * m)
        dx_ref[...] = dx
        dxb_ref[...] = dx.astype(BF16)

    row = pl.BlockSpec((ts, d), lambda i: (i, 0))
    vec = pl.BlockSpec((1, d), lambda i: (0, 0))
    lvec = pl.BlockSpec((1, LANES), lambda i: (0, 0))
    return pl.pallas_call(
        body, name=name, grid=(s // ts,), in_specs=[row, vec, row], out_specs=[lvec, row, row, vec],
        out_shape=[jax.ShapeDtypeStruct((1, LANES), F32), jax.ShapeDtypeStruct((s, d), F32),
                   jax.ShapeDtypeStruct((s, d), BF16), jax.ShapeDtypeStruct((1, d), F32)],
        compiler_params=_params("arbitrary"),
    )(x, g, target)


def _gate_fwd(name, pb, bpad, fl_block):
    s = pb.shape[0]
    nb = s // TILE

    def body(fl_ref, b_ref, ccol_ref, crow_ref, carry):
        @pl.when(pl.program_id(0) == 0)
        def _():
            carry[...] = jnp.zeros_like(carry)

        u = fl_ref[...] + b_ref[...]
        lf = jnp.minimum(u, 0.0) - jnp.log1p(jnp.exp(-jnp.abs(u)))
        lower = _tri(TILE, lambda r, c: c <= r)
        c = _sum_r3(lower, lf) + carry[0:1, :]
        ccol_ref[...] = c
        crow_ref[0] = c.T[0:8, :]
        carry[...] = jnp.broadcast_to(c[TILE - 1:TILE, :], carry.shape)

    return pl.pallas_call(
        body, name=name, grid=(nb,),
        in_specs=[pl.BlockSpec((TILE, LANES), lambda i: (i, fl_block)), pl.BlockSpec((1, LANES), lambda i: (0, 0))],
        out_specs=[pl.BlockSpec((TILE, LANES), lambda i: (i, 0)), pl.BlockSpec((1, 8, TILE), lambda i: (i, 0, 0))],
        out_shape=[jax.ShapeDtypeStruct((s, LANES), F32), jax.ShapeDtypeStruct((nb, 8, TILE), F32)],
        scratch_shapes=[pltpu.VMEM((8, LANES), F32)],
        compiler_params=_params("arbitrary"),
    )(pb, bpad)


def _gate_bwd(name, pb, bpad, colsum, fl_block):
    s = pb.shape[0]
    nb = s // TILE

    def body(fl_ref, b_ref, cs_ref, dl_ref, db_ref, carry):
        @pl.when(pl.program_id(0) == 0)
        def _():
            carry[...] = jnp.zeros_like(carry)
            db_ref[...] = jnp.zeros_like(db_ref)

        upper = _tri(TILE, lambda r, c: r >= c)
        rsum = _sum_l3(cs_ref[0], upper) + carry[:, 0:1]
        carry[...] = jnp.broadcast_to(rsum[:, 0:1], carry.shape)
        full = jnp.concatenate([rsum, jnp.zeros((LANES - 8, TILE), F32)], axis=0)
        dlf = -full.T
        u = fl_ref[...] + b_ref[...]
        dlogit = dlf * (1.0 - jax.nn.sigmoid(u))
        dl_ref[...] = dlogit.astype(BF16)
        db_ref[...] += jnp.sum(dlogit, axis=0, keepdims=True)

    rev = lambda i: (nb - 1 - i, 0)
    return pl.pallas_call(
        body, name=name, grid=(nb,),
        in_specs=[pl.BlockSpec((TILE, LANES), lambda i: (nb - 1 - i, fl_block)),
                  pl.BlockSpec((1, LANES), lambda i: (0, 0)),
                  pl.BlockSpec((1, 8, TILE), lambda i: (nb - 1 - i, 0, 0))],
        out_specs=[pl.BlockSpec((TILE, LANES), rev), pl.BlockSpec((1, LANES), lambda i: (0, 0))],
        out_shape=[jax.ShapeDtypeStruct((s, LANES), BF16), jax.ShapeDtypeStruct((1, LANES), F32)],
        scratch_shapes=[pltpu.VMEM((8, LANES), F32)],
        compiler_params=_params("arbitrary"),
    )(pb, bpad, colsum)


def _head_slices(hh):
    return slice(HEAD_DIM * hh, HEAD_DIM * (hh + 1))


def _scaled_q(q_ref, sl, scale=SCALE):
    return (q_ref[:, sl].astype(F32) * scale).astype(BF16)


def _neg_abs(x):
    sign = jnp.uint32(0x80000000)
    return lax.bitcast_convert_type(lax.bitcast_convert_type(x, jnp.uint32) | sign, F32)


def _sb_tile(qn, kj, carry, strict, u_after, diag):
    nz = _dot_nt(qn, kj)
    lf = jnp.minimum(nz, 0.0) - jnp.log(1.0 + jnp.exp(_neg_abs(nz)))
    lsig = lf - nz
    if diag:
        lf = jnp.where(strict, lf, 0.0)
    sx = _dot(lf.astype(BF16), u_after)
    a = jnp.exp(lsig + sx + carry)
    if diag:
        a = jnp.where(strict, a, 0.0)
    return lsig, a, carry + sx[:, 0:1] + lf[:, 0:1]


def _pair_grid_call(name, body, nb, in_specs, out_specs, out_shape, scratch, args, carried=None, groups=4):
    if carried is None:
        return pl.pallas_call(
            body, name=name, grid=(groups, nb), in_specs=in_specs, out_specs=out_specs, out_shape=out_shape,
            scratch_shapes=scratch, compiler_params=_params("arbitrary", "arbitrary"),
        )(*args)
    n_in, n_out, n_ex = len(in_specs), len(out_specs), carried.n

    def body_with_copies(*refs):
        own_in, ex_in = refs[:n_in], refs[n_in:n_in + n_ex]
        own_out = refs[n_in + n_ex:n_in + n_ex + n_out]
        ex_out = refs[n_in + n_ex + n_out:n_in + 2 * n_ex + n_out]
        own_scratch, sems = refs[n_in + 2 * n_ex + n_out:-2], refs[-2:]
        parts = (ex_in, ex_out, sems[0], sems[1])
        p, i = pl.program_id(0), pl.program_id(1)
        pl.when(jnp.logical_and(p == 0, i == 0))(lambda: carried.begin(*parts))
        if carried.relay is not None:
            pl.when(jnp.logical_and(p == groups - 1, i == max(nb - 2, 0)))(lambda: carried.relay(*parts))
        body(*own_in, *own_out, *own_scratch)
        pl.when(jnp.logical_and(p == groups - 1, i == nb - 1))(lambda: carried.finish(*parts))

    return pl.pallas_call(
        body_with_copies, name=name, grid=(groups, nb), in_specs=list(in_specs) + [HBM_SPEC] * n_ex,
        out_specs=list(out_specs) + [HBM_SPEC] * n_ex, out_shape=list(out_shape) + carried.out_shapes,
        scratch_shapes=list(scratch) + _dma_sems(carried.n_sems),
        compiler_params=_params("arbitrary", "arbitrary"),
    )(*args, *carried.inputs)


def _sb_fwd(name, pa, col0, carried=None):
    s = pa.shape[0]
    nb = s // TILE
    cb = col0 // SB_LANES
    kb = SB_WIDTH // SB_LANES

    def body(q_ref, k_ref, v_ref, o_ref, lsig_s, lf_s):
        i = pl.program_id(1)
        r = lax.broadcasted_iota(jnp.int32, (TILE, TILE), 0)
        c = lax.broadcasted_iota(jnp.int32, (TILE, TILE), 1)
        strict = c < r
        u_after = _tri(TILE, lambda rr, cc: rr > cc)
        qs = [_scaled_q(q_ref, _head_slices(hh), -SCALE) for hh in range(SB_GROUP)]

        def neg_z(j):
            kblk = k_ref[_rows(k_ref, j), :]
            return [_dot_nt(qs[hh], kblk[:, _head_slices(hh)]) for hh in range(SB_GROUP)]

        def scores(nzs, slot, diag):
            for hh, nz in enumerate(nzs):
                lf = jnp.minimum(nz, 0.0) - jnp.log(1.0 + jnp.exp(_neg_abs(nz)))
                lsig = lf - nz
                if diag:
                    lf = jnp.where(strict, lf, 0.0)
                    lsig = jnp.where(strict, lsig, MASKED)
                lsig_s[slot, hh] = lsig
                lf_s[slot, hh] = lf.astype(BF16)

        def weigh(j, slot, state):
            vblk = v_ref[_rows(v_ref, j), :]
            new = []
            for hh in range(SB_GROUP):
                carry, acc = state[hh]
                lfb = lf_s[slot, hh]
                sx = _dot(lfb, u_after)
                a = jnp.exp(lsig_s[slot, hh] + sx + carry)
                new.append((carry + sx[:, 0:1] + lfb[:, 0:1].astype(F32),
                            acc + _dot(a.astype(BF16), vblk[:, _head_slices(hh)])))
            return tuple(new)

        def step(t, state):
            state = weigh(i - t + 1, (t - 1) % 2, state)
            scores(neg_z(i - t), t % 2, False)
            return state

        zero = (jnp.zeros((TILE, 1), F32), jnp.zeros((TILE, HEAD_DIM), F32))
        scores(neg_z(i), 0, True)
        state = lax.fori_loop(1, i + 1, step, (zero,) * SB_GROUP)
        state = weigh(0, i % 2, state)
        o_ref[...] = jnp.concatenate([st[1] for st in state], axis=1)

    outs = _pair_grid_call(
        name, body, nb,
        in_specs=[pl.BlockSpec((TILE, SB_LANES), lambda p, i: (i, cb + p)),
                  pl.BlockSpec((s, SB_LANES), lambda p, i: (0, cb + kb + p)),
                  pl.BlockSpec((s, SB_LANES), lambda p, i: (0, cb + 2 * kb + p))],
        out_specs=[pl.BlockSpec((TILE, SB_LANES), lambda p, i: (i, p))],
        out_shape=[jax.ShapeDtypeStruct((s, SB_WIDTH), F32)],
        scratch=[pltpu.VMEM((2, SB_GROUP, TILE, TILE), F32), pltpu.VMEM((2, SB_GROUP, TILE, TILE), BF16)],
        args=(pa, pa, pa), carried=carried, groups=kb)
    return outs[0], outs[1:]


def _sb_bwd(name, pa, col0, dout, dcol0, carried=None):
    s = pa.shape[0]
    nb = s // TILE
    cb = col0 // SB_LANES
    kb = SB_WIDTH // SB_LANES
    db = dcol0 // SB_LANES

    def body(q_ref, k_ref, v_ref, do_ref, dq_ref, dk_ref, dv_ref, dk_acc, dv_acc, dpan, span, gsum, lsig_s, lf_s):
        i = pl.program_id(1)

        @pl.when(i == 0)
        def _():
            dk_acc[...] = jnp.zeros_like(dk_acc)
            dv_acc[...] = jnp.zeros_like(dv_acc)

        r = lax.broadcasted_iota(jnp.int32, (TILE, TILE), 0)
        c = lax.broadcasted_iota(jnp.int32, (TILE, TILE), 1)
        strict = c < r
        u_after = _tri(TILE, lambda rr, cc: rr > cc)
        u_before = _tri(TILE, lambda rr, cc: rr < cc)
        qs = [_scaled_q(q_ref, _head_slices(hh), -SCALE) for hh in range(SB_GROUP)]
        dos = [do_ref[:, _head_slices(hh)].astype(BF16) for hh in range(SB_GROUP)]

        def scores(j, slot, diag):
            kblk = k_ref[_rows(k_ref, j), :]
            for hh in range(SB_GROUP):
                nz = _dot_nt(qs[hh], kblk[:, _head_slices(hh)])
                lf = jnp.minimum(nz, 0.0) - jnp.log(1.0 + jnp.exp(_neg_abs(nz)))
                lsig = lf - nz
                if diag:
                    lf = jnp.where(strict, lf, 0.0)
                    lsig = jnp.where(strict, lsig, MASKED)
                lsig_s[slot, hh] = lsig
                lf_s[slot, hh] = lf.astype(BF16)

        def grads(j, slot, carries):
            vblk = v_ref[_rows(v_ref, j), :]
            new = []
            for hh in range(SB_GROUP):
                lfb = lf_s[slot, hh]
                lsig = lsig_s[slot, hh]
                sx = _dot(lfb, u_after)
                a = jnp.exp(lsig + sx + carries[hh])
                g = a * _dot_nt(dos[hh], vblk[:, _head_slices(hh)])
                sig = jnp.exp(lsig)
                inside = _dot(g.astype(BF16), u_before)
                dpan[hh, j] = sig * (inside + g) - g
                span[hh, j] = sig
                gsum[hh, j] = inside[:, TILE - 1:TILE] + g[:, TILE - 1:TILE]
                dv_acc[hh, _rows(None, j), :] += _dot_tn(a.astype(BF16), dos[hh])
                new.append(carries[hh] + sx[:, 0:1] + lfb[:, 0:1].astype(F32))
            return tuple(new)

        def step1(t, carries):
            carries = grads(i - t + 1, (t - 1) % 2, carries)
            scores(i - t, t % 2, False)
            return carries

        zero1 = jnp.zeros((TILE, 1), F32)
        scores(i, 0, True)
        carries = lax.fori_loop(1, i + 1, step1, (zero1,) * SB_GROUP)
        grads(0, i % 2, carries)

        def pass2(j, state):
            kblk = k_ref[_rows(k_ref, j), :]
            new = []
            for hh in range(SB_GROUP):
                before, ndq = state[hh]
                ndzb = (dpan[hh, j] + span[hh, j] * before).astype(BF16)
                dk_acc[hh, _rows(None, j), :] += _dot_tn(ndzb, qs[hh])
                new.append((before + gsum[hh, j], ndq + _dot(ndzb, kblk[:, _head_slices(hh)])))
            return tuple(new)

        zero2 = (zero1, jnp.zeros((TILE, HEAD_DIM), F32))
        state = lax.fori_loop(0, i + 1, pass2, (zero2,) * SB_GROUP)
        dq_ref[...] = jnp.concatenate([st[1] * -SCALE for st in state], axis=1).astype(BF16)

        @pl.when(i == nb - 1)
        def _():
            dk_ref[...] = jnp.concatenate([dk_acc[hh] for hh in range(SB_GROUP)], axis=1).astype(BF16)
            dv_ref[...] = jnp.concatenate([dv_acc[hh] for hh in range(SB_GROUP)], axis=1).astype(BF16)

    qspec = pl.BlockSpec((TILE, SB_LANES), lambda p, i: (i, p))
    kvspec = pl.BlockSpec((s, SB_LANES), lambda p, i: (0, p))
    out = jax.ShapeDtypeStruct((s, SB_WIDTH), BF16)
    outs = _pair_grid_call(
        name, body, nb,
        in_specs=[pl.BlockSpec((TILE, SB_LANES), lambda p, i: (i, cb + p)),
                  pl.BlockSpec((s, SB_LANES), lambda p, i: (0, cb + kb + p)),
                  pl.BlockSpec((s, SB_LANES), lambda p, i: (0, cb + 2 * kb + p)),
                  pl.BlockSpec((TILE, SB_LANES), lambda p, i: (i, db + p))],
        out_specs=[qspec, kvspec, kvspec], out_shape=[out, out, out],
        scratch=[pltpu.VMEM((SB_GROUP, s, HEAD_DIM), F32), pltpu.VMEM((SB_GROUP, s, HEAD_DIM), F32),
                 pltpu.VMEM((SB_GROUP, nb, TILE, TILE), F32), pltpu.VMEM((SB_GROUP, nb, TILE, TILE), F32),
                 pltpu.VMEM((SB_GROUP, nb, TILE, 1), F32),
                 pltpu.VMEM((2, SB_GROUP, TILE, TILE), F32), pltpu.VMEM((2, SB_GROUP, TILE, TILE), BF16)],
        args=(pa, pa, pa, dout), carried=carried, groups=kb)
    return outs[:3], outs[3:]


def _fox_scores(q, kj, cq, crj, causal, diag):
    sc = _dot_nt(q, kj) + (cq - crj)
    if diag:
        sc = jnp.where(causal, sc, NEG_INF)
    return sc


def _fox_fwd(name, pa, col0, ccol4, crow4, carried=None):
    s = pa.shape[0]
    nb = s // TILE
    cb = col0 // FOX_LANES
    kb = FOX_WIDTH // FOX_LANES

    def body(q_ref, k_ref, v_ref, cc_ref, cr_ref, o_ref, lse_ref, sc_s):
        i = pl.program_id(1)
        r = lax.broadcasted_iota(jnp.int32, (TILE, TILE), 0)
        c = lax.broadcasted_iota(jnp.int32, (TILE, TILE), 1)
        causal = c <= r
        qs = [_scaled_q(q_ref, _head_slices(hh)) for hh in range(FOX_GROUP)]
        cqs = [cc_ref[:, HEAD_DIM * hh:HEAD_DIM * hh + 1] for hh in range(FOX_GROUP)]

        def logits(j, slot, diag):
            kblk = k_ref[_rows(k_ref, j), :]
            tops = []
            for hh in range(FOX_GROUP):
                sc = _fox_scores(qs[hh], kblk[:, _head_slices(hh)], cqs[hh], cr_ref[j, hh:hh + 1, :], causal, diag)
                sc_s[slot, hh] = sc
                tops.append(jnp.max(sc, axis=1, keepdims=True))
            return tuple(tops)

        def update(j, slot, tops, state):
            vblk = v_ref[_rows(v_ref, j), :]
            new = []
            for hh in range(FOX_GROUP):
                m, l, acc = state[hh]
                m2 = jnp.maximum(m, tops[hh])
                alpha = jnp.exp(m - m2)
                p = jnp.exp(sc_s[slot, hh] - m2)
                new.append((m2, l * alpha + jnp.sum(p, axis=1, keepdims=True),
                            acc * alpha + _dot(p.astype(BF16), vblk[:, _head_slices(hh)])))
            return tuple(new)

        def step(t, both):
            tops, state = both
            state = update(i - t + 1, (t - 1) % 2, tops, state)
            return logits(i - t, t % 2, False), state

        zero = (jnp.full((TILE, 1), NEG_INF, F32), jnp.zeros((TILE, 1), F32), jnp.zeros((TILE, HEAD_DIM), F32))
        tops, state = lax.fori_loop(1, i + 1, step, (logits(i, 0, True), (zero,) * FOX_GROUP))
        state = update(0, i % 2, tops, state)
        o_ref[...] = jnp.concatenate([st[2] / st[1] for st in state], axis=1)
        lse_ref[...] = jnp.concatenate(
            [jnp.broadcast_to(st[0] + jnp.log(st[1]), (TILE, HEAD_DIM)) for st in state], axis=1)

    outs = _pair_grid_call(
        name, body, nb,
        in_specs=[pl.BlockSpec((TILE, FOX_LANES), lambda p, i: (i, cb + p)),
                  pl.BlockSpec((s, FOX_LANES), lambda p, i: (0, cb + kb + p)),
                  pl.BlockSpec((s, FOX_LANES), lambda p, i: (0, cb + 2 * kb + p)),
                  pl.BlockSpec((None, TILE, FOX_LANES), lambda p, i: (p, i, 0)),
                  pl.BlockSpec((None, nb, 8, TILE), lambda p, i: (p, 0, 0, 0))],
        out_specs=[pl.BlockSpec((TILE, FOX_LANES), lambda p, i: (i, p)),
                   pl.BlockSpec((None, TILE, FOX_LANES), lambda p, i: (p, i, 0))],
        out_shape=[jax.ShapeDtypeStruct((s, FOX_WIDTH), F32), jax.ShapeDtypeStruct((kb, s, FOX_LANES), F32)],
        scratch=[pltpu.VMEM((2, FOX_GROUP, TILE, TILE), F32)],
        args=(pa, pa, pa, ccol4, crow4), carried=carried, groups=kb)
    return outs[0], outs[1], outs[2:]


def _fox_bwd(name, pa, col0, ccol4, crow4, out, lse, dout, dcol0):
    s = pa.shape[0]
    nb = s // TILE
    cb = col0 // FOX_LANES
    kb = FOX_WIDTH // FOX_LANES
    db = dcol0 // FOX_LANES

    def body(q_ref, k_ref, v_ref, cc_ref, cr_ref, o_ref, lse_ref, do_ref,
             dq_ref, dk_ref, dv_ref, cs_ref, dk_acc, dv_acc, p_s, ds_s):
        i = pl.program_id(1)

        @pl.when(i == 0)
        def _():
            dk_acc[...] = jnp.zeros_like(dk_acc)
            dv_acc[...] = jnp.zeros_like(dv_acc)
            cs_ref[...] = jnp.zeros_like(cs_ref)

        r = lax.broadcasted_iota(jnp.int32, (TILE, TILE), 0)
        c = lax.broadcasted_iota(jnp.int32, (TILE, TILE), 1)
        causal = c <= r
        qs = [_scaled_q(q_ref, _head_slices(hh)) for hh in range(FOX_GROUP)]
        cqs = [cc_ref[:, HEAD_DIM * hh:HEAD_DIM * hh + 1] for hh in range(FOX_GROUP)]
        lses = [lse_ref[:, HEAD_DIM * hh:HEAD_DIM * hh + 1] for hh in range(FOX_GROUP)]
        dofs = [do_ref[:, _head_slices(hh)] for hh in range(FOX_GROUP)]
        dos = [d_.astype(BF16) for d_ in dofs]
        deltas = [jnp.sum(dofs[hh] * o_ref[:, _head_slices(hh)], axis=1, keepdims=True) for hh in range(FOX_GROUP)]

        def probs(j, slot, rowsums, diag):
            kblk = k_ref[_rows(k_ref, j), :]
            vblk = v_ref[_rows(v_ref, j), :]
            new = []
            for hh in range(FOX_GROUP):
                sl = _head_slices(hh)
                sc = _fox_scores(qs[hh], kblk[:, sl], cqs[hh], cr_ref[j, hh:hh + 1, :], causal, diag)
                p = jnp.exp(sc - lses[hh])
                ds = p * (_dot_nt(dos[hh], vblk[:, sl]) - deltas[hh])
                p_s[slot, hh] = p.astype(BF16)
                ds_s[slot, hh] = ds.astype(BF16)
                cs_ref[j, hh:hh + 1, :] += jnp.sum(ds, axis=0, keepdims=True)
                new.append(rowsums[hh] + jnp.sum(ds, axis=1, keepdims=True))
            return tuple(new)

        def accumulate(j, slot, dqs):
            kblk = k_ref[_rows(k_ref, j), :]
            new = []
            for hh in range(FOX_GROUP):
                dsb = ds_s[slot, hh]
                dv_acc[hh, _rows(None, j), :] += _dot_tn(p_s[slot, hh], dos[hh])
                dk_acc[hh, _rows(None, j), :] += _dot_tn(dsb, qs[hh])
                new.append(dqs[hh] + _dot(dsb, kblk[:, _head_slices(hh)]))
            return tuple(new)

        def step(t, both):
            rowsums, dqs = both
            dqs = accumulate(i - t + 1, (t - 1) % 2, dqs)
            return probs(i - t, t % 2, rowsums, False), dqs

        zero1 = jnp.zeros((TILE, 1), F32)
        zero64 = jnp.zeros((TILE, HEAD_DIM), F32)
        rowsums, dqs = lax.fori_loop(1, i + 1, step,
                                     (probs(i, 0, (zero1,) * FOX_GROUP, True), (zero64,) * FOX_GROUP))
        dqs = accumulate(0, i % 2, dqs)
        for hh in range(FOX_GROUP):
            cs_ref[i, hh:hh + 1, :] -= jnp.broadcast_to(rowsums[hh], (TILE, LANES)).T[0:1, :]
        dq_ref[...] = jnp.concatenate([dq * SCALE for dq in dqs], axis=1).astype(BF16)

        @pl.when(i == nb - 1)
        def _():
            dk_ref[...] = jnp.concatenate([dk_acc[hh] for hh in range(FOX_GROUP)], axis=1).astype(BF16)
            dv_ref[...] = jnp.concatenate([dv_acc[hh] for hh in range(FOX_GROUP)], axis=1).astype(BF16)

    qspec = pl.BlockSpec((TILE, FOX_LANES), lambda p, i: (i, p))
    kvspec = pl.BlockSpec((s, FOX_LANES), lambda p, i: (0, p))
    o3 = jax.ShapeDtypeStruct((s, FOX_WIDTH), BF16)
    return pl.pallas_call(
        body, name=name, grid=(kb, nb),
        in_specs=[pl.BlockSpec((TILE, FOX_LANES), lambda p, i: (i, cb + p)),
                  pl.BlockSpec((s, FOX_LANES), lambda p, i: (0, cb + kb + p)),
                  pl.BlockSpec((s, FOX_LANES), lambda p, i: (0, cb + 2 * kb + p)),
                  pl.BlockSpec((None, TILE, FOX_LANES), lambda p, i: (p, i, 0)),
                  pl.BlockSpec((None, nb, 8, TILE), lambda p, i: (p, 0, 0, 0)),
                  qspec,
                  pl.BlockSpec((None, TILE, FOX_LANES), lambda p, i: (p, i, 0)),
                  pl.BlockSpec((TILE, FOX_LANES), lambda p, i: (i, db + p))],
        out_specs=[qspec, kvspec, kvspec, pl.BlockSpec((None, nb, 8, TILE), lambda p, i: (p, 0, 0, 0))],
        out_shape=[o3, o3, o3, jax.ShapeDtypeStruct((kb, nb, 8, TILE), F32)],
        scratch_shapes=[pltpu.VMEM((FOX_GROUP, s, HEAD_DIM), F32), pltpu.VMEM((FOX_GROUP, s, HEAD_DIM), F32),
                        pltpu.VMEM((2, FOX_GROUP, TILE, TILE), BF16), pltpu.VMEM((2, FOX_GROUP, TILE, TILE), BF16)],
        compiler_params=_params("arbitrary", "arbitrary"),
    )(pa, pa, pa, ccol4, crow4, out, lse, dout)


def _mem_fwd(name, pa, mkv):
    s = pa.shape[0]
    ml = mkv.shape[0]
    nb = s // TILE
    cb = QKV_WIDTH // LANES

    def body(q_ref, k_ref, v_ref, o_ref, lse_ref):
        outs, lses = [], []
        for hh in range(2):
            sl = _head_slices(hh)
            sc = _dot_nt(_scaled_q(q_ref, sl), k_ref[:, sl])
            m = jnp.max(sc, axis=1, keepdims=True)
            p = jnp.exp(sc - m)
            l = jnp.sum(p, axis=1, keepdims=True)
            outs.append(_dot(p.astype(BF16), v_ref[:, sl]) / l)
            lses.append(jnp.broadcast_to(m + jnp.log(l), (TILE, HEAD_DIM)))
        o_ref[...] = jnp.concatenate(outs, axis=1)
        lse_ref[...] = jnp.concatenate(lses, axis=1)

    return pl.pallas_call(
        body, name=name, grid=(2, nb),
        in_specs=[pl.BlockSpec((TILE, LANES), lambda p, i: (i, cb + p)),
                  pl.BlockSpec((ml, LANES), lambda p, i: (0, p)),
                  pl.BlockSpec((ml, LANES), lambda p, i: (0, 2 + p))],
        out_specs=[pl.BlockSpec((TILE, LANES), lambda p, i: (i, p)),
                   pl.BlockSpec((None, TILE, LANES), lambda p, i: (p, i, 0))],
        out_shape=[jax.ShapeDtypeStruct((s, MEM_WIDTH), F32), jax.ShapeDtypeStruct((2, s, LANES), F32)],
        compiler_params=_params("parallel", "parallel"),
    )(pa, mkv, mkv)


def _mem_bwd(name, pa, mkv, out, lse, dout, dcol0):
    s = pa.shape[0]
    ml = mkv.shape[0]
    nb = s // TILE
    cb = QKV_WIDTH // LANES
    db = dcol0 // LANES

    def body(q_ref, k_ref, v_ref, o_ref, lse_ref, do_ref, dq_ref, dk_ref, dv_ref, dk_acc, dv_acc):
        i = pl.program_id(1)

        @pl.when(i == 0)
        def _():
            dk_acc[...] = jnp.zeros_like(dk_acc)
            dv_acc[...] = jnp.zeros_like(dv_acc)

        dqs = []
        for hh in range(2):
            sl = _head_slices(hh)
            q = _scaled_q(q_ref, sl)
            kh = k_ref[:, sl]
            dof = do_ref[:, sl]
            do = dof.astype(BF16)
            delta = jnp.sum(dof * o_ref[:, sl], axis=1, keepdims=True)
            p = jnp.exp(_dot_nt(q, kh) - lse_ref[:, HEAD_DIM * hh:HEAD_DIM * hh + 1])
            ds = (p * (_dot_nt(do, v_ref[:, sl]) - delta)).astype(BF16)
            dv_acc[hh] += _dot_tn(p.astype(BF16), do)
            dk_acc[hh] += _dot_tn(ds, q)
            dqs.append(_dot(ds, kh) * SCALE)
        dq_ref[...] = jnp.concatenate(dqs, axis=1).astype(BF16)

        @pl.when(i == nb - 1)
        def _():
            dk_ref[...] = jnp.concatenate([dk_acc[0], dk_acc[1]], axis=1).astype(BF16)
            dv_ref[...] = jnp.concatenate([dv_acc[0], dv_acc[1]], axis=1).astype(BF16)

    qspec = pl.BlockSpec((TILE, LANES), lambda p, i: (i, p))
    kvspec = pl.BlockSpec((ml, LANES), lambda p, i: (0, p))
    okv = jax.ShapeDtypeStruct((ml, MEM_WIDTH), BF16)
    return pl.pallas_call(
        body, name=name, grid=(2, nb),
        in_specs=[pl.BlockSpec((TILE, LANES), lambda p, i: (i, cb + p)),
                  pl.BlockSpec((ml, LANES), lambda p, i: (0, p)),
                  pl.BlockSpec((ml, LANES), lambda p, i: (0, 2 + p)),
                  qspec,
                  pl.BlockSpec((None, TILE, LANES), lambda p, i: (p, i, 0)),
                  pl.BlockSpec((TILE, LANES), lambda p, i: (i, db + p))],
        out_specs=[qspec, kvspec, kvspec],
        out_shape=[jax.ShapeDtypeStruct((s, MEM_WIDTH), BF16), okv, okv],
        scratch_shapes=[pltpu.VMEM((2, ml, HEAD_DIM), F32), pltpu.VMEM((2, ml, HEAD_DIM), F32)],
        compiler_params=_params("arbitrary", "arbitrary"),
    )(pa, mkv, mkv, out, lse, dout)


def _head_maps():
    col = jnp.arange(MIX_WIDTH)[:, None] // HEAD_DIM
    g = (col == jnp.arange(LANES)[None, :]).astype(BF16)
    return g, g.T


def _normed_heads(osb_ref, ofx_ref, om_ref, g_ref, gt_ref):
    y = jnp.concatenate([osb_ref[...], ofx_ref[...], om_ref[...]], axis=1)
    msq = _sum_l2(y * y, g_ref[...]) * (1.0 / HEAD_DIM)
    rf = _sum_l3(lax.rsqrt(msq + EPS), gt_ref[...])
    return y * rf, rf


def _out_fwd(name, o_sb, o_fx, o_m, pb, ow, x, w_out, ts):
    s, d = x.shape
    g, gt = _head_maps()

    def body(osb_ref, ofx_ref, om_ref, gate_ref, ow_ref, x_ref, w_ref, g_ref, gt_ref, xo_ref, y2_ref):
        yh, _ = _normed_heads(osb_ref, ofx_ref, om_ref, g_ref, gt_ref)
        gate = gate_ref[...]
        y2 = (yh * ow_ref[...] * (gate * jax.nn.sigmoid(gate))).astype(BF16)
        y2_ref[...] = y2
        xo_ref[...] = x_ref[...] + _dot(y2, w_ref[...])

    return pl.pallas_call(
        body, name=name, grid=(s // ts,),
        in_specs=[_row_spec(ts, SB_WIDTH), _row_spec(ts, FOX_WIDTH), _row_spec(ts, MEM_WIDTH),
                  _row_spec(ts, MIX_WIDTH), _const_spec((1, MIX_WIDTH)), _row_spec(ts, d),
                  _const_spec((MIX_WIDTH, d)),
                  _const_spec((MIX_WIDTH, LANES)), _const_spec((LANES, MIX_WIDTH))],
        out_specs=[_row_spec(ts, d), _row_spec(ts, MIX_WIDTH)],
        out_shape=[jax.ShapeDtypeStruct((s, d), F32), jax.ShapeDtypeStruct((s, MIX_WIDTH), BF16)],
        compiler_params=_params("parallel"),
    )(o_sb, o_fx, o_m, pb, ow, x, w_out, g, gt)


def _row_spec(ts, w):
    return pl.BlockSpec((ts, w), lambda i: (i, 0))


def _const_spec(shape):
    return pl.BlockSpec(shape, lambda i: (0,) * len(shape))


def _out_bwd(name, dxb, o_sb, o_fx, o_m, pb, ow, w_out, ts):
    s, d = dxb.shape
    g, gt = _head_maps()

    def body(dx_ref, osb_ref, ofx_ref, om_ref, gate_ref, ow_ref, w_ref, g_ref, gt_ref, dy_ref, dgate_ref, dow_ref):
        @pl.when(pl.program_id(0) == 0)
        def _():
            dow_ref[...] = jnp.zeros_like(dow_ref)

        dy2 = _dot_nt(dx_ref[...], w_ref[...])
        yh, rf = _normed_heads(osb_ref, ofx_ref, om_ref, g_ref, gt_ref)
        gate = gate_ref[...]
        sig = jax.nn.sigmoid(gate)
        ow_v = ow_ref[...]
        dgate_ref[...] = (dy2 * (yh * ow_v) * (sig * (1.0 + gate * (1.0 - sig)))).astype(BF16)
        dn = dy2 * (gate * sig)
        dow_ref[...] += jnp.sum(dn * yh, axis=0, keepdims=True)
        dyh = dn * ow_v
        t = _sum_l2(dyh * yh, g_ref[...]) * (1.0 / HEAD_DIM)
        dy_ref[...] = rf * (dyh - yh * _sum_l3(t, gt_ref[...]))

    return pl.pallas_call(
        body, name=name, grid=(s // ts,),
        in_specs=[_row_spec(ts, d), _row_spec(ts, SB_WIDTH), _row_spec(ts, FOX_WIDTH), _row_spec(ts, MEM_WIDTH),
                  _row_spec(ts, MIX_WIDTH), _const_spec((1, MIX_WIDTH)),
                  _const_spec((MIX_WIDTH, d)),
                  _const_spec((MIX_WIDTH, LANES)), _const_spec((LANES, MIX_WIDTH))],
        out_specs=[_row_spec(ts, MIX_WIDTH), _row_spec(ts, MIX_WIDTH), _const_spec((1, MIX_WIDTH))],
        out_shape=[jax.ShapeDtypeStruct((s, MIX_WIDTH), F32), jax.ShapeDtypeStruct((s, MIX_WIDTH), BF16),
                   jax.ShapeDtypeStruct((1, MIX_WIDTH), F32)],
        compiler_params=_params("arbitrary"),
    )(dxb, o_sb, o_fx, o_m, pb, ow, w_out, g, gt)


def _adamw(name, w, g, m, v, tr):
    def body(w_ref, g_ref, m_ref, v_ref, d_ref, m2_ref, v2_ref):
        gv = g_ref[...]
        m2 = ADAM_B1 * m_ref[...] + (1.0 - ADAM_B1) * gv
        v2 = ADAM_B2 * v_ref[...] + (1.0 - ADAM_B2) * (gv * gv)
        m_hat = m2 / (1.0 - ADAM_B1 ** ADAM_STEP)
        v_hat = v2 / (1.0 - ADAM_B2 ** ADAM_STEP)
        d_ref[...] = -ADAM_LR * (m_hat / (jnp.sqrt(v_hat) + ADAM_EPS) + ADAM_WD * w_ref[...])
        m2_ref[...] = m2
        v2_ref[...] = v2

    rest = w.shape[1:]
    spec = pl.BlockSpec((tr,) + rest, lambda i: (i,) + (0,) * len(rest))
    shp = jax.ShapeDtypeStruct(w.shape, F32)
    return pl.pallas_call(
        body, name=name, grid=(w.shape[0] // tr,), in_specs=[spec] * 4, out_specs=[spec] * 3, out_shape=[shp] * 3,
        compiler_params=_params("parallel"),
    )(w, g, m, v)


def _adamw_sharded(name, w, m, v, g_own, g_other, cvec, tr):
    depth, rows, cols = w.shape
    nt = rows // 2 // tr

    def body(c_ref, w_ref, m_ref, v_ref, *rest):
        g_refs, (g_ref, d_ref, m2_ref, v2_ref) = rest[:2 * depth], rest[2 * depth:]
        layer, mine = pl.program_id(0), pl.program_id(1) == c_ref[0]
        gv = None
        for lt in range(depth):
            cand = jnp.where(mine, g_refs[lt][...], g_refs[depth + lt][...])
            gv = cand if gv is None else jnp.where(layer == lt, cand, gv)
        m2 = ADAM_B1 * m_ref[...] + (1.0 - ADAM_B1) * gv
        v2 = ADAM_B2 * v_ref[...] + (1.0 - ADAM_B2) * (gv * gv)
        m_hat = m2 / (1.0 - ADAM_B1 ** ADAM_STEP)
        v_hat = v2 / (1.0 - ADAM_B2 ** ADAM_STEP)
        g_ref[...] = gv
        d_ref[...] = -ADAM_LR * (m_hat / (jnp.sqrt(v_hat) + ADAM_EPS) + ADAM_WD * w_ref[...])
        m2_ref[...] = m2
        v2_ref[...] = v2

    def g_map(lt, own):
        def index(l, hf, i, c_ref):
            use = jnp.logical_and(l == lt, (hf == c_ref[0]) == own)
            return jnp.where(use, i, 0), 0
        return index

    full = pl.BlockSpec((None, tr, cols), lambda l, hf, i, c_ref: (l, hf * nt + i, 0))
    g_specs = [pl.BlockSpec((tr, cols), g_map(lt, own)) for own in (True, False) for lt in range(depth)]
    shp = jax.ShapeDtypeStruct((depth, rows, cols), F32)
    return pl.pallas_call(
        body, name=name,
        grid_spec=pltpu.PrefetchScalarGridSpec(
            num_scalar_prefetch=1, grid=(depth, 2, nt), in_specs=[full] * 3 + g_specs, out_specs=[full] * 4),
        out_shape=[shp] * 4,
        compiler_params=_params("arbitrary", "arbitrary", "arbitrary"),
    )(cvec, w, m, v, *g_own, *g_other)


HBM_SPEC = pl.BlockSpec(memory_space=pltpu.HBM)


def _place():
    x, y, c = lax.axis_index("x"), lax.axis_index("y"), lax.axis_index("c")
    chips = [(1 - x, y), (x, 1 - y), (1 - x, 1 - y)]
    return x, y, c, chips


def _remote(src, dst, send_sems, recv_sems, k, to):
    return pltpu.make_async_remote_copy(src_ref=src, dst_ref=dst, send_sem=send_sems.at[k], recv_sem=recv_sems.at[k],
                                        device_id=to, device_id_type=MESH)


def _half_rows(n_rows, cc):
    rh = n_rows // 2
    return pl.ds(pl.multiple_of(cc * rh, 16), rh)


def _dma_sems(n):
    return [pltpu.SemaphoreType.DMA((n,)), pltpu.SemaphoreType.DMA((n,))]


class _Exchange:
    def __init__(self, inputs, out_shapes, n_sems, begin, relay, finish):
        self.inputs, self.out_shapes, self.n_sems = list(inputs), list(out_shapes), n_sems
        self.begin, self.relay, self.finish = begin, relay, finish

    @property
    def n(self):
        return len(self.inputs)

    def split(self, refs):
        return refs[:self.n], refs[self.n:2 * self.n], refs[2 * self.n], refs[2 * self.n + 1]


def _run_exchange(name, ex):
    def body(*refs):
        parts = ex.split(refs)
        for phase in (ex.begin, ex.relay, ex.finish):
            if phase is not None:
                phase(*parts)

    return pl.pallas_call(
        body, name=name, in_specs=[HBM_SPEC] * ex.n, out_specs=[HBM_SPEC] * ex.n, out_shape=ex.out_shapes,
        scratch_shapes=_dma_sems(ex.n_sems),
    )(*ex.inputs)


def _gather_exchange(shards):
    def ici(in_refs, out_refs, send_sems, recv_sems):
        x, y, c, chips = _place()
        return [_remote(in_ref.at[_half_rows(in_ref.shape[0], c)], out_ref.at[2 * x + y, _half_rows(in_ref.shape[0], c)],
                        send_sems, recv_sems, 6 * a + j, (cx, cy, c))
                for a, (in_ref, out_ref) in enumerate(zip(in_refs, out_refs)) for j, (cx, cy) in enumerate(chips)]

    def d2d(out_refs, send_sems, recv_sems, half_of):
        x, y, c, chips = _place()
        cps = []
        for a, out_ref in enumerate(out_refs):
            for j, (cx, cy) in enumerate(chips):
                piece = out_ref.at[2 * cx + cy, _half_rows(out_ref.shape[1], half_of(c))]
                cps.append(_remote(piece, piece, send_sems, recv_sems, 6 * a + 3 + j, (x, y, 1 - c)))
        return cps

    def begin(in_refs, out_refs, send_sems, recv_sems):
        for cp in ici(in_refs, out_refs, send_sems, recv_sems):
            cp.start()

    def relay(in_refs, out_refs, send_sems, recv_sems):
        x, y, c, chips = _place()
        for a, out_ref in enumerate(out_refs):
            for j, (cx, cy) in enumerate(chips):
                landed = out_ref.at[2 * cx + cy, _half_rows(out_ref.shape[1], c)]
                _remote(landed, landed, send_sems, recv_sems, 6 * a + j, (cx, cy, c)).wait_recv()
        for cp in d2d(out_refs, send_sems, recv_sems, lambda c_: c_):
            cp.start()

    def finish(in_refs, out_refs, send_sems, recv_sems):
        for cp in d2d(out_refs, send_sems, recv_sems, lambda c_: 1 - c_):
            cp.wait_recv()
        for cp in ici(in_refs, out_refs, send_sems, recv_sems) + d2d(out_refs, send_sems, recv_sems, lambda c_: c_):
            cp.wait_send()

    shapes = [jax.ShapeDtypeStruct((N_CHIPS,) + s_.shape, s_.dtype) for s_ in shards]
    return _Exchange(shards, shapes, 6 * len(shards), begin, relay, finish)


def _swap_halves(name, g4s):
    n = len(g4s)

    def body(*refs):
        in_refs, out_refs, (send_sems, recv_sems) = refs[:n], refs[n:2 * n], refs[2 * n:]
        x, y, c, _ = _place()
        cps = [_remote(in_ref.at[:, _half_rows(in_ref.shape[1], 1 - c), :], out_ref, send_sems, recv_sems, a, (x, y, 1 - c))
               for a, (in_ref, out_ref) in enumerate(zip(in_refs, out_refs))]
        for cp in cps:
            cp.start()
        for cp in cps:
            cp.wait()

    return pl.pallas_call(
        body, name=name, in_specs=[HBM_SPEC] * n, out_specs=[HBM_SPEC] * n,
        out_shape=[jax.ShapeDtypeStruct((g.shape[0], g.shape[1] // 2, g.shape[2]), g.dtype) for g in g4s],
        scratch_shapes=_dma_sems(n),
    )(*g4s)


def _add_half(name, g4, r1, cvec, tr):
    n, r, w = g4.shape
    rh = r // 2
    nblk = rh // tr

    def body(c_ref, a_ref, b_ref, o_ref):
        o_ref[...] = (a_ref[...] + b_ref[...]).astype(BF16)

    return pl.pallas_call(
        body, name=name,
        grid_spec=pltpu.PrefetchScalarGridSpec(
            num_scalar_prefetch=1, grid=(n, nblk),
            in_specs=[pl.BlockSpec((None, tr, w), lambda k, i, c_ref: (k, c_ref[0] * nblk + i, 0)),
                      pl.BlockSpec((None, tr, w), lambda k, i, c_ref: (k, i, 0))],
            out_specs=pl.BlockSpec((None, tr, w), lambda k, i, c_ref: (k, i, 0))),
        out_shape=jax.ShapeDtypeStruct((n, rh, w), BF16),
        compiler_params=_params("parallel", "parallel"),
    )(cvec, g4, r1)


def _scatter_exchange(h4s):
    def sends(in_refs, out_refs, send_sems, recv_sems):
        x, y, c, chips = _place()
        return [_remote(in_ref.at[2 * cx + cy], out_ref.at[j], send_sems, recv_sems, 3 * a + j, (cx, cy, c))
                for a, (in_ref, out_ref) in enumerate(zip(in_refs, out_refs)) for j, (cx, cy) in enumerate(chips)]

    def begin(*parts):
        for cp in sends(*parts):
            cp.start()

    def finish(in_refs, out_refs, send_sems, recv_sems):
        x, y, c, chips = _place()
        for a, out_ref in enumerate(out_refs):
            for j, (cx, cy) in enumerate(chips):
                got = out_ref.at[j]
                _remote(got, got, send_sems, recv_sems, 3 * a + j, (cx, cy, c)).wait_recv()
        for cp in sends(in_refs, out_refs, send_sems, recv_sems):
            cp.wait_send()

    shapes = [jax.ShapeDtypeStruct((3,) + h.shape[1:], h.dtype) for h in h4s]
    return _Exchange(h4s, shapes, 3 * len(h4s), begin, None, finish)


def _sum_chips(name, h4, r3, mvec, tr):
    _, rh, w = h4.shape

    def body(m_ref, a_ref, b_ref, c_ref, d_ref, o_ref):
        o_ref[...] = ((a_ref[...].astype(F32) + b_ref[...].astype(F32)) + c_ref[...].astype(F32)) + d_ref[...].astype(F32)

    specs = [pl.BlockSpec((None, tr, w), lambda i, m_ref: (m_ref[0], i, 0))]
    specs += [pl.BlockSpec((None, tr, w), functools.partial(lambda k, i, m_ref: (k, i, 0), k)) for k in range(3)]
    return pl.pallas_call(
        body, name=name,
        grid_spec=pltpu.PrefetchScalarGridSpec(
            num_scalar_prefetch=1, grid=(rh // tr,), in_specs=specs,
            out_specs=pl.BlockSpec((tr, w), lambda i, m_ref: (i, 0))),
        out_shape=jax.ShapeDtypeStruct((rh, w), F32),
        compiler_params=_params("parallel"),
    )(mvec, h4, r3, r3, r3)


def _swap_reduced(name, ghs):
    n = len(ghs)

    def body(*refs):
        in_refs, out_refs, (send_sems, recv_sems) = refs[:n], refs[n:2 * n], refs[2 * n:]
        x, y, c, _ = _place()
        cps = [_remote(in_ref, out_ref, send_sems, recv_sems, a, (x, y, 1 - c))
               for a, (in_ref, out_ref) in enumerate(zip(in_refs, out_refs))]
        for cp in cps:
            cp.start()
        for cp in cps:
            cp.wait()

    return pl.pallas_call(
        body, name=name, in_specs=[HBM_SPEC] * n, out_specs=[HBM_SPEC] * n,
        out_shape=[jax.ShapeDtypeStruct(g.shape, g.dtype) for g in ghs],
        scratch_shapes=_dma_sems(n),
    )(*ghs)


def _small_update(name, partials, weights, moments1, moments2):
    n = len(partials)
    width = max(p.shape[1] for p in partials)
    starts, at = [], 0
    for p in partials:
        starts.append(at)
        at += p.shape[0]
    rows = -(-at // 8) * 8
    has_w = [w is not None for w in weights]
    n_w = sum(has_w)

    def body(*refs):
        p_refs = refs[:n]
        w_refs, m_refs, v_refs = refs[n:n + n_w], refs[n + n_w:n + 2 * n_w], refs[n + 2 * n_w:n + 3 * n_w]
        outs = refs[n + 3 * n_w:-4]
        g_refs, upd_refs = outs[:n], outs[n:]
        vec, buf, send_sems, recv_sems = refs[-4:]
        x, y, c, _ = _place()
        me = 4 * x + 2 * y + c
        vec[...] = jnp.zeros_like(vec)
        for p_ref, r0 in zip(p_refs, starts):
            vec[r0:r0 + p_ref.shape[0], 0:p_ref.shape[1]] = p_ref[...]
        buf[me] = vec[...]
        flips = [(fx, fy, fc) for fx in (0, 1) for fy in (0, 1) for fc in (0, 1)][1:]
        peers = [(x + fx - 2 * x * fx, y + fy - 2 * y * fy, c + fc - 2 * c * fc) for fx, fy, fc in flips]
        sends = [_remote(vec, buf.at[me], send_sems, recv_sems, k, peer) for k, peer in enumerate(peers)]
        for cp in sends:
            cp.start()
        for k, (px, py, pc) in enumerate(peers):
            got = buf.at[4 * px + 2 * py + pc]
            _remote(got, got, send_sems, recv_sems, k, (px, py, pc)).wait_recv()
        for cp in sends:
            cp.wait_send()
        total = buf[0]
        for dev in range(1, N_DEV):
            total = total + buf[dev]
        k = 0
        for a in range(n):
            r, w = g_refs[a].shape
            g = total[starts[a]:starts[a] + r, 0:w]
            g_refs[a][...] = g
            if has_w[a]:
                m2 = ADAM_B1 * m_refs[k][...] + (1.0 - ADAM_B1) * g
                v2 = ADAM_B2 * v_refs[k][...] + (1.0 - ADAM_B2) * (g * g)
                m_hat = m2 / (1.0 - ADAM_B1 ** ADAM_STEP)
                v_hat = v2 / (1.0 - ADAM_B2 ** ADAM_STEP)
                upd_refs[3 * k][...] = -ADAM_LR * (m_hat / (jnp.sqrt(v_hat) + ADAM_EPS) + ADAM_WD * w_refs[k][...])
                upd_refs[3 * k + 1][...] = m2
                upd_refs[3 * k + 2][...] = v2
                k += 1

    ws = [w for w in weights if w is not None]
    g_shapes = [jax.ShapeDtypeStruct(p.shape if w is None else w.shape, F32) for p, w in zip(partials, weights)]
    u_shapes = [jax.ShapeDtypeStruct(w.shape, F32) for w in ws for _ in range(3)]
    vm = pl.BlockSpec(memory_space=pltpu.VMEM)
    n_args = n + 3 * n_w
    outs = pl.pallas_call(
        body, name=name, in_specs=[vm] * n_args, out_specs=[vm] * (n + 3 * n_w), out_shape=g_shapes + u_shapes,
        scratch_shapes=[pltpu.VMEM((rows, width), F32), pltpu.VMEM((N_DEV, rows, width), F32),
                        pltpu.SemaphoreType.DMA((7,)), pltpu.SemaphoreType.DMA((7,))],
    )(*partials, *ws, *[m for m in moments1 if m is not None], *[v for v in moments2 if v is not None])
    return outs[:n], outs[n:]


GATE_COL = 3 * SB_WIDTH + 3 * FOX_WIDTH + FOX_HEADS + MEM_WIDTH
FL_COL = QKV_WIDTH


GROUP_A_COLS = [(0, QKV_WIDTH), (FL_COL + FOX_HEADS, MEM_WIDTH)]
GROUP_B_COLS = [(GATE_COL, MIX_WIDTH), (FL_COL, FOX_HEADS)]


def _group_from_shards(shard_of, cw, spans, pad):
    parts = []
    for lo, width in spans:
        hi = lo + width
        for j in range(N_CHIPS):
            a, b = max(lo, j * cw), min(hi, (j + 1) * cw)
            if a < b:
                parts.append(shard_of(j)[:, a - j * cw:b - j * cw])
    if pad:
        parts.append(jnp.zeros((parts[0].shape[0], pad), parts[0].dtype))
    return jnp.concatenate(parts, axis=1)


def _shard_from_groups(ga, gb, j, cw):
    lo, hi = j * cw, (j + 1) * cw
    placed = []
    for grp, spans in ((ga, GROUP_A_COLS), (gb, GROUP_B_COLS)):
        at = 0
        for first, width in spans:
            a, b = max(lo, first), min(hi, first + width)
            if a < b:
                placed.append((a, grp[:, at + a - first:at + b - first]))
            at += width
    return jnp.concatenate([p for _, p in sorted(placed, key=lambda t: t[0])], axis=1)


def _tile_of(n, cap, unit):
    if n <= cap:
        return n
    best = None
    for t in range(unit, cap + 1, unit):
        if n % t == 0:
            best = t
    assert best is not None, (n, cap, unit)
    return best


def _column_major_rows(a):
    dp, r, c = a.shape
    return a.transpose(2, 0, 1).reshape(c, dp, r // LANES, LANES).transpose(0, 2, 1, 3).reshape(-1, 8, LANES)


def _from_column_major_rows(b, shape):
    dp, r, c = shape
    return b.reshape(c, r // LANES, dp, LANES).transpose(0, 2, 1, 3).reshape(c, dp, r).transpose(1, 2, 0)


def _pack_small(parts):
    rows = []
    for p in parts:
        f = p.reshape(-1).astype(F32)
        f = jnp.pad(f, (0, (-f.shape[0]) % LANES))
        rows.append(f.reshape(-1, LANES))
    out = jnp.concatenate(rows, axis=0)
    return jnp.pad(out, ((0, (-out.shape[0]) % 8), (0, 0)))


def _unpack_small(packed, shapes):
    outs, r = [], 0
    for shp in shapes:
        n = 1
        for s_ in shp:
            n *= s_
        nr = -(-n // LANES)
        outs.append(packed[r:r + nr].reshape(-1)[:n].reshape(shp))
        r += nr
    return outs


def kernel(x, mem, norm_w, w_in, b_forget, mem_norm_w, w_mem_kv, out_norm_w, w_out, final_norm_w, loss_target, m_norm_w, m_w_in, m_b_forget, m_mem_norm_w, m_w_mem_kv, m_out_norm_w, m_w_out, m_final_norm_w, v_norm_w, v_w_in, v_b_forget, v_mem_norm_w, v_w_mem_kv, v_out_norm_w, v_w_out, v_final_norm_w):
    xs = x[0]
    mems = mem[0]
    target = loss_target[0]
    s, d = xs.shape
    depth = norm_w.shape[0]
    nb = s // TILE
    ts = _tile_of(s, 256, 8)
    big = (w_in, w_mem_kv, w_out)
    core = lax.axis_index("c")
    chip = 2 * lax.axis_index("x") + lax.axis_index("y")
    cvec = core.astype(jnp.int32).reshape(1)
    mvec = chip.astype(jnp.int32).reshape(1)
    cw = w_in.shape[2]

    own_w = [[a[l].astype(BF16) for a in big] for l in range(depth)]

    def lay_out(own, got):
        full = [jnp.where(lax.broadcasted_iota(jnp.int32, g.shape, 0) == chip, o[None], g)
                for g, o in zip(got[1:], own[1:])]
        shard_of = lambda j: jnp.where(chip == j, own[0], got[0][j])
        wa_l = _group_from_shards(shard_of, cw, GROUP_A_COLS, 0)
        wb_l = _group_from_shards(shard_of, cw, GROUP_B_COLS, LANES - FOX_HEADS)
        return wa_l, wb_l, full[0].reshape(-1, full[0].shape[2]), full[1].reshape(-1, full[1].shape[2])

    layer_w = [lay_out(own_w[0], _run_exchange("gather_weights0", _gather_exchange(own_w[0])))]

    tm = _tile_of(s, 256, 8)
    fl_block = MIX_WIDTH // LANES

    saved = []
    cur = xs
    for l in range(depth):
        wa, wb, wkv, wout = layer_w[l]
        h = _rms_fwd(f"rms_fwd{l}", cur, norm_w[l][None], ts)
        pa = _mm(f"inproj_a{l}", h, wa, "nn", tm, _tile_of(PA, 1664, LANES), BF16)
        pb = _mm(f"inproj_b{l}", h, wb, "nn", tm, PB, F32)
        bpad = jnp.pad(b_forget[l], (0, LANES - FOX_HEADS))[None]
        ccol, crow = _gate_fwd(f"gate_fwd{l}", pb, bpad, fl_block)
        fg = FOX_HEADS // FOX_GROUP
        ccol4 = jnp.repeat(ccol[:, :FOX_HEADS].reshape(s, fg, FOX_GROUP).transpose(1, 0, 2), HEAD_DIM, axis=2)
        crow4 = jnp.pad(crow.reshape(nb, fg, FOX_GROUP, TILE).transpose(1, 0, 2, 3),
                        ((0, 0), (0, 0), (0, 8 - FOX_GROUP), (0, 0)))
        more = l + 1 < depth
        o_sb, got_in = _sb_fwd(f"sb_fwd{l}", pa, 0, carried=_gather_exchange(own_w[l + 1][:1]) if more else None)
        o_fx, lse_fx, got_rest = _fox_fwd(f"fox_fwd{l}", pa, 3 * SB_WIDTH, ccol4, crow4,
                                          carried=_gather_exchange(own_w[l + 1][1:]) if more else None)
        if more:
            layer_w.append(lay_out(own_w[l + 1], list(got_in) + list(got_rest)))
        mn = _rms_fwd(f"mem_rms{l}", mems, mem_norm_w[l][None], mems.shape[0])
        mkv = _mm(f"mem_kv{l}", mn, wkv, "nn", mems.shape[0], 2 * MEM_WIDTH, BF16)
        o_m, lse_m = _mem_fwd(f"mem_fwd{l}", pa, mkv)
        nxt, y2 = _out_fwd(f"out_fwd{l}", o_sb, o_fx, o_m, pb, out_norm_w[l][None], cur, wout, ts)
        saved.append((cur, h, pa, pb, bpad, ccol4, crow4, o_sb, o_fx, lse_fx, mn, mkv, o_m, lse_m, y2))
        cur = nxt

    loss_v, dx, dxb, g_final = _final_loss("final_loss", cur, final_norm_w[None], target, ts)

    g_norm, g_b, g_memnorm, g_outnorm = [None] * depth, [None] * depth, [None] * depth, [None] * depth
    g_wa, g_wb, g_wkv, g_wout = [None] * depth, [None] * depth, [None] * depth, [None] * depth
    g_own = [[None] * depth for _ in big]
    g_other = [[None] * depth for _ in big]

    def within_chip(tag, jobs):
        got = _swap_halves(f"grad_swap_halves{tag}", [g for _, _, g, _ in jobs])
        return [(lr, k, _add_half(f"grad_add_half{lr}_{k}", g, r_, cvec, t_), t_) for (lr, k, g, t_), r_ in zip(jobs, got)]

    def reduce_at_owner(tag, jobs, from_chips):
        halves = [_sum_chips(f"grad_sum_chips{lr}_{k}", h_, r_, mvec, t_) for (lr, k, h_, t_), r_ in zip(jobs, from_chips)]
        others = _swap_reduced(f"grad_swap_reduced{tag}", halves)
        for (lr, k, _, _), mine, other in zip(jobs, halves, others):
            g_own[k][lr], g_other[k][lr] = mine, other

    def job(lr, k, g4):
        return lr, k, g4, _tile_of(g4.shape[1] // 2, 256, 16)

    pending = []
    for l in reversed(range(depth)):
        xin, h, pa, pb, bpad, ccol4, crow4, o_sb, o_fx, lse_fx, mn, mkv, o_m, lse_m, y2 = saved[l]
        wa, wb, wkv, wout = layer_w[l]
        dy, dgate, g_outnorm[l] = _out_bwd(f"out_bwd{l}", dxb, o_sb, o_fx, o_m, pb, out_norm_w[l][None], wout, ts)
        g_wout[l] = _mm(f"dw_out{l}", y2, dxb, "tn", _tile_of(MIX_WIDTH, 640, LANES), d, F32)
        dq_m, dk_m, dv_m = _mem_bwd(f"mem_bwd{l}", pa, mkv, o_m, lse_m, dy, SB_WIDTH + FOX_WIDTH)
        dmkv = jnp.concatenate([dk_m, dv_m], axis=1)
        g_wkv[l] = _mm(f"dw_kv{l}", mn, dmkv, "tn", d, 2 * MEM_WIDTH, F32)
        dmn = _mm(f"dmem{l}", dmkv, wkv, "nt", mems.shape[0], d, F32)
        g_memnorm[l] = _rms_wgrad(f"mem_norm_grad{l}", mems, dmn)
        pending += within_chip(f"{l}s", [job(l, 1, g_wkv[l].reshape(N_CHIPS, -1, g_wkv[l].shape[1])),
                                         job(l, 2, g_wout[l].reshape(N_CHIPS, -1, d))])
        (dq_sb, dk_sb, dv_sb), from_chips = _sb_bwd(f"sb_bwd{l}", pa, 0, dy, 0,
                                                   carried=_scatter_exchange([j[2] for j in pending]))
        reduce_at_owner(f"{l}s", pending, from_chips)
        dq_fx, dk_fx, dv_fx, cs4 = _fox_bwd(f"fox_bwd{l}", pa, 3 * SB_WIDTH, ccol4, crow4, o_fx, lse_fx, dy, SB_WIDTH)
        colsum = cs4[:, :, :FOX_GROUP, :].transpose(1, 0, 2, 3).reshape(nb, 8, TILE)
        dlogit, g_b[l] = _gate_bwd(f"gate_bwd{l}", pb, bpad, colsum, fl_block)
        dpa = jnp.concatenate([dq_sb, dk_sb, dv_sb, dq_fx, dk_fx, dv_fx, dq_m], axis=1)
        dpb = jnp.concatenate([dgate, dlogit], axis=1)
        tw = _tile_of(d, 512, LANES)
        g_wa[l] = _mm(f"dw_in_a{l}", h, dpa, "tn", tw, _tile_of(PA, 1664, LANES), F32)
        g_wb[l] = _mm(f"dw_in_b{l}", h, dpb, "tn", tw, PB, F32)
        dh = _mm(f"dh_a{l}", dpa, wa, "nt", tm, d, F32)
        dh = _mm(f"dh_b{l}", dpb, wb, "nt", tm, d, F32, res=dh)
        dx, dxb, g_norm[l] = _rms_bwd(f"rms_bwd{l}", xin, norm_w[l][None], dh, dx, ts)
        g4_in = jnp.stack([_shard_from_groups(g_wa[l], g_wb[l], j, cw) for j in range(N_CHIPS)])
        pending = within_chip(f"{l}", [job(l, 0, g4_in)])
    reduce_at_owner("last", pending, _run_exchange("grad_scatter_chips_last", _scatter_exchange([j[2] for j in pending])))

    small_w = [norm_w, b_forget, mem_norm_w, out_norm_w, final_norm_w]
    small_m = [m_norm_w, m_b_forget, m_mem_norm_w, m_out_norm_w, m_final_norm_w]
    small_v = [v_norm_w, v_b_forget, v_mem_norm_w, v_out_norm_w, v_final_norm_w]
    rows2 = lambda a: a.reshape(-1, a.shape[-1])
    partials = [jnp.concatenate(g_norm, axis=0), jnp.concatenate(g_b, axis=0), jnp.concatenate(g_memnorm, axis=0),
                jnp.concatenate(g_outnorm, axis=0), g_final, loss_v]
    sums, updates = _small_update("small_update", partials, [rows2(a) for a in small_w] + [None],
                                  [rows2(a) for a in small_m] + [None], [rows2(a) for a in small_v] + [None])
    small_grads = [g.reshape(a.shape) for g, a in zip(sums, small_w)]
    loss = sums[-1][0, 0]
    small_delta, small_m2, small_v2 = ([updates[3 * k + t].reshape(a.shape) for k, a in enumerate(small_w)]
                                       for t in range(3))
    big_grads, big_delta, big_m2, big_v2 = [], [], [], []
    for k, (nm, w_, m_, v_) in enumerate(zip(("w_in", "w_mem_kv", "w_out"), big, (m_w_in, m_w_mem_kv, m_w_out),
                                             (v_w_in, v_w_mem_kv, v_w_out))):
        if w_.shape[2] % LANES:
            g_full = jnp.stack([jnp.concatenate([jnp.where(core == 0, go, gt), jnp.where(core == 0, gt, go)], axis=0)
                                for go, gt in zip(g_own[k], g_other[k])])
            w_p, g_p, m_p, v_p = (_column_major_rows(a) for a in (w_, g_full, m_, v_))
            outs = _adamw(f"adamw_{nm}", w_p, g_p, m_p, v_p, _tile_of(w_p.shape[0], 600, 1))
            outs = [_from_column_major_rows(o, w_.shape) for o in (g_p, *outs)]
        else:
            outs = _adamw_sharded(f"adamw_{nm}", w_, m_, v_, g_own[k], g_other[k], cvec,
                                  _tile_of(w_.shape[1] // 2, 256, 8))
        for lst, o in zip((big_grads, big_delta, big_m2, big_v2), outs):
            lst.append(o)

    def order(sm, bg):
        return [sm[0], bg[0], sm[1], sm[2], bg[1], sm[3], bg[2], sm[4]]

    return (loss, dx[None], *order(small_grads, big_grads), *order(small_delta, big_delta),
            *order(small_m2, big_m2), *order(small_v2, big_v2))
```

```python
import functools

import jax
import jax.numpy as jnp
from jax import lax
from jax.experimental import pallas as pl
from jax.experimental.pallas import tpu as pltpu

F32 = jnp.float32
BF16 = jnp.bfloat16

HEAD_DIM = 64
SB_WIDTH = 512
FOX_WIDTH = 512
FOX_HEADS = 8
MEM_WIDTH = 256
MIX_WIDTH = SB_WIDTH + FOX_WIDTH + MEM_WIDTH
TOTAL_HEADS = MIX_WIDTH // HEAD_DIM
IN_WIDTH = 3 * SB_WIDTH + 3 * FOX_WIDTH + FOX_HEADS + MEM_WIDTH + MIX_WIDTH
LANES = 128
QKV_WIDTH = 3 * SB_WIDTH + 3 * FOX_WIDTH
PA = QKV_WIDTH + MEM_WIDTH
PB = LANES + MIX_WIDTH
EPS = 1e-6
SCALE = HEAD_DIM ** -0.5
TILE = 256
SB_GROUP = 4
SB_LANES = SB_GROUP * HEAD_DIM
FOX_GROUP = 4
FOX_LANES = FOX_GROUP * HEAD_DIM
NEG_INF = float("-inf")
MASKED = -1e30

ADAM_LR = 0.001
ADAM_B1 = 0.9
ADAM_B2 = 0.999
ADAM_EPS = 1e-08
ADAM_WD = 0.01
ADAM_STEP = 10

N_CHIPS = 4
N_DEV = 8
VMEM_LIMIT = 48 * 1024 * 1024
MESH = pl.DeviceIdType.MESH


def _params(*sem):
    return pltpu.CompilerParams(dimension_semantics=tuple(sem), vmem_limit_bytes=VMEM_LIMIT)


def _dot(a, b):
    return jnp.dot(a, b, preferred_element_type=F32)


def _dot_nt(a, b):
    return lax.dot_general(a, b, (((1,), (1,)), ((), ())), preferred_element_type=F32)


def _dot_tn(a, b):
    return lax.dot_general(a, b, (((0,), (0,)), ((), ())), preferred_element_type=F32)


def _split2(x):
    hi = x.astype(BF16)
    lo = (x - hi.astype(F32)).astype(BF16)
    return hi, lo


def _split3(x):
    hi = x.astype(BF16)
    r = x - hi.astype(F32)
    mid = r.astype(BF16)
    lo = (r - mid.astype(F32)).astype(BF16)
    return hi, mid, lo


def _sum_l2(x, u):
    hi, lo = _split2(x)
    return _dot(hi, u) + _dot(lo, u)


def _sum_l3(x, u):
    hi, mid, lo = _split3(x)
    return _dot(hi, u) + _dot(mid, u) + _dot(lo, u)


def _sum_r3(u, x):
    hi, mid, lo = _split3(x)
    return _dot(u, hi) + _dot(u, mid) + _dot(u, lo)


def _softplus(z):
    return jnp.maximum(z, 0.0) + jnp.log1p(jnp.exp(-jnp.abs(z)))


def _tri(n, pred):
    r = lax.broadcasted_iota(jnp.int32, (n, n), 0)
    c = lax.broadcasted_iota(jnp.int32, (n, n), 1)
    return jnp.where(pred(r, c), 1.0, 0.0).astype(BF16)


def _rows(ref, j, n=TILE):
    return pl.ds(pl.multiple_of(j * n, n), n)


def _mm(name, a, b, mode, tm, tn, out_dtype, res=None, a_lead=(), b_lead=()):
    a2, b2 = a.shape[len(a_lead):], b.shape[len(b_lead):]
    if mode == "tn":
        k, m = a2
    else:
        m, k = a2
    n = b2[0] if mode == "nt" else b2[1]
    assert m % tm == 0 and n % tn == 0, (name, m, tm, n, tn)
    na, nb = (None,) * len(a_lead), (None,) * len(b_lead)
    if mode == "tn":
        a_spec = pl.BlockSpec(na + (k, tm), lambda j, i: a_lead + (0, i))
    else:
        a_spec = pl.BlockSpec(na + (tm, k), lambda j, i: a_lead + (i, 0))
    if mode == "nt":
        b_spec = pl.BlockSpec(nb + (tn, k), lambda j, i: b_lead + (j, 0))
    else:
        b_spec = pl.BlockSpec(nb + (k, tn), lambda j, i: b_lead + (0, j))
    o_spec = pl.BlockSpec((tm, tn), lambda j, i: (i, j))
    dot = {"nn": _dot, "nt": _dot_nt, "tn": _dot_tn}[mode]

    def body(a_ref, b_ref, *rest):
        o_ref = rest[-1]
        acc = dot(a_ref[...].astype(BF16), b_ref[...].astype(BF16))
        if res is not None:
            acc = acc + rest[0][...]
        o_ref[...] = acc.astype(o_ref.dtype)

    args, specs = [a, b], [a_spec, b_spec]
    if res is not None:
        args.append(res)
        specs.append(o_spec)
    return pl.pallas_call(
        body, name=name, grid=(n // tn, m // tm), in_specs=specs, out_specs=o_spec,
        out_shape=jax.ShapeDtypeStruct((m, n), out_dtype),
        compiler_params=_params("parallel", "parallel"),
    )(*args)


def _rms_fwd(name, x, g, ts):
    s, d = x.shape

    def body(x_ref, g_ref, o_ref):
        xf = x_ref[...]
        r = lax.rsqrt(jnp.mean(xf * xf, axis=1, keepdims=True) + EPS)
        o_ref[...] = (xf * r * g_ref[...]).astype(BF16)

    return pl.pallas_call(
        body, name=name, grid=(s // ts,),
        in_specs=[pl.BlockSpec((ts, d), lambda i: (i, 0)), pl.BlockSpec((1, d), lambda i: (0, 0))],
        out_specs=pl.BlockSpec((ts, d), lambda i: (i, 0)),
        out_shape=jax.ShapeDtypeStruct((s, d), BF16),
        compiler_params=_params("parallel"),
    )(x, g)


def _inproj_bwd(name, dpa, dpb, wa, wb, x, g, dres, ts, carried=None):
    s, d = x.shape

    def body(dpa_ref, dpb_ref, wa_ref, wb_ref, x_ref, g_ref, dres_ref, dx_ref, dxb_ref, dg_ref):
        @pl.when(pl.program_id(1) == 0)
        def _():
            dg_ref[...] = jnp.zeros_like(dg_ref)

        dhf = _dot_nt(dpa_ref[...], wa_ref[...]) + _dot_nt(dpb_ref[...], wb_ref[...])
        xf = x_ref[...]
        r = lax.rsqrt(jnp.mean(xf * xf, axis=1, keepdims=True) + EPS)
        xh = xf * r
        dg_ref[...] += jnp.sum(dhf * xh, axis=0, keepdims=True)
        dxh = dhf * g_ref[...]
        m = jnp.mean(dxh * xh, axis=1, keepdims=True)
        dx = r * (dxh - xh * m) + dres_ref[...]
        dx_ref[...] = dx
        dxb_ref[...] = dx.astype(BF16)

    row = lambda w: pl.BlockSpec((ts, w), lambda p, i: (i, 0))
    whole = lambda a: pl.BlockSpec(a.shape, lambda p, i: (0, 0))
    outs = _pair_grid_call(
        name, body, s // ts,
        in_specs=[row(dpa.shape[1]), row(dpb.shape[1]), whole(wa), whole(wb), row(d), whole(g), row(d)],
        out_specs=[row(d), row(d), pl.BlockSpec((1, d), lambda p, i: (0, 0))],
        out_shape=[jax.ShapeDtypeStruct((s, d), F32), jax.ShapeDtypeStruct((s, d), BF16),
                   jax.ShapeDtypeStruct((1, d), F32)],
        scratch=[], args=(dpa, dpb, wa, wb, x, g, dres), carried=carried, groups=1)
    return outs[0], outs[1], outs[2], outs[3:]


def _rms_wgrad(name, x, dh):
    m_, d = x.shape

    def body(x_ref, dh_ref, dg_ref):
        xf = x_ref[...]
        r = lax.rsqrt(jnp.mean(xf * xf, axis=1, keepdims=True) + EPS)
        dg_ref[...] = jnp.sum(dh_ref[...] * xf * r, axis=0, keepdims=True)

    return pl.pallas_call(
        body, name=name, out_shape=jax.ShapeDtypeStruct((1, d), F32),
    )(x, dh)


def _final_loss(name, x, g, target, ts):
    s, d = x.shape

    def body(x_ref, g_ref, t_ref, loss_ref, dx_ref, dxb_ref, dg_ref):
        @pl.when(pl.program_id(0) == 0)
        def _():
            dg_ref[...] = jnp.zeros_like(dg_ref)
            loss_ref[...] = jnp.zeros_like(loss_ref)

        xf = x_ref[...]
        gw = g_ref[...]
        r = lax.rsqrt(jnp.mean(xf * xf, axis=1, keepdims=True) + EPS)
        xh = xf * r
        e = xh * gw - t_ref[...]
        part = 0.5 * jnp.sum(jnp.mean(e * e, axis=1, keepdims=True), axis=0, keepdims=True)
        loss_ref[...] += jnp.broadcast_to(part, loss_ref.shape)
        dy = e * (1.0 / d)
        dg_ref[...] += jnp.sum(dy * xh, axis=0, keepdims=True)
        dxh = dy * gw
        m = jnp.mean(dxh * xh, axis=1, keepdims=True)
        dx = r * (dxh - xh * m)
        dx_ref[...] = dx
        dxb_ref[...] = dx.astype(BF16)

    row = pl.BlockSpec((ts, d), lambda i: (i, 0))
    vec = pl.BlockSpec((1, d), lambda i: (0, 0))
    lvec = pl.BlockSpec((1, LANES), lambda i: (0, 0))
    return pl.pallas_call(
        body, name=name, grid=(s // ts,), in_specs=[row, vec, row], out_specs=[lvec, row, row, vec],
        out_shape=[jax.ShapeDtypeStruct((1, LANES), F32), jax.ShapeDtypeStruct((s, d), F32),
                   jax.ShapeDtypeStruct((s, d), BF16), jax.ShapeDtypeStruct((1, d), F32)],
        compiler_params=_params("arbitrary"),
    )(x, g, target)


def _gate_fwd(name, pb, bpad, fl_block):
    s = pb.shape[0]
    nb = s // TILE

    def body(fl_ref, b_ref, ccol_ref, crow_ref, carry):
        @pl.when(pl.program_id(0) == 0)
        def _():
            carry[...] = jnp.zeros_like(carry)

        u = fl_ref[...] + b_ref[...]
        lf = jnp.minimum(u, 0.0) - jnp.log1p(jnp.exp(-jnp.abs(u)))
        lower = _tri(TILE, lambda r, c: c <= r)
        c = _sum_r3(lower, lf) + carry[0:1, :]
        ccol_ref[...] = c
        crow_ref[0] = c.T[0:8, :]
        carry[...] = jnp.broadcast_to(c[TILE - 1:TILE, :], carry.shape)

    return pl.pallas_call(
        body, name=name, grid=(nb,),
        in_specs=[pl.BlockSpec((TILE, LANES), lambda i: (i, fl_block)), pl.BlockSpec((1, LANES), lambda i: (0, 0))],
        out_specs=[pl.BlockSpec((TILE, LANES), lambda i: (i, 0)), pl.BlockSpec((1, 8, TILE), lambda i: (i, 0, 0))],
        out_shape=[jax.ShapeDtypeStruct((s, LANES), F32), jax.ShapeDtypeStruct((nb, 8, TILE), F32)],
        scratch_shapes=[pltpu.VMEM((8, LANES), F32)],
        compiler_params=_params("arbitrary"),
    )(pb, bpad)


def _gate_bwd(name, pb, bpad, colsum, fl_block):
    s = pb.shape[0]
    nb = s // TILE

    def body(fl_ref, b_ref, cs_ref, dl_ref, db_ref, carry):
        @pl.when(pl.program_id(0) == 0)
        def _():
            carry[...] = jnp.zeros_like(carry)
            db_ref[...] = jnp.zeros_like(db_ref)

        upper = _tri(TILE, lambda r, c: r >= c)
        rsum = _sum_l3(cs_ref[0], upper) + carry[:, 0:1]
        carry[...] = jnp.broadcast_to(rsum[:, 0:1], carry.shape)
        full = jnp.concatenate([rsum, jnp.zeros((LANES - 8, TILE), F32)], axis=0)
        dlf = -full.T
        u = fl_ref[...] + b_ref[...]
        dlogit = dlf * (1.0 - jax.nn.sigmoid(u))
        dl_ref[...] = dlogit.astype(BF16)
        db_ref[...] += jnp.sum(dlogit, axis=0, keepdims=True)

    rev = lambda i: (nb - 1 - i, 0)
    return pl.pallas_call(
        body, name=name, grid=(nb,),
        in_specs=[pl.BlockSpec((TILE, LANES), lambda i: (nb - 1 - i, fl_block)),
                  pl.BlockSpec((1, LANES), lambda i: (0, 0)),
                  pl.BlockSpec((1, 8, TILE), lambda i: (nb - 1 - i, 0, 0))],
        out_specs=[pl.BlockSpec((TILE, LANES), rev), pl.BlockSpec((1, LANES), lambda i: (0, 0))],
        out_shape=[jax.ShapeDtypeStruct((s, LANES), BF16), jax.ShapeDtypeStruct((1, LANES), F32)],
        scratch_shapes=[pltpu.VMEM((8, LANES), F32)],
        compiler_params=_params("arbitrary"),
    )(pb, bpad, colsum)


def _head_slices(hh):
    return slice(HEAD_DIM * hh, HEAD_DIM * (hh + 1))


def _scaled_q(q_ref, sl, scale=SCALE):
    return (q_ref[:, sl].astype(F32) * scale).astype(BF16)


def _neg_abs(x):
    sign = jnp.uint32(0x80000000)
    return lax.bitcast_convert_type(lax.bitcast_convert_type(x, jnp.uint32) | sign, F32)


def _sb_tile(qn, kj, carry, strict, u_after, diag):
    nz = _dot_nt(qn, kj)
    lf = jnp.minimum(nz, 0.0) - jnp.log(1.0 + jnp.exp(_neg_abs(nz)))
    lsig = lf - nz
    if diag:
        lf = jnp.where(strict, lf, 0.0)
    sx = _dot(lf.astype(BF16), u_after)
    a = jnp.exp(lsig + sx + carry)
    if diag:
        a = jnp.where(strict, a, 0.0)
    return lsig, a, carry + sx[:, 0:1] + lf[:, 0:1]


def _pair_grid_call(name, body, nb, in_specs, out_specs, out_shape, scratch, args, carried=None, groups=4):
    if carried is None:
        return pl.pallas_call(
            body, name=name, grid=(groups, nb), in_specs=in_specs, out_specs=out_specs, out_shape=out_shape,
            scratch_shapes=scratch, compiler_params=_params("arbitrary", "arbitrary"),
        )(*args)
    n_in, n_out, n_ex = len(in_specs), len(out_specs), carried.n

    def body_with_copies(*refs):
        own_in, ex_in = refs[:n_in], refs[n_in:n_in + n_ex]
        own_out = refs[n_in + n_ex:n_in + n_ex + n_out]
        ex_out = refs[n_in + n_ex + n_out:n_in + 2 * n_ex + n_out]
        own_scratch, sems = refs[n_in + 2 * n_ex + n_out:-2], refs[-2:]
        parts = (ex_in, ex_out, sems[0], sems[1])
        p, i = pl.program_id(0), pl.program_id(1)
        pl.when(jnp.logical_and(p == 0, i == 0))(lambda: carried.begin(*parts))
        if carried.relay is not None:
            pl.when(jnp.logical_and(p == groups - 1, i == max(nb - 2, 0)))(lambda: carried.relay(*parts))
        body(*own_in, *own_out, *own_scratch)
        pl.when(jnp.logical_and(p == groups - 1, i == nb - 1))(lambda: carried.finish(*parts))

    return pl.pallas_call(
        body_with_copies, name=name, grid=(groups, nb), in_specs=list(in_specs) + [HBM_SPEC] * n_ex,
        out_specs=list(out_specs) + [HBM_SPEC] * n_ex, out_shape=list(out_shape) + carried.out_shapes,
        scratch_shapes=list(scratch) + _dma_sems(carried.n_sems),
        compiler_params=_params("arbitrary", "arbitrary"),
    )(*args, *carried.inputs)


def _sb_fwd(name, pa, col0, carried=None):
    s = pa.shape[0]
    nb = s // TILE
    cb = col0 // SB_LANES
    kb = SB_WIDTH // SB_LANES

    def body(q_ref, k_ref, v_ref, o_ref, lsig_s, lf_s):
        i = pl.program_id(1)
        r = lax.broadcasted_iota(jnp.int32, (TILE, TILE), 0)
        c = lax.broadcasted_iota(jnp.int32, (TILE, TILE), 1)
        strict = c < r
        u_after = _tri(TILE, lambda rr, cc: rr > cc)
        qs = [_scaled_q(q_ref, _head_slices(hh), -SCALE) for hh in range(SB_GROUP)]

        def neg_z(j):
            kblk = k_ref[_rows(k_ref, j), :]
            return [_dot_nt(qs[hh], kblk[:, _head_slices(hh)]) for hh in range(SB_GROUP)]

        def scores(nzs, slot, diag):
            for hh, nz in enumerate(nzs):
                lf = jnp.minimum(nz, 0.0) - jnp.log(1.0 + jnp.exp(_neg_abs(nz)))
                lsig = lf - nz
                if diag:
                    lf = jnp.where(strict, lf, 0.0)
                    lsig = jnp.where(strict, lsig, MASKED)
                lsig_s[slot, hh] = lsig
                lf_s[slot, hh] = lf.astype(BF16)

        def weigh(j, slot, state):
            vblk = v_ref[_rows(v_ref, j), :]
            new = []
            for hh in range(SB_GROUP):
                carry, acc = state[hh]
                lfb = lf_s[slot, hh]
                sx = _dot(lfb, u_after)
                a = jnp.exp(lsig_s[slot, hh] + sx + carry)
                new.append((carry + sx[:, 0:1] + lfb[:, 0:1].astype(F32),
                            acc + _dot(a.astype(BF16), vblk[:, _head_slices(hh)])))
            return tuple(new)

        def step(t, state):
            state = weigh(i - t + 1, (t - 1) % 2, state)
            scores(neg_z(i - t), t % 2, False)
            return state

        zero = (jnp.zeros((TILE, 1), F32), jnp.zeros((TILE, HEAD_DIM), F32))
        scores(neg_z(i), 0, True)
        state = lax.fori_loop(1, i + 1, step, (zero,) * SB_GROUP)
        state = weigh(0, i % 2, state)
        o_ref[...] = jnp.concatenate([st[1] for st in state], axis=1)

    outs = _pair_grid_call(
        name, body, nb,
        in_specs=[pl.BlockSpec((TILE, SB_LANES), lambda p, i: (i, cb + p)),
                  pl.BlockSpec((s, SB_LANES), lambda p, i: (0, cb + kb + p)),
                  pl.BlockSpec((s, SB_LANES), lambda p, i: (0, cb + 2 * kb + p))],
        out_specs=[pl.BlockSpec((TILE, SB_LANES), lambda p, i: (i, p))],
        out_shape=[jax.ShapeDtypeStruct((s, SB_WIDTH), F32)],
        scratch=[pltpu.VMEM((2, SB_GROUP, TILE, TILE), F32), pltpu.VMEM((2, SB_GROUP, TILE, TILE), BF16)],
        args=(pa, pa, pa), carried=carried, groups=kb)
    return outs[0], outs[1:]


def _sb_bwd(name, pa, col0, dout, dcol0, carried=None):
    s = pa.shape[0]
    nb = s // TILE
    cb = col0 // SB_LANES
    kb = SB_WIDTH // SB_LANES
    db = dcol0 // SB_LANES

    def body(q_ref, k_ref, v_ref, do_ref, dq_ref, dk_ref, dv_ref, dk_acc, dv_acc, dpan, span, gsum, lsig_s, lf_s):
        i = pl.program_id(1)

        @pl.when(i == 0)
        def _():
            dk_acc[...] = jnp.zeros_like(dk_acc)
            dv_acc[...] = jnp.zeros_like(dv_acc)

        r = lax.broadcasted_iota(jnp.int32, (TILE, TILE), 0)
        c = lax.broadcasted_iota(jnp.int32, (TILE, TILE), 1)
        strict = c < r
        u_after = _tri(TILE, lambda rr, cc: rr > cc)
        u_before = _tri(TILE, lambda rr, cc: rr < cc)
        qs = [_scaled_q(q_ref, _head_slices(hh), -SCALE) for hh in range(SB_GROUP)]
        dos = [do_ref[:, _head_slices(hh)].astype(BF16) for hh in range(SB_GROUP)]

        def scores(j, slot, diag):
            kblk = k_ref[_rows(k_ref, j), :]
            for hh in range(SB_GROUP):
                nz = _dot_nt(qs[hh], kblk[:, _head_slices(hh)])
                lf = jnp.minimum(nz, 0.0) - jnp.log(1.0 + jnp.exp(_neg_abs(nz)))
                lsig = lf - nz
                if diag:
                    lf = jnp.where(strict, lf, 0.0)
                    lsig = jnp.where(strict, lsig, MASKED)
                lsig_s[slot, hh] = lsig
                lf_s[slot, hh] = lf.astype(BF16)

        def grads(j, slot, carries):
            vblk = v_ref[_rows(v_ref, j), :]
            new = []
            for hh in range(SB_GROUP):
                lfb = lf_s[slot, hh]
                lsig = lsig_s[slot, hh]
                sx = _dot(lfb, u_after)
                a = jnp.exp(lsig + sx + carries[hh])
                g = a * _dot_nt(dos[hh], vblk[:, _head_slices(hh)])
                sig = jnp.exp(lsig)
                inside = _dot(g.astype(BF16), u_before)
                dpan[hh, j] = sig * (inside + g) - g
                span[hh, j] = sig
                gsum[hh, j] = inside[:, TILE - 1:TILE] + g[:, TILE - 1:TILE]
                dv_acc[hh, _rows(None, j), :] += _dot_tn(a.astype(BF16), dos[hh])
                new.append(carries[hh] + sx[:, 0:1] + lfb[:, 0:1].astype(F32))
            return tuple(new)

        def step1(t, carries):
            carries = grads(i - t + 1, (t - 1) % 2, carries)
            scores(i - t, t % 2, False)
            return carries

        zero1 = jnp.zeros((TILE, 1), F32)
        scores(i, 0, True)
        carries = lax.fori_loop(1, i + 1, step1, (zero1,) * SB_GROUP)
        grads(0, i % 2, carries)

        def pass2(j, state):
            kblk = k_ref[_rows(k_ref, j), :]
            new = []
            for hh in range(SB_GROUP):
                before, ndq = state[hh]
                ndzb = (dpan[hh, j] + span[hh, j] * before).astype(BF16)
                dk_acc[hh, _rows(None, j), :] += _dot_tn(ndzb, qs[hh])
                new.append((before + gsum[hh, j], ndq + _dot(ndzb, kblk[:, _head_slices(hh)])))
            return tuple(new)

        zero2 = (zero1, jnp.zeros((TILE, HEAD_DIM), F32))
        state = lax.fori_loop(0, i + 1, pass2, (zero2,) * SB_GROUP)
        dq_ref[...] = jnp.concatenate([st[1] * -SCALE for st in state], axis=1).astype(BF16)

        @pl.when(i == nb - 1)
        def _():
            dk_ref[...] = jnp.concatenate([dk_acc[hh] for hh in range(SB_GROUP)], axis=1).astype(BF16)
            dv_ref[...] = jnp.concatenate([dv_acc[hh] for hh in range(SB_GROUP)], axis=1).astype(BF16)

    qspec = pl.BlockSpec((TILE, SB_LANES), lambda p, i: (i, p))
    kvspec = pl.BlockSpec((s, SB_LANES), lambda p, i: (0, p))
    out = jax.ShapeDtypeStruct((s, SB_WIDTH), BF16)
    outs = _pair_grid_call(
        name, body, nb,
        in_specs=[pl.BlockSpec((TILE, SB_LANES), lambda p, i: (i, cb + p)),
                  pl.BlockSpec((s, SB_LANES), lambda p, i: (0, cb + kb + p)),
                  pl.BlockSpec((s, SB_LANES), lambda p, i: (0, cb + 2 * kb + p)),
                  pl.BlockSpec((TILE, SB_LANES), lambda p, i: (i, db + p))],
        out_specs=[qspec, kvspec, kvspec], out_shape=[out, out, out],
        scratch=[pltpu.VMEM((SB_GROUP, s, HEAD_DIM), F32), pltpu.VMEM((SB_GROUP, s, HEAD_DIM), F32),
                 pltpu.VMEM((SB_GROUP, nb, TILE, TILE), F32), pltpu.VMEM((SB_GROUP, nb, TILE, TILE), F32),
                 pltpu.VMEM((SB_GROUP, nb, TILE, 1), F32),
                 pltpu.VMEM((2, SB_GROUP, TILE, TILE), F32), pltpu.VMEM((2, SB_GROUP, TILE, TILE), BF16)],
        args=(pa, pa, pa, dout), carried=carried, groups=kb)
    return outs[:3], outs[3:]


def _fox_scores(q, kj, cq, crj, causal, diag):
    sc = _dot_nt(q, kj) + (cq - crj)
    if diag:
        sc = jnp.where(causal, sc, NEG_INF)
    return sc


def _fox_fwd(name, pa, col0, ccol4, crow4, carried=None):
    s = pa.shape[0]
    nb = s // TILE
    cb = col0 // FOX_LANES
    kb = FOX_WIDTH // FOX_LANES

    def body(q_ref, k_ref, v_ref, cc_ref, cr_ref, o_ref, lse_ref, sc_s):
        i = pl.program_id(1)
        r = lax.broadcasted_iota(jnp.int32, (TILE, TILE), 0)
        c = lax.broadcasted_iota(jnp.int32, (TILE, TILE), 1)
        causal = c <= r
        qs = [_scaled_q(q_ref, _head_slices(hh)) for hh in range(FOX_GROUP)]
        cqs = [cc_ref[:, HEAD_DIM * hh:HEAD_DIM * hh + 1] for hh in range(FOX_GROUP)]

        def logits(j, slot, diag):
            kblk = k_ref[_rows(k_ref, j), :]
            tops = []
            for hh in range(FOX_GROUP):
                sc = _fox_scores(qs[hh], kblk[:, _head_slices(hh)], cqs[hh], cr_ref[j, hh:hh + 1, :], causal, diag)
                sc_s[slot, hh] = sc
                tops.append(jnp.max(sc, axis=1, keepdims=True))
            return tuple(tops)

        def update(j, slot, tops, state):
            vblk = v_ref[_rows(v_ref, j), :]
            new = []
            for hh in range(FOX_GROUP):
                m, l, acc = state[hh]
                m2 = jnp.maximum(m, tops[hh])
                alpha = jnp.exp(m - m2)
                p = jnp.exp(sc_s[slot, hh] - m2)
                new.append((m2, l * alpha + jnp.sum(p, axis=1, keepdims=True),
                            acc * alpha + _dot(p.astype(BF16), vblk[:, _head_slices(hh)])))
            return tuple(new)

        def step(t, both):
            tops, state = both
            state = update(i - t + 1, (t - 1) % 2, tops, state)
            return logits(i - t, t % 2, False), state

        zero = (jnp.full((TILE, 1), NEG_INF, F32), jnp.zeros((TILE, 1), F32), jnp.zeros((TILE, HEAD_DIM), F32))
        tops, state = lax.fori_loop(1, i + 1, step, (logits(i, 0, True), (zero,) * FOX_GROUP))
        state = update(0, i % 2, tops, state)
        o_ref[...] = jnp.concatenate([st[2] / st[1] for st in state], axis=1)
        lse_ref[...] = jnp.concatenate(
            [jnp.broadcast_to(st[0] + jnp.log(st[1]), (TILE, HEAD_DIM)) for st in state], axis=1)

    outs = _pair_grid_call(
        name, body, nb,
        in_specs=[pl.BlockSpec((TILE, FOX_LANES), lambda p, i: (i, cb + p)),
                  pl.BlockSpec((s, FOX_LANES), lambda p, i: (0, cb + kb + p)),
                  pl.BlockSpec((s, FOX_LANES), lambda p, i: (0, cb + 2 * kb + p)),
                  pl.BlockSpec((None, TILE, FOX_LANES), lambda p, i: (p, i, 0)),
                  pl.BlockSpec((None, nb, 8, TILE), lambda p, i: (p, 0, 0, 0))],
        out_specs=[pl.BlockSpec((TILE, FOX_LANES), lambda p, i: (i, p)),
                   pl.BlockSpec((None, TILE, FOX_LANES), lambda p, i: (p, i, 0))],
        out_shape=[jax.ShapeDtypeStruct((s, FOX_WIDTH), F32), jax.ShapeDtypeStruct((kb, s, FOX_LANES), F32)],
        scratch=[pltpu.VMEM((2, FOX_GROUP, TILE, TILE), F32)],
        args=(pa, pa, pa, ccol4, crow4), carried=carried, groups=kb)
    return outs[0], outs[1], outs[2:]


def _fox_bwd(name, pa, col0, ccol4, crow4, out, lse, dout, dcol0):
    s = pa.shape[0]
    nb = s // TILE
    cb = col0 // FOX_LANES
    kb = FOX_WIDTH // FOX_LANES
    db = dcol0 // FOX_LANES

    def body(q_ref, k_ref, v_ref, cc_ref, cr_ref, o_ref, lse_ref, do_ref,
             dq_ref, dk_ref, dv_ref, cs_ref, dk_acc, dv_acc, p_s, ds_s):
        i = pl.program_id(1)

        @pl.when(i == 0)
        def _():
            dk_acc[...] = jnp.zeros_like(dk_acc)
            dv_acc[...] = jnp.zeros_like(dv_acc)
            cs_ref[...] = jnp.zeros_like(cs_ref)

        r = lax.broadcasted_iota(jnp.int32, (TILE, TILE), 0)
        c = lax.broadcasted_iota(jnp.int32, (TILE, TILE), 1)
        causal = c <= r
        qs = [_scaled_q(q_ref, _head_slices(hh)) for hh in range(FOX_GROUP)]
        cqs = [cc_ref[:, HEAD_DIM * hh:HEAD_DIM * hh + 1] for hh in range(FOX_GROUP)]
        lses = [lse_ref[:, HEAD_DIM * hh:HEAD_DIM * hh + 1] for hh in range(FOX_GROUP)]
        dofs = [do_ref[:, _head_slices(hh)] for hh in range(FOX_GROUP)]
        dos = [d_.astype(BF16) for d_ in dofs]
        deltas = [jnp.sum(dofs[hh] * o_ref[:, _head_slices(hh)], axis=1, keepdims=True) for hh in range(FOX_GROUP)]

        def probs(j, slot, rowsums, diag):
            kblk = k_ref[_rows(k_ref, j), :]
            vblk = v_ref[_rows(v_ref, j), :]
            new = []
            for hh in range(FOX_GROUP):
                sl = _head_slices(hh)
                sc = _fox_scores(qs[hh], kblk[:, sl], cqs[hh], cr_ref[j, hh:hh + 1, :], causal, diag)
                p = jnp.exp(sc - lses[hh])
                ds = p * (_dot_nt(dos[hh], vblk[:, sl]) - deltas[hh])
                p_s[slot, hh] = p.astype(BF16)
                ds_s[slot, hh] = ds.astype(BF16)
                cs_ref[j, hh:hh + 1, :] += jnp.sum(ds, axis=0, keepdims=True)
                new.append(rowsums[hh] + jnp.sum(ds, axis=1, keepdims=True))
            return tuple(new)

        def accumulate(j, slot, dqs):
            kblk = k_ref[_rows(k_ref, j), :]
            new = []
            for hh in range(FOX_GROUP):
                dsb = ds_s[slot, hh]
                dv_acc[hh, _rows(None, j), :] += _dot_tn(p_s[slot, hh], dos[hh])
                dk_acc[hh, _rows(None, j), :] += _dot_tn(dsb, qs[hh])
                new.append(dqs[hh] + _dot(dsb, kblk[:, _head_slices(hh)]))
            return tuple(new)

        def step(t, both):
            rowsums, dqs = both
            dqs = accumulate(i - t + 1, (t - 1) % 2, dqs)
            return probs(i - t, t % 2, rowsums, False), dqs

        zero1 = jnp.zeros((TILE, 1), F32)
        zero64 = jnp.zeros((TILE, HEAD_DIM), F32)
        rowsums, dqs = lax.fori_loop(1, i + 1, step,
                                     (probs(i, 0, (zero1,) * FOX_GROUP, True), (zero64,) * FOX_GROUP))
        dqs = accumulate(0, i % 2, dqs)
        for hh in range(FOX_GROUP):
            cs_ref[i, hh:hh + 1, :] -= jnp.broadcast_to(rowsums[hh], (TILE, LANES)).T[0:1, :]
        dq_ref[...] = jnp.concatenate([dq * SCALE for dq in dqs], axis=1).astype(BF16)

        @pl.when(i == nb - 1)
        def _():
            dk_ref[...] = jnp.concatenate([dk_acc[hh] for hh in range(FOX_GROUP)], axis=1).astype(BF16)
            dv_ref[...] = jnp.concatenate([dv_acc[hh] for hh in range(FOX_GROUP)], axis=1).astype(BF16)

    qspec = pl.BlockSpec((TILE, FOX_LANES), lambda p, i: (i, p))
    kvspec = pl.BlockSpec((s, FOX_LANES), lambda p, i: (0, p))
    o3 = jax.ShapeDtypeStruct((s, FOX_WIDTH), BF16)
    return pl.pallas_call(
        body, name=name, grid=(kb, nb),
        in_specs=[pl.BlockSpec((TILE, FOX_LANES), lambda p, i: (i, cb + p)),
                  pl.BlockSpec((s, FOX_LANES), lambda p, i: (0, cb + kb + p)),
                  pl.BlockSpec((s, FOX_LANES), lambda p, i: (0, cb + 2 * kb + p)),
                  pl.BlockSpec((None, TILE, FOX_LANES), lambda p, i: (p, i, 0)),
                  pl.BlockSpec((None, nb, 8, TILE), lambda p, i: (p, 0, 0, 0)),
                  qspec,
                  pl.BlockSpec((None, TILE, FOX_LANES), lambda p, i: (p, i, 0)),
                  pl.BlockSpec((TILE, FOX_LANES), lambda p, i: (i, db + p))],
        out_specs=[qspec, kvspec, kvspec, pl.BlockSpec((None, nb, 8, TILE), lambda p, i: (p, 0, 0, 0))],
        out_shape=[o3, o3, o3, jax.ShapeDtypeStruct((kb, nb, 8, TILE), F32)],
        scratch_shapes=[pltpu.VMEM((FOX_GROUP, s, HEAD_DIM), F32), pltpu.VMEM((FOX_GROUP, s, HEAD_DIM), F32),
                        pltpu.VMEM((2, FOX_GROUP, TILE, TILE), BF16), pltpu.VMEM((2, FOX_GROUP, TILE, TILE), BF16)],
        compiler_params=_params("arbitrary", "arbitrary"),
    )(pa, pa, pa, ccol4, crow4, out, lse, dout)


def _mem_fwd(name, pa, mkv):
    s = pa.shape[0]
    ml = mkv.shape[0]
    nb = s // TILE
    cb = QKV_WIDTH // LANES

    def body(q_ref, k_ref, v_ref, o_ref, lse_ref):
        outs, lses = [], []
        for hh in range(2):
            sl = _head_slices(hh)
            sc = _dot_nt(_scaled_q(q_ref, sl), k_ref[:, sl])
            m = jnp.max(sc, axis=1, keepdims=True)
            p = jnp.exp(sc - m)
            l = jnp.sum(p, axis=1, keepdims=True)
            outs.append(_dot(p.astype(BF16), v_ref[:, sl]) / l)
            lses.append(jnp.broadcast_to(m + jnp.log(l), (TILE, HEAD_DIM)))
        o_ref[...] = jnp.concatenate(outs, axis=1)
        lse_ref[...] = jnp.concatenate(lses, axis=1)

    return pl.pallas_call(
        body, name=name, grid=(2, nb),
        in_specs=[pl.BlockSpec((TILE, LANES), lambda p, i: (i, cb + p)),
                  pl.BlockSpec((ml, LANES), lambda p, i: (0, p)),
                  pl.BlockSpec((ml, LANES), lambda p, i: (0, 2 + p))],
        out_specs=[pl.BlockSpec((TILE, LANES), lambda p, i: (i, p)),
                   pl.BlockSpec((None, TILE, LANES), lambda p, i: (p, i, 0))],
        out_shape=[jax.ShapeDtypeStruct((s, MEM_WIDTH), F32), jax.ShapeDtypeStruct((2, s, LANES), F32)],
        compiler_params=_params("parallel", "parallel"),
    )(pa, mkv, mkv)


def _mem_bwd(name, pa, mkv, out, lse, dout, dcol0):
    s = pa.shape[0]
    ml = mkv.shape[0]
    nb = s // TILE
    cb = QKV_WIDTH // LANES
    db = dcol0 // LANES

    def body(q_ref, k_ref, v_ref, o_ref, lse_ref, do_ref, dq_ref, dk_ref, dv_ref, dk_acc, dv_acc):
        i = pl.program_id(1)

        @pl.when(i == 0)
        def _():
            dk_acc[...] = jnp.zeros_like(dk_acc)
            dv_acc[...] = jnp.zeros_like(dv_acc)

        dqs = []
        for hh in range(2):
            sl = _head_slices(hh)
            q = _scaled_q(q_ref, sl)
            kh = k_ref[:, sl]
            dof = do_ref[:, sl]
            do = dof.astype(BF16)
            delta = jnp.sum(dof * o_ref[:, sl], axis=1, keepdims=True)
            p = jnp.exp(_dot_nt(q, kh) - lse_ref[:, HEAD_DIM * hh:HEAD_DIM * hh + 1])
            ds = (p * (_dot_nt(do, v_ref[:, sl]) - delta)).astype(BF16)
            dv_acc[hh] += _dot_tn(p.astype(BF16), do)
            dk_acc[hh] += _dot_tn(ds, q)
            dqs.append(_dot(ds, kh) * SCALE)
        dq_ref[...] = jnp.concatenate(dqs, axis=1).astype(BF16)

        @pl.when(i == nb - 1)
        def _():
            dk_ref[...] = jnp.concatenate([dk_acc[0], dk_acc[1]], axis=1).astype(BF16)
            dv_ref[...] = jnp.concatenate([dv_acc[0], dv_acc[1]], axis=1).astype(BF16)

    qspec = pl.BlockSpec((TILE, LANES), lambda p, i: (i, p))
    kvspec = pl.BlockSpec((ml, LANES), lambda p, i: (0, p))
    okv = jax.ShapeDtypeStruct((ml, MEM_WIDTH), BF16)
    return pl.pallas_call(
        body, name=name, grid=(2, nb),
        in_specs=[pl.BlockSpec((TILE, LANES), lambda p, i: (i, cb + p)),
                  pl.BlockSpec((ml, LANES), lambda p, i: (0, p)),
                  pl.BlockSpec((ml, LANES), lambda p, i: (0, 2 + p)),
                  qspec,
                  pl.BlockSpec((None, TILE, LANES), lambda p, i: (p, i, 0)),
                  pl.BlockSpec((TILE, LANES), lambda p, i: (i, db + p))],
        out_specs=[qspec, kvspec, kvspec],
        out_shape=[jax.ShapeDtypeStruct((s, MEM_WIDTH), BF16), okv, okv],
        scratch_shapes=[pltpu.VMEM((2, ml, HEAD_DIM), F32), pltpu.VMEM((2, ml, HEAD_DIM), F32)],
        compiler_params=_params("arbitrary", "arbitrary"),
    )(pa, mkv, mkv, out, lse, dout)


def _head_maps():
    col = jnp.arange(MIX_WIDTH)[:, None] // HEAD_DIM
    g = (col == jnp.arange(LANES)[None, :]).astype(BF16)
    return g, g.T


def _normed_heads(osb_ref, ofx_ref, om_ref, g_ref, gt_ref):
    y = jnp.concatenate([osb_ref[...], ofx_ref[...], om_ref[...]], axis=1)
    msq = _sum_l2(y * y, g_ref[...]) * (1.0 / HEAD_DIM)
    rf = _sum_l3(lax.rsqrt(msq + EPS), gt_ref[...])
    return y * rf, rf


def _out_fwd(name, o_sb, o_fx, o_m, pb, ow, x, w_out, ts):
    s, d = x.shape
    g, gt = _head_maps()

    def body(osb_ref, ofx_ref, om_ref, gate_ref, ow_ref, x_ref, w_ref, g_ref, gt_ref, xo_ref, y2_ref):
        yh, _ = _normed_heads(osb_ref, ofx_ref, om_ref, g_ref, gt_ref)
        gate = gate_ref[...]
        y2 = (yh * ow_ref[...] * (gate * jax.nn.sigmoid(gate))).astype(BF16)
        y2_ref[...] = y2
        xo_ref[...] = x_ref[...] + _dot(y2, w_ref[...])

    return pl.pallas_call(
        body, name=name, grid=(s // ts,),
        in_specs=[_row_spec(ts, SB_WIDTH), _row_spec(ts, FOX_WIDTH), _row_spec(ts, MEM_WIDTH),
                  _row_spec(ts, MIX_WIDTH), _const_spec((1, MIX_WIDTH)), _row_spec(ts, d),
                  _const_spec((MIX_WIDTH, d)),
                  _const_spec((MIX_WIDTH, LANES)), _const_spec((LANES, MIX_WIDTH))],
        out_specs=[_row_spec(ts, d), _row_spec(ts, MIX_WIDTH)],
        out_shape=[jax.ShapeDtypeStruct((s, d), F32), jax.ShapeDtypeStruct((s, MIX_WIDTH), BF16)],
        compiler_params=_params("parallel"),
    )(o_sb, o_fx, o_m, pb, ow, x, w_out, g, gt)


def _row_spec(ts, w):
    return pl.BlockSpec((ts, w), lambda i: (i, 0))


def _const_spec(shape):
    return pl.BlockSpec(shape, lambda i: (0,) * len(shape))


def _out_bwd(name, dxb, o_sb, o_fx, o_m, pb, ow, w_out, ts):
    s, d = dxb.shape
    g, gt = _head_maps()

    def body(dx_ref, osb_ref, ofx_ref, om_ref, gate_ref, ow_ref, w_ref, g_ref, gt_ref, dy_ref, dgate_ref, dow_ref):
        @pl.when(pl.program_id(0) == 0)
        def _():
            dow_ref[...] = jnp.zeros_like(dow_ref)

        dy2 = _dot_nt(dx_ref[...], w_ref[...])
        yh, rf = _normed_heads(osb_ref, ofx_ref, om_ref, g_ref, gt_ref)
        gate = gate_ref[...]
        sig = jax.nn.sigmoid(gate)
        ow_v = ow_ref[...]
        dgate_ref[...] = (dy2 * (yh * ow_v) * (sig * (1.0 + gate * (1.0 - sig)))).astype(BF16)
        dn = dy2 * (gate * sig)
        dow_ref[...] += jnp.sum(dn * yh, axis=0, keepdims=True)
        dyh = dn * ow_v
        t = _sum_l2(dyh * yh, g_ref[...]) * (1.0 / HEAD_DIM)
        dy_ref[...] = rf * (dyh - yh * _sum_l3(t, gt_ref[...]))

    return pl.pallas_call(
        body, name=name, grid=(s // ts,),
        in_specs=[_row_spec(ts, d), _row_spec(ts, SB_WIDTH), _row_spec(ts, FOX_WIDTH), _row_spec(ts, MEM_WIDTH),
                  _row_spec(ts, MIX_WIDTH), _const_spec((1, MIX_WIDTH)),
                  _const_spec((MIX_WIDTH, d)),
                  _const_spec((MIX_WIDTH, LANES)), _const_spec((LANES, MIX_WIDTH))],
        out_specs=[_row_spec(ts, MIX_WIDTH), _row_spec(ts, MIX_WIDTH), _const_spec((1, MIX_WIDTH))],
        out_shape=[jax.ShapeDtypeStruct((s, MIX_WIDTH), F32), jax.ShapeDtypeStruct((s, MIX_WIDTH), BF16),
                   jax.ShapeDtypeStruct((1, MIX_WIDTH), F32)],
        compiler_params=_params("arbitrary"),
    )(dxb, o_sb, o_fx, o_m, pb, ow, w_out, g, gt)


def _adamw(name, w, g, m, v, tr):
    def body(w_ref, g_ref, m_ref, v_ref, d_ref, m2_ref, v2_ref):
        gv = g_ref[...]
        m2 = ADAM_B1 * m_ref[...] + (1.0 - ADAM_B1) * gv
        v2 = ADAM_B2 * v_ref[...] + (1.0 - ADAM_B2) * (gv * gv)
        m_hat = m2 / (1.0 - ADAM_B1 ** ADAM_STEP)
        v_hat = v2 / (1.0 - ADAM_B2 ** ADAM_STEP)
        d_ref[...] = -ADAM_LR * (m_hat / (jnp.sqrt(v_hat) + ADAM_EPS) + ADAM_WD * w_ref[...])
        m2_ref[...] = m2
        v2_ref[...] = v2

    rest = w.shape[1:]
    spec = pl.BlockSpec((tr,) + rest, lambda i: (i,) + (0,) * len(rest))
    shp = jax.ShapeDtypeStruct(w.shape, F32)
    return pl.pallas_call(
        body, name=name, grid=(w.shape[0] // tr,), in_specs=[spec] * 4, out_specs=[spec] * 3, out_shape=[shp] * 3,
        compiler_params=_params("parallel"),
    )(w, g, m, v)


def _adamw_sharded(name, w, m, v, g_own, g_other, cvec, tr):
    depth, rows, cols = w.shape
    nt = rows // 2 // tr

    def body(c_ref, w_ref, m_ref, v_ref, *rest):
        g_refs, (g_ref, d_ref, m2_ref, v2_ref) = rest[:2 * depth], rest[2 * depth:]
        layer, mine = pl.program_id(0), pl.program_id(1) == c_ref[0]
        gv = None
        for lt in range(depth):
            cand = jnp.where(mine, g_refs[lt][...], g_refs[depth + lt][...])
            gv = cand if gv is None else jnp.where(layer == lt, cand, gv)
        m2 = ADAM_B1 * m_ref[...] + (1.0 - ADAM_B1) * gv
        v2 = ADAM_B2 * v_ref[...] + (1.0 - ADAM_B2) * (gv * gv)
        m_hat = m2 / (1.0 - ADAM_B1 ** ADAM_STEP)
        v_hat = v2 / (1.0 - ADAM_B2 ** ADAM_STEP)
        g_ref[...] = gv
        d_ref[...] = -ADAM_LR * (m_hat / (jnp.sqrt(v_hat) + ADAM_EPS) + ADAM_WD * w_ref[...])
        m2_ref[...] = m2
        v2_ref[...] = v2

    def g_map(lt, own):
        def index(l, hf, i, c_ref):
            use = jnp.logical_and(l == lt, (hf == c_ref[0]) == own)
            return jnp.where(use, i, 0), 0
        return index

    full = pl.BlockSpec((None, tr, cols), lambda l, hf, i, c_ref: (l, hf * nt + i, 0))
    g_specs = [pl.BlockSpec((tr, cols), g_map(lt, own)) for own in (True, False) for lt in range(depth)]
    shp = jax.ShapeDtypeStruct((depth, rows, cols), F32)
    return pl.pallas_call(
        body, name=name,
        grid_spec=pltpu.PrefetchScalarGridSpec(
            num_scalar_prefetch=1, grid=(depth, 2, nt), in_specs=[full] * 3 + g_specs, out_specs=[full] * 4),
        out_shape=[shp] * 4,
        compiler_params=_params("arbitrary", "arbitrary", "arbitrary"),
    )(cvec, w, m, v, *g_own, *g_other)


HBM_SPEC = pl.BlockSpec(memory_space=pltpu.HBM)


def _place():
    x, y, c = lax.axis_index("x"), lax.axis_index("y"), lax.axis_index("c")
    chips = [(1 - x, y), (x, 1 - y), (1 - x, 1 - y)]
    return x, y, c, chips


def _remote(src, dst, send_sems, recv_sems, k, to):
    return pltpu.make_async_remote_copy(src_ref=src, dst_ref=dst, send_sem=send_sems.at[k], recv_sem=recv_sems.at[k],
                                        device_id=to, device_id_type=MESH)


def _half_rows(n_rows, cc):
    rh = n_rows // 2
    return pl.ds(pl.multiple_of(cc * rh, 16), rh)


def _dma_sems(n):
    return [pltpu.SemaphoreType.DMA((n,)), pltpu.SemaphoreType.DMA((n,))]


class _Exchange:
    def __init__(self, inputs, out_shapes, n_sems, begin, relay, finish):
        self.inputs, self.out_shapes, self.n_sems = list(inputs), list(out_shapes), n_sems
        self.begin, self.relay, self.finish = begin, relay, finish

    @property
    def n(self):
        return len(self.inputs)

    def split(self, refs):
        return refs[:self.n], refs[self.n:2 * self.n], refs[2 * self.n], refs[2 * self.n + 1]


def _run_exchange(name, ex):
    def body(*refs):
        parts = ex.split(refs)
        for phase in (ex.begin, ex.relay, ex.finish):
            if phase is not None:
                phase(*parts)

    return pl.pallas_call(
        body, name=name, in_specs=[HBM_SPEC] * ex.n, out_specs=[HBM_SPEC] * ex.n, out_shape=ex.out_shapes,
        scratch_shapes=_dma_sems(ex.n_sems),
    )(*ex.inputs)


def _gather_exchange(shards):
    def ici(in_refs, out_refs, send_sems, recv_sems):
        x, y, c, chips = _place()
        return [_remote(in_ref.at[_half_rows(in_ref.shape[0], c)], out_ref.at[2 * x + y, _half_rows(in_ref.shape[0], c)],
                        send_sems, recv_sems, 6 * a + j, (cx, cy, c))
                for a, (in_ref, out_ref) in enumerate(zip(in_refs, out_refs)) for j, (cx, cy) in enumerate(chips)]

    def d2d(out_refs, send_sems, recv_sems, half_of):
        x, y, c, chips = _place()
        cps = []
        for a, out_ref in enumerate(out_refs):
            for j, (cx, cy) in enumerate(chips):
                piece = out_ref.at[2 * cx + cy, _half_rows(out_ref.shape[1], half_of(c))]
                cps.append(_remote(piece, piece, send_sems, recv_sems, 6 * a + 3 + j, (x, y, 1 - c)))
        return cps

    def begin(in_refs, out_refs, send_sems, recv_sems):
        for cp in ici(in_refs, out_refs, send_sems, recv_sems):
            cp.start()

    def relay(in_refs, out_refs, send_sems, recv_sems):
        x, y, c, chips = _place()
        for a, out_ref in enumerate(out_refs):
            for j, (cx, cy) in enumerate(chips):
                landed = out_ref.at[2 * cx + cy, _half_rows(out_ref.shape[1], c)]
                _remote(landed, landed, send_sems, recv_sems, 6 * a + j, (cx, cy, c)).wait_recv()
        for cp in d2d(out_refs, send_sems, recv_sems, lambda c_: c_):
            cp.start()

    def finish(in_refs, out_refs, send_sems, recv_sems):
        for cp in d2d(out_refs, send_sems, recv_sems, lambda c_: 1 - c_):
            cp.wait_recv()
        for cp in ici(in_refs, out_refs, send_sems, recv_sems) + d2d(out_refs, send_sems, recv_sems, lambda c_: c_):
            cp.wait_send()

    shapes = [jax.ShapeDtypeStruct((N_CHIPS,) + s_.shape, s_.dtype) for s_ in shards]
    return _Exchange(shards, shapes, 6 * len(shards), begin, relay, finish)


def _swap_halves(name, g4s):
    n = len(g4s)

    def body(*refs):
        in_refs, out_refs, (send_sems, recv_sems) = refs[:n], refs[n:2 * n], refs[2 * n:]
        x, y, c, _ = _place()
        cps = [_remote(in_ref.at[:, _half_rows(in_ref.shape[1], 1 - c), :], out_ref, send_sems, recv_sems, a, (x, y, 1 - c))
               for a, (in_ref, out_ref) in enumerate(zip(in_refs, out_refs))]
        for cp in cps:
            cp.start()
        for cp in cps:
            cp.wait()

    return pl.pallas_call(
        body, name=name, in_specs=[HBM_SPEC] * n, out_specs=[HBM_SPEC] * n,
        out_shape=[jax.ShapeDtypeStruct((g.shape[0], g.shape[1] // 2, g.shape[2]), g.dtype) for g in g4s],
        scratch_shapes=_dma_sems(n),
    )(*g4s)


def _add_half(name, g4, r1, cvec, tr):
    n, r, w = g4.shape
    rh = r // 2
    nblk = rh // tr

    def body(c_ref, a_ref, b_ref, o_ref):
        o_ref[...] = (a_ref[...] + b_ref[...]).astype(BF16)

    return pl.pallas_call(
        body, name=name,
        grid_spec=pltpu.PrefetchScalarGridSpec(
            num_scalar_prefetch=1, grid=(n, nblk),
            in_specs=[pl.BlockSpec((None, tr, w), lambda k, i, c_ref: (k, c_ref[0] * nblk + i, 0)),
                      pl.BlockSpec((None, tr, w), lambda k, i, c_ref: (k, i, 0))],
            out_specs=pl.BlockSpec((None, tr, w), lambda k, i, c_ref: (k, i, 0))),
        out_shape=jax.ShapeDtypeStruct((n, rh, w), BF16),
        compiler_params=_params("parallel", "parallel"),
    )(cvec, g4, r1)


def _scatter_exchange(h4s):
    def sends(in_refs, out_refs, send_sems, recv_sems):
        x, y, c, chips = _place()
        return [_remote(in_ref.at[2 * cx + cy], out_ref.at[j], send_sems, recv_sems, 3 * a + j, (cx, cy, c))
                for a, (in_ref, out_ref) in enumerate(zip(in_refs, out_refs)) for j, (cx, cy) in enumerate(chips)]

    def begin(*parts):
        for cp in sends(*parts):
            cp.start()

    def finish(in_refs, out_refs, send_sems, recv_sems):
        x, y, c, chips = _place()
        for a, out_ref in enumerate(out_refs):
            for j, (cx, cy) in enumerate(chips):
                got = out_ref.at[j]
                _remote(got, got, send_sems, recv_sems, 3 * a + j, (cx, cy, c)).wait_recv()
        for cp in sends(in_refs, out_refs, send_sems, recv_sems):
            cp.wait_send()

    shapes = [jax.ShapeDtypeStruct((3,) + h.shape[1:], h.dtype) for h in h4s]
    return _Exchange(h4s, shapes, 3 * len(h4s), begin, None, finish)


def _sum_chips(name, h4, r3, mvec, tr):
    _, rh, w = h4.shape

    def body(m_ref, a_ref, b_ref, c_ref, d_ref, o_ref):
        o_ref[...] = ((a_ref[...].astype(F32) + b_ref[...].astype(F32)) + c_ref[...].astype(F32)) + d_ref[...].astype(F32)

    specs = [pl.BlockSpec((None, tr, w), lambda i, m_ref: (m_ref[0], i, 0))]
    specs += [pl.BlockSpec((None, tr, w), functools.partial(lambda k, i, m_ref: (k, i, 0), k)) for k in range(3)]
    return pl.pallas_call(
        body, name=name,
        grid_spec=pltpu.PrefetchScalarGridSpec(
            num_scalar_prefetch=1, grid=(rh // tr,), in_specs=specs,
            out_specs=pl.BlockSpec((tr, w), lambda i, m_ref: (i, 0))),
        out_shape=jax.ShapeDtypeStruct((rh, w), F32),
        compiler_params=_params("parallel"),
    )(mvec, h4, r3, r3, r3)


def _swap_reduced(name, ghs):
    n = len(ghs)

    def body(*refs):
        in_refs, out_refs, (send_sems, recv_sems) = refs[:n], refs[n:2 * n], refs[2 * n:]
        x, y, c, _ = _place()
        cps = [_remote(in_ref, out_ref, send_sems, recv_sems, a, (x, y, 1 - c))
               for a, (in_ref, out_ref) in enumerate(zip(in_refs, out_refs))]
        for cp in cps:
            cp.start()
        for cp in cps:
            cp.wait()

    return pl.pallas_call(
        body, name=name, in_specs=[HBM_SPEC] * n, out_specs=[HBM_SPEC] * n,
        out_shape=[jax.ShapeDtypeStruct(g.shape, g.dtype) for g in ghs],
        scratch_shapes=_dma_sems(n),
    )(*ghs)


def _small_update(name, partials, weights, moments1, moments2):
    n = len(partials)
    width = max(p.shape[1] for p in partials)
    starts, at = [], 0
    for p in partials:
        starts.append(at)
        at += p.shape[0]
    rows = -(-at // 8) * 8
    has_w = [w is not None for w in weights]
    n_w = sum(has_w)

    def body(*refs):
        p_refs = refs[:n]
        w_refs, m_refs, v_refs = refs[n:n + n_w], refs[n + n_w:n + 2 * n_w], refs[n + 2 * n_w:n + 3 * n_w]
        outs = refs[n + 3 * n_w:-4]
        g_refs, upd_refs = outs[:n], outs[n:]
        vec, buf, send_sems, recv_sems = refs[-4:]
        x, y, c, _ = _place()
        me = 4 * x + 2 * y + c
        vec[...] = jnp.zeros_like(vec)
        for p_ref, r0 in zip(p_refs, starts):
            vec[r0:r0 + p_ref.shape[0], 0:p_ref.shape[1]] = p_ref[...]
        buf[me] = vec[...]
        flips = [(fx, fy, fc) for fx in (0, 1) for fy in (0, 1) for fc in (0, 1)][1:]
        peers = [(x + fx - 2 * x * fx, y + fy - 2 * y * fy, c + fc - 2 * c * fc) for fx, fy, fc in flips]
        sends = [_remote(vec, buf.at[me], send_sems, recv_sems, k, peer) for k, peer in enumerate(peers)]
        for cp in sends:
            cp.start()
        for k, (px, py, pc) in enumerate(peers):
            got = buf.at[4 * px + 2 * py + pc]
            _remote(got, got, send_sems, recv_sems, k, (px, py, pc)).wait_recv()
        for cp in sends:
            cp.wait_send()
        total = buf[0]
        for dev in range(1, N_DEV):
            total = total + buf[dev]
        k = 0
        for a in range(n):
            r, w = g_refs[a].shape
            g = total[starts[a]:starts[a] + r, 0:w]
            g_refs[a][...] = g
            if has_w[a]:
                m2 = ADAM_B1 * m_refs[k][...] + (1.0 - ADAM_B1) * g
                v2 = ADAM_B2 * v_refs[k][...] + (1.0 - ADAM_B2) * (g * g)
                m_hat = m2 / (1.0 - ADAM_B1 ** ADAM_STEP)
                v_hat = v2 / (1.0 - ADAM_B2 ** ADAM_STEP)
                upd_refs[3 * k][...] = -ADAM_LR * (m_hat / (jnp.sqrt(v_hat) + ADAM_EPS) + ADAM_WD * w_refs[k][...])
                upd_refs[3 * k + 1][...] = m2
                upd_refs[3 * k + 2][...] = v2
                k += 1

    ws = [w for w in weights if w is not None]
    g_shapes = [jax.ShapeDtypeStruct(p.shape if w is None else w.shape, F32) for p, w in zip(partials, weights)]
    u_shapes = [jax.ShapeDtypeStruct(w.shape, F32) for w in ws for _ in range(3)]
    vm = pl.BlockSpec(memory_space=pltpu.VMEM)
    n_args = n + 3 * n_w
    outs = pl.pallas_call(
        body, name=name, in_specs=[vm] * n_args, out_specs=[vm] * (n + 3 * n_w), out_shape=g_shapes + u_shapes,
        scratch_shapes=[pltpu.VMEM((rows, width), F32), pltpu.VMEM((N_DEV, rows, width), F32),
                        pltpu.SemaphoreType.DMA((7,)), pltpu.SemaphoreType.DMA((7,))],
    )(*partials, *ws, *[m for m in moments1 if m is not None], *[v for v in moments2 if v is not None])
    return outs[:n], outs[n:]


GATE_COL = 3 * SB_WIDTH + 3 * FOX_WIDTH + FOX_HEADS + MEM_WIDTH
FL_COL = QKV_WIDTH


GROUP_A_COLS = [(0, QKV_WIDTH), (FL_COL + FOX_HEADS, MEM_WIDTH)]
GROUP_B_COLS = [(GATE_COL, MIX_WIDTH), (FL_COL, FOX_HEADS)]


def _group_from_shards(shard_of, cw, spans, pad):
    parts = []
    for lo, width in spans:
        hi = lo + width
        for j in range(N_CHIPS):
            a, b = max(lo, j * cw), min(hi, (j + 1) * cw)
            if a < b:
                parts.append(shard_of(j)[:, a - j * cw:b - j * cw])
    if pad:
        parts.append(jnp.zeros((parts[0].shape[0], pad), parts[0].dtype))
    return jnp.concatenate(parts, axis=1)


def _shard_from_groups(ga, gb, j, cw):
    lo, hi = j * cw, (j + 1) * cw
    placed = []
    for grp, spans in ((ga, GROUP_A_COLS), (gb, GROUP_B_COLS)):
        at = 0
        for first, width in spans:
            a, b = max(lo, first), min(hi, first + width)
            if a < b:
                placed.append((a, grp[:, at + a - first:at + b - first]))
            at += width
    return jnp.concatenate([p for _, p in sorted(placed, key=lambda t: t[0])], axis=1)


def _tile_of(n, cap, unit):
    if n <= cap:
        return n
    best = None
    for t in range(unit, cap + 1, unit):
        if n % t == 0:
            best = t
    assert best is not None, (n, cap, unit)
    return best


def _column_major_rows(a):
    dp, r, c = a.shape
    return a.transpose(2, 0, 1).reshape(c, dp, r // LANES, LANES).transpose(0, 2, 1, 3).reshape(-1, 8, LANES)


def _from_column_major_rows(b, shape):
    dp, r, c = shape
    return b.reshape(c, r // LANES, dp, LANES).transpose(0, 2, 1, 3).reshape(c, dp, r).transpose(1, 2, 0)


def _pack_small(parts):
    rows = []
    for p in parts:
        f = p.reshape(-1).astype(F32)
        f = jnp.pad(f, (0, (-f.shape[0]) % LANES))
        rows.append(f.reshape(-1, LANES))
    out = jnp.concatenate(rows, axis=0)
    return jnp.pad(out, ((0, (-out.shape[0]) % 8), (0, 0)))


def _unpack_small(packed, shapes):
    outs, r = [], 0
    for shp in shapes:
        n = 1
        for s_ in shp:
            n *= s_
        nr = -(-n // LANES)
        outs.append(packed[r:r + nr].reshape(-1)[:n].reshape(shp))
        r += nr
    return outs


def kernel(x, mem, norm_w, w_in, b_forget, mem_norm_w, w_mem_kv, out_norm_w, w_out, final_norm_w, loss_target, m_norm_w, m_w_in, m_b_forget, m_mem_norm_w, m_w_mem_kv, m_out_norm_w, m_w_out, m_final_norm_w, v_norm_w, v_w_in, v_b_forget, v_mem_norm_w, v_w_mem_kv, v_out_norm_w, v_w_out, v_final_norm_w):
    xs = x[0]
    mems = mem[0]
    target = loss_target[0]
    s, d = xs.shape
    depth = norm_w.shape[0]
    nb = s // TILE
    ts = _tile_of(s, 256, 8)
    big = (w_in, w_mem_kv, w_out)
    core = lax.axis_index("c")
    chip = 2 * lax.axis_index("x") + lax.axis_index("y")
    cvec = core.astype(jnp.int32).reshape(1)
    mvec = chip.astype(jnp.int32).reshape(1)
    cw = w_in.shape[2]

    own_w = [[a[l].astype(BF16) for a in big] for l in range(depth)]

    def lay_out_in(own, got):
        shard_of = lambda j: jnp.where(chip == j, own, got[j])
        return (_group_from_shards(shard_of, cw, GROUP_A_COLS, 0),
                _group_from_shards(shard_of, cw, GROUP_B_COLS, LANES - FOX_HEADS))

    def lay_out_rows(own, got):
        full = jnp.where(lax.broadcasted_iota(jnp.int32, got.shape, 0) == chip, own[None], got)
        return full.reshape(-1, full.shape[2])

    w_in_groups = [lay_out_in(own_w[0][0], _run_exchange("gather_weights0", _gather_exchange(own_w[0][:1]))[0])]
    layer_w = []

    tm = _tile_of(s, 256, 8)
    fl_block = MIX_WIDTH // LANES

    saved = []
    cur = xs
    for l in range(depth):
        wa, wb = w_in_groups[l]
        h = _rms_fwd(f"rms_fwd{l}", cur, norm_w[l][None], ts)
        pa = _mm(f"inproj_a{l}", h, wa, "nn", tm, _tile_of(PA, 1664, LANES), BF16)
        pb = _mm(f"inproj_b{l}", h, wb, "nn", tm, PB, F32)
        bpad = jnp.pad(b_forget[l], (0, LANES - FOX_HEADS))[None]
        ccol, crow = _gate_fwd(f"gate_fwd{l}", pb, bpad, fl_block)
        fg = FOX_HEADS // FOX_GROUP
        ccol4 = jnp.repeat(ccol[:, :FOX_HEADS].reshape(s, fg, FOX_GROUP).transpose(1, 0, 2), HEAD_DIM, axis=2)
        crow4 = jnp.pad(crow.reshape(nb, fg, FOX_GROUP, TILE).transpose(1, 0, 2, 3),
                        ((0, 0), (0, 0), (0, 8 - FOX_GROUP), (0, 0)))
        more = l + 1 < depth
        riding = own_w[l][1:] + (own_w[l + 1][:1] if more else [])
        o_sb, got = _sb_fwd(f"sb_fwd{l}", pa, 0, carried=_gather_exchange(riding))
        wkv, wout = lay_out_rows(own_w[l][1], got[0]), lay_out_rows(own_w[l][2], got[1])
        layer_w.append((wa, wb, wkv, wout))
        if more:
            w_in_groups.append(lay_out_in(own_w[l + 1][0], got[2]))
        o_fx, lse_fx, _ = _fox_fwd(f"fox_fwd{l}", pa, 3 * SB_WIDTH, ccol4, crow4)
        mn = _rms_fwd(f"mem_rms{l}", mems, mem_norm_w[l][None], mems.shape[0])
        mkv = _mm(f"mem_kv{l}", mn, wkv, "nn", mems.shape[0], 2 * MEM_WIDTH, BF16)
        o_m, lse_m = _mem_fwd(f"mem_fwd{l}", pa, mkv)
        nxt, y2 = _out_fwd(f"out_fwd{l}", o_sb, o_fx, o_m, pb, out_norm_w[l][None], cur, wout, ts)
        saved.append((cur, h, pa, pb, bpad, ccol4, crow4, o_sb, o_fx, lse_fx, mn, mkv, o_m, lse_m, y2))
        cur = nxt

    loss_v, dx, dxb, g_final = _final_loss("final_loss", cur, final_norm_w[None], target, ts)

    g_norm, g_b, g_memnorm, g_outnorm = [None] * depth, [None] * depth, [None] * depth, [None] * depth
    g_wa, g_wb, g_wkv, g_wout = [None] * depth, [None] * depth, [None] * depth, [None] * depth
    g_own = [[None] * depth for _ in big]
    g_other = [[None] * depth for _ in big]

    def within_chip(tag, jobs):
        got = _swap_halves(f"grad_swap_halves{tag}", [g for _, _, g, _ in jobs])
        return [(lr, k, _add_half(f"grad_add_half{lr}_{k}", g, r_, cvec, t_), t_) for (lr, k, g, t_), r_ in zip(jobs, got)]

    def reduce_at_owner(tag, jobs, from_chips):
        halves = [_sum_chips(f"grad_sum_chips{lr}_{k}", h_, r_, mvec, t_) for (lr, k, h_, t_), r_ in zip(jobs, from_chips)]
        others = _swap_reduced(f"grad_swap_reduced{tag}", halves)
        for (lr, k, _, _), mine, other in zip(jobs, halves, others):
            g_own[k][lr], g_other[k][lr] = mine, other

    def job(lr, k, g4):
        return lr, k, g4, _tile_of(g4.shape[1] // 2, 256, 16)

    pending = []
    for l in reversed(range(depth)):
        xin, h, pa, pb, bpad, ccol4, crow4, o_sb, o_fx, lse_fx, mn, mkv, o_m, lse_m, y2 = saved[l]
        wa, wb, wkv, wout = layer_w[l]
        dy, dgate, g_outnorm[l] = _out_bwd(f"out_bwd{l}", dxb, o_sb, o_fx, o_m, pb, out_norm_w[l][None], wout, ts)
        g_wout[l] = _mm(f"dw_out{l}", y2, dxb, "tn", _tile_of(MIX_WIDTH, 640, LANES), d, F32)
        dq_m, dk_m, dv_m = _mem_bwd(f"mem_bwd{l}", pa, mkv, o_m, lse_m, dy, SB_WIDTH + FOX_WIDTH)
        dmkv = jnp.concatenate([dk_m, dv_m], axis=1)
        g_wkv[l] = _mm(f"dw_kv{l}", mn, dmkv, "tn", d, 2 * MEM_WIDTH, F32)
        dmn = _mm(f"dmem{l}", dmkv, wkv, "nt", mems.shape[0], d, F32)
        g_memnorm[l] = _rms_wgrad(f"mem_norm_grad{l}", mems, dmn)
        pending += within_chip(f"{l}s", [job(l, 1, g_wkv[l].reshape(N_CHIPS, -1, g_wkv[l].shape[1])),
                                         job(l, 2, g_wout[l].reshape(N_CHIPS, -1, d))])
        (dq_sb, dk_sb, dv_sb), from_chips = _sb_bwd(f"sb_bwd{l}", pa, 0, dy, 0,
                                                   carried=_scatter_exchange([j[2] for j in pending]))
        reduce_at_owner(f"{l}s", pending, from_chips)
        dq_fx, dk_fx, dv_fx, cs4 = _fox_bwd(f"fox_bwd{l}", pa, 3 * SB_WIDTH, ccol4, crow4, o_fx, lse_fx, dy, SB_WIDTH)
        colsum = cs4[:, :, :FOX_GROUP, :].transpose(1, 0, 2, 3).reshape(nb, 8, TILE)
        dlogit, g_b[l] = _gate_bwd(f"gate_bwd{l}", pb, bpad, colsum, fl_block)
        dpa = jnp.concatenate([dq_sb, dk_sb, dv_sb, dq_fx, dk_fx, dv_fx, dq_m], axis=1)
        dpb = jnp.concatenate([dgate, dlogit], axis=1)
        tw = _tile_of(d, 512, LANES)
        g_wa[l] = _mm(f"dw_in_a{l}", h, dpa, "tn", tw, _tile_of(PA, 1664, LANES), F32)
        g_wb[l] = _mm(f"dw_in_b{l}", h, dpb, "tn", tw, PB, F32)
        g4_in = jnp.stack([_shard_from_groups(g_wa[l], g_wb[l], j, cw) for j in range(N_CHIPS)])
        pending = within_chip(f"{l}", [job(l, 0, g4_in)])
        last = _scatter_exchange([j[2] for j in pending]) if l == 0 else None
        dx, dxb, g_norm[l], from_chips = _inproj_bwd(f"inproj_bwd{l}", dpa, dpb, wa, wb, xin, norm_w[l][None], dx, ts,
                                                      carried=last)
        if last is not None:
            reduce_at_owner("last", pending, from_chips)

    small_w = [norm_w, b_forget, mem_norm_w, out_norm_w, final_norm_w]
    small_m = [m_norm_w, m_b_forget, m_mem_norm_w, m_out_norm_w, m_final_norm_w]
    small_v = [v_norm_w, v_b_forget, v_mem_norm_w, v_out_norm_w, v_final_norm_w]
    rows2 = lambda a: a.reshape(-1, a.shape[-1])
    partials = [jnp.concatenate(g_norm, axis=0), jnp.concatenate(g_b, axis=0), jnp.concatenate(g_memnorm, axis=0),
                jnp.concatenate(g_outnorm, axis=0), g_final, loss_v]
    sums, updates = _small_update("small_update", partials, [rows2(a) for a in small_w] + [None],
                                  [rows2(a) for a in small_m] + [None], [rows2(a) for a in small_v] + [None])
    small_grads = [g.reshape(a.shape) for g, a in zip(sums, small_w)]
    loss = sums[-1][0, 0]
    small_delta, small_m2, small_v2 = ([updates[3 * k + t].reshape(a.shape) for k, a in enumerate(small_w)]
                                       for t in range(3))
    big_grads, big_delta, big_m2, big_v2 = [], [], [], []
    for k, (nm, w_, m_, v_) in enumerate(zip(("w_in", "w_mem_kv", "w_out"), big, (m_w_in, m_w_mem_kv, m_w_out),
                                             (v_w_in, v_w_mem_kv, v_w_out))):
        if w_.shape[2] % LANES:
            g_full = jnp.stack([jnp.concatenate([jnp.where(core == 0, go, gt), jnp.where(core == 0, gt, go)], axis=0)
                                for go, gt in zip(g_own[k], g_other[k])])
            w_p, g_p, m_p, v_p = (_column_major_rows(a) for a in (w_, g_full, m_, v_))
            outs = _adamw(f"adamw_{nm}", w_p, g_p, m_p, v_p, _tile_of(w_p.shape[0], 600, 1))
            outs = [_from_column_major_rows(o, w_.shape) for o in (g_p, *outs)]
        else:
            outs = _adamw_sharded(f"adamw_{nm}", w_, m_, v_, g_own[k], g_other[k], cvec,
                                  _tile_of(w_.shape[1] // 2, 256, 8))
        for lst, o in zip((big_grads, big_delta, big_m2, big_v2), outs):
            lst.append(o)

    def order(sm, bg):
        return [sm[0], bg[0], sm[1], sm[2], bg[1], sm[3], bg[2], sm[4]]

    return (loss, dx[None], *order(small_grads, big_grads), *order(small_delta, big_delta),
            *order(small_m2, big_m2), *order(small_v2, big_v2))
```

```python
import functools

import jax
import jax.numpy as jnp
from jax import lax
from jax.experimental import pallas as pl
from jax.experimental.pallas import tpu as pltpu

F32 = jnp.float32
BF16 = jnp.bfloat16

HEAD_DIM = 64
SB_WIDTH = 512
FOX_WIDTH = 512
FOX_HEADS = 8
MEM_WIDTH = 256
MIX_WIDTH = SB_WIDTH + FOX_WIDTH + MEM_WIDTH
TOTAL_HEADS = MIX_WIDTH // HEAD_DIM
IN_WIDTH = 3 * SB_WIDTH + 3 * FOX_WIDTH + FOX_HEADS + MEM_WIDTH + MIX_WIDTH
LANES = 128
QKV_WIDTH = 3 * SB_WIDTH + 3 * FOX_WIDTH
PA = QKV_WIDTH + MEM_WIDTH
PB = LANES + MIX_WIDTH
EPS = 1e-6
SCALE = HEAD_DIM ** -0.5
TILE = 256
SB_GROUP = 4
SB_LANES = SB_GROUP * HEAD_DIM
FOX_GROUP = 4
FOX_LANES = FOX_GROUP * HEAD_DIM
NEG_INF = float("-inf")
MASKED = -1e30

ADAM_LR = 0.001
ADAM_B1 = 0.9
ADAM_B2 = 0.999
ADAM_EPS = 1e-08
ADAM_WD = 0.01
ADAM_STEP = 10

N_CHIPS = 4
N_DEV = 8
VMEM_LIMIT = 48 * 1024 * 1024
MESH = pl.DeviceIdType.MESH


def _params(*sem):
    return pltpu.CompilerParams(dimension_semantics=tuple(sem), vmem_limit_bytes=VMEM_LIMIT)


def _dot(a, b):
    return jnp.dot(a, b, preferred_element_type=F32)


def _dot_nt(a, b):
    return lax.dot_general(a, b, (((1,), (1,)), ((), ())), preferred_element_type=F32)


def _dot_tn(a, b):
    return lax.dot_general(a, b, (((0,), (0,)), ((), ())), preferred_element_type=F32)


def _split2(x):
    hi = x.astype(BF16)
    lo = (x - hi.astype(F32)).astype(BF16)
    return hi, lo


def _split3(x):
    hi = x.astype(BF16)
    r = x - hi.astype(F32)
    mid = r.astype(BF16)
    lo = (r - mid.astype(F32)).astype(BF16)
    return hi, mid, lo


def _sum_l2(x, u):
    hi, lo = _split2(x)
    return _dot(hi, u) + _dot(lo, u)


def _sum_l3(x, u):
    hi, mid, lo = _split3(x)
    return _dot(hi, u) + _dot(mid, u) + _dot(lo, u)


def _sum_r3(u, x):
    hi, mid, lo = _split3(x)
    return _dot(u, hi) + _dot(u, mid) + _dot(u, lo)


def _softplus(z):
    return jnp.maximum(z, 0.0) + jnp.log1p(jnp.exp(-jnp.abs(z)))


def _tri(n, pred):
    r = lax.broadcasted_iota(jnp.int32, (n, n), 0)
    c = lax.broadcasted_iota(jnp.int32, (n, n), 1)
    return jnp.where(pred(r, c), 1.0, 0.0).astype(BF16)


def _rows(ref, j, n=TILE):
    return pl.ds(pl.multiple_of(j * n, n), n)


def _mm(name, a, b, mode, tm, tn, out_dtype, res=None, a_lead=(), b_lead=()):
    a2, b2 = a.shape[len(a_lead):], b.shape[len(b_lead):]
    if mode == "tn":
        k, m = a2
    else:
        m, k = a2
    n = b2[0] if mode == "nt" else b2[1]
    assert m % tm == 0 and n % tn == 0, (name, m, tm, n, tn)
    na, nb = (None,) * len(a_lead), (None,) * len(b_lead)
    if mode == "tn":
        a_spec = pl.BlockSpec(na + (k, tm), lambda j, i: a_lead + (0, i))
    else:
        a_spec = pl.BlockSpec(na + (tm, k), lambda j, i: a_lead + (i, 0))
    if mode == "nt":
        b_spec = pl.BlockSpec(nb + (tn, k), lambda j, i: b_lead + (j, 0))
    else:
        b_spec = pl.BlockSpec(nb + (k, tn), lambda j, i: b_lead + (0, j))
    o_spec = pl.BlockSpec((tm, tn), lambda j, i: (i, j))
    dot = {"nn": _dot, "nt": _dot_nt, "tn": _dot_tn}[mode]

    def body(a_ref, b_ref, *rest):
        o_ref = rest[-1]
        acc = dot(a_ref[...].astype(BF16), b_ref[...].astype(BF16))
        if res is not None:
            acc = acc + rest[0][...]
        o_ref[...] = acc.astype(o_ref.dtype)

    args, specs = [a, b], [a_spec, b_spec]
    if res is not None:
        args.append(res)
        specs.append(o_spec)
    return pl.pallas_call(
        body, name=name, grid=(n // tn, m // tm), in_specs=specs, out_specs=o_spec,
        out_shape=jax.ShapeDtypeStruct((m, n), out_dtype),
        compiler_params=_params("parallel", "parallel"),
    )(*args)


def _rms_fwd(name, x, g, ts):
    s, d = x.shape

    def body(x_ref, g_ref, o_ref):
        xf = x_ref[...]
        r = lax.rsqrt(jnp.mean(xf * xf, axis=1, keepdims=True) + EPS)
        o_ref[...] = (xf * r * g_ref[...]).astype(BF16)

    return pl.pallas_call(
        body, name=name, grid=(s // ts,),
        in_specs=[pl.BlockSpec((ts, d), lambda i: (i, 0)), pl.BlockSpec((1, d), lambda i: (0, 0))],
        out_specs=pl.BlockSpec((ts, d), lambda i: (i, 0)),
        out_shape=jax.ShapeDtypeStruct((s, d), BF16),
        compiler_params=_params("parallel"),
    )(x, g)


def _inproj_bwd(name, dpa, dpb, wa, wb, x, g, dres, ts, carried=None):
    s, d = x.shape

    def body(dpa_ref, dpb_ref, wa_ref, wb_ref, x_ref, g_ref, dres_ref, dx_ref, dxb_ref, dg_ref):
        @pl.when(pl.program_id(1) == 0)
        def _():
            dg_ref[...] = jnp.zeros_like(dg_ref)

        dhf = _dot_nt(dpa_ref[...], wa_ref[...]) + _dot_nt(dpb_ref[...], wb_ref[...])
        xf = x_ref[...]
        r = lax.rsqrt(jnp.mean(xf * xf, axis=1, keepdims=True) + EPS)
        xh = xf * r
        dg_ref[...] += jnp.sum(dhf * xh, axis=0, keepdims=True)
        dxh = dhf * g_ref[...]
        m = jnp.mean(dxh * xh, axis=1, keepdims=True)
        dx = r * (dxh - xh * m) + dres_ref[...]
        dx_ref[...] = dx
        dxb_ref[...] = dx.astype(BF16)

    row = lambda w: pl.BlockSpec((ts, w), lambda p, i: (i, 0))
    whole = lambda a: pl.BlockSpec(a.shape, lambda p, i: (0, 0))
    outs = _pair_grid_call(
        name, body, s // ts,
        in_specs=[row(dpa.shape[1]), row(dpb.shape[1]), whole(wa), whole(wb), row(d), whole(g), row(d)],
        out_specs=[row(d), row(d), pl.BlockSpec((1, d), lambda p, i: (0, 0))],
        out_shape=[jax.ShapeDtypeStruct((s, d), F32), jax.ShapeDtypeStruct((s, d), BF16),
                   jax.ShapeDtypeStruct((1, d), F32)],
        scratch=[], args=(dpa, dpb, wa, wb, x, g, dres), carried=carried, groups=1)
    return outs[0], outs[1], outs[2], outs[3:]


def _rms_wgrad(name, x, dh):
    m_, d = x.shape

    def body(x_ref, dh_ref, dg_ref):
        xf = x_ref[...]
        r = lax.rsqrt(jnp.mean(xf * xf, axis=1, keepdims=True) + EPS)
        dg_ref[...] = jnp.sum(dh_ref[...] * xf * r, axis=0, keepdims=True)

    return pl.pallas_call(
        body, name=name, out_shape=jax.ShapeDtypeStruct((1, d), F32),
    )(x, dh)


def _final_loss(name, x, g, target, ts):
    s, d = x.shape

    def body(x_ref, g_ref, t_ref, loss_ref, dx_ref, dxb_ref, dg_ref):
        @pl.when(pl.program_id(0) == 0)
        def _():
            dg_ref[...] = jnp.zeros_like(dg_ref)
            loss_ref[...] = jnp.zeros_like(loss_ref)

        xf = x_ref[...]
        gw = g_ref[...]
        r = lax.rsqrt(jnp.mean(xf * xf, axis=1, keepdims=True) + EPS)
        xh = xf * r
        e = xh * gw - t_ref[...]
        part = 0.5 * jnp.sum(jnp.mean(e * e, axis=1, keepdims=True), axis=0, keepdims=True)
        loss_ref[...] += jnp.broadcast_to(part, loss_ref.shape)
        dy = e * (1.0 / d)
        dg_ref[...] += jnp.sum(dy * xh, axis=0, keepdims=True)
        dxh = dy * gw
        m = jnp.mean(dxh * xh, axis=1, keepdims=True)
        dx = r * (dxh - xh * m)
        dx_ref[...] = dx
        dxb_ref[...] = dx.astype(BF16)

    row = pl.BlockSpec((ts, d), lambda i: (i, 0))
    vec = pl.BlockSpec((1, d), lambda i: (0, 0))
    lvec = pl.BlockSpec((1, LANES), lambda i: (0, 0))
    return pl.pallas_call(
        body, name=name, grid=(s // ts,), in_specs=[row, vec, row], out_specs=[lvec, row, row, vec],
        out_shape=[jax.ShapeDtypeStruct((1, LANES), F32), jax.ShapeDtypeStruct((s, d), F32),
                   jax.ShapeDtypeStruct((s, d), BF16), jax.ShapeDtypeStruct((1, d), F32)],
        compiler_params=_params("arbitrary"),
    )(x, g, target)


def _gate_fwd(name, pb, bpad, fl_block):
    s = pb.shape[0]
    nb = s // TILE

    def body(fl_ref, b_ref, ccol_ref, crow_ref, carry):
        @pl.when(pl.program_id(0) == 0)
        def _():
            carry[...] = jnp.zeros_like(carry)

        u = fl_ref[...] + b_ref[...]
        lf = jnp.minimum(u, 0.0) - jnp.log1p(jnp.exp(-jnp.abs(u)))
        lower = _tri(TILE, lambda r, c: c <= r)
        c = _sum_r3(lower, lf) + carry[0:1, :]
        ccol_ref[...] = c
        crow_ref[0] = c.T[0:8, :]
        carry[...] = jnp.broadcast_to(c[TILE - 1:TILE, :], carry.shape)

    return pl.pallas_call(
        body, name=name, grid=(nb,),
        in_specs=[pl.BlockSpec((TILE, LANES), lambda i: (i, fl_block)), pl.BlockSpec((1, LANES), lambda i: (0, 0))],
        out_specs=[pl.BlockSpec((TILE, LANES), lambda i: (i, 0)), pl.BlockSpec((1, 8, TILE), lambda i: (i, 0, 0))],
        out_shape=[jax.ShapeDtypeStruct((s, LANES), F32), jax.ShapeDtypeStruct((nb, 8, TILE), F32)],
        scratch_shapes=[pltpu.VMEM((8, LANES), F32)],
        compiler_params=_params("arbitrary"),
    )(pb, bpad)


def _gate_bwd(name, pb, bpad, colsum, fl_block):
    s = pb.shape[0]
    nb = s // TILE

    def body(fl_ref, b_ref, cs_ref, dl_ref, db_ref, carry):
        @pl.when(pl.program_id(0) == 0)
        def _():
            carry[...] = jnp.zeros_like(carry)
            db_ref[...] = jnp.zeros_like(db_ref)

        upper = _tri(TILE, lambda r, c: r >= c)
        rsum = _sum_l3(cs_ref[0], upper) + carry[:, 0:1]
        carry[...] = jnp.broadcast_to(rsum[:, 0:1], carry.shape)
        full = jnp.concatenate([rsum, jnp.zeros((LANES - 8, TILE), F32)], axis=0)
        dlf = -full.T
        u = fl_ref[...] + b_ref[...]
        dlogit = dlf * (1.0 - jax.nn.sigmoid(u))
        dl_ref[...] = dlogit.astype(BF16)
        db_ref[...] += jnp.sum(dlogit, axis=0, keepdims=True)

    rev = lambda i: (nb - 1 - i, 0)
    return pl.pallas_call(
        body, name=name, grid=(nb,),
        in_specs=[pl.BlockSpec((TILE, LANES), lambda i: (nb - 1 - i, fl_block)),
                  pl.BlockSpec((1, LANES), lambda i: (0, 0)),
                  pl.BlockSpec((1, 8, TILE), lambda i: (nb - 1 - i, 0, 0))],
        out_specs=[pl.BlockSpec((TILE, LANES), rev), pl.BlockSpec((1, LANES), lambda i: (0, 0))],
        out_shape=[jax.ShapeDtypeStruct((s, LANES), BF16), jax.ShapeDtypeStruct((1, LANES), F32)],
        scratch_shapes=[pltpu.VMEM((8, LANES), F32)],
        compiler_params=_params("arbitrary"),
    )(pb, bpad, colsum)


def _head_slices(hh):
    return slice(HEAD_DIM * hh, HEAD_DIM * (hh + 1))


def _scaled_q(q_ref, sl, scale=SCALE):
    return (q_ref[:, sl].astype(F32) * scale).astype(BF16)


def _neg_abs(x):
    sign = jnp.uint32(0x80000000)
    return lax.bitcast_convert_type(lax.bitcast_convert_type(x, jnp.uint32) | sign, F32)


def _sb_tile(qn, kj, carry, strict, u_after, diag):
    nz = _dot_nt(qn, kj)
    lf = jnp.minimum(nz, 0.0) - jnp.log(1.0 + jnp.exp(_neg_abs(nz)))
    lsig = lf - nz
    if diag:
        lf = jnp.where(strict, lf, 0.0)
    sx = _dot(lf.astype(BF16), u_after)
    a = jnp.exp(lsig + sx + carry)
    if diag:
        a = jnp.where(strict, a, 0.0)
    return lsig, a, carry + sx[:, 0:1] + lf[:, 0:1]


def _pair_grid_call(name, body, nb, in_specs, out_specs, out_shape, scratch, args, carried=None, groups=4):
    if carried is None:
        return pl.pallas_call(
            body, name=name, grid=(groups, nb), in_specs=in_specs, out_specs=out_specs, out_shape=out_shape,
            scratch_shapes=scratch, compiler_params=_params("arbitrary", "arbitrary"),
        )(*args)
    n_in, n_out, n_ex = len(in_specs), len(out_specs), carried.n

    def body_with_copies(*refs):
        own_in, ex_in = refs[:n_in], refs[n_in:n_in + n_ex]
        own_out = refs[n_in + n_ex:n_in + n_ex + n_out]
        ex_out = refs[n_in + n_ex + n_out:n_in + 2 * n_ex + n_out]
        own_scratch, sems = refs[n_in + 2 * n_ex + n_out:-2], refs[-2:]
        parts = (ex_in, ex_out, sems[0], sems[1])
        p, i = pl.program_id(0), pl.program_id(1)
        pl.when(jnp.logical_and(p == 0, i == 0))(lambda: carried.begin(*parts))
        if carried.relay is not None:
            pl.when(jnp.logical_and(p == groups - 1, i == max(nb - 2, 0)))(lambda: carried.relay(*parts))
        body(*own_in, *own_out, *own_scratch)
        pl.when(jnp.logical_and(p == groups - 1, i == nb - 1))(lambda: carried.finish(*parts))

    return pl.pallas_call(
        body_with_copies, name=name, grid=(groups, nb), in_specs=list(in_specs) + [HBM_SPEC] * n_ex,
        out_specs=list(out_specs) + [HBM_SPEC] * n_ex, out_shape=list(out_shape) + carried.out_shapes,
        scratch_shapes=list(scratch) + _dma_sems(carried.n_sems),
        compiler_params=_params("arbitrary", "arbitrary"),
    )(*args, *carried.inputs)


def _sb_fwd(name, pa, col0, carried=None):
    s = pa.shape[0]
    nb = s // TILE
    cb = col0 // SB_LANES
    kb = SB_WIDTH // SB_LANES

    def body(q_ref, k_ref, v_ref, o_ref, lsig_s, lf_s):
        i = pl.program_id(1)
        r = lax.broadcasted_iota(jnp.int32, (TILE, TILE), 0)
        c = lax.broadcasted_iota(jnp.int32, (TILE, TILE), 1)
        strict = c < r
        u_after = _tri(TILE, lambda rr, cc: rr > cc)
        qs = [_scaled_q(q_ref, _head_slices(hh), -SCALE) for hh in range(SB_GROUP)]

        def neg_z(j):
            kblk = k_ref[_rows(k_ref, j), :]
            return [_dot_nt(qs[hh], kblk[:, _head_slices(hh)]) for hh in range(SB_GROUP)]

        def scores(nzs, slot, diag):
            for hh, nz in enumerate(nzs):
                lf = jnp.minimum(nz, 0.0) - jnp.log(1.0 + jnp.exp(_neg_abs(nz)))
                lsig = lf - nz
                if diag:
                    lf = jnp.where(strict, lf, 0.0)
                    lsig = jnp.where(strict, lsig, MASKED)
                lsig_s[slot, hh] = lsig
                lf_s[slot, hh] = lf.astype(BF16)

        def weigh(j, slot, state):
            vblk = v_ref[_rows(v_ref, j), :]
            new = []
            for hh in range(SB_GROUP):
                carry, acc = state[hh]
                lfb = lf_s[slot, hh]
                sx = _dot(lfb, u_after)
                a = jnp.exp(lsig_s[slot, hh] + sx + carry)
                new.append((carry + sx[:, 0:1] + lfb[:, 0:1].astype(F32),
                            acc + _dot(a.astype(BF16), vblk[:, _head_slices(hh)])))
            return tuple(new)

        def step(t, state):
            state = weigh(i - t + 1, (t - 1) % 2, state)
            scores(neg_z(i - t), t % 2, False)
            return state

        zero = (jnp.zeros((TILE, 1), F32), jnp.zeros((TILE, HEAD_DIM), F32))
        scores(neg_z(i), 0, True)
        state = lax.fori_loop(1, i + 1, step, (zero,) * SB_GROUP)
        state = weigh(0, i % 2, state)
        o_ref[...] = jnp.concatenate([st[1] for st in state], axis=1)

    outs = _pair_grid_call(
        name, body, nb,
        in_specs=[pl.BlockSpec((TILE, SB_LANES), lambda p, i: (i, cb + p)),
                  pl.BlockSpec((s, SB_LANES), lambda p, i: (0, cb + kb + p)),
                  pl.BlockSpec((s, SB_LANES), lambda p, i: (0, cb + 2 * kb + p))],
        out_specs=[pl.BlockSpec((TILE, SB_LANES), lambda p, i: (i, p))],
        out_shape=[jax.ShapeDtypeStruct((s, SB_WIDTH), F32)],
        scratch=[pltpu.VMEM((2, SB_GROUP, TILE, TILE), F32), pltpu.VMEM((2, SB_GROUP, TILE, TILE), BF16)],
        args=(pa, pa, pa), carried=carried, groups=kb)
    return outs[0], outs[1:]


def _sb_bwd(name, pa, col0, dout, dcol0, carried=None):
    s = pa.shape[0]
    nb = s // TILE
    cb = col0 // SB_LANES
    kb = SB_WIDTH // SB_LANES
    db = dcol0 // SB_LANES

    def body(q_ref, k_ref, v_ref, do_ref, dq_ref, dk_ref, dv_ref, dk_acc, dv_acc, dpan, span, gsum, lsig_s, lf_s):
        i = pl.program_id(1)

        @pl.when(i == 0)
        def _():
            dk_acc[...] = jnp.zeros_like(dk_acc)
            dv_acc[...] = jnp.zeros_like(dv_acc)

        r = lax.broadcasted_iota(jnp.int32, (TILE, TILE), 0)
        c = lax.broadcasted_iota(jnp.int32, (TILE, TILE), 1)
        strict = c < r
        u_after = _tri(TILE, lambda rr, cc: rr > cc)
        u_before = _tri(TILE, lambda rr, cc: rr < cc)
        qs = [_scaled_q(q_ref, _head_slices(hh), -SCALE) for hh in range(SB_GROUP)]
        dos = [do_ref[:, _head_slices(hh)].astype(BF16) for hh in range(SB_GROUP)]
        dots = [do_ref[:, _head_slices(hh)].T.astype(BF16) for hh in range(SB_GROUP)]
        qts = [q.astype(F32).T.astype(BF16) for q in qs]

        def scores(j, slot, diag):
            kblk = k_ref[_rows(k_ref, j), :]
            for hh in range(SB_GROUP):
                nz = _dot_nt(qs[hh], kblk[:, _head_slices(hh)])
                lf = jnp.minimum(nz, 0.0) - jnp.log(1.0 + jnp.exp(_neg_abs(nz)))
                lsig = lf - nz
                if diag:
                    lf = jnp.where(strict, lf, 0.0)
                    lsig = jnp.where(strict, lsig, MASKED)
                lsig_s[slot, hh] = lsig
                lf_s[slot, hh] = lf.astype(BF16)

        def grads(j, slot, carries):
            vblk = v_ref[_rows(v_ref, j), :]
            new = []
            for hh in range(SB_GROUP):
                lfb = lf_s[slot, hh]
                lsig = lsig_s[slot, hh]
                sx = _dot(lfb, u_after)
                a = jnp.exp(lsig + sx + carries[hh])
                g = a * _dot_nt(dos[hh], vblk[:, _head_slices(hh)])
                sig = jnp.exp(lsig)
                inside = _dot(g.astype(BF16), u_before)
                dpan[hh, j] = sig * (inside + g) - g
                span[hh, j] = sig
                gsum[hh, j] = inside[:, TILE - 1:TILE] + g[:, TILE - 1:TILE]
                dv_acc[hh, j] += _dot(dots[hh], a.astype(BF16))
                new.append(carries[hh] + sx[:, 0:1] + lfb[:, 0:1].astype(F32))
            return tuple(new)

        def step1(t, carries):
            carries = grads(i - t + 1, (t - 1) % 2, carries)
            scores(i - t, t % 2, False)
            return carries

        zero1 = jnp.zeros((TILE, 1), F32)
        scores(i, 0, True)
        carries = lax.fori_loop(1, i + 1, step1, (zero1,) * SB_GROUP)
        grads(0, i % 2, carries)

        def pass2(j, state):
            kblk = k_ref[_rows(k_ref, j), :]
            new = []
            for hh in range(SB_GROUP):
                before, ndq = state[hh]
                ndzb = (dpan[hh, j] + span[hh, j] * before).astype(BF16)
                dk_acc[hh, j] += _dot(qts[hh], ndzb)
                new.append((before + gsum[hh, j], ndq + _dot(ndzb, kblk[:, _head_slices(hh)])))
            return tuple(new)

        zero2 = (zero1, jnp.zeros((TILE, HEAD_DIM), F32))
        state = lax.fori_loop(0, i + 1, pass2, (zero2,) * SB_GROUP)
        dq_ref[...] = jnp.concatenate([st[1] * -SCALE for st in state], axis=1).astype(BF16)

        @pl.when(i == nb - 1)
        def _():
            for acc, ref in ((dk_acc, dk_ref), (dv_acc, dv_ref)):
                for j in range(nb):
                    ref[j * TILE:(j + 1) * TILE, :] = jnp.concatenate(
                        [acc[hh, j].T for hh in range(SB_GROUP)], axis=1).astype(BF16)

    qspec = pl.BlockSpec((TILE, SB_LANES), lambda p, i: (i, p))
    kvspec = pl.BlockSpec((s, SB_LANES), lambda p, i: (0, p))
    out = jax.ShapeDtypeStruct((s, SB_WIDTH), BF16)
    outs = _pair_grid_call(
        name, body, nb,
        in_specs=[pl.BlockSpec((TILE, SB_LANES), lambda p, i: (i, cb + p)),
                  pl.BlockSpec((s, SB_LANES), lambda p, i: (0, cb + kb + p)),
                  pl.BlockSpec((s, SB_LANES), lambda p, i: (0, cb + 2 * kb + p)),
                  pl.BlockSpec((TILE, SB_LANES), lambda p, i: (i, db + p))],
        out_specs=[qspec, kvspec, kvspec], out_shape=[out, out, out],
        scratch=[pltpu.VMEM((SB_GROUP, nb, HEAD_DIM, TILE), F32), pltpu.VMEM((SB_GROUP, nb, HEAD_DIM, TILE), F32),
                 pltpu.VMEM((SB_GROUP, nb, TILE, TILE), F32), pltpu.VMEM((SB_GROUP, nb, TILE, TILE), F32),
                 pltpu.VMEM((SB_GROUP, nb, TILE, 1), F32),
                 pltpu.VMEM((2, SB_GROUP, TILE, TILE), F32), pltpu.VMEM((2, SB_GROUP, TILE, TILE), BF16)],
        args=(pa, pa, pa, dout), carried=carried, groups=kb)
    return outs[:3], outs[3:]


def _fox_scores(q, kj, cq, crj, causal, diag):
    sc = _dot_nt(q, kj) + (cq - crj)
    if diag:
        sc = jnp.where(causal, sc, NEG_INF)
    return sc


def _fox_fwd(name, pa, col0, ccol4, crow4, carried=None):
    s = pa.shape[0]
    nb = s // TILE
    cb = col0 // FOX_LANES
    kb = FOX_WIDTH // FOX_LANES

    def body(q_ref, k_ref, v_ref, cc_ref, cr_ref, o_ref, lse_ref, sc_s):
        i = pl.program_id(1)
        r = lax.broadcasted_iota(jnp.int32, (TILE, TILE), 0)
        c = lax.broadcasted_iota(jnp.int32, (TILE, TILE), 1)
        causal = c <= r
        qs = [_scaled_q(q_ref, _head_slices(hh)) for hh in range(FOX_GROUP)]
        cqs = [cc_ref[:, HEAD_DIM * hh:HEAD_DIM * hh + 1] for hh in range(FOX_GROUP)]

        def logits(j, slot, diag):
            kblk = k_ref[_rows(k_ref, j), :]
            tops = []
            for hh in range(FOX_GROUP):
                sc = _fox_scores(qs[hh], kblk[:, _head_slices(hh)], cqs[hh], cr_ref[j, hh:hh + 1, :], causal, diag)
                sc_s[slot, hh] = sc
                tops.append(jnp.max(sc, axis=1, keepdims=True))
            return tuple(tops)

        def update(j, slot, tops, state):
            vblk = v_ref[_rows(v_ref, j), :]
            new = []
            for hh in range(FOX_GROUP):
                m, l, acc = state[hh]
                m2 = jnp.maximum(m, tops[hh])
                alpha = jnp.exp(m - m2)
                p = jnp.exp(sc_s[slot, hh] - m2)
                new.append((m2, l * alpha + jnp.sum(p, axis=1, keepdims=True),
                            acc * alpha + _dot(p.astype(BF16), vblk[:, _head_slices(hh)])))
            return tuple(new)

        def step(t, both):
            tops, state = both
            state = update(i - t + 1, (t - 1) % 2, tops, state)
            return logits(i - t, t % 2, False), state

        zero = (jnp.full((TILE, 1), NEG_INF, F32), jnp.zeros((TILE, 1), F32), jnp.zeros((TILE, HEAD_DIM), F32))
        tops, state = lax.fori_loop(1, i + 1, step, (logits(i, 0, True), (zero,) * FOX_GROUP))
        state = update(0, i % 2, tops, state)
        o_ref[...] = jnp.concatenate([st[2] / st[1] for st in state], axis=1)
        lse_ref[...] = jnp.concatenate(
            [jnp.broadcast_to(st[0] + jnp.log(st[1]), (TILE, HEAD_DIM)) for st in state], axis=1)

    outs = _pair_grid_call(
        name, body, nb,
        in_specs=[pl.BlockSpec((TILE, FOX_LANES), lambda p, i: (i, cb + p)),
                  pl.BlockSpec((s, FOX_LANES), lambda p, i: (0, cb + kb + p)),
                  pl.BlockSpec((s, FOX_LANES), lambda p, i: (0, cb + 2 * kb + p)),
                  pl.BlockSpec((None, TILE, FOX_LANES), lambda p, i: (p, i, 0)),
                  pl.BlockSpec((None, nb, 8, TILE), lambda p, i: (p, 0, 0, 0))],
        out_specs=[pl.BlockSpec((TILE, FOX_LANES), lambda p, i: (i, p)),
                   pl.BlockSpec((None, TILE, FOX_LANES), lambda p, i: (p, i, 0))],
        out_shape=[jax.ShapeDtypeStruct((s, FOX_WIDTH), F32), jax.ShapeDtypeStruct((kb, s, FOX_LANES), F32)],
        scratch=[pltpu.VMEM((2, FOX_GROUP, TILE, TILE), F32)],
        args=(pa, pa, pa, ccol4, crow4), carried=carried, groups=kb)
    return outs[0], outs[1], outs[2:]


def _fox_bwd(name, pa, col0, ccol4, crow4, out, lse, dout, dcol0):
    s = pa.shape[0]
    nb = s // TILE
    cb = col0 // FOX_LANES
    kb = FOX_WIDTH // FOX_LANES
    db = dcol0 // FOX_LANES

    def body(q_ref, k_ref, v_ref, cc_ref, cr_ref, o_ref, lse_ref, do_ref,
             dq_ref, dk_ref, dv_ref, cs_ref, dk_acc, dv_acc, p_s, ds_s):
        i = pl.program_id(1)

        @pl.when(i == 0)
        def _():
            dk_acc[...] = jnp.zeros_like(dk_acc)
            dv_acc[...] = jnp.zeros_like(dv_acc)
            cs_ref[...] = jnp.zeros_like(cs_ref)

        r = lax.broadcasted_iota(jnp.int32, (TILE, TILE), 0)
        c = lax.broadcasted_iota(jnp.int32, (TILE, TILE), 1)
        causal = c <= r
        qs = [_scaled_q(q_ref, _head_slices(hh)) for hh in range(FOX_GROUP)]
        cqs = [cc_ref[:, HEAD_DIM * hh:HEAD_DIM * hh + 1] for hh in range(FOX_GROUP)]
        lses = [lse_ref[:, HEAD_DIM * hh:HEAD_DIM * hh + 1] for hh in range(FOX_GROUP)]
        dofs = [do_ref[:, _head_slices(hh)] for hh in range(FOX_GROUP)]
        dos = [d_.astype(BF16) for d_ in dofs]
        dots = [d_.T.astype(BF16) for d_ in dofs]
        qts = [q.astype(F32).T.astype(BF16) for q in qs]
        deltas = [jnp.sum(dofs[hh] * o_ref[:, _head_slices(hh)], axis=1, keepdims=True) for hh in range(FOX_GROUP)]

        def probs(j, slot, rowsums, diag):
            kblk = k_ref[_rows(k_ref, j), :]
            vblk = v_ref[_rows(v_ref, j), :]
            new = []
            for hh in range(FOX_GROUP):
                sl = _head_slices(hh)
                sc = _fox_scores(qs[hh], kblk[:, sl], cqs[hh], cr_ref[j, hh:hh + 1, :], causal, diag)
                p = jnp.exp(sc - lses[hh])
                ds = p * (_dot_nt(dos[hh], vblk[:, sl]) - deltas[hh])
                p_s[slot, hh] = p.astype(BF16)
                ds_s[slot, hh] = ds.astype(BF16)
                cs_ref[j, hh:hh + 1, :] += jnp.sum(ds, axis=0, keepdims=True)
                new.append(rowsums[hh] + jnp.sum(ds, axis=1, keepdims=True))
            return tuple(new)

        def accumulate(j, slot, dqs):
            kblk = k_ref[_rows(k_ref, j), :]
            new = []
            for hh in range(FOX_GROUP):
                dsb = ds_s[slot, hh]
                dv_acc[hh, j] += _dot(dots[hh], p_s[slot, hh])
                dk_acc[hh, j] += _dot(qts[hh], dsb)
                new.append(dqs[hh] + _dot(dsb, kblk[:, _head_slices(hh)]))
            return tuple(new)

        def step(t, both):
            rowsums, dqs = both
            dqs = accumulate(i - t + 1, (t - 1) % 2, dqs)
            return probs(i - t, t % 2, rowsums, False), dqs

        zero1 = jnp.zeros((TILE, 1), F32)
        zero64 = jnp.zeros((TILE, HEAD_DIM), F32)
        rowsums, dqs = lax.fori_loop(1, i + 1, step,
                                     (probs(i, 0, (zero1,) * FOX_GROUP, True), (zero64,) * FOX_GROUP))
        dqs = accumulate(0, i % 2, dqs)
        for hh in range(FOX_GROUP):
            cs_ref[i, hh:hh + 1, :] -= jnp.broadcast_to(rowsums[hh], (TILE, LANES)).T[0:1, :]
        dq_ref[...] = jnp.concatenate([dq * SCALE for dq in dqs], axis=1).astype(BF16)

        @pl.when(i == nb - 1)
        def _():
            for acc, ref in ((dk_acc, dk_ref), (dv_acc, dv_ref)):
                for j in range(nb):
                    ref[j * TILE:(j + 1) * TILE, :] = jnp.concatenate(
                        [acc[hh, j].T for hh in range(FOX_GROUP)], axis=1).astype(BF16)

    qspec = pl.BlockSpec((TILE, FOX_LANES), lambda p, i: (i, p))
    kvspec = pl.BlockSpec((s, FOX_LANES), lambda p, i: (0, p))
    o3 = jax.ShapeDtypeStruct((s, FOX_WIDTH), BF16)
    return pl.pallas_call(
        body, name=name, grid=(kb, nb),
        in_specs=[pl.BlockSpec((TILE, FOX_LANES), lambda p, i: (i, cb + p)),
                  pl.BlockSpec((s, FOX_LANES), lambda p, i: (0, cb + kb + p)),
                  pl.BlockSpec((s, FOX_LANES), lambda p, i: (0, cb + 2 * kb + p)),
                  pl.BlockSpec((None, TILE, FOX_LANES), lambda p, i: (p, i, 0)),
                  pl.BlockSpec((None, nb, 8, TILE), lambda p, i: (p, 0, 0, 0)),
                  qspec,
                  pl.BlockSpec((None, TILE, FOX_LANES), lambda p, i: (p, i, 0)),
                  pl.BlockSpec((TILE, FOX_LANES), lambda p, i: (i, db + p))],
        out_specs=[qspec, kvspec, kvspec, pl.BlockSpec((None, nb, 8, TILE), lambda p, i: (p, 0, 0, 0))],
        out_shape=[o3, o3, o3, jax.ShapeDtypeStruct((kb, nb, 8, TILE), F32)],
        scratch_shapes=[pltpu.VMEM((FOX_GROUP, nb, HEAD_DIM, TILE), F32), pltpu.VMEM((FOX_GROUP, nb, HEAD_DIM, TILE), F32),
                        pltpu.VMEM((2, FOX_GROUP, TILE, TILE), BF16), pltpu.VMEM((2, FOX_GROUP, TILE, TILE), BF16)],
        compiler_params=_params("arbitrary", "arbitrary"),
    )(pa, pa, pa, ccol4, crow4, out, lse, dout)


def _mem_fwd(name, pa, mkv):
    s = pa.shape[0]
    ml = mkv.shape[0]
    nb = s // TILE
    cb = QKV_WIDTH // LANES

    def body(q_ref, k_ref, v_ref, o_ref, lse_ref):
        outs, lses = [], []
        for hh in range(2):
            sl = _head_slices(hh)
            sc = _dot_nt(_scaled_q(q_ref, sl), k_ref[:, sl])
            m = jnp.max(sc, axis=1, keepdims=True)
            p = jnp.exp(sc - m)
            l = jnp.sum(p, axis=1, keepdims=True)
            outs.append(_dot(p.astype(BF16), v_ref[:, sl]) / l)
            lses.append(jnp.broadcast_to(m + jnp.log(l), (TILE, HEAD_DIM)))
        o_ref[...] = jnp.concatenate(outs, axis=1)
        lse_ref[...] = jnp.concatenate(lses, axis=1)

    return pl.pallas_call(
        body, name=name, grid=(2, nb),
        in_specs=[pl.BlockSpec((TILE, LANES), lambda p, i: (i, cb + p)),
                  pl.BlockSpec((ml, LANES), lambda p, i: (0, p)),
                  pl.BlockSpec((ml, LANES), lambda p, i: (0, 2 + p))],
        out_specs=[pl.BlockSpec((TILE, LANES), lambda p, i: (i, p)),
                   pl.BlockSpec((None, TILE, LANES), lambda p, i: (p, i, 0))],
        out_shape=[jax.ShapeDtypeStruct((s, MEM_WIDTH), F32), jax.ShapeDtypeStruct((2, s, LANES), F32)],
        compiler_params=_params("parallel", "parallel"),
    )(pa, mkv, mkv)


def _mem_bwd(name, pa, mkv, out, lse, dout, dcol0):
    s = pa.shape[0]
    ml = mkv.shape[0]
    nb = s // TILE
    cb = QKV_WIDTH // LANES
    db = dcol0 // LANES

    def body(q_ref, k_ref, v_ref, o_ref, lse_ref, do_ref, dq_ref, dk_ref, dv_ref, dk_acc, dv_acc):
        i = pl.program_id(1)

        @pl.when(i == 0)
        def _():
            dk_acc[...] = jnp.zeros_like(dk_acc)
            dv_acc[...] = jnp.zeros_like(dv_acc)

        dqs = []
        for hh in range(2):
            sl = _head_slices(hh)
            q = _scaled_q(q_ref, sl)
            kh = k_ref[:, sl]
            dof = do_ref[:, sl]
            do = dof.astype(BF16)
            delta = jnp.sum(dof * o_ref[:, sl], axis=1, keepdims=True)
            p = jnp.exp(_dot_nt(q, kh) - lse_ref[:, HEAD_DIM * hh:HEAD_DIM * hh + 1])
            ds = (p * (_dot_nt(do, v_ref[:, sl]) - delta)).astype(BF16)
            dv_acc[hh] += _dot_tn(p.astype(BF16), do)
            dk_acc[hh] += _dot_tn(ds, q)
            dqs.append(_dot(ds, kh) * SCALE)
        dq_ref[...] = jnp.concatenate(dqs, axis=1).astype(BF16)

        @pl.when(i == nb - 1)
        def _():
            dk_ref[...] = jnp.concatenate([dk_acc[0], dk_acc[1]], axis=1).astype(BF16)
            dv_ref[...] = jnp.concatenate([dv_acc[0], dv_acc[1]], axis=1).astype(BF16)

    qspec = pl.BlockSpec((TILE, LANES), lambda p, i: (i, p))
    kvspec = pl.BlockSpec((ml, LANES), lambda p, i: (0, p))
    okv = jax.ShapeDtypeStruct((ml, MEM_WIDTH), BF16)
    return pl.pallas_call(
        body, name=name, grid=(2, nb),
        in_specs=[pl.BlockSpec((TILE, LANES), lambda p, i: (i, cb + p)),
                  pl.BlockSpec((ml, LANES), lambda p, i: (0, p)),
                  pl.BlockSpec((ml, LANES), lambda p, i: (0, 2 + p)),
                  qspec,
                  pl.BlockSpec((None, TILE, LANES), lambda p, i: (p, i, 0)),
                  pl.BlockSpec((TILE, LANES), lambda p, i: (i, db + p))],
        out_specs=[qspec, kvspec, kvspec],
        out_shape=[jax.ShapeDtypeStruct((s, MEM_WIDTH), BF16), okv, okv],
        scratch_shapes=[pltpu.VMEM((2, ml, HEAD_DIM), F32), pltpu.VMEM((2, ml, HEAD_DIM), F32)],
        compiler_params=_params("arbitrary", "arbitrary"),
    )(pa, mkv, mkv, out, lse, dout)


def _head_maps():
    col = jnp.arange(MIX_WIDTH)[:, None] // HEAD_DIM
    g = (col == jnp.arange(LANES)[None, :]).astype(BF16)
    return g, g.T


def _normed_heads(osb_ref, ofx_ref, om_ref, g_ref, gt_ref):
    y = jnp.concatenate([osb_ref[...], ofx_ref[...], om_ref[...]], axis=1)
    msq = _sum_l2(y * y, g_ref[...]) * (1.0 / HEAD_DIM)
    rf = _sum_l3(lax.rsqrt(msq + EPS), gt_ref[...])
    return y * rf, rf


def _out_fwd(name, o_sb, o_fx, o_m, pb, ow, x, w_out, ts):
    s, d = x.shape
    g, gt = _head_maps()

    def body(osb_ref, ofx_ref, om_ref, gate_ref, ow_ref, x_ref, w_ref, g_ref, gt_ref, xo_ref, y2_ref):
        yh, _ = _normed_heads(osb_ref, ofx_ref, om_ref, g_ref, gt_ref)
        gate = gate_ref[...]
        y2 = (yh * ow_ref[...] * (gate * jax.nn.sigmoid(gate))).astype(BF16)
        y2_ref[...] = y2
        xo_ref[...] = x_ref[...] + _dot(y2, w_ref[...])

    return pl.pallas_call(
        body, name=name, grid=(s // ts,),
        in_specs=[_row_spec(ts, SB_WIDTH), _row_spec(ts, FOX_WIDTH), _row_spec(ts, MEM_WIDTH),
                  _row_spec(ts, MIX_WIDTH), _const_spec((1, MIX_WIDTH)), _row_spec(ts, d),
                  _const_spec((MIX_WIDTH, d)),
                  _const_spec((MIX_WIDTH, LANES)), _const_spec((LANES, MIX_WIDTH))],
        out_specs=[_row_spec(ts, d), _row_spec(ts, MIX_WIDTH)],
        out_shape=[jax.ShapeDtypeStruct((s, d), F32), jax.ShapeDtypeStruct((s, MIX_WIDTH), BF16)],
        compiler_params=_params("parallel"),
    )(o_sb, o_fx, o_m, pb, ow, x, w_out, g, gt)


def _row_spec(ts, w):
    return pl.BlockSpec((ts, w), lambda i: (i, 0))


def _const_spec(shape):
    return pl.BlockSpec(shape, lambda i: (0,) * len(shape))


def _out_bwd(name, dxb, o_sb, o_fx, o_m, pb, ow, w_out, ts):
    s, d = dxb.shape
    g, gt = _head_maps()

    def body(dx_ref, osb_ref, ofx_ref, om_ref, gate_ref, ow_ref, w_ref, g_ref, gt_ref, dy_ref, dgate_ref, dow_ref):
        @pl.when(pl.program_id(0) == 0)
        def _():
            dow_ref[...] = jnp.zeros_like(dow_ref)

        dy2 = _dot_nt(dx_ref[...], w_ref[...])
        yh, rf = _normed_heads(osb_ref, ofx_ref, om_ref, g_ref, gt_ref)
        gate = gate_ref[...]
        sig = jax.nn.sigmoid(gate)
        ow_v = ow_ref[...]
        dgate_ref[...] = (dy2 * (yh * ow_v) * (sig * (1.0 + gate * (1.0 - sig)))).astype(BF16)
        dn = dy2 * (gate * sig)
        dow_ref[...] += jnp.sum(dn * yh, axis=0, keepdims=True)
        dyh = dn * ow_v
        t = _sum_l2(dyh * yh, g_ref[...]) * (1.0 / HEAD_DIM)
        dy_ref[...] = rf * (dyh - yh * _sum_l3(t, gt_ref[...]))

    return pl.pallas_call(
        body, name=name, grid=(s // ts,),
        in_specs=[_row_spec(ts, d), _row_spec(ts, SB_WIDTH), _row_spec(ts, FOX_WIDTH), _row_spec(ts, MEM_WIDTH),
                  _row_spec(ts, MIX_WIDTH), _const_spec((1, MIX_WIDTH)),
                  _const_spec((MIX_WIDTH, d)),
                  _const_spec((MIX_WIDTH, LANES)), _const_spec((LANES, MIX_WIDTH))],
        out_specs=[_row_spec(ts, MIX_WIDTH), _row_spec(ts, MIX_WIDTH), _const_spec((1, MIX_WIDTH))],
        out_shape=[jax.ShapeDtypeStruct((s, MIX_WIDTH), F32), jax.ShapeDtypeStruct((s, MIX_WIDTH), BF16),
                   jax.ShapeDtypeStruct((1, MIX_WIDTH), F32)],
        compiler_params=_params("arbitrary"),
    )(dxb, o_sb, o_fx, o_m, pb, ow, w_out, g, gt)


def _adamw(name, w, g, m, v, tr):
    def body(w_ref, g_ref, m_ref, v_ref, d_ref, m2_ref, v2_ref):
        gv = g_ref[...]
        m2 = ADAM_B1 * m_ref[...] + (1.0 - ADAM_B1) * gv
        v2 = ADAM_B2 * v_ref[...] + (1.0 - ADAM_B2) * (gv * gv)
        m_hat = m2 / (1.0 - ADAM_B1 ** ADAM_STEP)
        v_hat = v2 / (1.0 - ADAM_B2 ** ADAM_STEP)
        d_ref[...] = -ADAM_LR * (m_hat / (jnp.sqrt(v_hat) + ADAM_EPS) + ADAM_WD * w_ref[...])
        m2_ref[...] = m2
        v2_ref[...] = v2

    rest = w.shape[1:]
    spec = pl.BlockSpec((tr,) + rest, lambda i: (i,) + (0,) * len(rest))
    shp = jax.ShapeDtypeStruct(w.shape, F32)
    return pl.pallas_call(
        body, name=name, grid=(w.shape[0] // tr,), in_specs=[spec] * 4, out_specs=[spec] * 3, out_shape=[shp] * 3,
        compiler_params=_params("parallel"),
    )(w, g, m, v)


def _adamw_sharded(name, w, m, v, g_own, g_other, cvec, tr):
    depth, rows, cols = w.shape
    nt = rows // 2 // tr

    def body(c_ref, w_ref, m_ref, v_ref, *rest):
        g_refs, (g_ref, d_ref, m2_ref, v2_ref) = rest[:2 * depth], rest[2 * depth:]
        layer, mine = pl.program_id(0), pl.program_id(1) == c_ref[0]
        gv = None
        for lt in range(depth):
            cand = jnp.where(mine, g_refs[lt][...], g_refs[depth + lt][...])
            gv = cand if gv is None else jnp.where(layer == lt, cand, gv)
        m2 = ADAM_B1 * m_ref[...] + (1.0 - ADAM_B1) * gv
        v2 = ADAM_B2 * v_ref[...] + (1.0 - ADAM_B2) * (gv * gv)
        m_hat = m2 / (1.0 - ADAM_B1 ** ADAM_STEP)
        v_hat = v2 / (1.0 - ADAM_B2 ** ADAM_STEP)
        g_ref[...] = gv
        d_ref[...] = -ADAM_LR * (m_hat / (jnp.sqrt(v_hat) + ADAM_EPS) + ADAM_WD * w_ref[...])
        m2_ref[...] = m2
        v2_ref[...] = v2

    def g_map(lt, own):
        def index(l, hf, i, c_ref):
            use = jnp.logical_and(l == lt, (hf == c_ref[0]) == own)
            return jnp.where(use, i, 0), 0
        return index

    full = pl.BlockSpec((None, tr, cols), lambda l, hf, i, c_ref: (l, hf * nt + i, 0))
    g_specs = [pl.BlockSpec((tr, cols), g_map(lt, own)) for own in (True, False) for lt in range(depth)]
    shp = jax.ShapeDtypeStruct((depth, rows, cols), F32)
    return pl.pallas_call(
        body, name=name,
        grid_spec=pltpu.PrefetchScalarGridSpec(
            num_scalar_prefetch=1, grid=(depth, 2, nt), in_specs=[full] * 3 + g_specs, out_specs=[full] * 4),
        out_shape=[shp] * 4,
        compiler_params=_params("arbitrary", "arbitrary", "arbitrary"),
    )(cvec, w, m, v, *g_own, *g_other)


HBM_SPEC = pl.BlockSpec(memory_space=pltpu.HBM)


def _place():
    x, y, c = lax.axis_index("x"), lax.axis_index("y"), lax.axis_index("c")
    chips = [(1 - x, y), (x, 1 - y), (1 - x, 1 - y)]
    return x, y, c, chips


def _remote(src, dst, send_sems, recv_sems, k, to):
    return pltpu.make_async_remote_copy(src_ref=src, dst_ref=dst, send_sem=send_sems.at[k], recv_sem=recv_sems.at[k],
                                        device_id=to, device_id_type=MESH)


def _half_rows(n_rows, cc):
    rh = n_rows // 2
    return pl.ds(pl.multiple_of(cc * rh, 16), rh)


def _dma_sems(n):
    return [pltpu.SemaphoreType.DMA((n,)), pltpu.SemaphoreType.DMA((n,))]


class _Exchange:
    def __init__(self, inputs, out_shapes, n_sems, begin, relay, finish):
        self.inputs, self.out_shapes, self.n_sems = list(inputs), list(out_shapes), n_sems
        self.begin, self.relay, self.finish = begin, relay, finish

    @property
    def n(self):
        return len(self.inputs)

    def split(self, refs):
        return refs[:self.n], refs[self.n:2 * self.n], refs[2 * self.n], refs[2 * self.n + 1]


def _run_exchange(name, ex):
    def body(*refs):
        parts = ex.split(refs)
        for phase in (ex.begin, ex.relay, ex.finish):
            if phase is not None:
                phase(*parts)

    return pl.pallas_call(
        body, name=name, in_specs=[HBM_SPEC] * ex.n, out_specs=[HBM_SPEC] * ex.n, out_shape=ex.out_shapes,
        scratch_shapes=_dma_sems(ex.n_sems),
    )(*ex.inputs)


def _gather_exchange(shards):
    def ici(in_refs, out_refs, send_sems, recv_sems):
        x, y, c, chips = _place()
        return [_remote(in_ref.at[_half_rows(in_ref.shape[0], c)], out_ref.at[2 * x + y, _half_rows(in_ref.shape[0], c)],
                        send_sems, recv_sems, 6 * a + j, (cx, cy, c))
                for a, (in_ref, out_ref) in enumerate(zip(in_refs, out_refs)) for j, (cx, cy) in enumerate(chips)]

    def d2d(out_refs, send_sems, recv_sems, half_of):
        x, y, c, chips = _place()
        cps = []
        for a, out_ref in enumerate(out_refs):
            for j, (cx, cy) in enumerate(chips):
                piece = out_ref.at[2 * cx + cy, _half_rows(out_ref.shape[1], half_of(c))]
                cps.append(_remote(piece, piece, send_sems, recv_sems, 6 * a + 3 + j, (x, y, 1 - c)))
        return cps

    def begin(in_refs, out_refs, send_sems, recv_sems):
        for cp in ici(in_refs, out_refs, send_sems, recv_sems):
            cp.start()

    def relay(in_refs, out_refs, send_sems, recv_sems):
        x, y, c, chips = _place()
        for a, out_ref in enumerate(out_refs):
            for j, (cx, cy) in enumerate(chips):
                landed = out_ref.at[2 * cx + cy, _half_rows(out_ref.shape[1], c)]
                _remote(landed, landed, send_sems, recv_sems, 6 * a + j, (cx, cy, c)).wait_recv()
        for cp in d2d(out_refs, send_sems, recv_sems, lambda c_: c_):
            cp.start()

    def finish(in_refs, out_refs, send_sems, recv_sems):
        for cp in d2d(out_refs, send_sems, recv_sems, lambda c_: 1 - c_):
            cp.wait_recv()
        for cp in ici(in_refs, out_refs, send_sems, recv_sems) + d2d(out_refs, send_sems, recv_sems, lambda c_: c_):
            cp.wait_send()

    shapes = [jax.ShapeDtypeStruct((N_CHIPS,) + s_.shape, s_.dtype) for s_ in shards]
    return _Exchange(shards, shapes, 6 * len(shards), begin, relay, finish)


def _swap_halves(name, g4s):
    n = len(g4s)

    def body(*refs):
        in_refs, out_refs, (send_sems, recv_sems) = refs[:n], refs[n:2 * n], refs[2 * n:]
        x, y, c, _ = _place()
        cps = [_remote(in_ref.at[:, _half_rows(in_ref.shape[1], 1 - c), :], out_ref, send_sems, recv_sems, a, (x, y, 1 - c))
               for a, (in_ref, out_ref) in enumerate(zip(in_refs, out_refs))]
        for cp in cps:
            cp.start()
        for cp in cps:
            cp.wait()

    return pl.pallas_call(
        body, name=name, in_specs=[HBM_SPEC] * n, out_specs=[HBM_SPEC] * n,
        out_shape=[jax.ShapeDtypeStruct((g.shape[0], g.shape[1] // 2, g.shape[2]), g.dtype) for g in g4s],
        scratch_shapes=_dma_sems(n),
    )(*g4s)


def _add_half(name, g4, r1, cvec, tr):
    n, r, w = g4.shape
    rh = r // 2
    nblk = rh // tr

    def body(c_ref, a_ref, b_ref, o_ref):
        o_ref[...] = (a_ref[...] + b_ref[...]).astype(BF16)

    return pl.pallas_call(
        body, name=name,
        grid_spec=pltpu.PrefetchScalarGridSpec(
            num_scalar_prefetch=1, grid=(n, nblk),
            in_specs=[pl.BlockSpec((None, tr, w), lambda k, i, c_ref: (k, c_ref[0] * nblk + i, 0)),
                      pl.BlockSpec((None, tr, w), lambda k, i, c_ref: (k, i, 0))],
            out_specs=pl.BlockSpec((None, tr, w), lambda k, i, c_ref: (k, i, 0))),
        out_shape=jax.ShapeDtypeStruct((n, rh, w), BF16),
        compiler_params=_params("parallel", "parallel"),
    )(cvec, g4, r1)


def _scatter_exchange(h4s):
    def sends(in_refs, out_refs, send_sems, recv_sems):
        x, y, c, chips = _place()
        return [_remote(in_ref.at[2 * cx + cy], out_ref.at[j], send_sems, recv_sems, 3 * a + j, (cx, cy, c))
                for a, (in_ref, out_ref) in enumerate(zip(in_refs, out_refs)) for j, (cx, cy) in enumerate(chips)]

    def begin(*parts):
        for cp in sends(*parts):
            cp.start()

    def finish(in_refs, out_refs, send_sems, recv_sems):
        x, y, c, chips = _place()
        for a, out_ref in enumerate(out_refs):
            for j, (cx, cy) in enumerate(chips):
                got = out_ref.at[j]
                _remote(got, got, send_sems, recv_sems, 3 * a + j, (cx, cy, c)).wait_recv()
        for cp in sends(in_refs, out_refs, send_sems, recv_sems):
            cp.wait_send()

    shapes = [jax.ShapeDtypeStruct((3,) + h.shape[1:], h.dtype) for h in h4s]
    return _Exchange(h4s, shapes, 3 * len(h4s), begin, None, finish)


def _sum_chips(name, h4, r3, mvec, tr):
    _, rh, w = h4.shape

    def body(m_ref, a_ref, b_ref, c_ref, d_ref, o_ref):
        o_ref[...] = ((a_ref[...].astype(F32) + b_ref[...].astype(F32)) + c_ref[...].astype(F32)) + d_ref[...].astype(F32)

    specs = [pl.BlockSpec((None, tr, w), lambda i, m_ref: (m_ref[0], i, 0))]
    specs += [pl.BlockSpec((None, tr, w), functools.partial(lambda k, i, m_ref: (k, i, 0), k)) for k in range(3)]
    return pl.pallas_call(
        body, name=name,
        grid_spec=pltpu.PrefetchScalarGridSpec(
            num_scalar_prefetch=1, grid=(rh // tr,), in_specs=specs,
            out_specs=pl.BlockSpec((tr, w), lambda i, m_ref: (i, 0))),
        out_shape=jax.ShapeDtypeStruct((rh, w), F32),
        compiler_params=_params("parallel"),
    )(mvec, h4, r3, r3, r3)


def _swap_reduced(name, ghs):
    n = len(ghs)

    def body(*refs):
        in_refs, out_refs, (send_sems, recv_sems) = refs[:n], refs[n:2 * n], refs[2 * n:]
        x, y, c, _ = _place()
        cps = [_remote(in_ref, out_ref, send_sems, recv_sems, a, (x, y, 1 - c))
               for a, (in_ref, out_ref) in enumerate(zip(in_refs, out_refs))]
        for cp in cps:
            cp.start()
        for cp in cps:
            cp.wait()

    return pl.pallas_call(
        body, name=name, in_specs=[HBM_SPEC] * n, out_specs=[HBM_SPEC] * n,
        out_shape=[jax.ShapeDtypeStruct(g.shape, g.dtype) for g in ghs],
        scratch_shapes=_dma_sems(n),
    )(*ghs)


def _small_update(name, partials, weights, moments1, moments2):
    n = len(partials)
    width = max(p.shape[1] for p in partials)
    starts, at = [], 0
    for p in partials:
        starts.append(at)
        at += p.shape[0]
    rows = -(-at // 8) * 8
    has_w = [w is not None for w in weights]
    n_w = sum(has_w)

    def body(*refs):
        p_refs = refs[:n]
        w_refs, m_refs, v_refs = refs[n:n + n_w], refs[n + n_w:n + 2 * n_w], refs[n + 2 * n_w:n + 3 * n_w]
        outs = refs[n + 3 * n_w:-4]
        g_refs, upd_refs = outs[:n], outs[n:]
        vec, buf, send_sems, recv_sems = refs[-4:]
        x, y, c, _ = _place()
        me = 4 * x + 2 * y + c
        vec[...] = jnp.zeros_like(vec)
        for p_ref, r0 in zip(p_refs, starts):
            vec[r0:r0 + p_ref.shape[0], 0:p_ref.shape[1]] = p_ref[...]
        buf[me] = vec[...]
        flips = [(fx, fy, fc) for fx in (0, 1) for fy in (0, 1) for fc in (0, 1)][1:]
        peers = [(x + fx - 2 * x * fx, y + fy - 2 * y * fy, c + fc - 2 * c * fc) for fx, fy, fc in flips]
        sends = [_remote(vec, buf.at[me], send_sems, recv_sems, k, peer) for k, peer in enumerate(peers)]
        for cp in sends:
            cp.start()
        for k, (px, py, pc) in enumerate(peers):
            got = buf.at[4 * px + 2 * py + pc]
            _remote(got, got, send_sems, recv_sems, k, (px, py, pc)).wait_recv()
        for cp in sends:
            cp.wait_send()
        total = buf[0]
        for dev in range(1, N_DEV):
            total = total + buf[dev]
        k = 0
        for a in range(n):
            r, w = g_refs[a].shape
            g = total[starts[a]:starts[a] + r, 0:w]
            g_refs[a][...] = g
            if has_w[a]:
                m2 = ADAM_B1 * m_refs[k][...] + (1.0 - ADAM_B1) * g
                v2 = ADAM_B2 * v_refs[k][...] + (1.0 - ADAM_B2) * (g * g)
                m_hat = m2 / (1.0 - ADAM_B1 ** ADAM_STEP)
                v_hat = v2 / (1.0 - ADAM_B2 ** ADAM_STEP)
                upd_refs[3 * k][...] = -ADAM_LR * (m_hat / (jnp.sqrt(v_hat) + ADAM_EPS) + ADAM_WD * w_refs[k][...])
                upd_refs[3 * k + 1][...] = m2
                upd_refs[3 * k + 2][...] = v2
                k += 1

    ws = [w for w in weights if w is not None]
    g_shapes = [jax.ShapeDtypeStruct(p.shape if w is None else w.shape, F32) for p, w in zip(partials, weights)]
    u_shapes = [jax.ShapeDtypeStruct(w.shape, F32) for w in ws for _ in range(3)]
    vm = pl.BlockSpec(memory_space=pltpu.VMEM)
    n_args = n + 3 * n_w
    outs = pl.pallas_call(
        body, name=name, in_specs=[vm] * n_args, out_specs=[vm] * (n + 3 * n_w), out_shape=g_shapes + u_shapes,
        scratch_shapes=[pltpu.VMEM((rows, width), F32), pltpu.VMEM((N_DEV, rows, width), F32),
                        pltpu.SemaphoreType.DMA((7,)), pltpu.SemaphoreType.DMA((7,))],
    )(*partials, *ws, *[m for m in moments1 if m is not None], *[v for v in moments2 if v is not None])
    return outs[:n], outs[n:]


GATE_COL = 3 * SB_WIDTH + 3 * FOX_WIDTH + FOX_HEADS + MEM_WIDTH
FL_COL = QKV_WIDTH


GROUP_A_COLS = [(0, QKV_WIDTH), (FL_COL + FOX_HEADS, MEM_WIDTH)]
GROUP_B_COLS = [(GATE_COL, MIX_WIDTH), (FL_COL, FOX_HEADS)]


def _group_from_shards(shard_of, cw, spans, pad):
    parts = []
    for lo, width in spans:
        hi = lo + width
        for j in range(N_CHIPS):
            a, b = max(lo, j * cw), min(hi, (j + 1) * cw)
            if a < b:
                parts.append(shard_of(j)[:, a - j * cw:b - j * cw])
    if pad:
        parts.append(jnp.zeros((parts[0].shape[0], pad), parts[0].dtype))
    return jnp.concatenate(parts, axis=1)


def _shard_from_groups(ga, gb, j, cw):
    lo, hi = j * cw, (j + 1) * cw
    placed = []
    for grp, spans in ((ga, GROUP_A_COLS), (gb, GROUP_B_COLS)):
        at = 0
        for first, width in spans:
            a, b = max(lo, first), min(hi, first + width)
            if a < b:
                placed.append((a, grp[:, at + a - first:at + b - first]))
            at += width
    return jnp.concatenate([p for _, p in sorted(placed, key=lambda t: t[0])], axis=1)


def _tile_of(n, cap, unit):
    if n <= cap:
        return n
    best = None
    for t in range(unit, cap + 1, unit):
        if n % t == 0:
            best = t
    assert best is not None, (n, cap, unit)
    return best


def _column_major_rows(a):
    dp, r, c = a.shape
    return a.transpose(2, 0, 1).reshape(c, dp, r // LANES, LANES).transpose(0, 2, 1, 3).reshape(-1, 8, LANES)


def _from_column_major_rows(b, shape):
    dp, r, c = shape
    return b.reshape(c, r // LANES, dp, LANES).transpose(0, 2, 1, 3).reshape(c, dp, r).transpose(1, 2, 0)


def _pack_small(parts):
    rows = []
    for p in parts:
        f = p.reshape(-1).astype(F32)
        f = jnp.pad(f, (0, (-f.shape[0]) % LANES))
        rows.append(f.reshape(-1, LANES))
    out = jnp.concatenate(rows, axis=0)
    return jnp.pad(out, ((0, (-out.shape[0]) % 8), (0, 0)))


def _unpack_small(packed, shapes):
    outs, r = [], 0
    for shp in shapes:
        n = 1
        for s_ in shp:
            n *= s_
        nr = -(-n // LANES)
        outs.append(packed[r:r + nr].reshape(-1)[:n].reshape(shp))
        r += nr
    return outs


def kernel(x, mem, norm_w, w_in, b_forget, mem_norm_w, w_mem_kv, out_norm_w, w_out, final_norm_w, loss_target, m_norm_w, m_w_in, m_b_forget, m_mem_norm_w, m_w_mem_kv, m_out_norm_w, m_w_out, m_final_norm_w, v_norm_w, v_w_in, v_b_forget, v_mem_norm_w, v_w_mem_kv, v_out_norm_w, v_w_out, v_final_norm_w):
    xs = x[0]
    mems = mem[0]
    target = loss_target[0]
    s, d = xs.shape
    depth = norm_w.shape[0]
    nb = s // TILE
    ts = _tile_of(s, 256, 8)
    big = (w_in, w_mem_kv, w_out)
    core = lax.axis_index("c")
    chip = 2 * lax.axis_index("x") + lax.axis_index("y")
    cvec = core.astype(jnp.int32).reshape(1)
    mvec = chip.astype(jnp.int32).reshape(1)
    cw = w_in.shape[2]

    own_w = [[a[l].astype(BF16) for a in big] for l in range(depth)]

    def lay_out_in(own, got):
        shard_of = lambda j: jnp.where(chip == j, own, got[j])
        return (_group_from_shards(shard_of, cw, GROUP_A_COLS, 0),
                _group_from_shards(shard_of, cw, GROUP_B_COLS, LANES - FOX_HEADS))

    def lay_out_rows(own, got):
        full = jnp.where(lax.broadcasted_iota(jnp.int32, got.shape, 0) == chip, own[None], got)
        return full.reshape(-1, full.shape[2])

    w_in_groups = [lay_out_in(own_w[0][0], _run_exchange("gather_weights0", _gather_exchange(own_w[0][:1]))[0])]
    layer_w = []

    tm = _tile_of(s, 256, 8)
    fl_block = MIX_WIDTH // LANES

    saved = []
    cur = xs
    for l in range(depth):
        wa, wb = w_in_groups[l]
        h = _rms_fwd(f"rms_fwd{l}", cur, norm_w[l][None], ts)
        pa = _mm(f"inproj_a{l}", h, wa, "nn", tm, _tile_of(PA, 1664, LANES), BF16)
        pb = _mm(f"inproj_b{l}", h, wb, "nn", tm, PB, F32)
        bpad = jnp.pad(b_forget[l], (0, LANES - FOX_HEADS))[None]
        ccol, crow = _gate_fwd(f"gate_fwd{l}", pb, bpad, fl_block)
        fg = FOX_HEADS // FOX_GROUP
        ccol4 = jnp.repeat(ccol[:, :FOX_HEADS].reshape(s, fg, FOX_GROUP).transpose(1, 0, 2), HEAD_DIM, axis=2)
        crow4 = jnp.pad(crow.reshape(nb, fg, FOX_GROUP, TILE).transpose(1, 0, 2, 3),
                        ((0, 0), (0, 0), (0, 8 - FOX_GROUP), (0, 0)))
        more = l + 1 < depth
        riding = own_w[l][1:] + (own_w[l + 1][:1] if more else [])
        o_sb, got = _sb_fwd(f"sb_fwd{l}", pa, 0, carried=_gather_exchange(riding))
        wkv, wout = lay_out_rows(own_w[l][1], got[0]), lay_out_rows(own_w[l][2], got[1])
        layer_w.append((wa, wb, wkv, wout))
        if more:
            w_in_groups.append(lay_out_in(own_w[l + 1][0], got[2]))
        o_fx, lse_fx, _ = _fox_fwd(f"fox_fwd{l}", pa, 3 * SB_WIDTH, ccol4, crow4)
        mn = _rms_fwd(f"mem_rms{l}", mems, mem_norm_w[l][None], mems.shape[0])
        mkv = _mm(f"mem_kv{l}", mn, wkv, "nn", mems.shape[0], 2 * MEM_WIDTH, BF16)
        o_m, lse_m = _mem_fwd(f"mem_fwd{l}", pa, mkv)
        nxt, y2 = _out_fwd(f"out_fwd{l}", o_sb, o_fx, o_m, pb, out_norm_w[l][None], cur, wout, ts)
        saved.append((cur, h, pa, pb, bpad, ccol4, crow4, o_sb, o_fx, lse_fx, mn, mkv, o_m, lse_m, y2))
        cur = nxt

    loss_v, dx, dxb, g_final = _final_loss("final_loss", cur, final_norm_w[None], target, ts)

    g_norm, g_b, g_memnorm, g_outnorm = [None] * depth, [None] * depth, [None] * depth, [None] * depth
    g_wa, g_wb, g_wkv, g_wout = [None] * depth, [None] * depth, [None] * depth, [None] * depth
    g_own = [[None] * depth for _ in big]
    g_other = [[None] * depth for _ in big]

    def within_chip(tag, jobs):
        got = _swap_halves(f"grad_swap_halves{tag}", [g for _, _, g, _ in jobs])
        return [(lr, k, _add_half(f"grad_add_half{lr}_{k}", g, r_, cvec, t_), t_) for (lr, k, g, t_), r_ in zip(jobs, got)]

    def reduce_at_owner(tag, jobs, from_chips):
        halves = [_sum_chips(f"grad_sum_chips{lr}_{k}", h_, r_, mvec, t_) for (lr, k, h_, t_), r_ in zip(jobs, from_chips)]
        others = _swap_reduced(f"grad_swap_reduced{tag}", halves)
        for (lr, k, _, _), mine, other in zip(jobs, halves, others):
            g_own[k][lr], g_other[k][lr] = mine, other

    def job(lr, k, g4):
        return lr, k, g4, _tile_of(g4.shape[1] // 2, 256, 16)

    pending = []
    for l in reversed(range(depth)):
        xin, h, pa, pb, bpad, ccol4, crow4, o_sb, o_fx, lse_fx, mn, mkv, o_m, lse_m, y2 = saved[l]
        wa, wb, wkv, wout = layer_w[l]
        dy, dgate, g_outnorm[l] = _out_bwd(f"out_bwd{l}", dxb, o_sb, o_fx, o_m, pb, out_norm_w[l][None], wout, ts)
        g_wout[l] = _mm(f"dw_out{l}", y2, dxb, "tn", _tile_of(MIX_WIDTH, 640, LANES), d, F32)
        dq_m, dk_m, dv_m = _mem_bwd(f"mem_bwd{l}", pa, mkv, o_m, lse_m, dy, SB_WIDTH + FOX_WIDTH)
        dmkv = jnp.concatenate([dk_m, dv_m], axis=1)
        g_wkv[l] = _mm(f"dw_kv{l}", mn, dmkv, "tn", d, 2 * MEM_WIDTH, F32)
        dmn = _mm(f"dmem{l}", dmkv, wkv, "nt", mems.shape[0], d, F32)
        g_memnorm[l] = _rms_wgrad(f"mem_norm_grad{l}", mems, dmn)
        pending += within_chip(f"{l}s", [job(l, 1, g_wkv[l].reshape(N_CHIPS, -1, g_wkv[l].shape[1])),
                                         job(l, 2, g_wout[l].reshape(N_CHIPS, -1, d))])
        (dq_sb, dk_sb, dv_sb), from_chips = _sb_bwd(f"sb_bwd{l}", pa, 0, dy, 0,
                                                   carried=_scatter_exchange([j[2] for j in pending]))
        reduce_at_owner(f"{l}s", pending, from_chips)
        dq_fx, dk_fx, dv_fx, cs4 = _fox_bwd(f"fox_bwd{l}", pa, 3 * SB_WIDTH, ccol4, crow4, o_fx, lse_fx, dy, SB_WIDTH)
        colsum = cs4[:, :, :FOX_GROUP, :].transpose(1, 0, 2, 3).reshape(nb, 8, TILE)
        dlogit, g_b[l] = _gate_bwd(f"gate_bwd{l}", pb, bpad, colsum, fl_block)
        dpa = jnp.concatenate([dq_sb, dk_sb, dv_sb, dq_fx, dk_fx, dv_fx, dq_m], axis=1)
        dpb = jnp.concatenate([dgate, dlogit], axis=1)
        tw = _tile_of(d, 512, LANES)
        g_wa[l] = _mm(f"dw_in_a{l}", h, dpa, "tn", tw, _tile_of(PA, 1664, LANES), F32)
        g_wb[l] = _mm(f"dw_in_b{l}", h, dpb, "tn", tw, PB, F32)
        g4_in = jnp.stack([_shard_from_groups(g_wa[l], g_wb[l], j, cw) for j in range(N_CHIPS)])
        pending = within_chip(f"{l}", [job(l, 0, g4_in)])
        last = _scatter_exchange([j[2] for j in pending]) if l == 0 else None
        dx, dxb, g_norm[l], from_chips = _inproj_bwd(f"inproj_bwd{l}", dpa, dpb, wa, wb, xin, norm_w[l][None], dx, ts,
                                                      carried=last)
        if last is not None:
            reduce_at_owner("last", pending, from_chips)

    small_w = [norm_w, b_forget, mem_norm_w, out_norm_w, final_norm_w]
    small_m = [m_norm_w, m_b_forget, m_mem_norm_w, m_out_norm_w, m_final_norm_w]
    small_v = [v_norm_w, v_b_forget, v_mem_norm_w, v_out_norm_w, v_final_norm_w]
    rows2 = lambda a: a.reshape(-1, a.shape[-1])
    partials = [jnp.concatenate(g_norm, axis=0), jnp.concatenate(g_b, axis=0), jnp.concatenate(g_memnorm, axis=0),
                jnp.concatenate(g_outnorm, axis=0), g_final, loss_v]
    sums, updates = _small_update("small_update", partials, [rows2(a) for a in small_w] + [None],
                                  [rows2(a) for a in small_m] + [None], [rows2(a) for a in small_v] + [None])
    small_grads = [g.reshape(a.shape) for g, a in zip(sums, small_w)]
    loss = sums[-1][0, 0]
    small_delta, small_m2, small_v2 = ([updates[3 * k + t].reshape(a.shape) for k, a in enumerate(small_w)]
                                       for t in range(3))
    big_grads, big_delta, big_m2, big_v2 = [], [], [], []
    for k, (nm, w_, m_, v_) in enumerate(zip(("w_in", "w_mem_kv", "w_out"), big, (m_w_in, m_w_mem_kv, m_w_out),
                                             (v_w_in, v_w_mem_kv, v_w_out))):
        if w_.shape[2] % LANES:
            g_full = jnp.stack([jnp.concatenate([jnp.where(core == 0, go, gt), jnp.where(core == 0, gt, go)], axis=0)
                                for go, gt in zip(g_own[k], g_other[k])])
            w_p, g_p, m_p, v_p = (_column_major_rows(a) for a in (w_, g_full, m_, v_))
            outs = _adamw(f"adamw_{nm}", w_p, g_p, m_p, v_p, _tile_of(w_p.shape[0], 600, 1))
            outs = [_from_column_major_rows(o, w_.shape) for o in (g_p, *outs)]
        else:
            outs = _adamw_sharded(f"adamw_{nm}", w_, m_, v_, g_own[k], g_other[k], cvec,
                                  _tile_of(w_.shape[1] // 2, 256, 8))
        for lst, o in zip((big_grads, big_delta, big_m2, big_v2), outs):
            lst.append(o)

    def order(sm, bg):
        return [sm[0], bg[0], sm[1], sm[2], bg[1], sm[3], bg[2], sm[4]]

    return (loss, dx[None], *order(small_grads, big_grads), *order(small_delta, big_delta),
            *order(small_m2, big_m2), *order(small_v2, big_v2))
```

```python
import functools

import jax
import jax.numpy as jnp
from jax import lax
from jax.experimental import pallas as pl
from jax.experimental.pallas import tpu as pltpu

F32 = jnp.float32
BF16 = jnp.bfloat16

HEAD_DIM = 64
SB_WIDTH = 512
FOX_WIDTH = 512
FOX_HEADS = 8
MEM_WIDTH = 256
MIX_WIDTH = SB_WIDTH + FOX_WIDTH + MEM_WIDTH
TOTAL_HEADS = MIX_WIDTH // HEAD_DIM
IN_WIDTH = 3 * SB_WIDTH + 3 * FOX_WIDTH + FOX_HEADS + MEM_WIDTH + MIX_WIDTH
LANES = 128
QKV_WIDTH = 3 * SB_WIDTH + 3 * FOX_WIDTH
PA = QKV_WIDTH + MEM_WIDTH
PB = LANES + MIX_WIDTH
EPS = 1e-6
SCALE = HEAD_DIM ** -0.5
TILE = 256
SB_GROUP = 4
SB_LANES = SB_GROUP * HEAD_DIM
FOX_GROUP = 4
FOX_LANES = FOX_GROUP * HEAD_DIM
NEG_INF = float("-inf")
MASKED = -1e30

ADAM_LR = 0.001
ADAM_B1 = 0.9
ADAM_B2 = 0.999
ADAM_EPS = 1e-08
ADAM_WD = 0.01
ADAM_STEP = 10

N_CHIPS = 4
N_DEV = 8
VMEM_LIMIT = 48 * 1024 * 1024
MESH = pl.DeviceIdType.MESH


def _params(*sem):
    return pltpu.CompilerParams(dimension_semantics=tuple(sem), vmem_limit_bytes=VMEM_LIMIT)


def _dot(a, b):
    return jnp.dot(a, b, preferred_element_type=F32)


def _dot_nt(a, b):
    return lax.dot_general(a, b, (((1,), (1,)), ((), ())), preferred_element_type=F32)


def _dot_tn(a, b):
    return lax.dot_general(a, b, (((0,), (0,)), ((), ())), preferred_element_type=F32)


def _split2(x):
    hi = x.astype(BF16)
    lo = (x - hi.astype(F32)).astype(BF16)
    return hi, lo


def _split3(x):
    hi = x.astype(BF16)
    r = x - hi.astype(F32)
    mid = r.astype(BF16)
    lo = (r - mid.astype(F32)).astype(BF16)
    return hi, mid, lo


def _sum_l2(x, u):
    hi, lo = _split2(x)
    return _dot(hi, u) + _dot(lo, u)


def _sum_l3(x, u):
    hi, mid, lo = _split3(x)
    return _dot(hi, u) + _dot(mid, u) + _dot(lo, u)


def _sum_r3(u, x):
    hi, mid, lo = _split3(x)
    return _dot(u, hi) + _dot(u, mid) + _dot(u, lo)


def _softplus(z):
    return jnp.maximum(z, 0.0) + jnp.log1p(jnp.exp(-jnp.abs(z)))


def _tri(n, pred):
    r = lax.broadcasted_iota(jnp.int32, (n, n), 0)
    c = lax.broadcasted_iota(jnp.int32, (n, n), 1)
    return jnp.where(pred(r, c), 1.0, 0.0).astype(BF16)


def _rows(ref, j, n=TILE):
    return pl.ds(pl.multiple_of(j * n, n), n)


def _mm(name, a, b, mode, tm, tn, out_dtype, res=None, a_lead=(), b_lead=()):
    a2, b2 = a.shape[len(a_lead):], b.shape[len(b_lead):]
    if mode == "tn":
        k, m = a2
    else:
        m, k = a2
    n = b2[0] if mode == "nt" else b2[1]
    assert m % tm == 0 and n % tn == 0, (name, m, tm, n, tn)
    na, nb = (None,) * len(a_lead), (None,) * len(b_lead)
    if mode == "tn":
        a_spec = pl.BlockSpec(na + (k, tm), lambda j, i: a_lead + (0, i))
    else:
        a_spec = pl.BlockSpec(na + (tm, k), lambda j, i: a_lead + (i, 0))
    if mode == "nt":
        b_spec = pl.BlockSpec(nb + (tn, k), lambda j, i: b_lead + (j, 0))
    else:
        b_spec = pl.BlockSpec(nb + (k, tn), lambda j, i: b_lead + (0, j))
    o_spec = pl.BlockSpec((tm, tn), lambda j, i: (i, j))
    dot = {"nn": _dot, "nt": _dot_nt, "tn": _dot_tn}[mode]

    def body(a_ref, b_ref, *rest):
        o_ref = rest[-1]
        acc = dot(a_ref[...].astype(BF16), b_ref[...].astype(BF16))
        if res is not None:
            acc = acc + rest[0][...]
        o_ref[...] = acc.astype(o_ref.dtype)

    args, specs = [a, b], [a_spec, b_spec]
    if res is not None:
        args.append(res)
        specs.append(o_spec)
    return pl.pallas_call(
        body, name=name, grid=(n // tn, m // tm), in_specs=specs, out_specs=o_spec,
        out_shape=jax.ShapeDtypeStruct((m, n), out_dtype),
        compiler_params=_params("parallel", "parallel"),
    )(*args)


def _rms_fwd(name, x, g, ts):
    s, d = x.shape

    def body(x_ref, g_ref, o_ref):
        xf = x_ref[...]
        r = lax.rsqrt(jnp.mean(xf * xf, axis=1, keepdims=True) + EPS)
        o_ref[...] = (xf * r * g_ref[...]).astype(BF16)

    return pl.pallas_call(
        body, name=name, grid=(s // ts,),
        in_specs=[pl.BlockSpec((ts, d), lambda i: (i, 0)), pl.BlockSpec((1, d), lambda i: (0, 0))],
        out_specs=pl.BlockSpec((ts, d), lambda i: (i, 0)),
        out_shape=jax.ShapeDtypeStruct((s, d), BF16),
        compiler_params=_params("parallel"),
    )(x, g)


def _inproj_bwd(name, dpa, dpb, wa, wb, x, g, dres, ts, carried=None):
    s, d = x.shape

    def body(dpa_ref, dpb_ref, wa_ref, wb_ref, x_ref, g_ref, dres_ref, dx_ref, dxb_ref, dg_ref):
        @pl.when(pl.program_id(1) == 0)
        def _():
            dg_ref[...] = jnp.zeros_like(dg_ref)

        dhf = _dot_nt(dpa_ref[...], wa_ref[...]) + _dot_nt(dpb_ref[...], wb_ref[...])
        xf = x_ref[...]
        r = lax.rsqrt(jnp.mean(xf * xf, axis=1, keepdims=True) + EPS)
        xh = xf * r
        dg_ref[...] += jnp.sum(dhf * xh, axis=0, keepdims=True)
        dxh = dhf * g_ref[...]
        m = jnp.mean(dxh * xh, axis=1, keepdims=True)
        dx = r * (dxh - xh * m) + dres_ref[...]
        dx_ref[...] = dx
        dxb_ref[...] = dx.astype(BF16)

    row = lambda w: pl.BlockSpec((ts, w), lambda p, i: (i, 0))
    whole = lambda a: pl.BlockSpec(a.shape, lambda p, i: (0, 0))
    outs = _pair_grid_call(
        name, body, s // ts,
        in_specs=[row(dpa.shape[1]), row(dpb.shape[1]), whole(wa), whole(wb), row(d), whole(g), row(d)],
        out_specs=[row(d), row(d), pl.BlockSpec((1, d), lambda p, i: (0, 0))],
        out_shape=[jax.ShapeDtypeStruct((s, d), F32), jax.ShapeDtypeStruct((s, d), BF16),
                   jax.ShapeDtypeStruct((1, d), F32)],
        scratch=[], args=(dpa, dpb, wa, wb, x, g, dres), carried=carried, groups=1)
    return outs[0], outs[1], outs[2], outs[3:]


def _rms_wgrad(name, x, dh):
    m_, d = x.shape

    def body(x_ref, dh_ref, dg_ref):
        xf = x_ref[...]
        r = lax.rsqrt(jnp.mean(xf * xf, axis=1, keepdims=True) + EPS)
        dg_ref[...] = jnp.sum(dh_ref[...] * xf * r, axis=0, keepdims=True)

    return pl.pallas_call(
        body, name=name, out_shape=jax.ShapeDtypeStruct((1, d), F32),
    )(x, dh)


def _final_loss(name, x, g, target, ts):
    s, d = x.shape

    def body(x_ref, g_ref, t_ref, loss_ref, dx_ref, dxb_ref, dg_ref):
        @pl.when(pl.program_id(0) == 0)
        def _():
            dg_ref[...] = jnp.zeros_like(dg_ref)
            loss_ref[...] = jnp.zeros_like(loss_ref)

        xf = x_ref[...]
        gw = g_ref[...]
        r = lax.rsqrt(jnp.mean(xf * xf, axis=1, keepdims=True) + EPS)
        xh = xf * r
        e = xh * gw - t_ref[...]
        part = 0.5 * jnp.sum(jnp.mean(e * e, axis=1, keepdims=True), axis=0, keepdims=True)
        loss_ref[...] += jnp.broadcast_to(part, loss_ref.shape)
        dy = e * (1.0 / d)
        dg_ref[...] += jnp.sum(dy * xh, axis=0, keepdims=True)
        dxh = dy * gw
        m = jnp.mean(dxh * xh, axis=1, keepdims=True)
        dx = r * (dxh - xh * m)
        dx_ref[...] = dx
        dxb_ref[...] = dx.astype(BF16)

    row = pl.BlockSpec((ts, d), lambda i: (i, 0))
    vec = pl.BlockSpec((1, d), lambda i: (0, 0))
    lvec = pl.BlockSpec((1, LANES), lambda i: (0, 0))
    return pl.pallas_call(
        body, name=name, grid=(s // ts,), in_specs=[row, vec, row], out_specs=[lvec, row, row, vec],
        out_shape=[jax.ShapeDtypeStruct((1, LANES), F32), jax.ShapeDtypeStruct((s, d), F32),
                   jax.ShapeDtypeStruct((s, d), BF16), jax.ShapeDtypeStruct((1, d), F32)],
        compiler_params=_params("arbitrary"),
    )(x, g, target)


def _gate_fwd(name, pb, bpad, fl_block):
    s = pb.shape[0]
    nb = s // TILE

    def body(fl_ref, b_ref, ccol_ref, crow_ref, carry):
        @pl.when(pl.program_id(0) == 0)
        def _():
            carry[...] = jnp.zeros_like(carry)

        u = fl_ref[...] + b_ref[...]
        lf = jnp.minimum(u, 0.0) - jnp.log1p(jnp.exp(-jnp.abs(u)))
        lower = _tri(TILE, lambda r, c: c <= r)
        c = _sum_r3(lower, lf) + carry[0:1, :]
        ccol_ref[...] = c
        crow_ref[0] = c.T[0:8, :]
        carry[...] = jnp.broadcast_to(c[TILE - 1:TILE, :], carry.shape)

    return pl.pallas_call(
        body, name=name, grid=(nb,),
        in_specs=[pl.BlockSpec((TILE, LANES), lambda i: (i, fl_block)), pl.BlockSpec((1, LANES), lambda i: (0, 0))],
        out_specs=[pl.BlockSpec((TILE, LANES), lambda i: (i, 0)), pl.BlockSpec((1, 8, TILE), lambda i: (i, 0, 0))],
        out_shape=[jax.ShapeDtypeStruct((s, LANES), F32), jax.ShapeDtypeStruct((nb, 8, TILE), F32)],
        scratch_shapes=[pltpu.VMEM((8, LANES), F32)],
        compiler_params=_params("arbitrary"),
    )(pb, bpad)


def _gate_bwd(name, pb, bpad, colsum, fl_block, dpb):
    s = pb.shape[0]
    nb = s // TILE

    def body(fl_ref, b_ref, cs_ref, dpb_ref, dl_ref, db_ref, carry):
        @pl.when(pl.program_id(0) == 0)
        def _():
            carry[...] = jnp.zeros_like(carry)
            db_ref[...] = jnp.zeros_like(db_ref)

        upper = _tri(TILE, lambda r, c: r >= c)
        rsum = _sum_l3(cs_ref[0], upper) + carry[:, 0:1]
        carry[...] = jnp.broadcast_to(rsum[:, 0:1], carry.shape)
        full = jnp.concatenate([rsum, jnp.zeros((LANES - 8, TILE), F32)], axis=0)
        dlf = -full.T
        u = fl_ref[...] + b_ref[...]
        dlogit = dlf * (1.0 - jax.nn.sigmoid(u))
        dl_ref[...] = dlogit.astype(BF16)
        db_ref[...] += jnp.sum(dlogit, axis=0, keepdims=True)

    logits_block = pl.BlockSpec((TILE, LANES), lambda i: (nb - 1 - i, fl_block))
    return pl.pallas_call(
        body, name=name, grid=(nb,),
        in_specs=[logits_block, pl.BlockSpec((1, LANES), lambda i: (0, 0)),
                  pl.BlockSpec((1, 8, TILE), lambda i: (nb - 1 - i, 0, 0)), pl.BlockSpec(memory_space=pl.ANY)],
        out_specs=[logits_block, pl.BlockSpec((1, LANES), lambda i: (0, 0))],
        out_shape=[jax.ShapeDtypeStruct(dpb.shape, BF16), jax.ShapeDtypeStruct((1, LANES), F32)],
        scratch_shapes=[pltpu.VMEM((8, LANES), F32)], input_output_aliases={3: 0},
        compiler_params=_params("arbitrary"),
    )(pb, bpad, colsum, dpb)


def _head_slices(hh):
    return slice(HEAD_DIM * hh, HEAD_DIM * (hh + 1))


def _scaled_q(q_ref, sl, scale=SCALE):
    return (q_ref[:, sl].astype(F32) * scale).astype(BF16)


def _neg_abs(x):
    sign = jnp.uint32(0x80000000)
    return lax.bitcast_convert_type(lax.bitcast_convert_type(x, jnp.uint32) | sign, F32)


def _sb_tile(qn, kj, carry, strict, u_after, diag):
    nz = _dot_nt(qn, kj)
    lf = jnp.minimum(nz, 0.0) - jnp.log(1.0 + jnp.exp(_neg_abs(nz)))
    lsig = lf - nz
    if diag:
        lf = jnp.where(strict, lf, 0.0)
    sx = _dot(lf.astype(BF16), u_after)
    a = jnp.exp(lsig + sx + carry)
    if diag:
        a = jnp.where(strict, a, 0.0)
    return lsig, a, carry + sx[:, 0:1] + lf[:, 0:1]


def _pair_grid_call(name, body, nb, in_specs, out_specs, out_shape, scratch, args, carried=None, groups=4):
    if carried is None:
        return pl.pallas_call(
            body, name=name, grid=(groups, nb), in_specs=in_specs, out_specs=out_specs, out_shape=out_shape,
            scratch_shapes=scratch, compiler_params=_params("arbitrary", "arbitrary"),
        )(*args)
    n_in, n_out, n_ex = len(in_specs), len(out_specs), carried.n

    def body_with_copies(*refs):
        own_in, ex_in = refs[:n_in], refs[n_in:n_in + n_ex]
        own_out = refs[n_in + n_ex:n_in + n_ex + n_out]
        ex_out = refs[n_in + n_ex + n_out:n_in + 2 * n_ex + n_out]
        own_scratch, sems = refs[n_in + 2 * n_ex + n_out:-2], refs[-2:]
        parts = (ex_in, ex_out, sems[0], sems[1])
        p, i = pl.program_id(0), pl.program_id(1)
        pl.when(jnp.logical_and(p == 0, i == 0))(lambda: carried.begin(*parts))
        if carried.relay is not None:
            pl.when(jnp.logical_and(p == groups - 1, i == max(nb - 2, 0)))(lambda: carried.relay(*parts))
        body(*own_in, *own_out, *own_scratch)
        pl.when(jnp.logical_and(p == groups - 1, i == nb - 1))(lambda: carried.finish(*parts))

    return pl.pallas_call(
        body_with_copies, name=name, grid=(groups, nb), in_specs=list(in_specs) + [HBM_SPEC] * n_ex,
        out_specs=list(out_specs) + [HBM_SPEC] * n_ex, out_shape=list(out_shape) + carried.out_shapes,
        scratch_shapes=list(scratch) + _dma_sems(carried.n_sems),
        compiler_params=_params("arbitrary", "arbitrary"),
    )(*args, *carried.inputs)


def _sb_fwd(name, pa, col0, carried=None):
    s = pa.shape[0]
    nb = s // TILE
    cb = col0 // SB_LANES
    kb = SB_WIDTH // SB_LANES

    def body(q_ref, k_ref, v_ref, o_ref, lsig_s, lf_s):
        i = pl.program_id(1)
        r = lax.broadcasted_iota(jnp.int32, (TILE, TILE), 0)
        c = lax.broadcasted_iota(jnp.int32, (TILE, TILE), 1)
        strict = c < r
        u_after = _tri(TILE, lambda rr, cc: rr > cc)
        qs = [_scaled_q(q_ref, _head_slices(hh), -SCALE) for hh in range(SB_GROUP)]

        def neg_z(j):
            kblk = k_ref[_rows(k_ref, j), :]
            return [_dot_nt(qs[hh], kblk[:, _head_slices(hh)]) for hh in range(SB_GROUP)]

        def scores(nzs, slot, diag):
            for hh, nz in enumerate(nzs):
                lf = jnp.minimum(nz, 0.0) - jnp.log(1.0 + jnp.exp(_neg_abs(nz)))
                lsig = lf - nz
                if diag:
                    lf = jnp.where(strict, lf, 0.0)
                    lsig = jnp.where(strict, lsig, MASKED)
                lsig_s[slot, hh] = lsig
                lf_s[slot, hh] = lf.astype(BF16)

        def weigh(j, slot, state):
            vblk = v_ref[_rows(v_ref, j), :]
            new = []
            for hh in range(SB_GROUP):
                carry, acc = state[hh]
                lfb = lf_s[slot, hh]
                sx = _dot(lfb, u_after)
                a = jnp.exp(lsig_s[slot, hh] + sx + carry)
                new.append((carry + sx[:, 0:1] + lfb[:, 0:1].astype(F32),
                            acc + _dot(a.astype(BF16), vblk[:, _head_slices(hh)])))
            return tuple(new)

        def step(t, state):
            state = weigh(i - t + 1, (t - 1) % 2, state)
            scores(neg_z(i - t), t % 2, False)
            return state

        zero = (jnp.zeros((TILE, 1), F32), jnp.zeros((TILE, HEAD_DIM), F32))
        scores(neg_z(i), 0, True)
        state = lax.fori_loop(1, i + 1, step, (zero,) * SB_GROUP)
        state = weigh(0, i % 2, state)
        o_ref[...] = jnp.concatenate([st[1] for st in state], axis=1)

    outs = _pair_grid_call(
        name, body, nb,
        in_specs=[pl.BlockSpec((TILE, SB_LANES), lambda p, i: (i, cb + p)),
                  pl.BlockSpec((s, SB_LANES), lambda p, i: (0, cb + kb + p)),
                  pl.BlockSpec((s, SB_LANES), lambda p, i: (0, cb + 2 * kb + p))],
        out_specs=[pl.BlockSpec((TILE, SB_LANES), lambda p, i: (i, p))],
        out_shape=[jax.ShapeDtypeStruct((s, SB_WIDTH), F32)],
        scratch=[pltpu.VMEM((2, SB_GROUP, TILE, TILE), F32), pltpu.VMEM((2, SB_GROUP, TILE, TILE), BF16)],
        args=(pa, pa, pa), carried=carried, groups=kb)
    return outs[0], outs[1:]


def _sb_bwd(name, pa, col0, dout, dcol0, carried=None):
    s = pa.shape[0]
    nb = s // TILE
    cb = col0 // SB_LANES
    kb = SB_WIDTH // SB_LANES
    db = dcol0 // SB_LANES

    def body(q_ref, k_ref, v_ref, do_ref, dq_ref, dk_ref, dv_ref, dk_acc, dv_acc, dpan, span, gsum, lsig_s, lf_s):
        i = pl.program_id(1)

        @pl.when(i == 0)
        def _():
            dk_acc[...] = jnp.zeros_like(dk_acc)
            dv_acc[...] = jnp.zeros_like(dv_acc)

        r = lax.broadcasted_iota(jnp.int32, (TILE, TILE), 0)
        c = lax.broadcasted_iota(jnp.int32, (TILE, TILE), 1)
        strict = c < r
        u_after = _tri(TILE, lambda rr, cc: rr > cc)
        u_before = _tri(TILE, lambda rr, cc: rr < cc)
        qs = [_scaled_q(q_ref, _head_slices(hh), -SCALE) for hh in range(SB_GROUP)]
        dos = [do_ref[:, _head_slices(hh)].astype(BF16) for hh in range(SB_GROUP)]
        dots = [do_ref[:, _head_slices(hh)].T.astype(BF16) for hh in range(SB_GROUP)]
        qts = [q.astype(F32).T.astype(BF16) for q in qs]

        def scores(j, slot, diag):
            kblk = k_ref[_rows(k_ref, j), :]
            for hh in range(SB_GROUP):
                nz = _dot_nt(qs[hh], kblk[:, _head_slices(hh)])
                lf = jnp.minimum(nz, 0.0) - jnp.log(1.0 + jnp.exp(_neg_abs(nz)))
                lsig = lf - nz
                if diag:
                    lf = jnp.where(strict, lf, 0.0)
                    lsig = jnp.where(strict, lsig, MASKED)
                lsig_s[slot, hh] = lsig
                lf_s[slot, hh] = lf.astype(BF16)

        def grads(j, slot, carries):
            vblk = v_ref[_rows(v_ref, j), :]
            new = []
            for hh in range(SB_GROUP):
                lfb = lf_s[slot, hh]
                lsig = lsig_s[slot, hh]
                sx = _dot(lfb, u_after)
                a = jnp.exp(lsig + sx + carries[hh])
                g = a * _dot_nt(dos[hh], vblk[:, _head_slices(hh)])
                sig = jnp.exp(lsig)
                inside = _dot(g.astype(BF16), u_before)
                dpan[hh, j] = sig * (inside + g) - g
                span[hh, j] = sig
                gsum[hh, j] = inside[:, TILE - 1:TILE] + g[:, TILE - 1:TILE]
                dv_acc[hh, j] += _dot(dots[hh], a.astype(BF16))
                new.append(carries[hh] + sx[:, 0:1] + lfb[:, 0:1].astype(F32))
            return tuple(new)

        def step1(t, carries):
            carries = grads(i - t + 1, (t - 1) % 2, carries)
            scores(i - t, t % 2, False)
            return carries

        zero1 = jnp.zeros((TILE, 1), F32)
        scores(i, 0, True)
        carries = lax.fori_loop(1, i + 1, step1, (zero1,) * SB_GROUP)
        grads(0, i % 2, carries)

        def pass2(j, state):
            kblk = k_ref[_rows(k_ref, j), :]
            new = []
            for hh in range(SB_GROUP):
                before, ndq = state[hh]
                ndzb = (dpan[hh, j] + span[hh, j] * before).astype(BF16)
                dk_acc[hh, j] += _dot(qts[hh], ndzb)
                new.append((before + gsum[hh, j], ndq + _dot(ndzb, kblk[:, _head_slices(hh)])))
            return tuple(new)

        zero2 = (zero1, jnp.zeros((TILE, HEAD_DIM), F32))
        state = lax.fori_loop(0, i + 1, pass2, (zero2,) * SB_GROUP)
        dq_ref[...] = jnp.concatenate([st[1] * -SCALE for st in state], axis=1).astype(BF16)

        @pl.when(i == nb - 1)
        def _():
            for acc, ref in ((dk_acc, dk_ref), (dv_acc, dv_ref)):
                for j in range(nb):
                    ref[j * TILE:(j + 1) * TILE, :] = jnp.concatenate(
                        [acc[hh, j].T for hh in range(SB_GROUP)], axis=1).astype(BF16)

    qspec = pl.BlockSpec((TILE, SB_LANES), lambda p, i: (i, p))
    kvspec = pl.BlockSpec((s, SB_LANES), lambda p, i: (0, p))
    out = jax.ShapeDtypeStruct((s, SB_WIDTH), BF16)
    outs = _pair_grid_call(
        name, body, nb,
        in_specs=[pl.BlockSpec((TILE, SB_LANES), lambda p, i: (i, cb + p)),
                  pl.BlockSpec((s, SB_LANES), lambda p, i: (0, cb + kb + p)),
                  pl.BlockSpec((s, SB_LANES), lambda p, i: (0, cb + 2 * kb + p)),
                  pl.BlockSpec((TILE, SB_LANES), lambda p, i: (i, db + p))],
        out_specs=[qspec, kvspec, kvspec], out_shape=[out, out, out],
        scratch=[pltpu.VMEM((SB_GROUP, nb, HEAD_DIM, TILE), F32), pltpu.VMEM((SB_GROUP, nb, HEAD_DIM, TILE), F32),
                 pltpu.VMEM((SB_GROUP, nb, TILE, TILE), F32), pltpu.VMEM((SB_GROUP, nb, TILE, TILE), F32),
                 pltpu.VMEM((SB_GROUP, nb, TILE, 1), F32),
                 pltpu.VMEM((2, SB_GROUP, TILE, TILE), F32), pltpu.VMEM((2, SB_GROUP, TILE, TILE), BF16)],
        args=(pa, pa, pa, dout), carried=carried, groups=kb)
    return outs[:3], outs[3:]


def _fox_scores(q, kj, cq, crj, causal, diag):
    sc = _dot_nt(q, kj) + (cq - crj)
    if diag:
        sc = jnp.where(causal, sc, NEG_INF)
    return sc


def _fox_fwd(name, pa, col0, ccol4, crow4, carried=None):
    s = pa.shape[0]
    nb = s // TILE
    cb = col0 // FOX_LANES
    kb = FOX_WIDTH // FOX_LANES

    def body(q_ref, k_ref, v_ref, cc_ref, cr_ref, o_ref, lse_ref, sc_s):
        i = pl.program_id(1)
        r = lax.broadcasted_iota(jnp.int32, (TILE, TILE), 0)
        c = lax.broadcasted_iota(jnp.int32, (TILE, TILE), 1)
        causal = c <= r
        qs = [_scaled_q(q_ref, _head_slices(hh)) for hh in range(FOX_GROUP)]
        cqs = [cc_ref[:, HEAD_DIM * hh:HEAD_DIM * hh + 1] for hh in range(FOX_GROUP)]

        def logits(j, slot, diag):
            kblk = k_ref[_rows(k_ref, j), :]
            tops = []
            for hh in range(FOX_GROUP):
                sc = _fox_scores(qs[hh], kblk[:, _head_slices(hh)], cqs[hh], cr_ref[j, hh:hh + 1, :], causal, diag)
                sc_s[slot, hh] = sc
                tops.append(jnp.max(sc, axis=1, keepdims=True))
            return tuple(tops)

        def update(j, slot, tops, state):
            vblk = v_ref[_rows(v_ref, j), :]
            new = []
            for hh in range(FOX_GROUP):
                m, l, acc = state[hh]
                m2 = jnp.maximum(m, tops[hh])
                alpha = jnp.exp(m - m2)
                p = jnp.exp(sc_s[slot, hh] - m2)
                new.append((m2, l * alpha + jnp.sum(p, axis=1, keepdims=True),
                            acc * alpha + _dot(p.astype(BF16), vblk[:, _head_slices(hh)])))
            return tuple(new)

        def step(t, both):
            tops, state = both
            state = update(i - t + 1, (t - 1) % 2, tops, state)
            return logits(i - t, t % 2, False), state

        zero = (jnp.full((TILE, 1), NEG_INF, F32), jnp.zeros((TILE, 1), F32), jnp.zeros((TILE, HEAD_DIM), F32))
        tops, state = lax.fori_loop(1, i + 1, step, (logits(i, 0, True), (zero,) * FOX_GROUP))
        state = update(0, i % 2, tops, state)
        o_ref[...] = jnp.concatenate([st[2] / st[1] for st in state], axis=1)
        lse_ref[...] = jnp.concatenate(
            [jnp.broadcast_to(st[0] + jnp.log(st[1]), (TILE, HEAD_DIM)) for st in state], axis=1)

    outs = _pair_grid_call(
        name, body, nb,
        in_specs=[pl.BlockSpec((TILE, FOX_LANES), lambda p, i: (i, cb + p)),
                  pl.BlockSpec((s, FOX_LANES), lambda p, i: (0, cb + kb + p)),
                  pl.BlockSpec((s, FOX_LANES), lambda p, i: (0, cb + 2 * kb + p)),
                  pl.BlockSpec((None, TILE, FOX_LANES), lambda p, i: (p, i, 0)),
                  pl.BlockSpec((None, nb, 8, TILE), lambda p, i: (p, 0, 0, 0))],
        out_specs=[pl.BlockSpec((TILE, FOX_LANES), lambda p, i: (i, p)),
                   pl.BlockSpec((None, TILE, FOX_LANES), lambda p, i: (p, i, 0))],
        out_shape=[jax.ShapeDtypeStruct((s, FOX_WIDTH), F32), jax.ShapeDtypeStruct((kb, s, FOX_LANES), F32)],
        scratch=[pltpu.VMEM((2, FOX_GROUP, TILE, TILE), F32)],
        args=(pa, pa, pa, ccol4, crow4), carried=carried, groups=kb)
    return outs[0], outs[1], outs[2:]


def _fox_bwd(name, pa, col0, ccol4, crow4, out, lse, dout, dcol0):
    s = pa.shape[0]
    nb = s // TILE
    cb = col0 // FOX_LANES
    kb = FOX_WIDTH // FOX_LANES
    db = dcol0 // FOX_LANES

    def body(q_ref, k_ref, v_ref, cc_ref, cr_ref, o_ref, lse_ref, do_ref,
             dq_ref, dk_ref, dv_ref, cs_ref, dk_acc, dv_acc, p_s, ds_s):
        i = pl.program_id(1)

        @pl.when(i == 0)
        def _():
            dk_acc[...] = jnp.zeros_like(dk_acc)
            dv_acc[...] = jnp.zeros_like(dv_acc)
            cs_ref[...] = jnp.zeros_like(cs_ref)

        r = lax.broadcasted_iota(jnp.int32, (TILE, TILE), 0)
        c = lax.broadcasted_iota(jnp.int32, (TILE, TILE), 1)
        causal = c <= r
        qs = [_scaled_q(q_ref, _head_slices(hh)) for hh in range(FOX_GROUP)]
        cqs = [cc_ref[:, HEAD_DIM * hh:HEAD_DIM * hh + 1] for hh in range(FOX_GROUP)]
        lses = [lse_ref[:, HEAD_DIM * hh:HEAD_DIM * hh + 1] for hh in range(FOX_GROUP)]
        dofs = [do_ref[:, _head_slices(hh)] for hh in range(FOX_GROUP)]
        dos = [d_.astype(BF16) for d_ in dofs]
        dots = [d_.T.astype(BF16) for d_ in dofs]
        qts = [q.astype(F32).T.astype(BF16) for q in qs]
        deltas = [jnp.sum(dofs[hh] * o_ref[:, _head_slices(hh)], axis=1, keepdims=True) for hh in range(FOX_GROUP)]

        def probs(j, slot, rowsums, diag):
            kblk = k_ref[_rows(k_ref, j), :]
            vblk = v_ref[_rows(v_ref, j), :]
            new = []
            for hh in range(FOX_GROUP):
                sl = _head_slices(hh)
                sc = _fox_scores(qs[hh], kblk[:, sl], cqs[hh], cr_ref[j, hh:hh + 1, :], causal, diag)
                p = jnp.exp(sc - lses[hh])
                ds = p * (_dot_nt(dos[hh], vblk[:, sl]) - deltas[hh])
                p_s[slot, hh] = p.astype(BF16)
                ds_s[slot, hh] = ds.astype(BF16)
                cs_ref[j, hh:hh + 1, :] += jnp.sum(ds, axis=0, keepdims=True)
                new.append(rowsums[hh] + jnp.sum(ds, axis=1, keepdims=True))
            return tuple(new)

        def accumulate(j, slot, dqs):
            kblk = k_ref[_rows(k_ref, j), :]
            new = []
            for hh in range(FOX_GROUP):
                dsb = ds_s[slot, hh]
                dv_acc[hh, j] += _dot(dots[hh], p_s[slot, hh])
                dk_acc[hh, j] += _dot(qts[hh], dsb)
                new.append(dqs[hh] + _dot(dsb, kblk[:, _head_slices(hh)]))
            return tuple(new)

        def step(t, both):
            rowsums, dqs = both
            dqs = accumulate(i - t + 1, (t - 1) % 2, dqs)
            return probs(i - t, t % 2, rowsums, False), dqs

        zero1 = jnp.zeros((TILE, 1), F32)
        zero64 = jnp.zeros((TILE, HEAD_DIM), F32)
        rowsums, dqs = lax.fori_loop(1, i + 1, step,
                                     (probs(i, 0, (zero1,) * FOX_GROUP, True), (zero64,) * FOX_GROUP))
        dqs = accumulate(0, i % 2, dqs)
        for hh in range(FOX_GROUP):
            cs_ref[i, hh:hh + 1, :] -= jnp.broadcast_to(rowsums[hh], (TILE, LANES)).T[0:1, :]
        dq_ref[...] = jnp.concatenate([dq * SCALE for dq in dqs], axis=1).astype(BF16)

        @pl.when(i == nb - 1)
        def _():
            for acc, ref in ((dk_acc, dk_ref), (dv_acc, dv_ref)):
                for j in range(nb):
                    ref[j * TILE:(j + 1) * TILE, :] = jnp.concatenate(
                        [acc[hh, j].T for hh in range(FOX_GROUP)], axis=1).astype(BF16)

    qspec = pl.BlockSpec((TILE, FOX_LANES), lambda p, i: (i, p))
    kvspec = pl.BlockSpec((s, FOX_LANES), lambda p, i: (0, p))
    o3 = jax.ShapeDtypeStruct((s, FOX_WIDTH), BF16)
    return pl.pallas_call(
        body, name=name, grid=(kb, nb),
        in_specs=[pl.BlockSpec((TILE, FOX_LANES), lambda p, i: (i, cb + p)),
                  pl.BlockSpec((s, FOX_LANES), lambda p, i: (0, cb + kb + p)),
                  pl.BlockSpec((s, FOX_LANES), lambda p, i: (0, cb + 2 * kb + p)),
                  pl.BlockSpec((None, TILE, FOX_LANES), lambda p, i: (p, i, 0)),
                  pl.BlockSpec((None, nb, 8, TILE), lambda p, i: (p, 0, 0, 0)),
                  qspec,
                  pl.BlockSpec((None, TILE, FOX_LANES), lambda p, i: (p, i, 0)),
                  pl.BlockSpec((TILE, FOX_LANES), lambda p, i: (i, db + p))],
        out_specs=[qspec, kvspec, kvspec, pl.BlockSpec((None, nb, 8, TILE), lambda p, i: (p, 0, 0, 0))],
        out_shape=[o3, o3, o3, jax.ShapeDtypeStruct((kb, nb, 8, TILE), F32)],
        scratch_shapes=[pltpu.VMEM((FOX_GROUP, nb, HEAD_DIM, TILE), F32), pltpu.VMEM((FOX_GROUP, nb, HEAD_DIM, TILE), F32),
                        pltpu.VMEM((2, FOX_GROUP, TILE, TILE), BF16), pltpu.VMEM((2, FOX_GROUP, TILE, TILE), BF16)],
        compiler_params=_params("arbitrary", "arbitrary"),
    )(pa, pa, pa, ccol4, crow4, out, lse, dout)


def _mem_fwd(name, pa, mkv):
    s = pa.shape[0]
    ml = mkv.shape[0]
    nb = s // TILE
    cb = QKV_WIDTH // LANES

    def body(q_ref, k_ref, v_ref, o_ref, lse_ref):
        outs, lses = [], []
        for hh in range(2):
            sl = _head_slices(hh)
            sc = _dot_nt(_scaled_q(q_ref, sl), k_ref[:, sl])
            m = jnp.max(sc, axis=1, keepdims=True)
            p = jnp.exp(sc - m)
            l = jnp.sum(p, axis=1, keepdims=True)
            outs.append(_dot(p.astype(BF16), v_ref[:, sl]) / l)
            lses.append(jnp.broadcast_to(m + jnp.log(l), (TILE, HEAD_DIM)))
        o_ref[...] = jnp.concatenate(outs, axis=1)
        lse_ref[...] = jnp.concatenate(lses, axis=1)

    return pl.pallas_call(
        body, name=name, grid=(2, nb),
        in_specs=[pl.BlockSpec((TILE, LANES), lambda p, i: (i, cb + p)),
                  pl.BlockSpec((ml, LANES), lambda p, i: (0, p)),
                  pl.BlockSpec((ml, LANES), lambda p, i: (0, 2 + p))],
        out_specs=[pl.BlockSpec((TILE, LANES), lambda p, i: (i, p)),
                   pl.BlockSpec((None, TILE, LANES), lambda p, i: (p, i, 0))],
        out_shape=[jax.ShapeDtypeStruct((s, MEM_WIDTH), F32), jax.ShapeDtypeStruct((2, s, LANES), F32)],
        compiler_params=_params("parallel", "parallel"),
    )(pa, mkv, mkv)


def _mem_bwd(name, pa, mkv, out, lse, dout, dcol0):
    s = pa.shape[0]
    ml = mkv.shape[0]
    nb = s // TILE
    cb = QKV_WIDTH // LANES
    db = dcol0 // LANES

    def body(q_ref, k_ref, v_ref, o_ref, lse_ref, do_ref, dq_ref, dk_ref, dv_ref, dk_acc, dv_acc):
        i = pl.program_id(1)

        @pl.when(i == 0)
        def _():
            dk_acc[...] = jnp.zeros_like(dk_acc)
            dv_acc[...] = jnp.zeros_like(dv_acc)

        dqs = []
        for hh in range(2):
            sl = _head_slices(hh)
            q = _scaled_q(q_ref, sl)
            kh = k_ref[:, sl]
            dof = do_ref[:, sl]
            do = dof.astype(BF16)
            delta = jnp.sum(dof * o_ref[:, sl], axis=1, keepdims=True)
            p = jnp.exp(_dot_nt(q, kh) - lse_ref[:, HEAD_DIM * hh:HEAD_DIM * hh + 1])
            ds = (p * (_dot_nt(do, v_ref[:, sl]) - delta)).astype(BF16)
            dv_acc[hh] += _dot_tn(p.astype(BF16), do)
            dk_acc[hh] += _dot_tn(ds, q)
            dqs.append(_dot(ds, kh) * SCALE)
        dq_ref[...] = jnp.concatenate(dqs, axis=1).astype(BF16)

        @pl.when(i == nb - 1)
        def _():
            dk_ref[...] = jnp.concatenate([dk_acc[0], dk_acc[1]], axis=1).astype(BF16)
            dv_ref[...] = jnp.concatenate([dv_acc[0], dv_acc[1]], axis=1).astype(BF16)

    qspec = pl.BlockSpec((TILE, LANES), lambda p, i: (i, p))
    kvspec = pl.BlockSpec((ml, LANES), lambda p, i: (0, p))
    okv = jax.ShapeDtypeStruct((ml, MEM_WIDTH), BF16)
    return pl.pallas_call(
        body, name=name, grid=(2, nb),
        in_specs=[pl.BlockSpec((TILE, LANES), lambda p, i: (i, cb + p)),
                  pl.BlockSpec((ml, LANES), lambda p, i: (0, p)),
                  pl.BlockSpec((ml, LANES), lambda p, i: (0, 2 + p)),
                  qspec,
                  pl.BlockSpec((None, TILE, LANES), lambda p, i: (p, i, 0)),
                  pl.BlockSpec((TILE, LANES), lambda p, i: (i, db + p))],
        out_specs=[qspec, kvspec, kvspec],
        out_shape=[jax.ShapeDtypeStruct((s, MEM_WIDTH), BF16), okv, okv],
        scratch_shapes=[pltpu.VMEM((2, ml, HEAD_DIM), F32), pltpu.VMEM((2, ml, HEAD_DIM), F32)],
        compiler_params=_params("arbitrary", "arbitrary"),
    )(pa, mkv, mkv, out, lse, dout)


def _head_maps():
    col = jnp.arange(MIX_WIDTH)[:, None] // HEAD_DIM
    g = (col == jnp.arange(LANES)[None, :]).astype(BF16)
    return g, g.T


def _normed_heads(osb_ref, ofx_ref, om_ref, g_ref, gt_ref):
    y = jnp.concatenate([osb_ref[...], ofx_ref[...], om_ref[...]], axis=1)
    msq = _sum_l2(y * y, g_ref[...]) * (1.0 / HEAD_DIM)
    rf = _sum_l3(lax.rsqrt(msq + EPS), gt_ref[...])
    return y * rf, rf


def _out_fwd(name, o_sb, o_fx, o_m, pb, ow, x, w_out, ts):
    s, d = x.shape
    g, gt = _head_maps()

    def body(osb_ref, ofx_ref, om_ref, gate_ref, ow_ref, x_ref, w_ref, g_ref, gt_ref, xo_ref, y2_ref):
        yh, _ = _normed_heads(osb_ref, ofx_ref, om_ref, g_ref, gt_ref)
        gate = gate_ref[...]
        y2 = (yh * ow_ref[...] * (gate * jax.nn.sigmoid(gate))).astype(BF16)
        y2_ref[...] = y2
        xo_ref[...] = x_ref[...] + _dot(y2, w_ref[...])

    return pl.pallas_call(
        body, name=name, grid=(s // ts,),
        in_specs=[_row_spec(ts, SB_WIDTH), _row_spec(ts, FOX_WIDTH), _row_spec(ts, MEM_WIDTH),
                  _row_spec(ts, MIX_WIDTH), _const_spec((1, MIX_WIDTH)), _row_spec(ts, d),
                  _const_spec((MIX_WIDTH, d)),
                  _const_spec((MIX_WIDTH, LANES)), _const_spec((LANES, MIX_WIDTH))],
        out_specs=[_row_spec(ts, d), _row_spec(ts, MIX_WIDTH)],
        out_shape=[jax.ShapeDtypeStruct((s, d), F32), jax.ShapeDtypeStruct((s, MIX_WIDTH), BF16)],
        compiler_params=_params("parallel"),
    )(o_sb, o_fx, o_m, pb, ow, x, w_out, g, gt)


def _row_spec(ts, w):
    return pl.BlockSpec((ts, w), lambda i: (i, 0))


def _const_spec(shape):
    return pl.BlockSpec(shape, lambda i: (0,) * len(shape))


def _out_bwd(name, dxb, o_sb, o_fx, o_m, pb, ow, w_out, ts):
    s, d = dxb.shape
    g, gt = _head_maps()

    def body(dx_ref, osb_ref, ofx_ref, om_ref, gate_ref, ow_ref, w_ref, g_ref, gt_ref, dy_ref, dgate_ref, dow_ref):
        @pl.when(pl.program_id(0) == 0)
        def _():
            dow_ref[...] = jnp.zeros_like(dow_ref)

        dy2 = _dot_nt(dx_ref[...], w_ref[...])
        yh, rf = _normed_heads(osb_ref, ofx_ref, om_ref, g_ref, gt_ref)
        gate = gate_ref[...]
        sig = jax.nn.sigmoid(gate)
        ow_v = ow_ref[...]
        dgate_ref[...] = (dy2 * (yh * ow_v) * (sig * (1.0 + gate * (1.0 - sig)))).astype(BF16)
        dn = dy2 * (gate * sig)
        dow_ref[...] += jnp.sum(dn * yh, axis=0, keepdims=True)
        dyh = dn * ow_v
        t = _sum_l2(dyh * yh, g_ref[...]) * (1.0 / HEAD_DIM)
        dy_ref[...] = rf * (dyh - yh * _sum_l3(t, gt_ref[...]))

    return pl.pallas_call(
        body, name=name, grid=(s // ts,),
        in_specs=[_row_spec(ts, d), _row_spec(ts, SB_WIDTH), _row_spec(ts, FOX_WIDTH), _row_spec(ts, MEM_WIDTH),
                  _row_spec(ts, MIX_WIDTH), _const_spec((1, MIX_WIDTH)),
                  _const_spec((MIX_WIDTH, d)),
                  _const_spec((MIX_WIDTH, LANES)), _const_spec((LANES, MIX_WIDTH))],
        out_specs=[_row_spec(ts, MIX_WIDTH), _row_spec(ts, MIX_WIDTH), _const_spec((1, MIX_WIDTH))],
        out_shape=[jax.ShapeDtypeStruct((s, MIX_WIDTH), F32), jax.ShapeDtypeStruct((s, PB), BF16),
                   jax.ShapeDtypeStruct((1, MIX_WIDTH), F32)],
        compiler_params=_params("arbitrary"),
    )(dxb, o_sb, o_fx, o_m, pb, ow, w_out, g, gt)


def _adamw(name, w, g, m, v, tr):
    def body(w_ref, g_ref, m_ref, v_ref, d_ref, m2_ref, v2_ref):
        gv = g_ref[...]
        m2 = ADAM_B1 * m_ref[...] + (1.0 - ADAM_B1) * gv
        v2 = ADAM_B2 * v_ref[...] + (1.0 - ADAM_B2) * (gv * gv)
        m_hat = m2 / (1.0 - ADAM_B1 ** ADAM_STEP)
        v_hat = v2 / (1.0 - ADAM_B2 ** ADAM_STEP)
        d_ref[...] = -ADAM_LR * (m_hat / (jnp.sqrt(v_hat) + ADAM_EPS) + ADAM_WD * w_ref[...])
        m2_ref[...] = m2
        v2_ref[...] = v2

    rest = w.shape[1:]
    spec = pl.BlockSpec((tr,) + rest, lambda i: (i,) + (0,) * len(rest))
    shp = jax.ShapeDtypeStruct(w.shape, F32)
    return pl.pallas_call(
        body, name=name, grid=(w.shape[0] // tr,), in_specs=[spec] * 4, out_specs=[spec] * 3, out_shape=[shp] * 3,
        compiler_params=_params("parallel"),
    )(w, g, m, v)


def _adamw_sharded(name, w, m, v, g_own, g_other, cvec, tr):
    depth, rows, cols = w.shape
    nt = rows // 2 // tr

    def body(c_ref, w_ref, m_ref, v_ref, *rest):
        g_refs, (g_ref, d_ref, m2_ref, v2_ref) = rest[:2 * depth], rest[2 * depth:]
        layer, mine = pl.program_id(0), pl.program_id(1) == c_ref[0]
        gv = None
        for lt in range(depth):
            cand = jnp.where(mine, g_refs[lt][...], g_refs[depth + lt][...])
            gv = cand if gv is None else jnp.where(layer == lt, cand, gv)
        m2 = ADAM_B1 * m_ref[...] + (1.0 - ADAM_B1) * gv
        v2 = ADAM_B2 * v_ref[...] + (1.0 - ADAM_B2) * (gv * gv)
        m_hat = m2 / (1.0 - ADAM_B1 ** ADAM_STEP)
        v_hat = v2 / (1.0 - ADAM_B2 ** ADAM_STEP)
        g_ref[...] = gv
        d_ref[...] = -ADAM_LR * (m_hat / (jnp.sqrt(v_hat) + ADAM_EPS) + ADAM_WD * w_ref[...])
        m2_ref[...] = m2
        v2_ref[...] = v2

    def g_map(lt, own):
        def index(l, hf, i, c_ref):
            use = jnp.logical_and(l == lt, (hf == c_ref[0]) == own)
            return jnp.where(use, i, 0), 0
        return index

    full = pl.BlockSpec((None, tr, cols), lambda l, hf, i, c_ref: (l, hf * nt + i, 0))
    g_specs = [pl.BlockSpec((tr, cols), g_map(lt, own)) for own in (True, False) for lt in range(depth)]
    shp = jax.ShapeDtypeStruct((depth, rows, cols), F32)
    return pl.pallas_call(
        body, name=name,
        grid_spec=pltpu.PrefetchScalarGridSpec(
            num_scalar_prefetch=1, grid=(depth, 2, nt), in_specs=[full] * 3 + g_specs, out_specs=[full] * 4),
        out_shape=[shp] * 4,
        compiler_params=_params("arbitrary", "arbitrary", "arbitrary"),
    )(cvec, w, m, v, *g_own, *g_other)


HBM_SPEC = pl.BlockSpec(memory_space=pltpu.HBM)


def _place():
    x, y, c = lax.axis_index("x"), lax.axis_index("y"), lax.axis_index("c")
    chips = [(1 - x, y), (x, 1 - y), (1 - x, 1 - y)]
    return x, y, c, chips


def _remote(src, dst, send_sems, recv_sems, k, to):
    return pltpu.make_async_remote_copy(src_ref=src, dst_ref=dst, send_sem=send_sems.at[k], recv_sem=recv_sems.at[k],
                                        device_id=to, device_id_type=MESH)


def _half_rows(n_rows, cc):
    rh = n_rows // 2
    return pl.ds(pl.multiple_of(cc * rh, 16), rh)


def _dma_sems(n):
    return [pltpu.SemaphoreType.DMA((n,)), pltpu.SemaphoreType.DMA((n,))]


class _Exchange:
    def __init__(self, inputs, out_shapes, n_sems, begin, relay, finish):
        self.inputs, self.out_shapes, self.n_sems = list(inputs), list(out_shapes), n_sems
        self.begin, self.relay, self.finish = begin, relay, finish

    @property
    def n(self):
        return len(self.inputs)

    def split(self, refs):
        return refs[:self.n], refs[self.n:2 * self.n], refs[2 * self.n], refs[2 * self.n + 1]


def _run_exchange(name, ex):
    def body(*refs):
        parts = ex.split(refs)
        for phase in (ex.begin, ex.relay, ex.finish):
            if phase is not None:
                phase(*parts)

    return pl.pallas_call(
        body, name=name, in_specs=[HBM_SPEC] * ex.n, out_specs=[HBM_SPEC] * ex.n, out_shape=ex.out_shapes,
        scratch_shapes=_dma_sems(ex.n_sems),
    )(*ex.inputs)


def _gather_exchange(shards):
    def ici(in_refs, out_refs, send_sems, recv_sems):
        x, y, c, chips = _place()
        return [_remote(in_ref.at[_half_rows(in_ref.shape[0], c)], out_ref.at[2 * x + y, _half_rows(in_ref.shape[0], c)],
                        send_sems, recv_sems, 6 * a + j, (cx, cy, c))
                for a, (in_ref, out_ref) in enumerate(zip(in_refs, out_refs)) for j, (cx, cy) in enumerate(chips)]

    def d2d(out_refs, send_sems, recv_sems, half_of):
        x, y, c, chips = _place()
        cps = []
        for a, out_ref in enumerate(out_refs):
            for j, (cx, cy) in enumerate(chips):
                piece = out_ref.at[2 * cx + cy, _half_rows(out_ref.shape[1], half_of(c))]
                cps.append(_remote(piece, piece, send_sems, recv_sems, 6 * a + 3 + j, (x, y, 1 - c)))
        return cps

    def begin(in_refs, out_refs, send_sems, recv_sems):
        for cp in ici(in_refs, out_refs, send_sems, recv_sems):
            cp.start()

    def relay(in_refs, out_refs, send_sems, recv_sems):
        x, y, c, chips = _place()
        for a, out_ref in enumerate(out_refs):
            for j, (cx, cy) in enumerate(chips):
                landed = out_ref.at[2 * cx + cy, _half_rows(out_ref.shape[1], c)]
                _remote(landed, landed, send_sems, recv_sems, 6 * a + j, (cx, cy, c)).wait_recv()
        for cp in d2d(out_refs, send_sems, recv_sems, lambda c_: c_):
            cp.start()

    def finish(in_refs, out_refs, send_sems, recv_sems):
        for cp in d2d(out_refs, send_sems, recv_sems, lambda c_: 1 - c_):
            cp.wait_recv()
        for cp in ici(in_refs, out_refs, send_sems, recv_sems) + d2d(out_refs, send_sems, recv_sems, lambda c_: c_):
            cp.wait_send()

    shapes = [jax.ShapeDtypeStruct((N_CHIPS,) + s_.shape, s_.dtype) for s_ in shards]
    return _Exchange(shards, shapes, 6 * len(shards), begin, relay, finish)


def _swap_halves(name, g4s):
    n = len(g4s)

    def body(*refs):
        in_refs, out_refs, (send_sems, recv_sems) = refs[:n], refs[n:2 * n], refs[2 * n:]
        x, y, c, _ = _place()
        cps = [_remote(in_ref.at[:, _half_rows(in_ref.shape[1], 1 - c), :], out_ref, send_sems, recv_sems, a, (x, y, 1 - c))
               for a, (in_ref, out_ref) in enumerate(zip(in_refs, out_refs))]
        for cp in cps:
            cp.start()
        for cp in cps:
            cp.wait()

    return pl.pallas_call(
        body, name=name, in_specs=[HBM_SPEC] * n, out_specs=[HBM_SPEC] * n,
        out_shape=[jax.ShapeDtypeStruct((g.shape[0], g.shape[1] // 2, g.shape[2]), g.dtype) for g in g4s],
        scratch_shapes=_dma_sems(n),
    )(*g4s)


def _add_half(name, g4, r1, cvec, tr):
    n, r, w = g4.shape
    rh = r // 2
    nblk = rh // tr

    def body(c_ref, a_ref, b_ref, o_ref):
        o_ref[...] = (a_ref[...].astype(F32) + b_ref[...].astype(F32)).astype(BF16)

    return pl.pallas_call(
        body, name=name,
        grid_spec=pltpu.PrefetchScalarGridSpec(
            num_scalar_prefetch=1, grid=(n, nblk),
            in_specs=[pl.BlockSpec((None, tr, w), lambda k, i, c_ref: (k, c_ref[0] * nblk + i, 0)),
                      pl.BlockSpec((None, tr, w), lambda k, i, c_ref: (k, i, 0))],
            out_specs=pl.BlockSpec((None, tr, w), lambda k, i, c_ref: (k, i, 0))),
        out_shape=jax.ShapeDtypeStruct((n, rh, w), BF16),
        compiler_params=_params("parallel", "parallel"),
    )(cvec, g4, r1)


def _scatter_exchange(h4s):
    def sends(in_refs, out_refs, send_sems, recv_sems):
        x, y, c, chips = _place()
        return [_remote(in_ref.at[2 * cx + cy], out_ref.at[j], send_sems, recv_sems, 3 * a + j, (cx, cy, c))
                for a, (in_ref, out_ref) in enumerate(zip(in_refs, out_refs)) for j, (cx, cy) in enumerate(chips)]

    def begin(*parts):
        for cp in sends(*parts):
            cp.start()

    def finish(in_refs, out_refs, send_sems, recv_sems):
        x, y, c, chips = _place()
        for a, out_ref in enumerate(out_refs):
            for j, (cx, cy) in enumerate(chips):
                got = out_ref.at[j]
                _remote(got, got, send_sems, recv_sems, 3 * a + j, (cx, cy, c)).wait_recv()
        for cp in sends(in_refs, out_refs, send_sems, recv_sems):
            cp.wait_send()

    shapes = [jax.ShapeDtypeStruct((3,) + h.shape[1:], h.dtype) for h in h4s]
    return _Exchange(h4s, shapes, 3 * len(h4s), begin, None, finish)


def _sum_chips(name, h4, r3, mvec, tr):
    _, rh, w = h4.shape

    def body(m_ref, a_ref, b_ref, c_ref, d_ref, o_ref):
        o_ref[...] = ((a_ref[...].astype(F32) + b_ref[...].astype(F32)) + c_ref[...].astype(F32)) + d_ref[...].astype(F32)

    specs = [pl.BlockSpec((None, tr, w), lambda i, m_ref: (m_ref[0], i, 0))]
    specs += [pl.BlockSpec((None, tr, w), functools.partial(lambda k, i, m_ref: (k, i, 0), k)) for k in range(3)]
    return pl.pallas_call(
        body, name=name,
        grid_spec=pltpu.PrefetchScalarGridSpec(
            num_scalar_prefetch=1, grid=(rh // tr,), in_specs=specs,
            out_specs=pl.BlockSpec((tr, w), lambda i, m_ref: (i, 0))),
        out_shape=jax.ShapeDtypeStruct((rh, w), F32),
        compiler_params=_params("parallel"),
    )(mvec, h4, r3, r3, r3)


def _swap_reduced(name, ghs):
    n = len(ghs)

    def body(*refs):
        in_refs, out_refs, (send_sems, recv_sems) = refs[:n], refs[n:2 * n], refs[2 * n:]
        x, y, c, _ = _place()
        cps = [_remote(in_ref, out_ref, send_sems, recv_sems, a, (x, y, 1 - c))
               for a, (in_ref, out_ref) in enumerate(zip(in_refs, out_refs))]
        for cp in cps:
            cp.start()
        for cp in cps:
            cp.wait()

    return pl.pallas_call(
        body, name=name, in_specs=[HBM_SPEC] * n, out_specs=[HBM_SPEC] * n,
        out_shape=[jax.ShapeDtypeStruct(g.shape, g.dtype) for g in ghs],
        scratch_shapes=_dma_sems(n),
    )(*ghs)


def _small_update(name, partials, weights, moments1, moments2):
    n = len(partials)
    width = max(p.shape[1] for p in partials)
    starts, at = [], 0
    for p in partials:
        starts.append(at)
        at += p.shape[0]
    rows = -(-at // 8) * 8
    has_w = [w is not None for w in weights]
    n_w = sum(has_w)

    def body(*refs):
        p_refs = refs[:n]
        w_refs, m_refs, v_refs = refs[n:n + n_w], refs[n + n_w:n + 2 * n_w], refs[n + 2 * n_w:n + 3 * n_w]
        outs = refs[n + 3 * n_w:-4]
        g_refs, upd_refs = outs[:n], outs[n:]
        vec, buf, send_sems, recv_sems = refs[-4:]
        x, y, c, _ = _place()
        me = 4 * x + 2 * y + c
        vec[...] = jnp.zeros_like(vec)
        for p_ref, r0 in zip(p_refs, starts):
            vec[r0:r0 + p_ref.shape[0], 0:p_ref.shape[1]] = p_ref[...]
        buf[me] = vec[...]
        flips = [(fx, fy, fc) for fx in (0, 1) for fy in (0, 1) for fc in (0, 1)][1:]
        peers = [(x + fx - 2 * x * fx, y + fy - 2 * y * fy, c + fc - 2 * c * fc) for fx, fy, fc in flips]
        sends = [_remote(vec, buf.at[me], send_sems, recv_sems, k, peer) for k, peer in enumerate(peers)]
        for cp in sends:
            cp.start()
        for k, (px, py, pc) in enumerate(peers):
            got = buf.at[4 * px + 2 * py + pc]
            _remote(got, got, send_sems, recv_sems, k, (px, py, pc)).wait_recv()
        for cp in sends:
            cp.wait_send()
        total = buf[0]
        for dev in range(1, N_DEV):
            total = total + buf[dev]
        k = 0
        for a in range(n):
            r, w = g_refs[a].shape
            g = total[starts[a]:starts[a] + r, 0:w]
            g_refs[a][...] = g
            if has_w[a]:
                m2 = ADAM_B1 * m_refs[k][...] + (1.0 - ADAM_B1) * g
                v2 = ADAM_B2 * v_refs[k][...] + (1.0 - ADAM_B2) * (g * g)
                m_hat = m2 / (1.0 - ADAM_B1 ** ADAM_STEP)
                v_hat = v2 / (1.0 - ADAM_B2 ** ADAM_STEP)
                upd_refs[3 * k][...] = -ADAM_LR * (m_hat / (jnp.sqrt(v_hat) + ADAM_EPS) + ADAM_WD * w_refs[k][...])
                upd_refs[3 * k + 1][...] = m2
                upd_refs[3 * k + 2][...] = v2
                k += 1

    ws = [w for w in weights if w is not None]
    g_shapes = [jax.ShapeDtypeStruct(p.shape if w is None else w.shape, F32) for p, w in zip(partials, weights)]
    u_shapes = [jax.ShapeDtypeStruct(w.shape, F32) for w in ws for _ in range(3)]
    vm = pl.BlockSpec(memory_space=pltpu.VMEM)
    n_args = n + 3 * n_w
    outs = pl.pallas_call(
        body, name=name, in_specs=[vm] * n_args, out_specs=[vm] * (n + 3 * n_w), out_shape=g_shapes + u_shapes,
        scratch_shapes=[pltpu.VMEM((rows, width), F32), pltpu.VMEM((N_DEV, rows, width), F32),
                        pltpu.SemaphoreType.DMA((7,)), pltpu.SemaphoreType.DMA((7,))],
    )(*partials, *ws, *[m for m in moments1 if m is not None], *[v for v in moments2 if v is not None])
    return outs[:n], outs[n:]


GATE_COL = 3 * SB_WIDTH + 3 * FOX_WIDTH + FOX_HEADS + MEM_WIDTH
FL_COL = QKV_WIDTH


GROUP_A_COLS = [(0, QKV_WIDTH), (FL_COL + FOX_HEADS, MEM_WIDTH)]
GROUP_B_COLS = [(GATE_COL, MIX_WIDTH), (FL_COL, FOX_HEADS)]


def _group_from_shards(shard_of, cw, spans, pad):
    parts = []
    for lo, width in spans:
        hi = lo + width
        for j in range(N_CHIPS):
            a, b = max(lo, j * cw), min(hi, (j + 1) * cw)
            if a < b:
                parts.append(shard_of(j)[:, a - j * cw:b - j * cw])
    if pad:
        parts.append(jnp.zeros((parts[0].shape[0], pad), parts[0].dtype))
    return jnp.concatenate(parts, axis=1)


def _shard_from_groups(ga, gb, j, cw):
    lo, hi = j * cw, (j + 1) * cw
    placed = []
    for grp, spans in ((ga, GROUP_A_COLS), (gb, GROUP_B_COLS)):
        at = 0
        for first, width in spans:
            a, b = max(lo, first), min(hi, first + width)
            if a < b:
                placed.append((a, grp[:, at + a - first:at + b - first]))
            at += width
    return jnp.concatenate([p for _, p in sorted(placed, key=lambda t: t[0])], axis=1)


def _tile_of(n, cap, unit):
    if n <= cap:
        return n
    best = None
    for t in range(unit, cap + 1, unit):
        if n % t == 0:
            best = t
    assert best is not None, (n, cap, unit)
    return best


def _column_major_rows(a):
    dp, r, c = a.shape
    return a.transpose(2, 0, 1).reshape(c, dp, r // LANES, LANES).transpose(0, 2, 1, 3).reshape(-1, 8, LANES)


def _from_column_major_rows(b, shape):
    dp, r, c = shape
    return b.reshape(c, r // LANES, dp, LANES).transpose(0, 2, 1, 3).reshape(c, dp, r).transpose(1, 2, 0)


def _pack_small(parts):
    rows = []
    for p in parts:
        f = p.reshape(-1).astype(F32)
        f = jnp.pad(f, (0, (-f.shape[0]) % LANES))
        rows.append(f.reshape(-1, LANES))
    out = jnp.concatenate(rows, axis=0)
    return jnp.pad(out, ((0, (-out.shape[0]) % 8), (0, 0)))


def _unpack_small(packed, shapes):
    outs, r = [], 0
    for shp in shapes:
        n = 1
        for s_ in shp:
            n *= s_
        nr = -(-n // LANES)
        outs.append(packed[r:r + nr].reshape(-1)[:n].reshape(shp))
        r += nr
    return outs


def kernel(x, mem, norm_w, w_in, b_forget, mem_norm_w, w_mem_kv, out_norm_w, w_out, final_norm_w, loss_target, m_norm_w, m_w_in, m_b_forget, m_mem_norm_w, m_w_mem_kv, m_out_norm_w, m_w_out, m_final_norm_w, v_norm_w, v_w_in, v_b_forget, v_mem_norm_w, v_w_mem_kv, v_out_norm_w, v_w_out, v_final_norm_w):
    xs = x[0]
    mems = mem[0]
    target = loss_target[0]
    s, d = xs.shape
    depth = norm_w.shape[0]
    nb = s // TILE
    ts = _tile_of(s, 256, 8)
    big = (w_in, w_mem_kv, w_out)
    core = lax.axis_index("c")
    chip = 2 * lax.axis_index("x") + lax.axis_index("y")
    cvec = core.astype(jnp.int32).reshape(1)
    mvec = chip.astype(jnp.int32).reshape(1)
    cw = w_in.shape[2]

    own_w = [[a[l].astype(BF16) for a in big] for l in range(depth)]

    def lay_out_in(own, got):
        shard_of = lambda j: jnp.where(chip == j, own, got[j])
        return (_group_from_shards(shard_of, cw, GROUP_A_COLS, 0),
                _group_from_shards(shard_of, cw, GROUP_B_COLS, LANES - FOX_HEADS))

    def lay_out_rows(own, got):
        full = jnp.where(lax.broadcasted_iota(jnp.int32, got.shape, 0) == chip, own[None], got)
        return full.reshape(-1, full.shape[2])

    w_in_groups = [lay_out_in(own_w[0][0], _run_exchange("gather_weights0", _gather_exchange(own_w[0][:1]))[0])]
    layer_w = []

    tm = _tile_of(s, 256, 8)
    fl_block = MIX_WIDTH // LANES

    saved = []
    cur = xs
    for l in range(depth):
        wa, wb = w_in_groups[l]
        h = _rms_fwd(f"rms_fwd{l}", cur, norm_w[l][None], ts)
        pa = _mm(f"inproj_a{l}", h, wa, "nn", tm, _tile_of(PA, 1664, LANES), BF16)
        pb = _mm(f"inproj_b{l}", h, wb, "nn", tm, PB, F32)
        bpad = jnp.pad(b_forget[l], (0, LANES - FOX_HEADS))[None]
        ccol, crow = _gate_fwd(f"gate_fwd{l}", pb, bpad, fl_block)
        fg = FOX_HEADS // FOX_GROUP
        ccol4 = jnp.repeat(ccol[:, :FOX_HEADS].reshape(s, fg, FOX_GROUP).transpose(1, 0, 2), HEAD_DIM, axis=2)
        crow4 = jnp.pad(crow.reshape(nb, fg, FOX_GROUP, TILE).transpose(1, 0, 2, 3),
                        ((0, 0), (0, 0), (0, 8 - FOX_GROUP), (0, 0)))
        more = l + 1 < depth
        riding = own_w[l][1:] + (own_w[l + 1][:1] if more else [])
        o_sb, got = _sb_fwd(f"sb_fwd{l}", pa, 0, carried=_gather_exchange(riding))
        wkv, wout = lay_out_rows(own_w[l][1], got[0]), lay_out_rows(own_w[l][2], got[1])
        layer_w.append((wa, wb, wkv, wout))
        if more:
            w_in_groups.append(lay_out_in(own_w[l + 1][0], got[2]))
        o_fx, lse_fx, _ = _fox_fwd(f"fox_fwd{l}", pa, 3 * SB_WIDTH, ccol4, crow4)
        mn = _rms_fwd(f"mem_rms{l}", mems, mem_norm_w[l][None], mems.shape[0])
        mkv = _mm(f"mem_kv{l}", mn, wkv, "nn", mems.shape[0], 2 * MEM_WIDTH, BF16)
        o_m, lse_m = _mem_fwd(f"mem_fwd{l}", pa, mkv)
        nxt, y2 = _out_fwd(f"out_fwd{l}", o_sb, o_fx, o_m, pb, out_norm_w[l][None], cur, wout, ts)
        saved.append((cur, h, pa, pb, bpad, ccol4, crow4, o_sb, o_fx, lse_fx, mn, mkv, o_m, lse_m, y2))
        cur = nxt

    loss_v, dx, dxb, g_final = _final_loss("final_loss", cur, final_norm_w[None], target, ts)

    g_norm, g_b, g_memnorm, g_outnorm = [None] * depth, [None] * depth, [None] * depth, [None] * depth
    g_wa, g_wb, g_wkv, g_wout = [None] * depth, [None] * depth, [None] * depth, [None] * depth
    g_own = [[None] * depth for _ in big]
    g_other = [[None] * depth for _ in big]

    def within_chip(tag, jobs):
        got = _swap_halves(f"grad_swap_halves{tag}", [g for _, _, g, _ in jobs])
        return [(lr, k, _add_half(f"grad_add_half{lr}_{k}", g, r_, cvec, t_), t_) for (lr, k, g, t_), r_ in zip(jobs, got)]

    def reduce_at_owner(tag, jobs, from_chips):
        halves = [_sum_chips(f"grad_sum_chips{lr}_{k}", h_, r_, mvec, t_) for (lr, k, h_, t_), r_ in zip(jobs, from_chips)]
        others = _swap_reduced(f"grad_swap_reduced{tag}", halves)
        for (lr, k, _, _), mine, other in zip(jobs, halves, others):
            g_own[k][lr], g_other[k][lr] = mine, other

    def job(lr, k, g4):
        return lr, k, g4, _tile_of(g4.shape[1] // 2, 256, 16)

    pending = []
    for l in reversed(range(depth)):
        xin, h, pa, pb, bpad, ccol4, crow4, o_sb, o_fx, lse_fx, mn, mkv, o_m, lse_m, y2 = saved[l]
        wa, wb, wkv, wout = layer_w[l]
        dy, dgate, g_outnorm[l] = _out_bwd(f"out_bwd{l}", dxb, o_sb, o_fx, o_m, pb, out_norm_w[l][None], wout, ts)
        g_wout[l] = _mm(f"dw_out{l}", y2, dxb, "tn", _tile_of(MIX_WIDTH, 640, LANES), d, F32)
        dq_m, dk_m, dv_m = _mem_bwd(f"mem_bwd{l}", pa, mkv, o_m, lse_m, dy, SB_WIDTH + FOX_WIDTH)
        dmkv = jnp.concatenate([dk_m, dv_m], axis=1)
        g_wkv[l] = _mm(f"dw_kv{l}", mn, dmkv, "tn", d, 2 * MEM_WIDTH, F32)
        dmn = _mm(f"dmem{l}", dmkv, wkv, "nt", mems.shape[0], d, F32)
        g_memnorm[l] = _rms_wgrad(f"mem_norm_grad{l}", mems, dmn)
        pending += within_chip(f"{l}s", [job(l, 1, g_wkv[l].reshape(N_CHIPS, -1, g_wkv[l].shape[1])),
                                         job(l, 2, g_wout[l].reshape(N_CHIPS, -1, d))])
        (dq_sb, dk_sb, dv_sb), from_chips = _sb_bwd(f"sb_bwd{l}", pa, 0, dy, 0,
                                                   carried=_scatter_exchange([j[2] for j in pending]))
        reduce_at_owner(f"{l}s", pending, from_chips)
        dq_fx, dk_fx, dv_fx, cs4 = _fox_bwd(f"fox_bwd{l}", pa, 3 * SB_WIDTH, ccol4, crow4, o_fx, lse_fx, dy, SB_WIDTH)
        colsum = cs4[:, :, :FOX_GROUP, :].transpose(1, 0, 2, 3).reshape(nb, 8, TILE)
        dpb, g_b[l] = _gate_bwd(f"gate_bwd{l}", pb, bpad, colsum, fl_block, dgate)
        dpa = jnp.concatenate([dq_sb, dk_sb, dv_sb, dq_fx, dk_fx, dv_fx, dq_m], axis=1)
        tw = _tile_of(d, 512, LANES)
        g_wa[l] = _mm(f"dw_in_a{l}", h, dpa, "tn", tw, _tile_of(PA, 1664, LANES), BF16)
        g_wb[l] = _mm(f"dw_in_b{l}", h, dpb, "tn", tw, PB, BF16)
        g4_in = jnp.stack([_shard_from_groups(g_wa[l], g_wb[l], j, cw) for j in range(N_CHIPS)])
        pending = within_chip(f"{l}", [job(l, 0, g4_in)])
        last = _scatter_exchange([j[2] for j in pending]) if l == 0 else None
        dx, dxb, g_norm[l], from_chips = _inproj_bwd(f"inproj_bwd{l}", dpa, dpb, wa, wb, xin, norm_w[l][None], dx, ts,
                                                      carried=last)
        if last is not None:
            reduce_at_owner("last", pending, from_chips)

    small_w = [norm_w, b_forget, mem_norm_w, out_norm_w, final_norm_w]
    small_m = [m_norm_w, m_b_forget, m_mem_norm_w, m_out_norm_w, m_final_norm_w]
    small_v = [v_norm_w, v_b_forget, v_mem_norm_w, v_out_norm_w, v_final_norm_w]
    rows2 = lambda a: a.reshape(-1, a.shape[-1])
    partials = [jnp.concatenate(g_norm, axis=0), jnp.concatenate(g_b, axis=0), jnp.concatenate(g_memnorm, axis=0),
                jnp.concatenate(g_outnorm, axis=0), g_final, loss_v]
    sums, updates = _small_update("small_update", partials, [rows2(a) for a in small_w] + [None],
                                  [rows2(a) for a in small_m] + [None], [rows2(a) for a in small_v] + [None])
    small_grads = [g.reshape(a.shape) for g, a in zip(sums, small_w)]
    loss = sums[-1][0, 0]
    small_delta, small_m2, small_v2 = ([updates[3 * k + t].reshape(a.shape) for k, a in enumerate(small_w)]
                                       for t in range(3))
    big_grads, big_delta, big_m2, big_v2 = [], [], [], []
    for k, (nm, w_, m_, v_) in enumerate(zip(("w_in", "w_mem_kv", "w_out"), big, (m_w_in, m_w_mem_kv, m_w_out),
                                             (v_w_in, v_w_mem_kv, v_w_out))):
        if w_.shape[2] % LANES:
            g_full = jnp.stack([jnp.concatenate([jnp.where(core == 0, go, gt), jnp.where(core == 0, gt, go)], axis=0)
                                for go, gt in zip(g_own[k], g_other[k])])
            w_p, g_p, m_p, v_p = (_column_major_rows(a) for a in (w_, g_full, m_, v_))
            outs = _adamw(f"adamw_{nm}", w_p, g_p, m_p, v_p, _tile_of(w_p.shape[0], 600, 1))
            outs = [_from_column_major_rows(o, w_.shape) for o in (g_p, *outs)]
        else:
            outs = _adamw_sharded(f"adamw_{nm}", w_, m_, v_, g_own[k], g_other[k], cvec,
                                  _tile_of(w_.shape[1] // 2, 256, 8))
        for lst, o in zip((big_grads, big_delta, big_m2, big_v2), outs):
            lst.append(o)

    def order(sm, bg):
        return [sm[0], bg[0], sm[1], sm[2], bg[1], sm[3], bg[2], sm[4]]

    return (loss, dx[None], *order(small_grads, big_grads), *order(small_delta, big_delta),
            *order(small_m2, big_m2), *order(small_v2, big_v2))
```

```python
import functools

import jax
import jax.numpy as jnp
from jax import lax
from jax.experimental import pallas as pl
from jax.experimental.pallas import tpu as pltpu

F32 = jnp.float32
BF16 = jnp.bfloat16

HEAD_DIM = 64
SB_WIDTH = 512
FOX_WIDTH = 512
FOX_HEADS = 8
MEM_WIDTH = 256
MIX_WIDTH = SB_WIDTH + FOX_WIDTH + MEM_WIDTH
TOTAL_HEADS = MIX_WIDTH // HEAD_DIM
IN_WIDTH = 3 * SB_WIDTH + 3 * FOX_WIDTH + FOX_HEADS + MEM_WIDTH + MIX_WIDTH
LANES = 128
QKV_WIDTH = 3 * SB_WIDTH + 3 * FOX_WIDTH
PA = QKV_WIDTH + MEM_WIDTH
PB = LANES + MIX_WIDTH
EPS = 1e-6
SCALE = HEAD_DIM ** -0.5
TILE = 256
SB_GROUP = 4
SB_LANES = SB_GROUP * HEAD_DIM
FOX_GROUP = 4
FOX_LANES = FOX_GROUP * HEAD_DIM
NEG_INF = float("-inf")
MASKED = -1e30

ADAM_LR = 0.001
ADAM_B1 = 0.9
ADAM_B2 = 0.999
ADAM_EPS = 1e-08
ADAM_WD = 0.01
ADAM_STEP = 10

N_CHIPS = 4
N_DEV = 8
VMEM_LIMIT = 48 * 1024 * 1024
MESH = pl.DeviceIdType.MESH


def _params(*sem):
    return pltpu.CompilerParams(dimension_semantics=tuple(sem), vmem_limit_bytes=VMEM_LIMIT)


def _dot(a, b):
    return jnp.dot(a, b, preferred_element_type=F32)


def _dot_nt(a, b):
    return lax.dot_general(a, b, (((1,), (1,)), ((), ())), preferred_element_type=F32)


def _dot_tn(a, b):
    return lax.dot_general(a, b, (((0,), (0,)), ((), ())), preferred_element_type=F32)


def _split2(x):
    hi = x.astype(BF16)
    lo = (x - hi.astype(F32)).astype(BF16)
    return hi, lo


def _split3(x):
    hi = x.astype(BF16)
    r = x - hi.astype(F32)
    mid = r.astype(BF16)
    lo = (r - mid.astype(F32)).astype(BF16)
    return hi, mid, lo


def _sum_l2(x, u):
    hi, lo = _split2(x)
    return _dot(hi, u) + _dot(lo, u)


def _sum_l3(x, u):
    hi, mid, lo = _split3(x)
    return _dot(hi, u) + _dot(mid, u) + _dot(lo, u)


def _sum_r3(u, x):
    hi, mid, lo = _split3(x)
    return _dot(u, hi) + _dot(u, mid) + _dot(u, lo)


def _softplus(z):
    return jnp.maximum(z, 0.0) + jnp.log1p(jnp.exp(-jnp.abs(z)))


def _tri(n, pred):
    r = lax.broadcasted_iota(jnp.int32, (n, n), 0)
    c = lax.broadcasted_iota(jnp.int32, (n, n), 1)
    return jnp.where(pred(r, c), 1.0, 0.0).astype(BF16)


def _rows(ref, j, n=TILE):
    return pl.ds(pl.multiple_of(j * n, n), n)


def _mm(name, a, b, mode, tm, tn, out_dtype, res=None, a_lead=(), b_lead=()):
    a2, b2 = a.shape[len(a_lead):], b.shape[len(b_lead):]
    if mode == "tn":
        k, m = a2
    else:
        m, k = a2
    n = b2[0] if mode == "nt" else b2[1]
    assert m % tm == 0 and n % tn == 0, (name, m, tm, n, tn)
    na, nb = (None,) * len(a_lead), (None,) * len(b_lead)
    if mode == "tn":
        a_spec = pl.BlockSpec(na + (k, tm), lambda j, i: a_lead + (0, i))
    else:
        a_spec = pl.BlockSpec(na + (tm, k), lambda j, i: a_lead + (i, 0))
    if mode == "nt":
        b_spec = pl.BlockSpec(nb + (tn, k), lambda j, i: b_lead + (j, 0))
    else:
        b_spec = pl.BlockSpec(nb + (k, tn), lambda j, i: b_lead + (0, j))
    o_spec = pl.BlockSpec((tm, tn), lambda j, i: (i, j))
    dot = {"nn": _dot, "nt": _dot_nt, "tn": _dot_tn}[mode]

    def body(a_ref, b_ref, *rest):
        o_ref = rest[-1]
        acc = dot(a_ref[...].astype(BF16), b_ref[...].astype(BF16))
        if res is not None:
            acc = acc + rest[0][...]
        o_ref[...] = acc.astype(o_ref.dtype)

    args, specs = [a, b], [a_spec, b_spec]
    if res is not None:
        args.append(res)
        specs.append(o_spec)
    return pl.pallas_call(
        body, name=name, grid=(n // tn, m // tm), in_specs=specs, out_specs=o_spec,
        out_shape=jax.ShapeDtypeStruct((m, n), out_dtype),
        compiler_params=_params("parallel", "parallel"),
    )(*args)


def _rms_fwd(name, x, g, ts):
    s, d = x.shape

    def body(x_ref, g_ref, o_ref):
        xf = x_ref[...]
        r = lax.rsqrt(jnp.mean(xf * xf, axis=1, keepdims=True) + EPS)
        o_ref[...] = (xf * r * g_ref[...]).astype(BF16)

    return pl.pallas_call(
        body, name=name, grid=(s // ts,),
        in_specs=[pl.BlockSpec((ts, d), lambda i: (i, 0)), pl.BlockSpec((1, d), lambda i: (0, 0))],
        out_specs=pl.BlockSpec((ts, d), lambda i: (i, 0)),
        out_shape=jax.ShapeDtypeStruct((s, d), BF16),
        compiler_params=_params("parallel"),
    )(x, g)


def _inproj_bwd(name, dpa, dpb, wa, wb, x, g, dres, ts, carried=None):
    s, d = x.shape

    def body(dpa_ref, dpb_ref, wa_ref, wb_ref, x_ref, g_ref, dres_ref, dx_ref, dxb_ref, dg_ref):
        @pl.when(pl.program_id(1) == 0)
        def _():
            dg_ref[...] = jnp.zeros_like(dg_ref)

        dhf = _dot_nt(dpa_ref[...], wa_ref[...]) + _dot_nt(dpb_ref[...], wb_ref[...])
        xf = x_ref[...]
        r = lax.rsqrt(jnp.mean(xf * xf, axis=1, keepdims=True) + EPS)
        xh = xf * r
        dg_ref[...] += jnp.sum(dhf * xh, axis=0, keepdims=True)
        dxh = dhf * g_ref[...]
        m = jnp.mean(dxh * xh, axis=1, keepdims=True)
        dx = r * (dxh - xh * m) + dres_ref[...]
        dx_ref[...] = dx
        dxb_ref[...] = dx.astype(BF16)

    row = lambda w: pl.BlockSpec((ts, w), lambda p, i: (i, 0))
    whole = lambda a: pl.BlockSpec(a.shape, lambda p, i: (0, 0))
    outs = _pair_grid_call(
        name, body, s // ts,
        in_specs=[row(dpa.shape[1]), row(dpb.shape[1]), whole(wa), whole(wb), row(d), whole(g), row(d)],
        out_specs=[row(d), row(d), pl.BlockSpec((1, d), lambda p, i: (0, 0))],
        out_shape=[jax.ShapeDtypeStruct((s, d), F32), jax.ShapeDtypeStruct((s, d), BF16),
                   jax.ShapeDtypeStruct((1, d), F32)],
        scratch=[], args=(dpa, dpb, wa, wb, x, g, dres), carried=carried, groups=1)
    return outs[0], outs[1], outs[2], outs[3:]


def _rms_wgrad(name, x, dh):
    m_, d = x.shape

    def body(x_ref, dh_ref, dg_ref):
        xf = x_ref[...]
        r = lax.rsqrt(jnp.mean(xf * xf, axis=1, keepdims=True) + EPS)
        dg_ref[...] = jnp.sum(dh_ref[...] * xf * r, axis=0, keepdims=True)

    return pl.pallas_call(
        body, name=name, out_shape=jax.ShapeDtypeStruct((1, d), F32),
    )(x, dh)


def _final_loss(name, x, g, target, ts):
    s, d = x.shape

    def body(x_ref, g_ref, t_ref, loss_ref, dx_ref, dxb_ref, dg_ref):
        @pl.when(pl.program_id(0) == 0)
        def _():
            dg_ref[...] = jnp.zeros_like(dg_ref)
            loss_ref[...] = jnp.zeros_like(loss_ref)

        xf = x_ref[...]
        gw = g_ref[...]
        r = lax.rsqrt(jnp.mean(xf * xf, axis=1, keepdims=True) + EPS)
        xh = xf * r
        e = xh * gw - t_ref[...]
        part = 0.5 * jnp.sum(jnp.mean(e * e, axis=1, keepdims=True), axis=0, keepdims=True)
        loss_ref[...] += jnp.broadcast_to(part, loss_ref.shape)
        dy = e * (1.0 / d)
        dg_ref[...] += jnp.sum(dy * xh, axis=0, keepdims=True)
        dxh = dy * gw
        m = jnp.mean(dxh * xh, axis=1, keepdims=True)
        dx = r * (dxh - xh * m)
        dx_ref[...] = dx
        dxb_ref[...] = dx.astype(BF16)

    row = pl.BlockSpec((ts, d), lambda i: (i, 0))
    vec = pl.BlockSpec((1, d), lambda i: (0, 0))
    lvec = pl.BlockSpec((1, LANES), lambda i: (0, 0))
    return pl.pallas_call(
        body, name=name, grid=(s // ts,), in_specs=[row, vec, row], out_specs=[lvec, row, row, vec],
        out_shape=[jax.ShapeDtypeStruct((1, LANES), F32), jax.ShapeDtypeStruct((s, d), F32),
                   jax.ShapeDtypeStruct((s, d), BF16), jax.ShapeDtypeStruct((1, d), F32)],
        compiler_params=_params("arbitrary"),
    )(x, g, target)


def _gate_fwd(name, pb, bpad, fl_block):
    s = pb.shape[0]
    nb = s // TILE

    def body(fl_ref, b_ref, ccol_ref, crow_ref, carry):
        @pl.when(pl.program_id(0) == 0)
        def _():
            carry[...] = jnp.zeros_like(carry)

        u = fl_ref[...] + b_ref[...]
        lf = jnp.minimum(u, 0.0) - jnp.log1p(jnp.exp(-jnp.abs(u)))
        lower = _tri(TILE, lambda r, c: c <= r)
        c = _sum_r3(lower, lf) + carry[0:1, :]
        ccol_ref[...] = c
        crow_ref[0] = c.T[0:8, :]
        carry[...] = jnp.broadcast_to(c[TILE - 1:TILE, :], carry.shape)

    return pl.pallas_call(
        body, name=name, grid=(nb,),
        in_specs=[pl.BlockSpec((TILE, LANES), lambda i: (i, fl_block)), pl.BlockSpec((1, LANES), lambda i: (0, 0))],
        out_specs=[pl.BlockSpec((TILE, LANES), lambda i: (i, 0)), pl.BlockSpec((1, 8, TILE), lambda i: (i, 0, 0))],
        out_shape=[jax.ShapeDtypeStruct((s, LANES), F32), jax.ShapeDtypeStruct((nb, 8, TILE), F32)],
        scratch_shapes=[pltpu.VMEM((8, LANES), F32)],
        compiler_params=_params("arbitrary"),
    )(pb, bpad)


def _gate_bwd(name, pb, bpad, colsum, fl_block, dpb):
    s = pb.shape[0]
    nb = s // TILE

    def body(fl_ref, b_ref, cs_ref, dpb_ref, dl_ref, db_ref, carry):
        @pl.when(pl.program_id(0) == 0)
        def _():
            carry[...] = jnp.zeros_like(carry)
            db_ref[...] = jnp.zeros_like(db_ref)

        upper = _tri(TILE, lambda r, c: r >= c)
        rsum = _sum_l3(cs_ref[0], upper) + carry[:, 0:1]
        carry[...] = jnp.broadcast_to(rsum[:, 0:1], carry.shape)
        full = jnp.concatenate([rsum, jnp.zeros((LANES - 8, TILE), F32)], axis=0)
        dlf = -full.T
        u = fl_ref[...] + b_ref[...]
        dlogit = dlf * (1.0 - jax.nn.sigmoid(u))
        dl_ref[...] = dlogit.astype(BF16)
        db_ref[...] += jnp.sum(dlogit, axis=0, keepdims=True)

    logits_block = pl.BlockSpec((TILE, LANES), lambda i: (nb - 1 - i, fl_block))
    return pl.pallas_call(
        body, name=name, grid=(nb,),
        in_specs=[logits_block, pl.BlockSpec((1, LANES), lambda i: (0, 0)),
                  pl.BlockSpec((1, 8, TILE), lambda i: (nb - 1 - i, 0, 0)), pl.BlockSpec(memory_space=pl.ANY)],
        out_specs=[logits_block, pl.BlockSpec((1, LANES), lambda i: (0, 0))],
        out_shape=[jax.ShapeDtypeStruct(dpb.shape, BF16), jax.ShapeDtypeStruct((1, LANES), F32)],
        scratch_shapes=[pltpu.VMEM((8, LANES), F32)], input_output_aliases={3: 0},
        compiler_params=_params("arbitrary"),
    )(pb, bpad, colsum, dpb)


def _head_slices(hh):
    return slice(HEAD_DIM * hh, HEAD_DIM * (hh + 1))


def _scaled_q(q_ref, sl, scale=SCALE):
    return (q_ref[:, sl].astype(F32) * scale).astype(BF16)


def _neg_abs(x):
    sign = jnp.uint32(0x80000000)
    return lax.bitcast_convert_type(lax.bitcast_convert_type(x, jnp.uint32) | sign, F32)


def _sb_tile(qn, kj, carry, strict, u_after, diag):
    nz = _dot_nt(qn, kj)
    lf = jnp.minimum(nz, 0.0) - jnp.log(1.0 + jnp.exp(_neg_abs(nz)))
    lsig = lf - nz
    if diag:
        lf = jnp.where(strict, lf, 0.0)
    sx = _dot(lf.astype(BF16), u_after)
    a = jnp.exp(lsig + sx + carry)
    if diag:
        a = jnp.where(strict, a, 0.0)
    return lsig, a, carry + sx[:, 0:1] + lf[:, 0:1]


def _pair_grid_call(name, body, nb, in_specs, out_specs, out_shape, scratch, args, carried=None, groups=4):
    if carried is None:
        return pl.pallas_call(
            body, name=name, grid=(groups, nb), in_specs=in_specs, out_specs=out_specs, out_shape=out_shape,
            scratch_shapes=scratch, compiler_params=_params("arbitrary", "arbitrary"),
        )(*args)
    n_in, n_out, n_ex = len(in_specs), len(out_specs), carried.n

    def body_with_copies(*refs):
        own_in, ex_in = refs[:n_in], refs[n_in:n_in + n_ex]
        own_out = refs[n_in + n_ex:n_in + n_ex + n_out]
        ex_out = refs[n_in + n_ex + n_out:n_in + 2 * n_ex + n_out]
        own_scratch, sems = refs[n_in + 2 * n_ex + n_out:-2], refs[-2:]
        parts = (ex_in, ex_out, sems[0], sems[1])
        p, i = pl.program_id(0), pl.program_id(1)
        pl.when(jnp.logical_and(p == 0, i == 0))(lambda: carried.begin(*parts))
        if carried.relay is not None:
            pl.when(jnp.logical_and(p == groups - 1, i == max(nb - 2, 0)))(lambda: carried.relay(*parts))
        body(*own_in, *own_out, *own_scratch)
        pl.when(jnp.logical_and(p == groups - 1, i == nb - 1))(lambda: carried.finish(*parts))

    return pl.pallas_call(
        body_with_copies, name=name, grid=(groups, nb), in_specs=list(in_specs) + [HBM_SPEC] * n_ex,
        out_specs=list(out_specs) + [HBM_SPEC] * n_ex, out_shape=list(out_shape) + carried.out_shapes,
        scratch_shapes=list(scratch) + _dma_sems(carried.n_sems),
        compiler_params=_params("arbitrary", "arbitrary"),
    )(*args, *carried.inputs)


def _sb_fwd(name, pa, col0, carried=None):
    s = pa.shape[0]
    nb = s // TILE
    cb = col0 // SB_LANES
    kb = SB_WIDTH // SB_LANES

    def body(q_ref, k_ref, v_ref, o_ref, lsig_s, lf_s):
        i = pl.program_id(1)
        r = lax.broadcasted_iota(jnp.int32, (TILE, TILE), 0)
        c = lax.broadcasted_iota(jnp.int32, (TILE, TILE), 1)
        strict = c < r
        u_after = _tri(TILE, lambda rr, cc: rr > cc)
        qs = [_scaled_q(q_ref, _head_slices(hh), -SCALE) for hh in range(SB_GROUP)]

        def neg_z(j):
            kblk = k_ref[_rows(k_ref, j), :]
            return [_dot_nt(qs[hh], kblk[:, _head_slices(hh)]) for hh in range(SB_GROUP)]

        def scores(nzs, slot, diag):
            for hh, nz in enumerate(nzs):
                lf = jnp.minimum(nz, 0.0) - jnp.log(1.0 + jnp.exp(_neg_abs(nz)))
                lsig = lf - nz
                if diag:
                    lf = jnp.where(strict, lf, 0.0)
                    lsig = jnp.where(strict, lsig, MASKED)
                lsig_s[slot, hh] = lsig
                lf_s[slot, hh] = lf.astype(BF16)

        def weigh(j, slot, state):
            vblk = v_ref[_rows(v_ref, j), :]
            new = []
            for hh in range(SB_GROUP):
                carry, acc = state[hh]
                lfb = lf_s[slot, hh]
                sx = _dot(lfb, u_after)
                a = jnp.exp(lsig_s[slot, hh] + sx + carry)
                new.append((carry + sx[:, 0:1] + lfb[:, 0:1].astype(F32),
                            acc + _dot(a.astype(BF16), vblk[:, _head_slices(hh)])))
            return tuple(new)

        def step(t, state):
            state = weigh(i - t + 1, (t - 1) % 2, state)
            scores(neg_z(i - t), t % 2, False)
            return state

        zero = (jnp.zeros((TILE, 1), F32), jnp.zeros((TILE, HEAD_DIM), F32))
        scores(neg_z(i), 0, True)
        state = lax.fori_loop(1, i + 1, step, (zero,) * SB_GROUP)
        state = weigh(0, i % 2, state)
        o_ref[...] = jnp.concatenate([st[1] for st in state], axis=1)

    outs = _pair_grid_call(
        name, body, nb,
        in_specs=[pl.BlockSpec((TILE, SB_LANES), lambda p, i: (i, cb + p)),
                  pl.BlockSpec((s, SB_LANES), lambda p, i: (0, cb + kb + p)),
                  pl.BlockSpec((s, SB_LANES), lambda p, i: (0, cb + 2 * kb + p))],
        out_specs=[pl.BlockSpec((TILE, SB_LANES), lambda p, i: (i, p))],
        out_shape=[jax.ShapeDtypeStruct((s, SB_WIDTH), F32)],
        scratch=[pltpu.VMEM((2, SB_GROUP, TILE, TILE), F32), pltpu.VMEM((2, SB_GROUP, TILE, TILE), BF16)],
        args=(pa, pa, pa), carried=carried, groups=kb)
    return outs[0], outs[1:]


def _sb_bwd(name, pa, col0, dout, dcol0, carried=None):
    s = pa.shape[0]
    nb = s // TILE
    cb = col0 // SB_LANES
    kb = SB_WIDTH // SB_LANES
    db = dcol0 // SB_LANES

    def body(q_ref, k_ref, v_ref, do_ref, dq_ref, dk_ref, dv_ref, dk_acc, dv_acc, dpan, span, gsum, lsig_s, lf_s):
        i = pl.program_id(1)

        @pl.when(i == 0)
        def _():
            dk_acc[...] = jnp.zeros_like(dk_acc)
            dv_acc[...] = jnp.zeros_like(dv_acc)

        r = lax.broadcasted_iota(jnp.int32, (TILE, TILE), 0)
        c = lax.broadcasted_iota(jnp.int32, (TILE, TILE), 1)
        strict = c < r
        u_after = _tri(TILE, lambda rr, cc: rr > cc)
        u_before = _tri(TILE, lambda rr, cc: rr < cc)
        qs = [_scaled_q(q_ref, _head_slices(hh), -SCALE) for hh in range(SB_GROUP)]
        dos = [do_ref[:, _head_slices(hh)].astype(BF16) for hh in range(SB_GROUP)]
        dots = [do_ref[:, _head_slices(hh)].T.astype(BF16) for hh in range(SB_GROUP)]
        qts = [q.astype(F32).T.astype(BF16) for q in qs]

        def scores(j, slot, diag):
            kblk = k_ref[_rows(k_ref, j), :]
            for hh in range(SB_GROUP):
                nz = _dot_nt(qs[hh], kblk[:, _head_slices(hh)])
                lf = jnp.minimum(nz, 0.0) - jnp.log(1.0 + jnp.exp(_neg_abs(nz)))
                lsig = lf - nz
                if diag:
                    lf = jnp.where(strict, lf, 0.0)
                    lsig = jnp.where(strict, lsig, MASKED)
                lsig_s[slot, hh] = lsig
                lf_s[slot, hh] = lf.astype(BF16)

        def grads(j, slot, carries):
            vblk = v_ref[_rows(v_ref, j), :]
            new = []
            for hh in range(SB_GROUP):
                lfb = lf_s[slot, hh]
                lsig = lsig_s[slot, hh]
                sx = _dot(lfb, u_after)
                a = jnp.exp(lsig + sx + carries[hh])
                g = a * _dot_nt(dos[hh], vblk[:, _head_slices(hh)])
                sig = jnp.exp(lsig)
                inside = _dot(g.astype(BF16), u_before)
                dpan[hh, j] = sig * (inside + g) - g
                span[hh, j] = sig
                gsum[hh, j] = inside[:, TILE - 1:TILE] + g[:, TILE - 1:TILE]
                dv_acc[hh, j] += _dot(dots[hh], a.astype(BF16))
                new.append(carries[hh] + sx[:, 0:1] + lfb[:, 0:1].astype(F32))
            return tuple(new)

        def step1(t, carries):
            carries = grads(i - t + 1, (t - 1) % 2, carries)
            scores(i - t, t % 2, False)
            return carries

        zero1 = jnp.zeros((TILE, 1), F32)
        scores(i, 0, True)
        carries = lax.fori_loop(1, i + 1, step1, (zero1,) * SB_GROUP)
        grads(0, i % 2, carries)

        def pass2(j, state):
            kblk = k_ref[_rows(k_ref, j), :]
            new = []
            for hh in range(SB_GROUP):
                before, ndq = state[hh]
                ndzb = (dpan[hh, j] + span[hh, j] * before).astype(BF16)
                dk_acc[hh, j] += _dot(qts[hh], ndzb)
                new.append((before + gsum[hh, j], ndq + _dot(ndzb, kblk[:, _head_slices(hh)])))
            return tuple(new)

        zero2 = (zero1, jnp.zeros((TILE, HEAD_DIM), F32))
        state = lax.fori_loop(0, i + 1, pass2, (zero2,) * SB_GROUP)
        dq_ref[...] = jnp.concatenate([st[1] * -SCALE for st in state], axis=1).astype(BF16)

        @pl.when(i == nb - 1)
        def _():
            for acc, ref in ((dk_acc, dk_ref), (dv_acc, dv_ref)):
                for j in range(nb):
                    ref[j * TILE:(j + 1) * TILE, :] = jnp.concatenate(
                        [acc[hh, j].T for hh in range(SB_GROUP)], axis=1).astype(BF16)

    qspec = pl.BlockSpec((TILE, SB_LANES), lambda p, i: (i, p))
    kvspec = pl.BlockSpec((s, SB_LANES), lambda p, i: (0, p))
    out = jax.ShapeDtypeStruct((s, SB_WIDTH), BF16)
    outs = _pair_grid_call(
        name, body, nb,
        in_specs=[pl.BlockSpec((TILE, SB_LANES), lambda p, i: (i, cb + p)),
                  pl.BlockSpec((s, SB_LANES), lambda p, i: (0, cb + kb + p)),
                  pl.BlockSpec((s, SB_LANES), lambda p, i: (0, cb + 2 * kb + p)),
                  pl.BlockSpec((TILE, SB_LANES), lambda p, i: (i, db + p))],
        out_specs=[qspec, kvspec, kvspec], out_shape=[out, out, out],
        scratch=[pltpu.VMEM((SB_GROUP, nb, HEAD_DIM, TILE), F32), pltpu.VMEM((SB_GROUP, nb, HEAD_DIM, TILE), F32),
                 pltpu.VMEM((SB_GROUP, nb, TILE, TILE), F32), pltpu.VMEM((SB_GROUP, nb, TILE, TILE), F32),
                 pltpu.VMEM((SB_GROUP, nb, TILE, 1), F32),
                 pltpu.VMEM((2, SB_GROUP, TILE, TILE), F32), pltpu.VMEM((2, SB_GROUP, TILE, TILE), BF16)],
        args=(pa, pa, pa, dout), carried=carried, groups=kb)
    return outs[:3], outs[3:]


def _fox_scores(q, kj, cq, crj, causal, diag):
    sc = _dot_nt(q, kj) + (cq - crj)
    if diag:
        sc = jnp.where(causal, sc, NEG_INF)
    return sc


def _fox_fwd(name, pa, col0, ccol4, crow4, carried=None):
    s = pa.shape[0]
    nb = s // TILE
    cb = col0 // FOX_LANES
    kb = FOX_WIDTH // FOX_LANES

    def body(q_ref, k_ref, v_ref, cc_ref, cr_ref, o_ref, lse_ref, sc_s):
        i = pl.program_id(1)
        r = lax.broadcasted_iota(jnp.int32, (TILE, TILE), 0)
        c = lax.broadcasted_iota(jnp.int32, (TILE, TILE), 1)
        causal = c <= r
        qs = [_scaled_q(q_ref, _head_slices(hh)) for hh in range(FOX_GROUP)]
        cqs = [cc_ref[:, HEAD_DIM * hh:HEAD_DIM * hh + 1] for hh in range(FOX_GROUP)]

        def logits(j, slot, diag):
            kblk = k_ref[_rows(k_ref, j), :]
            tops = []
            for hh in range(FOX_GROUP):
                sc = _fox_scores(qs[hh], kblk[:, _head_slices(hh)], cqs[hh], cr_ref[j, hh:hh + 1, :], causal, diag)
                sc_s[slot, hh] = sc
                tops.append(jnp.max(sc, axis=1, keepdims=True))
            return tuple(tops)

        def update(j, slot, tops, state):
            vblk = v_ref[_rows(v_ref, j), :]
            new = []
            for hh in range(FOX_GROUP):
                m, l, acc = state[hh]
                m2 = jnp.maximum(m, tops[hh])
                alpha = jnp.exp(m - m2)
                p = jnp.exp(sc_s[slot, hh] - m2)
                new.append((m2, l * alpha + jnp.sum(p, axis=1, keepdims=True),
                            acc * alpha + _dot(p.astype(BF16), vblk[:, _head_slices(hh)])))
            return tuple(new)

        def step(t, both):
            tops, state = both
            state = update(i - t + 1, (t - 1) % 2, tops, state)
            return logits(i - t, t % 2, False), state

        zero = (jnp.full((TILE, 1), NEG_INF, F32), jnp.zeros((TILE, 1), F32), jnp.zeros((TILE, HEAD_DIM), F32))
        tops, state = lax.fori_loop(1, i + 1, step, (logits(i, 0, True), (zero,) * FOX_GROUP))
        state = update(0, i % 2, tops, state)
        o_ref[...] = jnp.concatenate([st[2] / st[1] for st in state], axis=1)
        lse_ref[...] = jnp.concatenate(
            [jnp.broadcast_to(st[0] + jnp.log(st[1]), (TILE, HEAD_DIM)) for st in state], axis=1)

    outs = _pair_grid_call(
        name, body, nb,
        in_specs=[pl.BlockSpec((TILE, FOX_LANES), lambda p, i: (i, cb + p)),
                  pl.BlockSpec((s, FOX_LANES), lambda p, i: (0, cb + kb + p)),
                  pl.BlockSpec((s, FOX_LANES), lambda p, i: (0, cb + 2 * kb + p)),
                  pl.BlockSpec((None, TILE, FOX_LANES), lambda p, i: (p, i, 0)),
                  pl.BlockSpec((None, nb, 8, TILE), lambda p, i: (p, 0, 0, 0))],
        out_specs=[pl.BlockSpec((TILE, FOX_LANES), lambda p, i: (i, p)),
                   pl.BlockSpec((None, TILE, FOX_LANES), lambda p, i: (p, i, 0))],
        out_shape=[jax.ShapeDtypeStruct((s, FOX_WIDTH), F32), jax.ShapeDtypeStruct((kb, s, FOX_LANES), F32)],
        scratch=[pltpu.VMEM((2, FOX_GROUP, TILE, TILE), F32)],
        args=(pa, pa, pa, ccol4, crow4), carried=carried, groups=kb)
    return outs[0], outs[1], outs[2:]


def _fox_bwd(name, pa, col0, ccol4, crow4, out, lse, dout, dcol0, carried=None):
    s = pa.shape[0]
    nb = s // TILE
    cb = col0 // FOX_LANES
    kb = FOX_WIDTH // FOX_LANES
    db = dcol0 // FOX_LANES

    def body(q_ref, k_ref, v_ref, cc_ref, cr_ref, o_ref, lse_ref, do_ref,
             dq_ref, dk_ref, dv_ref, cs_ref, dk_acc, dv_acc, p_s, ds_s):
        i = pl.program_id(1)

        @pl.when(i == 0)
        def _():
            dk_acc[...] = jnp.zeros_like(dk_acc)
            dv_acc[...] = jnp.zeros_like(dv_acc)
            cs_ref[...] = jnp.zeros_like(cs_ref)

        r = lax.broadcasted_iota(jnp.int32, (TILE, TILE), 0)
        c = lax.broadcasted_iota(jnp.int32, (TILE, TILE), 1)
        causal = c <= r
        qs = [_scaled_q(q_ref, _head_slices(hh)) for hh in range(FOX_GROUP)]
        cqs = [cc_ref[:, HEAD_DIM * hh:HEAD_DIM * hh + 1] for hh in range(FOX_GROUP)]
        lses = [lse_ref[:, HEAD_DIM * hh:HEAD_DIM * hh + 1] for hh in range(FOX_GROUP)]
        dofs = [do_ref[:, _head_slices(hh)] for hh in range(FOX_GROUP)]
        dos = [d_.astype(BF16) for d_ in dofs]
        dots = [d_.T.astype(BF16) for d_ in dofs]
        qts = [q.astype(F32).T.astype(BF16) for q in qs]
        deltas = [jnp.sum(dofs[hh] * o_ref[:, _head_slices(hh)], axis=1, keepdims=True) for hh in range(FOX_GROUP)]

        def probs(j, slot, rowsums, diag):
            kblk = k_ref[_rows(k_ref, j), :]
            vblk = v_ref[_rows(v_ref, j), :]
            new = []
            for hh in range(FOX_GROUP):
                sl = _head_slices(hh)
                sc = _fox_scores(qs[hh], kblk[:, sl], cqs[hh], cr_ref[j, hh:hh + 1, :], causal, diag)
                p = jnp.exp(sc - lses[hh])
                ds = p * (_dot_nt(dos[hh], vblk[:, sl]) - deltas[hh])
                p_s[slot, hh] = p.astype(BF16)
                ds_s[slot, hh] = ds.astype(BF16)
                cs_ref[j, hh:hh + 1, :] += jnp.sum(ds, axis=0, keepdims=True)
                new.append(rowsums[hh] + jnp.sum(ds, axis=1, keepdims=True))
            return tuple(new)

        def accumulate(j, slot, dqs):
            kblk = k_ref[_rows(k_ref, j), :]
            new = []
            for hh in range(FOX_GROUP):
                dsb = ds_s[slot, hh]
                dv_acc[hh, j] += _dot(dots[hh], p_s[slot, hh])
                dk_acc[hh, j] += _dot(qts[hh], dsb)
                new.append(dqs[hh] + _dot(dsb, kblk[:, _head_slices(hh)]))
            return tuple(new)

        def step(t, both):
            rowsums, dqs = both
            dqs = accumulate(i - t + 1, (t - 1) % 2, dqs)
            return probs(i - t, t % 2, rowsums, False), dqs

        zero1 = jnp.zeros((TILE, 1), F32)
        zero64 = jnp.zeros((TILE, HEAD_DIM), F32)
        rowsums, dqs = lax.fori_loop(1, i + 1, step,
                                     (probs(i, 0, (zero1,) * FOX_GROUP, True), (zero64,) * FOX_GROUP))
        dqs = accumulate(0, i % 2, dqs)
        for hh in range(FOX_GROUP):
            cs_ref[i, hh:hh + 1, :] -= jnp.broadcast_to(rowsums[hh], (TILE, LANES)).T[0:1, :]
        dq_ref[...] = jnp.concatenate([dq * SCALE for dq in dqs], axis=1).astype(BF16)

        @pl.when(i == nb - 1)
        def _():
            for acc, ref in ((dk_acc, dk_ref), (dv_acc, dv_ref)):
                for j in range(nb):
                    ref[j * TILE:(j + 1) * TILE, :] = jnp.concatenate(
                        [acc[hh, j].T for hh in range(FOX_GROUP)], axis=1).astype(BF16)

    qspec = pl.BlockSpec((TILE, FOX_LANES), lambda p, i: (i, p))
    kvspec = pl.BlockSpec((s, FOX_LANES), lambda p, i: (0, p))
    o3 = jax.ShapeDtypeStruct((s, FOX_WIDTH), BF16)
    outs = _pair_grid_call(
        name, body, nb,
        in_specs=[pl.BlockSpec((TILE, FOX_LANES), lambda p, i: (i, cb + p)),
                  pl.BlockSpec((s, FOX_LANES), lambda p, i: (0, cb + kb + p)),
                  pl.BlockSpec((s, FOX_LANES), lambda p, i: (0, cb + 2 * kb + p)),
                  pl.BlockSpec((None, TILE, FOX_LANES), lambda p, i: (p, i, 0)),
                  pl.BlockSpec((None, nb, 8, TILE), lambda p, i: (p, 0, 0, 0)),
                  qspec,
                  pl.BlockSpec((None, TILE, FOX_LANES), lambda p, i: (p, i, 0)),
                  pl.BlockSpec((TILE, FOX_LANES), lambda p, i: (i, db + p))],
        out_specs=[qspec, kvspec, kvspec, pl.BlockSpec((None, nb, 8, TILE), lambda p, i: (p, 0, 0, 0))],
        out_shape=[o3, o3, o3, jax.ShapeDtypeStruct((kb, nb, 8, TILE), F32)],
        scratch=[pltpu.VMEM((FOX_GROUP, nb, HEAD_DIM, TILE), F32), pltpu.VMEM((FOX_GROUP, nb, HEAD_DIM, TILE), F32),
                 pltpu.VMEM((2, FOX_GROUP, TILE, TILE), BF16), pltpu.VMEM((2, FOX_GROUP, TILE, TILE), BF16)],
        args=(pa, pa, pa, ccol4, crow4, out, lse, dout), carried=carried, groups=kb)
    return outs[:4], outs[4:]


def _mem_fwd(name, pa, mkv):
    s = pa.shape[0]
    ml = mkv.shape[0]
    nb = s // TILE
    cb = QKV_WIDTH // LANES

    def body(q_ref, k_ref, v_ref, o_ref, lse_ref):
        outs, lses = [], []
        for hh in range(2):
            sl = _head_slices(hh)
            sc = _dot_nt(_scaled_q(q_ref, sl), k_ref[:, sl])
            m = jnp.max(sc, axis=1, keepdims=True)
            p = jnp.exp(sc - m)
            l = jnp.sum(p, axis=1, keepdims=True)
            outs.append(_dot(p.astype(BF16), v_ref[:, sl]) / l)
            lses.append(jnp.broadcast_to(m + jnp.log(l), (TILE, HEAD_DIM)))
        o_ref[...] = jnp.concatenate(outs, axis=1)
        lse_ref[...] = jnp.concatenate(lses, axis=1)

    return pl.pallas_call(
        body, name=name, grid=(2, nb),
        in_specs=[pl.BlockSpec((TILE, LANES), lambda p, i: (i, cb + p)),
                  pl.BlockSpec((ml, LANES), lambda p, i: (0, p)),
                  pl.BlockSpec((ml, LANES), lambda p, i: (0, 2 + p))],
        out_specs=[pl.BlockSpec((TILE, LANES), lambda p, i: (i, p)),
                   pl.BlockSpec((None, TILE, LANES), lambda p, i: (p, i, 0))],
        out_shape=[jax.ShapeDtypeStruct((s, MEM_WIDTH), F32), jax.ShapeDtypeStruct((2, s, LANES), F32)],
        compiler_params=_params("parallel", "parallel"),
    )(pa, mkv, mkv)


def _mem_bwd(name, pa, mkv, out, lse, dout, dcol0):
    s = pa.shape[0]
    ml = mkv.shape[0]
    nb = s // TILE
    cb = QKV_WIDTH // LANES
    db = dcol0 // LANES

    def body(q_ref, k_ref, v_ref, o_ref, lse_ref, do_ref, dq_ref, dk_ref, dv_ref, dk_acc, dv_acc):
        i = pl.program_id(1)

        @pl.when(i == 0)
        def _():
            dk_acc[...] = jnp.zeros_like(dk_acc)
            dv_acc[...] = jnp.zeros_like(dv_acc)

        dqs = []
        for hh in range(2):
            sl = _head_slices(hh)
            q = _scaled_q(q_ref, sl)
            kh = k_ref[:, sl]
            dof = do_ref[:, sl]
            do = dof.astype(BF16)
            delta = jnp.sum(dof * o_ref[:, sl], axis=1, keepdims=True)
            p = jnp.exp(_dot_nt(q, kh) - lse_ref[:, HEAD_DIM * hh:HEAD_DIM * hh + 1])
            ds = (p * (_dot_nt(do, v_ref[:, sl]) - delta)).astype(BF16)
            dv_acc[hh] += _dot_tn(p.astype(BF16), do)
            dk_acc[hh] += _dot_tn(ds, q)
            dqs.append(_dot(ds, kh) * SCALE)
        dq_ref[...] = jnp.concatenate(dqs, axis=1).astype(BF16)

        @pl.when(i == nb - 1)
        def _():
            dk_ref[...] = jnp.concatenate([dk_acc[0], dk_acc[1]], axis=1).astype(BF16)
            dv_ref[...] = jnp.concatenate([dv_acc[0], dv_acc[1]], axis=1).astype(BF16)

    qspec = pl.BlockSpec((TILE, LANES), lambda p, i: (i, p))
    kvspec = pl.BlockSpec((ml, LANES), lambda p, i: (0, p))
    okv = jax.ShapeDtypeStruct((ml, MEM_WIDTH), BF16)
    return pl.pallas_call(
        body, name=name, grid=(2, nb),
        in_specs=[pl.BlockSpec((TILE, LANES), lambda p, i: (i, cb + p)),
                  pl.BlockSpec((ml, LANES), lambda p, i: (0, p)),
                  pl.BlockSpec((ml, LANES), lambda p, i: (0, 2 + p)),
                  qspec,
                  pl.BlockSpec((None, TILE, LANES), lambda p, i: (p, i, 0)),
                  pl.BlockSpec((TILE, LANES), lambda p, i: (i, db + p))],
        out_specs=[qspec, kvspec, kvspec],
        out_shape=[jax.ShapeDtypeStruct((s, MEM_WIDTH), BF16), okv, okv],
        scratch_shapes=[pltpu.VMEM((2, ml, HEAD_DIM), F32), pltpu.VMEM((2, ml, HEAD_DIM), F32)],
        compiler_params=_params("arbitrary", "arbitrary"),
    )(pa, mkv, mkv, out, lse, dout)


def _head_maps():
    col = jnp.arange(MIX_WIDTH)[:, None] // HEAD_DIM
    g = (col == jnp.arange(LANES)[None, :]).astype(BF16)
    return g, g.T


def _normed_heads(osb_ref, ofx_ref, om_ref, g_ref, gt_ref):
    y = jnp.concatenate([osb_ref[...], ofx_ref[...], om_ref[...]], axis=1)
    msq = _sum_l2(y * y, g_ref[...]) * (1.0 / HEAD_DIM)
    rf = _sum_l3(lax.rsqrt(msq + EPS), gt_ref[...])
    return y * rf, rf


def _out_fwd(name, o_sb, o_fx, o_m, pb, ow, x, w_out, ts):
    s, d = x.shape
    g, gt = _head_maps()

    def body(osb_ref, ofx_ref, om_ref, gate_ref, ow_ref, x_ref, w_ref, g_ref, gt_ref, xo_ref, y2_ref):
        yh, _ = _normed_heads(osb_ref, ofx_ref, om_ref, g_ref, gt_ref)
        gate = gate_ref[...]
        y2 = (yh * ow_ref[...] * (gate * jax.nn.sigmoid(gate))).astype(BF16)
        y2_ref[...] = y2
        xo_ref[...] = x_ref[...] + _dot(y2, w_ref[...])

    return pl.pallas_call(
        body, name=name, grid=(s // ts,),
        in_specs=[_row_spec(ts, SB_WIDTH), _row_spec(ts, FOX_WIDTH), _row_spec(ts, MEM_WIDTH),
                  _row_spec(ts, MIX_WIDTH), _const_spec((1, MIX_WIDTH)), _row_spec(ts, d),
                  _const_spec((MIX_WIDTH, d)),
                  _const_spec((MIX_WIDTH, LANES)), _const_spec((LANES, MIX_WIDTH))],
        out_specs=[_row_spec(ts, d), _row_spec(ts, MIX_WIDTH)],
        out_shape=[jax.ShapeDtypeStruct((s, d), F32), jax.ShapeDtypeStruct((s, MIX_WIDTH), BF16)],
        compiler_params=_params("parallel"),
    )(o_sb, o_fx, o_m, pb, ow, x, w_out, g, gt)


def _row_spec(ts, w):
    return pl.BlockSpec((ts, w), lambda i: (i, 0))


def _const_spec(shape):
    return pl.BlockSpec(shape, lambda i: (0,) * len(shape))


def _out_bwd(name, dxb, o_sb, o_fx, o_m, pb, ow, w_out, ts):
    s, d = dxb.shape
    g, gt = _head_maps()

    def body(dx_ref, osb_ref, ofx_ref, om_ref, gate_ref, ow_ref, w_ref, g_ref, gt_ref, dy_ref, dgate_ref, dow_ref):
        @pl.when(pl.program_id(0) == 0)
        def _():
            dow_ref[...] = jnp.zeros_like(dow_ref)

        dy2 = _dot_nt(dx_ref[...], w_ref[...])
        yh, rf = _normed_heads(osb_ref, ofx_ref, om_ref, g_ref, gt_ref)
        gate = gate_ref[...]
        sig = jax.nn.sigmoid(gate)
        ow_v = ow_ref[...]
        dgate_ref[...] = (dy2 * (yh * ow_v) * (sig * (1.0 + gate * (1.0 - sig)))).astype(BF16)
        dn = dy2 * (gate * sig)
        dow_ref[...] += jnp.sum(dn * yh, axis=0, keepdims=True)
        dyh = dn * ow_v
        t = _sum_l2(dyh * yh, g_ref[...]) * (1.0 / HEAD_DIM)
        dy_ref[...] = rf * (dyh - yh * _sum_l3(t, gt_ref[...]))

    return pl.pallas_call(
        body, name=name, grid=(s // ts,),
        in_specs=[_row_spec(ts, d), _row_spec(ts, SB_WIDTH), _row_spec(ts, FOX_WIDTH), _row_spec(ts, MEM_WIDTH),
                  _row_spec(ts, MIX_WIDTH), _const_spec((1, MIX_WIDTH)),
                  _const_spec((MIX_WIDTH, d)),
                  _const_spec((MIX_WIDTH, LANES)), _const_spec((LANES, MIX_WIDTH))],
        out_specs=[_row_spec(ts, MIX_WIDTH), _row_spec(ts, MIX_WIDTH), _const_spec((1, MIX_WIDTH))],
        out_shape=[jax.ShapeDtypeStruct((s, MIX_WIDTH), F32), jax.ShapeDtypeStruct((s, PB), BF16),
                   jax.ShapeDtypeStruct((1, MIX_WIDTH), F32)],
        compiler_params=_params("arbitrary"),
    )(dxb, o_sb, o_fx, o_m, pb, ow, w_out, g, gt)


def _adamw(name, w, g, m, v, tr):
    def body(w_ref, g_ref, m_ref, v_ref, d_ref, m2_ref, v2_ref):
        gv = g_ref[...]
        m2 = ADAM_B1 * m_ref[...] + (1.0 - ADAM_B1) * gv
        v2 = ADAM_B2 * v_ref[...] + (1.0 - ADAM_B2) * (gv * gv)
        m_hat = m2 / (1.0 - ADAM_B1 ** ADAM_STEP)
        v_hat = v2 / (1.0 - ADAM_B2 ** ADAM_STEP)
        d_ref[...] = -ADAM_LR * (m_hat / (jnp.sqrt(v_hat) + ADAM_EPS) + ADAM_WD * w_ref[...])
        m2_ref[...] = m2
        v2_ref[...] = v2

    rest = w.shape[1:]
    spec = pl.BlockSpec((tr,) + rest, lambda i: (i,) + (0,) * len(rest))
    shp = jax.ShapeDtypeStruct(w.shape, F32)
    return pl.pallas_call(
        body, name=name, grid=(w.shape[0] // tr,), in_specs=[spec] * 4, out_specs=[spec] * 3, out_shape=[shp] * 3,
        compiler_params=_params("parallel"),
    )(w, g, m, v)


def _adamw_sharded(name, w, m, v, g_own, g_other, cvec, tr):
    depth, rows, cols = w.shape
    nt = rows // 2 // tr

    def body(c_ref, w_ref, m_ref, v_ref, *rest):
        g_refs, (g_ref, d_ref, m2_ref, v2_ref) = rest[:2 * depth], rest[2 * depth:]
        layer, mine = pl.program_id(0), pl.program_id(1) == c_ref[0]
        gv = None
        for lt in range(depth):
            cand = jnp.where(mine, g_refs[lt][...], g_refs[depth + lt][...])
            gv = cand if gv is None else jnp.where(layer == lt, cand, gv)
        m2 = ADAM_B1 * m_ref[...] + (1.0 - ADAM_B1) * gv
        v2 = ADAM_B2 * v_ref[...] + (1.0 - ADAM_B2) * (gv * gv)
        m_hat = m2 / (1.0 - ADAM_B1 ** ADAM_STEP)
        v_hat = v2 / (1.0 - ADAM_B2 ** ADAM_STEP)
        g_ref[...] = gv
        d_ref[...] = -ADAM_LR * (m_hat / (jnp.sqrt(v_hat) + ADAM_EPS) + ADAM_WD * w_ref[...])
        m2_ref[...] = m2
        v2_ref[...] = v2

    def g_map(lt, own):
        def index(l, hf, i, c_ref):
            use = jnp.logical_and(l == lt, (hf == c_ref[0]) == own)
            return jnp.where(use, i, 0), 0
        return index

    full = pl.BlockSpec((None, tr, cols), lambda l, hf, i, c_ref: (l, hf * nt + i, 0))
    g_specs = [pl.BlockSpec((tr, cols), g_map(lt, own)) for own in (True, False) for lt in range(depth)]
    shp = jax.ShapeDtypeStruct((depth, rows, cols), F32)
    return pl.pallas_call(
        body, name=name,
        grid_spec=pltpu.PrefetchScalarGridSpec(
            num_scalar_prefetch=1, grid=(depth, 2, nt), in_specs=[full] * 3 + g_specs, out_specs=[full] * 4),
        out_shape=[shp] * 4,
        compiler_params=_params("arbitrary", "arbitrary", "arbitrary"),
    )(cvec, w, m, v, *g_own, *g_other)


HBM_SPEC = pl.BlockSpec(memory_space=pltpu.HBM)


def _place():
    x, y, c = lax.axis_index("x"), lax.axis_index("y"), lax.axis_index("c")
    chips = [(1 - x, y), (x, 1 - y), (1 - x, 1 - y)]
    return x, y, c, chips


def _remote(src, dst, send_sems, recv_sems, k, to):
    return pltpu.make_async_remote_copy(src_ref=src, dst_ref=dst, send_sem=send_sems.at[k], recv_sem=recv_sems.at[k],
                                        device_id=to, device_id_type=MESH)


def _half_rows(n_rows, cc):
    rh = n_rows // 2
    return pl.ds(pl.multiple_of(cc * rh, 16), rh)


def _dma_sems(n):
    return [pltpu.SemaphoreType.DMA((n,)), pltpu.SemaphoreType.DMA((n,))]


class _Exchange:
    def __init__(self, inputs, out_shapes, n_sems, begin, relay, finish):
        self.inputs, self.out_shapes, self.n_sems = list(inputs), list(out_shapes), n_sems
        self.begin, self.relay, self.finish = begin, relay, finish

    @property
    def n(self):
        return len(self.inputs)

    def split(self, refs):
        return refs[:self.n], refs[self.n:2 * self.n], refs[2 * self.n], refs[2 * self.n + 1]


def _run_exchange(name, ex):
    def body(*refs):
        parts = ex.split(refs)
        for phase in (ex.begin, ex.relay, ex.finish):
            if phase is not None:
                phase(*parts)

    return pl.pallas_call(
        body, name=name, in_specs=[HBM_SPEC] * ex.n, out_specs=[HBM_SPEC] * ex.n, out_shape=ex.out_shapes,
        scratch_shapes=_dma_sems(ex.n_sems),
    )(*ex.inputs)


def _gather_exchange(shards):
    def ici(in_refs, out_refs, send_sems, recv_sems):
        x, y, c, chips = _place()
        return [_remote(in_ref.at[_half_rows(in_ref.shape[0], c)], out_ref.at[2 * x + y, _half_rows(in_ref.shape[0], c)],
                        send_sems, recv_sems, 6 * a + j, (cx, cy, c))
                for a, (in_ref, out_ref) in enumerate(zip(in_refs, out_refs)) for j, (cx, cy) in enumerate(chips)]

    def d2d(out_refs, send_sems, recv_sems, half_of):
        x, y, c, chips = _place()
        cps = []
        for a, out_ref in enumerate(out_refs):
            for j, (cx, cy) in enumerate(chips):
                piece = out_ref.at[2 * cx + cy, _half_rows(out_ref.shape[1], half_of(c))]
                cps.append(_remote(piece, piece, send_sems, recv_sems, 6 * a + 3 + j, (x, y, 1 - c)))
        return cps

    def begin(in_refs, out_refs, send_sems, recv_sems):
        for cp in ici(in_refs, out_refs, send_sems, recv_sems):
            cp.start()

    def relay(in_refs, out_refs, send_sems, recv_sems):
        x, y, c, chips = _place()
        for a, out_ref in enumerate(out_refs):
            for j, (cx, cy) in enumerate(chips):
                landed = out_ref.at[2 * cx + cy, _half_rows(out_ref.shape[1], c)]
                _remote(landed, landed, send_sems, recv_sems, 6 * a + j, (cx, cy, c)).wait_recv()
        for cp in d2d(out_refs, send_sems, recv_sems, lambda c_: c_):
            cp.start()

    def finish(in_refs, out_refs, send_sems, recv_sems):
        for cp in d2d(out_refs, send_sems, recv_sems, lambda c_: 1 - c_):
            cp.wait_recv()
        for cp in ici(in_refs, out_refs, send_sems, recv_sems) + d2d(out_refs, send_sems, recv_sems, lambda c_: c_):
            cp.wait_send()

    shapes = [jax.ShapeDtypeStruct((N_CHIPS,) + s_.shape, s_.dtype) for s_ in shards]
    return _Exchange(shards, shapes, 6 * len(shards), begin, relay, finish)


def _swap_exchange(g4s):
    def copies(in_refs, out_refs, send_sems, recv_sems):
        x, y, c, _ = _place()
        return [_remote(in_ref.at[:, _half_rows(in_ref.shape[1], 1 - c), :], out_ref, send_sems, recv_sems, a, (x, y, 1 - c))
                for a, (in_ref, out_ref) in enumerate(zip(in_refs, out_refs))]

    def begin(*parts):
        for cp in copies(*parts):
            cp.start()

    def finish(*parts):
        for cp in copies(*parts):
            cp.wait()

    shapes = [jax.ShapeDtypeStruct((g.shape[0], g.shape[1] // 2, g.shape[2]), g.dtype) for g in g4s]
    return _Exchange(g4s, shapes, len(g4s), begin, None, finish)


def _add_half(name, g4, r1, cvec, tr):
    n, r, w = g4.shape
    rh = r // 2
    nblk = rh // tr

    def body(c_ref, a_ref, b_ref, o_ref):
        o_ref[...] = (a_ref[...].astype(F32) + b_ref[...].astype(F32)).astype(BF16)

    return pl.pallas_call(
        body, name=name,
        grid_spec=pltpu.PrefetchScalarGridSpec(
            num_scalar_prefetch=1, grid=(n, nblk),
            in_specs=[pl.BlockSpec((None, tr, w), lambda k, i, c_ref: (k, c_ref[0] * nblk + i, 0)),
                      pl.BlockSpec((None, tr, w), lambda k, i, c_ref: (k, i, 0))],
            out_specs=pl.BlockSpec((None, tr, w), lambda k, i, c_ref: (k, i, 0))),
        out_shape=jax.ShapeDtypeStruct((n, rh, w), BF16),
        compiler_params=_params("parallel", "parallel"),
    )(cvec, g4, r1)


def _scatter_exchange(h4s):
    def sends(in_refs, out_refs, send_sems, recv_sems):
        x, y, c, chips = _place()
        return [_remote(in_ref.at[2 * cx + cy], out_ref.at[j], send_sems, recv_sems, 3 * a + j, (cx, cy, c))
                for a, (in_ref, out_ref) in enumerate(zip(in_refs, out_refs)) for j, (cx, cy) in enumerate(chips)]

    def begin(*parts):
        for cp in sends(*parts):
            cp.start()

    def finish(in_refs, out_refs, send_sems, recv_sems):
        x, y, c, chips = _place()
        for a, out_ref in enumerate(out_refs):
            for j, (cx, cy) in enumerate(chips):
                got = out_ref.at[j]
                _remote(got, got, send_sems, recv_sems, 3 * a + j, (cx, cy, c)).wait_recv()
        for cp in sends(in_refs, out_refs, send_sems, recv_sems):
            cp.wait_send()

    shapes = [jax.ShapeDtypeStruct((3,) + h.shape[1:], h.dtype) for h in h4s]
    return _Exchange(h4s, shapes, 3 * len(h4s), begin, None, finish)


def _sum_chips(name, h4, r3, mvec, tr):
    _, rh, w = h4.shape

    def body(m_ref, a_ref, b_ref, c_ref, d_ref, o_ref):
        o_ref[...] = ((a_ref[...].astype(F32) + b_ref[...].astype(F32)) + c_ref[...].astype(F32)) + d_ref[...].astype(F32)

    specs = [pl.BlockSpec((None, tr, w), lambda i, m_ref: (m_ref[0], i, 0))]
    specs += [pl.BlockSpec((None, tr, w), functools.partial(lambda k, i, m_ref: (k, i, 0), k)) for k in range(3)]
    return pl.pallas_call(
        body, name=name,
        grid_spec=pltpu.PrefetchScalarGridSpec(
            num_scalar_prefetch=1, grid=(rh // tr,), in_specs=specs,
            out_specs=pl.BlockSpec((tr, w), lambda i, m_ref: (i, 0))),
        out_shape=jax.ShapeDtypeStruct((rh, w), F32),
        compiler_params=_params("parallel"),
    )(mvec, h4, r3, r3, r3)


def _swap_reduced(name, ghs):
    n = len(ghs)

    def body(*refs):
        in_refs, out_refs, (send_sems, recv_sems) = refs[:n], refs[n:2 * n], refs[2 * n:]
        x, y, c, _ = _place()
        cps = [_remote(in_ref, out_ref, send_sems, recv_sems, a, (x, y, 1 - c))
               for a, (in_ref, out_ref) in enumerate(zip(in_refs, out_refs))]
        for cp in cps:
            cp.start()
        for cp in cps:
            cp.wait()

    return pl.pallas_call(
        body, name=name, in_specs=[HBM_SPEC] * n, out_specs=[HBM_SPEC] * n,
        out_shape=[jax.ShapeDtypeStruct(g.shape, g.dtype) for g in ghs],
        scratch_shapes=_dma_sems(n),
    )(*ghs)


def _small_update(name, partials, weights, moments1, moments2):
    n = len(partials)
    width = max(p.shape[1] for p in partials)
    starts, at = [], 0
    for p in partials:
        starts.append(at)
        at += p.shape[0]
    rows = -(-at // 8) * 8
    has_w = [w is not None for w in weights]
    n_w = sum(has_w)

    def body(*refs):
        p_refs = refs[:n]
        w_refs, m_refs, v_refs = refs[n:n + n_w], refs[n + n_w:n + 2 * n_w], refs[n + 2 * n_w:n + 3 * n_w]
        outs = refs[n + 3 * n_w:-4]
        g_refs, upd_refs = outs[:n], outs[n:]
        vec, buf, send_sems, recv_sems = refs[-4:]
        x, y, c, _ = _place()
        me = 4 * x + 2 * y + c
        vec[...] = jnp.zeros_like(vec)
        for p_ref, r0 in zip(p_refs, starts):
            vec[r0:r0 + p_ref.shape[0], 0:p_ref.shape[1]] = p_ref[...]
        buf[me] = vec[...]
        flips = [(fx, fy, fc) for fx in (0, 1) for fy in (0, 1) for fc in (0, 1)][1:]
        peers = [(x + fx - 2 * x * fx, y + fy - 2 * y * fy, c + fc - 2 * c * fc) for fx, fy, fc in flips]
        sends = [_remote(vec, buf.at[me], send_sems, recv_sems, k, peer) for k, peer in enumerate(peers)]
        for cp in sends:
            cp.start()
        for k, (px, py, pc) in enumerate(peers):
            got = buf.at[4 * px + 2 * py + pc]
            _remote(got, got, send_sems, recv_sems, k, (px, py, pc)).wait_recv()
        for cp in sends:
            cp.wait_send()
        total = buf[0]
        for dev in range(1, N_DEV):
            total = total + buf[dev]
        k = 0
        for a in range(n):
            r, w = g_refs[a].shape
            g = total[starts[a]:starts[a] + r, 0:w]
            g_refs[a][...] = g
            if has_w[a]:
                m2 = ADAM_B1 * m_refs[k][...] + (1.0 - ADAM_B1) * g
                v2 = ADAM_B2 * v_refs[k][...] + (1.0 - ADAM_B2) * (g * g)
                m_hat = m2 / (1.0 - ADAM_B1 ** ADAM_STEP)
                v_hat = v2 / (1.0 - ADAM_B2 ** ADAM_STEP)
                upd_refs[3 * k][...] = -ADAM_LR * (m_hat / (jnp.sqrt(v_hat) + ADAM_EPS) + ADAM_WD * w_refs[k][...])
                upd_refs[3 * k + 1][...] = m2
                upd_refs[3 * k + 2][...] = v2
                k += 1

    ws = [w for w in weights if w is not None]
    g_shapes = [jax.ShapeDtypeStruct(p.shape if w is None else w.shape, F32) for p, w in zip(partials, weights)]
    u_shapes = [jax.ShapeDtypeStruct(w.shape, F32) for w in ws for _ in range(3)]
    vm = pl.BlockSpec(memory_space=pltpu.VMEM)
    n_args = n + 3 * n_w
    outs = pl.pallas_call(
        body, name=name, in_specs=[vm] * n_args, out_specs=[vm] * (n + 3 * n_w), out_shape=g_shapes + u_shapes,
        scratch_shapes=[pltpu.VMEM((rows, width), F32), pltpu.VMEM((N_DEV, rows, width), F32),
                        pltpu.SemaphoreType.DMA((7,)), pltpu.SemaphoreType.DMA((7,))],
    )(*partials, *ws, *[m for m in moments1 if m is not None], *[v for v in moments2 if v is not None])
    return outs[:n], outs[n:]


GATE_COL = 3 * SB_WIDTH + 3 * FOX_WIDTH + FOX_HEADS + MEM_WIDTH
FL_COL = QKV_WIDTH


GROUP_A_COLS = [(0, QKV_WIDTH), (FL_COL + FOX_HEADS, MEM_WIDTH)]
GROUP_B_COLS = [(GATE_COL, MIX_WIDTH), (FL_COL, FOX_HEADS)]


def _group_from_shards(shard_of, cw, spans, pad):
    parts = []
    for lo, width in spans:
        hi = lo + width
        for j in range(N_CHIPS):
            a, b = max(lo, j * cw), min(hi, (j + 1) * cw)
            if a < b:
                parts.append(shard_of(j)[:, a - j * cw:b - j * cw])
    if pad:
        parts.append(jnp.zeros((parts[0].shape[0], pad), parts[0].dtype))
    return jnp.concatenate(parts, axis=1)


def _shard_from_groups(ga, gb, j, cw):
    lo, hi = j * cw, (j + 1) * cw
    placed = []
    for grp, spans in ((ga, GROUP_A_COLS), (gb, GROUP_B_COLS)):
        at = 0
        for first, width in spans:
            a, b = max(lo, first), min(hi, first + width)
            if a < b:
                placed.append((a, grp[:, at + a - first:at + b - first]))
            at += width
    return jnp.concatenate([p for _, p in sorted(placed, key=lambda t: t[0])], axis=1)


def _tile_of(n, cap, unit):
    if n <= cap:
        return n
    best = None
    for t in range(unit, cap + 1, unit):
        if n % t == 0:
            best = t
    assert best is not None, (n, cap, unit)
    return best


def _column_major_rows(a):
    dp, r, c = a.shape
    return a.transpose(2, 0, 1).reshape(c, dp, r // LANES, LANES).transpose(0, 2, 1, 3).reshape(-1, 8, LANES)


def _from_column_major_rows(b, shape):
    dp, r, c = shape
    return b.reshape(c, r // LANES, dp, LANES).transpose(0, 2, 1, 3).reshape(c, dp, r).transpose(1, 2, 0)


def _pack_small(parts):
    rows = []
    for p in parts:
        f = p.reshape(-1).astype(F32)
        f = jnp.pad(f, (0, (-f.shape[0]) % LANES))
        rows.append(f.reshape(-1, LANES))
    out = jnp.concatenate(rows, axis=0)
    return jnp.pad(out, ((0, (-out.shape[0]) % 8), (0, 0)))


def _unpack_small(packed, shapes):
    outs, r = [], 0
    for shp in shapes:
        n = 1
        for s_ in shp:
            n *= s_
        nr = -(-n // LANES)
        outs.append(packed[r:r + nr].reshape(-1)[:n].reshape(shp))
        r += nr
    return outs


def kernel(x, mem, norm_w, w_in, b_forget, mem_norm_w, w_mem_kv, out_norm_w, w_out, final_norm_w, loss_target, m_norm_w, m_w_in, m_b_forget, m_mem_norm_w, m_w_mem_kv, m_out_norm_w, m_w_out, m_final_norm_w, v_norm_w, v_w_in, v_b_forget, v_mem_norm_w, v_w_mem_kv, v_out_norm_w, v_w_out, v_final_norm_w):
    xs = x[0]
    mems = mem[0]
    target = loss_target[0]
    s, d = xs.shape
    depth = norm_w.shape[0]
    nb = s // TILE
    ts = _tile_of(s, 256, 8)
    big = (w_in, w_mem_kv, w_out)
    core = lax.axis_index("c")
    chip = 2 * lax.axis_index("x") + lax.axis_index("y")
    cvec = core.astype(jnp.int32).reshape(1)
    mvec = chip.astype(jnp.int32).reshape(1)
    cw = w_in.shape[2]

    own_w = [[a[l].astype(BF16) for a in big] for l in range(depth)]

    def lay_out_in(own, got):
        shard_of = lambda j: jnp.where(chip == j, own, got[j])
        return (_group_from_shards(shard_of, cw, GROUP_A_COLS, 0),
                _group_from_shards(shard_of, cw, GROUP_B_COLS, LANES - FOX_HEADS))

    def lay_out_rows(own, got):
        full = jnp.where(lax.broadcasted_iota(jnp.int32, got.shape, 0) == chip, own[None], got)
        return full.reshape(-1, full.shape[2])

    w_in_groups = [lay_out_in(own_w[0][0], _run_exchange("gather_weights0", _gather_exchange(own_w[0][:1]))[0])]
    layer_w = []

    tm = _tile_of(s, 256, 8)
    fl_block = MIX_WIDTH // LANES

    saved = []
    cur = xs
    for l in range(depth):
        wa, wb = w_in_groups[l]
        h = _rms_fwd(f"rms_fwd{l}", cur, norm_w[l][None], ts)
        pa = _mm(f"inproj_a{l}", h, wa, "nn", tm, _tile_of(PA, 1664, LANES), BF16)
        pb = _mm(f"inproj_b{l}", h, wb, "nn", tm, PB, F32)
        bpad = jnp.pad(b_forget[l], (0, LANES - FOX_HEADS))[None]
        ccol, crow = _gate_fwd(f"gate_fwd{l}", pb, bpad, fl_block)
        fg = FOX_HEADS // FOX_GROUP
        ccol4 = jnp.repeat(ccol[:, :FOX_HEADS].reshape(s, fg, FOX_GROUP).transpose(1, 0, 2), HEAD_DIM, axis=2)
        crow4 = jnp.pad(crow.reshape(nb, fg, FOX_GROUP, TILE).transpose(1, 0, 2, 3),
                        ((0, 0), (0, 0), (0, 8 - FOX_GROUP), (0, 0)))
        more = l + 1 < depth
        riding = own_w[l][1:] + (own_w[l + 1][:1] if more else [])
        o_sb, got = _sb_fwd(f"sb_fwd{l}", pa, 0, carried=_gather_exchange(riding))
        wkv, wout = lay_out_rows(own_w[l][1], got[0]), lay_out_rows(own_w[l][2], got[1])
        layer_w.append((wa, wb, wkv, wout))
        if more:
            w_in_groups.append(lay_out_in(own_w[l + 1][0], got[2]))
        o_fx, lse_fx, _ = _fox_fwd(f"fox_fwd{l}", pa, 3 * SB_WIDTH, ccol4, crow4)
        mn = _rms_fwd(f"mem_rms{l}", mems, mem_norm_w[l][None], mems.shape[0])
        mkv = _mm(f"mem_kv{l}", mn, wkv, "nn", mems.shape[0], 2 * MEM_WIDTH, BF16)
        o_m, lse_m = _mem_fwd(f"mem_fwd{l}", pa, mkv)
        nxt, y2 = _out_fwd(f"out_fwd{l}", o_sb, o_fx, o_m, pb, out_norm_w[l][None], cur, wout, ts)
        saved.append((cur, h, pa, pb, bpad, ccol4, crow4, o_sb, o_fx, lse_fx, mn, mkv, o_m, lse_m, y2))
        cur = nxt

    loss_v, dx, dxb, g_final = _final_loss("final_loss", cur, final_norm_w[None], target, ts)

    g_norm, g_b, g_memnorm, g_outnorm = [None] * depth, [None] * depth, [None] * depth, [None] * depth
    g_wa, g_wb, g_wkv, g_wout = [None] * depth, [None] * depth, [None] * depth, [None] * depth
    g_own = [[None] * depth for _ in big]
    g_other = [[None] * depth for _ in big]

    def swap_of(jobs):
        return _swap_exchange([g for _, _, g, _ in jobs])

    def chip_sums(jobs, got):
        return [(lr, k, _add_half(f"grad_add_half{lr}_{k}", g, r_, cvec, t_), t_) for (lr, k, g, t_), r_ in zip(jobs, got)]

    def reduce_at_owner(tag, jobs, from_chips):
        halves = [_sum_chips(f"grad_sum_chips{lr}_{k}", h_, r_, mvec, t_) for (lr, k, h_, t_), r_ in zip(jobs, from_chips)]
        others = _swap_reduced(f"grad_swap_reduced{tag}", halves)
        for (lr, k, _, _), mine, other in zip(jobs, halves, others):
            g_own[k][lr], g_other[k][lr] = mine, other

    def job(lr, k, g4):
        return lr, k, g4, _tile_of(g4.shape[1] // 2, 256, 16)

    pending = []
    for l in reversed(range(depth)):
        xin, h, pa, pb, bpad, ccol4, crow4, o_sb, o_fx, lse_fx, mn, mkv, o_m, lse_m, y2 = saved[l]
        wa, wb, wkv, wout = layer_w[l]
        dy, dgate, g_outnorm[l] = _out_bwd(f"out_bwd{l}", dxb, o_sb, o_fx, o_m, pb, out_norm_w[l][None], wout, ts)
        g_wout[l] = _mm(f"dw_out{l}", y2, dxb, "tn", _tile_of(MIX_WIDTH, 640, LANES), d, F32)
        dq_m, dk_m, dv_m = _mem_bwd(f"mem_bwd{l}", pa, mkv, o_m, lse_m, dy, SB_WIDTH + FOX_WIDTH)
        dmkv = jnp.concatenate([dk_m, dv_m], axis=1)
        g_wkv[l] = _mm(f"dw_kv{l}", mn, dmkv, "tn", d, 2 * MEM_WIDTH, F32)
        dmn = _mm(f"dmem{l}", dmkv, wkv, "nt", mems.shape[0], d, F32)
        g_memnorm[l] = _rms_wgrad(f"mem_norm_grad{l}", mems, dmn)
        small = [job(l, 1, g_wkv[l].reshape(N_CHIPS, -1, g_wkv[l].shape[1])), job(l, 2, g_wout[l].reshape(N_CHIPS, -1, d))]
        (dq_fx, dk_fx, dv_fx, cs4), got = _fox_bwd(f"fox_bwd{l}", pa, 3 * SB_WIDTH, ccol4, crow4, o_fx, lse_fx, dy,
                                                    SB_WIDTH, carried=swap_of(small))
        pending += chip_sums(small, got)
        (dq_sb, dk_sb, dv_sb), from_chips = _sb_bwd(f"sb_bwd{l}", pa, 0, dy, 0,
                                                   carried=_scatter_exchange([j[2] for j in pending]))
        reduce_at_owner(f"{l}s", pending, from_chips)
        colsum = cs4[:, :, :FOX_GROUP, :].transpose(1, 0, 2, 3).reshape(nb, 8, TILE)
        dpb, g_b[l] = _gate_bwd(f"gate_bwd{l}", pb, bpad, colsum, fl_block, dgate)
        dpa = jnp.concatenate([dq_sb, dk_sb, dv_sb, dq_fx, dk_fx, dv_fx, dq_m], axis=1)
        tw = _tile_of(d, 512, LANES)
        g_wa[l] = _mm(f"dw_in_a{l}", h, dpa, "tn", tw, _tile_of(PA, 1664, LANES), BF16)
        g_wb[l] = _mm(f"dw_in_b{l}", h, dpb, "tn", tw, PB, BF16)
        g4_in = jnp.stack([_shard_from_groups(g_wa[l], g_wb[l], j, cw) for j in range(N_CHIPS)])
        w_in_job = [job(l, 0, g4_in)]
        if l > 0:
            dx, dxb, g_norm[l], got = _inproj_bwd(f"inproj_bwd{l}", dpa, dpb, wa, wb, xin, norm_w[l][None], dx, ts,
                                                  carried=swap_of(w_in_job))
            pending = chip_sums(w_in_job, got)
        else:
            pending = chip_sums(w_in_job, _run_exchange("grad_swap_halves_last", swap_of(w_in_job)))
            dx, dxb, g_norm[l], from_chips = _inproj_bwd(f"inproj_bwd{l}", dpa, dpb, wa, wb, xin, norm_w[l][None], dx, ts,
                                                          carried=_scatter_exchange([j[2] for j in pending]))
            reduce_at_owner("last", pending, from_chips)

    small_w = [norm_w, b_forget, mem_norm_w, out_norm_w, final_norm_w]
    small_m = [m_norm_w, m_b_forget, m_mem_norm_w, m_out_norm_w, m_final_norm_w]
    small_v = [v_norm_w, v_b_forget, v_mem_norm_w, v_out_norm_w, v_final_norm_w]
    rows2 = lambda a: a.reshape(-1, a.shape[-1])
    partials = [jnp.concatenate(g_norm, axis=0), jnp.concatenate(g_b, axis=0), jnp.concatenate(g_memnorm, axis=0),
                jnp.concatenate(g_outnorm, axis=0), g_final, loss_v]
    sums, updates = _small_update("small_update", partials, [rows2(a) for a in small_w] + [None],
                                  [rows2(a) for a in small_m] + [None], [rows2(a) for a in small_v] + [None])
    small_grads = [g.reshape(a.shape) for g, a in zip(sums, small_w)]
    loss = sums[-1][0, 0]
    small_delta, small_m2, small_v2 = ([updates[3 * k + t].reshape(a.shape) for k, a in enumerate(small_w)]
                                       for t in range(3))
    big_grads, big_delta, big_m2, big_v2 = [], [], [], []
    for k, (nm, w_, m_, v_) in enumerate(zip(("w_in", "w_mem_kv", "w_out"), big, (m_w_in, m_w_mem_kv, m_w_out),
                                             (v_w_in, v_w_mem_kv, v_w_out))):
        if w_.shape[2] % LANES:
            g_full = jnp.stack([jnp.concatenate([jnp.where(core == 0, go, gt), jnp.where(core == 0, gt, go)], axis=0)
                                for go, gt in zip(g_own[k], g_other[k])])
            w_p, g_p, m_p, v_p = (_column_major_rows(a) for a in (w_, g_full, m_, v_))
            outs = _adamw(f"adamw_{nm}", w_p, g_p, m_p, v_p, _tile_of(w_p.shape[0], 600, 1))
            outs = [_from_column_major_rows(o, w_.shape) for o in (g_p, *outs)]
        else:
            outs = _adamw_sharded(f"adamw_{nm}", w_, m_, v_, g_own[k], g_other[k], cvec,
                                  _tile_of(w_.shape[1] // 2, 256, 8))
        for lst, o in zip((big_grads, big_delta, big_m2, big_v2), outs):
            lst.append(o)

    def order(sm, bg):
        return [sm[0], bg[0], sm[1], sm[2], bg[1], sm[3], bg[2], sm[4]]

    return (loss, dx[None], *order(small_grads, big_grads), *order(small_delta, big_delta),
            *order(small_m2, big_m2), *order(small_v2, big_v2))
```

```python
import functools

import jax
import jax.numpy as jnp
from jax import lax
from jax.experimental import pallas as pl
from jax.experimental.pallas import tpu as pltpu

F32 = jnp.float32
BF16 = jnp.bfloat16

HEAD_DIM = 64
SB_WIDTH = 512
FOX_WIDTH = 512
FOX_HEADS = 8
MEM_WIDTH = 256
MIX_WIDTH = SB_WIDTH + FOX_WIDTH + MEM_WIDTH
TOTAL_HEADS = MIX_WIDTH // HEAD_DIM
IN_WIDTH = 3 * SB_WIDTH + 3 * FOX_WIDTH + FOX_HEADS + MEM_WIDTH + MIX_WIDTH
LANES = 128
QKV_WIDTH = 3 * SB_WIDTH + 3 * FOX_WIDTH
PA = QKV_WIDTH + MEM_WIDTH
PB = LANES + MIX_WIDTH
EPS = 1e-6
SCALE = HEAD_DIM ** -0.5
TILE = 256
SB_GROUP = 4
SB_LANES = SB_GROUP * HEAD_DIM
FOX_GROUP = 4
FOX_LANES = FOX_GROUP * HEAD_DIM
NEG_INF = float("-inf")
MASKED = -1e30

ADAM_LR = 0.001
ADAM_B1 = 0.9
ADAM_B2 = 0.999
ADAM_EPS = 1e-08
ADAM_WD = 0.01
ADAM_STEP = 10

N_CHIPS = 4
N_DEV = 8
VMEM_LIMIT = 48 * 1024 * 1024
MESH = pl.DeviceIdType.MESH


def _params(*sem):
    return pltpu.CompilerParams(dimension_semantics=tuple(sem), vmem_limit_bytes=VMEM_LIMIT)


def _dot(a, b):
    return jnp.dot(a, b, preferred_element_type=F32)


def _dot_nt(a, b):
    return lax.dot_general(a, b, (((1,), (1,)), ((), ())), preferred_element_type=F32)


def _dot_tn(a, b):
    return lax.dot_general(a, b, (((0,), (0,)), ((), ())), preferred_element_type=F32)


def _split2(x):
    hi = x.astype(BF16)
    lo = (x - hi.astype(F32)).astype(BF16)
    return hi, lo


def _split3(x):
    hi = x.astype(BF16)
    r = x - hi.astype(F32)
    mid = r.astype(BF16)
    lo = (r - mid.astype(F32)).astype(BF16)
    return hi, mid, lo


def _sum_l2(x, u):
    hi, lo = _split2(x)
    return _dot(hi, u) + _dot(lo, u)


def _sum_l3(x, u):
    hi, mid, lo = _split3(x)
    return _dot(hi, u) + _dot(mid, u) + _dot(lo, u)


def _sum_r3(u, x):
    hi, mid, lo = _split3(x)
    return _dot(u, hi) + _dot(u, mid) + _dot(u, lo)


def _softplus(z):
    return jnp.maximum(z, 0.0) + jnp.log1p(jnp.exp(-jnp.abs(z)))


def _tri(n, pred):
    r = lax.broadcasted_iota(jnp.int32, (n, n), 0)
    c = lax.broadcasted_iota(jnp.int32, (n, n), 1)
    return jnp.where(pred(r, c), 1.0, 0.0).astype(BF16)


def _rows(ref, j, n=TILE):
    return pl.ds(pl.multiple_of(j * n, n), n)


def _mm(name, a, b, mode, tm, tn, out_dtype, res=None, a_lead=(), b_lead=()):
    a2, b2 = a.shape[len(a_lead):], b.shape[len(b_lead):]
    if mode == "tn":
        k, m = a2
    else:
        m, k = a2
    n = b2[0] if mode == "nt" else b2[1]
    assert m % tm == 0 and n % tn == 0, (name, m, tm, n, tn)
    na, nb = (None,) * len(a_lead), (None,) * len(b_lead)
    if mode == "tn":
        a_spec = pl.BlockSpec(na + (k, tm), lambda j, i: a_lead + (0, i))
    else:
        a_spec = pl.BlockSpec(na + (tm, k), lambda j, i: a_lead + (i, 0))
    if mode == "nt":
        b_spec = pl.BlockSpec(nb + (tn, k), lambda j, i: b_lead + (j, 0))
    else:
        b_spec = pl.BlockSpec(nb + (k, tn), lambda j, i: b_lead + (0, j))
    o_spec = pl.BlockSpec((tm, tn), lambda j, i: (i, j))
    dot = {"nn": _dot, "nt": _dot_nt, "tn": _dot_tn}[mode]

    def body(a_ref, b_ref, *rest):
        o_ref = rest[-1]
        acc = dot(a_ref[...].astype(BF16), b_ref[...].astype(BF16))
        if res is not None:
            acc = acc + rest[0][...]
        o_ref[...] = acc.astype(o_ref.dtype)

    args, specs = [a, b], [a_spec, b_spec]
    if res is not None:
        args.append(res)
        specs.append(o_spec)
    return pl.pallas_call(
        body, name=name, grid=(n // tn, m // tm), in_specs=specs, out_specs=o_spec,
        out_shape=jax.ShapeDtypeStruct((m, n), out_dtype),
        compiler_params=_params("parallel", "parallel"),
    )(*args)


def _rms_fwd(name, x, g, ts):
    s, d = x.shape

    def body(x_ref, g_ref, o_ref):
        xf = x_ref[...]
        r = lax.rsqrt(jnp.mean(xf * xf, axis=1, keepdims=True) + EPS)
        o_ref[...] = (xf * r * g_ref[...]).astype(BF16)

    return pl.pallas_call(
        body, name=name, grid=(s // ts,),
        in_specs=[pl.BlockSpec((ts, d), lambda i: (i, 0)), pl.BlockSpec((1, d), lambda i: (0, 0))],
        out_specs=pl.BlockSpec((ts, d), lambda i: (i, 0)),
        out_shape=jax.ShapeDtypeStruct((s, d), BF16),
        compiler_params=_params("parallel"),
    )(x, g)


def _inproj_bwd(name, dpa, dpb, wa, wb, x, g, dres, ts, carried=None):
    s, d = x.shape

    def body(dpa_ref, dpb_ref, wa_ref, wb_ref, x_ref, g_ref, dres_ref, dx_ref, dxb_ref, dg_ref):
        @pl.when(pl.program_id(1) == 0)
        def _():
            dg_ref[...] = jnp.zeros_like(dg_ref)

        dhf = _dot_nt(dpa_ref[...], wa_ref[...]) + _dot_nt(dpb_ref[...], wb_ref[...])
        xf = x_ref[...]
        r = lax.rsqrt(jnp.mean(xf * xf, axis=1, keepdims=True) + EPS)
        xh = xf * r
        dg_ref[...] += jnp.sum(dhf * xh, axis=0, keepdims=True)
        dxh = dhf * g_ref[...]
        m = jnp.mean(dxh * xh, axis=1, keepdims=True)
        dx = r * (dxh - xh * m) + dres_ref[...]
        dx_ref[...] = dx
        dxb_ref[...] = dx.astype(BF16)

    row = lambda w: pl.BlockSpec((ts, w), lambda p, i: (i, 0))
    whole = lambda a: pl.BlockSpec(a.shape, lambda p, i: (0, 0))
    outs = _pair_grid_call(
        name, body, s // ts,
        in_specs=[row(dpa.shape[1]), row(dpb.shape[1]), whole(wa), whole(wb), row(d), whole(g), row(d)],
        out_specs=[row(d), row(d), pl.BlockSpec((1, d), lambda p, i: (0, 0))],
        out_shape=[jax.ShapeDtypeStruct((s, d), F32), jax.ShapeDtypeStruct((s, d), BF16),
                   jax.ShapeDtypeStruct((1, d), F32)],
        scratch=[], args=(dpa, dpb, wa, wb, x, g, dres), carried=carried, groups=1)
    return outs[0], outs[1], outs[2], outs[3:]


def _rms_wgrad(name, x, dh):
    m_, d = x.shape

    def body(x_ref, dh_ref, dg_ref):
        xf = x_ref[...]
        r = lax.rsqrt(jnp.mean(xf * xf, axis=1, keepdims=True) + EPS)
        dg_ref[...] = jnp.sum(dh_ref[...] * xf * r, axis=0, keepdims=True)

    return pl.pallas_call(
        body, name=name, out_shape=jax.ShapeDtypeStruct((1, d), F32),
    )(x, dh)


def _final_loss(name, x, g, target, ts):
    s, d = x.shape

    def body(x_ref, g_ref, t_ref, loss_ref, dx_ref, dxb_ref, dg_ref):
        @pl.when(pl.program_id(0) == 0)
        def _():
            dg_ref[...] = jnp.zeros_like(dg_ref)
            loss_ref[...] = jnp.zeros_like(loss_ref)

        xf = x_ref[...]
        gw = g_ref[...]
        r = lax.rsqrt(jnp.mean(xf * xf, axis=1, keepdims=True) + EPS)
        xh = xf * r
        e = xh * gw - t_ref[...]
        part = 0.5 * jnp.sum(jnp.mean(e * e, axis=1, keepdims=True), axis=0, keepdims=True)
        loss_ref[...] += jnp.broadcast_to(part, loss_ref.shape)
        dy = e * (1.0 / d)
        dg_ref[...] += jnp.sum(dy * xh, axis=0, keepdims=True)
        dxh = dy * gw
        m = jnp.mean(dxh * xh, axis=1, keepdims=True)
        dx = r * (dxh - xh * m)
        dx_ref[...] = dx
        dxb_ref[...] = dx.astype(BF16)

    row = pl.BlockSpec((ts, d), lambda i: (i, 0))
    vec = pl.BlockSpec((1, d), lambda i: (0, 0))
    lvec = pl.BlockSpec((1, LANES), lambda i: (0, 0))
    return pl.pallas_call(
        body, name=name, grid=(s // ts,), in_specs=[row, vec, row], out_specs=[lvec, row, row, vec],
        out_shape=[jax.ShapeDtypeStruct((1, LANES), F32), jax.ShapeDtypeStruct((s, d), F32),
                   jax.ShapeDtypeStruct((s, d), BF16), jax.ShapeDtypeStruct((1, d), F32)],
        compiler_params=_params("arbitrary"),
    )(x, g, target)


def _gate_fwd(name, pb, bpad, fl_block):
    s = pb.shape[0]
    nb = s // TILE
    fg = FOX_HEADS // FOX_GROUP

    def body(fl_ref, b_ref, ccol_ref, crow_ref, carry):
        @pl.when(pl.program_id(0) == 0)
        def _():
            carry[...] = jnp.zeros_like(carry)

        u = fl_ref[...] + b_ref[...]
        lf = jnp.minimum(u, 0.0) - jnp.log1p(jnp.exp(-jnp.abs(u)))
        lower = _tri(TILE, lambda r, c: c <= r)
        c = _sum_r3(lower, lf) + carry[0:1, :]
        for grp in range(fg):
            ccol_ref[grp] = jnp.concatenate(
                [jnp.broadcast_to(c[:, grp * FOX_GROUP + hh:grp * FOX_GROUP + hh + 1], (TILE, HEAD_DIM))
                 for hh in range(FOX_GROUP)], axis=1)
        crow_ref[0] = c.T[0:8, :]
        carry[...] = jnp.broadcast_to(c[TILE - 1:TILE, :], carry.shape)

    return pl.pallas_call(
        body, name=name, grid=(nb,),
        in_specs=[pl.BlockSpec((TILE, LANES), lambda i: (i, fl_block)), pl.BlockSpec((1, LANES), lambda i: (0, 0))],
        out_specs=[pl.BlockSpec((fg, TILE, FOX_LANES), lambda i: (0, i, 0)), pl.BlockSpec((1, 8, TILE), lambda i: (i, 0, 0))],
        out_shape=[jax.ShapeDtypeStruct((fg, s, FOX_LANES), F32), jax.ShapeDtypeStruct((nb, 8, TILE), F32)],
        scratch_shapes=[pltpu.VMEM((8, LANES), F32)],
        compiler_params=_params("arbitrary"),
    )(pb, bpad)


def _gate_bwd(name, pb, bpad, colsum, fl_block, dpb):
    s = pb.shape[0]
    nb = s // TILE

    def body(fl_ref, b_ref, cs_ref, dpb_ref, dl_ref, db_ref, carry):
        @pl.when(pl.program_id(0) == 0)
        def _():
            carry[...] = jnp.zeros_like(carry)
            db_ref[...] = jnp.zeros_like(db_ref)

        upper = _tri(TILE, lambda r, c: r >= c)
        rsum = _sum_l3(cs_ref[0], upper) + carry[:, 0:1]
        carry[...] = jnp.broadcast_to(rsum[:, 0:1], carry.shape)
        full = jnp.concatenate([rsum, jnp.zeros((LANES - 8, TILE), F32)], axis=0)
        dlf = -full.T
        u = fl_ref[...] + b_ref[...]
        dlogit = dlf * (1.0 - jax.nn.sigmoid(u))
        dl_ref[...] = dlogit.astype(BF16)
        db_ref[...] += jnp.sum(dlogit, axis=0, keepdims=True)

    logits_block = pl.BlockSpec((TILE, LANES), lambda i: (nb - 1 - i, fl_block))
    return pl.pallas_call(
        body, name=name, grid=(nb,),
        in_specs=[logits_block, pl.BlockSpec((1, LANES), lambda i: (0, 0)),
                  pl.BlockSpec((1, 8, TILE), lambda i: (nb - 1 - i, 0, 0)), pl.BlockSpec(memory_space=pl.ANY)],
        out_specs=[logits_block, pl.BlockSpec((1, LANES), lambda i: (0, 0))],
        out_shape=[jax.ShapeDtypeStruct(dpb.shape, BF16), jax.ShapeDtypeStruct((1, LANES), F32)],
        scratch_shapes=[pltpu.VMEM((8, LANES), F32)], input_output_aliases={3: 0},
        compiler_params=_params("arbitrary"),
    )(pb, bpad, colsum, dpb)


def _head_slices(hh):
    return slice(HEAD_DIM * hh, HEAD_DIM * (hh + 1))


def _scaled_q(q_ref, sl, scale=SCALE):
    return (q_ref[:, sl].astype(F32) * scale).astype(BF16)


def _neg_abs(x):
    sign = jnp.uint32(0x80000000)
    return lax.bitcast_convert_type(lax.bitcast_convert_type(x, jnp.uint32) | sign, F32)


def _sb_tile(qn, kj, carry, strict, u_after, diag):
    nz = _dot_nt(qn, kj)
    lf = jnp.minimum(nz, 0.0) - jnp.log(1.0 + jnp.exp(_neg_abs(nz)))
    lsig = lf - nz
    if diag:
        lf = jnp.where(strict, lf, 0.0)
    sx = _dot(lf.astype(BF16), u_after)
    a = jnp.exp(lsig + sx + carry)
    if diag:
        a = jnp.where(strict, a, 0.0)
    return lsig, a, carry + sx[:, 0:1] + lf[:, 0:1]


def _pair_grid_call(name, body, nb, in_specs, out_specs, out_shape, scratch, args, carried=None, groups=4):
    if carried is None:
        return pl.pallas_call(
            body, name=name, grid=(groups, nb), in_specs=in_specs, out_specs=out_specs, out_shape=out_shape,
            scratch_shapes=scratch, compiler_params=_params("arbitrary", "arbitrary"),
        )(*args)
    n_in, n_out, n_ex = len(in_specs), len(out_specs), carried.n

    def body_with_copies(*refs):
        own_in, ex_in = refs[:n_in], refs[n_in:n_in + n_ex]
        own_out = refs[n_in + n_ex:n_in + n_ex + n_out]
        ex_out = refs[n_in + n_ex + n_out:n_in + 2 * n_ex + n_out]
        own_scratch, sems = refs[n_in + 2 * n_ex + n_out:-2], refs[-2:]
        parts = (ex_in, ex_out, sems[0], sems[1])
        p, i = pl.program_id(0), pl.program_id(1)
        pl.when(jnp.logical_and(p == 0, i == 0))(lambda: carried.begin(*parts))
        if carried.relay is not None:
            pl.when(jnp.logical_and(p == groups - 1, i == max(nb - 2, 0)))(lambda: carried.relay(*parts))
        body(*own_in, *own_out, *own_scratch)
        pl.when(jnp.logical_and(p == groups - 1, i == nb - 1))(lambda: carried.finish(*parts))

    return pl.pallas_call(
        body_with_copies, name=name, grid=(groups, nb), in_specs=list(in_specs) + [HBM_SPEC] * n_ex,
        out_specs=list(out_specs) + [HBM_SPEC] * n_ex, out_shape=list(out_shape) + carried.out_shapes,
        scratch_shapes=list(scratch) + _dma_sems(carried.n_sems),
        compiler_params=_params("arbitrary", "arbitrary"),
    )(*args, *carried.inputs)


def _sb_fwd(name, pa, col0, carried=None):
    s = pa.shape[0]
    nb = s // TILE
    cb = col0 // SB_LANES
    kb = SB_WIDTH // SB_LANES

    def body(q_ref, k_ref, v_ref, o_ref, lsig_s, lf_s):
        i = pl.program_id(1)
        r = lax.broadcasted_iota(jnp.int32, (TILE, TILE), 0)
        c = lax.broadcasted_iota(jnp.int32, (TILE, TILE), 1)
        strict = c < r
        u_after = _tri(TILE, lambda rr, cc: rr > cc)
        qs = [_scaled_q(q_ref, _head_slices(hh), -SCALE) for hh in range(SB_GROUP)]

        def neg_z(j):
            kblk = k_ref[_rows(k_ref, j), :]
            return [_dot_nt(qs[hh], kblk[:, _head_slices(hh)]) for hh in range(SB_GROUP)]

        def scores(nzs, slot, diag):
            for hh, nz in enumerate(nzs):
                lf = jnp.minimum(nz, 0.0) - jnp.log(1.0 + jnp.exp(_neg_abs(nz)))
                lsig = lf - nz
                if diag:
                    lf = jnp.where(strict, lf, 0.0)
                    lsig = jnp.where(strict, lsig, MASKED)
                lsig_s[slot, hh] = lsig
                lf_s[slot, hh] = lf.astype(BF16)

        def weigh(j, slot, state):
            vblk = v_ref[_rows(v_ref, j), :]
            new = []
            for hh in range(SB_GROUP):
                carry, acc = state[hh]
                lfb = lf_s[slot, hh]
                sx = _dot(lfb, u_after)
                a = jnp.exp(lsig_s[slot, hh] + sx + carry)
                new.append((carry + sx[:, 0:1] + lfb[:, 0:1].astype(F32),
                            acc + _dot(a.astype(BF16), vblk[:, _head_slices(hh)])))
            return tuple(new)

        def step(t, state):
            state = weigh(i - t + 1, (t - 1) % 2, state)
            scores(neg_z(i - t), t % 2, False)
            return state

        zero = (jnp.zeros((TILE, 1), F32), jnp.zeros((TILE, HEAD_DIM), F32))
        scores(neg_z(i), 0, True)
        state = lax.fori_loop(1, i + 1, step, (zero,) * SB_GROUP)
        state = weigh(0, i % 2, state)
        o_ref[...] = jnp.concatenate([st[1] for st in state], axis=1)

    outs = _pair_grid_call(
        name, body, nb,
        in_specs=[pl.BlockSpec((TILE, SB_LANES), lambda p, i: (i, cb + p)),
                  pl.BlockSpec((s, SB_LANES), lambda p, i: (0, cb + kb + p)),
                  pl.BlockSpec((s, SB_LANES), lambda p, i: (0, cb + 2 * kb + p))],
        out_specs=[pl.BlockSpec((TILE, SB_LANES), lambda p, i: (i, p))],
        out_shape=[jax.ShapeDtypeStruct((s, SB_WIDTH), F32)],
        scratch=[pltpu.VMEM((2, SB_GROUP, TILE, TILE), F32), pltpu.VMEM((2, SB_GROUP, TILE, TILE), BF16)],
        args=(pa, pa, pa), carried=carried, groups=kb)
    return outs[0], outs[1:]


def _sb_bwd(name, pa, col0, dout, dcol0, carried=None):
    s = pa.shape[0]
    nb = s // TILE
    cb = col0 // SB_LANES
    kb = SB_WIDTH // SB_LANES
    db = dcol0 // SB_LANES

    def body(q_ref, k_ref, v_ref, do_ref, dq_ref, dk_ref, dv_ref, dk_acc, dv_acc, dpan, span, gsum, lsig_s, lf_s):
        i = pl.program_id(1)

        @pl.when(i == 0)
        def _():
            dk_acc[...] = jnp.zeros_like(dk_acc)
            dv_acc[...] = jnp.zeros_like(dv_acc)

        r = lax.broadcasted_iota(jnp.int32, (TILE, TILE), 0)
        c = lax.broadcasted_iota(jnp.int32, (TILE, TILE), 1)
        strict = c < r
        u_after = _tri(TILE, lambda rr, cc: rr > cc)
        u_before = _tri(TILE, lambda rr, cc: rr < cc)
        qs = [_scaled_q(q_ref, _head_slices(hh), -SCALE) for hh in range(SB_GROUP)]
        dos = [do_ref[:, _head_slices(hh)].astype(BF16) for hh in range(SB_GROUP)]
        dots = [do_ref[:, _head_slices(hh)].T.astype(BF16) for hh in range(SB_GROUP)]
        qts = [q.astype(F32).T.astype(BF16) for q in qs]

        def scores(j, slot, diag):
            kblk = k_ref[_rows(k_ref, j), :]
            for hh in range(SB_GROUP):
                nz = _dot_nt(qs[hh], kblk[:, _head_slices(hh)])
                lf = jnp.minimum(nz, 0.0) - jnp.log(1.0 + jnp.exp(_neg_abs(nz)))
                lsig = lf - nz
                if diag:
                    lf = jnp.where(strict, lf, 0.0)
                    lsig = jnp.where(strict, lsig, MASKED)
                lsig_s[slot, hh] = lsig
                lf_s[slot, hh] = lf.astype(BF16)

        def grads(j, slot, carries):
            vblk = v_ref[_rows(v_ref, j), :]
            new = []
            for hh in range(SB_GROUP):
                lfb = lf_s[slot, hh]
                lsig = lsig_s[slot, hh]
                sx = _dot(lfb, u_after)
                a = jnp.exp(lsig + sx + carries[hh])
                g = a * _dot_nt(dos[hh], vblk[:, _head_slices(hh)])
                sig = jnp.exp(lsig)
                inside = _dot(g.astype(BF16), u_before)
                dpan[hh, j] = sig * (inside + g) - g
                span[hh, j] = sig
                gsum[hh, j] = inside[:, TILE - 1:TILE] + g[:, TILE - 1:TILE]
                dv_acc[hh, j] += _dot(dots[hh], a.astype(BF16))
                new.append(carries[hh] + sx[:, 0:1] + lfb[:, 0:1].astype(F32))
            return tuple(new)

        def step1(t, carries):
            carries = grads(i - t + 1, (t - 1) % 2, carries)
            scores(i - t, t % 2, False)
            return carries

        zero1 = jnp.zeros((TILE, 1), F32)
        scores(i, 0, True)
        carries = lax.fori_loop(1, i + 1, step1, (zero1,) * SB_GROUP)
        grads(0, i % 2, carries)

        def pass2(j, state):
            kblk = k_ref[_rows(k_ref, j), :]
            new = []
            for hh in range(SB_GROUP):
                before, ndq = state[hh]
                ndzb = (dpan[hh, j] + span[hh, j] * before).astype(BF16)
                dk_acc[hh, j] += _dot(qts[hh], ndzb)
                new.append((before + gsum[hh, j], ndq + _dot(ndzb, kblk[:, _head_slices(hh)])))
            return tuple(new)

        zero2 = (zero1, jnp.zeros((TILE, HEAD_DIM), F32))
        state = lax.fori_loop(0, i + 1, pass2, (zero2,) * SB_GROUP)
        dq_ref[...] = jnp.concatenate([st[1] * -SCALE for st in state], axis=1).astype(BF16)

        @pl.when(i == nb - 1)
        def _():
            for acc, ref in ((dk_acc, dk_ref), (dv_acc, dv_ref)):
                for j in range(nb):
                    ref[j * TILE:(j + 1) * TILE, :] = jnp.concatenate(
                        [acc[hh, j].T for hh in range(SB_GROUP)], axis=1).astype(BF16)

    qspec = pl.BlockSpec((TILE, SB_LANES), lambda p, i: (i, p))
    kvspec = pl.BlockSpec((s, SB_LANES), lambda p, i: (0, p))
    out = jax.ShapeDtypeStruct((s, SB_WIDTH), BF16)
    outs = _pair_grid_call(
        name, body, nb,
        in_specs=[pl.BlockSpec((TILE, SB_LANES), lambda p, i: (i, cb + p)),
                  pl.BlockSpec((s, SB_LANES), lambda p, i: (0, cb + kb + p)),
                  pl.BlockSpec((s, SB_LANES), lambda p, i: (0, cb + 2 * kb + p)),
                  pl.BlockSpec((TILE, SB_LANES), lambda p, i: (i, db + p))],
        out_specs=[qspec, kvspec, kvspec], out_shape=[out, out, out],
        scratch=[pltpu.VMEM((SB_GROUP, nb, HEAD_DIM, TILE), F32), pltpu.VMEM((SB_GROUP, nb, HEAD_DIM, TILE), F32),
                 pltpu.VMEM((SB_GROUP, nb, TILE, TILE), F32), pltpu.VMEM((SB_GROUP, nb, TILE, TILE), F32),
                 pltpu.VMEM((SB_GROUP, nb, TILE, 1), F32),
                 pltpu.VMEM((2, SB_GROUP, TILE, TILE), F32), pltpu.VMEM((2, SB_GROUP, TILE, TILE), BF16)],
        args=(pa, pa, pa, dout), carried=carried, groups=kb)
    return outs[:3], outs[3:]


def _fox_scores(q, kj, cq, crj, causal, diag):
    sc = _dot_nt(q, kj) + (cq - crj)
    if diag:
        sc = jnp.where(causal, sc, NEG_INF)
    return sc


def _fox_fwd(name, pa, col0, ccol4, crow4, carried=None):
    s = pa.shape[0]
    nb = s // TILE
    cb = col0 // FOX_LANES
    kb = FOX_WIDTH // FOX_LANES

    def body(q_ref, k_ref, v_ref, cc_ref, cr_ref, o_ref, lse_ref, sc_s):
        i = pl.program_id(1)
        head0 = pl.program_id(0) * FOX_GROUP
        r = lax.broadcasted_iota(jnp.int32, (TILE, TILE), 0)
        c = lax.broadcasted_iota(jnp.int32, (TILE, TILE), 1)
        causal = c <= r
        qs = [_scaled_q(q_ref, _head_slices(hh)) for hh in range(FOX_GROUP)]
        cqs = [cc_ref[:, HEAD_DIM * hh:HEAD_DIM * hh + 1] for hh in range(FOX_GROUP)]

        def logits(j, slot, diag):
            kblk = k_ref[_rows(k_ref, j), :]
            tops = []
            for hh in range(FOX_GROUP):
                sc = _fox_scores(qs[hh], kblk[:, _head_slices(hh)], cqs[hh], cr_ref[j, pl.ds(head0 + hh, 1), :], causal, diag)
                sc_s[slot, hh] = sc
                tops.append(jnp.max(sc, axis=1, keepdims=True))
            return tuple(tops)

        def update(j, slot, tops, state):
            vblk = v_ref[_rows(v_ref, j), :]
            new = []
            for hh in range(FOX_GROUP):
                m, l, acc = state[hh]
                m2 = jnp.maximum(m, tops[hh])
                alpha = jnp.exp(m - m2)
                p = jnp.exp(sc_s[slot, hh] - m2)
                new.append((m2, l * alpha + jnp.sum(p, axis=1, keepdims=True),
                            acc * alpha + _dot(p.astype(BF16), vblk[:, _head_slices(hh)])))
            return tuple(new)

        def step(t, both):
            tops, state = both
            state = update(i - t + 1, (t - 1) % 2, tops, state)
            return logits(i - t, t % 2, False), state

        zero = (jnp.full((TILE, 1), NEG_INF, F32), jnp.zeros((TILE, 1), F32), jnp.zeros((TILE, HEAD_DIM), F32))
        tops, state = lax.fori_loop(1, i + 1, step, (logits(i, 0, True), (zero,) * FOX_GROUP))
        state = update(0, i % 2, tops, state)
        o_ref[...] = jnp.concatenate([st[2] / st[1] for st in state], axis=1)
        lse_ref[...] = jnp.concatenate(
            [jnp.broadcast_to(st[0] + jnp.log(st[1]), (TILE, HEAD_DIM)) for st in state], axis=1)

    outs = _pair_grid_call(
        name, body, nb,
        in_specs=[pl.BlockSpec((TILE, FOX_LANES), lambda p, i: (i, cb + p)),
                  pl.BlockSpec((s, FOX_LANES), lambda p, i: (0, cb + kb + p)),
                  pl.BlockSpec((s, FOX_LANES), lambda p, i: (0, cb + 2 * kb + p)),
                  pl.BlockSpec((None, TILE, FOX_LANES), lambda p, i: (p, i, 0)),
                  pl.BlockSpec((nb, 8, TILE), lambda p, i: (0, 0, 0))],
        out_specs=[pl.BlockSpec((TILE, FOX_LANES), lambda p, i: (i, p)),
                   pl.BlockSpec((None, TILE, FOX_LANES), lambda p, i: (p, i, 0))],
        out_shape=[jax.ShapeDtypeStruct((s, FOX_WIDTH), F32), jax.ShapeDtypeStruct((kb, s, FOX_LANES), F32)],
        scratch=[pltpu.VMEM((2, FOX_GROUP, TILE, TILE), F32)],
        args=(pa, pa, pa, ccol4, crow4), carried=carried, groups=kb)
    return outs[0], outs[1], outs[2:]


def _fox_bwd(name, pa, col0, ccol4, crow4, out, lse, dout, dcol0, carried=None):
    s = pa.shape[0]
    nb = s // TILE
    cb = col0 // FOX_LANES
    kb = FOX_WIDTH // FOX_LANES
    db = dcol0 // FOX_LANES

    def body(q_ref, k_ref, v_ref, cc_ref, cr_ref, o_ref, lse_ref, do_ref,
             dq_ref, dk_ref, dv_ref, cs_ref, dk_acc, dv_acc, p_s, ds_s):
        i = pl.program_id(1)
        head0 = pl.program_id(0) * FOX_GROUP

        @pl.when(i == 0)
        def _():
            dk_acc[...] = jnp.zeros_like(dk_acc)
            dv_acc[...] = jnp.zeros_like(dv_acc)

        @pl.when(jnp.logical_and(i == 0, head0 == 0))
        def _():
            cs_ref[...] = jnp.zeros_like(cs_ref)

        r = lax.broadcasted_iota(jnp.int32, (TILE, TILE), 0)
        c = lax.broadcasted_iota(jnp.int32, (TILE, TILE), 1)
        causal = c <= r
        qs = [_scaled_q(q_ref, _head_slices(hh)) for hh in range(FOX_GROUP)]
        cqs = [cc_ref[:, HEAD_DIM * hh:HEAD_DIM * hh + 1] for hh in range(FOX_GROUP)]
        lses = [lse_ref[:, HEAD_DIM * hh:HEAD_DIM * hh + 1] for hh in range(FOX_GROUP)]
        dofs = [do_ref[:, _head_slices(hh)] for hh in range(FOX_GROUP)]
        dos = [d_.astype(BF16) for d_ in dofs]
        dots = [d_.T.astype(BF16) for d_ in dofs]
        qts = [q.astype(F32).T.astype(BF16) for q in qs]
        deltas = [jnp.sum(dofs[hh] * o_ref[:, _head_slices(hh)], axis=1, keepdims=True) for hh in range(FOX_GROUP)]

        def probs(j, slot, rowsums, diag):
            kblk = k_ref[_rows(k_ref, j), :]
            vblk = v_ref[_rows(v_ref, j), :]
            new = []
            for hh in range(FOX_GROUP):
                sl = _head_slices(hh)
                sc = _fox_scores(qs[hh], kblk[:, sl], cqs[hh], cr_ref[j, pl.ds(head0 + hh, 1), :], causal, diag)
                p = jnp.exp(sc - lses[hh])
                ds = p * (_dot_nt(dos[hh], vblk[:, sl]) - deltas[hh])
                p_s[slot, hh] = p.astype(BF16)
                ds_s[slot, hh] = ds.astype(BF16)
                cs_ref[j, pl.ds(head0 + hh, 1), :] += jnp.sum(ds, axis=0, keepdims=True)
                new.append(rowsums[hh] + jnp.sum(ds, axis=1, keepdims=True))
            return tuple(new)

        def accumulate(j, slot, dqs):
            kblk = k_ref[_rows(k_ref, j), :]
            new = []
            for hh in range(FOX_GROUP):
                dsb = ds_s[slot, hh]
                dv_acc[hh, j] += _dot(dots[hh], p_s[slot, hh])
                dk_acc[hh, j] += _dot(qts[hh], dsb)
                new.append(dqs[hh] + _dot(dsb, kblk[:, _head_slices(hh)]))
            return tuple(new)

        def step(t, both):
            rowsums, dqs = both
            dqs = accumulate(i - t + 1, (t - 1) % 2, dqs)
            return probs(i - t, t % 2, rowsums, False), dqs

        zero1 = jnp.zeros((TILE, 1), F32)
        zero64 = jnp.zeros((TILE, HEAD_DIM), F32)
        rowsums, dqs = lax.fori_loop(1, i + 1, step,
                                     (probs(i, 0, (zero1,) * FOX_GROUP, True), (zero64,) * FOX_GROUP))
        dqs = accumulate(0, i % 2, dqs)
        for hh in range(FOX_GROUP):
            cs_ref[i, pl.ds(head0 + hh, 1), :] -= jnp.broadcast_to(rowsums[hh], (TILE, LANES)).T[0:1, :]
        dq_ref[...] = jnp.concatenate([dq * SCALE for dq in dqs], axis=1).astype(BF16)

        @pl.when(i == nb - 1)
        def _():
            for acc, ref in ((dk_acc, dk_ref), (dv_acc, dv_ref)):
                for j in range(nb):
                    ref[j * TILE:(j + 1) * TILE, :] = jnp.concatenate(
                        [acc[hh, j].T for hh in range(FOX_GROUP)], axis=1).astype(BF16)

    qspec = pl.BlockSpec((TILE, FOX_LANES), lambda p, i: (i, p))
    kvspec = pl.BlockSpec((s, FOX_LANES), lambda p, i: (0, p))
    o3 = jax.ShapeDtypeStruct((s, FOX_WIDTH), BF16)
    outs = _pair_grid_call(
        name, body, nb,
        in_specs=[pl.BlockSpec((TILE, FOX_LANES), lambda p, i: (i, cb + p)),
                  pl.BlockSpec((s, FOX_LANES), lambda p, i: (0, cb + kb + p)),
                  pl.BlockSpec((s, FOX_LANES), lambda p, i: (0, cb + 2 * kb + p)),
                  pl.BlockSpec((None, TILE, FOX_LANES), lambda p, i: (p, i, 0)),
                  pl.BlockSpec((nb, 8, TILE), lambda p, i: (0, 0, 0)),
                  qspec,
                  pl.BlockSpec((None, TILE, FOX_LANES), lambda p, i: (p, i, 0)),
                  pl.BlockSpec((TILE, FOX_LANES), lambda p, i: (i, db + p))],
        out_specs=[qspec, kvspec, kvspec, pl.BlockSpec((nb, 8, TILE), lambda p, i: (0, 0, 0))],
        out_shape=[o3, o3, o3, jax.ShapeDtypeStruct((nb, 8, TILE), F32)],
        scratch=[pltpu.VMEM((FOX_GROUP, nb, HEAD_DIM, TILE), F32), pltpu.VMEM((FOX_GROUP, nb, HEAD_DIM, TILE), F32),
                 pltpu.VMEM((2, FOX_GROUP, TILE, TILE), BF16), pltpu.VMEM((2, FOX_GROUP, TILE, TILE), BF16)],
        args=(pa, pa, pa, ccol4, crow4, out, lse, dout), carried=carried, groups=kb)
    return outs[:4], outs[4:]


def _mem_fwd(name, pa, mkv):
    s = pa.shape[0]
    ml = mkv.shape[0]
    nb = s // TILE
    cb = QKV_WIDTH // LANES

    def body(q_ref, k_ref, v_ref, o_ref, lse_ref):
        outs, lses = [], []
        for hh in range(2):
            sl = _head_slices(hh)
            sc = _dot_nt(_scaled_q(q_ref, sl), k_ref[:, sl])
            m = jnp.max(sc, axis=1, keepdims=True)
            p = jnp.exp(sc - m)
            l = jnp.sum(p, axis=1, keepdims=True)
            outs.append(_dot(p.astype(BF16), v_ref[:, sl]) / l)
            lses.append(jnp.broadcast_to(m + jnp.log(l), (TILE, HEAD_DIM)))
        o_ref[...] = jnp.concatenate(outs, axis=1)
        lse_ref[...] = jnp.concatenate(lses, axis=1)

    return pl.pallas_call(
        body, name=name, grid=(2, nb),
        in_specs=[pl.BlockSpec((TILE, LANES), lambda p, i: (i, cb + p)),
                  pl.BlockSpec((ml, LANES), lambda p, i: (0, p)),
                  pl.BlockSpec((ml, LANES), lambda p, i: (0, 2 + p))],
        out_specs=[pl.BlockSpec((TILE, LANES), lambda p, i: (i, p)),
                   pl.BlockSpec((None, TILE, LANES), lambda p, i: (p, i, 0))],
        out_shape=[jax.ShapeDtypeStruct((s, MEM_WIDTH), F32), jax.ShapeDtypeStruct((2, s, LANES), F32)],
        compiler_params=_params("parallel", "parallel"),
    )(pa, mkv, mkv)


def _mem_bwd(name, pa, mkv, out, lse, dout, dcol0):
    s = pa.shape[0]
    ml = mkv.shape[0]
    nb = s // TILE
    cb = QKV_WIDTH // LANES
    db = dcol0 // LANES

    def body(q_ref, k_ref, v_ref, o_ref, lse_ref, do_ref, dq_ref, dk_ref, dv_ref, dk_acc, dv_acc):
        i = pl.program_id(1)

        @pl.when(i == 0)
        def _():
            dk_acc[...] = jnp.zeros_like(dk_acc)
            dv_acc[...] = jnp.zeros_like(dv_acc)

        dqs = []
        for hh in range(2):
            sl = _head_slices(hh)
            q = _scaled_q(q_ref, sl)
            kh = k_ref[:, sl]
            dof = do_ref[:, sl]
            do = dof.astype(BF16)
            delta = jnp.sum(dof * o_ref[:, sl], axis=1, keepdims=True)
            p = jnp.exp(_dot_nt(q, kh) - lse_ref[:, HEAD_DIM * hh:HEAD_DIM * hh + 1])
            ds = (p * (_dot_nt(do, v_ref[:, sl]) - delta)).astype(BF16)
            dv_acc[hh] += _dot(dof.T.astype(BF16), p.astype(BF16))
            dk_acc[hh] += _dot(q.astype(F32).T.astype(BF16), ds)
            dqs.append(_dot(ds, kh) * SCALE)
        dq_ref[...] = jnp.concatenate(dqs, axis=1).astype(BF16)

        @pl.when(i == nb - 1)
        def _():
            dk_ref[...] = jnp.concatenate([dk_acc[0].T, dk_acc[1].T], axis=1).astype(BF16)
            dv_ref[...] = jnp.concatenate([dv_acc[0].T, dv_acc[1].T], axis=1).astype(BF16)

    qspec = pl.BlockSpec((TILE, LANES), lambda p, i: (i, p))
    kvspec = pl.BlockSpec((ml, LANES), lambda p, i: (0, p))
    okv = jax.ShapeDtypeStruct((ml, MEM_WIDTH), BF16)
    return pl.pallas_call(
        body, name=name, grid=(2, nb),
        in_specs=[pl.BlockSpec((TILE, LANES), lambda p, i: (i, cb + p)),
                  pl.BlockSpec((ml, LANES), lambda p, i: (0, p)),
                  pl.BlockSpec((ml, LANES), lambda p, i: (0, 2 + p)),
                  qspec,
                  pl.BlockSpec((None, TILE, LANES), lambda p, i: (p, i, 0)),
                  pl.BlockSpec((TILE, LANES), lambda p, i: (i, db + p))],
        out_specs=[qspec, kvspec, kvspec],
        out_shape=[jax.ShapeDtypeStruct((s, MEM_WIDTH), BF16), okv, okv],
        scratch_shapes=[pltpu.VMEM((2, HEAD_DIM, ml), F32), pltpu.VMEM((2, HEAD_DIM, ml), F32)],
        compiler_params=_params("arbitrary", "arbitrary"),
    )(pa, mkv, mkv, out, lse, dout)


def _head_maps():
    col = jnp.arange(MIX_WIDTH)[:, None] // HEAD_DIM
    g = (col == jnp.arange(LANES)[None, :]).astype(BF16)
    return g, g.T


def _normed_heads(osb_ref, ofx_ref, om_ref, g_ref, gt_ref):
    y = jnp.concatenate([osb_ref[...], ofx_ref[...], om_ref[...]], axis=1)
    msq = _sum_l2(y * y, g_ref[...]) * (1.0 / HEAD_DIM)
    rf = _sum_l3(lax.rsqrt(msq + EPS), gt_ref[...])
    return y * rf, rf


def _out_fwd(name, o_sb, o_fx, o_m, pb, ow, x, w_out, ts):
    s, d = x.shape
    g, gt = _head_maps()

    def body(osb_ref, ofx_ref, om_ref, gate_ref, ow_ref, x_ref, w_ref, g_ref, gt_ref, xo_ref, y2_ref):
        yh, _ = _normed_heads(osb_ref, ofx_ref, om_ref, g_ref, gt_ref)
        gate = gate_ref[...]
        y2 = (yh * ow_ref[...] * (gate * jax.nn.sigmoid(gate))).astype(BF16)
        y2_ref[...] = y2
        xo_ref[...] = x_ref[...] + _dot(y2, w_ref[...])

    return pl.pallas_call(
        body, name=name, grid=(s // ts,),
        in_specs=[_row_spec(ts, SB_WIDTH), _row_spec(ts, FOX_WIDTH), _row_spec(ts, MEM_WIDTH),
                  _row_spec(ts, MIX_WIDTH), _const_spec((1, MIX_WIDTH)), _row_spec(ts, d),
                  _const_spec((MIX_WIDTH, d)),
                  _const_spec((MIX_WIDTH, LANES)), _const_spec((LANES, MIX_WIDTH))],
        out_specs=[_row_spec(ts, d), _row_spec(ts, MIX_WIDTH)],
        out_shape=[jax.ShapeDtypeStruct((s, d), F32), jax.ShapeDtypeStruct((s, MIX_WIDTH), BF16)],
        compiler_params=_params("parallel"),
    )(o_sb, o_fx, o_m, pb, ow, x, w_out, g, gt)


def _row_spec(ts, w):
    return pl.BlockSpec((ts, w), lambda i: (i, 0))


def _const_spec(shape):
    return pl.BlockSpec(shape, lambda i: (0,) * len(shape))


def _out_bwd(name, dxb, o_sb, o_fx, o_m, pb, ow, w_out, ts):
    s, d = dxb.shape
    g, gt = _head_maps()

    def body(dx_ref, osb_ref, ofx_ref, om_ref, gate_ref, ow_ref, w_ref, g_ref, gt_ref, dy_ref, dgate_ref, dow_ref):
        @pl.when(pl.program_id(0) == 0)
        def _():
            dow_ref[...] = jnp.zeros_like(dow_ref)

        dy2 = _dot_nt(dx_ref[...], w_ref[...])
        yh, rf = _normed_heads(osb_ref, ofx_ref, om_ref, g_ref, gt_ref)
        gate = gate_ref[...]
        sig = jax.nn.sigmoid(gate)
        ow_v = ow_ref[...]
        dgate_ref[...] = (dy2 * (yh * ow_v) * (sig * (1.0 + gate * (1.0 - sig)))).astype(BF16)
        dn = dy2 * (gate * sig)
        dow_ref[...] += jnp.sum(dn * yh, axis=0, keepdims=True)
        dyh = dn * ow_v
        t = _sum_l2(dyh * yh, g_ref[...]) * (1.0 / HEAD_DIM)
        dy_ref[...] = rf * (dyh - yh * _sum_l3(t, gt_ref[...]))

    return pl.pallas_call(
        body, name=name, grid=(s // ts,),
        in_specs=[_row_spec(ts, d), _row_spec(ts, SB_WIDTH), _row_spec(ts, FOX_WIDTH), _row_spec(ts, MEM_WIDTH),
                  _row_spec(ts, MIX_WIDTH), _const_spec((1, MIX_WIDTH)),
                  _const_spec((MIX_WIDTH, d)),
                  _const_spec((MIX_WIDTH, LANES)), _const_spec((LANES, MIX_WIDTH))],
        out_specs=[_row_spec(ts, MIX_WIDTH), _row_spec(ts, MIX_WIDTH), _const_spec((1, MIX_WIDTH))],
        out_shape=[jax.ShapeDtypeStruct((s, MIX_WIDTH), F32), jax.ShapeDtypeStruct((s, PB), BF16),
                   jax.ShapeDtypeStruct((1, MIX_WIDTH), F32)],
        compiler_params=_params("arbitrary"),
    )(dxb, o_sb, o_fx, o_m, pb, ow, w_out, g, gt)


def _adamw(name, w, g, m, v, tr):
    def body(w_ref, g_ref, m_ref, v_ref, d_ref, m2_ref, v2_ref):
        gv = g_ref[...]
        m2 = ADAM_B1 * m_ref[...] + (1.0 - ADAM_B1) * gv
        v2 = ADAM_B2 * v_ref[...] + (1.0 - ADAM_B2) * (gv * gv)
        m_hat = m2 / (1.0 - ADAM_B1 ** ADAM_STEP)
        v_hat = v2 / (1.0 - ADAM_B2 ** ADAM_STEP)
        d_ref[...] = -ADAM_LR * (m_hat / (jnp.sqrt(v_hat) + ADAM_EPS) + ADAM_WD * w_ref[...])
        m2_ref[...] = m2
        v2_ref[...] = v2

    rest = w.shape[1:]
    spec = pl.BlockSpec((tr,) + rest, lambda i: (i,) + (0,) * len(rest))
    shp = jax.ShapeDtypeStruct(w.shape, F32)
    return pl.pallas_call(
        body, name=name, grid=(w.shape[0] // tr,), in_specs=[spec] * 4, out_specs=[spec] * 3, out_shape=[shp] * 3,
        compiler_params=_params("parallel"),
    )(w, g, m, v)


def _adamw_sharded(name, w, m, v, g_own, g_other, cvec, tr):
    depth, rows, cols = w.shape
    nt = rows // 2 // tr

    def body(c_ref, w_ref, m_ref, v_ref, *rest):
        g_refs, (g_ref, d_ref, m2_ref, v2_ref) = rest[:2 * depth], rest[2 * depth:]
        layer, mine = pl.program_id(0), pl.program_id(1) == c_ref[0]
        gv = None
        for lt in range(depth):
            cand = jnp.where(mine, g_refs[lt][...], g_refs[depth + lt][...])
            gv = cand if gv is None else jnp.where(layer == lt, cand, gv)
        m2 = ADAM_B1 * m_ref[...] + (1.0 - ADAM_B1) * gv
        v2 = ADAM_B2 * v_ref[...] + (1.0 - ADAM_B2) * (gv * gv)
        m_hat = m2 / (1.0 - ADAM_B1 ** ADAM_STEP)
        v_hat = v2 / (1.0 - ADAM_B2 ** ADAM_STEP)
        g_ref[...] = gv
        d_ref[...] = -ADAM_LR * (m_hat / (jnp.sqrt(v_hat) + ADAM_EPS) + ADAM_WD * w_ref[...])
        m2_ref[...] = m2
        v2_ref[...] = v2

    def g_map(lt, own):
        def index(l, hf, i, c_ref):
            use = jnp.logical_and(l == lt, (hf == c_ref[0]) == own)
            return jnp.where(use, i, 0), 0
        return index

    full = pl.BlockSpec((None, tr, cols), lambda l, hf, i, c_ref: (l, hf * nt + i, 0))
    g_specs = [pl.BlockSpec((tr, cols), g_map(lt, own)) for own in (True, False) for lt in range(depth)]
    shp = jax.ShapeDtypeStruct((depth, rows, cols), F32)
    return pl.pallas_call(
        body, name=name,
        grid_spec=pltpu.PrefetchScalarGridSpec(
            num_scalar_prefetch=1, grid=(depth, 2, nt), in_specs=[full] * 3 + g_specs, out_specs=[full] * 4),
        out_shape=[shp] * 4,
        compiler_params=_params("arbitrary", "arbitrary", "arbitrary"),
    )(cvec, w, m, v, *g_own, *g_other)


HBM_SPEC = pl.BlockSpec(memory_space=pltpu.HBM)


def _place():
    x, y, c = lax.axis_index("x"), lax.axis_index("y"), lax.axis_index("c")
    chips = [(1 - x, y), (x, 1 - y), (1 - x, 1 - y)]
    return x, y, c, chips


def _remote(src, dst, send_sems, recv_sems, k, to):
    return pltpu.make_async_remote_copy(src_ref=src, dst_ref=dst, send_sem=send_sems.at[k], recv_sem=recv_sems.at[k],
                                        device_id=to, device_id_type=MESH)


def _half_rows(n_rows, cc):
    rh = n_rows // 2
    return pl.ds(pl.multiple_of(cc * rh, 16), rh)


def _dma_sems(n):
    return [pltpu.SemaphoreType.DMA((n,)), pltpu.SemaphoreType.DMA((n,))]


class _Exchange:
    def __init__(self, inputs, out_shapes, n_sems, begin, relay, finish):
        self.inputs, self.out_shapes, self.n_sems = list(inputs), list(out_shapes), n_sems
        self.begin, self.relay, self.finish = begin, relay, finish

    @property
    def n(self):
        return len(self.inputs)

    def split(self, refs):
        return refs[:self.n], refs[self.n:2 * self.n], refs[2 * self.n], refs[2 * self.n + 1]


def _run_exchange(name, ex):
    def body(*refs):
        parts = ex.split(refs)
        for phase in (ex.begin, ex.relay, ex.finish):
            if phase is not None:
                phase(*parts)

    return pl.pallas_call(
        body, name=name, in_specs=[HBM_SPEC] * ex.n, out_specs=[HBM_SPEC] * ex.n, out_shape=ex.out_shapes,
        scratch_shapes=_dma_sems(ex.n_sems),
    )(*ex.inputs)


def _gather_exchange(shards):
    def ici(in_refs, out_refs, send_sems, recv_sems):
        x, y, c, chips = _place()
        return [_remote(in_ref.at[_half_rows(in_ref.shape[0], c)], out_ref.at[2 * x + y, _half_rows(in_ref.shape[0], c)],
                        send_sems, recv_sems, 6 * a + j, (cx, cy, c))
                for a, (in_ref, out_ref) in enumerate(zip(in_refs, out_refs)) for j, (cx, cy) in enumerate(chips)]

    def d2d(out_refs, send_sems, recv_sems, half_of):
        x, y, c, chips = _place()
        cps = []
        for a, out_ref in enumerate(out_refs):
            for j, (cx, cy) in enumerate(chips):
                piece = out_ref.at[2 * cx + cy, _half_rows(out_ref.shape[1], half_of(c))]
                cps.append(_remote(piece, piece, send_sems, recv_sems, 6 * a + 3 + j, (x, y, 1 - c)))
        return cps

    def begin(in_refs, out_refs, send_sems, recv_sems):
        for cp in ici(in_refs, out_refs, send_sems, recv_sems):
            cp.start()

    def relay(in_refs, out_refs, send_sems, recv_sems):
        x, y, c, chips = _place()
        for a, out_ref in enumerate(out_refs):
            for j, (cx, cy) in enumerate(chips):
                landed = out_ref.at[2 * cx + cy, _half_rows(out_ref.shape[1], c)]
                _remote(landed, landed, send_sems, recv_sems, 6 * a + j, (cx, cy, c)).wait_recv()
        for cp in d2d(out_refs, send_sems, recv_sems, lambda c_: c_):
            cp.start()

    def finish(in_refs, out_refs, send_sems, recv_sems):
        for cp in d2d(out_refs, send_sems, recv_sems, lambda c_: 1 - c_):
            cp.wait_recv()
        for cp in ici(in_refs, out_refs, send_sems, recv_sems) + d2d(out_refs, send_sems, recv_sems, lambda c_: c_):
            cp.wait_send()

    shapes = [jax.ShapeDtypeStruct((N_CHIPS,) + s_.shape, s_.dtype) for s_ in shards]
    return _Exchange(shards, shapes, 6 * len(shards), begin, relay, finish)


def _swap_exchange(g4s):
    def copies(in_refs, out_refs, send_sems, recv_sems):
        x, y, c, _ = _place()
        return [_remote(in_ref.at[:, _half_rows(in_ref.shape[1], 1 - c), :], out_ref, send_sems, recv_sems, a, (x, y, 1 - c))
                for a, (in_ref, out_ref) in enumerate(zip(in_refs, out_refs))]

    def begin(*parts):
        for cp in copies(*parts):
            cp.start()

    def finish(*parts):
        for cp in copies(*parts):
            cp.wait()

    shapes = [jax.ShapeDtypeStruct((g.shape[0], g.shape[1] // 2, g.shape[2]), g.dtype) for g in g4s]
    return _Exchange(g4s, shapes, len(g4s), begin, None, finish)


def _add_half(name, g4, r1, cvec, tr):
    n, r, w = g4.shape
    rh = r // 2
    nblk = rh // tr

    def body(c_ref, a_ref, b_ref, o_ref):
        o_ref[...] = (a_ref[...].astype(F32) + b_ref[...].astype(F32)).astype(BF16)

    return pl.pallas_call(
        body, name=name,
        grid_spec=pltpu.PrefetchScalarGridSpec(
            num_scalar_prefetch=1, grid=(n, nblk),
            in_specs=[pl.BlockSpec((None, tr, w), lambda k, i, c_ref: (k, c_ref[0] * nblk + i, 0)),
                      pl.BlockSpec((None, tr, w), lambda k, i, c_ref: (k, i, 0))],
            out_specs=pl.BlockSpec((None, tr, w), lambda k, i, c_ref: (k, i, 0))),
        out_shape=jax.ShapeDtypeStruct((n, rh, w), BF16),
        compiler_params=_params("parallel", "parallel"),
    )(cvec, g4, r1)


def _scatter_exchange(h4s):
    def sends(in_refs, out_refs, send_sems, recv_sems):
        x, y, c, chips = _place()
        return [_remote(in_ref.at[2 * cx + cy], out_ref.at[j], send_sems, recv_sems, 3 * a + j, (cx, cy, c))
                for a, (in_ref, out_ref) in enumerate(zip(in_refs, out_refs)) for j, (cx, cy) in enumerate(chips)]

    def begin(*parts):
        for cp in sends(*parts):
            cp.start()

    def finish(in_refs, out_refs, send_sems, recv_sems):
        x, y, c, chips = _place()
        for a, out_ref in enumerate(out_refs):
            for j, (cx, cy) in enumerate(chips):
                got = out_ref.at[j]
                _remote(got, got, send_sems, recv_sems, 3 * a + j, (cx, cy, c)).wait_recv()
        for cp in sends(in_refs, out_refs, send_sems, recv_sems):
            cp.wait_send()

    shapes = [jax.ShapeDtypeStruct((3,) + h.shape[1:], h.dtype) for h in h4s]
    return _Exchange(h4s, shapes, 3 * len(h4s), begin, None, finish)


def _sum_chips(name, h4, r3, mvec, tr):
    _, rh, w = h4.shape

    def body(m_ref, a_ref, b_ref, c_ref, d_ref, o_ref):
        o_ref[...] = ((a_ref[...].astype(F32) + b_ref[...].astype(F32)) + c_ref[...].astype(F32)) + d_ref[...].astype(F32)

    specs = [pl.BlockSpec((None, tr, w), lambda i, m_ref: (m_ref[0], i, 0))]
    specs += [pl.BlockSpec((None, tr, w), functools.partial(lambda k, i, m_ref: (k, i, 0), k)) for k in range(3)]
    return pl.pallas_call(
        body, name=name,
        grid_spec=pltpu.PrefetchScalarGridSpec(
            num_scalar_prefetch=1, grid=(rh // tr,), in_specs=specs,
            out_specs=pl.BlockSpec((tr, w), lambda i, m_ref: (i, 0))),
        out_shape=jax.ShapeDtypeStruct((rh, w), F32),
        compiler_params=_params("parallel"),
    )(mvec, h4, r3, r3, r3)


def _swap_reduced(name, ghs):
    n = len(ghs)

    def body(*refs):
        in_refs, out_refs, (send_sems, recv_sems) = refs[:n], refs[n:2 * n], refs[2 * n:]
        x, y, c, _ = _place()
        cps = [_remote(in_ref, out_ref, send_sems, recv_sems, a, (x, y, 1 - c))
               for a, (in_ref, out_ref) in enumerate(zip(in_refs, out_refs))]
        for cp in cps:
            cp.start()
        for cp in cps:
            cp.wait()

    return pl.pallas_call(
        body, name=name, in_specs=[HBM_SPEC] * n, out_specs=[HBM_SPEC] * n,
        out_shape=[jax.ShapeDtypeStruct(g.shape, g.dtype) for g in ghs],
        scratch_shapes=_dma_sems(n),
    )(*ghs)


def _small_update(name, partials, weights, moments1, moments2):
    n = len(partials)
    width = max(p.shape[1] for p in partials)
    starts, at = [], 0
    for p in partials:
        starts.append(at)
        at += p.shape[0]
    rows = -(-at // 8) * 8
    has_w = [w is not None for w in weights]
    n_w = sum(has_w)

    def body(*refs):
        p_refs = refs[:n]
        w_refs, m_refs, v_refs = refs[n:n + n_w], refs[n + n_w:n + 2 * n_w], refs[n + 2 * n_w:n + 3 * n_w]
        outs = refs[n + 3 * n_w:-4]
        g_refs, upd_refs = outs[:n], outs[n:]
        vec, buf, send_sems, recv_sems = refs[-4:]
        x, y, c, _ = _place()
        me = 4 * x + 2 * y + c
        vec[...] = jnp.zeros_like(vec)
        for p_ref, r0 in zip(p_refs, starts):
            vec[r0:r0 + p_ref.shape[0], 0:p_ref.shape[1]] = p_ref[...]
        buf[me] = vec[...]
        flips = [(fx, fy, fc) for fx in (0, 1) for fy in (0, 1) for fc in (0, 1)][1:]
        peers = [(x + fx - 2 * x * fx, y + fy - 2 * y * fy, c + fc - 2 * c * fc) for fx, fy, fc in flips]
        sends = [_remote(vec, buf.at[me], send_sems, recv_sems, k, peer) for k, peer in enumerate(peers)]
        for cp in sends:
            cp.start()
        for k, (px, py, pc) in enumerate(peers):
            got = buf.at[4 * px + 2 * py + pc]
            _remote(got, got, send_sems, recv_sems, k, (px, py, pc)).wait_recv()
        for cp in sends:
            cp.wait_send()
        total = buf[0]
        for dev in range(1, N_DEV):
            total = total + buf[dev]
        k = 0
        for a in range(n):
            r, w = g_refs[a].shape
            g = total[starts[a]:starts[a] + r, 0:w]
            g_refs[a][...] = g
            if has_w[a]:
                m2 = ADAM_B1 * m_refs[k][...] + (1.0 - ADAM_B1) * g
                v2 = ADAM_B2 * v_refs[k][...] + (1.0 - ADAM_B2) * (g * g)
                m_hat = m2 / (1.0 - ADAM_B1 ** ADAM_STEP)
                v_hat = v2 / (1.0 - ADAM_B2 ** ADAM_STEP)
                upd_refs[3 * k][...] = -ADAM_LR * (m_hat / (jnp.sqrt(v_hat) + ADAM_EPS) + ADAM_WD * w_refs[k][...])
                upd_refs[3 * k + 1][...] = m2
                upd_refs[3 * k + 2][...] = v2
                k += 1

    ws = [w for w in weights if w is not None]
    g_shapes = [jax.ShapeDtypeStruct(p.shape if w is None else w.shape, F32) for p, w in zip(partials, weights)]
    u_shapes = [jax.ShapeDtypeStruct(w.shape, F32) for w in ws for _ in range(3)]
    vm = pl.BlockSpec(memory_space=pltpu.VMEM)
    n_args = n + 3 * n_w
    outs = pl.pallas_call(
        body, name=name, in_specs=[vm] * n_args, out_specs=[vm] * (n + 3 * n_w), out_shape=g_shapes + u_shapes,
        scratch_shapes=[pltpu.VMEM((rows, width), F32), pltpu.VMEM((N_DEV, rows, width), F32),
                        pltpu.SemaphoreType.DMA((7,)), pltpu.SemaphoreType.DMA((7,))],
    )(*partials, *ws, *[m for m in moments1 if m is not None], *[v for v in moments2 if v is not None])
    return outs[:n], outs[n:]


GATE_COL = 3 * SB_WIDTH + 3 * FOX_WIDTH + FOX_HEADS + MEM_WIDTH
FL_COL = QKV_WIDTH


GROUP_A_COLS = [(0, QKV_WIDTH), (FL_COL + FOX_HEADS, MEM_WIDTH)]
GROUP_B_COLS = [(GATE_COL, MIX_WIDTH), (FL_COL, FOX_HEADS)]


def _group_from_shards(shard_of, cw, spans, pad):
    parts = []
    for lo, width in spans:
        hi = lo + width
        for j in range(N_CHIPS):
            a, b = max(lo, j * cw), min(hi, (j + 1) * cw)
            if a < b:
                parts.append(shard_of(j)[:, a - j * cw:b - j * cw])
    if pad:
        parts.append(jnp.zeros((parts[0].shape[0], pad), parts[0].dtype))
    return jnp.concatenate(parts, axis=1)


def _shard_from_groups(ga, gb, j, cw):
    lo, hi = j * cw, (j + 1) * cw
    placed = []
    for grp, spans in ((ga, GROUP_A_COLS), (gb, GROUP_B_COLS)):
        at = 0
        for first, width in spans:
            a, b = max(lo, first), min(hi, first + width)
            if a < b:
                placed.append((a, grp[:, at + a - first:at + b - first]))
            at += width
    return jnp.concatenate([p for _, p in sorted(placed, key=lambda t: t[0])], axis=1)


def _tile_of(n, cap, unit):
    if n <= cap:
        return n
    best = None
    for t in range(unit, cap + 1, unit):
        if n % t == 0:
            best = t
    assert best is not None, (n, cap, unit)
    return best


def _column_major_rows(a):
    dp, r, c = a.shape
    return a.transpose(2, 0, 1).reshape(c, dp, r // LANES, LANES).transpose(0, 2, 1, 3).reshape(-1, 8, LANES)


def _from_column_major_rows(b, shape):
    dp, r, c = shape
    return b.reshape(c, r // LANES, dp, LANES).transpose(0, 2, 1, 3).reshape(c, dp, r).transpose(1, 2, 0)


def _pack_small(parts):
    rows = []
    for p in parts:
        f = p.reshape(-1).astype(F32)
        f = jnp.pad(f, (0, (-f.shape[0]) % LANES))
        rows.append(f.reshape(-1, LANES))
    out = jnp.concatenate(rows, axis=0)
    return jnp.pad(out, ((0, (-out.shape[0]) % 8), (0, 0)))


def _unpack_small(packed, shapes):
    outs, r = [], 0
    for shp in shapes:
        n = 1
        for s_ in shp:
            n *= s_
        nr = -(-n // LANES)
        outs.append(packed[r:r + nr].reshape(-1)[:n].reshape(shp))
        r += nr
    return outs


def kernel(x, mem, norm_w, w_in, b_forget, mem_norm_w, w_mem_kv, out_norm_w, w_out, final_norm_w, loss_target, m_norm_w, m_w_in, m_b_forget, m_mem_norm_w, m_w_mem_kv, m_out_norm_w, m_w_out, m_final_norm_w, v_norm_w, v_w_in, v_b_forget, v_mem_norm_w, v_w_mem_kv, v_out_norm_w, v_w_out, v_final_norm_w):
    xs = x[0]
    mems = mem[0]
    target = loss_target[0]
    s, d = xs.shape
    depth = norm_w.shape[0]
    nb = s // TILE
    ts = _tile_of(s, 256, 8)
    big = (w_in, w_mem_kv, w_out)
    core = lax.axis_index("c")
    chip = 2 * lax.axis_index("x") + lax.axis_index("y")
    cvec = core.astype(jnp.int32).reshape(1)
    mvec = chip.astype(jnp.int32).reshape(1)
    cw = w_in.shape[2]

    own_w = [[a[l].astype(BF16) for a in big] for l in range(depth)]

    def lay_out_in(own, got):
        shard_of = lambda j: jnp.where(chip == j, own, got[j])
        return (_group_from_shards(shard_of, cw, GROUP_A_COLS, 0),
                _group_from_shards(shard_of, cw, GROUP_B_COLS, LANES - FOX_HEADS))

    def lay_out_rows(own, got):
        full = jnp.where(lax.broadcasted_iota(jnp.int32, got.shape, 0) == chip, own[None], got)
        return full.reshape(-1, full.shape[2])

    w_in_groups = [lay_out_in(own_w[0][0], _run_exchange("gather_weights0", _gather_exchange(own_w[0][:1]))[0])]
    layer_w = []

    tm = _tile_of(s, 256, 8)
    fl_block = MIX_WIDTH // LANES

    saved = []
    cur = xs
    for l in range(depth):
        wa, wb = w_in_groups[l]
        h = _rms_fwd(f"rms_fwd{l}", cur, norm_w[l][None], ts)
        pa = _mm(f"inproj_a{l}", h, wa, "nn", tm, _tile_of(PA, 1664, LANES), BF16)
        pb = _mm(f"inproj_b{l}", h, wb, "nn", tm, PB, F32)
        bpad = jnp.pad(b_forget[l], (0, LANES - FOX_HEADS))[None]
        ccol4, crow4 = _gate_fwd(f"gate_fwd{l}", pb, bpad, fl_block)
        more = l + 1 < depth
        riding = own_w[l][1:] + (own_w[l + 1][:1] if more else [])
        o_sb, got = _sb_fwd(f"sb_fwd{l}", pa, 0, carried=_gather_exchange(riding))
        wkv, wout = lay_out_rows(own_w[l][1], got[0]), lay_out_rows(own_w[l][2], got[1])
        layer_w.append((wa, wb, wkv, wout))
        if more:
            w_in_groups.append(lay_out_in(own_w[l + 1][0], got[2]))
        o_fx, lse_fx, _ = _fox_fwd(f"fox_fwd{l}", pa, 3 * SB_WIDTH, ccol4, crow4)
        mn = _rms_fwd(f"mem_rms{l}", mems, mem_norm_w[l][None], mems.shape[0])
        mkv = _mm(f"mem_kv{l}", mn, wkv, "nn", mems.shape[0], 2 * MEM_WIDTH, BF16)
        o_m, lse_m = _mem_fwd(f"mem_fwd{l}", pa, mkv)
        nxt, y2 = _out_fwd(f"out_fwd{l}", o_sb, o_fx, o_m, pb, out_norm_w[l][None], cur, wout, ts)
        saved.append((cur, h, pa, pb, bpad, ccol4, crow4, o_sb, o_fx, lse_fx, mn, mkv, o_m, lse_m, y2))
        cur = nxt

    loss_v, dx, dxb, g_final = _final_loss("final_loss", cur, final_norm_w[None], target, ts)

    g_norm, g_b, g_memnorm, g_outnorm = [None] * depth, [None] * depth, [None] * depth, [None] * depth
    g_wa, g_wb, g_wkv, g_wout = [None] * depth, [None] * depth, [None] * depth, [None] * depth
    g_own = [[None] * depth for _ in big]
    g_other = [[None] * depth for _ in big]

    def swap_of(jobs):
        return _swap_exchange([g for _, _, g, _ in jobs])

    def chip_sums(jobs, got):
        return [(lr, k, _add_half(f"grad_add_half{lr}_{k}", g, r_, cvec, t_), t_) for (lr, k, g, t_), r_ in zip(jobs, got)]

    def reduce_at_owner(tag, jobs, from_chips):
        halves = [_sum_chips(f"grad_sum_chips{lr}_{k}", h_, r_, mvec, t_) for (lr, k, h_, t_), r_ in zip(jobs, from_chips)]
        others = _swap_reduced(f"grad_swap_reduced{tag}", halves)
        for (lr, k, _, _), mine, other in zip(jobs, halves, others):
            g_own[k][lr], g_other[k][lr] = mine, other

    def job(lr, k, g4):
        return lr, k, g4, _tile_of(g4.shape[1] // 2, 256, 16)

    pending = []
    for l in reversed(range(depth)):
        xin, h, pa, pb, bpad, ccol4, crow4, o_sb, o_fx, lse_fx, mn, mkv, o_m, lse_m, y2 = saved[l]
        wa, wb, wkv, wout = layer_w[l]
        dy, dgate, g_outnorm[l] = _out_bwd(f"out_bwd{l}", dxb, o_sb, o_fx, o_m, pb, out_norm_w[l][None], wout, ts)
        g_wout[l] = _mm(f"dw_out{l}", y2, dxb, "tn", _tile_of(MIX_WIDTH, 640, LANES), d, F32)
        dq_m, dk_m, dv_m = _mem_bwd(f"mem_bwd{l}", pa, mkv, o_m, lse_m, dy, SB_WIDTH + FOX_WIDTH)
        dmkv = jnp.concatenate([dk_m, dv_m], axis=1)
        g_wkv[l] = _mm(f"dw_kv{l}", mn, dmkv, "tn", d, 2 * MEM_WIDTH, F32)
        dmn = _mm(f"dmem{l}", dmkv, wkv, "nt", mems.shape[0], d, F32)
        g_memnorm[l] = _rms_wgrad(f"mem_norm_grad{l}", mems, dmn)
        small = [job(l, 1, g_wkv[l].reshape(N_CHIPS, -1, g_wkv[l].shape[1])), job(l, 2, g_wout[l].reshape(N_CHIPS, -1, d))]
        (dq_fx, dk_fx, dv_fx, cs4), got = _fox_bwd(f"fox_bwd{l}", pa, 3 * SB_WIDTH, ccol4, crow4, o_fx, lse_fx, dy,
                                                    SB_WIDTH, carried=swap_of(small))
        pending += chip_sums(small, got)
        (dq_sb, dk_sb, dv_sb), from_chips = _sb_bwd(f"sb_bwd{l}", pa, 0, dy, 0,
                                                   carried=_scatter_exchange([j[2] for j in pending]))
        reduce_at_owner(f"{l}s", pending, from_chips)
        dpb, g_b[l] = _gate_bwd(f"gate_bwd{l}", pb, bpad, cs4, fl_block, dgate)
        dpa = jnp.concatenate([dq_sb, dk_sb, dv_sb, dq_fx, dk_fx, dv_fx, dq_m], axis=1)
        tw = _tile_of(d, 512, LANES)
        g_wa[l] = _mm(f"dw_in_a{l}", h, dpa, "tn", tw, _tile_of(PA, 1664, LANES), BF16)
        g_wb[l] = _mm(f"dw_in_b{l}", h, dpb, "tn", tw, PB, BF16)
        g4_in = jnp.stack([_shard_from_groups(g_wa[l], g_wb[l], j, cw) for j in range(N_CHIPS)])
        w_in_job = [job(l, 0, g4_in)]
        if l > 0:
            dx, dxb, g_norm[l], got = _inproj_bwd(f"inproj_bwd{l}", dpa, dpb, wa, wb, xin, norm_w[l][None], dx, ts,
                                                  carried=swap_of(w_in_job))
            pending = chip_sums(w_in_job, got)
        else:
            pending = chip_sums(w_in_job, _run_exchange("grad_swap_halves_last", swap_of(w_in_job)))
            dx, dxb, g_norm[l], from_chips = _inproj_bwd(f"inproj_bwd{l}", dpa, dpb, wa, wb, xin, norm_w[l][None], dx, ts,
                                                          carried=_scatter_exchange([j[2] for j in pending]))
            reduce_at_owner("last", pending, from_chips)

    small_w = [norm_w, b_forget, mem_norm_w, out_norm_w, final_norm_w]
    small_m = [m_norm_w, m_b_forget, m_mem_norm_w, m_out_norm_w, m_final_norm_w]
    small_v = [v_norm_w, v_b_forget, v_mem_norm_w, v_out_norm_w, v_final_norm_w]
    rows2 = lambda a: a.reshape(-1, a.shape[-1])
    partials = [jnp.concatenate(g_norm, axis=0), jnp.concatenate(g_b, axis=0), jnp.concatenate(g_memnorm, axis=0),
                jnp.concatenate(g_outnorm, axis=0), g_final, loss_v]
    sums, updates = _small_update("small_update", partials, [rows2(a) for a in small_w] + [None],
                                  [rows2(a) for a in small_m] + [None], [rows2(a) for a in small_v] + [None])
    small_grads = [g.reshape(a.shape) for g, a in zip(sums, small_w)]
    loss = sums[-1][0, 0]
    small_delta, small_m2, small_v2 = ([updates[3 * k + t].reshape(a.shape) for k, a in enumerate(small_w)]
                                       for t in range(3))
    big_grads, big_delta, big_m2, big_v2 = [], [], [], []
    for k, (nm, w_, m_, v_) in enumerate(zip(("w_in", "w_mem_kv", "w_out"), big, (m_w_in, m_w_mem_kv, m_w_out),
                                             (v_w_in, v_w_mem_kv, v_w_out))):
        if w_.shape[2] % LANES:
            g_full = jnp.stack([jnp.concatenate([jnp.where(core == 0, go, gt), jnp.where(core == 0, gt, go)], axis=0)
                                for go, gt in zip(g_own[k], g_other[k])])
            w_p, g_p, m_p, v_p = (_column_major_rows(a) for a in (w_, g_full, m_, v_))
            outs = _adamw(f"adamw_{nm}", w_p, g_p, m_p, v_p, _tile_of(w_p.shape[0], 600, 1))
            outs = [_from_column_major_rows(o, w_.shape) for o in (g_p, *outs)]
        else:
            outs = _adamw_sharded(f"adamw_{nm}", w_, m_, v_, g_own[k], g_other[k], cvec,
                                  _tile_of(w_.shape[1] // 2, 256, 8))
        for lst, o in zip((big_grads, big_delta, big_m2, big_v2), outs):
            lst.append(o)

    def order(sm, bg):
        return [sm[0], bg[0], sm[1], sm[2], bg[1], sm[3], bg[2], sm[4]]

    return (loss, dx[None], *order(small_grads, big_grads), *order(small_delta, big_delta),
            *order(small_m2, big_m2), *order(small_v2, big_v2))
```

```python
import functools

import jax
import jax.numpy as jnp
from jax import lax
from jax.experimental import pallas as pl
from jax.experimental.pallas import tpu as pltpu

F32 = jnp.float32
BF16 = jnp.bfloat16

HEAD_DIM = 64
SB_WIDTH = 512
FOX_WIDTH = 512
FOX_HEADS = 8
MEM_WIDTH = 256
MIX_WIDTH = SB_WIDTH + FOX_WIDTH + MEM_WIDTH
TOTAL_HEADS = MIX_WIDTH // HEAD_DIM
IN_WIDTH = 3 * SB_WIDTH + 3 * FOX_WIDTH + FOX_HEADS + MEM_WIDTH + MIX_WIDTH
LANES = 128
QKV_WIDTH = 3 * SB_WIDTH + 3 * FOX_WIDTH
PA = QKV_WIDTH + MEM_WIDTH
PB = LANES + MIX_WIDTH
EPS = 1e-6
SCALE = HEAD_DIM ** -0.5
TILE = 256
SB_GROUP = 4
SB_LANES = SB_GROUP * HEAD_DIM
FOX_GROUP = 4
FOX_LANES = FOX_GROUP * HEAD_DIM
NEG_INF = float("-inf")
MASKED = -1e30

ADAM_LR = 0.001
ADAM_B1 = 0.9
ADAM_B2 = 0.999
ADAM_EPS = 1e-08
ADAM_WD = 0.01
ADAM_STEP = 10

N_CHIPS = 4
N_DEV = 8
VMEM_LIMIT = 48 * 1024 * 1024
MESH = pl.DeviceIdType.MESH


def _params(*sem):
    return pltpu.CompilerParams(dimension_semantics=tuple(sem), vmem_limit_bytes=VMEM_LIMIT)


def _dot(a, b):
    return jnp.dot(a, b, preferred_element_type=F32)


def _dot_nt(a, b):
    return lax.dot_general(a, b, (((1,), (1,)), ((), ())), preferred_element_type=F32)


def _dot_tn(a, b):
    return lax.dot_general(a, b, (((0,), (0,)), ((), ())), preferred_element_type=F32)


def _split2(x):
    hi = x.astype(BF16)
    lo = (x - hi.astype(F32)).astype(BF16)
    return hi, lo


def _split3(x):
    hi = x.astype(BF16)
    r = x - hi.astype(F32)
    mid = r.astype(BF16)
    lo = (r - mid.astype(F32)).astype(BF16)
    return hi, mid, lo


def _sum_l2(x, u):
    hi, lo = _split2(x)
    return _dot(hi, u) + _dot(lo, u)


def _sum_l3(x, u):
    hi, mid, lo = _split3(x)
    return _dot(hi, u) + _dot(mid, u) + _dot(lo, u)


def _sum_r3(u, x):
    hi, mid, lo = _split3(x)
    return _dot(u, hi) + _dot(u, mid) + _dot(u, lo)


def _softplus(z):
    return jnp.maximum(z, 0.0) + jnp.log1p(jnp.exp(-jnp.abs(z)))


def _tri(n, pred):
    r = lax.broadcasted_iota(jnp.int32, (n, n), 0)
    c = lax.broadcasted_iota(jnp.int32, (n, n), 1)
    return jnp.where(pred(r, c), 1.0, 0.0).astype(BF16)


def _rows(ref, j, n=TILE):
    return pl.ds(pl.multiple_of(j * n, n), n)


def _mm(name, a, b, mode, tm, tn, out_dtype, res=None, a_lead=(), b_lead=()):
    a2, b2 = a.shape[len(a_lead):], b.shape[len(b_lead):]
    if mode == "tn":
        k, m = a2
    else:
        m, k = a2
    n = b2[0] if mode == "nt" else b2[1]
    assert m % tm == 0 and n % tn == 0, (name, m, tm, n, tn)
    na, nb = (None,) * len(a_lead), (None,) * len(b_lead)
    if mode == "tn":
        a_spec = pl.BlockSpec(na + (k, tm), lambda j, i: a_lead + (0, i))
    else:
        a_spec = pl.BlockSpec(na + (tm, k), lambda j, i: a_lead + (i, 0))
    if mode == "nt":
        b_spec = pl.BlockSpec(nb + (tn, k), lambda j, i: b_lead + (j, 0))
    else:
        b_spec = pl.BlockSpec(nb + (k, tn), lambda j, i: b_lead + (0, j))
    o_spec = pl.BlockSpec((tm, tn), lambda j, i: (i, j))
    dot = {"nn": _dot, "nt": _dot_nt, "tn": _dot_tn}[mode]

    def body(a_ref, b_ref, *rest):
        o_ref = rest[-1]
        acc = dot(a_ref[...].astype(BF16), b_ref[...].astype(BF16))
        if res is not None:
            acc = acc + rest[0][...]
        o_ref[...] = acc.astype(o_ref.dtype)

    args, specs = [a, b], [a_spec, b_spec]
    if res is not None:
        args.append(res)
        specs.append(o_spec)
    return pl.pallas_call(
        body, name=name, grid=(n // tn, m // tm), in_specs=specs, out_specs=o_spec,
        out_shape=jax.ShapeDtypeStruct((m, n), out_dtype),
        compiler_params=_params("parallel", "parallel"),
    )(*args)


def _rms_fwd(name, x, g, ts):
    s, d = x.shape

    def body(x_ref, g_ref, o_ref):
        xf = x_ref[...]
        r = lax.rsqrt(jnp.mean(xf * xf, axis=1, keepdims=True) + EPS)
        o_ref[...] = (xf * r * g_ref[...]).astype(BF16)

    return pl.pallas_call(
        body, name=name, grid=(s // ts,),
        in_specs=[pl.BlockSpec((ts, d), lambda i: (i, 0)), pl.BlockSpec((1, d), lambda i: (0, 0))],
        out_specs=pl.BlockSpec((ts, d), lambda i: (i, 0)),
        out_shape=jax.ShapeDtypeStruct((s, d), BF16),
        compiler_params=_params("parallel"),
    )(x, g)


def _inproj_bwd(name, dpa, dpb, wa, wb, x, g, dres, ts, carried=None):
    s, d = x.shape

    def body(dpa_ref, dpb_ref, wa_ref, wb_ref, x_ref, g_ref, dres_ref, dx_ref, dxb_ref, dg_ref):
        @pl.when(pl.program_id(1) == 0)
        def _():
            dg_ref[...] = jnp.zeros_like(dg_ref)

        dhf = _dot_nt(dpa_ref[...], wa_ref[...]) + _dot_nt(dpb_ref[...], wb_ref[...])
        xf = x_ref[...]
        r = lax.rsqrt(jnp.mean(xf * xf, axis=1, keepdims=True) + EPS)
        xh = xf * r
        dg_ref[...] += jnp.sum(dhf * xh, axis=0, keepdims=True)
        dxh = dhf * g_ref[...]
        m = jnp.mean(dxh * xh, axis=1, keepdims=True)
        dx = r * (dxh - xh * m) + dres_ref[...]
        dx_ref[...] = dx
        dxb_ref[...] = dx.astype(BF16)

    row = lambda w: pl.BlockSpec((ts, w), lambda p, i: (i, 0))
    whole = lambda a: pl.BlockSpec(a.shape, lambda p, i: (0, 0))
    outs = _pair_grid_call(
        name, body, s // ts,
        in_specs=[row(dpa.shape[1]), row(dpb.shape[1]), whole(wa), whole(wb), row(d), whole(g), row(d)],
        out_specs=[row(d), row(d), pl.BlockSpec((1, d), lambda p, i: (0, 0))],
        out_shape=[jax.ShapeDtypeStruct((s, d), F32), jax.ShapeDtypeStruct((s, d), BF16),
                   jax.ShapeDtypeStruct((1, d), F32)],
        scratch=[], args=(dpa, dpb, wa, wb, x, g, dres), carried=carried, groups=1)
    return outs[0], outs[1], outs[2], outs[3:]


def _rms_wgrad(name, x, dh):
    m_, d = x.shape

    def body(x_ref, dh_ref, dg_ref):
        xf = x_ref[...]
        r = lax.rsqrt(jnp.mean(xf * xf, axis=1, keepdims=True) + EPS)
        dg_ref[...] = jnp.sum(dh_ref[...] * xf * r, axis=0, keepdims=True)

    return pl.pallas_call(
        body, name=name, out_shape=jax.ShapeDtypeStruct((1, d), F32),
    )(x, dh)


def _final_loss(name, x, g, target, ts):
    s, d = x.shape

    def body(x_ref, g_ref, t_ref, loss_ref, dx_ref, dxb_ref, dg_ref):
        @pl.when(pl.program_id(0) == 0)
        def _():
            dg_ref[...] = jnp.zeros_like(dg_ref)
            loss_ref[...] = jnp.zeros_like(loss_ref)

        xf = x_ref[...]
        gw = g_ref[...]
        r = lax.rsqrt(jnp.mean(xf * xf, axis=1, keepdims=True) + EPS)
        xh = xf * r
        e = xh * gw - t_ref[...]
        part = 0.5 * jnp.sum(jnp.mean(e * e, axis=1, keepdims=True), axis=0, keepdims=True)
        loss_ref[...] += jnp.broadcast_to(part, loss_ref.shape)
        dy = e * (1.0 / d)
        dg_ref[...] += jnp.sum(dy * xh, axis=0, keepdims=True)
        dxh = dy * gw
        m = jnp.mean(dxh * xh, axis=1, keepdims=True)
        dx = r * (dxh - xh * m)
        dx_ref[...] = dx
        dxb_ref[...] = dx.astype(BF16)

    row = pl.BlockSpec((ts, d), lambda i: (i, 0))
    vec = pl.BlockSpec((1, d), lambda i: (0, 0))
    lvec = pl.BlockSpec((1, LANES), lambda i: (0, 0))
    return pl.pallas_call(
        body, name=name, grid=(s // ts,), in_specs=[row, vec, row], out_specs=[lvec, row, row, vec],
        out_shape=[jax.ShapeDtypeStruct((1, LANES), F32), jax.ShapeDtypeStruct((s, d), F32),
                   jax.ShapeDtypeStruct((s, d), BF16), jax.ShapeDtypeStruct((1, d), F32)],
        compiler_params=_params("arbitrary"),
    )(x, g, target)


def _gate_fwd(name, pb, bpad, fl_block):
    s = pb.shape[0]
    nb = s // TILE
    fg = FOX_HEADS // FOX_GROUP

    def body(fl_ref, b_ref, ccol_ref, crow_ref, carry):
        @pl.when(pl.program_id(0) == 0)
        def _():
            carry[...] = jnp.zeros_like(carry)

        u = fl_ref[...] + b_ref[...]
        lf = jnp.minimum(u, 0.0) - jnp.log1p(jnp.exp(-jnp.abs(u)))
        lower = _tri(TILE, lambda r, c: c <= r)
        c = _sum_r3(lower, lf) + carry[0:1, :]
        for grp in range(fg):
            ccol_ref[grp] = jnp.concatenate(
                [jnp.broadcast_to(c[:, grp * FOX_GROUP + hh:grp * FOX_GROUP + hh + 1], (TILE, HEAD_DIM))
                 for hh in range(FOX_GROUP)], axis=1)
        crow_ref[0] = c.T[0:8, :]
        carry[...] = jnp.broadcast_to(c[TILE - 1:TILE, :], carry.shape)

    return pl.pallas_call(
        body, name=name, grid=(nb,),
        in_specs=[pl.BlockSpec((TILE, LANES), lambda i: (i, fl_block)), pl.BlockSpec((1, LANES), lambda i: (0, 0))],
        out_specs=[pl.BlockSpec((fg, TILE, FOX_LANES), lambda i: (0, i, 0)), pl.BlockSpec((1, 8, TILE), lambda i: (i, 0, 0))],
        out_shape=[jax.ShapeDtypeStruct((fg, s, FOX_LANES), F32), jax.ShapeDtypeStruct((nb, 8, TILE), F32)],
        scratch_shapes=[pltpu.VMEM((8, LANES), F32)],
        compiler_params=_params("arbitrary"),
    )(pb, bpad)


def _gate_bwd(name, pb, bpad, colsum, fl_block, dpb):
    s = pb.shape[0]
    nb = s // TILE

    def body(fl_ref, b_ref, cs_ref, dpb_ref, dl_ref, db_ref, carry):
        @pl.when(pl.program_id(0) == 0)
        def _():
            carry[...] = jnp.zeros_like(carry)
            db_ref[...] = jnp.zeros_like(db_ref)

        upper = _tri(TILE, lambda r, c: r >= c)
        rsum = _sum_l3(cs_ref[0], upper) + carry[:, 0:1]
        carry[...] = jnp.broadcast_to(rsum[:, 0:1], carry.shape)
        full = jnp.concatenate([rsum, jnp.zeros((LANES - 8, TILE), F32)], axis=0)
        dlf = -full.T
        u = fl_ref[...] + b_ref[...]
        dlogit = dlf * (1.0 - jax.nn.sigmoid(u))
        dl_ref[...] = dlogit.astype(BF16)
        db_ref[...] += jnp.sum(dlogit, axis=0, keepdims=True)

    logits_block = pl.BlockSpec((TILE, LANES), lambda i: (nb - 1 - i, fl_block))
    return pl.pallas_call(
        body, name=name, grid=(nb,),
        in_specs=[logits_block, pl.BlockSpec((1, LANES), lambda i: (0, 0)),
                  pl.BlockSpec((1, 8, TILE), lambda i: (nb - 1 - i, 0, 0)), pl.BlockSpec(memory_space=pl.ANY)],
        out_specs=[logits_block, pl.BlockSpec((1, LANES), lambda i: (0, 0))],
        out_shape=[jax.ShapeDtypeStruct(dpb.shape, BF16), jax.ShapeDtypeStruct((1, LANES), F32)],
        scratch_shapes=[pltpu.VMEM((8, LANES), F32)], input_output_aliases={3: 0},
        compiler_params=_params("arbitrary"),
    )(pb, bpad, colsum, dpb)


def _head_slices(hh):
    return slice(HEAD_DIM * hh, HEAD_DIM * (hh + 1))


def _scaled_q(q_ref, sl, scale=SCALE):
    return (q_ref[:, sl].astype(F32) * scale).astype(BF16)


def _neg_abs(x):
    sign = jnp.uint32(0x80000000)
    return lax.bitcast_convert_type(lax.bitcast_convert_type(x, jnp.uint32) | sign, F32)


def _sb_tile(qn, kj, carry, strict, u_after, diag):
    nz = _dot_nt(qn, kj)
    lf = jnp.minimum(nz, 0.0) - jnp.log(1.0 + jnp.exp(_neg_abs(nz)))
    lsig = lf - nz
    if diag:
        lf = jnp.where(strict, lf, 0.0)
    sx = _dot(lf.astype(BF16), u_after)
    a = jnp.exp(lsig + sx + carry)
    if diag:
        a = jnp.where(strict, a, 0.0)
    return lsig, a, carry + sx[:, 0:1] + lf[:, 0:1]


def _pair_grid_call(name, body, nb, in_specs, out_specs, out_shape, scratch, args, carried=None, groups=4):
    if carried is None:
        return pl.pallas_call(
            body, name=name, grid=(groups, nb), in_specs=in_specs, out_specs=out_specs, out_shape=out_shape,
            scratch_shapes=scratch, compiler_params=_params("arbitrary", "arbitrary"),
        )(*args)
    n_in, n_out, n_ex = len(in_specs), len(out_specs), carried.n

    def body_with_copies(*refs):
        own_in, ex_in = refs[:n_in], refs[n_in:n_in + n_ex]
        own_out = refs[n_in + n_ex:n_in + n_ex + n_out]
        ex_out = refs[n_in + n_ex + n_out:n_in + 2 * n_ex + n_out]
        own_scratch, sems = refs[n_in + 2 * n_ex + n_out:-2], refs[-2:]
        parts = (ex_in, ex_out, sems[0], sems[1])
        p, i = pl.program_id(0), pl.program_id(1)
        pl.when(jnp.logical_and(p == 0, i == 0))(lambda: carried.begin(*parts))
        if carried.relay is not None:
            pl.when(jnp.logical_and(p == groups - 1, i == max(nb - 2, 0)))(lambda: carried.relay(*parts))
        body(*own_in, *own_out, *own_scratch)
        pl.when(jnp.logical_and(p == groups - 1, i == nb - 1))(lambda: carried.finish(*parts))

    return pl.pallas_call(
        body_with_copies, name=name, grid=(groups, nb), in_specs=list(in_specs) + [HBM_SPEC] * n_ex,
        out_specs=list(out_specs) + [HBM_SPEC] * n_ex, out_shape=list(out_shape) + carried.out_shapes,
        scratch_shapes=list(scratch) + _dma_sems(carried.n_sems),
        compiler_params=_params("arbitrary", "arbitrary"),
    )(*args, *carried.inputs)


def _sb_fwd(name, pa, col0, carried=None):
    s = pa.shape[0]
    nb = s // TILE
    cb = col0 // SB_LANES
    kb = SB_WIDTH // SB_LANES

    def body(q_ref, k_ref, v_ref, o_ref, lsig_s, lf_s):
        i = pl.program_id(1)
        r = lax.broadcasted_iota(jnp.int32, (TILE, TILE), 0)
        c = lax.broadcasted_iota(jnp.int32, (TILE, TILE), 1)
        strict = c < r
        u_after = _tri(TILE, lambda rr, cc: rr > cc)
        qs = [_scaled_q(q_ref, _head_slices(hh), -SCALE) for hh in range(SB_GROUP)]

        def neg_z(j):
            kblk = k_ref[_rows(k_ref, j), :]
            return [_dot_nt(qs[hh], kblk[:, _head_slices(hh)]) for hh in range(SB_GROUP)]

        def scores(nzs, slot, diag):
            for hh, nz in enumerate(nzs):
                lf = jnp.minimum(nz, 0.0) - jnp.log(1.0 + jnp.exp(_neg_abs(nz)))
                lsig = lf - nz
                if diag:
                    lf = jnp.where(strict, lf, 0.0)
                    lsig = jnp.where(strict, lsig, MASKED)
                lsig_s[slot, hh] = lsig
                lf_s[slot, hh] = lf.astype(BF16)

        def weigh(j, slot, state):
            vblk = v_ref[_rows(v_ref, j), :]
            new = []
            for hh in range(SB_GROUP):
                carry, acc = state[hh]
                lfb = lf_s[slot, hh]
                sx = _dot(lfb, u_after)
                a = jnp.exp(lsig_s[slot, hh] + sx + carry)
                new.append((carry + sx[:, 0:1] + lfb[:, 0:1].astype(F32),
                            acc + _dot(a.astype(BF16), vblk[:, _head_slices(hh)])))
            return tuple(new)

        def step(t, state):
            state = weigh(i - t + 1, (t - 1) % 2, state)
            scores(neg_z(i - t), t % 2, False)
            return state

        zero = (jnp.zeros((TILE, 1), F32), jnp.zeros((TILE, HEAD_DIM), F32))
        scores(neg_z(i), 0, True)
        state = lax.fori_loop(1, i + 1, step, (zero,) * SB_GROUP)
        state = weigh(0, i % 2, state)
        o_ref[...] = jnp.concatenate([st[1] for st in state], axis=1)

    outs = _pair_grid_call(
        name, body, nb,
        in_specs=[pl.BlockSpec((TILE, SB_LANES), lambda p, i: (i, cb + p)),
                  pl.BlockSpec((s, SB_LANES), lambda p, i: (0, cb + kb + p)),
                  pl.BlockSpec((s, SB_LANES), lambda p, i: (0, cb + 2 * kb + p))],
        out_specs=[pl.BlockSpec((TILE, SB_LANES), lambda p, i: (i, p))],
        out_shape=[jax.ShapeDtypeStruct((s, SB_WIDTH), F32)],
        scratch=[pltpu.VMEM((2, SB_GROUP, TILE, TILE), F32), pltpu.VMEM((2, SB_GROUP, TILE, TILE), BF16)],
        args=(pa, pa, pa), carried=carried, groups=kb)
    return outs[0], outs[1:]


def _sb_bwd(name, pa, col0, dout, dcol0, carried=None):
    s = pa.shape[0]
    nb = s // TILE
    cb = col0 // SB_LANES
    kb = SB_WIDTH // SB_LANES
    db = dcol0 // SB_LANES

    def body(q_ref, k_ref, v_ref, do_ref, dq_ref, dk_ref, dv_ref, dk_acc, dv_acc, dpan, span, gsum, lsig_s, lf_s):
        i = pl.program_id(1)

        @pl.when(i == 0)
        def _():
            dk_acc[...] = jnp.zeros_like(dk_acc)
            dv_acc[...] = jnp.zeros_like(dv_acc)

        r = lax.broadcasted_iota(jnp.int32, (TILE, TILE), 0)
        c = lax.broadcasted_iota(jnp.int32, (TILE, TILE), 1)
        strict = c < r
        u_after = _tri(TILE, lambda rr, cc: rr > cc)
        u_before = _tri(TILE, lambda rr, cc: rr < cc)
        qs = [_scaled_q(q_ref, _head_slices(hh), -SCALE) for hh in range(SB_GROUP)]
        dos = [do_ref[:, _head_slices(hh)].astype(BF16) for hh in range(SB_GROUP)]
        dots = [do_ref[:, _head_slices(hh)].T.astype(BF16) for hh in range(SB_GROUP)]
        qts = [q.astype(F32).T.astype(BF16) for q in qs]

        def scores(j, slot, diag):
            kblk = k_ref[_rows(k_ref, j), :]
            for hh in range(SB_GROUP):
                nz = _dot_nt(qs[hh], kblk[:, _head_slices(hh)])
                lf = jnp.minimum(nz, 0.0) - jnp.log(1.0 + jnp.exp(_neg_abs(nz)))
                lsig = lf - nz
                if diag:
                    lf = jnp.where(strict, lf, 0.0)
                    lsig = jnp.where(strict, lsig, MASKED)
                lsig_s[slot, hh] = lsig
                lf_s[slot, hh] = lf.astype(BF16)

        def grads(j, slot, carries):
            vblk = v_ref[_rows(v_ref, j), :]
            new = []
            for hh in range(SB_GROUP):
                lfb = lf_s[slot, hh]
                lsig = lsig_s[slot, hh]
                sx = _dot(lfb, u_after)
                a = jnp.exp(lsig + sx + carries[hh])
                g = a * _dot_nt(dos[hh], vblk[:, _head_slices(hh)])
                sig = jnp.exp(lsig)
                inside = _dot(g.astype(BF16), u_before)
                dpan[hh, j] = sig * (inside + g) - g
                span[hh, j] = sig
                gsum[hh, j] = inside[:, TILE - 1:TILE] + g[:, TILE - 1:TILE]
                dv_acc[hh, j] += _dot(dots[hh], a.astype(BF16))
                new.append(carries[hh] + sx[:, 0:1] + lfb[:, 0:1].astype(F32))
            return tuple(new)

        def step1(t, carries):
            carries = grads(i - t + 1, (t - 1) % 2, carries)
            scores(i - t, t % 2, False)
            return carries

        zero1 = jnp.zeros((TILE, 1), F32)
        scores(i, 0, True)
        carries = lax.fori_loop(1, i + 1, step1, (zero1,) * SB_GROUP)
        grads(0, i % 2, carries)

        def pass2(j, state):
            kblk = k_ref[_rows(k_ref, j), :]
            new = []
            for hh in range(SB_GROUP):
                before, ndq = state[hh]
                ndzb = (dpan[hh, j] + span[hh, j] * before).astype(BF16)
                dk_acc[hh, j] += _dot(qts[hh], ndzb)
                new.append((before + gsum[hh, j], ndq + _dot(ndzb, kblk[:, _head_slices(hh)])))
            return tuple(new)

        zero2 = (zero1, jnp.zeros((TILE, HEAD_DIM), F32))
        state = lax.fori_loop(0, i + 1, pass2, (zero2,) * SB_GROUP)
        dq_ref[...] = jnp.concatenate([st[1] * -SCALE for st in state], axis=1).astype(BF16)

        @pl.when(i == nb - 1)
        def _():
            for acc, ref in ((dk_acc, dk_ref), (dv_acc, dv_ref)):
                for j in range(nb):
                    ref[j * TILE:(j + 1) * TILE, :] = jnp.concatenate(
                        [acc[hh, j].T for hh in range(SB_GROUP)], axis=1).astype(BF16)

    qspec = pl.BlockSpec((TILE, SB_LANES), lambda p, i: (i, p))
    kvspec = pl.BlockSpec((s, SB_LANES), lambda p, i: (0, p))
    out = jax.ShapeDtypeStruct((s, SB_WIDTH), BF16)
    outs = _pair_grid_call(
        name, body, nb,
        in_specs=[pl.BlockSpec((TILE, SB_LANES), lambda p, i: (i, cb + p)),
                  pl.BlockSpec((s, SB_LANES), lambda p, i: (0, cb + kb + p)),
                  pl.BlockSpec((s, SB_LANES), lambda p, i: (0, cb + 2 * kb + p)),
                  pl.BlockSpec((TILE, SB_LANES), lambda p, i: (i, db + p))],
        out_specs=[qspec, kvspec, kvspec], out_shape=[out, out, out],
        scratch=[pltpu.VMEM((SB_GROUP, nb, HEAD_DIM, TILE), F32), pltpu.VMEM((SB_GROUP, nb, HEAD_DIM, TILE), F32),
                 pltpu.VMEM((SB_GROUP, nb, TILE, TILE), F32), pltpu.VMEM((SB_GROUP, nb, TILE, TILE), F32),
                 pltpu.VMEM((SB_GROUP, nb, TILE, 1), F32),
                 pltpu.VMEM((2, SB_GROUP, TILE, TILE), F32), pltpu.VMEM((2, SB_GROUP, TILE, TILE), BF16)],
        args=(pa, pa, pa, dout), carried=carried, groups=kb)
    return outs[:3], outs[3:]


def _fox_scores(q, kj, cq, crj, causal, diag):
    sc = _dot_nt(q, kj) + (cq - crj)
    if diag:
        sc = jnp.where(causal, sc, NEG_INF)
    return sc


def _fox_fwd(name, pa, col0, ccol4, crow4, carried=None):
    s = pa.shape[0]
    nb = s // TILE
    cb = col0 // FOX_LANES
    kb = FOX_WIDTH // FOX_LANES

    def body(q_ref, k_ref, v_ref, cc_ref, cr_ref, o_ref, lse_ref, sc_s):
        i = pl.program_id(1)
        head0 = pl.program_id(0) * FOX_GROUP
        r = lax.broadcasted_iota(jnp.int32, (TILE, TILE), 0)
        c = lax.broadcasted_iota(jnp.int32, (TILE, TILE), 1)
        causal = c <= r
        qs = [_scaled_q(q_ref, _head_slices(hh)) for hh in range(FOX_GROUP)]
        cqs = [cc_ref[:, HEAD_DIM * hh:HEAD_DIM * hh + 1] for hh in range(FOX_GROUP)]

        def logits(j, slot, diag):
            kblk = k_ref[_rows(k_ref, j), :]
            tops = []
            for hh in range(FOX_GROUP):
                sc = _fox_scores(qs[hh], kblk[:, _head_slices(hh)], cqs[hh], cr_ref[j, pl.ds(head0 + hh, 1), :], causal, diag)
                sc_s[slot, hh] = sc
                tops.append(jnp.max(sc, axis=1, keepdims=True))
            return tuple(tops)

        def update(j, slot, tops, state):
            vblk = v_ref[_rows(v_ref, j), :]
            new = []
            for hh in range(FOX_GROUP):
                m, l, acc = state[hh]
                m2 = jnp.maximum(m, tops[hh])
                alpha = jnp.exp(m - m2)
                p = jnp.exp(sc_s[slot, hh] - m2)
                new.append((m2, l * alpha + jnp.sum(p, axis=1, keepdims=True),
                            acc * alpha + _dot(p.astype(BF16), vblk[:, _head_slices(hh)])))
            return tuple(new)

        def step(t, both):
            tops, state = both
            state = update(i - t + 1, (t - 1) % 2, tops, state)
            return logits(i - t, t % 2, False), state

        zero = (jnp.full((TILE, 1), NEG_INF, F32), jnp.zeros((TILE, 1), F32), jnp.zeros((TILE, HEAD_DIM), F32))
        tops, state = lax.fori_loop(1, i + 1, step, (logits(i, 0, True), (zero,) * FOX_GROUP))
        state = update(0, i % 2, tops, state)
        o_ref[...] = jnp.concatenate([st[2] / st[1] for st in state], axis=1)
        lse_ref[...] = jnp.concatenate(
            [jnp.broadcast_to(st[0] + jnp.log(st[1]), (TILE, HEAD_DIM)) for st in state], axis=1)

    outs = _pair_grid_call(
        name, body, nb,
        in_specs=[pl.BlockSpec((TILE, FOX_LANES), lambda p, i: (i, cb + p)),
                  pl.BlockSpec((s, FOX_LANES), lambda p, i: (0, cb + kb + p)),
                  pl.BlockSpec((s, FOX_LANES), lambda p, i: (0, cb + 2 * kb + p)),
                  pl.BlockSpec((None, TILE, FOX_LANES), lambda p, i: (p, i, 0)),
                  pl.BlockSpec((nb, 8, TILE), lambda p, i: (0, 0, 0))],
        out_specs=[pl.BlockSpec((TILE, FOX_LANES), lambda p, i: (i, p)),
                   pl.BlockSpec((None, TILE, FOX_LANES), lambda p, i: (p, i, 0))],
        out_shape=[jax.ShapeDtypeStruct((s, FOX_WIDTH), F32), jax.ShapeDtypeStruct((kb, s, FOX_LANES), F32)],
        scratch=[pltpu.VMEM((2, FOX_GROUP, TILE, TILE), F32)],
        args=(pa, pa, pa, ccol4, crow4), carried=carried, groups=kb)
    return outs[0], outs[1], outs[2:]


def _fox_bwd(name, pa, col0, ccol4, crow4, out, lse, dout, dcol0, carried=None):
    s = pa.shape[0]
    nb = s // TILE
    cb = col0 // FOX_LANES
    kb = FOX_WIDTH // FOX_LANES
    db = dcol0 // FOX_LANES

    def body(q_ref, k_ref, v_ref, cc_ref, cr_ref, o_ref, lse_ref, do_ref,
             dq_ref, dk_ref, dv_ref, cs_ref, dk_acc, dv_acc, p_s, ds_s):
        i = pl.program_id(1)
        head0 = pl.program_id(0) * FOX_GROUP

        @pl.when(i == 0)
        def _():
            dk_acc[...] = jnp.zeros_like(dk_acc)
            dv_acc[...] = jnp.zeros_like(dv_acc)

        @pl.when(jnp.logical_and(i == 0, head0 == 0))
        def _():
            cs_ref[...] = jnp.zeros_like(cs_ref)

        r = lax.broadcasted_iota(jnp.int32, (TILE, TILE), 0)
        c = lax.broadcasted_iota(jnp.int32, (TILE, TILE), 1)
        causal = c <= r
        qs = [_scaled_q(q_ref, _head_slices(hh)) for hh in range(FOX_GROUP)]
        cqs = [cc_ref[:, HEAD_DIM * hh:HEAD_DIM * hh + 1] for hh in range(FOX_GROUP)]
        lses = [lse_ref[:, HEAD_DIM * hh:HEAD_DIM * hh + 1] for hh in range(FOX_GROUP)]
        dofs = [do_ref[:, _head_slices(hh)] for hh in range(FOX_GROUP)]
        dos = [d_.astype(BF16) for d_ in dofs]
        dots = [d_.T.astype(BF16) for d_ in dofs]
        qts = [q.astype(F32).T.astype(BF16) for q in qs]
        deltas = [jnp.sum(dofs[hh] * o_ref[:, _head_slices(hh)], axis=1, keepdims=True) for hh in range(FOX_GROUP)]

        def probs(j, slot, rowsums, diag):
            kblk = k_ref[_rows(k_ref, j), :]
            vblk = v_ref[_rows(v_ref, j), :]
            new = []
            for hh in range(FOX_GROUP):
                sl = _head_slices(hh)
                sc = _fox_scores(qs[hh], kblk[:, sl], cqs[hh], cr_ref[j, pl.ds(head0 + hh, 1), :], causal, diag)
                p = jnp.exp(sc - lses[hh])
                ds = p * (_dot_nt(dos[hh], vblk[:, sl]) - deltas[hh])
                p_s[slot, hh] = p.astype(BF16)
                ds_s[slot, hh] = ds.astype(BF16)
                cs_ref[j, pl.ds(head0 + hh, 1), :] += jnp.sum(ds, axis=0, keepdims=True)
                new.append(rowsums[hh] + jnp.sum(ds, axis=1, keepdims=True))
            return tuple(new)

        def accumulate(j, slot, dqs):
            kblk = k_ref[_rows(k_ref, j), :]
            new = []
            for hh in range(FOX_GROUP):
                dsb = ds_s[slot, hh]
                dv_acc[hh, j] += _dot(dots[hh], p_s[slot, hh])
                dk_acc[hh, j] += _dot(qts[hh], dsb)
                new.append(dqs[hh] + _dot(dsb, kblk[:, _head_slices(hh)]))
            return tuple(new)

        def step(t, both):
            rowsums, dqs = both
            dqs = accumulate(i - t + 1, (t - 1) % 2, dqs)
            return probs(i - t, t % 2, rowsums, False), dqs

        zero1 = jnp.zeros((TILE, 1), F32)
        zero64 = jnp.zeros((TILE, HEAD_DIM), F32)
        rowsums, dqs = lax.fori_loop(1, i + 1, step,
                                     (probs(i, 0, (zero1,) * FOX_GROUP, True), (zero64,) * FOX_GROUP))
        dqs = accumulate(0, i % 2, dqs)
        for hh in range(FOX_GROUP):
            cs_ref[i, pl.ds(head0 + hh, 1), :] -= jnp.broadcast_to(rowsums[hh], (TILE, LANES)).T[0:1, :]
        dq_ref[...] = jnp.concatenate([dq * SCALE for dq in dqs], axis=1).astype(BF16)

        @pl.when(i == nb - 1)
        def _():
            for acc, ref in ((dk_acc, dk_ref), (dv_acc, dv_ref)):
                for j in range(nb):
                    ref[j * TILE:(j + 1) * TILE, :] = jnp.concatenate(
                        [acc[hh, j].T for hh in range(FOX_GROUP)], axis=1).astype(BF16)

    qspec = pl.BlockSpec((TILE, FOX_LANES), lambda p, i: (i, p))
    kvspec = pl.BlockSpec((s, FOX_LANES), lambda p, i: (0, p))
    o3 = jax.ShapeDtypeStruct((s, FOX_WIDTH), BF16)
    outs = _pair_grid_call(
        name, body, nb,
        in_specs=[pl.BlockSpec((TILE, FOX_LANES), lambda p, i: (i, cb + p)),
                  pl.BlockSpec((s, FOX_LANES), lambda p, i: (0, cb + kb + p)),
                  pl.BlockSpec((s, FOX_LANES), lambda p, i: (0, cb + 2 * kb + p)),
                  pl.BlockSpec((None, TILE, FOX_LANES), lambda p, i: (p, i, 0)),
                  pl.BlockSpec((nb, 8, TILE), lambda p, i: (0, 0, 0)),
                  qspec,
                  pl.BlockSpec((None, TILE, FOX_LANES), lambda p, i: (p, i, 0)),
                  pl.BlockSpec((TILE, FOX_LANES), lambda p, i: (i, db + p))],
        out_specs=[qspec, kvspec, kvspec, pl.BlockSpec((nb, 8, TILE), lambda p, i: (0, 0, 0))],
        out_shape=[o3, o3, o3, jax.ShapeDtypeStruct((nb, 8, TILE), F32)],
        scratch=[pltpu.VMEM((FOX_GROUP, nb, HEAD_DIM, TILE), F32), pltpu.VMEM((FOX_GROUP, nb, HEAD_DIM, TILE), F32),
                 pltpu.VMEM((2, FOX_GROUP, TILE, TILE), BF16), pltpu.VMEM((2, FOX_GROUP, TILE, TILE), BF16)],
        args=(pa, pa, pa, ccol4, crow4, out, lse, dout), carried=carried, groups=kb)
    return outs[:4], outs[4:]


def _mem_fwd(name, pa, mkv):
    s = pa.shape[0]
    ml = mkv.shape[0]
    nb = s // TILE
    cb = QKV_WIDTH // LANES

    def body(q_ref, k_ref, v_ref, o_ref, lse_ref):
        outs, lses = [], []
        for hh in range(2):
            sl = _head_slices(hh)
            sc = _dot_nt(_scaled_q(q_ref, sl), k_ref[:, sl])
            m = jnp.max(sc, axis=1, keepdims=True)
            p = jnp.exp(sc - m)
            l = jnp.sum(p, axis=1, keepdims=True)
            outs.append(_dot(p.astype(BF16), v_ref[:, sl]) / l)
            lses.append(jnp.broadcast_to(m + jnp.log(l), (TILE, HEAD_DIM)))
        o_ref[...] = jnp.concatenate(outs, axis=1)
        lse_ref[...] = jnp.concatenate(lses, axis=1)

    return pl.pallas_call(
        body, name=name, grid=(2, nb),
        in_specs=[pl.BlockSpec((TILE, LANES), lambda p, i: (i, cb + p)),
                  pl.BlockSpec((ml, LANES), lambda p, i: (0, p)),
                  pl.BlockSpec((ml, LANES), lambda p, i: (0, 2 + p))],
        out_specs=[pl.BlockSpec((TILE, LANES), lambda p, i: (i, p)),
                   pl.BlockSpec((None, TILE, LANES), lambda p, i: (p, i, 0))],
        out_shape=[jax.ShapeDtypeStruct((s, MEM_WIDTH), F32), jax.ShapeDtypeStruct((2, s, LANES), F32)],
        compiler_params=_params("parallel", "parallel"),
    )(pa, mkv, mkv)


def _mem_bwd(name, pa, mkv, out, lse, dout, dcol0):
    s = pa.shape[0]
    ml = mkv.shape[0]
    nb = s // TILE
    cb = QKV_WIDTH // LANES
    db = dcol0 // LANES

    def body(q_ref, k_ref, v_ref, o_ref, lse_ref, do_ref, dq_ref, dk_ref, dv_ref, dk_acc, dv_acc):
        i = pl.program_id(1)

        @pl.when(i == 0)
        def _():
            dk_acc[...] = jnp.zeros_like(dk_acc)
            dv_acc[...] = jnp.zeros_like(dv_acc)

        dqs = []
        for hh in range(2):
            sl = _head_slices(hh)
            q = _scaled_q(q_ref, sl)
            kh = k_ref[:, sl]
            dof = do_ref[:, sl]
            do = dof.astype(BF16)
            delta = jnp.sum(dof * o_ref[:, sl], axis=1, keepdims=True)
            p = jnp.exp(_dot_nt(q, kh) - lse_ref[:, HEAD_DIM * hh:HEAD_DIM * hh + 1])
            ds = (p * (_dot_nt(do, v_ref[:, sl]) - delta)).astype(BF16)
            dv_acc[hh] += _dot(dof.T.astype(BF16), p.astype(BF16))
            dk_acc[hh] += _dot(q.astype(F32).T.astype(BF16), ds)
            dqs.append(_dot(ds, kh) * SCALE)
        dq_ref[...] = jnp.concatenate(dqs, axis=1).astype(BF16)

        @pl.when(i == nb - 1)
        def _():
            dk_ref[...] = jnp.concatenate([dk_acc[0].T, dk_acc[1].T], axis=1).astype(BF16)
            dv_ref[...] = jnp.concatenate([dv_acc[0].T, dv_acc[1].T], axis=1).astype(BF16)

    qspec = pl.BlockSpec((TILE, LANES), lambda p, i: (i, p))
    kvspec = pl.BlockSpec((ml, LANES), lambda p, i: (0, p))
    okv = jax.ShapeDtypeStruct((ml, MEM_WIDTH), BF16)
    return pl.pallas_call(
        body, name=name, grid=(2, nb),
        in_specs=[pl.BlockSpec((TILE, LANES), lambda p, i: (i, cb + p)),
                  pl.BlockSpec((ml, LANES), lambda p, i: (0, p)),
                  pl.BlockSpec((ml, LANES), lambda p, i: (0, 2 + p)),
                  qspec,
                  pl.BlockSpec((None, TILE, LANES), lambda p, i: (p, i, 0)),
                  pl.BlockSpec((TILE, LANES), lambda p, i: (i, db + p))],
        out_specs=[qspec, kvspec, kvspec],
        out_shape=[jax.ShapeDtypeStruct((s, MEM_WIDTH), BF16), okv, okv],
        scratch_shapes=[pltpu.VMEM((2, HEAD_DIM, ml), F32), pltpu.VMEM((2, HEAD_DIM, ml), F32)],
        compiler_params=_params("arbitrary", "arbitrary"),
    )(pa, mkv, mkv, out, lse, dout)


def _head_maps():
    col = jnp.arange(MIX_WIDTH)[:, None] // HEAD_DIM
    g = (col == jnp.arange(LANES)[None, :]).astype(BF16)
    return g, g.T


def _normed_heads(osb_ref, ofx_ref, om_ref, g_ref, gt_ref):
    y = jnp.concatenate([osb_ref[...], ofx_ref[...], om_ref[...]], axis=1)
    msq = _sum_l2(y * y, g_ref[...]) * (1.0 / HEAD_DIM)
    rf = _sum_l3(lax.rsqrt(msq + EPS), gt_ref[...])
    return y * rf, rf


def _out_fwd(name, o_sb, o_fx, o_m, pb, ow, x, w_out, ts):
    s, d = x.shape
    g, gt = _head_maps()

    def body(osb_ref, ofx_ref, om_ref, gate_ref, ow_ref, x_ref, w_ref, g_ref, gt_ref, xo_ref, y2_ref):
        yh, _ = _normed_heads(osb_ref, ofx_ref, om_ref, g_ref, gt_ref)
        gate = gate_ref[...]
        y2 = (yh * ow_ref[...] * (gate * jax.nn.sigmoid(gate))).astype(BF16)
        y2_ref[...] = y2
        xo_ref[...] = x_ref[...] + _dot(y2, w_ref[...])

    return pl.pallas_call(
        body, name=name, grid=(s // ts,),
        in_specs=[_row_spec(ts, SB_WIDTH), _row_spec(ts, FOX_WIDTH), _row_spec(ts, MEM_WIDTH),
                  _row_spec(ts, MIX_WIDTH), _const_spec((1, MIX_WIDTH)), _row_spec(ts, d),
                  _const_spec((MIX_WIDTH, d)),
                  _const_spec((MIX_WIDTH, LANES)), _const_spec((LANES, MIX_WIDTH))],
        out_specs=[_row_spec(ts, d), _row_spec(ts, MIX_WIDTH)],
        out_shape=[jax.ShapeDtypeStruct((s, d), F32), jax.ShapeDtypeStruct((s, MIX_WIDTH), BF16)],
        compiler_params=_params("parallel"),
    )(o_sb, o_fx, o_m, pb, ow, x, w_out, g, gt)


def _row_spec(ts, w):
    return pl.BlockSpec((ts, w), lambda i: (i, 0))


def _const_spec(shape):
    return pl.BlockSpec(shape, lambda i: (0,) * len(shape))


def _out_bwd(name, dxb, o_sb, o_fx, o_m, pb, ow, w_out, ts):
    s, d = dxb.shape
    g, gt = _head_maps()

    def body(dx_ref, osb_ref, ofx_ref, om_ref, gate_ref, ow_ref, w_ref, g_ref, gt_ref, dy_ref, dgate_ref, dow_ref):
        @pl.when(pl.program_id(0) == 0)
        def _():
            dow_ref[...] = jnp.zeros_like(dow_ref)

        dy2 = _dot_nt(dx_ref[...], w_ref[...])
        yh, rf = _normed_heads(osb_ref, ofx_ref, om_ref, g_ref, gt_ref)
        gate = gate_ref[...]
        sig = jax.nn.sigmoid(gate)
        ow_v = ow_ref[...]
        dgate_ref[...] = (dy2 * (yh * ow_v) * (sig * (1.0 + gate * (1.0 - sig)))).astype(BF16)
        dn = dy2 * (gate * sig)
        dow_ref[...] += jnp.sum(dn * yh, axis=0, keepdims=True)
        dyh = dn * ow_v
        t = _sum_l2(dyh * yh, g_ref[...]) * (1.0 / HEAD_DIM)
        dy_ref[...] = rf * (dyh - yh * _sum_l3(t, gt_ref[...]))

    return pl.pallas_call(
        body, name=name, grid=(s // ts,),
        in_specs=[_row_spec(ts, d), _row_spec(ts, SB_WIDTH), _row_spec(ts, FOX_WIDTH), _row_spec(ts, MEM_WIDTH),
                  _row_spec(ts, MIX_WIDTH), _const_spec((1, MIX_WIDTH)),
                  _const_spec((MIX_WIDTH, d)),
                  _const_spec((MIX_WIDTH, LANES)), _const_spec((LANES, MIX_WIDTH))],
        out_specs=[_row_spec(ts, MIX_WIDTH), _row_spec(ts, MIX_WIDTH), _const_spec((1, MIX_WIDTH))],
        out_shape=[jax.ShapeDtypeStruct((s, MIX_WIDTH), F32), jax.ShapeDtypeStruct((s, PB), BF16),
                   jax.ShapeDtypeStruct((1, MIX_WIDTH), F32)],
        compiler_params=_params("arbitrary"),
    )(dxb, o_sb, o_fx, o_m, pb, ow, w_out, g, gt)


def _adamw(name, w, g, m, v, tr):
    def body(w_ref, g_ref, m_ref, v_ref, d_ref, m2_ref, v2_ref):
        gv = g_ref[...]
        m2 = ADAM_B1 * m_ref[...] + (1.0 - ADAM_B1) * gv
        v2 = ADAM_B2 * v_ref[...] + (1.0 - ADAM_B2) * (gv * gv)
        m_hat = m2 / (1.0 - ADAM_B1 ** ADAM_STEP)
        v_hat = v2 / (1.0 - ADAM_B2 ** ADAM_STEP)
        d_ref[...] = -ADAM_LR * (m_hat / (jnp.sqrt(v_hat) + ADAM_EPS) + ADAM_WD * w_ref[...])
        m2_ref[...] = m2
        v2_ref[...] = v2

    rest = w.shape[1:]
    spec = pl.BlockSpec((tr,) + rest, lambda i: (i,) + (0,) * len(rest))
    shp = jax.ShapeDtypeStruct(w.shape, F32)
    return pl.pallas_call(
        body, name=name, grid=(w.shape[0] // tr,), in_specs=[spec] * 4, out_specs=[spec] * 3, out_shape=[shp] * 3,
        compiler_params=_params("parallel"),
    )(w, g, m, v)


def _adamw_sharded(name, w, m, v, g_own, g_other, cvec, tr):
    depth, rows, cols = w.shape
    nt = rows // 2 // tr

    def body(c_ref, w_ref, m_ref, v_ref, *rest):
        g_refs, (g_ref, d_ref, m2_ref, v2_ref) = rest[:2 * depth], rest[2 * depth:]
        layer, mine = pl.program_id(0), pl.program_id(1) == c_ref[0]
        gv = None
        for lt in range(depth):
            cand = jnp.where(mine, g_refs[lt][...], g_refs[depth + lt][...])
            gv = cand if gv is None else jnp.where(layer == lt, cand, gv)
        m2 = ADAM_B1 * m_ref[...] + (1.0 - ADAM_B1) * gv
        v2 = ADAM_B2 * v_ref[...] + (1.0 - ADAM_B2) * (gv * gv)
        m_hat = m2 / (1.0 - ADAM_B1 ** ADAM_STEP)
        v_hat = v2 / (1.0 - ADAM_B2 ** ADAM_STEP)
        g_ref[...] = gv
        d_ref[...] = -ADAM_LR * (m_hat / (jnp.sqrt(v_hat) + ADAM_EPS) + ADAM_WD * w_ref[...])
        m2_ref[...] = m2
        v2_ref[...] = v2

    def g_map(lt, own):
        def index(l, hf, i, c_ref):
            use = jnp.logical_and(l == lt, (hf == c_ref[0]) == own)
            return jnp.where(use, i, 0), 0
        return index

    full = pl.BlockSpec((None, tr, cols), lambda l, hf, i, c_ref: (l, hf * nt + i, 0))
    g_specs = [pl.BlockSpec((tr, cols), g_map(lt, own)) for own in (True, False) for lt in range(depth)]
    shp = jax.ShapeDtypeStruct((depth, rows, cols), F32)
    return pl.pallas_call(
        body, name=name,
        grid_spec=pltpu.PrefetchScalarGridSpec(
            num_scalar_prefetch=1, grid=(depth, 2, nt), in_specs=[full] * 3 + g_specs, out_specs=[full] * 4),
        out_shape=[shp] * 4,
        compiler_params=_params("arbitrary", "arbitrary", "arbitrary"),
    )(cvec, w, m, v, *g_own, *g_other)


HBM_SPEC = pl.BlockSpec(memory_space=pltpu.HBM)


def _place():
    x, y, c = lax.axis_index("x"), lax.axis_index("y"), lax.axis_index("c")
    chips = [(1 - x, y), (x, 1 - y), (1 - x, 1 - y)]
    return x, y, c, chips


def _remote(src, dst, send_sems, recv_sems, k, to):
    return pltpu.make_async_remote_copy(src_ref=src, dst_ref=dst, send_sem=send_sems.at[k], recv_sem=recv_sems.at[k],
                                        device_id=to, device_id_type=MESH)


def _half_rows(n_rows, cc):
    rh = n_rows // 2
    return pl.ds(pl.multiple_of(cc * rh, 16), rh)


def _dma_sems(n):
    return [pltpu.SemaphoreType.DMA((n,)), pltpu.SemaphoreType.DMA((n,))]


class _Exchange:
    def __init__(self, inputs, out_shapes, n_sems, begin, relay, finish):
        self.inputs, self.out_shapes, self.n_sems = list(inputs), list(out_shapes), n_sems
        self.begin, self.relay, self.finish = begin, relay, finish

    @property
    def n(self):
        return len(self.inputs)

    def split(self, refs):
        return refs[:self.n], refs[self.n:2 * self.n], refs[2 * self.n], refs[2 * self.n + 1]


def _run_exchange(name, ex):
    def body(*refs):
        parts = ex.split(refs)
        for phase in (ex.begin, ex.relay, ex.finish):
            if phase is not None:
                phase(*parts)

    return pl.pallas_call(
        body, name=name, in_specs=[HBM_SPEC] * ex.n, out_specs=[HBM_SPEC] * ex.n, out_shape=ex.out_shapes,
        scratch_shapes=_dma_sems(ex.n_sems),
    )(*ex.inputs)


def _gather_exchange(shards):
    def ici(in_refs, out_refs, send_sems, recv_sems):
        x, y, c, chips = _place()
        return [_remote(in_ref.at[_half_rows(in_ref.shape[0], c)], out_ref.at[2 * x + y, _half_rows(in_ref.shape[0], c)],
                        send_sems, recv_sems, 6 * a + j, (cx, cy, c))
                for a, (in_ref, out_ref) in enumerate(zip(in_refs, out_refs)) for j, (cx, cy) in enumerate(chips)]

    def d2d(out_refs, send_sems, recv_sems, half_of):
        x, y, c, chips = _place()
        cps = []
        for a, out_ref in enumerate(out_refs):
            for j, (cx, cy) in enumerate(chips):
                piece = out_ref.at[2 * cx + cy, _half_rows(out_ref.shape[1], half_of(c))]
                cps.append(_remote(piece, piece, send_sems, recv_sems, 6 * a + 3 + j, (x, y, 1 - c)))
        return cps

    def begin(in_refs, out_refs, send_sems, recv_sems):
        for cp in ici(in_refs, out_refs, send_sems, recv_sems):
            cp.start()

    def relay(in_refs, out_refs, send_sems, recv_sems):
        x, y, c, chips = _place()
        for a, out_ref in enumerate(out_refs):
            for j, (cx, cy) in enumerate(chips):
                landed = out_ref.at[2 * cx + cy, _half_rows(out_ref.shape[1], c)]
                _remote(landed, landed, send_sems, recv_sems, 6 * a + j, (cx, cy, c)).wait_recv()
        for cp in d2d(out_refs, send_sems, recv_sems, lambda c_: c_):
            cp.start()

    def finish(in_refs, out_refs, send_sems, recv_sems):
        for cp in d2d(out_refs, send_sems, recv_sems, lambda c_: 1 - c_):
            cp.wait_recv()
        for cp in ici(in_refs, out_refs, send_sems, recv_sems) + d2d(out_refs, send_sems, recv_sems, lambda c_: c_):
            cp.wait_send()

    shapes = [jax.ShapeDtypeStruct((N_CHIPS,) + s_.shape, s_.dtype) for s_ in shards]
    return _Exchange(shards, shapes, 6 * len(shards), begin, relay, finish)


def _swap_exchange(g4s):
    def copies(in_refs, out_refs, send_sems, recv_sems):
        x, y, c, _ = _place()
        return [_remote(in_ref.at[:, _half_rows(in_ref.shape[1], 1 - c), :], out_ref, send_sems, recv_sems, a, (x, y, 1 - c))
                for a, (in_ref, out_ref) in enumerate(zip(in_refs, out_refs))]

    def begin(*parts):
        for cp in copies(*parts):
            cp.start()

    def finish(*parts):
        for cp in copies(*parts):
            cp.wait()

    shapes = [jax.ShapeDtypeStruct((g.shape[0], g.shape[1] // 2, g.shape[2]), g.dtype) for g in g4s]
    return _Exchange(g4s, shapes, len(g4s), begin, None, finish)


def _add_half(name, g4, r1, cvec, tr):
    n, r, w = g4.shape
    rh = r // 2
    nblk = rh // tr

    def body(c_ref, a_ref, b_ref, o_ref):
        o_ref[...] = (a_ref[...].astype(F32) + b_ref[...].astype(F32)).astype(BF16)

    return pl.pallas_call(
        body, name=name,
        grid_spec=pltpu.PrefetchScalarGridSpec(
            num_scalar_prefetch=1, grid=(n, nblk),
            in_specs=[pl.BlockSpec((None, tr, w), lambda k, i, c_ref: (k, c_ref[0] * nblk + i, 0)),
                      pl.BlockSpec((None, tr, w), lambda k, i, c_ref: (k, i, 0))],
            out_specs=pl.BlockSpec((None, tr, w), lambda k, i, c_ref: (k, i, 0))),
        out_shape=jax.ShapeDtypeStruct((n, rh, w), BF16),
        compiler_params=_params("parallel", "parallel"),
    )(cvec, g4, r1)


def _scatter_exchange(h4s):
    def sends(in_refs, out_refs, send_sems, recv_sems):
        x, y, c, chips = _place()
        return [_remote(in_ref.at[2 * cx + cy], out_ref.at[j], send_sems, recv_sems, 3 * a + j, (cx, cy, c))
                for a, (in_ref, out_ref) in enumerate(zip(in_refs, out_refs)) for j, (cx, cy) in enumerate(chips)]

    def begin(*parts):
        for cp in sends(*parts):
            cp.start()

    def finish(in_refs, out_refs, send_sems, recv_sems):
        x, y, c, chips = _place()
        for a, out_ref in enumerate(out_refs):
            for j, (cx, cy) in enumerate(chips):
                got = out_ref.at[j]
                _remote(got, got, send_sems, recv_sems, 3 * a + j, (cx, cy, c)).wait_recv()
        for cp in sends(in_refs, out_refs, send_sems, recv_sems):
            cp.wait_send()

    shapes = [jax.ShapeDtypeStruct((3,) + h.shape[1:], h.dtype) for h in h4s]
    return _Exchange(h4s, shapes, 3 * len(h4s), begin, None, finish)


def _sum_chips(name, h4, r3, mvec, tr):
    _, rh, w = h4.shape

    def body(m_ref, a_ref, b_ref, c_ref, d_ref, o_ref):
        o_ref[...] = ((a_ref[...].astype(F32) + b_ref[...].astype(F32)) + c_ref[...].astype(F32)) + d_ref[...].astype(F32)

    specs = [pl.BlockSpec((None, tr, w), lambda i, m_ref: (m_ref[0], i, 0))]
    specs += [pl.BlockSpec((None, tr, w), functools.partial(lambda k, i, m_ref: (k, i, 0), k)) for k in range(3)]
    return pl.pallas_call(
        body, name=name,
        grid_spec=pltpu.PrefetchScalarGridSpec(
            num_scalar_prefetch=1, grid=(rh // tr,), in_specs=specs,
            out_specs=pl.BlockSpec((tr, w), lambda i, m_ref: (i, 0))),
        out_shape=jax.ShapeDtypeStruct((rh, w), F32),
        compiler_params=_params("parallel"),
    )(mvec, h4, r3, r3, r3)


def _swap_reduced(name, ghs):
    n = len(ghs)

    def body(*refs):
        in_refs, out_refs, (send_sems, recv_sems) = refs[:n], refs[n:2 * n], refs[2 * n:]
        x, y, c, _ = _place()
        cps = [_remote(in_ref, out_ref, send_sems, recv_sems, a, (x, y, 1 - c))
               for a, (in_ref, out_ref) in enumerate(zip(in_refs, out_refs))]
        for cp in cps:
            cp.start()
        for cp in cps:
            cp.wait()

    return pl.pallas_call(
        body, name=name, in_specs=[HBM_SPEC] * n, out_specs=[HBM_SPEC] * n,
        out_shape=[jax.ShapeDtypeStruct(g.shape, g.dtype) for g in ghs],
        scratch_shapes=_dma_sems(n),
    )(*ghs)


def _small_update(name, partials, weights, moments1, moments2):
    n = len(partials)
    width = max(p.shape[1] for p in partials)
    starts, at = [], 0
    for p in partials:
        starts.append(at)
        at += p.shape[0]
    rows = -(-at // 8) * 8
    has_w = [w is not None for w in weights]
    n_w = sum(has_w)

    def body(*refs):
        p_refs = refs[:n]
        w_refs, m_refs, v_refs = refs[n:n + n_w], refs[n + n_w:n + 2 * n_w], refs[n + 2 * n_w:n + 3 * n_w]
        outs = refs[n + 3 * n_w:-4]
        g_refs, upd_refs = outs[:n], outs[n:]
        vec, buf, send_sems, recv_sems = refs[-4:]
        x, y, c, _ = _place()
        me = 4 * x + 2 * y + c
        vec[...] = jnp.zeros_like(vec)
        for p_ref, r0 in zip(p_refs, starts):
            vec[r0:r0 + p_ref.shape[0], 0:p_ref.shape[1]] = p_ref[...]
        buf[me] = vec[...]
        flips = [(fx, fy, fc) for fx in (0, 1) for fy in (0, 1) for fc in (0, 1)][1:]
        peers = [(x + fx - 2 * x * fx, y + fy - 2 * y * fy, c + fc - 2 * c * fc) for fx, fy, fc in flips]
        sends = [_remote(vec, buf.at[me], send_sems, recv_sems, k, peer) for k, peer in enumerate(peers)]
        for cp in sends:
            cp.start()
        for k, (px, py, pc) in enumerate(peers):
            got = buf.at[4 * px + 2 * py + pc]
            _remote(got, got, send_sems, recv_sems, k, (px, py, pc)).wait_recv()
        for cp in sends:
            cp.wait_send()
        total = buf[0]
        for dev in range(1, N_DEV):
            total = total + buf[dev]
        k = 0
        for a in range(n):
            r, w = g_refs[a].shape
            g = total[starts[a]:starts[a] + r, 0:w]
            g_refs[a][...] = g
            if has_w[a]:
                m2 = ADAM_B1 * m_refs[k][...] + (1.0 - ADAM_B1) * g
                v2 = ADAM_B2 * v_refs[k][...] + (1.0 - ADAM_B2) * (g * g)
                m_hat = m2 / (1.0 - ADAM_B1 ** ADAM_STEP)
                v_hat = v2 / (1.0 - ADAM_B2 ** ADAM_STEP)
                upd_refs[3 * k][...] = -ADAM_LR * (m_hat / (jnp.sqrt(v_hat) + ADAM_EPS) + ADAM_WD * w_refs[k][...])
                upd_refs[3 * k + 1][...] = m2
                upd_refs[3 * k + 2][...] = v2
                k += 1

    ws = [w for w in weights if w is not None]
    g_shapes = [jax.ShapeDtypeStruct(p.shape if w is None else w.shape, F32) for p, w in zip(partials, weights)]
    u_shapes = [jax.ShapeDtypeStruct(w.shape, F32) for w in ws for _ in range(3)]
    vm = pl.BlockSpec(memory_space=pltpu.VMEM)
    n_args = n + 3 * n_w
    outs = pl.pallas_call(
        body, name=name, in_specs=[vm] * n_args, out_specs=[vm] * (n + 3 * n_w), out_shape=g_shapes + u_shapes,
        scratch_shapes=[pltpu.VMEM((rows, width), F32), pltpu.VMEM((N_DEV, rows, width), F32),
                        pltpu.SemaphoreType.DMA((7,)), pltpu.SemaphoreType.DMA((7,))],
    )(*partials, *ws, *[m for m in moments1 if m is not None], *[v for v in moments2 if v is not None])
    return outs[:n], outs[n:]


GATE_COL = 3 * SB_WIDTH + 3 * FOX_WIDTH + FOX_HEADS + MEM_WIDTH
FL_COL = QKV_WIDTH


GROUP_A_COLS = [(0, QKV_WIDTH), (FL_COL + FOX_HEADS, MEM_WIDTH)]
GROUP_B_COLS = [(GATE_COL, MIX_WIDTH), (FL_COL, FOX_HEADS)]


def _group_from_shards(shard_of, cw, spans, pad):
    parts = []
    for lo, width in spans:
        hi = lo + width
        for j in range(N_CHIPS):
            a, b = max(lo, j * cw), min(hi, (j + 1) * cw)
            if a < b:
                parts.append(shard_of(j)[:, a - j * cw:b - j * cw])
    if pad:
        parts.append(jnp.zeros((parts[0].shape[0], pad), parts[0].dtype))
    return jnp.concatenate(parts, axis=1)


def _shard_from_groups(ga, gb, j, cw):
    lo, hi = j * cw, (j + 1) * cw
    placed = []
    for grp, spans in ((ga, GROUP_A_COLS), (gb, GROUP_B_COLS)):
        at = 0
        for first, width in spans:
            a, b = max(lo, first), min(hi, first + width)
            if a < b:
                placed.append((a, grp[:, at + a - first:at + b - first]))
            at += width
    return jnp.concatenate([p for _, p in sorted(placed, key=lambda t: t[0])], axis=1)


def _tile_of(n, cap, unit):
    if n <= cap:
        return n
    best = None
    for t in range(unit, cap + 1, unit):
        if n % t == 0:
            best = t
    assert best is not None, (n, cap, unit)
    return best


def _column_major_rows(a):
    dp, r, c = a.shape
    return a.transpose(2, 0, 1).reshape(c, dp, r // LANES, LANES).transpose(0, 2, 1, 3).reshape(-1, 8, LANES)


def _from_column_major_rows(b, shape):
    dp, r, c = shape
    return b.reshape(c, r // LANES, dp, LANES).transpose(0, 2, 1, 3).reshape(c, dp, r).transpose(1, 2, 0)


def _pack_small(parts):
    rows = []
    for p in parts:
        f = p.reshape(-1).astype(F32)
        f = jnp.pad(f, (0, (-f.shape[0]) % LANES))
        rows.append(f.reshape(-1, LANES))
    out = jnp.concatenate(rows, axis=0)
    return jnp.pad(out, ((0, (-out.shape[0]) % 8), (0, 0)))


def _unpack_small(packed, shapes):
    outs, r = [], 0
    for shp in shapes:
        n = 1
        for s_ in shp:
            n *= s_
        nr = -(-n // LANES)
        outs.append(packed[r:r + nr].reshape(-1)[:n].reshape(shp))
        r += nr
    return outs


def kernel(x, mem, norm_w, w_in, b_forget, mem_norm_w, w_mem_kv, out_norm_w, w_out, final_norm_w, loss_target, m_norm_w, m_w_in, m_b_forget, m_mem_norm_w, m_w_mem_kv, m_out_norm_w, m_w_out, m_final_norm_w, v_norm_w, v_w_in, v_b_forget, v_mem_norm_w, v_w_mem_kv, v_out_norm_w, v_w_out, v_final_norm_w):
    xs = x[0]
    mems = mem[0]
    target = loss_target[0]
    s, d = xs.shape
    depth = norm_w.shape[0]
    nb = s // TILE
    ts = _tile_of(s, 256, 8)
    big = (w_in, w_mem_kv, w_out)
    core = lax.axis_index("c")
    chip = 2 * lax.axis_index("x") + lax.axis_index("y")
    cvec = core.astype(jnp.int32).reshape(1)
    mvec = chip.astype(jnp.int32).reshape(1)
    cw = w_in.shape[2]

    own_w = [[a[l].astype(BF16) for a in big] for l in range(depth)]

    def lay_out_in(own, got):
        shard_of = lambda j: jnp.where(chip == j, own, got[j])
        return (_group_from_shards(shard_of, cw, GROUP_A_COLS, 0),
                _group_from_shards(shard_of, cw, GROUP_B_COLS, LANES - FOX_HEADS))

    def lay_out_rows(own, got):
        full = jnp.where(lax.broadcasted_iota(jnp.int32, got.shape, 0) == chip, own[None], got)
        return full.reshape(-1, full.shape[2])

    w_in_groups = [lay_out_in(own_w[0][0], _run_exchange("gather_weights0", _gather_exchange(own_w[0][:1]))[0])]
    layer_w = []

    tm = _tile_of(s, 256, 8)
    fl_block = MIX_WIDTH // LANES

    saved = []
    cur = xs
    for l in range(depth):
        wa, wb = w_in_groups[l]
        h = _rms_fwd(f"rms_fwd{l}", cur, norm_w[l][None], ts)
        pa = _mm(f"inproj_a{l}", h, wa, "nn", tm, _tile_of(PA, 1664, LANES), BF16)
        pb = _mm(f"inproj_b{l}", h, wb, "nn", tm, PB, F32)
        bpad = jnp.pad(b_forget[l], (0, LANES - FOX_HEADS))[None]
        ccol4, crow4 = _gate_fwd(f"gate_fwd{l}", pb, bpad, fl_block)
        more = l + 1 < depth
        o_sb, got = _sb_fwd(f"sb_fwd{l}", pa, 0, carried=_gather_exchange(own_w[l][1:]))
        wkv, wout = lay_out_rows(own_w[l][1], got[0]), lay_out_rows(own_w[l][2], got[1])
        layer_w.append((wa, wb, wkv, wout))
        o_fx, lse_fx, got = _fox_fwd(f"fox_fwd{l}", pa, 3 * SB_WIDTH, ccol4, crow4,
                                     carried=_gather_exchange(own_w[l + 1][:1]) if more else None)
        if more:
            w_in_groups.append(lay_out_in(own_w[l + 1][0], got[0]))
        mn = _rms_fwd(f"mem_rms{l}", mems, mem_norm_w[l][None], mems.shape[0])
        mkv = _mm(f"mem_kv{l}", mn, wkv, "nn", mems.shape[0], 2 * MEM_WIDTH, BF16)
        o_m, lse_m = _mem_fwd(f"mem_fwd{l}", pa, mkv)
        nxt, y2 = _out_fwd(f"out_fwd{l}", o_sb, o_fx, o_m, pb, out_norm_w[l][None], cur, wout, ts)
        saved.append((cur, h, pa, pb, bpad, ccol4, crow4, o_sb, o_fx, lse_fx, mn, mkv, o_m, lse_m, y2))
        cur = nxt

    loss_v, dx, dxb, g_final = _final_loss("final_loss", cur, final_norm_w[None], target, ts)

    g_norm, g_b, g_memnorm, g_outnorm = [None] * depth, [None] * depth, [None] * depth, [None] * depth
    g_wa, g_wb, g_wkv, g_wout = [None] * depth, [None] * depth, [None] * depth, [None] * depth
    g_own = [[None] * depth for _ in big]
    g_other = [[None] * depth for _ in big]

    def swap_of(jobs):
        return _swap_exchange([g for _, _, g, _ in jobs])

    def chip_sums(jobs, got):
        return [(lr, k, _add_half(f"grad_add_half{lr}_{k}", g, r_, cvec, t_), t_) for (lr, k, g, t_), r_ in zip(jobs, got)]

    def reduce_at_owner(tag, jobs, from_chips):
        halves = [_sum_chips(f"grad_sum_chips{lr}_{k}", h_, r_, mvec, t_) for (lr, k, h_, t_), r_ in zip(jobs, from_chips)]
        others = _swap_reduced(f"grad_swap_reduced{tag}", halves)
        for (lr, k, _, _), mine, other in zip(jobs, halves, others):
            g_own[k][lr], g_other[k][lr] = mine, other

    def job(lr, k, g4):
        return lr, k, g4, _tile_of(g4.shape[1] // 2, 256, 16)

    pending = []
    for l in reversed(range(depth)):
        xin, h, pa, pb, bpad, ccol4, crow4, o_sb, o_fx, lse_fx, mn, mkv, o_m, lse_m, y2 = saved[l]
        wa, wb, wkv, wout = layer_w[l]
        dy, dgate, g_outnorm[l] = _out_bwd(f"out_bwd{l}", dxb, o_sb, o_fx, o_m, pb, out_norm_w[l][None], wout, ts)
        g_wout[l] = _mm(f"dw_out{l}", y2, dxb, "tn", _tile_of(MIX_WIDTH, 640, LANES), d, F32)
        dq_m, dk_m, dv_m = _mem_bwd(f"mem_bwd{l}", pa, mkv, o_m, lse_m, dy, SB_WIDTH + FOX_WIDTH)
        dmkv = jnp.concatenate([dk_m, dv_m], axis=1)
        g_wkv[l] = _mm(f"dw_kv{l}", mn, dmkv, "tn", d, 2 * MEM_WIDTH, F32)
        dmn = _mm(f"dmem{l}", dmkv, wkv, "nt", mems.shape[0], d, F32)
        g_memnorm[l] = _rms_wgrad(f"mem_norm_grad{l}", mems, dmn)
        small = [job(l, 1, g_wkv[l].reshape(N_CHIPS, -1, g_wkv[l].shape[1])), job(l, 2, g_wout[l].reshape(N_CHIPS, -1, d))]
        (dq_fx, dk_fx, dv_fx, cs4), got = _fox_bwd(f"fox_bwd{l}", pa, 3 * SB_WIDTH, ccol4, crow4, o_fx, lse_fx, dy,
                                                    SB_WIDTH, carried=swap_of(small))
        pending += chip_sums(small, got)
        (dq_sb, dk_sb, dv_sb), from_chips = _sb_bwd(f"sb_bwd{l}", pa, 0, dy, 0,
                                                   carried=_scatter_exchange([j[2] for j in pending]))
        reduce_at_owner(f"{l}s", pending, from_chips)
        dpb, g_b[l] = _gate_bwd(f"gate_bwd{l}", pb, bpad, cs4, fl_block, dgate)
        dpa = jnp.concatenate([dq_sb, dk_sb, dv_sb, dq_fx, dk_fx, dv_fx, dq_m], axis=1)
        tw = _tile_of(d, 512, LANES)
        g_wa[l] = _mm(f"dw_in_a{l}", h, dpa, "tn", tw, _tile_of(PA, 1664, LANES), BF16)
        g_wb[l] = _mm(f"dw_in_b{l}", h, dpb, "tn", tw, PB, BF16)
        g4_in = jnp.stack([_shard_from_groups(g_wa[l], g_wb[l], j, cw) for j in range(N_CHIPS)])
        w_in_job = [job(l, 0, g4_in)]
        if l > 0:
            dx, dxb, g_norm[l], got = _inproj_bwd(f"inproj_bwd{l}", dpa, dpb, wa, wb, xin, norm_w[l][None], dx, ts,
                                                  carried=swap_of(w_in_job))
            pending = chip_sums(w_in_job, got)
        else:
            pending = chip_sums(w_in_job, _run_exchange("grad_swap_halves_last", swap_of(w_in_job)))
            dx, dxb, g_norm[l], from_chips = _inproj_bwd(f"inproj_bwd{l}", dpa, dpb, wa, wb, xin, norm_w[l][None], dx, ts,
                                                          carried=_scatter_exchange([j[2] for j in pending]))
            reduce_at_owner("last", pending, from_chips)

    small_w = [norm_w, b_forget, mem_norm_w, out_norm_w, final_norm_w]
    small_m = [m_norm_w, m_b_forget, m_mem_norm_w, m_out_norm_w, m_final_norm_w]
    small_v = [v_norm_w, v_b_forget, v_mem_norm_w, v_out_norm_w, v_final_norm_w]
    rows2 = lambda a: a.reshape(-1, a.shape[-1])
    partials = [jnp.concatenate(g_norm, axis=0), jnp.concatenate(g_b, axis=0), jnp.concatenate(g_memnorm, axis=0),
                jnp.concatenate(g_outnorm, axis=0), g_final, loss_v]
    sums, updates = _small_update("small_update", partials, [rows2(a) for a in small_w] + [None],
                                  [rows2(a) for a in small_m] + [None], [rows2(a) for a in small_v] + [None])
    small_grads = [g.reshape(a.shape) for g, a in zip(sums, small_w)]
    loss = sums[-1][0, 0]
    small_delta, small_m2, small_v2 = ([updates[3 * k + t].reshape(a.shape) for k, a in enumerate(small_w)]
                                       for t in range(3))
    big_grads, big_delta, big_m2, big_v2 = [], [], [], []
    for k, (nm, w_, m_, v_) in enumerate(zip(("w_in", "w_mem_kv", "w_out"), big, (m_w_in, m_w_mem_kv, m_w_out),
                                             (v_w_in, v_w_mem_kv, v_w_out))):
        if w_.shape[2] % LANES:
            g_full = jnp.stack([jnp.concatenate([jnp.where(core == 0, go, gt), jnp.where(core == 0, gt, go)], axis=0)
                                for go, gt in zip(g_own[k], g_other[k])])
            w_p, g_p, m_p, v_p = (_column_major_rows(a) for a in (w_, g_full, m_, v_))
            outs = _adamw(f"adamw_{nm}", w_p, g_p, m_p, v_p, _tile_of(w_p.shape[0], 600, 1))
            outs = [_from_column_major_rows(o, w_.shape) for o in (g_p, *outs)]
        else:
            outs = _adamw_sharded(f"adamw_{nm}", w_, m_, v_, g_own[k], g_other[k], cvec,
                                  _tile_of(w_.shape[1] // 2, 256, 8))
        for lst, o in zip((big_grads, big_delta, big_m2, big_v2), outs):
            lst.append(o)

    def order(sm, bg):
        return [sm[0], bg[0], sm[1], sm[2], bg[1], sm[3], bg[2], sm[4]]

    return (loss, dx[None], *order(small_grads, big_grads), *order(small_delta, big_delta),
            *order(small_m2, big_m2), *order(small_v2, big_v2))
```

```python
import functools

import jax
import jax.numpy as jnp
from jax import lax
from jax.experimental import pallas as pl
from jax.experimental.pallas import tpu as pltpu

F32 = jnp.float32
BF16 = jnp.bfloat16

HEAD_DIM = 64
SB_WIDTH = 512
FOX_WIDTH = 512
FOX_HEADS = 8
MEM_WIDTH = 256
MIX_WIDTH = SB_WIDTH + FOX_WIDTH + MEM_WIDTH
TOTAL_HEADS = MIX_WIDTH // HEAD_DIM
IN_WIDTH = 3 * SB_WIDTH + 3 * FOX_WIDTH + FOX_HEADS + MEM_WIDTH + MIX_WIDTH
LANES = 128
QKV_WIDTH = 3 * SB_WIDTH + 3 * FOX_WIDTH
PA = QKV_WIDTH + MEM_WIDTH
PB = LANES + MIX_WIDTH
EPS = 1e-6
SCALE = HEAD_DIM ** -0.5
TILE = 256
SB_GROUP = 4
SB_LANES = SB_GROUP * HEAD_DIM
FOX_GROUP = 4
FOX_LANES = FOX_GROUP * HEAD_DIM
NEG_INF = float("-inf")
MASKED = -1e30

ADAM_LR = 0.001
ADAM_B1 = 0.9
ADAM_B2 = 0.999
ADAM_EPS = 1e-08
ADAM_WD = 0.01
ADAM_STEP = 10

N_CHIPS = 4
N_DEV = 8
VMEM_LIMIT = 48 * 1024 * 1024
MESH = pl.DeviceIdType.MESH


def _params(*sem):
    return pltpu.CompilerParams(dimension_semantics=tuple(sem), vmem_limit_bytes=VMEM_LIMIT)


def _dot(a, b):
    return jnp.dot(a, b, preferred_element_type=F32)


def _dot_nt(a, b):
    return lax.dot_general(a, b, (((1,), (1,)), ((), ())), preferred_element_type=F32)


def _dot_tn(a, b):
    return lax.dot_general(a, b, (((0,), (0,)), ((), ())), preferred_element_type=F32)


def _split2(x):
    hi = x.astype(BF16)
    lo = (x - hi.astype(F32)).astype(BF16)
    return hi, lo


def _split3(x):
    hi = x.astype(BF16)
    r = x - hi.astype(F32)
    mid = r.astype(BF16)
    lo = (r - mid.astype(F32)).astype(BF16)
    return hi, mid, lo


def _sum_l2(x, u):
    hi, lo = _split2(x)
    return _dot(hi, u) + _dot(lo, u)


def _sum_l3(x, u):
    hi, mid, lo = _split3(x)
    return _dot(hi, u) + _dot(mid, u) + _dot(lo, u)


def _sum_r3(u, x):
    hi, mid, lo = _split3(x)
    return _dot(u, hi) + _dot(u, mid) + _dot(u, lo)


def _softplus(z):
    return jnp.maximum(z, 0.0) + jnp.log1p(jnp.exp(-jnp.abs(z)))


def _tri(n, pred):
    r = lax.broadcasted_iota(jnp.int32, (n, n), 0)
    c = lax.broadcasted_iota(jnp.int32, (n, n), 1)
    return jnp.where(pred(r, c), 1.0, 0.0).astype(BF16)


def _rows(ref, j, n=TILE):
    return pl.ds(pl.multiple_of(j * n, n), n)


def _mm(name, a, b, mode, tm, tn, out_dtype, res=None, a_lead=(), b_lead=()):
    a2, b2 = a.shape[len(a_lead):], b.shape[len(b_lead):]
    if mode == "tn":
        k, m = a2
    else:
        m, k = a2
    n = b2[0] if mode == "nt" else b2[1]
    assert m % tm == 0 and n % tn == 0, (name, m, tm, n, tn)
    na, nb = (None,) * len(a_lead), (None,) * len(b_lead)
    if mode == "tn":
        a_spec = pl.BlockSpec(na + (k, tm), lambda j, i: a_lead + (0, i))
    else:
        a_spec = pl.BlockSpec(na + (tm, k), lambda j, i: a_lead + (i, 0))
    if mode == "nt":
        b_spec = pl.BlockSpec(nb + (tn, k), lambda j, i: b_lead + (j, 0))
    else:
        b_spec = pl.BlockSpec(nb + (k, tn), lambda j, i: b_lead + (0, j))
    o_spec = pl.BlockSpec((tm, tn), lambda j, i: (i, j))
    dot = {"nn": _dot, "nt": _dot_nt, "tn": _dot_tn}[mode]

    def body(a_ref, b_ref, *rest):
        o_ref = rest[-1]
        acc = dot(a_ref[...].astype(BF16), b_ref[...].astype(BF16))
        if res is not None:
            acc = acc + rest[0][...]
        o_ref[...] = acc.astype(o_ref.dtype)

    args, specs = [a, b], [a_spec, b_spec]
    if res is not None:
        args.append(res)
        specs.append(o_spec)
    return pl.pallas_call(
        body, name=name, grid=(n // tn, m // tm), in_specs=specs, out_specs=o_spec,
        out_shape=jax.ShapeDtypeStruct((m, n), out_dtype),
        compiler_params=_params("parallel", "parallel"),
    )(*args)


def _rms_fwd(name, x, g, ts):
    s, d = x.shape

    def body(x_ref, g_ref, o_ref):
        xf = x_ref[...]
        r = lax.rsqrt(jnp.mean(xf * xf, axis=1, keepdims=True) + EPS)
        o_ref[...] = (xf * r * g_ref[...]).astype(BF16)

    return pl.pallas_call(
        body, name=name, grid=(s // ts,),
        in_specs=[pl.BlockSpec((ts, d), lambda i: (i, 0)), pl.BlockSpec((1, d), lambda i: (0, 0))],
        out_specs=pl.BlockSpec((ts, d), lambda i: (i, 0)),
        out_shape=jax.ShapeDtypeStruct((s, d), BF16),
        compiler_params=_params("parallel"),
    )(x, g)


def _inproj_bwd(name, dpa, dpb, wa, wb, x, g, dres, ts, carried=None):
    s, d = x.shape

    def body(dpa_ref, dpb_ref, wa_ref, wb_ref, x_ref, g_ref, dres_ref, dx_ref, dxb_ref, dg_ref):
        @pl.when(pl.program_id(1) == 0)
        def _():
            dg_ref[...] = jnp.zeros_like(dg_ref)

        dhf = _dot_nt(dpa_ref[...], wa_ref[...]) + _dot_nt(dpb_ref[...], wb_ref[...])
        xf = x_ref[...]
        r = lax.rsqrt(jnp.mean(xf * xf, axis=1, keepdims=True) + EPS)
        xh = xf * r
        dg_ref[...] += jnp.sum(dhf * xh, axis=0, keepdims=True)
        dxh = dhf * g_ref[...]
        m = jnp.mean(dxh * xh, axis=1, keepdims=True)
        dx = r * (dxh - xh * m) + dres_ref[...]
        dx_ref[...] = dx
        dxb_ref[...] = dx.astype(BF16)

    row = lambda w: pl.BlockSpec((ts, w), lambda p, i: (i, 0))
    whole = lambda a: pl.BlockSpec(a.shape, lambda p, i: (0, 0))
    outs = _pair_grid_call(
        name, body, s // ts,
        in_specs=[row(dpa.shape[1]), row(dpb.shape[1]), whole(wa), whole(wb), row(d), whole(g), row(d)],
        out_specs=[row(d), row(d), pl.BlockSpec((1, d), lambda p, i: (0, 0))],
        out_shape=[jax.ShapeDtypeStruct((s, d), F32), jax.ShapeDtypeStruct((s, d), BF16),
                   jax.ShapeDtypeStruct((1, d), F32)],
        scratch=[], args=(dpa, dpb, wa, wb, x, g, dres), carried=carried, groups=1)
    return outs[0], outs[1], outs[2], outs[3:]


def _rms_wgrad(name, x, dh):
    m_, d = x.shape

    def body(x_ref, dh_ref, dg_ref):
        xf = x_ref[...]
        r = lax.rsqrt(jnp.mean(xf * xf, axis=1, keepdims=True) + EPS)
        dg_ref[...] = jnp.sum(dh_ref[...] * xf * r, axis=0, keepdims=True)

    return pl.pallas_call(
        body, name=name, out_shape=jax.ShapeDtypeStruct((1, d), F32),
    )(x, dh)


def _final_loss(name, x, g, target, ts):
    s, d = x.shape

    def body(x_ref, g_ref, t_ref, loss_ref, dx_ref, dxb_ref, dg_ref):
        @pl.when(pl.program_id(0) == 0)
        def _():
            dg_ref[...] = jnp.zeros_like(dg_ref)
            loss_ref[...] = jnp.zeros_like(loss_ref)

        xf = x_ref[...]
        gw = g_ref[...]
        r = lax.rsqrt(jnp.mean(xf * xf, axis=1, keepdims=True) + EPS)
        xh = xf * r
        e = xh * gw - t_ref[...]
        part = 0.5 * jnp.sum(jnp.mean(e * e, axis=1, keepdims=True), axis=0, keepdims=True)
        loss_ref[...] += jnp.broadcast_to(part, loss_ref.shape)
        dy = e * (1.0 / d)
        dg_ref[...] += jnp.sum(dy * xh, axis=0, keepdims=True)
        dxh = dy * gw
        m = jnp.mean(dxh * xh, axis=1, keepdims=True)
        dx = r * (dxh - xh * m)
        dx_ref[...] = dx
        dxb_ref[...] = dx.astype(BF16)

    row = pl.BlockSpec((ts, d), lambda i: (i, 0))
    vec = pl.BlockSpec((1, d), lambda i: (0, 0))
    lvec = pl.BlockSpec((1, LANES), lambda i: (0, 0))
    return pl.pallas_call(
        body, name=name, grid=(s // ts,), in_specs=[row, vec, row], out_specs=[lvec, row, row, vec],
        out_shape=[jax.ShapeDtypeStruct((1, LANES), F32), jax.ShapeDtypeStruct((s, d), F32),
                   jax.ShapeDtypeStruct((s, d), BF16), jax.ShapeDtypeStruct((1, d), F32)],
        compiler_params=_params("arbitrary"),
    )(x, g, target)


def _gate_fwd(name, pb, bpad, fl_block):
    s = pb.shape[0]
    nb = s // TILE
    fg = FOX_HEADS // FOX_GROUP

    def body(fl_ref, b_ref, ccol_ref, crow_ref, carry):
        @pl.when(pl.program_id(0) == 0)
        def _():
            carry[...] = jnp.zeros_like(carry)

        u = fl_ref[...] + b_ref[...]
        lf = jnp.minimum(u, 0.0) - jnp.log1p(jnp.exp(-jnp.abs(u)))
        lower = _tri(TILE, lambda r, c: c <= r)
        c = _sum_r3(lower, lf) + carry[0:1, :]
        for grp in range(fg):
            ccol_ref[grp] = jnp.concatenate(
                [jnp.broadcast_to(c[:, grp * FOX_GROUP + hh:grp * FOX_GROUP + hh + 1], (TILE, HEAD_DIM))
                 for hh in range(FOX_GROUP)], axis=1)
        crow_ref[0] = c.T[0:8, :]
        carry[...] = jnp.broadcast_to(c[TILE - 1:TILE, :], carry.shape)

    return pl.pallas_call(
        body, name=name, grid=(nb,),
        in_specs=[pl.BlockSpec((TILE, LANES), lambda i: (i, fl_block)), pl.BlockSpec((1, LANES), lambda i: (0, 0))],
        out_specs=[pl.BlockSpec((fg, TILE, FOX_LANES), lambda i: (0, i, 0)), pl.BlockSpec((1, 8, TILE), lambda i: (i, 0, 0))],
        out_shape=[jax.ShapeDtypeStruct((fg, s, FOX_LANES), F32), jax.ShapeDtypeStruct((nb, 8, TILE), F32)],
        scratch_shapes=[pltpu.VMEM((8, LANES), F32)],
        compiler_params=_params("arbitrary"),
    )(pb, bpad)


def _gate_bwd(name, pb, bpad, colsum, fl_block, dpb):
    s = pb.shape[0]
    nb = s // TILE

    def body(fl_ref, b_ref, cs_ref, dpb_ref, dl_ref, db_ref, carry):
        @pl.when(pl.program_id(0) == 0)
        def _():
            carry[...] = jnp.zeros_like(carry)
            db_ref[...] = jnp.zeros_like(db_ref)

        upper = _tri(TILE, lambda r, c: r >= c)
        rsum = _sum_l3(cs_ref[0], upper) + carry[:, 0:1]
        carry[...] = jnp.broadcast_to(rsum[:, 0:1], carry.shape)
        full = jnp.concatenate([rsum, jnp.zeros((LANES - 8, TILE), F32)], axis=0)
        dlf = -full.T
        u = fl_ref[...] + b_ref[...]
        dlogit = dlf * (1.0 - jax.nn.sigmoid(u))
        dl_ref[...] = dlogit.astype(BF16)
        db_ref[...] += jnp.sum(dlogit, axis=0, keepdims=True)

    logits_block = pl.BlockSpec((TILE, LANES), lambda i: (nb - 1 - i, fl_block))
    return pl.pallas_call(
        body, name=name, grid=(nb,),
        in_specs=[logits_block, pl.BlockSpec((1, LANES), lambda i: (0, 0)),
                  pl.BlockSpec((1, 8, TILE), lambda i: (nb - 1 - i, 0, 0)), pl.BlockSpec(memory_space=pl.ANY)],
        out_specs=[logits_block, pl.BlockSpec((1, LANES), lambda i: (0, 0))],
        out_shape=[jax.ShapeDtypeStruct(dpb.shape, BF16), jax.ShapeDtypeStruct((1, LANES), F32)],
        scratch_shapes=[pltpu.VMEM((8, LANES), F32)], input_output_aliases={3: 0},
        compiler_params=_params("arbitrary"),
    )(pb, bpad, colsum, dpb)


def _head_slices(hh):
    return slice(HEAD_DIM * hh, HEAD_DIM * (hh + 1))


def _scaled_q(q_ref, sl, scale=SCALE):
    return (q_ref[:, sl].astype(F32) * scale).astype(BF16)


def _neg_abs(x):
    sign = jnp.uint32(0x80000000)
    return lax.bitcast_convert_type(lax.bitcast_convert_type(x, jnp.uint32) | sign, F32)


def _sb_tile(qn, kj, carry, strict, u_after, diag):
    nz = _dot_nt(qn, kj)
    lf = jnp.minimum(nz, 0.0) - jnp.log(1.0 + jnp.exp(_neg_abs(nz)))
    lsig = lf - nz
    if diag:
        lf = jnp.where(strict, lf, 0.0)
    sx = _dot(lf.astype(BF16), u_after)
    a = jnp.exp(lsig + sx + carry)
    if diag:
        a = jnp.where(strict, a, 0.0)
    return lsig, a, carry + sx[:, 0:1] + lf[:, 0:1]


def _pair_grid_call(name, body, nb, in_specs, out_specs, out_shape, scratch, args, carried=None, groups=4):
    if carried is None:
        return pl.pallas_call(
            body, name=name, grid=(groups, nb), in_specs=in_specs, out_specs=out_specs, out_shape=out_shape,
            scratch_shapes=scratch, compiler_params=_params("arbitrary", "arbitrary"),
        )(*args)
    n_in, n_out, n_ex = len(in_specs), len(out_specs), carried.n

    def body_with_copies(*refs):
        own_in, ex_in = refs[:n_in], refs[n_in:n_in + n_ex]
        own_out = refs[n_in + n_ex:n_in + n_ex + n_out]
        ex_out = refs[n_in + n_ex + n_out:n_in + 2 * n_ex + n_out]
        own_scratch, sems = refs[n_in + 2 * n_ex + n_out:-2], refs[-2:]
        parts = (ex_in, ex_out, sems[0], sems[1])
        p, i = pl.program_id(0), pl.program_id(1)
        pl.when(jnp.logical_and(p == 0, i == 0))(lambda: carried.begin(*parts))
        if carried.relay is not None:
            pl.when(jnp.logical_and(p == groups - 1, i == max(nb - 2, 0)))(lambda: carried.relay(*parts))
        body(*own_in, *own_out, *own_scratch)
        pl.when(jnp.logical_and(p == groups - 1, i == nb - 1))(lambda: carried.finish(*parts))

    return pl.pallas_call(
        body_with_copies, name=name, grid=(groups, nb), in_specs=list(in_specs) + [HBM_SPEC] * n_ex,
        out_specs=list(out_specs) + [HBM_SPEC] * n_ex, out_shape=list(out_shape) + carried.out_shapes,
        scratch_shapes=list(scratch) + _dma_sems(carried.n_sems),
        compiler_params=_params("arbitrary", "arbitrary"),
    )(*args, *carried.inputs)


def _sb_fwd(name, pa, col0, carried=None):
    s = pa.shape[0]
    nb = s // TILE
    cb = col0 // SB_LANES
    kb = SB_WIDTH // SB_LANES

    def body(q_ref, k_ref, v_ref, o_ref, lsig_s, lf_s):
        i = pl.program_id(1)
        r = lax.broadcasted_iota(jnp.int32, (TILE, TILE), 0)
        c = lax.broadcasted_iota(jnp.int32, (TILE, TILE), 1)
        strict = c < r
        u_after = _tri(TILE, lambda rr, cc: rr > cc)
        qs = [_scaled_q(q_ref, _head_slices(hh), -SCALE) for hh in range(SB_GROUP)]

        def neg_z(j):
            kblk = k_ref[_rows(k_ref, j), :]
            return [_dot_nt(qs[hh], kblk[:, _head_slices(hh)]) for hh in range(SB_GROUP)]

        def scores(nzs, slot, diag):
            for hh, nz in enumerate(nzs):
                lf = jnp.minimum(nz, 0.0) - jnp.log(1.0 + jnp.exp(_neg_abs(nz)))
                lsig = lf - nz
                if diag:
                    lf = jnp.where(strict, lf, 0.0)
                    lsig = jnp.where(strict, lsig, MASKED)
                lsig_s[slot, hh] = lsig
                lf_s[slot, hh] = lf.astype(BF16)

        def weigh(j, slot, state):
            vblk = v_ref[_rows(v_ref, j), :]
            new = []
            for hh in range(SB_GROUP):
                carry, acc = state[hh]
                lfb = lf_s[slot, hh]
                sx = _dot(lfb, u_after)
                a = jnp.exp(lsig_s[slot, hh] + sx + carry)
                new.append((carry + sx[:, 0:1] + lfb[:, 0:1].astype(F32),
                            acc + _dot(a.astype(BF16), vblk[:, _head_slices(hh)])))
            return tuple(new)

        def step(t, state):
            state = weigh(i - t + 1, (t - 1) % 2, state)
            scores(neg_z(i - t), t % 2, False)
            return state

        zero = (jnp.zeros((TILE, 1), F32), jnp.zeros((TILE, HEAD_DIM), F32))
        scores(neg_z(i), 0, True)
        state = lax.fori_loop(1, i + 1, step, (zero,) * SB_GROUP)
        state = weigh(0, i % 2, state)
        o_ref[...] = jnp.concatenate([st[1] for st in state], axis=1)

    outs = _pair_grid_call(
        name, body, nb,
        in_specs=[pl.BlockSpec((TILE, SB_LANES), lambda p, i: (i, cb + p)),
                  pl.BlockSpec((s, SB_LANES), lambda p, i: (0, cb + kb + p)),
                  pl.BlockSpec((s, SB_LANES), lambda p, i: (0, cb + 2 * kb + p))],
        out_specs=[pl.BlockSpec((TILE, SB_LANES), lambda p, i: (i, p))],
        out_shape=[jax.ShapeDtypeStruct((s, SB_WIDTH), F32)],
        scratch=[pltpu.VMEM((2, SB_GROUP, TILE, TILE), F32), pltpu.VMEM((2, SB_GROUP, TILE, TILE), BF16)],
        args=(pa, pa, pa), carried=carried, groups=kb)
    return outs[0], outs[1:]


def _sb_bwd(name, pa, col0, dout, dcol0, carried=None):
    s = pa.shape[0]
    nb = s // TILE
    cb = col0 // SB_LANES
    kb = SB_WIDTH // SB_LANES
    db = dcol0 // SB_LANES

    def body(q_ref, k_ref, v_ref, do_ref, dq_ref, dk_ref, dv_ref, dk_acc, dv_acc, dpan, span, gsum, lsig_s, lf_s):
        i = pl.program_id(1)

        @pl.when(i == 0)
        def _():
            dk_acc[...] = jnp.zeros_like(dk_acc)
            dv_acc[...] = jnp.zeros_like(dv_acc)

        r = lax.broadcasted_iota(jnp.int32, (TILE, TILE), 0)
        c = lax.broadcasted_iota(jnp.int32, (TILE, TILE), 1)
        strict = c < r
        u_after = _tri(TILE, lambda rr, cc: rr > cc)
        u_before = _tri(TILE, lambda rr, cc: rr < cc)
        qs = [_scaled_q(q_ref, _head_slices(hh), -SCALE) for hh in range(SB_GROUP)]
        dos = [do_ref[:, _head_slices(hh)].astype(BF16) for hh in range(SB_GROUP)]
        dots = [do_ref[:, _head_slices(hh)].T.astype(BF16) for hh in range(SB_GROUP)]
        qts = [q.astype(F32).T.astype(BF16) for q in qs]

        def scores(j, slot, diag):
            kblk = k_ref[_rows(k_ref, j), :]
            for hh in range(SB_GROUP):
                nz = _dot_nt(qs[hh], kblk[:, _head_slices(hh)])
                lf = jnp.minimum(nz, 0.0) - jnp.log(1.0 + jnp.exp(_neg_abs(nz)))
                lsig = lf - nz
                if diag:
                    lf = jnp.where(strict, lf, 0.0)
                    lsig = jnp.where(strict, lsig, MASKED)
                lsig_s[slot, hh] = lsig
                lf_s[slot, hh] = lf.astype(BF16)

        def grads(j, slot, carries):
            vblk = v_ref[_rows(v_ref, j), :]
            new = []
            for hh in range(SB_GROUP):
                lfb = lf_s[slot, hh]
                lsig = lsig_s[slot, hh]
                sx = _dot(lfb, u_after)
                a = jnp.exp(lsig + sx + carries[hh])
                g = a * _dot_nt(dos[hh], vblk[:, _head_slices(hh)])
                sig = jnp.exp(lsig)
                inside = _dot(g.astype(BF16), u_before)
                dpan[hh, j] = sig * (inside + g) - g
                span[hh, j] = sig
                gsum[hh, j] = inside[:, TILE - 1:TILE] + g[:, TILE - 1:TILE]
                dv_acc[hh, j] += _dot(dots[hh], a.astype(BF16))
                new.append(carries[hh] + sx[:, 0:1] + lfb[:, 0:1].astype(F32))
            return tuple(new)

        def step1(t, carries):
            carries = grads(i - t + 1, (t - 1) % 2, carries)
            scores(i - t, t % 2, False)
            return carries

        zero1 = jnp.zeros((TILE, 1), F32)
        scores(i, 0, True)
        carries = lax.fori_loop(1, i + 1, step1, (zero1,) * SB_GROUP)
        grads(0, i % 2, carries)

        def pass2(j, state):
            kblk = k_ref[_rows(k_ref, j), :]
            new = []
            for hh in range(SB_GROUP):
                before, ndq = state[hh]
                ndzb = (dpan[hh, j] + span[hh, j] * before).astype(BF16)
                dk_acc[hh, j] += _dot(qts[hh], ndzb)
                new.append((before + gsum[hh, j], ndq + _dot(ndzb, kblk[:, _head_slices(hh)])))
            return tuple(new)

        zero2 = (zero1, jnp.zeros((TILE, HEAD_DIM), F32))
        state = lax.fori_loop(0, i + 1, pass2, (zero2,) * SB_GROUP)
        dq_ref[...] = jnp.concatenate([st[1] * -SCALE for st in state], axis=1).astype(BF16)

        @pl.when(i == nb - 1)
        def _():
            for acc, ref in ((dk_acc, dk_ref), (dv_acc, dv_ref)):
                for j in range(nb):
                    ref[j * TILE:(j + 1) * TILE, :] = jnp.concatenate(
                        [acc[hh, j].T for hh in range(SB_GROUP)], axis=1).astype(BF16)

    qspec = pl.BlockSpec((TILE, SB_LANES), lambda p, i: (i, p))
    kvspec = pl.BlockSpec((s, SB_LANES), lambda p, i: (0, p))
    out = jax.ShapeDtypeStruct((s, SB_WIDTH), BF16)
    outs = _pair_grid_call(
        name, body, nb,
        in_specs=[pl.BlockSpec((TILE, SB_LANES), lambda p, i: (i, cb + p)),
                  pl.BlockSpec((s, SB_LANES), lambda p, i: (0, cb + kb + p)),
                  pl.BlockSpec((s, SB_LANES), lambda p, i: (0, cb + 2 * kb + p)),
                  pl.BlockSpec((TILE, SB_LANES), lambda p, i: (i, db + p))],
        out_specs=[qspec, kvspec, kvspec], out_shape=[out, out, out],
        scratch=[pltpu.VMEM((SB_GROUP, nb, HEAD_DIM, TILE), F32), pltpu.VMEM((SB_GROUP, nb, HEAD_DIM, TILE), F32),
                 pltpu.VMEM((SB_GROUP, nb, TILE, TILE), F32), pltpu.VMEM((SB_GROUP, nb, TILE, TILE), F32),
                 pltpu.VMEM((SB_GROUP, nb, TILE, 1), F32),
                 pltpu.VMEM((2, SB_GROUP, TILE, TILE), F32), pltpu.VMEM((2, SB_GROUP, TILE, TILE), BF16)],
        args=(pa, pa, pa, dout), carried=carried, groups=kb)
    return outs[:3], outs[3:]


def _fox_scores(q, kj, cq, crj, causal, diag):
    sc = _dot_nt(q, kj) + (cq - crj)
    if diag:
        sc = jnp.where(causal, sc, NEG_INF)
    return sc


def _fox_fwd(name, pa, col0, ccol4, crow4, carried=None):
    s = pa.shape[0]
    nb = s // TILE
    cb = col0 // FOX_LANES
    kb = FOX_WIDTH // FOX_LANES

    def body(q_ref, k_ref, v_ref, cc_ref, cr_ref, o_ref, lse_ref, sc_s):
        i = pl.program_id(1)
        head0 = pl.program_id(0) * FOX_GROUP
        r = lax.broadcasted_iota(jnp.int32, (TILE, TILE), 0)
        c = lax.broadcasted_iota(jnp.int32, (TILE, TILE), 1)
        causal = c <= r
        qs = [_scaled_q(q_ref, _head_slices(hh)) for hh in range(FOX_GROUP)]
        cqs = [cc_ref[:, HEAD_DIM * hh:HEAD_DIM * hh + 1] for hh in range(FOX_GROUP)]

        def logits(j, slot, diag):
            kblk = k_ref[_rows(k_ref, j), :]
            tops = []
            for hh in range(FOX_GROUP):
                sc = _fox_scores(qs[hh], kblk[:, _head_slices(hh)], cqs[hh], cr_ref[j, pl.ds(head0 + hh, 1), :], causal, diag)
                sc_s[slot, hh] = sc
                tops.append(jnp.max(sc, axis=1, keepdims=True))
            return tuple(tops)

        def update(j, slot, tops, state):
            vblk = v_ref[_rows(v_ref, j), :]
            new = []
            for hh in range(FOX_GROUP):
                m, l, acc = state[hh]
                m2 = jnp.maximum(m, tops[hh])
                alpha = jnp.exp(m - m2)
                p = jnp.exp(sc_s[slot, hh] - m2)
                new.append((m2, l * alpha + jnp.sum(p, axis=1, keepdims=True),
                            acc * alpha + _dot(p.astype(BF16), vblk[:, _head_slices(hh)])))
            return tuple(new)

        def step(t, both):
            tops, state = both
            state = update(i - t + 1, (t - 1) % 2, tops, state)
            return logits(i - t, t % 2, False), state

        zero = (jnp.full((TILE, 1), NEG_INF, F32), jnp.zeros((TILE, 1), F32), jnp.zeros((TILE, HEAD_DIM), F32))
        tops, state = lax.fori_loop(1, i + 1, step, (logits(i, 0, True), (zero,) * FOX_GROUP))
        state = update(0, i % 2, tops, state)
        o_ref[...] = jnp.concatenate([st[2] / st[1] for st in state], axis=1)
        lse_ref[...] = jnp.concatenate(
            [jnp.broadcast_to(st[0] + jnp.log(st[1]), (TILE, HEAD_DIM)) for st in state], axis=1)

    outs = _pair_grid_call(
        name, body, nb,
        in_specs=[pl.BlockSpec((TILE, FOX_LANES), lambda p, i: (i, cb + p)),
                  pl.BlockSpec((s, FOX_LANES), lambda p, i: (0, cb + kb + p)),
                  pl.BlockSpec((s, FOX_LANES), lambda p, i: (0, cb + 2 * kb + p)),
                  pl.BlockSpec((None, TILE, FOX_LANES), lambda p, i: (p, i, 0)),
                  pl.BlockSpec((nb, 8, TILE), lambda p, i: (0, 0, 0))],
        out_specs=[pl.BlockSpec((TILE, FOX_LANES), lambda p, i: (i, p)),
                   pl.BlockSpec((None, TILE, FOX_LANES), lambda p, i: (p, i, 0))],
        out_shape=[jax.ShapeDtypeStruct((s, FOX_WIDTH), F32), jax.ShapeDtypeStruct((kb, s, FOX_LANES), F32)],
        scratch=[pltpu.VMEM((2, FOX_GROUP, TILE, TILE), F32)],
        args=(pa, pa, pa, ccol4, crow4), carried=carried, groups=kb)
    return outs[0], outs[1], outs[2:]


def _fox_bwd(name, pa, col0, ccol4, crow4, out, lse, dout, dcol0, carried=None):
    s = pa.shape[0]
    nb = s // TILE
    cb = col0 // FOX_LANES
    kb = FOX_WIDTH // FOX_LANES
    db = dcol0 // FOX_LANES

    def body(q_ref, k_ref, v_ref, cc_ref, cr_ref, o_ref, lse_ref, do_ref,
             dq_ref, dk_ref, dv_ref, cs_ref, dk_acc, dv_acc, p_s, ds_s):
        i = pl.program_id(1)
        head0 = pl.program_id(0) * FOX_GROUP

        @pl.when(i == 0)
        def _():
            dk_acc[...] = jnp.zeros_like(dk_acc)
            dv_acc[...] = jnp.zeros_like(dv_acc)

        @pl.when(jnp.logical_and(i == 0, head0 == 0))
        def _():
            cs_ref[...] = jnp.zeros_like(cs_ref)

        r = lax.broadcasted_iota(jnp.int32, (TILE, TILE), 0)
        c = lax.broadcasted_iota(jnp.int32, (TILE, TILE), 1)
        causal = c <= r
        qs = [_scaled_q(q_ref, _head_slices(hh)) for hh in range(FOX_GROUP)]
        cqs = [cc_ref[:, HEAD_DIM * hh:HEAD_DIM * hh + 1] for hh in range(FOX_GROUP)]
        lses = [lse_ref[:, HEAD_DIM * hh:HEAD_DIM * hh + 1] for hh in range(FOX_GROUP)]
        dofs = [do_ref[:, _head_slices(hh)] for hh in range(FOX_GROUP)]
        dos = [d_.astype(BF16) for d_ in dofs]
        dots = [d_.T.astype(BF16) for d_ in dofs]
        qts = [q.astype(F32).T.astype(BF16) for q in qs]
        deltas = [jnp.sum(dofs[hh] * o_ref[:, _head_slices(hh)], axis=1, keepdims=True) for hh in range(FOX_GROUP)]

        def probs(j, slot, rowsums, diag):
            kblk = k_ref[_rows(k_ref, j), :]
            vblk = v_ref[_rows(v_ref, j), :]
            new = []
            for hh in range(FOX_GROUP):
                sl = _head_slices(hh)
                sc = _fox_scores(qs[hh], kblk[:, sl], cqs[hh], cr_ref[j, pl.ds(head0 + hh, 1), :], causal, diag)
                p = jnp.exp(sc - lses[hh])
                ds = p * (_dot_nt(dos[hh], vblk[:, sl]) - deltas[hh])
                p_s[slot, hh] = p.astype(BF16)
                ds_s[slot, hh] = ds.astype(BF16)
                cs_ref[j, pl.ds(head0 + hh, 1), :] += jnp.sum(ds, axis=0, keepdims=True)
                new.append(rowsums[hh] + jnp.sum(ds, axis=1, keepdims=True))
            return tuple(new)

        def accumulate(j, slot, dqs):
            kblk = k_ref[_rows(k_ref, j), :]
            new = []
            for hh in range(FOX_GROUP):
                dsb = ds_s[slot, hh]
                dv_acc[hh, j] += _dot(dots[hh], p_s[slot, hh])
                dk_acc[hh, j] += _dot(qts[hh], dsb)
                new.append(dqs[hh] + _dot(dsb, kblk[:, _head_slices(hh)]))
            return tuple(new)

        def step(t, both):
            rowsums, dqs = both
            dqs = accumulate(i - t + 1, (t - 1) % 2, dqs)
            return probs(i - t, t % 2, rowsums, False), dqs

        zero1 = jnp.zeros((TILE, 1), F32)
        zero64 = jnp.zeros((TILE, HEAD_DIM), F32)
        rowsums, dqs = lax.fori_loop(1, i + 1, step,
                                     (probs(i, 0, (zero1,) * FOX_GROUP, True), (zero64,) * FOX_GROUP))
        dqs = accumulate(0, i % 2, dqs)
        for hh in range(FOX_GROUP):
            cs_ref[i, pl.ds(head0 + hh, 1), :] -= jnp.broadcast_to(rowsums[hh], (TILE, LANES)).T[0:1, :]
        dq_ref[...] = jnp.concatenate([dq * SCALE for dq in dqs], axis=1).astype(BF16)

        @pl.when(i == nb - 1)
        def _():
            for acc, ref in ((dk_acc, dk_ref), (dv_acc, dv_ref)):
                for j in range(nb):
                    ref[j * TILE:(j + 1) * TILE, :] = jnp.concatenate(
                        [acc[hh, j].T for hh in range(FOX_GROUP)], axis=1).astype(BF16)

    qspec = pl.BlockSpec((TILE, FOX_LANES), lambda p, i: (i, p))
    kvspec = pl.BlockSpec((s, FOX_LANES), lambda p, i: (0, p))
    o3 = jax.ShapeDtypeStruct((s, FOX_WIDTH), BF16)
    outs = _pair_grid_call(
        name, body, nb,
        in_specs=[pl.BlockSpec((TILE, FOX_LANES), lambda p, i: (i, cb + p)),
                  pl.BlockSpec((s, FOX_LANES), lambda p, i: (0, cb + kb + p)),
                  pl.BlockSpec((s, FOX_LANES), lambda p, i: (0, cb + 2 * kb + p)),
                  pl.BlockSpec((None, TILE, FOX_LANES), lambda p, i: (p, i, 0)),
                  pl.BlockSpec((nb, 8, TILE), lambda p, i: (0, 0, 0)),
                  qspec,
                  pl.BlockSpec((None, TILE, FOX_LANES), lambda p, i: (p, i, 0)),
                  pl.BlockSpec((TILE, FOX_LANES), lambda p, i: (i, db + p))],
        out_specs=[qspec, kvspec, kvspec, pl.BlockSpec((nb, 8, TILE), lambda p, i: (0, 0, 0))],
        out_shape=[o3, o3, o3, jax.ShapeDtypeStruct((nb, 8, TILE), F32)],
        scratch=[pltpu.VMEM((FOX_GROUP, nb, HEAD_DIM, TILE), F32), pltpu.VMEM((FOX_GROUP, nb, HEAD_DIM, TILE), F32),
                 pltpu.VMEM((2, FOX_GROUP, TILE, TILE), BF16), pltpu.VMEM((2, FOX_GROUP, TILE, TILE), BF16)],
        args=(pa, pa, pa, ccol4, crow4, out, lse, dout), carried=carried, groups=kb)
    return outs[:4], outs[4:]


def _mem_fwd(name, pa, mkv):
    s = pa.shape[0]
    ml = mkv.shape[0]
    nb = s // TILE
    cb = QKV_WIDTH // LANES

    def body(q_ref, k_ref, v_ref, o_ref, lse_ref):
        outs, lses = [], []
        for hh in range(2):
            sl = _head_slices(hh)
            sc = _dot_nt(_scaled_q(q_ref, sl), k_ref[:, sl])
            m = jnp.max(sc, axis=1, keepdims=True)
            p = jnp.exp(sc - m)
            l = jnp.sum(p, axis=1, keepdims=True)
            outs.append(_dot(p.astype(BF16), v_ref[:, sl]) / l)
            lses.append(jnp.broadcast_to(m + jnp.log(l), (TILE, HEAD_DIM)))
        o_ref[...] = jnp.concatenate(outs, axis=1)
        lse_ref[...] = jnp.concatenate(lses, axis=1)

    return pl.pallas_call(
        body, name=name, grid=(2, nb),
        in_specs=[pl.BlockSpec((TILE, LANES), lambda p, i: (i, cb + p)),
                  pl.BlockSpec((ml, LANES), lambda p, i: (0, p)),
                  pl.BlockSpec((ml, LANES), lambda p, i: (0, 2 + p))],
        out_specs=[pl.BlockSpec((TILE, LANES), lambda p, i: (i, p)),
                   pl.BlockSpec((None, TILE, LANES), lambda p, i: (p, i, 0))],
        out_shape=[jax.ShapeDtypeStruct((s, MEM_WIDTH), F32), jax.ShapeDtypeStruct((2, s, LANES), F32)],
        compiler_params=_params("parallel", "parallel"),
    )(pa, mkv, mkv)


def _mem_bwd(name, pa, mkv, out, lse, dout, dcol0):
    s = pa.shape[0]
    ml = mkv.shape[0]
    nb = s // TILE
    cb = QKV_WIDTH // LANES
    db = dcol0 // LANES

    def body(q_ref, k_ref, v_ref, o_ref, lse_ref, do_ref, dq_ref, dk_ref, dv_ref, dk_acc, dv_acc):
        i = pl.program_id(1)

        @pl.when(i == 0)
        def _():
            dk_acc[...] = jnp.zeros_like(dk_acc)
            dv_acc[...] = jnp.zeros_like(dv_acc)

        dqs = []
        for hh in range(2):
            sl = _head_slices(hh)
            q = _scaled_q(q_ref, sl)
            kh = k_ref[:, sl]
            dof = do_ref[:, sl]
            do = dof.astype(BF16)
            delta = jnp.sum(dof * o_ref[:, sl], axis=1, keepdims=True)
            p = jnp.exp(_dot_nt(q, kh) - lse_ref[:, HEAD_DIM * hh:HEAD_DIM * hh + 1])
            ds = (p * (_dot_nt(do, v_ref[:, sl]) - delta)).astype(BF16)
            dv_acc[hh] += _dot(dof.T.astype(BF16), p.astype(BF16))
            dk_acc[hh] += _dot(q.astype(F32).T.astype(BF16), ds)
            dqs.append(_dot(ds, kh) * SCALE)
        dq_ref[...] = jnp.concatenate(dqs, axis=1).astype(BF16)

        @pl.when(i == nb - 1)
        def _():
            dk_ref[...] = jnp.concatenate([dk_acc[0].T, dk_acc[1].T], axis=1).astype(BF16)
            dv_ref[...] = jnp.concatenate([dv_acc[0].T, dv_acc[1].T], axis=1).astype(BF16)

    qspec = pl.BlockSpec((TILE, LANES), lambda p, i: (i, p))
    kvspec = pl.BlockSpec((ml, LANES), lambda p, i: (0, p))
    okv = jax.ShapeDtypeStruct((ml, MEM_WIDTH), BF16)
    return pl.pallas_call(
        body, name=name, grid=(2, nb),
        in_specs=[pl.BlockSpec((TILE, LANES), lambda p, i: (i, cb + p)),
                  pl.BlockSpec((ml, LANES), lambda p, i: (0, p)),
                  pl.BlockSpec((ml, LANES), lambda p, i: (0, 2 + p)),
                  qspec,
                  pl.BlockSpec((None, TILE, LANES), lambda p, i: (p, i, 0)),
                  pl.BlockSpec((TILE, LANES), lambda p, i: (i, db + p))],
        out_specs=[qspec, kvspec, kvspec],
        out_shape=[jax.ShapeDtypeStruct((s, MEM_WIDTH), BF16), okv, okv],
        scratch_shapes=[pltpu.VMEM((2, HEAD_DIM, ml), F32), pltpu.VMEM((2, HEAD_DIM, ml), F32)],
        compiler_params=_params("arbitrary", "arbitrary"),
    )(pa, mkv, mkv, out, lse, dout)


def _head_maps():
    col = jnp.arange(MIX_WIDTH)[:, None] // HEAD_DIM
    g = (col == jnp.arange(LANES)[None, :]).astype(BF16)
    return g, g.T


def _normed_heads(osb_ref, ofx_ref, om_ref, g_ref, gt_ref):
    y = jnp.concatenate([osb_ref[...], ofx_ref[...], om_ref[...]], axis=1)
    msq = _sum_l2(y * y, g_ref[...]) * (1.0 / HEAD_DIM)
    rf = _sum_l3(lax.rsqrt(msq + EPS), gt_ref[...])
    return y * rf, rf


def _out_fwd(name, o_sb, o_fx, o_m, pb, ow, x, w_out, ts):
    s, d = x.shape
    g, gt = _head_maps()

    def body(osb_ref, ofx_ref, om_ref, gate_ref, ow_ref, x_ref, w_ref, g_ref, gt_ref, xo_ref, y2_ref):
        yh, _ = _normed_heads(osb_ref, ofx_ref, om_ref, g_ref, gt_ref)
        gate = gate_ref[...]
        y2 = (yh * ow_ref[...] * (gate * jax.nn.sigmoid(gate))).astype(BF16)
        y2_ref[...] = y2
        xo_ref[...] = x_ref[...] + _dot(y2, w_ref[...])

    return pl.pallas_call(
        body, name=name, grid=(s // ts,),
        in_specs=[_row_spec(ts, SB_WIDTH), _row_spec(ts, FOX_WIDTH), _row_spec(ts, MEM_WIDTH),
                  _row_spec(ts, MIX_WIDTH), _const_spec((1, MIX_WIDTH)), _row_spec(ts, d),
                  _const_spec((MIX_WIDTH, d)),
                  _const_spec((MIX_WIDTH, LANES)), _const_spec((LANES, MIX_WIDTH))],
        out_specs=[_row_spec(ts, d), _row_spec(ts, MIX_WIDTH)],
        out_shape=[jax.ShapeDtypeStruct((s, d), F32), jax.ShapeDtypeStruct((s, MIX_WIDTH), BF16)],
        compiler_params=_params("parallel"),
    )(o_sb, o_fx, o_m, pb, ow, x, w_out, g, gt)


def _row_spec(ts, w):
    return pl.BlockSpec((ts, w), lambda i: (i, 0))


def _const_spec(shape):
    return pl.BlockSpec(shape, lambda i: (0,) * len(shape))


def _out_bwd(name, dxb, o_sb, o_fx, o_m, pb, ow, w_out, ts):
    s, d = dxb.shape
    g, gt = _head_maps()

    def body(dx_ref, osb_ref, ofx_ref, om_ref, gate_ref, ow_ref, w_ref, g_ref, gt_ref, dy_ref, dgate_ref, dow_ref):
        @pl.when(pl.program_id(0) == 0)
        def _():
            dow_ref[...] = jnp.zeros_like(dow_ref)

        dy2 = _dot_nt(dx_ref[...], w_ref[...])
        yh, rf = _normed_heads(osb_ref, ofx_ref, om_ref, g_ref, gt_ref)
        gate = gate_ref[...]
        sig = jax.nn.sigmoid(gate)
        ow_v = ow_ref[...]
        dgate_ref[...] = (dy2 * (yh * ow_v) * (sig * (1.0 + gate * (1.0 - sig)))).astype(BF16)
        dn = dy2 * (gate * sig)
        dow_ref[...] += jnp.sum(dn * yh, axis=0, keepdims=True)
        dyh = dn * ow_v
        t = _sum_l2(dyh * yh, g_ref[...]) * (1.0 / HEAD_DIM)
        dy_ref[...] = rf * (dyh - yh * _sum_l3(t, gt_ref[...]))

    return pl.pallas_call(
        body, name=name, grid=(s // ts,),
        in_specs=[_row_spec(ts, d), _row_spec(ts, SB_WIDTH), _row_spec(ts, FOX_WIDTH), _row_spec(ts, MEM_WIDTH),
                  _row_spec(ts, MIX_WIDTH), _const_spec((1, MIX_WIDTH)),
                  _const_spec((MIX_WIDTH, d)),
                  _const_spec((MIX_WIDTH, LANES)), _const_spec((LANES, MIX_WIDTH))],
        out_specs=[_row_spec(ts, MIX_WIDTH), _row_spec(ts, MIX_WIDTH), _const_spec((1, MIX_WIDTH))],
        out_shape=[jax.ShapeDtypeStruct((s, MIX_WIDTH), F32), jax.ShapeDtypeStruct((s, PB), BF16),
                   jax.ShapeDtypeStruct((1, MIX_WIDTH), F32)],
        compiler_params=_params("arbitrary"),
    )(dxb, o_sb, o_fx, o_m, pb, ow, w_out, g, gt)


def _adamw(name, w, g, m, v, tr):
    def body(w_ref, g_ref, m_ref, v_ref, d_ref, m2_ref, v2_ref):
        gv = g_ref[...]
        m2 = ADAM_B1 * m_ref[...] + (1.0 - ADAM_B1) * gv
        v2 = ADAM_B2 * v_ref[...] + (1.0 - ADAM_B2) * (gv * gv)
        m_hat = m2 / (1.0 - ADAM_B1 ** ADAM_STEP)
        v_hat = v2 / (1.0 - ADAM_B2 ** ADAM_STEP)
        d_ref[...] = -ADAM_LR * (m_hat / (jnp.sqrt(v_hat) + ADAM_EPS) + ADAM_WD * w_ref[...])
        m2_ref[...] = m2
        v2_ref[...] = v2

    rest = w.shape[1:]
    spec = pl.BlockSpec((tr,) + rest, lambda i: (i,) + (0,) * len(rest))
    shp = jax.ShapeDtypeStruct(w.shape, F32)
    return pl.pallas_call(
        body, name=name, grid=(w.shape[0] // tr,), in_specs=[spec] * 4, out_specs=[spec] * 3, out_shape=[shp] * 3,
        compiler_params=_params("parallel"),
    )(w, g, m, v)


def _adamw_sharded(name, w, m, v, g_own, g_other, cvec, tr):
    depth, rows, cols = w.shape
    nt = rows // 2 // tr

    def body(c_ref, w_ref, m_ref, v_ref, *rest):
        g_refs, (g_ref, d_ref, m2_ref, v2_ref) = rest[:2 * depth], rest[2 * depth:]
        layer, mine = pl.program_id(0), pl.program_id(1) == c_ref[0]
        gv = None
        for lt in range(depth):
            cand = jnp.where(mine, g_refs[lt][...], g_refs[depth + lt][...])
            gv = cand if gv is None else jnp.where(layer == lt, cand, gv)
        m2 = ADAM_B1 * m_ref[...] + (1.0 - ADAM_B1) * gv
        v2 = ADAM_B2 * v_ref[...] + (1.0 - ADAM_B2) * (gv * gv)
        m_hat = m2 / (1.0 - ADAM_B1 ** ADAM_STEP)
        v_hat = v2 / (1.0 - ADAM_B2 ** ADAM_STEP)
        g_ref[...] = gv
        d_ref[...] = -ADAM_LR * (m_hat / (jnp.sqrt(v_hat) + ADAM_EPS) + ADAM_WD * w_ref[...])
        m2_ref[...] = m2
        v2_ref[...] = v2

    def g_map(lt, own):
        def index(l, hf, i, c_ref):
            use = jnp.logical_and(l == lt, (hf == c_ref[0]) == own)
            return jnp.where(use, i, 0), 0
        return index

    full = pl.BlockSpec((None, tr, cols), lambda l, hf, i, c_ref: (l, hf * nt + i, 0))
    g_specs = [pl.BlockSpec((tr, cols), g_map(lt, own)) for own in (True, False) for lt in range(depth)]
    shp = jax.ShapeDtypeStruct((depth, rows, cols), F32)
    return pl.pallas_call(
        body, name=name,
        grid_spec=pltpu.PrefetchScalarGridSpec(
            num_scalar_prefetch=1, grid=(depth, 2, nt), in_specs=[full] * 3 + g_specs, out_specs=[full] * 4),
        out_shape=[shp] * 4,
        compiler_params=_params("arbitrary", "arbitrary", "arbitrary"),
    )(cvec, w, m, v, *g_own, *g_other)


HBM_SPEC = pl.BlockSpec(memory_space=pltpu.HBM)


def _place():
    x, y, c = lax.axis_index("x"), lax.axis_index("y"), lax.axis_index("c")
    chips = [(1 - x, y), (x, 1 - y), (1 - x, 1 - y)]
    return x, y, c, chips


def _remote(src, dst, send_sems, recv_sems, k, to):
    return pltpu.make_async_remote_copy(src_ref=src, dst_ref=dst, send_sem=send_sems.at[k], recv_sem=recv_sems.at[k],
                                        device_id=to, device_id_type=MESH)


def _half_rows(n_rows, cc):
    rh = n_rows // 2
    return pl.ds(pl.multiple_of(cc * rh, 16), rh)


def _dma_sems(n):
    return [pltpu.SemaphoreType.DMA((n,)), pltpu.SemaphoreType.DMA((n,))]


class _Exchange:
    def __init__(self, inputs, out_shapes, n_sems, begin, relay, finish):
        self.inputs, self.out_shapes, self.n_sems = list(inputs), list(out_shapes), n_sems
        self.begin, self.relay, self.finish = begin, relay, finish

    @property
    def n(self):
        return len(self.inputs)

    def split(self, refs):
        return refs[:self.n], refs[self.n:2 * self.n], refs[2 * self.n], refs[2 * self.n + 1]


def _run_exchange(name, ex):
    def body(*refs):
        parts = ex.split(refs)
        for phase in (ex.begin, ex.relay, ex.finish):
            if phase is not None:
                phase(*parts)

    return pl.pallas_call(
        body, name=name, in_specs=[HBM_SPEC] * ex.n, out_specs=[HBM_SPEC] * ex.n, out_shape=ex.out_shapes,
        scratch_shapes=_dma_sems(ex.n_sems),
    )(*ex.inputs)


def _gather_exchange(shards):
    def ici(in_refs, out_refs, send_sems, recv_sems):
        x, y, c, chips = _place()
        return [_remote(in_ref.at[_half_rows(in_ref.shape[0], c)], out_ref.at[2 * x + y, _half_rows(in_ref.shape[0], c)],
                        send_sems, recv_sems, 6 * a + j, (cx, cy, c))
                for a, (in_ref, out_ref) in enumerate(zip(in_refs, out_refs)) for j, (cx, cy) in enumerate(chips)]

    def d2d(out_refs, send_sems, recv_sems, half_of):
        x, y, c, chips = _place()
        cps = []
        for a, out_ref in enumerate(out_refs):
            for j, (cx, cy) in enumerate(chips):
                piece = out_ref.at[2 * cx + cy, _half_rows(out_ref.shape[1], half_of(c))]
                cps.append(_remote(piece, piece, send_sems, recv_sems, 6 * a + 3 + j, (x, y, 1 - c)))
        return cps

    def begin(in_refs, out_refs, send_sems, recv_sems):
        for cp in ici(in_refs, out_refs, send_sems, recv_sems):
            cp.start()

    def relay(in_refs, out_refs, send_sems, recv_sems):
        x, y, c, chips = _place()
        for a, out_ref in enumerate(out_refs):
            for j, (cx, cy) in enumerate(chips):
                landed = out_ref.at[2 * cx + cy, _half_rows(out_ref.shape[1], c)]
                _remote(landed, landed, send_sems, recv_sems, 6 * a + j, (cx, cy, c)).wait_recv()
        for cp in d2d(out_refs, send_sems, recv_sems, lambda c_: c_):
            cp.start()

    def finish(in_refs, out_refs, send_sems, recv_sems):
        for cp in d2d(out_refs, send_sems, recv_sems, lambda c_: 1 - c_):
            cp.wait_recv()
        for cp in ici(in_refs, out_refs, send_sems, recv_sems) + d2d(out_refs, send_sems, recv_sems, lambda c_: c_):
            cp.wait_send()

    shapes = [jax.ShapeDtypeStruct((N_CHIPS,) + s_.shape, s_.dtype) for s_ in shards]
    return _Exchange(shards, shapes, 6 * len(shards), begin, relay, finish)


def _swap_exchange(g4s):
    def copies(in_refs, out_refs, send_sems, recv_sems):
        x, y, c, _ = _place()
        return [_remote(in_ref.at[:, _half_rows(in_ref.shape[1], 1 - c), :], out_ref, send_sems, recv_sems, a, (x, y, 1 - c))
                for a, (in_ref, out_ref) in enumerate(zip(in_refs, out_refs))]

    def begin(*parts):
        for cp in copies(*parts):
            cp.start()

    def finish(*parts):
        for cp in copies(*parts):
            cp.wait()

    shapes = [jax.ShapeDtypeStruct((g.shape[0], g.shape[1] // 2, g.shape[2]), g.dtype) for g in g4s]
    return _Exchange(g4s, shapes, len(g4s), begin, None, finish)


def _add_half(name, g4, r1, cvec, tr):
    n, r, w = g4.shape
    rh = r // 2
    nblk = rh // tr

    def body(c_ref, a_ref, b_ref, o_ref):
        o_ref[...] = (a_ref[...].astype(F32) + b_ref[...].astype(F32)).astype(BF16)

    return pl.pallas_call(
        body, name=name,
        grid_spec=pltpu.PrefetchScalarGridSpec(
            num_scalar_prefetch=1, grid=(n, nblk),
            in_specs=[pl.BlockSpec((None, tr, w), lambda k, i, c_ref: (k, c_ref[0] * nblk + i, 0)),
                      pl.BlockSpec((None, tr, w), lambda k, i, c_ref: (k, i, 0))],
            out_specs=pl.BlockSpec((None, tr, w), lambda k, i, c_ref: (k, i, 0))),
        out_shape=jax.ShapeDtypeStruct((n, rh, w), BF16),
        compiler_params=_params("parallel", "parallel"),
    )(cvec, g4, r1)


def _scatter_exchange(h4s):
    def sends(in_refs, out_refs, send_sems, recv_sems):
        x, y, c, chips = _place()
        return [_remote(in_ref.at[2 * cx + cy], out_ref.at[j], send_sems, recv_sems, 3 * a + j, (cx, cy, c))
                for a, (in_ref, out_ref) in enumerate(zip(in_refs, out_refs)) for j, (cx, cy) in enumerate(chips)]

    def begin(*parts):
        for cp in sends(*parts):
            cp.start()

    def finish(in_refs, out_refs, send_sems, recv_sems):
        x, y, c, chips = _place()
        for a, out_ref in enumerate(out_refs):
            for j, (cx, cy) in enumerate(chips):
                got = out_ref.at[j]
                _remote(got, got, send_sems, recv_sems, 3 * a + j, (cx, cy, c)).wait_recv()
        for cp in sends(in_refs, out_refs, send_sems, recv_sems):
            cp.wait_send()

    shapes = [jax.ShapeDtypeStruct((3,) + h.shape[1:], h.dtype) for h in h4s]
    return _Exchange(h4s, shapes, 3 * len(h4s), begin, None, finish)


def _sum_chips(name, h4, r3, mvec, tr):
    _, rh, w = h4.shape

    def body(m_ref, a_ref, b_ref, c_ref, d_ref, o_ref):
        o_ref[...] = ((a_ref[...].astype(F32) + b_ref[...].astype(F32)) + c_ref[...].astype(F32)) + d_ref[...].astype(F32)

    specs = [pl.BlockSpec((None, tr, w), lambda i, m_ref: (m_ref[0], i, 0))]
    specs += [pl.BlockSpec((None, tr, w), functools.partial(lambda k, i, m_ref: (k, i, 0), k)) for k in range(3)]
    return pl.pallas_call(
        body, name=name,
        grid_spec=pltpu.PrefetchScalarGridSpec(
            num_scalar_prefetch=1, grid=(rh // tr,), in_specs=specs,
            out_specs=pl.BlockSpec((tr, w), lambda i, m_ref: (i, 0))),
        out_shape=jax.ShapeDtypeStruct((rh, w), F32),
        compiler_params=_params("parallel"),
    )(mvec, h4, r3, r3, r3)


def _swap_reduced_exchange(ghs):
    def copies(in_refs, out_refs, send_sems, recv_sems):
        x, y, c, _ = _place()
        return [_remote(in_ref, out_ref, send_sems, recv_sems, a, (x, y, 1 - c))
                for a, (in_ref, out_ref) in enumerate(zip(in_refs, out_refs))]

    def begin(*parts):
        for cp in copies(*parts):
            cp.start()

    def finish(*parts):
        for cp in copies(*parts):
            cp.wait()

    return _Exchange(ghs, [jax.ShapeDtypeStruct(g.shape, g.dtype) for g in ghs], len(ghs), begin, None, finish)


class _SemaphoresFrom:
    def __init__(self, sems, first):
        self.sems, self.first = sems, first

    @property
    def at(self):
        return self

    def __getitem__(self, k):
        return self.sems.at[self.first + k]


def _both(a, b):
    def phase(fa, fb):
        if fa is None and fb is None:
            return None

        def run(in_refs, out_refs, send_sems, recv_sems):
            if fa is not None:
                fa(in_refs[:a.n], out_refs[:a.n], send_sems, recv_sems)
            if fb is not None:
                fb(in_refs[a.n:], out_refs[a.n:], _SemaphoresFrom(send_sems, a.n_sems), _SemaphoresFrom(recv_sems, a.n_sems))

        return run

    return _Exchange(a.inputs + b.inputs, a.out_shapes + b.out_shapes, a.n_sems + b.n_sems,
                     phase(a.begin, b.begin), phase(a.relay, b.relay), phase(a.finish, b.finish))


def _small_update(name, partials, weights, moments1, moments2):
    n = len(partials)
    width = max(p.shape[1] for p in partials)
    starts, at = [], 0
    for p in partials:
        starts.append(at)
        at += p.shape[0]
    rows = -(-at // 8) * 8
    has_w = [w is not None for w in weights]
    n_w = sum(has_w)

    def body(*refs):
        p_refs = refs[:n]
        w_refs, m_refs, v_refs = refs[n:n + n_w], refs[n + n_w:n + 2 * n_w], refs[n + 2 * n_w:n + 3 * n_w]
        outs = refs[n + 3 * n_w:-4]
        g_refs, upd_refs = outs[:n], outs[n:]
        vec, buf, send_sems, recv_sems = refs[-4:]
        x, y, c, _ = _place()
        me = 4 * x + 2 * y + c
        vec[...] = jnp.zeros_like(vec)
        for p_ref, r0 in zip(p_refs, starts):
            vec[r0:r0 + p_ref.shape[0], 0:p_ref.shape[1]] = p_ref[...]
        buf[me] = vec[...]
        flips = [(fx, fy, fc) for fx in (0, 1) for fy in (0, 1) for fc in (0, 1)][1:]
        peers = [(x + fx - 2 * x * fx, y + fy - 2 * y * fy, c + fc - 2 * c * fc) for fx, fy, fc in flips]
        sends = [_remote(vec, buf.at[me], send_sems, recv_sems, k, peer) for k, peer in enumerate(peers)]
        for cp in sends:
            cp.start()
        for k, (px, py, pc) in enumerate(peers):
            got = buf.at[4 * px + 2 * py + pc]
            _remote(got, got, send_sems, recv_sems, k, (px, py, pc)).wait_recv()
        for cp in sends:
            cp.wait_send()
        total = buf[0]
        for dev in range(1, N_DEV):
            total = total + buf[dev]
        k = 0
        for a in range(n):
            r, w = g_refs[a].shape
            g = total[starts[a]:starts[a] + r, 0:w]
            g_refs[a][...] = g
            if has_w[a]:
                m2 = ADAM_B1 * m_refs[k][...] + (1.0 - ADAM_B1) * g
                v2 = ADAM_B2 * v_refs[k][...] + (1.0 - ADAM_B2) * (g * g)
                m_hat = m2 / (1.0 - ADAM_B1 ** ADAM_STEP)
                v_hat = v2 / (1.0 - ADAM_B2 ** ADAM_STEP)
                upd_refs[3 * k][...] = -ADAM_LR * (m_hat / (jnp.sqrt(v_hat) + ADAM_EPS) + ADAM_WD * w_refs[k][...])
                upd_refs[3 * k + 1][...] = m2
                upd_refs[3 * k + 2][...] = v2
                k += 1

    ws = [w for w in weights if w is not None]
    g_shapes = [jax.ShapeDtypeStruct(p.shape if w is None else w.shape, F32) for p, w in zip(partials, weights)]
    u_shapes = [jax.ShapeDtypeStruct(w.shape, F32) for w in ws for _ in range(3)]
    vm = pl.BlockSpec(memory_space=pltpu.VMEM)
    n_args = n + 3 * n_w
    outs = pl.pallas_call(
        body, name=name, in_specs=[vm] * n_args, out_specs=[vm] * (n + 3 * n_w), out_shape=g_shapes + u_shapes,
        scratch_shapes=[pltpu.VMEM((rows, width), F32), pltpu.VMEM((N_DEV, rows, width), F32),
                        pltpu.SemaphoreType.DMA((7,)), pltpu.SemaphoreType.DMA((7,))],
    )(*partials, *ws, *[m for m in moments1 if m is not None], *[v for v in moments2 if v is not None])
    return outs[:n], outs[n:]


GATE_COL = 3 * SB_WIDTH + 3 * FOX_WIDTH + FOX_HEADS + MEM_WIDTH
FL_COL = QKV_WIDTH


GROUP_A_COLS = [(0, QKV_WIDTH), (FL_COL + FOX_HEADS, MEM_WIDTH)]
GROUP_B_COLS = [(GATE_COL, MIX_WIDTH), (FL_COL, FOX_HEADS)]


def _group_from_shards(shard_of, cw, spans, pad):
    parts = []
    for lo, width in spans:
        hi = lo + width
        for j in range(N_CHIPS):
            a, b = max(lo, j * cw), min(hi, (j + 1) * cw)
            if a < b:
                parts.append(shard_of(j)[:, a - j * cw:b - j * cw])
    if pad:
        parts.append(jnp.zeros((parts[0].shape[0], pad), parts[0].dtype))
    return jnp.concatenate(parts, axis=1)


def _shard_from_groups(ga, gb, j, cw):
    lo, hi = j * cw, (j + 1) * cw
    placed = []
    for grp, spans in ((ga, GROUP_A_COLS), (gb, GROUP_B_COLS)):
        at = 0
        for first, width in spans:
            a, b = max(lo, first), min(hi, first + width)
            if a < b:
                placed.append((a, grp[:, at + a - first:at + b - first]))
            at += width
    return jnp.concatenate([p for _, p in sorted(placed, key=lambda t: t[0])], axis=1)


def _tile_of(n, cap, unit):
    if n <= cap:
        return n
    best = None
    for t in range(unit, cap + 1, unit):
        if n % t == 0:
            best = t
    assert best is not None, (n, cap, unit)
    return best


def _column_major_rows(a):
    dp, r, c = a.shape
    return a.transpose(2, 0, 1).reshape(c, dp, r // LANES, LANES).transpose(0, 2, 1, 3).reshape(-1, 8, LANES)


def _from_column_major_rows(b, shape):
    dp, r, c = shape
    return b.reshape(c, r // LANES, dp, LANES).transpose(0, 2, 1, 3).reshape(c, dp, r).transpose(1, 2, 0)


def _pack_small(parts):
    rows = []
    for p in parts:
        f = p.reshape(-1).astype(F32)
        f = jnp.pad(f, (0, (-f.shape[0]) % LANES))
        rows.append(f.reshape(-1, LANES))
    out = jnp.concatenate(rows, axis=0)
    return jnp.pad(out, ((0, (-out.shape[0]) % 8), (0, 0)))


def _unpack_small(packed, shapes):
    outs, r = [], 0
    for shp in shapes:
        n = 1
        for s_ in shp:
            n *= s_
        nr = -(-n // LANES)
        outs.append(packed[r:r + nr].reshape(-1)[:n].reshape(shp))
        r += nr
    return outs


def kernel(x, mem, norm_w, w_in, b_forget, mem_norm_w, w_mem_kv, out_norm_w, w_out, final_norm_w, loss_target, m_norm_w, m_w_in, m_b_forget, m_mem_norm_w, m_w_mem_kv, m_out_norm_w, m_w_out, m_final_norm_w, v_norm_w, v_w_in, v_b_forget, v_mem_norm_w, v_w_mem_kv, v_out_norm_w, v_w_out, v_final_norm_w):
    xs = x[0]
    mems = mem[0]
    target = loss_target[0]
    s, d = xs.shape
    depth = norm_w.shape[0]
    nb = s // TILE
    ts = _tile_of(s, 256, 8)
    big = (w_in, w_mem_kv, w_out)
    core = lax.axis_index("c")
    chip = 2 * lax.axis_index("x") + lax.axis_index("y")
    cvec = core.astype(jnp.int32).reshape(1)
    mvec = chip.astype(jnp.int32).reshape(1)
    cw = w_in.shape[2]

    own_w = [[a[l].astype(BF16) for a in big] for l in range(depth)]

    def lay_out_in(own, got):
        shard_of = lambda j: jnp.where(chip == j, own, got[j])
        return (_group_from_shards(shard_of, cw, GROUP_A_COLS, 0),
                _group_from_shards(shard_of, cw, GROUP_B_COLS, LANES - FOX_HEADS))

    def lay_out_rows(own, got):
        full = jnp.where(lax.broadcasted_iota(jnp.int32, got.shape, 0) == chip, own[None], got)
        return full.reshape(-1, full.shape[2])

    w_in_groups = [lay_out_in(own_w[0][0], _run_exchange("gather_weights0", _gather_exchange(own_w[0][:1]))[0])]
    layer_w = []

    tm = _tile_of(s, 256, 8)
    fl_block = MIX_WIDTH // LANES

    saved = []
    cur = xs
    for l in range(depth):
        wa, wb = w_in_groups[l]
        h = _rms_fwd(f"rms_fwd{l}", cur, norm_w[l][None], ts)
        pa = _mm(f"inproj_a{l}", h, wa, "nn", tm, _tile_of(PA, 1664, LANES), BF16)
        pb = _mm(f"inproj_b{l}", h, wb, "nn", tm, PB, F32)
        bpad = jnp.pad(b_forget[l], (0, LANES - FOX_HEADS))[None]
        ccol4, crow4 = _gate_fwd(f"gate_fwd{l}", pb, bpad, fl_block)
        more = l + 1 < depth
        o_sb, got = _sb_fwd(f"sb_fwd{l}", pa, 0, carried=_gather_exchange(own_w[l][1:]))
        wkv, wout = lay_out_rows(own_w[l][1], got[0]), lay_out_rows(own_w[l][2], got[1])
        layer_w.append((wa, wb, wkv, wout))
        o_fx, lse_fx, got = _fox_fwd(f"fox_fwd{l}", pa, 3 * SB_WIDTH, ccol4, crow4,
                                     carried=_gather_exchange(own_w[l + 1][:1]) if more else None)
        if more:
            w_in_groups.append(lay_out_in(own_w[l + 1][0], got[0]))
        mn = _rms_fwd(f"mem_rms{l}", mems, mem_norm_w[l][None], mems.shape[0])
        mkv = _mm(f"mem_kv{l}", mn, wkv, "nn", mems.shape[0], 2 * MEM_WIDTH, BF16)
        o_m, lse_m = _mem_fwd(f"mem_fwd{l}", pa, mkv)
        nxt, y2 = _out_fwd(f"out_fwd{l}", o_sb, o_fx, o_m, pb, out_norm_w[l][None], cur, wout, ts)
        saved.append((cur, h, pa, pb, bpad, ccol4, crow4, o_sb, o_fx, lse_fx, mn, mkv, o_m, lse_m, y2))
        cur = nxt

    loss_v, dx, dxb, g_final = _final_loss("final_loss", cur, final_norm_w[None], target, ts)

    g_norm, g_b, g_memnorm, g_outnorm = [None] * depth, [None] * depth, [None] * depth, [None] * depth
    g_wa, g_wb, g_wkv, g_wout = [None] * depth, [None] * depth, [None] * depth, [None] * depth
    g_own = [[None] * depth for _ in big]
    g_other = [[None] * depth for _ in big]

    def swap_of(jobs):
        return _swap_exchange([g for _, _, g, _ in jobs])

    def chip_sums(jobs, got):
        return [(lr, k, _add_half(f"grad_add_half{lr}_{k}", g, r_, cvec, t_), t_) for (lr, k, g, t_), r_ in zip(jobs, got)]

    def sum_at_owner(jobs, from_chips):
        return [_sum_chips(f"grad_sum_chips{lr}_{k}", h_, r_, mvec, t_) for (lr, k, h_, t_), r_ in zip(jobs, from_chips)]

    def keep(jobs, halves, others):
        for (lr, k, _, _), mine, other in zip(jobs, halves, others):
            g_own[k][lr], g_other[k][lr] = mine, other

    def job(lr, k, g4):
        return lr, k, g4, _tile_of(g4.shape[1] // 2, 256, 16)

    pending = []
    for l in reversed(range(depth)):
        xin, h, pa, pb, bpad, ccol4, crow4, o_sb, o_fx, lse_fx, mn, mkv, o_m, lse_m, y2 = saved[l]
        wa, wb, wkv, wout = layer_w[l]
        dy, dgate, g_outnorm[l] = _out_bwd(f"out_bwd{l}", dxb, o_sb, o_fx, o_m, pb, out_norm_w[l][None], wout, ts)
        g_wout[l] = _mm(f"dw_out{l}", y2, dxb, "tn", _tile_of(MIX_WIDTH, 640, LANES), d, F32)
        dq_m, dk_m, dv_m = _mem_bwd(f"mem_bwd{l}", pa, mkv, o_m, lse_m, dy, SB_WIDTH + FOX_WIDTH)
        dmkv = jnp.concatenate([dk_m, dv_m], axis=1)
        g_wkv[l] = _mm(f"dw_kv{l}", mn, dmkv, "tn", d, 2 * MEM_WIDTH, F32)
        dmn = _mm(f"dmem{l}", dmkv, wkv, "nt", mems.shape[0], d, F32)
        g_memnorm[l] = _rms_wgrad(f"mem_norm_grad{l}", mems, dmn)
        small = [job(l, 1, g_wkv[l].reshape(N_CHIPS, -1, g_wkv[l].shape[1])), job(l, 2, g_wout[l].reshape(N_CHIPS, -1, d))]
        (dq_fx, dk_fx, dv_fx, cs4), got = _fox_bwd(f"fox_bwd{l}", pa, 3 * SB_WIDTH, ccol4, crow4, o_fx, lse_fx, dy,
                                                    SB_WIDTH, carried=swap_of(small))
        pending += chip_sums(small, got)
        (dq_sb, dk_sb, dv_sb), from_chips = _sb_bwd(f"sb_bwd{l}", pa, 0, dy, 0,
                                                   carried=_scatter_exchange([j[2] for j in pending]))
        reduced_jobs, reduced = pending, sum_at_owner(pending, from_chips)
        swap_back = _swap_reduced_exchange(reduced)
        dpb, g_b[l] = _gate_bwd(f"gate_bwd{l}", pb, bpad, cs4, fl_block, dgate)
        dpa = jnp.concatenate([dq_sb, dk_sb, dv_sb, dq_fx, dk_fx, dv_fx, dq_m], axis=1)
        tw = _tile_of(d, 512, LANES)
        g_wa[l] = _mm(f"dw_in_a{l}", h, dpa, "tn", tw, _tile_of(PA, 1664, LANES), BF16)
        g_wb[l] = _mm(f"dw_in_b{l}", h, dpb, "tn", tw, PB, BF16)
        g4_in = jnp.stack([_shard_from_groups(g_wa[l], g_wb[l], j, cw) for j in range(N_CHIPS)])
        w_in_job = [job(l, 0, g4_in)]
        if l > 0:
            dx, dxb, g_norm[l], got = _inproj_bwd(f"inproj_bwd{l}", dpa, dpb, wa, wb, xin, norm_w[l][None], dx, ts,
                                                  carried=_both(swap_of(w_in_job), swap_back))
            pending = chip_sums(w_in_job, got[:1])
            keep(reduced_jobs, reduced, got[1:])
        else:
            pending = chip_sums(w_in_job, _run_exchange("grad_swap_halves_last", swap_of(w_in_job)))
            dx, dxb, g_norm[l], got = _inproj_bwd(f"inproj_bwd{l}", dpa, dpb, wa, wb, xin, norm_w[l][None], dx, ts,
                                                  carried=_both(_scatter_exchange([j[2] for j in pending]), swap_back))
            keep(reduced_jobs, reduced, got[1:])
            last = sum_at_owner(pending, got[:1])
            keep(pending, last, _run_exchange("grad_swap_reduced_last", _swap_reduced_exchange(last)))

    small_w = [norm_w, b_forget, mem_norm_w, out_norm_w, final_norm_w]
    small_m = [m_norm_w, m_b_forget, m_mem_norm_w, m_out_norm_w, m_final_norm_w]
    small_v = [v_norm_w, v_b_forget, v_mem_norm_w, v_out_norm_w, v_final_norm_w]
    rows2 = lambda a: a.reshape(-1, a.shape[-1])
    partials = [jnp.concatenate(g_norm, axis=0), jnp.concatenate(g_b, axis=0), jnp.concatenate(g_memnorm, axis=0),
                jnp.concatenate(g_outnorm, axis=0), g_final, loss_v]
    sums, updates = _small_update("small_update", partials, [rows2(a) for a in small_w] + [None],
                                  [rows2(a) for a in small_m] + [None], [rows2(a) for a in small_v] + [None])
    small_grads = [g.reshape(a.shape) for g, a in zip(sums, small_w)]
    loss = sums[-1][0, 0]
    small_delta, small_m2, small_v2 = ([updates[3 * k + t].reshape(a.shape) for k, a in enumerate(small_w)]
                                       for t in range(3))
    big_grads, big_delta, big_m2, big_v2 = [], [], [], []
    for k, (nm, w_, m_, v_) in enumerate(zip(("w_in", "w_mem_kv", "w_out"), big, (m_w_in, m_w_mem_kv, m_w_out),
                                             (v_w_in, v_w_mem_kv, v_w_out))):
        if w_.shape[2] % LANES:
            g_full = jnp.stack([jnp.concatenate([jnp.where(core == 0, go, gt), jnp.where(core == 0, gt, go)], axis=0)
                                for go, gt in zip(g_own[k], g_other[k])])
            w_p, g_p, m_p, v_p = (_column_major_rows(a) for a in (w_, g_full, m_, v_))
            outs = _adamw(f"adamw_{nm}", w_p, g_p, m_p, v_p, _tile_of(w_p.shape[0], 600, 1))
            outs = [_from_column_major_rows(o, w_.shape) for o in (g_p, *outs)]
        else:
            outs = _adamw_sharded(f"adamw_{nm}", w_, m_, v_, g_own[k], g_other[k], cvec,
                                  _tile_of(w_.shape[1] // 2, 256, 8))
        for lst, o in zip((big_grads, big_delta, big_m2, big_v2), outs):
            lst.append(o)

    def order(sm, bg):
        return [sm[0], bg[0], sm[1], sm[2], bg[1], sm[3], bg[2], sm[4]]

    return (loss, dx[None], *order(small_grads, big_grads), *order(small_delta, big_delta),
            *order(small_m2, big_m2), *order(small_v2, big_v2))
```

```python
import functools

import jax
import jax.numpy as jnp
from jax import lax
from jax.experimental import pallas as pl
from jax.experimental.pallas import tpu as pltpu

F32 = jnp.float32
BF16 = jnp.bfloat16

HEAD_DIM = 64
SB_WIDTH = 512
FOX_WIDTH = 512
FOX_HEADS = 8
MEM_WIDTH = 256
MEM_HEADS = MEM_WIDTH // HEAD_DIM
MIX_WIDTH = SB_WIDTH + FOX_WIDTH + MEM_WIDTH
TOTAL_HEADS = MIX_WIDTH // HEAD_DIM
IN_WIDTH = 3 * SB_WIDTH + 3 * FOX_WIDTH + FOX_HEADS + MEM_WIDTH + MIX_WIDTH
LANES = 128
QKV_WIDTH = 3 * SB_WIDTH + 3 * FOX_WIDTH
PA = QKV_WIDTH + MEM_WIDTH
PB = LANES + MIX_WIDTH
EPS = 1e-6
SCALE = HEAD_DIM ** -0.5
TILE = 256
SB_GROUP = 4
SB_LANES = SB_GROUP * HEAD_DIM
FOX_GROUP = 4
FOX_LANES = FOX_GROUP * HEAD_DIM
NEG_INF = float("-inf")
MASKED = -1e30

ADAM_LR = 0.001
ADAM_B1 = 0.9
ADAM_B2 = 0.999
ADAM_EPS = 1e-08
ADAM_WD = 0.01
ADAM_STEP = 10

N_CHIPS = 4
N_DEV = 8
VMEM_LIMIT = 48 * 1024 * 1024
MESH = pl.DeviceIdType.MESH


def _params(*sem):
    return pltpu.CompilerParams(dimension_semantics=tuple(sem), vmem_limit_bytes=VMEM_LIMIT)


def _dot(a, b):
    return jnp.dot(a, b, preferred_element_type=F32)


def _dot_nt(a, b):
    return lax.dot_general(a, b, (((1,), (1,)), ((), ())), preferred_element_type=F32)


def _dot_tn(a, b):
    return lax.dot_general(a, b, (((0,), (0,)), ((), ())), preferred_element_type=F32)


def _split2(x):
    hi = x.astype(BF16)
    lo = (x - hi.astype(F32)).astype(BF16)
    return hi, lo


def _split3(x):
    hi = x.astype(BF16)
    r = x - hi.astype(F32)
    mid = r.astype(BF16)
    lo = (r - mid.astype(F32)).astype(BF16)
    return hi, mid, lo


def _sum_l2(x, u):
    hi, lo = _split2(x)
    return _dot(hi, u) + _dot(lo, u)


def _sum_l3(x, u):
    hi, mid, lo = _split3(x)
    return _dot(hi, u) + _dot(mid, u) + _dot(lo, u)


def _sum_r3(u, x):
    hi, mid, lo = _split3(x)
    return _dot(u, hi) + _dot(u, mid) + _dot(u, lo)


def _softplus(z):
    return jnp.maximum(z, 0.0) + jnp.log1p(jnp.exp(-jnp.abs(z)))


def _tri(n, pred):
    r = lax.broadcasted_iota(jnp.int32, (n, n), 0)
    c = lax.broadcasted_iota(jnp.int32, (n, n), 1)
    return jnp.where(pred(r, c), 1.0, 0.0).astype(BF16)


def _rows(ref, j, n=TILE):
    return pl.ds(pl.multiple_of(j * n, n), n)


def _mm(name, a, b, mode, tm, tn, out_dtype, res=None, a_lead=(), b_lead=()):
    a2, b2 = a.shape[len(a_lead):], b.shape[len(b_lead):]
    if mode == "tn":
        k, m = a2
    else:
        m, k = a2
    n = b2[0] if mode == "nt" else b2[1]
    assert m % tm == 0 and n % tn == 0, (name, m, tm, n, tn)
    na, nb = (None,) * len(a_lead), (None,) * len(b_lead)
    if mode == "tn":
        a_spec = pl.BlockSpec(na + (k, tm), lambda j, i: a_lead + (0, i))
    else:
        a_spec = pl.BlockSpec(na + (tm, k), lambda j, i: a_lead + (i, 0))
    if mode == "nt":
        b_spec = pl.BlockSpec(nb + (tn, k), lambda j, i: b_lead + (j, 0))
    else:
        b_spec = pl.BlockSpec(nb + (k, tn), lambda j, i: b_lead + (0, j))
    o_spec = pl.BlockSpec((tm, tn), lambda j, i: (i, j))
    dot = {"nn": _dot, "nt": _dot_nt, "tn": _dot_tn}[mode]

    def body(a_ref, b_ref, *rest):
        o_ref = rest[-1]
        acc = dot(a_ref[...].astype(BF16), b_ref[...].astype(BF16))
        if res is not None:
            acc = acc + rest[0][...]
        o_ref[...] = acc.astype(o_ref.dtype)

    args, specs = [a, b], [a_spec, b_spec]
    if res is not None:
        args.append(res)
        specs.append(o_spec)
    return pl.pallas_call(
        body, name=name, grid=(n // tn, m // tm), in_specs=specs, out_specs=o_spec,
        out_shape=jax.ShapeDtypeStruct((m, n), out_dtype),
        compiler_params=_params("parallel", "parallel"),
    )(*args)


def _rms_fwd(name, x, g, ts):
    s, d = x.shape

    def body(x_ref, g_ref, o_ref):
        xf = x_ref[...]
        r = lax.rsqrt(jnp.mean(xf * xf, axis=1, keepdims=True) + EPS)
        o_ref[...] = (xf * r * g_ref[...]).astype(BF16)

    return pl.pallas_call(
        body, name=name, grid=(s // ts,),
        in_specs=[pl.BlockSpec((ts, d), lambda i: (i, 0)), pl.BlockSpec((1, d), lambda i: (0, 0))],
        out_specs=pl.BlockSpec((ts, d), lambda i: (i, 0)),
        out_shape=jax.ShapeDtypeStruct((s, d), BF16),
        compiler_params=_params("parallel"),
    )(x, g)


def _inproj_bwd(name, dpa, dpb, wa, wb, x, g, dres, ts, carried=None):
    s, d = x.shape

    def body(dpa_ref, dpb_ref, wa_ref, wb_ref, x_ref, g_ref, dres_ref, dx_ref, dxb_ref, dg_ref):
        @pl.when(pl.program_id(1) == 0)
        def _():
            dg_ref[...] = jnp.zeros_like(dg_ref)

        dhf = _dot_nt(dpa_ref[...], wa_ref[...]) + _dot_nt(dpb_ref[...], wb_ref[...])
        xf = x_ref[...]
        r = lax.rsqrt(jnp.mean(xf * xf, axis=1, keepdims=True) + EPS)
        xh = xf * r
        dg_ref[...] += jnp.sum(dhf * xh, axis=0, keepdims=True)
        dxh = dhf * g_ref[...]
        m = jnp.mean(dxh * xh, axis=1, keepdims=True)
        dx = r * (dxh - xh * m) + dres_ref[...]
        dx_ref[...] = dx
        dxb_ref[...] = dx.astype(BF16)

    row = lambda w: pl.BlockSpec((ts, w), lambda p, i: (i, 0))
    whole = lambda a: pl.BlockSpec(a.shape, lambda p, i: (0, 0))
    outs = _pair_grid_call(
        name, body, s // ts,
        in_specs=[row(dpa.shape[1]), row(dpb.shape[1]), whole(wa), whole(wb), row(d), whole(g), row(d)],
        out_specs=[row(d), row(d), pl.BlockSpec((1, d), lambda p, i: (0, 0))],
        out_shape=[jax.ShapeDtypeStruct((s, d), F32), jax.ShapeDtypeStruct((s, d), BF16),
                   jax.ShapeDtypeStruct((1, d), F32)],
        scratch=[], args=(dpa, dpb, wa, wb, x, g, dres), carried=carried, groups=1)
    return outs[0], outs[1], outs[2], outs[3:]


def _rms_wgrad(name, x, dh):
    m_, d = x.shape

    def body(x_ref, dh_ref, dg_ref):
        xf = x_ref[...]
        r = lax.rsqrt(jnp.mean(xf * xf, axis=1, keepdims=True) + EPS)
        dg_ref[...] = jnp.sum(dh_ref[...] * xf * r, axis=0, keepdims=True)

    return pl.pallas_call(
        body, name=name, out_shape=jax.ShapeDtypeStruct((1, d), F32),
    )(x, dh)


def _final_loss(name, x, g, target, ts):
    s, d = x.shape

    def body(x_ref, g_ref, t_ref, loss_ref, dx_ref, dxb_ref, dg_ref):
        @pl.when(pl.program_id(0) == 0)
        def _():
            dg_ref[...] = jnp.zeros_like(dg_ref)
            loss_ref[...] = jnp.zeros_like(loss_ref)

        xf = x_ref[...]
        gw = g_ref[...]
        r = lax.rsqrt(jnp.mean(xf * xf, axis=1, keepdims=True) + EPS)
        xh = xf * r
        e = xh * gw - t_ref[...]
        part = 0.5 * jnp.sum(jnp.mean(e * e, axis=1, keepdims=True), axis=0, keepdims=True)
        loss_ref[...] += jnp.broadcast_to(part, loss_ref.shape)
        dy = e * (1.0 / d)
        dg_ref[...] += jnp.sum(dy * xh, axis=0, keepdims=True)
        dxh = dy * gw
        m = jnp.mean(dxh * xh, axis=1, keepdims=True)
        dx = r * (dxh - xh * m)
        dx_ref[...] = dx
        dxb_ref[...] = dx.astype(BF16)

    row = pl.BlockSpec((ts, d), lambda i: (i, 0))
    vec = pl.BlockSpec((1, d), lambda i: (0, 0))
    lvec = pl.BlockSpec((1, LANES), lambda i: (0, 0))
    return pl.pallas_call(
        body, name=name, grid=(s // ts,), in_specs=[row, vec, row], out_specs=[lvec, row, row, vec],
        out_shape=[jax.ShapeDtypeStruct((1, LANES), F32), jax.ShapeDtypeStruct((s, d), F32),
                   jax.ShapeDtypeStruct((s, d), BF16), jax.ShapeDtypeStruct((1, d), F32)],
        compiler_params=_params("arbitrary"),
    )(x, g, target)


def _gate_fwd(name, pb, bpad, fl_block):
    s = pb.shape[0]
    nb = s // TILE
    fg = FOX_HEADS // FOX_GROUP

    def body(fl_ref, b_ref, ccol_ref, crow_ref, carry):
        @pl.when(pl.program_id(0) == 0)
        def _():
            carry[...] = jnp.zeros_like(carry)

        u = fl_ref[...] + b_ref[...]
        lf = jnp.minimum(u, 0.0) - jnp.log1p(jnp.exp(-jnp.abs(u)))
        lower = _tri(TILE, lambda r, c: c <= r)
        c = _sum_r3(lower, lf) + carry[0:1, :]
        for grp in range(fg):
            ccol_ref[grp] = jnp.concatenate(
                [jnp.broadcast_to(c[:, grp * FOX_GROUP + hh:grp * FOX_GROUP + hh + 1], (TILE, HEAD_DIM))
                 for hh in range(FOX_GROUP)], axis=1)
        crow_ref[0] = c.T[0:8, :]
        carry[...] = jnp.broadcast_to(c[TILE - 1:TILE, :], carry.shape)

    return pl.pallas_call(
        body, name=name, grid=(nb,),
        in_specs=[pl.BlockSpec((TILE, LANES), lambda i: (i, fl_block)), pl.BlockSpec((1, LANES), lambda i: (0, 0))],
        out_specs=[pl.BlockSpec((fg, TILE, FOX_LANES), lambda i: (0, i, 0)), pl.BlockSpec((1, 8, TILE), lambda i: (i, 0, 0))],
        out_shape=[jax.ShapeDtypeStruct((fg, s, FOX_LANES), F32), jax.ShapeDtypeStruct((nb, 8, TILE), F32)],
        scratch_shapes=[pltpu.VMEM((8, LANES), F32)],
        compiler_params=_params("arbitrary"),
    )(pb, bpad)


def _gate_bwd(name, pb, bpad, colsum, fl_block, dpb):
    s = pb.shape[0]
    nb = s // TILE

    def body(fl_ref, b_ref, cs_ref, dpb_ref, dl_ref, db_ref, carry):
        @pl.when(pl.program_id(0) == 0)
        def _():
            carry[...] = jnp.zeros_like(carry)
            db_ref[...] = jnp.zeros_like(db_ref)

        upper = _tri(TILE, lambda r, c: r >= c)
        rsum = _sum_l3(cs_ref[0], upper) + carry[:, 0:1]
        carry[...] = jnp.broadcast_to(rsum[:, 0:1], carry.shape)
        full = jnp.concatenate([rsum, jnp.zeros((LANES - 8, TILE), F32)], axis=0)
        dlf = -full.T
        u = fl_ref[...] + b_ref[...]
        dlogit = dlf * (1.0 - jax.nn.sigmoid(u))
        dl_ref[...] = dlogit.astype(BF16)
        db_ref[...] += jnp.sum(dlogit, axis=0, keepdims=True)

    logits_block = pl.BlockSpec((TILE, LANES), lambda i: (nb - 1 - i, fl_block))
    return pl.pallas_call(
        body, name=name, grid=(nb,),
        in_specs=[logits_block, pl.BlockSpec((1, LANES), lambda i: (0, 0)),
                  pl.BlockSpec((1, 8, TILE), lambda i: (nb - 1 - i, 0, 0)), pl.BlockSpec(memory_space=pl.ANY)],
        out_specs=[logits_block, pl.BlockSpec((1, LANES), lambda i: (0, 0))],
        out_shape=[jax.ShapeDtypeStruct(dpb.shape, BF16), jax.ShapeDtypeStruct((1, LANES), F32)],
        scratch_shapes=[pltpu.VMEM((8, LANES), F32)], input_output_aliases={3: 0},
        compiler_params=_params("arbitrary"),
    )(pb, bpad, colsum, dpb)


def _head_slices(hh):
    return slice(HEAD_DIM * hh, HEAD_DIM * (hh + 1))


def _scaled_q(q_ref, sl, scale=SCALE):
    return (q_ref[:, sl].astype(F32) * scale).astype(BF16)


def _neg_abs(x):
    sign = jnp.uint32(0x80000000)
    return lax.bitcast_convert_type(lax.bitcast_convert_type(x, jnp.uint32) | sign, F32)


def _sb_tile(qn, kj, carry, strict, u_after, diag):
    nz = _dot_nt(qn, kj)
    lf = jnp.minimum(nz, 0.0) - jnp.log(1.0 + jnp.exp(_neg_abs(nz)))
    lsig = lf - nz
    if diag:
        lf = jnp.where(strict, lf, 0.0)
    sx = _dot(lf.astype(BF16), u_after)
    a = jnp.exp(lsig + sx + carry)
    if diag:
        a = jnp.where(strict, a, 0.0)
    return lsig, a, carry + sx[:, 0:1] + lf[:, 0:1]


def _pair_grid_call(name, body, nb, in_specs, out_specs, out_shape, scratch, args, carried=None, groups=4):
    if carried is None:
        return pl.pallas_call(
            body, name=name, grid=(groups, nb), in_specs=in_specs, out_specs=out_specs, out_shape=out_shape,
            scratch_shapes=scratch, compiler_params=_params("arbitrary", "arbitrary"),
        )(*args)
    n_in, n_out, n_ex = len(in_specs), len(out_specs), carried.n

    def body_with_copies(*refs):
        own_in, ex_in = refs[:n_in], refs[n_in:n_in + n_ex]
        own_out = refs[n_in + n_ex:n_in + n_ex + n_out]
        ex_out = refs[n_in + n_ex + n_out:n_in + 2 * n_ex + n_out]
        own_scratch, sems = refs[n_in + 2 * n_ex + n_out:-2], refs[-2:]
        parts = (ex_in, ex_out, sems[0], sems[1])
        p, i = pl.program_id(0), pl.program_id(1)
        pl.when(jnp.logical_and(p == 0, i == 0))(lambda: carried.begin(*parts))
        if carried.relay is not None:
            pl.when(jnp.logical_and(p == groups - 1, i == max(nb - 2, 0)))(lambda: carried.relay(*parts))
        body(*own_in, *own_out, *own_scratch)
        pl.when(jnp.logical_and(p == groups - 1, i == nb - 1))(lambda: carried.finish(*parts))

    return pl.pallas_call(
        body_with_copies, name=name, grid=(groups, nb), in_specs=list(in_specs) + [HBM_SPEC] * n_ex,
        out_specs=list(out_specs) + [HBM_SPEC] * n_ex, out_shape=list(out_shape) + carried.out_shapes,
        scratch_shapes=list(scratch) + _dma_sems(carried.n_sems),
        compiler_params=_params("arbitrary", "arbitrary"),
    )(*args, *carried.inputs)


def _sb_fwd(name, pa, col0, carried=None):
    s = pa.shape[0]
    nb = s // TILE
    cb = col0 // SB_LANES
    kb = SB_WIDTH // SB_LANES

    def body(q_ref, k_ref, v_ref, o_ref, lsig_s, lf_s):
        i = pl.program_id(1)
        r = lax.broadcasted_iota(jnp.int32, (TILE, TILE), 0)
        c = lax.broadcasted_iota(jnp.int32, (TILE, TILE), 1)
        strict = c < r
        u_after = _tri(TILE, lambda rr, cc: rr > cc)
        qs = [_scaled_q(q_ref, _head_slices(hh), -SCALE) for hh in range(SB_GROUP)]

        def neg_z(j):
            kblk = k_ref[_rows(k_ref, j), :]
            return [_dot_nt(qs[hh], kblk[:, _head_slices(hh)]) for hh in range(SB_GROUP)]

        def scores(nzs, slot, diag):
            for hh, nz in enumerate(nzs):
                lf = jnp.minimum(nz, 0.0) - jnp.log(1.0 + jnp.exp(_neg_abs(nz)))
                lsig = lf - nz
                if diag:
                    lf = jnp.where(strict, lf, 0.0)
                    lsig = jnp.where(strict, lsig, MASKED)
                lsig_s[slot, hh] = lsig
                lf_s[slot, hh] = lf.astype(BF16)

        def weigh(j, slot, state):
            vblk = v_ref[_rows(v_ref, j), :]
            new = []
            for hh in range(SB_GROUP):
                carry, acc = state[hh]
                lfb = lf_s[slot, hh]
                sx = _dot(lfb, u_after)
                a = jnp.exp(lsig_s[slot, hh] + sx + carry)
                new.append((carry + sx[:, 0:1] + lfb[:, 0:1].astype(F32),
                            acc + _dot(a.astype(BF16), vblk[:, _head_slices(hh)])))
            return tuple(new)

        def step(t, state):
            state = weigh(i - t + 1, (t - 1) % 2, state)
            scores(neg_z(i - t), t % 2, False)
            return state

        zero = (jnp.zeros((TILE, 1), F32), jnp.zeros((TILE, HEAD_DIM), F32))
        scores(neg_z(i), 0, True)
        state = lax.fori_loop(1, i + 1, step, (zero,) * SB_GROUP)
        state = weigh(0, i % 2, state)
        o_ref[...] = jnp.concatenate([st[1] for st in state], axis=1)

    outs = _pair_grid_call(
        name, body, nb,
        in_specs=[pl.BlockSpec((TILE, SB_LANES), lambda p, i: (i, cb + p)),
                  pl.BlockSpec((s, SB_LANES), lambda p, i: (0, cb + kb + p)),
                  pl.BlockSpec((s, SB_LANES), lambda p, i: (0, cb + 2 * kb + p))],
        out_specs=[pl.BlockSpec((TILE, SB_LANES), lambda p, i: (i, p))],
        out_shape=[jax.ShapeDtypeStruct((s, SB_WIDTH), F32)],
        scratch=[pltpu.VMEM((2, SB_GROUP, TILE, TILE), F32), pltpu.VMEM((2, SB_GROUP, TILE, TILE), BF16)],
        args=(pa, pa, pa), carried=carried, groups=kb)
    return outs[0], outs[1:]


def _sb_bwd(name, pa, col0, dout, dcol0, carried=None):
    s = pa.shape[0]
    nb = s // TILE
    cb = col0 // SB_LANES
    kb = SB_WIDTH // SB_LANES
    db = dcol0 // SB_LANES

    def body(q_ref, k_ref, v_ref, do_ref, dq_ref, dk_ref, dv_ref, dk_acc, dv_acc, dpan, span, gsum, lsig_s, lf_s):
        i = pl.program_id(1)

        @pl.when(i == 0)
        def _():
            dk_acc[...] = jnp.zeros_like(dk_acc)
            dv_acc[...] = jnp.zeros_like(dv_acc)

        r = lax.broadcasted_iota(jnp.int32, (TILE, TILE), 0)
        c = lax.broadcasted_iota(jnp.int32, (TILE, TILE), 1)
        strict = c < r
        u_after = _tri(TILE, lambda rr, cc: rr > cc)
        u_before = _tri(TILE, lambda rr, cc: rr < cc)
        qs = [_scaled_q(q_ref, _head_slices(hh), -SCALE) for hh in range(SB_GROUP)]
        dos = [do_ref[:, _head_slices(hh)].astype(BF16) for hh in range(SB_GROUP)]
        dots = [do_ref[:, _head_slices(hh)].T.astype(BF16) for hh in range(SB_GROUP)]
        qts = [q.astype(F32).T.astype(BF16) for q in qs]

        def scores(j, slot, diag):
            kblk = k_ref[_rows(k_ref, j), :]
            for hh in range(SB_GROUP):
                nz = _dot_nt(qs[hh], kblk[:, _head_slices(hh)])
                lf = jnp.minimum(nz, 0.0) - jnp.log(1.0 + jnp.exp(_neg_abs(nz)))
                lsig = lf - nz
                if diag:
                    lf = jnp.where(strict, lf, 0.0)
                    lsig = jnp.where(strict, lsig, MASKED)
                lsig_s[slot, hh] = lsig
                lf_s[slot, hh] = lf.astype(BF16)

        def grads(j, slot, carries):
            vblk = v_ref[_rows(v_ref, j), :]
            new = []
            for hh in range(SB_GROUP):
                lfb = lf_s[slot, hh]
                lsig = lsig_s[slot, hh]
                sx = _dot(lfb, u_after)
                a = jnp.exp(lsig + sx + carries[hh])
                g = a * _dot_nt(dos[hh], vblk[:, _head_slices(hh)])
                sig = jnp.exp(lsig)
                inside = _dot(g.astype(BF16), u_before)
                dpan[hh, j] = sig * (inside + g) - g
                span[hh, j] = sig
                gsum[hh, j] = inside[:, TILE - 1:TILE] + g[:, TILE - 1:TILE]
                dv_acc[hh, j] += _dot(dots[hh], a.astype(BF16))
                new.append(carries[hh] + sx[:, 0:1] + lfb[:, 0:1].astype(F32))
            return tuple(new)

        def step1(t, carries):
            carries = grads(i - t + 1, (t - 1) % 2, carries)
            scores(i - t, t % 2, False)
            return carries

        zero1 = jnp.zeros((TILE, 1), F32)
        scores(i, 0, True)
        carries = lax.fori_loop(1, i + 1, step1, (zero1,) * SB_GROUP)
        grads(0, i % 2, carries)

        def pass2(j, state):
            kblk = k_ref[_rows(k_ref, j), :]
            new = []
            for hh in range(SB_GROUP):
                before, ndq = state[hh]
                ndzb = (dpan[hh, j] + span[hh, j] * before).astype(BF16)
                dk_acc[hh, j] += _dot(qts[hh], ndzb)
                new.append((before + gsum[hh, j], ndq + _dot(ndzb, kblk[:, _head_slices(hh)])))
            return tuple(new)

        zero2 = (zero1, jnp.zeros((TILE, HEAD_DIM), F32))
        state = lax.fori_loop(0, i + 1, pass2, (zero2,) * SB_GROUP)
        dq_ref[...] = jnp.concatenate([st[1] * -SCALE for st in state], axis=1).astype(BF16)

        @pl.when(i == nb - 1)
        def _():
            for acc, ref in ((dk_acc, dk_ref), (dv_acc, dv_ref)):
                for j in range(nb):
                    ref[j * TILE:(j + 1) * TILE, :] = jnp.concatenate(
                        [acc[hh, j].T for hh in range(SB_GROUP)], axis=1).astype(BF16)

    qspec = pl.BlockSpec((TILE, SB_LANES), lambda p, i: (i, p))
    kvspec = pl.BlockSpec((s, SB_LANES), lambda p, i: (0, p))
    out = jax.ShapeDtypeStruct((s, SB_WIDTH), BF16)
    outs = _pair_grid_call(
        name, body, nb,
        in_specs=[pl.BlockSpec((TILE, SB_LANES), lambda p, i: (i, cb + p)),
                  pl.BlockSpec((s, SB_LANES), lambda p, i: (0, cb + kb + p)),
                  pl.BlockSpec((s, SB_LANES), lambda p, i: (0, cb + 2 * kb + p)),
                  pl.BlockSpec((TILE, SB_LANES), lambda p, i: (i, db + p))],
        out_specs=[qspec, kvspec, kvspec], out_shape=[out, out, out],
        scratch=[pltpu.VMEM((SB_GROUP, nb, HEAD_DIM, TILE), F32), pltpu.VMEM((SB_GROUP, nb, HEAD_DIM, TILE), F32),
                 pltpu.VMEM((SB_GROUP, nb, TILE, TILE), F32), pltpu.VMEM((SB_GROUP, nb, TILE, TILE), F32),
                 pltpu.VMEM((SB_GROUP, nb, TILE, 1), F32),
                 pltpu.VMEM((2, SB_GROUP, TILE, TILE), F32), pltpu.VMEM((2, SB_GROUP, TILE, TILE), BF16)],
        args=(pa, pa, pa, dout), carried=carried, groups=kb)
    return outs[:3], outs[3:]


def _fox_scores(q, kj, cq, crj, causal, diag):
    sc = _dot_nt(q, kj) + (cq - crj)
    if diag:
        sc = jnp.where(causal, sc, NEG_INF)
    return sc


def _fox_fwd(name, pa, col0, ccol4, crow4, carried=None):
    s = pa.shape[0]
    nb = s // TILE
    cb = col0 // FOX_LANES
    kb = FOX_WIDTH // FOX_LANES

    def body(q_ref, k_ref, v_ref, cc_ref, cr_ref, o_ref, lse_ref, sc_s):
        i = pl.program_id(1)
        head0 = pl.program_id(0) * FOX_GROUP
        r = lax.broadcasted_iota(jnp.int32, (TILE, TILE), 0)
        c = lax.broadcasted_iota(jnp.int32, (TILE, TILE), 1)
        causal = c <= r
        qs = [_scaled_q(q_ref, _head_slices(hh)) for hh in range(FOX_GROUP)]
        cqs = [cc_ref[:, HEAD_DIM * hh:HEAD_DIM * hh + 1] for hh in range(FOX_GROUP)]

        def logits(j, slot, diag):
            kblk = k_ref[_rows(k_ref, j), :]
            tops = []
            for hh in range(FOX_GROUP):
                sc = _fox_scores(qs[hh], kblk[:, _head_slices(hh)], cqs[hh], cr_ref[j, pl.ds(head0 + hh, 1), :], causal, diag)
                sc_s[slot, hh] = sc
                tops.append(jnp.max(sc, axis=1, keepdims=True))
            return tuple(tops)

        def update(j, slot, tops, state):
            vblk = v_ref[_rows(v_ref, j), :]
            new = []
            for hh in range(FOX_GROUP):
                m, l, acc = state[hh]
                m2 = jnp.maximum(m, tops[hh])
                alpha = jnp.exp(m - m2)
                p = jnp.exp(sc_s[slot, hh] - m2)
                new.append((m2, l * alpha + jnp.sum(p, axis=1, keepdims=True),
                            acc * alpha + _dot(p.astype(BF16), vblk[:, _head_slices(hh)])))
            return tuple(new)

        def step(t, both):
            tops, state = both
            state = update(i - t + 1, (t - 1) % 2, tops, state)
            return logits(i - t, t % 2, False), state

        zero = (jnp.full((TILE, 1), NEG_INF, F32), jnp.zeros((TILE, 1), F32), jnp.zeros((TILE, HEAD_DIM), F32))
        tops, state = lax.fori_loop(1, i + 1, step, (logits(i, 0, True), (zero,) * FOX_GROUP))
        state = update(0, i % 2, tops, state)
        o_ref[...] = jnp.concatenate([st[2] / st[1] for st in state], axis=1)
        lse_ref[...] = jnp.concatenate(
            [jnp.broadcast_to(st[0] + jnp.log(st[1]), (TILE, HEAD_DIM)) for st in state], axis=1)

    outs = _pair_grid_call(
        name, body, nb,
        in_specs=[pl.BlockSpec((TILE, FOX_LANES), lambda p, i: (i, cb + p)),
                  pl.BlockSpec((s, FOX_LANES), lambda p, i: (0, cb + kb + p)),
                  pl.BlockSpec((s, FOX_LANES), lambda p, i: (0, cb + 2 * kb + p)),
                  pl.BlockSpec((None, TILE, FOX_LANES), lambda p, i: (p, i, 0)),
                  pl.BlockSpec((nb, 8, TILE), lambda p, i: (0, 0, 0))],
        out_specs=[pl.BlockSpec((TILE, FOX_LANES), lambda p, i: (i, p)),
                   pl.BlockSpec((None, TILE, FOX_LANES), lambda p, i: (p, i, 0))],
        out_shape=[jax.ShapeDtypeStruct((s, FOX_WIDTH), F32), jax.ShapeDtypeStruct((kb, s, FOX_LANES), F32)],
        scratch=[pltpu.VMEM((2, FOX_GROUP, TILE, TILE), F32)],
        args=(pa, pa, pa, ccol4, crow4), carried=carried, groups=kb)
    return outs[0], outs[1], outs[2:]


def _fox_bwd(name, pa, col0, ccol4, crow4, out, lse, dout, dcol0, carried=None):
    s = pa.shape[0]
    nb = s // TILE
    cb = col0 // FOX_LANES
    kb = FOX_WIDTH // FOX_LANES
    db = dcol0 // FOX_LANES

    def body(q_ref, k_ref, v_ref, cc_ref, cr_ref, o_ref, lse_ref, do_ref,
             dq_ref, dk_ref, dv_ref, cs_ref, dk_acc, dv_acc, p_s, ds_s):
        i = pl.program_id(1)
        head0 = pl.program_id(0) * FOX_GROUP

        @pl.when(i == 0)
        def _():
            dk_acc[...] = jnp.zeros_like(dk_acc)
            dv_acc[...] = jnp.zeros_like(dv_acc)

        @pl.when(jnp.logical_and(i == 0, head0 == 0))
        def _():
            cs_ref[...] = jnp.zeros_like(cs_ref)

        r = lax.broadcasted_iota(jnp.int32, (TILE, TILE), 0)
        c = lax.broadcasted_iota(jnp.int32, (TILE, TILE), 1)
        causal = c <= r
        qs = [_scaled_q(q_ref, _head_slices(hh)) for hh in range(FOX_GROUP)]
        cqs = [cc_ref[:, HEAD_DIM * hh:HEAD_DIM * hh + 1] for hh in range(FOX_GROUP)]
        lses = [lse_ref[:, HEAD_DIM * hh:HEAD_DIM * hh + 1] for hh in range(FOX_GROUP)]
        dofs = [do_ref[:, _head_slices(hh)] for hh in range(FOX_GROUP)]
        dos = [d_.astype(BF16) for d_ in dofs]
        dots = [d_.T.astype(BF16) for d_ in dofs]
        qts = [q.astype(F32).T.astype(BF16) for q in qs]
        deltas = [jnp.sum(dofs[hh] * o_ref[:, _head_slices(hh)], axis=1, keepdims=True) for hh in range(FOX_GROUP)]

        def probs(j, slot, rowsums, diag):
            kblk = k_ref[_rows(k_ref, j), :]
            vblk = v_ref[_rows(v_ref, j), :]
            new = []
            for hh in range(FOX_GROUP):
                sl = _head_slices(hh)
                sc = _fox_scores(qs[hh], kblk[:, sl], cqs[hh], cr_ref[j, pl.ds(head0 + hh, 1), :], causal, diag)
                p = jnp.exp(sc - lses[hh])
                ds = p * (_dot_nt(dos[hh], vblk[:, sl]) - deltas[hh])
                p_s[slot, hh] = p.astype(BF16)
                ds_s[slot, hh] = ds.astype(BF16)
                cs_ref[j, pl.ds(head0 + hh, 1), :] += jnp.sum(ds, axis=0, keepdims=True)
                new.append(rowsums[hh] + jnp.sum(ds, axis=1, keepdims=True))
            return tuple(new)

        def accumulate(j, slot, dqs):
            kblk = k_ref[_rows(k_ref, j), :]
            new = []
            for hh in range(FOX_GROUP):
                dsb = ds_s[slot, hh]
                dv_acc[hh, j] += _dot(dots[hh], p_s[slot, hh])
                dk_acc[hh, j] += _dot(qts[hh], dsb)
                new.append(dqs[hh] + _dot(dsb, kblk[:, _head_slices(hh)]))
            return tuple(new)

        def step(t, both):
            rowsums, dqs = both
            dqs = accumulate(i - t + 1, (t - 1) % 2, dqs)
            return probs(i - t, t % 2, rowsums, False), dqs

        zero1 = jnp.zeros((TILE, 1), F32)
        zero64 = jnp.zeros((TILE, HEAD_DIM), F32)
        rowsums, dqs = lax.fori_loop(1, i + 1, step,
                                     (probs(i, 0, (zero1,) * FOX_GROUP, True), (zero64,) * FOX_GROUP))
        dqs = accumulate(0, i % 2, dqs)
        for hh in range(FOX_GROUP):
            cs_ref[i, pl.ds(head0 + hh, 1), :] -= jnp.broadcast_to(rowsums[hh], (TILE, LANES)).T[0:1, :]
        dq_ref[...] = jnp.concatenate([dq * SCALE for dq in dqs], axis=1).astype(BF16)

        @pl.when(i == nb - 1)
        def _():
            for acc, ref in ((dk_acc, dk_ref), (dv_acc, dv_ref)):
                for j in range(nb):
                    ref[j * TILE:(j + 1) * TILE, :] = jnp.concatenate(
                        [acc[hh, j].T for hh in range(FOX_GROUP)], axis=1).astype(BF16)

    qspec = pl.BlockSpec((TILE, FOX_LANES), lambda p, i: (i, p))
    kvspec = pl.BlockSpec((s, FOX_LANES), lambda p, i: (0, p))
    o3 = jax.ShapeDtypeStruct((s, FOX_WIDTH), BF16)
    outs = _pair_grid_call(
        name, body, nb,
        in_specs=[pl.BlockSpec((TILE, FOX_LANES), lambda p, i: (i, cb + p)),
                  pl.BlockSpec((s, FOX_LANES), lambda p, i: (0, cb + kb + p)),
                  pl.BlockSpec((s, FOX_LANES), lambda p, i: (0, cb + 2 * kb + p)),
                  pl.BlockSpec((None, TILE, FOX_LANES), lambda p, i: (p, i, 0)),
                  pl.BlockSpec((nb, 8, TILE), lambda p, i: (0, 0, 0)),
                  qspec,
                  pl.BlockSpec((None, TILE, FOX_LANES), lambda p, i: (p, i, 0)),
                  pl.BlockSpec((TILE, FOX_LANES), lambda p, i: (i, db + p))],
        out_specs=[qspec, kvspec, kvspec, pl.BlockSpec((nb, 8, TILE), lambda p, i: (0, 0, 0))],
        out_shape=[o3, o3, o3, jax.ShapeDtypeStruct((nb, 8, TILE), F32)],
        scratch=[pltpu.VMEM((FOX_GROUP, nb, HEAD_DIM, TILE), F32), pltpu.VMEM((FOX_GROUP, nb, HEAD_DIM, TILE), F32),
                 pltpu.VMEM((2, FOX_GROUP, TILE, TILE), BF16), pltpu.VMEM((2, FOX_GROUP, TILE, TILE), BF16)],
        args=(pa, pa, pa, ccol4, crow4, out, lse, dout), carried=carried, groups=kb)
    return outs[:4], outs[4:]


def _mem_fwd(name, pa, mkv):
    s = pa.shape[0]
    ml = mkv.shape[0]
    nb = s // TILE
    cb = QKV_WIDTH // MEM_WIDTH

    def body(q_ref, k_ref, v_ref, o_ref, lse_ref):
        outs, lses = [], []
        for hh in range(MEM_HEADS):
            sl = _head_slices(hh)
            sc = _dot_nt(_scaled_q(q_ref, sl), k_ref[:, sl])
            m = jnp.max(sc, axis=1, keepdims=True)
            p = jnp.exp(sc - m)
            l = jnp.sum(p, axis=1, keepdims=True)
            outs.append(_dot(p.astype(BF16), v_ref[:, sl]) / l)
            lses.append(jnp.broadcast_to(m + jnp.log(l), (TILE, HEAD_DIM)))
        o_ref[...] = jnp.concatenate(outs, axis=1)
        lse_ref[...] = jnp.concatenate(lses, axis=1)

    row = pl.BlockSpec((TILE, MEM_WIDTH), lambda i: (i, 0))
    return pl.pallas_call(
        body, name=name, grid=(nb,),
        in_specs=[pl.BlockSpec((TILE, MEM_WIDTH), lambda i: (i, cb)),
                  pl.BlockSpec((ml, MEM_WIDTH), lambda i: (0, 0)),
                  pl.BlockSpec((ml, MEM_WIDTH), lambda i: (0, 1))],
        out_specs=[row, row],
        out_shape=[jax.ShapeDtypeStruct((s, MEM_WIDTH), F32), jax.ShapeDtypeStruct((s, MEM_WIDTH), F32)],
        compiler_params=_params("parallel"),
    )(pa, mkv, mkv)


def _mem_bwd(name, pa, mkv, out, lse, dout, dcol0):
    s = pa.shape[0]
    ml = mkv.shape[0]
    nb = s // TILE
    cb = QKV_WIDTH // MEM_WIDTH
    db = dcol0 // MEM_WIDTH

    def body(q_ref, k_ref, v_ref, o_ref, lse_ref, do_ref, dq_ref, dkv_ref, dk_acc, dv_acc):
        i = pl.program_id(0)

        @pl.when(i == 0)
        def _():
            dk_acc[...] = jnp.zeros_like(dk_acc)
            dv_acc[...] = jnp.zeros_like(dv_acc)

        dqs = []
        for hh in range(MEM_HEADS):
            sl = _head_slices(hh)
            q = _scaled_q(q_ref, sl)
            kh = k_ref[:, sl]
            dof = do_ref[:, sl]
            do = dof.astype(BF16)
            delta = jnp.sum(dof * o_ref[:, sl], axis=1, keepdims=True)
            p = jnp.exp(_dot_nt(q, kh) - lse_ref[:, HEAD_DIM * hh:HEAD_DIM * hh + 1])
            ds = (p * (_dot_nt(do, v_ref[:, sl]) - delta)).astype(BF16)
            dv_acc[hh] += _dot(dof.T.astype(BF16), p.astype(BF16))
            dk_acc[hh] += _dot(q.astype(F32).T.astype(BF16), ds)
            dqs.append(_dot(ds, kh) * SCALE)
        dq_ref[...] = jnp.concatenate(dqs, axis=1).astype(BF16)

        @pl.when(i == nb - 1)
        def _():
            dkv_ref[...] = jnp.concatenate([dk_acc[hh].T for hh in range(MEM_HEADS)]
                                           + [dv_acc[hh].T for hh in range(MEM_HEADS)], axis=1).astype(BF16)

    row = pl.BlockSpec((TILE, MEM_WIDTH), lambda i: (i, 0))
    return pl.pallas_call(
        body, name=name, grid=(nb,),
        in_specs=[pl.BlockSpec((TILE, MEM_WIDTH), lambda i: (i, cb)),
                  pl.BlockSpec((ml, MEM_WIDTH), lambda i: (0, 0)),
                  pl.BlockSpec((ml, MEM_WIDTH), lambda i: (0, 1)),
                  row, row,
                  pl.BlockSpec((TILE, MEM_WIDTH), lambda i: (i, db))],
        out_specs=[row, pl.BlockSpec((ml, 2 * MEM_WIDTH), lambda i: (0, 0))],
        out_shape=[jax.ShapeDtypeStruct((s, MEM_WIDTH), BF16), jax.ShapeDtypeStruct((ml, 2 * MEM_WIDTH), BF16)],
        scratch_shapes=[pltpu.VMEM((MEM_HEADS, HEAD_DIM, ml), F32), pltpu.VMEM((MEM_HEADS, HEAD_DIM, ml), F32)],
        compiler_params=_params("arbitrary"),
    )(pa, mkv, mkv, out, lse, dout)


def _head_maps():
    col = jnp.arange(MIX_WIDTH)[:, None] // HEAD_DIM
    g = (col == jnp.arange(LANES)[None, :]).astype(BF16)
    return g, g.T


def _normed_heads(osb_ref, ofx_ref, om_ref, g_ref, gt_ref):
    y = jnp.concatenate([osb_ref[...], ofx_ref[...], om_ref[...]], axis=1)
    msq = _sum_l2(y * y, g_ref[...]) * (1.0 / HEAD_DIM)
    rf = _sum_l3(lax.rsqrt(msq + EPS), gt_ref[...])
    return y * rf, rf


def _out_fwd(name, o_sb, o_fx, o_m, pb, ow, x, w_out, ts):
    s, d = x.shape
    g, gt = _head_maps()

    def body(osb_ref, ofx_ref, om_ref, gate_ref, ow_ref, x_ref, w_ref, g_ref, gt_ref, xo_ref, y2_ref):
        yh, _ = _normed_heads(osb_ref, ofx_ref, om_ref, g_ref, gt_ref)
        gate = gate_ref[...]
        y2 = (yh * ow_ref[...] * (gate * jax.nn.sigmoid(gate))).astype(BF16)
        y2_ref[...] = y2
        xo_ref[...] = x_ref[...] + _dot(y2, w_ref[...])

    return pl.pallas_call(
        body, name=name, grid=(s // ts,),
        in_specs=[_row_spec(ts, SB_WIDTH), _row_spec(ts, FOX_WIDTH), _row_spec(ts, MEM_WIDTH),
                  _row_spec(ts, MIX_WIDTH), _const_spec((1, MIX_WIDTH)), _row_spec(ts, d),
                  _const_spec((MIX_WIDTH, d)),
                  _const_spec((MIX_WIDTH, LANES)), _const_spec((LANES, MIX_WIDTH))],
        out_specs=[_row_spec(ts, d), _row_spec(ts, MIX_WIDTH)],
        out_shape=[jax.ShapeDtypeStruct((s, d), F32), jax.ShapeDtypeStruct((s, MIX_WIDTH), BF16)],
        compiler_params=_params("parallel"),
    )(o_sb, o_fx, o_m, pb, ow, x, w_out, g, gt)


def _row_spec(ts, w):
    return pl.BlockSpec((ts, w), lambda i: (i, 0))


def _const_spec(shape):
    return pl.BlockSpec(shape, lambda i: (0,) * len(shape))


def _out_bwd(name, dxb, o_sb, o_fx, o_m, pb, ow, w_out, ts):
    s, d = dxb.shape
    g, gt = _head_maps()

    def body(dx_ref, osb_ref, ofx_ref, om_ref, gate_ref, ow_ref, w_ref, g_ref, gt_ref, dy_ref, dgate_ref, dow_ref):
        @pl.when(pl.program_id(0) == 0)
        def _():
            dow_ref[...] = jnp.zeros_like(dow_ref)

        dy2 = _dot_nt(dx_ref[...], w_ref[...])
        yh, rf = _normed_heads(osb_ref, ofx_ref, om_ref, g_ref, gt_ref)
        gate = gate_ref[...]
        sig = jax.nn.sigmoid(gate)
        ow_v = ow_ref[...]
        dgate_ref[...] = (dy2 * (yh * ow_v) * (sig * (1.0 + gate * (1.0 - sig)))).astype(BF16)
        dn = dy2 * (gate * sig)
        dow_ref[...] += jnp.sum(dn * yh, axis=0, keepdims=True)
        dyh = dn * ow_v
        t = _sum_l2(dyh * yh, g_ref[...]) * (1.0 / HEAD_DIM)
        dy_ref[...] = rf * (dyh - yh * _sum_l3(t, gt_ref[...]))

    return pl.pallas_call(
        body, name=name, grid=(s // ts,),
        in_specs=[_row_spec(ts, d), _row_spec(ts, SB_WIDTH), _row_spec(ts, FOX_WIDTH), _row_spec(ts, MEM_WIDTH),
                  _row_spec(ts, MIX_WIDTH), _const_spec((1, MIX_WIDTH)),
                  _const_spec((MIX_WIDTH, d)),
                  _const_spec((MIX_WIDTH, LANES)), _const_spec((LANES, MIX_WIDTH))],
        out_specs=[_row_spec(ts, MIX_WIDTH), _row_spec(ts, MIX_WIDTH), _const_spec((1, MIX_WIDTH))],
        out_shape=[jax.ShapeDtypeStruct((s, MIX_WIDTH), F32), jax.ShapeDtypeStruct((s, PB), BF16),
                   jax.ShapeDtypeStruct((1, MIX_WIDTH), F32)],
        compiler_params=_params("arbitrary"),
    )(dxb, o_sb, o_fx, o_m, pb, ow, w_out, g, gt)


def _adamw(name, w, g, m, v, tr):
    def body(w_ref, g_ref, m_ref, v_ref, d_ref, m2_ref, v2_ref):
        gv = g_ref[...]
        m2 = ADAM_B1 * m_ref[...] + (1.0 - ADAM_B1) * gv
        v2 = ADAM_B2 * v_ref[...] + (1.0 - ADAM_B2) * (gv * gv)
        m_hat = m2 / (1.0 - ADAM_B1 ** ADAM_STEP)
        v_hat = v2 / (1.0 - ADAM_B2 ** ADAM_STEP)
        d_ref[...] = -ADAM_LR * (m_hat / (jnp.sqrt(v_hat) + ADAM_EPS) + ADAM_WD * w_ref[...])
        m2_ref[...] = m2
        v2_ref[...] = v2

    rest = w.shape[1:]
    spec = pl.BlockSpec((tr,) + rest, lambda i: (i,) + (0,) * len(rest))
    shp = jax.ShapeDtypeStruct(w.shape, F32)
    return pl.pallas_call(
        body, name=name, grid=(w.shape[0] // tr,), in_specs=[spec] * 4, out_specs=[spec] * 3, out_shape=[shp] * 3,
        compiler_params=_params("parallel"),
    )(w, g, m, v)


def _adamw_sharded(name, w, m, v, g_own, g_other, cvec, tr):
    depth, rows, cols = w.shape
    nt = rows // 2 // tr

    def body(c_ref, w_ref, m_ref, v_ref, *rest):
        g_refs, (g_ref, d_ref, m2_ref, v2_ref) = rest[:2 * depth], rest[2 * depth:]
        layer, mine = pl.program_id(0), pl.program_id(1) == c_ref[0]
        gv = None
        for lt in range(depth):
            cand = jnp.where(mine, g_refs[lt][...], g_refs[depth + lt][...])
            gv = cand if gv is None else jnp.where(layer == lt, cand, gv)
        m2 = ADAM_B1 * m_ref[...] + (1.0 - ADAM_B1) * gv
        v2 = ADAM_B2 * v_ref[...] + (1.0 - ADAM_B2) * (gv * gv)
        m_hat = m2 / (1.0 - ADAM_B1 ** ADAM_STEP)
        v_hat = v2 / (1.0 - ADAM_B2 ** ADAM_STEP)
        g_ref[...] = gv
        d_ref[...] = -ADAM_LR * (m_hat / (jnp.sqrt(v_hat) + ADAM_EPS) + ADAM_WD * w_ref[...])
        m2_ref[...] = m2
        v2_ref[...] = v2

    def g_map(lt, own):
        def index(l, hf, i, c_ref):
            use = jnp.logical_and(l == lt, (hf == c_ref[0]) == own)
            return jnp.where(use, i, 0), 0
        return index

    full = pl.BlockSpec((None, tr, cols), lambda l, hf, i, c_ref: (l, hf * nt + i, 0))
    g_specs = [pl.BlockSpec((tr, cols), g_map(lt, own)) for own in (True, False) for lt in range(depth)]
    shp = jax.ShapeDtypeStruct((depth, rows, cols), F32)
    return pl.pallas_call(
        body, name=name,
        grid_spec=pltpu.PrefetchScalarGridSpec(
            num_scalar_prefetch=1, grid=(depth, 2, nt), in_specs=[full] * 3 + g_specs, out_specs=[full] * 4),
        out_shape=[shp] * 4,
        compiler_params=_params("arbitrary", "arbitrary", "arbitrary"),
    )(cvec, w, m, v, *g_own, *g_other)


HBM_SPEC = pl.BlockSpec(memory_space=pltpu.HBM)


def _place():
    x, y, c = lax.axis_index("x"), lax.axis_index("y"), lax.axis_index("c")
    chips = [(1 - x, y), (x, 1 - y), (1 - x, 1 - y)]
    return x, y, c, chips


def _remote(src, dst, send_sems, recv_sems, k, to):
    return pltpu.make_async_remote_copy(src_ref=src, dst_ref=dst, send_sem=send_sems.at[k], recv_sem=recv_sems.at[k],
                                        device_id=to, device_id_type=MESH)


def _half_rows(n_rows, cc):
    rh = n_rows // 2
    return pl.ds(pl.multiple_of(cc * rh, 16), rh)


def _dma_sems(n):
    return [pltpu.SemaphoreType.DMA((n,)), pltpu.SemaphoreType.DMA((n,))]


class _Exchange:
    def __init__(self, inputs, out_shapes, n_sems, begin, relay, finish):
        self.inputs, self.out_shapes, self.n_sems = list(inputs), list(out_shapes), n_sems
        self.begin, self.relay, self.finish = begin, relay, finish

    @property
    def n(self):
        return len(self.inputs)

    def split(self, refs):
        return refs[:self.n], refs[self.n:2 * self.n], refs[2 * self.n], refs[2 * self.n + 1]


def _run_exchange(name, ex):
    def body(*refs):
        parts = ex.split(refs)
        for phase in (ex.begin, ex.relay, ex.finish):
            if phase is not None:
                phase(*parts)

    return pl.pallas_call(
        body, name=name, in_specs=[HBM_SPEC] * ex.n, out_specs=[HBM_SPEC] * ex.n, out_shape=ex.out_shapes,
        scratch_shapes=_dma_sems(ex.n_sems),
    )(*ex.inputs)


def _gather_exchange(shards):
    def ici(in_refs, out_refs, send_sems, recv_sems):
        x, y, c, chips = _place()
        return [_remote(in_ref.at[_half_rows(in_ref.shape[0], c)], out_ref.at[2 * x + y, _half_rows(in_ref.shape[0], c)],
                        send_sems, recv_sems, 6 * a + j, (cx, cy, c))
                for a, (in_ref, out_ref) in enumerate(zip(in_refs, out_refs)) for j, (cx, cy) in enumerate(chips)]

    def d2d(out_refs, send_sems, recv_sems, half_of):
        x, y, c, chips = _place()
        cps = []
        for a, out_ref in enumerate(out_refs):
            for j, (cx, cy) in enumerate(chips):
                piece = out_ref.at[2 * cx + cy, _half_rows(out_ref.shape[1], half_of(c))]
                cps.append(_remote(piece, piece, send_sems, recv_sems, 6 * a + 3 + j, (x, y, 1 - c)))
        return cps

    def begin(in_refs, out_refs, send_sems, recv_sems):
        for cp in ici(in_refs, out_refs, send_sems, recv_sems):
            cp.start()

    def relay(in_refs, out_refs, send_sems, recv_sems):
        x, y, c, chips = _place()
        for a, out_ref in enumerate(out_refs):
            for j, (cx, cy) in enumerate(chips):
                landed = out_ref.at[2 * cx + cy, _half_rows(out_ref.shape[1], c)]
                _remote(landed, landed, send_sems, recv_sems, 6 * a + j, (cx, cy, c)).wait_recv()
        for cp in d2d(out_refs, send_sems, recv_sems, lambda c_: c_):
            cp.start()

    def finish(in_refs, out_refs, send_sems, recv_sems):
        for cp in d2d(out_refs, send_sems, recv_sems, lambda c_: 1 - c_):
            cp.wait_recv()
        for cp in ici(in_refs, out_refs, send_sems, recv_sems) + d2d(out_refs, send_sems, recv_sems, lambda c_: c_):
            cp.wait_send()

    shapes = [jax.ShapeDtypeStruct((N_CHIPS,) + s_.shape, s_.dtype) for s_ in shards]
    return _Exchange(shards, shapes, 6 * len(shards), begin, relay, finish)


def _swap_exchange(g4s):
    def copies(in_refs, out_refs, send_sems, recv_sems):
        x, y, c, _ = _place()
        return [_remote(in_ref.at[:, _half_rows(in_ref.shape[1], 1 - c), :], out_ref, send_sems, recv_sems, a, (x, y, 1 - c))
                for a, (in_ref, out_ref) in enumerate(zip(in_refs, out_refs))]

    def begin(*parts):
        for cp in copies(*parts):
            cp.start()

    def finish(*parts):
        for cp in copies(*parts):
            cp.wait()

    shapes = [jax.ShapeDtypeStruct((g.shape[0], g.shape[1] // 2, g.shape[2]), g.dtype) for g in g4s]
    return _Exchange(g4s, shapes, len(g4s), begin, None, finish)


def _add_half(name, g4, r1, cvec, tr):
    n, r, w = g4.shape
    rh = r // 2
    nblk = rh // tr

    def body(c_ref, a_ref, b_ref, o_ref):
        o_ref[...] = (a_ref[...].astype(F32) + b_ref[...].astype(F32)).astype(BF16)

    return pl.pallas_call(
        body, name=name,
        grid_spec=pltpu.PrefetchScalarGridSpec(
            num_scalar_prefetch=1, grid=(n, nblk),
            in_specs=[pl.BlockSpec((None, tr, w), lambda k, i, c_ref: (k, c_ref[0] * nblk + i, 0)),
                      pl.BlockSpec((None, tr, w), lambda k, i, c_ref: (k, i, 0))],
            out_specs=pl.BlockSpec((None, tr, w), lambda k, i, c_ref: (k, i, 0))),
        out_shape=jax.ShapeDtypeStruct((n, rh, w), BF16),
        compiler_params=_params("parallel", "parallel"),
    )(cvec, g4, r1)


def _scatter_exchange(h4s):
    def sends(in_refs, out_refs, send_sems, recv_sems):
        x, y, c, chips = _place()
        return [_remote(in_ref.at[2 * cx + cy], out_ref.at[j], send_sems, recv_sems, 3 * a + j, (cx, cy, c))
                for a, (in_ref, out_ref) in enumerate(zip(in_refs, out_refs)) for j, (cx, cy) in enumerate(chips)]

    def begin(*parts):
        for cp in sends(*parts):
            cp.start()

    def finish(in_refs, out_refs, send_sems, recv_sems):
        x, y, c, chips = _place()
        for a, out_ref in enumerate(out_refs):
            for j, (cx, cy) in enumerate(chips):
                got = out_ref.at[j]
                _remote(got, got, send_sems, recv_sems, 3 * a + j, (cx, cy, c)).wait_recv()
        for cp in sends(in_refs, out_refs, send_sems, recv_sems):
            cp.wait_send()

    shapes = [jax.ShapeDtypeStruct((3,) + h.shape[1:], h.dtype) for h in h4s]
    return _Exchange(h4s, shapes, 3 * len(h4s), begin, None, finish)


def _sum_chips(name, h4, r3, mvec, tr):
    _, rh, w = h4.shape

    def body(m_ref, a_ref, b_ref, c_ref, d_ref, o_ref):
        o_ref[...] = ((a_ref[...].astype(F32) + b_ref[...].astype(F32)) + c_ref[...].astype(F32)) + d_ref[...].astype(F32)

    specs = [pl.BlockSpec((None, tr, w), lambda i, m_ref: (m_ref[0], i, 0))]
    specs += [pl.BlockSpec((None, tr, w), functools.partial(lambda k, i, m_ref: (k, i, 0), k)) for k in range(3)]
    return pl.pallas_call(
        body, name=name,
        grid_spec=pltpu.PrefetchScalarGridSpec(
            num_scalar_prefetch=1, grid=(rh // tr,), in_specs=specs,
            out_specs=pl.BlockSpec((tr, w), lambda i, m_ref: (i, 0))),
        out_shape=jax.ShapeDtypeStruct((rh, w), F32),
        compiler_params=_params("parallel"),
    )(mvec, h4, r3, r3, r3)


def _swap_reduced_exchange(ghs):
    def copies(in_refs, out_refs, send_sems, recv_sems):
        x, y, c, _ = _place()
        return [_remote(in_ref, out_ref, send_sems, recv_sems, a, (x, y, 1 - c))
                for a, (in_ref, out_ref) in enumerate(zip(in_refs, out_refs))]

    def begin(*parts):
        for cp in copies(*parts):
            cp.start()

    def finish(*parts):
        for cp in copies(*parts):
            cp.wait()

    return _Exchange(ghs, [jax.ShapeDtypeStruct(g.shape, g.dtype) for g in ghs], len(ghs), begin, None, finish)


class _SemaphoresFrom:
    def __init__(self, sems, first):
        self.sems, self.first = sems, first

    @property
    def at(self):
        return self

    def __getitem__(self, k):
        return self.sems.at[self.first + k]


def _both(a, b):
    def phase(fa, fb):
        if fa is None and fb is None:
            return None

        def run(in_refs, out_refs, send_sems, recv_sems):
            if fa is not None:
                fa(in_refs[:a.n], out_refs[:a.n], send_sems, recv_sems)
            if fb is not None:
                fb(in_refs[a.n:], out_refs[a.n:], _SemaphoresFrom(send_sems, a.n_sems), _SemaphoresFrom(recv_sems, a.n_sems))

        return run

    return _Exchange(a.inputs + b.inputs, a.out_shapes + b.out_shapes, a.n_sems + b.n_sems,
                     phase(a.begin, b.begin), phase(a.relay, b.relay), phase(a.finish, b.finish))


def _small_update(name, partials, weights, moments1, moments2):
    n = len(partials)
    width = max(p.shape[1] for p in partials)
    starts, at = [], 0
    for p in partials:
        starts.append(at)
        at += p.shape[0]
    rows = -(-at // 8) * 8
    has_w = [w is not None for w in weights]
    n_w = sum(has_w)

    def body(*refs):
        p_refs = refs[:n]
        w_refs, m_refs, v_refs = refs[n:n + n_w], refs[n + n_w:n + 2 * n_w], refs[n + 2 * n_w:n + 3 * n_w]
        outs = refs[n + 3 * n_w:-4]
        g_refs, upd_refs = outs[:n], outs[n:]
        vec, buf, send_sems, recv_sems = refs[-4:]
        x, y, c, _ = _place()
        me = 4 * x + 2 * y + c
        vec[...] = jnp.zeros_like(vec)
        for p_ref, r0 in zip(p_refs, starts):
            vec[r0:r0 + p_ref.shape[0], 0:p_ref.shape[1]] = p_ref[...]
        buf[me] = vec[...]
        flips = [(fx, fy, fc) for fx in (0, 1) for fy in (0, 1) for fc in (0, 1)][1:]
        peers = [(x + fx - 2 * x * fx, y + fy - 2 * y * fy, c + fc - 2 * c * fc) for fx, fy, fc in flips]
        sends = [_remote(vec, buf.at[me], send_sems, recv_sems, k, peer) for k, peer in enumerate(peers)]
        for cp in sends:
            cp.start()
        for k, (px, py, pc) in enumerate(peers):
            got = buf.at[4 * px + 2 * py + pc]
            _remote(got, got, send_sems, recv_sems, k, (px, py, pc)).wait_recv()
        for cp in sends:
            cp.wait_send()
        total = buf[0]
        for dev in range(1, N_DEV):
            total = total + buf[dev]
        k = 0
        for a in range(n):
            r, w = g_refs[a].shape
            g = total[starts[a]:starts[a] + r, 0:w]
            g_refs[a][...] = g
            if has_w[a]:
                m2 = ADAM_B1 * m_refs[k][...] + (1.0 - ADAM_B1) * g
                v2 = ADAM_B2 * v_refs[k][...] + (1.0 - ADAM_B2) * (g * g)
                m_hat = m2 / (1.0 - ADAM_B1 ** ADAM_STEP)
                v_hat = v2 / (1.0 - ADAM_B2 ** ADAM_STEP)
                upd_refs[3 * k][...] = -ADAM_LR * (m_hat / (jnp.sqrt(v_hat) + ADAM_EPS) + ADAM_WD * w_refs[k][...])
                upd_refs[3 * k + 1][...] = m2
                upd_refs[3 * k + 2][...] = v2
                k += 1

    ws = [w for w in weights if w is not None]
    g_shapes = [jax.ShapeDtypeStruct(p.shape if w is None else w.shape, F32) for p, w in zip(partials, weights)]
    u_shapes = [jax.ShapeDtypeStruct(w.shape, F32) for w in ws for _ in range(3)]
    vm = pl.BlockSpec(memory_space=pltpu.VMEM)
    n_args = n + 3 * n_w
    outs = pl.pallas_call(
        body, name=name, in_specs=[vm] * n_args, out_specs=[vm] * (n + 3 * n_w), out_shape=g_shapes + u_shapes,
        scratch_shapes=[pltpu.VMEM((rows, width), F32), pltpu.VMEM((N_DEV, rows, width), F32),
                        pltpu.SemaphoreType.DMA((7,)), pltpu.SemaphoreType.DMA((7,))],
    )(*partials, *ws, *[m for m in moments1 if m is not None], *[v for v in moments2 if v is not None])
    return outs[:n], outs[n:]


GATE_COL = 3 * SB_WIDTH + 3 * FOX_WIDTH + FOX_HEADS + MEM_WIDTH
FL_COL = QKV_WIDTH


GROUP_A_COLS = [(0, QKV_WIDTH), (FL_COL + FOX_HEADS, MEM_WIDTH)]
GROUP_B_COLS = [(GATE_COL, MIX_WIDTH), (FL_COL, FOX_HEADS)]


def _group_from_shards(shard_of, cw, spans, pad):
    parts = []
    for lo, width in spans:
        hi = lo + width
        for j in range(N_CHIPS):
            a, b = max(lo, j * cw), min(hi, (j + 1) * cw)
            if a < b:
                parts.append(shard_of(j)[:, a - j * cw:b - j * cw])
    if pad:
        parts.append(jnp.zeros((parts[0].shape[0], pad), parts[0].dtype))
    return jnp.concatenate(parts, axis=1)


def _shard_from_groups(ga, gb, j, cw):
    lo, hi = j * cw, (j + 1) * cw
    placed = []
    for grp, spans in ((ga, GROUP_A_COLS), (gb, GROUP_B_COLS)):
        at = 0
        for first, width in spans:
            a, b = max(lo, first), min(hi, first + width)
            if a < b:
                placed.append((a, grp[:, at + a - first:at + b - first]))
            at += width
    return jnp.concatenate([p for _, p in sorted(placed, key=lambda t: t[0])], axis=1)


def _tile_of(n, cap, unit):
    if n <= cap:
        return n
    best = None
    for t in range(unit, cap + 1, unit):
        if n % t == 0:
            best = t
    assert best is not None, (n, cap, unit)
    return best


def _column_major_rows(a):
    dp, r, c = a.shape
    return a.transpose(2, 0, 1).reshape(c, dp, r // LANES, LANES).transpose(0, 2, 1, 3).reshape(-1, 8, LANES)


def _from_column_major_rows(b, shape):
    dp, r, c = shape
    return b.reshape(c, r // LANES, dp, LANES).transpose(0, 2, 1, 3).reshape(c, dp, r).transpose(1, 2, 0)


def _pack_small(parts):
    rows = []
    for p in parts:
        f = p.reshape(-1).astype(F32)
        f = jnp.pad(f, (0, (-f.shape[0]) % LANES))
        rows.append(f.reshape(-1, LANES))
    out = jnp.concatenate(rows, axis=0)
    return jnp.pad(out, ((0, (-out.shape[0]) % 8), (0, 0)))


def _unpack_small(packed, shapes):
    outs, r = [], 0
    for shp in shapes:
        n = 1
        for s_ in shp:
            n *= s_
        nr = -(-n // LANES)
        outs.append(packed[r:r + nr].reshape(-1)[:n].reshape(shp))
        r += nr
    return outs


def kernel(x, mem, norm_w, w_in, b_forget, mem_norm_w, w_mem_kv, out_norm_w, w_out, final_norm_w, loss_target, m_norm_w, m_w_in, m_b_forget, m_mem_norm_w, m_w_mem_kv, m_out_norm_w, m_w_out, m_final_norm_w, v_norm_w, v_w_in, v_b_forget, v_mem_norm_w, v_w_mem_kv, v_out_norm_w, v_w_out, v_final_norm_w):
    xs = x[0]
    mems = mem[0]
    target = loss_target[0]
    s, d = xs.shape
    depth = norm_w.shape[0]
    nb = s // TILE
    ts = _tile_of(s, 512, 8)
    big = (w_in, w_mem_kv, w_out)
    core = lax.axis_index("c")
    chip = 2 * lax.axis_index("x") + lax.axis_index("y")
    cvec = core.astype(jnp.int32).reshape(1)
    mvec = chip.astype(jnp.int32).reshape(1)
    cw = w_in.shape[2]

    own_w = [[a[l].astype(BF16) for a in big] for l in range(depth)]

    def lay_out_in(own, got):
        shard_of = lambda j: jnp.where(chip == j, own, got[j])
        return (_group_from_shards(shard_of, cw, GROUP_A_COLS, 0),
                _group_from_shards(shard_of, cw, GROUP_B_COLS, LANES - FOX_HEADS))

    def lay_out_rows(own, got):
        full = jnp.where(lax.broadcasted_iota(jnp.int32, got.shape, 0) == chip, own[None], got)
        return full.reshape(-1, full.shape[2])

    w_in_groups = [lay_out_in(own_w[0][0], _run_exchange("gather_weights0", _gather_exchange(own_w[0][:1]))[0])]
    layer_w = []

    tm = _tile_of(s, 256, 8)
    fl_block = MIX_WIDTH // LANES

    saved = []
    cur = xs
    for l in range(depth):
        wa, wb = w_in_groups[l]
        h = _rms_fwd(f"rms_fwd{l}", cur, norm_w[l][None], ts)
        pa = _mm(f"inproj_a{l}", h, wa, "nn", tm, _tile_of(PA, 1664, LANES), BF16)
        pb = _mm(f"inproj_b{l}", h, wb, "nn", tm, PB, F32)
        bpad = jnp.pad(b_forget[l], (0, LANES - FOX_HEADS))[None]
        ccol4, crow4 = _gate_fwd(f"gate_fwd{l}", pb, bpad, fl_block)
        more = l + 1 < depth
        o_sb, got = _sb_fwd(f"sb_fwd{l}", pa, 0, carried=_gather_exchange(own_w[l][1:]))
        wkv, wout = lay_out_rows(own_w[l][1], got[0]), lay_out_rows(own_w[l][2], got[1])
        layer_w.append((wa, wb, wkv, wout))
        o_fx, lse_fx, got = _fox_fwd(f"fox_fwd{l}", pa, 3 * SB_WIDTH, ccol4, crow4,
                                     carried=_gather_exchange(own_w[l + 1][:1]) if more else None)
        if more:
            w_in_groups.append(lay_out_in(own_w[l + 1][0], got[0]))
        mn = _rms_fwd(f"mem_rms{l}", mems, mem_norm_w[l][None], mems.shape[0])
        mkv = _mm(f"mem_kv{l}", mn, wkv, "nn", mems.shape[0], 2 * MEM_WIDTH, BF16)
        o_m, lse_m = _mem_fwd(f"mem_fwd{l}", pa, mkv)
        nxt, y2 = _out_fwd(f"out_fwd{l}", o_sb, o_fx, o_m, pb, out_norm_w[l][None], cur, wout, ts)
        saved.append((cur, h, pa, pb, bpad, ccol4, crow4, o_sb, o_fx, lse_fx, mn, mkv, o_m, lse_m, y2))
        cur = nxt

    loss_v, dx, dxb, g_final = _final_loss("final_loss", cur, final_norm_w[None], target, ts)

    g_norm, g_b, g_memnorm, g_outnorm = [None] * depth, [None] * depth, [None] * depth, [None] * depth
    g_wa, g_wb, g_wkv, g_wout = [None] * depth, [None] * depth, [None] * depth, [None] * depth
    g_own = [[None] * depth for _ in big]
    g_other = [[None] * depth for _ in big]

    def swap_of(jobs):
        return _swap_exchange([g for _, _, g, _ in jobs])

    def chip_sums(jobs, got):
        return [(lr, k, _add_half(f"grad_add_half{lr}_{k}", g, r_, cvec, t_), t_) for (lr, k, g, t_), r_ in zip(jobs, got)]

    def sum_at_owner(jobs, from_chips):
        return [_sum_chips(f"grad_sum_chips{lr}_{k}", h_, r_, mvec, t_) for (lr, k, h_, t_), r_ in zip(jobs, from_chips)]

    def keep(jobs, halves, others):
        for (lr, k, _, _), mine, other in zip(jobs, halves, others):
            g_own[k][lr], g_other[k][lr] = mine, other

    def job(lr, k, g4):
        return lr, k, g4, _tile_of(g4.shape[1] // 2, 256, 16)

    pending = []
    for l in reversed(range(depth)):
        xin, h, pa, pb, bpad, ccol4, crow4, o_sb, o_fx, lse_fx, mn, mkv, o_m, lse_m, y2 = saved[l]
        wa, wb, wkv, wout = layer_w[l]
        dy, dgate, g_outnorm[l] = _out_bwd(f"out_bwd{l}", dxb, o_sb, o_fx, o_m, pb, out_norm_w[l][None], wout, ts)
        g_wout[l] = _mm(f"dw_out{l}", y2, dxb, "tn", _tile_of(MIX_WIDTH, 640, LANES), d, F32)
        dq_m, dmkv = _mem_bwd(f"mem_bwd{l}", pa, mkv, o_m, lse_m, dy, SB_WIDTH + FOX_WIDTH)
        g_wkv[l] = _mm(f"dw_kv{l}", mn, dmkv, "tn", d, 2 * MEM_WIDTH, F32)
        dmn = _mm(f"dmem{l}", dmkv, wkv, "nt", mems.shape[0], d, F32)
        g_memnorm[l] = _rms_wgrad(f"mem_norm_grad{l}", mems, dmn)
        small = [job(l, 1, g_wkv[l].reshape(N_CHIPS, -1, g_wkv[l].shape[1])), job(l, 2, g_wout[l].reshape(N_CHIPS, -1, d))]
        (dq_fx, dk_fx, dv_fx, cs4), got = _fox_bwd(f"fox_bwd{l}", pa, 3 * SB_WIDTH, ccol4, crow4, o_fx, lse_fx, dy,
                                                    SB_WIDTH, carried=swap_of(small))
        pending += chip_sums(small, got)
        (dq_sb, dk_sb, dv_sb), from_chips = _sb_bwd(f"sb_bwd{l}", pa, 0, dy, 0,
                                                   carried=_scatter_exchange([j[2] for j in pending]))
        reduced_jobs, reduced = pending, sum_at_owner(pending, from_chips)
        swap_back = _swap_reduced_exchange(reduced)
        dpb, g_b[l] = _gate_bwd(f"gate_bwd{l}", pb, bpad, cs4, fl_block, dgate)
        dpa = jnp.concatenate([dq_sb, dk_sb, dv_sb, dq_fx, dk_fx, dv_fx, dq_m], axis=1)
        tw = _tile_of(d, 512, LANES)
        g_wa[l] = _mm(f"dw_in_a{l}", h, dpa, "tn", tw, _tile_of(PA, 1664, LANES), BF16)
        g_wb[l] = _mm(f"dw_in_b{l}", h, dpb, "tn", tw, PB, BF16)
        g4_in = jnp.stack([_shard_from_groups(g_wa[l], g_wb[l], j, cw) for j in range(N_CHIPS)])
        w_in_job = [job(l, 0, g4_in)]
        if l > 0:
            dx, dxb, g_norm[l], got = _inproj_bwd(f"inproj_bwd{l}", dpa, dpb, wa, wb, xin, norm_w[l][None], dx, tm,
                                                  carried=_both(swap_of(w_in_job), swap_back))
            pending = chip_sums(w_in_job, got[:1])
            keep(reduced_jobs, reduced, got[1:])
        else:
            pending = chip_sums(w_in_job, _run_exchange("grad_swap_halves_last", swap_of(w_in_job)))
            dx, dxb, g_norm[l], got = _inproj_bwd(f"inproj_bwd{l}", dpa, dpb, wa, wb, xin, norm_w[l][None], dx, tm,
                                                  carried=_both(_scatter_exchange([j[2] for j in pending]), swap_back))
            keep(reduced_jobs, reduced, got[1:])
            last = sum_at_owner(pending, got[:1])
            keep(pending, last, _run_exchange("grad_swap_reduced_last", _swap_reduced_exchange(last)))

    small_w = [norm_w, b_forget, mem_norm_w, out_norm_w, final_norm_w]
    small_m = [m_norm_w, m_b_forget, m_mem_norm_w, m_out_norm_w, m_final_norm_w]
    small_v = [v_norm_w, v_b_forget, v_mem_norm_w, v_out_norm_w, v_final_norm_w]
    rows2 = lambda a: a.reshape(-1, a.shape[-1])
    partials = [jnp.concatenate(g_norm, axis=0), jnp.concatenate(g_b, axis=0), jnp.concatenate(g_memnorm, axis=0),
                jnp.concatenate(g_outnorm, axis=0), g_final, loss_v]
    sums, updates = _small_update("small_update", partials, [rows2(a) for a in small_w] + [None],
                                  [rows2(a) for a in small_m] + [None], [rows2(a) for a in small_v] + [None])
    small_grads = [g.reshape(a.shape) for g, a in zip(sums, small_w)]
    loss = sums[-1][0, 0]
    small_delta, small_m2, small_v2 = ([updates[3 * k + t].reshape(a.shape) for k, a in enumerate(small_w)]
                                       for t in range(3))
    big_grads, big_delta, big_m2, big_v2 = [], [], [], []
    for k, (nm, w_, m_, v_) in enumerate(zip(("w_in", "w_mem_kv", "w_out"), big, (m_w_in, m_w_mem_kv, m_w_out),
                                             (v_w_in, v_w_mem_kv, v_w_out))):
        if w_.shape[2] % LANES:
            g_full = jnp.stack([jnp.concatenate([jnp.where(core == 0, go, gt), jnp.where(core == 0, gt, go)], axis=0)
                                for go, gt in zip(g_own[k], g_other[k])])
            w_p, g_p, m_p, v_p = (_column_major_rows(a) for a in (w_, g_full, m_, v_))
            outs = _adamw(f"adamw_{nm}", w_p, g_p, m_p, v_p, _tile_of(w_p.shape[0], 600, 1))
            outs = [_from_column_major_rows(o, w_.shape) for o in (g_p, *outs)]
        else:
            outs = _adamw_sharded(f"adamw_{nm}", w_, m_, v_, g_own[k], g_other[k], cvec,
                                  _tile_of(w_.shape[1] // 2, 256, 8))
        for lst, o in zip((big_grads, big_delta, big_m2, big_v2), outs):
            lst.append(o)

    def order(sm, bg):
        return [sm[0], bg[0], sm[1], sm[2], bg[1], sm[3], bg[2], sm[4]]

    return (loss, dx[None], *order(small_grads, big_grads), *order(small_delta, big_delta),
            *order(small_m2, big_m2), *order(small_v2, big_v2))
```

```python
import functools

import jax
import jax.numpy as jnp
from jax import lax
from jax.experimental import pallas as pl
from jax.experimental.pallas import tpu as pltpu

F32 = jnp.float32
BF16 = jnp.bfloat16

HEAD_DIM = 64
SB_WIDTH = 512
FOX_WIDTH = 512
FOX_HEADS = 8
MEM_WIDTH = 256
MEM_HEADS = MEM_WIDTH // HEAD_DIM
MIX_WIDTH = SB_WIDTH + FOX_WIDTH + MEM_WIDTH
TOTAL_HEADS = MIX_WIDTH // HEAD_DIM
IN_WIDTH = 3 * SB_WIDTH + 3 * FOX_WIDTH + FOX_HEADS + MEM_WIDTH + MIX_WIDTH
LANES = 128
QKV_WIDTH = 3 * SB_WIDTH + 3 * FOX_WIDTH
PA = QKV_WIDTH + MEM_WIDTH
PB = LANES + MIX_WIDTH
EPS = 1e-6
SCALE = HEAD_DIM ** -0.5
TILE = 256
SB_GROUP = 4
SB_LANES = SB_GROUP * HEAD_DIM
SB_FWD_GROUP = 8
FOX_GROUP = 4
FOX_LANES = FOX_GROUP * HEAD_DIM
NEG_INF = float("-inf")
MASKED = -1e30

ADAM_LR = 0.001
ADAM_B1 = 0.9
ADAM_B2 = 0.999
ADAM_EPS = 1e-08
ADAM_WD = 0.01
ADAM_STEP = 10

N_CHIPS = 4
N_DEV = 8
VMEM_LIMIT = 48 * 1024 * 1024
MESH = pl.DeviceIdType.MESH


def _params(*sem):
    return pltpu.CompilerParams(dimension_semantics=tuple(sem), vmem_limit_bytes=VMEM_LIMIT)


def _dot(a, b):
    return jnp.dot(a, b, preferred_element_type=F32)


def _dot_nt(a, b):
    return lax.dot_general(a, b, (((1,), (1,)), ((), ())), preferred_element_type=F32)


def _dot_tn(a, b):
    return lax.dot_general(a, b, (((0,), (0,)), ((), ())), preferred_element_type=F32)


def _split2(x):
    hi = x.astype(BF16)
    lo = (x - hi.astype(F32)).astype(BF16)
    return hi, lo


def _split3(x):
    hi = x.astype(BF16)
    r = x - hi.astype(F32)
    mid = r.astype(BF16)
    lo = (r - mid.astype(F32)).astype(BF16)
    return hi, mid, lo


def _sum_l2(x, u):
    hi, lo = _split2(x)
    return _dot(hi, u) + _dot(lo, u)


def _sum_l3(x, u):
    hi, mid, lo = _split3(x)
    return _dot(hi, u) + _dot(mid, u) + _dot(lo, u)


def _sum_r3(u, x):
    hi, mid, lo = _split3(x)
    return _dot(u, hi) + _dot(u, mid) + _dot(u, lo)


def _tri(n, pred):
    r = lax.broadcasted_iota(jnp.int32, (n, n), 0)
    c = lax.broadcasted_iota(jnp.int32, (n, n), 1)
    return jnp.where(pred(r, c), 1.0, 0.0).astype(BF16)


def _rows(ref, j, n=TILE):
    return pl.ds(pl.multiple_of(j * n, n), n)


def _mm(name, a, b, mode, tm, tn, out_dtype, res=None, a_lead=(), b_lead=()):
    a2, b2 = a.shape[len(a_lead):], b.shape[len(b_lead):]
    if mode == "tn":
        k, m = a2
    else:
        m, k = a2
    n = b2[0] if mode == "nt" else b2[1]
    assert m % tm == 0 and n % tn == 0, (name, m, tm, n, tn)
    na, nb = (None,) * len(a_lead), (None,) * len(b_lead)
    if mode == "tn":
        a_spec = pl.BlockSpec(na + (k, tm), lambda j, i: a_lead + (0, i))
    else:
        a_spec = pl.BlockSpec(na + (tm, k), lambda j, i: a_lead + (i, 0))
    if mode == "nt":
        b_spec = pl.BlockSpec(nb + (tn, k), lambda j, i: b_lead + (j, 0))
    else:
        b_spec = pl.BlockSpec(nb + (k, tn), lambda j, i: b_lead + (0, j))
    o_spec = pl.BlockSpec((tm, tn), lambda j, i: (i, j))
    dot = {"nn": _dot, "nt": _dot_nt, "tn": _dot_tn}[mode]

    def body(a_ref, b_ref, *rest):
        o_ref = rest[-1]
        acc = dot(a_ref[...].astype(BF16), b_ref[...].astype(BF16))
        if res is not None:
            acc = acc + rest[0][...]
        o_ref[...] = acc.astype(o_ref.dtype)

    args, specs = [a, b], [a_spec, b_spec]
    if res is not None:
        args.append(res)
        specs.append(o_spec)
    return pl.pallas_call(
        body, name=name, grid=(n // tn, m // tm), in_specs=specs, out_specs=o_spec,
        out_shape=jax.ShapeDtypeStruct((m, n), out_dtype),
        compiler_params=_params("parallel", "parallel"),
    )(*args)


def _rms_fwd(name, x, g, ts):
    s, d = x.shape

    def body(x_ref, g_ref, o_ref):
        xf = x_ref[...]
        r = lax.rsqrt(jnp.mean(xf * xf, axis=1, keepdims=True) + EPS)
        o_ref[...] = (xf * r * g_ref[...]).astype(BF16)

    return pl.pallas_call(
        body, name=name, grid=(s // ts,),
        in_specs=[pl.BlockSpec((ts, d), lambda i: (i, 0)), pl.BlockSpec((1, d), lambda i: (0, 0))],
        out_specs=pl.BlockSpec((ts, d), lambda i: (i, 0)),
        out_shape=jax.ShapeDtypeStruct((s, d), BF16),
        compiler_params=_params("parallel"),
    )(x, g)


def _inproj_bwd(name, dpa, dpb, wa, wb, x, g, dres, ts, carried=None):
    s, d = x.shape

    def body(dpa_ref, dpb_ref, wa_ref, wb_ref, x_ref, g_ref, dres_ref, dx_ref, dxb_ref, dg_ref):
        @pl.when(pl.program_id(1) == 0)
        def _():
            dg_ref[...] = jnp.zeros_like(dg_ref)

        dhf = _dot_nt(dpa_ref[...], wa_ref[...]) + _dot_nt(dpb_ref[...], wb_ref[...])
        xf = x_ref[...]
        r = lax.rsqrt(jnp.mean(xf * xf, axis=1, keepdims=True) + EPS)
        xh = xf * r
        dg_ref[...] += jnp.sum(dhf * xh, axis=0, keepdims=True)
        dxh = dhf * g_ref[...]
        m = jnp.mean(dxh * xh, axis=1, keepdims=True)
        dx = r * (dxh - xh * m) + dres_ref[...]
        dx_ref[...] = dx
        dxb_ref[...] = dx.astype(BF16)

    row = lambda w: pl.BlockSpec((ts, w), lambda p, i: (i, 0))
    whole = lambda a: pl.BlockSpec(a.shape, lambda p, i: (0, 0))
    outs = _pair_grid_call(
        name, body, s // ts,
        in_specs=[row(dpa.shape[1]), row(dpb.shape[1]), whole(wa), whole(wb), row(d), whole(g), row(d)],
        out_specs=[row(d), row(d), pl.BlockSpec((1, d), lambda p, i: (0, 0))],
        out_shape=[jax.ShapeDtypeStruct((s, d), F32), jax.ShapeDtypeStruct((s, d), BF16),
                   jax.ShapeDtypeStruct((1, d), F32)],
        scratch=[], args=(dpa, dpb, wa, wb, x, g, dres), carried=carried, groups=1)
    return outs[0], outs[1], outs[2], outs[3:]


def _rms_wgrad(name, x, dh):
    m_, d = x.shape

    def body(x_ref, dh_ref, dg_ref):
        xf = x_ref[...]
        r = lax.rsqrt(jnp.mean(xf * xf, axis=1, keepdims=True) + EPS)
        dg_ref[...] = jnp.sum(dh_ref[...] * xf * r, axis=0, keepdims=True)

    return pl.pallas_call(
        body, name=name, out_shape=jax.ShapeDtypeStruct((1, d), F32),
    )(x, dh)


def _final_loss(name, x, g, target, ts):
    s, d = x.shape

    def body(x_ref, g_ref, t_ref, loss_ref, dx_ref, dxb_ref, dg_ref):
        @pl.when(pl.program_id(0) == 0)
        def _():
            dg_ref[...] = jnp.zeros_like(dg_ref)
            loss_ref[...] = jnp.zeros_like(loss_ref)

        xf = x_ref[...]
        gw = g_ref[...]
        r = lax.rsqrt(jnp.mean(xf * xf, axis=1, keepdims=True) + EPS)
        xh = xf * r
        e = xh * gw - t_ref[...]
        part = 0.5 * jnp.sum(jnp.mean(e * e, axis=1, keepdims=True), axis=0, keepdims=True)
        loss_ref[...] += jnp.broadcast_to(part, loss_ref.shape)
        dy = e * (1.0 / d)
        dg_ref[...] += jnp.sum(dy * xh, axis=0, keepdims=True)
        dxh = dy * gw
        m = jnp.mean(dxh * xh, axis=1, keepdims=True)
        dx = r * (dxh - xh * m)
        dx_ref[...] = dx
        dxb_ref[...] = dx.astype(BF16)

    row = pl.BlockSpec((ts, d), lambda i: (i, 0))
    vec = pl.BlockSpec((1, d), lambda i: (0, 0))
    lvec = pl.BlockSpec((1, LANES), lambda i: (0, 0))
    return pl.pallas_call(
        body, name=name, grid=(s // ts,), in_specs=[row, vec, row], out_specs=[lvec, row, row, vec],
        out_shape=[jax.ShapeDtypeStruct((1, LANES), F32), jax.ShapeDtypeStruct((s, d), F32),
                   jax.ShapeDtypeStruct((s, d), BF16), jax.ShapeDtypeStruct((1, d), F32)],
        compiler_params=_params("arbitrary"),
    )(x, g, target)


def _gate_fwd(name, pb, bpad, fl_block):
    s = pb.shape[0]
    nb = s // TILE
    fg = FOX_HEADS // FOX_GROUP

    def body(fl_ref, b_ref, ccol_ref, crow_ref, carry):
        @pl.when(pl.program_id(0) == 0)
        def _():
            carry[...] = jnp.zeros_like(carry)

        u = fl_ref[...] + b_ref[...]
        lf = jnp.minimum(u, 0.0) - jnp.log1p(jnp.exp(-jnp.abs(u)))
        lower = _tri(TILE, lambda r, c: c <= r)
        c = _sum_r3(lower, lf) + carry[0:1, :]
        for grp in range(fg):
            ccol_ref[grp] = jnp.concatenate(
                [jnp.broadcast_to(c[:, grp * FOX_GROUP + hh:grp * FOX_GROUP + hh + 1], (TILE, HEAD_DIM))
                 for hh in range(FOX_GROUP)], axis=1)
        crow_ref[0] = c.T[0:8, :]
        carry[...] = jnp.broadcast_to(c[TILE - 1:TILE, :], carry.shape)

    return pl.pallas_call(
        body, name=name, grid=(nb,),
        in_specs=[pl.BlockSpec((TILE, LANES), lambda i: (i, fl_block)), pl.BlockSpec((1, LANES), lambda i: (0, 0))],
        out_specs=[pl.BlockSpec((fg, TILE, FOX_LANES), lambda i: (0, i, 0)), pl.BlockSpec((1, 8, TILE), lambda i: (i, 0, 0))],
        out_shape=[jax.ShapeDtypeStruct((fg, s, FOX_LANES), F32), jax.ShapeDtypeStruct((nb, 8, TILE), F32)],
        scratch_shapes=[pltpu.VMEM((8, LANES), F32)],
        compiler_params=_params("arbitrary"),
    )(pb, bpad)


def _gate_bwd(name, pb, bpad, colsum, fl_block, dpb):
    s = pb.shape[0]
    nb = s // TILE

    def body(fl_ref, b_ref, cs_ref, dpb_ref, dl_ref, db_ref, carry):
        @pl.when(pl.program_id(0) == 0)
        def _():
            carry[...] = jnp.zeros_like(carry)
            db_ref[...] = jnp.zeros_like(db_ref)

        upper = _tri(TILE, lambda r, c: r >= c)
        rsum = _sum_l3(cs_ref[0], upper) + carry[:, 0:1]
        carry[...] = jnp.broadcast_to(rsum[:, 0:1], carry.shape)
        full = jnp.concatenate([rsum, jnp.zeros((LANES - 8, TILE), F32)], axis=0)
        dlf = -full.T
        u = fl_ref[...] + b_ref[...]
        dlogit = dlf * (1.0 - jax.nn.sigmoid(u))
        dl_ref[...] = dlogit.astype(BF16)
        db_ref[...] += jnp.sum(dlogit, axis=0, keepdims=True)

    logits_block = pl.BlockSpec((TILE, LANES), lambda i: (nb - 1 - i, fl_block))
    return pl.pallas_call(
        body, name=name, grid=(nb,),
        in_specs=[logits_block, pl.BlockSpec((1, LANES), lambda i: (0, 0)),
                  pl.BlockSpec((1, 8, TILE), lambda i: (nb - 1 - i, 0, 0)), pl.BlockSpec(memory_space=pl.ANY)],
        out_specs=[logits_block, pl.BlockSpec((1, LANES), lambda i: (0, 0))],
        out_shape=[jax.ShapeDtypeStruct(dpb.shape, BF16), jax.ShapeDtypeStruct((1, LANES), F32)],
        scratch_shapes=[pltpu.VMEM((8, LANES), F32)], input_output_aliases={3: 0},
        compiler_params=_params("arbitrary"),
    )(pb, bpad, colsum, dpb)


def _head_slices(hh):
    return slice(HEAD_DIM * hh, HEAD_DIM * (hh + 1))


def _scaled_q(q_ref, sl, scale=SCALE):
    return (q_ref[:, sl].astype(F32) * scale).astype(BF16)


def _neg_abs(x):
    sign = jnp.uint32(0x80000000)
    return lax.bitcast_convert_type(lax.bitcast_convert_type(x, jnp.uint32) | sign, F32)


def _pair_grid_call(name, body, nb, in_specs, out_specs, out_shape, scratch, args, carried=None, groups=4):
    if carried is None:
        return pl.pallas_call(
            body, name=name, grid=(groups, nb), in_specs=in_specs, out_specs=out_specs, out_shape=out_shape,
            scratch_shapes=scratch, compiler_params=_params("arbitrary", "arbitrary"),
        )(*args)
    n_in, n_out, n_ex = len(in_specs), len(out_specs), carried.n

    def body_with_copies(*refs):
        own_in, ex_in = refs[:n_in], refs[n_in:n_in + n_ex]
        own_out = refs[n_in + n_ex:n_in + n_ex + n_out]
        ex_out = refs[n_in + n_ex + n_out:n_in + 2 * n_ex + n_out]
        own_scratch, sems = refs[n_in + 2 * n_ex + n_out:-2], refs[-2:]
        parts = (ex_in, ex_out, sems[0], sems[1])
        p, i = pl.program_id(0), pl.program_id(1)
        pl.when(jnp.logical_and(p == 0, i == 0))(lambda: carried.begin(*parts))
        if carried.relay is not None:
            pl.when(jnp.logical_and(p == groups - 1, i == max(nb - 2, 0)))(lambda: carried.relay(*parts))
        body(*own_in, *own_out, *own_scratch)
        pl.when(jnp.logical_and(p == groups - 1, i == nb - 1))(lambda: carried.finish(*parts))

    return pl.pallas_call(
        body_with_copies, name=name, grid=(groups, nb), in_specs=list(in_specs) + [HBM_SPEC] * n_ex,
        out_specs=list(out_specs) + [HBM_SPEC] * n_ex, out_shape=list(out_shape) + carried.out_shapes,
        scratch_shapes=list(scratch) + _dma_sems(carried.n_sems),
        compiler_params=_params("arbitrary", "arbitrary"),
    )(*args, *carried.inputs)


def _sb_fwd(name, pa, col0, carried=None):
    s = pa.shape[0]
    nb = s // TILE
    heads, lanes = SB_FWD_GROUP, SB_FWD_GROUP * HEAD_DIM
    cb = col0 // lanes
    kb = SB_WIDTH // lanes

    def body(q_ref, k_ref, v_ref, o_ref, lsig_s, lf_s):
        i = pl.program_id(1)
        r = lax.broadcasted_iota(jnp.int32, (TILE, TILE), 0)
        c = lax.broadcasted_iota(jnp.int32, (TILE, TILE), 1)
        strict = c < r
        u_after = _tri(TILE, lambda rr, cc: rr > cc)
        qs = [_scaled_q(q_ref, _head_slices(hh), -SCALE) for hh in range(heads)]

        def neg_z(j):
            kblk = k_ref[_rows(k_ref, j), :]
            return [_dot_nt(qs[hh], kblk[:, _head_slices(hh)]) for hh in range(heads)]

        def scores(nzs, slot, diag):
            for hh, nz in enumerate(nzs):
                lf = jnp.minimum(nz, 0.0) - jnp.log(1.0 + jnp.exp(_neg_abs(nz)))
                lsig = lf - nz
                if diag:
                    lf = jnp.where(strict, lf, 0.0)
                    lsig = jnp.where(strict, lsig, MASKED)
                lsig_s[slot, hh] = lsig
                lf_s[slot, hh] = lf.astype(BF16)

        def weigh(j, slot, state):
            vblk = v_ref[_rows(v_ref, j), :]
            new = []
            for hh in range(heads):
                carry, acc = state[hh]
                lfb = lf_s[slot, hh]
                sx = _dot(lfb, u_after)
                a = jnp.exp(lsig_s[slot, hh] + sx + carry)
                new.append((carry + sx[:, 0:1] + lfb[:, 0:1].astype(F32),
                            acc + _dot(a.astype(BF16), vblk[:, _head_slices(hh)])))
            return tuple(new)

        def step(t, state):
            state = weigh(i - t + 1, (t - 1) % 2, state)
            scores(neg_z(i - t), t % 2, False)
            return state

        zero = (jnp.zeros((TILE, 1), F32), jnp.zeros((TILE, HEAD_DIM), F32))
        scores(neg_z(i), 0, True)
        state = lax.fori_loop(1, i + 1, step, (zero,) * heads)
        state = weigh(0, i % 2, state)
        o_ref[...] = jnp.concatenate([st[1] for st in state], axis=1)

    outs = _pair_grid_call(
        name, body, nb,
        in_specs=[pl.BlockSpec((TILE, lanes), lambda p, i: (i, cb + p)),
                  pl.BlockSpec((s, lanes), lambda p, i: (0, cb + kb + p)),
                  pl.BlockSpec((s, lanes), lambda p, i: (0, cb + 2 * kb + p))],
        out_specs=[pl.BlockSpec((TILE, lanes), lambda p, i: (i, p))],
        out_shape=[jax.ShapeDtypeStruct((s, SB_WIDTH), F32)],
        scratch=[pltpu.VMEM((2, heads, TILE, TILE), F32), pltpu.VMEM((2, heads, TILE, TILE), BF16)],
        args=(pa, pa, pa), carried=carried, groups=kb)
    return outs[0], outs[1:]


def _sb_bwd(name, pa, col0, dout, dcol0, carried=None):
    s = pa.shape[0]
    nb = s // TILE
    cb = col0 // SB_LANES
    kb = SB_WIDTH // SB_LANES
    db = dcol0 // SB_LANES

    def body(q_ref, k_ref, v_ref, do_ref, dq_ref, dk_ref, dv_ref, dk_acc, dv_acc, dpan, span, gsum, lsig_s, lf_s):
        i = pl.program_id(1)

        @pl.when(i == 0)
        def _():
            dk_acc[...] = jnp.zeros_like(dk_acc)
            dv_acc[...] = jnp.zeros_like(dv_acc)

        r = lax.broadcasted_iota(jnp.int32, (TILE, TILE), 0)
        c = lax.broadcasted_iota(jnp.int32, (TILE, TILE), 1)
        strict = c < r
        u_after = _tri(TILE, lambda rr, cc: rr > cc)
        u_before = _tri(TILE, lambda rr, cc: rr < cc)
        qs = [_scaled_q(q_ref, _head_slices(hh), -SCALE) for hh in range(SB_GROUP)]
        dos = [do_ref[:, _head_slices(hh)].astype(BF16) for hh in range(SB_GROUP)]
        dots = [do_ref[:, _head_slices(hh)].T.astype(BF16) for hh in range(SB_GROUP)]
        qts = [q.astype(F32).T.astype(BF16) for q in qs]

        def scores(j, slot, diag):
            kblk = k_ref[_rows(k_ref, j), :]
            for hh in range(SB_GROUP):
                nz = _dot_nt(qs[hh], kblk[:, _head_slices(hh)])
                lf = jnp.minimum(nz, 0.0) - jnp.log(1.0 + jnp.exp(_neg_abs(nz)))
                lsig = lf - nz
                if diag:
                    lf = jnp.where(strict, lf, 0.0)
                    lsig = jnp.where(strict, lsig, MASKED)
                lsig_s[slot, hh] = lsig
                lf_s[slot, hh] = lf.astype(BF16)

        def grads(j, slot, carries):
            vblk = v_ref[_rows(v_ref, j), :]
            new = []
            for hh in range(SB_GROUP):
                lfb = lf_s[slot, hh]
                lsig = lsig_s[slot, hh]
                sx = _dot(lfb, u_after)
                a = jnp.exp(lsig + sx + carries[hh])
                g = a * _dot_nt(dos[hh], vblk[:, _head_slices(hh)])
                sig = jnp.exp(lsig)
                inside = _dot(g.astype(BF16), u_before)
                dpan[hh, j] = sig * (inside + g) - g
                span[hh, j] = sig
                gsum[hh, j] = inside[:, TILE - 1:TILE] + g[:, TILE - 1:TILE]
                dv_acc[hh, j] += _dot(dots[hh], a.astype(BF16))
                new.append(carries[hh] + sx[:, 0:1] + lfb[:, 0:1].astype(F32))
            return tuple(new)

        def step1(t, carries):
            carries = grads(i - t + 1, (t - 1) % 2, carries)
            scores(i - t, t % 2, False)
            return carries

        zero1 = jnp.zeros((TILE, 1), F32)
        scores(i, 0, True)
        carries = lax.fori_loop(1, i + 1, step1, (zero1,) * SB_GROUP)
        grads(0, i % 2, carries)

        def pass2(j, state):
            kblk = k_ref[_rows(k_ref, j), :]
            new = []
            for hh in range(SB_GROUP):
                before, ndq = state[hh]
                ndzb = (dpan[hh, j] + span[hh, j] * before).astype(BF16)
                dk_acc[hh, j] += _dot(qts[hh], ndzb)
                new.append((before + gsum[hh, j], ndq + _dot(ndzb, kblk[:, _head_slices(hh)])))
            return tuple(new)

        zero2 = (zero1, jnp.zeros((TILE, HEAD_DIM), F32))
        state = lax.fori_loop(0, i + 1, pass2, (zero2,) * SB_GROUP)
        dq_ref[...] = jnp.concatenate([st[1] * -SCALE for st in state], axis=1).astype(BF16)

        @pl.when(i == nb - 1)
        def _():
            for acc, ref in ((dk_acc, dk_ref), (dv_acc, dv_ref)):
                for j in range(nb):
                    ref[j * TILE:(j + 1) * TILE, :] = jnp.concatenate(
                        [acc[hh, j].T for hh in range(SB_GROUP)], axis=1).astype(BF16)

    qspec = pl.BlockSpec((TILE, SB_LANES), lambda p, i: (i, p))
    kvspec = pl.BlockSpec((s, SB_LANES), lambda p, i: (0, p))
    out = jax.ShapeDtypeStruct((s, SB_WIDTH), BF16)
    outs = _pair_grid_call(
        name, body, nb,
        in_specs=[pl.BlockSpec((TILE, SB_LANES), lambda p, i: (i, cb + p)),
                  pl.BlockSpec((s, SB_LANES), lambda p, i: (0, cb + kb + p)),
                  pl.BlockSpec((s, SB_LANES), lambda p, i: (0, cb + 2 * kb + p)),
                  pl.BlockSpec((TILE, SB_LANES), lambda p, i: (i, db + p))],
        out_specs=[qspec, kvspec, kvspec], out_shape=[out, out, out],
        scratch=[pltpu.VMEM((SB_GROUP, nb, HEAD_DIM, TILE), F32), pltpu.VMEM((SB_GROUP, nb, HEAD_DIM, TILE), F32),
                 pltpu.VMEM((SB_GROUP, nb, TILE, TILE), F32), pltpu.VMEM((SB_GROUP, nb, TILE, TILE), F32),
                 pltpu.VMEM((SB_GROUP, nb, TILE, 1), F32),
                 pltpu.VMEM((2, SB_GROUP, TILE, TILE), F32), pltpu.VMEM((2, SB_GROUP, TILE, TILE), BF16)],
        args=(pa, pa, pa, dout), carried=carried, groups=kb)
    return outs[:3], outs[3:]


def _fox_scores(q, kj, cq, crj, causal, diag):
    sc = _dot_nt(q, kj) + (cq - crj)
    if diag:
        sc = jnp.where(causal, sc, NEG_INF)
    return sc


def _fox_fwd(name, pa, col0, ccol4, crow4, carried=None):
    s = pa.shape[0]
    nb = s // TILE
    cb = col0 // FOX_LANES
    kb = FOX_WIDTH // FOX_LANES

    def body(q_ref, k_ref, v_ref, cc_ref, cr_ref, o_ref, lse_ref, sc_s):
        i = pl.program_id(1)
        head0 = pl.program_id(0) * FOX_GROUP
        r = lax.broadcasted_iota(jnp.int32, (TILE, TILE), 0)
        c = lax.broadcasted_iota(jnp.int32, (TILE, TILE), 1)
        causal = c <= r
        qs = [_scaled_q(q_ref, _head_slices(hh)) for hh in range(FOX_GROUP)]
        cqs = [cc_ref[:, HEAD_DIM * hh:HEAD_DIM * hh + 1] for hh in range(FOX_GROUP)]

        def logits(j, slot, diag):
            kblk = k_ref[_rows(k_ref, j), :]
            tops = []
            for hh in range(FOX_GROUP):
                sc = _fox_scores(qs[hh], kblk[:, _head_slices(hh)], cqs[hh], cr_ref[j, pl.ds(head0 + hh, 1), :], causal, diag)
                sc_s[slot, hh] = sc
                tops.append(jnp.max(sc, axis=1, keepdims=True))
            return tuple(tops)

        def update(j, slot, tops, state):
            vblk = v_ref[_rows(v_ref, j), :]
            new = []
            for hh in range(FOX_GROUP):
                m, l, acc = state[hh]
                m2 = jnp.maximum(m, tops[hh])
                alpha = jnp.exp(m - m2)
                p = jnp.exp(sc_s[slot, hh] - m2)
                new.append((m2, l * alpha + jnp.sum(p, axis=1, keepdims=True),
                            acc * alpha + _dot(p.astype(BF16), vblk[:, _head_slices(hh)])))
            return tuple(new)

        def step(t, both):
            tops, state = both
            state = update(i - t + 1, (t - 1) % 2, tops, state)
            return logits(i - t, t % 2, False), state

        zero = (jnp.full((TILE, 1), NEG_INF, F32), jnp.zeros((TILE, 1), F32), jnp.zeros((TILE, HEAD_DIM), F32))
        tops, state = lax.fori_loop(1, i + 1, step, (logits(i, 0, True), (zero,) * FOX_GROUP))
        state = update(0, i % 2, tops, state)
        o_ref[...] = jnp.concatenate([st[2] / st[1] for st in state], axis=1)
        lse_ref[...] = jnp.concatenate(
            [jnp.broadcast_to(st[0] + jnp.log(st[1]), (TILE, HEAD_DIM)) for st in state], axis=1)

    outs = _pair_grid_call(
        name, body, nb,
        in_specs=[pl.BlockSpec((TILE, FOX_LANES), lambda p, i: (i, cb + p)),
                  pl.BlockSpec((s, FOX_LANES), lambda p, i: (0, cb + kb + p)),
                  pl.BlockSpec((s, FOX_LANES), lambda p, i: (0, cb + 2 * kb + p)),
                  pl.BlockSpec((None, TILE, FOX_LANES), lambda p, i: (p, i, 0)),
                  pl.BlockSpec((nb, 8, TILE), lambda p, i: (0, 0, 0))],
        out_specs=[pl.BlockSpec((TILE, FOX_LANES), lambda p, i: (i, p)),
                   pl.BlockSpec((None, TILE, FOX_LANES), lambda p, i: (p, i, 0))],
        out_shape=[jax.ShapeDtypeStruct((s, FOX_WIDTH), F32), jax.ShapeDtypeStruct((kb, s, FOX_LANES), F32)],
        scratch=[pltpu.VMEM((2, FOX_GROUP, TILE, TILE), F32)],
        args=(pa, pa, pa, ccol4, crow4), carried=carried, groups=kb)
    return outs[0], outs[1], outs[2:]


def _fox_bwd(name, pa, col0, ccol4, crow4, out, lse, dout, dcol0, carried=None):
    s = pa.shape[0]
    nb = s // TILE
    cb = col0 // FOX_LANES
    kb = FOX_WIDTH // FOX_LANES
    db = dcol0 // FOX_LANES

    def body(q_ref, k_ref, v_ref, cc_ref, cr_ref, o_ref, lse_ref, do_ref,
             dq_ref, dk_ref, dv_ref, cs_ref, dk_acc, dv_acc, p_s, ds_s):
        i = pl.program_id(1)
        head0 = pl.program_id(0) * FOX_GROUP

        @pl.when(i == 0)
        def _():
            dk_acc[...] = jnp.zeros_like(dk_acc)
            dv_acc[...] = jnp.zeros_like(dv_acc)

        @pl.when(jnp.logical_and(i == 0, head0 == 0))
        def _():
            cs_ref[...] = jnp.zeros_like(cs_ref)

        r = lax.broadcasted_iota(jnp.int32, (TILE, TILE), 0)
        c = lax.broadcasted_iota(jnp.int32, (TILE, TILE), 1)
        causal = c <= r
        qs = [_scaled_q(q_ref, _head_slices(hh)) for hh in range(FOX_GROUP)]
        cqs = [cc_ref[:, HEAD_DIM * hh:HEAD_DIM * hh + 1] for hh in range(FOX_GROUP)]
        lses = [lse_ref[:, HEAD_DIM * hh:HEAD_DIM * hh + 1] for hh in range(FOX_GROUP)]
        dofs = [do_ref[:, _head_slices(hh)] for hh in range(FOX_GROUP)]
        dos = [d_.astype(BF16) for d_ in dofs]
        dots = [d_.T.astype(BF16) for d_ in dofs]
        qts = [q.astype(F32).T.astype(BF16) for q in qs]
        deltas = [jnp.sum(dofs[hh] * o_ref[:, _head_slices(hh)], axis=1, keepdims=True) for hh in range(FOX_GROUP)]

        def probs(j, slot, rowsums, diag):
            kblk = k_ref[_rows(k_ref, j), :]
            vblk = v_ref[_rows(v_ref, j), :]
            new = []
            for hh in range(FOX_GROUP):
                sl = _head_slices(hh)
                sc = _fox_scores(qs[hh], kblk[:, sl], cqs[hh], cr_ref[j, pl.ds(head0 + hh, 1), :], causal, diag)
                p = jnp.exp(sc - lses[hh])
                ds = p * (_dot_nt(dos[hh], vblk[:, sl]) - deltas[hh])
                p_s[slot, hh] = p.astype(BF16)
                ds_s[slot, hh] = ds.astype(BF16)
                cs_ref[j, pl.ds(head0 + hh, 1), :] += jnp.sum(ds, axis=0, keepdims=True)
                new.append(rowsums[hh] + jnp.sum(ds, axis=1, keepdims=True))
            return tuple(new)

        def accumulate(j, slot, dqs):
            kblk = k_ref[_rows(k_ref, j), :]
            new = []
            for hh in range(FOX_GROUP):
                dsb = ds_s[slot, hh]
                dv_acc[hh, j] += _dot(dots[hh], p_s[slot, hh])
                dk_acc[hh, j] += _dot(qts[hh], dsb)
                new.append(dqs[hh] + _dot(dsb, kblk[:, _head_slices(hh)]))
            return tuple(new)

        def step(t, both):
            rowsums, dqs = both
            dqs = accumulate(i - t + 1, (t - 1) % 2, dqs)
            return probs(i - t, t % 2, rowsums, False), dqs

        zero1 = jnp.zeros((TILE, 1), F32)
        zero64 = jnp.zeros((TILE, HEAD_DIM), F32)
        rowsums, dqs = lax.fori_loop(1, i + 1, step,
                                     (probs(i, 0, (zero1,) * FOX_GROUP, True), (zero64,) * FOX_GROUP))
        dqs = accumulate(0, i % 2, dqs)
        for hh in range(FOX_GROUP):
            cs_ref[i, pl.ds(head0 + hh, 1), :] -= jnp.broadcast_to(rowsums[hh], (TILE, LANES)).T[0:1, :]
        dq_ref[...] = jnp.concatenate([dq * SCALE for dq in dqs], axis=1).astype(BF16)

        @pl.when(i == nb - 1)
        def _():
            for acc, ref in ((dk_acc, dk_ref), (dv_acc, dv_ref)):
                for j in range(nb):
                    ref[j * TILE:(j + 1) * TILE, :] = jnp.concatenate(
                        [acc[hh, j].T for hh in range(FOX_GROUP)], axis=1).astype(BF16)

    qspec = pl.BlockSpec((TILE, FOX_LANES), lambda p, i: (i, p))
    kvspec = pl.BlockSpec((s, FOX_LANES), lambda p, i: (0, p))
    o3 = jax.ShapeDtypeStruct((s, FOX_WIDTH), BF16)
    outs = _pair_grid_call(
        name, body, nb,
        in_specs=[pl.BlockSpec((TILE, FOX_LANES), lambda p, i: (i, cb + p)),
                  pl.BlockSpec((s, FOX_LANES), lambda p, i: (0, cb + kb + p)),
                  pl.BlockSpec((s, FOX_LANES), lambda p, i: (0, cb + 2 * kb + p)),
                  pl.BlockSpec((None, TILE, FOX_LANES), lambda p, i: (p, i, 0)),
                  pl.BlockSpec((nb, 8, TILE), lambda p, i: (0, 0, 0)),
                  qspec,
                  pl.BlockSpec((None, TILE, FOX_LANES), lambda p, i: (p, i, 0)),
                  pl.BlockSpec((TILE, FOX_LANES), lambda p, i: (i, db + p))],
        out_specs=[qspec, kvspec, kvspec, pl.BlockSpec((nb, 8, TILE), lambda p, i: (0, 0, 0))],
        out_shape=[o3, o3, o3, jax.ShapeDtypeStruct((nb, 8, TILE), F32)],
        scratch=[pltpu.VMEM((FOX_GROUP, nb, HEAD_DIM, TILE), F32), pltpu.VMEM((FOX_GROUP, nb, HEAD_DIM, TILE), F32),
                 pltpu.VMEM((2, FOX_GROUP, TILE, TILE), BF16), pltpu.VMEM((2, FOX_GROUP, TILE, TILE), BF16)],
        args=(pa, pa, pa, ccol4, crow4, out, lse, dout), carried=carried, groups=kb)
    return outs[:4], outs[4:]


def _mem_fwd(name, pa, mkv):
    s = pa.shape[0]
    ml = mkv.shape[0]
    nb = s // TILE
    cb = QKV_WIDTH // MEM_WIDTH

    def body(q_ref, k_ref, v_ref, o_ref, lse_ref):
        outs, lses = [], []
        for hh in range(MEM_HEADS):
            sl = _head_slices(hh)
            sc = _dot_nt(_scaled_q(q_ref, sl), k_ref[:, sl])
            m = jnp.max(sc, axis=1, keepdims=True)
            p = jnp.exp(sc - m)
            l = jnp.sum(p, axis=1, keepdims=True)
            outs.append(_dot(p.astype(BF16), v_ref[:, sl]) / l)
            lses.append(jnp.broadcast_to(m + jnp.log(l), (TILE, HEAD_DIM)))
        o_ref[...] = jnp.concatenate(outs, axis=1)
        lse_ref[...] = jnp.concatenate(lses, axis=1)

    row = pl.BlockSpec((TILE, MEM_WIDTH), lambda i: (i, 0))
    return pl.pallas_call(
        body, name=name, grid=(nb,),
        in_specs=[pl.BlockSpec((TILE, MEM_WIDTH), lambda i: (i, cb)),
                  pl.BlockSpec((ml, MEM_WIDTH), lambda i: (0, 0)),
                  pl.BlockSpec((ml, MEM_WIDTH), lambda i: (0, 1))],
        out_specs=[row, row],
        out_shape=[jax.ShapeDtypeStruct((s, MEM_WIDTH), F32), jax.ShapeDtypeStruct((s, MEM_WIDTH), F32)],
        compiler_params=_params("parallel"),
    )(pa, mkv, mkv)


def _mem_bwd(name, pa, mkv, out, lse, dout, dcol0):
    s = pa.shape[0]
    ml = mkv.shape[0]
    nb = s // TILE
    cb = QKV_WIDTH // MEM_WIDTH
    db = dcol0 // MEM_WIDTH

    def body(q_ref, k_ref, v_ref, o_ref, lse_ref, do_ref, dq_ref, dkv_ref, dk_acc, dv_acc):
        i = pl.program_id(0)

        @pl.when(i == 0)
        def _():
            dk_acc[...] = jnp.zeros_like(dk_acc)
            dv_acc[...] = jnp.zeros_like(dv_acc)

        dqs = []
        for hh in range(MEM_HEADS):
            sl = _head_slices(hh)
            q = _scaled_q(q_ref, sl)
            kh = k_ref[:, sl]
            dof = do_ref[:, sl]
            do = dof.astype(BF16)
            delta = jnp.sum(dof * o_ref[:, sl], axis=1, keepdims=True)
            p = jnp.exp(_dot_nt(q, kh) - lse_ref[:, HEAD_DIM * hh:HEAD_DIM * hh + 1])
            ds = (p * (_dot_nt(do, v_ref[:, sl]) - delta)).astype(BF16)
            dv_acc[hh] += _dot(dof.T.astype(BF16), p.astype(BF16))
            dk_acc[hh] += _dot(q.astype(F32).T.astype(BF16), ds)
            dqs.append(_dot(ds, kh) * SCALE)
        dq_ref[...] = jnp.concatenate(dqs, axis=1).astype(BF16)

        @pl.when(i == nb - 1)
        def _():
            dkv_ref[...] = jnp.concatenate([dk_acc[hh].T for hh in range(MEM_HEADS)]
                                           + [dv_acc[hh].T for hh in range(MEM_HEADS)], axis=1).astype(BF16)

    row = pl.BlockSpec((TILE, MEM_WIDTH), lambda i: (i, 0))
    return pl.pallas_call(
        body, name=name, grid=(nb,),
        in_specs=[pl.BlockSpec((TILE, MEM_WIDTH), lambda i: (i, cb)),
                  pl.BlockSpec((ml, MEM_WIDTH), lambda i: (0, 0)),
                  pl.BlockSpec((ml, MEM_WIDTH), lambda i: (0, 1)),
                  row, row,
                  pl.BlockSpec((TILE, MEM_WIDTH), lambda i: (i, db))],
        out_specs=[row, pl.BlockSpec((ml, 2 * MEM_WIDTH), lambda i: (0, 0))],
        out_shape=[jax.ShapeDtypeStruct((s, MEM_WIDTH), BF16), jax.ShapeDtypeStruct((ml, 2 * MEM_WIDTH), BF16)],
        scratch_shapes=[pltpu.VMEM((MEM_HEADS, HEAD_DIM, ml), F32), pltpu.VMEM((MEM_HEADS, HEAD_DIM, ml), F32)],
        compiler_params=_params("arbitrary"),
    )(pa, mkv, mkv, out, lse, dout)


def _head_maps():
    col = jnp.arange(MIX_WIDTH)[:, None] // HEAD_DIM
    g = (col == jnp.arange(LANES)[None, :]).astype(BF16)
    return g, g.T


def _normed_heads(osb_ref, ofx_ref, om_ref, g_ref, gt_ref):
    y = jnp.concatenate([osb_ref[...], ofx_ref[...], om_ref[...]], axis=1)
    msq = _sum_l2(y * y, g_ref[...]) * (1.0 / HEAD_DIM)
    rf = _sum_l3(lax.rsqrt(msq + EPS), gt_ref[...])
    return y * rf, rf


def _out_fwd(name, o_sb, o_fx, o_m, pb, ow, x, w_out, ts):
    s, d = x.shape
    g, gt = _head_maps()

    def body(osb_ref, ofx_ref, om_ref, gate_ref, ow_ref, x_ref, w_ref, g_ref, gt_ref, xo_ref, y2_ref):
        yh, _ = _normed_heads(osb_ref, ofx_ref, om_ref, g_ref, gt_ref)
        gate = gate_ref[...]
        y2 = (yh * ow_ref[...] * (gate * jax.nn.sigmoid(gate))).astype(BF16)
        y2_ref[...] = y2
        xo_ref[...] = x_ref[...] + _dot(y2, w_ref[...])

    return pl.pallas_call(
        body, name=name, grid=(s // ts,),
        in_specs=[_row_spec(ts, SB_WIDTH), _row_spec(ts, FOX_WIDTH), _row_spec(ts, MEM_WIDTH),
                  _row_spec(ts, MIX_WIDTH), _const_spec((1, MIX_WIDTH)), _row_spec(ts, d),
                  _const_spec((MIX_WIDTH, d)),
                  _const_spec((MIX_WIDTH, LANES)), _const_spec((LANES, MIX_WIDTH))],
        out_specs=[_row_spec(ts, d), _row_spec(ts, MIX_WIDTH)],
        out_shape=[jax.ShapeDtypeStruct((s, d), F32), jax.ShapeDtypeStruct((s, MIX_WIDTH), BF16)],
        compiler_params=_params("parallel"),
    )(o_sb, o_fx, o_m, pb, ow, x, w_out, g, gt)


def _row_spec(ts, w):
    return pl.BlockSpec((ts, w), lambda i: (i, 0))


def _const_spec(shape):
    return pl.BlockSpec(shape, lambda i: (0,) * len(shape))


def _out_bwd(name, dxb, o_sb, o_fx, o_m, pb, ow, w_out, ts):
    s, d = dxb.shape
    g, gt = _head_maps()

    def body(dx_ref, osb_ref, ofx_ref, om_ref, gate_ref, ow_ref, w_ref, g_ref, gt_ref, dy_ref, dgate_ref, dow_ref):
        @pl.when(pl.program_id(0) == 0)
        def _():
            dow_ref[...] = jnp.zeros_like(dow_ref)

        dy2 = _dot_nt(dx_ref[...], w_ref[...])
        yh, rf = _normed_heads(osb_ref, ofx_ref, om_ref, g_ref, gt_ref)
        gate = gate_ref[...]
        sig = jax.nn.sigmoid(gate)
        ow_v = ow_ref[...]
        dgate_ref[...] = (dy2 * (yh * ow_v) * (sig * (1.0 + gate * (1.0 - sig)))).astype(BF16)
        dn = dy2 * (gate * sig)
        dow_ref[...] += jnp.sum(dn * yh, axis=0, keepdims=True)
        dyh = dn * ow_v
        t = _sum_l2(dyh * yh, g_ref[...]) * (1.0 / HEAD_DIM)
        dy_ref[...] = rf * (dyh - yh * _sum_l3(t, gt_ref[...]))

    return pl.pallas_call(
        body, name=name, grid=(s // ts,),
        in_specs=[_row_spec(ts, d), _row_spec(ts, SB_WIDTH), _row_spec(ts, FOX_WIDTH), _row_spec(ts, MEM_WIDTH),
                  _row_spec(ts, MIX_WIDTH), _const_spec((1, MIX_WIDTH)),
                  _const_spec((MIX_WIDTH, d)),
                  _const_spec((MIX_WIDTH, LANES)), _const_spec((LANES, MIX_WIDTH))],
        out_specs=[_row_spec(ts, MIX_WIDTH), _row_spec(ts, MIX_WIDTH), _const_spec((1, MIX_WIDTH))],
        out_shape=[jax.ShapeDtypeStruct((s, MIX_WIDTH), F32), jax.ShapeDtypeStruct((s, PB), BF16),
                   jax.ShapeDtypeStruct((1, MIX_WIDTH), F32)],
        compiler_params=_params("arbitrary"),
    )(dxb, o_sb, o_fx, o_m, pb, ow, w_out, g, gt)


def _adamw(name, w, g, m, v, tr):
    def body(w_ref, g_ref, m_ref, v_ref, d_ref, m2_ref, v2_ref):
        gv = g_ref[...]
        m2 = ADAM_B1 * m_ref[...] + (1.0 - ADAM_B1) * gv
        v2 = ADAM_B2 * v_ref[...] + (1.0 - ADAM_B2) * (gv * gv)
        m_hat = m2 / (1.0 - ADAM_B1 ** ADAM_STEP)
        v_hat = v2 / (1.0 - ADAM_B2 ** ADAM_STEP)
        d_ref[...] = -ADAM_LR * (m_hat / (jnp.sqrt(v_hat) + ADAM_EPS) + ADAM_WD * w_ref[...])
        m2_ref[...] = m2
        v2_ref[...] = v2

    rest = w.shape[1:]
    spec = pl.BlockSpec((tr,) + rest, lambda i: (i,) + (0,) * len(rest))
    shp = jax.ShapeDtypeStruct(w.shape, F32)
    return pl.pallas_call(
        body, name=name, grid=(w.shape[0] // tr,), in_specs=[spec] * 4, out_specs=[spec] * 3, out_shape=[shp] * 3,
        compiler_params=_params("parallel"),
    )(w, g, m, v)


def _adamw_sharded(name, w, m, v, g_own, g_other, cvec, tr):
    depth, rows, cols = w.shape
    nt = rows // 2 // tr

    def body(c_ref, w_ref, m_ref, v_ref, *rest):
        g_refs, (g_ref, d_ref, m2_ref, v2_ref) = rest[:2 * depth], rest[2 * depth:]
        layer, mine = pl.program_id(0), pl.program_id(1) == c_ref[0]
        gv = None
        for lt in range(depth):
            cand = jnp.where(mine, g_refs[lt][...], g_refs[depth + lt][...])
            gv = cand if gv is None else jnp.where(layer == lt, cand, gv)
        m2 = ADAM_B1 * m_ref[...] + (1.0 - ADAM_B1) * gv
        v2 = ADAM_B2 * v_ref[...] + (1.0 - ADAM_B2) * (gv * gv)
        m_hat = m2 / (1.0 - ADAM_B1 ** ADAM_STEP)
        v_hat = v2 / (1.0 - ADAM_B2 ** ADAM_STEP)
        g_ref[...] = gv
        d_ref[...] = -ADAM_LR * (m_hat / (jnp.sqrt(v_hat) + ADAM_EPS) + ADAM_WD * w_ref[...])
        m2_ref[...] = m2
        v2_ref[...] = v2

    def g_map(lt, own):
        def index(l, hf, i, c_ref):
            use = jnp.logical_and(l == lt, (hf == c_ref[0]) == own)
            return jnp.where(use, i, 0), 0
        return index

    full = pl.BlockSpec((None, tr, cols), lambda l, hf, i, c_ref: (l, hf * nt + i, 0))
    g_specs = [pl.BlockSpec((tr, cols), g_map(lt, own)) for own in (True, False) for lt in range(depth)]
    shp = jax.ShapeDtypeStruct((depth, rows, cols), F32)
    return pl.pallas_call(
        body, name=name,
        grid_spec=pltpu.PrefetchScalarGridSpec(
            num_scalar_prefetch=1, grid=(depth, 2, nt), in_specs=[full] * 3 + g_specs, out_specs=[full] * 4),
        out_shape=[shp] * 4,
        compiler_params=_params("arbitrary", "arbitrary", "arbitrary"),
    )(cvec, w, m, v, *g_own, *g_other)


HBM_SPEC = pl.BlockSpec(memory_space=pltpu.HBM)


def _place():
    x, y, c = lax.axis_index("x"), lax.axis_index("y"), lax.axis_index("c")
    chips = [(1 - x, y), (x, 1 - y), (1 - x, 1 - y)]
    return x, y, c, chips


def _remote(src, dst, send_sems, recv_sems, k, to):
    return pltpu.make_async_remote_copy(src_ref=src, dst_ref=dst, send_sem=send_sems.at[k], recv_sem=recv_sems.at[k],
                                        device_id=to, device_id_type=MESH)


def _half_rows(n_rows, cc):
    rh = n_rows // 2
    return pl.ds(pl.multiple_of(cc * rh, 16), rh)


def _dma_sems(n):
    return [pltpu.SemaphoreType.DMA((n,)), pltpu.SemaphoreType.DMA((n,))]


class _Exchange:
    def __init__(self, inputs, out_shapes, n_sems, begin, relay, finish):
        self.inputs, self.out_shapes, self.n_sems = list(inputs), list(out_shapes), n_sems
        self.begin, self.relay, self.finish = begin, relay, finish

    @property
    def n(self):
        return len(self.inputs)

    def split(self, refs):
        return refs[:self.n], refs[self.n:2 * self.n], refs[2 * self.n], refs[2 * self.n + 1]


def _run_exchange(name, ex):
    def body(*refs):
        parts = ex.split(refs)
        for phase in (ex.begin, ex.relay, ex.finish):
            if phase is not None:
                phase(*parts)

    return pl.pallas_call(
        body, name=name, in_specs=[HBM_SPEC] * ex.n, out_specs=[HBM_SPEC] * ex.n, out_shape=ex.out_shapes,
        scratch_shapes=_dma_sems(ex.n_sems),
    )(*ex.inputs)


def _gather_exchange(shards):
    def ici(in_refs, out_refs, send_sems, recv_sems):
        x, y, c, chips = _place()
        return [_remote(in_ref.at[_half_rows(in_ref.shape[0], c)], out_ref.at[2 * x + y, _half_rows(in_ref.shape[0], c)],
                        send_sems, recv_sems, 6 * a + j, (cx, cy, c))
                for a, (in_ref, out_ref) in enumerate(zip(in_refs, out_refs)) for j, (cx, cy) in enumerate(chips)]

    def d2d(out_refs, send_sems, recv_sems, half_of):
        x, y, c, chips = _place()
        cps = []
        for a, out_ref in enumerate(out_refs):
            for j, (cx, cy) in enumerate(chips):
                piece = out_ref.at[2 * cx + cy, _half_rows(out_ref.shape[1], half_of(c))]
                cps.append(_remote(piece, piece, send_sems, recv_sems, 6 * a + 3 + j, (x, y, 1 - c)))
        return cps

    def begin(in_refs, out_refs, send_sems, recv_sems):
        for cp in ici(in_refs, out_refs, send_sems, recv_sems):
            cp.start()

    def relay(in_refs, out_refs, send_sems, recv_sems):
        x, y, c, chips = _place()
        for a, out_ref in enumerate(out_refs):
            for j, (cx, cy) in enumerate(chips):
                landed = out_ref.at[2 * cx + cy, _half_rows(out_ref.shape[1], c)]
                _remote(landed, landed, send_sems, recv_sems, 6 * a + j, (cx, cy, c)).wait_recv()
        for cp in d2d(out_refs, send_sems, recv_sems, lambda c_: c_):
            cp.start()

    def finish(in_refs, out_refs, send_sems, recv_sems):
        for cp in d2d(out_refs, send_sems, recv_sems, lambda c_: 1 - c_):
            cp.wait_recv()
        for cp in ici(in_refs, out_refs, send_sems, recv_sems) + d2d(out_refs, send_sems, recv_sems, lambda c_: c_):
            cp.wait_send()

    shapes = [jax.ShapeDtypeStruct((N_CHIPS,) + s_.shape, s_.dtype) for s_ in shards]
    return _Exchange(shards, shapes, 6 * len(shards), begin, relay, finish)


def _swap_exchange(g4s):
    def copies(in_refs, out_refs, send_sems, recv_sems):
        x, y, c, _ = _place()
        return [_remote(in_ref.at[:, _half_rows(in_ref.shape[1], 1 - c), :], out_ref, send_sems, recv_sems, a, (x, y, 1 - c))
                for a, (in_ref, out_ref) in enumerate(zip(in_refs, out_refs))]

    def begin(*parts):
        for cp in copies(*parts):
            cp.start()

    def finish(*parts):
        for cp in copies(*parts):
            cp.wait()

    shapes = [jax.ShapeDtypeStruct((g.shape[0], g.shape[1] // 2, g.shape[2]), g.dtype) for g in g4s]
    return _Exchange(g4s, shapes, len(g4s), begin, None, finish)


def _add_half(name, g4, r1, cvec, tr):
    n, r, w = g4.shape
    rh = r // 2
    nblk = rh // tr

    def body(c_ref, a_ref, b_ref, o_ref):
        o_ref[...] = (a_ref[...].astype(F32) + b_ref[...].astype(F32)).astype(BF16)

    return pl.pallas_call(
        body, name=name,
        grid_spec=pltpu.PrefetchScalarGridSpec(
            num_scalar_prefetch=1, grid=(n, nblk),
            in_specs=[pl.BlockSpec((None, tr, w), lambda k, i, c_ref: (k, c_ref[0] * nblk + i, 0)),
                      pl.BlockSpec((None, tr, w), lambda k, i, c_ref: (k, i, 0))],
            out_specs=pl.BlockSpec((None, tr, w), lambda k, i, c_ref: (k, i, 0))),
        out_shape=jax.ShapeDtypeStruct((n, rh, w), BF16),
        compiler_params=_params("parallel", "parallel"),
    )(cvec, g4, r1)


def _scatter_exchange(h4s):
    def sends(in_refs, out_refs, send_sems, recv_sems):
        x, y, c, chips = _place()
        return [_remote(in_ref.at[2 * cx + cy], out_ref.at[j], send_sems, recv_sems, 3 * a + j, (cx, cy, c))
                for a, (in_ref, out_ref) in enumerate(zip(in_refs, out_refs)) for j, (cx, cy) in enumerate(chips)]

    def begin(*parts):
        for cp in sends(*parts):
            cp.start()

    def finish(in_refs, out_refs, send_sems, recv_sems):
        x, y, c, chips = _place()
        for a, out_ref in enumerate(out_refs):
            for j, (cx, cy) in enumerate(chips):
                got = out_ref.at[j]
                _remote(got, got, send_sems, recv_sems, 3 * a + j, (cx, cy, c)).wait_recv()
        for cp in sends(in_refs, out_refs, send_sems, recv_sems):
            cp.wait_send()

    shapes = [jax.ShapeDtypeStruct((3,) + h.shape[1:], h.dtype) for h in h4s]
    return _Exchange(h4s, shapes, 3 * len(h4s), begin, None, finish)


def _sum_chips(name, h4, r3, mvec, tr):
    _, rh, w = h4.shape

    def body(m_ref, a_ref, b_ref, c_ref, d_ref, o_ref):
        o_ref[...] = ((a_ref[...].astype(F32) + b_ref[...].astype(F32)) + c_ref[...].astype(F32)) + d_ref[...].astype(F32)

    specs = [pl.BlockSpec((None, tr, w), lambda i, m_ref: (m_ref[0], i, 0))]
    specs += [pl.BlockSpec((None, tr, w), functools.partial(lambda k, i, m_ref: (k, i, 0), k)) for k in range(3)]
    return pl.pallas_call(
        body, name=name,
        grid_spec=pltpu.PrefetchScalarGridSpec(
            num_scalar_prefetch=1, grid=(rh // tr,), in_specs=specs,
            out_specs=pl.BlockSpec((tr, w), lambda i, m_ref: (i, 0))),
        out_shape=jax.ShapeDtypeStruct((rh, w), F32),
        compiler_params=_params("parallel"),
    )(mvec, h4, r3, r3, r3)


def _swap_reduced_exchange(ghs):
    def copies(in_refs, out_refs, send_sems, recv_sems):
        x, y, c, _ = _place()
        return [_remote(in_ref, out_ref, send_sems, recv_sems, a, (x, y, 1 - c))
                for a, (in_ref, out_ref) in enumerate(zip(in_refs, out_refs))]

    def begin(*parts):
        for cp in copies(*parts):
            cp.start()

    def finish(*parts):
        for cp in copies(*parts):
            cp.wait()

    return _Exchange(ghs, [jax.ShapeDtypeStruct(g.shape, g.dtype) for g in ghs], len(ghs), begin, None, finish)


class _SemaphoresFrom:
    def __init__(self, sems, first):
        self.sems, self.first = sems, first

    @property
    def at(self):
        return self

    def __getitem__(self, k):
        return self.sems.at[self.first + k]


def _both(a, b):
    def phase(fa, fb):
        if fa is None and fb is None:
            return None

        def run(in_refs, out_refs, send_sems, recv_sems):
            if fa is not None:
                fa(in_refs[:a.n], out_refs[:a.n], send_sems, recv_sems)
            if fb is not None:
                fb(in_refs[a.n:], out_refs[a.n:], _SemaphoresFrom(send_sems, a.n_sems), _SemaphoresFrom(recv_sems, a.n_sems))

        return run

    return _Exchange(a.inputs + b.inputs, a.out_shapes + b.out_shapes, a.n_sems + b.n_sems,
                     phase(a.begin, b.begin), phase(a.relay, b.relay), phase(a.finish, b.finish))


def _small_update(name, partials, weights, moments1, moments2):
    n = len(partials)
    width = max(p.shape[1] for p in partials)
    starts, at = [], 0
    for p in partials:
        starts.append(at)
        at += p.shape[0]
    rows = -(-at // 8) * 8
    has_w = [w is not None for w in weights]
    n_w = sum(has_w)

    def body(*refs):
        p_refs = refs[:n]
        w_refs, m_refs, v_refs = refs[n:n + n_w], refs[n + n_w:n + 2 * n_w], refs[n + 2 * n_w:n + 3 * n_w]
        outs = refs[n + 3 * n_w:-4]
        g_refs, upd_refs = outs[:n], outs[n:]
        vec, buf, send_sems, recv_sems = refs[-4:]
        x, y, c, _ = _place()
        me = 4 * x + 2 * y + c
        vec[...] = jnp.zeros_like(vec)
        for p_ref, r0 in zip(p_refs, starts):
            vec[r0:r0 + p_ref.shape[0], 0:p_ref.shape[1]] = p_ref[...]
        buf[me] = vec[...]
        flips = [(fx, fy, fc) for fx in (0, 1) for fy in (0, 1) for fc in (0, 1)][1:]
        peers = [(x + fx - 2 * x * fx, y + fy - 2 * y * fy, c + fc - 2 * c * fc) for fx, fy, fc in flips]
        sends = [_remote(vec, buf.at[me], send_sems, recv_sems, k, peer) for k, peer in enumerate(peers)]
        for cp in sends:
            cp.start()
        for k, (px, py, pc) in enumerate(peers):
            got = buf.at[4 * px + 2 * py + pc]
            _remote(got, got, send_sems, recv_sems, k, (px, py, pc)).wait_recv()
        for cp in sends:
            cp.wait_send()
        total = buf[0]
        for dev in range(1, N_DEV):
            total = total + buf[dev]
        k = 0
        for a in range(n):
            r, w = g_refs[a].shape
            g = total[starts[a]:starts[a] + r, 0:w]
            g_refs[a][...] = g
            if has_w[a]:
                m2 = ADAM_B1 * m_refs[k][...] + (1.0 - ADAM_B1) * g
                v2 = ADAM_B2 * v_refs[k][...] + (1.0 - ADAM_B2) * (g * g)
                m_hat = m2 / (1.0 - ADAM_B1 ** ADAM_STEP)
                v_hat = v2 / (1.0 - ADAM_B2 ** ADAM_STEP)
                upd_refs[3 * k][...] = -ADAM_LR * (m_hat / (jnp.sqrt(v_hat) + ADAM_EPS) + ADAM_WD * w_refs[k][...])
                upd_refs[3 * k + 1][...] = m2
                upd_refs[3 * k + 2][...] = v2
                k += 1

    ws = [w for w in weights if w is not None]
    g_shapes = [jax.ShapeDtypeStruct(p.shape if w is None else w.shape, F32) for p, w in zip(partials, weights)]
    u_shapes = [jax.ShapeDtypeStruct(w.shape, F32) for w in ws for _ in range(3)]
    vm = pl.BlockSpec(memory_space=pltpu.VMEM)
    n_args = n + 3 * n_w
    outs = pl.pallas_call(
        body, name=name, in_specs=[vm] * n_args, out_specs=[vm] * (n + 3 * n_w), out_shape=g_shapes + u_shapes,
        scratch_shapes=[pltpu.VMEM((rows, width), F32), pltpu.VMEM((N_DEV, rows, width), F32),
                        pltpu.SemaphoreType.DMA((7,)), pltpu.SemaphoreType.DMA((7,))],
    )(*partials, *ws, *[m for m in moments1 if m is not None], *[v for v in moments2 if v is not None])
    return outs[:n], outs[n:]


GATE_COL = 3 * SB_WIDTH + 3 * FOX_WIDTH + FOX_HEADS + MEM_WIDTH
FL_COL = QKV_WIDTH


GROUP_A_COLS = [(0, QKV_WIDTH), (FL_COL + FOX_HEADS, MEM_WIDTH)]
GROUP_B_COLS = [(GATE_COL, MIX_WIDTH), (FL_COL, FOX_HEADS)]


def _group_from_shards(shard_of, cw, spans, pad):
    parts = []
    for lo, width in spans:
        hi = lo + width
        for j in range(N_CHIPS):
            a, b = max(lo, j * cw), min(hi, (j + 1) * cw)
            if a < b:
                parts.append(shard_of(j)[:, a - j * cw:b - j * cw])
    if pad:
        parts.append(jnp.zeros((parts[0].shape[0], pad), parts[0].dtype))
    return jnp.concatenate(parts, axis=1)


def _shard_from_groups(ga, gb, j, cw):
    lo, hi = j * cw, (j + 1) * cw
    placed = []
    for grp, spans in ((ga, GROUP_A_COLS), (gb, GROUP_B_COLS)):
        at = 0
        for first, width in spans:
            a, b = max(lo, first), min(hi, first + width)
            if a < b:
                placed.append((a, grp[:, at + a - first:at + b - first]))
            at += width
    return jnp.concatenate([p for _, p in sorted(placed, key=lambda t: t[0])], axis=1)


def _tile_of(n, cap, unit):
    if n <= cap:
        return n
    best = None
    for t in range(unit, cap + 1, unit):
        if n % t == 0:
            best = t
    assert best is not None, (n, cap, unit)
    return best


def _column_major_rows(a):
    dp, r, c = a.shape
    return a.transpose(2, 0, 1).reshape(c, dp, r // LANES, LANES).transpose(0, 2, 1, 3).reshape(-1, 8, LANES)


def _from_column_major_rows(b, shape):
    dp, r, c = shape
    return b.reshape(c, r // LANES, dp, LANES).transpose(0, 2, 1, 3).reshape(c, dp, r).transpose(1, 2, 0)


def kernel(x, mem, norm_w, w_in, b_forget, mem_norm_w, w_mem_kv, out_norm_w, w_out, final_norm_w, loss_target, m_norm_w, m_w_in, m_b_forget, m_mem_norm_w, m_w_mem_kv, m_out_norm_w, m_w_out, m_final_norm_w, v_norm_w, v_w_in, v_b_forget, v_mem_norm_w, v_w_mem_kv, v_out_norm_w, v_w_out, v_final_norm_w):
    xs = x[0]
    mems = mem[0]
    target = loss_target[0]
    s, d = xs.shape
    depth = norm_w.shape[0]
    nb = s // TILE
    ts = _tile_of(s, 512, 8)
    big = (w_in, w_mem_kv, w_out)
    core = lax.axis_index("c")
    chip = 2 * lax.axis_index("x") + lax.axis_index("y")
    cvec = core.astype(jnp.int32).reshape(1)
    mvec = chip.astype(jnp.int32).reshape(1)
    cw = w_in.shape[2]

    own_w = [[a[l].astype(BF16) for a in big] for l in range(depth)]

    def lay_out_in(own, got):
        shard_of = lambda j: jnp.where(chip == j, own, got[j])
        return (_group_from_shards(shard_of, cw, GROUP_A_COLS, 0),
                _group_from_shards(shard_of, cw, GROUP_B_COLS, LANES - FOX_HEADS))

    def lay_out_rows(own, got):
        full = jnp.where(lax.broadcasted_iota(jnp.int32, got.shape, 0) == chip, own[None], got)
        return full.reshape(-1, full.shape[2])

    w_in_groups = [lay_out_in(own_w[0][0], _run_exchange("gather_weights0", _gather_exchange(own_w[0][:1]))[0])]
    layer_w = []

    tm = _tile_of(s, 256, 8)
    fl_block = MIX_WIDTH // LANES

    saved = []
    cur = xs
    for l in range(depth):
        wa, wb = w_in_groups[l]
        h = _rms_fwd(f"rms_fwd{l}", cur, norm_w[l][None], ts)
        pa = _mm(f"inproj_a{l}", h, wa, "nn", tm, _tile_of(PA, 1664, LANES), BF16)
        pb = _mm(f"inproj_b{l}", h, wb, "nn", tm, PB, F32)
        bpad = jnp.pad(b_forget[l], (0, LANES - FOX_HEADS))[None]
        ccol4, crow4 = _gate_fwd(f"gate_fwd{l}", pb, bpad, fl_block)
        more = l + 1 < depth
        o_sb, got = _sb_fwd(f"sb_fwd{l}", pa, 0, carried=_gather_exchange(own_w[l][1:]))
        wkv, wout = lay_out_rows(own_w[l][1], got[0]), lay_out_rows(own_w[l][2], got[1])
        layer_w.append((wa, wb, wkv, wout))
        o_fx, lse_fx, got = _fox_fwd(f"fox_fwd{l}", pa, 3 * SB_WIDTH, ccol4, crow4,
                                     carried=_gather_exchange(own_w[l + 1][:1]) if more else None)
        if more:
            w_in_groups.append(lay_out_in(own_w[l + 1][0], got[0]))
        mn = _rms_fwd(f"mem_rms{l}", mems, mem_norm_w[l][None], mems.shape[0])
        mkv = _mm(f"mem_kv{l}", mn, wkv, "nn", mems.shape[0], 2 * MEM_WIDTH, BF16)
        o_m, lse_m = _mem_fwd(f"mem_fwd{l}", pa, mkv)
        nxt, y2 = _out_fwd(f"out_fwd{l}", o_sb, o_fx, o_m, pb, out_norm_w[l][None], cur, wout, ts)
        saved.append((cur, h, pa, pb, bpad, ccol4, crow4, o_sb, o_fx, lse_fx, mn, mkv, o_m, lse_m, y2))
        cur = nxt

    loss_v, dx, dxb, g_final = _final_loss("final_loss", cur, final_norm_w[None], target, ts)

    g_norm, g_b, g_memnorm, g_outnorm = [None] * depth, [None] * depth, [None] * depth, [None] * depth
    g_wa, g_wb, g_wkv, g_wout = [None] * depth, [None] * depth, [None] * depth, [None] * depth
    g_own = [[None] * depth for _ in big]
    g_other = [[None] * depth for _ in big]

    def swap_of(jobs):
        return _swap_exchange([g for _, _, g, _ in jobs])

    def chip_sums(jobs, got):
        return [(lr, k, _add_half(f"grad_add_half{lr}_{k}", g, r_, cvec, t_), t_) for (lr, k, g, t_), r_ in zip(jobs, got)]

    def sum_at_owner(jobs, from_chips):
        return [_sum_chips(f"grad_sum_chips{lr}_{k}", h_, r_, mvec, t_) for (lr, k, h_, t_), r_ in zip(jobs, from_chips)]

    def keep(jobs, halves, others):
        for (lr, k, _, _), mine, other in zip(jobs, halves, others):
            g_own[k][lr], g_other[k][lr] = mine, other

    def job(lr, k, g4):
        return lr, k, g4, _tile_of(g4.shape[1] // 2, 256, 16)

    pending = []
    for l in reversed(range(depth)):
        xin, h, pa, pb, bpad, ccol4, crow4, o_sb, o_fx, lse_fx, mn, mkv, o_m, lse_m, y2 = saved[l]
        wa, wb, wkv, wout = layer_w[l]
        dy, dgate, g_outnorm[l] = _out_bwd(f"out_bwd{l}", dxb, o_sb, o_fx, o_m, pb, out_norm_w[l][None], wout, ts)
        g_wout[l] = _mm(f"dw_out{l}", y2, dxb, "tn", _tile_of(MIX_WIDTH, 640, LANES), d, F32)
        dq_m, dmkv = _mem_bwd(f"mem_bwd{l}", pa, mkv, o_m, lse_m, dy, SB_WIDTH + FOX_WIDTH)
        g_wkv[l] = _mm(f"dw_kv{l}", mn, dmkv, "tn", d, 2 * MEM_WIDTH, F32)
        dmn = _mm(f"dmem{l}", dmkv, wkv, "nt", mems.shape[0], d, F32)
        g_memnorm[l] = _rms_wgrad(f"mem_norm_grad{l}", mems, dmn)
        small = [job(l, 1, g_wkv[l].reshape(N_CHIPS, -1, g_wkv[l].shape[1])), job(l, 2, g_wout[l].reshape(N_CHIPS, -1, d))]
        (dq_fx, dk_fx, dv_fx, cs4), got = _fox_bwd(f"fox_bwd{l}", pa, 3 * SB_WIDTH, ccol4, crow4, o_fx, lse_fx, dy,
                                                    SB_WIDTH, carried=swap_of(small))
        pending += chip_sums(small, got)
        (dq_sb, dk_sb, dv_sb), from_chips = _sb_bwd(f"sb_bwd{l}", pa, 0, dy, 0,
                                                   carried=_scatter_exchange([j[2] for j in pending]))
        reduced_jobs, reduced = pending, sum_at_owner(pending, from_chips)
        swap_back = _swap_reduced_exchange(reduced)
        dpb, g_b[l] = _gate_bwd(f"gate_bwd{l}", pb, bpad, cs4, fl_block, dgate)
        dpa = jnp.concatenate([dq_sb, dk_sb, dv_sb, dq_fx, dk_fx, dv_fx, dq_m], axis=1)
        tw = _tile_of(d, 512, LANES)
        g_wa[l] = _mm(f"dw_in_a{l}", h, dpa, "tn", tw, _tile_of(PA, 1664, LANES), BF16)
        g_wb[l] = _mm(f"dw_in_b{l}", h, dpb, "tn", tw, PB, BF16)
        g4_in = jnp.stack([_shard_from_groups(g_wa[l], g_wb[l], j, cw) for j in range(N_CHIPS)])
        w_in_job = [job(l, 0, g4_in)]
        if l > 0:
            dx, dxb, g_norm[l], got = _inproj_bwd(f"inproj_bwd{l}", dpa, dpb, wa, wb, xin, norm_w[l][None], dx, tm,
                                                  carried=_both(swap_of(w_in_job), swap_back))
            pending = chip_sums(w_in_job, got[:1])
            keep(reduced_jobs, reduced, got[1:])
        else:
            pending = chip_sums(w_in_job, _run_exchange("grad_swap_halves_last", swap_of(w_in_job)))
            dx, dxb, g_norm[l], got = _inproj_bwd(f"inproj_bwd{l}", dpa, dpb, wa, wb, xin, norm_w[l][None], dx, tm,
                                                  carried=_both(_scatter_exchange([j[2] for j in pending]), swap_back))
            keep(reduced_jobs, reduced, got[1:])
            last = sum_at_owner(pending, got[:1])
            keep(pending, last, _run_exchange("grad_swap_reduced_last", _swap_reduced_exchange(last)))

    small_w = [norm_w, b_forget, mem_norm_w, out_norm_w, final_norm_w]
    small_m = [m_norm_w, m_b_forget, m_mem_norm_w, m_out_norm_w, m_final_norm_w]
    small_v = [v_norm_w, v_b_forget, v_mem_norm_w, v_out_norm_w, v_final_norm_w]
    rows2 = lambda a: a.reshape(-1, a.shape[-1])
    partials = [jnp.concatenate(g_norm, axis=0), jnp.concatenate(g_b, axis=0), jnp.concatenate(g_memnorm, axis=0),
                jnp.concatenate(g_outnorm, axis=0), g_final, loss_v]
    sums, updates = _small_update("small_update", partials, [rows2(a) for a in small_w] + [None],
                                  [rows2(a) for a in small_m] + [None], [rows2(a) for a in small_v] + [None])
    small_grads = [g.reshape(a.shape) for g, a in zip(sums, small_w)]
    loss = sums[-1][0, 0]
    small_delta, small_m2, small_v2 = ([updates[3 * k + t].reshape(a.shape) for k, a in enumerate(small_w)]
                                       for t in range(3))
    big_grads, big_delta, big_m2, big_v2 = [], [], [], []
    for k, (nm, w_, m_, v_) in enumerate(zip(("w_in", "w_mem_kv", "w_out"), big, (m_w_in, m_w_mem_kv, m_w_out),
                                             (v_w_in, v_w_mem_kv, v_w_out))):
        if w_.shape[2] % LANES:
            g_full = jnp.stack([jnp.concatenate([jnp.where(core == 0, go, gt), jnp.where(core == 0, gt, go)], axis=0)
                                for go, gt in zip(g_own[k], g_other[k])])
            w_p, g_p, m_p, v_p = (_column_major_rows(a) for a in (w_, g_full, m_, v_))
            outs = _adamw(f"adamw_{nm}", w_p, g_p, m_p, v_p, _tile_of(w_p.shape[0], 600, 1))
            outs = [_from_column_major_rows(o, w_.shape) for o in (g_p, *outs)]
        else:
            outs = _adamw_sharded(f"adamw_{nm}", w_, m_, v_, g_own[k], g_other[k], cvec,
                                  _tile_of(w_.shape[1] // 2, 256, 8))
        for lst, o in zip((big_grads, big_delta, big_m2, big_v2), outs):
            lst.append(o)

    def order(sm, bg):
        return [sm[0], bg[0], sm[1], sm[2], bg[1], sm[3], bg[2], sm[4]]

    return (loss, dx[None], *order(small_grads, big_grads), *order(small_delta, big_delta),
            *order(small_m2, big_m2), *order(small_v2, big_v2))
```

```python
import functools

import jax
import jax.numpy as jnp
from jax import lax
from jax.experimental import pallas as pl
from jax.experimental.pallas import tpu as pltpu

F32 = jnp.float32
BF16 = jnp.bfloat16

HEAD_DIM = 64
SB_WIDTH = 512
FOX_WIDTH = 512
FOX_HEADS = 8
MEM_WIDTH = 256
MEM_HEADS = MEM_WIDTH // HEAD_DIM
MIX_WIDTH = SB_WIDTH + FOX_WIDTH + MEM_WIDTH
TOTAL_HEADS = MIX_WIDTH // HEAD_DIM
IN_WIDTH = 3 * SB_WIDTH + 3 * FOX_WIDTH + FOX_HEADS + MEM_WIDTH + MIX_WIDTH
LANES = 128
QKV_WIDTH = 3 * SB_WIDTH + 3 * FOX_WIDTH
PA = QKV_WIDTH + MEM_WIDTH
PB = LANES + MIX_WIDTH
EPS = 1e-6
SCALE = HEAD_DIM ** -0.5
TILE = 256
SB_GROUP = 4
SB_LANES = SB_GROUP * HEAD_DIM
SB_FWD_GROUP = 8
FOX_GROUP = 4
FOX_LANES = FOX_GROUP * HEAD_DIM
FOX_FWD_GROUP = 4
NEG_INF = float("-inf")
MASKED = -1e30

ADAM_LR = 0.001
ADAM_B1 = 0.9
ADAM_B2 = 0.999
ADAM_EPS = 1e-08
ADAM_WD = 0.01
ADAM_STEP = 10

N_CHIPS = 4
N_DEV = 8
VMEM_LIMIT = 48 * 1024 * 1024
MESH = pl.DeviceIdType.MESH


def _params(*sem):
    return pltpu.CompilerParams(dimension_semantics=tuple(sem), vmem_limit_bytes=VMEM_LIMIT)


def _dot(a, b):
    return jnp.dot(a, b, preferred_element_type=F32)


def _dot_nt(a, b):
    return lax.dot_general(a, b, (((1,), (1,)), ((), ())), preferred_element_type=F32)


def _dot_tn(a, b):
    return lax.dot_general(a, b, (((0,), (0,)), ((), ())), preferred_element_type=F32)


def _split2(x):
    hi = x.astype(BF16)
    lo = (x - hi.astype(F32)).astype(BF16)
    return hi, lo


def _split3(x):
    hi = x.astype(BF16)
    r = x - hi.astype(F32)
    mid = r.astype(BF16)
    lo = (r - mid.astype(F32)).astype(BF16)
    return hi, mid, lo


def _sum_l2(x, u):
    hi, lo = _split2(x)
    return _dot(hi, u) + _dot(lo, u)


def _sum_l3(x, u):
    hi, mid, lo = _split3(x)
    return _dot(hi, u) + _dot(mid, u) + _dot(lo, u)


def _sum_r3(u, x):
    hi, mid, lo = _split3(x)
    return _dot(u, hi) + _dot(u, mid) + _dot(u, lo)


def _tri(n, pred):
    r = lax.broadcasted_iota(jnp.int32, (n, n), 0)
    c = lax.broadcasted_iota(jnp.int32, (n, n), 1)
    return jnp.where(pred(r, c), 1.0, 0.0).astype(BF16)


def _rows(ref, j, n=TILE):
    return pl.ds(pl.multiple_of(j * n, n), n)


def _mm(name, a, b, mode, tm, tn, out_dtype, res=None, a_lead=(), b_lead=()):
    a2, b2 = a.shape[len(a_lead):], b.shape[len(b_lead):]
    if mode == "tn":
        k, m = a2
    else:
        m, k = a2
    n = b2[0] if mode == "nt" else b2[1]
    assert m % tm == 0 and n % tn == 0, (name, m, tm, n, tn)
    na, nb = (None,) * len(a_lead), (None,) * len(b_lead)
    if mode == "tn":
        a_spec = pl.BlockSpec(na + (k, tm), lambda j, i: a_lead + (0, i))
    else:
        a_spec = pl.BlockSpec(na + (tm, k), lambda j, i: a_lead + (i, 0))
    if mode == "nt":
        b_spec = pl.BlockSpec(nb + (tn, k), lambda j, i: b_lead + (j, 0))
    else:
        b_spec = pl.BlockSpec(nb + (k, tn), lambda j, i: b_lead + (0, j))
    o_spec = pl.BlockSpec((tm, tn), lambda j, i: (i, j))
    dot = {"nn": _dot, "nt": _dot_nt, "tn": _dot_tn}[mode]

    def body(a_ref, b_ref, *rest):
        o_ref = rest[-1]
        acc = dot(a_ref[...].astype(BF16), b_ref[...].astype(BF16))
        if res is not None:
            acc = acc + rest[0][...]
        o_ref[...] = acc.astype(o_ref.dtype)

    args, specs = [a, b], [a_spec, b_spec]
    if res is not None:
        args.append(res)
        specs.append(o_spec)
    return pl.pallas_call(
        body, name=name, grid=(n // tn, m // tm), in_specs=specs, out_specs=o_spec,
        out_shape=jax.ShapeDtypeStruct((m, n), out_dtype),
        compiler_params=_params("parallel", "parallel"),
    )(*args)


def _rms_fwd(name, x, g, ts):
    s, d = x.shape

    def body(x_ref, g_ref, o_ref):
        xf = x_ref[...]
        r = lax.rsqrt(jnp.mean(xf * xf, axis=1, keepdims=True) + EPS)
        o_ref[...] = (xf * r * g_ref[...]).astype(BF16)

    return pl.pallas_call(
        body, name=name, grid=(s // ts,),
        in_specs=[pl.BlockSpec((ts, d), lambda i: (i, 0)), pl.BlockSpec((1, d), lambda i: (0, 0))],
        out_specs=pl.BlockSpec((ts, d), lambda i: (i, 0)),
        out_shape=jax.ShapeDtypeStruct((s, d), BF16),
        compiler_params=_params("parallel"),
    )(x, g)


def _inproj_bwd(name, dpa, dpb, wa, wb, x, g, dres, ts, carried=None):
    s, d = x.shape

    def body(dpa_ref, dpb_ref, wa_ref, wb_ref, x_ref, g_ref, dres_ref, dx_ref, dxb_ref, dg_ref):
        @pl.when(pl.program_id(1) == 0)
        def _():
            dg_ref[...] = jnp.zeros_like(dg_ref)

        dhf = _dot_nt(dpa_ref[...], wa_ref[...]) + _dot_nt(dpb_ref[...], wb_ref[...])
        xf = x_ref[...]
        r = lax.rsqrt(jnp.mean(xf * xf, axis=1, keepdims=True) + EPS)
        xh = xf * r
        dg_ref[...] += jnp.sum(dhf * xh, axis=0, keepdims=True)
        dxh = dhf * g_ref[...]
        m = jnp.mean(dxh * xh, axis=1, keepdims=True)
        dx = r * (dxh - xh * m) + dres_ref[...]
        dx_ref[...] = dx
        dxb_ref[...] = dx.astype(BF16)

    row = lambda w: pl.BlockSpec((ts, w), lambda p, i: (i, 0))
    whole = lambda a: pl.BlockSpec(a.shape, lambda p, i: (0, 0))
    outs = _pair_grid_call(
        name, body, s // ts,
        in_specs=[row(dpa.shape[1]), row(dpb.shape[1]), whole(wa), whole(wb), row(d), whole(g), row(d)],
        out_specs=[row(d), row(d), pl.BlockSpec((1, d), lambda p, i: (0, 0))],
        out_shape=[jax.ShapeDtypeStruct((s, d), F32), jax.ShapeDtypeStruct((s, d), BF16),
                   jax.ShapeDtypeStruct((1, d), F32)],
        scratch=[], args=(dpa, dpb, wa, wb, x, g, dres), carried=carried, groups=1)
    return outs[0], outs[1], outs[2], outs[3:]


def _rms_wgrad(name, x, dh):
    m_, d = x.shape

    def body(x_ref, dh_ref, dg_ref):
        xf = x_ref[...]
        r = lax.rsqrt(jnp.mean(xf * xf, axis=1, keepdims=True) + EPS)
        dg_ref[...] = jnp.sum(dh_ref[...] * xf * r, axis=0, keepdims=True)

    return pl.pallas_call(
        body, name=name, out_shape=jax.ShapeDtypeStruct((1, d), F32),
    )(x, dh)


def _final_loss(name, x, g, target, ts):
    s, d = x.shape

    def body(x_ref, g_ref, t_ref, loss_ref, dx_ref, dxb_ref, dg_ref):
        @pl.when(pl.program_id(0) == 0)
        def _():
            dg_ref[...] = jnp.zeros_like(dg_ref)
            loss_ref[...] = jnp.zeros_like(loss_ref)

        xf = x_ref[...]
        gw = g_ref[...]
        r = lax.rsqrt(jnp.mean(xf * xf, axis=1, keepdims=True) + EPS)
        xh = xf * r
        e = xh * gw - t_ref[...]
        part = 0.5 * jnp.sum(jnp.mean(e * e, axis=1, keepdims=True), axis=0, keepdims=True)
        loss_ref[...] += jnp.broadcast_to(part, loss_ref.shape)
        dy = e * (1.0 / d)
        dg_ref[...] += jnp.sum(dy * xh, axis=0, keepdims=True)
        dxh = dy * gw
        m = jnp.mean(dxh * xh, axis=1, keepdims=True)
        dx = r * (dxh - xh * m)
        dx_ref[...] = dx
        dxb_ref[...] = dx.astype(BF16)

    row = pl.BlockSpec((ts, d), lambda i: (i, 0))
    vec = pl.BlockSpec((1, d), lambda i: (0, 0))
    lvec = pl.BlockSpec((1, LANES), lambda i: (0, 0))
    return pl.pallas_call(
        body, name=name, grid=(s // ts,), in_specs=[row, vec, row], out_specs=[lvec, row, row, vec],
        out_shape=[jax.ShapeDtypeStruct((1, LANES), F32), jax.ShapeDtypeStruct((s, d), F32),
                   jax.ShapeDtypeStruct((s, d), BF16), jax.ShapeDtypeStruct((1, d), F32)],
        compiler_params=_params("arbitrary"),
    )(x, g, target)


def _gate_fwd(name, pb, bpad, fl_block):
    s = pb.shape[0]
    nb = s // TILE

    def body(fl_ref, b_ref, ccol_ref, crow_ref, carry):
        @pl.when(pl.program_id(0) == 0)
        def _():
            carry[...] = jnp.zeros_like(carry)

        u = fl_ref[...] + b_ref[...]
        lf = jnp.minimum(u, 0.0) - jnp.log1p(jnp.exp(-jnp.abs(u)))
        lower = _tri(TILE, lambda r, c: c <= r)
        c = _sum_r3(lower, lf) + carry[0:1, :]
        ccol_ref[...] = jnp.concatenate(
            [jnp.broadcast_to(c[:, hh:hh + 1], (TILE, HEAD_DIM)) for hh in range(FOX_HEADS)], axis=1)
        crow_ref[0] = c.T[0:8, :]
        carry[...] = jnp.broadcast_to(c[TILE - 1:TILE, :], carry.shape)

    return pl.pallas_call(
        body, name=name, grid=(nb,),
        in_specs=[pl.BlockSpec((TILE, LANES), lambda i: (i, fl_block)), pl.BlockSpec((1, LANES), lambda i: (0, 0))],
        out_specs=[pl.BlockSpec((TILE, FOX_WIDTH), lambda i: (i, 0)), pl.BlockSpec((1, 8, TILE), lambda i: (i, 0, 0))],
        out_shape=[jax.ShapeDtypeStruct((s, FOX_WIDTH), F32), jax.ShapeDtypeStruct((nb, 8, TILE), F32)],
        scratch_shapes=[pltpu.VMEM((8, LANES), F32)],
        compiler_params=_params("arbitrary"),
    )(pb, bpad)


def _gate_bwd(name, pb, bpad, colsum, fl_block, dpb):
    s = pb.shape[0]
    nb = s // TILE

    def body(fl_ref, b_ref, cs_ref, dpb_ref, dl_ref, db_ref, carry):
        @pl.when(pl.program_id(0) == 0)
        def _():
            carry[...] = jnp.zeros_like(carry)
            db_ref[...] = jnp.zeros_like(db_ref)

        upper = _tri(TILE, lambda r, c: r >= c)
        rsum = _sum_l3(cs_ref[0], upper) + carry[:, 0:1]
        carry[...] = jnp.broadcast_to(rsum[:, 0:1], carry.shape)
        full = jnp.concatenate([rsum, jnp.zeros((LANES - 8, TILE), F32)], axis=0)
        dlf = -full.T
        u = fl_ref[...] + b_ref[...]
        dlogit = dlf * (1.0 - jax.nn.sigmoid(u))
        dl_ref[...] = dlogit.astype(BF16)
        db_ref[...] += jnp.sum(dlogit, axis=0, keepdims=True)

    logits_block = pl.BlockSpec((TILE, LANES), lambda i: (nb - 1 - i, fl_block))
    return pl.pallas_call(
        body, name=name, grid=(nb,),
        in_specs=[logits_block, pl.BlockSpec((1, LANES), lambda i: (0, 0)),
                  pl.BlockSpec((1, 8, TILE), lambda i: (nb - 1 - i, 0, 0)), pl.BlockSpec(memory_space=pl.ANY)],
        out_specs=[logits_block, pl.BlockSpec((1, LANES), lambda i: (0, 0))],
        out_shape=[jax.ShapeDtypeStruct(dpb.shape, BF16), jax.ShapeDtypeStruct((1, LANES), F32)],
        scratch_shapes=[pltpu.VMEM((8, LANES), F32)], input_output_aliases={3: 0},
        compiler_params=_params("arbitrary"),
    )(pb, bpad, colsum, dpb)


def _head_slices(hh):
    return slice(HEAD_DIM * hh, HEAD_DIM * (hh + 1))


def _scaled_q(q_ref, sl, scale=SCALE):
    return (q_ref[:, sl].astype(F32) * scale).astype(BF16)


def _neg_abs(x):
    sign = jnp.uint32(0x80000000)
    return lax.bitcast_convert_type(lax.bitcast_convert_type(x, jnp.uint32) | sign, F32)


def _pair_grid_call(name, body, nb, in_specs, out_specs, out_shape, scratch, args, carried=None, groups=4):
    if carried is None:
        return pl.pallas_call(
            body, name=name, grid=(groups, nb), in_specs=in_specs, out_specs=out_specs, out_shape=out_shape,
            scratch_shapes=scratch, compiler_params=_params("arbitrary", "arbitrary"),
        )(*args)
    n_in, n_out, n_ex = len(in_specs), len(out_specs), carried.n

    def body_with_copies(*refs):
        own_in, ex_in = refs[:n_in], refs[n_in:n_in + n_ex]
        own_out = refs[n_in + n_ex:n_in + n_ex + n_out]
        ex_out = refs[n_in + n_ex + n_out:n_in + 2 * n_ex + n_out]
        own_scratch, sems = refs[n_in + 2 * n_ex + n_out:-2], refs[-2:]
        parts = (ex_in, ex_out, sems[0], sems[1])
        p, i = pl.program_id(0), pl.program_id(1)
        pl.when(jnp.logical_and(p == 0, i == 0))(lambda: carried.begin(*parts))
        if carried.relay is not None:
            pl.when(jnp.logical_and(p == groups - 1, i == max(nb - 2, 0)))(lambda: carried.relay(*parts))
        body(*own_in, *own_out, *own_scratch)
        pl.when(jnp.logical_and(p == groups - 1, i == nb - 1))(lambda: carried.finish(*parts))

    return pl.pallas_call(
        body_with_copies, name=name, grid=(groups, nb), in_specs=list(in_specs) + [HBM_SPEC] * n_ex,
        out_specs=list(out_specs) + [HBM_SPEC] * n_ex, out_shape=list(out_shape) + carried.out_shapes,
        scratch_shapes=list(scratch) + _dma_sems(carried.n_sems),
        compiler_params=_params("arbitrary", "arbitrary"),
    )(*args, *carried.inputs)


def _sb_fwd(name, pa, col0, carried=None):
    s = pa.shape[0]
    nb = s // TILE
    heads, lanes = SB_FWD_GROUP, SB_FWD_GROUP * HEAD_DIM
    cb = col0 // lanes
    kb = SB_WIDTH // lanes

    def body(q_ref, k_ref, v_ref, o_ref, lsig_s, lf_s):
        i = pl.program_id(1)
        r = lax.broadcasted_iota(jnp.int32, (TILE, TILE), 0)
        c = lax.broadcasted_iota(jnp.int32, (TILE, TILE), 1)
        strict = c < r
        u_after = _tri(TILE, lambda rr, cc: rr > cc)
        qs = [_scaled_q(q_ref, _head_slices(hh), -SCALE) for hh in range(heads)]

        def neg_z(j):
            kblk = k_ref[_rows(k_ref, j), :]
            return [_dot_nt(qs[hh], kblk[:, _head_slices(hh)]) for hh in range(heads)]

        def scores(nzs, slot, diag):
            for hh, nz in enumerate(nzs):
                lf = jnp.minimum(nz, 0.0) - jnp.log(1.0 + jnp.exp(_neg_abs(nz)))
                lsig = lf - nz
                if diag:
                    lf = jnp.where(strict, lf, 0.0)
                    lsig = jnp.where(strict, lsig, MASKED)
                lsig_s[slot, hh] = lsig
                lf_s[slot, hh] = lf.astype(BF16)

        def weigh(j, slot, state):
            vblk = v_ref[_rows(v_ref, j), :]
            new = []
            for hh in range(heads):
                carry, acc = state[hh]
                lfb = lf_s[slot, hh]
                sx = _dot(lfb, u_after)
                a = jnp.exp(lsig_s[slot, hh] + sx + carry)
                new.append((carry + sx[:, 0:1] + lfb[:, 0:1].astype(F32),
                            acc + _dot(a.astype(BF16), vblk[:, _head_slices(hh)])))
            return tuple(new)

        def step(t, state):
            state = weigh(i - t + 1, (t - 1) % 2, state)
            scores(neg_z(i - t), t % 2, False)
            return state

        zero = (jnp.zeros((TILE, 1), F32), jnp.zeros((TILE, HEAD_DIM), F32))
        scores(neg_z(i), 0, True)
        state = lax.fori_loop(1, i + 1, step, (zero,) * heads)
        state = weigh(0, i % 2, state)
        o_ref[...] = jnp.concatenate([st[1] for st in state], axis=1)

    outs = _pair_grid_call(
        name, body, nb,
        in_specs=[pl.BlockSpec((TILE, lanes), lambda p, i: (i, cb + p)),
                  pl.BlockSpec((s, lanes), lambda p, i: (0, cb + kb + p)),
                  pl.BlockSpec((s, lanes), lambda p, i: (0, cb + 2 * kb + p))],
        out_specs=[pl.BlockSpec((TILE, lanes), lambda p, i: (i, p))],
        out_shape=[jax.ShapeDtypeStruct((s, SB_WIDTH), F32)],
        scratch=[pltpu.VMEM((2, heads, TILE, TILE), F32), pltpu.VMEM((2, heads, TILE, TILE), BF16)],
        args=(pa, pa, pa), carried=carried, groups=kb)
    return outs[0], outs[1:]


def _sb_bwd(name, pa, col0, dout, dcol0, carried=None):
    s = pa.shape[0]
    nb = s // TILE
    cb = col0 // SB_LANES
    kb = SB_WIDTH // SB_LANES
    db = dcol0 // SB_LANES

    def body(q_ref, k_ref, v_ref, do_ref, dq_ref, dk_ref, dv_ref, dk_acc, dv_acc, dpan, span, gsum, lsig_s, lf_s):
        i = pl.program_id(1)

        @pl.when(i == 0)
        def _():
            dk_acc[...] = jnp.zeros_like(dk_acc)
            dv_acc[...] = jnp.zeros_like(dv_acc)

        r = lax.broadcasted_iota(jnp.int32, (TILE, TILE), 0)
        c = lax.broadcasted_iota(jnp.int32, (TILE, TILE), 1)
        strict = c < r
        u_after = _tri(TILE, lambda rr, cc: rr > cc)
        u_before = _tri(TILE, lambda rr, cc: rr < cc)
        qs = [_scaled_q(q_ref, _head_slices(hh), -SCALE) for hh in range(SB_GROUP)]
        dos = [do_ref[:, _head_slices(hh)].astype(BF16) for hh in range(SB_GROUP)]
        dots = [do_ref[:, _head_slices(hh)].T.astype(BF16) for hh in range(SB_GROUP)]
        qts = [q.astype(F32).T.astype(BF16) for q in qs]

        def scores(j, slot, diag):
            kblk = k_ref[_rows(k_ref, j), :]
            for hh in range(SB_GROUP):
                nz = _dot_nt(qs[hh], kblk[:, _head_slices(hh)])
                lf = jnp.minimum(nz, 0.0) - jnp.log(1.0 + jnp.exp(_neg_abs(nz)))
                lsig = lf - nz
                if diag:
                    lf = jnp.where(strict, lf, 0.0)
                    lsig = jnp.where(strict, lsig, MASKED)
                lsig_s[slot, hh] = lsig
                lf_s[slot, hh] = lf.astype(BF16)

        def grads(j, slot, carries):
            vblk = v_ref[_rows(v_ref, j), :]
            new = []
            for hh in range(SB_GROUP):
                lfb = lf_s[slot, hh]
                lsig = lsig_s[slot, hh]
                sx = _dot(lfb, u_after)
                a = jnp.exp(lsig + sx + carries[hh])
                g = a * _dot_nt(dos[hh], vblk[:, _head_slices(hh)])
                sig = jnp.exp(lsig)
                inside = _dot(g.astype(BF16), u_before)
                dpan[hh, j] = sig * (inside + g) - g
                span[hh, j] = sig
                gsum[hh, j] = inside[:, TILE - 1:TILE] + g[:, TILE - 1:TILE]
                dv_acc[hh, j] += _dot(dots[hh], a.astype(BF16))
                new.append(carries[hh] + sx[:, 0:1] + lfb[:, 0:1].astype(F32))
            return tuple(new)

        def step1(t, carries):
            carries = grads(i - t + 1, (t - 1) % 2, carries)
            scores(i - t, t % 2, False)
            return carries

        zero1 = jnp.zeros((TILE, 1), F32)
        scores(i, 0, True)
        carries = lax.fori_loop(1, i + 1, step1, (zero1,) * SB_GROUP)
        grads(0, i % 2, carries)

        def pass2(j, state):
            kblk = k_ref[_rows(k_ref, j), :]
            new = []
            for hh in range(SB_GROUP):
                before, ndq = state[hh]
                ndzb = (dpan[hh, j] + span[hh, j] * before).astype(BF16)
                dk_acc[hh, j] += _dot(qts[hh], ndzb)
                new.append((before + gsum[hh, j], ndq + _dot(ndzb, kblk[:, _head_slices(hh)])))
            return tuple(new)

        zero2 = (zero1, jnp.zeros((TILE, HEAD_DIM), F32))
        state = lax.fori_loop(0, i + 1, pass2, (zero2,) * SB_GROUP)
        dq_ref[...] = jnp.concatenate([st[1] * -SCALE for st in state], axis=1).astype(BF16)

        @pl.when(i == nb - 1)
        def _():
            for acc, ref in ((dk_acc, dk_ref), (dv_acc, dv_ref)):
                for j in range(nb):
                    ref[j * TILE:(j + 1) * TILE, :] = jnp.concatenate(
                        [acc[hh, j].T for hh in range(SB_GROUP)], axis=1).astype(BF16)

    qspec = pl.BlockSpec((TILE, SB_LANES), lambda p, i: (i, p))
    kvspec = pl.BlockSpec((s, SB_LANES), lambda p, i: (0, p))
    out = jax.ShapeDtypeStruct((s, SB_WIDTH), BF16)
    outs = _pair_grid_call(
        name, body, nb,
        in_specs=[pl.BlockSpec((TILE, SB_LANES), lambda p, i: (i, cb + p)),
                  pl.BlockSpec((s, SB_LANES), lambda p, i: (0, cb + kb + p)),
                  pl.BlockSpec((s, SB_LANES), lambda p, i: (0, cb + 2 * kb + p)),
                  pl.BlockSpec((TILE, SB_LANES), lambda p, i: (i, db + p))],
        out_specs=[qspec, kvspec, kvspec], out_shape=[out, out, out],
        scratch=[pltpu.VMEM((SB_GROUP, nb, HEAD_DIM, TILE), F32), pltpu.VMEM((SB_GROUP, nb, HEAD_DIM, TILE), F32),
                 pltpu.VMEM((SB_GROUP, nb, TILE, TILE), F32), pltpu.VMEM((SB_GROUP, nb, TILE, TILE), F32),
                 pltpu.VMEM((SB_GROUP, nb, TILE, 1), F32),
                 pltpu.VMEM((2, SB_GROUP, TILE, TILE), F32), pltpu.VMEM((2, SB_GROUP, TILE, TILE), BF16)],
        args=(pa, pa, pa, dout), carried=carried, groups=kb)
    return outs[:3], outs[3:]


def _fox_scores(q, kj, cq, crj, causal, diag):
    sc = _dot_nt(q, kj) + (cq - crj)
    if diag:
        sc = jnp.where(causal, sc, NEG_INF)
    return sc


def _fox_fwd(name, pa, col0, ccol4, crow4, carried=None):
    s = pa.shape[0]
    nb = s // TILE
    heads, lanes = FOX_FWD_GROUP, FOX_FWD_GROUP * HEAD_DIM
    cb = col0 // lanes
    kb = FOX_WIDTH // lanes

    def body(q_ref, k_ref, v_ref, cc_ref, cr_ref, o_ref, lse_ref, sc_s):
        i = pl.program_id(1)
        head0 = pl.program_id(0) * heads
        r = lax.broadcasted_iota(jnp.int32, (TILE, TILE), 0)
        c = lax.broadcasted_iota(jnp.int32, (TILE, TILE), 1)
        causal = c <= r
        qs = [_scaled_q(q_ref, _head_slices(hh)) for hh in range(heads)]
        cqs = [cc_ref[:, HEAD_DIM * hh:HEAD_DIM * hh + 1] for hh in range(heads)]

        def logits(j, slot, diag):
            kblk = k_ref[_rows(k_ref, j), :]
            tops = []
            for hh in range(heads):
                sc = _fox_scores(qs[hh], kblk[:, _head_slices(hh)], cqs[hh], cr_ref[j, pl.ds(head0 + hh, 1), :], causal, diag)
                sc_s[slot, hh] = sc
                tops.append(jnp.max(sc, axis=1, keepdims=True))
            return tuple(tops)

        def update(j, slot, tops, state):
            vblk = v_ref[_rows(v_ref, j), :]
            new = []
            for hh in range(heads):
                m, l, acc = state[hh]
                m2 = jnp.maximum(m, tops[hh])
                alpha = jnp.exp(m - m2)
                p = jnp.exp(sc_s[slot, hh] - m2)
                new.append((m2, l * alpha + jnp.sum(p, axis=1, keepdims=True),
                            acc * alpha + _dot(p.astype(BF16), vblk[:, _head_slices(hh)])))
            return tuple(new)

        def step(t, both):
            tops, state = both
            state = update(i - t + 1, (t - 1) % 2, tops, state)
            return logits(i - t, t % 2, False), state

        zero = (jnp.full((TILE, 1), NEG_INF, F32), jnp.zeros((TILE, 1), F32), jnp.zeros((TILE, HEAD_DIM), F32))
        tops, state = lax.fori_loop(1, i + 1, step, (logits(i, 0, True), (zero,) * heads))
        state = update(0, i % 2, tops, state)
        o_ref[...] = jnp.concatenate([st[2] / st[1] for st in state], axis=1)
        lse_ref[...] = jnp.concatenate(
            [jnp.broadcast_to(st[0] + jnp.log(st[1]), (TILE, HEAD_DIM)) for st in state], axis=1)

    outs = _pair_grid_call(
        name, body, nb,
        in_specs=[pl.BlockSpec((TILE, lanes), lambda p, i: (i, cb + p)),
                  pl.BlockSpec((s, lanes), lambda p, i: (0, cb + kb + p)),
                  pl.BlockSpec((s, lanes), lambda p, i: (0, cb + 2 * kb + p)),
                  pl.BlockSpec((TILE, lanes), lambda p, i: (i, p)),
                  pl.BlockSpec((nb, 8, TILE), lambda p, i: (0, 0, 0))],
        out_specs=[pl.BlockSpec((TILE, lanes), lambda p, i: (i, p)), pl.BlockSpec((TILE, lanes), lambda p, i: (i, p))],
        out_shape=[jax.ShapeDtypeStruct((s, FOX_WIDTH), F32), jax.ShapeDtypeStruct((s, FOX_WIDTH), F32)],
        scratch=[pltpu.VMEM((2, heads, TILE, TILE), F32)],
        args=(pa, pa, pa, ccol4, crow4), carried=carried, groups=kb)
    return outs[0], outs[1], outs[2:]


def _fox_bwd(name, pa, col0, ccol4, crow4, out, lse, dout, dcol0, carried=None):
    s = pa.shape[0]
    nb = s // TILE
    cb = col0 // FOX_LANES
    kb = FOX_WIDTH // FOX_LANES
    db = dcol0 // FOX_LANES

    def body(q_ref, k_ref, v_ref, cc_ref, cr_ref, o_ref, lse_ref, do_ref,
             dq_ref, dk_ref, dv_ref, cs_ref, dk_acc, dv_acc, p_s, ds_s):
        i = pl.program_id(1)
        head0 = pl.program_id(0) * FOX_GROUP

        @pl.when(i == 0)
        def _():
            dk_acc[...] = jnp.zeros_like(dk_acc)
            dv_acc[...] = jnp.zeros_like(dv_acc)

        @pl.when(jnp.logical_and(i == 0, head0 == 0))
        def _():
            cs_ref[...] = jnp.zeros_like(cs_ref)

        r = lax.broadcasted_iota(jnp.int32, (TILE, TILE), 0)
        c = lax.broadcasted_iota(jnp.int32, (TILE, TILE), 1)
        causal = c <= r
        qs = [_scaled_q(q_ref, _head_slices(hh)) for hh in range(FOX_GROUP)]
        cqs = [cc_ref[:, HEAD_DIM * hh:HEAD_DIM * hh + 1] for hh in range(FOX_GROUP)]
        lses = [lse_ref[:, HEAD_DIM * hh:HEAD_DIM * hh + 1] for hh in range(FOX_GROUP)]
        dofs = [do_ref[:, _head_slices(hh)] for hh in range(FOX_GROUP)]
        dos = [d_.astype(BF16) for d_ in dofs]
        dots = [d_.T.astype(BF16) for d_ in dofs]
        qts = [q.astype(F32).T.astype(BF16) for q in qs]
        deltas = [jnp.sum(dofs[hh] * o_ref[:, _head_slices(hh)], axis=1, keepdims=True) for hh in range(FOX_GROUP)]

        def probs(j, slot, rowsums, diag):
            kblk = k_ref[_rows(k_ref, j), :]
            vblk = v_ref[_rows(v_ref, j), :]
            new = []
            for hh in range(FOX_GROUP):
                sl = _head_slices(hh)
                sc = _fox_scores(qs[hh], kblk[:, sl], cqs[hh], cr_ref[j, pl.ds(head0 + hh, 1), :], causal, diag)
                p = jnp.exp(sc - lses[hh])
                ds = p * (_dot_nt(dos[hh], vblk[:, sl]) - deltas[hh])
                p_s[slot, hh] = p.astype(BF16)
                ds_s[slot, hh] = ds.astype(BF16)
                cs_ref[j, pl.ds(head0 + hh, 1), :] += jnp.sum(ds, axis=0, keepdims=True)
                new.append(rowsums[hh] + jnp.sum(ds, axis=1, keepdims=True))
            return tuple(new)

        def accumulate(j, slot, dqs):
            kblk = k_ref[_rows(k_ref, j), :]
            new = []
            for hh in range(FOX_GROUP):
                dsb = ds_s[slot, hh]
                dv_acc[hh, j] += _dot(dots[hh], p_s[slot, hh])
                dk_acc[hh, j] += _dot(qts[hh], dsb)
                new.append(dqs[hh] + _dot(dsb, kblk[:, _head_slices(hh)]))
            return tuple(new)

        def step(t, both):
            rowsums, dqs = both
            dqs = accumulate(i - t + 1, (t - 1) % 2, dqs)
            return probs(i - t, t % 2, rowsums, False), dqs

        zero1 = jnp.zeros((TILE, 1), F32)
        zero64 = jnp.zeros((TILE, HEAD_DIM), F32)
        rowsums, dqs = lax.fori_loop(1, i + 1, step,
                                     (probs(i, 0, (zero1,) * FOX_GROUP, True), (zero64,) * FOX_GROUP))
        dqs = accumulate(0, i % 2, dqs)
        for hh in range(FOX_GROUP):
            cs_ref[i, pl.ds(head0 + hh, 1), :] -= jnp.broadcast_to(rowsums[hh], (TILE, LANES)).T[0:1, :]
        dq_ref[...] = jnp.concatenate([dq * SCALE for dq in dqs], axis=1).astype(BF16)

        @pl.when(i == nb - 1)
        def _():
            for acc, ref in ((dk_acc, dk_ref), (dv_acc, dv_ref)):
                for j in range(nb):
                    ref[j * TILE:(j + 1) * TILE, :] = jnp.concatenate(
                        [acc[hh, j].T for hh in range(FOX_GROUP)], axis=1).astype(BF16)

    qspec = pl.BlockSpec((TILE, FOX_LANES), lambda p, i: (i, p))
    kvspec = pl.BlockSpec((s, FOX_LANES), lambda p, i: (0, p))
    o3 = jax.ShapeDtypeStruct((s, FOX_WIDTH), BF16)
    outs = _pair_grid_call(
        name, body, nb,
        in_specs=[pl.BlockSpec((TILE, FOX_LANES), lambda p, i: (i, cb + p)),
                  pl.BlockSpec((s, FOX_LANES), lambda p, i: (0, cb + kb + p)),
                  pl.BlockSpec((s, FOX_LANES), lambda p, i: (0, cb + 2 * kb + p)),
                  pl.BlockSpec((TILE, FOX_LANES), lambda p, i: (i, p)),
                  pl.BlockSpec((nb, 8, TILE), lambda p, i: (0, 0, 0)),
                  qspec,
                  pl.BlockSpec((TILE, FOX_LANES), lambda p, i: (i, p)),
                  pl.BlockSpec((TILE, FOX_LANES), lambda p, i: (i, db + p))],
        out_specs=[qspec, kvspec, kvspec, pl.BlockSpec((nb, 8, TILE), lambda p, i: (0, 0, 0))],
        out_shape=[o3, o3, o3, jax.ShapeDtypeStruct((nb, 8, TILE), F32)],
        scratch=[pltpu.VMEM((FOX_GROUP, nb, HEAD_DIM, TILE), F32), pltpu.VMEM((FOX_GROUP, nb, HEAD_DIM, TILE), F32),
                 pltpu.VMEM((2, FOX_GROUP, TILE, TILE), BF16), pltpu.VMEM((2, FOX_GROUP, TILE, TILE), BF16)],
        args=(pa, pa, pa, ccol4, crow4, out, lse, dout), carried=carried, groups=kb)
    return outs[:4], outs[4:]


def _mem_fwd(name, pa, mkv):
    s = pa.shape[0]
    ml = mkv.shape[0]
    nb = s // TILE
    cb = QKV_WIDTH // MEM_WIDTH

    def body(q_ref, k_ref, v_ref, o_ref, lse_ref):
        outs, lses = [], []
        for hh in range(MEM_HEADS):
            sl = _head_slices(hh)
            sc = _dot_nt(_scaled_q(q_ref, sl), k_ref[:, sl])
            m = jnp.max(sc, axis=1, keepdims=True)
            p = jnp.exp(sc - m)
            l = jnp.sum(p, axis=1, keepdims=True)
            outs.append(_dot(p.astype(BF16), v_ref[:, sl]) / l)
            lses.append(jnp.broadcast_to(m + jnp.log(l), (TILE, HEAD_DIM)))
        o_ref[...] = jnp.concatenate(outs, axis=1)
        lse_ref[...] = jnp.concatenate(lses, axis=1)

    row = pl.BlockSpec((TILE, MEM_WIDTH), lambda i: (i, 0))
    return pl.pallas_call(
        body, name=name, grid=(nb,),
        in_specs=[pl.BlockSpec((TILE, MEM_WIDTH), lambda i: (i, cb)),
                  pl.BlockSpec((ml, MEM_WIDTH), lambda i: (0, 0)),
                  pl.BlockSpec((ml, MEM_WIDTH), lambda i: (0, 1))],
        out_specs=[row, row],
        out_shape=[jax.ShapeDtypeStruct((s, MEM_WIDTH), F32), jax.ShapeDtypeStruct((s, MEM_WIDTH), F32)],
        compiler_params=_params("parallel"),
    )(pa, mkv, mkv)


def _mem_bwd(name, pa, mkv, out, lse, dout, dcol0):
    s = pa.shape[0]
    ml = mkv.shape[0]
    nb = s // TILE
    cb = QKV_WIDTH // MEM_WIDTH
    db = dcol0 // MEM_WIDTH

    def body(q_ref, k_ref, v_ref, o_ref, lse_ref, do_ref, dq_ref, dkv_ref, dk_acc, dv_acc):
        i = pl.program_id(0)

        @pl.when(i == 0)
        def _():
            dk_acc[...] = jnp.zeros_like(dk_acc)
            dv_acc[...] = jnp.zeros_like(dv_acc)

        dqs = []
        for hh in range(MEM_HEADS):
            sl = _head_slices(hh)
            q = _scaled_q(q_ref, sl)
            kh = k_ref[:, sl]
            dof = do_ref[:, sl]
            do = dof.astype(BF16)
            delta = jnp.sum(dof * o_ref[:, sl], axis=1, keepdims=True)
            p = jnp.exp(_dot_nt(q, kh) - lse_ref[:, HEAD_DIM * hh:HEAD_DIM * hh + 1])
            ds = (p * (_dot_nt(do, v_ref[:, sl]) - delta)).astype(BF16)
            dv_acc[hh] += _dot(dof.T.astype(BF16), p.astype(BF16))
            dk_acc[hh] += _dot(q.astype(F32).T.astype(BF16), ds)
            dqs.append(_dot(ds, kh) * SCALE)
        dq_ref[...] = jnp.concatenate(dqs, axis=1).astype(BF16)

        @pl.when(i == nb - 1)
        def _():
            dkv_ref[...] = jnp.concatenate([dk_acc[hh].T for hh in range(MEM_HEADS)]
                                           + [dv_acc[hh].T for hh in range(MEM_HEADS)], axis=1).astype(BF16)

    row = pl.BlockSpec((TILE, MEM_WIDTH), lambda i: (i, 0))
    return pl.pallas_call(
        body, name=name, grid=(nb,),
        in_specs=[pl.BlockSpec((TILE, MEM_WIDTH), lambda i: (i, cb)),
                  pl.BlockSpec((ml, MEM_WIDTH), lambda i: (0, 0)),
                  pl.BlockSpec((ml, MEM_WIDTH), lambda i: (0, 1)),
                  row, row,
                  pl.BlockSpec((TILE, MEM_WIDTH), lambda i: (i, db))],
        out_specs=[row, pl.BlockSpec((ml, 2 * MEM_WIDTH), lambda i: (0, 0))],
        out_shape=[jax.ShapeDtypeStruct((s, MEM_WIDTH), BF16), jax.ShapeDtypeStruct((ml, 2 * MEM_WIDTH), BF16)],
        scratch_shapes=[pltpu.VMEM((MEM_HEADS, HEAD_DIM, ml), F32), pltpu.VMEM((MEM_HEADS, HEAD_DIM, ml), F32)],
        compiler_params=_params("arbitrary"),
    )(pa, mkv, mkv, out, lse, dout)


def _head_maps():
    col = jnp.arange(MIX_WIDTH)[:, None] // HEAD_DIM
    g = (col == jnp.arange(LANES)[None, :]).astype(BF16)
    return g, g.T


def _normed_heads(osb_ref, ofx_ref, om_ref, g_ref, gt_ref):
    y = jnp.concatenate([osb_ref[...], ofx_ref[...], om_ref[...]], axis=1)
    msq = _sum_l2(y * y, g_ref[...]) * (1.0 / HEAD_DIM)
    rf = _sum_l3(lax.rsqrt(msq + EPS), gt_ref[...])
    return y * rf, rf


def _out_fwd(name, o_sb, o_fx, o_m, pb, ow, x, w_out, ts):
    s, d = x.shape
    g, gt = _head_maps()

    def body(osb_ref, ofx_ref, om_ref, gate_ref, ow_ref, x_ref, w_ref, g_ref, gt_ref, xo_ref, y2_ref):
        yh, _ = _normed_heads(osb_ref, ofx_ref, om_ref, g_ref, gt_ref)
        gate = gate_ref[...]
        y2 = (yh * ow_ref[...] * (gate * jax.nn.sigmoid(gate))).astype(BF16)
        y2_ref[...] = y2
        xo_ref[...] = x_ref[...] + _dot(y2, w_ref[...])

    return pl.pallas_call(
        body, name=name, grid=(s // ts,),
        in_specs=[_row_spec(ts, SB_WIDTH), _row_spec(ts, FOX_WIDTH), _row_spec(ts, MEM_WIDTH),
                  _row_spec(ts, MIX_WIDTH), _const_spec((1, MIX_WIDTH)), _row_spec(ts, d),
                  _const_spec((MIX_WIDTH, d)),
                  _const_spec((MIX_WIDTH, LANES)), _const_spec((LANES, MIX_WIDTH))],
        out_specs=[_row_spec(ts, d), _row_spec(ts, MIX_WIDTH)],
        out_shape=[jax.ShapeDtypeStruct((s, d), F32), jax.ShapeDtypeStruct((s, MIX_WIDTH), BF16)],
        compiler_params=_params("parallel"),
    )(o_sb, o_fx, o_m, pb, ow, x, w_out, g, gt)


def _row_spec(ts, w):
    return pl.BlockSpec((ts, w), lambda i: (i, 0))


def _const_spec(shape):
    return pl.BlockSpec(shape, lambda i: (0,) * len(shape))


def _out_bwd(name, dxb, o_sb, o_fx, o_m, pb, ow, w_out, ts):
    s, d = dxb.shape
    g, gt = _head_maps()

    def body(dx_ref, osb_ref, ofx_ref, om_ref, gate_ref, ow_ref, w_ref, g_ref, gt_ref, dy_ref, dgate_ref, dow_ref):
        @pl.when(pl.program_id(0) == 0)
        def _():
            dow_ref[...] = jnp.zeros_like(dow_ref)

        dy2 = _dot_nt(dx_ref[...], w_ref[...])
        yh, rf = _normed_heads(osb_ref, ofx_ref, om_ref, g_ref, gt_ref)
        gate = gate_ref[...]
        sig = jax.nn.sigmoid(gate)
        ow_v = ow_ref[...]
        dgate_ref[...] = (dy2 * (yh * ow_v) * (sig * (1.0 + gate * (1.0 - sig)))).astype(BF16)
        dn = dy2 * (gate * sig)
        dow_ref[...] += jnp.sum(dn * yh, axis=0, keepdims=True)
        dyh = dn * ow_v
        t = _sum_l2(dyh * yh, g_ref[...]) * (1.0 / HEAD_DIM)
        dy_ref[...] = rf * (dyh - yh * _sum_l3(t, gt_ref[...]))

    return pl.pallas_call(
        body, name=name, grid=(s // ts,),
        in_specs=[_row_spec(ts, d), _row_spec(ts, SB_WIDTH), _row_spec(ts, FOX_WIDTH), _row_spec(ts, MEM_WIDTH),
                  _row_spec(ts, MIX_WIDTH), _const_spec((1, MIX_WIDTH)),
                  _const_spec((MIX_WIDTH, d)),
                  _const_spec((MIX_WIDTH, LANES)), _const_spec((LANES, MIX_WIDTH))],
        out_specs=[_row_spec(ts, MIX_WIDTH), _row_spec(ts, MIX_WIDTH), _const_spec((1, MIX_WIDTH))],
        out_shape=[jax.ShapeDtypeStruct((s, MIX_WIDTH), F32), jax.ShapeDtypeStruct((s, PB), BF16),
                   jax.ShapeDtypeStruct((1, MIX_WIDTH), F32)],
        compiler_params=_params("arbitrary"),
    )(dxb, o_sb, o_fx, o_m, pb, ow, w_out, g, gt)


def _adamw(name, w, g, m, v, tr):
    def body(w_ref, g_ref, m_ref, v_ref, d_ref, m2_ref, v2_ref):
        gv = g_ref[...]
        m2 = ADAM_B1 * m_ref[...] + (1.0 - ADAM_B1) * gv
        v2 = ADAM_B2 * v_ref[...] + (1.0 - ADAM_B2) * (gv * gv)
        m_hat = m2 / (1.0 - ADAM_B1 ** ADAM_STEP)
        v_hat = v2 / (1.0 - ADAM_B2 ** ADAM_STEP)
        d_ref[...] = -ADAM_LR * (m_hat / (jnp.sqrt(v_hat) + ADAM_EPS) + ADAM_WD * w_ref[...])
        m2_ref[...] = m2
        v2_ref[...] = v2

    rest = w.shape[1:]
    spec = pl.BlockSpec((tr,) + rest, lambda i: (i,) + (0,) * len(rest))
    shp = jax.ShapeDtypeStruct(w.shape, F32)
    return pl.pallas_call(
        body, name=name, grid=(w.shape[0] // tr,), in_specs=[spec] * 4, out_specs=[spec] * 3, out_shape=[shp] * 3,
        compiler_params=_params("parallel"),
    )(w, g, m, v)


def _adamw_sharded(name, w, m, v, g_own, g_other, cvec, tr):
    depth, rows, cols = w.shape
    nt = rows // 2 // tr

    def body(c_ref, w_ref, m_ref, v_ref, *rest):
        g_refs, (g_ref, d_ref, m2_ref, v2_ref) = rest[:2 * depth], rest[2 * depth:]
        layer, mine = pl.program_id(0), pl.program_id(1) == c_ref[0]
        gv = None
        for lt in range(depth):
            cand = jnp.where(mine, g_refs[lt][...], g_refs[depth + lt][...])
            gv = cand if gv is None else jnp.where(layer == lt, cand, gv)
        m2 = ADAM_B1 * m_ref[...] + (1.0 - ADAM_B1) * gv
        v2 = ADAM_B2 * v_ref[...] + (1.0 - ADAM_B2) * (gv * gv)
        m_hat = m2 / (1.0 - ADAM_B1 ** ADAM_STEP)
        v_hat = v2 / (1.0 - ADAM_B2 ** ADAM_STEP)
        g_ref[...] = gv
        d_ref[...] = -ADAM_LR * (m_hat / (jnp.sqrt(v_hat) + ADAM_EPS) + ADAM_WD * w_ref[...])
        m2_ref[...] = m2
        v2_ref[...] = v2

    def g_map(lt, own):
        def index(l, hf, i, c_ref):
            use = jnp.logical_and(l == lt, (hf == c_ref[0]) == own)
            return jnp.where(use, i, 0), 0
        return index

    full = pl.BlockSpec((None, tr, cols), lambda l, hf, i, c_ref: (l, hf * nt + i, 0))
    g_specs = [pl.BlockSpec((tr, cols), g_map(lt, own)) for own in (True, False) for lt in range(depth)]
    shp = jax.ShapeDtypeStruct((depth, rows, cols), F32)
    return pl.pallas_call(
        body, name=name,
        grid_spec=pltpu.PrefetchScalarGridSpec(
            num_scalar_prefetch=1, grid=(depth, 2, nt), in_specs=[full] * 3 + g_specs, out_specs=[full] * 4),
        out_shape=[shp] * 4,
        compiler_params=_params("arbitrary", "arbitrary", "arbitrary"),
    )(cvec, w, m, v, *g_own, *g_other)


HBM_SPEC = pl.BlockSpec(memory_space=pltpu.HBM)


def _place():
    x, y, c = lax.axis_index("x"), lax.axis_index("y"), lax.axis_index("c")
    chips = [(1 - x, y), (x, 1 - y), (1 - x, 1 - y)]
    return x, y, c, chips


def _remote(src, dst, send_sems, recv_sems, k, to):
    return pltpu.make_async_remote_copy(src_ref=src, dst_ref=dst, send_sem=send_sems.at[k], recv_sem=recv_sems.at[k],
                                        device_id=to, device_id_type=MESH)


def _half_rows(n_rows, cc):
    rh = n_rows // 2
    return pl.ds(pl.multiple_of(cc * rh, 16), rh)


def _dma_sems(n):
    return [pltpu.SemaphoreType.DMA((n,)), pltpu.SemaphoreType.DMA((n,))]


class _Exchange:
    def __init__(self, inputs, out_shapes, n_sems, begin, relay, finish):
        self.inputs, self.out_shapes, self.n_sems = list(inputs), list(out_shapes), n_sems
        self.begin, self.relay, self.finish = begin, relay, finish

    @property
    def n(self):
        return len(self.inputs)

    def split(self, refs):
        return refs[:self.n], refs[self.n:2 * self.n], refs[2 * self.n], refs[2 * self.n + 1]


def _run_exchange(name, ex):
    def body(*refs):
        parts = ex.split(refs)
        for phase in (ex.begin, ex.relay, ex.finish):
            if phase is not None:
                phase(*parts)

    return pl.pallas_call(
        body, name=name, in_specs=[HBM_SPEC] * ex.n, out_specs=[HBM_SPEC] * ex.n, out_shape=ex.out_shapes,
        scratch_shapes=_dma_sems(ex.n_sems),
    )(*ex.inputs)


def _gather_exchange(shards):
    def ici(in_refs, out_refs, send_sems, recv_sems):
        x, y, c, chips = _place()
        return [_remote(in_ref.at[_half_rows(in_ref.shape[0], c)], out_ref.at[2 * x + y, _half_rows(in_ref.shape[0], c)],
                        send_sems, recv_sems, 6 * a + j, (cx, cy, c))
                for a, (in_ref, out_ref) in enumerate(zip(in_refs, out_refs)) for j, (cx, cy) in enumerate(chips)]

    def d2d(out_refs, send_sems, recv_sems, half_of):
        x, y, c, chips = _place()
        cps = []
        for a, out_ref in enumerate(out_refs):
            for j, (cx, cy) in enumerate(chips):
                piece = out_ref.at[2 * cx + cy, _half_rows(out_ref.shape[1], half_of(c))]
                cps.append(_remote(piece, piece, send_sems, recv_sems, 6 * a + 3 + j, (x, y, 1 - c)))
        return cps

    def begin(in_refs, out_refs, send_sems, recv_sems):
        for cp in ici(in_refs, out_refs, send_sems, recv_sems):
            cp.start()

    def relay(in_refs, out_refs, send_sems, recv_sems):
        x, y, c, chips = _place()
        for a, out_ref in enumerate(out_refs):
            for j, (cx, cy) in enumerate(chips):
                landed = out_ref.at[2 * cx + cy, _half_rows(out_ref.shape[1], c)]
                _remote(landed, landed, send_sems, recv_sems, 6 * a + j, (cx, cy, c)).wait_recv()
        for cp in d2d(out_refs, send_sems, recv_sems, lambda c_: c_):
            cp.start()

    def finish(in_refs, out_refs, send_sems, recv_sems):
        for cp in d2d(out_refs, send_sems, recv_sems, lambda c_: 1 - c_):
            cp.wait_recv()
        for cp in ici(in_refs, out_refs, send_sems, recv_sems) + d2d(out_refs, send_sems, recv_sems, lambda c_: c_):
            cp.wait_send()

    shapes = [jax.ShapeDtypeStruct((N_CHIPS,) + s_.shape, s_.dtype) for s_ in shards]
    return _Exchange(shards, shapes, 6 * len(shards), begin, relay, finish)


def _swap_exchange(g4s):
    def copies(in_refs, out_refs, send_sems, recv_sems):
        x, y, c, _ = _place()
        return [_remote(in_ref.at[:, _half_rows(in_ref.shape[1], 1 - c), :], out_ref, send_sems, recv_sems, a, (x, y, 1 - c))
                for a, (in_ref, out_ref) in enumerate(zip(in_refs, out_refs))]

    def begin(*parts):
        for cp in copies(*parts):
            cp.start()

    def finish(*parts):
        for cp in copies(*parts):
            cp.wait()

    shapes = [jax.ShapeDtypeStruct((g.shape[0], g.shape[1] // 2, g.shape[2]), g.dtype) for g in g4s]
    return _Exchange(g4s, shapes, len(g4s), begin, None, finish)


def _add_half(name, g4, r1, cvec, tr):
    n, r, w = g4.shape
    rh = r // 2
    nblk = rh // tr

    def body(c_ref, a_ref, b_ref, o_ref):
        o_ref[...] = (a_ref[...].astype(F32) + b_ref[...].astype(F32)).astype(BF16)

    return pl.pallas_call(
        body, name=name,
        grid_spec=pltpu.PrefetchScalarGridSpec(
            num_scalar_prefetch=1, grid=(n, nblk),
            in_specs=[pl.BlockSpec((None, tr, w), lambda k, i, c_ref: (k, c_ref[0] * nblk + i, 0)),
                      pl.BlockSpec((None, tr, w), lambda k, i, c_ref: (k, i, 0))],
            out_specs=pl.BlockSpec((None, tr, w), lambda k, i, c_ref: (k, i, 0))),
        out_shape=jax.ShapeDtypeStruct((n, rh, w), BF16),
        compiler_params=_params("parallel", "parallel"),
    )(cvec, g4, r1)


def _scatter_exchange(h4s):
    def sends(in_refs, out_refs, send_sems, recv_sems):
        x, y, c, chips = _place()
        return [_remote(in_ref.at[2 * cx + cy], out_ref.at[j], send_sems, recv_sems, 3 * a + j, (cx, cy, c))
                for a, (in_ref, out_ref) in enumerate(zip(in_refs, out_refs)) for j, (cx, cy) in enumerate(chips)]

    def begin(*parts):
        for cp in sends(*parts):
            cp.start()

    def finish(in_refs, out_refs, send_sems, recv_sems):
        x, y, c, chips = _place()
        for a, out_ref in enumerate(out_refs):
            for j, (cx, cy) in enumerate(chips):
                got = out_ref.at[j]
                _remote(got, got, send_sems, recv_sems, 3 * a + j, (cx, cy, c)).wait_recv()
        for cp in sends(in_refs, out_refs, send_sems, recv_sems):
            cp.wait_send()

    shapes = [jax.ShapeDtypeStruct((3,) + h.shape[1:], h.dtype) for h in h4s]
    return _Exchange(h4s, shapes, 3 * len(h4s), begin, None, finish)


def _sum_chips(name, h4, r3, mvec, tr):
    _, rh, w = h4.shape

    def body(m_ref, a_ref, b_ref, c_ref, d_ref, o_ref):
        o_ref[...] = ((a_ref[...].astype(F32) + b_ref[...].astype(F32)) + c_ref[...].astype(F32)) + d_ref[...].astype(F32)

    specs = [pl.BlockSpec((None, tr, w), lambda i, m_ref: (m_ref[0], i, 0))]
    specs += [pl.BlockSpec((None, tr, w), functools.partial(lambda k, i, m_ref: (k, i, 0), k)) for k in range(3)]
    return pl.pallas_call(
        body, name=name,
        grid_spec=pltpu.PrefetchScalarGridSpec(
            num_scalar_prefetch=1, grid=(rh // tr,), in_specs=specs,
            out_specs=pl.BlockSpec((tr, w), lambda i, m_ref: (i, 0))),
        out_shape=jax.ShapeDtypeStruct((rh, w), F32),
        compiler_params=_params("parallel"),
    )(mvec, h4, r3, r3, r3)


def _swap_reduced_exchange(ghs):
    def copies(in_refs, out_refs, send_sems, recv_sems):
        x, y, c, _ = _place()
        return [_remote(in_ref, out_ref, send_sems, recv_sems, a, (x, y, 1 - c))
                for a, (in_ref, out_ref) in enumerate(zip(in_refs, out_refs))]

    def begin(*parts):
        for cp in copies(*parts):
            cp.start()

    def finish(*parts):
        for cp in copies(*parts):
            cp.wait()

    return _Exchange(ghs, [jax.ShapeDtypeStruct(g.shape, g.dtype) for g in ghs], len(ghs), begin, None, finish)


class _SemaphoresFrom:
    def __init__(self, sems, first):
        self.sems, self.first = sems, first

    @property
    def at(self):
        return self

    def __getitem__(self, k):
        return self.sems.at[self.first + k]


def _both(a, b):
    def phase(fa, fb):
        if fa is None and fb is None:
            return None

        def run(in_refs, out_refs, send_sems, recv_sems):
            if fa is not None:
                fa(in_refs[:a.n], out_refs[:a.n], send_sems, recv_sems)
            if fb is not None:
                fb(in_refs[a.n:], out_refs[a.n:], _SemaphoresFrom(send_sems, a.n_sems), _SemaphoresFrom(recv_sems, a.n_sems))

        return run

    return _Exchange(a.inputs + b.inputs, a.out_shapes + b.out_shapes, a.n_sems + b.n_sems,
                     phase(a.begin, b.begin), phase(a.relay, b.relay), phase(a.finish, b.finish))


def _small_update(name, partials, weights, moments1, moments2):
    n = len(partials)
    width = max(p.shape[1] for p in partials)
    starts, at = [], 0
    for p in partials:
        starts.append(at)
        at += p.shape[0]
    rows = -(-at // 8) * 8
    has_w = [w is not None for w in weights]
    n_w = sum(has_w)

    def body(*refs):
        p_refs = refs[:n]
        w_refs, m_refs, v_refs = refs[n:n + n_w], refs[n + n_w:n + 2 * n_w], refs[n + 2 * n_w:n + 3 * n_w]
        outs = refs[n + 3 * n_w:-4]
        g_refs, upd_refs = outs[:n], outs[n:]
        vec, buf, send_sems, recv_sems = refs[-4:]
        x, y, c, _ = _place()
        me = 4 * x + 2 * y + c
        vec[...] = jnp.zeros_like(vec)
        for p_ref, r0 in zip(p_refs, starts):
            vec[r0:r0 + p_ref.shape[0], 0:p_ref.shape[1]] = p_ref[...]
        buf[me] = vec[...]
        flips = [(fx, fy, fc) for fx in (0, 1) for fy in (0, 1) for fc in (0, 1)][1:]
        peers = [(x + fx - 2 * x * fx, y + fy - 2 * y * fy, c + fc - 2 * c * fc) for fx, fy, fc in flips]
        sends = [_remote(vec, buf.at[me], send_sems, recv_sems, k, peer) for k, peer in enumerate(peers)]
        for cp in sends:
            cp.start()
        for k, (px, py, pc) in enumerate(peers):
            got = buf.at[4 * px + 2 * py + pc]
            _remote(got, got, send_sems, recv_sems, k, (px, py, pc)).wait_recv()
        for cp in sends:
            cp.wait_send()
        total = buf[0]
        for dev in range(1, N_DEV):
            total = total + buf[dev]
        k = 0
        for a in range(n):
            r, w = g_refs[a].shape
            g = total[starts[a]:starts[a] + r, 0:w]
            g_refs[a][...] = g
            if has_w[a]:
                m2 = ADAM_B1 * m_refs[k][...] + (1.0 - ADAM_B1) * g
                v2 = ADAM_B2 * v_refs[k][...] + (1.0 - ADAM_B2) * (g * g)
                m_hat = m2 / (1.0 - ADAM_B1 ** ADAM_STEP)
                v_hat = v2 / (1.0 - ADAM_B2 ** ADAM_STEP)
                upd_refs[3 * k][...] = -ADAM_LR * (m_hat / (jnp.sqrt(v_hat) + ADAM_EPS) + ADAM_WD * w_refs[k][...])
                upd_refs[3 * k + 1][...] = m2
                upd_refs[3 * k + 2][...] = v2
                k += 1

    ws = [w for w in weights if w is not None]
    g_shapes = [jax.ShapeDtypeStruct(p.shape if w is None else w.shape, F32) for p, w in zip(partials, weights)]
    u_shapes = [jax.ShapeDtypeStruct(w.shape, F32) for w in ws for _ in range(3)]
    vm = pl.BlockSpec(memory_space=pltpu.VMEM)
    n_args = n + 3 * n_w
    outs = pl.pallas_call(
        body, name=name, in_specs=[vm] * n_args, out_specs=[vm] * (n + 3 * n_w), out_shape=g_shapes + u_shapes,
        scratch_shapes=[pltpu.VMEM((rows, width), F32), pltpu.VMEM((N_DEV, rows, width), F32),
                        pltpu.SemaphoreType.DMA((7,)), pltpu.SemaphoreType.DMA((7,))],
    )(*partials, *ws, *[m for m in moments1 if m is not None], *[v for v in moments2 if v is not None])
    return outs[:n], outs[n:]


GATE_COL = 3 * SB_WIDTH + 3 * FOX_WIDTH + FOX_HEADS + MEM_WIDTH
FL_COL = QKV_WIDTH


GROUP_A_COLS = [(0, QKV_WIDTH), (FL_COL + FOX_HEADS, MEM_WIDTH)]
GROUP_B_COLS = [(GATE_COL, MIX_WIDTH), (FL_COL, FOX_HEADS)]


def _group_from_shards(shard_of, cw, spans, pad):
    parts = []
    for lo, width in spans:
        hi = lo + width
        for j in range(N_CHIPS):
            a, b = max(lo, j * cw), min(hi, (j + 1) * cw)
            if a < b:
                parts.append(shard_of(j)[:, a - j * cw:b - j * cw])
    if pad:
        parts.append(jnp.zeros((parts[0].shape[0], pad), parts[0].dtype))
    return jnp.concatenate(parts, axis=1)


def _shard_from_groups(ga, gb, j, cw):
    lo, hi = j * cw, (j + 1) * cw
    placed = []
    for grp, spans in ((ga, GROUP_A_COLS), (gb, GROUP_B_COLS)):
        at = 0
        for first, width in spans:
            a, b = max(lo, first), min(hi, first + width)
            if a < b:
                placed.append((a, grp[:, at + a - first:at + b - first]))
            at += width
    return jnp.concatenate([p for _, p in sorted(placed, key=lambda t: t[0])], axis=1)


def _tile_of(n, cap, unit):
    if n <= cap:
        return n
    best = None
    for t in range(unit, cap + 1, unit):
        if n % t == 0:
            best = t
    assert best is not None, (n, cap, unit)
    return best


def _column_major_rows(a):
    dp, r, c = a.shape
    return a.transpose(2, 0, 1).reshape(c, dp, r // LANES, LANES).transpose(0, 2, 1, 3).reshape(-1, 8, LANES)


def _from_column_major_rows(b, shape):
    dp, r, c = shape
    return b.reshape(c, r // LANES, dp, LANES).transpose(0, 2, 1, 3).reshape(c, dp, r).transpose(1, 2, 0)


def kernel(x, mem, norm_w, w_in, b_forget, mem_norm_w, w_mem_kv, out_norm_w, w_out, final_norm_w, loss_target, m_norm_w, m_w_in, m_b_forget, m_mem_norm_w, m_w_mem_kv, m_out_norm_w, m_w_out, m_final_norm_w, v_norm_w, v_w_in, v_b_forget, v_mem_norm_w, v_w_mem_kv, v_out_norm_w, v_w_out, v_final_norm_w):
    xs = x[0]
    mems = mem[0]
    target = loss_target[0]
    s, d = xs.shape
    depth = norm_w.shape[0]
    nb = s // TILE
    ts = _tile_of(s, 512, 8)
    big = (w_in, w_mem_kv, w_out)
    core = lax.axis_index("c")
    chip = 2 * lax.axis_index("x") + lax.axis_index("y")
    cvec = core.astype(jnp.int32).reshape(1)
    mvec = chip.astype(jnp.int32).reshape(1)
    cw = w_in.shape[2]

    own_w = [[a[l].astype(BF16) for a in big] for l in range(depth)]

    def lay_out_in(own, got):
        shard_of = lambda j: jnp.where(chip == j, own, got[j])
        return (_group_from_shards(shard_of, cw, GROUP_A_COLS, 0),
                _group_from_shards(shard_of, cw, GROUP_B_COLS, LANES - FOX_HEADS))

    def lay_out_rows(own, got):
        full = jnp.where(lax.broadcasted_iota(jnp.int32, got.shape, 0) == chip, own[None], got)
        return full.reshape(-1, full.shape[2])

    w_in_groups = [lay_out_in(own_w[0][0], _run_exchange("gather_weights0", _gather_exchange(own_w[0][:1]))[0])]
    layer_w = []

    tm = _tile_of(s, 256, 8)
    fl_block = MIX_WIDTH // LANES

    saved = []
    cur = xs
    for l in range(depth):
        wa, wb = w_in_groups[l]
        h = _rms_fwd(f"rms_fwd{l}", cur, norm_w[l][None], ts)
        pa = _mm(f"inproj_a{l}", h, wa, "nn", tm, _tile_of(PA, 1664, LANES), BF16)
        pb = _mm(f"inproj_b{l}", h, wb, "nn", tm, PB, F32)
        bpad = jnp.pad(b_forget[l], (0, LANES - FOX_HEADS))[None]
        ccol4, crow4 = _gate_fwd(f"gate_fwd{l}", pb, bpad, fl_block)
        more = l + 1 < depth
        o_sb, got = _sb_fwd(f"sb_fwd{l}", pa, 0, carried=_gather_exchange(own_w[l][1:]))
        wkv, wout = lay_out_rows(own_w[l][1], got[0]), lay_out_rows(own_w[l][2], got[1])
        layer_w.append((wa, wb, wkv, wout))
        o_fx, lse_fx, got = _fox_fwd(f"fox_fwd{l}", pa, 3 * SB_WIDTH, ccol4, crow4,
                                     carried=_gather_exchange(own_w[l + 1][:1]) if more else None)
        if more:
            w_in_groups.append(lay_out_in(own_w[l + 1][0], got[0]))
        mn = _rms_fwd(f"mem_rms{l}", mems, mem_norm_w[l][None], mems.shape[0])
        mkv = _mm(f"mem_kv{l}", mn, wkv, "nn", mems.shape[0], 2 * MEM_WIDTH, BF16)
        o_m, lse_m = _mem_fwd(f"mem_fwd{l}", pa, mkv)
        nxt, y2 = _out_fwd(f"out_fwd{l}", o_sb, o_fx, o_m, pb, out_norm_w[l][None], cur, wout, ts)
        saved.append((cur, h, pa, pb, bpad, ccol4, crow4, o_sb, o_fx, lse_fx, mn, mkv, o_m, lse_m, y2))
        cur = nxt

    loss_v, dx, dxb, g_final = _final_loss("final_loss", cur, final_norm_w[None], target, ts)

    g_norm, g_b, g_memnorm, g_outnorm = [None] * depth, [None] * depth, [None] * depth, [None] * depth
    g_wa, g_wb, g_wkv, g_wout = [None] * depth, [None] * depth, [None] * depth, [None] * depth
    g_own = [[None] * depth for _ in big]
    g_other = [[None] * depth for _ in big]

    def swap_of(jobs):
        return _swap_exchange([g for _, _, g, _ in jobs])

    def chip_sums(jobs, got):
        return [(lr, k, _add_half(f"grad_add_half{lr}_{k}", g, r_, cvec, t_), t_) for (lr, k, g, t_), r_ in zip(jobs, got)]

    def sum_at_owner(jobs, from_chips):
        return [_sum_chips(f"grad_sum_chips{lr}_{k}", h_, r_, mvec, t_) for (lr, k, h_, t_), r_ in zip(jobs, from_chips)]

    def keep(jobs, halves, others):
        for (lr, k, _, _), mine, other in zip(jobs, halves, others):
            g_own[k][lr], g_other[k][lr] = mine, other

    def job(lr, k, g4):
        return lr, k, g4, _tile_of(g4.shape[1] // 2, 256, 16)

    pending = []
    for l in reversed(range(depth)):
        xin, h, pa, pb, bpad, ccol4, crow4, o_sb, o_fx, lse_fx, mn, mkv, o_m, lse_m, y2 = saved[l]
        wa, wb, wkv, wout = layer_w[l]
        dy, dgate, g_outnorm[l] = _out_bwd(f"out_bwd{l}", dxb, o_sb, o_fx, o_m, pb, out_norm_w[l][None], wout, ts)
        g_wout[l] = _mm(f"dw_out{l}", y2, dxb, "tn", _tile_of(MIX_WIDTH, 640, LANES), d, F32)
        dq_m, dmkv = _mem_bwd(f"mem_bwd{l}", pa, mkv, o_m, lse_m, dy, SB_WIDTH + FOX_WIDTH)
        g_wkv[l] = _mm(f"dw_kv{l}", mn, dmkv, "tn", d, 2 * MEM_WIDTH, F32)
        dmn = _mm(f"dmem{l}", dmkv, wkv, "nt", mems.shape[0], d, F32)
        g_memnorm[l] = _rms_wgrad(f"mem_norm_grad{l}", mems, dmn)
        small = [job(l, 1, g_wkv[l].reshape(N_CHIPS, -1, g_wkv[l].shape[1])), job(l, 2, g_wout[l].reshape(N_CHIPS, -1, d))]
        (dq_fx, dk_fx, dv_fx, cs4), got = _fox_bwd(f"fox_bwd{l}", pa, 3 * SB_WIDTH, ccol4, crow4, o_fx, lse_fx, dy,
                                                    SB_WIDTH, carried=swap_of(small))
        pending += chip_sums(small, got)
        (dq_sb, dk_sb, dv_sb), from_chips = _sb_bwd(f"sb_bwd{l}", pa, 0, dy, 0,
                                                   carried=_scatter_exchange([j[2] for j in pending]))
        reduced_jobs, reduced = pending, sum_at_owner(pending, from_chips)
        swap_back = _swap_reduced_exchange(reduced)
        dpb, g_b[l] = _gate_bwd(f"gate_bwd{l}", pb, bpad, cs4, fl_block, dgate)
        dpa = jnp.concatenate([dq_sb, dk_sb, dv_sb, dq_fx, dk_fx, dv_fx, dq_m], axis=1)
        tw = _tile_of(d, 512, LANES)
        g_wa[l] = _mm(f"dw_in_a{l}", h, dpa, "tn", tw, _tile_of(PA, 1664, LANES), BF16)
        g_wb[l] = _mm(f"dw_in_b{l}", h, dpb, "tn", tw, PB, BF16)
        g4_in = jnp.stack([_shard_from_groups(g_wa[l], g_wb[l], j, cw) for j in range(N_CHIPS)])
        w_in_job = [job(l, 0, g4_in)]
        if l > 0:
            dx, dxb, g_norm[l], got = _inproj_bwd(f"inproj_bwd{l}", dpa, dpb, wa, wb, xin, norm_w[l][None], dx, tm,
                                                  carried=_both(swap_of(w_in_job), swap_back))
            pending = chip_sums(w_in_job, got[:1])
            keep(reduced_jobs, reduced, got[1:])
        else:
            pending = chip_sums(w_in_job, _run_exchange("grad_swap_halves_last", swap_of(w_in_job)))
            dx, dxb, g_norm[l], got = _inproj_bwd(f"inproj_bwd{l}", dpa, dpb, wa, wb, xin, norm_w[l][None], dx, tm,
                                                  carried=_both(_scatter_exchange([j[2] for j in pending]), swap_back))
            keep(reduced_jobs, reduced, got[1:])
            last = sum_at_owner(pending, got[:1])
            keep(pending, last, _run_exchange("grad_swap_reduced_last", _swap_reduced_exchange(last)))

    small_w = [norm_w, b_forget, mem_norm_w, out_norm_w, final_norm_w]
    small_m = [m_norm_w, m_b_forget, m_mem_norm_w, m_out_norm_w, m_final_norm_w]
    small_v = [v_norm_w, v_b_forget, v_mem_norm_w, v_out_norm_w, v_final_norm_w]
    rows2 = lambda a: a.reshape(-1, a.shape[-1])
    partials = [jnp.concatenate(g_norm, axis=0), jnp.concatenate(g_b, axis=0), jnp.concatenate(g_memnorm, axis=0),
                jnp.concatenate(g_outnorm, axis=0), g_final, loss_v]
    sums, updates = _small_update("small_update", partials, [rows2(a) for a in small_w] + [None],
                                  [rows2(a) for a in small_m] + [None], [rows2(a) for a in small_v] + [None])
    small_grads = [g.reshape(a.shape) for g, a in zip(sums, small_w)]
    loss = sums[-1][0, 0]
    small_delta, small_m2, small_v2 = ([updates[3 * k + t].reshape(a.shape) for k, a in enumerate(small_w)]
                                       for t in range(3))
    big_grads, big_delta, big_m2, big_v2 = [], [], [], []
    for k, (nm, w_, m_, v_) in enumerate(zip(("w_in", "w_mem_kv", "w_out"), big, (m_w_in, m_w_mem_kv, m_w_out),
                                             (v_w_in, v_w_mem_kv, v_w_out))):
        if w_.shape[2] % LANES:
            g_full = jnp.stack([jnp.concatenate([jnp.where(core == 0, go, gt), jnp.where(core == 0, gt, go)], axis=0)
                                for go, gt in zip(g_own[k], g_other[k])])
            w_p, g_p, m_p, v_p = (_column_major_rows(a) for a in (w_, g_full, m_, v_))
            outs = _adamw(f"adamw_{nm}", w_p, g_p, m_p, v_p, _tile_of(w_p.shape[0], 600, 1))
            outs = [_from_column_major_rows(o, w_.shape) for o in (g_p, *outs)]
        else:
            outs = _adamw_sharded(f"adamw_{nm}", w_, m_, v_, g_own[k], g_other[k], cvec,
                                  _tile_of(w_.shape[1] // 2, 256, 8))
        for lst, o in zip((big_grads, big_delta, big_m2, big_v2), outs):
            lst.append(o)

    def order(sm, bg):
        return [sm[0], bg[0], sm[1], sm[2], bg[1], sm[3], bg[2], sm[4]]

    return (loss, dx[None], *order(small_grads, big_grads), *order(small_delta, big_delta),
            *order(small_m2, big_m2), *order(small_v2, big_v2))
```

```python
import functools

import jax
import jax.numpy as jnp
from jax import lax
from jax.experimental import pallas as pl
from jax.experimental.pallas import tpu as pltpu

F32 = jnp.float32
BF16 = jnp.bfloat16

HEAD_DIM = 64
SB_WIDTH = 512
FOX_WIDTH = 512
FOX_HEADS = 8
MEM_WIDTH = 256
MEM_HEADS = MEM_WIDTH // HEAD_DIM
MIX_WIDTH = SB_WIDTH + FOX_WIDTH + MEM_WIDTH
TOTAL_HEADS = MIX_WIDTH // HEAD_DIM
IN_WIDTH = 3 * SB_WIDTH + 3 * FOX_WIDTH + FOX_HEADS + MEM_WIDTH + MIX_WIDTH
LANES = 128
QKV_WIDTH = 3 * SB_WIDTH + 3 * FOX_WIDTH
PA = QKV_WIDTH + MEM_WIDTH
PB = LANES + MIX_WIDTH
EPS = 1e-6
SCALE = HEAD_DIM ** -0.5
TILE = 256
SB_GROUP = 4
SB_LANES = SB_GROUP * HEAD_DIM
SB_FWD_GROUP = 8
FOX_GROUP = 4
FOX_LANES = FOX_GROUP * HEAD_DIM
FOX_FWD_GROUP = 4
NEG_INF = float("-inf")
MASKED = -1e30

ADAM_LR = 0.001
ADAM_B1 = 0.9
ADAM_B2 = 0.999
ADAM_EPS = 1e-08
ADAM_WD = 0.01
ADAM_STEP = 10

N_CHIPS = 4
N_DEV = 8
VMEM_LIMIT = 48 * 1024 * 1024
MESH = pl.DeviceIdType.MESH


def _params(*sem):
    return pltpu.CompilerParams(dimension_semantics=tuple(sem), vmem_limit_bytes=VMEM_LIMIT)


def _dot(a, b):
    return jnp.dot(a, b, preferred_element_type=F32)


def _dot_nt(a, b):
    return lax.dot_general(a, b, (((1,), (1,)), ((), ())), preferred_element_type=F32)


def _dot_tn(a, b):
    return lax.dot_general(a, b, (((0,), (0,)), ((), ())), preferred_element_type=F32)


def _split2(x):
    hi = x.astype(BF16)
    lo = (x - hi.astype(F32)).astype(BF16)
    return hi, lo


def _split3(x):
    hi = x.astype(BF16)
    r = x - hi.astype(F32)
    mid = r.astype(BF16)
    lo = (r - mid.astype(F32)).astype(BF16)
    return hi, mid, lo


def _sum_l2(x, u):
    hi, lo = _split2(x)
    return _dot(hi, u) + _dot(lo, u)


def _sum_l3(x, u):
    hi, mid, lo = _split3(x)
    return _dot(hi, u) + _dot(mid, u) + _dot(lo, u)


def _sum_r3(u, x):
    hi, mid, lo = _split3(x)
    return _dot(u, hi) + _dot(u, mid) + _dot(u, lo)


def _tri(n, pred):
    r = lax.broadcasted_iota(jnp.int32, (n, n), 0)
    c = lax.broadcasted_iota(jnp.int32, (n, n), 1)
    return jnp.where(pred(r, c), 1.0, 0.0).astype(BF16)


def _rows(ref, j, n=TILE):
    return pl.ds(pl.multiple_of(j * n, n), n)


def _mm(name, a, b, mode, tm, tn, out_dtype, res=None, a_lead=(), b_lead=()):
    a2, b2 = a.shape[len(a_lead):], b.shape[len(b_lead):]
    if mode == "tn":
        k, m = a2
    else:
        m, k = a2
    n = b2[0] if mode == "nt" else b2[1]
    assert m % tm == 0 and n % tn == 0, (name, m, tm, n, tn)
    na, nb = (None,) * len(a_lead), (None,) * len(b_lead)
    if mode == "tn":
        a_spec = pl.BlockSpec(na + (k, tm), lambda j, i: a_lead + (0, i))
    else:
        a_spec = pl.BlockSpec(na + (tm, k), lambda j, i: a_lead + (i, 0))
    if mode == "nt":
        b_spec = pl.BlockSpec(nb + (tn, k), lambda j, i: b_lead + (j, 0))
    else:
        b_spec = pl.BlockSpec(nb + (k, tn), lambda j, i: b_lead + (0, j))
    o_spec = pl.BlockSpec((tm, tn), lambda j, i: (i, j))
    dot = {"nn": _dot, "nt": _dot_nt, "tn": _dot_tn}[mode]

    def body(a_ref, b_ref, *rest):
        o_ref = rest[-1]
        acc = dot(a_ref[...].astype(BF16), b_ref[...].astype(BF16))
        if res is not None:
            acc = acc + rest[0][...]
        o_ref[...] = acc.astype(o_ref.dtype)

    args, specs = [a, b], [a_spec, b_spec]
    if res is not None:
        args.append(res)
        specs.append(o_spec)
    return pl.pallas_call(
        body, name=name, grid=(n // tn, m // tm), in_specs=specs, out_specs=o_spec,
        out_shape=jax.ShapeDtypeStruct((m, n), out_dtype),
        compiler_params=_params("parallel", "parallel"),
    )(*args)


def _rms_fwd(name, x, g, ts):
    s, d = x.shape

    def body(x_ref, g_ref, o_ref):
        xf = x_ref[...]
        r = lax.rsqrt(jnp.mean(xf * xf, axis=1, keepdims=True) + EPS)
        o_ref[...] = (xf * r * g_ref[...]).astype(BF16)

    return pl.pallas_call(
        body, name=name, grid=(s // ts,),
        in_specs=[pl.BlockSpec((ts, d), lambda i: (i, 0)), pl.BlockSpec((1, d), lambda i: (0, 0))],
        out_specs=pl.BlockSpec((ts, d), lambda i: (i, 0)),
        out_shape=jax.ShapeDtypeStruct((s, d), BF16),
        compiler_params=_params("parallel"),
    )(x, g)


def _inproj_bwd(name, dpa, dpb, wa, wb, x, g, dres, ts, carried=None):
    s, d = x.shape

    def body(dpa_ref, dpb_ref, wa_ref, wb_ref, x_ref, g_ref, dres_ref, dx_ref, dxb_ref, dg_ref):
        @pl.when(pl.program_id(1) == 0)
        def _():
            dg_ref[...] = jnp.zeros_like(dg_ref)

        dhf = _dot_nt(dpa_ref[...], wa_ref[...]) + _dot_nt(dpb_ref[...], wb_ref[...])
        xf = x_ref[...]
        r = lax.rsqrt(jnp.mean(xf * xf, axis=1, keepdims=True) + EPS)
        xh = xf * r
        dg_ref[...] += jnp.sum(dhf * xh, axis=0, keepdims=True)
        dxh = dhf * g_ref[...]
        m = jnp.mean(dxh * xh, axis=1, keepdims=True)
        dx = r * (dxh - xh * m) + dres_ref[...]
        dx_ref[...] = dx
        dxb_ref[...] = dx.astype(BF16)

    row = lambda w: pl.BlockSpec((ts, w), lambda p, i: (i, 0))
    whole = lambda a: pl.BlockSpec(a.shape, lambda p, i: (0, 0))
    outs = _pair_grid_call(
        name, body, s // ts,
        in_specs=[row(dpa.shape[1]), row(dpb.shape[1]), whole(wa), whole(wb), row(d), whole(g), row(d)],
        out_specs=[row(d), row(d), pl.BlockSpec((1, d), lambda p, i: (0, 0))],
        out_shape=[jax.ShapeDtypeStruct((s, d), F32), jax.ShapeDtypeStruct((s, d), BF16),
                   jax.ShapeDtypeStruct((1, d), F32)],
        scratch=[], args=(dpa, dpb, wa, wb, x, g, dres), carried=carried, groups=1)
    return outs[0], outs[1], outs[2], outs[3:]


def _rms_wgrad(name, x, dh):
    m_, d = x.shape

    def body(x_ref, dh_ref, dg_ref):
        xf = x_ref[...]
        r = lax.rsqrt(jnp.mean(xf * xf, axis=1, keepdims=True) + EPS)
        dg_ref[...] = jnp.sum(dh_ref[...] * xf * r, axis=0, keepdims=True)

    return pl.pallas_call(
        body, name=name, out_shape=jax.ShapeDtypeStruct((1, d), F32),
    )(x, dh)


def _final_loss(name, x, g, target, ts):
    s, d = x.shape

    def body(x_ref, g_ref, t_ref, loss_ref, dx_ref, dxb_ref, dg_ref):
        @pl.when(pl.program_id(0) == 0)
        def _():
            dg_ref[...] = jnp.zeros_like(dg_ref)
            loss_ref[...] = jnp.zeros_like(loss_ref)

        xf = x_ref[...]
        gw = g_ref[...]
        r = lax.rsqrt(jnp.mean(xf * xf, axis=1, keepdims=True) + EPS)
        xh = xf * r
        e = xh * gw - t_ref[...]
        part = 0.5 * jnp.sum(jnp.mean(e * e, axis=1, keepdims=True), axis=0, keepdims=True)
        loss_ref[...] += jnp.broadcast_to(part, loss_ref.shape)
        dy = e * (1.0 / d)
        dg_ref[...] += jnp.sum(dy * xh, axis=0, keepdims=True)
        dxh = dy * gw
        m = jnp.mean(dxh * xh, axis=1, keepdims=True)
        dx = r * (dxh - xh * m)
        dx_ref[...] = dx
        dxb_ref[...] = dx.astype(BF16)

    row = pl.BlockSpec((ts, d), lambda i: (i, 0))
    vec = pl.BlockSpec((1, d), lambda i: (0, 0))
    lvec = pl.BlockSpec((1, LANES), lambda i: (0, 0))
    return pl.pallas_call(
        body, name=name, grid=(s // ts,), in_specs=[row, vec, row], out_specs=[lvec, row, row, vec],
        out_shape=[jax.ShapeDtypeStruct((1, LANES), F32), jax.ShapeDtypeStruct((s, d), F32),
                   jax.ShapeDtypeStruct((s, d), BF16), jax.ShapeDtypeStruct((1, d), F32)],
        compiler_params=_params("arbitrary"),
    )(x, g, target)


def _gate_fwd(name, pb, bpad, fl_block):
    s = pb.shape[0]
    nb = s // TILE

    def body(fl_ref, b_ref, ccol_ref, crow_ref, carry):
        @pl.when(pl.program_id(0) == 0)
        def _():
            carry[...] = jnp.zeros_like(carry)

        u = fl_ref[...] + b_ref[...]
        lf = jnp.minimum(u, 0.0) - jnp.log1p(jnp.exp(-jnp.abs(u)))
        lower = _tri(TILE, lambda r, c: c <= r)
        c = _sum_r3(lower, lf) + carry[0:1, :]
        ccol_ref[...] = jnp.concatenate(
            [jnp.broadcast_to(c[:, hh:hh + 1], (TILE, HEAD_DIM)) for hh in range(FOX_HEADS)], axis=1)
        crow_ref[0] = c.T[0:8, :]
        carry[...] = jnp.broadcast_to(c[TILE - 1:TILE, :], carry.shape)

    return pl.pallas_call(
        body, name=name, grid=(nb,),
        in_specs=[pl.BlockSpec((TILE, LANES), lambda i: (i, fl_block)), pl.BlockSpec((1, LANES), lambda i: (0, 0))],
        out_specs=[pl.BlockSpec((TILE, FOX_WIDTH), lambda i: (i, 0)), pl.BlockSpec((1, 8, TILE), lambda i: (i, 0, 0))],
        out_shape=[jax.ShapeDtypeStruct((s, FOX_WIDTH), F32), jax.ShapeDtypeStruct((nb, 8, TILE), F32)],
        scratch_shapes=[pltpu.VMEM((8, LANES), F32)],
        compiler_params=_params("arbitrary"),
    )(pb, bpad)


def _gate_bwd(name, pb, bpad, colsum, fl_block, dpb):
    s = pb.shape[0]
    nb = s // TILE

    def body(fl_ref, b_ref, cs_ref, dpb_ref, dl_ref, db_ref, carry):
        @pl.when(pl.program_id(0) == 0)
        def _():
            carry[...] = jnp.zeros_like(carry)
            db_ref[...] = jnp.zeros_like(db_ref)

        upper = _tri(TILE, lambda r, c: r >= c)
        rsum = _sum_l3(cs_ref[0], upper) + carry[:, 0:1]
        carry[...] = jnp.broadcast_to(rsum[:, 0:1], carry.shape)
        full = jnp.concatenate([rsum, jnp.zeros((LANES - 8, TILE), F32)], axis=0)
        dlf = -full.T
        u = fl_ref[...] + b_ref[...]
        dlogit = dlf * (1.0 - jax.nn.sigmoid(u))
        dl_ref[...] = dlogit.astype(BF16)
        db_ref[...] += jnp.sum(dlogit, axis=0, keepdims=True)

    logits_block = pl.BlockSpec((TILE, LANES), lambda i: (nb - 1 - i, fl_block))
    return pl.pallas_call(
        body, name=name, grid=(nb,),
        in_specs=[logits_block, pl.BlockSpec((1, LANES), lambda i: (0, 0)),
                  pl.BlockSpec((1, 8, TILE), lambda i: (nb - 1 - i, 0, 0)), pl.BlockSpec(memory_space=pl.ANY)],
        out_specs=[logits_block, pl.BlockSpec((1, LANES), lambda i: (0, 0))],
        out_shape=[jax.ShapeDtypeStruct(dpb.shape, BF16), jax.ShapeDtypeStruct((1, LANES), F32)],
        scratch_shapes=[pltpu.VMEM((8, LANES), F32)], input_output_aliases={3: 0},
        compiler_params=_params("arbitrary"),
    )(pb, bpad, colsum, dpb)


def _head_slices(hh):
    return slice(HEAD_DIM * hh, HEAD_DIM * (hh + 1))


def _scaled_q(q_ref, sl, scale=SCALE):
    return (q_ref[:, sl].astype(F32) * scale).astype(BF16)


def _neg_abs(x):
    sign = jnp.uint32(0x80000000)
    return lax.bitcast_convert_type(lax.bitcast_convert_type(x, jnp.uint32) | sign, F32)


def _pair_grid_call(name, body, nb, in_specs, out_specs, out_shape, scratch, args, carried=None, groups=4):
    if carried is None:
        return pl.pallas_call(
            body, name=name, grid=(groups, nb), in_specs=in_specs, out_specs=out_specs, out_shape=out_shape,
            scratch_shapes=scratch, compiler_params=_params("arbitrary", "arbitrary"),
        )(*args)
    n_in, n_out, n_ex = len(in_specs), len(out_specs), carried.n

    def body_with_copies(*refs):
        own_in, ex_in = refs[:n_in], refs[n_in:n_in + n_ex]
        own_out = refs[n_in + n_ex:n_in + n_ex + n_out]
        ex_out = refs[n_in + n_ex + n_out:n_in + 2 * n_ex + n_out]
        own_scratch, sems = refs[n_in + 2 * n_ex + n_out:-2], refs[-2:]
        parts = (ex_in, ex_out, sems[0], sems[1])
        p, i = pl.program_id(0), pl.program_id(1)
        pl.when(jnp.logical_and(p == 0, i == 0))(lambda: carried.begin(*parts))
        if carried.relay is not None:
            pl.when(jnp.logical_and(p == groups - 1, i == max(nb - 2, 0)))(lambda: carried.relay(*parts))
        body(*own_in, *own_out, *own_scratch)
        pl.when(jnp.logical_and(p == groups - 1, i == nb - 1))(lambda: carried.finish(*parts))

    return pl.pallas_call(
        body_with_copies, name=name, grid=(groups, nb), in_specs=list(in_specs) + [HBM_SPEC] * n_ex,
        out_specs=list(out_specs) + [HBM_SPEC] * n_ex, out_shape=list(out_shape) + carried.out_shapes,
        scratch_shapes=list(scratch) + _dma_sems(carried.n_sems),
        compiler_params=_params("arbitrary", "arbitrary"),
    )(*args, *carried.inputs)


def _sb_fwd(name, pa, col0, carried=None):
    s = pa.shape[0]
    nb = s // TILE
    heads, lanes = SB_FWD_GROUP, SB_FWD_GROUP * HEAD_DIM
    cb = col0 // lanes
    kb = SB_WIDTH // lanes

    def body(q_ref, k_ref, v_ref, o_ref, lsig_s, lf_s):
        i = pl.program_id(1)
        r = lax.broadcasted_iota(jnp.int32, (TILE, TILE), 0)
        c = lax.broadcasted_iota(jnp.int32, (TILE, TILE), 1)
        strict = c < r
        u_after = _tri(TILE, lambda rr, cc: rr > cc)
        qs = [_scaled_q(q_ref, _head_slices(hh), -SCALE) for hh in range(heads)]

        def neg_z(j):
            kblk = k_ref[_rows(k_ref, j), :]
            return [_dot_nt(qs[hh], kblk[:, _head_slices(hh)]) for hh in range(heads)]

        def scores(nzs, slot, diag):
            for hh, nz in enumerate(nzs):
                lf = jnp.minimum(nz, 0.0) - jnp.log(1.0 + jnp.exp(_neg_abs(nz)))
                lsig = lf - nz
                if diag:
                    lf = jnp.where(strict, lf, 0.0)
                    lsig = jnp.where(strict, lsig, MASKED)
                lsig_s[slot, hh] = lsig
                lf_s[slot, hh] = lf.astype(BF16)

        def weigh(j, slot, state):
            vblk = v_ref[_rows(v_ref, j), :]
            new = []
            for hh in range(heads):
                carry, acc = state[hh]
                lfb = lf_s[slot, hh]
                sx = _dot(lfb, u_after)
                a = jnp.exp(lsig_s[slot, hh] + sx + carry)
                new.append((carry + sx[:, 0:1] + lfb[:, 0:1].astype(F32),
                            acc + _dot(a.astype(BF16), vblk[:, _head_slices(hh)])))
            return tuple(new)

        def step(t, state):
            state = weigh(i - t + 1, (t - 1) % 2, state)
            scores(neg_z(i - t), t % 2, False)
            return state

        zero = (jnp.zeros((TILE, 1), F32), jnp.zeros((TILE, HEAD_DIM), F32))
        scores(neg_z(i), 0, True)
        state = lax.fori_loop(1, i + 1, step, (zero,) * heads)
        state = weigh(0, i % 2, state)
        o_ref[...] = jnp.concatenate([st[1] for st in state], axis=1)

    outs = _pair_grid_call(
        name, body, nb,
        in_specs=[pl.BlockSpec((TILE, lanes), lambda p, i: (i, cb + p)),
                  pl.BlockSpec((s, lanes), lambda p, i: (0, cb + kb + p)),
                  pl.BlockSpec((s, lanes), lambda p, i: (0, cb + 2 * kb + p))],
        out_specs=[pl.BlockSpec((TILE, lanes), lambda p, i: (i, p))],
        out_shape=[jax.ShapeDtypeStruct((s, SB_WIDTH), F32)],
        scratch=[pltpu.VMEM((2, heads, TILE, TILE), F32), pltpu.VMEM((2, heads, TILE, TILE), BF16)],
        args=(pa, pa, pa), carried=carried, groups=kb)
    return outs[0], outs[1:]


def _sb_bwd(name, pa, col0, dout, dcol0, carried=None):
    s = pa.shape[0]
    nb = s // TILE
    cb = col0 // SB_LANES
    kb = SB_WIDTH // SB_LANES
    db = dcol0 // SB_LANES

    def body(q_ref, k_ref, v_ref, do_ref, dq_ref, dk_ref, dv_ref, dk_acc, dv_acc, dpan, span, gsum, lsig_s, lf_s):
        i = pl.program_id(1)

        @pl.when(i == 0)
        def _():
            dk_acc[...] = jnp.zeros_like(dk_acc)
            dv_acc[...] = jnp.zeros_like(dv_acc)

        r = lax.broadcasted_iota(jnp.int32, (TILE, TILE), 0)
        c = lax.broadcasted_iota(jnp.int32, (TILE, TILE), 1)
        strict = c < r
        u_after = _tri(TILE, lambda rr, cc: rr > cc)
        u_before = _tri(TILE, lambda rr, cc: rr < cc)
        qs = [_scaled_q(q_ref, _head_slices(hh), -SCALE) for hh in range(SB_GROUP)]
        dos = [do_ref[:, _head_slices(hh)].astype(BF16) for hh in range(SB_GROUP)]
        dots = [do_ref[:, _head_slices(hh)].T.astype(BF16) for hh in range(SB_GROUP)]
        qts = [q.astype(F32).T.astype(BF16) for q in qs]

        def scores(j, slot, diag):
            kblk = k_ref[_rows(k_ref, j), :]
            for hh in range(SB_GROUP):
                nz = _dot_nt(qs[hh], kblk[:, _head_slices(hh)])
                lf = jnp.minimum(nz, 0.0) - jnp.log(1.0 + jnp.exp(_neg_abs(nz)))
                lsig = lf - nz
                if diag:
                    lf = jnp.where(strict, lf, 0.0)
                    lsig = jnp.where(strict, lsig, MASKED)
                lsig_s[slot, hh] = lsig
                lf_s[slot, hh] = lf.astype(BF16)

        def grads(j, slot, carries):
            vblk = v_ref[_rows(v_ref, j), :]
            new = []
            for hh in range(SB_GROUP):
                lfb = lf_s[slot, hh]
                lsig = lsig_s[slot, hh]
                sx = _dot(lfb, u_after)
                a = jnp.exp(lsig + sx + carries[hh])
                g = a * _dot_nt(dos[hh], vblk[:, _head_slices(hh)])
                sig = jnp.exp(lsig)
                inside = _dot(g.astype(BF16), u_before)
                dpan[hh, j] = sig * (inside + g) - g
                span[hh, j] = sig
                gsum[hh, j] = inside[:, TILE - 1:TILE] + g[:, TILE - 1:TILE]
                dv_acc[hh, j] += _dot(dots[hh], a.astype(BF16))
                new.append(carries[hh] + sx[:, 0:1] + lfb[:, 0:1].astype(F32))
            return tuple(new)

        def step1(t, carries):
            carries = grads(i - t + 1, (t - 1) % 2, carries)
            scores(i - t, t % 2, False)
            return carries

        zero1 = jnp.zeros((TILE, 1), F32)
        scores(i, 0, True)
        carries = lax.fori_loop(1, i + 1, step1, (zero1,) * SB_GROUP)
        grads(0, i % 2, carries)

        def pass2(j, state):
            kblk = k_ref[_rows(k_ref, j), :]
            new = []
            for hh in range(SB_GROUP):
                before, ndq = state[hh]
                ndzb = (dpan[hh, j] + span[hh, j] * before).astype(BF16)
                dk_acc[hh, j] += _dot(qts[hh], ndzb)
                new.append((before + gsum[hh, j], ndq + _dot(ndzb, kblk[:, _head_slices(hh)])))
            return tuple(new)

        zero2 = (zero1, jnp.zeros((TILE, HEAD_DIM), F32))
        state = lax.fori_loop(0, i + 1, pass2, (zero2,) * SB_GROUP)
        dq_ref[...] = jnp.concatenate([st[1] * -SCALE for st in state], axis=1).astype(BF16)

        @pl.when(i == nb - 1)
        def _():
            for acc, ref in ((dk_acc, dk_ref), (dv_acc, dv_ref)):
                for j in range(nb):
                    ref[j * TILE:(j + 1) * TILE, :] = jnp.concatenate(
                        [acc[hh, j].T for hh in range(SB_GROUP)], axis=1).astype(BF16)

    qspec = pl.BlockSpec((TILE, SB_LANES), lambda p, i: (i, p))
    kvspec = pl.BlockSpec((s, SB_LANES), lambda p, i: (0, p))
    out = jax.ShapeDtypeStruct((s, SB_WIDTH), BF16)
    outs = _pair_grid_call(
        name, body, nb,
        in_specs=[pl.BlockSpec((TILE, SB_LANES), lambda p, i: (i, cb + p)),
                  pl.BlockSpec((s, SB_LANES), lambda p, i: (0, cb + kb + p)),
                  pl.BlockSpec((s, SB_LANES), lambda p, i: (0, cb + 2 * kb + p)),
                  pl.BlockSpec((TILE, SB_LANES), lambda p, i: (i, db + p))],
        out_specs=[qspec, kvspec, kvspec], out_shape=[out, out, out],
        scratch=[pltpu.VMEM((SB_GROUP, nb, HEAD_DIM, TILE), F32), pltpu.VMEM((SB_GROUP, nb, HEAD_DIM, TILE), F32),
                 pltpu.VMEM((SB_GROUP, nb, TILE, TILE), F32), pltpu.VMEM((SB_GROUP, nb, TILE, TILE), F32),
                 pltpu.VMEM((SB_GROUP, nb, TILE, 1), F32),
                 pltpu.VMEM((2, SB_GROUP, TILE, TILE), F32), pltpu.VMEM((2, SB_GROUP, TILE, TILE), BF16)],
        args=(pa, pa, pa, dout), carried=carried, groups=kb)
    return outs[:3], outs[3:]


def _fox_scores(q, kj, cq, crj, causal, diag):
    sc = _dot_nt(q, kj) + (cq - crj)
    if diag:
        sc = jnp.where(causal, sc, NEG_INF)
    return sc


def _fox_fwd(name, pa, col0, ccol4, crow4, carried=None):
    s = pa.shape[0]
    nb = s // TILE
    heads, lanes = FOX_FWD_GROUP, FOX_FWD_GROUP * HEAD_DIM
    cb = col0 // lanes
    kb = FOX_WIDTH // lanes

    def body(q_ref, k_ref, v_ref, cc_ref, cr_ref, o_ref, lse_ref, sc_s):
        i = pl.program_id(1)
        head0 = pl.program_id(0) * heads
        r = lax.broadcasted_iota(jnp.int32, (TILE, TILE), 0)
        c = lax.broadcasted_iota(jnp.int32, (TILE, TILE), 1)
        causal = c <= r
        qs = [_scaled_q(q_ref, _head_slices(hh)) for hh in range(heads)]
        cqs = [cc_ref[:, HEAD_DIM * hh:HEAD_DIM * hh + 1] for hh in range(heads)]

        def logits(j, slot, diag):
            kblk = k_ref[_rows(k_ref, j), :]
            tops = []
            for hh in range(heads):
                sc = _fox_scores(qs[hh], kblk[:, _head_slices(hh)], cqs[hh], cr_ref[j, pl.ds(head0 + hh, 1), :], causal, diag)
                sc_s[slot, hh] = sc
                tops.append(jnp.max(sc, axis=1, keepdims=True))
            return tuple(tops)

        def update(j, slot, tops, state):
            vblk = v_ref[_rows(v_ref, j), :]
            new = []
            for hh in range(heads):
                m, l, acc = state[hh]
                m2 = jnp.maximum(m, tops[hh])
                alpha = jnp.exp(m - m2)
                p = jnp.exp(sc_s[slot, hh] - m2)
                new.append((m2, l * alpha + jnp.sum(p, axis=1, keepdims=True),
                            acc * alpha + _dot(p.astype(BF16), vblk[:, _head_slices(hh)])))
            return tuple(new)

        def step(t, both):
            tops, state = both
            state = update(i - t + 1, (t - 1) % 2, tops, state)
            return logits(i - t, t % 2, False), state

        zero = (jnp.full((TILE, 1), NEG_INF, F32), jnp.zeros((TILE, 1), F32), jnp.zeros((TILE, HEAD_DIM), F32))
        tops, state = lax.fori_loop(1, i + 1, step, (logits(i, 0, True), (zero,) * heads))
        state = update(0, i % 2, tops, state)
        o_ref[...] = jnp.concatenate([st[2] / st[1] for st in state], axis=1)
        lse_ref[...] = jnp.concatenate(
            [jnp.broadcast_to(st[0] + jnp.log(st[1]), (TILE, HEAD_DIM)) for st in state], axis=1)

    outs = _pair_grid_call(
        name, body, nb,
        in_specs=[pl.BlockSpec((TILE, lanes), lambda p, i: (i, cb + p)),
                  pl.BlockSpec((s, lanes), lambda p, i: (0, cb + kb + p)),
                  pl.BlockSpec((s, lanes), lambda p, i: (0, cb + 2 * kb + p)),
                  pl.BlockSpec((TILE, lanes), lambda p, i: (i, p)),
                  pl.BlockSpec((nb, 8, TILE), lambda p, i: (0, 0, 0))],
        out_specs=[pl.BlockSpec((TILE, lanes), lambda p, i: (i, p)), pl.BlockSpec((TILE, lanes), lambda p, i: (i, p))],
        out_shape=[jax.ShapeDtypeStruct((s, FOX_WIDTH), F32), jax.ShapeDtypeStruct((s, FOX_WIDTH), F32)],
        scratch=[pltpu.VMEM((2, heads, TILE, TILE), F32)],
        args=(pa, pa, pa, ccol4, crow4), carried=carried, groups=kb)
    return outs[0], outs[1], outs[2:]


def _fox_bwd(name, pa, col0, ccol4, crow4, out, lse, dout, dcol0, carried=None):
    s = pa.shape[0]
    nb = s // TILE
    cb = col0 // FOX_LANES
    kb = FOX_WIDTH // FOX_LANES
    db = dcol0 // FOX_LANES

    def body(q_ref, k_ref, v_ref, cc_ref, cr_ref, o_ref, lse_ref, do_ref,
             dq_ref, dk_ref, dv_ref, cs_ref, dk_acc, dv_acc, p_s, ds_s):
        i = pl.program_id(1)
        head0 = pl.program_id(0) * FOX_GROUP

        @pl.when(i == 0)
        def _():
            dk_acc[...] = jnp.zeros_like(dk_acc)
            dv_acc[...] = jnp.zeros_like(dv_acc)

        @pl.when(jnp.logical_and(i == 0, head0 == 0))
        def _():
            cs_ref[...] = jnp.zeros_like(cs_ref)

        r = lax.broadcasted_iota(jnp.int32, (TILE, TILE), 0)
        c = lax.broadcasted_iota(jnp.int32, (TILE, TILE), 1)
        causal = c <= r
        qs = [_scaled_q(q_ref, _head_slices(hh)) for hh in range(FOX_GROUP)]
        cqs = [cc_ref[:, HEAD_DIM * hh:HEAD_DIM * hh + 1] for hh in range(FOX_GROUP)]
        lses = [lse_ref[:, HEAD_DIM * hh:HEAD_DIM * hh + 1] for hh in range(FOX_GROUP)]
        dofs = [do_ref[:, _head_slices(hh)] for hh in range(FOX_GROUP)]
        dos = [d_.astype(BF16) for d_ in dofs]
        dots = [d_.T.astype(BF16) for d_ in dofs]
        qts = [q.astype(F32).T.astype(BF16) for q in qs]
        deltas = [jnp.sum(dofs[hh] * o_ref[:, _head_slices(hh)], axis=1, keepdims=True) for hh in range(FOX_GROUP)]

        def probs(j, slot, rowsums, diag):
            kblk = k_ref[_rows(k_ref, j), :]
            vblk = v_ref[_rows(v_ref, j), :]
            new = []
            for hh in range(FOX_GROUP):
                sl = _head_slices(hh)
                sc = _fox_scores(qs[hh], kblk[:, sl], cqs[hh], cr_ref[j, pl.ds(head0 + hh, 1), :], causal, diag)
                p = jnp.exp(sc - lses[hh])
                ds = p * (_dot_nt(dos[hh], vblk[:, sl]) - deltas[hh])
                p_s[slot, hh] = p.astype(BF16)
                ds_s[slot, hh] = ds.astype(BF16)
                cs_ref[j, pl.ds(head0 + hh, 1), :] += jnp.sum(ds, axis=0, keepdims=True)
                new.append(rowsums[hh] + jnp.sum(ds, axis=1, keepdims=True))
            return tuple(new)

        def accumulate(j, slot, dqs):
            kblk = k_ref[_rows(k_ref, j), :]
            new = []
            for hh in range(FOX_GROUP):
                dsb = ds_s[slot, hh]
                dv_acc[hh, j] += _dot(dots[hh], p_s[slot, hh])
                dk_acc[hh, j] += _dot(qts[hh], dsb)
                new.append(dqs[hh] + _dot(dsb, kblk[:, _head_slices(hh)]))
            return tuple(new)

        def step(t, both):
            rowsums, dqs = both
            dqs = accumulate(i - t + 1, (t - 1) % 2, dqs)
            return probs(i - t, t % 2, rowsums, False), dqs

        zero1 = jnp.zeros((TILE, 1), F32)
        zero64 = jnp.zeros((TILE, HEAD_DIM), F32)
        rowsums, dqs = lax.fori_loop(1, i + 1, step,
                                     (probs(i, 0, (zero1,) * FOX_GROUP, True), (zero64,) * FOX_GROUP))
        dqs = accumulate(0, i % 2, dqs)
        for hh in range(FOX_GROUP):
            cs_ref[i, pl.ds(head0 + hh, 1), :] -= jnp.broadcast_to(rowsums[hh], (TILE, LANES)).T[0:1, :]
        dq_ref[...] = jnp.concatenate([dq * SCALE for dq in dqs], axis=1).astype(BF16)

        @pl.when(i == nb - 1)
        def _():
            for acc, ref in ((dk_acc, dk_ref), (dv_acc, dv_ref)):
                for j in range(nb):
                    ref[j * TILE:(j + 1) * TILE, :] = jnp.concatenate(
                        [acc[hh, j].T for hh in range(FOX_GROUP)], axis=1).astype(BF16)

    qspec = pl.BlockSpec((TILE, FOX_LANES), lambda p, i: (i, p))
    kvspec = pl.BlockSpec((s, FOX_LANES), lambda p, i: (0, p))
    o3 = jax.ShapeDtypeStruct((s, FOX_WIDTH), BF16)
    outs = _pair_grid_call(
        name, body, nb,
        in_specs=[pl.BlockSpec((TILE, FOX_LANES), lambda p, i: (i, cb + p)),
                  pl.BlockSpec((s, FOX_LANES), lambda p, i: (0, cb + kb + p)),
                  pl.BlockSpec((s, FOX_LANES), lambda p, i: (0, cb + 2 * kb + p)),
                  pl.BlockSpec((TILE, FOX_LANES), lambda p, i: (i, p)),
                  pl.BlockSpec((nb, 8, TILE), lambda p, i: (0, 0, 0)),
                  qspec,
                  pl.BlockSpec((TILE, FOX_LANES), lambda p, i: (i, p)),
                  pl.BlockSpec((TILE, FOX_LANES), lambda p, i: (i, db + p))],
        out_specs=[qspec, kvspec, kvspec, pl.BlockSpec((nb, 8, TILE), lambda p, i: (0, 0, 0))],
        out_shape=[o3, o3, o3, jax.ShapeDtypeStruct((nb, 8, TILE), F32)],
        scratch=[pltpu.VMEM((FOX_GROUP, nb, HEAD_DIM, TILE), F32), pltpu.VMEM((FOX_GROUP, nb, HEAD_DIM, TILE), F32),
                 pltpu.VMEM((2, FOX_GROUP, TILE, TILE), BF16), pltpu.VMEM((2, FOX_GROUP, TILE, TILE), BF16)],
        args=(pa, pa, pa, ccol4, crow4, out, lse, dout), carried=carried, groups=kb)
    return outs[:4], outs[4:]


def _mem_fwd(name, pa, mkv):
    s = pa.shape[0]
    ml = mkv.shape[0]
    nb = s // TILE
    cb = QKV_WIDTH // MEM_WIDTH

    def body(q_ref, k_ref, v_ref, o_ref, lse_ref):
        outs, lses = [], []
        for hh in range(MEM_HEADS):
            sl = _head_slices(hh)
            sc = _dot_nt(_scaled_q(q_ref, sl), k_ref[:, sl])
            m = jnp.max(sc, axis=1, keepdims=True)
            p = jnp.exp(sc - m)
            l = jnp.sum(p, axis=1, keepdims=True)
            outs.append(_dot(p.astype(BF16), v_ref[:, sl]) / l)
            lses.append(jnp.broadcast_to(m + jnp.log(l), (TILE, HEAD_DIM)))
        o_ref[...] = jnp.concatenate(outs, axis=1)
        lse_ref[...] = jnp.concatenate(lses, axis=1)

    row = pl.BlockSpec((TILE, MEM_WIDTH), lambda i: (i, 0))
    return pl.pallas_call(
        body, name=name, grid=(nb,),
        in_specs=[pl.BlockSpec((TILE, MEM_WIDTH), lambda i: (i, cb)),
                  pl.BlockSpec((ml, MEM_WIDTH), lambda i: (0, 0)),
                  pl.BlockSpec((ml, MEM_WIDTH), lambda i: (0, 1))],
        out_specs=[row, row],
        out_shape=[jax.ShapeDtypeStruct((s, MEM_WIDTH), F32), jax.ShapeDtypeStruct((s, MEM_WIDTH), F32)],
        compiler_params=_params("parallel"),
    )(pa, mkv, mkv)


def _mem_bwd(name, pa, mkv, out, lse, dout, dcol0):
    s = pa.shape[0]
    ml = mkv.shape[0]
    nb = s // TILE
    cb = QKV_WIDTH // MEM_WIDTH
    db = dcol0 // MEM_WIDTH

    def body(q_ref, k_ref, v_ref, o_ref, lse_ref, do_ref, dq_ref, dkv_ref, dk_acc, dv_acc):
        i = pl.program_id(0)

        @pl.when(i == 0)
        def _():
            dk_acc[...] = jnp.zeros_like(dk_acc)
            dv_acc[...] = jnp.zeros_like(dv_acc)

        dqs = []
        for hh in range(MEM_HEADS):
            sl = _head_slices(hh)
            q = _scaled_q(q_ref, sl)
            kh = k_ref[:, sl]
            dof = do_ref[:, sl]
            do = dof.astype(BF16)
            delta = jnp.sum(dof * o_ref[:, sl], axis=1, keepdims=True)
            p = jnp.exp(_dot_nt(q, kh) - lse_ref[:, HEAD_DIM * hh:HEAD_DIM * hh + 1])
            ds = (p * (_dot_nt(do, v_ref[:, sl]) - delta)).astype(BF16)
            dv_acc[hh] += _dot(dof.T.astype(BF16), p.astype(BF16))
            dk_acc[hh] += _dot(q.astype(F32).T.astype(BF16), ds)
            dqs.append(_dot(ds, kh) * SCALE)
        dq_ref[...] = jnp.concatenate(dqs, axis=1).astype(BF16)

        @pl.when(i == nb - 1)
        def _():
            dkv_ref[...] = jnp.concatenate([dk_acc[hh].T for hh in range(MEM_HEADS)]
                                           + [dv_acc[hh].T for hh in range(MEM_HEADS)], axis=1).astype(BF16)

    row = pl.BlockSpec((TILE, MEM_WIDTH), lambda i: (i, 0))
    return pl.pallas_call(
        body, name=name, grid=(nb,),
        in_specs=[pl.BlockSpec((TILE, MEM_WIDTH), lambda i: (i, cb)),
                  pl.BlockSpec((ml, MEM_WIDTH), lambda i: (0, 0)),
                  pl.BlockSpec((ml, MEM_WIDTH), lambda i: (0, 1)),
                  row, row,
                  pl.BlockSpec((TILE, MEM_WIDTH), lambda i: (i, db))],
        out_specs=[row, pl.BlockSpec((ml, 2 * MEM_WIDTH), lambda i: (0, 0))],
        out_shape=[jax.ShapeDtypeStruct((s, MEM_WIDTH), BF16), jax.ShapeDtypeStruct((ml, 2 * MEM_WIDTH), BF16)],
        scratch_shapes=[pltpu.VMEM((MEM_HEADS, HEAD_DIM, ml), F32), pltpu.VMEM((MEM_HEADS, HEAD_DIM, ml), F32)],
        compiler_params=_params("arbitrary"),
    )(pa, mkv, mkv, out, lse, dout)


def _head_maps():
    col = jnp.arange(MIX_WIDTH)[:, None] // HEAD_DIM
    g = (col == jnp.arange(LANES)[None, :]).astype(BF16)
    return g, g.T


def _normed_heads(osb_ref, ofx_ref, om_ref, g_ref, gt_ref):
    y = jnp.concatenate([osb_ref[...], ofx_ref[...], om_ref[...]], axis=1)
    msq = _sum_l2(y * y, g_ref[...]) * (1.0 / HEAD_DIM)
    rf = _sum_l3(lax.rsqrt(msq + EPS), gt_ref[...])
    return y * rf, rf


def _out_fwd(name, o_sb, o_fx, o_m, pb, ow, x, w_out, ts):
    s, d = x.shape
    g, gt = _head_maps()

    def body(osb_ref, ofx_ref, om_ref, gate_ref, ow_ref, x_ref, w_ref, g_ref, gt_ref, xo_ref, y2_ref):
        yh, _ = _normed_heads(osb_ref, ofx_ref, om_ref, g_ref, gt_ref)
        gate = gate_ref[...]
        y2 = (yh * ow_ref[...] * (gate * jax.nn.sigmoid(gate))).astype(BF16)
        y2_ref[...] = y2
        xo_ref[...] = x_ref[...] + _dot(y2, w_ref[...])

    return pl.pallas_call(
        body, name=name, grid=(s // ts,),
        in_specs=[_row_spec(ts, SB_WIDTH), _row_spec(ts, FOX_WIDTH), _row_spec(ts, MEM_WIDTH),
                  _row_spec(ts, MIX_WIDTH), _const_spec((1, MIX_WIDTH)), _row_spec(ts, d),
                  _const_spec((MIX_WIDTH, d)),
                  _const_spec((MIX_WIDTH, LANES)), _const_spec((LANES, MIX_WIDTH))],
        out_specs=[_row_spec(ts, d), _row_spec(ts, MIX_WIDTH)],
        out_shape=[jax.ShapeDtypeStruct((s, d), F32), jax.ShapeDtypeStruct((s, MIX_WIDTH), BF16)],
        compiler_params=_params("parallel"),
    )(o_sb, o_fx, o_m, pb, ow, x, w_out, g, gt)


def _row_spec(ts, w):
    return pl.BlockSpec((ts, w), lambda i: (i, 0))


def _const_spec(shape):
    return pl.BlockSpec(shape, lambda i: (0,) * len(shape))


def _out_bwd(name, dxb, o_sb, o_fx, o_m, pb, ow, w_out, ts):
    s, d = dxb.shape
    g, gt = _head_maps()

    def body(dx_ref, osb_ref, ofx_ref, om_ref, gate_ref, ow_ref, w_ref, g_ref, gt_ref, dy_ref, dgate_ref, dow_ref):
        @pl.when(pl.program_id(0) == 0)
        def _():
            dow_ref[...] = jnp.zeros_like(dow_ref)

        dy2 = _dot_nt(dx_ref[...], w_ref[...])
        yh, rf = _normed_heads(osb_ref, ofx_ref, om_ref, g_ref, gt_ref)
        gate = gate_ref[...]
        sig = jax.nn.sigmoid(gate)
        ow_v = ow_ref[...]
        dgate_ref[...] = (dy2 * (yh * ow_v) * (sig * (1.0 + gate * (1.0 - sig)))).astype(BF16)
        dn = dy2 * (gate * sig)
        dow_ref[...] += jnp.sum(dn * yh, axis=0, keepdims=True)
        dyh = dn * ow_v
        t = _sum_l2(dyh * yh, g_ref[...]) * (1.0 / HEAD_DIM)
        dy_ref[...] = rf * (dyh - yh * _sum_l3(t, gt_ref[...]))

    return pl.pallas_call(
        body, name=name, grid=(s // ts,),
        in_specs=[_row_spec(ts, d), _row_spec(ts, SB_WIDTH), _row_spec(ts, FOX_WIDTH), _row_spec(ts, MEM_WIDTH),
                  _row_spec(ts, MIX_WIDTH), _const_spec((1, MIX_WIDTH)),
                  _const_spec((MIX_WIDTH, d)),
                  _const_spec((MIX_WIDTH, LANES)), _const_spec((LANES, MIX_WIDTH))],
        out_specs=[_row_spec(ts, MIX_WIDTH), _row_spec(ts, MIX_WIDTH), _const_spec((1, MIX_WIDTH))],
        out_shape=[jax.ShapeDtypeStruct((s, MIX_WIDTH), F32), jax.ShapeDtypeStruct((s, PB), BF16),
                   jax.ShapeDtypeStruct((1, MIX_WIDTH), F32)],
        compiler_params=_params("arbitrary"),
    )(dxb, o_sb, o_fx, o_m, pb, ow, w_out, g, gt)


def _adamw(name, w, g, m, v, tr):
    def body(w_ref, g_ref, m_ref, v_ref, d_ref, m2_ref, v2_ref):
        gv = g_ref[...]
        m2 = ADAM_B1 * m_ref[...] + (1.0 - ADAM_B1) * gv
        v2 = ADAM_B2 * v_ref[...] + (1.0 - ADAM_B2) * (gv * gv)
        m_hat = m2 / (1.0 - ADAM_B1 ** ADAM_STEP)
        v_hat = v2 / (1.0 - ADAM_B2 ** ADAM_STEP)
        d_ref[...] = -ADAM_LR * (m_hat / (jnp.sqrt(v_hat) + ADAM_EPS) + ADAM_WD * w_ref[...])
        m2_ref[...] = m2
        v2_ref[...] = v2

    rest = w.shape[1:]
    spec = pl.BlockSpec((tr,) + rest, lambda i: (i,) + (0,) * len(rest))
    shp = jax.ShapeDtypeStruct(w.shape, F32)
    return pl.pallas_call(
        body, name=name, grid=(w.shape[0] // tr,), in_specs=[spec] * 4, out_specs=[spec] * 3, out_shape=[shp] * 3,
        compiler_params=_params("parallel"),
    )(w, g, m, v)


def _adamw_sharded(name, w, m, v, g_own, g_other, cvec, tr):
    depth, rows, cols = w.shape
    nt = rows // 2 // tr

    def body(c_ref, w_ref, m_ref, v_ref, *rest):
        g_refs, (g_ref, d_ref, m2_ref, v2_ref) = rest[:2 * depth], rest[2 * depth:]
        layer, mine = pl.program_id(0), pl.program_id(1) == c_ref[0]
        gv = None
        for lt in range(depth):
            cand = jnp.where(mine, g_refs[lt][...], g_refs[depth + lt][...])
            gv = cand if gv is None else jnp.where(layer == lt, cand, gv)
        m2 = ADAM_B1 * m_ref[...] + (1.0 - ADAM_B1) * gv
        v2 = ADAM_B2 * v_ref[...] + (1.0 - ADAM_B2) * (gv * gv)
        m_hat = m2 / (1.0 - ADAM_B1 ** ADAM_STEP)
        v_hat = v2 / (1.0 - ADAM_B2 ** ADAM_STEP)
        g_ref[...] = gv
        d_ref[...] = -ADAM_LR * (m_hat / (jnp.sqrt(v_hat) + ADAM_EPS) + ADAM_WD * w_ref[...])
        m2_ref[...] = m2
        v2_ref[...] = v2

    def g_map(lt, own):
        def index(l, hf, i, c_ref):
            use = jnp.logical_and(l == lt, (hf == c_ref[0]) == own)
            return jnp.where(use, i, 0), 0
        return index

    full = pl.BlockSpec((None, tr, cols), lambda l, hf, i, c_ref: (l, hf * nt + i, 0))
    g_specs = [pl.BlockSpec((tr, cols), g_map(lt, own)) for own in (True, False) for lt in range(depth)]
    shp = jax.ShapeDtypeStruct((depth, rows, cols), F32)
    return pl.pallas_call(
        body, name=name,
        grid_spec=pltpu.PrefetchScalarGridSpec(
            num_scalar_prefetch=1, grid=(depth, 2, nt), in_specs=[full] * 3 + g_specs, out_specs=[full] * 4),
        out_shape=[shp] * 4,
        compiler_params=_params("arbitrary", "arbitrary", "arbitrary"),
    )(cvec, w, m, v, *g_own, *g_other)


HBM_SPEC = pl.BlockSpec(memory_space=pltpu.HBM)


def _place():
    x, y, c = lax.axis_index("x"), lax.axis_index("y"), lax.axis_index("c")
    chips = [(1 - x, y), (x, 1 - y), (1 - x, 1 - y)]
    return x, y, c, chips


def _remote(src, dst, send_sems, recv_sems, k, to):
    return pltpu.make_async_remote_copy(src_ref=src, dst_ref=dst, send_sem=send_sems.at[k], recv_sem=recv_sems.at[k],
                                        device_id=to, device_id_type=MESH)


def _half_rows(n_rows, cc):
    rh = n_rows // 2
    return pl.ds(pl.multiple_of(cc * rh, 16), rh)


def _dma_sems(n):
    return [pltpu.SemaphoreType.DMA((n,)), pltpu.SemaphoreType.DMA((n,))]


class _Exchange:
    def __init__(self, inputs, out_shapes, n_sems, begin, relay, finish):
        self.inputs, self.out_shapes, self.n_sems = list(inputs), list(out_shapes), n_sems
        self.begin, self.relay, self.finish = begin, relay, finish

    @property
    def n(self):
        return len(self.inputs)

    def split(self, refs):
        return refs[:self.n], refs[self.n:2 * self.n], refs[2 * self.n], refs[2 * self.n + 1]


def _run_exchange(name, ex):
    def body(*refs):
        parts = ex.split(refs)
        for phase in (ex.begin, ex.relay, ex.finish):
            if phase is not None:
                phase(*parts)

    return pl.pallas_call(
        body, name=name, in_specs=[HBM_SPEC] * ex.n, out_specs=[HBM_SPEC] * ex.n, out_shape=ex.out_shapes,
        scratch_shapes=_dma_sems(ex.n_sems),
    )(*ex.inputs)


def _gather_exchange(shards):
    def ici(in_refs, out_refs, send_sems, recv_sems):
        x, y, c, chips = _place()
        return [_remote(in_ref.at[_half_rows(in_ref.shape[0], c)], out_ref.at[2 * x + y, _half_rows(in_ref.shape[0], c)],
                        send_sems, recv_sems, 6 * a + j, (cx, cy, c))
                for a, (in_ref, out_ref) in enumerate(zip(in_refs, out_refs)) for j, (cx, cy) in enumerate(chips)]

    def d2d(out_refs, send_sems, recv_sems, half_of):
        x, y, c, chips = _place()
        cps = []
        for a, out_ref in enumerate(out_refs):
            for j, (cx, cy) in enumerate(chips):
                piece = out_ref.at[2 * cx + cy, _half_rows(out_ref.shape[1], half_of(c))]
                cps.append(_remote(piece, piece, send_sems, recv_sems, 6 * a + 3 + j, (x, y, 1 - c)))
        return cps

    def begin(in_refs, out_refs, send_sems, recv_sems):
        for cp in ici(in_refs, out_refs, send_sems, recv_sems):
            cp.start()

    def relay(in_refs, out_refs, send_sems, recv_sems):
        x, y, c, chips = _place()
        for a, out_ref in enumerate(out_refs):
            for j, (cx, cy) in enumerate(chips):
                landed = out_ref.at[2 * cx + cy, _half_rows(out_ref.shape[1], c)]
                _remote(landed, landed, send_sems, recv_sems, 6 * a + j, (cx, cy, c)).wait_recv()
        for cp in d2d(out_refs, send_sems, recv_sems, lambda c_: c_):
            cp.start()

    def finish(in_refs, out_refs, send_sems, recv_sems):
        for cp in d2d(out_refs, send_sems, recv_sems, lambda c_: 1 - c_):
            cp.wait_recv()
        for cp in ici(in_refs, out_refs, send_sems, recv_sems) + d2d(out_refs, send_sems, recv_sems, lambda c_: c_):
            cp.wait_send()

    shapes = [jax.ShapeDtypeStruct((N_CHIPS,) + s_.shape, s_.dtype) for s_ in shards]
    return _Exchange(shards, shapes, 6 * len(shards), begin, relay, finish)


def _swap_exchange(g4s):
    def copies(in_refs, out_refs, send_sems, recv_sems):
        x, y, c, _ = _place()
        return [_remote(in_ref.at[:, _half_rows(in_ref.shape[1], 1 - c), :], out_ref, send_sems, recv_sems, a, (x, y, 1 - c))
                for a, (in_ref, out_ref) in enumerate(zip(in_refs, out_refs))]

    def begin(*parts):
        for cp in copies(*parts):
            cp.start()

    def finish(*parts):
        for cp in copies(*parts):
            cp.wait()

    shapes = [jax.ShapeDtypeStruct((g.shape[0], g.shape[1] // 2, g.shape[2]), g.dtype) for g in g4s]
    return _Exchange(g4s, shapes, len(g4s), begin, None, finish)


def _add_half(name, g4, r1, cvec, tr):
    n, r, w = g4.shape
    rh = r // 2
    nblk = rh // tr

    def body(c_ref, a_ref, b_ref, o_ref):
        o_ref[...] = (a_ref[...].astype(F32) + b_ref[...].astype(F32)).astype(BF16)

    return pl.pallas_call(
        body, name=name,
        grid_spec=pltpu.PrefetchScalarGridSpec(
            num_scalar_prefetch=1, grid=(n, nblk),
            in_specs=[pl.BlockSpec((None, tr, w), lambda k, i, c_ref: (k, c_ref[0] * nblk + i, 0)),
                      pl.BlockSpec((None, tr, w), lambda k, i, c_ref: (k, i, 0))],
            out_specs=pl.BlockSpec((None, tr, w), lambda k, i, c_ref: (k, i, 0))),
        out_shape=jax.ShapeDtypeStruct((n, rh, w), BF16),
        compiler_params=_params("parallel", "parallel"),
    )(cvec, g4, r1)


def _scatter_exchange(h4s):
    def sends(in_refs, out_refs, send_sems, recv_sems):
        x, y, c, chips = _place()
        return [_remote(in_ref.at[2 * cx + cy], out_ref.at[j], send_sems, recv_sems, 3 * a + j, (cx, cy, c))
                for a, (in_ref, out_ref) in enumerate(zip(in_refs, out_refs)) for j, (cx, cy) in enumerate(chips)]

    def begin(*parts):
        for cp in sends(*parts):
            cp.start()

    def finish(in_refs, out_refs, send_sems, recv_sems):
        x, y, c, chips = _place()
        for a, out_ref in enumerate(out_refs):
            for j, (cx, cy) in enumerate(chips):
                got = out_ref.at[j]
                _remote(got, got, send_sems, recv_sems, 3 * a + j, (cx, cy, c)).wait_recv()
        for cp in sends(in_refs, out_refs, send_sems, recv_sems):
            cp.wait_send()

    shapes = [jax.ShapeDtypeStruct((3,) + h.shape[1:], h.dtype) for h in h4s]
    return _Exchange(h4s, shapes, 3 * len(h4s), begin, None, finish)


def _sum_chips(name, h4, r3, mvec, tr):
    _, rh, w = h4.shape

    def body(m_ref, a_ref, b_ref, c_ref, d_ref, o_ref):
        o_ref[...] = ((a_ref[...].astype(F32) + b_ref[...].astype(F32)) + c_ref[...].astype(F32)) + d_ref[...].astype(F32)

    specs = [pl.BlockSpec((None, tr, w), lambda i, m_ref: (m_ref[0], i, 0))]
    specs += [pl.BlockSpec((None, tr, w), functools.partial(lambda k, i, m_ref: (k, i, 0), k)) for k in range(3)]
    return pl.pallas_call(
        body, name=name,
        grid_spec=pltpu.PrefetchScalarGridSpec(
            num_scalar_prefetch=1, grid=(rh // tr,), in_specs=specs,
            out_specs=pl.BlockSpec((tr, w), lambda i, m_ref: (i, 0))),
        out_shape=jax.ShapeDtypeStruct((rh, w), F32),
        compiler_params=_params("parallel"),
    )(mvec, h4, r3, r3, r3)


def _swap_reduced_exchange(ghs):
    def copies(in_refs, out_refs, send_sems, recv_sems):
        x, y, c, _ = _place()
        return [_remote(in_ref, out_ref, send_sems, recv_sems, a, (x, y, 1 - c))
                for a, (in_ref, out_ref) in enumerate(zip(in_refs, out_refs))]

    def begin(*parts):
        for cp in copies(*parts):
            cp.start()

    def finish(*parts):
        for cp in copies(*parts):
            cp.wait()

    return _Exchange(ghs, [jax.ShapeDtypeStruct(g.shape, g.dtype) for g in ghs], len(ghs), begin, None, finish)


class _SemaphoresFrom:
    def __init__(self, sems, first):
        self.sems, self.first = sems, first

    @property
    def at(self):
        return self

    def __getitem__(self, k):
        return self.sems.at[self.first + k]


def _both(a, b):
    def phase(fa, fb):
        if fa is None and fb is None:
            return None

        def run(in_refs, out_refs, send_sems, recv_sems):
            if fa is not None:
                fa(in_refs[:a.n], out_refs[:a.n], send_sems, recv_sems)
            if fb is not None:
                fb(in_refs[a.n:], out_refs[a.n:], _SemaphoresFrom(send_sems, a.n_sems), _SemaphoresFrom(recv_sems, a.n_sems))

        return run

    return _Exchange(a.inputs + b.inputs, a.out_shapes + b.out_shapes, a.n_sems + b.n_sems,
                     phase(a.begin, b.begin), phase(a.relay, b.relay), phase(a.finish, b.finish))


def _small_update(name, partials, weights, moments1, moments2):
    n = len(partials)
    width = max(p.shape[1] for p in partials)
    starts, at = [], 0
    for p in partials:
        starts.append(at)
        at += p.shape[0]
    rows = -(-at // 8) * 8
    has_w = [w is not None for w in weights]
    n_w = sum(has_w)

    def body(*refs):
        p_refs = refs[:n]
        w_refs, m_refs, v_refs = refs[n:n + n_w], refs[n + n_w:n + 2 * n_w], refs[n + 2 * n_w:n + 3 * n_w]
        outs = refs[n + 3 * n_w:-4]
        g_refs, upd_refs = outs[:n], outs[n:]
        vec, buf, send_sems, recv_sems = refs[-4:]
        x, y, c, _ = _place()
        me = 4 * x + 2 * y + c
        vec[...] = jnp.zeros_like(vec)
        for p_ref, r0 in zip(p_refs, starts):
            vec[r0:r0 + p_ref.shape[0], 0:p_ref.shape[1]] = p_ref[...]
        buf[me] = vec[...]
        flips = [(fx, fy, fc) for fx in (0, 1) for fy in (0, 1) for fc in (0, 1)][1:]
        peers = [(x + fx - 2 * x * fx, y + fy - 2 * y * fy, c + fc - 2 * c * fc) for fx, fy, fc in flips]
        sends = [_remote(vec, buf.at[me], send_sems, recv_sems, k, peer) for k, peer in enumerate(peers)]
        for cp in sends:
            cp.start()
        for k, (px, py, pc) in enumerate(peers):
            got = buf.at[4 * px + 2 * py + pc]
            _remote(got, got, send_sems, recv_sems, k, (px, py, pc)).wait_recv()
        for cp in sends:
            cp.wait_send()
        total = buf[0]
        for dev in range(1, N_DEV):
            total = total + buf[dev]
        k = 0
        for a in range(n):
            r, w = g_refs[a].shape
            g = total[starts[a]:starts[a] + r, 0:w]
            g_refs[a][...] = g
            if has_w[a]:
                m2 = ADAM_B1 * m_refs[k][...] + (1.0 - ADAM_B1) * g
                v2 = ADAM_B2 * v_refs[k][...] + (1.0 - ADAM_B2) * (g * g)
                m_hat = m2 / (1.0 - ADAM_B1 ** ADAM_STEP)
                v_hat = v2 / (1.0 - ADAM_B2 ** ADAM_STEP)
                upd_refs[3 * k][...] = -ADAM_LR * (m_hat / (jnp.sqrt(v_hat) + ADAM_EPS) + ADAM_WD * w_refs[k][...])
                upd_refs[3 * k + 1][...] = m2
                upd_refs[3 * k + 2][...] = v2
                k += 1

    ws = [w for w in weights if w is not None]
    g_shapes = [jax.ShapeDtypeStruct(p.shape if w is None else w.shape, F32) for p, w in zip(partials, weights)]
    u_shapes = [jax.ShapeDtypeStruct(w.shape, F32) for w in ws for _ in range(3)]
    vm = pl.BlockSpec(memory_space=pltpu.VMEM)
    n_args = n + 3 * n_w
    outs = pl.pallas_call(
        body, name=name, in_specs=[vm] * n_args, out_specs=[vm] * (n + 3 * n_w), out_shape=g_shapes + u_shapes,
        scratch_shapes=[pltpu.VMEM((rows, width), F32), pltpu.VMEM((N_DEV, rows, width), F32),
                        pltpu.SemaphoreType.DMA((7,)), pltpu.SemaphoreType.DMA((7,))],
    )(*partials, *ws, *[m for m in moments1 if m is not None], *[v for v in moments2 if v is not None])
    return outs[:n], outs[n:]


GATE_COL = 3 * SB_WIDTH + 3 * FOX_WIDTH + FOX_HEADS + MEM_WIDTH
FL_COL = QKV_WIDTH


GROUP_A_COLS = [(0, QKV_WIDTH), (FL_COL + FOX_HEADS, MEM_WIDTH)]
GROUP_B_COLS = [(GATE_COL, MIX_WIDTH), (FL_COL, FOX_HEADS)]


def _group_from_shards(shard_of, cw, spans, pad):
    parts = []
    for lo, width in spans:
        hi = lo + width
        for j in range(N_CHIPS):
            a, b = max(lo, j * cw), min(hi, (j + 1) * cw)
            if a < b:
                parts.append(shard_of(j)[:, a - j * cw:b - j * cw])
    if pad:
        parts.append(jnp.zeros((parts[0].shape[0], pad), parts[0].dtype))
    return jnp.concatenate(parts, axis=1)


def _shard_from_groups(ga, gb, j, cw):
    lo, hi = j * cw, (j + 1) * cw
    placed = []
    for grp, spans in ((ga, GROUP_A_COLS), (gb, GROUP_B_COLS)):
        at = 0
        for first, width in spans:
            a, b = max(lo, first), min(hi, first + width)
            if a < b:
                placed.append((a, grp[:, at + a - first:at + b - first]))
            at += width
    return jnp.concatenate([p for _, p in sorted(placed, key=lambda t: t[0])], axis=1)


def _tile_of(n, cap, unit):
    if n <= cap:
        return n
    best = None
    for t in range(unit, cap + 1, unit):
        if n % t == 0:
            best = t
    assert best is not None, (n, cap, unit)
    return best


def _column_major_rows(a):
    dp, r, c = a.shape
    return a.transpose(2, 0, 1).reshape(c, dp, r // LANES, LANES).transpose(0, 2, 1, 3).reshape(-1, 8, LANES)


def _from_column_major_rows(b, shape):
    dp, r, c = shape
    return b.reshape(c, r // LANES, dp, LANES).transpose(0, 2, 1, 3).reshape(c, dp, r).transpose(1, 2, 0)


def kernel(x, mem, norm_w, w_in, b_forget, mem_norm_w, w_mem_kv, out_norm_w, w_out, final_norm_w, loss_target, m_norm_w, m_w_in, m_b_forget, m_mem_norm_w, m_w_mem_kv, m_out_norm_w, m_w_out, m_final_norm_w, v_norm_w, v_w_in, v_b_forget, v_mem_norm_w, v_w_mem_kv, v_out_norm_w, v_w_out, v_final_norm_w):
    xs = x[0]
    mems = mem[0]
    target = loss_target[0]
    s, d = xs.shape
    depth = norm_w.shape[0]
    nb = s // TILE
    ts = _tile_of(s, 512, 8)
    big = (w_in, w_mem_kv, w_out)
    core = lax.axis_index("c")
    chip = 2 * lax.axis_index("x") + lax.axis_index("y")
    cvec = core.astype(jnp.int32).reshape(1)
    mvec = chip.astype(jnp.int32).reshape(1)
    cw = w_in.shape[2]

    own_w = [[a[l].astype(BF16) for a in big] for l in range(depth)]

    def lay_out_in(own, got):
        shard_of = lambda j: jnp.where(chip == j, own, got[j])
        return (_group_from_shards(shard_of, cw, GROUP_A_COLS, 0),
                _group_from_shards(shard_of, cw, GROUP_B_COLS, LANES - FOX_HEADS))

    def lay_out_rows(own, got):
        full = jnp.where(lax.broadcasted_iota(jnp.int32, got.shape, 0) == chip, own[None], got)
        return full.reshape(-1, full.shape[2])

    w_in_groups = [lay_out_in(own_w[0][0], _run_exchange("gather_weights0", _gather_exchange(own_w[0][:1]))[0])]
    layer_w = []

    tm = _tile_of(s, 256, 8)
    fl_block = MIX_WIDTH // LANES

    saved = []
    cur = xs
    for l in range(depth):
        wa, wb = w_in_groups[l]
        h = _rms_fwd(f"rms_fwd{l}", cur, norm_w[l][None], ts)
        pa = _mm(f"inproj_a{l}", h, wa, "nn", tm, _tile_of(PA, 1664, LANES), BF16)
        pb = _mm(f"inproj_b{l}", h, wb, "nn", tm, PB, F32)
        bpad = jnp.pad(b_forget[l], (0, LANES - FOX_HEADS))[None]
        ccol4, crow4 = _gate_fwd(f"gate_fwd{l}", pb, bpad, fl_block)
        more = l + 1 < depth
        o_sb, got = _sb_fwd(f"sb_fwd{l}", pa, 0, carried=_gather_exchange(own_w[l][1:]))
        wkv, wout = lay_out_rows(own_w[l][1], got[0]), lay_out_rows(own_w[l][2], got[1])
        layer_w.append((wa, wb, wkv, wout))
        o_fx, lse_fx, got = _fox_fwd(f"fox_fwd{l}", pa, 3 * SB_WIDTH, ccol4, crow4,
                                     carried=_gather_exchange(own_w[l + 1][:1]) if more else None)
        if more:
            w_in_groups.append(lay_out_in(own_w[l + 1][0], got[0]))
        mn = _rms_fwd(f"mem_rms{l}", mems, mem_norm_w[l][None], mems.shape[0])
        mkv = _mm(f"mem_kv{l}", mn, wkv, "nn", mems.shape[0], 2 * MEM_WIDTH, BF16)
        o_m, lse_m = _mem_fwd(f"mem_fwd{l}", pa, mkv)
        nxt, y2 = _out_fwd(f"out_fwd{l}", o_sb, o_fx, o_m, pb, out_norm_w[l][None], cur, wout, tm)
        saved.append((cur, h, pa, pb, bpad, ccol4, crow4, o_sb, o_fx, lse_fx, mn, mkv, o_m, lse_m, y2))
        cur = nxt

    loss_v, dx, dxb, g_final = _final_loss("final_loss", cur, final_norm_w[None], target, ts)

    g_norm, g_b, g_memnorm, g_outnorm = [None] * depth, [None] * depth, [None] * depth, [None] * depth
    g_wa, g_wb, g_wkv, g_wout = [None] * depth, [None] * depth, [None] * depth, [None] * depth
    g_own = [[None] * depth for _ in big]
    g_other = [[None] * depth for _ in big]

    def swap_of(jobs):
        return _swap_exchange([g for _, _, g, _ in jobs])

    def chip_sums(jobs, got):
        return [(lr, k, _add_half(f"grad_add_half{lr}_{k}", g, r_, cvec, t_), t_) for (lr, k, g, t_), r_ in zip(jobs, got)]

    def sum_at_owner(jobs, from_chips):
        return [_sum_chips(f"grad_sum_chips{lr}_{k}", h_, r_, mvec, t_) for (lr, k, h_, t_), r_ in zip(jobs, from_chips)]

    def keep(jobs, halves, others):
        for (lr, k, _, _), mine, other in zip(jobs, halves, others):
            g_own[k][lr], g_other[k][lr] = mine, other

    def job(lr, k, g4):
        return lr, k, g4, _tile_of(g4.shape[1] // 2, 256, 16)

    pending = []
    for l in reversed(range(depth)):
        xin, h, pa, pb, bpad, ccol4, crow4, o_sb, o_fx, lse_fx, mn, mkv, o_m, lse_m, y2 = saved[l]
        wa, wb, wkv, wout = layer_w[l]
        dy, dgate, g_outnorm[l] = _out_bwd(f"out_bwd{l}", dxb, o_sb, o_fx, o_m, pb, out_norm_w[l][None], wout, tm)
        g_wout[l] = _mm(f"dw_out{l}", y2, dxb, "tn", _tile_of(MIX_WIDTH, 640, LANES), d, F32)
        dq_m, dmkv = _mem_bwd(f"mem_bwd{l}", pa, mkv, o_m, lse_m, dy, SB_WIDTH + FOX_WIDTH)
        g_wkv[l] = _mm(f"dw_kv{l}", mn, dmkv, "tn", d, 2 * MEM_WIDTH, F32)
        dmn = _mm(f"dmem{l}", dmkv, wkv, "nt", mems.shape[0], d, F32)
        g_memnorm[l] = _rms_wgrad(f"mem_norm_grad{l}", mems, dmn)
        small = [job(l, 1, g_wkv[l].reshape(N_CHIPS, -1, g_wkv[l].shape[1])), job(l, 2, g_wout[l].reshape(N_CHIPS, -1, d))]
        (dq_fx, dk_fx, dv_fx, cs4), got = _fox_bwd(f"fox_bwd{l}", pa, 3 * SB_WIDTH, ccol4, crow4, o_fx, lse_fx, dy,
                                                    SB_WIDTH, carried=swap_of(small))
        pending += chip_sums(small, got)
        (dq_sb, dk_sb, dv_sb), from_chips = _sb_bwd(f"sb_bwd{l}", pa, 0, dy, 0,
                                                   carried=_scatter_exchange([j[2] for j in pending]))
        reduced_jobs, reduced = pending, sum_at_owner(pending, from_chips)
        swap_back = _swap_reduced_exchange(reduced)
        dpb, g_b[l] = _gate_bwd(f"gate_bwd{l}", pb, bpad, cs4, fl_block, dgate)
        dpa = jnp.concatenate([dq_sb, dk_sb, dv_sb, dq_fx, dk_fx, dv_fx, dq_m], axis=1)
        tw = _tile_of(d, 512, LANES)
        g_wa[l] = _mm(f"dw_in_a{l}", h, dpa, "tn", tw, _tile_of(PA, 1664, LANES), BF16)
        g_wb[l] = _mm(f"dw_in_b{l}", h, dpb, "tn", tw, PB, BF16)
        g4_in = jnp.stack([_shard_from_groups(g_wa[l], g_wb[l], j, cw) for j in range(N_CHIPS)])
        w_in_job = [job(l, 0, g4_in)]
        if l > 0:
            dx, dxb, g_norm[l], got = _inproj_bwd(f"inproj_bwd{l}", dpa, dpb, wa, wb, xin, norm_w[l][None], dx, tm,
                                                  carried=_both(swap_of(w_in_job), swap_back))
            pending = chip_sums(w_in_job, got[:1])
            keep(reduced_jobs, reduced, got[1:])
        else:
            pending = chip_sums(w_in_job, _run_exchange("grad_swap_halves_last", swap_of(w_in_job)))
            dx, dxb, g_norm[l], got = _inproj_bwd(f"inproj_bwd{l}", dpa, dpb, wa, wb, xin, norm_w[l][None], dx, tm,
                                                  carried=_both(_scatter_exchange([j[2] for j in pending]), swap_back))
            keep(reduced_jobs, reduced, got[1:])
            last = sum_at_owner(pending, got[:1])
            keep(pending, last, _run_exchange("grad_swap_reduced_last", _swap_reduced_exchange(last)))

    small_w = [norm_w, b_forget, mem_norm_w, out_norm_w, final_norm_w]
    small_m = [m_norm_w, m_b_forget, m_mem_norm_w, m_out_norm_w, m_final_norm_w]
    small_v = [v_norm_w, v_b_forget, v_mem_norm_w, v_out_norm_w, v_final_norm_w]
    rows2 = lambda a: a.reshape(-1, a.shape[-1])
    partials = [jnp.concatenate(g_norm, axis=0), jnp.concatenate(g_b, axis=0), jnp.concatenate(g_memnorm, axis=0),
                jnp.concatenate(g_outnorm, axis=0), g_final, loss_v]
    sums, updates = _small_update("small_update", partials, [rows2(a) for a in small_w] + [None],
                                  [rows2(a) for a in small_m] + [None], [rows2(a) for a in small_v] + [None])
    small_grads = [g.reshape(a.shape) for g, a in zip(sums, small_w)]
    loss = sums[-1][0, 0]
    small_delta, small_m2, small_v2 = ([updates[3 * k + t].reshape(a.shape) for k, a in enumerate(small_w)]
                                       for t in range(3))
    big_grads, big_delta, big_m2, big_v2 = [], [], [], []
    for k, (nm, w_, m_, v_) in enumerate(zip(("w_in", "w_mem_kv", "w_out"), big, (m_w_in, m_w_mem_kv, m_w_out),
                                             (v_w_in, v_w_mem_kv, v_w_out))):
        if w_.shape[2] % LANES:
            g_full = jnp.stack([jnp.concatenate([jnp.where(core == 0, go, gt), jnp.where(core == 0, gt, go)], axis=0)
                                for go, gt in zip(g_own[k], g_other[k])])
            w_p, g_p, m_p, v_p = (_column_major_rows(a) for a in (w_, g_full, m_, v_))
            outs = _adamw(f"adamw_{nm}", w_p, g_p, m_p, v_p, _tile_of(w_p.shape[0], 600, 1))
            outs = [_from_column_major_rows(o, w_.shape) for o in (g_p, *outs)]
        else:
            outs = _adamw_sharded(f"adamw_{nm}", w_, m_, v_, g_own[k], g_other[k], cvec,
                                  _tile_of(w_.shape[1] // 2, 256, 8))
        for lst, o in zip((big_grads, big_delta, big_m2, big_v2), outs):
            lst.append(o)

    def order(sm, bg):
        return [sm[0], bg[0], sm[1], sm[2], bg[1], sm[3], bg[2], sm[4]]

    return (loss, dx[None], *order(small_grads, big_grads), *order(small_delta, big_delta),
            *order(small_m2, big_m2), *order(small_v2, big_v2))
```

```python
import functools

import jax
import jax.numpy as jnp
from jax import lax
from jax.experimental import pallas as pl
from jax.experimental.pallas import tpu as pltpu

F32 = jnp.float32
BF16 = jnp.bfloat16

HEAD_DIM = 64
SB_WIDTH = 512
FOX_WIDTH = 512
FOX_HEADS = 8
MEM_WIDTH = 256
MEM_HEADS = MEM_WIDTH // HEAD_DIM
MIX_WIDTH = SB_WIDTH + FOX_WIDTH + MEM_WIDTH
TOTAL_HEADS = MIX_WIDTH // HEAD_DIM
IN_WIDTH = 3 * SB_WIDTH + 3 * FOX_WIDTH + FOX_HEADS + MEM_WIDTH + MIX_WIDTH
LANES = 128
QKV_WIDTH = 3 * SB_WIDTH + 3 * FOX_WIDTH
PA = QKV_WIDTH + MEM_WIDTH
PB = LANES + MIX_WIDTH
EPS = 1e-6
SCALE = HEAD_DIM ** -0.5
TILE = 256
SB_GROUP = 4
SB_LANES = SB_GROUP * HEAD_DIM
SB_FWD_GROUP = 8
FOX_GROUP = 4
FOX_LANES = FOX_GROUP * HEAD_DIM
FOX_FWD_GROUP = 4
NEG_INF = float("-inf")
MASKED = -1e30

ADAM_LR = 0.001
ADAM_B1 = 0.9
ADAM_B2 = 0.999
ADAM_EPS = 1e-08
ADAM_WD = 0.01
ADAM_STEP = 10

N_CHIPS = 4
N_DEV = 8
VMEM_LIMIT = 48 * 1024 * 1024
MESH = pl.DeviceIdType.MESH


def _params(*sem):
    return pltpu.CompilerParams(dimension_semantics=tuple(sem), vmem_limit_bytes=VMEM_LIMIT)


def _dot(a, b):
    return jnp.dot(a, b, preferred_element_type=F32)


def _dot_nt(a, b):
    return lax.dot_general(a, b, (((1,), (1,)), ((), ())), preferred_element_type=F32)


def _dot_tn(a, b):
    return lax.dot_general(a, b, (((0,), (0,)), ((), ())), preferred_element_type=F32)


def _split2(x):
    hi = x.astype(BF16)
    lo = (x - hi.astype(F32)).astype(BF16)
    return hi, lo


def _split3(x):
    hi = x.astype(BF16)
    r = x - hi.astype(F32)
    mid = r.astype(BF16)
    lo = (r - mid.astype(F32)).astype(BF16)
    return hi, mid, lo


def _sum_l2(x, u):
    hi, lo = _split2(x)
    return _dot(hi, u) + _dot(lo, u)


def _sum_l3(x, u):
    hi, mid, lo = _split3(x)
    return _dot(hi, u) + _dot(mid, u) + _dot(lo, u)


def _sum_r3(u, x):
    hi, mid, lo = _split3(x)
    return _dot(u, hi) + _dot(u, mid) + _dot(u, lo)


def _tri(n, pred):
    r = lax.broadcasted_iota(jnp.int32, (n, n), 0)
    c = lax.broadcasted_iota(jnp.int32, (n, n), 1)
    return jnp.where(pred(r, c), 1.0, 0.0).astype(BF16)


def _rows(ref, j, n=TILE):
    return pl.ds(pl.multiple_of(j * n, n), n)


def _mm(name, a, b, mode, tm, tn, out_dtype, res=None, a_lead=(), b_lead=()):
    a2, b2 = a.shape[len(a_lead):], b.shape[len(b_lead):]
    if mode == "tn":
        k, m = a2
    else:
        m, k = a2
    n = b2[0] if mode == "nt" else b2[1]
    assert m % tm == 0 and n % tn == 0, (name, m, tm, n, tn)
    na, nb = (None,) * len(a_lead), (None,) * len(b_lead)
    if mode == "tn":
        a_spec = pl.BlockSpec(na + (k, tm), lambda j, i: a_lead + (0, i))
    else:
        a_spec = pl.BlockSpec(na + (tm, k), lambda j, i: a_lead + (i, 0))
    if mode == "nt":
        b_spec = pl.BlockSpec(nb + (tn, k), lambda j, i: b_lead + (j, 0))
    else:
        b_spec = pl.BlockSpec(nb + (k, tn), lambda j, i: b_lead + (0, j))
    o_spec = pl.BlockSpec((tm, tn), lambda j, i: (i, j))
    dot = {"nn": _dot, "nt": _dot_nt, "tn": _dot_tn}[mode]

    def body(a_ref, b_ref, *rest):
        o_ref = rest[-1]
        acc = dot(a_ref[...].astype(BF16), b_ref[...].astype(BF16))
        if res is not None:
            acc = acc + rest[0][...]
        o_ref[...] = acc.astype(o_ref.dtype)

    args, specs = [a, b], [a_spec, b_spec]
    if res is not None:
        args.append(res)
        specs.append(o_spec)
    return pl.pallas_call(
        body, name=name, grid=(n // tn, m // tm), in_specs=specs, out_specs=o_spec,
        out_shape=jax.ShapeDtypeStruct((m, n), out_dtype),
        compiler_params=_params("parallel", "parallel"),
    )(*args)


def _rms_fwd(name, x, g, ts):
    s, d = x.shape

    def body(x_ref, g_ref, o_ref):
        xf = x_ref[...]
        r = lax.rsqrt(jnp.mean(xf * xf, axis=1, keepdims=True) + EPS)
        o_ref[...] = (xf * r * g_ref[...]).astype(BF16)

    return pl.pallas_call(
        body, name=name, grid=(s // ts,),
        in_specs=[pl.BlockSpec((ts, d), lambda i: (i, 0)), pl.BlockSpec((1, d), lambda i: (0, 0))],
        out_specs=pl.BlockSpec((ts, d), lambda i: (i, 0)),
        out_shape=jax.ShapeDtypeStruct((s, d), BF16),
        compiler_params=_params("parallel"),
    )(x, g)


def _inproj_bwd(name, dpa, dpb, wa, wb, x, g, dres, ts, carried=None):
    s, d = x.shape

    def body(dpa_ref, dpb_ref, wa_ref, wb_ref, x_ref, g_ref, dres_ref, dx_ref, dxb_ref, dg_ref):
        @pl.when(pl.program_id(1) == 0)
        def _():
            dg_ref[...] = jnp.zeros_like(dg_ref)

        dhf = _dot_nt(dpa_ref[...], wa_ref[...]) + _dot_nt(dpb_ref[...], wb_ref[...])
        xf = x_ref[...]
        r = lax.rsqrt(jnp.mean(xf * xf, axis=1, keepdims=True) + EPS)
        xh = xf * r
        dg_ref[...] += jnp.sum(dhf * xh, axis=0, keepdims=True)
        dxh = dhf * g_ref[...]
        m = jnp.mean(dxh * xh, axis=1, keepdims=True)
        dx = r * (dxh - xh * m) + dres_ref[...]
        dx_ref[...] = dx
        dxb_ref[...] = dx.astype(BF16)

    row = lambda w: pl.BlockSpec((ts, w), lambda p, i: (i, 0))
    whole = lambda a: pl.BlockSpec(a.shape, lambda p, i: (0, 0))
    outs = _pair_grid_call(
        name, body, s // ts,
        in_specs=[row(dpa.shape[1]), row(dpb.shape[1]), whole(wa), whole(wb), row(d), whole(g), row(d)],
        out_specs=[row(d), row(d), pl.BlockSpec((1, d), lambda p, i: (0, 0))],
        out_shape=[jax.ShapeDtypeStruct((s, d), F32), jax.ShapeDtypeStruct((s, d), BF16),
                   jax.ShapeDtypeStruct((1, d), F32)],
        scratch=[], args=(dpa, dpb, wa, wb, x, g, dres), carried=carried, groups=1)
    return outs[0], outs[1], outs[2], outs[3:]


def _rms_wgrad(name, x, dh):
    m_, d = x.shape

    def body(x_ref, dh_ref, dg_ref):
        xf = x_ref[...]
        r = lax.rsqrt(jnp.mean(xf * xf, axis=1, keepdims=True) + EPS)
        dg_ref[...] = jnp.sum(dh_ref[...] * xf * r, axis=0, keepdims=True)

    return pl.pallas_call(
        body, name=name, out_shape=jax.ShapeDtypeStruct((1, d), F32),
    )(x, dh)


def _final_loss(name, x, g, target, ts):
    s, d = x.shape

    def body(x_ref, g_ref, t_ref, loss_ref, dx_ref, dxb_ref, dg_ref):
        @pl.when(pl.program_id(0) == 0)
        def _():
            dg_ref[...] = jnp.zeros_like(dg_ref)
            loss_ref[...] = jnp.zeros_like(loss_ref)

        xf = x_ref[...]
        gw = g_ref[...]
        r = lax.rsqrt(jnp.mean(xf * xf, axis=1, keepdims=True) + EPS)
        xh = xf * r
        e = xh * gw - t_ref[...]
        part = 0.5 * jnp.sum(jnp.mean(e * e, axis=1, keepdims=True), axis=0, keepdims=True)
        loss_ref[...] += jnp.broadcast_to(part, loss_ref.shape)
        dy = e * (1.0 / d)
        dg_ref[...] += jnp.sum(dy * xh, axis=0, keepdims=True)
        dxh = dy * gw
        m = jnp.mean(dxh * xh, axis=1, keepdims=True)
        dx = r * (dxh - xh * m)
        dx_ref[...] = dx
        dxb_ref[...] = dx.astype(BF16)

    row = pl.BlockSpec((ts, d), lambda i: (i, 0))
    vec = pl.BlockSpec((1, d), lambda i: (0, 0))
    lvec = pl.BlockSpec((1, LANES), lambda i: (0, 0))
    return pl.pallas_call(
        body, name=name, grid=(s // ts,), in_specs=[row, vec, row], out_specs=[lvec, row, row, vec],
        out_shape=[jax.ShapeDtypeStruct((1, LANES), F32), jax.ShapeDtypeStruct((s, d), F32),
                   jax.ShapeDtypeStruct((s, d), BF16), jax.ShapeDtypeStruct((1, d), F32)],
        compiler_params=_params("arbitrary"),
    )(x, g, target)


def _gate_fwd(name, pb, bpad, fl_block):
    s = pb.shape[0]
    nb = s // TILE

    def body(fl_ref, b_ref, ccol_ref, crow_ref, carry):
        @pl.when(pl.program_id(0) == 0)
        def _():
            carry[...] = jnp.zeros_like(carry)

        u = fl_ref[...] + b_ref[...]
        lf = jnp.minimum(u, 0.0) - jnp.log1p(jnp.exp(-jnp.abs(u)))
        lower = _tri(TILE, lambda r, c: c <= r)
        c = _sum_r3(lower, lf) + carry[0:1, :]
        ccol_ref[...] = jnp.concatenate(
            [jnp.broadcast_to(c[:, hh:hh + 1], (TILE, HEAD_DIM)) for hh in range(FOX_HEADS)], axis=1)
        crow_ref[0] = c.T[0:8, :]
        carry[...] = jnp.broadcast_to(c[TILE - 1:TILE, :], carry.shape)

    return pl.pallas_call(
        body, name=name, grid=(nb,),
        in_specs=[pl.BlockSpec((TILE, LANES), lambda i: (i, fl_block)), pl.BlockSpec((1, LANES), lambda i: (0, 0))],
        out_specs=[pl.BlockSpec((TILE, FOX_WIDTH), lambda i: (i, 0)), pl.BlockSpec((1, 8, TILE), lambda i: (i, 0, 0))],
        out_shape=[jax.ShapeDtypeStruct((s, FOX_WIDTH), F32), jax.ShapeDtypeStruct((nb, 8, TILE), F32)],
        scratch_shapes=[pltpu.VMEM((8, LANES), F32)],
        compiler_params=_params("arbitrary"),
    )(pb, bpad)


def _gate_bwd(name, pb, bpad, colsum, fl_block, dpb):
    s = pb.shape[0]
    nb = s // TILE

    def body(fl_ref, b_ref, cs_ref, dpb_ref, dl_ref, db_ref, carry):
        @pl.when(pl.program_id(0) == 0)
        def _():
            carry[...] = jnp.zeros_like(carry)
            db_ref[...] = jnp.zeros_like(db_ref)

        upper = _tri(TILE, lambda r, c: r >= c)
        rsum = _sum_l3(cs_ref[0], upper) + carry[:, 0:1]
        carry[...] = jnp.broadcast_to(rsum[:, 0:1], carry.shape)
        full = jnp.concatenate([rsum, jnp.zeros((LANES - 8, TILE), F32)], axis=0)
        dlf = -full.T
        u = fl_ref[...] + b_ref[...]
        dlogit = dlf * (1.0 - jax.nn.sigmoid(u))
        dl_ref[...] = dlogit.astype(BF16)
        db_ref[...] += jnp.sum(dlogit, axis=0, keepdims=True)

    logits_block = pl.BlockSpec((TILE, LANES), lambda i: (nb - 1 - i, fl_block))
    return pl.pallas_call(
        body, name=name, grid=(nb,),
        in_specs=[logits_block, pl.BlockSpec((1, LANES), lambda i: (0, 0)),
                  pl.BlockSpec((1, 8, TILE), lambda i: (nb - 1 - i, 0, 0)), pl.BlockSpec(memory_space=pl.ANY)],
        out_specs=[logits_block, pl.BlockSpec((1, LANES), lambda i: (0, 0))],
        out_shape=[jax.ShapeDtypeStruct(dpb.shape, BF16), jax.ShapeDtypeStruct((1, LANES), F32)],
        scratch_shapes=[pltpu.VMEM((8, LANES), F32)], input_output_aliases={3: 0},
        compiler_params=_params("arbitrary"),
    )(pb, bpad, colsum, dpb)


def _head_slices(hh):
    return slice(HEAD_DIM * hh, HEAD_DIM * (hh + 1))


def _scaled_q(q_ref, sl, scale=SCALE):
    return (q_ref[:, sl].astype(F32) * scale).astype(BF16)


def _neg_abs(x):
    sign = jnp.uint32(0x80000000)
    return lax.bitcast_convert_type(lax.bitcast_convert_type(x, jnp.uint32) | sign, F32)


def _pair_grid_call(name, body, nb, in_specs, out_specs, out_shape, scratch, args, carried=None, groups=4):
    if carried is None:
        return pl.pallas_call(
            body, name=name, grid=(groups, nb), in_specs=in_specs, out_specs=out_specs, out_shape=out_shape,
            scratch_shapes=scratch, compiler_params=_params("arbitrary", "arbitrary"),
        )(*args)
    n_in, n_out, n_ex = len(in_specs), len(out_specs), carried.n

    def body_with_copies(*refs):
        own_in, ex_in = refs[:n_in], refs[n_in:n_in + n_ex]
        own_out = refs[n_in + n_ex:n_in + n_ex + n_out]
        ex_out = refs[n_in + n_ex + n_out:n_in + 2 * n_ex + n_out]
        own_scratch, sems = refs[n_in + 2 * n_ex + n_out:-2], refs[-2:]
        parts = (ex_in, ex_out, sems[0], sems[1])
        p, i = pl.program_id(0), pl.program_id(1)
        pl.when(jnp.logical_and(p == 0, i == 0))(lambda: carried.begin(*parts))
        if carried.relay is not None:
            pl.when(jnp.logical_and(p == groups - 1, i == max(nb - 2, 0)))(lambda: carried.relay(*parts))
        body(*own_in, *own_out, *own_scratch)
        pl.when(jnp.logical_and(p == groups - 1, i == nb - 1))(lambda: carried.finish(*parts))

    return pl.pallas_call(
        body_with_copies, name=name, grid=(groups, nb), in_specs=list(in_specs) + [HBM_SPEC] * n_ex,
        out_specs=list(out_specs) + [HBM_SPEC] * n_ex, out_shape=list(out_shape) + carried.out_shapes,
        scratch_shapes=list(scratch) + _dma_sems(carried.n_sems),
        compiler_params=_params("arbitrary", "arbitrary"),
    )(*args, *carried.inputs)


def _sb_fwd(name, pa, col0, carried=None):
    s = pa.shape[0]
    nb = s // TILE
    heads, lanes = SB_FWD_GROUP, SB_FWD_GROUP * HEAD_DIM
    cb = col0 // lanes
    kb = SB_WIDTH // lanes

    def body(q_ref, k_ref, v_ref, o_ref, lsig_s, lf_s):
        i = pl.program_id(1)
        r = lax.broadcasted_iota(jnp.int32, (TILE, TILE), 0)
        c = lax.broadcasted_iota(jnp.int32, (TILE, TILE), 1)
        strict = c < r
        u_after = _tri(TILE, lambda rr, cc: rr > cc)
        qs = [_scaled_q(q_ref, _head_slices(hh), -SCALE) for hh in range(heads)]

        def neg_z(j):
            kblk = k_ref[_rows(k_ref, j), :]
            return [_dot_nt(qs[hh], kblk[:, _head_slices(hh)]) for hh in range(heads)]

        def scores(nzs, slot, diag):
            for hh, nz in enumerate(nzs):
                lf = jnp.minimum(nz, 0.0) - jnp.log(1.0 + jnp.exp(_neg_abs(nz)))
                lsig = lf - nz
                if diag:
                    lf = jnp.where(strict, lf, 0.0)
                    lsig = jnp.where(strict, lsig, MASKED)
                lsig_s[slot, hh] = lsig
                lf_s[slot, hh] = lf.astype(BF16)

        def weigh(j, slot, state):
            vblk = v_ref[_rows(v_ref, j), :]
            new = []
            for hh in range(heads):
                carry, acc = state[hh]
                lfb = lf_s[slot, hh]
                sx = _dot(lfb, u_after)
                a = jnp.exp(lsig_s[slot, hh] + sx + carry)
                new.append((carry + sx[:, 0:1] + lfb[:, 0:1].astype(F32),
                            acc + _dot(a.astype(BF16), vblk[:, _head_slices(hh)])))
            return tuple(new)

        def step(t, state):
            state = weigh(i - t + 1, (t - 1) % 2, state)
            scores(neg_z(i - t), t % 2, False)
            return state

        zero = (jnp.zeros((TILE, 1), F32), jnp.zeros((TILE, HEAD_DIM), F32))
        scores(neg_z(i), 0, True)
        state = lax.fori_loop(1, i + 1, step, (zero,) * heads)
        state = weigh(0, i % 2, state)
        o_ref[...] = jnp.concatenate([st[1] for st in state], axis=1)

    outs = _pair_grid_call(
        name, body, nb,
        in_specs=[pl.BlockSpec((TILE, lanes), lambda p, i: (i, cb + p)),
                  pl.BlockSpec((s, lanes), lambda p, i: (0, cb + kb + p)),
                  pl.BlockSpec((s, lanes), lambda p, i: (0, cb + 2 * kb + p))],
        out_specs=[pl.BlockSpec((TILE, lanes), lambda p, i: (i, p))],
        out_shape=[jax.ShapeDtypeStruct((s, SB_WIDTH), F32)],
        scratch=[pltpu.VMEM((2, heads, TILE, TILE), F32), pltpu.VMEM((2, heads, TILE, TILE), BF16)],
        args=(pa, pa, pa), carried=carried, groups=kb)
    return outs[0], outs[1:]


def _sb_bwd(name, pa, col0, dout, dcol0, carried=None):
    s = pa.shape[0]
    nb = s // TILE
    cb = col0 // SB_LANES
    kb = SB_WIDTH // SB_LANES
    db = dcol0 // SB_LANES

    def body(q_ref, k_ref, v_ref, do_ref, dq_ref, dk_ref, dv_ref, dk_acc, dv_acc, dpan, span, gsum, lsig_s, lf_s):
        i = pl.program_id(1)

        @pl.when(i == 0)
        def _():
            dk_acc[...] = jnp.zeros_like(dk_acc)
            dv_acc[...] = jnp.zeros_like(dv_acc)

        r = lax.broadcasted_iota(jnp.int32, (TILE, TILE), 0)
        c = lax.broadcasted_iota(jnp.int32, (TILE, TILE), 1)
        strict = c < r
        u_after = _tri(TILE, lambda rr, cc: rr > cc)
        u_before = _tri(TILE, lambda rr, cc: rr < cc)
        qs = [_scaled_q(q_ref, _head_slices(hh), -SCALE) for hh in range(SB_GROUP)]
        dos = [do_ref[:, _head_slices(hh)].astype(BF16) for hh in range(SB_GROUP)]
        dots = [do_ref[:, _head_slices(hh)].T.astype(BF16) for hh in range(SB_GROUP)]
        qts = [q.astype(F32).T.astype(BF16) for q in qs]

        def scores(j, slot, diag):
            kblk = k_ref[_rows(k_ref, j), :]
            for hh in range(SB_GROUP):
                nz = _dot_nt(qs[hh], kblk[:, _head_slices(hh)])
                lf = jnp.minimum(nz, 0.0) - jnp.log(1.0 + jnp.exp(_neg_abs(nz)))
                lsig = lf - nz
                if diag:
                    lf = jnp.where(strict, lf, 0.0)
                    lsig = jnp.where(strict, lsig, MASKED)
                lsig_s[slot, hh] = lsig
                lf_s[slot, hh] = lf.astype(BF16)

        def grads(j, slot, carries):
            vblk = v_ref[_rows(v_ref, j), :]
            new = []
            for hh in range(SB_GROUP):
                lfb = lf_s[slot, hh]
                lsig = lsig_s[slot, hh]
                sx = _dot(lfb, u_after)
                a = jnp.exp(lsig + sx + carries[hh])
                g = a * _dot_nt(dos[hh], vblk[:, _head_slices(hh)])
                sig = jnp.exp(lsig)
                inside = _dot(g.astype(BF16), u_before)
                dpan[hh, j] = sig * (inside + g) - g
                span[hh, j] = sig
                gsum[hh, j] = inside[:, TILE - 1:TILE] + g[:, TILE - 1:TILE]
                dv_acc[hh, j] += _dot(dots[hh], a.astype(BF16))
                new.append(carries[hh] + sx[:, 0:1] + lfb[:, 0:1].astype(F32))
            return tuple(new)

        def step1(t, carries):
            carries = grads(i - t + 1, (t - 1) % 2, carries)
            scores(i - t, t % 2, False)
            return carries

        zero1 = jnp.zeros((TILE, 1), F32)
        scores(i, 0, True)
        carries = lax.fori_loop(1, i + 1, step1, (zero1,) * SB_GROUP)
        grads(0, i % 2, carries)

        def pass2(j, state):
            kblk = k_ref[_rows(k_ref, j), :]
            new = []
            for hh in range(SB_GROUP):
                before, ndq = state[hh]
                ndzb = (dpan[hh, j] + span[hh, j] * before).astype(BF16)
                dk_acc[hh, j] += _dot(qts[hh], ndzb)
                new.append((before + gsum[hh, j], ndq + _dot(ndzb, kblk[:, _head_slices(hh)])))
            return tuple(new)

        zero2 = (zero1, jnp.zeros((TILE, HEAD_DIM), F32))
        state = lax.fori_loop(0, i + 1, pass2, (zero2,) * SB_GROUP)
        dq_ref[...] = jnp.concatenate([st[1] * -SCALE for st in state], axis=1).astype(BF16)

        @pl.when(i == nb - 1)
        def _():
            for acc, ref in ((dk_acc, dk_ref), (dv_acc, dv_ref)):
                for j in range(nb):
                    ref[j * TILE:(j + 1) * TILE, :] = jnp.concatenate(
                        [acc[hh, j].T for hh in range(SB_GROUP)], axis=1).astype(BF16)

    qspec = pl.BlockSpec((TILE, SB_LANES), lambda p, i: (i, p))
    kvspec = pl.BlockSpec((s, SB_LANES), lambda p, i: (0, p))
    out = jax.ShapeDtypeStruct((s, SB_WIDTH), BF16)
    outs = _pair_grid_call(
        name, body, nb,
        in_specs=[pl.BlockSpec((TILE, SB_LANES), lambda p, i: (i, cb + p)),
                  pl.BlockSpec((s, SB_LANES), lambda p, i: (0, cb + kb + p)),
                  pl.BlockSpec((s, SB_LANES), lambda p, i: (0, cb + 2 * kb + p)),
                  pl.BlockSpec((TILE, SB_LANES), lambda p, i: (i, db + p))],
        out_specs=[qspec, kvspec, kvspec], out_shape=[out, out, out],
        scratch=[pltpu.VMEM((SB_GROUP, nb, HEAD_DIM, TILE), F32), pltpu.VMEM((SB_GROUP, nb, HEAD_DIM, TILE), F32),
                 pltpu.VMEM((SB_GROUP, nb, TILE, TILE), F32), pltpu.VMEM((SB_GROUP, nb, TILE, TILE), F32),
                 pltpu.VMEM((SB_GROUP, nb, TILE, 1), F32),
                 pltpu.VMEM((2, SB_GROUP, TILE, TILE), F32), pltpu.VMEM((2, SB_GROUP, TILE, TILE), BF16)],
        args=(pa, pa, pa, dout), carried=carried, groups=kb)
    return outs[:3], outs[3:]


def _fox_scores(q, kj, cq, crj, causal, diag):
    sc = _dot_nt(q, kj) + (cq - crj)
    if diag:
        sc = jnp.where(causal, sc, NEG_INF)
    return sc


def _fox_fwd(name, pa, col0, ccol4, crow4, carried=None):
    s = pa.shape[0]
    nb = s // TILE
    heads, lanes = FOX_FWD_GROUP, FOX_FWD_GROUP * HEAD_DIM
    cb = col0 // lanes
    kb = FOX_WIDTH // lanes

    def body(q_ref, k_ref, v_ref, cc_ref, cr_ref, o_ref, lse_ref, sc_s):
        i = pl.program_id(1)
        head0 = pl.program_id(0) * heads
        r = lax.broadcasted_iota(jnp.int32, (TILE, TILE), 0)
        c = lax.broadcasted_iota(jnp.int32, (TILE, TILE), 1)
        causal = c <= r
        qs = [_scaled_q(q_ref, _head_slices(hh)) for hh in range(heads)]
        cqs = [cc_ref[:, HEAD_DIM * hh:HEAD_DIM * hh + 1] for hh in range(heads)]

        def logits(j, slot, diag):
            kblk = k_ref[_rows(k_ref, j), :]
            tops = []
            for hh in range(heads):
                sc = _fox_scores(qs[hh], kblk[:, _head_slices(hh)], cqs[hh], cr_ref[j, pl.ds(head0 + hh, 1), :], causal, diag)
                sc_s[slot, hh] = sc
                tops.append(jnp.max(sc, axis=1, keepdims=True))
            return tuple(tops)

        def update(j, slot, tops, state):
            vblk = v_ref[_rows(v_ref, j), :]
            new = []
            for hh in range(heads):
                m, l, acc = state[hh]
                m2 = jnp.maximum(m, tops[hh])
                alpha = jnp.exp(m - m2)
                p = jnp.exp(sc_s[slot, hh] - m2)
                new.append((m2, l * alpha + jnp.sum(p, axis=1, keepdims=True),
                            acc * alpha + _dot(p.astype(BF16), vblk[:, _head_slices(hh)])))
            return tuple(new)

        def step(t, both):
            tops, state = both
            state = update(i - t + 1, (t - 1) % 2, tops, state)
            return logits(i - t, t % 2, False), state

        zero = (jnp.full((TILE, 1), NEG_INF, F32), jnp.zeros((TILE, 1), F32), jnp.zeros((TILE, HEAD_DIM), F32))
        tops, state = lax.fori_loop(1, i + 1, step, (logits(i, 0, True), (zero,) * heads))
        state = update(0, i % 2, tops, state)
        o_ref[...] = jnp.concatenate([st[2] / st[1] for st in state], axis=1)
        lse_ref[...] = jnp.concatenate(
            [jnp.broadcast_to(st[0] + jnp.log(st[1]), (TILE, HEAD_DIM)) for st in state], axis=1)

    outs = _pair_grid_call(
        name, body, nb,
        in_specs=[pl.BlockSpec((TILE, lanes), lambda p, i: (i, cb + p)),
                  pl.BlockSpec((s, lanes), lambda p, i: (0, cb + kb + p)),
                  pl.BlockSpec((s, lanes), lambda p, i: (0, cb + 2 * kb + p)),
                  pl.BlockSpec((TILE, lanes), lambda p, i: (i, p)),
                  pl.BlockSpec((nb, 8, TILE), lambda p, i: (0, 0, 0))],
        out_specs=[pl.BlockSpec((TILE, lanes), lambda p, i: (i, p)), pl.BlockSpec((TILE, lanes), lambda p, i: (i, p))],
        out_shape=[jax.ShapeDtypeStruct((s, FOX_WIDTH), F32), jax.ShapeDtypeStruct((s, FOX_WIDTH), F32)],
        scratch=[pltpu.VMEM((2, heads, TILE, TILE), F32)],
        args=(pa, pa, pa, ccol4, crow4), carried=carried, groups=kb)
    return outs[0], outs[1], outs[2:]


def _fox_bwd(name, pa, col0, ccol4, crow4, out, lse, dout, dcol0, carried=None):
    s = pa.shape[0]
    nb = s // TILE
    cb = col0 // FOX_LANES
    kb = FOX_WIDTH // FOX_LANES
    db = dcol0 // FOX_LANES

    def body(q_ref, k_ref, v_ref, cc_ref, cr_ref, o_ref, lse_ref, do_ref,
             dq_ref, dk_ref, dv_ref, cs_ref, dk_acc, dv_acc, p_s, ds_s):
        i = pl.program_id(1)
        head0 = pl.program_id(0) * FOX_GROUP

        @pl.when(i == 0)
        def _():
            dk_acc[...] = jnp.zeros_like(dk_acc)
            dv_acc[...] = jnp.zeros_like(dv_acc)

        @pl.when(jnp.logical_and(i == 0, head0 == 0))
        def _():
            cs_ref[...] = jnp.zeros_like(cs_ref)

        r = lax.broadcasted_iota(jnp.int32, (TILE, TILE), 0)
        c = lax.broadcasted_iota(jnp.int32, (TILE, TILE), 1)
        causal = c <= r
        qs = [_scaled_q(q_ref, _head_slices(hh)) for hh in range(FOX_GROUP)]
        cqs = [cc_ref[:, HEAD_DIM * hh:HEAD_DIM * hh + 1] for hh in range(FOX_GROUP)]
        lses = [lse_ref[:, HEAD_DIM * hh:HEAD_DIM * hh + 1] for hh in range(FOX_GROUP)]
        dofs = [do_ref[:, _head_slices(hh)] for hh in range(FOX_GROUP)]
        dos = [d_.astype(BF16) for d_ in dofs]
        dots = [d_.T.astype(BF16) for d_ in dofs]
        qts = [q.astype(F32).T.astype(BF16) for q in qs]
        deltas = [jnp.sum(dofs[hh] * o_ref[:, _head_slices(hh)], axis=1, keepdims=True) for hh in range(FOX_GROUP)]

        def probs(j, slot, rowsums, diag):
            kblk = k_ref[_rows(k_ref, j), :]
            vblk = v_ref[_rows(v_ref, j), :]
            new = []
            for hh in range(FOX_GROUP):
                sl = _head_slices(hh)
                sc = _fox_scores(qs[hh], kblk[:, sl], cqs[hh], cr_ref[j, pl.ds(head0 + hh, 1), :], causal, diag)
                p = jnp.exp(sc - lses[hh])
                ds = p * (_dot_nt(dos[hh], vblk[:, sl]) - deltas[hh])
                p_s[slot, hh] = p.astype(BF16)
                ds_s[slot, hh] = ds.astype(BF16)
                cs_ref[j, pl.ds(head0 + hh, 1), :] += jnp.sum(ds, axis=0, keepdims=True)
                new.append(rowsums[hh] + jnp.sum(ds, axis=1, keepdims=True))
            return tuple(new)

        def accumulate(j, slot, dqs):
            kblk = k_ref[_rows(k_ref, j), :]
            new = []
            for hh in range(FOX_GROUP):
                dsb = ds_s[slot, hh]
                dv_acc[hh, j] += _dot(dots[hh], p_s[slot, hh])
                dk_acc[hh, j] += _dot(qts[hh], dsb)
                new.append(dqs[hh] + _dot(dsb, kblk[:, _head_slices(hh)]))
            return tuple(new)

        def step(t, both):
            rowsums, dqs = both
            dqs = accumulate(i - t + 1, (t - 1) % 2, dqs)
            return probs(i - t, t % 2, rowsums, False), dqs

        zero1 = jnp.zeros((TILE, 1), F32)
        zero64 = jnp.zeros((TILE, HEAD_DIM), F32)
        rowsums, dqs = lax.fori_loop(1, i + 1, step,
                                     (probs(i, 0, (zero1,) * FOX_GROUP, True), (zero64,) * FOX_GROUP))
        dqs = accumulate(0, i % 2, dqs)
        for hh in range(FOX_GROUP):
            cs_ref[i, pl.ds(head0 + hh, 1), :] -= jnp.broadcast_to(rowsums[hh], (TILE, LANES)).T[0:1, :]
        dq_ref[...] = jnp.concatenate([dq * SCALE for dq in dqs], axis=1).astype(BF16)

        @pl.when(i == nb - 1)
        def _():
            for acc, ref in ((dk_acc, dk_ref), (dv_acc, dv_ref)):
                for j in range(nb):
                    ref[j * TILE:(j + 1) * TILE, :] = jnp.concatenate(
                        [acc[hh, j].T for hh in range(FOX_GROUP)], axis=1).astype(BF16)

    qspec = pl.BlockSpec((TILE, FOX_LANES), lambda p, i: (i, p))
    kvspec = pl.BlockSpec((s, FOX_LANES), lambda p, i: (0, p))
    o3 = jax.ShapeDtypeStruct((s, FOX_WIDTH), BF16)
    outs = _pair_grid_call(
        name, body, nb,
        in_specs=[pl.BlockSpec((TILE, FOX_LANES), lambda p, i: (i, cb + p)),
                  pl.BlockSpec((s, FOX_LANES), lambda p, i: (0, cb + kb + p)),
                  pl.BlockSpec((s, FOX_LANES), lambda p, i: (0, cb + 2 * kb + p)),
                  pl.BlockSpec((TILE, FOX_LANES), lambda p, i: (i, p)),
                  pl.BlockSpec((nb, 8, TILE), lambda p, i: (0, 0, 0)),
                  qspec,
                  pl.BlockSpec((TILE, FOX_LANES), lambda p, i: (i, p)),
                  pl.BlockSpec((TILE, FOX_LANES), lambda p, i: (i, db + p))],
        out_specs=[qspec, kvspec, kvspec, pl.BlockSpec((nb, 8, TILE), lambda p, i: (0, 0, 0))],
        out_shape=[o3, o3, o3, jax.ShapeDtypeStruct((nb, 8, TILE), F32)],
        scratch=[pltpu.VMEM((FOX_GROUP, nb, HEAD_DIM, TILE), F32), pltpu.VMEM((FOX_GROUP, nb, HEAD_DIM, TILE), F32),
                 pltpu.VMEM((2, FOX_GROUP, TILE, TILE), BF16), pltpu.VMEM((2, FOX_GROUP, TILE, TILE), BF16)],
        args=(pa, pa, pa, ccol4, crow4, out, lse, dout), carried=carried, groups=kb)
    return outs[:4], outs[4:]


def _mem_fwd(name, pa, mkv):
    s = pa.shape[0]
    ml = mkv.shape[0]
    nb = s // TILE
    cb = QKV_WIDTH // MEM_WIDTH

    def body(q_ref, k_ref, v_ref, o_ref, lse_ref):
        outs, lses = [], []
        for hh in range(MEM_HEADS):
            sl = _head_slices(hh)
            sc = _dot_nt(_scaled_q(q_ref, sl), k_ref[:, sl])
            m = jnp.max(sc, axis=1, keepdims=True)
            p = jnp.exp(sc - m)
            l = jnp.sum(p, axis=1, keepdims=True)
            outs.append(_dot(p.astype(BF16), v_ref[:, sl]) / l)
            lses.append(jnp.broadcast_to(m + jnp.log(l), (TILE, HEAD_DIM)))
        o_ref[...] = jnp.concatenate(outs, axis=1)
        lse_ref[...] = jnp.concatenate(lses, axis=1)

    row = pl.BlockSpec((TILE, MEM_WIDTH), lambda i: (i, 0))
    return pl.pallas_call(
        body, name=name, grid=(nb,),
        in_specs=[pl.BlockSpec((TILE, MEM_WIDTH), lambda i: (i, cb)),
                  pl.BlockSpec((ml, MEM_WIDTH), lambda i: (0, 0)),
                  pl.BlockSpec((ml, MEM_WIDTH), lambda i: (0, 1))],
        out_specs=[row, row],
        out_shape=[jax.ShapeDtypeStruct((s, MEM_WIDTH), F32), jax.ShapeDtypeStruct((s, MEM_WIDTH), F32)],
        compiler_params=_params("parallel"),
    )(pa, mkv, mkv)


def _mem_bwd(name, pa, mkv, out, lse, dout, dcol0):
    s = pa.shape[0]
    ml = mkv.shape[0]
    nb = s // TILE
    cb = QKV_WIDTH // MEM_WIDTH
    db = dcol0 // MEM_WIDTH

    def body(q_ref, k_ref, v_ref, o_ref, lse_ref, do_ref, dq_ref, dkv_ref, dk_acc, dv_acc):
        i = pl.program_id(0)

        @pl.when(i == 0)
        def _():
            dk_acc[...] = jnp.zeros_like(dk_acc)
            dv_acc[...] = jnp.zeros_like(dv_acc)

        dqs = []
        for hh in range(MEM_HEADS):
            sl = _head_slices(hh)
            q = _scaled_q(q_ref, sl)
            kh = k_ref[:, sl]
            dof = do_ref[:, sl]
            do = dof.astype(BF16)
            delta = jnp.sum(dof * o_ref[:, sl], axis=1, keepdims=True)
            p = jnp.exp(_dot_nt(q, kh) - lse_ref[:, HEAD_DIM * hh:HEAD_DIM * hh + 1])
            ds = (p * (_dot_nt(do, v_ref[:, sl]) - delta)).astype(BF16)
            dv_acc[hh] += _dot(dof.T.astype(BF16), p.astype(BF16))
            dk_acc[hh] += _dot(q.astype(F32).T.astype(BF16), ds)
            dqs.append(_dot(ds, kh) * SCALE)
        dq_ref[...] = jnp.concatenate(dqs, axis=1).astype(BF16)

        @pl.when(i == nb - 1)
        def _():
            dkv_ref[...] = jnp.concatenate([dk_acc[hh].T for hh in range(MEM_HEADS)]
                                           + [dv_acc[hh].T for hh in range(MEM_HEADS)], axis=1).astype(BF16)

    row = pl.BlockSpec((TILE, MEM_WIDTH), lambda i: (i, 0))
    return pl.pallas_call(
        body, name=name, grid=(nb,),
        in_specs=[pl.BlockSpec((TILE, MEM_WIDTH), lambda i: (i, cb)),
                  pl.BlockSpec((ml, MEM_WIDTH), lambda i: (0, 0)),
                  pl.BlockSpec((ml, MEM_WIDTH), lambda i: (0, 1)),
                  row, row,
                  pl.BlockSpec((TILE, MEM_WIDTH), lambda i: (i, db))],
        out_specs=[row, pl.BlockSpec((ml, 2 * MEM_WIDTH), lambda i: (0, 0))],
        out_shape=[jax.ShapeDtypeStruct((s, MEM_WIDTH), BF16), jax.ShapeDtypeStruct((ml, 2 * MEM_WIDTH), BF16)],
        scratch_shapes=[pltpu.VMEM((MEM_HEADS, HEAD_DIM, ml), F32), pltpu.VMEM((MEM_HEADS, HEAD_DIM, ml), F32)],
        compiler_params=_params("arbitrary"),
    )(pa, mkv, mkv, out, lse, dout)


def _head_maps():
    col = jnp.arange(MIX_WIDTH)[:, None] // HEAD_DIM
    g = (col == jnp.arange(LANES)[None, :]).astype(BF16)
    return g, g.T


def _normed_heads(osb_ref, ofx_ref, om_ref, g_ref, gt_ref):
    y = jnp.concatenate([osb_ref[...], ofx_ref[...], om_ref[...]], axis=1)
    msq = _sum_l2(y * y, g_ref[...]) * (1.0 / HEAD_DIM)
    rf = _sum_l3(lax.rsqrt(msq + EPS), gt_ref[...])
    return y * rf, rf


def _out_fwd(name, o_sb, o_fx, o_m, pb, ow, x, w_out, ts):
    s, d = x.shape
    g, gt = _head_maps()

    def body(osb_ref, ofx_ref, om_ref, gate_ref, ow_ref, x_ref, w_ref, g_ref, gt_ref, xo_ref, y2_ref):
        yh, _ = _normed_heads(osb_ref, ofx_ref, om_ref, g_ref, gt_ref)
        gate = gate_ref[...]
        y2 = (yh * ow_ref[...] * (gate * jax.nn.sigmoid(gate))).astype(BF16)
        y2_ref[...] = y2
        xo_ref[...] = x_ref[...] + _dot(y2, w_ref[...])

    return pl.pallas_call(
        body, name=name, grid=(s // ts,),
        in_specs=[_row_spec(ts, SB_WIDTH), _row_spec(ts, FOX_WIDTH), _row_spec(ts, MEM_WIDTH),
                  _row_spec(ts, MIX_WIDTH), _const_spec((1, MIX_WIDTH)), _row_spec(ts, d),
                  _const_spec((MIX_WIDTH, d)),
                  _const_spec((MIX_WIDTH, LANES)), _const_spec((LANES, MIX_WIDTH))],
        out_specs=[_row_spec(ts, d), _row_spec(ts, MIX_WIDTH)],
        out_shape=[jax.ShapeDtypeStruct((s, d), F32), jax.ShapeDtypeStruct((s, MIX_WIDTH), BF16)],
        compiler_params=_params("parallel"),
    )(o_sb, o_fx, o_m, pb, ow, x, w_out, g, gt)


def _row_spec(ts, w):
    return pl.BlockSpec((ts, w), lambda i: (i, 0))


def _const_spec(shape):
    return pl.BlockSpec(shape, lambda i: (0,) * len(shape))


def _out_bwd(name, dxb, o_sb, o_fx, o_m, pb, ow, w_out, ts):
    s, d = dxb.shape
    g, gt = _head_maps()

    def body(dx_ref, osb_ref, ofx_ref, om_ref, gate_ref, ow_ref, w_ref, g_ref, gt_ref, dy_ref, dgate_ref, dow_ref):
        @pl.when(pl.program_id(0) == 0)
        def _():
            dow_ref[...] = jnp.zeros_like(dow_ref)

        dy2 = _dot_nt(dx_ref[...], w_ref[...])
        yh, rf = _normed_heads(osb_ref, ofx_ref, om_ref, g_ref, gt_ref)
        gate = gate_ref[...]
        sig = jax.nn.sigmoid(gate)
        ow_v = ow_ref[...]
        dgate_ref[...] = (dy2 * (yh * ow_v) * (sig * (1.0 + gate * (1.0 - sig)))).astype(BF16)
        dn = dy2 * (gate * sig)
        dow_ref[...] += jnp.sum(dn * yh, axis=0, keepdims=True)
        dyh = dn * ow_v
        t = _sum_l2(dyh * yh, g_ref[...]) * (1.0 / HEAD_DIM)
        dy_ref[...] = rf * (dyh - yh * _sum_l3(t, gt_ref[...]))

    return pl.pallas_call(
        body, name=name, grid=(s // ts,),
        in_specs=[_row_spec(ts, d), _row_spec(ts, SB_WIDTH), _row_spec(ts, FOX_WIDTH), _row_spec(ts, MEM_WIDTH),
                  _row_spec(ts, MIX_WIDTH), _const_spec((1, MIX_WIDTH)),
                  _const_spec((MIX_WIDTH, d)),
                  _const_spec((MIX_WIDTH, LANES)), _const_spec((LANES, MIX_WIDTH))],
        out_specs=[_row_spec(ts, MIX_WIDTH), _row_spec(ts, MIX_WIDTH), _const_spec((1, MIX_WIDTH))],
        out_shape=[jax.ShapeDtypeStruct((s, MIX_WIDTH), F32), jax.ShapeDtypeStruct((s, PB), BF16),
                   jax.ShapeDtypeStruct((1, MIX_WIDTH), F32)],
        compiler_params=_params("arbitrary"),
    )(dxb, o_sb, o_fx, o_m, pb, ow, w_out, g, gt)


def _adamw(name, w, g, m, v, tr):
    def body(w_ref, g_ref, m_ref, v_ref, d_ref, m2_ref, v2_ref):
        gv = g_ref[...]
        m2 = ADAM_B1 * m_ref[...] + (1.0 - ADAM_B1) * gv
        v2 = ADAM_B2 * v_ref[...] + (1.0 - ADAM_B2) * (gv * gv)
        m_hat = m2 / (1.0 - ADAM_B1 ** ADAM_STEP)
        v_hat = v2 / (1.0 - ADAM_B2 ** ADAM_STEP)
        d_ref[...] = -ADAM_LR * (m_hat / (jnp.sqrt(v_hat) + ADAM_EPS) + ADAM_WD * w_ref[...])
        m2_ref[...] = m2
        v2_ref[...] = v2

    rest = w.shape[1:]
    spec = pl.BlockSpec((tr,) + rest, lambda i: (i,) + (0,) * len(rest))
    shp = jax.ShapeDtypeStruct(w.shape, F32)
    return pl.pallas_call(
        body, name=name, grid=(w.shape[0] // tr,), in_specs=[spec] * 4, out_specs=[spec] * 3, out_shape=[shp] * 3,
        compiler_params=_params("parallel"),
    )(w, g, m, v)


def _adamw_sharded(name, w, m, v, g_own, g_other, cvec, tr):
    depth, rows, cols = w.shape
    nt = rows // 2 // tr

    def body(c_ref, w_ref, m_ref, v_ref, *rest):
        g_refs, (g_ref, d_ref, m2_ref, v2_ref) = rest[:2 * depth], rest[2 * depth:]
        layer, mine = pl.program_id(0), pl.program_id(1) == c_ref[0]
        gv = None
        for lt in range(depth):
            cand = jnp.where(mine, g_refs[lt][...], g_refs[depth + lt][...])
            gv = cand if gv is None else jnp.where(layer == lt, cand, gv)
        m2 = ADAM_B1 * m_ref[...] + (1.0 - ADAM_B1) * gv
        v2 = ADAM_B2 * v_ref[...] + (1.0 - ADAM_B2) * (gv * gv)
        m_hat = m2 / (1.0 - ADAM_B1 ** ADAM_STEP)
        v_hat = v2 / (1.0 - ADAM_B2 ** ADAM_STEP)
        g_ref[...] = gv
        d_ref[...] = -ADAM_LR * (m_hat / (jnp.sqrt(v_hat) + ADAM_EPS) + ADAM_WD * w_ref[...])
        m2_ref[...] = m2
        v2_ref[...] = v2

    def g_map(lt, own):
        def index(l, hf, i, c_ref):
            use = jnp.logical_and(l == lt, (hf == c_ref[0]) == own)
            return jnp.where(use, i, 0), 0
        return index

    full = pl.BlockSpec((None, tr, cols), lambda l, hf, i, c_ref: (l, hf * nt + i, 0))
    g_specs = [pl.BlockSpec((tr, cols), g_map(lt, own)) for own in (True, False) for lt in range(depth)]
    shp = jax.ShapeDtypeStruct((depth, rows, cols), F32)
    return pl.pallas_call(
        body, name=name,
        grid_spec=pltpu.PrefetchScalarGridSpec(
            num_scalar_prefetch=1, grid=(depth, 2, nt), in_specs=[full] * 3 + g_specs, out_specs=[full] * 4),
        out_shape=[shp] * 4,
        compiler_params=_params("arbitrary", "arbitrary", "arbitrary"),
    )(cvec, w, m, v, *g_own, *g_other)


HBM_SPEC = pl.BlockSpec(memory_space=pltpu.HBM)


def _place():
    x, y, c = lax.axis_index("x"), lax.axis_index("y"), lax.axis_index("c")
    chips = [(1 - x, y), (x, 1 - y), (1 - x, 1 - y)]
    return x, y, c, chips


def _remote(src, dst, send_sems, recv_sems, k, to):
    return pltpu.make_async_remote_copy(src_ref=src, dst_ref=dst, send_sem=send_sems.at[k], recv_sem=recv_sems.at[k],
                                        device_id=to, device_id_type=MESH)


def _half_rows(n_rows, cc):
    rh = n_rows // 2
    return pl.ds(pl.multiple_of(cc * rh, 16), rh)


def _dma_sems(n):
    return [pltpu.SemaphoreType.DMA((n,)), pltpu.SemaphoreType.DMA((n,))]


class _Exchange:
    def __init__(self, inputs, out_shapes, n_sems, begin, relay, finish):
        self.inputs, self.out_shapes, self.n_sems = list(inputs), list(out_shapes), n_sems
        self.begin, self.relay, self.finish = begin, relay, finish

    @property
    def n(self):
        return len(self.inputs)

    def split(self, refs):
        return refs[:self.n], refs[self.n:2 * self.n], refs[2 * self.n], refs[2 * self.n + 1]


def _run_exchange(name, ex):
    def body(*refs):
        parts = ex.split(refs)
        for phase in (ex.begin, ex.relay, ex.finish):
            if phase is not None:
                phase(*parts)

    return pl.pallas_call(
        body, name=name, in_specs=[HBM_SPEC] * ex.n, out_specs=[HBM_SPEC] * ex.n, out_shape=ex.out_shapes,
        scratch_shapes=_dma_sems(ex.n_sems),
    )(*ex.inputs)


def _gather_exchange(shards):
    def ici(in_refs, out_refs, send_sems, recv_sems):
        x, y, c, chips = _place()
        return [_remote(in_ref.at[_half_rows(in_ref.shape[0], c)], out_ref.at[2 * x + y, _half_rows(in_ref.shape[0], c)],
                        send_sems, recv_sems, 6 * a + j, (cx, cy, c))
                for a, (in_ref, out_ref) in enumerate(zip(in_refs, out_refs)) for j, (cx, cy) in enumerate(chips)]

    def d2d(out_refs, send_sems, recv_sems, half_of):
        x, y, c, chips = _place()
        cps = []
        for a, out_ref in enumerate(out_refs):
            for j, (cx, cy) in enumerate(chips):
                piece = out_ref.at[2 * cx + cy, _half_rows(out_ref.shape[1], half_of(c))]
                cps.append(_remote(piece, piece, send_sems, recv_sems, 6 * a + 3 + j, (x, y, 1 - c)))
        return cps

    def begin(in_refs, out_refs, send_sems, recv_sems):
        for cp in ici(in_refs, out_refs, send_sems, recv_sems):
            cp.start()

    def relay(in_refs, out_refs, send_sems, recv_sems):
        x, y, c, chips = _place()
        for a, out_ref in enumerate(out_refs):
            for j, (cx, cy) in enumerate(chips):
                landed = out_ref.at[2 * cx + cy, _half_rows(out_ref.shape[1], c)]
                _remote(landed, landed, send_sems, recv_sems, 6 * a + j, (cx, cy, c)).wait_recv()
        for cp in d2d(out_refs, send_sems, recv_sems, lambda c_: c_):
            cp.start()

    def finish(in_refs, out_refs, send_sems, recv_sems):
        for cp in d2d(out_refs, send_sems, recv_sems, lambda c_: 1 - c_):
            cp.wait_recv()
        for cp in ici(in_refs, out_refs, send_sems, recv_sems) + d2d(out_refs, send_sems, recv_sems, lambda c_: c_):
            cp.wait_send()

    shapes = [jax.ShapeDtypeStruct((N_CHIPS,) + s_.shape, s_.dtype) for s_ in shards]
    return _Exchange(shards, shapes, 6 * len(shards), begin, relay, finish)


def _swap_exchange(g4s):
    def copies(in_refs, out_refs, send_sems, recv_sems):
        x, y, c, _ = _place()
        return [_remote(in_ref.at[:, _half_rows(in_ref.shape[1], 1 - c), :], out_ref, send_sems, recv_sems, a, (x, y, 1 - c))
                for a, (in_ref, out_ref) in enumerate(zip(in_refs, out_refs))]

    def begin(*parts):
        for cp in copies(*parts):
            cp.start()

    def finish(*parts):
        for cp in copies(*parts):
            cp.wait()

    shapes = [jax.ShapeDtypeStruct((g.shape[0], g.shape[1] // 2, g.shape[2]), g.dtype) for g in g4s]
    return _Exchange(g4s, shapes, len(g4s), begin, None, finish)


def _add_half(name, g4, r1, cvec, tr):
    n, r, w = g4.shape
    rh = r // 2
    nblk = rh // tr

    def body(c_ref, a_ref, b_ref, o_ref):
        o_ref[...] = (a_ref[...].astype(F32) + b_ref[...].astype(F32)).astype(BF16)

    return pl.pallas_call(
        body, name=name,
        grid_spec=pltpu.PrefetchScalarGridSpec(
            num_scalar_prefetch=1, grid=(n, nblk),
            in_specs=[pl.BlockSpec((None, tr, w), lambda k, i, c_ref: (k, c_ref[0] * nblk + i, 0)),
                      pl.BlockSpec((None, tr, w), lambda k, i, c_ref: (k, i, 0))],
            out_specs=pl.BlockSpec((None, tr, w), lambda k, i, c_ref: (k, i, 0))),
        out_shape=jax.ShapeDtypeStruct((n, rh, w), BF16),
        compiler_params=_params("parallel", "parallel"),
    )(cvec, g4, r1)


def _scatter_exchange(h4s):
    def sends(in_refs, out_refs, send_sems, recv_sems):
        x, y, c, chips = _place()
        return [_remote(in_ref.at[2 * cx + cy], out_ref.at[j], send_sems, recv_sems, 3 * a + j, (cx, cy, c))
                for a, (in_ref, out_ref) in enumerate(zip(in_refs, out_refs)) for j, (cx, cy) in enumerate(chips)]

    def begin(*parts):
        for cp in sends(*parts):
            cp.start()

    def finish(in_refs, out_refs, send_sems, recv_sems):
        x, y, c, chips = _place()
        for a, out_ref in enumerate(out_refs):
            for j, (cx, cy) in enumerate(chips):
                got = out_ref.at[j]
                _remote(got, got, send_sems, recv_sems, 3 * a + j, (cx, cy, c)).wait_recv()
        for cp in sends(in_refs, out_refs, send_sems, recv_sems):
            cp.wait_send()

    shapes = [jax.ShapeDtypeStruct((3,) + h.shape[1:], h.dtype) for h in h4s]
    return _Exchange(h4s, shapes, 3 * len(h4s), begin, None, finish)


def _sum_chips(name, h4, r3, mvec, tr):
    _, rh, w = h4.shape

    def body(m_ref, a_ref, b_ref, c_ref, d_ref, o_ref):
        o_ref[...] = ((a_ref[...].astype(F32) + b_ref[...].astype(F32)) + c_ref[...].astype(F32)) + d_ref[...].astype(F32)

    specs = [pl.BlockSpec((None, tr, w), lambda i, m_ref: (m_ref[0], i, 0))]
    specs += [pl.BlockSpec((None, tr, w), functools.partial(lambda k, i, m_ref: (k, i, 0), k)) for k in range(3)]
    return pl.pallas_call(
        body, name=name,
        grid_spec=pltpu.PrefetchScalarGridSpec(
            num_scalar_prefetch=1, grid=(rh // tr,), in_specs=specs,
            out_specs=pl.BlockSpec((tr, w), lambda i, m_ref: (i, 0))),
        out_shape=jax.ShapeDtypeStruct((rh, w), F32),
        compiler_params=_params("parallel"),
    )(mvec, h4, r3, r3, r3)


def _swap_reduced_exchange(ghs):
    def copies(in_refs, out_refs, send_sems, recv_sems):
        x, y, c, _ = _place()
        return [_remote(in_ref, out_ref, send_sems, recv_sems, a, (x, y, 1 - c))
                for a, (in_ref, out_ref) in enumerate(zip(in_refs, out_refs))]

    def begin(*parts):
        for cp in copies(*parts):
            cp.start()

    def finish(*parts):
        for cp in copies(*parts):
            cp.wait()

    return _Exchange(ghs, [jax.ShapeDtypeStruct(g.shape, g.dtype) for g in ghs], len(ghs), begin, None, finish)


class _SemaphoresFrom:
    def __init__(self, sems, first):
        self.sems, self.first = sems, first

    @property
    def at(self):
        return self

    def __getitem__(self, k):
        return self.sems.at[self.first + k]


def _both(a, b):
    def phase(fa, fb):
        if fa is None and fb is None:
            return None

        def run(in_refs, out_refs, send_sems, recv_sems):
            if fa is not None:
                fa(in_refs[:a.n], out_refs[:a.n], send_sems, recv_sems)
            if fb is not None:
                fb(in_refs[a.n:], out_refs[a.n:], _SemaphoresFrom(send_sems, a.n_sems), _SemaphoresFrom(recv_sems, a.n_sems))

        return run

    return _Exchange(a.inputs + b.inputs, a.out_shapes + b.out_shapes, a.n_sems + b.n_sems,
                     phase(a.begin, b.begin), phase(a.relay, b.relay), phase(a.finish, b.finish))


def _small_update(name, partials, weights, moments1, moments2):
    n = len(partials)
    width = max(p.shape[1] for p in partials)
    starts, at = [], 0
    for p in partials:
        starts.append(at)
        at += p.shape[0]
    rows = -(-at // 8) * 8
    has_w = [w is not None for w in weights]
    n_w = sum(has_w)

    def body(*refs):
        p_refs = refs[:n]
        w_refs, m_refs, v_refs = refs[n:n + n_w], refs[n + n_w:n + 2 * n_w], refs[n + 2 * n_w:n + 3 * n_w]
        outs = refs[n + 3 * n_w:-4]
        g_refs, upd_refs = outs[:n], outs[n:]
        vec, buf, send_sems, recv_sems = refs[-4:]
        x, y, c, _ = _place()
        me = 4 * x + 2 * y + c
        vec[...] = jnp.zeros_like(vec)
        for p_ref, r0 in zip(p_refs, starts):
            vec[r0:r0 + p_ref.shape[0], 0:p_ref.shape[1]] = p_ref[...]
        buf[me] = vec[...]
        flips = [(fx, fy, fc) for fx in (0, 1) for fy in (0, 1) for fc in (0, 1)][1:]
        peers = [(x + fx - 2 * x * fx, y + fy - 2 * y * fy, c + fc - 2 * c * fc) for fx, fy, fc in flips]
        sends = [_remote(vec, buf.at[me], send_sems, recv_sems, k, peer) for k, peer in enumerate(peers)]
        for cp in sends:
            cp.start()
        for k, (px, py, pc) in enumerate(peers):
            got = buf.at[4 * px + 2 * py + pc]
            _remote(got, got, send_sems, recv_sems, k, (px, py, pc)).wait_recv()
        for cp in sends:
            cp.wait_send()
        total = buf[0]
        for dev in range(1, N_DEV):
            total = total + buf[dev]
        k = 0
        for a in range(n):
            r, w = g_refs[a].shape
            g = total[starts[a]:starts[a] + r, 0:w]
            g_refs[a][...] = g
            if has_w[a]:
                m2 = ADAM_B1 * m_refs[k][...] + (1.0 - ADAM_B1) * g
                v2 = ADAM_B2 * v_refs[k][...] + (1.0 - ADAM_B2) * (g * g)
                m_hat = m2 / (1.0 - ADAM_B1 ** ADAM_STEP)
                v_hat = v2 / (1.0 - ADAM_B2 ** ADAM_STEP)
                upd_refs[3 * k][...] = -ADAM_LR * (m_hat / (jnp.sqrt(v_hat) + ADAM_EPS) + ADAM_WD * w_refs[k][...])
                upd_refs[3 * k + 1][...] = m2
                upd_refs[3 * k + 2][...] = v2
                k += 1

    ws = [w for w in weights if w is not None]
    g_shapes = [jax.ShapeDtypeStruct(p.shape if w is None else w.shape, F32) for p, w in zip(partials, weights)]
    u_shapes = [jax.ShapeDtypeStruct(w.shape, F32) for w in ws for _ in range(3)]
    vm = pl.BlockSpec(memory_space=pltpu.VMEM)
    n_args = n + 3 * n_w
    outs = pl.pallas_call(
        body, name=name, in_specs=[vm] * n_args, out_specs=[vm] * (n + 3 * n_w), out_shape=g_shapes + u_shapes,
        scratch_shapes=[pltpu.VMEM((rows, width), F32), pltpu.VMEM((N_DEV, rows, width), F32),
                        pltpu.SemaphoreType.DMA((7,)), pltpu.SemaphoreType.DMA((7,))],
    )(*partials, *ws, *[m for m in moments1 if m is not None], *[v for v in moments2 if v is not None])
    return outs[:n], outs[n:]


GATE_COL = 3 * SB_WIDTH + 3 * FOX_WIDTH + FOX_HEADS + MEM_WIDTH
FL_COL = QKV_WIDTH


GROUP_A_COLS = [(0, QKV_WIDTH), (FL_COL + FOX_HEADS, MEM_WIDTH)]
GROUP_B_COLS = [(GATE_COL, MIX_WIDTH), (FL_COL, FOX_HEADS)]


def _group_from_shards(shard_of, cw, spans, pad):
    parts = []
    for lo, width in spans:
        hi = lo + width
        for j in range(N_CHIPS):
            a, b = max(lo, j * cw), min(hi, (j + 1) * cw)
            if a < b:
                parts.append(shard_of(j)[:, a - j * cw:b - j * cw])
    if pad:
        parts.append(jnp.zeros((parts[0].shape[0], pad), parts[0].dtype))
    return jnp.concatenate(parts, axis=1)


def _shard_from_groups(ga, gb, j, cw):
    lo, hi = j * cw, (j + 1) * cw
    placed = []
    for grp, spans in ((ga, GROUP_A_COLS), (gb, GROUP_B_COLS)):
        at = 0
        for first, width in spans:
            a, b = max(lo, first), min(hi, first + width)
            if a < b:
                placed.append((a, grp[:, at + a - first:at + b - first]))
            at += width
    return jnp.concatenate([p for _, p in sorted(placed, key=lambda t: t[0])], axis=1)


def _tile_of(n, cap, unit):
    if n <= cap:
        return n
    best = None
    for t in range(unit, cap + 1, unit):
        if n % t == 0:
            best = t
    assert best is not None, (n, cap, unit)
    return best


def _column_major_rows(a):
    dp, r, c = a.shape
    return a.transpose(2, 0, 1).reshape(c, dp, r // LANES, LANES).transpose(0, 2, 1, 3).reshape(-1, 8, LANES)


def _from_column_major_rows(b, shape):
    dp, r, c = shape
    return b.reshape(c, r // LANES, dp, LANES).transpose(0, 2, 1, 3).reshape(c, dp, r).transpose(1, 2, 0)


def kernel(x, mem, norm_w, w_in, b_forget, mem_norm_w, w_mem_kv, out_norm_w, w_out, final_norm_w, loss_target, m_norm_w, m_w_in, m_b_forget, m_mem_norm_w, m_w_mem_kv, m_out_norm_w, m_w_out, m_final_norm_w, v_norm_w, v_w_in, v_b_forget, v_mem_norm_w, v_w_mem_kv, v_out_norm_w, v_w_out, v_final_norm_w):
    xs = x[0]
    mems = mem[0]
    target = loss_target[0]
    s, d = xs.shape
    depth = norm_w.shape[0]
    nb = s // TILE
    ts = _tile_of(s, 512, 8)
    big = (w_in, w_mem_kv, w_out)
    core = lax.axis_index("c")
    chip = 2 * lax.axis_index("x") + lax.axis_index("y")
    cvec = core.astype(jnp.int32).reshape(1)
    mvec = chip.astype(jnp.int32).reshape(1)
    cw = w_in.shape[2]

    own_w = [[a[l].astype(BF16) for a in big] for l in range(depth)]

    def lay_out_in(own, got):
        shard_of = lambda j: jnp.where(chip == j, own, got[j])
        return (_group_from_shards(shard_of, cw, GROUP_A_COLS, 0),
                _group_from_shards(shard_of, cw, GROUP_B_COLS, LANES - FOX_HEADS))

    def lay_out_rows(own, got):
        full = jnp.where(lax.broadcasted_iota(jnp.int32, got.shape, 0) == chip, own[None], got)
        return full.reshape(-1, full.shape[2])

    w_in_groups = [lay_out_in(own_w[0][0], _run_exchange("gather_weights0", _gather_exchange(own_w[0][:1]))[0])]
    layer_w = []

    tm = _tile_of(s, 256, 8)
    fl_block = MIX_WIDTH // LANES

    saved = []
    cur = xs
    for l in range(depth):
        wa, wb = w_in_groups[l]
        h = _rms_fwd(f"rms_fwd{l}", cur, norm_w[l][None], ts)
        pa = _mm(f"inproj_a{l}", h, wa, "nn", tm, PA, BF16)
        pb = _mm(f"inproj_b{l}", h, wb, "nn", tm, PB, F32)
        bpad = jnp.pad(b_forget[l], (0, LANES - FOX_HEADS))[None]
        ccol4, crow4 = _gate_fwd(f"gate_fwd{l}", pb, bpad, fl_block)
        more = l + 1 < depth
        o_sb, got = _sb_fwd(f"sb_fwd{l}", pa, 0, carried=_gather_exchange(own_w[l][1:]))
        wkv, wout = lay_out_rows(own_w[l][1], got[0]), lay_out_rows(own_w[l][2], got[1])
        layer_w.append((wa, wb, wkv, wout))
        o_fx, lse_fx, got = _fox_fwd(f"fox_fwd{l}", pa, 3 * SB_WIDTH, ccol4, crow4,
                                     carried=_gather_exchange(own_w[l + 1][:1]) if more else None)
        if more:
            w_in_groups.append(lay_out_in(own_w[l + 1][0], got[0]))
        mn = _rms_fwd(f"mem_rms{l}", mems, mem_norm_w[l][None], mems.shape[0])
        mkv = _mm(f"mem_kv{l}", mn, wkv, "nn", mems.shape[0], 2 * MEM_WIDTH, BF16)
        o_m, lse_m = _mem_fwd(f"mem_fwd{l}", pa, mkv)
        nxt, y2 = _out_fwd(f"out_fwd{l}", o_sb, o_fx, o_m, pb, out_norm_w[l][None], cur, wout, tm)
        saved.append((cur, h, pa, pb, bpad, ccol4, crow4, o_sb, o_fx, lse_fx, mn, mkv, o_m, lse_m, y2))
        cur = nxt

    loss_v, dx, dxb, g_final = _final_loss("final_loss", cur, final_norm_w[None], target, ts)

    g_norm, g_b, g_memnorm, g_outnorm = [None] * depth, [None] * depth, [None] * depth, [None] * depth
    g_wa, g_wb, g_wkv, g_wout = [None] * depth, [None] * depth, [None] * depth, [None] * depth
    g_own = [[None] * depth for _ in big]
    g_other = [[None] * depth for _ in big]

    def swap_of(jobs):
        return _swap_exchange([g for _, _, g, _ in jobs])

    def chip_sums(jobs, got):
        return [(lr, k, _add_half(f"grad_add_half{lr}_{k}", g, r_, cvec, t_), t_) for (lr, k, g, t_), r_ in zip(jobs, got)]

    def sum_at_owner(jobs, from_chips):
        return [_sum_chips(f"grad_sum_chips{lr}_{k}", h_, r_, mvec, t_) for (lr, k, h_, t_), r_ in zip(jobs, from_chips)]

    def keep(jobs, halves, others):
        for (lr, k, _, _), mine, other in zip(jobs, halves, others):
            g_own[k][lr], g_other[k][lr] = mine, other

    def job(lr, k, g4):
        return lr, k, g4, _tile_of(g4.shape[1] // 2, 256, 16)

    pending = []
    for l in reversed(range(depth)):
        xin, h, pa, pb, bpad, ccol4, crow4, o_sb, o_fx, lse_fx, mn, mkv, o_m, lse_m, y2 = saved[l]
        wa, wb, wkv, wout = layer_w[l]
        dy, dgate, g_outnorm[l] = _out_bwd(f"out_bwd{l}", dxb, o_sb, o_fx, o_m, pb, out_norm_w[l][None], wout, tm)
        g_wout[l] = _mm(f"dw_out{l}", y2, dxb, "tn", _tile_of(MIX_WIDTH, 640, LANES), d, F32)
        dq_m, dmkv = _mem_bwd(f"mem_bwd{l}", pa, mkv, o_m, lse_m, dy, SB_WIDTH + FOX_WIDTH)
        g_wkv[l] = _mm(f"dw_kv{l}", mn, dmkv, "tn", d, 2 * MEM_WIDTH, F32)
        dmn = _mm(f"dmem{l}", dmkv, wkv, "nt", mems.shape[0], d, F32)
        g_memnorm[l] = _rms_wgrad(f"mem_norm_grad{l}", mems, dmn)
        small = [job(l, 1, g_wkv[l].reshape(N_CHIPS, -1, g_wkv[l].shape[1])), job(l, 2, g_wout[l].reshape(N_CHIPS, -1, d))]
        (dq_fx, dk_fx, dv_fx, cs4), got = _fox_bwd(f"fox_bwd{l}", pa, 3 * SB_WIDTH, ccol4, crow4, o_fx, lse_fx, dy,
                                                    SB_WIDTH, carried=swap_of(small))
        pending += chip_sums(small, got)
        (dq_sb, dk_sb, dv_sb), from_chips = _sb_bwd(f"sb_bwd{l}", pa, 0, dy, 0,
                                                   carried=_scatter_exchange([j[2] for j in pending]))
        reduced_jobs, reduced = pending, sum_at_owner(pending, from_chips)
        swap_back = _swap_reduced_exchange(reduced)
        dpb, g_b[l] = _gate_bwd(f"gate_bwd{l}", pb, bpad, cs4, fl_block, dgate)
        dpa = jnp.concatenate([dq_sb, dk_sb, dv_sb, dq_fx, dk_fx, dv_fx, dq_m], axis=1)
        tw = _tile_of(d, 512, LANES)
        g_wa[l] = _mm(f"dw_in_a{l}", h, dpa, "tn", tw, _tile_of(PA, 1664, LANES), BF16)
        g_wb[l] = _mm(f"dw_in_b{l}", h, dpb, "tn", tw, PB, BF16)
        g4_in = jnp.stack([_shard_from_groups(g_wa[l], g_wb[l], j, cw) for j in range(N_CHIPS)])
        w_in_job = [job(l, 0, g4_in)]
        if l > 0:
            dx, dxb, g_norm[l], got = _inproj_bwd(f"inproj_bwd{l}", dpa, dpb, wa, wb, xin, norm_w[l][None], dx, tm,
                                                  carried=_both(swap_of(w_in_job), swap_back))
            pending = chip_sums(w_in_job, got[:1])
            keep(reduced_jobs, reduced, got[1:])
        else:
            pending = chip_sums(w_in_job, _run_exchange("grad_swap_halves_last", swap_of(w_in_job)))
            dx, dxb, g_norm[l], got = _inproj_bwd(f"inproj_bwd{l}", dpa, dpb, wa, wb, xin, norm_w[l][None], dx, tm,
                                                  carried=_both(_scatter_exchange([j[2] for j in pending]), swap_back))
            keep(reduced_jobs, reduced, got[1:])
            last = sum_at_owner(pending, got[:1])
            keep(pending, last, _run_exchange("grad_swap_reduced_last", _swap_reduced_exchange(last)))

    small_w = [norm_w, b_forget, mem_norm_w, out_norm_w, final_norm_w]
    small_m = [m_norm_w, m_b_forget, m_mem_norm_w, m_out_norm_w, m_final_norm_w]
    small_v = [v_norm_w, v_b_forget, v_mem_norm_w, v_out_norm_w, v_final_norm_w]
    rows2 = lambda a: a.reshape(-1, a.shape[-1])
    partials = [jnp.concatenate(g_norm, axis=0), jnp.concatenate(g_b, axis=0), jnp.concatenate(g_memnorm, axis=0),
                jnp.concatenate(g_outnorm, axis=0), g_final, loss_v]
    sums, updates = _small_update("small_update", partials, [rows2(a) for a in small_w] + [None],
                                  [rows2(a) for a in small_m] + [None], [rows2(a) for a in small_v] + [None])
    small_grads = [g.reshape(a.shape) for g, a in zip(sums, small_w)]
    loss = sums[-1][0, 0]
    small_delta, small_m2, small_v2 = ([updates[3 * k + t].reshape(a.shape) for k, a in enumerate(small_w)]
                                       for t in range(3))
    big_grads, big_delta, big_m2, big_v2 = [], [], [], []
    for k, (nm, w_, m_, v_) in enumerate(zip(("w_in", "w_mem_kv", "w_out"), big, (m_w_in, m_w_mem_kv, m_w_out),
                                             (v_w_in, v_w_mem_kv, v_w_out))):
        if w_.shape[2] % LANES:
            g_full = jnp.stack([jnp.concatenate([jnp.where(core == 0, go, gt), jnp.where(core == 0, gt, go)], axis=0)
                                for go, gt in zip(g_own[k], g_other[k])])
            w_p, g_p, m_p, v_p = (_column_major_rows(a) for a in (w_, g_full, m_, v_))
            outs = _adamw(f"adamw_{nm}", w_p, g_p, m_p, v_p, _tile_of(w_p.shape[0], 600, 1))
            outs = [_from_column_major_rows(o, w_.shape) for o in (g_p, *outs)]
        else:
            outs = _adamw_sharded(f"adamw_{nm}", w_, m_, v_, g_own[k], g_other[k], cvec,
                                  _tile_of(w_.shape[1] // 2, 256, 8))
        for lst, o in zip((big_grads, big_delta, big_m2, big_v2), outs):
            lst.append(o)

    def order(sm, bg):
        return [sm[0], bg[0], sm[1], sm[2], bg[1], sm[3], bg[2], sm[4]]

    return (loss, dx[None], *order(small_grads, big_grads), *order(small_delta, big_delta),
            *order(small_m2, big_m2), *order(small_v2, big_v2))
```

```python
import functools

import jax
import jax.numpy as jnp
from jax import lax
from jax.experimental import pallas as pl
from jax.experimental.pallas import tpu as pltpu

F32 = jnp.float32
BF16 = jnp.bfloat16

HEAD_DIM = 64
SB_WIDTH = 512
FOX_WIDTH = 512
FOX_HEADS = 8
MEM_WIDTH = 256
MEM_HEADS = MEM_WIDTH // HEAD_DIM
MIX_WIDTH = SB_WIDTH + FOX_WIDTH + MEM_WIDTH
TOTAL_HEADS = MIX_WIDTH // HEAD_DIM
IN_WIDTH = 3 * SB_WIDTH + 3 * FOX_WIDTH + FOX_HEADS + MEM_WIDTH + MIX_WIDTH
LANES = 128
QKV_WIDTH = 3 * SB_WIDTH + 3 * FOX_WIDTH
PA = QKV_WIDTH + MEM_WIDTH
PB = LANES + MIX_WIDTH
EPS = 1e-6
SCALE = HEAD_DIM ** -0.5
TILE = 256
SB_GROUP = 4
SB_LANES = SB_GROUP * HEAD_DIM
SB_FWD_GROUP = 8
FOX_GROUP = 4
FOX_LANES = FOX_GROUP * HEAD_DIM
FOX_FWD_GROUP = 4
NEG_INF = float("-inf")
MASKED = -1e30

ADAM_LR = 0.001
ADAM_B1 = 0.9
ADAM_B2 = 0.999
ADAM_EPS = 1e-08
ADAM_WD = 0.01
ADAM_STEP = 10

N_CHIPS = 4
N_DEV = 8
VMEM_LIMIT = 48 * 1024 * 1024
MESH = pl.DeviceIdType.MESH


def _params(*sem):
    return pltpu.CompilerParams(dimension_semantics=tuple(sem), vmem_limit_bytes=VMEM_LIMIT)


def _dot(a, b):
    return jnp.dot(a, b, preferred_element_type=F32)


def _dot_nt(a, b):
    return lax.dot_general(a, b, (((1,), (1,)), ((), ())), preferred_element_type=F32)


def _dot_tn(a, b):
    return lax.dot_general(a, b, (((0,), (0,)), ((), ())), preferred_element_type=F32)


def _split2(x):
    hi = x.astype(BF16)
    lo = (x - hi.astype(F32)).astype(BF16)
    return hi, lo


def _split3(x):
    hi = x.astype(BF16)
    r = x - hi.astype(F32)
    mid = r.astype(BF16)
    lo = (r - mid.astype(F32)).astype(BF16)
    return hi, mid, lo


def _sum_l2(x, u):
    hi, lo = _split2(x)
    return _dot(hi, u) + _dot(lo, u)


def _sum_l3(x, u):
    hi, mid, lo = _split3(x)
    return _dot(hi, u) + _dot(mid, u) + _dot(lo, u)


def _sum_r3(u, x):
    hi, mid, lo = _split3(x)
    return _dot(u, hi) + _dot(u, mid) + _dot(u, lo)


def _tri(n, pred):
    r = lax.broadcasted_iota(jnp.int32, (n, n), 0)
    c = lax.broadcasted_iota(jnp.int32, (n, n), 1)
    return jnp.where(pred(r, c), 1.0, 0.0).astype(BF16)


def _rows(ref, j, n=TILE):
    return pl.ds(pl.multiple_of(j * n, n), n)


def _mm(name, a, b, mode, tm, tn, out_dtype, res=None, a_lead=(), b_lead=()):
    a2, b2 = a.shape[len(a_lead):], b.shape[len(b_lead):]
    if mode == "tn":
        k, m = a2
    else:
        m, k = a2
    n = b2[0] if mode == "nt" else b2[1]
    assert m % tm == 0 and n % tn == 0, (name, m, tm, n, tn)
    na, nb = (None,) * len(a_lead), (None,) * len(b_lead)
    if mode == "tn":
        a_spec = pl.BlockSpec(na + (k, tm), lambda j, i: a_lead + (0, i))
    else:
        a_spec = pl.BlockSpec(na + (tm, k), lambda j, i: a_lead + (i, 0))
    if mode == "nt":
        b_spec = pl.BlockSpec(nb + (tn, k), lambda j, i: b_lead + (j, 0))
    else:
        b_spec = pl.BlockSpec(nb + (k, tn), lambda j, i: b_lead + (0, j))
    o_spec = pl.BlockSpec((tm, tn), lambda j, i: (i, j))
    dot = {"nn": _dot, "nt": _dot_nt, "tn": _dot_tn}[mode]

    def body(a_ref, b_ref, *rest):
        o_ref = rest[-1]
        acc = dot(a_ref[...].astype(BF16), b_ref[...].astype(BF16))
        if res is not None:
            acc = acc + rest[0][...]
        o_ref[...] = acc.astype(o_ref.dtype)

    args, specs = [a, b], [a_spec, b_spec]
    if res is not None:
        args.append(res)
        specs.append(o_spec)
    return pl.pallas_call(
        body, name=name, grid=(n // tn, m // tm), in_specs=specs, out_specs=o_spec,
        out_shape=jax.ShapeDtypeStruct((m, n), out_dtype),
        compiler_params=_params("parallel", "parallel"),
    )(*args)


def _rms_fwd(name, x, g, ts):
    s, d = x.shape

    def body(x_ref, g_ref, o_ref):
        xf = x_ref[...]
        r = lax.rsqrt(jnp.mean(xf * xf, axis=1, keepdims=True) + EPS)
        o_ref[...] = (xf * r * g_ref[...]).astype(BF16)

    return pl.pallas_call(
        body, name=name, grid=(s // ts,),
        in_specs=[pl.BlockSpec((ts, d), lambda i: (i, 0)), pl.BlockSpec((1, d), lambda i: (0, 0))],
        out_specs=pl.BlockSpec((ts, d), lambda i: (i, 0)),
        out_shape=jax.ShapeDtypeStruct((s, d), BF16),
        compiler_params=_params("parallel"),
    )(x, g)


def _inproj_bwd(name, dpa, dpb, wa, wb, x, g, dres, ts, carried=None):
    s, d = x.shape

    def body(dpa_ref, dpb_ref, wa_ref, wb_ref, x_ref, g_ref, dres_ref, dx_ref, dxb_ref, dg_ref):
        @pl.when(pl.program_id(1) == 0)
        def _():
            dg_ref[...] = jnp.zeros_like(dg_ref)

        dhf = _dot_nt(dpa_ref[...], wa_ref[...]) + _dot_nt(dpb_ref[...], wb_ref[...])
        xf = x_ref[...]
        r = lax.rsqrt(jnp.mean(xf * xf, axis=1, keepdims=True) + EPS)
        xh = xf * r
        dg_ref[...] += jnp.sum(dhf * xh, axis=0, keepdims=True)
        dxh = dhf * g_ref[...]
        m = jnp.mean(dxh * xh, axis=1, keepdims=True)
        dx = r * (dxh - xh * m) + dres_ref[...]
        dx_ref[...] = dx
        dxb_ref[...] = dx.astype(BF16)

    row = lambda w: pl.BlockSpec((ts, w), lambda p, i: (i, 0))
    whole = lambda a: pl.BlockSpec(a.shape, lambda p, i: (0, 0))
    outs = _pair_grid_call(
        name, body, s // ts,
        in_specs=[row(dpa.shape[1]), row(dpb.shape[1]), whole(wa), whole(wb), row(d), whole(g), row(d)],
        out_specs=[row(d), row(d), pl.BlockSpec((1, d), lambda p, i: (0, 0))],
        out_shape=[jax.ShapeDtypeStruct((s, d), F32), jax.ShapeDtypeStruct((s, d), BF16),
                   jax.ShapeDtypeStruct((1, d), F32)],
        scratch=[], args=(dpa, dpb, wa, wb, x, g, dres), carried=carried, groups=1)
    return outs[0], outs[1], outs[2], outs[3:]


def _rms_wgrad(name, x, dh):
    m_, d = x.shape

    def body(x_ref, dh_ref, dg_ref):
        xf = x_ref[...]
        r = lax.rsqrt(jnp.mean(xf * xf, axis=1, keepdims=True) + EPS)
        dg_ref[...] = jnp.sum(dh_ref[...] * xf * r, axis=0, keepdims=True)

    return pl.pallas_call(
        body, name=name, out_shape=jax.ShapeDtypeStruct((1, d), F32),
    )(x, dh)


def _final_loss(name, x, g, target, ts):
    s, d = x.shape

    def body(x_ref, g_ref, t_ref, loss_ref, dx_ref, dxb_ref, dg_ref):
        @pl.when(pl.program_id(0) == 0)
        def _():
            dg_ref[...] = jnp.zeros_like(dg_ref)
            loss_ref[...] = jnp.zeros_like(loss_ref)

        xf = x_ref[...]
        gw = g_ref[...]
        r = lax.rsqrt(jnp.mean(xf * xf, axis=1, keepdims=True) + EPS)
        xh = xf * r
        e = xh * gw - t_ref[...]
        part = 0.5 * jnp.sum(jnp.mean(e * e, axis=1, keepdims=True), axis=0, keepdims=True)
        loss_ref[...] += jnp.broadcast_to(part, loss_ref.shape)
        dy = e * (1.0 / d)
        dg_ref[...] += jnp.sum(dy * xh, axis=0, keepdims=True)
        dxh = dy * gw
        m = jnp.mean(dxh * xh, axis=1, keepdims=True)
        dx = r * (dxh - xh * m)
        dx_ref[...] = dx
        dxb_ref[...] = dx.astype(BF16)

    row = pl.BlockSpec((ts, d), lambda i: (i, 0))
    vec = pl.BlockSpec((1, d), lambda i: (0, 0))
    lvec = pl.BlockSpec((1, LANES), lambda i: (0, 0))
    return pl.pallas_call(
        body, name=name, grid=(s // ts,), in_specs=[row, vec, row], out_specs=[lvec, row, row, vec],
        out_shape=[jax.ShapeDtypeStruct((1, LANES), F32), jax.ShapeDtypeStruct((s, d), F32),
                   jax.ShapeDtypeStruct((s, d), BF16), jax.ShapeDtypeStruct((1, d), F32)],
        compiler_params=_params("arbitrary"),
    )(x, g, target)


def _gate_fwd(name, pb, bpad, fl_block):
    s = pb.shape[0]
    nb = s // TILE

    def body(fl_ref, b_ref, ccol_ref, crow_ref, carry):
        @pl.when(pl.program_id(0) == 0)
        def _():
            carry[...] = jnp.zeros_like(carry)

        u = fl_ref[...] + b_ref[...]
        lf = jnp.minimum(u, 0.0) - jnp.log1p(jnp.exp(-jnp.abs(u)))
        lower = _tri(TILE, lambda r, c: c <= r)
        c = _sum_r3(lower, lf) + carry[0:1, :]
        ccol_ref[...] = jnp.concatenate(
            [jnp.broadcast_to(c[:, hh:hh + 1], (TILE, HEAD_DIM)) for hh in range(FOX_HEADS)], axis=1)
        crow_ref[0] = c.T[0:8, :]
        carry[...] = jnp.broadcast_to(c[TILE - 1:TILE, :], carry.shape)

    return pl.pallas_call(
        body, name=name, grid=(nb,),
        in_specs=[pl.BlockSpec((TILE, LANES), lambda i: (i, fl_block)), pl.BlockSpec((1, LANES), lambda i: (0, 0))],
        out_specs=[pl.BlockSpec((TILE, FOX_WIDTH), lambda i: (i, 0)), pl.BlockSpec((1, 8, TILE), lambda i: (i, 0, 0))],
        out_shape=[jax.ShapeDtypeStruct((s, FOX_WIDTH), F32), jax.ShapeDtypeStruct((nb, 8, TILE), F32)],
        scratch_shapes=[pltpu.VMEM((8, LANES), F32)],
        compiler_params=_params("arbitrary"),
    )(pb, bpad)


def _gate_bwd(name, pb, bpad, colsum, fl_block, dpb):
    s = pb.shape[0]
    nb = s // TILE

    def body(fl_ref, b_ref, cs_ref, dpb_ref, dl_ref, db_ref, carry):
        @pl.when(pl.program_id(0) == 0)
        def _():
            carry[...] = jnp.zeros_like(carry)
            db_ref[...] = jnp.zeros_like(db_ref)

        upper = _tri(TILE, lambda r, c: r >= c)
        rsum = _sum_l3(cs_ref[0], upper) + carry[:, 0:1]
        carry[...] = jnp.broadcast_to(rsum[:, 0:1], carry.shape)
        full = jnp.concatenate([rsum, jnp.zeros((LANES - 8, TILE), F32)], axis=0)
        dlf = -full.T
        u = fl_ref[...] + b_ref[...]
        dlogit = dlf * (1.0 - jax.nn.sigmoid(u))
        dl_ref[...] = dlogit.astype(BF16)
        db_ref[...] += jnp.sum(dlogit, axis=0, keepdims=True)

    logits_block = pl.BlockSpec((TILE, LANES), lambda i: (nb - 1 - i, fl_block))
    return pl.pallas_call(
        body, name=name, grid=(nb,),
        in_specs=[logits_block, pl.BlockSpec((1, LANES), lambda i: (0, 0)),
                  pl.BlockSpec((1, 8, TILE), lambda i: (nb - 1 - i, 0, 0)), pl.BlockSpec(memory_space=pl.ANY)],
        out_specs=[logits_block, pl.BlockSpec((1, LANES), lambda i: (0, 0))],
        out_shape=[jax.ShapeDtypeStruct(dpb.shape, BF16), jax.ShapeDtypeStruct((1, LANES), F32)],
        scratch_shapes=[pltpu.VMEM((8, LANES), F32)], input_output_aliases={3: 0},
        compiler_params=_params("arbitrary"),
    )(pb, bpad, colsum, dpb)


def _head_slices(hh):
    return slice(HEAD_DIM * hh, HEAD_DIM * (hh + 1))


def _scaled_q(q_ref, sl, scale=SCALE):
    return (q_ref[:, sl].astype(F32) * scale).astype(BF16)


def _neg_abs(x):
    sign = jnp.uint32(0x80000000)
    return lax.bitcast_convert_type(lax.bitcast_convert_type(x, jnp.uint32) | sign, F32)


def _pair_grid_call(name, body, nb, in_specs, out_specs, out_shape, scratch, args, carried=None, groups=4):
    if carried is None:
        return pl.pallas_call(
            body, name=name, grid=(groups, nb), in_specs=in_specs, out_specs=out_specs, out_shape=out_shape,
            scratch_shapes=scratch, compiler_params=_params("arbitrary", "arbitrary"),
        )(*args)
    n_in, n_out, n_ex = len(in_specs), len(out_specs), carried.n

    def body_with_copies(*refs):
        own_in, ex_in = refs[:n_in], refs[n_in:n_in + n_ex]
        own_out = refs[n_in + n_ex:n_in + n_ex + n_out]
        ex_out = refs[n_in + n_ex + n_out:n_in + 2 * n_ex + n_out]
        own_scratch, sems = refs[n_in + 2 * n_ex + n_out:-2], refs[-2:]
        parts = (ex_in, ex_out, sems[0], sems[1])
        p, i = pl.program_id(0), pl.program_id(1)
        pl.when(jnp.logical_and(p == 0, i == 0))(lambda: carried.begin(*parts))
        if carried.relay is not None:
            pl.when(jnp.logical_and(p == groups - 1, i == max(nb - 2, 0)))(lambda: carried.relay(*parts))
        body(*own_in, *own_out, *own_scratch)
        pl.when(jnp.logical_and(p == groups - 1, i == nb - 1))(lambda: carried.finish(*parts))

    return pl.pallas_call(
        body_with_copies, name=name, grid=(groups, nb), in_specs=list(in_specs) + [HBM_SPEC] * n_ex,
        out_specs=list(out_specs) + [HBM_SPEC] * n_ex, out_shape=list(out_shape) + carried.out_shapes,
        scratch_shapes=list(scratch) + _dma_sems(carried.n_sems),
        compiler_params=_params("arbitrary", "arbitrary"),
    )(*args, *carried.inputs)


def _sb_fwd(name, pa, col0, carried=None):
    s = pa.shape[0]
    nb = s // TILE
    heads, lanes = SB_FWD_GROUP, SB_FWD_GROUP * HEAD_DIM
    cb = col0 // lanes
    kb = SB_WIDTH // lanes

    def body(q_ref, k_ref, v_ref, o_ref, lsig_s, lf_s):
        i = pl.program_id(1)
        r = lax.broadcasted_iota(jnp.int32, (TILE, TILE), 0)
        c = lax.broadcasted_iota(jnp.int32, (TILE, TILE), 1)
        strict = c < r
        u_after = _tri(TILE, lambda rr, cc: rr > cc)
        qs = [_scaled_q(q_ref, _head_slices(hh), -SCALE) for hh in range(heads)]

        def neg_z(j):
            kblk = k_ref[_rows(k_ref, j), :]
            return [_dot_nt(qs[hh], kblk[:, _head_slices(hh)]) for hh in range(heads)]

        def scores(nzs, slot, diag):
            for hh, nz in enumerate(nzs):
                lf = jnp.minimum(nz, 0.0) - jnp.log(1.0 + jnp.exp(_neg_abs(nz)))
                lsig = lf - nz
                if diag:
                    lf = jnp.where(strict, lf, 0.0)
                    lsig = jnp.where(strict, lsig, MASKED)
                lsig_s[slot, hh] = lsig
                lf_s[slot, hh] = lf.astype(BF16)

        def weigh(j, slot, state):
            vblk = v_ref[_rows(v_ref, j), :]
            new = []
            for hh in range(heads):
                carry, acc = state[hh]
                lfb = lf_s[slot, hh]
                sx = _dot(lfb, u_after)
                a = jnp.exp(lsig_s[slot, hh] + sx + carry)
                new.append((carry + sx[:, 0:1] + lfb[:, 0:1].astype(F32),
                            acc + _dot(a.astype(BF16), vblk[:, _head_slices(hh)])))
            return tuple(new)

        def step(t, state):
            state = weigh(i - t + 1, (t - 1) % 2, state)
            scores(neg_z(i - t), t % 2, False)
            return state

        zero = (jnp.zeros((TILE, 1), F32), jnp.zeros((TILE, HEAD_DIM), F32))
        scores(neg_z(i), 0, True)
        state = lax.fori_loop(1, i + 1, step, (zero,) * heads)
        state = weigh(0, i % 2, state)
        o_ref[...] = jnp.concatenate([st[1] for st in state], axis=1)

    outs = _pair_grid_call(
        name, body, nb,
        in_specs=[pl.BlockSpec((TILE, lanes), lambda p, i: (i, cb + p)),
                  pl.BlockSpec((s, lanes), lambda p, i: (0, cb + kb + p)),
                  pl.BlockSpec((s, lanes), lambda p, i: (0, cb + 2 * kb + p))],
        out_specs=[pl.BlockSpec((TILE, lanes), lambda p, i: (i, p))],
        out_shape=[jax.ShapeDtypeStruct((s, SB_WIDTH), F32)],
        scratch=[pltpu.VMEM((2, heads, TILE, TILE), F32), pltpu.VMEM((2, heads, TILE, TILE), BF16)],
        args=(pa, pa, pa), carried=carried, groups=kb)
    return outs[0], outs[1:]


def _sb_bwd(name, pa, col0, dout, dcol0, carried=None):
    s = pa.shape[0]
    nb = s // TILE
    cb = col0 // SB_LANES
    kb = SB_WIDTH // SB_LANES
    db = dcol0 // SB_LANES

    def body(q_ref, k_ref, v_ref, do_ref, dq_ref, dk_ref, dv_ref, dk_acc, dv_acc, dpan, span, gsum, lsig_s, lf_s):
        i = pl.program_id(1)

        @pl.when(i == 0)
        def _():
            dk_acc[...] = jnp.zeros_like(dk_acc)
            dv_acc[...] = jnp.zeros_like(dv_acc)

        r = lax.broadcasted_iota(jnp.int32, (TILE, TILE), 0)
        c = lax.broadcasted_iota(jnp.int32, (TILE, TILE), 1)
        strict = c < r
        u_after = _tri(TILE, lambda rr, cc: rr > cc)
        u_before = _tri(TILE, lambda rr, cc: rr < cc)
        qs = [_scaled_q(q_ref, _head_slices(hh), -SCALE) for hh in range(SB_GROUP)]
        dos = [do_ref[:, _head_slices(hh)].astype(BF16) for hh in range(SB_GROUP)]
        dots = [do_ref[:, _head_slices(hh)].T.astype(BF16) for hh in range(SB_GROUP)]
        qts = [q.astype(F32).T.astype(BF16) for q in qs]

        def scores(j, slot, diag):
            kblk = k_ref[_rows(k_ref, j), :]
            for hh in range(SB_GROUP):
                nz = _dot_nt(qs[hh], kblk[:, _head_slices(hh)])
                lf = jnp.minimum(nz, 0.0) - jnp.log(1.0 + jnp.exp(_neg_abs(nz)))
                lsig = lf - nz
                if diag:
                    lf = jnp.where(strict, lf, 0.0)
                    lsig = jnp.where(strict, lsig, MASKED)
                lsig_s[slot, hh] = lsig
                lf_s[slot, hh] = lf.astype(BF16)

        def grads(j, slot, carries):
            vblk = v_ref[_rows(v_ref, j), :]
            new = []
            for hh in range(SB_GROUP):
                lfb = lf_s[slot, hh]
                lsig = lsig_s[slot, hh]
                sx = _dot(lfb, u_after)
                a = jnp.exp(lsig + sx + carries[hh])
                g = a * _dot_nt(dos[hh], vblk[:, _head_slices(hh)])
                sig = jnp.exp(lsig)
                inside = _dot(g.astype(BF16), u_before)
                dpan[hh, j] = sig * (inside + g) - g
                span[hh, j] = sig
                gsum[hh, j] = inside[:, TILE - 1:TILE] + g[:, TILE - 1:TILE]
                dv_acc[hh, j] += _dot(dots[hh], a.astype(BF16))
                new.append(carries[hh] + sx[:, 0:1] + lfb[:, 0:1].astype(F32))
            return tuple(new)

        def step1(t, carries):
            carries = grads(i - t + 1, (t - 1) % 2, carries)
            scores(i - t, t % 2, False)
            return carries

        zero1 = jnp.zeros((TILE, 1), F32)
        scores(i, 0, True)
        carries = lax.fori_loop(1, i + 1, step1, (zero1,) * SB_GROUP)
        grads(0, i % 2, carries)

        def pass2(j, state):
            kblk = k_ref[_rows(k_ref, j), :]
            new = []
            for hh in range(SB_GROUP):
                before, ndq = state[hh]
                ndzb = (dpan[hh, j] + span[hh, j] * before).astype(BF16)
                dk_acc[hh, j] += _dot(qts[hh], ndzb)
                new.append((before + gsum[hh, j], ndq + _dot(ndzb, kblk[:, _head_slices(hh)])))
            return tuple(new)

        zero2 = (zero1, jnp.zeros((TILE, HEAD_DIM), F32))
        state = lax.fori_loop(0, i + 1, pass2, (zero2,) * SB_GROUP)
        dq_ref[...] = jnp.concatenate([st[1] * -SCALE for st in state], axis=1).astype(BF16)

        @pl.when(i == nb - 1)
        def _():
            for acc, ref in ((dk_acc, dk_ref), (dv_acc, dv_ref)):
                for j in range(nb):
                    ref[j * TILE:(j + 1) * TILE, :] = jnp.concatenate(
                        [acc[hh, j].T for hh in range(SB_GROUP)], axis=1).astype(BF16)

    qspec = pl.BlockSpec((TILE, SB_LANES), lambda p, i: (i, p))
    kvspec = pl.BlockSpec((s, SB_LANES), lambda p, i: (0, p))
    out = jax.ShapeDtypeStruct((s, SB_WIDTH), BF16)
    outs = _pair_grid_call(
        name, body, nb,
        in_specs=[pl.BlockSpec((TILE, SB_LANES), lambda p, i: (i, cb + p)),
                  pl.BlockSpec((s, SB_LANES), lambda p, i: (0, cb + kb + p)),
                  pl.BlockSpec((s, SB_LANES), lambda p, i: (0, cb + 2 * kb + p)),
                  pl.BlockSpec((TILE, SB_LANES), lambda p, i: (i, db + p))],
        out_specs=[qspec, kvspec, kvspec], out_shape=[out, out, out],
        scratch=[pltpu.VMEM((SB_GROUP, nb, HEAD_DIM, TILE), F32), pltpu.VMEM((SB_GROUP, nb, HEAD_DIM, TILE), F32),
                 pltpu.VMEM((SB_GROUP, nb, TILE, TILE), F32), pltpu.VMEM((SB_GROUP, nb, TILE, TILE), F32),
                 pltpu.VMEM((SB_GROUP, nb, TILE, 1), F32),
                 pltpu.VMEM((2, SB_GROUP, TILE, TILE), F32), pltpu.VMEM((2, SB_GROUP, TILE, TILE), BF16)],
        args=(pa, pa, pa, dout), carried=carried, groups=kb)
    return outs[:3], outs[3:]


def _fox_scores(q, kj, cq, crj, causal, diag):
    sc = _dot_nt(q, kj) + (cq - crj)
    if diag:
        sc = jnp.where(causal, sc, NEG_INF)
    return sc


def _fox_fwd(name, pa, col0, ccol4, crow4, carried=None):
    s = pa.shape[0]
    nb = s // TILE
    heads, lanes = FOX_FWD_GROUP, FOX_FWD_GROUP * HEAD_DIM
    cb = col0 // lanes
    kb = FOX_WIDTH // lanes

    def body(q_ref, k_ref, v_ref, cc_ref, cr_ref, o_ref, lse_ref, sc_s):
        i = pl.program_id(1)
        head0 = pl.program_id(0) * heads
        r = lax.broadcasted_iota(jnp.int32, (TILE, TILE), 0)
        c = lax.broadcasted_iota(jnp.int32, (TILE, TILE), 1)
        causal = c <= r
        qs = [_scaled_q(q_ref, _head_slices(hh)) for hh in range(heads)]
        cqs = [cc_ref[:, HEAD_DIM * hh:HEAD_DIM * hh + 1] for hh in range(heads)]

        def logits(j, slot, diag):
            kblk = k_ref[_rows(k_ref, j), :]
            tops = []
            for hh in range(heads):
                sc = _fox_scores(qs[hh], kblk[:, _head_slices(hh)], cqs[hh], cr_ref[j, pl.ds(head0 + hh, 1), :], causal, diag)
                sc_s[slot, hh] = sc
                tops.append(jnp.max(sc, axis=1, keepdims=True))
            return tuple(tops)

        def update(j, slot, tops, state):
            vblk = v_ref[_rows(v_ref, j), :]
            new = []
            for hh in range(heads):
                m, l, acc = state[hh]
                m2 = jnp.maximum(m, tops[hh])
                alpha = jnp.exp(m - m2)
                p = jnp.exp(sc_s[slot, hh] - m2)
                new.append((m2, l * alpha + jnp.sum(p, axis=1, keepdims=True),
                            acc * alpha + _dot(p.astype(BF16), vblk[:, _head_slices(hh)])))
            return tuple(new)

        def step(t, both):
            tops, state = both
            state = update(i - t + 1, (t - 1) % 2, tops, state)
            return logits(i - t, t % 2, False), state

        zero = (jnp.full((TILE, 1), NEG_INF, F32), jnp.zeros((TILE, 1), F32), jnp.zeros((TILE, HEAD_DIM), F32))
        tops, state = lax.fori_loop(1, i + 1, step, (logits(i, 0, True), (zero,) * heads))
        state = update(0, i % 2, tops, state)
        o_ref[...] = jnp.concatenate([st[2] / st[1] for st in state], axis=1)
        lse_ref[...] = jnp.concatenate(
            [jnp.broadcast_to(st[0] + jnp.log(st[1]), (TILE, HEAD_DIM)) for st in state], axis=1)

    outs = _pair_grid_call(
        name, body, nb,
        in_specs=[pl.BlockSpec((TILE, lanes), lambda p, i: (i, cb + p)),
                  pl.BlockSpec((s, lanes), lambda p, i: (0, cb + kb + p)),
                  pl.BlockSpec((s, lanes), lambda p, i: (0, cb + 2 * kb + p)),
                  pl.BlockSpec((TILE, lanes), lambda p, i: (i, p)),
                  pl.BlockSpec((nb, 8, TILE), lambda p, i: (0, 0, 0))],
        out_specs=[pl.BlockSpec((TILE, lanes), lambda p, i: (i, p)), pl.BlockSpec((TILE, lanes), lambda p, i: (i, p))],
        out_shape=[jax.ShapeDtypeStruct((s, FOX_WIDTH), F32), jax.ShapeDtypeStruct((s, FOX_WIDTH), F32)],
        scratch=[pltpu.VMEM((2, heads, TILE, TILE), F32)],
        args=(pa, pa, pa, ccol4, crow4), carried=carried, groups=kb)
    return outs[0], outs[1], outs[2:]


def _fox_bwd(name, pa, col0, ccol4, crow4, out, lse, dout, dcol0, carried=None):
    s = pa.shape[0]
    nb = s // TILE
    cb = col0 // FOX_LANES
    kb = FOX_WIDTH // FOX_LANES
    db = dcol0 // FOX_LANES

    def body(q_ref, k_ref, v_ref, cc_ref, cr_ref, o_ref, lse_ref, do_ref,
             dq_ref, dk_ref, dv_ref, cs_ref, dk_acc, dv_acc, p_s, ds_s):
        i = pl.program_id(1)
        head0 = pl.program_id(0) * FOX_GROUP

        @pl.when(i == 0)
        def _():
            dk_acc[...] = jnp.zeros_like(dk_acc)
            dv_acc[...] = jnp.zeros_like(dv_acc)

        @pl.when(jnp.logical_and(i == 0, head0 == 0))
        def _():
            cs_ref[...] = jnp.zeros_like(cs_ref)

        r = lax.broadcasted_iota(jnp.int32, (TILE, TILE), 0)
        c = lax.broadcasted_iota(jnp.int32, (TILE, TILE), 1)
        causal = c <= r
        qs = [_scaled_q(q_ref, _head_slices(hh)) for hh in range(FOX_GROUP)]
        cqs = [cc_ref[:, HEAD_DIM * hh:HEAD_DIM * hh + 1] for hh in range(FOX_GROUP)]
        lses = [lse_ref[:, HEAD_DIM * hh:HEAD_DIM * hh + 1] for hh in range(FOX_GROUP)]
        dofs = [do_ref[:, _head_slices(hh)] for hh in range(FOX_GROUP)]
        dos = [d_.astype(BF16) for d_ in dofs]
        dots = [d_.T.astype(BF16) for d_ in dofs]
        qts = [q.astype(F32).T.astype(BF16) for q in qs]
        deltas = [jnp.sum(dofs[hh] * o_ref[:, _head_slices(hh)], axis=1, keepdims=True) for hh in range(FOX_GROUP)]

        def probs(j, slot, rowsums, diag):
            kblk = k_ref[_rows(k_ref, j), :]
            vblk = v_ref[_rows(v_ref, j), :]
            new = []
            for hh in range(FOX_GROUP):
                sl = _head_slices(hh)
                sc = _fox_scores(qs[hh], kblk[:, sl], cqs[hh], cr_ref[j, pl.ds(head0 + hh, 1), :], causal, diag)
                p = jnp.exp(sc - lses[hh])
                ds = p * (_dot_nt(dos[hh], vblk[:, sl]) - deltas[hh])
                p_s[slot, hh] = p.astype(BF16)
                ds_s[slot, hh] = ds.astype(BF16)
                cs_ref[j, pl.ds(head0 + hh, 1), :] += jnp.sum(ds, axis=0, keepdims=True)
                new.append(rowsums[hh] + jnp.sum(ds, axis=1, keepdims=True))
            return tuple(new)

        def accumulate(j, slot, dqs):
            kblk = k_ref[_rows(k_ref, j), :]
            new = []
            for hh in range(FOX_GROUP):
                dsb = ds_s[slot, hh]
                dv_acc[hh, j] += _dot(dots[hh], p_s[slot, hh])
                dk_acc[hh, j] += _dot(qts[hh], dsb)
                new.append(dqs[hh] + _dot(dsb, kblk[:, _head_slices(hh)]))
            return tuple(new)

        def step(t, both):
            rowsums, dqs = both
            dqs = accumulate(i - t + 1, (t - 1) % 2, dqs)
            return probs(i - t, t % 2, rowsums, False), dqs

        zero1 = jnp.zeros((TILE, 1), F32)
        zero64 = jnp.zeros((TILE, HEAD_DIM), F32)
        rowsums, dqs = lax.fori_loop(1, i + 1, step,
                                     (probs(i, 0, (zero1,) * FOX_GROUP, True), (zero64,) * FOX_GROUP))
        dqs = accumulate(0, i % 2, dqs)
        for hh in range(FOX_GROUP):
            cs_ref[i, pl.ds(head0 + hh, 1), :] -= jnp.broadcast_to(rowsums[hh], (TILE, LANES)).T[0:1, :]
        dq_ref[...] = jnp.concatenate([dq * SCALE for dq in dqs], axis=1).astype(BF16)

        @pl.when(i == nb - 1)
        def _():
            for acc, ref in ((dk_acc, dk_ref), (dv_acc, dv_ref)):
                for j in range(nb):
                    ref[j * TILE:(j + 1) * TILE, :] = jnp.concatenate(
                        [acc[hh, j].T for hh in range(FOX_GROUP)], axis=1).astype(BF16)

    qspec = pl.BlockSpec((TILE, FOX_LANES), lambda p, i: (i, p))
    kvspec = pl.BlockSpec((s, FOX_LANES), lambda p, i: (0, p))
    o3 = jax.ShapeDtypeStruct((s, FOX_WIDTH), BF16)
    outs = _pair_grid_call(
        name, body, nb,
        in_specs=[pl.BlockSpec((TILE, FOX_LANES), lambda p, i: (i, cb + p)),
                  pl.BlockSpec((s, FOX_LANES), lambda p, i: (0, cb + kb + p)),
                  pl.BlockSpec((s, FOX_LANES), lambda p, i: (0, cb + 2 * kb + p)),
                  pl.BlockSpec((TILE, FOX_LANES), lambda p, i: (i, p)),
                  pl.BlockSpec((nb, 8, TILE), lambda p, i: (0, 0, 0)),
                  qspec,
                  pl.BlockSpec((TILE, FOX_LANES), lambda p, i: (i, p)),
                  pl.BlockSpec((TILE, FOX_LANES), lambda p, i: (i, db + p))],
        out_specs=[qspec, kvspec, kvspec, pl.BlockSpec((nb, 8, TILE), lambda p, i: (0, 0, 0))],
        out_shape=[o3, o3, o3, jax.ShapeDtypeStruct((nb, 8, TILE), F32)],
        scratch=[pltpu.VMEM((FOX_GROUP, nb, HEAD_DIM, TILE), F32), pltpu.VMEM((FOX_GROUP, nb, HEAD_DIM, TILE), F32),
                 pltpu.VMEM((2, FOX_GROUP, TILE, TILE), BF16), pltpu.VMEM((2, FOX_GROUP, TILE, TILE), BF16)],
        args=(pa, pa, pa, ccol4, crow4, out, lse, dout), carried=carried, groups=kb)
    return outs[:4], outs[4:]


def _mem_fwd(name, pa, mkv):
    s = pa.shape[0]
    ml = mkv.shape[0]
    nb = s // TILE
    cb = QKV_WIDTH // MEM_WIDTH

    def body(q_ref, k_ref, v_ref, o_ref, lse_ref):
        outs, lses = [], []
        for hh in range(MEM_HEADS):
            sl = _head_slices(hh)
            sc = _dot_nt(_scaled_q(q_ref, sl), k_ref[:, sl])
            m = jnp.max(sc, axis=1, keepdims=True)
            p = jnp.exp(sc - m)
            l = jnp.sum(p, axis=1, keepdims=True)
            outs.append(_dot(p.astype(BF16), v_ref[:, sl]) / l)
            lses.append(jnp.broadcast_to(m + jnp.log(l), (TILE, HEAD_DIM)))
        o_ref[...] = jnp.concatenate(outs, axis=1)
        lse_ref[...] = jnp.concatenate(lses, axis=1)

    row = pl.BlockSpec((TILE, MEM_WIDTH), lambda i: (i, 0))
    return pl.pallas_call(
        body, name=name, grid=(nb,),
        in_specs=[pl.BlockSpec((TILE, MEM_WIDTH), lambda i: (i, cb)),
                  pl.BlockSpec((ml, MEM_WIDTH), lambda i: (0, 0)),
                  pl.BlockSpec((ml, MEM_WIDTH), lambda i: (0, 1))],
        out_specs=[row, row],
        out_shape=[jax.ShapeDtypeStruct((s, MEM_WIDTH), F32), jax.ShapeDtypeStruct((s, MEM_WIDTH), F32)],
        compiler_params=_params("parallel"),
    )(pa, mkv, mkv)


def _mem_bwd(name, pa, mkv, out, lse, dout, dcol0):
    s = pa.shape[0]
    ml = mkv.shape[0]
    nb = s // TILE
    cb = QKV_WIDTH // MEM_WIDTH
    db = dcol0 // MEM_WIDTH

    def body(q_ref, k_ref, v_ref, o_ref, lse_ref, do_ref, dq_ref, dkv_ref, dk_acc, dv_acc):
        i = pl.program_id(0)

        @pl.when(i == 0)
        def _():
            dk_acc[...] = jnp.zeros_like(dk_acc)
            dv_acc[...] = jnp.zeros_like(dv_acc)

        dqs = []
        for hh in range(MEM_HEADS):
            sl = _head_slices(hh)
            q = _scaled_q(q_ref, sl)
            kh = k_ref[:, sl]
            dof = do_ref[:, sl]
            do = dof.astype(BF16)
            delta = jnp.sum(dof * o_ref[:, sl], axis=1, keepdims=True)
            p = jnp.exp(_dot_nt(q, kh) - lse_ref[:, HEAD_DIM * hh:HEAD_DIM * hh + 1])
            ds = (p * (_dot_nt(do, v_ref[:, sl]) - delta)).astype(BF16)
            dv_acc[hh] += _dot(dof.T.astype(BF16), p.astype(BF16))
            dk_acc[hh] += _dot(q.astype(F32).T.astype(BF16), ds)
            dqs.append(_dot(ds, kh) * SCALE)
        dq_ref[...] = jnp.concatenate(dqs, axis=1).astype(BF16)

        @pl.when(i == nb - 1)
        def _():
            dkv_ref[...] = jnp.concatenate([dk_acc[hh].T for hh in range(MEM_HEADS)]
                                           + [dv_acc[hh].T for hh in range(MEM_HEADS)], axis=1).astype(BF16)

    row = pl.BlockSpec((TILE, MEM_WIDTH), lambda i: (i, 0))
    return pl.pallas_call(
        body, name=name, grid=(nb,),
        in_specs=[pl.BlockSpec((TILE, MEM_WIDTH), lambda i: (i, cb)),
                  pl.BlockSpec((ml, MEM_WIDTH), lambda i: (0, 0)),
                  pl.BlockSpec((ml, MEM_WIDTH), lambda i: (0, 1)),
                  row, row,
                  pl.BlockSpec((TILE, MEM_WIDTH), lambda i: (i, db))],
        out_specs=[row, pl.BlockSpec((ml, 2 * MEM_WIDTH), lambda i: (0, 0))],
        out_shape=[jax.ShapeDtypeStruct((s, MEM_WIDTH), BF16), jax.ShapeDtypeStruct((ml, 2 * MEM_WIDTH), BF16)],
        scratch_shapes=[pltpu.VMEM((MEM_HEADS, HEAD_DIM, ml), F32), pltpu.VMEM((MEM_HEADS, HEAD_DIM, ml), F32)],
        compiler_params=_params("arbitrary"),
    )(pa, mkv, mkv, out, lse, dout)


def _head_maps():
    col = jnp.arange(MIX_WIDTH)[:, None] // HEAD_DIM
    g = (col == jnp.arange(LANES)[None, :]).astype(BF16)
    return g, g.T


def _normed_heads(osb_ref, ofx_ref, om_ref, g_ref, gt_ref):
    y = jnp.concatenate([osb_ref[...], ofx_ref[...], om_ref[...]], axis=1)
    msq = _sum_l2(y * y, g_ref[...]) * (1.0 / HEAD_DIM)
    rf = _sum_l3(lax.rsqrt(msq + EPS), gt_ref[...])
    return y * rf, rf


def _out_fwd(name, o_sb, o_fx, o_m, pb, ow, x, w_out, ts):
    s, d = x.shape
    g, gt = _head_maps()

    def body(osb_ref, ofx_ref, om_ref, gate_ref, ow_ref, x_ref, w_ref, g_ref, gt_ref, xo_ref, y2_ref):
        yh, _ = _normed_heads(osb_ref, ofx_ref, om_ref, g_ref, gt_ref)
        gate = gate_ref[...]
        y2 = (yh * ow_ref[...] * (gate * jax.nn.sigmoid(gate))).astype(BF16)
        y2_ref[...] = y2
        xo_ref[...] = x_ref[...] + _dot(y2, w_ref[...])

    return pl.pallas_call(
        body, name=name, grid=(s // ts,),
        in_specs=[_row_spec(ts, SB_WIDTH), _row_spec(ts, FOX_WIDTH), _row_spec(ts, MEM_WIDTH),
                  _row_spec(ts, MIX_WIDTH), _const_spec((1, MIX_WIDTH)), _row_spec(ts, d),
                  _const_spec((MIX_WIDTH, d)),
                  _const_spec((MIX_WIDTH, LANES)), _const_spec((LANES, MIX_WIDTH))],
        out_specs=[_row_spec(ts, d), _row_spec(ts, MIX_WIDTH)],
        out_shape=[jax.ShapeDtypeStruct((s, d), F32), jax.ShapeDtypeStruct((s, MIX_WIDTH), BF16)],
        compiler_params=_params("parallel"),
    )(o_sb, o_fx, o_m, pb, ow, x, w_out, g, gt)


def _row_spec(ts, w):
    return pl.BlockSpec((ts, w), lambda i: (i, 0))


def _const_spec(shape):
    return pl.BlockSpec(shape, lambda i: (0,) * len(shape))


def _out_bwd(name, dxb, o_sb, o_fx, o_m, pb, ow, w_out, ts):
    s, d = dxb.shape
    g, gt = _head_maps()

    def body(dx_ref, osb_ref, ofx_ref, om_ref, gate_ref, ow_ref, w_ref, g_ref, gt_ref, dy_ref, dgate_ref, dow_ref):
        @pl.when(pl.program_id(0) == 0)
        def _():
            dow_ref[...] = jnp.zeros_like(dow_ref)

        dy2 = _dot_nt(dx_ref[...], w_ref[...])
        yh, rf = _normed_heads(osb_ref, ofx_ref, om_ref, g_ref, gt_ref)
        gate = gate_ref[...]
        sig = jax.nn.sigmoid(gate)
        ow_v = ow_ref[...]
        dgate_ref[...] = (dy2 * (yh * ow_v) * (sig * (1.0 + gate * (1.0 - sig)))).astype(BF16)
        dn = dy2 * (gate * sig)
        dow_ref[...] += jnp.sum(dn * yh, axis=0, keepdims=True)
        dyh = dn * ow_v
        t = _sum_l2(dyh * yh, g_ref[...]) * (1.0 / HEAD_DIM)
        dy_ref[...] = rf * (dyh - yh * _sum_l3(t, gt_ref[...]))

    return pl.pallas_call(
        body, name=name, grid=(s // ts,),
        in_specs=[_row_spec(ts, d), _row_spec(ts, SB_WIDTH), _row_spec(ts, FOX_WIDTH), _row_spec(ts, MEM_WIDTH),
                  _row_spec(ts, MIX_WIDTH), _const_spec((1, MIX_WIDTH)),
                  _const_spec((MIX_WIDTH, d)),
                  _const_spec((MIX_WIDTH, LANES)), _const_spec((LANES, MIX_WIDTH))],
        out_specs=[_row_spec(ts, MIX_WIDTH), _row_spec(ts, MIX_WIDTH), _const_spec((1, MIX_WIDTH))],
        out_shape=[jax.ShapeDtypeStruct((s, MIX_WIDTH), F32), jax.ShapeDtypeStruct((s, PB), BF16),
                   jax.ShapeDtypeStruct((1, MIX_WIDTH), F32)],
        compiler_params=_params("arbitrary"),
    )(dxb, o_sb, o_fx, o_m, pb, ow, w_out, g, gt)


def _adamw(name, w, g, m, v, tr):
    def body(w_ref, g_ref, m_ref, v_ref, d_ref, m2_ref, v2_ref):
        gv = g_ref[...]
        m2 = ADAM_B1 * m_ref[...] + (1.0 - ADAM_B1) * gv
        v2 = ADAM_B2 * v_ref[...] + (1.0 - ADAM_B2) * (gv * gv)
        m_hat = m2 / (1.0 - ADAM_B1 ** ADAM_STEP)
        v_hat = v2 / (1.0 - ADAM_B2 ** ADAM_STEP)
        d_ref[...] = -ADAM_LR * (m_hat / (jnp.sqrt(v_hat) + ADAM_EPS) + ADAM_WD * w_ref[...])
        m2_ref[...] = m2
        v2_ref[...] = v2

    rest = w.shape[1:]
    spec = pl.BlockSpec((tr,) + rest, lambda i: (i,) + (0,) * len(rest))
    shp = jax.ShapeDtypeStruct(w.shape, F32)
    return pl.pallas_call(
        body, name=name, grid=(w.shape[0] // tr,), in_specs=[spec] * 4, out_specs=[spec] * 3, out_shape=[shp] * 3,
        compiler_params=_params("parallel"),
    )(w, g, m, v)


def _adamw_sharded(name, w, m, v, g_own, g_other, cvec, tr):
    depth, rows, cols = w.shape
    nt = rows // 2 // tr

    def body(c_ref, w_ref, m_ref, v_ref, *rest):
        g_refs, (g_ref, d_ref, m2_ref, v2_ref) = rest[:2 * depth], rest[2 * depth:]
        layer, mine = pl.program_id(0), pl.program_id(1) == c_ref[0]
        gv = None
        for lt in range(depth):
            cand = jnp.where(mine, g_refs[lt][...], g_refs[depth + lt][...])
            gv = cand if gv is None else jnp.where(layer == lt, cand, gv)
        m2 = ADAM_B1 * m_ref[...] + (1.0 - ADAM_B1) * gv
        v2 = ADAM_B2 * v_ref[...] + (1.0 - ADAM_B2) * (gv * gv)
        m_hat = m2 / (1.0 - ADAM_B1 ** ADAM_STEP)
        v_hat = v2 / (1.0 - ADAM_B2 ** ADAM_STEP)
        g_ref[...] = gv
        d_ref[...] = -ADAM_LR * (m_hat / (jnp.sqrt(v_hat) + ADAM_EPS) + ADAM_WD * w_ref[...])
        m2_ref[...] = m2
        v2_ref[...] = v2

    def g_map(lt, own):
        def index(l, hf, i, c_ref):
            use = jnp.logical_and(l == lt, (hf == c_ref[0]) == own)
            return jnp.where(use, i, 0), 0
        return index

    full = pl.BlockSpec((None, tr, cols), lambda l, hf, i, c_ref: (l, hf * nt + i, 0))
    g_specs = [pl.BlockSpec((tr, cols), g_map(lt, own)) for own in (True, False) for lt in range(depth)]
    shp = jax.ShapeDtypeStruct((depth, rows, cols), F32)
    return pl.pallas_call(
        body, name=name,
        grid_spec=pltpu.PrefetchScalarGridSpec(
            num_scalar_prefetch=1, grid=(depth, 2, nt), in_specs=[full] * 3 + g_specs, out_specs=[full] * 4),
        out_shape=[shp] * 4,
        compiler_params=_params("arbitrary", "arbitrary", "arbitrary"),
    )(cvec, w, m, v, *g_own, *g_other)


HBM_SPEC = pl.BlockSpec(memory_space=pltpu.HBM)


def _place():
    x, y, c = lax.axis_index("x"), lax.axis_index("y"), lax.axis_index("c")
    chips = [(1 - x, y), (x, 1 - y), (1 - x, 1 - y)]
    return x, y, c, chips


def _remote(src, dst, send_sems, recv_sems, k, to):
    return pltpu.make_async_remote_copy(src_ref=src, dst_ref=dst, send_sem=send_sems.at[k], recv_sem=recv_sems.at[k],
                                        device_id=to, device_id_type=MESH)


def _half_rows(n_rows, cc):
    rh = n_rows // 2
    return pl.ds(pl.multiple_of(cc * rh, 16), rh)


def _dma_sems(n):
    return [pltpu.SemaphoreType.DMA((n,)), pltpu.SemaphoreType.DMA((n,))]


class _Exchange:
    def __init__(self, inputs, out_shapes, n_sems, begin, relay, finish):
        self.inputs, self.out_shapes, self.n_sems = list(inputs), list(out_shapes), n_sems
        self.begin, self.relay, self.finish = begin, relay, finish

    @property
    def n(self):
        return len(self.inputs)

    def split(self, refs):
        return refs[:self.n], refs[self.n:2 * self.n], refs[2 * self.n], refs[2 * self.n + 1]


def _run_exchange(name, ex):
    def body(*refs):
        parts = ex.split(refs)
        for phase in (ex.begin, ex.relay, ex.finish):
            if phase is not None:
                phase(*parts)

    return pl.pallas_call(
        body, name=name, in_specs=[HBM_SPEC] * ex.n, out_specs=[HBM_SPEC] * ex.n, out_shape=ex.out_shapes,
        scratch_shapes=_dma_sems(ex.n_sems),
    )(*ex.inputs)


def _gather_exchange(shards):
    def ici(in_refs, out_refs, send_sems, recv_sems):
        x, y, c, chips = _place()
        return [_remote(in_ref.at[_half_rows(in_ref.shape[0], c)], out_ref.at[2 * x + y, _half_rows(in_ref.shape[0], c)],
                        send_sems, recv_sems, 6 * a + j, (cx, cy, c))
                for a, (in_ref, out_ref) in enumerate(zip(in_refs, out_refs)) for j, (cx, cy) in enumerate(chips)]

    def d2d(out_refs, send_sems, recv_sems, half_of):
        x, y, c, chips = _place()
        cps = []
        for a, out_ref in enumerate(out_refs):
            for j, (cx, cy) in enumerate(chips):
                piece = out_ref.at[2 * cx + cy, _half_rows(out_ref.shape[1], half_of(c))]
                cps.append(_remote(piece, piece, send_sems, recv_sems, 6 * a + 3 + j, (x, y, 1 - c)))
        return cps

    def begin(in_refs, out_refs, send_sems, recv_sems):
        for cp in ici(in_refs, out_refs, send_sems, recv_sems):
            cp.start()

    def relay(in_refs, out_refs, send_sems, recv_sems):
        x, y, c, chips = _place()
        for a, out_ref in enumerate(out_refs):
            for j, (cx, cy) in enumerate(chips):
                landed = out_ref.at[2 * cx + cy, _half_rows(out_ref.shape[1], c)]
                _remote(landed, landed, send_sems, recv_sems, 6 * a + j, (cx, cy, c)).wait_recv()
        for cp in d2d(out_refs, send_sems, recv_sems, lambda c_: c_):
            cp.start()

    def finish(in_refs, out_refs, send_sems, recv_sems):
        for cp in d2d(out_refs, send_sems, recv_sems, lambda c_: 1 - c_):
            cp.wait_recv()
        for cp in ici(in_refs, out_refs, send_sems, recv_sems) + d2d(out_refs, send_sems, recv_sems, lambda c_: c_):
            cp.wait_send()

    shapes = [jax.ShapeDtypeStruct((N_CHIPS,) + s_.shape, s_.dtype) for s_ in shards]
    return _Exchange(shards, shapes, 6 * len(shards), begin, relay, finish)


def _swap_exchange(g4s):
    def copies(in_refs, out_refs, send_sems, recv_sems):
        x, y, c, _ = _place()
        return [_remote(in_ref.at[:, _half_rows(in_ref.shape[1], 1 - c), :], out_ref, send_sems, recv_sems, a, (x, y, 1 - c))
                for a, (in_ref, out_ref) in enumerate(zip(in_refs, out_refs))]

    def begin(*parts):
        for cp in copies(*parts):
            cp.start()

    def finish(*parts):
        for cp in copies(*parts):
            cp.wait()

    shapes = [jax.ShapeDtypeStruct((g.shape[0], g.shape[1] // 2, g.shape[2]), g.dtype) for g in g4s]
    return _Exchange(g4s, shapes, len(g4s), begin, None, finish)


def _add_half(name, g4, r1, cvec, tr):
    n, r, w = g4.shape
    rh = r // 2
    nblk = rh // tr

    def body(c_ref, a_ref, b_ref, o_ref):
        o_ref[...] = (a_ref[...].astype(F32) + b_ref[...].astype(F32)).astype(BF16)

    return pl.pallas_call(
        body, name=name,
        grid_spec=pltpu.PrefetchScalarGridSpec(
            num_scalar_prefetch=1, grid=(n, nblk),
            in_specs=[pl.BlockSpec((None, tr, w), lambda k, i, c_ref: (k, c_ref[0] * nblk + i, 0)),
                      pl.BlockSpec((None, tr, w), lambda k, i, c_ref: (k, i, 0))],
            out_specs=pl.BlockSpec((None, tr, w), lambda k, i, c_ref: (k, i, 0))),
        out_shape=jax.ShapeDtypeStruct((n, rh, w), BF16),
        compiler_params=_params("parallel", "parallel"),
    )(cvec, g4, r1)


def _scatter_exchange(h4s):
    def sends(in_refs, out_refs, send_sems, recv_sems):
        x, y, c, chips = _place()
        return [_remote(in_ref.at[2 * cx + cy], out_ref.at[j], send_sems, recv_sems, 3 * a + j, (cx, cy, c))
                for a, (in_ref, out_ref) in enumerate(zip(in_refs, out_refs)) for j, (cx, cy) in enumerate(chips)]

    def begin(*parts):
        for cp in sends(*parts):
            cp.start()

    def finish(in_refs, out_refs, send_sems, recv_sems):
        x, y, c, chips = _place()
        for a, out_ref in enumerate(out_refs):
            for j, (cx, cy) in enumerate(chips):
                got = out_ref.at[j]
                _remote(got, got, send_sems, recv_sems, 3 * a + j, (cx, cy, c)).wait_recv()
        for cp in sends(in_refs, out_refs, send_sems, recv_sems):
            cp.wait_send()

    shapes = [jax.ShapeDtypeStruct((3,) + h.shape[1:], h.dtype) for h in h4s]
    return _Exchange(h4s, shapes, 3 * len(h4s), begin, None, finish)


def _sum_chips(name, h4, r3, mvec, tr):
    _, rh, w = h4.shape

    def body(m_ref, a_ref, b_ref, c_ref, d_ref, o_ref):
        o_ref[...] = ((a_ref[...].astype(F32) + b_ref[...].astype(F32)) + c_ref[...].astype(F32)) + d_ref[...].astype(F32)

    specs = [pl.BlockSpec((None, tr, w), lambda i, m_ref: (m_ref[0], i, 0))]
    specs += [pl.BlockSpec((None, tr, w), functools.partial(lambda k, i, m_ref: (k, i, 0), k)) for k in range(3)]
    return pl.pallas_call(
        body, name=name,
        grid_spec=pltpu.PrefetchScalarGridSpec(
            num_scalar_prefetch=1, grid=(rh // tr,), in_specs=specs,
            out_specs=pl.BlockSpec((tr, w), lambda i, m_ref: (i, 0))),
        out_shape=jax.ShapeDtypeStruct((rh, w), F32),
        compiler_params=_params("parallel"),
    )(mvec, h4, r3, r3, r3)


def _swap_reduced_exchange(ghs):
    def copies(in_refs, out_refs, send_sems, recv_sems):
        x, y, c, _ = _place()
        return [_remote(in_ref, out_ref, send_sems, recv_sems, a, (x, y, 1 - c))
                for a, (in_ref, out_ref) in enumerate(zip(in_refs, out_refs))]

    def begin(*parts):
        for cp in copies(*parts):
            cp.start()

    def finish(*parts):
        for cp in copies(*parts):
            cp.wait()

    return _Exchange(ghs, [jax.ShapeDtypeStruct(g.shape, g.dtype) for g in ghs], len(ghs), begin, None, finish)


class _SemaphoresFrom:
    def __init__(self, sems, first):
        self.sems, self.first = sems, first

    @property
    def at(self):
        return self

    def __getitem__(self, k):
        return self.sems.at[self.first + k]


def _both(a, b):
    def phase(fa, fb):
        if fa is None and fb is None:
            return None

        def run(in_refs, out_refs, send_sems, recv_sems):
            if fa is not None:
                fa(in_refs[:a.n], out_refs[:a.n], send_sems, recv_sems)
            if fb is not None:
                fb(in_refs[a.n:], out_refs[a.n:], _SemaphoresFrom(send_sems, a.n_sems), _SemaphoresFrom(recv_sems, a.n_sems))

        return run

    return _Exchange(a.inputs + b.inputs, a.out_shapes + b.out_shapes, a.n_sems + b.n_sems,
                     phase(a.begin, b.begin), phase(a.relay, b.relay), phase(a.finish, b.finish))


def _small_update(name, partials, weights, moments1, moments2):
    n = len(partials)
    width = max(p.shape[1] for p in partials)
    starts, at = [], 0
    for p in partials:
        starts.append(at)
        at += p.shape[0]
    rows = -(-at // 8) * 8
    has_w = [w is not None for w in weights]
    n_w = sum(has_w)

    def body(*refs):
        p_refs = refs[:n]
        w_refs, m_refs, v_refs = refs[n:n + n_w], refs[n + n_w:n + 2 * n_w], refs[n + 2 * n_w:n + 3 * n_w]
        outs = refs[n + 3 * n_w:-4]
        g_refs, upd_refs = outs[:n], outs[n:]
        vec, buf, send_sems, recv_sems = refs[-4:]
        x, y, c, _ = _place()
        me = 4 * x + 2 * y + c
        vec[...] = jnp.zeros_like(vec)
        for p_ref, r0 in zip(p_refs, starts):
            vec[r0:r0 + p_ref.shape[0], 0:p_ref.shape[1]] = p_ref[...]
        buf[me] = vec[...]
        flips = [(fx, fy, fc) for fx in (0, 1) for fy in (0, 1) for fc in (0, 1)][1:]
        peers = [(x + fx - 2 * x * fx, y + fy - 2 * y * fy, c + fc - 2 * c * fc) for fx, fy, fc in flips]
        sends = [_remote(vec, buf.at[me], send_sems, recv_sems, k, peer) for k, peer in enumerate(peers)]
        for cp in sends:
            cp.start()
        for k, (px, py, pc) in enumerate(peers):
            got = buf.at[4 * px + 2 * py + pc]
            _remote(got, got, send_sems, recv_sems, k, (px, py, pc)).wait_recv()
        for cp in sends:
            cp.wait_send()
        total = buf[0]
        for dev in range(1, N_DEV):
            total = total + buf[dev]
        k = 0
        for a in range(n):
            r, w = g_refs[a].shape
            g = total[starts[a]:starts[a] + r, 0:w]
            g_refs[a][...] = g
            if has_w[a]:
                m2 = ADAM_B1 * m_refs[k][...] + (1.0 - ADAM_B1) * g
                v2 = ADAM_B2 * v_refs[k][...] + (1.0 - ADAM_B2) * (g * g)
                m_hat = m2 / (1.0 - ADAM_B1 ** ADAM_STEP)
                v_hat = v2 / (1.0 - ADAM_B2 ** ADAM_STEP)
                upd_refs[3 * k][...] = -ADAM_LR * (m_hat / (jnp.sqrt(v_hat) + ADAM_EPS) + ADAM_WD * w_refs[k][...])
                upd_refs[3 * k + 1][...] = m2
                upd_refs[3 * k + 2][...] = v2
                k += 1

    ws = [w for w in weights if w is not None]
    g_shapes = [jax.ShapeDtypeStruct(p.shape if w is None else w.shape, F32) for p, w in zip(partials, weights)]
    u_shapes = [jax.ShapeDtypeStruct(w.shape, F32) for w in ws for _ in range(3)]
    vm = pl.BlockSpec(memory_space=pltpu.VMEM)
    n_args = n + 3 * n_w
    outs = pl.pallas_call(
        body, name=name, in_specs=[vm] * n_args, out_specs=[vm] * (n + 3 * n_w), out_shape=g_shapes + u_shapes,
        scratch_shapes=[pltpu.VMEM((rows, width), F32), pltpu.VMEM((N_DEV, rows, width), F32),
                        pltpu.SemaphoreType.DMA((7,)), pltpu.SemaphoreType.DMA((7,))],
    )(*partials, *ws, *[m for m in moments1 if m is not None], *[v for v in moments2 if v is not None])
    return outs[:n], outs[n:]


GATE_COL = 3 * SB_WIDTH + 3 * FOX_WIDTH + FOX_HEADS + MEM_WIDTH
FL_COL = QKV_WIDTH


GROUP_A_COLS = [(0, QKV_WIDTH), (FL_COL + FOX_HEADS, MEM_WIDTH)]
GROUP_B_COLS = [(GATE_COL, MIX_WIDTH), (FL_COL, FOX_HEADS)]


def _group_from_shards(shard_of, cw, spans, pad):
    parts = []
    for lo, width in spans:
        hi = lo + width
        for j in range(N_CHIPS):
            a, b = max(lo, j * cw), min(hi, (j + 1) * cw)
            if a < b:
                parts.append(shard_of(j)[:, a - j * cw:b - j * cw])
    if pad:
        parts.append(jnp.zeros((parts[0].shape[0], pad), parts[0].dtype))
    return jnp.concatenate(parts, axis=1)


def _shard_from_groups(ga, gb, j, cw):
    lo, hi = j * cw, (j + 1) * cw
    placed = []
    for grp, spans in ((ga, GROUP_A_COLS), (gb, GROUP_B_COLS)):
        at = 0
        for first, width in spans:
            a, b = max(lo, first), min(hi, first + width)
            if a < b:
                placed.append((a, grp[:, at + a - first:at + b - first]))
            at += width
    return jnp.concatenate([p for _, p in sorted(placed, key=lambda t: t[0])], axis=1)


def _tile_of(n, cap, unit):
    if n <= cap:
        return n
    best = None
    for t in range(unit, cap + 1, unit):
        if n % t == 0:
            best = t
    assert best is not None, (n, cap, unit)
    return best


def _column_major_rows(a):
    dp, r, c = a.shape
    return a.transpose(2, 0, 1).reshape(c, dp, r // LANES, LANES).transpose(0, 2, 1, 3).reshape(-1, 8, LANES)


def _from_column_major_rows(b, shape):
    dp, r, c = shape
    return b.reshape(c, r // LANES, dp, LANES).transpose(0, 2, 1, 3).reshape(c, dp, r).transpose(1, 2, 0)


def kernel(x, mem, norm_w, w_in, b_forget, mem_norm_w, w_mem_kv, out_norm_w, w_out, final_norm_w, loss_target, m_norm_w, m_w_in, m_b_forget, m_mem_norm_w, m_w_mem_kv, m_out_norm_w, m_w_out, m_final_norm_w, v_norm_w, v_w_in, v_b_forget, v_mem_norm_w, v_w_mem_kv, v_out_norm_w, v_w_out, v_final_norm_w):
    xs = x[0]
    mems = mem[0]
    target = loss_target[0]
    s, d = xs.shape
    depth = norm_w.shape[0]
    nb = s // TILE
    ts = _tile_of(s, 512, 8)
    big = (w_in, w_mem_kv, w_out)
    core = lax.axis_index("c")
    chip = 2 * lax.axis_index("x") + lax.axis_index("y")
    cvec = core.astype(jnp.int32).reshape(1)
    mvec = chip.astype(jnp.int32).reshape(1)
    cw = w_in.shape[2]

    own_w = [[a[l].astype(BF16) for a in big] for l in range(depth)]

    def lay_out_in(own, got):
        shard_of = lambda j: jnp.where(chip == j, own, got[j])
        return (_group_from_shards(shard_of, cw, GROUP_A_COLS, 0),
                _group_from_shards(shard_of, cw, GROUP_B_COLS, LANES - FOX_HEADS))

    def lay_out_rows(own, got):
        full = jnp.where(lax.broadcasted_iota(jnp.int32, got.shape, 0) == chip, own[None], got)
        return full.reshape(-1, full.shape[2])

    w_in_groups = [lay_out_in(own_w[0][0], _run_exchange("gather_weights0", _gather_exchange(own_w[0][:1]))[0])]
    layer_w = []

    tm = _tile_of(s, 256, 8)
    fl_block = MIX_WIDTH // LANES

    saved = []
    cur = xs
    for l in range(depth):
        wa, wb = w_in_groups[l]
        h = _rms_fwd(f"rms_fwd{l}", cur, norm_w[l][None], ts)
        pa = _mm(f"inproj_a{l}", h, wa, "nn", tm, PA, BF16)
        pb = _mm(f"inproj_b{l}", h, wb, "nn", tm, PB, F32)
        bpad = jnp.pad(b_forget[l], (0, LANES - FOX_HEADS))[None]
        ccol4, crow4 = _gate_fwd(f"gate_fwd{l}", pb, bpad, fl_block)
        more = l + 1 < depth
        o_sb, got = _sb_fwd(f"sb_fwd{l}", pa, 0, carried=_gather_exchange(own_w[l][1:]))
        wkv, wout = lay_out_rows(own_w[l][1], got[0]), lay_out_rows(own_w[l][2], got[1])
        layer_w.append((wa, wb, wkv, wout))
        o_fx, lse_fx, got = _fox_fwd(f"fox_fwd{l}", pa, 3 * SB_WIDTH, ccol4, crow4,
                                     carried=_gather_exchange(own_w[l + 1][:1]) if more else None)
        if more:
            w_in_groups.append(lay_out_in(own_w[l + 1][0], got[0]))
        mn = _rms_fwd(f"mem_rms{l}", mems, mem_norm_w[l][None], mems.shape[0])
        mkv = _mm(f"mem_kv{l}", mn, wkv, "nn", mems.shape[0], 2 * MEM_WIDTH, BF16)
        o_m, lse_m = _mem_fwd(f"mem_fwd{l}", pa, mkv)
        nxt, y2 = _out_fwd(f"out_fwd{l}", o_sb, o_fx, o_m, pb, out_norm_w[l][None], cur, wout, tm)
        saved.append((cur, h, pa, pb, bpad, ccol4, crow4, o_sb, o_fx, lse_fx, mn, mkv, o_m, lse_m, y2))
        cur = nxt

    loss_v, dx, dxb, g_final = _final_loss("final_loss", cur, final_norm_w[None], target, ts)

    g_norm, g_b, g_memnorm, g_outnorm = [None] * depth, [None] * depth, [None] * depth, [None] * depth
    g_wa, g_wb, g_wkv, g_wout = [None] * depth, [None] * depth, [None] * depth, [None] * depth
    g_own = [[None] * depth for _ in big]
    g_other = [[None] * depth for _ in big]

    def swap_of(jobs):
        return _swap_exchange([g for _, _, g, _ in jobs])

    def chip_sums(jobs, got):
        return [(lr, k, _add_half(f"grad_add_half{lr}_{k}", g, r_, cvec, t_), t_) for (lr, k, g, t_), r_ in zip(jobs, got)]

    def sum_at_owner(jobs, from_chips):
        return [_sum_chips(f"grad_sum_chips{lr}_{k}", h_, r_, mvec, t_) for (lr, k, h_, t_), r_ in zip(jobs, from_chips)]

    def keep(jobs, halves, others):
        for (lr, k, _, _), mine, other in zip(jobs, halves, others):
            g_own[k][lr], g_other[k][lr] = mine, other

    def job(lr, k, g4):
        return lr, k, g4, _tile_of(g4.shape[1] // 2, 256, 16)

    pending = []
    for l in reversed(range(depth)):
        xin, h, pa, pb, bpad, ccol4, crow4, o_sb, o_fx, lse_fx, mn, mkv, o_m, lse_m, y2 = saved[l]
        wa, wb, wkv, wout = layer_w[l]
        dy, dgate, g_outnorm[l] = _out_bwd(f"out_bwd{l}", dxb, o_sb, o_fx, o_m, pb, out_norm_w[l][None], wout, tm)
        g_wout[l] = _mm(f"dw_out{l}", y2, dxb, "tn", _tile_of(MIX_WIDTH, 640, LANES), d, F32)
        dq_m, dmkv = _mem_bwd(f"mem_bwd{l}", pa, mkv, o_m, lse_m, dy, SB_WIDTH + FOX_WIDTH)
        g_wkv[l] = _mm(f"dw_kv{l}", mn, dmkv, "tn", d, 2 * MEM_WIDTH, F32)
        dmn = _mm(f"dmem{l}", dmkv, wkv, "nt", mems.shape[0], d, F32)
        g_memnorm[l] = _rms_wgrad(f"mem_norm_grad{l}", mems, dmn)
        small = [job(l, 1, g_wkv[l].reshape(N_CHIPS, -1, g_wkv[l].shape[1])), job(l, 2, g_wout[l].reshape(N_CHIPS, -1, d))]
        (dq_fx, dk_fx, dv_fx, cs4), got = _fox_bwd(f"fox_bwd{l}", pa, 3 * SB_WIDTH, ccol4, crow4, o_fx, lse_fx, dy,
                                                    SB_WIDTH, carried=swap_of(small))
        pending += chip_sums(small, got)
        (dq_sb, dk_sb, dv_sb), from_chips = _sb_bwd(f"sb_bwd{l}", pa, 0, dy, 0,
                                                   carried=_scatter_exchange([j[2] for j in pending]))
        reduced_jobs, reduced = pending, sum_at_owner(pending, from_chips)
        swap_back = _swap_reduced_exchange(reduced)
        dpb, g_b[l] = _gate_bwd(f"gate_bwd{l}", pb, bpad, cs4, fl_block, dgate)
        dpa = jnp.concatenate([dq_sb, dk_sb, dv_sb, dq_fx, dk_fx, dv_fx, dq_m], axis=1)
        tw = _tile_of(d, 512, LANES)
        g_wa[l] = _mm(f"dw_in_a{l}", h, dpa, "tn", tw, PA, BF16)
        g_wb[l] = _mm(f"dw_in_b{l}", h, dpb, "tn", tw, PB, BF16)
        g4_in = jnp.stack([_shard_from_groups(g_wa[l], g_wb[l], j, cw) for j in range(N_CHIPS)])
        w_in_job = [job(l, 0, g4_in)]
        if l > 0:
            dx, dxb, g_norm[l], got = _inproj_bwd(f"inproj_bwd{l}", dpa, dpb, wa, wb, xin, norm_w[l][None], dx, tm,
                                                  carried=_both(swap_of(w_in_job), swap_back))
            pending = chip_sums(w_in_job, got[:1])
            keep(reduced_jobs, reduced, got[1:])
        else:
            pending = chip_sums(w_in_job, _run_exchange("grad_swap_halves_last", swap_of(w_in_job)))
            dx, dxb, g_norm[l], got = _inproj_bwd(f"inproj_bwd{l}", dpa, dpb, wa, wb, xin, norm_w[l][None], dx, tm,
                                                  carried=_both(_scatter_exchange([j[2] for j in pending]), swap_back))
            keep(reduced_jobs, reduced, got[1:])
            last = sum_at_owner(pending, got[:1])
            keep(pending, last, _run_exchange("grad_swap_reduced_last", _swap_reduced_exchange(last)))

    small_w = [norm_w, b_forget, mem_norm_w, out_norm_w, final_norm_w]
    small_m = [m_norm_w, m_b_forget, m_mem_norm_w, m_out_norm_w, m_final_norm_w]
    small_v = [v_norm_w, v_b_forget, v_mem_norm_w, v_out_norm_w, v_final_norm_w]
    rows2 = lambda a: a.reshape(-1, a.shape[-1])
    partials = [jnp.concatenate(g_norm, axis=0), jnp.concatenate(g_b, axis=0), jnp.concatenate(g_memnorm, axis=0),
                jnp.concatenate(g_outnorm, axis=0), g_final, loss_v]
    sums, updates = _small_update("small_update", partials, [rows2(a) for a in small_w] + [None],
                                  [rows2(a) for a in small_m] + [None], [rows2(a) for a in small_v] + [None])
    small_grads = [g.reshape(a.shape) for g, a in zip(sums, small_w)]
    loss = sums[-1][0, 0]
    small_delta, small_m2, small_v2 = ([updates[3 * k + t].reshape(a.shape) for k, a in enumerate(small_w)]
                                       for t in range(3))
    big_grads, big_delta, big_m2, big_v2 = [], [], [], []
    for k, (nm, w_, m_, v_) in enumerate(zip(("w_in", "w_mem_kv", "w_out"), big, (m_w_in, m_w_mem_kv, m_w_out),
                                             (v_w_in, v_w_mem_kv, v_w_out))):
        if w_.shape[2] % LANES:
            g_full = jnp.stack([jnp.concatenate([jnp.where(core == 0, go, gt), jnp.where(core == 0, gt, go)], axis=0)
                                for go, gt in zip(g_own[k], g_other[k])])
            w_p, g_p, m_p, v_p = (_column_major_rows(a) for a in (w_, g_full, m_, v_))
            outs = _adamw(f"adamw_{nm}", w_p, g_p, m_p, v_p, _tile_of(w_p.shape[0], 600, 1))
            outs = [_from_column_major_rows(o, w_.shape) for o in (g_p, *outs)]
        else:
            outs = _adamw_sharded(f"adamw_{nm}", w_, m_, v_, g_own[k], g_other[k], cvec,
                                  _tile_of(w_.shape[1] // 2, 256, 8))
        for lst, o in zip((big_grads, big_delta, big_m2, big_v2), outs):
            lst.append(o)

    def order(sm, bg):
        return [sm[0], bg[0], sm[1], sm[2], bg[1], sm[3], bg[2], sm[4]]

    return (loss, dx[None], *order(small_grads, big_grads), *order(small_delta, big_delta),
            *order(small_m2, big_m2), *order(small_v2, big_v2))
```

```python
import functools

import jax
import jax.numpy as jnp
from jax import lax
from jax.experimental import pallas as pl
from jax.experimental.pallas import tpu as pltpu

F32 = jnp.float32
BF16 = jnp.bfloat16

HEAD_DIM = 64
SB_WIDTH = 512
FOX_WIDTH = 512
FOX_HEADS = 8
MEM_WIDTH = 256
MEM_HEADS = MEM_WIDTH // HEAD_DIM
MIX_WIDTH = SB_WIDTH + FOX_WIDTH + MEM_WIDTH
TOTAL_HEADS = MIX_WIDTH // HEAD_DIM
IN_WIDTH = 3 * SB_WIDTH + 3 * FOX_WIDTH + FOX_HEADS + MEM_WIDTH + MIX_WIDTH
LANES = 128
QKV_WIDTH = 3 * SB_WIDTH + 3 * FOX_WIDTH
PA = QKV_WIDTH + MEM_WIDTH
PB = LANES + MIX_WIDTH
EPS = 1e-6
SCALE = HEAD_DIM ** -0.5
TILE = 256
SB_GROUP = 4
SB_LANES = SB_GROUP * HEAD_DIM
SB_FWD_GROUP = 8
FOX_GROUP = 4
FOX_LANES = FOX_GROUP * HEAD_DIM
FOX_FWD_GROUP = 4
NEG_INF = float("-inf")
MASKED = -1e30

ADAM_LR = 0.001
ADAM_B1 = 0.9
ADAM_B2 = 0.999
ADAM_EPS = 1e-08
ADAM_WD = 0.01
ADAM_STEP = 10

N_CHIPS = 4
N_DEV = 8
VMEM_LIMIT = 48 * 1024 * 1024
MESH = pl.DeviceIdType.MESH


def _params(*sem):
    return pltpu.CompilerParams(dimension_semantics=tuple(sem), vmem_limit_bytes=VMEM_LIMIT)


def _dot(a, b):
    return jnp.dot(a, b, preferred_element_type=F32)


def _dot_nt(a, b):
    return lax.dot_general(a, b, (((1,), (1,)), ((), ())), preferred_element_type=F32)


def _dot_tn(a, b):
    return lax.dot_general(a, b, (((0,), (0,)), ((), ())), preferred_element_type=F32)


def _split2(x):
    hi = x.astype(BF16)
    lo = (x - hi.astype(F32)).astype(BF16)
    return hi, lo


def _split3(x):
    hi = x.astype(BF16)
    r = x - hi.astype(F32)
    mid = r.astype(BF16)
    lo = (r - mid.astype(F32)).astype(BF16)
    return hi, mid, lo


def _sum_l2(x, u):
    hi, lo = _split2(x)
    return _dot(hi, u) + _dot(lo, u)


def _sum_l3(x, u):
    hi, mid, lo = _split3(x)
    return _dot(hi, u) + _dot(mid, u) + _dot(lo, u)


def _sum_r3(u, x):
    hi, mid, lo = _split3(x)
    return _dot(u, hi) + _dot(u, mid) + _dot(u, lo)


def _tri(n, pred):
    r = lax.broadcasted_iota(jnp.int32, (n, n), 0)
    c = lax.broadcasted_iota(jnp.int32, (n, n), 1)
    return jnp.where(pred(r, c), 1.0, 0.0).astype(BF16)


def _rows(ref, j, n=TILE):
    return pl.ds(pl.multiple_of(j * n, n), n)


def _mm(name, a, b, mode, tm, tn, out_dtype, res=None, a_lead=(), b_lead=()):
    a2, b2 = a.shape[len(a_lead):], b.shape[len(b_lead):]
    if mode == "tn":
        k, m = a2
    else:
        m, k = a2
    n = b2[0] if mode == "nt" else b2[1]
    assert m % tm == 0 and n % tn == 0, (name, m, tm, n, tn)
    na, nb = (None,) * len(a_lead), (None,) * len(b_lead)
    if mode == "tn":
        a_spec = pl.BlockSpec(na + (k, tm), lambda j, i: a_lead + (0, i))
    else:
        a_spec = pl.BlockSpec(na + (tm, k), lambda j, i: a_lead + (i, 0))
    if mode == "nt":
        b_spec = pl.BlockSpec(nb + (tn, k), lambda j, i: b_lead + (j, 0))
    else:
        b_spec = pl.BlockSpec(nb + (k, tn), lambda j, i: b_lead + (0, j))
    o_spec = pl.BlockSpec((tm, tn), lambda j, i: (i, j))
    dot = {"nn": _dot, "nt": _dot_nt, "tn": _dot_tn}[mode]

    def body(a_ref, b_ref, *rest):
        o_ref = rest[-1]
        acc = dot(a_ref[...].astype(BF16), b_ref[...].astype(BF16))
        if res is not None:
            acc = acc + rest[0][...]
        o_ref[...] = acc.astype(o_ref.dtype)

    args, specs = [a, b], [a_spec, b_spec]
    if res is not None:
        args.append(res)
        specs.append(o_spec)
    return pl.pallas_call(
        body, name=name, grid=(n // tn, m // tm), in_specs=specs, out_specs=o_spec,
        out_shape=jax.ShapeDtypeStruct((m, n), out_dtype),
        compiler_params=_params("parallel", "parallel"),
    )(*args)


def _rms_fwd(name, x, g, ts):
    s, d = x.shape

    def body(x_ref, g_ref, o_ref):
        xf = x_ref[...]
        r = lax.rsqrt(jnp.mean(xf * xf, axis=1, keepdims=True) + EPS)
        o_ref[...] = (xf * r * g_ref[...]).astype(BF16)

    return pl.pallas_call(
        body, name=name, grid=(s // ts,),
        in_specs=[pl.BlockSpec((ts, d), lambda i: (i, 0)), pl.BlockSpec((1, d), lambda i: (0, 0))],
        out_specs=pl.BlockSpec((ts, d), lambda i: (i, 0)),
        out_shape=jax.ShapeDtypeStruct((s, d), BF16),
        compiler_params=_params("parallel"),
    )(x, g)


def _inproj_bwd(name, dpa, dpb, wa, wb, x, g, dres, ts, carried=None):
    s, d = x.shape

    def body(dpa_ref, dpb_ref, wa_ref, wb_ref, x_ref, g_ref, dres_ref, dx_ref, dxb_ref, dg_ref):
        @pl.when(pl.program_id(1) == 0)
        def _():
            dg_ref[...] = jnp.zeros_like(dg_ref)

        dhf = _dot_nt(dpa_ref[...], wa_ref[...]) + _dot_nt(dpb_ref[...], wb_ref[...])
        xf = x_ref[...]
        r = lax.rsqrt(jnp.mean(xf * xf, axis=1, keepdims=True) + EPS)
        xh = xf * r
        dg_ref[...] += jnp.sum(dhf * xh, axis=0, keepdims=True)
        dxh = dhf * g_ref[...]
        m = jnp.mean(dxh * xh, axis=1, keepdims=True)
        dx = r * (dxh - xh * m) + dres_ref[...]
        dx_ref[...] = dx
        dxb_ref[...] = dx.astype(BF16)

    row = lambda w: pl.BlockSpec((ts, w), lambda p, i: (i, 0))
    whole = lambda a: pl.BlockSpec(a.shape, lambda p, i: (0, 0))
    outs = _pair_grid_call(
        name, body, s // ts,
        in_specs=[row(dpa.shape[1]), row(dpb.shape[1]), whole(wa), whole(wb), row(d), whole(g), row(d)],
        out_specs=[row(d), row(d), pl.BlockSpec((1, d), lambda p, i: (0, 0))],
        out_shape=[jax.ShapeDtypeStruct((s, d), F32), jax.ShapeDtypeStruct((s, d), BF16),
                   jax.ShapeDtypeStruct((1, d), F32)],
        scratch=[], args=(dpa, dpb, wa, wb, x, g, dres), carried=carried, groups=1)
    return outs[0], outs[1], outs[2], outs[3:]


def _rms_wgrad(name, x, dh):
    m_, d = x.shape

    def body(x_ref, dh_ref, dg_ref):
        xf = x_ref[...]
        r = lax.rsqrt(jnp.mean(xf * xf, axis=1, keepdims=True) + EPS)
        dg_ref[...] = jnp.sum(dh_ref[...] * xf * r, axis=0, keepdims=True)

    return pl.pallas_call(
        body, name=name, out_shape=jax.ShapeDtypeStruct((1, d), F32),
    )(x, dh)


def _final_loss(name, x, g, target, ts):
    s, d = x.shape

    def body(x_ref, g_ref, t_ref, loss_ref, dx_ref, dxb_ref, dg_ref):
        @pl.when(pl.program_id(0) == 0)
        def _():
            dg_ref[...] = jnp.zeros_like(dg_ref)
            loss_ref[...] = jnp.zeros_like(loss_ref)

        xf = x_ref[...]
        gw = g_ref[...]
        r = lax.rsqrt(jnp.mean(xf * xf, axis=1, keepdims=True) + EPS)
        xh = xf * r
        e = xh * gw - t_ref[...]
        part = 0.5 * jnp.sum(jnp.mean(e * e, axis=1, keepdims=True), axis=0, keepdims=True)
        loss_ref[...] += jnp.broadcast_to(part, loss_ref.shape)
        dy = e * (1.0 / d)
        dg_ref[...] += jnp.sum(dy * xh, axis=0, keepdims=True)
        dxh = dy * gw
        m = jnp.mean(dxh * xh, axis=1, keepdims=True)
        dx = r * (dxh - xh * m)
        dx_ref[...] = dx
        dxb_ref[...] = dx.astype(BF16)

    row = pl.BlockSpec((ts, d), lambda i: (i, 0))
    vec = pl.BlockSpec((1, d), lambda i: (0, 0))
    lvec = pl.BlockSpec((1, LANES), lambda i: (0, 0))
    return pl.pallas_call(
        body, name=name, grid=(s // ts,), in_specs=[row, vec, row], out_specs=[lvec, row, row, vec],
        out_shape=[jax.ShapeDtypeStruct((1, LANES), F32), jax.ShapeDtypeStruct((s, d), F32),
                   jax.ShapeDtypeStruct((s, d), BF16), jax.ShapeDtypeStruct((1, d), F32)],
        compiler_params=_params("arbitrary"),
    )(x, g, target)


def _gate_fwd(name, pb, bpad, fl_block):
    s = pb.shape[0]
    nb = s // TILE

    def body(fl_ref, b_ref, ccol_ref, crow_ref, carry):
        @pl.when(pl.program_id(0) == 0)
        def _():
            carry[...] = jnp.zeros_like(carry)

        u = fl_ref[...] + b_ref[...]
        lf = jnp.minimum(u, 0.0) - jnp.log1p(jnp.exp(-jnp.abs(u)))
        lower = _tri(TILE, lambda r, c: c <= r)
        c = _sum_r3(lower, lf) + carry[0:1, :]
        ccol_ref[...] = jnp.concatenate(
            [jnp.broadcast_to(c[:, hh:hh + 1], (TILE, HEAD_DIM)) for hh in range(FOX_HEADS)], axis=1)
        crow_ref[0] = c.T[0:8, :]
        carry[...] = jnp.broadcast_to(c[TILE - 1:TILE, :], carry.shape)

    return pl.pallas_call(
        body, name=name, grid=(nb,),
        in_specs=[pl.BlockSpec((TILE, LANES), lambda i: (i, fl_block)), pl.BlockSpec((1, LANES), lambda i: (0, 0))],
        out_specs=[pl.BlockSpec((TILE, FOX_WIDTH), lambda i: (i, 0)), pl.BlockSpec((1, 8, TILE), lambda i: (i, 0, 0))],
        out_shape=[jax.ShapeDtypeStruct((s, FOX_WIDTH), F32), jax.ShapeDtypeStruct((nb, 8, TILE), F32)],
        scratch_shapes=[pltpu.VMEM((8, LANES), F32)],
        compiler_params=_params("arbitrary"),
    )(pb, bpad)


def _gate_bwd(name, pb, bpad, colsum, fl_block, dpb):
    s = pb.shape[0]
    nb = s // TILE

    def body(fl_ref, b_ref, cs_ref, dpb_ref, dl_ref, db_ref, carry):
        @pl.when(pl.program_id(0) == 0)
        def _():
            carry[...] = jnp.zeros_like(carry)
            db_ref[...] = jnp.zeros_like(db_ref)

        upper = _tri(TILE, lambda r, c: r >= c)
        rsum = _sum_l3(cs_ref[0], upper) + carry[:, 0:1]
        carry[...] = jnp.broadcast_to(rsum[:, 0:1], carry.shape)
        full = jnp.concatenate([rsum, jnp.zeros((LANES - 8, TILE), F32)], axis=0)
        dlf = -full.T
        u = fl_ref[...] + b_ref[...]
        dlogit = dlf * (1.0 - jax.nn.sigmoid(u))
        dl_ref[...] = dlogit.astype(BF16)
        db_ref[...] += jnp.sum(dlogit, axis=0, keepdims=True)

    logits_block = pl.BlockSpec((TILE, LANES), lambda i: (nb - 1 - i, fl_block))
    return pl.pallas_call(
        body, name=name, grid=(nb,),
        in_specs=[logits_block, pl.BlockSpec((1, LANES), lambda i: (0, 0)),
                  pl.BlockSpec((1, 8, TILE), lambda i: (nb - 1 - i, 0, 0)), pl.BlockSpec(memory_space=pl.ANY)],
        out_specs=[logits_block, pl.BlockSpec((1, LANES), lambda i: (0, 0))],
        out_shape=[jax.ShapeDtypeStruct(dpb.shape, BF16), jax.ShapeDtypeStruct((1, LANES), F32)],
        scratch_shapes=[pltpu.VMEM((8, LANES), F32)], input_output_aliases={3: 0},
        compiler_params=_params("arbitrary"),
    )(pb, bpad, colsum, dpb)


def _head_slices(hh):
    return slice(HEAD_DIM * hh, HEAD_DIM * (hh + 1))


def _scaled_q(q_ref, sl, scale=SCALE):
    return (q_ref[:, sl].astype(F32) * scale).astype(BF16)


def _neg_abs(x):
    sign = jnp.uint32(0x80000000)
    return lax.bitcast_convert_type(lax.bitcast_convert_type(x, jnp.uint32) | sign, F32)


def _pair_grid_call(name, body, nb, in_specs, out_specs, out_shape, scratch, args, carried=None, groups=4):
    if carried is None:
        return pl.pallas_call(
            body, name=name, grid=(groups, nb), in_specs=in_specs, out_specs=out_specs, out_shape=out_shape,
            scratch_shapes=scratch, compiler_params=_params("arbitrary", "arbitrary"),
        )(*args)
    n_in, n_out, n_ex = len(in_specs), len(out_specs), carried.n

    def body_with_copies(*refs):
        own_in, ex_in = refs[:n_in], refs[n_in:n_in + n_ex]
        own_out = refs[n_in + n_ex:n_in + n_ex + n_out]
        ex_out = refs[n_in + n_ex + n_out:n_in + 2 * n_ex + n_out]
        own_scratch, sems = refs[n_in + 2 * n_ex + n_out:-2], refs[-2:]
        parts = (ex_in, ex_out, sems[0], sems[1])
        p, i = pl.program_id(0), pl.program_id(1)
        pl.when(jnp.logical_and(p == 0, i == 0))(lambda: carried.begin(*parts))
        if carried.relay is not None:
            pl.when(jnp.logical_and(p == groups - 1, i == max(nb - 2, 0)))(lambda: carried.relay(*parts))
        body(*own_in, *own_out, *own_scratch)
        pl.when(jnp.logical_and(p == groups - 1, i == nb - 1))(lambda: carried.finish(*parts))

    return pl.pallas_call(
        body_with_copies, name=name, grid=(groups, nb), in_specs=list(in_specs) + [HBM_SPEC] * n_ex,
        out_specs=list(out_specs) + [HBM_SPEC] * n_ex, out_shape=list(out_shape) + carried.out_shapes,
        scratch_shapes=list(scratch) + _dma_sems(carried.n_sems),
        compiler_params=_params("arbitrary", "arbitrary"),
    )(*args, *carried.inputs)


def _sb_fwd(name, pa, col0, carried=None):
    s = pa.shape[0]
    nb = s // TILE
    heads, lanes = SB_FWD_GROUP, SB_FWD_GROUP * HEAD_DIM
    cb = col0 // lanes
    kb = SB_WIDTH // lanes

    def body(q_ref, k_ref, v_ref, o_ref, lsig_s, lf_s):
        i = pl.program_id(1)
        r = lax.broadcasted_iota(jnp.int32, (TILE, TILE), 0)
        c = lax.broadcasted_iota(jnp.int32, (TILE, TILE), 1)
        strict = c < r
        u_after = _tri(TILE, lambda rr, cc: rr > cc)
        qs = [_scaled_q(q_ref, _head_slices(hh), -SCALE) for hh in range(heads)]

        def neg_z(j):
            kblk = k_ref[_rows(k_ref, j), :]
            return [_dot_nt(qs[hh], kblk[:, _head_slices(hh)]) for hh in range(heads)]

        def scores(nzs, slot, diag):
            for hh, nz in enumerate(nzs):
                lf = jnp.minimum(nz, 0.0) - jnp.log(1.0 + jnp.exp(_neg_abs(nz)))
                lsig = lf - nz
                if diag:
                    lf = jnp.where(strict, lf, 0.0)
                    lsig = jnp.where(strict, lsig, MASKED)
                lsig_s[slot, hh] = lsig
                lf_s[slot, hh] = lf.astype(BF16)

        def weigh(j, slot, state):
            vblk = v_ref[_rows(v_ref, j), :]
            new = []
            for hh in range(heads):
                carry, acc = state[hh]
                lfb = lf_s[slot, hh]
                sx = _dot(lfb, u_after)
                a = jnp.exp(lsig_s[slot, hh] + sx + carry)
                new.append((carry + sx[:, 0:1] + lfb[:, 0:1].astype(F32),
                            acc + _dot(a.astype(BF16), vblk[:, _head_slices(hh)])))
            return tuple(new)

        def step(t, state):
            state = weigh(i - t + 1, (t - 1) % 2, state)
            scores(neg_z(i - t), t % 2, False)
            return state

        zero = (jnp.zeros((TILE, 1), F32), jnp.zeros((TILE, HEAD_DIM), F32))
        scores(neg_z(i), 0, True)
        state = lax.fori_loop(1, i + 1, step, (zero,) * heads)
        state = weigh(0, i % 2, state)
        o_ref[...] = jnp.concatenate([st[1] for st in state], axis=1)

    outs = _pair_grid_call(
        name, body, nb,
        in_specs=[pl.BlockSpec((TILE, lanes), lambda p, i: (i, cb + p)),
                  pl.BlockSpec((s, lanes), lambda p, i: (0, cb + kb + p)),
                  pl.BlockSpec((s, lanes), lambda p, i: (0, cb + 2 * kb + p))],
        out_specs=[pl.BlockSpec((TILE, lanes), lambda p, i: (i, p))],
        out_shape=[jax.ShapeDtypeStruct((s, SB_WIDTH), F32)],
        scratch=[pltpu.VMEM((2, heads, TILE, TILE), F32), pltpu.VMEM((2, heads, TILE, TILE), BF16)],
        args=(pa, pa, pa), carried=carried, groups=kb)
    return outs[0], outs[1:]


def _sb_bwd(name, pa, col0, dout, dcol0, carried=None):
    s = pa.shape[0]
    nb = s // TILE
    cb = col0 // SB_LANES
    kb = SB_WIDTH // SB_LANES
    db = dcol0 // SB_LANES

    def body(q_ref, k_ref, v_ref, do_ref, dq_ref, dk_ref, dv_ref, dk_acc, dv_acc, dpan, span, gsum, lsig_s, lf_s):
        i = pl.program_id(1)

        @pl.when(i == 0)
        def _():
            dk_acc[...] = jnp.zeros_like(dk_acc)
            dv_acc[...] = jnp.zeros_like(dv_acc)

        r = lax.broadcasted_iota(jnp.int32, (TILE, TILE), 0)
        c = lax.broadcasted_iota(jnp.int32, (TILE, TILE), 1)
        strict = c < r
        u_after = _tri(TILE, lambda rr, cc: rr > cc)
        u_before = _tri(TILE, lambda rr, cc: rr < cc)
        qs = [_scaled_q(q_ref, _head_slices(hh), -SCALE) for hh in range(SB_GROUP)]
        dos = [do_ref[:, _head_slices(hh)].astype(BF16) for hh in range(SB_GROUP)]
        dots = [do_ref[:, _head_slices(hh)].T.astype(BF16) for hh in range(SB_GROUP)]
        qts = [q.astype(F32).T.astype(BF16) for q in qs]

        def scores(j, slot, diag):
            kblk = k_ref[_rows(k_ref, j), :]
            for hh in range(SB_GROUP):
                nz = _dot_nt(qs[hh], kblk[:, _head_slices(hh)])
                lf = jnp.minimum(nz, 0.0) - jnp.log(1.0 + jnp.exp(_neg_abs(nz)))
                lsig = lf - nz
                if diag:
                    lf = jnp.where(strict, lf, 0.0)
                    lsig = jnp.where(strict, lsig, MASKED)
                lsig_s[slot, hh] = lsig
                lf_s[slot, hh] = lf.astype(BF16)

        def grads(j, slot, carries):
            vblk = v_ref[_rows(v_ref, j), :]
            new = []
            for hh in range(SB_GROUP):
                lfb = lf_s[slot, hh]
                lsig = lsig_s[slot, hh]
                sx = _dot(lfb, u_after)
                a = jnp.exp(lsig + sx + carries[hh])
                g = a * _dot_nt(dos[hh], vblk[:, _head_slices(hh)])
                sig = jnp.exp(lsig)
                inside = _dot(g.astype(BF16), u_before)
                dpan[hh, j] = sig * (inside + g) - g
                span[hh, j] = sig
                gsum[hh, j] = inside[:, TILE - 1:TILE] + g[:, TILE - 1:TILE]
                dv_acc[hh, j] += _dot(dots[hh], a.astype(BF16))
                new.append(carries[hh] + sx[:, 0:1] + lfb[:, 0:1].astype(F32))
            return tuple(new)

        def step1(t, carries):
            carries = grads(i - t + 1, (t - 1) % 2, carries)
            scores(i - t, t % 2, False)
            return carries

        zero1 = jnp.zeros((TILE, 1), F32)
        scores(i, 0, True)
        carries = lax.fori_loop(1, i + 1, step1, (zero1,) * SB_GROUP)
        grads(0, i % 2, carries)

        def pass2(j, state):
            kblk = k_ref[_rows(k_ref, j), :]
            new = []
            for hh in range(SB_GROUP):
                before, ndq = state[hh]
                ndzb = (dpan[hh, j] + span[hh, j] * before).astype(BF16)
                dk_acc[hh, j] += _dot(qts[hh], ndzb)
                new.append((before + gsum[hh, j], ndq + _dot(ndzb, kblk[:, _head_slices(hh)])))
            return tuple(new)

        zero2 = (zero1, jnp.zeros((TILE, HEAD_DIM), F32))
        state = lax.fori_loop(0, i + 1, pass2, (zero2,) * SB_GROUP)
        dq_ref[...] = jnp.concatenate([st[1] * -SCALE for st in state], axis=1).astype(BF16)

        @pl.when(i == nb - 1)
        def _():
            for acc, ref in ((dk_acc, dk_ref), (dv_acc, dv_ref)):
                for j in range(nb):
                    ref[j * TILE:(j + 1) * TILE, :] = jnp.concatenate(
                        [acc[hh, j].T for hh in range(SB_GROUP)], axis=1).astype(BF16)

    qspec = pl.BlockSpec((TILE, SB_LANES), lambda p, i: (i, p))
    kvspec = pl.BlockSpec((s, SB_LANES), lambda p, i: (0, p))
    out = jax.ShapeDtypeStruct((s, SB_WIDTH), BF16)
    outs = _pair_grid_call(
        name, body, nb,
        in_specs=[pl.BlockSpec((TILE, SB_LANES), lambda p, i: (i, cb + p)),
                  pl.BlockSpec((s, SB_LANES), lambda p, i: (0, cb + kb + p)),
                  pl.BlockSpec((s, SB_LANES), lambda p, i: (0, cb + 2 * kb + p)),
                  pl.BlockSpec((TILE, SB_LANES), lambda p, i: (i, db + p))],
        out_specs=[qspec, kvspec, kvspec], out_shape=[out, out, out],
        scratch=[pltpu.VMEM((SB_GROUP, nb, HEAD_DIM, TILE), F32), pltpu.VMEM((SB_GROUP, nb, HEAD_DIM, TILE), F32),
                 pltpu.VMEM((SB_GROUP, nb, TILE, TILE), F32), pltpu.VMEM((SB_GROUP, nb, TILE, TILE), F32),
                 pltpu.VMEM((SB_GROUP, nb, TILE, 1), F32),
                 pltpu.VMEM((2, SB_GROUP, TILE, TILE), F32), pltpu.VMEM((2, SB_GROUP, TILE, TILE), BF16)],
        args=(pa, pa, pa, dout), carried=carried, groups=kb)
    return outs[:3], outs[3:]


def _fox_scores(q, kj, cq, crj, causal, diag):
    sc = _dot_nt(q, kj) + (cq - crj)
    if diag:
        sc = jnp.where(causal, sc, NEG_INF)
    return sc


def _fox_fwd(name, pa, col0, ccol4, crow4, carried=None):
    s = pa.shape[0]
    nb = s // TILE
    heads, lanes = FOX_FWD_GROUP, FOX_FWD_GROUP * HEAD_DIM
    cb = col0 // lanes
    kb = FOX_WIDTH // lanes

    def body(q_ref, k_ref, v_ref, cc_ref, cr_ref, o_ref, lse_ref, sc_s):
        i = pl.program_id(1)
        head0 = pl.program_id(0) * heads
        r = lax.broadcasted_iota(jnp.int32, (TILE, TILE), 0)
        c = lax.broadcasted_iota(jnp.int32, (TILE, TILE), 1)
        causal = c <= r
        qs = [_scaled_q(q_ref, _head_slices(hh)) for hh in range(heads)]
        cqs = [cc_ref[:, HEAD_DIM * hh:HEAD_DIM * hh + 1] for hh in range(heads)]

        def logits(j, slot, diag):
            kblk = k_ref[_rows(k_ref, j), :]
            tops = []
            for hh in range(heads):
                sc = _fox_scores(qs[hh], kblk[:, _head_slices(hh)], cqs[hh], cr_ref[j, pl.ds(head0 + hh, 1), :], causal, diag)
                sc_s[slot, hh] = sc
                tops.append(jnp.max(sc, axis=1, keepdims=True))
            return tuple(tops)

        def update(j, slot, tops, state):
            vblk = v_ref[_rows(v_ref, j), :]
            new = []
            for hh in range(heads):
                m, l, acc = state[hh]
                m2 = jnp.maximum(m, tops[hh])
                alpha = jnp.exp(m - m2)
                p = jnp.exp(sc_s[slot, hh] - m2)
                new.append((m2, l * alpha + jnp.sum(p, axis=1, keepdims=True),
                            acc * alpha + _dot(p.astype(BF16), vblk[:, _head_slices(hh)])))
            return tuple(new)

        def step(t, both):
            tops, state = both
            state = update(i - t + 1, (t - 1) % 2, tops, state)
            return logits(i - t, t % 2, False), state

        zero = (jnp.full((TILE, 1), NEG_INF, F32), jnp.zeros((TILE, 1), F32), jnp.zeros((TILE, HEAD_DIM), F32))
        tops, state = lax.fori_loop(1, i + 1, step, (logits(i, 0, True), (zero,) * heads))
        state = update(0, i % 2, tops, state)
        o_ref[...] = jnp.concatenate([st[2] / st[1] for st in state], axis=1)
        lse_ref[...] = jnp.concatenate(
            [jnp.broadcast_to(st[0] + jnp.log(st[1]), (TILE, HEAD_DIM)) for st in state], axis=1)

    outs = _pair_grid_call(
        name, body, nb,
        in_specs=[pl.BlockSpec((TILE, lanes), lambda p, i: (i, cb + p)),
                  pl.BlockSpec((s, lanes), lambda p, i: (0, cb + kb + p)),
                  pl.BlockSpec((s, lanes), lambda p, i: (0, cb + 2 * kb + p)),
                  pl.BlockSpec((TILE, lanes), lambda p, i: (i, p)),
                  pl.BlockSpec((nb, 8, TILE), lambda p, i: (0, 0, 0))],
        out_specs=[pl.BlockSpec((TILE, lanes), lambda p, i: (i, p)), pl.BlockSpec((TILE, lanes), lambda p, i: (i, p))],
        out_shape=[jax.ShapeDtypeStruct((s, FOX_WIDTH), F32), jax.ShapeDtypeStruct((s, FOX_WIDTH), F32)],
        scratch=[pltpu.VMEM((2, heads, TILE, TILE), F32)],
        args=(pa, pa, pa, ccol4, crow4), carried=carried, groups=kb)
    return outs[0], outs[1], outs[2:]


def _fox_bwd(name, pa, col0, ccol4, crow4, out, lse, dout, dcol0, carried=None):
    s = pa.shape[0]
    nb = s // TILE
    cb = col0 // FOX_LANES
    kb = FOX_WIDTH // FOX_LANES
    db = dcol0 // FOX_LANES

    def body(q_ref, k_ref, v_ref, cc_ref, cr_ref, o_ref, lse_ref, do_ref,
             dq_ref, dk_ref, dv_ref, cs_ref, dk_acc, dv_acc, p_s, ds_s):
        i = pl.program_id(1)
        head0 = pl.program_id(0) * FOX_GROUP

        @pl.when(i == 0)
        def _():
            dk_acc[...] = jnp.zeros_like(dk_acc)
            dv_acc[...] = jnp.zeros_like(dv_acc)

        @pl.when(jnp.logical_and(i == 0, head0 == 0))
        def _():
            cs_ref[...] = jnp.zeros_like(cs_ref)

        r = lax.broadcasted_iota(jnp.int32, (TILE, TILE), 0)
        c = lax.broadcasted_iota(jnp.int32, (TILE, TILE), 1)
        causal = c <= r
        qs = [_scaled_q(q_ref, _head_slices(hh)) for hh in range(FOX_GROUP)]
        cqs = [cc_ref[:, HEAD_DIM * hh:HEAD_DIM * hh + 1] for hh in range(FOX_GROUP)]
        lses = [lse_ref[:, HEAD_DIM * hh:HEAD_DIM * hh + 1] for hh in range(FOX_GROUP)]
        dofs = [do_ref[:, _head_slices(hh)] for hh in range(FOX_GROUP)]
        dos = [d_.astype(BF16) for d_ in dofs]
        dots = [d_.T.astype(BF16) for d_ in dofs]
        qts = [q.astype(F32).T.astype(BF16) for q in qs]
        deltas = [jnp.sum(dofs[hh] * o_ref[:, _head_slices(hh)], axis=1, keepdims=True) for hh in range(FOX_GROUP)]

        def probs(j, slot, rowsums, diag):
            kblk = k_ref[_rows(k_ref, j), :]
            vblk = v_ref[_rows(v_ref, j), :]
            new = []
            for hh in range(FOX_GROUP):
                sl = _head_slices(hh)
                sc = _fox_scores(qs[hh], kblk[:, sl], cqs[hh], cr_ref[j, pl.ds(head0 + hh, 1), :], causal, diag)
                p = jnp.exp(sc - lses[hh])
                ds = p * (_dot_nt(dos[hh], vblk[:, sl]) - deltas[hh])
                p_s[slot, hh] = p.astype(BF16)
                ds_s[slot, hh] = ds.astype(BF16)
                cs_ref[j, pl.ds(head0 + hh, 1), :] += jnp.sum(ds, axis=0, keepdims=True)
                new.append(rowsums[hh] + jnp.sum(ds, axis=1, keepdims=True))
            return tuple(new)

        def accumulate(j, slot, dqs):
            kblk = k_ref[_rows(k_ref, j), :]
            new = []
            for hh in range(FOX_GROUP):
                dsb = ds_s[slot, hh]
                dv_acc[hh, j] += _dot(dots[hh], p_s[slot, hh])
                dk_acc[hh, j] += _dot(qts[hh], dsb)
                new.append(dqs[hh] + _dot(dsb, kblk[:, _head_slices(hh)]))
            return tuple(new)

        def step(t, both):
            rowsums, dqs = both
            dqs = accumulate(i - t + 1, (t - 1) % 2, dqs)
            return probs(i - t, t % 2, rowsums, False), dqs

        zero1 = jnp.zeros((TILE, 1), F32)
        zero64 = jnp.zeros((TILE, HEAD_DIM), F32)
        rowsums, dqs = lax.fori_loop(1, i + 1, step,
                                     (probs(i, 0, (zero1,) * FOX_GROUP, True), (zero64,) * FOX_GROUP))
        dqs = accumulate(0, i % 2, dqs)
        for hh in range(FOX_GROUP):
            cs_ref[i, pl.ds(head0 + hh, 1), :] -= jnp.broadcast_to(rowsums[hh], (TILE, LANES)).T[0:1, :]
        dq_ref[...] = jnp.concatenate([dq * SCALE for dq in dqs], axis=1).astype(BF16)

        @pl.when(i == nb - 1)
        def _():
            for acc, ref in ((dk_acc, dk_ref), (dv_acc, dv_ref)):
                for j in range(nb):
                    ref[j * TILE:(j + 1) * TILE, :] = jnp.concatenate(
                        [acc[hh, j].T for hh in range(FOX_GROUP)], axis=1).astype(BF16)

    qspec = pl.BlockSpec((TILE, FOX_LANES), lambda p, i: (i, p))
    kvspec = pl.BlockSpec((s, FOX_LANES), lambda p, i: (0, p))
    o3 = jax.ShapeDtypeStruct((s, FOX_WIDTH), BF16)
    outs = _pair_grid_call(
        name, body, nb,
        in_specs=[pl.BlockSpec((TILE, FOX_LANES), lambda p, i: (i, cb + p)),
                  pl.BlockSpec((s, FOX_LANES), lambda p, i: (0, cb + kb + p)),
                  pl.BlockSpec((s, FOX_LANES), lambda p, i: (0, cb + 2 * kb + p)),
                  pl.BlockSpec((TILE, FOX_LANES), lambda p, i: (i, p)),
                  pl.BlockSpec((nb, 8, TILE), lambda p, i: (0, 0, 0)),
                  qspec,
                  pl.BlockSpec((TILE, FOX_LANES), lambda p, i: (i, p)),
                  pl.BlockSpec((TILE, FOX_LANES), lambda p, i: (i, db + p))],
        out_specs=[qspec, kvspec, kvspec, pl.BlockSpec((nb, 8, TILE), lambda p, i: (0, 0, 0))],
        out_shape=[o3, o3, o3, jax.ShapeDtypeStruct((nb, 8, TILE), F32)],
        scratch=[pltpu.VMEM((FOX_GROUP, nb, HEAD_DIM, TILE), F32), pltpu.VMEM((FOX_GROUP, nb, HEAD_DIM, TILE), F32),
                 pltpu.VMEM((2, FOX_GROUP, TILE, TILE), BF16), pltpu.VMEM((2, FOX_GROUP, TILE, TILE), BF16)],
        args=(pa, pa, pa, ccol4, crow4, out, lse, dout), carried=carried, groups=kb)
    return outs[:4], outs[4:]


def _mem_fwd(name, pa, mkv):
    s = pa.shape[0]
    ml = mkv.shape[0]
    nb = s // TILE
    cb = QKV_WIDTH // MEM_WIDTH

    def body(q_ref, k_ref, v_ref, o_ref, lse_ref):
        outs, lses = [], []
        for hh in range(MEM_HEADS):
            sl = _head_slices(hh)
            sc = _dot_nt(_scaled_q(q_ref, sl), k_ref[:, sl])
            m = jnp.max(sc, axis=1, keepdims=True)
            p = jnp.exp(sc - m)
            l = jnp.sum(p, axis=1, keepdims=True)
            outs.append(_dot(p.astype(BF16), v_ref[:, sl]) / l)
            lses.append(jnp.broadcast_to(m + jnp.log(l), (TILE, HEAD_DIM)))
        o_ref[...] = jnp.concatenate(outs, axis=1)
        lse_ref[...] = jnp.concatenate(lses, axis=1)

    row = pl.BlockSpec((TILE, MEM_WIDTH), lambda i: (i, 0))
    return pl.pallas_call(
        body, name=name, grid=(nb,),
        in_specs=[pl.BlockSpec((TILE, MEM_WIDTH), lambda i: (i, cb)),
                  pl.BlockSpec((ml, MEM_WIDTH), lambda i: (0, 0)),
                  pl.BlockSpec((ml, MEM_WIDTH), lambda i: (0, 1))],
        out_specs=[row, row],
        out_shape=[jax.ShapeDtypeStruct((s, MEM_WIDTH), F32), jax.ShapeDtypeStruct((s, MEM_WIDTH), F32)],
        compiler_params=_params("parallel"),
    )(pa, mkv, mkv)


def _mem_bwd(name, pa, mkv, out, lse, dout, dcol0):
    s = pa.shape[0]
    ml = mkv.shape[0]
    nb = s // TILE
    cb = QKV_WIDTH // MEM_WIDTH
    db = dcol0 // MEM_WIDTH

    def body(q_ref, k_ref, v_ref, o_ref, lse_ref, do_ref, dq_ref, dkv_ref, dk_acc, dv_acc):
        i = pl.program_id(0)

        @pl.when(i == 0)
        def _():
            dk_acc[...] = jnp.zeros_like(dk_acc)
            dv_acc[...] = jnp.zeros_like(dv_acc)

        dqs = []
        for hh in range(MEM_HEADS):
            sl = _head_slices(hh)
            q = _scaled_q(q_ref, sl)
            kh = k_ref[:, sl]
            dof = do_ref[:, sl]
            do = dof.astype(BF16)
            delta = jnp.sum(dof * o_ref[:, sl], axis=1, keepdims=True)
            p = jnp.exp(_dot_nt(q, kh) - lse_ref[:, HEAD_DIM * hh:HEAD_DIM * hh + 1])
            ds = (p * (_dot_nt(do, v_ref[:, sl]) - delta)).astype(BF16)
            dv_acc[hh] += _dot(dof.T.astype(BF16), p.astype(BF16))
            dk_acc[hh] += _dot(q.astype(F32).T.astype(BF16), ds)
            dqs.append(_dot(ds, kh) * SCALE)
        dq_ref[...] = jnp.concatenate(dqs, axis=1).astype(BF16)

        @pl.when(i == nb - 1)
        def _():
            dkv_ref[...] = jnp.concatenate([dk_acc[hh].T for hh in range(MEM_HEADS)]
                                           + [dv_acc[hh].T for hh in range(MEM_HEADS)], axis=1).astype(BF16)

    row = pl.BlockSpec((TILE, MEM_WIDTH), lambda i: (i, 0))
    return pl.pallas_call(
        body, name=name, grid=(nb,),
        in_specs=[pl.BlockSpec((TILE, MEM_WIDTH), lambda i: (i, cb)),
                  pl.BlockSpec((ml, MEM_WIDTH), lambda i: (0, 0)),
                  pl.BlockSpec((ml, MEM_WIDTH), lambda i: (0, 1)),
                  row, row,
                  pl.BlockSpec((TILE, MEM_WIDTH), lambda i: (i, db))],
        out_specs=[row, pl.BlockSpec((ml, 2 * MEM_WIDTH), lambda i: (0, 0))],
        out_shape=[jax.ShapeDtypeStruct((s, MEM_WIDTH), BF16), jax.ShapeDtypeStruct((ml, 2 * MEM_WIDTH), BF16)],
        scratch_shapes=[pltpu.VMEM((MEM_HEADS, HEAD_DIM, ml), F32), pltpu.VMEM((MEM_HEADS, HEAD_DIM, ml), F32)],
        compiler_params=_params("arbitrary"),
    )(pa, mkv, mkv, out, lse, dout)


def _head_maps():
    col = jnp.arange(MIX_WIDTH)[:, None] // HEAD_DIM
    g = (col == jnp.arange(LANES)[None, :]).astype(BF16)
    return g, g.T


def _normed_heads(osb_ref, ofx_ref, om_ref, g_ref, gt_ref):
    y = jnp.concatenate([osb_ref[...], ofx_ref[...], om_ref[...]], axis=1)
    msq = _sum_l2(y * y, g_ref[...]) * (1.0 / HEAD_DIM)
    rf = _sum_l3(lax.rsqrt(msq + EPS), gt_ref[...])
    return y * rf, rf


def _out_fwd(name, o_sb, o_fx, o_m, pb, ow, x, w_out, ts):
    s, d = x.shape
    g, gt = _head_maps()

    def body(osb_ref, ofx_ref, om_ref, gate_ref, ow_ref, x_ref, w_ref, g_ref, gt_ref, xo_ref, y2_ref):
        yh, _ = _normed_heads(osb_ref, ofx_ref, om_ref, g_ref, gt_ref)
        gate = gate_ref[...]
        y2 = (yh * ow_ref[...] * (gate * jax.nn.sigmoid(gate))).astype(BF16)
        y2_ref[...] = y2
        xo_ref[...] = x_ref[...] + _dot(y2, w_ref[...])

    return pl.pallas_call(
        body, name=name, grid=(s // ts,),
        in_specs=[_row_spec(ts, SB_WIDTH), _row_spec(ts, FOX_WIDTH), _row_spec(ts, MEM_WIDTH),
                  _row_spec(ts, MIX_WIDTH), _const_spec((1, MIX_WIDTH)), _row_spec(ts, d),
                  _const_spec((MIX_WIDTH, d)),
                  _const_spec((MIX_WIDTH, LANES)), _const_spec((LANES, MIX_WIDTH))],
        out_specs=[_row_spec(ts, d), _row_spec(ts, MIX_WIDTH)],
        out_shape=[jax.ShapeDtypeStruct((s, d), F32), jax.ShapeDtypeStruct((s, MIX_WIDTH), BF16)],
        compiler_params=_params("parallel"),
    )(o_sb, o_fx, o_m, pb, ow, x, w_out, g, gt)


def _row_spec(ts, w):
    return pl.BlockSpec((ts, w), lambda i: (i, 0))


def _const_spec(shape):
    return pl.BlockSpec(shape, lambda i: (0,) * len(shape))


def _out_bwd(name, dxb, o_sb, o_fx, o_m, pb, ow, w_out, ts):
    s, d = dxb.shape
    g, gt = _head_maps()

    def body(dx_ref, osb_ref, ofx_ref, om_ref, gate_ref, ow_ref, w_ref, g_ref, gt_ref, dy_ref, dgate_ref, dow_ref):
        @pl.when(pl.program_id(0) == 0)
        def _():
            dow_ref[...] = jnp.zeros_like(dow_ref)

        dy2 = _dot_nt(dx_ref[...], w_ref[...])
        yh, rf = _normed_heads(osb_ref, ofx_ref, om_ref, g_ref, gt_ref)
        gate = gate_ref[...]
        sig = jax.nn.sigmoid(gate)
        ow_v = ow_ref[...]
        dgate_ref[...] = (dy2 * (yh * ow_v) * (sig * (1.0 + gate * (1.0 - sig)))).astype(BF16)
        dn = dy2 * (gate * sig)
        dow_ref[...] += jnp.sum(dn * yh, axis=0, keepdims=True)
        dyh = dn * ow_v
        t = _sum_l2(dyh * yh, g_ref[...]) * (1.0 / HEAD_DIM)
        dy_ref[...] = rf * (dyh - yh * _sum_l3(t, gt_ref[...]))

    return pl.pallas_call(
        body, name=name, grid=(s // ts,),
        in_specs=[_row_spec(ts, d), _row_spec(ts, SB_WIDTH), _row_spec(ts, FOX_WIDTH), _row_spec(ts, MEM_WIDTH),
                  _row_spec(ts, MIX_WIDTH), _const_spec((1, MIX_WIDTH)),
                  _const_spec((MIX_WIDTH, d)),
                  _const_spec((MIX_WIDTH, LANES)), _const_spec((LANES, MIX_WIDTH))],
        out_specs=[_row_spec(ts, MIX_WIDTH), _row_spec(ts, MIX_WIDTH), _const_spec((1, MIX_WIDTH))],
        out_shape=[jax.ShapeDtypeStruct((s, MIX_WIDTH), F32), jax.ShapeDtypeStruct((s, PB), BF16),
                   jax.ShapeDtypeStruct((1, MIX_WIDTH), F32)],
        compiler_params=_params("arbitrary"),
    )(dxb, o_sb, o_fx, o_m, pb, ow, w_out, g, gt)


def _adamw(name, w, g, m, v, tr):
    def body(w_ref, g_ref, m_ref, v_ref, d_ref, m2_ref, v2_ref):
        gv = g_ref[...]
        m2 = ADAM_B1 * m_ref[...] + (1.0 - ADAM_B1) * gv
        v2 = ADAM_B2 * v_ref[...] + (1.0 - ADAM_B2) * (gv * gv)
        m_hat = m2 / (1.0 - ADAM_B1 ** ADAM_STEP)
        v_hat = v2 / (1.0 - ADAM_B2 ** ADAM_STEP)
        d_ref[...] = -ADAM_LR * (m_hat / (jnp.sqrt(v_hat) + ADAM_EPS) + ADAM_WD * w_ref[...])
        m2_ref[...] = m2
        v2_ref[...] = v2

    rest = w.shape[1:]
    spec = pl.BlockSpec((tr,) + rest, lambda i: (i,) + (0,) * len(rest))
    shp = jax.ShapeDtypeStruct(w.shape, F32)
    return pl.pallas_call(
        body, name=name, grid=(w.shape[0] // tr,), in_specs=[spec] * 4, out_specs=[spec] * 3, out_shape=[shp] * 3,
        compiler_params=_params("parallel"),
    )(w, g, m, v)


def _adamw_sharded(name, w, m, v, g_own, g_other, cvec, tr):
    depth, rows, cols = w.shape
    nt = rows // 2 // tr

    def body(c_ref, w_ref, m_ref, v_ref, *rest):
        g_refs, (g_ref, d_ref, m2_ref, v2_ref) = rest[:2 * depth], rest[2 * depth:]
        layer, mine = pl.program_id(0), pl.program_id(1) == c_ref[0]
        gv = None
        for lt in range(depth):
            cand = jnp.where(mine, g_refs[lt][...], g_refs[depth + lt][...])
            gv = cand if gv is None else jnp.where(layer == lt, cand, gv)
        m2 = ADAM_B1 * m_ref[...] + (1.0 - ADAM_B1) * gv
        v2 = ADAM_B2 * v_ref[...] + (1.0 - ADAM_B2) * (gv * gv)
        m_hat = m2 / (1.0 - ADAM_B1 ** ADAM_STEP)
        v_hat = v2 / (1.0 - ADAM_B2 ** ADAM_STEP)
        g_ref[...] = gv
        d_ref[...] = -ADAM_LR * (m_hat / (jnp.sqrt(v_hat) + ADAM_EPS) + ADAM_WD * w_ref[...])
        m2_ref[...] = m2
        v2_ref[...] = v2

    def g_map(lt, own):
        def index(l, hf, i, c_ref):
            use = jnp.logical_and(l == lt, (hf == c_ref[0]) == own)
            return jnp.where(use, i, 0), 0
        return index

    full = pl.BlockSpec((None, tr, cols), lambda l, hf, i, c_ref: (l, hf * nt + i, 0))
    g_specs = [pl.BlockSpec((tr, cols), g_map(lt, own)) for own in (True, False) for lt in range(depth)]
    shp = jax.ShapeDtypeStruct((depth, rows, cols), F32)
    return pl.pallas_call(
        body, name=name,
        grid_spec=pltpu.PrefetchScalarGridSpec(
            num_scalar_prefetch=1, grid=(depth, 2, nt), in_specs=[full] * 3 + g_specs, out_specs=[full] * 4),
        out_shape=[shp] * 4,
        compiler_params=_params("arbitrary", "arbitrary", "arbitrary"),
    )(cvec, w, m, v, *g_own, *g_other)


HBM_SPEC = pl.BlockSpec(memory_space=pltpu.HBM)


def _place():
    x, y, c = lax.axis_index("x"), lax.axis_index("y"), lax.axis_index("c")
    chips = [(1 - x, y), (x, 1 - y), (1 - x, 1 - y)]
    return x, y, c, chips


def _remote(src, dst, send_sems, recv_sems, k, to):
    return pltpu.make_async_remote_copy(src_ref=src, dst_ref=dst, send_sem=send_sems.at[k], recv_sem=recv_sems.at[k],
                                        device_id=to, device_id_type=MESH)


def _half_rows(n_rows, cc):
    rh = n_rows // 2
    return pl.ds(pl.multiple_of(cc * rh, 16), rh)


def _dma_sems(n):
    return [pltpu.SemaphoreType.DMA((n,)), pltpu.SemaphoreType.DMA((n,))]


class _Exchange:
    def __init__(self, inputs, out_shapes, n_sems, begin, relay, finish):
        self.inputs, self.out_shapes, self.n_sems = list(inputs), list(out_shapes), n_sems
        self.begin, self.relay, self.finish = begin, relay, finish

    @property
    def n(self):
        return len(self.inputs)

    def split(self, refs):
        return refs[:self.n], refs[self.n:2 * self.n], refs[2 * self.n], refs[2 * self.n + 1]


def _run_exchange(name, ex):
    def body(*refs):
        parts = ex.split(refs)
        for phase in (ex.begin, ex.relay, ex.finish):
            if phase is not None:
                phase(*parts)

    return pl.pallas_call(
        body, name=name, in_specs=[HBM_SPEC] * ex.n, out_specs=[HBM_SPEC] * ex.n, out_shape=ex.out_shapes,
        scratch_shapes=_dma_sems(ex.n_sems),
    )(*ex.inputs)


def _gather_exchange(shards):
    def ici(in_refs, out_refs, send_sems, recv_sems):
        x, y, c, chips = _place()
        return [_remote(in_ref.at[_half_rows(in_ref.shape[0], c)], out_ref.at[2 * x + y, _half_rows(in_ref.shape[0], c)],
                        send_sems, recv_sems, 6 * a + j, (cx, cy, c))
                for a, (in_ref, out_ref) in enumerate(zip(in_refs, out_refs)) for j, (cx, cy) in enumerate(chips)]

    def d2d(out_refs, send_sems, recv_sems, half_of):
        x, y, c, chips = _place()
        cps = []
        for a, out_ref in enumerate(out_refs):
            for j, (cx, cy) in enumerate(chips):
                piece = out_ref.at[2 * cx + cy, _half_rows(out_ref.shape[1], half_of(c))]
                cps.append(_remote(piece, piece, send_sems, recv_sems, 6 * a + 3 + j, (x, y, 1 - c)))
        return cps

    def begin(in_refs, out_refs, send_sems, recv_sems):
        for cp in ici(in_refs, out_refs, send_sems, recv_sems):
            cp.start()

    def relay(in_refs, out_refs, send_sems, recv_sems):
        x, y, c, chips = _place()
        for a, out_ref in enumerate(out_refs):
            for j, (cx, cy) in enumerate(chips):
                landed = out_ref.at[2 * cx + cy, _half_rows(out_ref.shape[1], c)]
                _remote(landed, landed, send_sems, recv_sems, 6 * a + j, (cx, cy, c)).wait_recv()
        for cp in d2d(out_refs, send_sems, recv_sems, lambda c_: c_):
            cp.start()

    def finish(in_refs, out_refs, send_sems, recv_sems):
        for cp in d2d(out_refs, send_sems, recv_sems, lambda c_: 1 - c_):
            cp.wait_recv()
        for cp in ici(in_refs, out_refs, send_sems, recv_sems) + d2d(out_refs, send_sems, recv_sems, lambda c_: c_):
            cp.wait_send()

    shapes = [jax.ShapeDtypeStruct((N_CHIPS,) + s_.shape, s_.dtype) for s_ in shards]
    return _Exchange(shards, shapes, 6 * len(shards), begin, relay, finish)


def _swap_exchange(g4s):
    def copies(in_refs, out_refs, send_sems, recv_sems):
        x, y, c, _ = _place()
        return [_remote(in_ref.at[:, _half_rows(in_ref.shape[1], 1 - c), :], out_ref, send_sems, recv_sems, a, (x, y, 1 - c))
                for a, (in_ref, out_ref) in enumerate(zip(in_refs, out_refs))]

    def begin(*parts):
        for cp in copies(*parts):
            cp.start()

    def finish(*parts):
        for cp in copies(*parts):
            cp.wait()

    shapes = [jax.ShapeDtypeStruct((g.shape[0], g.shape[1] // 2, g.shape[2]), g.dtype) for g in g4s]
    return _Exchange(g4s, shapes, len(g4s), begin, None, finish)


def _add_half(name, g4, r1, cvec, tr):
    n, r, w = g4.shape
    rh = r // 2
    nblk = rh // tr

    def body(c_ref, a_ref, b_ref, o_ref):
        o_ref[...] = (a_ref[...].astype(F32) + b_ref[...].astype(F32)).astype(BF16)

    return pl.pallas_call(
        body, name=name,
        grid_spec=pltpu.PrefetchScalarGridSpec(
            num_scalar_prefetch=1, grid=(n, nblk),
            in_specs=[pl.BlockSpec((None, tr, w), lambda k, i, c_ref: (k, c_ref[0] * nblk + i, 0)),
                      pl.BlockSpec((None, tr, w), lambda k, i, c_ref: (k, i, 0))],
            out_specs=pl.BlockSpec((None, tr, w), lambda k, i, c_ref: (k, i, 0))),
        out_shape=jax.ShapeDtypeStruct((n, rh, w), BF16),
        compiler_params=_params("parallel", "parallel"),
    )(cvec, g4, r1)


def _scatter_exchange(h4s):
    def sends(in_refs, out_refs, send_sems, recv_sems):
        x, y, c, chips = _place()
        return [_remote(in_ref.at[2 * cx + cy], out_ref.at[j], send_sems, recv_sems, 3 * a + j, (cx, cy, c))
                for a, (in_ref, out_ref) in enumerate(zip(in_refs, out_refs)) for j, (cx, cy) in enumerate(chips)]

    def begin(*parts):
        for cp in sends(*parts):
            cp.start()

    def finish(in_refs, out_refs, send_sems, recv_sems):
        x, y, c, chips = _place()
        for a, out_ref in enumerate(out_refs):
            for j, (cx, cy) in enumerate(chips):
                got = out_ref.at[j]
                _remote(got, got, send_sems, recv_sems, 3 * a + j, (cx, cy, c)).wait_recv()
        for cp in sends(in_refs, out_refs, send_sems, recv_sems):
            cp.wait_send()

    shapes = [jax.ShapeDtypeStruct((3,) + h.shape[1:], h.dtype) for h in h4s]
    return _Exchange(h4s, shapes, 3 * len(h4s), begin, None, finish)


def _sum_chips(name, h4, r3, mvec, tr):
    _, rh, w = h4.shape

    def body(m_ref, a_ref, b_ref, c_ref, d_ref, o_ref):
        o_ref[...] = ((a_ref[...].astype(F32) + b_ref[...].astype(F32)) + c_ref[...].astype(F32)) + d_ref[...].astype(F32)

    specs = [pl.BlockSpec((None, tr, w), lambda i, m_ref: (m_ref[0], i, 0))]
    specs += [pl.BlockSpec((None, tr, w), functools.partial(lambda k, i, m_ref: (k, i, 0), k)) for k in range(3)]
    return pl.pallas_call(
        body, name=name,
        grid_spec=pltpu.PrefetchScalarGridSpec(
            num_scalar_prefetch=1, grid=(rh // tr,), in_specs=specs,
            out_specs=pl.BlockSpec((tr, w), lambda i, m_ref: (i, 0))),
        out_shape=jax.ShapeDtypeStruct((rh, w), F32),
        compiler_params=_params("parallel"),
    )(mvec, h4, r3, r3, r3)


def _swap_reduced_exchange(ghs):
    def copies(in_refs, out_refs, send_sems, recv_sems):
        x, y, c, _ = _place()
        return [_remote(in_ref, out_ref, send_sems, recv_sems, a, (x, y, 1 - c))
                for a, (in_ref, out_ref) in enumerate(zip(in_refs, out_refs))]

    def begin(*parts):
        for cp in copies(*parts):
            cp.start()

    def finish(*parts):
        for cp in copies(*parts):
            cp.wait()

    return _Exchange(ghs, [jax.ShapeDtypeStruct(g.shape, g.dtype) for g in ghs], len(ghs), begin, None, finish)


class _SemaphoresFrom:
    def __init__(self, sems, first):
        self.sems, self.first = sems, first

    @property
    def at(self):
        return self

    def __getitem__(self, k):
        return self.sems.at[self.first + k]


def _both(a, b):
    def phase(fa, fb):
        if fa is None and fb is None:
            return None

        def run(in_refs, out_refs, send_sems, recv_sems):
            if fa is not None:
                fa(in_refs[:a.n], out_refs[:a.n], send_sems, recv_sems)
            if fb is not None:
                fb(in_refs[a.n:], out_refs[a.n:], _SemaphoresFrom(send_sems, a.n_sems), _SemaphoresFrom(recv_sems, a.n_sems))

        return run

    return _Exchange(a.inputs + b.inputs, a.out_shapes + b.out_shapes, a.n_sems + b.n_sems,
                     phase(a.begin, b.begin), phase(a.relay, b.relay), phase(a.finish, b.finish))


def _small_update(name, partials, weights, moments1, moments2):
    n = len(partials)
    width = max(p.shape[1] for p in partials)
    starts, at = [], 0
    for p in partials:
        starts.append(at)
        at += p.shape[0]
    rows = -(-at // 8) * 8
    has_w = [w is not None for w in weights]
    n_w = sum(has_w)

    def body(*refs):
        p_refs = refs[:n]
        w_refs, m_refs, v_refs = refs[n:n + n_w], refs[n + n_w:n + 2 * n_w], refs[n + 2 * n_w:n + 3 * n_w]
        outs = refs[n + 3 * n_w:-4]
        g_refs, upd_refs = outs[:n], outs[n:]
        vec, buf, send_sems, recv_sems = refs[-4:]
        x, y, c, _ = _place()
        me = 4 * x + 2 * y + c
        vec[...] = jnp.zeros_like(vec)
        for p_ref, r0 in zip(p_refs, starts):
            vec[r0:r0 + p_ref.shape[0], 0:p_ref.shape[1]] = p_ref[...]
        buf[me] = vec[...]
        flips = [(fx, fy, fc) for fx in (0, 1) for fy in (0, 1) for fc in (0, 1)][1:]
        peers = [(x + fx - 2 * x * fx, y + fy - 2 * y * fy, c + fc - 2 * c * fc) for fx, fy, fc in flips]
        sends = [_remote(vec, buf.at[me], send_sems, recv_sems, k, peer) for k, peer in enumerate(peers)]
        for cp in sends:
            cp.start()
        for k, (px, py, pc) in enumerate(peers):
            got = buf.at[4 * px + 2 * py + pc]
            _remote(got, got, send_sems, recv_sems, k, (px, py, pc)).wait_recv()
        for cp in sends:
            cp.wait_send()
        total = buf[0]
        for dev in range(1, N_DEV):
            total = total + buf[dev]
        k = 0
        for a in range(n):
            r, w = g_refs[a].shape
            g = total[starts[a]:starts[a] + r, 0:w]
            g_refs[a][...] = g
            if has_w[a]:
                m2 = ADAM_B1 * m_refs[k][...] + (1.0 - ADAM_B1) * g
                v2 = ADAM_B2 * v_refs[k][...] + (1.0 - ADAM_B2) * (g * g)
                m_hat = m2 / (1.0 - ADAM_B1 ** ADAM_STEP)
                v_hat = v2 / (1.0 - ADAM_B2 ** ADAM_STEP)
                upd_refs[3 * k][...] = -ADAM_LR * (m_hat / (jnp.sqrt(v_hat) + ADAM_EPS) + ADAM_WD * w_refs[k][...])
                upd_refs[3 * k + 1][...] = m2
                upd_refs[3 * k + 2][...] = v2
                k += 1

    ws = [w for w in weights if w is not None]
    g_shapes = [jax.ShapeDtypeStruct(p.shape if w is None else w.shape, F32) for p, w in zip(partials, weights)]
    u_shapes = [jax.ShapeDtypeStruct(w.shape, F32) for w in ws for _ in range(3)]
    vm = pl.BlockSpec(memory_space=pltpu.VMEM)
    n_args = n + 3 * n_w
    outs = pl.pallas_call(
        body, name=name, in_specs=[vm] * n_args, out_specs=[vm] * (n + 3 * n_w), out_shape=g_shapes + u_shapes,
        scratch_shapes=[pltpu.VMEM((rows, width), F32), pltpu.VMEM((N_DEV, rows, width), F32),
                        pltpu.SemaphoreType.DMA((7,)), pltpu.SemaphoreType.DMA((7,))],
    )(*partials, *ws, *[m for m in moments1 if m is not None], *[v for v in moments2 if v is not None])
    return outs[:n], outs[n:]


GATE_COL = 3 * SB_WIDTH + 3 * FOX_WIDTH + FOX_HEADS + MEM_WIDTH
FL_COL = QKV_WIDTH


GROUP_A_COLS = [(0, QKV_WIDTH), (FL_COL + FOX_HEADS, MEM_WIDTH)]
GROUP_B_COLS = [(GATE_COL, MIX_WIDTH), (FL_COL, FOX_HEADS)]


def _group_from_shards(shard_of, cw, spans, pad):
    parts = []
    for lo, width in spans:
        hi = lo + width
        for j in range(N_CHIPS):
            a, b = max(lo, j * cw), min(hi, (j + 1) * cw)
            if a < b:
                parts.append(shard_of(j)[:, a - j * cw:b - j * cw])
    if pad:
        parts.append(jnp.zeros((parts[0].shape[0], pad), parts[0].dtype))
    return jnp.concatenate(parts, axis=1)


def _shard_from_groups(ga, gb, j, cw):
    lo, hi = j * cw, (j + 1) * cw
    placed = []
    for grp, spans in ((ga, GROUP_A_COLS), (gb, GROUP_B_COLS)):
        at = 0
        for first, width in spans:
            a, b = max(lo, first), min(hi, first + width)
            if a < b:
                placed.append((a, grp[:, at + a - first:at + b - first]))
            at += width
    return jnp.concatenate([p for _, p in sorted(placed, key=lambda t: t[0])], axis=1)


def _tile_of(n, cap, unit):
    if n <= cap:
        return n
    best = None
    for t in range(unit, cap + 1, unit):
        if n % t == 0:
            best = t
    assert best is not None, (n, cap, unit)
    return best


def _column_major_rows(a):
    dp, r, c = a.shape
    return a.transpose(2, 0, 1).reshape(c, dp, r // LANES, LANES).transpose(0, 2, 1, 3).reshape(-1, 8, LANES)


def _from_column_major_rows(b, shape):
    dp, r, c = shape
    return b.reshape(c, r // LANES, dp, LANES).transpose(0, 2, 1, 3).reshape(c, dp, r).transpose(1, 2, 0)


def kernel(x, mem, norm_w, w_in, b_forget, mem_norm_w, w_mem_kv, out_norm_w, w_out, final_norm_w, loss_target, m_norm_w, m_w_in, m_b_forget, m_mem_norm_w, m_w_mem_kv, m_out_norm_w, m_w_out, m_final_norm_w, v_norm_w, v_w_in, v_b_forget, v_mem_norm_w, v_w_mem_kv, v_out_norm_w, v_w_out, v_final_norm_w):
    xs = x[0]
    mems = mem[0]
    target = loss_target[0]
    s, d = xs.shape
    depth = norm_w.shape[0]
    nb = s // TILE
    ts = _tile_of(s, 512, 8)
    big = (w_in, w_mem_kv, w_out)
    core = lax.axis_index("c")
    chip = 2 * lax.axis_index("x") + lax.axis_index("y")
    cvec = core.astype(jnp.int32).reshape(1)
    mvec = chip.astype(jnp.int32).reshape(1)
    cw = w_in.shape[2]

    own_w = [[a[l].astype(BF16) for a in big] for l in range(depth)]

    def lay_out_in(own, got):
        shard_of = lambda j: jnp.where(chip == j, own, got[j])
        return (_group_from_shards(shard_of, cw, GROUP_A_COLS, 0),
                _group_from_shards(shard_of, cw, GROUP_B_COLS, LANES - FOX_HEADS))

    def lay_out_rows(own, got):
        full = jnp.where(lax.broadcasted_iota(jnp.int32, got.shape, 0) == chip, own[None], got)
        return full.reshape(-1, full.shape[2])

    w_in_groups = [lay_out_in(own_w[0][0], _run_exchange("gather_weights0", _gather_exchange(own_w[0][:1]))[0])]
    layer_w = []

    tm = _tile_of(s, 256, 8)
    fl_block = MIX_WIDTH // LANES

    saved = []
    cur = xs
    for l in range(depth):
        wa, wb = w_in_groups[l]
        h = _rms_fwd(f"rms_fwd{l}", cur, norm_w[l][None], ts)
        pa = _mm(f"inproj_a{l}", h, wa, "nn", ts, PA, BF16)
        pb = _mm(f"inproj_b{l}", h, wb, "nn", ts, PB, F32)
        bpad = jnp.pad(b_forget[l], (0, LANES - FOX_HEADS))[None]
        ccol4, crow4 = _gate_fwd(f"gate_fwd{l}", pb, bpad, fl_block)
        more = l + 1 < depth
        o_sb, got = _sb_fwd(f"sb_fwd{l}", pa, 0, carried=_gather_exchange(own_w[l][1:]))
        wkv, wout = lay_out_rows(own_w[l][1], got[0]), lay_out_rows(own_w[l][2], got[1])
        layer_w.append((wa, wb, wkv, wout))
        o_fx, lse_fx, got = _fox_fwd(f"fox_fwd{l}", pa, 3 * SB_WIDTH, ccol4, crow4,
                                     carried=_gather_exchange(own_w[l + 1][:1]) if more else None)
        if more:
            w_in_groups.append(lay_out_in(own_w[l + 1][0], got[0]))
        mn = _rms_fwd(f"mem_rms{l}", mems, mem_norm_w[l][None], mems.shape[0])
        mkv = _mm(f"mem_kv{l}", mn, wkv, "nn", mems.shape[0], 2 * MEM_WIDTH, BF16)
        o_m, lse_m = _mem_fwd(f"mem_fwd{l}", pa, mkv)
        nxt, y2 = _out_fwd(f"out_fwd{l}", o_sb, o_fx, o_m, pb, out_norm_w[l][None], cur, wout, tm)
        saved.append((cur, h, pa, pb, bpad, ccol4, crow4, o_sb, o_fx, lse_fx, mn, mkv, o_m, lse_m, y2))
        cur = nxt

    loss_v, dx, dxb, g_final = _final_loss("final_loss", cur, final_norm_w[None], target, ts)

    g_norm, g_b, g_memnorm, g_outnorm = [None] * depth, [None] * depth, [None] * depth, [None] * depth
    g_wa, g_wb, g_wkv, g_wout = [None] * depth, [None] * depth, [None] * depth, [None] * depth
    g_own = [[None] * depth for _ in big]
    g_other = [[None] * depth for _ in big]

    def swap_of(jobs):
        return _swap_exchange([g for _, _, g, _ in jobs])

    def chip_sums(jobs, got):
        return [(lr, k, _add_half(f"grad_add_half{lr}_{k}", g, r_, cvec, t_), t_) for (lr, k, g, t_), r_ in zip(jobs, got)]

    def sum_at_owner(jobs, from_chips):
        return [_sum_chips(f"grad_sum_chips{lr}_{k}", h_, r_, mvec, t_) for (lr, k, h_, t_), r_ in zip(jobs, from_chips)]

    def keep(jobs, halves, others):
        for (lr, k, _, _), mine, other in zip(jobs, halves, others):
            g_own[k][lr], g_other[k][lr] = mine, other

    def job(lr, k, g4):
        return lr, k, g4, _tile_of(g4.shape[1] // 2, 256, 16)

    pending = []
    for l in reversed(range(depth)):
        xin, h, pa, pb, bpad, ccol4, crow4, o_sb, o_fx, lse_fx, mn, mkv, o_m, lse_m, y2 = saved[l]
        wa, wb, wkv, wout = layer_w[l]
        dy, dgate, g_outnorm[l] = _out_bwd(f"out_bwd{l}", dxb, o_sb, o_fx, o_m, pb, out_norm_w[l][None], wout, tm)
        g_wout[l] = _mm(f"dw_out{l}", y2, dxb, "tn", _tile_of(MIX_WIDTH, 640, LANES), d, F32)
        dq_m, dmkv = _mem_bwd(f"mem_bwd{l}", pa, mkv, o_m, lse_m, dy, SB_WIDTH + FOX_WIDTH)
        g_wkv[l] = _mm(f"dw_kv{l}", mn, dmkv, "tn", d, 2 * MEM_WIDTH, F32)
        dmn = _mm(f"dmem{l}", dmkv, wkv, "nt", mems.shape[0], d, F32)
        g_memnorm[l] = _rms_wgrad(f"mem_norm_grad{l}", mems, dmn)
        small = [job(l, 1, g_wkv[l].reshape(N_CHIPS, -1, g_wkv[l].shape[1])), job(l, 2, g_wout[l].reshape(N_CHIPS, -1, d))]
        (dq_fx, dk_fx, dv_fx, cs4), got = _fox_bwd(f"fox_bwd{l}", pa, 3 * SB_WIDTH, ccol4, crow4, o_fx, lse_fx, dy,
                                                    SB_WIDTH, carried=swap_of(small))
        pending += chip_sums(small, got)
        (dq_sb, dk_sb, dv_sb), from_chips = _sb_bwd(f"sb_bwd{l}", pa, 0, dy, 0,
                                                   carried=_scatter_exchange([j[2] for j in pending]))
        reduced_jobs, reduced = pending, sum_at_owner(pending, from_chips)
        swap_back = _swap_reduced_exchange(reduced)
        dpb, g_b[l] = _gate_bwd(f"gate_bwd{l}", pb, bpad, cs4, fl_block, dgate)
        dpa = jnp.concatenate([dq_sb, dk_sb, dv_sb, dq_fx, dk_fx, dv_fx, dq_m], axis=1)
        tw = _tile_of(d, 512, LANES)
        g_wa[l] = _mm(f"dw_in_a{l}", h, dpa, "tn", tw, _tile_of(PA, 1664, LANES), BF16)
        g_wb[l] = _mm(f"dw_in_b{l}", h, dpb, "tn", tw, PB, BF16)
        g4_in = jnp.stack([_shard_from_groups(g_wa[l], g_wb[l], j, cw) for j in range(N_CHIPS)])
        w_in_job = [job(l, 0, g4_in)]
        if l > 0:
            dx, dxb, g_norm[l], got = _inproj_bwd(f"inproj_bwd{l}", dpa, dpb, wa, wb, xin, norm_w[l][None], dx, tm,
                                                  carried=_both(swap_of(w_in_job), swap_back))
            pending = chip_sums(w_in_job, got[:1])
            keep(reduced_jobs, reduced, got[1:])
        else:
            pending = chip_sums(w_in_job, _run_exchange("grad_swap_halves_last", swap_of(w_in_job)))
            dx, dxb, g_norm[l], got = _inproj_bwd(f"inproj_bwd{l}", dpa, dpb, wa, wb, xin, norm_w[l][None], dx, tm,
                                                  carried=_both(_scatter_exchange([j[2] for j in pending]), swap_back))
            keep(reduced_jobs, reduced, got[1:])
            last = sum_at_owner(pending, got[:1])
            keep(pending, last, _run_exchange("grad_swap_reduced_last", _swap_reduced_exchange(last)))

    small_w = [norm_w, b_forget, mem_norm_w, out_norm_w, final_norm_w]
    small_m = [m_norm_w, m_b_forget, m_mem_norm_w, m_out_norm_w, m_final_norm_w]
    small_v = [v_norm_w, v_b_forget, v_mem_norm_w, v_out_norm_w, v_final_norm_w]
    rows2 = lambda a: a.reshape(-1, a.shape[-1])
    partials = [jnp.concatenate(g_norm, axis=0), jnp.concatenate(g_b, axis=0), jnp.concatenate(g_memnorm, axis=0),
                jnp.concatenate(g_outnorm, axis=0), g_final, loss_v]
    sums, updates = _small_update("small_update", partials, [rows2(a) for a in small_w] + [None],
                                  [rows2(a) for a in small_m] + [None], [rows2(a) for a in small_v] + [None])
    small_grads = [g.reshape(a.shape) for g, a in zip(sums, small_w)]
    loss = sums[-1][0, 0]
    small_delta, small_m2, small_v2 = ([updates[3 * k + t].reshape(a.shape) for k, a in enumerate(small_w)]
                                       for t in range(3))
    big_grads, big_delta, big_m2, big_v2 = [], [], [], []
    for k, (nm, w_, m_, v_) in enumerate(zip(("w_in", "w_mem_kv", "w_out"), big, (m_w_in, m_w_mem_kv, m_w_out),
                                             (v_w_in, v_w_mem_kv, v_w_out))):
        if w_.shape[2] % LANES:
            g_full = jnp.stack([jnp.concatenate([jnp.where(core == 0, go, gt), jnp.where(core == 0, gt, go)], axis=0)
                                for go, gt in zip(g_own[k], g_other[k])])
            w_p, g_p, m_p, v_p = (_column_major_rows(a) for a in (w_, g_full, m_, v_))
            outs = _adamw(f"adamw_{nm}", w_p, g_p, m_p, v_p, _tile_of(w_p.shape[0], 600, 1))
            outs = [_from_column_major_rows(o, w_.shape) for o in (g_p, *outs)]
        else:
            outs = _adamw_sharded(f"adamw_{nm}", w_, m_, v_, g_own[k], g_other[k], cvec,
                                  _tile_of(w_.shape[1] // 2, 256, 8))
        for lst, o in zip((big_grads, big_delta, big_m2, big_v2), outs):
            lst.append(o)

    def order(sm, bg):
        return [sm[0], bg[0], sm[1], sm[2], bg[1], sm[3], bg[2], sm[4]]

    return (loss, dx[None], *order(small_grads, big_grads), *order(small_delta, big_delta),
            *order(small_m2, big_m2), *order(small_v2, big_v2))
```

```python
import functools

import jax
import jax.numpy as jnp
from jax import lax
from jax.experimental import pallas as pl
from jax.experimental.pallas import tpu as pltpu

F32 = jnp.float32
BF16 = jnp.bfloat16

HEAD_DIM = 64
SB_WIDTH = 512
FOX_WIDTH = 512
FOX_HEADS = 8
MEM_WIDTH = 256
MEM_HEADS = MEM_WIDTH // HEAD_DIM
MIX_WIDTH = SB_WIDTH + FOX_WIDTH + MEM_WIDTH
TOTAL_HEADS = MIX_WIDTH // HEAD_DIM
IN_WIDTH = 3 * SB_WIDTH + 3 * FOX_WIDTH + FOX_HEADS + MEM_WIDTH + MIX_WIDTH
LANES = 128
QKV_WIDTH = 3 * SB_WIDTH + 3 * FOX_WIDTH
PA = QKV_WIDTH + MEM_WIDTH
PB = LANES + MIX_WIDTH
EPS = 1e-6
SCALE = HEAD_DIM ** -0.5
TILE = 256
SB_GROUP = 4
SB_LANES = SB_GROUP * HEAD_DIM
SB_FWD_GROUP = 8
FOX_GROUP = 4
FOX_LANES = FOX_GROUP * HEAD_DIM
FOX_FWD_GROUP = 4
NEG_INF = float("-inf")
MASKED = -1e30

ADAM_LR = 0.001
ADAM_B1 = 0.9
ADAM_B2 = 0.999
ADAM_EPS = 1e-08
ADAM_WD = 0.01
ADAM_STEP = 10

N_CHIPS = 4
N_DEV = 8
VMEM_LIMIT = 48 * 1024 * 1024
MESH = pl.DeviceIdType.MESH


def _params(*sem):
    return pltpu.CompilerParams(dimension_semantics=tuple(sem), vmem_limit_bytes=VMEM_LIMIT)


def _dot(a, b):
    return jnp.dot(a, b, preferred_element_type=F32)


def _dot_nt(a, b):
    return lax.dot_general(a, b, (((1,), (1,)), ((), ())), preferred_element_type=F32)


def _dot_tn(a, b):
    return lax.dot_general(a, b, (((0,), (0,)), ((), ())), preferred_element_type=F32)


def _split2(x):
    hi = x.astype(BF16)
    lo = (x - hi.astype(F32)).astype(BF16)
    return hi, lo


def _split3(x):
    hi = x.astype(BF16)
    r = x - hi.astype(F32)
    mid = r.astype(BF16)
    lo = (r - mid.astype(F32)).astype(BF16)
    return hi, mid, lo


def _sum_l2(x, u):
    hi, lo = _split2(x)
    return _dot(hi, u) + _dot(lo, u)


def _sum_l3(x, u):
    hi, mid, lo = _split3(x)
    return _dot(hi, u) + _dot(mid, u) + _dot(lo, u)


def _sum_r3(u, x):
    hi, mid, lo = _split3(x)
    return _dot(u, hi) + _dot(u, mid) + _dot(u, lo)


def _tri(n, pred):
    r = lax.broadcasted_iota(jnp.int32, (n, n), 0)
    c = lax.broadcasted_iota(jnp.int32, (n, n), 1)
    return jnp.where(pred(r, c), 1.0, 0.0).astype(BF16)


def _rows(ref, j, n=TILE):
    return pl.ds(pl.multiple_of(j * n, n), n)


def _mm(name, a, b, mode, tm, tn, out_dtype, res=None, a_lead=(), b_lead=()):
    a2, b2 = a.shape[len(a_lead):], b.shape[len(b_lead):]
    if mode == "tn":
        k, m = a2
    else:
        m, k = a2
    n = b2[0] if mode == "nt" else b2[1]
    assert m % tm == 0 and n % tn == 0, (name, m, tm, n, tn)
    na, nb = (None,) * len(a_lead), (None,) * len(b_lead)
    if mode == "tn":
        a_spec = pl.BlockSpec(na + (k, tm), lambda j, i: a_lead + (0, i))
    else:
        a_spec = pl.BlockSpec(na + (tm, k), lambda j, i: a_lead + (i, 0))
    if mode == "nt":
        b_spec = pl.BlockSpec(nb + (tn, k), lambda j, i: b_lead + (j, 0))
    else:
        b_spec = pl.BlockSpec(nb + (k, tn), lambda j, i: b_lead + (0, j))
    o_spec = pl.BlockSpec((tm, tn), lambda j, i: (i, j))
    dot = {"nn": _dot, "nt": _dot_nt, "tn": _dot_tn}[mode]

    def body(a_ref, b_ref, *rest):
        o_ref = rest[-1]
        acc = dot(a_ref[...].astype(BF16), b_ref[...].astype(BF16))
        if res is not None:
            acc = acc + rest[0][...]
        o_ref[...] = acc.astype(o_ref.dtype)

    args, specs = [a, b], [a_spec, b_spec]
    if res is not None:
        args.append(res)
        specs.append(o_spec)
    return pl.pallas_call(
        body, name=name, grid=(n // tn, m // tm), in_specs=specs, out_specs=o_spec,
        out_shape=jax.ShapeDtypeStruct((m, n), out_dtype),
        compiler_params=_params("parallel", "parallel"),
    )(*args)


def _rms_fwd(name, x, g, ts):
    s, d = x.shape

    def body(x_ref, g_ref, o_ref):
        xf = x_ref[...]
        r = lax.rsqrt(jnp.mean(xf * xf, axis=1, keepdims=True) + EPS)
        o_ref[...] = (xf * r * g_ref[...]).astype(BF16)

    return pl.pallas_call(
        body, name=name, grid=(s // ts,),
        in_specs=[pl.BlockSpec((ts, d), lambda i: (i, 0)), pl.BlockSpec((1, d), lambda i: (0, 0))],
        out_specs=pl.BlockSpec((ts, d), lambda i: (i, 0)),
        out_shape=jax.ShapeDtypeStruct((s, d), BF16),
        compiler_params=_params("parallel"),
    )(x, g)


def _inproj_bwd(name, dpa, dpb, wa, wb, x, g, dres, ts, carried=None):
    s, d = x.shape

    def body(dpa_ref, dpb_ref, wa_ref, wb_ref, x_ref, g_ref, dres_ref, dx_ref, dxb_ref, dg_ref):
        @pl.when(pl.program_id(1) == 0)
        def _():
            dg_ref[...] = jnp.zeros_like(dg_ref)

        dhf = _dot_nt(dpa_ref[...], wa_ref[...]) + _dot_nt(dpb_ref[...], wb_ref[...])
        xf = x_ref[...]
        r = lax.rsqrt(jnp.mean(xf * xf, axis=1, keepdims=True) + EPS)
        xh = xf * r
        dg_ref[...] += jnp.sum(dhf * xh, axis=0, keepdims=True)
        dxh = dhf * g_ref[...]
        m = jnp.mean(dxh * xh, axis=1, keepdims=True)
        dx = r * (dxh - xh * m) + dres_ref[...]
        dx_ref[...] = dx
        dxb_ref[...] = dx.astype(BF16)

    row = lambda w: pl.BlockSpec((ts, w), lambda p, i: (i, 0))
    whole = lambda a: pl.BlockSpec(a.shape, lambda p, i: (0, 0))
    outs = _pair_grid_call(
        name, body, s // ts,
        in_specs=[row(dpa.shape[1]), row(dpb.shape[1]), whole(wa), whole(wb), row(d), whole(g), row(d)],
        out_specs=[row(d), row(d), pl.BlockSpec((1, d), lambda p, i: (0, 0))],
        out_shape=[jax.ShapeDtypeStruct((s, d), F32), jax.ShapeDtypeStruct((s, d), BF16),
                   jax.ShapeDtypeStruct((1, d), F32)],
        scratch=[], args=(dpa, dpb, wa, wb, x, g, dres), carried=carried, groups=1)
    return outs[0], outs[1], outs[2], outs[3:]


def _rms_wgrad(name, x, dh):
    m_, d = x.shape

    def body(x_ref, dh_ref, dg_ref):
        xf = x_ref[...]
        r = lax.rsqrt(jnp.mean(xf * xf, axis=1, keepdims=True) + EPS)
        dg_ref[...] = jnp.sum(dh_ref[...] * xf * r, axis=0, keepdims=True)

    return pl.pallas_call(
        body, name=name, out_shape=jax.ShapeDtypeStruct((1, d), F32),
    )(x, dh)


def _final_loss(name, x, g, target, ts):
    s, d = x.shape

    def body(x_ref, g_ref, t_ref, loss_ref, dx_ref, dxb_ref, dg_ref):
        @pl.when(pl.program_id(0) == 0)
        def _():
            dg_ref[...] = jnp.zeros_like(dg_ref)
            loss_ref[...] = jnp.zeros_like(loss_ref)

        xf = x_ref[...]
        gw = g_ref[...]
        r = lax.rsqrt(jnp.mean(xf * xf, axis=1, keepdims=True) + EPS)
        xh = xf * r
        e = xh * gw - t_ref[...]
        part = 0.5 * jnp.sum(jnp.mean(e * e, axis=1, keepdims=True), axis=0, keepdims=True)
        loss_ref[...] += jnp.broadcast_to(part, loss_ref.shape)
        dy = e * (1.0 / d)
        dg_ref[...] += jnp.sum(dy * xh, axis=0, keepdims=True)
        dxh = dy * gw
        m = jnp.mean(dxh * xh, axis=1, keepdims=True)
        dx = r * (dxh - xh * m)
        dx_ref[...] = dx
        dxb_ref[...] = dx.astype(BF16)

    row = pl.BlockSpec((ts, d), lambda i: (i, 0))
    vec = pl.BlockSpec((1, d), lambda i: (0, 0))
    lvec = pl.BlockSpec((1, LANES), lambda i: (0, 0))
    return pl.pallas_call(
        body, name=name, grid=(s // ts,), in_specs=[row, vec, row], out_specs=[lvec, row, row, vec],
        out_shape=[jax.ShapeDtypeStruct((1, LANES), F32), jax.ShapeDtypeStruct((s, d), F32),
                   jax.ShapeDtypeStruct((s, d), BF16), jax.ShapeDtypeStruct((1, d), F32)],
        compiler_params=_params("arbitrary"),
    )(x, g, target)


def _gate_fwd(name, pb, bpad, fl_block):
    s = pb.shape[0]
    nb = s // TILE

    def body(fl_ref, b_ref, ccol_ref, crow_ref, carry):
        @pl.when(pl.program_id(0) == 0)
        def _():
            carry[...] = jnp.zeros_like(carry)

        u = fl_ref[...] + b_ref[...]
        lf = jnp.minimum(u, 0.0) - jnp.log1p(jnp.exp(-jnp.abs(u)))
        lower = _tri(TILE, lambda r, c: c <= r)
        c = _sum_r3(lower, lf) + carry[0:1, :]
        ccol_ref[...] = jnp.concatenate(
            [jnp.broadcast_to(c[:, hh:hh + 1], (TILE, HEAD_DIM)) for hh in range(FOX_HEADS)], axis=1)
        crow_ref[0] = c.T[0:8, :]
        carry[...] = jnp.broadcast_to(c[TILE - 1:TILE, :], carry.shape)

    return pl.pallas_call(
        body, name=name, grid=(nb,),
        in_specs=[pl.BlockSpec((TILE, LANES), lambda i: (i, fl_block)), pl.BlockSpec((1, LANES), lambda i: (0, 0))],
        out_specs=[pl.BlockSpec((TILE, FOX_WIDTH), lambda i: (i, 0)), pl.BlockSpec((1, 8, TILE), lambda i: (i, 0, 0))],
        out_shape=[jax.ShapeDtypeStruct((s, FOX_WIDTH), F32), jax.ShapeDtypeStruct((nb, 8, TILE), F32)],
        scratch_shapes=[pltpu.VMEM((8, LANES), F32)],
        compiler_params=_params("arbitrary"),
    )(pb, bpad)


def _gate_bwd(name, pb, bpad, colsum, fl_block, dpb):
    s = pb.shape[0]
    nb = s // TILE

    def body(fl_ref, b_ref, cs_ref, dpb_ref, dl_ref, db_ref, carry):
        @pl.when(pl.program_id(0) == 0)
        def _():
            carry[...] = jnp.zeros_like(carry)
            db_ref[...] = jnp.zeros_like(db_ref)

        upper = _tri(TILE, lambda r, c: r >= c)
        rsum = _sum_l3(cs_ref[0], upper) + carry[:, 0:1]
        carry[...] = jnp.broadcast_to(rsum[:, 0:1], carry.shape)
        full = jnp.concatenate([rsum, jnp.zeros((LANES - 8, TILE), F32)], axis=0)
        dlf = -full.T
        u = fl_ref[...] + b_ref[...]
        dlogit = dlf * (1.0 - jax.nn.sigmoid(u))
        dl_ref[...] = dlogit.astype(BF16)
        db_ref[...] += jnp.sum(dlogit, axis=0, keepdims=True)

    logits_block = pl.BlockSpec((TILE, LANES), lambda i: (nb - 1 - i, fl_block))
    return pl.pallas_call(
        body, name=name, grid=(nb,),
        in_specs=[logits_block, pl.BlockSpec((1, LANES), lambda i: (0, 0)),
                  pl.BlockSpec((1, 8, TILE), lambda i: (nb - 1 - i, 0, 0)), pl.BlockSpec(memory_space=pl.ANY)],
        out_specs=[logits_block, pl.BlockSpec((1, LANES), lambda i: (0, 0))],
        out_shape=[jax.ShapeDtypeStruct(dpb.shape, BF16), jax.ShapeDtypeStruct((1, LANES), F32)],
        scratch_shapes=[pltpu.VMEM((8, LANES), F32)], input_output_aliases={3: 0},
        compiler_params=_params("arbitrary"),
    )(pb, bpad, colsum, dpb)


def _head_slices(hh):
    return slice(HEAD_DIM * hh, HEAD_DIM * (hh + 1))


def _scaled_q(q_ref, sl, scale=SCALE):
    return (q_ref[:, sl].astype(F32) * scale).astype(BF16)


def _neg_abs(x):
    sign = jnp.uint32(0x80000000)
    return lax.bitcast_convert_type(lax.bitcast_convert_type(x, jnp.uint32) | sign, F32)


def _pair_grid_call(name, body, nb, in_specs, out_specs, out_shape, scratch, args, carried=None, groups=4):
    if carried is None:
        return pl.pallas_call(
            body, name=name, grid=(groups, nb), in_specs=in_specs, out_specs=out_specs, out_shape=out_shape,
            scratch_shapes=scratch, compiler_params=_params("arbitrary", "arbitrary"),
        )(*args)
    n_in, n_out, n_ex = len(in_specs), len(out_specs), carried.n

    def body_with_copies(*refs):
        own_in, ex_in = refs[:n_in], refs[n_in:n_in + n_ex]
        own_out = refs[n_in + n_ex:n_in + n_ex + n_out]
        ex_out = refs[n_in + n_ex + n_out:n_in + 2 * n_ex + n_out]
        own_scratch, sems = refs[n_in + 2 * n_ex + n_out:-2], refs[-2:]
        parts = (ex_in, ex_out, sems[0], sems[1])
        p, i = pl.program_id(0), pl.program_id(1)
        pl.when(jnp.logical_and(p == 0, i == 0))(lambda: carried.begin(*parts))
        if carried.relay is not None:
            pl.when(jnp.logical_and(p == groups - 1, i == max(nb - 2, 0)))(lambda: carried.relay(*parts))
        body(*own_in, *own_out, *own_scratch)
        pl.when(jnp.logical_and(p == groups - 1, i == nb - 1))(lambda: carried.finish(*parts))

    return pl.pallas_call(
        body_with_copies, name=name, grid=(groups, nb), in_specs=list(in_specs) + [HBM_SPEC] * n_ex,
        out_specs=list(out_specs) + [HBM_SPEC] * n_ex, out_shape=list(out_shape) + carried.out_shapes,
        scratch_shapes=list(scratch) + _dma_sems(carried.n_sems),
        compiler_params=_params("arbitrary", "arbitrary"),
    )(*args, *carried.inputs)


def _sb_fwd(name, pa, col0, carried=None):
    s = pa.shape[0]
    nb = s // TILE
    heads, lanes = SB_FWD_GROUP, SB_FWD_GROUP * HEAD_DIM
    cb = col0 // lanes
    kb = SB_WIDTH // lanes

    def body(q_ref, k_ref, v_ref, o_ref, lsig_s, lf_s):
        i = pl.program_id(1)
        r = lax.broadcasted_iota(jnp.int32, (TILE, TILE), 0)
        c = lax.broadcasted_iota(jnp.int32, (TILE, TILE), 1)
        strict = c < r
        u_after = _tri(TILE, lambda rr, cc: rr > cc)
        qs = [_scaled_q(q_ref, _head_slices(hh), -SCALE) for hh in range(heads)]

        def neg_z(j):
            kblk = k_ref[_rows(k_ref, j), :]
            return [_dot_nt(qs[hh], kblk[:, _head_slices(hh)]) for hh in range(heads)]

        def scores(nzs, slot, diag):
            for hh, nz in enumerate(nzs):
                lf = jnp.minimum(nz, 0.0) - jnp.log(1.0 + jnp.exp(_neg_abs(nz)))
                lsig = lf - nz
                if diag:
                    lf = jnp.where(strict, lf, 0.0)
                    lsig = jnp.where(strict, lsig, MASKED)
                lsig_s[slot, hh] = lsig
                lf_s[slot, hh] = lf.astype(BF16)

        def weigh(j, slot, state):
            vblk = v_ref[_rows(v_ref, j), :]
            new = []
            for hh in range(heads):
                carry, acc = state[hh]
                lfb = lf_s[slot, hh]
                sx = _dot(lfb, u_after)
                a = jnp.exp(lsig_s[slot, hh] + sx + carry)
                new.append((carry + sx[:, 0:1] + lfb[:, 0:1].astype(F32),
                            acc + _dot(a.astype(BF16), vblk[:, _head_slices(hh)])))
            return tuple(new)

        def step(t, state):
            state = weigh(i - t + 1, (t - 1) % 2, state)
            scores(neg_z(i - t), t % 2, False)
            return state

        zero = (jnp.zeros((TILE, 1), F32), jnp.zeros((TILE, HEAD_DIM), F32))
        scores(neg_z(i), 0, True)
        state = lax.fori_loop(1, i + 1, step, (zero,) * heads)
        state = weigh(0, i % 2, state)
        o_ref[...] = jnp.concatenate([st[1] for st in state], axis=1)

    outs = _pair_grid_call(
        name, body, nb,
        in_specs=[pl.BlockSpec((TILE, lanes), lambda p, i: (i, cb + p)),
                  pl.BlockSpec((s, lanes), lambda p, i: (0, cb + kb + p)),
                  pl.BlockSpec((s, lanes), lambda p, i: (0, cb + 2 * kb + p))],
        out_specs=[pl.BlockSpec((TILE, lanes), lambda p, i: (i, p))],
        out_shape=[jax.ShapeDtypeStruct((s, SB_WIDTH), F32)],
        scratch=[pltpu.VMEM((2, heads, TILE, TILE), F32), pltpu.VMEM((2, heads, TILE, TILE), BF16)],
        args=(pa, pa, pa), carried=carried, groups=kb)
    return outs[0], outs[1:]


def _sb_bwd(name, pa, col0, dout, dcol0, carried=None):
    s = pa.shape[0]
    nb = s // TILE
    cb = col0 // SB_LANES
    kb = SB_WIDTH // SB_LANES
    db = dcol0 // SB_LANES

    def body(q_ref, k_ref, v_ref, do_ref, dq_ref, dk_ref, dv_ref, dk_acc, dv_acc, dpan, span, gsum, lsig_s, lf_s):
        i = pl.program_id(1)

        @pl.when(i == 0)
        def _():
            dk_acc[...] = jnp.zeros_like(dk_acc)
            dv_acc[...] = jnp.zeros_like(dv_acc)

        r = lax.broadcasted_iota(jnp.int32, (TILE, TILE), 0)
        c = lax.broadcasted_iota(jnp.int32, (TILE, TILE), 1)
        strict = c < r
        u_after = _tri(TILE, lambda rr, cc: rr > cc)
        u_before = _tri(TILE, lambda rr, cc: rr < cc)
        qs = [_scaled_q(q_ref, _head_slices(hh), -SCALE) for hh in range(SB_GROUP)]
        dos = [do_ref[:, _head_slices(hh)].astype(BF16) for hh in range(SB_GROUP)]
        dots = [do_ref[:, _head_slices(hh)].T.astype(BF16) for hh in range(SB_GROUP)]
        qts = [q.astype(F32).T.astype(BF16) for q in qs]

        def scores(j, slot, diag):
            kblk = k_ref[_rows(k_ref, j), :]
            for hh in range(SB_GROUP):
                nz = _dot_nt(qs[hh], kblk[:, _head_slices(hh)])
                lf = jnp.minimum(nz, 0.0) - jnp.log(1.0 + jnp.exp(_neg_abs(nz)))
                lsig = lf - nz
                if diag:
                    lf = jnp.where(strict, lf, 0.0)
                    lsig = jnp.where(strict, lsig, MASKED)
                lsig_s[slot, hh] = lsig
                lf_s[slot, hh] = lf.astype(BF16)

        def grads(j, slot, carries):
            vblk = v_ref[_rows(v_ref, j), :]
            new = []
            for hh in range(SB_GROUP):
                lfb = lf_s[slot, hh]
                lsig = lsig_s[slot, hh]
                sx = _dot(lfb, u_after)
                a = jnp.exp(lsig + sx + carries[hh])
                g = a * _dot_nt(dos[hh], vblk[:, _head_slices(hh)])
                sig = jnp.exp(lsig)
                inside = _dot(g.astype(BF16), u_before)
                dpan[hh, j] = sig * (inside + g) - g
                span[hh, j] = sig
                gsum[hh, j] = inside[:, TILE - 1:TILE] + g[:, TILE - 1:TILE]
                dv_acc[hh, j] += _dot(dots[hh], a.astype(BF16))
                new.append(carries[hh] + sx[:, 0:1] + lfb[:, 0:1].astype(F32))
            return tuple(new)

        def step1(t, carries):
            carries = grads(i - t + 1, (t - 1) % 2, carries)
            scores(i - t, t % 2, False)
            return carries

        zero1 = jnp.zeros((TILE, 1), F32)
        scores(i, 0, True)
        carries = lax.fori_loop(1, i + 1, step1, (zero1,) * SB_GROUP)
        grads(0, i % 2, carries)

        def pass2(j, state):
            kblk = k_ref[_rows(k_ref, j), :]
            new = []
            for hh in range(SB_GROUP):
                before, ndq = state[hh]
                ndzb = (dpan[hh, j] + span[hh, j] * before).astype(BF16)
                dk_acc[hh, j] += _dot(qts[hh], ndzb)
                new.append((before + gsum[hh, j], ndq + _dot(ndzb, kblk[:, _head_slices(hh)])))
            return tuple(new)

        zero2 = (zero1, jnp.zeros((TILE, HEAD_DIM), F32))
        state = lax.fori_loop(0, i + 1, pass2, (zero2,) * SB_GROUP)
        dq_ref[...] = jnp.concatenate([st[1] * -SCALE for st in state], axis=1).astype(BF16)

        @pl.when(i == nb - 1)
        def _():
            for acc, ref in ((dk_acc, dk_ref), (dv_acc, dv_ref)):
                for j in range(nb):
                    ref[j * TILE:(j + 1) * TILE, :] = jnp.concatenate(
                        [acc[hh, j].T for hh in range(SB_GROUP)], axis=1).astype(BF16)

    qspec = pl.BlockSpec((TILE, SB_LANES), lambda p, i: (i, p))
    kvspec = pl.BlockSpec((s, SB_LANES), lambda p, i: (0, p))
    out = jax.ShapeDtypeStruct((s, SB_WIDTH), BF16)
    outs = _pair_grid_call(
        name, body, nb,
        in_specs=[pl.BlockSpec((TILE, SB_LANES), lambda p, i: (i, cb + p)),
                  pl.BlockSpec((s, SB_LANES), lambda p, i: (0, cb + kb + p)),
                  pl.BlockSpec((s, SB_LANES), lambda p, i: (0, cb + 2 * kb + p)),
                  pl.BlockSpec((TILE, SB_LANES), lambda p, i: (i, db + p))],
        out_specs=[qspec, kvspec, kvspec], out_shape=[out, out, out],
        scratch=[pltpu.VMEM((SB_GROUP, nb, HEAD_DIM, TILE), F32), pltpu.VMEM((SB_GROUP, nb, HEAD_DIM, TILE), F32),
                 pltpu.VMEM((SB_GROUP, nb, TILE, TILE), F32), pltpu.VMEM((SB_GROUP, nb, TILE, TILE), F32),
                 pltpu.VMEM((SB_GROUP, nb, TILE, 1), F32),
                 pltpu.VMEM((2, SB_GROUP, TILE, TILE), F32), pltpu.VMEM((2, SB_GROUP, TILE, TILE), BF16)],
        args=(pa, pa, pa, dout), carried=carried, groups=kb)
    return outs[:3], outs[3:]


def _fox_scores(q, kj, cq, crj, causal, diag):
    sc = _dot_nt(q, kj) + (cq - crj)
    if diag:
        sc = jnp.where(causal, sc, NEG_INF)
    return sc


def _fox_fwd(name, pa, col0, ccol4, crow4, carried=None):
    s = pa.shape[0]
    nb = s // TILE
    heads, lanes = FOX_FWD_GROUP, FOX_FWD_GROUP * HEAD_DIM
    cb = col0 // lanes
    kb = FOX_WIDTH // lanes

    def body(q_ref, k_ref, v_ref, cc_ref, cr_ref, o_ref, lse_ref, sc_s):
        i = pl.program_id(1)
        head0 = pl.program_id(0) * heads
        r = lax.broadcasted_iota(jnp.int32, (TILE, TILE), 0)
        c = lax.broadcasted_iota(jnp.int32, (TILE, TILE), 1)
        causal = c <= r
        qs = [_scaled_q(q_ref, _head_slices(hh)) for hh in range(heads)]
        cqs = [cc_ref[:, HEAD_DIM * hh:HEAD_DIM * hh + 1] for hh in range(heads)]

        def logits(j, slot, diag):
            kblk = k_ref[_rows(k_ref, j), :]
            tops = []
            for hh in range(heads):
                sc = _fox_scores(qs[hh], kblk[:, _head_slices(hh)], cqs[hh], cr_ref[j, pl.ds(head0 + hh, 1), :], causal, diag)
                sc_s[slot, hh] = sc
                tops.append(jnp.max(sc, axis=1, keepdims=True))
            return tuple(tops)

        def update(j, slot, tops, state):
            vblk = v_ref[_rows(v_ref, j), :]
            new = []
            for hh in range(heads):
                m, l, acc = state[hh]
                m2 = jnp.maximum(m, tops[hh])
                alpha = jnp.exp(m - m2)
                p = jnp.exp(sc_s[slot, hh] - m2)
                new.append((m2, l * alpha + jnp.sum(p, axis=1, keepdims=True),
                            acc * alpha + _dot(p.astype(BF16), vblk[:, _head_slices(hh)])))
            return tuple(new)

        def step(t, both):
            tops, state = both
            state = update(i - t + 1, (t - 1) % 2, tops, state)
            return logits(i - t, t % 2, False), state

        zero = (jnp.full((TILE, 1), NEG_INF, F32), jnp.zeros((TILE, 1), F32), jnp.zeros((TILE, HEAD_DIM), F32))
        tops, state = lax.fori_loop(1, i + 1, step, (logits(i, 0, True), (zero,) * heads))
        state = update(0, i % 2, tops, state)
        o_ref[...] = jnp.concatenate([st[2] / st[1] for st in state], axis=1)
        lse_ref[...] = jnp.concatenate(
            [jnp.broadcast_to(st[0] + jnp.log(st[1]), (TILE, HEAD_DIM)) for st in state], axis=1)

    outs = _pair_grid_call(
        name, body, nb,
        in_specs=[pl.BlockSpec((TILE, lanes), lambda p, i: (i, cb + p)),
                  pl.BlockSpec((s, lanes), lambda p, i: (0, cb + kb + p)),
                  pl.BlockSpec((s, lanes), lambda p, i: (0, cb + 2 * kb + p)),
                  pl.BlockSpec((TILE, lanes), lambda p, i: (i, p)),
                  pl.BlockSpec((nb, 8, TILE), lambda p, i: (0, 0, 0))],
        out_specs=[pl.BlockSpec((TILE, lanes), lambda p, i: (i, p)), pl.BlockSpec((TILE, lanes), lambda p, i: (i, p))],
        out_shape=[jax.ShapeDtypeStruct((s, FOX_WIDTH), F32), jax.ShapeDtypeStruct((s, FOX_WIDTH), F32)],
        scratch=[pltpu.VMEM((2, heads, TILE, TILE), F32)],
        args=(pa, pa, pa, ccol4, crow4), carried=carried, groups=kb)
    return outs[0], outs[1], outs[2:]


def _fox_bwd(name, pa, col0, ccol4, crow4, out, lse, dout, dcol0, carried=None):
    s = pa.shape[0]
    nb = s // TILE
    cb = col0 // FOX_LANES
    kb = FOX_WIDTH // FOX_LANES
    db = dcol0 // FOX_LANES

    def body(q_ref, k_ref, v_ref, cc_ref, cr_ref, o_ref, lse_ref, do_ref,
             dq_ref, dk_ref, dv_ref, cs_ref, dk_acc, dv_acc, p_s, ds_s):
        i = pl.program_id(1)
        head0 = pl.program_id(0) * FOX_GROUP

        @pl.when(i == 0)
        def _():
            dk_acc[...] = jnp.zeros_like(dk_acc)
            dv_acc[...] = jnp.zeros_like(dv_acc)

        @pl.when(jnp.logical_and(i == 0, head0 == 0))
        def _():
            cs_ref[...] = jnp.zeros_like(cs_ref)

        r = lax.broadcasted_iota(jnp.int32, (TILE, TILE), 0)
        c = lax.broadcasted_iota(jnp.int32, (TILE, TILE), 1)
        causal = c <= r
        qs = [_scaled_q(q_ref, _head_slices(hh)) for hh in range(FOX_GROUP)]
        cqs = [cc_ref[:, HEAD_DIM * hh:HEAD_DIM * hh + 1] for hh in range(FOX_GROUP)]
        lses = [lse_ref[:, HEAD_DIM * hh:HEAD_DIM * hh + 1] for hh in range(FOX_GROUP)]
        dofs = [do_ref[:, _head_slices(hh)] for hh in range(FOX_GROUP)]
        dos = [d_.astype(BF16) for d_ in dofs]
        dots = [d_.T.astype(BF16) for d_ in dofs]
        qts = [q.astype(F32).T.astype(BF16) for q in qs]
        deltas = [jnp.sum(dofs[hh] * o_ref[:, _head_slices(hh)], axis=1, keepdims=True) for hh in range(FOX_GROUP)]

        def probs(j, slot, rowsums, diag):
            kblk = k_ref[_rows(k_ref, j), :]
            vblk = v_ref[_rows(v_ref, j), :]
            new = []
            for hh in range(FOX_GROUP):
                sl = _head_slices(hh)
                sc = _fox_scores(qs[hh], kblk[:, sl], cqs[hh], cr_ref[j, pl.ds(head0 + hh, 1), :], causal, diag)
                p = jnp.exp(sc - lses[hh])
                ds = p * (_dot_nt(dos[hh], vblk[:, sl]) - deltas[hh])
                p_s[slot, hh] = p.astype(BF16)
                ds_s[slot, hh] = ds.astype(BF16)
                cs_ref[j, pl.ds(head0 + hh, 1), :] += jnp.sum(ds, axis=0, keepdims=True)
                new.append(rowsums[hh] + jnp.sum(ds, axis=1, keepdims=True))
            return tuple(new)

        def accumulate(j, slot, dqs):
            kblk = k_ref[_rows(k_ref, j), :]
            new = []
            for hh in range(FOX_GROUP):
                dsb = ds_s[slot, hh]
                dv_acc[hh, j] += _dot(dots[hh], p_s[slot, hh])
                dk_acc[hh, j] += _dot(qts[hh], dsb)
                new.append(dqs[hh] + _dot(dsb, kblk[:, _head_slices(hh)]))
            return tuple(new)

        def step(t, both):
            rowsums, dqs = both
            dqs = accumulate(i - t + 1, (t - 1) % 2, dqs)
            return probs(i - t, t % 2, rowsums, False), dqs

        zero1 = jnp.zeros((TILE, 1), F32)
        zero64 = jnp.zeros((TILE, HEAD_DIM), F32)
        rowsums, dqs = lax.fori_loop(1, i + 1, step,
                                     (probs(i, 0, (zero1,) * FOX_GROUP, True), (zero64,) * FOX_GROUP))
        dqs = accumulate(0, i % 2, dqs)
        for hh in range(FOX_GROUP):
            cs_ref[i, pl.ds(head0 + hh, 1), :] -= jnp.broadcast_to(rowsums[hh], (TILE, LANES)).T[0:1, :]
        dq_ref[...] = jnp.concatenate([dq * SCALE for dq in dqs], axis=1).astype(BF16)

        @pl.when(i == nb - 1)
        def _():
            for acc, ref in ((dk_acc, dk_ref), (dv_acc, dv_ref)):
                for j in range(nb):
                    ref[j * TILE:(j + 1) * TILE, :] = jnp.concatenate(
                        [acc[hh, j].T for hh in range(FOX_GROUP)], axis=1).astype(BF16)

    qspec = pl.BlockSpec((TILE, FOX_LANES), lambda p, i: (i, p))
    kvspec = pl.BlockSpec((s, FOX_LANES), lambda p, i: (0, p))
    o3 = jax.ShapeDtypeStruct((s, FOX_WIDTH), BF16)
    outs = _pair_grid_call(
        name, body, nb,
        in_specs=[pl.BlockSpec((TILE, FOX_LANES), lambda p, i: (i, cb + p)),
                  pl.BlockSpec((s, FOX_LANES), lambda p, i: (0, cb + kb + p)),
                  pl.BlockSpec((s, FOX_LANES), lambda p, i: (0, cb + 2 * kb + p)),
                  pl.BlockSpec((TILE, FOX_LANES), lambda p, i: (i, p)),
                  pl.BlockSpec((nb, 8, TILE), lambda p, i: (0, 0, 0)),
                  qspec,
                  pl.BlockSpec((TILE, FOX_LANES), lambda p, i: (i, p)),
                  pl.BlockSpec((TILE, FOX_LANES), lambda p, i: (i, db + p))],
        out_specs=[qspec, kvspec, kvspec, pl.BlockSpec((nb, 8, TILE), lambda p, i: (0, 0, 0))],
        out_shape=[o3, o3, o3, jax.ShapeDtypeStruct((nb, 8, TILE), F32)],
        scratch=[pltpu.VMEM((FOX_GROUP, nb, HEAD_DIM, TILE), F32), pltpu.VMEM((FOX_GROUP, nb, HEAD_DIM, TILE), F32),
                 pltpu.VMEM((2, FOX_GROUP, TILE, TILE), BF16), pltpu.VMEM((2, FOX_GROUP, TILE, TILE), BF16)],
        args=(pa, pa, pa, ccol4, crow4, out, lse, dout), carried=carried, groups=kb)
    return outs[:4], outs[4:]


def _mem_fwd(name, pa, mkv):
    s = pa.shape[0]
    ml = mkv.shape[0]
    nb = s // TILE
    cb = QKV_WIDTH // MEM_WIDTH

    def body(q_ref, k_ref, v_ref, o_ref, lse_ref):
        outs, lses = [], []
        for hh in range(MEM_HEADS):
            sl = _head_slices(hh)
            sc = _dot_nt(_scaled_q(q_ref, sl), k_ref[:, sl])
            m = jnp.max(sc, axis=1, keepdims=True)
            p = jnp.exp(sc - m)
            l = jnp.sum(p, axis=1, keepdims=True)
            outs.append(_dot(p.astype(BF16), v_ref[:, sl]) / l)
            lses.append(jnp.broadcast_to(m + jnp.log(l), (TILE, HEAD_DIM)))
        o_ref[...] = jnp.concatenate(outs, axis=1)
        lse_ref[...] = jnp.concatenate(lses, axis=1)

    row = pl.BlockSpec((TILE, MEM_WIDTH), lambda i: (i, 0))
    return pl.pallas_call(
        body, name=name, grid=(nb,),
        in_specs=[pl.BlockSpec((TILE, MEM_WIDTH), lambda i: (i, cb)),
                  pl.BlockSpec((ml, MEM_WIDTH), lambda i: (0, 0)),
                  pl.BlockSpec((ml, MEM_WIDTH), lambda i: (0, 1))],
        out_specs=[row, row],
        out_shape=[jax.ShapeDtypeStruct((s, MEM_WIDTH), F32), jax.ShapeDtypeStruct((s, MEM_WIDTH), F32)],
        compiler_params=_params("parallel"),
    )(pa, mkv, mkv)


def _mem_bwd(name, pa, mkv, out, lse, dout, dcol0):
    s = pa.shape[0]
    ml = mkv.shape[0]
    nb = s // TILE
    cb = QKV_WIDTH // MEM_WIDTH
    db = dcol0 // MEM_WIDTH

    def body(q_ref, k_ref, v_ref, o_ref, lse_ref, do_ref, dq_ref, dkv_ref, dk_acc, dv_acc):
        i = pl.program_id(0)

        @pl.when(i == 0)
        def _():
            dk_acc[...] = jnp.zeros_like(dk_acc)
            dv_acc[...] = jnp.zeros_like(dv_acc)

        dqs = []
        for hh in range(MEM_HEADS):
            sl = _head_slices(hh)
            q = _scaled_q(q_ref, sl)
            kh = k_ref[:, sl]
            dof = do_ref[:, sl]
            do = dof.astype(BF16)
            delta = jnp.sum(dof * o_ref[:, sl], axis=1, keepdims=True)
            p = jnp.exp(_dot_nt(q, kh) - lse_ref[:, HEAD_DIM * hh:HEAD_DIM * hh + 1])
            ds = (p * (_dot_nt(do, v_ref[:, sl]) - delta)).astype(BF16)
            dv_acc[hh] += _dot(dof.T.astype(BF16), p.astype(BF16))
            dk_acc[hh] += _dot(q.astype(F32).T.astype(BF16), ds)
            dqs.append(_dot(ds, kh) * SCALE)
        dq_ref[...] = jnp.concatenate(dqs, axis=1).astype(BF16)

        @pl.when(i == nb - 1)
        def _():
            dkv_ref[...] = jnp.concatenate([dk_acc[hh].T for hh in range(MEM_HEADS)]
                                           + [dv_acc[hh].T for hh in range(MEM_HEADS)], axis=1).astype(BF16)

    row = pl.BlockSpec((TILE, MEM_WIDTH), lambda i: (i, 0))
    return pl.pallas_call(
        body, name=name, grid=(nb,),
        in_specs=[pl.BlockSpec((TILE, MEM_WIDTH), lambda i: (i, cb)),
                  pl.BlockSpec((ml, MEM_WIDTH), lambda i: (0, 0)),
                  pl.BlockSpec((ml, MEM_WIDTH), lambda i: (0, 1)),
                  row, row,
                  pl.BlockSpec((TILE, MEM_WIDTH), lambda i: (i, db))],
        out_specs=[row, pl.BlockSpec((ml, 2 * MEM_WIDTH), lambda i: (0, 0))],
        out_shape=[jax.ShapeDtypeStruct((s, MEM_WIDTH), BF16), jax.ShapeDtypeStruct((ml, 2 * MEM_WIDTH), BF16)],
        scratch_shapes=[pltpu.VMEM((MEM_HEADS, HEAD_DIM, ml), F32), pltpu.VMEM((MEM_HEADS, HEAD_DIM, ml), F32)],
        compiler_params=_params("arbitrary"),
    )(pa, mkv, mkv, out, lse, dout)


def _head_maps():
    col = jnp.arange(MIX_WIDTH)[:, None] // HEAD_DIM
    g = (col == jnp.arange(LANES)[None, :]).astype(BF16)
    return g, g.T


def _normed_heads(osb_ref, ofx_ref, om_ref, g_ref, gt_ref):
    y = jnp.concatenate([osb_ref[...], ofx_ref[...], om_ref[...]], axis=1)
    msq = _sum_l2(y * y, g_ref[...]) * (1.0 / HEAD_DIM)
    rf = _sum_l3(lax.rsqrt(msq + EPS), gt_ref[...])
    return y * rf, rf


def _out_fwd(name, o_sb, o_fx, o_m, pb, ow, x, w_out, ts):
    s, d = x.shape
    g, gt = _head_maps()

    def body(osb_ref, ofx_ref, om_ref, gate_ref, ow_ref, x_ref, w_ref, g_ref, gt_ref, xo_ref, y2_ref):
        yh, _ = _normed_heads(osb_ref, ofx_ref, om_ref, g_ref, gt_ref)
        gate = gate_ref[...]
        y2 = (yh * ow_ref[...] * (gate * jax.nn.sigmoid(gate))).astype(BF16)
        y2_ref[...] = y2
        xo_ref[...] = x_ref[...] + _dot(y2, w_ref[...])

    return pl.pallas_call(
        body, name=name, grid=(s // ts,),
        in_specs=[_row_spec(ts, SB_WIDTH), _row_spec(ts, FOX_WIDTH), _row_spec(ts, MEM_WIDTH),
                  _row_spec(ts, MIX_WIDTH), _const_spec((1, MIX_WIDTH)), _row_spec(ts, d),
                  _const_spec((MIX_WIDTH, d)),
                  _const_spec((MIX_WIDTH, LANES)), _const_spec((LANES, MIX_WIDTH))],
        out_specs=[_row_spec(ts, d), _row_spec(ts, MIX_WIDTH)],
        out_shape=[jax.ShapeDtypeStruct((s, d), F32), jax.ShapeDtypeStruct((s, MIX_WIDTH), BF16)],
        compiler_params=_params("parallel"),
    )(o_sb, o_fx, o_m, pb, ow, x, w_out, g, gt)


def _row_spec(ts, w):
    return pl.BlockSpec((ts, w), lambda i: (i, 0))


def _const_spec(shape):
    return pl.BlockSpec(shape, lambda i: (0,) * len(shape))


def _out_bwd(name, dxb, o_sb, o_fx, o_m, pb, ow, w_out, ts):
    s, d = dxb.shape
    g, gt = _head_maps()

    def body(dx_ref, osb_ref, ofx_ref, om_ref, gate_ref, ow_ref, w_ref, g_ref, gt_ref, dy_ref, dgate_ref, dow_ref):
        @pl.when(pl.program_id(0) == 0)
        def _():
            dow_ref[...] = jnp.zeros_like(dow_ref)

        dy2 = _dot_nt(dx_ref[...], w_ref[...])
        yh, rf = _normed_heads(osb_ref, ofx_ref, om_ref, g_ref, gt_ref)
        gate = gate_ref[...]
        sig = jax.nn.sigmoid(gate)
        ow_v = ow_ref[...]
        dgate_ref[...] = (dy2 * (yh * ow_v) * (sig * (1.0 + gate * (1.0 - sig)))).astype(BF16)
        dn = dy2 * (gate * sig)
        dow_ref[...] += jnp.sum(dn * yh, axis=0, keepdims=True)
        dyh = dn * ow_v
        t = _sum_l2(dyh * yh, g_ref[...]) * (1.0 / HEAD_DIM)
        dy_ref[...] = rf * (dyh - yh * _sum_l3(t, gt_ref[...]))

    return pl.pallas_call(
        body, name=name, grid=(s // ts,),
        in_specs=[_row_spec(ts, d), _row_spec(ts, SB_WIDTH), _row_spec(ts, FOX_WIDTH), _row_spec(ts, MEM_WIDTH),
                  _row_spec(ts, MIX_WIDTH), _const_spec((1, MIX_WIDTH)),
                  _const_spec((MIX_WIDTH, d)),
                  _const_spec((MIX_WIDTH, LANES)), _const_spec((LANES, MIX_WIDTH))],
        out_specs=[_row_spec(ts, MIX_WIDTH), _row_spec(ts, MIX_WIDTH), _const_spec((1, MIX_WIDTH))],
        out_shape=[jax.ShapeDtypeStruct((s, MIX_WIDTH), F32), jax.ShapeDtypeStruct((s, PB), BF16),
                   jax.ShapeDtypeStruct((1, MIX_WIDTH), F32)],
        compiler_params=_params("arbitrary"),
    )(dxb, o_sb, o_fx, o_m, pb, ow, w_out, g, gt)


def _adamw(name, w, g, m, v, tr):
    def body(w_ref, g_ref, m_ref, v_ref, d_ref, m2_ref, v2_ref):
        gv = g_ref[...]
        m2 = ADAM_B1 * m_ref[...] + (1.0 - ADAM_B1) * gv
        v2 = ADAM_B2 * v_ref[...] + (1.0 - ADAM_B2) * (gv * gv)
        m_hat = m2 / (1.0 - ADAM_B1 ** ADAM_STEP)
        v_hat = v2 / (1.0 - ADAM_B2 ** ADAM_STEP)
        d_ref[...] = -ADAM_LR * (m_hat / (jnp.sqrt(v_hat) + ADAM_EPS) + ADAM_WD * w_ref[...])
        m2_ref[...] = m2
        v2_ref[...] = v2

    rest = w.shape[1:]
    spec = pl.BlockSpec((tr,) + rest, lambda i: (i,) + (0,) * len(rest))
    shp = jax.ShapeDtypeStruct(w.shape, F32)
    return pl.pallas_call(
        body, name=name, grid=(w.shape[0] // tr,), in_specs=[spec] * 4, out_specs=[spec] * 3, out_shape=[shp] * 3,
        compiler_params=_params("parallel"),
    )(w, g, m, v)


def _adamw_sharded(name, w, m, v, g_own, g_other, cvec, tr):
    depth, rows, cols = w.shape
    nt = rows // 2 // tr

    def body(c_ref, w_ref, m_ref, v_ref, *rest):
        g_refs, (g_ref, d_ref, m2_ref, v2_ref) = rest[:2 * depth], rest[2 * depth:]
        layer, mine = pl.program_id(0), pl.program_id(1) == c_ref[0]
        gv = None
        for lt in range(depth):
            cand = jnp.where(mine, g_refs[lt][...], g_refs[depth + lt][...])
            gv = cand if gv is None else jnp.where(layer == lt, cand, gv)
        m2 = ADAM_B1 * m_ref[...] + (1.0 - ADAM_B1) * gv
        v2 = ADAM_B2 * v_ref[...] + (1.0 - ADAM_B2) * (gv * gv)
        m_hat = m2 / (1.0 - ADAM_B1 ** ADAM_STEP)
        v_hat = v2 / (1.0 - ADAM_B2 ** ADAM_STEP)
        g_ref[...] = gv
        d_ref[...] = -ADAM_LR * (m_hat / (jnp.sqrt(v_hat) + ADAM_EPS) + ADAM_WD * w_ref[...])
        m2_ref[...] = m2
        v2_ref[...] = v2

    def g_map(lt, own):
        def index(l, hf, i, c_ref):
            use = jnp.logical_and(l == lt, (hf == c_ref[0]) == own)
            return jnp.where(use, i, 0), 0
        return index

    full = pl.BlockSpec((None, tr, cols), lambda l, hf, i, c_ref: (l, hf * nt + i, 0))
    g_specs = [pl.BlockSpec((tr, cols), g_map(lt, own)) for own in (True, False) for lt in range(depth)]
    shp = jax.ShapeDtypeStruct((depth, rows, cols), F32)
    return pl.pallas_call(
        body, name=name,
        grid_spec=pltpu.PrefetchScalarGridSpec(
            num_scalar_prefetch=1, grid=(depth, 2, nt), in_specs=[full] * 3 + g_specs, out_specs=[full] * 4),
        out_shape=[shp] * 4,
        compiler_params=_params("arbitrary", "arbitrary", "arbitrary"),
    )(cvec, w, m, v, *g_own, *g_other)


HBM_SPEC = pl.BlockSpec(memory_space=pltpu.HBM)


def _place():
    x, y, c = lax.axis_index("x"), lax.axis_index("y"), lax.axis_index("c")
    chips = [(1 - x, y), (x, 1 - y), (1 - x, 1 - y)]
    return x, y, c, chips


def _remote(src, dst, send_sems, recv_sems, k, to):
    return pltpu.make_async_remote_copy(src_ref=src, dst_ref=dst, send_sem=send_sems.at[k], recv_sem=recv_sems.at[k],
                                        device_id=to, device_id_type=MESH)


def _half_rows(n_rows, cc):
    rh = n_rows // 2
    return pl.ds(pl.multiple_of(cc * rh, 16), rh)


def _dma_sems(n):
    return [pltpu.SemaphoreType.DMA((n,)), pltpu.SemaphoreType.DMA((n,))]


class _Exchange:
    def __init__(self, inputs, out_shapes, n_sems, begin, relay, finish):
        self.inputs, self.out_shapes, self.n_sems = list(inputs), list(out_shapes), n_sems
        self.begin, self.relay, self.finish = begin, relay, finish

    @property
    def n(self):
        return len(self.inputs)

    def split(self, refs):
        return refs[:self.n], refs[self.n:2 * self.n], refs[2 * self.n], refs[2 * self.n + 1]


def _run_exchange(name, ex):
    def body(*refs):
        parts = ex.split(refs)
        for phase in (ex.begin, ex.relay, ex.finish):
            if phase is not None:
                phase(*parts)

    return pl.pallas_call(
        body, name=name, in_specs=[HBM_SPEC] * ex.n, out_specs=[HBM_SPEC] * ex.n, out_shape=ex.out_shapes,
        scratch_shapes=_dma_sems(ex.n_sems),
    )(*ex.inputs)


def _gather_exchange(shards):
    def ici(in_refs, out_refs, send_sems, recv_sems):
        x, y, c, chips = _place()
        return [_remote(in_ref.at[_half_rows(in_ref.shape[0], c)], out_ref.at[2 * x + y, _half_rows(in_ref.shape[0], c)],
                        send_sems, recv_sems, 6 * a + j, (cx, cy, c))
                for a, (in_ref, out_ref) in enumerate(zip(in_refs, out_refs)) for j, (cx, cy) in enumerate(chips)]

    def d2d(out_refs, send_sems, recv_sems, half_of):
        x, y, c, chips = _place()
        cps = []
        for a, out_ref in enumerate(out_refs):
            for j, (cx, cy) in enumerate(chips):
                piece = out_ref.at[2 * cx + cy, _half_rows(out_ref.shape[1], half_of(c))]
                cps.append(_remote(piece, piece, send_sems, recv_sems, 6 * a + 3 + j, (x, y, 1 - c)))
        return cps

    def begin(in_refs, out_refs, send_sems, recv_sems):
        for cp in ici(in_refs, out_refs, send_sems, recv_sems):
            cp.start()

    def relay(in_refs, out_refs, send_sems, recv_sems):
        x, y, c, chips = _place()
        for a, out_ref in enumerate(out_refs):
            for j, (cx, cy) in enumerate(chips):
                landed = out_ref.at[2 * cx + cy, _half_rows(out_ref.shape[1], c)]
                _remote(landed, landed, send_sems, recv_sems, 6 * a + j, (cx, cy, c)).wait_recv()
        for cp in d2d(out_refs, send_sems, recv_sems, lambda c_: c_):
            cp.start()

    def finish(in_refs, out_refs, send_sems, recv_sems):
        for cp in d2d(out_refs, send_sems, recv_sems, lambda c_: 1 - c_):
            cp.wait_recv()
        for cp in ici(in_refs, out_refs, send_sems, recv_sems) + d2d(out_refs, send_sems, recv_sems, lambda c_: c_):
            cp.wait_send()

    shapes = [jax.ShapeDtypeStruct((N_CHIPS,) + s_.shape, s_.dtype) for s_ in shards]
    return _Exchange(shards, shapes, 6 * len(shards), begin, relay, finish)


def _swap_exchange(g4s):
    def copies(in_refs, out_refs, send_sems, recv_sems):
        x, y, c, _ = _place()
        return [_remote(in_ref.at[:, _half_rows(in_ref.shape[1], 1 - c), :], out_ref, send_sems, recv_sems, a, (x, y, 1 - c))
                for a, (in_ref, out_ref) in enumerate(zip(in_refs, out_refs))]

    def begin(*parts):
        for cp in copies(*parts):
            cp.start()

    def finish(*parts):
        for cp in copies(*parts):
            cp.wait()

    shapes = [jax.ShapeDtypeStruct((g.shape[0], g.shape[1] // 2, g.shape[2]), g.dtype) for g in g4s]
    return _Exchange(g4s, shapes, len(g4s), begin, None, finish)


def _add_half(name, g4, r1, cvec, tr):
    n, r, w = g4.shape
    rh = r // 2
    nblk = rh // tr

    def body(c_ref, a_ref, b_ref, o_ref):
        o_ref[...] = (a_ref[...].astype(F32) + b_ref[...].astype(F32)).astype(BF16)

    return pl.pallas_call(
        body, name=name,
        grid_spec=pltpu.PrefetchScalarGridSpec(
            num_scalar_prefetch=1, grid=(n, nblk),
            in_specs=[pl.BlockSpec((None, tr, w), lambda k, i, c_ref: (k, c_ref[0] * nblk + i, 0)),
                      pl.BlockSpec((None, tr, w), lambda k, i, c_ref: (k, i, 0))],
            out_specs=pl.BlockSpec((None, tr, w), lambda k, i, c_ref: (k, i, 0))),
        out_shape=jax.ShapeDtypeStruct((n, rh, w), BF16),
        compiler_params=_params("parallel", "parallel"),
    )(cvec, g4, r1)


def _scatter_exchange(h4s):
    def sends(in_refs, out_refs, send_sems, recv_sems):
        x, y, c, chips = _place()
        return [_remote(in_ref.at[2 * cx + cy], out_ref.at[j], send_sems, recv_sems, 3 * a + j, (cx, cy, c))
                for a, (in_ref, out_ref) in enumerate(zip(in_refs, out_refs)) for j, (cx, cy) in enumerate(chips)]

    def begin(*parts):
        for cp in sends(*parts):
            cp.start()

    def finish(in_refs, out_refs, send_sems, recv_sems):
        x, y, c, chips = _place()
        for a, out_ref in enumerate(out_refs):
            for j, (cx, cy) in enumerate(chips):
                got = out_ref.at[j]
                _remote(got, got, send_sems, recv_sems, 3 * a + j, (cx, cy, c)).wait_recv()
        for cp in sends(in_refs, out_refs, send_sems, recv_sems):
            cp.wait_send()

    shapes = [jax.ShapeDtypeStruct((3,) + h.shape[1:], h.dtype) for h in h4s]
    return _Exchange(h4s, shapes, 3 * len(h4s), begin, None, finish)


def _sum_chips(name, h4, r3, mvec, tr):
    _, rh, w = h4.shape

    def body(m_ref, a_ref, b_ref, c_ref, d_ref, o_ref):
        o_ref[...] = ((a_ref[...].astype(F32) + b_ref[...].astype(F32)) + c_ref[...].astype(F32)) + d_ref[...].astype(F32)

    specs = [pl.BlockSpec((None, tr, w), lambda i, m_ref: (m_ref[0], i, 0))]
    specs += [pl.BlockSpec((None, tr, w), functools.partial(lambda k, i, m_ref: (k, i, 0), k)) for k in range(3)]
    return pl.pallas_call(
        body, name=name,
        grid_spec=pltpu.PrefetchScalarGridSpec(
            num_scalar_prefetch=1, grid=(rh // tr,), in_specs=specs,
            out_specs=pl.BlockSpec((tr, w), lambda i, m_ref: (i, 0))),
        out_shape=jax.ShapeDtypeStruct((rh, w), F32),
        compiler_params=_params("parallel"),
    )(mvec, h4, r3, r3, r3)


def _swap_reduced_exchange(ghs):
    def copies(in_refs, out_refs, send_sems, recv_sems):
        x, y, c, _ = _place()
        return [_remote(in_ref, out_ref, send_sems, recv_sems, a, (x, y, 1 - c))
                for a, (in_ref, out_ref) in enumerate(zip(in_refs, out_refs))]

    def begin(*parts):
        for cp in copies(*parts):
            cp.start()

    def finish(*parts):
        for cp in copies(*parts):
            cp.wait()

    return _Exchange(ghs, [jax.ShapeDtypeStruct(g.shape, g.dtype) for g in ghs], len(ghs), begin, None, finish)


class _SemaphoresFrom:
    def __init__(self, sems, first):
        self.sems, self.first = sems, first

    @property
    def at(self):
        return self

    def __getitem__(self, k):
        return self.sems.at[self.first + k]


def _both(a, b):
    def phase(fa, fb):
        if fa is None and fb is None:
            return None

        def run(in_refs, out_refs, send_sems, recv_sems):
            if fa is not None:
                fa(in_refs[:a.n], out_refs[:a.n], send_sems, recv_sems)
            if fb is not None:
                fb(in_refs[a.n:], out_refs[a.n:], _SemaphoresFrom(send_sems, a.n_sems), _SemaphoresFrom(recv_sems, a.n_sems))

        return run

    return _Exchange(a.inputs + b.inputs, a.out_shapes + b.out_shapes, a.n_sems + b.n_sems,
                     phase(a.begin, b.begin), phase(a.relay, b.relay), phase(a.finish, b.finish))


def _small_update(name, partials, weights, moments1, moments2):
    n = len(partials)
    width = max(p.shape[1] for p in partials)
    starts, at = [], 0
    for p in partials:
        starts.append(at)
        at += p.shape[0]
    rows = -(-at // 8) * 8
    has_w = [w is not None for w in weights]
    n_w = sum(has_w)

    def body(*refs):
        p_refs = refs[:n]
        w_refs, m_refs, v_refs = refs[n:n + n_w], refs[n + n_w:n + 2 * n_w], refs[n + 2 * n_w:n + 3 * n_w]
        outs = refs[n + 3 * n_w:-4]
        g_refs, upd_refs = outs[:n], outs[n:]
        vec, buf, send_sems, recv_sems = refs[-4:]
        x, y, c, _ = _place()
        me = 4 * x + 2 * y + c
        vec[...] = jnp.zeros_like(vec)
        for p_ref, r0 in zip(p_refs, starts):
            vec[r0:r0 + p_ref.shape[0], 0:p_ref.shape[1]] = p_ref[...]
        buf[me] = vec[...]
        flips = [(fx, fy, fc) for fx in (0, 1) for fy in (0, 1) for fc in (0, 1)][1:]
        peers = [(x + fx - 2 * x * fx, y + fy - 2 * y * fy, c + fc - 2 * c * fc) for fx, fy, fc in flips]
        sends = [_remote(vec, buf.at[me], send_sems, recv_sems, k, peer) for k, peer in enumerate(peers)]
        for cp in sends:
            cp.start()
        for k, (px, py, pc) in enumerate(peers):
            got = buf.at[4 * px + 2 * py + pc]
            _remote(got, got, send_sems, recv_sems, k, (px, py, pc)).wait_recv()
        for cp in sends:
            cp.wait_send()
        total = buf[0]
        for dev in range(1, N_DEV):
            total = total + buf[dev]
        k = 0
        for a in range(n):
            r, w = g_refs[a].shape
            g = total[starts[a]:starts[a] + r, 0:w]
            g_refs[a][...] = g
            if has_w[a]:
                m2 = ADAM_B1 * m_refs[k][...] + (1.0 - ADAM_B1) * g
                v2 = ADAM_B2 * v_refs[k][...] + (1.0 - ADAM_B2) * (g * g)
                m_hat = m2 / (1.0 - ADAM_B1 ** ADAM_STEP)
                v_hat = v2 / (1.0 - ADAM_B2 ** ADAM_STEP)
                upd_refs[3 * k][...] = -ADAM_LR * (m_hat / (jnp.sqrt(v_hat) + ADAM_EPS) + ADAM_WD * w_refs[k][...])
                upd_refs[3 * k + 1][...] = m2
                upd_refs[3 * k + 2][...] = v2
                k += 1

    ws = [w for w in weights if w is not None]
    g_shapes = [jax.ShapeDtypeStruct(p.shape if w is None else w.shape, F32) for p, w in zip(partials, weights)]
    u_shapes = [jax.ShapeDtypeStruct(w.shape, F32) for w in ws for _ in range(3)]
    vm = pl.BlockSpec(memory_space=pltpu.VMEM)
    n_args = n + 3 * n_w
    outs = pl.pallas_call(
        body, name=name, in_specs=[vm] * n_args, out_specs=[vm] * (n + 3 * n_w), out_shape=g_shapes + u_shapes,
        scratch_shapes=[pltpu.VMEM((rows, width), F32), pltpu.VMEM((N_DEV, rows, width), F32),
                        pltpu.SemaphoreType.DMA((7,)), pltpu.SemaphoreType.DMA((7,))],
    )(*partials, *ws, *[m for m in moments1 if m is not None], *[v for v in moments2 if v is not None])
    return outs[:n], outs[n:]


GATE_COL = 3 * SB_WIDTH + 3 * FOX_WIDTH + FOX_HEADS + MEM_WIDTH
FL_COL = QKV_WIDTH


GROUP_A_COLS = [(0, QKV_WIDTH), (FL_COL + FOX_HEADS, MEM_WIDTH)]
GROUP_B_COLS = [(GATE_COL, MIX_WIDTH), (FL_COL, FOX_HEADS)]


def _group_from_shards(shard_of, cw, spans, pad):
    parts = []
    for lo, width in spans:
        hi = lo + width
        for j in range(N_CHIPS):
            a, b = max(lo, j * cw), min(hi, (j + 1) * cw)
            if a < b:
                parts.append(shard_of(j)[:, a - j * cw:b - j * cw])
    if pad:
        parts.append(jnp.zeros((parts[0].shape[0], pad), parts[0].dtype))
    return jnp.concatenate(parts, axis=1)


def _shard_from_groups(ga, gb, j, cw):
    lo, hi = j * cw, (j + 1) * cw
    placed = []
    for grp, spans in ((ga, GROUP_A_COLS), (gb, GROUP_B_COLS)):
        at = 0
        for first, width in spans:
            a, b = max(lo, first), min(hi, first + width)
            if a < b:
                placed.append((a, grp[:, at + a - first:at + b - first]))
            at += width
    return jnp.concatenate([p for _, p in sorted(placed, key=lambda t: t[0])], axis=1)


def _tile_of(n, cap, unit):
    if n <= cap:
        return n
    best = None
    for t in range(unit, cap + 1, unit):
        if n % t == 0:
            best = t
    assert best is not None, (n, cap, unit)
    return best


def _column_major_rows(a):
    dp, r, c = a.shape
    return a.transpose(2, 0, 1).reshape(c, dp, r // LANES, LANES).transpose(0, 2, 1, 3).reshape(-1, 8, LANES)


def _from_column_major_rows(b, shape):
    dp, r, c = shape
    return b.reshape(c, r // LANES, dp, LANES).transpose(0, 2, 1, 3).reshape(c, dp, r).transpose(1, 2, 0)


def kernel(x, mem, norm_w, w_in, b_forget, mem_norm_w, w_mem_kv, out_norm_w, w_out, final_norm_w, loss_target, m_norm_w, m_w_in, m_b_forget, m_mem_norm_w, m_w_mem_kv, m_out_norm_w, m_w_out, m_final_norm_w, v_norm_w, v_w_in, v_b_forget, v_mem_norm_w, v_w_mem_kv, v_out_norm_w, v_w_out, v_final_norm_w):
    xs = x[0]
    mems = mem[0]
    target = loss_target[0]
    s, d = xs.shape
    depth = norm_w.shape[0]
    nb = s // TILE
    ts = _tile_of(s, 512, 8)
    big = (w_in, w_mem_kv, w_out)
    core = lax.axis_index("c")
    chip = 2 * lax.axis_index("x") + lax.axis_index("y")
    cvec = core.astype(jnp.int32).reshape(1)
    mvec = chip.astype(jnp.int32).reshape(1)
    cw = w_in.shape[2]

    own_w = [[a[l].astype(BF16) for a in big] for l in range(depth)]

    def lay_out_in(own, got):
        shard_of = lambda j: jnp.where(chip == j, own, got[j])
        return (_group_from_shards(shard_of, cw, GROUP_A_COLS, 0),
                _group_from_shards(shard_of, cw, GROUP_B_COLS, LANES - FOX_HEADS))

    def lay_out_rows(own, got):
        full = jnp.where(lax.broadcasted_iota(jnp.int32, got.shape, 0) == chip, own[None], got)
        return full.reshape(-1, full.shape[2])

    w_in_groups = [lay_out_in(own_w[0][0], _run_exchange("gather_weights0", _gather_exchange(own_w[0][:1]))[0])]
    layer_w = []

    tm = _tile_of(s, 256, 8)
    fl_block = MIX_WIDTH // LANES

    saved = []
    cur = xs
    for l in range(depth):
        wa, wb = w_in_groups[l]
        h = _rms_fwd(f"rms_fwd{l}", cur, norm_w[l][None], ts)
        pa = _mm(f"inproj_a{l}", h, wa, "nn", ts, PA, BF16)
        pb = _mm(f"inproj_b{l}", h, wb, "nn", ts, PB, F32)
        bpad = jnp.pad(b_forget[l], (0, LANES - FOX_HEADS))[None]
        ccol4, crow4 = _gate_fwd(f"gate_fwd{l}", pb, bpad, fl_block)
        more = l + 1 < depth
        o_sb, got = _sb_fwd(f"sb_fwd{l}", pa, 0, carried=_gather_exchange(own_w[l][1:]))
        wkv, wout = lay_out_rows(own_w[l][1], got[0]), lay_out_rows(own_w[l][2], got[1])
        layer_w.append((wa, wb, wkv, wout))
        o_fx, lse_fx, got = _fox_fwd(f"fox_fwd{l}", pa, 3 * SB_WIDTH, ccol4, crow4,
                                     carried=_gather_exchange(own_w[l + 1][:1]) if more else None)
        if more:
            w_in_groups.append(lay_out_in(own_w[l + 1][0], got[0]))
        mn = _rms_fwd(f"mem_rms{l}", mems, mem_norm_w[l][None], mems.shape[0])
        mkv = _mm(f"mem_kv{l}", mn, wkv, "nn", mems.shape[0], 2 * MEM_WIDTH, BF16)
        o_m, lse_m = _mem_fwd(f"mem_fwd{l}", pa, mkv)
        nxt, y2 = _out_fwd(f"out_fwd{l}", o_sb, o_fx, o_m, pb, out_norm_w[l][None], cur, wout, tm)
        saved.append((cur, h, pa, pb, bpad, ccol4, crow4, o_sb, o_fx, lse_fx, mn, mkv, o_m, lse_m, y2))
        cur = nxt

    loss_v, dx, dxb, g_final = _final_loss("final_loss", cur, final_norm_w[None], target, ts)

    g_norm, g_b, g_memnorm, g_outnorm = [None] * depth, [None] * depth, [None] * depth, [None] * depth
    g_wa, g_wb, g_wkv, g_wout = [None] * depth, [None] * depth, [None] * depth, [None] * depth
    g_own = [[None] * depth for _ in big]
    g_other = [[None] * depth for _ in big]

    def swap_of(jobs):
        return _swap_exchange([g for _, _, g, _ in jobs])

    def chip_sums(jobs, got):
        return [(lr, k, _add_half(f"grad_add_half{lr}_{k}", g, r_, cvec, t_), t_) for (lr, k, g, t_), r_ in zip(jobs, got)]

    def sum_at_owner(jobs, from_chips):
        return [_sum_chips(f"grad_sum_chips{lr}_{k}", h_, r_, mvec, t_) for (lr, k, h_, t_), r_ in zip(jobs, from_chips)]

    def keep(jobs, halves, others):
        for (lr, k, _, _), mine, other in zip(jobs, halves, others):
            g_own[k][lr], g_other[k][lr] = mine, other

    def job(lr, k, g4):
        return lr, k, g4, _tile_of(g4.shape[1] // 2, 256, 16)

    pending = []
    for l in reversed(range(depth)):
        xin, h, pa, pb, bpad, ccol4, crow4, o_sb, o_fx, lse_fx, mn, mkv, o_m, lse_m, y2 = saved[l]
        wa, wb, wkv, wout = layer_w[l]
        dy, dgate, g_outnorm[l] = _out_bwd(f"out_bwd{l}", dxb, o_sb, o_fx, o_m, pb, out_norm_w[l][None], wout, tm)
        g_wout[l] = _mm(f"dw_out{l}", y2, dxb, "tn", _tile_of(MIX_WIDTH, 640, LANES), d, F32)
        dq_m, dmkv = _mem_bwd(f"mem_bwd{l}", pa, mkv, o_m, lse_m, dy, SB_WIDTH + FOX_WIDTH)
        g_wkv[l] = _mm(f"dw_kv{l}", mn, dmkv, "tn", d, 2 * MEM_WIDTH, F32)
        dmn = _mm(f"dmem{l}", dmkv, wkv, "nt", mems.shape[0], d, F32)
        g_memnorm[l] = _rms_wgrad(f"mem_norm_grad{l}", mems, dmn)
        small = [job(l, 1, g_wkv[l].reshape(N_CHIPS, -1, g_wkv[l].shape[1])), job(l, 2, g_wout[l].reshape(N_CHIPS, -1, d))]
        (dq_fx, dk_fx, dv_fx, cs4), got = _fox_bwd(f"fox_bwd{l}", pa, 3 * SB_WIDTH, ccol4, crow4, o_fx, lse_fx, dy,
                                                    SB_WIDTH, carried=swap_of(small))
        pending += chip_sums(small, got)
        (dq_sb, dk_sb, dv_sb), from_chips = _sb_bwd(f"sb_bwd{l}", pa, 0, dy, 0,
                                                   carried=_scatter_exchange([j[2] for j in pending]))
        reduced_jobs, reduced = pending, sum_at_owner(pending, from_chips)
        swap_back = _swap_reduced_exchange(reduced)
        dpb, g_b[l] = _gate_bwd(f"gate_bwd{l}", pb, bpad, cs4, fl_block, dgate)
        dpa = jnp.concatenate([dq_sb, dk_sb, dv_sb, dq_fx, dk_fx, dv_fx, dq_m], axis=1)
        tw = _tile_of(d, 1024, LANES)
        g_wa[l] = _mm(f"dw_in_a{l}", h, dpa, "tn", tw, _tile_of(PA, 1664, LANES), BF16)
        g_wb[l] = _mm(f"dw_in_b{l}", h, dpb, "tn", tw, PB, BF16)
        g4_in = jnp.stack([_shard_from_groups(g_wa[l], g_wb[l], j, cw) for j in range(N_CHIPS)])
        w_in_job = [job(l, 0, g4_in)]
        if l > 0:
            dx, dxb, g_norm[l], got = _inproj_bwd(f"inproj_bwd{l}", dpa, dpb, wa, wb, xin, norm_w[l][None], dx, tm,
                                                  carried=_both(swap_of(w_in_job), swap_back))
            pending = chip_sums(w_in_job, got[:1])
            keep(reduced_jobs, reduced, got[1:])
        else:
            pending = chip_sums(w_in_job, _run_exchange("grad_swap_halves_last", swap_of(w_in_job)))
            dx, dxb, g_norm[l], got = _inproj_bwd(f"inproj_bwd{l}", dpa, dpb, wa, wb, xin, norm_w[l][None], dx, tm,
                                                  carried=_both(_scatter_exchange([j[2] for j in pending]), swap_back))
            keep(reduced_jobs, reduced, got[1:])
            last = sum_at_owner(pending, got[:1])
            keep(pending, last, _run_exchange("grad_swap_reduced_last", _swap_reduced_exchange(last)))

    small_w = [norm_w, b_forget, mem_norm_w, out_norm_w, final_norm_w]
    small_m = [m_norm_w, m_b_forget, m_mem_norm_w, m_out_norm_w, m_final_norm_w]
    small_v = [v_norm_w, v_b_forget, v_mem_norm_w, v_out_norm_w, v_final_norm_w]
    rows2 = lambda a: a.reshape(-1, a.shape[-1])
    partials = [jnp.concatenate(g_norm, axis=0), jnp.concatenate(g_b, axis=0), jnp.concatenate(g_memnorm, axis=0),
                jnp.concatenate(g_outnorm, axis=0), g_final, loss_v]
    sums, updates = _small_update("small_update", partials, [rows2(a) for a in small_w] + [None],
                                  [rows2(a) for a in small_m] + [None], [rows2(a) for a in small_v] + [None])
    small_grads = [g.reshape(a.shape) for g, a in zip(sums, small_w)]
    loss = sums[-1][0, 0]
    small_delta, small_m2, small_v2 = ([updates[3 * k + t].reshape(a.shape) for k, a in enumerate(small_w)]
                                       for t in range(3))
    big_grads, big_delta, big_m2, big_v2 = [], [], [], []
    for k, (nm, w_, m_, v_) in enumerate(zip(("w_in", "w_mem_kv", "w_out"), big, (m_w_in, m_w_mem_kv, m_w_out),
                                             (v_w_in, v_w_mem_kv, v_w_out))):
        if w_.shape[2] % LANES:
            g_full = jnp.stack([jnp.concatenate([jnp.where(core == 0, go, gt), jnp.where(core == 0, gt, go)], axis=0)
                                for go, gt in zip(g_own[k], g_other[k])])
            w_p, g_p, m_p, v_p = (_column_major_rows(a) for a in (w_, g_full, m_, v_))
            outs = _adamw(f"adamw_{nm}", w_p, g_p, m_p, v_p, _tile_of(w_p.shape[0], 600, 1))
            outs = [_from_column_major_rows(o, w_.shape) for o in (g_p, *outs)]
        else:
            outs = _adamw_sharded(f"adamw_{nm}", w_, m_, v_, g_own[k], g_other[k], cvec,
                                  _tile_of(w_.shape[1] // 2, 256, 8))
        for lst, o in zip((big_grads, big_delta, big_m2, big_v2), outs):
            lst.append(o)

    def order(sm, bg):
        return [sm[0], bg[0], sm[1], sm[2], bg[1], sm[3], bg[2], sm[4]]

    return (loss, dx[None], *order(small_grads, big_grads), *order(small_delta, big_delta),
            *order(small_m2, big_m2), *order(small_v2, big_v2))
```
